```python
import math
import jax, jax.numpy as jnp
from jax import lax
import numpy as np

D_MODEL = 1024
BATCH = 16
SEQ = 2048
DEPTH = 1

CTX_LEN = 256
GRID_W = 64

MIX_WIDTH = D_MODEL
DN_HEAD_DIM = 128
DN_WIDTH = MIX_WIDTH // 2
DN_HEADS = DN_WIDTH // DN_HEAD_DIM
NA_HEAD_DIM = 64
NA_WIDTH = MIX_WIDTH - DN_WIDTH
NA_HEADS = NA_WIDTH // NA_HEAD_DIM

DN_IN = 4 * DN_WIDTH + 4 * DN_HEADS
NA_IN = 3 * NA_WIDTH
IN_COLS = DN_IN + NA_IN

CONV_K = 5
CHUNK = 64

WIN_ROWS_MAX = 8
WIN_COLS = 16
QB_W = 16
KB_W = 32
N_COL_BLK = GRID_W // QB_W

D_FF = -(-8 * D_MODEL // (3 * 256)) * 256
EPS = 1e-6

kernel_name = "hymba_gdn_natten_dit_prefix"


def rms_norm(x, w):
    xf = x.astype(jnp.float32)
    y = xf * lax.rsqrt(jnp.mean(xf * xf, axis=-1, keepdims=True) + EPS)
    return (y * w.astype(jnp.float32)).astype(x.dtype)


def l2_normalize(x):
    xf = x.astype(jnp.float32)
    return xf * lax.rsqrt(jnp.sum(xf * xf, axis=-1, keepdims=True) + EPS)


def modulate(h, shift, scale):
    return h * (1 + scale) + shift


def short_conv(u, w):
    out = lax.conv_general_dilated(
        u, w[:, None, :].astype(u.dtype), window_strides=(1,),
        padding=[(CONV_K // 2, CONV_K // 2)],
        dimension_numbers=("NWC", "WIO", "NWC"), feature_group_count=u.shape[-1])
    return jax.nn.silu(out)


def chunk_gated_delta_rule(q, k, v, g, beta, S0):
    B, H, L, dk = q.shape
    dv = v.shape[-1]
    n = L // CHUNK
    f32 = jnp.float32
    q = q.astype(f32).reshape(B, H, n, CHUNK, dk)
    k = k.astype(f32).reshape(B, H, n, CHUNK, dk)
    v = v.astype(f32).reshape(B, H, n, CHUNK, dv)
    beta = beta.astype(f32).reshape(B, H, n, CHUNK)
    gam = jnp.cumsum(g.astype(f32).reshape(B, H, n, CHUNK), axis=-1)
    tril = jnp.tril(jnp.ones((CHUNK, CHUNK), dtype=bool))
    strict = jnp.tril(jnp.ones((CHUNK, CHUNK), dtype=bool), -1)
    decay = jnp.exp(jnp.where(tril, gam[..., :, None] - gam[..., None, :], -jnp.inf))
    kb = k * beta[..., None]
    m = jnp.where(strict, jnp.einsum("bhncd,bhnmd->bhncm", kb, k) * decay, 0.0)
    a_mat = jnp.eye(CHUNK, dtype=f32) + m
    u = lax.linalg.triangular_solve(a_mat, v * beta[..., None], left_side=True, lower=True, unit_diagonal=True)
    w = lax.linalg.triangular_solve(a_mat, kb * jnp.exp(gam)[..., None], left_side=True, lower=True, unit_diagonal=True)
    qk = jnp.where(tril, jnp.einsum("bhncd,bhnmd->bhncm", q, k) * decay, 0.0)

    def step(S, xs):
        q_i, k_i, u_i, w_i, qk_i, gam_i = xs
        v_new = u_i - jnp.einsum("bhck,bhkv->bhcv", w_i, S)
        o_i = (jnp.einsum("bhck,bhkv->bhcv", q_i * jnp.exp(gam_i)[..., None], S)
               + jnp.einsum("bhcm,bhmv->bhcv", qk_i, v_new))
        g_last = gam_i[..., -1:]
        k_dec = k_i * jnp.exp(g_last - gam_i)[..., None]
        S = S * jnp.exp(g_last)[..., None] + jnp.einsum("bhck,bhcv->bhkv", k_dec, v_new)
        return S, o_i

    xs = tuple(jnp.moveaxis(t, 2, 0) for t in (q, k, u, w, qk, gam))
    S, o = lax.scan(step, S0.astype(f32), xs)
    o = jnp.moveaxis(o, 0, 2).reshape(B, H, L, dv)
    return o, S


def gated_deltanet(p, conv_w, A_log, dt_bias, out_norm_w, S0_f, S0_b):
    B, L, _ = p.shape
    qkv = short_conv(p[..., :3 * DN_WIDTH], conv_w)
    z = p[..., 3 * DN_WIDTH:4 * DN_WIDTH].reshape(B, L, DN_HEADS, DN_HEAD_DIM)
    a = p[..., 4 * DN_WIDTH:4 * DN_WIDTH + 2 * DN_HEADS].reshape(B, L, 2, DN_HEADS)
    b = p[..., 4 * DN_WIDTH + 2 * DN_HEADS:].reshape(B, L, 2, DN_HEADS)
    q, k, v = [t.reshape(B, L, DN_HEADS, DN_HEAD_DIM).transpose(0, 2, 1, 3) for t in jnp.split(qkv, 3, axis=-1)]
    q = l2_normalize(q) * DN_HEAD_DIM ** -0.5
    k = l2_normalize(k)
    g = (-jnp.exp(A_log.astype(jnp.float32))
         * jax.nn.softplus(a.astype(jnp.float32) + dt_bias.astype(jnp.float32))).transpose(2, 0, 3, 1)
    beta = jax.nn.sigmoid(b.astype(jnp.float32)).transpose(2, 0, 3, 1)
    o_f, S_f = chunk_gated_delta_rule(q, k, v, g[0], beta[0], S0_f)
    rev = lambda t: jnp.flip(t, axis=2)
    o_b, S_b = chunk_gated_delta_rule(rev(q), rev(k), rev(v), rev(g[1]), rev(beta[1]), S0_b)
    o = (o_f + rev(o_b)).transpose(0, 2, 1, 3)
    o = rms_norm(o, out_norm_w) * jax.nn.silu(z.astype(jnp.float32))
    return o.reshape(B, L, DN_WIDTH).astype(p.dtype), S_f, S_b


def na_qkv(p, q_norm_w, k_norm_w):
    B, L, _ = p.shape
    q, k, v = [t.reshape(B, L, NA_HEADS, NA_HEAD_DIM) for t in jnp.split(p, 3, axis=-1)]
    q = rms_norm(q, q_norm_w) * NA_HEAD_DIM ** -0.5
    k = rms_norm(k, k_norm_w)
    return q, k, v


def na_column_tables():
    cols = np.arange(GRID_W)
    win_start = np.clip(cols - WIN_COLS // 2, 0, GRID_W - WIN_COLS).reshape(N_COL_BLK, QB_W)
    blk_start = np.clip(np.arange(N_COL_BLK) * QB_W - WIN_COLS // 2, 0, GRID_W - KB_W)
    key_cols = blk_start[:, None] + np.arange(KB_W)
    q_cols = cols.reshape(N_COL_BLK, QB_W)
    kc = key_cols[:, None, :]
    valid = (kc >= win_start[:, :, None]) & (kc < win_start[:, :, None] + WIN_COLS)
    rel_idx = np.clip(kc - q_cols[:, :, None] + WIN_COLS - 1, 0, 2 * WIN_COLS - 2)
    return key_cols, valid, rel_idx


def neighborhood_attention(q, k, v, k_ctx, v_ctx, rpb, rows):
    B, L, H, Dh = q.shape
    win_rows = min(WIN_ROWS_MAX, rows)
    n_win = win_rows * KB_W
    key_cols, col_valid, rel_col_idx = na_column_tables()
    qg = q.reshape(B, rows, N_COL_BLK, QB_W, H, Dh)
    kg = k.reshape(B, rows, GRID_W, H, Dh)
    vg = v.reshape(B, rows, GRID_W, H, Dh)
    bias_cols = rpb[:, :, rel_col_idx]

    def row_block(r):
        r0 = jnp.clip(r - win_rows // 2, 0, rows - win_rows)
        q_r = lax.dynamic_index_in_dim(qg, r, axis=1, keepdims=False)
        k_r = lax.dynamic_slice_in_dim(kg, r0, win_rows, axis=1)[:, :, key_cols]
        v_r = lax.dynamic_slice_in_dim(vg, r0, win_rows, axis=1)[:, :, key_cols]
        dr_idx = r0 + jnp.arange(win_rows) - r + WIN_ROWS_MAX - 1
        bias = jnp.take(bias_cols, dr_idx, axis=1).transpose(0, 2, 3, 1, 4)
        s_win = jnp.einsum("bnqhd,binchd->bhnqic", q_r, k_r).astype(jnp.float32) + bias.astype(jnp.float32)
        s_win = jnp.where(col_valid[:, :, None, :], s_win, -jnp.inf)
        s_ctx = jnp.einsum("bnqhd,bkhd->bhnqk", q_r, k_ctx).astype(jnp.float32)
        s = jnp.concatenate([s_win.reshape(B, H, N_COL_BLK, QB_W, n_win), s_ctx], axis=-1)
        p = jax.nn.softmax(s, axis=-1).astype(v.dtype)
        p_win = p[..., :n_win].reshape(B, H, N_COL_BLK, QB_W, win_rows, KB_W)
        o = (jnp.einsum("bhnqic,binchd->bnqhd", p_win, v_r)
             + jnp.einsum("bhnqk,bkhd->bnqhd", p[..., n_win:], v_ctx))
        return o.reshape(B, GRID_W, H, Dh)

    out = lax.map(row_block, jnp.arange(rows))
    return jnp.moveaxis(out, 0, 1).reshape(B, L, H * Dh)


def context_attention(q, k, v):
    B, N, H, Dh = q.shape
    s = jnp.einsum("bqhd,bkhd->bhqk", q, k).astype(jnp.float32)
    p = jax.nn.softmax(s, axis=-1).astype(v.dtype)
    return jnp.einsum("bhqk,bkhd->bqhd", p, v).reshape(B, N, H * Dh)


def swiglu(h, w_in, w_out):
    gate, up = jnp.split(h @ w_in, 2, axis=-1)
    return (jax.nn.silu(gate) * up) @ w_out


def _fwd_setup_inputs(seed: int = 0) -> dict:
    key = jax.random.key(seed)
    ks = jax.random.split(key, 20)
    nrm = lambda k, shape, s: jax.random.normal(k, shape, jnp.float32) * s
    A = jax.random.uniform(ks[10], (DEPTH, 2, DN_HEADS), jnp.float32, minval=1.0, maxval=16.0)
    dt = jnp.exp(jax.random.uniform(ks[11], (DEPTH, 2, DN_HEADS), jnp.float32,
                                    minval=math.log(1e-3), maxval=math.log(1e-1)))
    return {
        "x": nrm(ks[0], (BATCH, SEQ, D_MODEL), 1.0),
        "c": nrm(ks[1], (BATCH, D_MODEL), 1.0),
        "ctx": nrm(ks[2], (BATCH, CTX_LEN, D_MODEL), 1.0),
        "c_ctx": nrm(ks[3], (D_MODEL,), 1.0),
        "norm1_w": 1.0 + nrm(ks[4], (DEPTH, D_MODEL), 0.02),
        "norm2_w": 1.0 + nrm(ks[5], (DEPTH, D_MODEL), 0.02),
        "w_ada": nrm(ks[6], (DEPTH, D_MODEL, 6 * D_MODEL), 0.5 * D_MODEL ** -0.5),
        "b_ada": nrm(ks[7], (DEPTH, 6 * D_MODEL), 0.02),
        "w_in": nrm(ks[8], (DEPTH, D_MODEL, IN_COLS), D_MODEL ** -0.5),
        "dn_conv_w": nrm(ks[9], (DEPTH, CONV_K, 3 * DN_WIDTH), CONV_K ** -0.5),
        "dn_A_log": jnp.log(A),
        "dn_dt_bias": dt + jnp.log(-jnp.expm1(-dt)),
        "dn_out_norm_w": 1.0 + nrm(ks[12], (DEPTH, DN_HEAD_DIM), 0.02),
        "na_q_norm_w": 1.0 + nrm(ks[13], (DEPTH, NA_HEAD_DIM), 0.02),
        "na_k_norm_w": 1.0 + nrm(ks[14], (DEPTH, NA_HEAD_DIM), 0.02),
        "na_rpb": nrm(ks[15], (DEPTH, NA_HEADS, 2 * WIN_ROWS_MAX - 1, 2 * WIN_COLS - 1), 0.1),
        "w_out": nrm(ks[16], (DEPTH, MIX_WIDTH, D_MODEL), MIX_WIDTH ** -0.5),
        "w_ffn_in": nrm(ks[17], (DEPTH, D_MODEL, 2 * D_FF), D_MODEL ** -0.5),
        "w_ffn_out": nrm(ks[18], (DEPTH, D_FF, D_MODEL), D_FF ** -0.5),
    }


def _fwd_reference(x, c, ctx, c_ctx, norm1_w, norm2_w, w_ada, b_ada, w_in, dn_conv_w, dn_A_log, dn_dt_bias,
              dn_out_norm_w, na_q_norm_w, na_k_norm_w, na_rpb, w_out, w_ffn_in, w_ffn_out):
    B, L, _ = x.shape
    rows = L // GRID_W
    for i in range(DEPTH):
        last = i == DEPTH - 1
        mod_x = (jax.nn.silu(c) @ w_ada[i] + b_ada[i])[:, None, :]
        mod_c = jax.nn.silu(c_ctx) @ w_ada[i] + b_ada[i]
        sh1, sc1, g1, sh2, sc2, g2 = jnp.split(mod_x, 6, axis=-1)
        csh1, csc1, cg1, csh2, csc2, cg2 = jnp.split(mod_c, 6, axis=-1)

        px = modulate(rms_norm(x, norm1_w[i]), sh1, sc1) @ w_in[i]
        pc = modulate(rms_norm(ctx, norm1_w[i]), csh1, csc1) @ w_in[i]

        s_zero = jnp.zeros((B, DN_HEADS, DN_HEAD_DIM, DN_HEAD_DIM), jnp.float32)
        dn_c, s_ctx_f, s_ctx_b = gated_deltanet(pc[..., :DN_IN], dn_conv_w[i], dn_A_log[i], dn_dt_bias[i],
                                                dn_out_norm_w[i], s_zero, s_zero)
        dn_x, _, _ = gated_deltanet(px[..., :DN_IN], dn_conv_w[i], dn_A_log[i], dn_dt_bias[i],
                                    dn_out_norm_w[i], s_ctx_f, s_ctx_b)

        qx, kx, vx = na_qkv(px[..., DN_IN:], na_q_norm_w[i], na_k_norm_w[i])
        qc, kc, vc = na_qkv(pc[..., DN_IN:], na_q_norm_w[i], na_k_norm_w[i])
        na_x = neighborhood_attention(qx, kx, vx, kc, vc, na_rpb[i], rows)

        x = x + g1 * (jnp.concatenate([dn_x, na_x], axis=-1) @ w_out[i])
        x = x + g2 * swiglu(modulate(rms_norm(x, norm2_w[i]), sh2, sc2), w_ffn_in[i], w_ffn_out[i])

        if not last:
            na_c = context_attention(qc, kc, vc)
            ctx = ctx + cg1 * (jnp.concatenate([dn_c, na_c], axis=-1) @ w_out[i])
            ctx = ctx + cg2 * swiglu(modulate(rms_norm(ctx, norm2_w[i]), csh2, csc2), w_ffn_in[i], w_ffn_out[i])
    return x


import jax as _jax
import jax.numpy as _jnp

TWIN_FORMAT = 'train_step'
FWD_PARAMS = ['x', 'c', 'ctx', 'c_ctx', 'norm1_w', 'norm2_w', 'w_ada', 'b_ada', 'w_in', 'dn_conv_w', 'dn_A_log', 'dn_dt_bias', 'dn_out_norm_w', 'na_q_norm_w', 'na_k_norm_w', 'na_rpb', 'w_out', 'w_ffn_in', 'w_ffn_out']
TWIN_WEIGHTS = ['c_ctx', 'norm1_w', 'norm2_w', 'w_ada', 'b_ada', 'w_in', 'dn_conv_w', 'dn_A_log', 'dn_dt_bias', 'dn_out_norm_w', 'na_q_norm_w', 'na_k_norm_w', 'na_rpb', 'w_out', 'w_ffn_in', 'w_ffn_out']
TWIN_DIFF_INPUT = 'x'
TWIN_INPUTS = ['x', 'c', 'ctx', 'c_ctx', 'norm1_w', 'norm2_w', 'w_ada', 'b_ada', 'w_in', 'dn_conv_w', 'dn_A_log', 'dn_dt_bias', 'dn_out_norm_w', 'na_q_norm_w', 'na_k_norm_w', 'na_rpb', 'w_out', 'w_ffn_in', 'w_ffn_out', 'loss_target', 'm_c_ctx', 'm_norm1_w', 'm_norm2_w', 'm_w_ada', 'm_b_ada', 'm_w_in', 'm_dn_conv_w', 'm_dn_A_log', 'm_dn_dt_bias', 'm_dn_out_norm_w', 'm_na_q_norm_w', 'm_na_k_norm_w', 'm_na_rpb', 'm_w_out', 'm_w_ffn_in', 'm_w_ffn_out', 'v_c_ctx', 'v_norm1_w', 'v_norm2_w', 'v_w_ada', 'v_b_ada', 'v_w_in', 'v_dn_conv_w', 'v_dn_A_log', 'v_dn_dt_bias', 'v_dn_out_norm_w', 'v_na_q_norm_w', 'v_na_k_norm_w', 'v_na_rpb', 'v_w_out', 'v_w_ffn_in', 'v_w_ffn_out']
TWIN_OUTPUTS = ['loss', 'grad_x', 'grad_c_ctx', 'grad_norm1_w', 'grad_norm2_w', 'grad_w_ada', 'grad_b_ada', 'grad_w_in', 'grad_dn_conv_w', 'grad_dn_A_log', 'grad_dn_dt_bias', 'grad_dn_out_norm_w', 'grad_na_q_norm_w', 'grad_na_k_norm_w', 'grad_na_rpb', 'grad_w_out', 'grad_w_ffn_in', 'grad_w_ffn_out', 'delta_c_ctx', 'delta_norm1_w', 'delta_norm2_w', 'delta_w_ada', 'delta_b_ada', 'delta_w_in', 'delta_dn_conv_w', 'delta_dn_A_log', 'delta_dn_dt_bias', 'delta_dn_out_norm_w', 'delta_na_q_norm_w', 'delta_na_k_norm_w', 'delta_na_rpb', 'delta_w_out', 'delta_w_ffn_in', 'delta_w_ffn_out', 'new_m_c_ctx', 'new_m_norm1_w', 'new_m_norm2_w', 'new_m_w_ada', 'new_m_b_ada', 'new_m_w_in', 'new_m_dn_conv_w', 'new_m_dn_A_log', 'new_m_dn_dt_bias', 'new_m_dn_out_norm_w', 'new_m_na_q_norm_w', 'new_m_na_k_norm_w', 'new_m_na_rpb', 'new_m_w_out', 'new_m_w_ffn_in', 'new_m_w_ffn_out', 'new_v_c_ctx', 'new_v_norm1_w', 'new_v_norm2_w', 'new_v_w_ada', 'new_v_b_ada', 'new_v_w_in', 'new_v_dn_conv_w', 'new_v_dn_A_log', 'new_v_dn_dt_bias', 'new_v_dn_out_norm_w', 'new_v_na_q_norm_w', 'new_v_na_k_norm_w', 'new_v_na_rpb', 'new_v_w_out', 'new_v_w_ffn_in', 'new_v_w_ffn_out']
TWIN_LEAF_KINDS = {'loss': 'loss', 'grad_x': 'grad_x', 'grad_c_ctx': 'grad_w', 'grad_norm1_w': 'grad_w', 'grad_norm2_w': 'grad_w', 'grad_w_ada': 'grad_w', 'grad_b_ada': 'grad_w', 'grad_w_in': 'grad_w', 'grad_dn_conv_w': 'grad_w', 'grad_dn_A_log': 'grad_w', 'grad_dn_dt_bias': 'grad_w', 'grad_dn_out_norm_w': 'grad_w', 'grad_na_q_norm_w': 'grad_w', 'grad_na_k_norm_w': 'grad_w', 'grad_na_rpb': 'grad_w', 'grad_w_out': 'grad_w', 'grad_w_ffn_in': 'grad_w', 'grad_w_ffn_out': 'grad_w', 'delta_c_ctx': 'delta_w', 'delta_norm1_w': 'delta_w', 'delta_norm2_w': 'delta_w', 'delta_w_ada': 'delta_w', 'delta_b_ada': 'delta_w', 'delta_w_in': 'delta_w', 'delta_dn_conv_w': 'delta_w', 'delta_dn_A_log': 'delta_w', 'delta_dn_dt_bias': 'delta_w', 'delta_dn_out_norm_w': 'delta_w', 'delta_na_q_norm_w': 'delta_w', 'delta_na_k_norm_w': 'delta_w', 'delta_na_rpb': 'delta_w', 'delta_w_out': 'delta_w', 'delta_w_ffn_in': 'delta_w', 'delta_w_ffn_out': 'delta_w', 'new_m_c_ctx': 'new_m', 'new_m_norm1_w': 'new_m', 'new_m_norm2_w': 'new_m', 'new_m_w_ada': 'new_m', 'new_m_b_ada': 'new_m', 'new_m_w_in': 'new_m', 'new_m_dn_conv_w': 'new_m', 'new_m_dn_A_log': 'new_m', 'new_m_dn_dt_bias': 'new_m', 'new_m_dn_out_norm_w': 'new_m', 'new_m_na_q_norm_w': 'new_m', 'new_m_na_k_norm_w': 'new_m', 'new_m_na_rpb': 'new_m', 'new_m_w_out': 'new_m', 'new_m_w_ffn_in': 'new_m', 'new_m_w_ffn_out': 'new_m', 'new_v_c_ctx': 'new_v', 'new_v_norm1_w': 'new_v', 'new_v_norm2_w': 'new_v', 'new_v_w_ada': 'new_v', 'new_v_b_ada': 'new_v', 'new_v_w_in': 'new_v', 'new_v_dn_conv_w': 'new_v', 'new_v_dn_A_log': 'new_v', 'new_v_dn_dt_bias': 'new_v', 'new_v_dn_out_norm_w': 'new_v', 'new_v_na_q_norm_w': 'new_v', 'new_v_na_k_norm_w': 'new_v', 'new_v_na_rpb': 'new_v', 'new_v_w_out': 'new_v', 'new_v_w_ffn_in': 'new_v', 'new_v_w_ffn_out': 'new_v'}


def _forward(args):
    return _fwd_reference(*[args[k] for k in FWD_PARAMS])


def _output_shape():
    out = _jax.eval_shape(lambda: _forward(_fwd_setup_inputs(0)))
    return out.shape, out.dtype

N_MICROBATCH = 1
ADAM_LR = 0.001
ADAM_B1 = 0.9
ADAM_B2 = 0.999
ADAM_EPS = 1e-08
ADAM_WD = 0.01
ADAM_STEP = 10
PER_EXAMPLE_BATCH_AXIS = {'x': 0, 'c': 0, 'ctx': 0, 'loss_target': 0}
SHARED_INPUTS = []
_WEIGHT_DTYPES = {'c_ctx': _jnp.float32, 'norm1_w': _jnp.float32, 'norm2_w': _jnp.float32, 'w_ada': _jnp.float32, 'b_ada': _jnp.float32, 'w_in': _jnp.float32, 'dn_conv_w': _jnp.float32, 'dn_A_log': _jnp.float32, 'dn_dt_bias': _jnp.float32, 'dn_out_norm_w': _jnp.float32, 'na_q_norm_w': _jnp.float32, 'na_k_norm_w': _jnp.float32, 'na_rpb': _jnp.float32, 'w_out': _jnp.float32, 'w_ffn_in': _jnp.float32, 'w_ffn_out': _jnp.float32}
MOMENT_SCALE = {'c_ctx': 1.338625e-01, 'norm1_w': 6.966141e-01, 'norm2_w': 3.431205e+00, 'w_ada': 8.899233e-01, 'b_ada': 1.956383e+00, 'w_in': 1.359486e-01, 'dn_conv_w': 1.516451e-01, 'dn_A_log': 2.908494e-01, 'dn_dt_bias': 2.857452e-01, 'dn_out_norm_w': 6.277174e+00, 'na_q_norm_w': 1.977130e-01, 'na_k_norm_w': 1.972467e-01, 'na_rpb': 3.128736e-03, 'w_out': 1.330572e-01, 'w_ffn_in': 5.990039e-02, 'w_ffn_out': 7.557868e-02}


def _to_microbatches(a, axis):
    t = _jnp.moveaxis(a, axis, 0)
    t = t.reshape((N_MICROBATCH, t.shape[0] // N_MICROBATCH) + t.shape[1:])
    return _jnp.moveaxis(t, 1, axis + 1)


def setup_inputs(seed: int = 0) -> dict:
    inp = _fwd_setup_inputs(seed)
    key = _jax.random.fold_in(_jax.random.key(seed), 7919)
    shape, _ = _output_shape()
    out = dict(inp)
    out["loss_target"] = _jax.random.normal(_jax.random.fold_in(key, 0), shape, _jnp.float32)
    for i, name in enumerate(TWIN_WEIGHTS):
        w = inp[name].astype(_jnp.float32)
        if MOMENT_SCALE is None:
            s = _jnp.sqrt(_jnp.mean(_jnp.square(w)) + 1e-30)
        else:
            s = MOMENT_SCALE[name]
        km, kv = _jax.random.split(_jax.random.fold_in(key, i + 1))
        out[name] = w
        out["m_" + name] = s * _jax.random.normal(km, w.shape, _jnp.float32)
        out["v_" + name] = (s * s) * _jax.random.uniform(kv, w.shape, _jnp.float32, 0.5, 1.5)
    if N_MICROBATCH > 1:
        for name, axis in PER_EXAMPLE_BATCH_AXIS.items():
            out[name] = _to_microbatches(out[name], axis)
    return {'x': out['x'], 'c': out['c'], 'ctx': out['ctx'], 'c_ctx': out['c_ctx'], 'norm1_w': out['norm1_w'], 'norm2_w': out['norm2_w'], 'w_ada': out['w_ada'], 'b_ada': out['b_ada'], 'w_in': out['w_in'], 'dn_conv_w': out['dn_conv_w'], 'dn_A_log': out['dn_A_log'], 'dn_dt_bias': out['dn_dt_bias'], 'dn_out_norm_w': out['dn_out_norm_w'], 'na_q_norm_w': out['na_q_norm_w'], 'na_k_norm_w': out['na_k_norm_w'], 'na_rpb': out['na_rpb'], 'w_out': out['w_out'], 'w_ffn_in': out['w_ffn_in'], 'w_ffn_out': out['w_ffn_out'], 'loss_target': out['loss_target'], 'm_c_ctx': out['m_c_ctx'], 'm_norm1_w': out['m_norm1_w'], 'm_norm2_w': out['m_norm2_w'], 'm_w_ada': out['m_w_ada'], 'm_b_ada': out['m_b_ada'], 'm_w_in': out['m_w_in'], 'm_dn_conv_w': out['m_dn_conv_w'], 'm_dn_A_log': out['m_dn_A_log'], 'm_dn_dt_bias': out['m_dn_dt_bias'], 'm_dn_out_norm_w': out['m_dn_out_norm_w'], 'm_na_q_norm_w': out['m_na_q_norm_w'], 'm_na_k_norm_w': out['m_na_k_norm_w'], 'm_na_rpb': out['m_na_rpb'], 'm_w_out': out['m_w_out'], 'm_w_ffn_in': out['m_w_ffn_in'], 'm_w_ffn_out': out['m_w_ffn_out'], 'v_c_ctx': out['v_c_ctx'], 'v_norm1_w': out['v_norm1_w'], 'v_norm2_w': out['v_norm2_w'], 'v_w_ada': out['v_w_ada'], 'v_b_ada': out['v_b_ada'], 'v_w_in': out['v_w_in'], 'v_dn_conv_w': out['v_dn_conv_w'], 'v_dn_A_log': out['v_dn_A_log'], 'v_dn_dt_bias': out['v_dn_dt_bias'], 'v_dn_out_norm_w': out['v_dn_out_norm_w'], 'v_na_q_norm_w': out['v_na_q_norm_w'], 'v_na_k_norm_w': out['v_na_k_norm_w'], 'v_na_rpb': out['v_na_rpb'], 'v_w_out': out['v_w_out'], 'v_w_ffn_in': out['v_w_ffn_in'], 'v_w_ffn_out': out['v_w_ffn_out']}


def _loss(weights, diff, rest, loss_target):
    with _jax.named_scope("forward"):
        args = {**rest, TWIN_DIFF_INPUT: diff, **{k: w.astype(_WEIGHT_DTYPES[k]) for k, w in weights.items()}}
        y = _forward(args)
    with _jax.named_scope("loss_head"):
        err = _jnp.square(y.astype(_jnp.float32) - loss_target)
        return 0.5 * _jnp.sum(_jnp.mean(err, axis=-1)) if err.ndim else 0.5 * err


def _adamw(w, g, m, v):
    m = ADAM_B1 * m + (1.0 - ADAM_B1) * g
    v = ADAM_B2 * v + (1.0 - ADAM_B2) * _jnp.square(g)
    m_hat = m / (1.0 - ADAM_B1 ** ADAM_STEP)
    v_hat = v / (1.0 - ADAM_B2 ** ADAM_STEP)
    delta = -ADAM_LR * (m_hat / (_jnp.sqrt(v_hat) + ADAM_EPS) + ADAM_WD * w)
    return delta, m, v


def reference(x, c, ctx, c_ctx, norm1_w, norm2_w, w_ada, b_ada, w_in, dn_conv_w, dn_A_log, dn_dt_bias, dn_out_norm_w, na_q_norm_w, na_k_norm_w, na_rpb, w_out, w_ffn_in, w_ffn_out, loss_target, m_c_ctx, m_norm1_w, m_norm2_w, m_w_ada, m_b_ada, m_w_in, m_dn_conv_w, m_dn_A_log, m_dn_dt_bias, m_dn_out_norm_w, m_na_q_norm_w, m_na_k_norm_w, m_na_rpb, m_w_out, m_w_ffn_in, m_w_ffn_out, v_c_ctx, v_norm1_w, v_norm2_w, v_w_ada, v_b_ada, v_w_in, v_dn_conv_w, v_dn_A_log, v_dn_dt_bias, v_dn_out_norm_w, v_na_q_norm_w, v_na_k_norm_w, v_na_rpb, v_w_out, v_w_ffn_in, v_w_ffn_out):
    given = dict(x=x, c=c, ctx=ctx, c_ctx=c_ctx, norm1_w=norm1_w, norm2_w=norm2_w, w_ada=w_ada, b_ada=b_ada, w_in=w_in, dn_conv_w=dn_conv_w, dn_A_log=dn_A_log, dn_dt_bias=dn_dt_bias, dn_out_norm_w=dn_out_norm_w, na_q_norm_w=na_q_norm_w, na_k_norm_w=na_k_norm_w, na_rpb=na_rpb, w_out=w_out, w_ffn_in=w_ffn_in, w_ffn_out=w_ffn_out, loss_target=loss_target, m_c_ctx=m_c_ctx, m_norm1_w=m_norm1_w, m_norm2_w=m_norm2_w, m_w_ada=m_w_ada, m_b_ada=m_b_ada, m_w_in=m_w_in, m_dn_conv_w=m_dn_conv_w, m_dn_A_log=m_dn_A_log, m_dn_dt_bias=m_dn_dt_bias, m_dn_out_norm_w=m_dn_out_norm_w, m_na_q_norm_w=m_na_q_norm_w, m_na_k_norm_w=m_na_k_norm_w, m_na_rpb=m_na_rpb, m_w_out=m_w_out, m_w_ffn_in=m_w_ffn_in, m_w_ffn_out=m_w_ffn_out, v_c_ctx=v_c_ctx, v_norm1_w=v_norm1_w, v_norm2_w=v_norm2_w, v_w_ada=v_w_ada, v_b_ada=v_b_ada, v_w_in=v_w_in, v_dn_conv_w=v_dn_conv_w, v_dn_A_log=v_dn_A_log, v_dn_dt_bias=v_dn_dt_bias, v_dn_out_norm_w=v_dn_out_norm_w, v_na_q_norm_w=v_na_q_norm_w, v_na_k_norm_w=v_na_k_norm_w, v_na_rpb=v_na_rpb, v_w_out=v_w_out, v_w_ffn_in=v_w_ffn_in, v_w_ffn_out=v_w_ffn_out)
    weights = {n: given[n] for n in TWIN_WEIGHTS}
    shared = {n: given[n] for n in SHARED_INPUTS}
    per_example = {n: given[n] for n in ['x', 'c', 'ctx']}
    grad_fn = _jax.value_and_grad(_loss, argnums=(0, 1))

    def one_microbatch(ex, loss_target):
        ex = dict(ex)
        diff = ex.pop(TWIN_DIFF_INPUT)
        return grad_fn(weights, diff, {**shared, **ex}, loss_target)

    if N_MICROBATCH == 1:
        loss, (grad_w, grad_x) = one_microbatch(per_example, given["loss_target"])
    else:
        def body(carry, xs):
            loss_sum, grad_sum = carry
            l_k, (gw_k, gx_k) = one_microbatch(xs[0], xs[1])
            with _jax.named_scope("update"):
                return (loss_sum + l_k, _jax.tree.map(_jnp.add, grad_sum, gw_k)), gx_k

        init = (_jnp.zeros((), _jnp.float32), _jax.tree.map(_jnp.zeros_like, weights))
        (loss, grad_w), grad_x = _jax.lax.scan(body, init, (per_example, given["loss_target"]))
    with _jax.named_scope("update"):
        delta_w, new_m, new_v = {}, {}, {}
        for n in TWIN_WEIGHTS:
            delta_w[n], new_m[n], new_v[n] = _adamw(weights[n], grad_w[n], given["m_" + n], given["v_" + n])
    return (loss, grad_x, *[grad_w[n] for n in TWIN_WEIGHTS], *[delta_w[n] for n in TWIN_WEIGHTS],
            *[new_m[n] for n in TWIN_WEIGHTS], *[new_v[n] for n in TWIN_WEIGHTS])
```

```python
import functools

import numpy as np
import jax
import jax.numpy as jnp
from jax import lax
from jax.experimental import pallas as pl
from jax.experimental.pallas import tpu as pltpu

F32 = jnp.float32
BF16 = jnp.bfloat16

D = 1024
NB = 2
SEQ = 2048
CT = 256
T = CT + SEQ
NTOK = NB * SEQ
CH = 64
NCH = T // CH
NCH_CTX = CT // CH
DNH, DNW = 4, 512
NAH, NAW = 8, 512
GRID_W = 64
ROWS = SEQ // GRID_W
WIN_ROWS = 8
WIN_COLS = 16
NWIN = WIN_ROWS * GRID_W
DFF = 2816
FFT = 1408
IN_COLS = 3600
PC = 3712
COL_Z, COL_NA, COL_G = 1536, 2048, 3584
EPS = 1e-6
NEG = -1e30
RT = 256
VMEM_BIG = 56 * 1024 * 1024
MESH = pl.DeviceIdType.MESH

ADAM_LR, ADAM_B1, ADAM_B2, ADAM_EPS, ADAM_WD, ADAM_STEP = 0.001, 0.9, 0.999, 1e-08, 0.01, 10


def _cp(vmem=None, sem=None):
    kw = {}
    if vmem is not None:
        kw["vmem_limit_bytes"] = vmem
    if sem is not None:
        kw["dimension_semantics"] = sem
    return pltpu.CompilerParams(**kw)


_NN = (((1,), (0,)), ((), ()))
_NT = (((1,), (1,)), ((), ()))
_TN = (((0,), (0,)), ((), ()))


def _dg(a, b, dims):
    return lax.dot_general(a.astype(BF16), b.astype(BF16), dims, preferred_element_type=F32)


@jax.custom_vjp
def _mm(a, b):
    return _dg(a, b, _NN)


_mm.defvjp(lambda a, b: (_dg(a, b, _NN), (a, b)),
           lambda r, g: (_dg(g, r[1], _NT), _dg(r[0], g, _TN)))


@jax.custom_vjp
def _mm_nt(a, b):
    return _dg(a, b, _NT)


_mm_nt.defvjp(lambda a, b: (_dg(a, b, _NT), (a, b)),
              lambda r, g: (_dg(g, r[1], _NN), _dg(g, r[0], _TN)))


@jax.custom_vjp
def _mm_tn(a, b):
    return _dg(a, b, _TN)


_mm_tn.defvjp(lambda a, b: (_dg(a, b, _TN), (a, b)),
              lambda r, g: (_dg(r[1], g, _NT), _dg(r[0], g, _NN)))


def _hmm(a, b):
    return jnp.dot(a, b, precision=lax.Precision.HIGHEST, preferred_element_type=F32)


def _sigmoid(x):
    return 1.0 / (1.0 + jnp.exp(-x))


def _silu(x):
    return x * _sigmoid(x)


def _norm_mod(x, w, sh, sc):
    r = lax.rsqrt(jnp.mean(x * x, axis=-1, keepdims=True) + EPS)
    return (x * r * w) * (1.0 + sc) + sh


def _my_pos():
    return lax.axis_index("x"), lax.axis_index("y"), lax.axis_index("c")


def _flip(pos, k):
    x, y, c = pos
    return (1 - x if k & 4 else x, 1 - y if k & 2 else y, 1 - c if k & 1 else c)


def _slot(pos):
    return 4 * pos[0] + 2 * pos[1] + pos[2]


def _allgather8(v, name, with_sum=False):
    R, W = v.shape

    def body(v_ref, out_ref, *rest):
        if with_sum:
            tot_ref, send_sems, recv_sems, lsem = rest
        else:
            send_sems, recv_sems, lsem = rest
        me = _my_pos()
        mine = pltpu.make_async_copy(v_ref, out_ref.at[_slot(me)], lsem)
        mine.start()
        sends = []
        for k in range(1, 8):
            cp = pltpu.make_async_remote_copy(
                src_ref=v_ref, dst_ref=out_ref.at[_slot(me)], send_sem=send_sems.at[k - 1],
                recv_sem=recv_sems.at[k - 1], device_id=_flip(me, k), device_id_type=MESH)
            cp.start()
            sends.append(cp)
        for k in range(1, 8):
            peer = _flip(me, k)
            pltpu.make_async_remote_copy(
                src_ref=v_ref, dst_ref=out_ref.at[_slot(peer)], send_sem=send_sems.at[k - 1],
                recv_sem=recv_sems.at[k - 1], device_id=peer, device_id_type=MESH).wait_recv()
        for cp in sends:
            cp.wait_send()
        mine.wait()
        if with_sum:
            acc = out_ref[0]
            for s in range(1, 8):
                acc = acc + out_ref[s]
            tot_ref[...] = acc

    out_shape = [jax.ShapeDtypeStruct((8, R, W), F32)]
    out_specs = [pl.BlockSpec(memory_space=pltpu.VMEM)]
    if with_sum:
        out_shape.append(jax.ShapeDtypeStruct((R, W), F32))
        out_specs.append(pl.BlockSpec(memory_space=pltpu.VMEM))
    res = pl.pallas_call(
        body, name=name, out_shape=out_shape,
        in_specs=[pl.BlockSpec(memory_space=pltpu.VMEM)], out_specs=out_specs,
        scratch_shapes=[pltpu.SemaphoreType.DMA((7,)), pltpu.SemaphoreType.DMA((7,)), pltpu.SemaphoreType.DMA],
        compiler_params=_cp(vmem=VMEM_BIG),
    )(v)
    return res if with_sum else res[0]


def _gather_weights(halves):
    n = len(halves)

    def body(*refs):
        ins, outs = refs[:n], refs[n:2 * n]
        send_sems, recv_sems, lsems = refs[2 * n:]
        me = _my_pos()
        x, y, c = me
        sib = (x, y, 1 - c)
        chips = [4, 2, 6]

        def cp(a, k, block, to, src=None):
            dst = outs[a].at[_slot(block)]
            return pltpu.make_async_remote_copy(
                src_ref=dst if src is None else src, dst_ref=dst, send_sem=send_sems.at[a, k],
                recv_sem=recv_sems.at[a, k], device_id=to, device_id_type=MESH)

        mine, first, passed = [], [], []
        for a in range(n):
            m = pltpu.make_async_copy(ins[a], outs[a].at[_slot(me)], lsems.at[a])
            m.start()
            mine.append(m)
            first.append(cp(a, 0, me, sib, src=ins[a]))
            for j, k in enumerate(chips):
                first.append(cp(a, 1 + j, me, _flip(me, k), src=ins[a]))
        for f in first:
            f.start()
        for j, k in enumerate(chips):
            for a in range(n):
                cp(a, 1 + j, _flip(me, k), me).wait_recv()
                p = cp(a, 4 + j, _flip(me, k), sib)
                p.start()
                passed.append(p)
        for a in range(n):
            cp(a, 0, sib, me).wait_recv()
            for j, k in enumerate(chips):
                cp(a, 4 + j, _flip(sib, k), me).wait_recv()
        for f in first + passed:
            f.wait_send()
        for m in mine:
            m.wait()

    hbm = pl.BlockSpec(memory_space=pltpu.HBM)
    return pl.pallas_call(
        body, name="gather_weights",
        out_shape=[jax.ShapeDtypeStruct((8,) + h.shape, h.dtype) for h in halves],
        in_specs=[hbm] * n, out_specs=[hbm] * n,
        scratch_shapes=[pltpu.SemaphoreType.DMA((n, 7)), pltpu.SemaphoreType.DMA((n, 7)),
                        pltpu.SemaphoreType.DMA((n,))],
    )(*halves)


def _scatter_grads(grads):
    n = len(grads)

    def body(*refs):
        ins, outs = refs[:n], refs[n:2 * n]
        send_sems, recv_sems, lsems = refs[2 * n:]
        me = _my_pos()
        mine, sends = [], []
        for a in range(n):
            m = pltpu.make_async_copy(ins[a].at[_slot(me)], outs[a].at[_slot(me)], lsems.at[a])
            m.start()
            mine.append(m)
            for k in range(1, 8):
                peer = _flip(me, k)
                s = pltpu.make_async_remote_copy(
                    src_ref=ins[a].at[_slot(peer)], dst_ref=outs[a].at[_slot(me)],
                    send_sem=send_sems.at[a, k - 1], recv_sem=recv_sems.at[a, k - 1],
                    device_id=peer, device_id_type=MESH)
                s.start()
                sends.append(s)
        for a in range(n):
            for k in range(1, 8):
                peer = _flip(me, k)
                pltpu.make_async_remote_copy(
                    src_ref=ins[a].at[_slot(peer)], dst_ref=outs[a].at[_slot(peer)],
                    send_sem=send_sems.at[a, k - 1], recv_sem=recv_sems.at[a, k - 1],
                    device_id=peer, device_id_type=MESH).wait_recv()
        for s in sends:
            s.wait_send()
        for m in mine:
            m.wait()

    hbm = pl.BlockSpec(memory_space=pltpu.HBM)
    return pl.pallas_call(
        body, name="scatter_grads",
        out_shape=[jax.ShapeDtypeStruct(g.shape, g.dtype) for g in grads],
        in_specs=[hbm] * n, out_specs=[hbm] * n,
        scratch_shapes=[pltpu.SemaphoreType.DMA((n, 7)), pltpu.SemaphoreType.DMA((n, 7)),
                        pltpu.SemaphoreType.DMA((n,))],
    )(*grads)


def _sibling_exchange(halves):
    n = len(halves)

    def body(*refs):
        ins, outs = refs[:n], refs[n:2 * n]
        send_sems, recv_sems, lsems = refs[2 * n:]
        x, y, c = _my_pos()
        sib = (x, y, 1 - c)
        mine, sends = [], []
        for a in range(n):
            m = pltpu.make_async_copy(ins[a], outs[a].at[c], lsems.at[a])
            m.start()
            mine.append(m)
            s = pltpu.make_async_remote_copy(
                src_ref=ins[a], dst_ref=outs[a].at[c], send_sem=send_sems.at[a], recv_sem=recv_sems.at[a],
                device_id=sib, device_id_type=MESH)
            s.start()
            sends.append(s)
        for a in range(n):
            pltpu.make_async_remote_copy(
                src_ref=ins[a], dst_ref=outs[a].at[1 - c], send_sem=send_sems.at[a], recv_sem=recv_sems.at[a],
                device_id=sib, device_id_type=MESH).wait_recv()
        for s in sends:
            s.wait_send()
        for m in mine:
            m.wait()

    hbm = pl.BlockSpec(memory_space=pltpu.HBM)
    return pl.pallas_call(
        body, name="sibling_exchange",
        out_shape=[jax.ShapeDtypeStruct((2,) + h.shape, h.dtype) for h in halves],
        in_specs=[hbm] * n, out_specs=[hbm] * n,
        scratch_shapes=[pltpu.SemaphoreType.DMA((n,)), pltpu.SemaphoreType.DMA((n,)),
                        pltpu.SemaphoreType.DMA((n,))],
    )(*halves)


def _row_tile(r, cap=512):
    for t in (cap, 256, 128, 64, 32, 16, 8):
        if t <= cap and r % t == 0:
            return t
    return r


def _cast_bf16(a, name):
    R, W = a.shape
    tr = _row_tile(R, 256)

    def body(a_ref, o_ref):
        o_ref[...] = a_ref[...].astype(BF16)

    return pl.pallas_call(
        body, name=name, grid=(R // tr,), out_shape=jax.ShapeDtypeStruct((R, W), BF16),
        in_specs=[pl.BlockSpec((tr, W), lambda i: (i, 0))], out_specs=pl.BlockSpec((tr, W), lambda i: (i, 0)),
    )(a)


def _sum_slots(r, name):
    _, R, W = r.shape
    tr = _row_tile(R, 128)

    def body(r_ref, o_ref):
        acc = r_ref[0]
        for s in range(1, 8):
            acc = acc + r_ref[s]
        o_ref[...] = acc

    return pl.pallas_call(
        body, name=name, grid=(R // tr,), out_shape=jax.ShapeDtypeStruct((R, W), F32),
        in_specs=[pl.BlockSpec((8, tr, W), lambda i: (0, i, 0))], out_specs=pl.BlockSpec((tr, W), lambda i: (i, 0)),
    )(r)


def _adamw(w, g, m, v, name):
    R, W = w.shape
    tr = _row_tile(R, 256)
    c1 = 1.0 / (1.0 - ADAM_B1 ** ADAM_STEP)
    c2 = 1.0 / (1.0 - ADAM_B2 ** ADAM_STEP)

    def body(w_ref, g_ref, m_ref, v_ref, d_ref, nm_ref, nv_ref):
        gg = g_ref[...]
        nm = ADAM_B1 * m_ref[...] + (1.0 - ADAM_B1) * gg
        nv = ADAM_B2 * v_ref[...] + (1.0 - ADAM_B2) * (gg * gg)
        d_ref[...] = -ADAM_LR * ((nm * c1) / (jnp.sqrt(nv * c2) + ADAM_EPS) + ADAM_WD * w_ref[...])
        nm_ref[...] = nm
        nv_ref[...] = nv

    spec = pl.BlockSpec((tr, W), lambda i: (i, 0))
    return pl.pallas_call(
        body, name=name, grid=(R // tr,), out_shape=[jax.ShapeDtypeStruct((R, W), F32)] * 3,
        in_specs=[spec] * 4, out_specs=[spec] * 3,
    )(w, g, m, v)


def _matmul_tn(a, b, tm, tn, tk, name, nn=1, b_map=None, out_shape=None, out_block=None, out_map=None,
               b_block=None):
    K, M = a.shape
    nk = K // tk
    nm = M // tm

    def body(a_ref, b_ref, o_ref, acc_ref):
        kk = pl.program_id(2)

        @pl.when(kk == 0)
        def _():
            acc_ref[...] = jnp.zeros_like(acc_ref)

        acc_ref[...] += lax.dot_general(a_ref[...], b_ref[...], _TN, preferred_element_type=F32)

        @pl.when(kk == nk - 1)
        def _():
            o_ref[...] = acc_ref[...]

    return pl.pallas_call(
        body, name=name, grid=(nm, nn, nk),
        out_shape=jax.ShapeDtypeStruct(out_shape or (M, b.shape[-1]), F32),
        in_specs=[pl.BlockSpec((tk, tm), lambda i, j, k: (k, i)),
                  pl.BlockSpec(b_block or (tk, tn), b_map or (lambda i, j, k: (k, j)))],
        out_specs=pl.BlockSpec(out_block or (tm, tn), out_map or (lambda i, j, k: (i, j))),
        scratch_shapes=[pltpu.VMEM((tm, tn), F32)],
        compiler_params=_cp(vmem=VMEM_BIG),
    )(a, b)


def _ada_fwd(cin, w_ada_s, b_s):
    tn = 512

    def body(c_ref, w_ref, b_ref, o_ref):
        o_ref[...] = _dg(_silu(c_ref[...]), w_ref[...], _NN) + b_ref[...]

    return pl.pallas_call(
        body, name="ada_fwd", grid=(1536 // tn,), out_shape=jax.ShapeDtypeStruct((64, 1536), F32),
        in_specs=[pl.BlockSpec((64, D), lambda j: (0, 0)), pl.BlockSpec((D, tn), lambda j: (0, j)),
                  pl.BlockSpec((1, tn), lambda j: (0, j))],
        out_specs=pl.BlockSpec((64, tn), lambda j: (0, j)),
        compiler_params=_cp(vmem=VMEM_BIG),
    )(cin, w_ada_s, b_s)


def _ada_bwd(cin, gmod_s, w_ada_s):
    tn = 512

    def body(c_ref, g_ref, w_ref, gw_ref, pc_ref):
        j = pl.program_id(0)
        g = g_ref[...]
        gw_ref[...] = _dg(_silu(c_ref[...]), g, _TN)
        rr = lax.broadcasted_iota(jnp.int32, (8, 64), 0)
        cc = lax.broadcasted_iota(jnp.int32, (8, 64), 1)
        sel = jnp.where((rr == 0) & ((cc & 7) == 2), 1.0, 0.0).astype(F32)
        dctx = _hmm(sel, g)
        part = _dg(dctx, w_ref[...], _NT)

        @pl.when(j == 0)
        def _():
            pc_ref[...] = jnp.zeros_like(pc_ref)

        pc_ref[...] += part

    return pl.pallas_call(
        body, name="ada_bwd", grid=(1536 // tn,),
        out_shape=[jax.ShapeDtypeStruct((D, 1536), F32), jax.ShapeDtypeStruct((8, D), F32)],
        in_specs=[pl.BlockSpec((64, D), lambda j: (0, 0)), pl.BlockSpec((64, tn), lambda j: (0, j)),
                  pl.BlockSpec((D, tn), lambda j: (0, j))],
        out_specs=[pl.BlockSpec((D, tn), lambda j: (0, j)), pl.BlockSpec((8, D), lambda j: (0, 0))],
        compiler_params=_cp(vmem=VMEM_BIG),
    )(cin, gmod_s, w_ada_s)


def _ada_small(gmod, parts, c_ctx):
    def body(g_ref, p_ref, c_ref, gb_ref, gc_ref):
        gb_ref[...] = jnp.sum(g_ref[...], axis=0, keepdims=True)
        ds = p_ref[0][0:1] + p_ref[2][0:1] + p_ref[4][0:1] + p_ref[6][0:1]
        cc = c_ref[...]
        s = _sigmoid(cc)
        gc_ref[...] = ds * (s * (1.0 + cc * (1.0 - s)))

    vm = pl.BlockSpec(memory_space=pltpu.VMEM)
    return pl.pallas_call(
        body, name="ada_small",
        out_shape=[jax.ShapeDtypeStruct((1, 6 * D), F32), jax.ShapeDtypeStruct((1, D), F32)],
        in_specs=[vm, vm, vm], out_specs=[vm, vm],
    )(gmod, parts, c_ctx)


def _tok_specs():
    xs = pl.BlockSpec((None, RT, D), lambda b, j: (b, jnp.maximum(j - 1, 0), 0))
    cs = pl.BlockSpec((None, RT, D), lambda b, j: (b, 0, 0))
    ms = pl.BlockSpec((None, 1, D), lambda b, j: (b, 0, 0))
    cms = pl.BlockSpec((None, 1, D), lambda b, j: (0, 0, 0))
    return xs, cs, ms, cms


def _inproj_fwd(x, ctx, sh1, sc1, csh1, csc1, n1w, w_in_p):
    def body(x_ref, c_ref, sh_ref, sc_ref, csh_ref, csc_ref, w1_ref, w_ref, p_ref, hm_ref):
        is_ctx = pl.program_id(1) == 0
        xt = jnp.where(is_ctx, c_ref[...], x_ref[...])
        sh = jnp.where(is_ctx, csh_ref[...], sh_ref[...])
        sc = jnp.where(is_ctx, csc_ref[...], sc_ref[...])
        hb = _norm_mod(xt, w1_ref[...], sh, sc).astype(BF16)
        hm_ref[...] = hb
        p_ref[...] = jnp.dot(hb, w_ref[...], preferred_element_type=F32)

    xs, cs, ms, cms = _tok_specs()
    return pl.pallas_call(
        body, name="inproj_fwd", grid=(NB, T // RT),
        out_shape=[jax.ShapeDtypeStruct((NB, T, PC), F32), jax.ShapeDtypeStruct((NB, T, D), BF16)],
        in_specs=[xs, cs, ms, ms, cms, cms, pl.BlockSpec((1, D), lambda b, j: (0, 0)),
                  pl.BlockSpec((D, PC), lambda b, j: (0, 0))],
        out_specs=[pl.BlockSpec((None, RT, PC), lambda b, j: (b, j, 0)),
                   pl.BlockSpec((None, RT, D), lambda b, j: (b, j, 0))],
        compiler_params=_cp(vmem=VMEM_BIG),
    )(x, ctx, sh1, sc1, csh1, csc1, n1w, w_in_p)


def _inproj_bwd(dp_dn, dz, dp_na, dgt, w_in_p, x, ctx, sh1, sc1, csh1, csc1, n1w, dx1):
    def body(d1_ref, d2_ref, d3_ref, d4_ref, w_ref, x_ref, c_ref, sh_ref, sc_ref, csh_ref, csc_ref, w1_ref,
             dx1_ref, dpb_ref, gx_ref, dsh_ref, dsc_ref, dcsh_ref, dcsc_ref, dw1_ref):
        b, j = pl.program_id(0), pl.program_id(1)
        is_ctx = j == 0
        dp = jnp.concatenate([d1_ref[...], d2_ref[...], d3_ref[...], d4_ref[...]], axis=1).astype(BF16)
        dpb_ref[...] = dp
        dhm = lax.dot_general(dp, w_ref[...], _NT, preferred_element_type=F32)
        xt = jnp.where(is_ctx, c_ref[...], x_ref[...])
        sh = jnp.where(is_ctx, csh_ref[...], sh_ref[...])
        sc = jnp.where(is_ctx, csc_ref[...], sc_ref[...])
        _, vjp = jax.vjp(_norm_mod, xt, w1_ref[...], sh, sc)
        dxt, dw1, dsh, dsc = vjp(dhm)

        @pl.when((b == 0) & (j == 0))
        def _():
            dw1_ref[...] = jnp.zeros_like(dw1_ref)
            dcsh_ref[...] = jnp.zeros_like(dcsh_ref)
            dcsc_ref[...] = jnp.zeros_like(dcsc_ref)

        dw1_ref[...] += dw1

        @pl.when(is_ctx)
        def _():
            dcsh_ref[...] += dsh
            dcsc_ref[...] += dsc

        @pl.when(j == 1)
        def _():
            dsh_ref[...] = jnp.zeros_like(dsh_ref)
            dsc_ref[...] = jnp.zeros_like(dsc_ref)

        @pl.when(j >= 1)
        def _():
            dsh_ref[...] += dsh
            dsc_ref[...] += dsc
            gx_ref[...] = dx1_ref[...] + dxt

    xs, cs, ms, cms = _tok_specs()
    tsp = lambda w: pl.BlockSpec((None, RT, w), lambda b, j: (b, j, 0))
    one = pl.BlockSpec((1, D), lambda b, j: (0, 0))
    return pl.pallas_call(
        body, name="inproj_bwd", grid=(NB, T // RT),
        out_shape=[jax.ShapeDtypeStruct((NB, T, PC), BF16), jax.ShapeDtypeStruct((NB, SEQ, D), F32),
                   jax.ShapeDtypeStruct((NB, 1, D), F32), jax.ShapeDtypeStruct((NB, 1, D), F32),
                   jax.ShapeDtypeStruct((1, 1, D), F32), jax.ShapeDtypeStruct((1, 1, D), F32),
                   jax.ShapeDtypeStruct((1, D), F32)],
        in_specs=[tsp(1536), tsp(512), tsp(1536), tsp(128), pl.BlockSpec((D, PC), lambda b, j: (0, 0)),
                  xs, cs, ms, ms, cms, cms, one, xs],
        out_specs=[tsp(PC), xs, ms, ms, cms, cms, one],
        compiler_params=_cp(vmem=VMEM_BIG),
    )(dp_dn, dz, dp_na, dgt, w_in_p, x, ctx, sh1, sc1, csh1, csc1, n1w, dx1)


def _seg_valid(t, off):
    lo = jnp.where(t < CT, 0, CT)
    hi = jnp.where(t < CT, CT, T)
    return (t + off >= lo) & (t + off < hi)


def _shifted(u, off):
    t = lax.broadcasted_iota(jnp.int32, (T, 1), 0)
    s = u if off == 0 else pltpu.roll(u, (-off) % T, 0)
    return jnp.where(_seg_valid(t, off), s, 0.0)


def _conv_pre(u, w):
    acc = _shifted(u, -2) * w[0:1, :]
    for kk in range(1, 5):
        acc = acc + _shifted(u, kk - 2) * w[kk:kk + 1, :]
    return acc


def _act_q(pre):
    a = _silu(pre)
    return a * lax.rsqrt(jnp.sum(a * a, axis=-1, keepdims=True) + EPS) * (128.0 ** -0.5)


def _act_k(pre):
    a = _silu(pre)
    return a * lax.rsqrt(jnp.sum(a * a, axis=-1, keepdims=True) + EPS)


def _dn_prep_fwd(p, cw):
    def body(pq_ref, pk_ref, pv_ref, wq_ref, wk_ref, wv_ref, q_ref, k_ref, v_ref):
        q_ref[...] = _act_q(_conv_pre(pq_ref[...], wq_ref[...]))
        k_ref[...] = _act_k(_conv_pre(pk_ref[...], wk_ref[...]))
        v_ref[...] = _silu(_conv_pre(pv_ref[...], wv_ref[...]))

    ps = lambda o: pl.BlockSpec((None, T, 128), lambda b, h: (b, 0, o + h))
    ws = lambda o: pl.BlockSpec((8, 128), lambda b, h: (0, o + h))
    os_ = pl.BlockSpec((None, T, 128), lambda b, h: (b, 0, h))
    return pl.pallas_call(
        body, name="dn_prep_fwd", grid=(NB, DNH), out_shape=[jax.ShapeDtypeStruct((NB, T, DNW), F32)] * 3,
        in_specs=[ps(0), ps(4), ps(8), ws(0), ws(4), ws(8)], out_specs=[os_] * 3,
        compiler_params=_cp(vmem=VMEM_BIG),
    )(p, p, p, cw, cw, cw)


def _dn_prep_bwd(p, cw, dq, dk, dv):
    def body(pq_ref, pk_ref, pv_ref, wq_ref, wk_ref, wv_ref, dq_ref, dk_ref, dv_ref,
             oq_ref, ok_ref, ov_ref, gq_ref, gk_ref, gv_ref):
        b = pl.program_id(1)

        def one(p_ref, w_ref, d_ref, act, o_ref, g_ref):
            u = p_ref[...]
            w = w_ref[...]
            _, vjp = jax.vjp(act, _conv_pre(u, w))
            (dpre,) = vjp(d_ref[0] + d_ref[1])
            du = _shifted(dpre, 2) * w[0:1, :]
            rows = [jnp.sum(dpre * _shifted(u, -2), axis=0, keepdims=True)]
            for kk in range(1, 5):
                du = du + _shifted(dpre, 2 - kk) * w[kk:kk + 1, :]
                rows.append(jnp.sum(dpre * _shifted(u, kk - 2), axis=0, keepdims=True))
            o_ref[...] = du
            gw = jnp.concatenate(rows + [jnp.zeros((3, 128), F32)], axis=0)

            @pl.when(b == 0)
            def _():
                g_ref[...] = gw

            @pl.when(b > 0)
            def _():
                g_ref[...] += gw

        one(pq_ref, wq_ref, dq_ref, _act_q, oq_ref, gq_ref)
        one(pk_ref, wk_ref, dk_ref, _act_k, ok_ref, gk_ref)
        one(pv_ref, wv_ref, dv_ref, _silu, ov_ref, gv_ref)

    ps = lambda o: pl.BlockSpec((None, T, 128), lambda h, b: (b, 0, o + h))
    ws = lambda o: pl.BlockSpec((8, 128), lambda h, b: (0, o + h))
    ds = pl.BlockSpec((2, None, T, 128), lambda h, b: (0, b, 0, h))
    outs = pl.pallas_call(
        body, name="dn_prep_bwd", grid=(DNH, NB),
        out_shape=[jax.ShapeDtypeStruct((NB, T, DNW), F32)] * 3 + [jax.ShapeDtypeStruct((8, DNW), F32)] * 3,
        in_specs=[ps(0), ps(4), ps(8), ws(0), ws(4), ws(8), ds, ds, ds],
        out_specs=[pl.BlockSpec((None, T, 128), lambda h, b: (b, 0, h))] * 3
        + [pl.BlockSpec((8, 128), lambda h, b: (0, h))] * 3,
        compiler_params=_cp(vmem=VMEM_BIG),
    )(p, p, p, cw, cw, cw, dq, dk, dv)
    return outs


def _gate_fn(a, prm):
    lane = lax.broadcasted_iota(jnp.int32, a.shape, 1)
    z = a + prm[1:2, :]
    sp = jnp.maximum(z, 0.0) + jnp.log(1.0 + jnp.exp(-jnp.abs(z)))
    g = -jnp.exp(prm[0:1, :]) * sp
    return jnp.where(lane < 8, g, jnp.where(lane < 16, _sigmoid(a), 0.0))


def _gates_fwd(p, prm):
    def body(a_ref, prm_ref, o_ref):
        o_ref[...] = _gate_fn(a_ref[...], prm_ref[...])

    return pl.pallas_call(
        body, name="gates_fwd", grid=(NB,), out_shape=jax.ShapeDtypeStruct((NB, T, 128), F32),
        in_specs=[pl.BlockSpec((None, T, 128), lambda b: (b, 0, COL_G // 128)), pl.BlockSpec((8, 128), lambda b: (0, 0))],
        out_specs=pl.BlockSpec((None, T, 128), lambda b: (b, 0, 0)),
    )(p, prm)


def _gates_bwd(p, prm, dg):
    def body(a_ref, prm_ref, dg_ref, o_ref, dprm_ref):
        b = pl.program_id(0)
        tot = dg_ref[0, 0]
        for d in range(2):
            for h in range(DNH):
                if d or h:
                    tot = tot + dg_ref[d, h]
        _, vjp = jax.vjp(_gate_fn, a_ref[...], prm_ref[...])
        da, dprm = vjp(tot)
        o_ref[...] = da

        @pl.when(b == 0)
        def _():
            dprm_ref[...] = dprm

        @pl.when(b > 0)
        def _():
            dprm_ref[...] += dprm

    return pl.pallas_call(
        body, name="gates_bwd", grid=(NB,),
        out_shape=[jax.ShapeDtypeStruct((NB, T, 128), F32), jax.ShapeDtypeStruct((8, 128), F32)],
        in_specs=[pl.BlockSpec((None, T, 128), lambda b: (b, 0, COL_G // 128)), pl.BlockSpec((8, 128), lambda b: (0, 0)),
                  pl.BlockSpec((2, DNH, None, T, 128), lambda b: (0, 0, b, 0, 0))],
        out_specs=[pl.BlockSpec((None, T, 128), lambda b: (b, 0, 0)), pl.BlockSpec((8, 128), lambda b: (0, 0))],
        compiler_params=_cp(vmem=VMEM_BIG),
    )(p, prm, dg)


def _tri_inv_impl(m, md, m1, m2, eye):
    n = -(m * md)
    t = eye + n
    pw = n
    for _ in range(3):
        pw = _hmm(pw, pw)
        t = t + _hmm(t, pw)
    t = t - _hmm(_hmm(t, m * m1), t)
    t = t - _hmm(_hmm(t, m * m2), t)
    return t


@jax.custom_vjp
def _tri_inv(m, md, m1, m2, eye):
    return _tri_inv_impl(m, md, m1, m2, eye)


def _tri_inv_f(m, md, m1, m2, eye):
    t = _tri_inv_impl(m, md, m1, m2, eye)
    return t, (t, md)


def _tri_inv_b(res, g):
    t, md = res
    tt = t.T
    z = jnp.zeros_like(md)
    return (-_hmm(_hmm(tt, g), tt), z, z, z, z)


_tri_inv.defvjp(_tri_inv_f, _tri_inv_b)


def _chunk_consts(d, h):
    ii = lax.broadcasted_iota(jnp.int32, (CH, CH), 0)
    jj = lax.broadcasted_iota(jnp.int32, (CH, CH), 1)
    diff = jnp.where(d == 0, ii - jj, jj - ii)
    f = lambda m: jnp.where(m, 1.0, 0.0).astype(F32)
    incl, strict = f(diff >= 0), f(diff > 0)
    md = f((ii >> 4) == (jj >> 4))
    m1 = f(((ii >> 5) == (jj >> 5)) & ((ii >> 4) != (jj >> 4)))
    m2 = f((ii >> 5) != (jj >> 5))
    eye = f(ii == jj)
    lane = lax.broadcasted_iota(jnp.int32, (1, 128), 1)
    sub = lax.broadcasted_iota(jnp.int32, (128, 1), 0)
    gl = d * DNH + h
    return dict(incl=incl, strict=strict, md=md, m1=m1, m2=m2, eye=eye,
                oh_g=f(lane == gl), oh_b=f(lane == gl + 8), oh_gt=f(sub == gl))


def _chunk_fn(q, k, v, gb, s, cst):
    incl = cst["incl"]
    g = jnp.sum(gb * cst["oh_g"], axis=1, keepdims=True)
    beta = jnp.sum(gb * cst["oh_b"], axis=1, keepdims=True)
    gcum = _hmm(incl, gb)
    gam = jnp.sum(gcum * cst["oh_g"], axis=1, keepdims=True)
    gam_row = jnp.sum(gcum.T * cst["oh_gt"], axis=0, keepdims=True)
    gtot = jnp.sum(g, axis=0, keepdims=True)
    dec = jnp.exp(jnp.where(incl > 0.5, gam - gam_row, NEG))
    e_gam = jnp.exp(gam)
    kb = k * beta
    m = cst["strict"] * (_mm_nt(kb, k) * dec)
    tm = _tri_inv(m, cst["md"], cst["m1"], cst["m2"], cst["eye"])
    u = _mm(tm, v * beta)
    w = _mm(tm, kb * e_gam)
    qk = _mm_nt(q, k) * dec
    v_new = u - _mm(w, s)
    o = _mm(q * e_gam, s) + _mm(qk, v_new)
    s_new = s * jnp.exp(gtot) + _mm_tn(k * jnp.exp(gtot - gam), v_new)
    return o, s_new


def _chunk_of(d, s):
    return jnp.where(d == 0, s, jnp.where(s < NCH_CTX, NCH_CTX - 1 - s, NCH + NCH_CTX - 1 - s))


def _scan_fwd(q, k, v, gates):
    def body(q_ref, k_ref, v_ref, g_ref, o_ref, st_ref, s_ref):
        h, d, s = pl.program_id(1), pl.program_id(2), pl.program_id(3)

        @pl.when(s == 0)
        def _():
            s_ref[...] = jnp.zeros_like(s_ref)

        st = s_ref[...]
        st_ref[...] = st
        o, s_new = _chunk_fn(q_ref[...], k_ref[...], v_ref[...], g_ref[...], st, _chunk_consts(d, h))
        o_ref[...] = o
        s_ref[...] = s_new

    tok = pl.BlockSpec((None, CH, 128), lambda b, h, d, s: (b, _chunk_of(d, s), h))
    return pl.pallas_call(
        body, name="scan_fwd", grid=(NB, DNH, 2, NCH),
        out_shape=[jax.ShapeDtypeStruct((2, NB, T, DNW), F32),
                   jax.ShapeDtypeStruct((2, NB, DNH, NCH, 128, 128), F32)],
        in_specs=[tok, tok, tok, pl.BlockSpec((None, CH, 128), lambda b, h, d, s: (b, _chunk_of(d, s), 0))],
        out_specs=[pl.BlockSpec((None, None, CH, 128), lambda b, h, d, s: (d, b, _chunk_of(d, s), h)),
                   pl.BlockSpec((None, None, None, None, 128, 128), lambda b, h, d, s: (d, b, h, _chunk_of(d, s), 0, 0))],
        scratch_shapes=[pltpu.VMEM((128, 128), F32)],
    )(q, k, v, gates)


def _scan_bwd(q, k, v, gates, states, do):
    def body(q_ref, k_ref, v_ref, g_ref, st_ref, do_ref, dq_ref, dk_ref, dv_ref, dg_ref, ds_ref):
        h, d, s = pl.program_id(1), pl.program_id(2), pl.program_id(3)

        @pl.when(s == 0)
        def _():
            ds_ref[...] = jnp.zeros_like(ds_ref)

        cst = _chunk_consts(d, h)
        _, vjp = jax.vjp(lambda a, b_, c, e, f: _chunk_fn(a, b_, c, e, f, cst),
                         q_ref[...], k_ref[...], v_ref[...], g_ref[...], st_ref[...])
        dq, dk, dv, dg, ds = vjp((do_ref[...], ds_ref[...]))
        dq_ref[...] = dq
        dk_ref[...] = dk
        dv_ref[...] = dv
        dg_ref[...] = dg
        ds_ref[...] = ds

    ch = lambda d, s: _chunk_of(d, NCH - 1 - s)
    tok = pl.BlockSpec((None, CH, 128), lambda b, h, d, s: (b, ch(d, s), h))
    dtok = pl.BlockSpec((None, None, CH, 128), lambda b, h, d, s: (d, b, ch(d, s), h))
    return pl.pallas_call(
        body, name="scan_bwd", grid=(NB, DNH, 2, NCH),
        out_shape=[jax.ShapeDtypeStruct((2, NB, T, DNW), F32)] * 3
        + [jax.ShapeDtypeStruct((2, DNH, NB, T, 128), F32)],
        in_specs=[tok, tok, tok, pl.BlockSpec((None, CH, 128), lambda b, h, d, s: (b, ch(d, s), 0)),
                  pl.BlockSpec((None, None, None, None, 128, 128), lambda b, h, d, s: (d, b, h, ch(d, s), 0, 0)),
                  tok],
        out_specs=[dtok, dtok, dtok,
                   pl.BlockSpec((None, None, None, CH, 128), lambda b, h, d, s: (d, h, b, ch(d, s), 0))],
        scratch_shapes=[pltpu.VMEM((128, 128), F32)],
    )(q, k, v, gates, states, do)


def _segmean(a, bd):
    hi = a.astype(BF16)
    lo = (a - hi.astype(F32)).astype(BF16)
    return jnp.dot(hi, bd, preferred_element_type=F32) + jnp.dot(lo, bd, preferred_element_type=F32)


def _na_prep_fwd(p, qw, kw, bd):
    def body(q_ref, k_ref, v_ref, qw_ref, kw_ref, bd_ref, oq_ref, ok_ref, ov_ref):
        bdm = bd_ref[...]
        q = q_ref[...]
        k = k_ref[...]
        oq_ref[...] = (q * lax.rsqrt(_segmean(q * q, bdm) + EPS) * qw_ref[...] * 0.125).astype(BF16)
        ok_ref[...] = (k * lax.rsqrt(_segmean(k * k, bdm) + EPS) * kw_ref[...]).astype(BF16)
        ov_ref[...] = v_ref[...].astype(BF16)

    ps = lambda o: pl.BlockSpec((None, RT, NAW), lambda b, j: (b, j, COL_NA // NAW + o))
    one = pl.BlockSpec((1, NAW), lambda b, j: (0, 0))
    os_ = pl.BlockSpec((None, RT, NAW), lambda b, j: (b, j, 0))
    return pl.pallas_call(
        body, name="na_prep_fwd", grid=(NB, T // RT), out_shape=[jax.ShapeDtypeStruct((NB, T, NAW), BF16)] * 3,
        in_specs=[ps(0), ps(1), ps(2), one, one, pl.BlockSpec((NAW, NAW), lambda b, j: (0, 0))], out_specs=[os_] * 3,
    )(p, p, p, qw, kw, bd)


def _na_prep_bwd(p, qw, kw, bd, dq, dk, dv):
    def body(q_ref, k_ref, qw_ref, kw_ref, bd_ref, dq_ref, dk_ref, dv_ref, o_ref, gq_ref, gk_ref):
        b, j = pl.program_id(0), pl.program_id(1)
        bdm = bd_ref[...]

        def one(x, w, dy, scale):
            r = lax.rsqrt(_segmean(x * x, bdm) + EPS)
            g = dy * w * scale
            dx = r * g - x * (r * r * r) * _segmean(g * x, bdm)
            dw = jnp.sum(dy * x * r * scale, axis=0, keepdims=True)
            return dx, dw

        dqv = jnp.where(j == 0, 0.0, dq_ref[...])
        dxq, dwq = one(q_ref[...], qw_ref[...], dqv, 0.125)
        dxk, dwk = one(k_ref[...], kw_ref[...], dk_ref[...], 1.0)
        o_ref[...] = jnp.concatenate([dxq, dxk, dv_ref[...]], axis=1)
        pad = jnp.zeros((7, NAW), F32)

        @pl.when((b == 0) & (j == 0))
        def _():
            gq_ref[...] = jnp.zeros_like(gq_ref)
            gk_ref[...] = jnp.zeros_like(gk_ref)

        gq_ref[...] += jnp.concatenate([dwq, pad], axis=0)
        gk_ref[...] += jnp.concatenate([dwk, pad], axis=0)

    ps = lambda o: pl.BlockSpec((None, RT, NAW), lambda b, j: (b, j, COL_NA // NAW + o))
    one = pl.BlockSpec((1, NAW), lambda b, j: (0, 0))
    ts = pl.BlockSpec((None, RT, NAW), lambda b, j: (b, j, 0))
    acc = pl.BlockSpec((8, NAW), lambda b, j: (0, 0))
    return pl.pallas_call(
        body, name="na_prep_bwd", grid=(NB, T // RT),
        out_shape=[jax.ShapeDtypeStruct((NB, T, 1536), F32), jax.ShapeDtypeStruct((8, NAW), F32),
                   jax.ShapeDtypeStruct((8, NAW), F32)],
        in_specs=[ps(0), ps(1), one, one, pl.BlockSpec((NAW, NAW), lambda b, j: (0, 0)),
                  pl.BlockSpec((None, RT, NAW), lambda b, j: (b, jnp.maximum(j - 1, 0), 0)), ts, ts],
        out_specs=[pl.BlockSpec((None, RT, 1536), lambda b, j: (b, j, 0)), acc, acc],
        compiler_params=_cp(vmem=VMEM_BIG),
    )(p, p, qw, kw, bd, dq, dk, dv)


def _na_row0(r):
    return jnp.clip(r - WIN_ROWS // 2, 0, ROWS - WIN_ROWS)


def _na_cls(r):
    return _na_row0(r) - r + WIN_ROWS - 1


def _na_scores(q, kw, kc, bias, hh):
    lane = lax.broadcasted_iota(jnp.int32, (1, 128), 1)
    hm = ((lane >> 6) == hh)
    qh = jnp.where(hm, q, jnp.zeros_like(q))
    s_w = lax.dot_general(qh, kw, _NT, preferred_element_type=F32) + bias
    s_c = lax.dot_general(qh, kc, _NT, preferred_element_type=F32)
    mx = jnp.maximum(jnp.max(s_w, axis=1, keepdims=True), jnp.max(s_c, axis=1, keepdims=True))
    p_w = jnp.exp(s_w - mx)
    p_c = jnp.exp(s_c - mx)
    inv = 1.0 / (jnp.sum(p_w, axis=1, keepdims=True) + jnp.sum(p_c, axis=1, keepdims=True))
    return hm, qh, p_w, p_c, inv


def _na_fwd(qn, kn, vb, bias):
    def body(q_ref, k_ref, v_ref, b_ref, o_ref):
        r = pl.program_id(2)
        start = pl.multiple_of(CT + _na_row0(r) * GRID_W, GRID_W)
        q = q_ref[...]
        kw, vw = k_ref[pl.ds(start, NWIN), :], v_ref[pl.ds(start, NWIN), :]
        kc, vc = k_ref[0:CT, :], v_ref[0:CT, :]
        out = jnp.zeros((GRID_W, 128), F32)
        for hh in range(2):
            hm, _, p_w, p_c, inv = _na_scores(q, kw, kc, b_ref[hh], hh)
            o = (jnp.dot(p_w.astype(BF16), vw, preferred_element_type=F32)
                 + jnp.dot(p_c.astype(BF16), vc, preferred_element_type=F32)) * inv
            out = jnp.where(hm, o, out)
        o_ref[...] = out.astype(BF16)

    kv = pl.BlockSpec((None, T, 128), lambda g, b, r: (b, 0, g))
    return pl.pallas_call(
        body, name="na_fwd", grid=(NAH // 2, NB, ROWS), out_shape=jax.ShapeDtypeStruct((NB, SEQ, NAW), BF16),
        in_specs=[pl.BlockSpec((None, GRID_W, 128), lambda g, b, r: (b, CT // GRID_W + r, g)), kv, kv,
                  pl.BlockSpec((None, 2, GRID_W, NWIN), lambda g, b, r: (_na_cls(r), g, 0, 0))],
        out_specs=pl.BlockSpec((None, GRID_W, 128), lambda g, b, r: (b, r, g)),
    )(qn, kn, vb, bias)


def _na_bwd(qn, kn, vb, bias, do):
    def body(q_ref, k_ref, v_ref, b_ref, do_ref, dq_ref, dk_ref, dv_ref, db_ref):
        b, r = pl.program_id(1), pl.program_id(2)

        @pl.when((b == 0) & (r == 0))
        def _():
            db_ref[...] = jnp.zeros_like(db_ref)

        @pl.when(r == 0)
        def _():
            dk_ref[...] = jnp.zeros_like(dk_ref)
            dv_ref[...] = jnp.zeros_like(dv_ref)

        start = pl.multiple_of(CT + _na_row0(r) * GRID_W, GRID_W)
        cls = _na_cls(r)
        q = q_ref[...]
        kw, vw = k_ref[pl.ds(start, NWIN), :], v_ref[pl.ds(start, NWIN), :]
        kc, vc = k_ref[0:CT, :], v_ref[0:CT, :]
        dout = do_ref[...]
        dq = jnp.zeros((GRID_W, 128), F32)
        dkw = jnp.zeros((NWIN, 128), F32)
        dvw = jnp.zeros((NWIN, 128), F32)
        dkc = jnp.zeros((CT, 128), F32)
        dvc = jnp.zeros((CT, 128), F32)
        for hh in range(2):
            hm, qh, p_w, p_c, inv = _na_scores(q, kw, kc, b_ref[hh], hh)
            p_w = p_w * inv
            p_c = p_c * inv
            doh = jnp.where(hm, dout, 0.0).astype(BF16)
            dp_w = lax.dot_general(doh, vw, _NT, preferred_element_type=F32)
            dp_c = lax.dot_general(doh, vc, _NT, preferred_element_type=F32)
            delta = jnp.sum(p_w * dp_w, axis=1, keepdims=True) + jnp.sum(p_c * dp_c, axis=1, keepdims=True)
            ds_w = p_w * (dp_w - delta)
            ds_c = p_c * (dp_c - delta)
            db_ref[cls, hh] += ds_w
            ds_wb, ds_cb = ds_w.astype(BF16), ds_c.astype(BF16)
            dqh = jnp.dot(ds_wb, kw, preferred_element_type=F32) + jnp.dot(ds_cb, kc, preferred_element_type=F32)
            dq = jnp.where(hm, dqh, dq)
            dkw = dkw + lax.dot_general(ds_wb, qh, _TN, preferred_element_type=F32)
            dkc = dkc + lax.dot_general(ds_cb, qh, _TN, preferred_element_type=F32)
            dvw = dvw + lax.dot_general(p_w.astype(BF16), doh, _TN, preferred_element_type=F32)
            dvc = dvc + lax.dot_general(p_c.astype(BF16), doh, _TN, preferred_element_type=F32)
        dq_ref[...] = dq
        dk_ref[pl.ds(start, NWIN), :] += dkw
        dv_ref[pl.ds(start, NWIN), :] += dvw
        dk_ref[0:CT, :] += dkc
        dv_ref[0:CT, :] += dvc

    kv = pl.BlockSpec((None, T, 128), lambda g, b, r: (b, 0, g))
    return pl.pallas_call(
        body, name="na_bwd", grid=(NAH // 2, NB, ROWS),
        out_shape=[jax.ShapeDtypeStruct((NB, SEQ, NAW), F32)] + [jax.ShapeDtypeStruct((NB, T, NAW), F32)] * 2
        + [jax.ShapeDtypeStruct((WIN_ROWS, NAH, GRID_W, NWIN), F32)],
        in_specs=[pl.BlockSpec((None, GRID_W, 128), lambda g, b, r: (b, CT // GRID_W + r, g)), kv, kv,
                  pl.BlockSpec((None, 2, GRID_W, NWIN), lambda g, b, r: (_na_cls(r), g, 0, 0)),
                  pl.BlockSpec((None, GRID_W, 128), lambda g, b, r: (b, r, g))],
        out_specs=[pl.BlockSpec((None, GRID_W, 128), lambda g, b, r: (b, r, g)), kv, kv,
                   pl.BlockSpec((WIN_ROWS, 2, GRID_W, NWIN), lambda g, b, r: (0, g, 0, 0))],
        compiler_params=_cp(vmem=VMEM_BIG),
    )(qn, kn, vb, bias, do)


def _na_tables():
    cols = np.arange(GRID_W)
    win_start = np.clip(cols - WIN_COLS // 2, 0, GRID_W - WIN_COLS)
    kc = cols[None, :]
    valid = (kc >= win_start[:, None]) & (kc < win_start[:, None] + WIN_COLS)
    rel = np.clip(kc - cols[:, None] + WIN_COLS - 1, 0, 2 * WIN_COLS - 2)
    return valid, rel


def _bias_table(rpb):
    valid, rel = _na_tables()
    t15 = jnp.where(valid[None, None], rpb[:, :, rel], NEG)
    tab = jnp.stack([t15[:, c:c + WIN_ROWS] for c in range(WIN_ROWS)], axis=0)
    return tab.transpose(0, 1, 3, 2, 4).reshape(WIN_ROWS, NAH, GRID_W, NWIN)


def _rpb_fold(db):
    def body1(d_ref, o_ref):
        for dr in range(2 * WIN_ROWS - 1):
            acc = None
            for c in range(WIN_ROWS):
                i = dr - c
                if 0 <= i < WIN_ROWS:
                    blk = d_ref[c, :, i * GRID_W:(i + 1) * GRID_W]
                    acc = blk if acc is None else acc + blk
            o_ref[dr] = acc

    t15 = pl.pallas_call(
        body1, name="rpb_fold_rows", grid=(NAH,),
        out_shape=jax.ShapeDtypeStruct((NAH, 2 * WIN_ROWS - 1, GRID_W, GRID_W), F32),
        in_specs=[pl.BlockSpec((WIN_ROWS, None, GRID_W, NWIN), lambda h: (0, h, 0, 0))],
        out_specs=pl.BlockSpec((None, 2 * WIN_ROWS - 1, GRID_W, GRID_W), lambda h: (h, 0, 0, 0)),
        compiler_params=_cp(vmem=VMEM_BIG),
    )(db)
    valid, rel = _na_tables()
    onehot = np.zeros((GRID_W * GRID_W, 128), np.float32)
    flat_rel, flat_valid = rel.reshape(-1), valid.reshape(-1)
    onehot[np.arange(GRID_W * GRID_W)[flat_valid], flat_rel[flat_valid]] = 1.0

    def body2(a_ref, oh_ref, o_ref):
        o_ref[...] = _hmm(a_ref[...], oh_ref[...])

    vm = pl.BlockSpec(memory_space=pltpu.VMEM)
    out = pl.pallas_call(
        body2, name="rpb_fold_cols", out_shape=jax.ShapeDtypeStruct((NAH * 15, 128), F32),
        in_specs=[vm, vm], out_specs=vm, compiler_params=_cp(vmem=VMEM_BIG),
    )(t15.reshape(NAH * 15, GRID_W * GRID_W), jnp.asarray(onehot))
    return out[:, :31].reshape(NAH, 15, 31)


def _dn_out(o, z, w):
    outs = []
    for h in range(DNH):
        sl = slice(128 * h, 128 * h + 128)
        oh = o[:, sl]
        r = lax.rsqrt(jnp.mean(oh * oh, axis=-1, keepdims=True) + EPS)
        outs.append(oh * r * w[:, sl] * _silu(z[:, sl]))
    return jnp.concatenate(outs, axis=1)


def _outproj_fwd(o2, p, na, onw, x, g1, w_out):
    def body(o_ref, z_ref, na_ref, w_ref, x_ref, g_ref, wo_ref, x1_ref, ai_ref, ao_ref):
        dn = _dn_out(o_ref[0] + o_ref[1], z_ref[...], w_ref[...])
        ai = jnp.concatenate([dn.astype(BF16), na_ref[...]], axis=1)
        ai_ref[...] = ai
        ao = jnp.dot(ai, wo_ref[...], preferred_element_type=F32)
        ao_ref[...] = ao
        x1_ref[...] = x_ref[...] + g_ref[...] * ao

    lat = lambda w: pl.BlockSpec((None, RT, w), lambda b, j: (b, j, 0))
    return pl.pallas_call(
        body, name="outproj_fwd", grid=(NB, SEQ // RT),
        out_shape=[jax.ShapeDtypeStruct((NB, SEQ, D), F32), jax.ShapeDtypeStruct((NB, SEQ, D), BF16),
                   jax.ShapeDtypeStruct((NB, SEQ, D), F32)],
        in_specs=[pl.BlockSpec((2, None, RT, DNW), lambda b, j: (0, b, j + 1, 0)),
                  pl.BlockSpec((None, RT, DNW), lambda b, j: (b, j + 1, COL_Z // DNW)),
                  lat(NAW), pl.BlockSpec((1, DNW), lambda b, j: (0, 0)), lat(D),
                  pl.BlockSpec((None, 1, D), lambda b, j: (b, 0, 0)), pl.BlockSpec((D, D), lambda b, j: (0, 0))],
        out_specs=[lat(D), lat(D), lat(D)],
        compiler_params=_cp(vmem=VMEM_BIG),
    )(o2, p, na, onw, x, g1, w_out)


def _outproj_bwd(dx1, ao, g1, w_out, o2, p, onw):
    def body(dx_ref, ao_ref, g_ref, wo_ref, o_ref, z_ref, w_ref, dao_ref, do_ref, dz_ref, dna_ref, dg_ref, dw_ref):
        b, j = pl.program_id(0), pl.program_id(1)

        @pl.when((b == 0) & (j == 0))
        def _():
            dw_ref[...] = jnp.zeros_like(dw_ref)

        @pl.when(j == 0)
        def _():
            do_ref[...] = jnp.zeros_like(do_ref)
            dz_ref[...] = jnp.zeros_like(dz_ref)

        @pl.when(j == 1)
        def _():
            dg_ref[...] = jnp.zeros_like(dg_ref)

        @pl.when(j >= 1)
        def _():
            dx = dx_ref[...]
            dg_ref[...] += jnp.sum(dx * ao_ref[...], axis=0, keepdims=True)
            dao = (g_ref[...] * dx).astype(BF16)
            dao_ref[...] = dao
            dai = lax.dot_general(dao, wo_ref[...], _NT, preferred_element_type=F32)
            dna_ref[...] = dai[:, DNW:]
            _, vjp = jax.vjp(_dn_out, o_ref[0] + o_ref[1], z_ref[...], w_ref[...])
            do, dz, dw = vjp(dai[:, :DNW])
            do_ref[...] = do
            dz_ref[...] = dz
            dw_ref[...] += jnp.concatenate([dw, jnp.zeros((7, DNW), F32)], axis=0)

    lat = lambda w: pl.BlockSpec((None, RT, w), lambda b, j: (b, jnp.maximum(j - 1, 0), 0))
    tok = lambda w: pl.BlockSpec((None, RT, w), lambda b, j: (b, j, 0))
    return pl.pallas_call(
        body, name="outproj_bwd", grid=(NB, T // RT),
        out_shape=[jax.ShapeDtypeStruct((NB, SEQ, D), BF16), jax.ShapeDtypeStruct((NB, T, DNW), F32),
                   jax.ShapeDtypeStruct((NB, T, DNW), F32), jax.ShapeDtypeStruct((NB, SEQ, NAW), F32),
                   jax.ShapeDtypeStruct((NB, 1, D), F32), jax.ShapeDtypeStruct((8, DNW), F32)],
        in_specs=[lat(D), lat(D), pl.BlockSpec((None, 1, D), lambda b, j: (b, 0, 0)),
                  pl.BlockSpec((D, D), lambda b, j: (0, 0)),
                  pl.BlockSpec((2, None, RT, DNW), lambda b, j: (0, b, j, 0)),
                  pl.BlockSpec((None, RT, DNW), lambda b, j: (b, j, COL_Z // DNW)),
                  pl.BlockSpec((1, DNW), lambda b, j: (0, 0))],
        out_specs=[lat(D), tok(DNW), tok(DNW), lat(NAW), pl.BlockSpec((None, 1, D), lambda b, j: (b, 0, 0)),
                   pl.BlockSpec((8, DNW), lambda b, j: (0, 0))],
        compiler_params=_cp(vmem=VMEM_BIG),
    )(dx1, ao, g1, w_out, o2, p, onw)


FT = 256
NFT = NTOK // FT
TPE = SEQ // FT
NFF = DFF // FFT


def _ffn_specs():
    row = lambda w: pl.BlockSpec((FT, w), lambda i, f: (i, 0))
    ex = pl.BlockSpec((None, 1, D), lambda i, f: (i // TPE, 0, 0))
    one = pl.BlockSpec((1, D), lambda i, f: (0, 0))
    wg = pl.BlockSpec((None, D, FFT), lambda i, f: (f, 0, 0))
    wu = pl.BlockSpec((None, D, FFT), lambda i, f: (NFF + f, 0, 0))
    wo = pl.BlockSpec((FFT, D), lambda i, f: (f, 0))
    ff = pl.BlockSpec((FT, FFT), lambda i, f: (i, f))
    return row, ex, one, wg, wu, wo, ff


def _ffn_fwd(x1, sh2, sc2, g2, n2w, w_in4, w_out, target):
    def body(x_ref, sh_ref, sc_ref, g_ref, w2_ref, wg_ref, wu_ref, wo_ref, t_ref,
             h2_ref, gt_ref, up_ref, dy_ref, dg_ref, loss_ref, acc_ref):
        i, f = pl.program_id(0), pl.program_id(1)

        @pl.when(f == 0)
        def _():
            h2_ref[...] = _norm_mod(x_ref[...], w2_ref[...], sh_ref[...], sc_ref[...]).astype(BF16)
            acc_ref[...] = jnp.zeros_like(acc_ref)

        h2 = h2_ref[...]
        gate = jnp.dot(h2, wg_ref[...], preferred_element_type=F32)
        up = jnp.dot(h2, wu_ref[...], preferred_element_type=F32)
        gt_ref[...] = gate.astype(BF16)
        up_ref[...] = up.astype(BF16)
        acc_ref[...] += jnp.dot((_silu(gate) * up).astype(BF16), wo_ref[...], preferred_element_type=F32)

        @pl.when((i == 0) & (f == 0))
        def _():
            loss_ref[...] = jnp.zeros_like(loss_ref)

        @pl.when(f == NFF - 1)
        def _():
            ffn = acc_ref[...]
            err = x_ref[...] + g_ref[...] * ffn - t_ref[...]
            dy = err * (1.0 / D)
            dy_ref[...] = dy
            part = jnp.sum(dy * ffn, axis=0, keepdims=True)
            loss_ref[...] += jnp.sum(jnp.sum(err * err, axis=1, keepdims=True), axis=0, keepdims=True) * (0.5 / D)

            @pl.when(i % TPE == 0)
            def _():
                dg_ref[...] = part

            @pl.when(i % TPE != 0)
            def _():
                dg_ref[...] += part

    row, ex, one, wg, wu, wo, ff = _ffn_specs()
    return pl.pallas_call(
        body, name="ffn_fwd", grid=(NFT, NFF),
        out_shape=[jax.ShapeDtypeStruct((NTOK, D), BF16), jax.ShapeDtypeStruct((NTOK, DFF), BF16),
                   jax.ShapeDtypeStruct((NTOK, DFF), BF16), jax.ShapeDtypeStruct((NTOK, D), F32),
                   jax.ShapeDtypeStruct((NB, 1, D), F32), jax.ShapeDtypeStruct((8, 128), F32)],
        in_specs=[row(D), ex, ex, ex, one, wg, wu, wo, row(D)],
        out_specs=[row(D), ff, ff, row(D), ex, pl.BlockSpec((8, 128), lambda i, f: (0, 0))],
        scratch_shapes=[pltpu.VMEM((FT, D), F32)],
        compiler_params=_cp(vmem=VMEM_BIG),
    )(x1, sh2, sc2, g2, n2w, w_in4, w_in4, w_out, target)


def _ffn_bwd(dy, gate, up, x1, sh2, sc2, g2, n2w, w_in4, w_out):
    def body(dy_ref, gt_ref, up_ref, x_ref, sh_ref, sc_ref, g_ref, w2_ref, wg_ref, wu_ref, wo_ref,
             dff_ref, act_ref, dgu_ref, dx_ref, dsh_ref, dsc_ref, dw_ref, acc_ref):
        i, f = pl.program_id(0), pl.program_id(1)

        @pl.when(f == 0)
        def _():
            dff_ref[...] = (g_ref[...] * dy_ref[...]).astype(BF16)
            acc_ref[...] = jnp.zeros_like(acc_ref)

        dact = lax.dot_general(dff_ref[...], wo_ref[...], _NT, preferred_element_type=F32)
        gate, up = gt_ref[...].astype(F32), up_ref[...].astype(F32)
        sg = _sigmoid(gate)
        sl = gate * sg
        act_ref[...] = (sl * up).astype(BF16)
        dgate = (dact * up * (sg * (1.0 + gate * (1.0 - sg)))).astype(BF16)
        dup = (dact * sl).astype(BF16)
        dgu_ref[0] = dgate
        dgu_ref[1] = dup
        acc_ref[...] += (lax.dot_general(dgate, wg_ref[...], _NT, preferred_element_type=F32)
                         + lax.dot_general(dup, wu_ref[...], _NT, preferred_element_type=F32))

        @pl.when((i == 0) & (f == 0))
        def _():
            dw_ref[...] = jnp.zeros_like(dw_ref)

        @pl.when(f == NFF - 1)
        def _():
            _, vjp = jax.vjp(_norm_mod, x_ref[...], w2_ref[...], sh_ref[...], sc_ref[...])
            dx, dw, dsh, dsc = vjp(acc_ref[...])
            dx_ref[...] = dy_ref[...] + dx
            dw_ref[...] += dw

            @pl.when(i % TPE == 0)
            def _():
                dsh_ref[...] = dsh
                dsc_ref[...] = dsc

            @pl.when(i % TPE != 0)
            def _():
                dsh_ref[...] += dsh
                dsc_ref[...] += dsc

    row, ex, one, wg, wu, wo, ff = _ffn_specs()
    return pl.pallas_call(
        body, name="ffn_bwd", grid=(NFT, NFF),
        out_shape=[jax.ShapeDtypeStruct((NTOK, D), BF16), jax.ShapeDtypeStruct((NTOK, DFF), BF16),
                   jax.ShapeDtypeStruct((2, NTOK, DFF), BF16), jax.ShapeDtypeStruct((NTOK, D), F32),
                   jax.ShapeDtypeStruct((NB, 1, D), F32), jax.ShapeDtypeStruct((NB, 1, D), F32),
                   jax.ShapeDtypeStruct((1, D), F32)],
        in_specs=[row(D), ff, ff, row(D), ex, ex, ex, one, wg, wu, wo],
        out_specs=[row(D), ff, pl.BlockSpec((2, FT, FFT), lambda i, f: (0, i, f)), row(D), ex, ex, one],
        scratch_shapes=[pltpu.VMEM((FT, D), F32)],
        compiler_params=_cp(vmem=VMEM_BIG),
    )(dy, gate, up, x1, sh2, sc2, g2, n2w, w_in4, w_in4, w_out)


def _local_step(x, ctx, target, mod_x, mod_c, norm1_w, norm2_w, w_in_p, conv_w, a_log, dt_bias, out_norm_w,
                q_norm_w, k_norm_w, rpb, w_out_b, w_ffn_in4, w_ffn_out_b):
    sh1, sc1, g1, sh2, sc2, g2 = [mod_x[:, i * D:(i + 1) * D].reshape(NB, 1, D) for i in range(6)]
    csh1, csc1 = mod_c[0:D].reshape(1, 1, D), mod_c[D:2 * D].reshape(1, 1, D)
    cw = jnp.concatenate([conv_w, jnp.zeros((3, 1536), F32)], axis=0)
    prm = jnp.zeros((8, 128), F32).at[0, :8].set(a_log.reshape(8)).at[1, :8].set(dt_bias.reshape(8))
    onw = jnp.tile(out_norm_w.reshape(1, 128), (1, DNH))
    qw = jnp.tile(q_norm_w.reshape(1, 64), (1, NAH))
    kw = jnp.tile(k_norm_w.reshape(1, 64), (1, NAH))
    seg = np.arange(NAW) // 64
    bd = jnp.asarray((seg[:, None] == seg[None, :]).astype(np.float32) / 64.0, dtype=BF16)
    bias = _bias_table(rpb)

    p, hm = _inproj_fwd(x, ctx, sh1, sc1, csh1, csc1, norm1_w, w_in_p)
    dq_, dk_, dv_ = _dn_prep_fwd(p, cw)
    gates = _gates_fwd(p, prm)
    o2, states = _scan_fwd(dq_, dk_, dv_, gates)
    qn, kn, vb = _na_prep_fwd(p, qw, kw, bd)
    na = _na_fwd(qn, kn, vb, bias)
    x1, ai, ao = _outproj_fwd(o2, p, na, onw, x, g1, w_out_b)
    x1f = x1.reshape(NTOK, D)
    h2, gate, up, dy, dg2, loss8 = _ffn_fwd(x1f, sh2, sc2, g2, norm2_w, w_ffn_in4, w_ffn_out_b,
                                             target.reshape(NTOK, D))
    dff, act, dgu, dx1, dsh2, dsc2, dn2w = _ffn_bwd(dy, gate, up, x1f, sh2, sc2, g2, norm2_w, w_ffn_in4, w_ffn_out_b)
    g_w_ffn_out = _matmul_tn(act, dff, FFT, D, 512, "wgrad_ffn_out")
    g_w_ffn_in4 = _matmul_tn(
        h2, dgu, 512, FFT, 512, "wgrad_ffn_in", nn=2 * NFF,
        b_block=(None, 512, FFT), b_map=lambda i, j, k: (j // NFF, k, j % NFF),
        out_shape=(4, D, FFT), out_block=(None, 512, FFT), out_map=lambda i, j, k: (j, i, 0))
    dao, do, dz, dna, dg1, donw = _outproj_bwd(dx1.reshape(NB, SEQ, D), ao, g1, w_out_b, o2, p, onw)
    g_w_out = _matmul_tn(ai.reshape(NTOK, D), dao.reshape(NTOK, D), 512, D, 512, "wgrad_out")
    dqn, dkn, dvn, dbias = _na_bwd(qn, kn, vb, bias, dna)
    dp_na, dqw, dkw = _na_prep_bwd(p, qw, kw, bd, dqn, dkn, dvn)
    sdq, sdk, sdv, sdg = _scan_bwd(dq_, dk_, dv_, gates, states, do)
    dpq, dpk, dpv, gcq, gck, gcv = _dn_prep_bwd(p, cw, sdq, sdk, sdv)
    dgt, dprm = _gates_bwd(p, prm, sdg)
    dp_dn = jnp.concatenate([dpq, dpk, dpv], axis=2)
    dpb, grad_x, dsh1, dsc1, dcsh1, dcsc1, dn1w = _inproj_bwd(
        dp_dn, dz, dp_na, dgt, w_in_p, x, ctx, sh1, sc1, csh1, csc1, norm1_w, dx1.reshape(NB, SEQ, D))
    g_w_in_p = _matmul_tn(hm.reshape(NB * T, D), dpb.reshape(NB * T, PC), 256, PC, 512, "wgrad_in")
    g_rpb = _rpb_fold(dbias)

    dmod_x = jnp.concatenate([dsh1, dsc1, dg1, dsh2, dsc2, dg2], axis=2).reshape(NB, 6 * D)
    dmod_c = jnp.concatenate([dcsh1.reshape(D), dcsc1.reshape(D), jnp.zeros((4 * D,), F32)])
    g_conv = jnp.concatenate([gcq[:5], gck[:5], gcv[:5]], axis=1)
    small = dict(
        norm1_w=dn1w, norm2_w=dn2w, a_log=dprm[0, :8], dt_bias=dprm[1, :8],
        out_norm_w=donw[0].reshape(DNH, 128).sum(0), q_norm_w=dqw[0].reshape(NAH, 64).sum(0),
        k_norm_w=dkw[0].reshape(NAH, 64).sum(0), rpb=g_rpb, conv_w=g_conv)
    return loss8[0, 0], grad_x, dmod_x, dmod_c, small, g_w_in_p, g_w_out, g_w_ffn_in4, g_w_ffn_out


def _perm_w_in(w):
    return jnp.concatenate([w[:, :2048], w[:, 2064:3600], w[:, 2048:2064],
                            jnp.zeros((w.shape[0], PC - IN_COLS), w.dtype)], axis=1)


def _unperm_w_in(g):
    return jnp.concatenate([g[:, :2048], g[:, COL_G:COL_G + 16], g[:, 2048:COL_G]], axis=1)


def _pack_small(c_ctx, n1, n2, b_ada, a_log, dt_bias, onw, qnw, knw, rpb, conv_s):
    misc = jnp.zeros((D,), F32)
    misc = misc.at[0:8].set(a_log.reshape(8)).at[8:16].set(dt_bias.reshape(8))
    misc = misc.at[128:256].set(onw.reshape(128)).at[256:320].set(qnw.reshape(64)).at[320:384].set(knw.reshape(64))
    rp = jnp.concatenate([rpb.reshape(-1), jnp.zeros((4 * D - 3720,), F32)])
    cv = jnp.concatenate([conv_s.reshape(-1), jnp.zeros((2 * D - 1920,), F32)])
    return jnp.concatenate([c_ctx.reshape(-1), n1.reshape(-1), n2.reshape(-1), b_ada.reshape(-1), misc, rp, cv]).reshape(16, D)


def _unpack_small(a):
    f = a.reshape(-1)
    misc = f[9 * D:10 * D]
    return dict(
        c_ctx=f[0:D], norm1_w=f[D:2 * D].reshape(1, D), norm2_w=f[2 * D:3 * D].reshape(1, D),
        b_ada=f[3 * D:9 * D].reshape(1, 6 * D), dn_A_log=misc[0:8].reshape(1, 2, 4), dn_dt_bias=misc[8:16].reshape(1, 2, 4),
        dn_out_norm_w=misc[128:256].reshape(1, 128), na_q_norm_w=misc[256:320].reshape(1, 64),
        na_k_norm_w=misc[320:384].reshape(1, 64), na_rpb=f[10 * D:10 * D + 3720].reshape(1, 8, 15, 31),
        dn_conv_w=f[14 * D:14 * D + 1920].reshape(1, 5, 384))


_WEIGHTS = ['c_ctx', 'norm1_w', 'norm2_w', 'w_ada', 'b_ada', 'w_in', 'dn_conv_w', 'dn_A_log', 'dn_dt_bias',
            'dn_out_norm_w', 'na_q_norm_w', 'na_k_norm_w', 'na_rpb', 'w_out', 'w_ffn_in', 'w_ffn_out']


def kernel(x, c, ctx, c_ctx, norm1_w, norm2_w, w_ada, b_ada, w_in, dn_conv_w, dn_A_log, dn_dt_bias, dn_out_norm_w, na_q_norm_w, na_k_norm_w, na_rpb, w_out, w_ffn_in, w_ffn_out, loss_target, m_c_ctx, m_norm1_w, m_norm2_w, m_w_ada, m_b_ada, m_w_in, m_dn_conv_w, m_dn_A_log, m_dn_dt_bias, m_dn_out_norm_w, m_na_q_norm_w, m_na_k_norm_w, m_na_rpb, m_w_out, m_w_ffn_in, m_w_ffn_out, v_c_ctx, v_norm1_w, v_norm2_w, v_w_ada, v_b_ada, v_w_in, v_dn_conv_w, v_dn_A_log, v_dn_dt_bias, v_dn_out_norm_w, v_na_q_norm_w, v_na_k_norm_w, v_na_rpb, v_w_out, v_w_ffn_in, v_w_ffn_out):
    ix, iy, ic = lax.axis_index("x"), lax.axis_index("y"), lax.axis_index("c")
    chip = 2 * ix + iy
    me = 4 * ix + 2 * iy + ic

    crow = jnp.concatenate([c, c_ctx.reshape(1, D), jnp.zeros((5, D), F32)], axis=0)
    cvrow = jnp.concatenate([dn_conv_w[0].reshape(-1), jnp.zeros((2 * D - 1920,), F32)]).reshape(2, D)
    g0 = _allgather8(jnp.concatenate([crow, cvrow, jnp.zeros((6, D), F32)], axis=0), "gather_cond")
    cin = g0[:, :8].reshape(64, D)
    conv_full = jnp.concatenate([g0[2 * s, 8:10].reshape(-1)[:1920].reshape(5, 384) for s in range(4)], axis=1)

    w_ada_s = w_ada[0]
    b_s = lax.dynamic_slice(b_ada, (0, chip * 1536), (1, 1536))
    mod_s = _ada_fwd(cin, w_ada_s, b_s)
    mods = _allgather8(mod_s, "gather_mod")
    mod_all = jnp.concatenate([mods[0], mods[2], mods[4], mods[6]], axis=1)
    mod_mine = lax.dynamic_slice(mod_all, (8 * me, 0), (8, 6 * D))
    mod_x, mod_c = mod_mine[0:NB], mod_mine[2]

    def half(w2d):
        r = w2d.shape[0] // 2
        return lax.dynamic_slice(w2d, (ic * r, 0), (r, w2d.shape[1]))

    halves = [_cast_bf16(half(w_in[0]), "cast_w_in"), _cast_bf16(half(w_out[0]), "cast_w_out"),
              _cast_bf16(half(w_ffn_in[0]), "cast_w_ffn_in"), _cast_bf16(half(w_ffn_out[0]), "cast_w_ffn_out")]
    gw_in, gw_out, gw_ffn_in, gw_ffn_out = _gather_weights(halves)
    w_in_full = gw_in.reshape(4, D, 900).transpose(1, 0, 2).reshape(D, IN_COLS)
    w_in_p = _perm_w_in(w_in_full)
    w_out_b = gw_out.reshape(D, D)
    w_ffn_in4 = gw_ffn_in.reshape(4, D, FFT)
    w_ffn_out_b = gw_ffn_out.reshape(DFF, D)

    (loss_part, grad_x, dmod_x, dmod_c, small, g_w_in_p, g_w_out, g_w_ffn_in4, g_w_ffn_out) = _local_step(
        x, ctx, loss_target, mod_x, mod_c, norm1_w, norm2_w, w_in_p, conv_full, dn_A_log[0], dn_dt_bias[0],
        dn_out_norm_w[0], na_q_norm_w[0], na_k_norm_w[0], na_rpb[0], w_out_b, w_ffn_in4, w_ffn_out_b)
    loss = lax.psum(loss_part, ("x", "y", "c"))

    dm = jnp.concatenate([dmod_x, dmod_c.reshape(1, 6 * D), jnp.zeros((5, 6 * D), F32)], axis=0).reshape(48, D)
    misc = jnp.zeros((D,), F32)
    misc = misc.at[0:8].set(small["a_log"]).at[8:16].set(small["dt_bias"]).at[128:256].set(small["out_norm_w"])
    misc = misc.at[256:320].set(small["q_norm_w"]).at[320:384].set(small["k_norm_w"])
    sm = jnp.concatenate([
        small["norm1_w"].reshape(-1), small["norm2_w"].reshape(-1), misc,
        small["rpb"].reshape(-1), jnp.zeros((4 * D - 3720,), F32),
        small["conv_w"].reshape(-1), jnp.zeros((8 * D - 7680,), F32), jnp.zeros((D,), F32)]).reshape(16, D)
    gsl, gtot = _allgather8(jnp.concatenate([dm, sm], axis=0), "gather_small", with_sum=True)
    gmod = gsl[:, :48].reshape(64, 6 * D)
    st = gtot[48:64].reshape(-1)
    g_n1, g_n2, g_misc = st[0:D].reshape(1, D), st[D:2 * D].reshape(1, D), st[2 * D:3 * D]
    g_rpb = st[3 * D:3 * D + 3720].reshape(1, 8, 15, 31)
    g_conv_full = st[7 * D:7 * D + 7680].reshape(5, 1536)
    g_conv_s = lax.dynamic_slice(g_conv_full, (0, chip * 384), (5, 384)).reshape(1, 5, 384)

    gmod_s = lax.dynamic_slice(gmod, (0, chip * 1536), (64, 1536))
    g_w_ada_s, pctx = _ada_bwd(cin, gmod_s, w_ada_s)
    parts = _allgather8(pctx, "gather_cctx")
    g_b_ada, g_c_ctx = _ada_small(gmod, parts, c_ctx.reshape(1, D))

    g_w_in_full = _unperm_w_in(g_w_in_p)
    g_w_in8 = g_w_in_full.reshape(D, 4, 900).transpose(1, 0, 2).reshape(8, D // 2, 900)
    rs = _scatter_grads([g_w_in8, g_w_out.reshape(8, D // 8, D), g_w_ffn_in4.reshape(8, D // 2, FFT),
                         g_w_ffn_out.reshape(8, DFF // 8, D)])
    sums = [_sum_slots(r, "sum_" + nm) for r, nm in zip(rs, ["w_in", "w_out", "w_ffn_in", "w_ffn_out"])]
    full = _sibling_exchange(sums)
    g_w_in_s = full[0].reshape(1, D, 900)
    g_w_out_s = full[1].reshape(1, D // 4, D)
    g_w_ffn_in_s = full[2].reshape(1, D, FFT)
    g_w_ffn_out_s = full[3].reshape(1, DFF // 4, D)

    grads = dict(
        c_ctx=g_c_ctx.reshape(D), norm1_w=g_n1, norm2_w=g_n2, w_ada=g_w_ada_s.reshape(1, D, 1536), b_ada=g_b_ada,
        w_in=g_w_in_s, dn_conv_w=g_conv_s, dn_A_log=g_misc[0:8].reshape(1, 2, 4), dn_dt_bias=g_misc[8:16].reshape(1, 2, 4),
        dn_out_norm_w=g_misc[128:256].reshape(1, 128), na_q_norm_w=g_misc[256:320].reshape(1, 64),
        na_k_norm_w=g_misc[320:384].reshape(1, 64), na_rpb=g_rpb, w_out=g_w_out_s, w_ffn_in=g_w_ffn_in_s,
        w_ffn_out=g_w_ffn_out_s)
    pk = lambda d_: _pack_small(d_["c_ctx"], d_["norm1_w"], d_["norm2_w"], d_["b_ada"], d_["dn_A_log"], d_["dn_dt_bias"],
                                d_["dn_out_norm_w"], d_["na_q_norm_w"], d_["na_k_norm_w"], d_["na_rpb"], d_["dn_conv_w"])
    wts = dict(c_ctx=c_ctx, norm1_w=norm1_w, norm2_w=norm2_w, b_ada=b_ada, dn_A_log=dn_A_log, dn_dt_bias=dn_dt_bias,
               dn_out_norm_w=dn_out_norm_w, na_q_norm_w=na_q_norm_w, na_k_norm_w=na_k_norm_w, na_rpb=na_rpb, dn_conv_w=dn_conv_w)
    ms = dict(c_ctx=m_c_ctx, norm1_w=m_norm1_w, norm2_w=m_norm2_w, b_ada=m_b_ada, dn_A_log=m_dn_A_log, dn_dt_bias=m_dn_dt_bias,
              dn_out_norm_w=m_dn_out_norm_w, na_q_norm_w=m_na_q_norm_w, na_k_norm_w=m_na_k_norm_w, na_rpb=m_na_rpb,
              dn_conv_w=m_dn_conv_w)
    vs = dict(c_ctx=v_c_ctx, norm1_w=v_norm1_w, norm2_w=v_norm2_w, b_ada=v_b_ada, dn_A_log=v_dn_A_log, dn_dt_bias=v_dn_dt_bias,
              dn_out_norm_w=v_dn_out_norm_w, na_q_norm_w=v_na_q_norm_w, na_k_norm_w=v_na_k_norm_w, na_rpb=v_na_rpb,
              dn_conv_w=v_dn_conv_w)
    sd, snm, snv = _adamw(pk(wts), pk(grads), pk(ms), pk(vs), "adamw_small")
    delta, new_m, new_v = _unpack_small(sd), _unpack_small(snm), _unpack_small(snv)
    big = dict(w_ada=(w_ada, m_w_ada, v_w_ada), w_in=(w_in, m_w_in, v_w_in), w_out=(w_out, m_w_out, v_w_out),
               w_ffn_in=(w_ffn_in, m_w_ffn_in, v_w_ffn_in), w_ffn_out=(w_ffn_out, m_w_ffn_out, v_w_ffn_out))
    for nm, (w_, m_, v_) in big.items():
        shp = w_.shape
        r2 = lambda a: a.reshape(shp[1], shp[2])
        d_, nm_, nv_ = _adamw(r2(w_), r2(grads[nm]), r2(m_), r2(v_), "adamw_" + nm)
        delta[nm], new_m[nm], new_v[nm] = d_.reshape(shp), nm_.reshape(shp), nv_.reshape(shp)

    return (loss, grad_x, *[grads[n] for n in _WEIGHTS], *[delta[n] for n in _WEIGHTS],
            *[new_m[n] for n in _WEIGHTS], *[new_v[n] for n in _WEIGHTS])
```

```python
import functools

import numpy as np
import jax
import jax.numpy as jnp
from jax import lax
from jax.experimental import pallas as pl
from jax.experimental.pallas import tpu as pltpu

F32 = jnp.float32
BF16 = jnp.bfloat16

D = 1024
NB = 2
SEQ = 2048
CT = 256
T = CT + SEQ
NTOK = NB * SEQ
CH = 64
NCH = T // CH
NCH_CTX = CT // CH
DNH, DNW = 4, 512
NAH, NAW = 8, 512
GRID_W = 64
ROWS = SEQ // GRID_W
WIN_ROWS = 8
WIN_COLS = 16
NWIN = WIN_ROWS * GRID_W
DFF = 2816
FFT = 1408
IN_COLS = 3600
PC = 3712
COL_Z, COL_NA, COL_G = 1536, 2048, 3584
EPS = 1e-6
NEG = -1e30
RT = 256
VMEM_BIG = 56 * 1024 * 1024
MESH = pl.DeviceIdType.MESH

ADAM_LR, ADAM_B1, ADAM_B2, ADAM_EPS, ADAM_WD, ADAM_STEP = 0.001, 0.9, 0.999, 1e-08, 0.01, 10


def _cp(vmem=None, sem=None):
    kw = {}
    if vmem is not None:
        kw["vmem_limit_bytes"] = vmem
    if sem is not None:
        kw["dimension_semantics"] = sem
    return pltpu.CompilerParams(**kw)


_NN = (((1,), (0,)), ((), ()))
_NT = (((1,), (1,)), ((), ()))
_TN = (((0,), (0,)), ((), ()))


def _dg(a, b, dims):
    return lax.dot_general(a.astype(BF16), b.astype(BF16), dims, preferred_element_type=F32)


@jax.custom_vjp
def _mm(a, b):
    return _dg(a, b, _NN)


_mm.defvjp(lambda a, b: (_dg(a, b, _NN), (a, b)),
           lambda r, g: (_dg(g, r[1], _NT), _dg(r[0], g, _TN)))


@jax.custom_vjp
def _mm_nt(a, b):
    return _dg(a, b, _NT)


_mm_nt.defvjp(lambda a, b: (_dg(a, b, _NT), (a, b)),
              lambda r, g: (_dg(g, r[1], _NN), _dg(g, r[0], _TN)))


@jax.custom_vjp
def _mm_tn(a, b):
    return _dg(a, b, _TN)


_mm_tn.defvjp(lambda a, b: (_dg(a, b, _TN), (a, b)),
              lambda r, g: (_dg(r[1], g, _NT), _dg(r[0], g, _NN)))


def _hmm(a, b):
    return jnp.dot(a, b, precision=lax.Precision.HIGHEST, preferred_element_type=F32)


def _sigmoid(x):
    return 1.0 / (1.0 + jnp.exp(-x))


def _silu(x):
    return x * _sigmoid(x)


def _norm_mod(x, w, sh, sc):
    r = lax.rsqrt(jnp.mean(x * x, axis=-1, keepdims=True) + EPS)
    return (x * r * w) * (1.0 + sc) + sh


def _my_pos():
    return lax.axis_index("x"), lax.axis_index("y"), lax.axis_index("c")


def _flip(pos, k):
    x, y, c = pos
    return (1 - x if k & 4 else x, 1 - y if k & 2 else y, 1 - c if k & 1 else c)


def _slot(pos):
    return 4 * pos[0] + 2 * pos[1] + pos[2]


def _allgather8(v, name, with_sum=False):
    R, W = v.shape

    def body(v_ref, out_ref, *rest):
        if with_sum:
            tot_ref, send_sems, recv_sems, lsem = rest
        else:
            send_sems, recv_sems, lsem = rest
        me = _my_pos()
        mine = pltpu.make_async_copy(v_ref, out_ref.at[_slot(me)], lsem)
        mine.start()
        sends = []
        for k in range(1, 8):
            cp = pltpu.make_async_remote_copy(
                src_ref=v_ref, dst_ref=out_ref.at[_slot(me)], send_sem=send_sems.at[k - 1],
                recv_sem=recv_sems.at[k - 1], device_id=_flip(me, k), device_id_type=MESH)
            cp.start()
            sends.append(cp)
        for k in range(1, 8):
            peer = _flip(me, k)
            pltpu.make_async_remote_copy(
                src_ref=v_ref, dst_ref=out_ref.at[_slot(peer)], send_sem=send_sems.at[k - 1],
                recv_sem=recv_sems.at[k - 1], device_id=peer, device_id_type=MESH).wait_recv()
        for cp in sends:
            cp.wait_send()
        mine.wait()
        if with_sum:
            acc = out_ref[0]
            for s in range(1, 8):
                acc = acc + out_ref[s]
            tot_ref[...] = acc

    out_shape = [jax.ShapeDtypeStruct((8, R, W), F32)]
    out_specs = [pl.BlockSpec(memory_space=pltpu.VMEM)]
    if with_sum:
        out_shape.append(jax.ShapeDtypeStruct((R, W), F32))
        out_specs.append(pl.BlockSpec(memory_space=pltpu.VMEM))
    res = pl.pallas_call(
        body, name=name, out_shape=out_shape,
        in_specs=[pl.BlockSpec(memory_space=pltpu.VMEM)], out_specs=out_specs,
        scratch_shapes=[pltpu.SemaphoreType.DMA((7,)), pltpu.SemaphoreType.DMA((7,)), pltpu.SemaphoreType.DMA],
        compiler_params=_cp(vmem=VMEM_BIG),
    )(v)
    return res if with_sum else res[0]


def _gather_weights(halves):
    n = len(halves)

    def body(*refs):
        ins, outs = refs[:n], refs[n:2 * n]
        send_sems, recv_sems, lsems = refs[2 * n:]
        me = _my_pos()
        x, y, c = me
        sib = (x, y, 1 - c)
        chips = [4, 2, 6]

        def cp(a, k, block, to, src=None):
            dst = outs[a].at[_slot(block)]
            return pltpu.make_async_remote_copy(
                src_ref=dst if src is None else src, dst_ref=dst, send_sem=send_sems.at[a, k],
                recv_sem=recv_sems.at[a, k], device_id=to, device_id_type=MESH)

        mine, first, passed = [], [], []
        for a in range(n):
            m = pltpu.make_async_copy(ins[a], outs[a].at[_slot(me)], lsems.at[a])
            m.start()
            mine.append(m)
            first.append(cp(a, 0, me, sib, src=ins[a]))
            for j, k in enumerate(chips):
                first.append(cp(a, 1 + j, me, _flip(me, k), src=ins[a]))
        for f in first:
            f.start()
        for j, k in enumerate(chips):
            for a in range(n):
                cp(a, 1 + j, _flip(me, k), me).wait_recv()
                p = cp(a, 4 + j, _flip(me, k), sib)
                p.start()
                passed.append(p)
        for a in range(n):
            cp(a, 0, sib, me).wait_recv()
            for j, k in enumerate(chips):
                cp(a, 4 + j, _flip(sib, k), me).wait_recv()
        for f in first + passed:
            f.wait_send()
        for m in mine:
            m.wait()

    hbm = pl.BlockSpec(memory_space=pltpu.HBM)
    return pl.pallas_call(
        body, name="gather_weights",
        out_shape=[jax.ShapeDtypeStruct((8,) + h.shape, h.dtype) for h in halves],
        in_specs=[hbm] * n, out_specs=[hbm] * n,
        scratch_shapes=[pltpu.SemaphoreType.DMA((n, 7)), pltpu.SemaphoreType.DMA((n, 7)),
                        pltpu.SemaphoreType.DMA((n,))],
    )(*halves)


def _scatter_grads(grads):
    n = len(grads)

    def body(*refs):
        ins, outs = refs[:n], refs[n:2 * n]
        send_sems, recv_sems, lsems = refs[2 * n:]
        me = _my_pos()
        mine, sends = [], []
        for a in range(n):
            m = pltpu.make_async_copy(ins[a].at[_slot(me)], outs[a].at[_slot(me)], lsems.at[a])
            m.start()
            mine.append(m)
            for k in range(1, 8):
                peer = _flip(me, k)
                s = pltpu.make_async_remote_copy(
                    src_ref=ins[a].at[_slot(peer)], dst_ref=outs[a].at[_slot(me)],
                    send_sem=send_sems.at[a, k - 1], recv_sem=recv_sems.at[a, k - 1],
                    device_id=peer, device_id_type=MESH)
                s.start()
                sends.append(s)
        for a in range(n):
            for k in range(1, 8):
                peer = _flip(me, k)
                pltpu.make_async_remote_copy(
                    src_ref=ins[a].at[_slot(peer)], dst_ref=outs[a].at[_slot(peer)],
                    send_sem=send_sems.at[a, k - 1], recv_sem=recv_sems.at[a, k - 1],
                    device_id=peer, device_id_type=MESH).wait_recv()
        for s in sends:
            s.wait_send()
        for m in mine:
            m.wait()

    hbm = pl.BlockSpec(memory_space=pltpu.HBM)
    return pl.pallas_call(
        body, name="scatter_grads",
        out_shape=[jax.ShapeDtypeStruct(g.shape, g.dtype) for g in grads],
        in_specs=[hbm] * n, out_specs=[hbm] * n,
        scratch_shapes=[pltpu.SemaphoreType.DMA((n, 7)), pltpu.SemaphoreType.DMA((n, 7)),
                        pltpu.SemaphoreType.DMA((n,))],
    )(*grads)


def _sibling_exchange(halves):
    n = len(halves)

    def body(*refs):
        ins, outs = refs[:n], refs[n:2 * n]
        send_sems, recv_sems, lsems = refs[2 * n:]
        x, y, c = _my_pos()
        sib = (x, y, 1 - c)
        mine, sends = [], []
        for a in range(n):
            m = pltpu.make_async_copy(ins[a], outs[a].at[c], lsems.at[a])
            m.start()
            mine.append(m)
            s = pltpu.make_async_remote_copy(
                src_ref=ins[a], dst_ref=outs[a].at[c], send_sem=send_sems.at[a], recv_sem=recv_sems.at[a],
                device_id=sib, device_id_type=MESH)
            s.start()
            sends.append(s)
        for a in range(n):
            pltpu.make_async_remote_copy(
                src_ref=ins[a], dst_ref=outs[a].at[1 - c], send_sem=send_sems.at[a], recv_sem=recv_sems.at[a],
                device_id=sib, device_id_type=MESH).wait_recv()
        for s in sends:
            s.wait_send()
        for m in mine:
            m.wait()

    hbm = pl.BlockSpec(memory_space=pltpu.HBM)
    return pl.pallas_call(
        body, name="sibling_exchange",
        out_shape=[jax.ShapeDtypeStruct((2,) + h.shape, h.dtype) for h in halves],
        in_specs=[hbm] * n, out_specs=[hbm] * n,
        scratch_shapes=[pltpu.SemaphoreType.DMA((n,)), pltpu.SemaphoreType.DMA((n,)),
                        pltpu.SemaphoreType.DMA((n,))],
    )(*halves)


def _row_tile(r, cap=512):
    for t in (cap, 256, 128, 64, 32, 16, 8):
        if t <= cap and r % t == 0:
            return t
    return r


def _cast_bf16(a, name):
    R, W = a.shape
    tr = _row_tile(R, 256)

    def body(a_ref, o_ref):
        o_ref[...] = a_ref[...].astype(BF16)

    return pl.pallas_call(
        body, name=name, grid=(R // tr,), out_shape=jax.ShapeDtypeStruct((R, W), BF16),
        in_specs=[pl.BlockSpec((tr, W), lambda i: (i, 0))], out_specs=pl.BlockSpec((tr, W), lambda i: (i, 0)),
    )(a)


def _sum_slots(r, name):
    _, R, W = r.shape
    tr = _row_tile(R, 128)

    def body(r_ref, o_ref):
        acc = r_ref[0]
        for s in range(1, 8):
            acc = acc + r_ref[s]
        o_ref[...] = acc

    return pl.pallas_call(
        body, name=name, grid=(R // tr,), out_shape=jax.ShapeDtypeStruct((R, W), F32),
        in_specs=[pl.BlockSpec((8, tr, W), lambda i: (0, i, 0))], out_specs=pl.BlockSpec((tr, W), lambda i: (i, 0)),
    )(r)


def _adamw(w, g, m, v, name):
    R, W = w.shape
    tr = _row_tile(R, 256)
    c1 = 1.0 / (1.0 - ADAM_B1 ** ADAM_STEP)
    c2 = 1.0 / (1.0 - ADAM_B2 ** ADAM_STEP)

    def body(w_ref, g_ref, m_ref, v_ref, d_ref, nm_ref, nv_ref):
        gg = g_ref[...]
        nm = ADAM_B1 * m_ref[...] + (1.0 - ADAM_B1) * gg
        nv = ADAM_B2 * v_ref[...] + (1.0 - ADAM_B2) * (gg * gg)
        d_ref[...] = -ADAM_LR * ((nm * c1) / (jnp.sqrt(nv * c2) + ADAM_EPS) + ADAM_WD * w_ref[...])
        nm_ref[...] = nm
        nv_ref[...] = nv

    spec = pl.BlockSpec((tr, W), lambda i: (i, 0))
    return pl.pallas_call(
        body, name=name, grid=(R // tr,), out_shape=[jax.ShapeDtypeStruct((R, W), F32)] * 3,
        in_specs=[spec] * 4, out_specs=[spec] * 3,
    )(w, g, m, v)


def _matmul_tn(a, b, tm, tn, tk, name, nn=1, b_map=None, out_shape=None, out_block=None, out_map=None,
               b_block=None):
    K, M = a.shape
    nk = K // tk
    nm = M // tm

    def body(a_ref, b_ref, o_ref, acc_ref):
        kk = pl.program_id(2)

        @pl.when(kk == 0)
        def _():
            acc_ref[...] = jnp.zeros_like(acc_ref)

        acc_ref[...] += lax.dot_general(a_ref[...], b_ref[...], _TN, preferred_element_type=F32)

        @pl.when(kk == nk - 1)
        def _():
            o_ref[...] = acc_ref[...]

    return pl.pallas_call(
        body, name=name, grid=(nm, nn, nk),
        out_shape=jax.ShapeDtypeStruct(out_shape or (M, b.shape[-1]), F32),
        in_specs=[pl.BlockSpec((tk, tm), lambda i, j, k: (k, i)),
                  pl.BlockSpec(b_block or (tk, tn), b_map or (lambda i, j, k: (k, j)))],
        out_specs=pl.BlockSpec(out_block or (tm, tn), out_map or (lambda i, j, k: (i, j))),
        scratch_shapes=[pltpu.VMEM((tm, tn), F32)],
        compiler_params=_cp(vmem=VMEM_BIG),
    )(a, b)


def _ada_fwd(cin, w_ada_s, b_s):
    tn = 512

    def body(c_ref, w_ref, b_ref, o_ref):
        o_ref[...] = _dg(_silu(c_ref[...]), w_ref[...], _NN) + b_ref[...]

    return pl.pallas_call(
        body, name="ada_fwd", grid=(1536 // tn,), out_shape=jax.ShapeDtypeStruct((64, 1536), F32),
        in_specs=[pl.BlockSpec((64, D), lambda j: (0, 0)), pl.BlockSpec((D, tn), lambda j: (0, j)),
                  pl.BlockSpec((1, tn), lambda j: (0, j))],
        out_specs=pl.BlockSpec((64, tn), lambda j: (0, j)),
        compiler_params=_cp(vmem=VMEM_BIG),
    )(cin, w_ada_s, b_s)


def _ada_bwd(cin, gmod_s, w_ada_s):
    tn = 512

    def body(c_ref, g_ref, w_ref, gw_ref, pc_ref):
        j = pl.program_id(0)
        g = g_ref[...]
        gw_ref[...] = _dg(_silu(c_ref[...]), g, _TN)
        rr = lax.broadcasted_iota(jnp.int32, (8, 64), 0)
        cc = lax.broadcasted_iota(jnp.int32, (8, 64), 1)
        sel = jnp.where((rr == 0) & ((cc & 7) == 2), 1.0, 0.0).astype(F32)
        dctx = _hmm(sel, g)
        part = _dg(dctx, w_ref[...], _NT)

        @pl.when(j == 0)
        def _():
            pc_ref[...] = jnp.zeros_like(pc_ref)

        pc_ref[...] += part

    return pl.pallas_call(
        body, name="ada_bwd", grid=(1536 // tn,),
        out_shape=[jax.ShapeDtypeStruct((D, 1536), F32), jax.ShapeDtypeStruct((8, D), F32)],
        in_specs=[pl.BlockSpec((64, D), lambda j: (0, 0)), pl.BlockSpec((64, tn), lambda j: (0, j)),
                  pl.BlockSpec((D, tn), lambda j: (0, j))],
        out_specs=[pl.BlockSpec((D, tn), lambda j: (0, j)), pl.BlockSpec((8, D), lambda j: (0, 0))],
        compiler_params=_cp(vmem=VMEM_BIG),
    )(cin, gmod_s, w_ada_s)


def _ada_small(gmod, parts, c_ctx):
    def body(g_ref, p_ref, c_ref, gb_ref, gc_ref):
        gb_ref[...] = jnp.sum(g_ref[...], axis=0, keepdims=True)
        ds = p_ref[0][0:1] + p_ref[2][0:1] + p_ref[4][0:1] + p_ref[6][0:1]
        cc = c_ref[...]
        s = _sigmoid(cc)
        gc_ref[...] = ds * (s * (1.0 + cc * (1.0 - s)))

    vm = pl.BlockSpec(memory_space=pltpu.VMEM)
    return pl.pallas_call(
        body, name="ada_small",
        out_shape=[jax.ShapeDtypeStruct((1, 6 * D), F32), jax.ShapeDtypeStruct((1, D), F32)],
        in_specs=[vm, vm, vm], out_specs=[vm, vm],
    )(gmod, parts, c_ctx)


def _tok_specs():
    xs = pl.BlockSpec((None, RT, D), lambda b, j: (b, jnp.maximum(j - 1, 0), 0))
    cs = pl.BlockSpec((None, RT, D), lambda b, j: (b, 0, 0))
    ms = pl.BlockSpec((None, 1, D), lambda b, j: (b, 0, 0))
    cms = pl.BlockSpec((None, 1, D), lambda b, j: (0, 0, 0))
    return xs, cs, ms, cms


def _inproj_fwd(x, ctx, sh1, sc1, csh1, csc1, n1w, w_in_p):
    def body(x_ref, c_ref, sh_ref, sc_ref, csh_ref, csc_ref, w1_ref, w_ref, p_ref, hm_ref):
        is_ctx = pl.program_id(1) == 0
        xt = jnp.where(is_ctx, c_ref[...], x_ref[...])
        sh = jnp.where(is_ctx, csh_ref[...], sh_ref[...])
        sc = jnp.where(is_ctx, csc_ref[...], sc_ref[...])
        hb = _norm_mod(xt, w1_ref[...], sh, sc).astype(BF16)
        hm_ref[...] = hb
        p_ref[...] = jnp.dot(hb, w_ref[...], preferred_element_type=F32)

    xs, cs, ms, cms = _tok_specs()
    return pl.pallas_call(
        body, name="inproj_fwd", grid=(NB, T // RT),
        out_shape=[jax.ShapeDtypeStruct((NB, T, PC), F32), jax.ShapeDtypeStruct((NB, T, D), BF16)],
        in_specs=[xs, cs, ms, ms, cms, cms, pl.BlockSpec((1, D), lambda b, j: (0, 0)),
                  pl.BlockSpec((D, PC), lambda b, j: (0, 0))],
        out_specs=[pl.BlockSpec((None, RT, PC), lambda b, j: (b, j, 0)),
                   pl.BlockSpec((None, RT, D), lambda b, j: (b, j, 0))],
        compiler_params=_cp(vmem=VMEM_BIG),
    )(x, ctx, sh1, sc1, csh1, csc1, n1w, w_in_p)


def _inproj_bwd(dp_dn, dz, dp_na, dgt, w_in_p, x, ctx, sh1, sc1, csh1, csc1, n1w, dx1):
    def body(d1_ref, d2_ref, d3_ref, d4_ref, w_ref, x_ref, c_ref, sh_ref, sc_ref, csh_ref, csc_ref, w1_ref,
             dx1_ref, dpb_ref, gx_ref, dsh_ref, dsc_ref, dcsh_ref, dcsc_ref, dw1_ref):
        b, j = pl.program_id(0), pl.program_id(1)
        is_ctx = j == 0
        dp = jnp.concatenate([d1_ref[...], d2_ref[...], d3_ref[...], d4_ref[...]], axis=1).astype(BF16)
        dpb_ref[...] = dp
        dhm = lax.dot_general(dp, w_ref[...], _NT, preferred_element_type=F32)
        xt = jnp.where(is_ctx, c_ref[...], x_ref[...])
        sh = jnp.where(is_ctx, csh_ref[...], sh_ref[...])
        sc = jnp.where(is_ctx, csc_ref[...], sc_ref[...])
        _, vjp = jax.vjp(_norm_mod, xt, w1_ref[...], sh, sc)
        dxt, dw1, dsh, dsc = vjp(dhm)

        @pl.when((b == 0) & (j == 0))
        def _():
            dw1_ref[...] = jnp.zeros_like(dw1_ref)
            dcsh_ref[...] = jnp.zeros_like(dcsh_ref)
            dcsc_ref[...] = jnp.zeros_like(dcsc_ref)

        dw1_ref[...] += dw1

        @pl.when(is_ctx)
        def _():
            dcsh_ref[...] += dsh
            dcsc_ref[...] += dsc

        @pl.when(j == 1)
        def _():
            dsh_ref[...] = jnp.zeros_like(dsh_ref)
            dsc_ref[...] = jnp.zeros_like(dsc_ref)

        @pl.when(j >= 1)
        def _():
            dsh_ref[...] += dsh
            dsc_ref[...] += dsc
            gx_ref[...] = dx1_ref[...] + dxt

    xs, cs, ms, cms = _tok_specs()
    tsp = lambda w: pl.BlockSpec((None, RT, w), lambda b, j: (b, j, 0))
    one = pl.BlockSpec((1, D), lambda b, j: (0, 0))
    return pl.pallas_call(
        body, name="inproj_bwd", grid=(NB, T // RT),
        out_shape=[jax.ShapeDtypeStruct((NB, T, PC), BF16), jax.ShapeDtypeStruct((NB, SEQ, D), F32),
                   jax.ShapeDtypeStruct((NB, 1, D), F32), jax.ShapeDtypeStruct((NB, 1, D), F32),
                   jax.ShapeDtypeStruct((1, 1, D), F32), jax.ShapeDtypeStruct((1, 1, D), F32),
                   jax.ShapeDtypeStruct((1, D), F32)],
        in_specs=[tsp(1536), tsp(512), tsp(1536), tsp(128), pl.BlockSpec((D, PC), lambda b, j: (0, 0)),
                  xs, cs, ms, ms, cms, cms, one, xs],
        out_specs=[tsp(PC), xs, ms, ms, cms, cms, one],
        compiler_params=_cp(vmem=VMEM_BIG),
    )(dp_dn, dz, dp_na, dgt, w_in_p, x, ctx, sh1, sc1, csh1, csc1, n1w, dx1)


def _seg_valid(t, off):
    lo = jnp.where(t < CT, 0, CT)
    hi = jnp.where(t < CT, CT, T)
    return (t + off >= lo) & (t + off < hi)


def _shifted(u, off):
    t = lax.broadcasted_iota(jnp.int32, (T, 1), 0)
    s = u if off == 0 else pltpu.roll(u, (-off) % T, 0)
    return jnp.where(_seg_valid(t, off), s, 0.0)


def _conv_pre(u, w):
    acc = _shifted(u, -2) * w[0:1, :]
    for kk in range(1, 5):
        acc = acc + _shifted(u, kk - 2) * w[kk:kk + 1, :]
    return acc


def _act_q(pre):
    a = _silu(pre)
    return a * lax.rsqrt(jnp.sum(a * a, axis=-1, keepdims=True) + EPS) * (128.0 ** -0.5)


def _act_k(pre):
    a = _silu(pre)
    return a * lax.rsqrt(jnp.sum(a * a, axis=-1, keepdims=True) + EPS)


def _dn_prep_fwd(p, cw):
    def body(pq_ref, pk_ref, pv_ref, wq_ref, wk_ref, wv_ref, q_ref, k_ref, v_ref):
        q_ref[...] = _act_q(_conv_pre(pq_ref[...], wq_ref[...]))
        k_ref[...] = _act_k(_conv_pre(pk_ref[...], wk_ref[...]))
        v_ref[...] = _silu(_conv_pre(pv_ref[...], wv_ref[...]))

    ps = lambda o: pl.BlockSpec((None, T, 128), lambda b, h: (b, 0, o + h))
    ws = lambda o: pl.BlockSpec((8, 128), lambda b, h: (0, o + h))
    os_ = pl.BlockSpec((None, T, 128), lambda b, h: (b, 0, h))
    return pl.pallas_call(
        body, name="dn_prep_fwd", grid=(NB, DNH), out_shape=[jax.ShapeDtypeStruct((NB, T, DNW), F32)] * 3,
        in_specs=[ps(0), ps(4), ps(8), ws(0), ws(4), ws(8)], out_specs=[os_] * 3,
        compiler_params=_cp(vmem=VMEM_BIG),
    )(p, p, p, cw, cw, cw)


def _dn_prep_bwd(p, cw, dqs, dks, dvs):
    def body(pq_ref, pk_ref, pv_ref, wq_ref, wk_ref, wv_ref, dqf_ref, dqb_ref, dkf_ref, dkb_ref, dvf_ref, dvb_ref,
             oq_ref, ok_ref, ov_ref, gq_ref, gk_ref, gv_ref):
        b = pl.program_id(1)

        def one(p_ref, w_ref, df_ref, db_ref, act, o_ref, g_ref):
            u = p_ref[...]
            w = w_ref[...]
            _, vjp = jax.vjp(act, _conv_pre(u, w))
            (dpre,) = vjp(df_ref[...] + db_ref[...])
            du = _shifted(dpre, 2) * w[0:1, :]
            rows = [jnp.sum(dpre * _shifted(u, -2), axis=0, keepdims=True)]
            for kk in range(1, 5):
                du = du + _shifted(dpre, 2 - kk) * w[kk:kk + 1, :]
                rows.append(jnp.sum(dpre * _shifted(u, kk - 2), axis=0, keepdims=True))
            o_ref[...] = du
            gw = jnp.concatenate(rows + [jnp.zeros((3, 128), F32)], axis=0)

            @pl.when(b == 0)
            def _():
                g_ref[...] = gw

            @pl.when(b > 0)
            def _():
                g_ref[...] += gw

        one(pq_ref, wq_ref, dqf_ref, dqb_ref, _act_q, oq_ref, gq_ref)
        one(pk_ref, wk_ref, dkf_ref, dkb_ref, _act_k, ok_ref, gk_ref)
        one(pv_ref, wv_ref, dvf_ref, dvb_ref, _silu, ov_ref, gv_ref)

    ps = lambda o: pl.BlockSpec((None, T, 128), lambda h, b: (b, 0, o + h))
    ws = lambda o: pl.BlockSpec((8, 128), lambda h, b: (0, o + h))
    ds = pl.BlockSpec((None, T, 128), lambda h, b: (b, 0, h))
    outs = pl.pallas_call(
        body, name="dn_prep_bwd", grid=(DNH, NB),
        out_shape=[jax.ShapeDtypeStruct((NB, T, DNW), F32)] * 3 + [jax.ShapeDtypeStruct((8, DNW), F32)] * 3,
        in_specs=[ps(0), ps(4), ps(8), ws(0), ws(4), ws(8)] + [ds] * 6,
        out_specs=[ds] * 3
        + [pl.BlockSpec((8, 128), lambda h, b: (0, h))] * 3,
        compiler_params=_cp(vmem=VMEM_BIG),
    )(p, p, p, cw, cw, cw, *dqs, *dks, *dvs)
    return outs


def _gate_fn(a, prm):
    lane = lax.broadcasted_iota(jnp.int32, a.shape, 1)
    z = a + prm[1:2, :]
    sp = jnp.maximum(z, 0.0) + jnp.log(1.0 + jnp.exp(-jnp.abs(z)))
    g = -jnp.exp(prm[0:1, :]) * sp
    return jnp.where(lane < 8, g, jnp.where(lane < 16, _sigmoid(a), 0.0))


def _gates_fwd(p, prm):
    def body(a_ref, prm_ref, o_ref):
        o_ref[...] = _gate_fn(a_ref[...], prm_ref[...])

    return pl.pallas_call(
        body, name="gates_fwd", grid=(NB,), out_shape=jax.ShapeDtypeStruct((NB, T, 128), F32),
        in_specs=[pl.BlockSpec((None, T, 128), lambda b: (b, 0, COL_G // 128)), pl.BlockSpec((8, 128), lambda b: (0, 0))],
        out_specs=pl.BlockSpec((None, T, 128), lambda b: (b, 0, 0)),
    )(p, prm)


def _gates_bwd(p, prm, dgf, dgb):
    def body(a_ref, prm_ref, dgf_ref, dgb_ref, o_ref, dprm_ref):
        b = pl.program_id(0)
        tot = dgf_ref[...] + dgb_ref[...]
        _, vjp = jax.vjp(_gate_fn, a_ref[...], prm_ref[...])
        da, dprm = vjp(tot)
        o_ref[...] = da

        @pl.when(b == 0)
        def _():
            dprm_ref[...] = dprm

        @pl.when(b > 0)
        def _():
            dprm_ref[...] += dprm

    return pl.pallas_call(
        body, name="gates_bwd", grid=(NB,),
        out_shape=[jax.ShapeDtypeStruct((NB, T, 128), F32), jax.ShapeDtypeStruct((8, 128), F32)],
        in_specs=[pl.BlockSpec((None, T, 128), lambda b: (b, 0, COL_G // 128)), pl.BlockSpec((8, 128), lambda b: (0, 0)),
                  pl.BlockSpec((None, T, 128), lambda b: (b, 0, 0)), pl.BlockSpec((None, T, 128), lambda b: (b, 0, 0))],
        out_specs=[pl.BlockSpec((None, T, 128), lambda b: (b, 0, 0)), pl.BlockSpec((8, 128), lambda b: (0, 0))],
        compiler_params=_cp(vmem=VMEM_BIG),
    )(p, prm, dgf, dgb)


def _tri_inv_impl(m, md, m1, m2, eye):
    n = -(m * md)
    t = eye + n
    pw = n
    for _ in range(3):
        pw = _hmm(pw, pw)
        t = t + _hmm(t, pw)
    t = t - _hmm(_hmm(t, m * m1), t)
    t = t - _hmm(_hmm(t, m * m2), t)
    return t


@jax.custom_vjp
def _tri_inv(m, md, m1, m2, eye):
    return _tri_inv_impl(m, md, m1, m2, eye)


def _tri_inv_f(m, md, m1, m2, eye):
    t = _tri_inv_impl(m, md, m1, m2, eye)
    return t, (t, md)


def _tri_inv_b(res, g):
    t, md = res
    tt = t.T
    z = jnp.zeros_like(md)
    return (-_hmm(_hmm(tt, g), tt), z, z, z, z)


_tri_inv.defvjp(_tri_inv_f, _tri_inv_b)


def _chunk_consts(d, h):
    ii = lax.broadcasted_iota(jnp.int32, (CH, CH), 0)
    jj = lax.broadcasted_iota(jnp.int32, (CH, CH), 1)
    diff = ii - jj if d == 0 else jj - ii
    f = lambda m: jnp.where(m, 1.0, 0.0).astype(F32)
    incl, strict = f(diff >= 0), f(diff > 0)
    md = f((ii >> 4) == (jj >> 4))
    m1 = f(((ii >> 5) == (jj >> 5)) & ((ii >> 4) != (jj >> 4)))
    m2 = f((ii >> 5) != (jj >> 5))
    eye = f(ii == jj)
    lane = lax.broadcasted_iota(jnp.int32, (1, 128), 1)
    sub = lax.broadcasted_iota(jnp.int32, (128, 1), 0)
    gl = d * DNH + h
    return dict(incl=incl, strict=strict, md=md, m1=m1, m2=m2, eye=eye,
                oh_g=f(lane == gl), oh_b=f(lane == gl + 8), oh_gt=f(sub == gl))


def _chunk_fn(q, k, v, gb, s, cst):
    incl = cst["incl"]
    g = jnp.sum(gb * cst["oh_g"], axis=1, keepdims=True)
    beta = jnp.sum(gb * cst["oh_b"], axis=1, keepdims=True)
    gcum = _hmm(incl, gb)
    gam = jnp.sum(gcum * cst["oh_g"], axis=1, keepdims=True)
    gam_row = jnp.sum(gcum.T * cst["oh_gt"], axis=0, keepdims=True)
    gtot = jnp.sum(g, axis=0, keepdims=True)
    dec = jnp.exp(jnp.where(incl > 0.5, gam - gam_row, NEG))
    e_gam = jnp.exp(gam)
    kb = k * beta
    m = cst["strict"] * (_mm_nt(kb, k) * dec)
    tm = _tri_inv(m, cst["md"], cst["m1"], cst["m2"], cst["eye"])
    u = _mm(tm, v * beta)
    w = _mm(tm, kb * e_gam)
    qk = _mm_nt(q, k) * dec
    v_new = u - _mm(w, s)
    o = _mm(q * e_gam, s) + _mm(qk, v_new)
    s_new = s * jnp.exp(gtot) + _mm_tn(k * jnp.exp(gtot - gam), v_new)
    return o, s_new


def _chunk_rev(s):
    return jnp.where(s < NCH_CTX, NCH_CTX - 1 - s, NCH + NCH_CTX - 1 - s)


def _scan_specs(step):
    cf = lambda s: step(s)
    cb = lambda s: _chunk_rev(step(s))
    tok = lambda c: pl.BlockSpec((None, CH, DNW), lambda b, s: (b, c(s), 0))
    gat = lambda c: pl.BlockSpec((None, CH, 128), lambda b, s: (b, c(s), 0))
    sta = lambda c: pl.BlockSpec((None, None, DNH, 128, 128), lambda b, s: (b, c(s), 0, 0, 0))
    return (tok(cf), gat(cf), sta(cf)), (tok(cb), gat(cb), sta(cb))


def _scan_fwd(q, k, v, gates):
    def body(qf, kf, vf, gf, qb, kb, vb, gb, of_ref, sf_ref, ob_ref, sb_ref, s_ref):
        @pl.when(pl.program_id(1) == 0)
        def _():
            s_ref[...] = jnp.zeros_like(s_ref)

        for d, (q_r, k_r, v_r, g_r, o_r, st_r) in enumerate(((qf, kf, vf, gf, of_ref, sf_ref),
                                                             (qb, kb, vb, gb, ob_ref, sb_ref))):
            gblk = g_r[...]
            for h in range(DNH):
                sl = slice(128 * h, 128 * h + 128)
                st = s_ref[d * DNH + h]
                st_r[h] = st
                o, s_new = _chunk_fn(q_r[:, sl], k_r[:, sl], v_r[:, sl], gblk, st, _chunk_consts(d, h))
                o_r[:, sl] = o
                s_ref[d * DNH + h] = s_new

    (tf, gf_, sf_), (tb, gb_, sb_) = _scan_specs(lambda s: s)
    o_sh = jax.ShapeDtypeStruct((NB, T, DNW), F32)
    st_sh = jax.ShapeDtypeStruct((NB, NCH, DNH, 128, 128), F32)
    return pl.pallas_call(
        body, name="scan_fwd", grid=(NB, NCH), out_shape=[o_sh, st_sh, o_sh, st_sh],
        in_specs=[tf, tf, tf, gf_, tb, tb, tb, gb_], out_specs=[tf, sf_, tb, sb_],
        scratch_shapes=[pltpu.VMEM((2 * DNH, 128, 128), F32)],
        compiler_params=_cp(vmem=VMEM_BIG),
    )(q, k, v, gates, q, k, v, gates)


def _scan_bwd(q, k, v, gates, st_f, st_b, do):
    def body(qf, kf, vf, gf, sf, dof, qb, kb, vb, gb, sb, dob,
             dqf, dkf, dvf, dgf, dqb, dkb, dvb, dgb, ds_ref):
        @pl.when(pl.program_id(1) == 0)
        def _():
            ds_ref[...] = jnp.zeros_like(ds_ref)

        for d, (q_r, k_r, v_r, g_r, st_r, do_r, dq_r, dk_r, dv_r, dg_r) in enumerate(
                ((qf, kf, vf, gf, sf, dof, dqf, dkf, dvf, dgf), (qb, kb, vb, gb, sb, dob, dqb, dkb, dvb, dgb))):
            gblk = g_r[...]
            dg_tot = jnp.zeros((CH, 128), F32)
            for h in range(DNH):
                sl = slice(128 * h, 128 * h + 128)
                cst = _chunk_consts(d, h)
                _, vjp = jax.vjp(lambda a, b_, c, e, f, cst=cst: _chunk_fn(a, b_, c, e, f, cst),
                                 q_r[:, sl], k_r[:, sl], v_r[:, sl], gblk, st_r[h])
                dq, dk, dv, dg, ds = vjp((do_r[:, sl], ds_ref[d * DNH + h]))
                dq_r[:, sl] = dq
                dk_r[:, sl] = dk
                dv_r[:, sl] = dv
                dg_tot = dg_tot + dg
                ds_ref[d * DNH + h] = ds
            dg_r[...] = dg_tot

    (tf, gf_, sf_), (tb, gb_, sb_) = _scan_specs(lambda s: NCH - 1 - s)
    o_sh = jax.ShapeDtypeStruct((NB, T, DNW), F32)
    g_sh = jax.ShapeDtypeStruct((NB, T, 128), F32)
    return pl.pallas_call(
        body, name="scan_bwd", grid=(NB, NCH), out_shape=[o_sh, o_sh, o_sh, g_sh] * 2,
        in_specs=[tf, tf, tf, gf_, sf_, tf, tb, tb, tb, gb_, sb_, tb],
        out_specs=[tf, tf, tf, gf_, tb, tb, tb, gb_],
        scratch_shapes=[pltpu.VMEM((2 * DNH, 128, 128), F32)],
        compiler_params=_cp(vmem=VMEM_BIG),
    )(q, k, v, gates, st_f, do, q, k, v, gates, st_b, do)


def _segmean(a, bd):
    hi = a.astype(BF16)
    lo = (a - hi.astype(F32)).astype(BF16)
    return jnp.dot(hi, bd, preferred_element_type=F32) + jnp.dot(lo, bd, preferred_element_type=F32)


def _na_prep_fwd(p, qw, kw, bd):
    def body(q_ref, k_ref, v_ref, qw_ref, kw_ref, bd_ref, oq_ref, ok_ref, ov_ref):
        bdm = bd_ref[...]
        q = q_ref[...]
        k = k_ref[...]
        oq_ref[...] = (q * lax.rsqrt(_segmean(q * q, bdm) + EPS) * qw_ref[...] * 0.125).astype(BF16)
        ok_ref[...] = (k * lax.rsqrt(_segmean(k * k, bdm) + EPS) * kw_ref[...]).astype(BF16)
        ov_ref[...] = v_ref[...].astype(BF16)

    ps = lambda o: pl.BlockSpec((None, RT, NAW), lambda b, j: (b, j, COL_NA // NAW + o))
    one = pl.BlockSpec((1, NAW), lambda b, j: (0, 0))
    os_ = pl.BlockSpec((None, RT, NAW), lambda b, j: (b, j, 0))
    return pl.pallas_call(
        body, name="na_prep_fwd", grid=(NB, T // RT), out_shape=[jax.ShapeDtypeStruct((NB, T, NAW), BF16)] * 3,
        in_specs=[ps(0), ps(1), ps(2), one, one, pl.BlockSpec((NAW, NAW), lambda b, j: (0, 0))], out_specs=[os_] * 3,
    )(p, p, p, qw, kw, bd)


def _na_prep_bwd(p, qw, kw, bd, dq, dk, dv):
    def body(q_ref, k_ref, qw_ref, kw_ref, bd_ref, dq_ref, dk_ref, dv_ref, o_ref, gq_ref, gk_ref):
        b, j = pl.program_id(0), pl.program_id(1)
        bdm = bd_ref[...]

        def one(x, w, dy, scale):
            r = lax.rsqrt(_segmean(x * x, bdm) + EPS)
            g = dy * w * scale
            dx = r * g - x * (r * r * r) * _segmean(g * x, bdm)
            dw = jnp.sum(dy * x * r * scale, axis=0, keepdims=True)
            return dx, dw

        dqv = jnp.where(j == 0, 0.0, dq_ref[...])
        dxq, dwq = one(q_ref[...], qw_ref[...], dqv, 0.125)
        dxk, dwk = one(k_ref[...], kw_ref[...], dk_ref[...], 1.0)
        o_ref[...] = jnp.concatenate([dxq, dxk, dv_ref[...]], axis=1)
        pad = jnp.zeros((7, NAW), F32)

        @pl.when((b == 0) & (j == 0))
        def _():
            gq_ref[...] = jnp.zeros_like(gq_ref)
            gk_ref[...] = jnp.zeros_like(gk_ref)

        gq_ref[...] += jnp.concatenate([dwq, pad], axis=0)
        gk_ref[...] += jnp.concatenate([dwk, pad], axis=0)

    ps = lambda o: pl.BlockSpec((None, RT, NAW), lambda b, j: (b, j, COL_NA // NAW + o))
    one = pl.BlockSpec((1, NAW), lambda b, j: (0, 0))
    ts = pl.BlockSpec((None, RT, NAW), lambda b, j: (b, j, 0))
    acc = pl.BlockSpec((8, NAW), lambda b, j: (0, 0))
    return pl.pallas_call(
        body, name="na_prep_bwd", grid=(NB, T // RT),
        out_shape=[jax.ShapeDtypeStruct((NB, T, 1536), F32), jax.ShapeDtypeStruct((8, NAW), F32),
                   jax.ShapeDtypeStruct((8, NAW), F32)],
        in_specs=[ps(0), ps(1), one, one, pl.BlockSpec((NAW, NAW), lambda b, j: (0, 0)),
                  pl.BlockSpec((None, RT, NAW), lambda b, j: (b, jnp.maximum(j - 1, 0), 0)), ts, ts],
        out_specs=[pl.BlockSpec((None, RT, 1536), lambda b, j: (b, j, 0)), acc, acc],
        compiler_params=_cp(vmem=VMEM_BIG),
    )(p, p, qw, kw, bd, dq, dk, dv)


def _na_row0(r):
    return jnp.clip(r - WIN_ROWS // 2, 0, ROWS - WIN_ROWS)


def _na_cls(r):
    return _na_row0(r) - r + WIN_ROWS - 1


def _na_scores(q, kw, kc, bias, hh):
    lane = lax.broadcasted_iota(jnp.int32, (1, 128), 1)
    hm = ((lane >> 6) == hh)
    qh = jnp.where(hm, q, jnp.zeros_like(q))
    s_w = lax.dot_general(qh, kw, _NT, preferred_element_type=F32) + bias
    s_c = lax.dot_general(qh, kc, _NT, preferred_element_type=F32)
    mx = jnp.maximum(jnp.max(s_w, axis=1, keepdims=True), jnp.max(s_c, axis=1, keepdims=True))
    p_w = jnp.exp(s_w - mx)
    p_c = jnp.exp(s_c - mx)
    inv = 1.0 / (jnp.sum(p_w, axis=1, keepdims=True) + jnp.sum(p_c, axis=1, keepdims=True))
    return hm, qh, p_w, p_c, inv


def _na_fwd(qn, kn, vb, bias):
    def body(q_ref, k_ref, v_ref, b_ref, o_ref):
        r = pl.program_id(2)
        start = pl.multiple_of(CT + _na_row0(r) * GRID_W, GRID_W)
        q = q_ref[...]
        kw, vw = k_ref[pl.ds(start, NWIN), :], v_ref[pl.ds(start, NWIN), :]
        kc, vc = k_ref[0:CT, :], v_ref[0:CT, :]
        out = jnp.zeros((GRID_W, 128), F32)
        for hh in range(2):
            hm, _, p_w, p_c, inv = _na_scores(q, kw, kc, b_ref[hh], hh)
            o = (jnp.dot(p_w.astype(BF16), vw, preferred_element_type=F32)
                 + jnp.dot(p_c.astype(BF16), vc, preferred_element_type=F32)) * inv
            out = jnp.where(hm, o, out)
        o_ref[...] = out.astype(BF16)

    kv = pl.BlockSpec((None, T, 128), lambda g, b, r: (b, 0, g))
    return pl.pallas_call(
        body, name="na_fwd", grid=(NAH // 2, NB, ROWS), out_shape=jax.ShapeDtypeStruct((NB, SEQ, NAW), BF16),
        in_specs=[pl.BlockSpec((None, GRID_W, 128), lambda g, b, r: (b, CT // GRID_W + r, g)), kv, kv,
                  pl.BlockSpec((None, 2, GRID_W, NWIN), lambda g, b, r: (_na_cls(r), g, 0, 0))],
        out_specs=pl.BlockSpec((None, GRID_W, 128), lambda g, b, r: (b, r, g)),
    )(qn, kn, vb, bias)


def _na_bwd(qn, kn, vb, bias, do):
    def body(q_ref, k_ref, v_ref, b_ref, do_ref, dq_ref, dk_ref, dv_ref, db_ref):
        b, r = pl.program_id(1), pl.program_id(2)

        @pl.when((b == 0) & (r == 0))
        def _():
            db_ref[...] = jnp.zeros_like(db_ref)

        @pl.when(r == 0)
        def _():
            dk_ref[...] = jnp.zeros_like(dk_ref)
            dv_ref[...] = jnp.zeros_like(dv_ref)

        start = pl.multiple_of(CT + _na_row0(r) * GRID_W, GRID_W)
        cls = _na_cls(r)
        q = q_ref[...]
        kw, vw = k_ref[pl.ds(start, NWIN), :], v_ref[pl.ds(start, NWIN), :]
        kc, vc = k_ref[0:CT, :], v_ref[0:CT, :]
        dout = do_ref[...]
        dq = jnp.zeros((GRID_W, 128), F32)
        dkw = jnp.zeros((NWIN, 128), F32)
        dvw = jnp.zeros((NWIN, 128), F32)
        dkc = jnp.zeros((CT, 128), F32)
        dvc = jnp.zeros((CT, 128), F32)
        for hh in range(2):
            hm, qh, p_w, p_c, inv = _na_scores(q, kw, kc, b_ref[hh], hh)
            p_w = p_w * inv
            p_c = p_c * inv
            doh = jnp.where(hm, dout, 0.0).astype(BF16)
            dp_w = lax.dot_general(doh, vw, _NT, preferred_element_type=F32)
            dp_c = lax.dot_general(doh, vc, _NT, preferred_element_type=F32)
            delta = jnp.sum(p_w * dp_w, axis=1, keepdims=True) + jnp.sum(p_c * dp_c, axis=1, keepdims=True)
            ds_w = p_w * (dp_w - delta)
            ds_c = p_c * (dp_c - delta)
            db_ref[cls, hh] += ds_w
            ds_wb, ds_cb = ds_w.astype(BF16), ds_c.astype(BF16)
            dqh = jnp.dot(ds_wb, kw, preferred_element_type=F32) + jnp.dot(ds_cb, kc, preferred_element_type=F32)
            dq = jnp.where(hm, dqh, dq)
            dkw = dkw + lax.dot_general(ds_wb, qh, _TN, preferred_element_type=F32)
            dkc = dkc + lax.dot_general(ds_cb, qh, _TN, preferred_element_type=F32)
            dvw = dvw + lax.dot_general(p_w.astype(BF16), doh, _TN, preferred_element_type=F32)
            dvc = dvc + lax.dot_general(p_c.astype(BF16), doh, _TN, preferred_element_type=F32)
        dq_ref[...] = dq
        dk_ref[pl.ds(start, NWIN), :] += dkw
        dv_ref[pl.ds(start, NWIN), :] += dvw
        dk_ref[0:CT, :] += dkc
        dv_ref[0:CT, :] += dvc

    kv = pl.BlockSpec((None, T, 128), lambda g, b, r: (b, 0, g))
    return pl.pallas_call(
        body, name="na_bwd", grid=(NAH // 2, NB, ROWS),
        out_shape=[jax.ShapeDtypeStruct((NB, SEQ, NAW), F32)] + [jax.ShapeDtypeStruct((NB, T, NAW), F32)] * 2
        + [jax.ShapeDtypeStruct((WIN_ROWS, NAH, GRID_W, NWIN), F32)],
        in_specs=[pl.BlockSpec((None, GRID_W, 128), lambda g, b, r: (b, CT // GRID_W + r, g)), kv, kv,
                  pl.BlockSpec((None, 2, GRID_W, NWIN), lambda g, b, r: (_na_cls(r), g, 0, 0)),
                  pl.BlockSpec((None, GRID_W, 128), lambda g, b, r: (b, r, g))],
        out_specs=[pl.BlockSpec((None, GRID_W, 128), lambda g, b, r: (b, r, g)), kv, kv,
                   pl.BlockSpec((WIN_ROWS, 2, GRID_W, NWIN), lambda g, b, r: (0, g, 0, 0))],
        compiler_params=_cp(vmem=VMEM_BIG),
    )(qn, kn, vb, bias, do)


def _na_tables():
    cols = np.arange(GRID_W)
    win_start = np.clip(cols - WIN_COLS // 2, 0, GRID_W - WIN_COLS)
    kc = cols[None, :]
    valid = (kc >= win_start[:, None]) & (kc < win_start[:, None] + WIN_COLS)
    rel = np.clip(kc - cols[:, None] + WIN_COLS - 1, 0, 2 * WIN_COLS - 2)
    return valid, rel


def _bias_table(rpb):
    valid, rel = _na_tables()
    t15 = jnp.where(valid[None, None], rpb[:, :, rel], NEG)
    tab = jnp.stack([t15[:, c:c + WIN_ROWS] for c in range(WIN_ROWS)], axis=0)
    return tab.transpose(0, 1, 3, 2, 4).reshape(WIN_ROWS, NAH, GRID_W, NWIN)


def _rpb_fold(db):
    def body1(d_ref, o_ref):
        for dr in range(2 * WIN_ROWS - 1):
            acc = None
            for c in range(WIN_ROWS):
                i = dr - c
                if 0 <= i < WIN_ROWS:
                    blk = d_ref[c, :, i * GRID_W:(i + 1) * GRID_W]
                    acc = blk if acc is None else acc + blk
            o_ref[dr] = acc

    t15 = pl.pallas_call(
        body1, name="rpb_fold_rows", grid=(NAH,),
        out_shape=jax.ShapeDtypeStruct((NAH, 2 * WIN_ROWS - 1, GRID_W, GRID_W), F32),
        in_specs=[pl.BlockSpec((WIN_ROWS, None, GRID_W, NWIN), lambda h: (0, h, 0, 0))],
        out_specs=pl.BlockSpec((None, 2 * WIN_ROWS - 1, GRID_W, GRID_W), lambda h: (h, 0, 0, 0)),
        compiler_params=_cp(vmem=VMEM_BIG),
    )(db)
    valid, rel = _na_tables()
    onehot = np.zeros((GRID_W * GRID_W, 128), np.float32)
    flat_rel, flat_valid = rel.reshape(-1), valid.reshape(-1)
    onehot[np.arange(GRID_W * GRID_W)[flat_valid], flat_rel[flat_valid]] = 1.0

    def body2(a_ref, oh_ref, o_ref):
        o_ref[...] = _hmm(a_ref[...], oh_ref[...])

    vm = pl.BlockSpec(memory_space=pltpu.VMEM)
    out = pl.pallas_call(
        body2, name="rpb_fold_cols", out_shape=jax.ShapeDtypeStruct((NAH * 15, 128), F32),
        in_specs=[vm, vm], out_specs=vm, compiler_params=_cp(vmem=VMEM_BIG),
    )(t15.reshape(NAH * 15, GRID_W * GRID_W), jnp.asarray(onehot))
    return out[:, :31].reshape(NAH, 15, 31)


def _dn_out(o, z, w):
    outs = []
    for h in range(DNH):
        sl = slice(128 * h, 128 * h + 128)
        oh = o[:, sl]
        r = lax.rsqrt(jnp.mean(oh * oh, axis=-1, keepdims=True) + EPS)
        outs.append(oh * r * w[:, sl] * _silu(z[:, sl]))
    return jnp.concatenate(outs, axis=1)


def _outproj_fwd(of, ob, p, na, onw, x, g1, w_out):
    def body(of_ref, ob_ref, z_ref, na_ref, w_ref, x_ref, g_ref, wo_ref, x1_ref, ai_ref, ao_ref):
        dn = _dn_out(of_ref[...] + ob_ref[...], z_ref[...], w_ref[...])
        ai = jnp.concatenate([dn.astype(BF16), na_ref[...]], axis=1)
        ai_ref[...] = ai
        ao = jnp.dot(ai, wo_ref[...], preferred_element_type=F32)
        ao_ref[...] = ao
        x1_ref[...] = x_ref[...] + g_ref[...] * ao

    lat = lambda w: pl.BlockSpec((None, RT, w), lambda b, j: (b, j, 0))
    return pl.pallas_call(
        body, name="outproj_fwd", grid=(NB, SEQ // RT),
        out_shape=[jax.ShapeDtypeStruct((NB, SEQ, D), F32), jax.ShapeDtypeStruct((NB, SEQ, D), BF16),
                   jax.ShapeDtypeStruct((NB, SEQ, D), F32)],
        in_specs=[pl.BlockSpec((None, RT, DNW), lambda b, j: (b, j + 1, 0)),
                  pl.BlockSpec((None, RT, DNW), lambda b, j: (b, j + 1, 0)),
                  pl.BlockSpec((None, RT, DNW), lambda b, j: (b, j + 1, COL_Z // DNW)),
                  lat(NAW), pl.BlockSpec((1, DNW), lambda b, j: (0, 0)), lat(D),
                  pl.BlockSpec((None, 1, D), lambda b, j: (b, 0, 0)), pl.BlockSpec((D, D), lambda b, j: (0, 0))],
        out_specs=[lat(D), lat(D), lat(D)],
        compiler_params=_cp(vmem=VMEM_BIG),
    )(of, ob, p, na, onw, x, g1, w_out)


def _outproj_bwd(dx1, ao, g1, w_out, of, ob, p, onw):
    def body(dx_ref, ao_ref, g_ref, wo_ref, of_ref, ob_ref, z_ref, w_ref, dao_ref, do_ref, dz_ref, dna_ref, dg_ref, dw_ref):
        b, j = pl.program_id(0), pl.program_id(1)

        @pl.when((b == 0) & (j == 0))
        def _():
            dw_ref[...] = jnp.zeros_like(dw_ref)

        @pl.when(j == 0)
        def _():
            do_ref[...] = jnp.zeros_like(do_ref)
            dz_ref[...] = jnp.zeros_like(dz_ref)

        @pl.when(j == 1)
        def _():
            dg_ref[...] = jnp.zeros_like(dg_ref)

        @pl.when(j >= 1)
        def _():
            dx = dx_ref[...]
            dg_ref[...] += jnp.sum(dx * ao_ref[...], axis=0, keepdims=True)
            dao = (g_ref[...] * dx).astype(BF16)
            dao_ref[...] = dao
            dai = lax.dot_general(dao, wo_ref[...], _NT, preferred_element_type=F32)
            dna_ref[...] = dai[:, DNW:]
            _, vjp = jax.vjp(_dn_out, of_ref[...] + ob_ref[...], z_ref[...], w_ref[...])
            do, dz, dw = vjp(dai[:, :DNW])
            do_ref[...] = do
            dz_ref[...] = dz
            dw_ref[...] += jnp.concatenate([dw, jnp.zeros((7, DNW), F32)], axis=0)

    lat = lambda w: pl.BlockSpec((None, RT, w), lambda b, j: (b, jnp.maximum(j - 1, 0), 0))
    tok = lambda w: pl.BlockSpec((None, RT, w), lambda b, j: (b, j, 0))
    return pl.pallas_call(
        body, name="outproj_bwd", grid=(NB, T // RT),
        out_shape=[jax.ShapeDtypeStruct((NB, SEQ, D), BF16), jax.ShapeDtypeStruct((NB, T, DNW), F32),
                   jax.ShapeDtypeStruct((NB, T, DNW), F32), jax.ShapeDtypeStruct((NB, SEQ, NAW), F32),
                   jax.ShapeDtypeStruct((NB, 1, D), F32), jax.ShapeDtypeStruct((8, DNW), F32)],
        in_specs=[lat(D), lat(D), pl.BlockSpec((None, 1, D), lambda b, j: (b, 0, 0)),
                  pl.BlockSpec((D, D), lambda b, j: (0, 0)),
                  tok(DNW), tok(DNW),
                  pl.BlockSpec((None, RT, DNW), lambda b, j: (b, j, COL_Z // DNW)),
                  pl.BlockSpec((1, DNW), lambda b, j: (0, 0))],
        out_specs=[lat(D), tok(DNW), tok(DNW), lat(NAW), pl.BlockSpec((None, 1, D), lambda b, j: (b, 0, 0)),
                   pl.BlockSpec((8, DNW), lambda b, j: (0, 0))],
        compiler_params=_cp(vmem=VMEM_BIG),
    )(dx1, ao, g1, w_out, of, ob, p, onw)


FT = 256
NFT = NTOK // FT
TPE = SEQ // FT
NFF = DFF // FFT


def _ffn_specs():
    row = lambda w: pl.BlockSpec((FT, w), lambda i, f: (i, 0))
    ex = pl.BlockSpec((None, 1, D), lambda i, f: (i // TPE, 0, 0))
    one = pl.BlockSpec((1, D), lambda i, f: (0, 0))
    wg = pl.BlockSpec((None, D, FFT), lambda i, f: (f, 0, 0))
    wu = pl.BlockSpec((None, D, FFT), lambda i, f: (NFF + f, 0, 0))
    wo = pl.BlockSpec((FFT, D), lambda i, f: (f, 0))
    ff = pl.BlockSpec((FT, FFT), lambda i, f: (i, f))
    return row, ex, one, wg, wu, wo, ff


def _ffn_fwd(x1, sh2, sc2, g2, n2w, w_in4, w_out, target):
    def body(x_ref, sh_ref, sc_ref, g_ref, w2_ref, wg_ref, wu_ref, wo_ref, t_ref,
             h2_ref, gt_ref, up_ref, dy_ref, dg_ref, loss_ref, acc_ref):
        i, f = pl.program_id(0), pl.program_id(1)

        @pl.when(f == 0)
        def _():
            h2_ref[...] = _norm_mod(x_ref[...], w2_ref[...], sh_ref[...], sc_ref[...]).astype(BF16)
            acc_ref[...] = jnp.zeros_like(acc_ref)

        h2 = h2_ref[...]
        gate = jnp.dot(h2, wg_ref[...], preferred_element_type=F32)
        up = jnp.dot(h2, wu_ref[...], preferred_element_type=F32)
        gt_ref[...] = gate.astype(BF16)
        up_ref[...] = up.astype(BF16)
        acc_ref[...] += jnp.dot((_silu(gate) * up).astype(BF16), wo_ref[...], preferred_element_type=F32)

        @pl.when((i == 0) & (f == 0))
        def _():
            loss_ref[...] = jnp.zeros_like(loss_ref)

        @pl.when(f == NFF - 1)
        def _():
            ffn = acc_ref[...]
            err = x_ref[...] + g_ref[...] * ffn - t_ref[...]
            dy = err * (1.0 / D)
            dy_ref[...] = dy
            part = jnp.sum(dy * ffn, axis=0, keepdims=True)
            loss_ref[...] += jnp.sum(jnp.sum(err * err, axis=1, keepdims=True), axis=0, keepdims=True) * (0.5 / D)

            @pl.when(i % TPE == 0)
            def _():
                dg_ref[...] = part

            @pl.when(i % TPE != 0)
            def _():
                dg_ref[...] += part

    row, ex, one, wg, wu, wo, ff = _ffn_specs()
    return pl.pallas_call(
        body, name="ffn_fwd", grid=(NFT, NFF),
        out_shape=[jax.ShapeDtypeStruct((NTOK, D), BF16), jax.ShapeDtypeStruct((NTOK, DFF), BF16),
                   jax.ShapeDtypeStruct((NTOK, DFF), BF16), jax.ShapeDtypeStruct((NTOK, D), F32),
                   jax.ShapeDtypeStruct((NB, 1, D), F32), jax.ShapeDtypeStruct((8, 128), F32)],
        in_specs=[row(D), ex, ex, ex, one, wg, wu, wo, row(D)],
        out_specs=[row(D), ff, ff, row(D), ex, pl.BlockSpec((8, 128), lambda i, f: (0, 0))],
        scratch_shapes=[pltpu.VMEM((FT, D), F32)],
        compiler_params=_cp(vmem=VMEM_BIG),
    )(x1, sh2, sc2, g2, n2w, w_in4, w_in4, w_out, target)


def _ffn_bwd(dy, gate, up, x1, sh2, sc2, g2, n2w, w_in4, w_out):
    def body(dy_ref, gt_ref, up_ref, x_ref, sh_ref, sc_ref, g_ref, w2_ref, wg_ref, wu_ref, wo_ref,
             dff_ref, act_ref, dgu_ref, dx_ref, dsh_ref, dsc_ref, dw_ref, acc_ref):
        i, f = pl.program_id(0), pl.program_id(1)

        @pl.when(f == 0)
        def _():
            dff_ref[...] = (g_ref[...] * dy_ref[...]).astype(BF16)
            acc_ref[...] = jnp.zeros_like(acc_ref)

        dact = lax.dot_general(dff_ref[...], wo_ref[...], _NT, preferred_element_type=F32)
        gate, up = gt_ref[...].astype(F32), up_ref[...].astype(F32)
        sg = _sigmoid(gate)
        sl = gate * sg
        act_ref[...] = (sl * up).astype(BF16)
        dgate = (dact * up * (sg * (1.0 + gate * (1.0 - sg)))).astype(BF16)
        dup = (dact * sl).astype(BF16)
        dgu_ref[0] = dgate
        dgu_ref[1] = dup
        acc_ref[...] += (lax.dot_general(dgate, wg_ref[...], _NT, preferred_element_type=F32)
                         + lax.dot_general(dup, wu_ref[...], _NT, preferred_element_type=F32))

        @pl.when((i == 0) & (f == 0))
        def _():
            dw_ref[...] = jnp.zeros_like(dw_ref)

        @pl.when(f == NFF - 1)
        def _():
            _, vjp = jax.vjp(_norm_mod, x_ref[...], w2_ref[...], sh_ref[...], sc_ref[...])
            dx, dw, dsh, dsc = vjp(acc_ref[...])
            dx_ref[...] = dy_ref[...] + dx
            dw_ref[...] += dw

            @pl.when(i % TPE == 0)
            def _():
                dsh_ref[...] = dsh
                dsc_ref[...] = dsc

            @pl.when(i % TPE != 0)
            def _():
                dsh_ref[...] += dsh
                dsc_ref[...] += dsc

    row, ex, one, wg, wu, wo, ff = _ffn_specs()
    return pl.pallas_call(
        body, name="ffn_bwd", grid=(NFT, NFF),
        out_shape=[jax.ShapeDtypeStruct((NTOK, D), BF16), jax.ShapeDtypeStruct((NTOK, DFF), BF16),
                   jax.ShapeDtypeStruct((2, NTOK, DFF), BF16), jax.ShapeDtypeStruct((NTOK, D), F32),
                   jax.ShapeDtypeStruct((NB, 1, D), F32), jax.ShapeDtypeStruct((NB, 1, D), F32),
                   jax.ShapeDtypeStruct((1, D), F32)],
        in_specs=[row(D), ff, ff, row(D), ex, ex, ex, one, wg, wu, wo],
        out_specs=[row(D), ff, pl.BlockSpec((2, FT, FFT), lambda i, f: (0, i, f)), row(D), ex, ex, one],
        scratch_shapes=[pltpu.VMEM((FT, D), F32)],
        compiler_params=_cp(vmem=VMEM_BIG),
    )(dy, gate, up, x1, sh2, sc2, g2, n2w, w_in4, w_in4, w_out)


def _local_step(x, ctx, target, mod_x, mod_c, norm1_w, norm2_w, w_in_p, conv_w, a_log, dt_bias, out_norm_w,
                q_norm_w, k_norm_w, rpb, w_out_b, w_ffn_in4, w_ffn_out_b):
    sh1, sc1, g1, sh2, sc2, g2 = [mod_x[:, i * D:(i + 1) * D].reshape(NB, 1, D) for i in range(6)]
    csh1, csc1 = mod_c[0:D].reshape(1, 1, D), mod_c[D:2 * D].reshape(1, 1, D)
    cw = jnp.concatenate([conv_w, jnp.zeros((3, 1536), F32)], axis=0)
    prm = jnp.zeros((8, 128), F32).at[0, :8].set(a_log.reshape(8)).at[1, :8].set(dt_bias.reshape(8))
    onw = jnp.tile(out_norm_w.reshape(1, 128), (1, DNH))
    qw = jnp.tile(q_norm_w.reshape(1, 64), (1, NAH))
    kw = jnp.tile(k_norm_w.reshape(1, 64), (1, NAH))
    seg = np.arange(NAW) // 64
    bd = jnp.asarray((seg[:, None] == seg[None, :]).astype(np.float32) / 64.0, dtype=BF16)
    bias = _bias_table(rpb)

    p, hm = _inproj_fwd(x, ctx, sh1, sc1, csh1, csc1, norm1_w, w_in_p)
    dq_, dk_, dv_ = _dn_prep_fwd(p, cw)
    gates = _gates_fwd(p, prm)
    o_f, st_f, o_b, st_b = _scan_fwd(dq_, dk_, dv_, gates)
    qn, kn, vb = _na_prep_fwd(p, qw, kw, bd)
    na = _na_fwd(qn, kn, vb, bias)
    x1, ai, ao = _outproj_fwd(o_f, o_b, p, na, onw, x, g1, w_out_b)
    x1f = x1.reshape(NTOK, D)
    h2, gate, up, dy, dg2, loss8 = _ffn_fwd(x1f, sh2, sc2, g2, norm2_w, w_ffn_in4, w_ffn_out_b,
                                             target.reshape(NTOK, D))
    dff, act, dgu, dx1, dsh2, dsc2, dn2w = _ffn_bwd(dy, gate, up, x1f, sh2, sc2, g2, norm2_w, w_ffn_in4, w_ffn_out_b)
    g_w_ffn_out = _matmul_tn(act, dff, FFT, D, 512, "wgrad_ffn_out")
    g_w_ffn_in4 = _matmul_tn(
        h2, dgu, 512, FFT, 512, "wgrad_ffn_in", nn=2 * NFF,
        b_block=(None, 512, FFT), b_map=lambda i, j, k: (j // NFF, k, j % NFF),
        out_shape=(4, D, FFT), out_block=(None, 512, FFT), out_map=lambda i, j, k: (j, i, 0))
    dao, do, dz, dna, dg1, donw = _outproj_bwd(dx1.reshape(NB, SEQ, D), ao, g1, w_out_b, o_f, o_b, p, onw)
    g_w_out = _matmul_tn(ai.reshape(NTOK, D), dao.reshape(NTOK, D), 512, D, 512, "wgrad_out")
    dqn, dkn, dvn, dbias = _na_bwd(qn, kn, vb, bias, dna)
    dp_na, dqw, dkw = _na_prep_bwd(p, qw, kw, bd, dqn, dkn, dvn)
    dqf, dkf, dvf, dgf, dqb, dkb, dvb, dgb = _scan_bwd(dq_, dk_, dv_, gates, st_f, st_b, do)
    dpq, dpk, dpv, gcq, gck, gcv = _dn_prep_bwd(p, cw, (dqf, dqb), (dkf, dkb), (dvf, dvb))
    dgt, dprm = _gates_bwd(p, prm, dgf, dgb)
    dp_dn = jnp.concatenate([dpq, dpk, dpv], axis=2)
    dpb, grad_x, dsh1, dsc1, dcsh1, dcsc1, dn1w = _inproj_bwd(
        dp_dn, dz, dp_na, dgt, w_in_p, x, ctx, sh1, sc1, csh1, csc1, norm1_w, dx1.reshape(NB, SEQ, D))
    g_w_in_p = _matmul_tn(hm.reshape(NB * T, D), dpb.reshape(NB * T, PC), 256, PC, 512, "wgrad_in")
    g_rpb = _rpb_fold(dbias)

    dmod_x = jnp.concatenate([dsh1, dsc1, dg1, dsh2, dsc2, dg2], axis=2).reshape(NB, 6 * D)
    dmod_c = jnp.concatenate([dcsh1.reshape(D), dcsc1.reshape(D), jnp.zeros((4 * D,), F32)])
    g_conv = jnp.concatenate([gcq[:5], gck[:5], gcv[:5]], axis=1)
    small = dict(
        norm1_w=dn1w, norm2_w=dn2w, a_log=dprm[0, :8], dt_bias=dprm[1, :8],
        out_norm_w=donw[0].reshape(DNH, 128).sum(0), q_norm_w=dqw[0].reshape(NAH, 64).sum(0),
        k_norm_w=dkw[0].reshape(NAH, 64).sum(0), rpb=g_rpb, conv_w=g_conv)
    return loss8[0, 0], grad_x, dmod_x, dmod_c, small, g_w_in_p, g_w_out, g_w_ffn_in4, g_w_ffn_out


def _perm_w_in(w):
    return jnp.concatenate([w[:, :2048], w[:, 2064:3600], w[:, 2048:2064],
                            jnp.zeros((w.shape[0], PC - IN_COLS), w.dtype)], axis=1)


def _unperm_w_in(g):
    return jnp.concatenate([g[:, :2048], g[:, COL_G:COL_G + 16], g[:, 2048:COL_G]], axis=1)


def _pack_small(c_ctx, n1, n2, b_ada, a_log, dt_bias, onw, qnw, knw, rpb, conv_s):
    misc = jnp.zeros((D,), F32)
    misc = misc.at[0:8].set(a_log.reshape(8)).at[8:16].set(dt_bias.reshape(8))
    misc = misc.at[128:256].set(onw.reshape(128)).at[256:320].set(qnw.reshape(64)).at[320:384].set(knw.reshape(64))
    rp = jnp.concatenate([rpb.reshape(-1), jnp.zeros((4 * D - 3720,), F32)])
    cv = jnp.concatenate([conv_s.reshape(-1), jnp.zeros((2 * D - 1920,), F32)])
    return jnp.concatenate([c_ctx.reshape(-1), n1.reshape(-1), n2.reshape(-1), b_ada.reshape(-1), misc, rp, cv]).reshape(16, D)


def _unpack_small(a):
    f = a.reshape(-1)
    misc = f[9 * D:10 * D]
    return dict(
        c_ctx=f[0:D], norm1_w=f[D:2 * D].reshape(1, D), norm2_w=f[2 * D:3 * D].reshape(1, D),
        b_ada=f[3 * D:9 * D].reshape(1, 6 * D), dn_A_log=misc[0:8].reshape(1, 2, 4), dn_dt_bias=misc[8:16].reshape(1, 2, 4),
        dn_out_norm_w=misc[128:256].reshape(1, 128), na_q_norm_w=misc[256:320].reshape(1, 64),
        na_k_norm_w=misc[320:384].reshape(1, 64), na_rpb=f[10 * D:10 * D + 3720].reshape(1, 8, 15, 31),
        dn_conv_w=f[14 * D:14 * D + 1920].reshape(1, 5, 384))


_WEIGHTS = ['c_ctx', 'norm1_w', 'norm2_w', 'w_ada', 'b_ada', 'w_in', 'dn_conv_w', 'dn_A_log', 'dn_dt_bias',
            'dn_out_norm_w', 'na_q_norm_w', 'na_k_norm_w', 'na_rpb', 'w_out', 'w_ffn_in', 'w_ffn_out']


def kernel(x, c, ctx, c_ctx, norm1_w, norm2_w, w_ada, b_ada, w_in, dn_conv_w, dn_A_log, dn_dt_bias, dn_out_norm_w, na_q_norm_w, na_k_norm_w, na_rpb, w_out, w_ffn_in, w_ffn_out, loss_target, m_c_ctx, m_norm1_w, m_norm2_w, m_w_ada, m_b_ada, m_w_in, m_dn_conv_w, m_dn_A_log, m_dn_dt_bias, m_dn_out_norm_w, m_na_q_norm_w, m_na_k_norm_w, m_na_rpb, m_w_out, m_w_ffn_in, m_w_ffn_out, v_c_ctx, v_norm1_w, v_norm2_w, v_w_ada, v_b_ada, v_w_in, v_dn_conv_w, v_dn_A_log, v_dn_dt_bias, v_dn_out_norm_w, v_na_q_norm_w, v_na_k_norm_w, v_na_rpb, v_w_out, v_w_ffn_in, v_w_ffn_out):
    ix, iy, ic = lax.axis_index("x"), lax.axis_index("y"), lax.axis_index("c")
    chip = 2 * ix + iy
    me = 4 * ix + 2 * iy + ic

    crow = jnp.concatenate([c, c_ctx.reshape(1, D), jnp.zeros((5, D), F32)], axis=0)
    cvrow = jnp.concatenate([dn_conv_w[0].reshape(-1), jnp.zeros((2 * D - 1920,), F32)]).reshape(2, D)
    g0 = _allgather8(jnp.concatenate([crow, cvrow, jnp.zeros((6, D), F32)], axis=0), "gather_cond")
    cin = g0[:, :8].reshape(64, D)
    conv_full = jnp.concatenate([g0[2 * s, 8:10].reshape(-1)[:1920].reshape(5, 384) for s in range(4)], axis=1)

    w_ada_s = w_ada[0]
    b_s = lax.dynamic_slice(b_ada, (0, chip * 1536), (1, 1536))
    mod_s = _ada_fwd(cin, w_ada_s, b_s)
    mods = _allgather8(mod_s, "gather_mod")
    mod_all = jnp.concatenate([mods[0], mods[2], mods[4], mods[6]], axis=1)
    mod_mine = lax.dynamic_slice(mod_all, (8 * me, 0), (8, 6 * D))
    mod_x, mod_c = mod_mine[0:NB], mod_mine[2]

    def half(w2d):
        r = w2d.shape[0] // 2
        return lax.dynamic_slice(w2d, (ic * r, 0), (r, w2d.shape[1]))

    halves = [_cast_bf16(half(w_in[0]), "cast_w_in"), _cast_bf16(half(w_out[0]), "cast_w_out"),
              _cast_bf16(half(w_ffn_in[0]), "cast_w_ffn_in"), _cast_bf16(half(w_ffn_out[0]), "cast_w_ffn_out")]
    gw_in, gw_out, gw_ffn_in, gw_ffn_out = _gather_weights(halves)
    w_in_full = gw_in.reshape(4, D, 900).transpose(1, 0, 2).reshape(D, IN_COLS)
    w_in_p = _perm_w_in(w_in_full)
    w_out_b = gw_out.reshape(D, D)
    w_ffn_in4 = gw_ffn_in.reshape(4, D, FFT)
    w_ffn_out_b = gw_ffn_out.reshape(DFF, D)

    (loss_part, grad_x, dmod_x, dmod_c, small, g_w_in_p, g_w_out, g_w_ffn_in4, g_w_ffn_out) = _local_step(
        x, ctx, loss_target, mod_x, mod_c, norm1_w, norm2_w, w_in_p, conv_full, dn_A_log[0], dn_dt_bias[0],
        dn_out_norm_w[0], na_q_norm_w[0], na_k_norm_w[0], na_rpb[0], w_out_b, w_ffn_in4, w_ffn_out_b)
    loss = lax.psum(loss_part, ("x", "y", "c"))

    dm = jnp.concatenate([dmod_x, dmod_c.reshape(1, 6 * D), jnp.zeros((5, 6 * D), F32)], axis=0).reshape(48, D)
    misc = jnp.zeros((D,), F32)
    misc = misc.at[0:8].set(small["a_log"]).at[8:16].set(small["dt_bias"]).at[128:256].set(small["out_norm_w"])
    misc = misc.at[256:320].set(small["q_norm_w"]).at[320:384].set(small["k_norm_w"])
    sm = jnp.concatenate([
        small["norm1_w"].reshape(-1), small["norm2_w"].reshape(-1), misc,
        small["rpb"].reshape(-1), jnp.zeros((4 * D - 3720,), F32),
        small["conv_w"].reshape(-1), jnp.zeros((8 * D - 7680,), F32), jnp.zeros((D,), F32)]).reshape(16, D)
    gsl, gtot = _allgather8(jnp.concatenate([dm, sm], axis=0), "gather_small", with_sum=True)
    gmod = gsl[:, :48].reshape(64, 6 * D)
    st = gtot[48:64].reshape(-1)
    g_n1, g_n2, g_misc = st[0:D].reshape(1, D), st[D:2 * D].reshape(1, D), st[2 * D:3 * D]
    g_rpb = st[3 * D:3 * D + 3720].reshape(1, 8, 15, 31)
    g_conv_full = st[7 * D:7 * D + 7680].reshape(5, 1536)
    g_conv_s = lax.dynamic_slice(g_conv_full, (0, chip * 384), (5, 384)).reshape(1, 5, 384)

    gmod_s = lax.dynamic_slice(gmod, (0, chip * 1536), (64, 1536))
    g_w_ada_s, pctx = _ada_bwd(cin, gmod_s, w_ada_s)
    parts = _allgather8(pctx, "gather_cctx")
    g_b_ada, g_c_ctx = _ada_small(gmod, parts, c_ctx.reshape(1, D))

    g_w_in_full = _unperm_w_in(g_w_in_p)
    g_w_in8 = g_w_in_full.reshape(D, 4, 900).transpose(1, 0, 2).reshape(8, D // 2, 900)
    rs = _scatter_grads([g_w_in8, g_w_out.reshape(8, D // 8, D), g_w_ffn_in4.reshape(8, D // 2, FFT),
                         g_w_ffn_out.reshape(8, DFF // 8, D)])
    sums = [_sum_slots(r, "sum_" + nm) for r, nm in zip(rs, ["w_in", "w_out", "w_ffn_in", "w_ffn_out"])]
    full = _sibling_exchange(sums)
    g_w_in_s = full[0].reshape(1, D, 900)
    g_w_out_s = full[1].reshape(1, D // 4, D)
    g_w_ffn_in_s = full[2].reshape(1, D, FFT)
    g_w_ffn_out_s = full[3].reshape(1, DFF // 4, D)

    grads = dict(
        c_ctx=g_c_ctx.reshape(D), norm1_w=g_n1, norm2_w=g_n2, w_ada=g_w_ada_s.reshape(1, D, 1536), b_ada=g_b_ada,
        w_in=g_w_in_s, dn_conv_w=g_conv_s, dn_A_log=g_misc[0:8].reshape(1, 2, 4), dn_dt_bias=g_misc[8:16].reshape(1, 2, 4),
        dn_out_norm_w=g_misc[128:256].reshape(1, 128), na_q_norm_w=g_misc[256:320].reshape(1, 64),
        na_k_norm_w=g_misc[320:384].reshape(1, 64), na_rpb=g_rpb, w_out=g_w_out_s, w_ffn_in=g_w_ffn_in_s,
        w_ffn_out=g_w_ffn_out_s)
    pk = lambda d_: _pack_small(d_["c_ctx"], d_["norm1_w"], d_["norm2_w"], d_["b_ada"], d_["dn_A_log"], d_["dn_dt_bias"],
                                d_["dn_out_norm_w"], d_["na_q_norm_w"], d_["na_k_norm_w"], d_["na_rpb"], d_["dn_conv_w"])
    wts = dict(c_ctx=c_ctx, norm1_w=norm1_w, norm2_w=norm2_w, b_ada=b_ada, dn_A_log=dn_A_log, dn_dt_bias=dn_dt_bias,
               dn_out_norm_w=dn_out_norm_w, na_q_norm_w=na_q_norm_w, na_k_norm_w=na_k_norm_w, na_rpb=na_rpb, dn_conv_w=dn_conv_w)
    ms = dict(c_ctx=m_c_ctx, norm1_w=m_norm1_w, norm2_w=m_norm2_w, b_ada=m_b_ada, dn_A_log=m_dn_A_log, dn_dt_bias=m_dn_dt_bias,
              dn_out_norm_w=m_dn_out_norm_w, na_q_norm_w=m_na_q_norm_w, na_k_norm_w=m_na_k_norm_w, na_rpb=m_na_rpb,
              dn_conv_w=m_dn_conv_w)
    vs = dict(c_ctx=v_c_ctx, norm1_w=v_norm1_w, norm2_w=v_norm2_w, b_ada=v_b_ada, dn_A_log=v_dn_A_log, dn_dt_bias=v_dn_dt_bias,
              dn_out_norm_w=v_dn_out_norm_w, na_q_norm_w=v_na_q_norm_w, na_k_norm_w=v_na_k_norm_w, na_rpb=v_na_rpb,
              dn_conv_w=v_dn_conv_w)
    sd, snm, snv = _adamw(pk(wts), pk(grads), pk(ms), pk(vs), "adamw_small")
    delta, new_m, new_v = _unpack_small(sd), _unpack_small(snm), _unpack_small(snv)
    big = dict(w_ada=(w_ada, m_w_ada, v_w_ada), w_in=(w_in, m_w_in, v_w_in), w_out=(w_out, m_w_out, v_w_out),
               w_ffn_in=(w_ffn_in, m_w_ffn_in, v_w_ffn_in), w_ffn_out=(w_ffn_out, m_w_ffn_out, v_w_ffn_out))
    for nm, (w_, m_, v_) in big.items():
        shp = w_.shape
        r2 = lambda a: a.reshape(shp[1], shp[2])
        d_, nm_, nv_ = _adamw(r2(w_), r2(grads[nm]), r2(m_), r2(v_), "adamw_" + nm)
        delta[nm], new_m[nm], new_v[nm] = d_.reshape(shp), nm_.reshape(shp), nv_.reshape(shp)

    return (loss, grad_x, *[grads[n] for n in _WEIGHTS], *[delta[n] for n in _WEIGHTS],
            *[new_m[n] for n in _WEIGHTS], *[new_v[n] for n in _WEIGHTS])
```

```python
import functools

import numpy as np
import jax
import jax.numpy as jnp
from jax import lax
from jax.experimental import pallas as pl
from jax.experimental.pallas import tpu as pltpu

F32 = jnp.float32
BF16 = jnp.bfloat16

D = 1024
NB = 2
SEQ = 2048
CT = 256
T = CT + SEQ
NTOK = NB * SEQ
CH = 64
NCH = T // CH
NCH_CTX = CT // CH
DNH, DNW = 4, 512
NAH, NAW = 8, 512
GRID_W = 64
ROWS = SEQ // GRID_W
WIN_ROWS = 8
WIN_COLS = 16
NWIN = WIN_ROWS * GRID_W
DFF = 2816
FFT = 1408
IN_COLS = 3600
PC = 3712
COL_Z, COL_NA, COL_G = 1536, 2048, 3584
EPS = 1e-6
NEG = -1e30
RT = 256
VMEM_BIG = 56 * 1024 * 1024
MESH = pl.DeviceIdType.MESH

ADAM_LR, ADAM_B1, ADAM_B2, ADAM_EPS, ADAM_WD, ADAM_STEP = 0.001, 0.9, 0.999, 1e-08, 0.01, 10


def _cp(vmem=None, sem=None):
    kw = {}
    if vmem is not None:
        kw["vmem_limit_bytes"] = vmem
    if sem is not None:
        kw["dimension_semantics"] = sem
    return pltpu.CompilerParams(**kw)


_NN = (((1,), (0,)), ((), ()))
_NT = (((1,), (1,)), ((), ()))
_TN = (((0,), (0,)), ((), ()))


def _dg(a, b, dims):
    return lax.dot_general(a.astype(BF16), b.astype(BF16), dims, preferred_element_type=F32)


@jax.custom_vjp
def _mm(a, b):
    return _dg(a, b, _NN)


_mm.defvjp(lambda a, b: (_dg(a, b, _NN), (a, b)),
           lambda r, g: (_dg(g, r[1], _NT), _dg(r[0], g, _TN)))


@jax.custom_vjp
def _mm_nt(a, b):
    return _dg(a, b, _NT)


_mm_nt.defvjp(lambda a, b: (_dg(a, b, _NT), (a, b)),
              lambda r, g: (_dg(g, r[1], _NN), _dg(g, r[0], _TN)))


@jax.custom_vjp
def _mm_tn(a, b):
    return _dg(a, b, _TN)


_mm_tn.defvjp(lambda a, b: (_dg(a, b, _TN), (a, b)),
              lambda r, g: (_dg(r[1], g, _NT), _dg(r[0], g, _NN)))


def _hmm(a, b):
    return jnp.dot(a, b, precision=lax.Precision.HIGHEST, preferred_element_type=F32)


def _sigmoid(x):
    return 1.0 / (1.0 + jnp.exp(-x))


def _silu(x):
    return x * _sigmoid(x)


def _norm_mod(x, w, sh, sc):
    r = lax.rsqrt(jnp.mean(x * x, axis=-1, keepdims=True) + EPS)
    return (x * r * w) * (1.0 + sc) + sh


def _my_pos():
    return lax.axis_index("x"), lax.axis_index("y"), lax.axis_index("c")


def _flip(pos, k):
    x, y, c = pos
    return (1 - x if k & 4 else x, 1 - y if k & 2 else y, 1 - c if k & 1 else c)


def _slot(pos):
    return 4 * pos[0] + 2 * pos[1] + pos[2]


def _allgather8(v, name, with_sum=False):
    R, W = v.shape

    def body(v_ref, out_ref, *rest):
        if with_sum:
            tot_ref, send_sems, recv_sems, lsem = rest
        else:
            send_sems, recv_sems, lsem = rest
        me = _my_pos()
        mine = pltpu.make_async_copy(v_ref, out_ref.at[_slot(me)], lsem)
        mine.start()
        sends = []
        for k in range(1, 8):
            cp = pltpu.make_async_remote_copy(
                src_ref=v_ref, dst_ref=out_ref.at[_slot(me)], send_sem=send_sems.at[k - 1],
                recv_sem=recv_sems.at[k - 1], device_id=_flip(me, k), device_id_type=MESH)
            cp.start()
            sends.append(cp)
        for k in range(1, 8):
            peer = _flip(me, k)
            pltpu.make_async_remote_copy(
                src_ref=v_ref, dst_ref=out_ref.at[_slot(peer)], send_sem=send_sems.at[k - 1],
                recv_sem=recv_sems.at[k - 1], device_id=peer, device_id_type=MESH).wait_recv()
        for cp in sends:
            cp.wait_send()
        mine.wait()
        if with_sum:
            acc = out_ref[0]
            for s in range(1, 8):
                acc = acc + out_ref[s]
            tot_ref[...] = acc

    out_shape = [jax.ShapeDtypeStruct((8, R, W), F32)]
    out_specs = [pl.BlockSpec(memory_space=pltpu.VMEM)]
    if with_sum:
        out_shape.append(jax.ShapeDtypeStruct((R, W), F32))
        out_specs.append(pl.BlockSpec(memory_space=pltpu.VMEM))
    res = pl.pallas_call(
        body, name=name, out_shape=out_shape,
        in_specs=[pl.BlockSpec(memory_space=pltpu.VMEM)], out_specs=out_specs,
        scratch_shapes=[pltpu.SemaphoreType.DMA((7,)), pltpu.SemaphoreType.DMA((7,)), pltpu.SemaphoreType.DMA],
        compiler_params=_cp(vmem=VMEM_BIG),
    )(v)
    return res if with_sum else res[0]


def _gather_weights(halves):
    n = len(halves)

    def body(*refs):
        ins, outs = refs[:n], refs[n:2 * n]
        send_sems, recv_sems, lsems = refs[2 * n:]
        me = _my_pos()
        x, y, c = me
        sib = (x, y, 1 - c)
        chips = [4, 2, 6]

        def cp(a, k, block, to, src=None):
            dst = outs[a].at[_slot(block)]
            return pltpu.make_async_remote_copy(
                src_ref=dst if src is None else src, dst_ref=dst, send_sem=send_sems.at[a, k],
                recv_sem=recv_sems.at[a, k], device_id=to, device_id_type=MESH)

        mine, first, passed = [], [], []
        for a in range(n):
            m = pltpu.make_async_copy(ins[a], outs[a].at[_slot(me)], lsems.at[a])
            m.start()
            mine.append(m)
            first.append(cp(a, 0, me, sib, src=ins[a]))
            for j, k in enumerate(chips):
                first.append(cp(a, 1 + j, me, _flip(me, k), src=ins[a]))
        for f in first:
            f.start()
        for j, k in enumerate(chips):
            for a in range(n):
                cp(a, 1 + j, _flip(me, k), me).wait_recv()
                p = cp(a, 4 + j, _flip(me, k), sib)
                p.start()
                passed.append(p)
        for a in range(n):
            cp(a, 0, sib, me).wait_recv()
            for j, k in enumerate(chips):
                cp(a, 4 + j, _flip(sib, k), me).wait_recv()
        for f in first + passed:
            f.wait_send()
        for m in mine:
            m.wait()

    hbm = pl.BlockSpec(memory_space=pltpu.HBM)
    return pl.pallas_call(
        body, name="gather_weights",
        out_shape=[jax.ShapeDtypeStruct((8,) + h.shape, h.dtype) for h in halves],
        in_specs=[hbm] * n, out_specs=[hbm] * n,
        scratch_shapes=[pltpu.SemaphoreType.DMA((n, 7)), pltpu.SemaphoreType.DMA((n, 7)),
                        pltpu.SemaphoreType.DMA((n,))],
    )(*halves)


def _scatter_grads(grads):
    n = len(grads)

    def body(*refs):
        ins, outs = refs[:n], refs[n:2 * n]
        send_sems, recv_sems, lsems = refs[2 * n:]
        me = _my_pos()
        mine, sends = [], []
        for a in range(n):
            m = pltpu.make_async_copy(ins[a].at[_slot(me)], outs[a].at[_slot(me)], lsems.at[a])
            m.start()
            mine.append(m)
            for k in range(1, 8):
                peer = _flip(me, k)
                s = pltpu.make_async_remote_copy(
                    src_ref=ins[a].at[_slot(peer)], dst_ref=outs[a].at[_slot(me)],
                    send_sem=send_sems.at[a, k - 1], recv_sem=recv_sems.at[a, k - 1],
                    device_id=peer, device_id_type=MESH)
                s.start()
                sends.append(s)
        for a in range(n):
            for k in range(1, 8):
                peer = _flip(me, k)
                pltpu.make_async_remote_copy(
                    src_ref=ins[a].at[_slot(peer)], dst_ref=outs[a].at[_slot(peer)],
                    send_sem=send_sems.at[a, k - 1], recv_sem=recv_sems.at[a, k - 1],
                    device_id=peer, device_id_type=MESH).wait_recv()
        for s in sends:
            s.wait_send()
        for m in mine:
            m.wait()

    hbm = pl.BlockSpec(memory_space=pltpu.HBM)
    return pl.pallas_call(
        body, name="scatter_grads",
        out_shape=[jax.ShapeDtypeStruct(g.shape, g.dtype) for g in grads],
        in_specs=[hbm] * n, out_specs=[hbm] * n,
        scratch_shapes=[pltpu.SemaphoreType.DMA((n, 7)), pltpu.SemaphoreType.DMA((n, 7)),
                        pltpu.SemaphoreType.DMA((n,))],
    )(*grads)


def _sibling_exchange(halves):
    n = len(halves)

    def body(*refs):
        ins, outs = refs[:n], refs[n:2 * n]
        send_sems, recv_sems, lsems = refs[2 * n:]
        x, y, c = _my_pos()
        sib = (x, y, 1 - c)
        mine, sends = [], []
        for a in range(n):
            m = pltpu.make_async_copy(ins[a], outs[a].at[c], lsems.at[a])
            m.start()
            mine.append(m)
            s = pltpu.make_async_remote_copy(
                src_ref=ins[a], dst_ref=outs[a].at[c], send_sem=send_sems.at[a], recv_sem=recv_sems.at[a],
                device_id=sib, device_id_type=MESH)
            s.start()
            sends.append(s)
        for a in range(n):
            pltpu.make_async_remote_copy(
                src_ref=ins[a], dst_ref=outs[a].at[1 - c], send_sem=send_sems.at[a], recv_sem=recv_sems.at[a],
                device_id=sib, device_id_type=MESH).wait_recv()
        for s in sends:
            s.wait_send()
        for m in mine:
            m.wait()

    hbm = pl.BlockSpec(memory_space=pltpu.HBM)
    return pl.pallas_call(
        body, name="sibling_exchange",
        out_shape=[jax.ShapeDtypeStruct((2,) + h.shape, h.dtype) for h in halves],
        in_specs=[hbm] * n, out_specs=[hbm] * n,
        scratch_shapes=[pltpu.SemaphoreType.DMA((n,)), pltpu.SemaphoreType.DMA((n,)),
                        pltpu.SemaphoreType.DMA((n,))],
    )(*halves)


def _row_tile(r, cap=512):
    for t in (cap, 256, 128, 64, 32, 16, 8):
        if t <= cap and r % t == 0:
            return t
    return r


def _cast_bf16(a, name):
    R, W = a.shape
    tr = _row_tile(R, 256)

    def body(a_ref, o_ref):
        o_ref[...] = a_ref[...].astype(BF16)

    return pl.pallas_call(
        body, name=name, grid=(R // tr,), out_shape=jax.ShapeDtypeStruct((R, W), BF16),
        in_specs=[pl.BlockSpec((tr, W), lambda i: (i, 0))], out_specs=pl.BlockSpec((tr, W), lambda i: (i, 0)),
    )(a)


def _sum_slots(r, name):
    _, R, W = r.shape
    tr = _row_tile(R, 128)

    def body(r_ref, o_ref):
        acc = r_ref[0]
        for s in range(1, 8):
            acc = acc + r_ref[s]
        o_ref[...] = acc

    return pl.pallas_call(
        body, name=name, grid=(R // tr,), out_shape=jax.ShapeDtypeStruct((R, W), F32),
        in_specs=[pl.BlockSpec((8, tr, W), lambda i: (0, i, 0))], out_specs=pl.BlockSpec((tr, W), lambda i: (i, 0)),
    )(r)


def _adamw(w, g, m, v, name):
    R, W = w.shape
    tr = _row_tile(R, 256)
    c1 = 1.0 / (1.0 - ADAM_B1 ** ADAM_STEP)
    c2 = 1.0 / (1.0 - ADAM_B2 ** ADAM_STEP)

    def body(w_ref, g_ref, m_ref, v_ref, d_ref, nm_ref, nv_ref):
        gg = g_ref[...]
        nm = ADAM_B1 * m_ref[...] + (1.0 - ADAM_B1) * gg
        nv = ADAM_B2 * v_ref[...] + (1.0 - ADAM_B2) * (gg * gg)
        d_ref[...] = -ADAM_LR * ((nm * c1) / (jnp.sqrt(nv * c2) + ADAM_EPS) + ADAM_WD * w_ref[...])
        nm_ref[...] = nm
        nv_ref[...] = nv

    spec = pl.BlockSpec((tr, W), lambda i: (i, 0))
    return pl.pallas_call(
        body, name=name, grid=(R // tr,), out_shape=[jax.ShapeDtypeStruct((R, W), F32)] * 3,
        in_specs=[spec] * 4, out_specs=[spec] * 3,
    )(w, g, m, v)


def _matmul_tn(a, b, tm, tn, tk, name, nn=1, b_map=None, out_shape=None, out_block=None, out_map=None,
               b_block=None):
    K, M = a.shape
    nk = K // tk
    nm = M // tm

    def body(a_ref, b_ref, o_ref, acc_ref):
        kk = pl.program_id(2)

        @pl.when(kk == 0)
        def _():
            acc_ref[...] = jnp.zeros_like(acc_ref)

        acc_ref[...] += lax.dot_general(a_ref[...], b_ref[...], _TN, preferred_element_type=F32)

        @pl.when(kk == nk - 1)
        def _():
            o_ref[...] = acc_ref[...]

    return pl.pallas_call(
        body, name=name, grid=(nm, nn, nk),
        out_shape=jax.ShapeDtypeStruct(out_shape or (M, b.shape[-1]), F32),
        in_specs=[pl.BlockSpec((tk, tm), lambda i, j, k: (k, i)),
                  pl.BlockSpec(b_block or (tk, tn), b_map or (lambda i, j, k: (k, j)))],
        out_specs=pl.BlockSpec(out_block or (tm, tn), out_map or (lambda i, j, k: (i, j))),
        scratch_shapes=[pltpu.VMEM((tm, tn), F32)],
        compiler_params=_cp(vmem=VMEM_BIG),
    )(a, b)


def _ada_fwd(cin, w_ada_s, b_s):
    tn = 512

    def body(c_ref, w_ref, b_ref, o_ref):
        o_ref[...] = _dg(_silu(c_ref[...]), w_ref[...], _NN) + b_ref[...]

    return pl.pallas_call(
        body, name="ada_fwd", grid=(1536 // tn,), out_shape=jax.ShapeDtypeStruct((64, 1536), F32),
        in_specs=[pl.BlockSpec((64, D), lambda j: (0, 0)), pl.BlockSpec((D, tn), lambda j: (0, j)),
                  pl.BlockSpec((1, tn), lambda j: (0, j))],
        out_specs=pl.BlockSpec((64, tn), lambda j: (0, j)),
        compiler_params=_cp(vmem=VMEM_BIG),
    )(cin, w_ada_s, b_s)


def _ada_bwd(cin, gmod_s, w_ada_s):
    tn = 512

    def body(c_ref, g_ref, w_ref, gw_ref, pc_ref):
        j = pl.program_id(0)
        g = g_ref[...]
        gw_ref[...] = _dg(_silu(c_ref[...]), g, _TN)
        rr = lax.broadcasted_iota(jnp.int32, (8, 64), 0)
        cc = lax.broadcasted_iota(jnp.int32, (8, 64), 1)
        sel = jnp.where((rr == 0) & ((cc & 7) == 2), 1.0, 0.0).astype(F32)
        dctx = _hmm(sel, g)
        part = _dg(dctx, w_ref[...], _NT)

        @pl.when(j == 0)
        def _():
            pc_ref[...] = jnp.zeros_like(pc_ref)

        pc_ref[...] += part

    return pl.pallas_call(
        body, name="ada_bwd", grid=(1536 // tn,),
        out_shape=[jax.ShapeDtypeStruct((D, 1536), F32), jax.ShapeDtypeStruct((8, D), F32)],
        in_specs=[pl.BlockSpec((64, D), lambda j: (0, 0)), pl.BlockSpec((64, tn), lambda j: (0, j)),
                  pl.BlockSpec((D, tn), lambda j: (0, j))],
        out_specs=[pl.BlockSpec((D, tn), lambda j: (0, j)), pl.BlockSpec((8, D), lambda j: (0, 0))],
        compiler_params=_cp(vmem=VMEM_BIG),
    )(cin, gmod_s, w_ada_s)


def _ada_small(gmod, parts, c_ctx):
    def body(g_ref, p_ref, c_ref, gb_ref, gc_ref):
        gb_ref[...] = jnp.sum(g_ref[...], axis=0, keepdims=True)
        ds = p_ref[0][0:1] + p_ref[2][0:1] + p_ref[4][0:1] + p_ref[6][0:1]
        cc = c_ref[...]
        s = _sigmoid(cc)
        gc_ref[...] = ds * (s * (1.0 + cc * (1.0 - s)))

    vm = pl.BlockSpec(memory_space=pltpu.VMEM)
    return pl.pallas_call(
        body, name="ada_small",
        out_shape=[jax.ShapeDtypeStruct((1, 6 * D), F32), jax.ShapeDtypeStruct((1, D), F32)],
        in_specs=[vm, vm, vm], out_specs=[vm, vm],
    )(gmod, parts, c_ctx)


def _tok_specs():
    xs = pl.BlockSpec((None, RT, D), lambda b, j: (b, jnp.maximum(j - 1, 0), 0))
    cs = pl.BlockSpec((None, RT, D), lambda b, j: (b, 0, 0))
    ms = pl.BlockSpec((None, 1, D), lambda b, j: (b, 0, 0))
    cms = pl.BlockSpec((None, 1, D), lambda b, j: (0, 0, 0))
    return xs, cs, ms, cms


def _inproj_fwd(x, ctx, sh1, sc1, csh1, csc1, n1w, w_in_p):
    def body(x_ref, c_ref, sh_ref, sc_ref, csh_ref, csc_ref, w1_ref, w_ref, p_ref, hm_ref):
        is_ctx = pl.program_id(1) == 0
        xt = jnp.where(is_ctx, c_ref[...], x_ref[...])
        sh = jnp.where(is_ctx, csh_ref[...], sh_ref[...])
        sc = jnp.where(is_ctx, csc_ref[...], sc_ref[...])
        hb = _norm_mod(xt, w1_ref[...], sh, sc).astype(BF16)
        hm_ref[...] = hb
        p_ref[...] = jnp.dot(hb, w_ref[...], preferred_element_type=F32)

    xs, cs, ms, cms = _tok_specs()
    return pl.pallas_call(
        body, name="inproj_fwd", grid=(NB, T // RT),
        out_shape=[jax.ShapeDtypeStruct((NB, T, PC), F32), jax.ShapeDtypeStruct((NB, T, D), BF16)],
        in_specs=[xs, cs, ms, ms, cms, cms, pl.BlockSpec((1, D), lambda b, j: (0, 0)),
                  pl.BlockSpec((D, PC), lambda b, j: (0, 0))],
        out_specs=[pl.BlockSpec((None, RT, PC), lambda b, j: (b, j, 0)),
                   pl.BlockSpec((None, RT, D), lambda b, j: (b, j, 0))],
        compiler_params=_cp(vmem=VMEM_BIG),
    )(x, ctx, sh1, sc1, csh1, csc1, n1w, w_in_p)


def _inproj_bwd(dp_dn, dz, dp_na, dgt, w_in_p, x, ctx, sh1, sc1, csh1, csc1, n1w, dx1):
    def body(d1_ref, d2_ref, d3_ref, d4_ref, w_ref, x_ref, c_ref, sh_ref, sc_ref, csh_ref, csc_ref, w1_ref,
             dx1_ref, dpb_ref, gx_ref, dsh_ref, dsc_ref, dcsh_ref, dcsc_ref, dw1_ref):
        b, j = pl.program_id(0), pl.program_id(1)
        is_ctx = j == 0
        dp = jnp.concatenate([d1_ref[...], d2_ref[...], d3_ref[...], d4_ref[...]], axis=1).astype(BF16)
        dpb_ref[...] = dp
        dhm = lax.dot_general(dp, w_ref[...], _NT, preferred_element_type=F32)
        xt = jnp.where(is_ctx, c_ref[...], x_ref[...])
        sh = jnp.where(is_ctx, csh_ref[...], sh_ref[...])
        sc = jnp.where(is_ctx, csc_ref[...], sc_ref[...])
        _, vjp = jax.vjp(_norm_mod, xt, w1_ref[...], sh, sc)
        dxt, dw1, dsh, dsc = vjp(dhm)

        @pl.when((b == 0) & (j == 0))
        def _():
            dw1_ref[...] = jnp.zeros_like(dw1_ref)
            dcsh_ref[...] = jnp.zeros_like(dcsh_ref)
            dcsc_ref[...] = jnp.zeros_like(dcsc_ref)

        dw1_ref[...] += dw1

        @pl.when(is_ctx)
        def _():
            dcsh_ref[...] += dsh
            dcsc_ref[...] += dsc

        @pl.when(j == 1)
        def _():
            dsh_ref[...] = jnp.zeros_like(dsh_ref)
            dsc_ref[...] = jnp.zeros_like(dsc_ref)

        @pl.when(j >= 1)
        def _():
            dsh_ref[...] += dsh
            dsc_ref[...] += dsc
            gx_ref[...] = dx1_ref[...] + dxt

    xs, cs, ms, cms = _tok_specs()
    tsp = lambda w: pl.BlockSpec((None, RT, w), lambda b, j: (b, j, 0))
    one = pl.BlockSpec((1, D), lambda b, j: (0, 0))
    return pl.pallas_call(
        body, name="inproj_bwd", grid=(NB, T // RT),
        out_shape=[jax.ShapeDtypeStruct((NB, T, PC), BF16), jax.ShapeDtypeStruct((NB, SEQ, D), F32),
                   jax.ShapeDtypeStruct((NB, 1, D), F32), jax.ShapeDtypeStruct((NB, 1, D), F32),
                   jax.ShapeDtypeStruct((1, 1, D), F32), jax.ShapeDtypeStruct((1, 1, D), F32),
                   jax.ShapeDtypeStruct((1, D), F32)],
        in_specs=[tsp(1536), tsp(512), tsp(1536), tsp(128), pl.BlockSpec((D, PC), lambda b, j: (0, 0)),
                  xs, cs, ms, ms, cms, cms, one, xs],
        out_specs=[tsp(PC), xs, ms, ms, cms, cms, one],
        compiler_params=_cp(vmem=VMEM_BIG),
    )(dp_dn, dz, dp_na, dgt, w_in_p, x, ctx, sh1, sc1, csh1, csc1, n1w, dx1)


def _seg_valid(t, off):
    lo = jnp.where(t < CT, 0, CT)
    hi = jnp.where(t < CT, CT, T)
    return (t + off >= lo) & (t + off < hi)


def _shifted(u, off):
    t = lax.broadcasted_iota(jnp.int32, (T, 1), 0)
    s = u if off == 0 else pltpu.roll(u, (-off) % T, 0)
    return jnp.where(_seg_valid(t, off), s, 0.0)


def _conv_pre(u, w):
    acc = _shifted(u, -2) * w[0:1, :]
    for kk in range(1, 5):
        acc = acc + _shifted(u, kk - 2) * w[kk:kk + 1, :]
    return acc


def _act_q(pre):
    a = _silu(pre)
    return a * lax.rsqrt(jnp.sum(a * a, axis=-1, keepdims=True) + EPS) * (128.0 ** -0.5)


def _act_k(pre):
    a = _silu(pre)
    return a * lax.rsqrt(jnp.sum(a * a, axis=-1, keepdims=True) + EPS)


def _dn_prep_fwd(p, cw):
    def body(pq_ref, pk_ref, pv_ref, wq_ref, wk_ref, wv_ref, q_ref, k_ref, v_ref):
        q_ref[...] = _act_q(_conv_pre(pq_ref[...], wq_ref[...]))
        k_ref[...] = _act_k(_conv_pre(pk_ref[...], wk_ref[...]))
        v_ref[...] = _silu(_conv_pre(pv_ref[...], wv_ref[...]))

    ps = lambda o: pl.BlockSpec((None, T, 128), lambda b, h: (b, 0, o + h))
    ws = lambda o: pl.BlockSpec((8, 128), lambda b, h: (0, o + h))
    os_ = pl.BlockSpec((None, T, 128), lambda b, h: (b, 0, h))
    return pl.pallas_call(
        body, name="dn_prep_fwd", grid=(NB, DNH), out_shape=[jax.ShapeDtypeStruct((NB, T, DNW), F32)] * 3,
        in_specs=[ps(0), ps(4), ps(8), ws(0), ws(4), ws(8)], out_specs=[os_] * 3,
        compiler_params=_cp(vmem=VMEM_BIG),
    )(p, p, p, cw, cw, cw)


def _dn_prep_bwd(p, cw, dqs, dks, dvs):
    def body(pq_ref, pk_ref, pv_ref, wq_ref, wk_ref, wv_ref, dqf_ref, dqb_ref, dkf_ref, dkb_ref, dvf_ref, dvb_ref,
             oq_ref, ok_ref, ov_ref, gq_ref, gk_ref, gv_ref):
        b = pl.program_id(1)

        def one(p_ref, w_ref, df_ref, db_ref, act, o_ref, g_ref):
            u = p_ref[...]
            w = w_ref[...]
            _, vjp = jax.vjp(act, _conv_pre(u, w))
            (dpre,) = vjp(df_ref[...] + db_ref[...])
            du = _shifted(dpre, 2) * w[0:1, :]
            rows = [jnp.sum(dpre * _shifted(u, -2), axis=0, keepdims=True)]
            for kk in range(1, 5):
                du = du + _shifted(dpre, 2 - kk) * w[kk:kk + 1, :]
                rows.append(jnp.sum(dpre * _shifted(u, kk - 2), axis=0, keepdims=True))
            o_ref[...] = du
            gw = jnp.concatenate(rows + [jnp.zeros((3, 128), F32)], axis=0)

            @pl.when(b == 0)
            def _():
                g_ref[...] = gw

            @pl.when(b > 0)
            def _():
                g_ref[...] += gw

        one(pq_ref, wq_ref, dqf_ref, dqb_ref, _act_q, oq_ref, gq_ref)
        one(pk_ref, wk_ref, dkf_ref, dkb_ref, _act_k, ok_ref, gk_ref)
        one(pv_ref, wv_ref, dvf_ref, dvb_ref, _silu, ov_ref, gv_ref)

    ps = lambda o: pl.BlockSpec((None, T, 128), lambda h, b: (b, 0, o + h))
    ws = lambda o: pl.BlockSpec((8, 128), lambda h, b: (0, o + h))
    ds = pl.BlockSpec((None, T, 128), lambda h, b: (b, 0, h))
    outs = pl.pallas_call(
        body, name="dn_prep_bwd", grid=(DNH, NB),
        out_shape=[jax.ShapeDtypeStruct((NB, T, DNW), F32)] * 3 + [jax.ShapeDtypeStruct((8, DNW), F32)] * 3,
        in_specs=[ps(0), ps(4), ps(8), ws(0), ws(4), ws(8)] + [ds] * 6,
        out_specs=[ds] * 3
        + [pl.BlockSpec((8, 128), lambda h, b: (0, h))] * 3,
        compiler_params=_cp(vmem=VMEM_BIG),
    )(p, p, p, cw, cw, cw, *dqs, *dks, *dvs)
    return outs


def _gate_fn(a, prm):
    lane = lax.broadcasted_iota(jnp.int32, a.shape, 1)
    z = a + prm[1:2, :]
    sp = jnp.maximum(z, 0.0) + jnp.log(1.0 + jnp.exp(-jnp.abs(z)))
    g = -jnp.exp(prm[0:1, :]) * sp
    return jnp.where(lane < 8, g, jnp.where(lane < 16, _sigmoid(a), 0.0))


def _gates_fwd(p, prm):
    def body(a_ref, prm_ref, o_ref):
        o_ref[...] = _gate_fn(a_ref[...], prm_ref[...])

    return pl.pallas_call(
        body, name="gates_fwd", grid=(NB,), out_shape=jax.ShapeDtypeStruct((NB, T, 128), F32),
        in_specs=[pl.BlockSpec((None, T, 128), lambda b: (b, 0, COL_G // 128)), pl.BlockSpec((8, 128), lambda b: (0, 0))],
        out_specs=pl.BlockSpec((None, T, 128), lambda b: (b, 0, 0)),
    )(p, prm)


def _gates_bwd(p, prm, dgf, dgb):
    def body(a_ref, prm_ref, dgf_ref, dgb_ref, o_ref, dprm_ref):
        b = pl.program_id(0)
        tot = dgf_ref[...] + dgb_ref[...]
        _, vjp = jax.vjp(_gate_fn, a_ref[...], prm_ref[...])
        da, dprm = vjp(tot)
        o_ref[...] = da

        @pl.when(b == 0)
        def _():
            dprm_ref[...] = dprm

        @pl.when(b > 0)
        def _():
            dprm_ref[...] += dprm

    return pl.pallas_call(
        body, name="gates_bwd", grid=(NB,),
        out_shape=[jax.ShapeDtypeStruct((NB, T, 128), F32), jax.ShapeDtypeStruct((8, 128), F32)],
        in_specs=[pl.BlockSpec((None, T, 128), lambda b: (b, 0, COL_G // 128)), pl.BlockSpec((8, 128), lambda b: (0, 0)),
                  pl.BlockSpec((None, T, 128), lambda b: (b, 0, 0)), pl.BlockSpec((None, T, 128), lambda b: (b, 0, 0))],
        out_specs=[pl.BlockSpec((None, T, 128), lambda b: (b, 0, 0)), pl.BlockSpec((8, 128), lambda b: (0, 0))],
        compiler_params=_cp(vmem=VMEM_BIG),
    )(p, prm, dgf, dgb)


def _tri_inv_impl(m, md, m1, m2, eye):
    n = -(m * md)
    t = eye + n
    pw = n
    for _ in range(3):
        pw = _dg(pw, pw, _NN)
        t = t + _dg(t, pw, _NN)
    t = t - _dg(_dg(t, m * m1, _NN), t, _NN)
    t = t - _dg(_dg(t, m * m2, _NN), t, _NN)
    return t


@jax.custom_vjp
def _tri_inv(m, md, m1, m2, eye):
    return _tri_inv_impl(m, md, m1, m2, eye)


def _tri_inv_f(m, md, m1, m2, eye):
    t = _tri_inv_impl(m, md, m1, m2, eye)
    return t, (t, md)


def _tri_inv_b(res, g):
    t, md = res
    tt = t.T
    z = jnp.zeros_like(md)
    return (-_dg(_dg(tt, g, _NN), tt, _NN), z, z, z, z)


_tri_inv.defvjp(_tri_inv_f, _tri_inv_b)


def _chunk_consts(d, h):
    ii = lax.broadcasted_iota(jnp.int32, (CH, CH), 0)
    jj = lax.broadcasted_iota(jnp.int32, (CH, CH), 1)
    diff = ii - jj if d == 0 else jj - ii
    f = lambda m: jnp.where(m, 1.0, 0.0).astype(F32)
    incl, strict = f(diff >= 0), f(diff > 0)
    md = f((ii >> 4) == (jj >> 4))
    m1 = f(((ii >> 5) == (jj >> 5)) & ((ii >> 4) != (jj >> 4)))
    m2 = f((ii >> 5) != (jj >> 5))
    eye = f(ii == jj)
    lane = lax.broadcasted_iota(jnp.int32, (1, 128), 1)
    sub = lax.broadcasted_iota(jnp.int32, (128, 1), 0)
    gl = d * DNH + h
    return dict(incl=incl, strict=strict, md=md, m1=m1, m2=m2, eye=eye,
                oh_g=f(lane == gl), oh_b=f(lane == gl + 8), oh_gt=f(sub == gl))


def _chunk_fn(q, k, v, gb, s, cst):
    incl = cst["incl"]
    g = jnp.sum(gb * cst["oh_g"], axis=1, keepdims=True)
    beta = jnp.sum(gb * cst["oh_b"], axis=1, keepdims=True)
    gcum = _mm(incl, gb)
    gam = jnp.sum(gcum * cst["oh_g"], axis=1, keepdims=True)
    gam_row = jnp.sum(gcum.T * cst["oh_gt"], axis=0, keepdims=True)
    gtot = jnp.sum(g, axis=0, keepdims=True)
    dec = jnp.exp(jnp.where(incl > 0.5, gam - gam_row, NEG))
    e_gam = jnp.exp(gam)
    kb = k * beta
    m = cst["strict"] * (_mm_nt(kb, k) * dec)
    tm = _tri_inv(m, cst["md"], cst["m1"], cst["m2"], cst["eye"])
    u = _mm(tm, v * beta)
    w = _mm(tm, kb * e_gam)
    qk = _mm_nt(q, k) * dec
    v_new = u - _mm(w, s)
    o = _mm(q * e_gam, s) + _mm(qk, v_new)
    s_new = s * jnp.exp(gtot) + _mm_tn(k * jnp.exp(gtot - gam), v_new)
    return o, s_new


def _chunk_rev(s):
    return jnp.where(s < NCH_CTX, NCH_CTX - 1 - s, NCH + NCH_CTX - 1 - s)


def _scan_specs(step):
    cf = lambda s: step(s)
    cb = lambda s: _chunk_rev(step(s))
    tok = lambda c: pl.BlockSpec((None, CH, DNW), lambda b, s: (b, c(s), 0))
    gat = lambda c: pl.BlockSpec((None, CH, 128), lambda b, s: (b, c(s), 0))
    sta = lambda c: pl.BlockSpec((None, None, DNH, 128, 128), lambda b, s: (b, c(s), 0, 0, 0))
    return (tok(cf), gat(cf), sta(cf)), (tok(cb), gat(cb), sta(cb))


def _scan_fwd(q, k, v, gates):
    def body(qf, kf, vf, gf, qb, kb, vb, gb, of_ref, sf_ref, ob_ref, sb_ref, s_ref):
        @pl.when(pl.program_id(1) == 0)
        def _():
            s_ref[...] = jnp.zeros_like(s_ref)

        dirs = ((qf, kf, vf, gf, of_ref, sf_ref), (qb, kb, vb, gb, ob_ref, sb_ref))
        loaded = []
        for d, (q_r, k_r, v_r, g_r, _, _) in enumerate(dirs):
            gblk = g_r[...]
            for h in range(DNH):
                sl = slice(128 * h, 128 * h + 128)
                loaded.append((q_r[:, sl], k_r[:, sl], v_r[:, sl], gblk, s_ref[d * DNH + h]))
        outs = [_chunk_fn(*loaded[i], _chunk_consts(i // DNH, i % DNH)) for i in range(2 * DNH)]
        for i, (o, s_new) in enumerate(outs):
            d, h = i // DNH, i % DNH
            dirs[d][5][h] = loaded[i][4]
            dirs[d][4][:, 128 * h:128 * h + 128] = o
            s_ref[i] = s_new

    (tf, gf_, sf_), (tb, gb_, sb_) = _scan_specs(lambda s: s)
    o_sh = jax.ShapeDtypeStruct((NB, T, DNW), F32)
    st_sh = jax.ShapeDtypeStruct((NB, NCH, DNH, 128, 128), F32)
    return pl.pallas_call(
        body, name="scan_fwd", grid=(NB, NCH), out_shape=[o_sh, st_sh, o_sh, st_sh],
        in_specs=[tf, tf, tf, gf_, tb, tb, tb, gb_], out_specs=[tf, sf_, tb, sb_],
        scratch_shapes=[pltpu.VMEM((2 * DNH, 128, 128), F32)],
        compiler_params=_cp(vmem=VMEM_BIG),
    )(q, k, v, gates, q, k, v, gates)


def _scan_bwd(q, k, v, gates, st_f, st_b, do):
    def body(qf, kf, vf, gf, sf, dof, qb, kb, vb, gb, sb, dob,
             dqf, dkf, dvf, dgf, dqb, dkb, dvb, dgb, ds_ref):
        @pl.when(pl.program_id(1) == 0)
        def _():
            ds_ref[...] = jnp.zeros_like(ds_ref)

        dirs = ((qf, kf, vf, gf, sf, dof, dqf, dkf, dvf, dgf), (qb, kb, vb, gb, sb, dob, dqb, dkb, dvb, dgb))
        loaded = []
        for d, (q_r, k_r, v_r, g_r, st_r, do_r) in enumerate(t[:6] for t in dirs):
            gblk = g_r[...]
            for h in range(DNH):
                sl = slice(128 * h, 128 * h + 128)
                loaded.append(((q_r[:, sl], k_r[:, sl], v_r[:, sl], gblk, st_r[h]), (do_r[:, sl], ds_ref[d * DNH + h])))
        grads = []
        for i, (prim, cot) in enumerate(loaded):
            cst = _chunk_consts(i // DNH, i % DNH)
            _, vjp = jax.vjp(lambda a, b_, c, e, f, cst=cst: _chunk_fn(a, b_, c, e, f, cst), *prim)
            grads.append(vjp(cot))
        for d in range(2):
            dq_r, dk_r, dv_r, dg_r = dirs[d][6:]
            dg_tot = grads[d * DNH][3]
            for h in range(DNH):
                sl = slice(128 * h, 128 * h + 128)
                dq, dk, dv, dg, ds = grads[d * DNH + h]
                dq_r[:, sl] = dq
                dk_r[:, sl] = dk
                dv_r[:, sl] = dv
                if h:
                    dg_tot = dg_tot + dg
                ds_ref[d * DNH + h] = ds
            dg_r[...] = dg_tot

    (tf, gf_, sf_), (tb, gb_, sb_) = _scan_specs(lambda s: NCH - 1 - s)
    o_sh = jax.ShapeDtypeStruct((NB, T, DNW), F32)
    g_sh = jax.ShapeDtypeStruct((NB, T, 128), F32)
    return pl.pallas_call(
        body, name="scan_bwd", grid=(NB, NCH), out_shape=[o_sh, o_sh, o_sh, g_sh] * 2,
        in_specs=[tf, tf, tf, gf_, sf_, tf, tb, tb, tb, gb_, sb_, tb],
        out_specs=[tf, tf, tf, gf_, tb, tb, tb, gb_],
        scratch_shapes=[pltpu.VMEM((2 * DNH, 128, 128), F32)],
        compiler_params=_cp(vmem=VMEM_BIG),
    )(q, k, v, gates, st_f, do, q, k, v, gates, st_b, do)


def _segmean(a, bd):
    hi = a.astype(BF16)
    lo = (a - hi.astype(F32)).astype(BF16)
    return jnp.dot(hi, bd, preferred_element_type=F32) + jnp.dot(lo, bd, preferred_element_type=F32)


def _na_prep_fwd(p, qw, kw, bd):
    def body(q_ref, k_ref, v_ref, qw_ref, kw_ref, bd_ref, oq_ref, ok_ref, ov_ref):
        bdm = bd_ref[...]
        q = q_ref[...]
        k = k_ref[...]
        oq_ref[...] = (q * lax.rsqrt(_segmean(q * q, bdm) + EPS) * qw_ref[...] * 0.125).astype(BF16)
        ok_ref[...] = (k * lax.rsqrt(_segmean(k * k, bdm) + EPS) * kw_ref[...]).astype(BF16)
        ov_ref[...] = v_ref[...].astype(BF16)

    ps = lambda o: pl.BlockSpec((None, RT, NAW), lambda b, j: (b, j, COL_NA // NAW + o))
    one = pl.BlockSpec((1, NAW), lambda b, j: (0, 0))
    os_ = pl.BlockSpec((None, RT, NAW), lambda b, j: (b, j, 0))
    return pl.pallas_call(
        body, name="na_prep_fwd", grid=(NB, T // RT), out_shape=[jax.ShapeDtypeStruct((NB, T, NAW), BF16)] * 3,
        in_specs=[ps(0), ps(1), ps(2), one, one, pl.BlockSpec((NAW, NAW), lambda b, j: (0, 0))], out_specs=[os_] * 3,
    )(p, p, p, qw, kw, bd)


def _na_prep_bwd(p, qw, kw, bd, dq, dk, dv):
    def body(q_ref, k_ref, qw_ref, kw_ref, bd_ref, dq_ref, dk_ref, dv_ref, o_ref, gq_ref, gk_ref):
        b, j = pl.program_id(0), pl.program_id(1)
        bdm = bd_ref[...]

        def one(x, w, dy, scale):
            r = lax.rsqrt(_segmean(x * x, bdm) + EPS)
            g = dy * w * scale
            dx = r * g - x * (r * r * r) * _segmean(g * x, bdm)
            dw = jnp.sum(dy * x * r * scale, axis=0, keepdims=True)
            return dx, dw

        dqv = jnp.where(j == 0, 0.0, dq_ref[...])
        dxq, dwq = one(q_ref[...], qw_ref[...], dqv, 0.125)
        dxk, dwk = one(k_ref[...], kw_ref[...], dk_ref[...], 1.0)
        o_ref[...] = jnp.concatenate([dxq, dxk, dv_ref[...]], axis=1)
        pad = jnp.zeros((7, NAW), F32)

        @pl.when((b == 0) & (j == 0))
        def _():
            gq_ref[...] = jnp.zeros_like(gq_ref)
            gk_ref[...] = jnp.zeros_like(gk_ref)

        gq_ref[...] += jnp.concatenate([dwq, pad], axis=0)
        gk_ref[...] += jnp.concatenate([dwk, pad], axis=0)

    ps = lambda o: pl.BlockSpec((None, RT, NAW), lambda b, j: (b, j, COL_NA // NAW + o))
    one = pl.BlockSpec((1, NAW), lambda b, j: (0, 0))
    ts = pl.BlockSpec((None, RT, NAW), lambda b, j: (b, j, 0))
    acc = pl.BlockSpec((8, NAW), lambda b, j: (0, 0))
    return pl.pallas_call(
        body, name="na_prep_bwd", grid=(NB, T // RT),
        out_shape=[jax.ShapeDtypeStruct((NB, T, 1536), F32), jax.ShapeDtypeStruct((8, NAW), F32),
                   jax.ShapeDtypeStruct((8, NAW), F32)],
        in_specs=[ps(0), ps(1), one, one, pl.BlockSpec((NAW, NAW), lambda b, j: (0, 0)),
                  pl.BlockSpec((None, RT, NAW), lambda b, j: (b, jnp.maximum(j - 1, 0), 0)), ts, ts],
        out_specs=[pl.BlockSpec((None, RT, 1536), lambda b, j: (b, j, 0)), acc, acc],
        compiler_params=_cp(vmem=VMEM_BIG),
    )(p, p, qw, kw, bd, dq, dk, dv)


def _na_row0(r):
    return jnp.clip(r - WIN_ROWS // 2, 0, ROWS - WIN_ROWS)


def _na_cls(r):
    return _na_row0(r) - r + WIN_ROWS - 1


def _na_scores(q, kw, kc, bias, hh):
    lane = lax.broadcasted_iota(jnp.int32, (1, 128), 1)
    hm = ((lane >> 6) == hh)
    qh = jnp.where(hm, q, jnp.zeros_like(q))
    s_w = lax.dot_general(qh, kw, _NT, preferred_element_type=F32) + bias
    s_c = lax.dot_general(qh, kc, _NT, preferred_element_type=F32)
    mx = jnp.maximum(jnp.max(s_w, axis=1, keepdims=True), jnp.max(s_c, axis=1, keepdims=True))
    p_w = jnp.exp(s_w - mx)
    p_c = jnp.exp(s_c - mx)
    inv = 1.0 / (jnp.sum(p_w, axis=1, keepdims=True) + jnp.sum(p_c, axis=1, keepdims=True))
    return hm, qh, p_w, p_c, inv


def _na_fwd(qn, kn, vb, bias):
    def body(q_ref, k_ref, v_ref, b_ref, o_ref):
        r = pl.program_id(2)
        start = pl.multiple_of(CT + _na_row0(r) * GRID_W, GRID_W)
        q = q_ref[...]
        kw, vw = k_ref[pl.ds(start, NWIN), :], v_ref[pl.ds(start, NWIN), :]
        kc, vc = k_ref[0:CT, :], v_ref[0:CT, :]
        out = jnp.zeros((GRID_W, 128), F32)
        for hh in range(2):
            hm, _, p_w, p_c, inv = _na_scores(q, kw, kc, b_ref[hh], hh)
            o = (jnp.dot(p_w.astype(BF16), vw, preferred_element_type=F32)
                 + jnp.dot(p_c.astype(BF16), vc, preferred_element_type=F32)) * inv
            out = jnp.where(hm, o, out)
        o_ref[...] = out.astype(BF16)

    kv = pl.BlockSpec((None, T, 128), lambda g, b, r: (b, 0, g))
    return pl.pallas_call(
        body, name="na_fwd", grid=(NAH // 2, NB, ROWS), out_shape=jax.ShapeDtypeStruct((NB, SEQ, NAW), BF16),
        in_specs=[pl.BlockSpec((None, GRID_W, 128), lambda g, b, r: (b, CT // GRID_W + r, g)), kv, kv,
                  pl.BlockSpec((None, 2, GRID_W, NWIN), lambda g, b, r: (_na_cls(r), g, 0, 0))],
        out_specs=pl.BlockSpec((None, GRID_W, 128), lambda g, b, r: (b, r, g)),
    )(qn, kn, vb, bias)


def _na_bwd(qn, kn, vb, bias, do):
    def body(q_ref, k_ref, v_ref, b_ref, do_ref, dq_ref, dk_ref, dv_ref, db_ref):
        b, r = pl.program_id(1), pl.program_id(2)

        @pl.when((b == 0) & (r == 0))
        def _():
            db_ref[...] = jnp.zeros_like(db_ref)

        @pl.when(r == 0)
        def _():
            dk_ref[...] = jnp.zeros_like(dk_ref)
            dv_ref[...] = jnp.zeros_like(dv_ref)

        start = pl.multiple_of(CT + _na_row0(r) * GRID_W, GRID_W)
        cls = _na_cls(r)
        q = q_ref[...]
        kw, vw = k_ref[pl.ds(start, NWIN), :], v_ref[pl.ds(start, NWIN), :]
        kc, vc = k_ref[0:CT, :], v_ref[0:CT, :]
        dout = do_ref[...]
        dq = jnp.zeros((GRID_W, 128), F32)
        dkw = jnp.zeros((NWIN, 128), F32)
        dvw = jnp.zeros((NWIN, 128), F32)
        dkc = jnp.zeros((CT, 128), F32)
        dvc = jnp.zeros((CT, 128), F32)
        for hh in range(2):
            hm, qh, p_w, p_c, inv = _na_scores(q, kw, kc, b_ref[hh], hh)
            p_w = p_w * inv
            p_c = p_c * inv
            doh = jnp.where(hm, dout, 0.0).astype(BF16)
            dp_w = lax.dot_general(doh, vw, _NT, preferred_element_type=F32)
            dp_c = lax.dot_general(doh, vc, _NT, preferred_element_type=F32)
            delta = jnp.sum(p_w * dp_w, axis=1, keepdims=True) + jnp.sum(p_c * dp_c, axis=1, keepdims=True)
            ds_w = p_w * (dp_w - delta)
            ds_c = p_c * (dp_c - delta)
            db_ref[cls, hh] += ds_w
            ds_wb, ds_cb = ds_w.astype(BF16), ds_c.astype(BF16)
            dqh = jnp.dot(ds_wb, kw, preferred_element_type=F32) + jnp.dot(ds_cb, kc, preferred_element_type=F32)
            dq = jnp.where(hm, dqh, dq)
            dkw = dkw + lax.dot_general(ds_wb, qh, _TN, preferred_element_type=F32)
            dkc = dkc + lax.dot_general(ds_cb, qh, _TN, preferred_element_type=F32)
            dvw = dvw + lax.dot_general(p_w.astype(BF16), doh, _TN, preferred_element_type=F32)
            dvc = dvc + lax.dot_general(p_c.astype(BF16), doh, _TN, preferred_element_type=F32)
        dq_ref[...] = dq
        dk_ref[pl.ds(start, NWIN), :] += dkw
        dv_ref[pl.ds(start, NWIN), :] += dvw
        dk_ref[0:CT, :] += dkc
        dv_ref[0:CT, :] += dvc

    kv = pl.BlockSpec((None, T, 128), lambda g, b, r: (b, 0, g))
    return pl.pallas_call(
        body, name="na_bwd", grid=(NAH // 2, NB, ROWS),
        out_shape=[jax.ShapeDtypeStruct((NB, SEQ, NAW), F32)] + [jax.ShapeDtypeStruct((NB, T, NAW), F32)] * 2
        + [jax.ShapeDtypeStruct((WIN_ROWS, NAH, GRID_W, NWIN), F32)],
        in_specs=[pl.BlockSpec((None, GRID_W, 128), lambda g, b, r: (b, CT // GRID_W + r, g)), kv, kv,
                  pl.BlockSpec((None, 2, GRID_W, NWIN), lambda g, b, r: (_na_cls(r), g, 0, 0)),
                  pl.BlockSpec((None, GRID_W, 128), lambda g, b, r: (b, r, g))],
        out_specs=[pl.BlockSpec((None, GRID_W, 128), lambda g, b, r: (b, r, g)), kv, kv,
                   pl.BlockSpec((WIN_ROWS, 2, GRID_W, NWIN), lambda g, b, r: (0, g, 0, 0))],
        compiler_params=_cp(vmem=VMEM_BIG),
    )(qn, kn, vb, bias, do)


def _na_tables():
    cols = np.arange(GRID_W)
    win_start = np.clip(cols - WIN_COLS // 2, 0, GRID_W - WIN_COLS)
    kc = cols[None, :]
    valid = (kc >= win_start[:, None]) & (kc < win_start[:, None] + WIN_COLS)
    rel = np.clip(kc - cols[:, None] + WIN_COLS - 1, 0, 2 * WIN_COLS - 2)
    return valid, rel


def _bias_table(rpb):
    valid, rel = _na_tables()
    t15 = jnp.where(valid[None, None], rpb[:, :, rel], NEG)
    tab = jnp.stack([t15[:, c:c + WIN_ROWS] for c in range(WIN_ROWS)], axis=0)
    return tab.transpose(0, 1, 3, 2, 4).reshape(WIN_ROWS, NAH, GRID_W, NWIN)


def _rpb_fold(db):
    def body1(d_ref, o_ref):
        for dr in range(2 * WIN_ROWS - 1):
            acc = None
            for c in range(WIN_ROWS):
                i = dr - c
                if 0 <= i < WIN_ROWS:
                    blk = d_ref[c, :, i * GRID_W:(i + 1) * GRID_W]
                    acc = blk if acc is None else acc + blk
            o_ref[dr] = acc

    t15 = pl.pallas_call(
        body1, name="rpb_fold_rows", grid=(NAH,),
        out_shape=jax.ShapeDtypeStruct((NAH, 2 * WIN_ROWS - 1, GRID_W, GRID_W), F32),
        in_specs=[pl.BlockSpec((WIN_ROWS, None, GRID_W, NWIN), lambda h: (0, h, 0, 0))],
        out_specs=pl.BlockSpec((None, 2 * WIN_ROWS - 1, GRID_W, GRID_W), lambda h: (h, 0, 0, 0)),
        compiler_params=_cp(vmem=VMEM_BIG),
    )(db)
    valid, rel = _na_tables()
    onehot = np.zeros((GRID_W * GRID_W, 128), np.float32)
    flat_rel, flat_valid = rel.reshape(-1), valid.reshape(-1)
    onehot[np.arange(GRID_W * GRID_W)[flat_valid], flat_rel[flat_valid]] = 1.0

    def body2(a_ref, oh_ref, o_ref):
        o_ref[...] = _hmm(a_ref[...], oh_ref[...])

    vm = pl.BlockSpec(memory_space=pltpu.VMEM)
    out = pl.pallas_call(
        body2, name="rpb_fold_cols", out_shape=jax.ShapeDtypeStruct((NAH * 15, 128), F32),
        in_specs=[vm, vm], out_specs=vm, compiler_params=_cp(vmem=VMEM_BIG),
    )(t15.reshape(NAH * 15, GRID_W * GRID_W), jnp.asarray(onehot))
    return out[:, :31].reshape(NAH, 15, 31)


def _dn_out(o, z, w):
    outs = []
    for h in range(DNH):
        sl = slice(128 * h, 128 * h + 128)
        oh = o[:, sl]
        r = lax.rsqrt(jnp.mean(oh * oh, axis=-1, keepdims=True) + EPS)
        outs.append(oh * r * w[:, sl] * _silu(z[:, sl]))
    return jnp.concatenate(outs, axis=1)


def _outproj_fwd(of, ob, p, na, onw, x, g1, w_out):
    def body(of_ref, ob_ref, z_ref, na_ref, w_ref, x_ref, g_ref, wo_ref, x1_ref, ai_ref, ao_ref):
        dn = _dn_out(of_ref[...] + ob_ref[...], z_ref[...], w_ref[...])
        ai = jnp.concatenate([dn.astype(BF16), na_ref[...]], axis=1)
        ai_ref[...] = ai
        ao = jnp.dot(ai, wo_ref[...], preferred_element_type=F32)
        ao_ref[...] = ao
        x1_ref[...] = x_ref[...] + g_ref[...] * ao

    lat = lambda w: pl.BlockSpec((None, RT, w), lambda b, j: (b, j, 0))
    return pl.pallas_call(
        body, name="outproj_fwd", grid=(NB, SEQ // RT),
        out_shape=[jax.ShapeDtypeStruct((NB, SEQ, D), F32), jax.ShapeDtypeStruct((NB, SEQ, D), BF16),
                   jax.ShapeDtypeStruct((NB, SEQ, D), F32)],
        in_specs=[pl.BlockSpec((None, RT, DNW), lambda b, j: (b, j + 1, 0)),
                  pl.BlockSpec((None, RT, DNW), lambda b, j: (b, j + 1, 0)),
                  pl.BlockSpec((None, RT, DNW), lambda b, j: (b, j + 1, COL_Z // DNW)),
                  lat(NAW), pl.BlockSpec((1, DNW), lambda b, j: (0, 0)), lat(D),
                  pl.BlockSpec((None, 1, D), lambda b, j: (b, 0, 0)), pl.BlockSpec((D, D), lambda b, j: (0, 0))],
        out_specs=[lat(D), lat(D), lat(D)],
        compiler_params=_cp(vmem=VMEM_BIG),
    )(of, ob, p, na, onw, x, g1, w_out)


def _outproj_bwd(dx1, ao, g1, w_out, of, ob, p, onw):
    def body(dx_ref, ao_ref, g_ref, wo_ref, of_ref, ob_ref, z_ref, w_ref, dao_ref, do_ref, dz_ref, dna_ref, dg_ref, dw_ref):
        b, j = pl.program_id(0), pl.program_id(1)

        @pl.when((b == 0) & (j == 0))
        def _():
            dw_ref[...] = jnp.zeros_like(dw_ref)

        @pl.when(j == 0)
        def _():
            do_ref[...] = jnp.zeros_like(do_ref)
            dz_ref[...] = jnp.zeros_like(dz_ref)

        @pl.when(j == 1)
        def _():
            dg_ref[...] = jnp.zeros_like(dg_ref)

        @pl.when(j >= 1)
        def _():
            dx = dx_ref[...]
            dg_ref[...] += jnp.sum(dx * ao_ref[...], axis=0, keepdims=True)
            dao = (g_ref[...] * dx).astype(BF16)
            dao_ref[...] = dao
            dai = lax.dot_general(dao, wo_ref[...], _NT, preferred_element_type=F32)
            dna_ref[...] = dai[:, DNW:]
            _, vjp = jax.vjp(_dn_out, of_ref[...] + ob_ref[...], z_ref[...], w_ref[...])
            do, dz, dw = vjp(dai[:, :DNW])
            do_ref[...] = do
            dz_ref[...] = dz
            dw_ref[...] += jnp.concatenate([dw, jnp.zeros((7, DNW), F32)], axis=0)

    lat = lambda w: pl.BlockSpec((None, RT, w), lambda b, j: (b, jnp.maximum(j - 1, 0), 0))
    tok = lambda w: pl.BlockSpec((None, RT, w), lambda b, j: (b, j, 0))
    return pl.pallas_call(
        body, name="outproj_bwd", grid=(NB, T // RT),
        out_shape=[jax.ShapeDtypeStruct((NB, SEQ, D), BF16), jax.ShapeDtypeStruct((NB, T, DNW), F32),
                   jax.ShapeDtypeStruct((NB, T, DNW), F32), jax.ShapeDtypeStruct((NB, SEQ, NAW), F32),
                   jax.ShapeDtypeStruct((NB, 1, D), F32), jax.ShapeDtypeStruct((8, DNW), F32)],
        in_specs=[lat(D), lat(D), pl.BlockSpec((None, 1, D), lambda b, j: (b, 0, 0)),
                  pl.BlockSpec((D, D), lambda b, j: (0, 0)),
                  tok(DNW), tok(DNW),
                  pl.BlockSpec((None, RT, DNW), lambda b, j: (b, j, COL_Z // DNW)),
                  pl.BlockSpec((1, DNW), lambda b, j: (0, 0))],
        out_specs=[lat(D), tok(DNW), tok(DNW), lat(NAW), pl.BlockSpec((None, 1, D), lambda b, j: (b, 0, 0)),
                   pl.BlockSpec((8, DNW), lambda b, j: (0, 0))],
        compiler_params=_cp(vmem=VMEM_BIG),
    )(dx1, ao, g1, w_out, of, ob, p, onw)


FT = 256
NFT = NTOK // FT
TPE = SEQ // FT
NFF = DFF // FFT


def _ffn_specs():
    row = lambda w: pl.BlockSpec((FT, w), lambda i, f: (i, 0))
    ex = pl.BlockSpec((None, 1, D), lambda i, f: (i // TPE, 0, 0))
    one = pl.BlockSpec((1, D), lambda i, f: (0, 0))
    wg = pl.BlockSpec((None, D, FFT), lambda i, f: (f, 0, 0))
    wu = pl.BlockSpec((None, D, FFT), lambda i, f: (NFF + f, 0, 0))
    wo = pl.BlockSpec((FFT, D), lambda i, f: (f, 0))
    ff = pl.BlockSpec((FT, FFT), lambda i, f: (i, f))
    return row, ex, one, wg, wu, wo, ff


def _ffn_fwd(x1, sh2, sc2, g2, n2w, w_in4, w_out, target):
    def body(x_ref, sh_ref, sc_ref, g_ref, w2_ref, wg_ref, wu_ref, wo_ref, t_ref,
             h2_ref, gt_ref, up_ref, dy_ref, dg_ref, loss_ref, acc_ref):
        i, f = pl.program_id(0), pl.program_id(1)

        @pl.when(f == 0)
        def _():
            h2_ref[...] = _norm_mod(x_ref[...], w2_ref[...], sh_ref[...], sc_ref[...]).astype(BF16)
            acc_ref[...] = jnp.zeros_like(acc_ref)

        h2 = h2_ref[...]
        gate = jnp.dot(h2, wg_ref[...], preferred_element_type=F32)
        up = jnp.dot(h2, wu_ref[...], preferred_element_type=F32)
        gt_ref[...] = gate.astype(BF16)
        up_ref[...] = up.astype(BF16)
        acc_ref[...] += jnp.dot((_silu(gate) * up).astype(BF16), wo_ref[...], preferred_element_type=F32)

        @pl.when((i == 0) & (f == 0))
        def _():
            loss_ref[...] = jnp.zeros_like(loss_ref)

        @pl.when(f == NFF - 1)
        def _():
            ffn = acc_ref[...]
            err = x_ref[...] + g_ref[...] * ffn - t_ref[...]
            dy = err * (1.0 / D)
            dy_ref[...] = dy
            part = jnp.sum(dy * ffn, axis=0, keepdims=True)
            loss_ref[...] += jnp.sum(jnp.sum(err * err, axis=1, keepdims=True), axis=0, keepdims=True) * (0.5 / D)

            @pl.when(i % TPE == 0)
            def _():
                dg_ref[...] = part

            @pl.when(i % TPE != 0)
            def _():
                dg_ref[...] += part

    row, ex, one, wg, wu, wo, ff = _ffn_specs()
    return pl.pallas_call(
        body, name="ffn_fwd", grid=(NFT, NFF),
        out_shape=[jax.ShapeDtypeStruct((NTOK, D), BF16), jax.ShapeDtypeStruct((NTOK, DFF), BF16),
                   jax.ShapeDtypeStruct((NTOK, DFF), BF16), jax.ShapeDtypeStruct((NTOK, D), F32),
                   jax.ShapeDtypeStruct((NB, 1, D), F32), jax.ShapeDtypeStruct((8, 128), F32)],
        in_specs=[row(D), ex, ex, ex, one, wg, wu, wo, row(D)],
        out_specs=[row(D), ff, ff, row(D), ex, pl.BlockSpec((8, 128), lambda i, f: (0, 0))],
        scratch_shapes=[pltpu.VMEM((FT, D), F32)],
        compiler_params=_cp(vmem=VMEM_BIG),
    )(x1, sh2, sc2, g2, n2w, w_in4, w_in4, w_out, target)


def _ffn_bwd(dy, gate, up, x1, sh2, sc2, g2, n2w, w_in4, w_out):
    def body(dy_ref, gt_ref, up_ref, x_ref, sh_ref, sc_ref, g_ref, w2_ref, wg_ref, wu_ref, wo_ref,
             dff_ref, act_ref, dgu_ref, dx_ref, dsh_ref, dsc_ref, dw_ref, acc_ref):
        i, f = pl.program_id(0), pl.program_id(1)

        @pl.when(f == 0)
        def _():
            dff_ref[...] = (g_ref[...] * dy_ref[...]).astype(BF16)
            acc_ref[...] = jnp.zeros_like(acc_ref)

        dact = lax.dot_general(dff_ref[...], wo_ref[...], _NT, preferred_element_type=F32)
        gate, up = gt_ref[...].astype(F32), up_ref[...].astype(F32)
        sg = _sigmoid(gate)
        sl = gate * sg
        act_ref[...] = (sl * up).astype(BF16)
        dgate = (dact * up * (sg * (1.0 + gate * (1.0 - sg)))).astype(BF16)
        dup = (dact * sl).astype(BF16)
        dgu_ref[0] = dgate
        dgu_ref[1] = dup
        acc_ref[...] += (lax.dot_general(dgate, wg_ref[...], _NT, preferred_element_type=F32)
                         + lax.dot_general(dup, wu_ref[...], _NT, preferred_element_type=F32))

        @pl.when((i == 0) & (f == 0))
        def _():
            dw_ref[...] = jnp.zeros_like(dw_ref)

        @pl.when(f == NFF - 1)
        def _():
            _, vjp = jax.vjp(_norm_mod, x_ref[...], w2_ref[...], sh_ref[...], sc_ref[...])
            dx, dw, dsh, dsc = vjp(acc_ref[...])
            dx_ref[...] = dy_ref[...] + dx
            dw_ref[...] += dw

            @pl.when(i % TPE == 0)
            def _():
                dsh_ref[...] = dsh
                dsc_ref[...] = dsc

            @pl.when(i % TPE != 0)
            def _():
                dsh_ref[...] += dsh
                dsc_ref[...] += dsc

    row, ex, one, wg, wu, wo, ff = _ffn_specs()
    return pl.pallas_call(
        body, name="ffn_bwd", grid=(NFT, NFF),
        out_shape=[jax.ShapeDtypeStruct((NTOK, D), BF16), jax.ShapeDtypeStruct((NTOK, DFF), BF16),
                   jax.ShapeDtypeStruct((2, NTOK, DFF), BF16), jax.ShapeDtypeStruct((NTOK, D), F32),
                   jax.ShapeDtypeStruct((NB, 1, D), F32), jax.ShapeDtypeStruct((NB, 1, D), F32),
                   jax.ShapeDtypeStruct((1, D), F32)],
        in_specs=[row(D), ff, ff, row(D), ex, ex, ex, one, wg, wu, wo],
        out_specs=[row(D), ff, pl.BlockSpec((2, FT, FFT), lambda i, f: (0, i, f)), row(D), ex, ex, one],
        scratch_shapes=[pltpu.VMEM((FT, D), F32)],
        compiler_params=_cp(vmem=VMEM_BIG),
    )(dy, gate, up, x1, sh2, sc2, g2, n2w, w_in4, w_in4, w_out)


def _local_step(x, ctx, target, mod_x, mod_c, norm1_w, norm2_w, w_in_p, conv_w, a_log, dt_bias, out_norm_w,
                q_norm_w, k_norm_w, rpb, w_out_b, w_ffn_in4, w_ffn_out_b):
    sh1, sc1, g1, sh2, sc2, g2 = [mod_x[:, i * D:(i + 1) * D].reshape(NB, 1, D) for i in range(6)]
    csh1, csc1 = mod_c[0:D].reshape(1, 1, D), mod_c[D:2 * D].reshape(1, 1, D)
    cw = jnp.concatenate([conv_w, jnp.zeros((3, 1536), F32)], axis=0)
    prm = jnp.zeros((8, 128), F32).at[0, :8].set(a_log.reshape(8)).at[1, :8].set(dt_bias.reshape(8))
    onw = jnp.tile(out_norm_w.reshape(1, 128), (1, DNH))
    qw = jnp.tile(q_norm_w.reshape(1, 64), (1, NAH))
    kw = jnp.tile(k_norm_w.reshape(1, 64), (1, NAH))
    seg = np.arange(NAW) // 64
    bd = jnp.asarray((seg[:, None] == seg[None, :]).astype(np.float32) / 64.0, dtype=BF16)
    bias = _bias_table(rpb)

    p, hm = _inproj_fwd(x, ctx, sh1, sc1, csh1, csc1, norm1_w, w_in_p)
    dq_, dk_, dv_ = _dn_prep_fwd(p, cw)
    gates = _gates_fwd(p, prm)
    o_f, st_f, o_b, st_b = _scan_fwd(dq_, dk_, dv_, gates)
    qn, kn, vb = _na_prep_fwd(p, qw, kw, bd)
    na = _na_fwd(qn, kn, vb, bias)
    x1, ai, ao = _outproj_fwd(o_f, o_b, p, na, onw, x, g1, w_out_b)
    x1f = x1.reshape(NTOK, D)
    h2, gate, up, dy, dg2, loss8 = _ffn_fwd(x1f, sh2, sc2, g2, norm2_w, w_ffn_in4, w_ffn_out_b,
                                             target.reshape(NTOK, D))
    dff, act, dgu, dx1, dsh2, dsc2, dn2w = _ffn_bwd(dy, gate, up, x1f, sh2, sc2, g2, norm2_w, w_ffn_in4, w_ffn_out_b)
    g_w_ffn_out = _matmul_tn(act, dff, FFT, D, 512, "wgrad_ffn_out")
    g_w_ffn_in4 = _matmul_tn(
        h2, dgu, 512, FFT, 512, "wgrad_ffn_in", nn=2 * NFF,
        b_block=(None, 512, FFT), b_map=lambda i, j, k: (j // NFF, k, j % NFF),
        out_shape=(4, D, FFT), out_block=(None, 512, FFT), out_map=lambda i, j, k: (j, i, 0))
    dao, do, dz, dna, dg1, donw = _outproj_bwd(dx1.reshape(NB, SEQ, D), ao, g1, w_out_b, o_f, o_b, p, onw)
    g_w_out = _matmul_tn(ai.reshape(NTOK, D), dao.reshape(NTOK, D), 512, D, 512, "wgrad_out")
    dqn, dkn, dvn, dbias = _na_bwd(qn, kn, vb, bias, dna)
    dp_na, dqw, dkw = _na_prep_bwd(p, qw, kw, bd, dqn, dkn, dvn)
    dqf, dkf, dvf, dgf, dqb, dkb, dvb, dgb = _scan_bwd(dq_, dk_, dv_, gates, st_f, st_b, do)
    dpq, dpk, dpv, gcq, gck, gcv = _dn_prep_bwd(p, cw, (dqf, dqb), (dkf, dkb), (dvf, dvb))
    dgt, dprm = _gates_bwd(p, prm, dgf, dgb)
    dp_dn = jnp.concatenate([dpq, dpk, dpv], axis=2)
    dpb, grad_x, dsh1, dsc1, dcsh1, dcsc1, dn1w = _inproj_bwd(
        dp_dn, dz, dp_na, dgt, w_in_p, x, ctx, sh1, sc1, csh1, csc1, norm1_w, dx1.reshape(NB, SEQ, D))
    g_w_in_p = _matmul_tn(hm.reshape(NB * T, D), dpb.reshape(NB * T, PC), 256, PC, 512, "wgrad_in")
    g_rpb = _rpb_fold(dbias)

    dmod_x = jnp.concatenate([dsh1, dsc1, dg1, dsh2, dsc2, dg2], axis=2).reshape(NB, 6 * D)
    dmod_c = jnp.concatenate([dcsh1.reshape(D), dcsc1.reshape(D), jnp.zeros((4 * D,), F32)])
    g_conv = jnp.concatenate([gcq[:5], gck[:5], gcv[:5]], axis=1)
    small = dict(
        norm1_w=dn1w, norm2_w=dn2w, a_log=dprm[0, :8], dt_bias=dprm[1, :8],
        out_norm_w=donw[0].reshape(DNH, 128).sum(0), q_norm_w=dqw[0].reshape(NAH, 64).sum(0),
        k_norm_w=dkw[0].reshape(NAH, 64).sum(0), rpb=g_rpb, conv_w=g_conv)
    return loss8[0, 0], grad_x, dmod_x, dmod_c, small, g_w_in_p, g_w_out, g_w_ffn_in4, g_w_ffn_out


def _perm_w_in(w):
    return jnp.concatenate([w[:, :2048], w[:, 2064:3600], w[:, 2048:2064],
                            jnp.zeros((w.shape[0], PC - IN_COLS), w.dtype)], axis=1)


def _unperm_w_in(g):
    return jnp.concatenate([g[:, :2048], g[:, COL_G:COL_G + 16], g[:, 2048:COL_G]], axis=1)


def _pack_small(c_ctx, n1, n2, b_ada, a_log, dt_bias, onw, qnw, knw, rpb, conv_s):
    misc = jnp.zeros((D,), F32)
    misc = misc.at[0:8].set(a_log.reshape(8)).at[8:16].set(dt_bias.reshape(8))
    misc = misc.at[128:256].set(onw.reshape(128)).at[256:320].set(qnw.reshape(64)).at[320:384].set(knw.reshape(64))
    rp = jnp.concatenate([rpb.reshape(-1), jnp.zeros((4 * D - 3720,), F32)])
    cv = jnp.concatenate([conv_s.reshape(-1), jnp.zeros((2 * D - 1920,), F32)])
    return jnp.concatenate([c_ctx.reshape(-1), n1.reshape(-1), n2.reshape(-1), b_ada.reshape(-1), misc, rp, cv]).reshape(16, D)


def _unpack_small(a):
    f = a.reshape(-1)
    misc = f[9 * D:10 * D]
    return dict(
        c_ctx=f[0:D], norm1_w=f[D:2 * D].reshape(1, D), norm2_w=f[2 * D:3 * D].reshape(1, D),
        b_ada=f[3 * D:9 * D].reshape(1, 6 * D), dn_A_log=misc[0:8].reshape(1, 2, 4), dn_dt_bias=misc[8:16].reshape(1, 2, 4),
        dn_out_norm_w=misc[128:256].reshape(1, 128), na_q_norm_w=misc[256:320].reshape(1, 64),
        na_k_norm_w=misc[320:384].reshape(1, 64), na_rpb=f[10 * D:10 * D + 3720].reshape(1, 8, 15, 31),
        dn_conv_w=f[14 * D:14 * D + 1920].reshape(1, 5, 384))


_WEIGHTS = ['c_ctx', 'norm1_w', 'norm2_w', 'w_ada', 'b_ada', 'w_in', 'dn_conv_w', 'dn_A_log', 'dn_dt_bias',
            'dn_out_norm_w', 'na_q_norm_w', 'na_k_norm_w', 'na_rpb', 'w_out', 'w_ffn_in', 'w_ffn_out']


def kernel(x, c, ctx, c_ctx, norm1_w, norm2_w, w_ada, b_ada, w_in, dn_conv_w, dn_A_log, dn_dt_bias, dn_out_norm_w, na_q_norm_w, na_k_norm_w, na_rpb, w_out, w_ffn_in, w_ffn_out, loss_target, m_c_ctx, m_norm1_w, m_norm2_w, m_w_ada, m_b_ada, m_w_in, m_dn_conv_w, m_dn_A_log, m_dn_dt_bias, m_dn_out_norm_w, m_na_q_norm_w, m_na_k_norm_w, m_na_rpb, m_w_out, m_w_ffn_in, m_w_ffn_out, v_c_ctx, v_norm1_w, v_norm2_w, v_w_ada, v_b_ada, v_w_in, v_dn_conv_w, v_dn_A_log, v_dn_dt_bias, v_dn_out_norm_w, v_na_q_norm_w, v_na_k_norm_w, v_na_rpb, v_w_out, v_w_ffn_in, v_w_ffn_out):
    ix, iy, ic = lax.axis_index("x"), lax.axis_index("y"), lax.axis_index("c")
    chip = 2 * ix + iy
    me = 4 * ix + 2 * iy + ic

    crow = jnp.concatenate([c, c_ctx.reshape(1, D), jnp.zeros((5, D), F32)], axis=0)
    cvrow = jnp.concatenate([dn_conv_w[0].reshape(-1), jnp.zeros((2 * D - 1920,), F32)]).reshape(2, D)
    g0 = _allgather8(jnp.concatenate([crow, cvrow, jnp.zeros((6, D), F32)], axis=0), "gather_cond")
    cin = g0[:, :8].reshape(64, D)
    conv_full = jnp.concatenate([g0[2 * s, 8:10].reshape(-1)[:1920].reshape(5, 384) for s in range(4)], axis=1)

    w_ada_s = w_ada[0]
    b_s = lax.dynamic_slice(b_ada, (0, chip * 1536), (1, 1536))
    mod_s = _ada_fwd(cin, w_ada_s, b_s)
    mods = _allgather8(mod_s, "gather_mod")
    mod_all = jnp.concatenate([mods[0], mods[2], mods[4], mods[6]], axis=1)
    mod_mine = lax.dynamic_slice(mod_all, (8 * me, 0), (8, 6 * D))
    mod_x, mod_c = mod_mine[0:NB], mod_mine[2]

    def half(w2d):
        r = w2d.shape[0] // 2
        return lax.dynamic_slice(w2d, (ic * r, 0), (r, w2d.shape[1]))

    halves = [_cast_bf16(half(w_in[0]), "cast_w_in"), _cast_bf16(half(w_out[0]), "cast_w_out"),
              _cast_bf16(half(w_ffn_in[0]), "cast_w_ffn_in"), _cast_bf16(half(w_ffn_out[0]), "cast_w_ffn_out")]
    gw_in, gw_out, gw_ffn_in, gw_ffn_out = _gather_weights(halves)
    w_in_full = gw_in.reshape(4, D, 900).transpose(1, 0, 2).reshape(D, IN_COLS)
    w_in_p = _perm_w_in(w_in_full)
    w_out_b = gw_out.reshape(D, D)
    w_ffn_in4 = gw_ffn_in.reshape(4, D, FFT)
    w_ffn_out_b = gw_ffn_out.reshape(DFF, D)

    (loss_part, grad_x, dmod_x, dmod_c, small, g_w_in_p, g_w_out, g_w_ffn_in4, g_w_ffn_out) = _local_step(
        x, ctx, loss_target, mod_x, mod_c, norm1_w, norm2_w, w_in_p, conv_full, dn_A_log[0], dn_dt_bias[0],
        dn_out_norm_w[0], na_q_norm_w[0], na_k_norm_w[0], na_rpb[0], w_out_b, w_ffn_in4, w_ffn_out_b)
    loss = lax.psum(loss_part, ("x", "y", "c"))

    dm = jnp.concatenate([dmod_x, dmod_c.reshape(1, 6 * D), jnp.zeros((5, 6 * D), F32)], axis=0).reshape(48, D)
    misc = jnp.zeros((D,), F32)
    misc = misc.at[0:8].set(small["a_log"]).at[8:16].set(small["dt_bias"]).at[128:256].set(small["out_norm_w"])
    misc = misc.at[256:320].set(small["q_norm_w"]).at[320:384].set(small["k_norm_w"])
    sm = jnp.concatenate([
        small["norm1_w"].reshape(-1), small["norm2_w"].reshape(-1), misc,
        small["rpb"].reshape(-1), jnp.zeros((4 * D - 3720,), F32),
        small["conv_w"].reshape(-1), jnp.zeros((8 * D - 7680,), F32), jnp.zeros((D,), F32)]).reshape(16, D)
    gsl, gtot = _allgather8(jnp.concatenate([dm, sm], axis=0), "gather_small", with_sum=True)
    gmod = gsl[:, :48].reshape(64, 6 * D)
    st = gtot[48:64].reshape(-1)
    g_n1, g_n2, g_misc = st[0:D].reshape(1, D), st[D:2 * D].reshape(1, D), st[2 * D:3 * D]
    g_rpb = st[3 * D:3 * D + 3720].reshape(1, 8, 15, 31)
    g_conv_full = st[7 * D:7 * D + 7680].reshape(5, 1536)
    g_conv_s = lax.dynamic_slice(g_conv_full, (0, chip * 384), (5, 384)).reshape(1, 5, 384)

    gmod_s = lax.dynamic_slice(gmod, (0, chip * 1536), (64, 1536))
    g_w_ada_s, pctx = _ada_bwd(cin, gmod_s, w_ada_s)
    parts = _allgather8(pctx, "gather_cctx")
    g_b_ada, g_c_ctx = _ada_small(gmod, parts, c_ctx.reshape(1, D))

    g_w_in_full = _unperm_w_in(g_w_in_p)
    g_w_in8 = g_w_in_full.reshape(D, 4, 900).transpose(1, 0, 2).reshape(8, D // 2, 900)
    rs = _scatter_grads([g_w_in8, g_w_out.reshape(8, D // 8, D), g_w_ffn_in4.reshape(8, D // 2, FFT),
                         g_w_ffn_out.reshape(8, DFF // 8, D)])
    sums = [_sum_slots(r, "sum_" + nm) for r, nm in zip(rs, ["w_in", "w_out", "w_ffn_in", "w_ffn_out"])]
    full = _sibling_exchange(sums)
    g_w_in_s = full[0].reshape(1, D, 900)
    g_w_out_s = full[1].reshape(1, D // 4, D)
    g_w_ffn_in_s = full[2].reshape(1, D, FFT)
    g_w_ffn_out_s = full[3].reshape(1, DFF // 4, D)

    grads = dict(
        c_ctx=g_c_ctx.reshape(D), norm1_w=g_n1, norm2_w=g_n2, w_ada=g_w_ada_s.reshape(1, D, 1536), b_ada=g_b_ada,
        w_in=g_w_in_s, dn_conv_w=g_conv_s, dn_A_log=g_misc[0:8].reshape(1, 2, 4), dn_dt_bias=g_misc[8:16].reshape(1, 2, 4),
        dn_out_norm_w=g_misc[128:256].reshape(1, 128), na_q_norm_w=g_misc[256:320].reshape(1, 64),
        na_k_norm_w=g_misc[320:384].reshape(1, 64), na_rpb=g_rpb, w_out=g_w_out_s, w_ffn_in=g_w_ffn_in_s,
        w_ffn_out=g_w_ffn_out_s)
    pk = lambda d_: _pack_small(d_["c_ctx"], d_["norm1_w"], d_["norm2_w"], d_["b_ada"], d_["dn_A_log"], d_["dn_dt_bias"],
                                d_["dn_out_norm_w"], d_["na_q_norm_w"], d_["na_k_norm_w"], d_["na_rpb"], d_["dn_conv_w"])
    wts = dict(c_ctx=c_ctx, norm1_w=norm1_w, norm2_w=norm2_w, b_ada=b_ada, dn_A_log=dn_A_log, dn_dt_bias=dn_dt_bias,
               dn_out_norm_w=dn_out_norm_w, na_q_norm_w=na_q_norm_w, na_k_norm_w=na_k_norm_w, na_rpb=na_rpb, dn_conv_w=dn_conv_w)
    ms = dict(c_ctx=m_c_ctx, norm1_w=m_norm1_w, norm2_w=m_norm2_w, b_ada=m_b_ada, dn_A_log=m_dn_A_log, dn_dt_bias=m_dn_dt_bias,
              dn_out_norm_w=m_dn_out_norm_w, na_q_norm_w=m_na_q_norm_w, na_k_norm_w=m_na_k_norm_w, na_rpb=m_na_rpb,
              dn_conv_w=m_dn_conv_w)
    vs = dict(c_ctx=v_c_ctx, norm1_w=v_norm1_w, norm2_w=v_norm2_w, b_ada=v_b_ada, dn_A_log=v_dn_A_log, dn_dt_bias=v_dn_dt_bias,
              dn_out_norm_w=v_dn_out_norm_w, na_q_norm_w=v_na_q_norm_w, na_k_norm_w=v_na_k_norm_w, na_rpb=v_na_rpb,
              dn_conv_w=v_dn_conv_w)
    sd, snm, snv = _adamw(pk(wts), pk(grads), pk(ms), pk(vs), "adamw_small")
    delta, new_m, new_v = _unpack_small(sd), _unpack_small(snm), _unpack_small(snv)
    big = dict(w_ada=(w_ada, m_w_ada, v_w_ada), w_in=(w_in, m_w_in, v_w_in), w_out=(w_out, m_w_out, v_w_out),
               w_ffn_in=(w_ffn_in, m_w_ffn_in, v_w_ffn_in), w_ffn_out=(w_ffn_out, m_w_ffn_out, v_w_ffn_out))
    for nm, (w_, m_, v_) in big.items():
        shp = w_.shape
        r2 = lambda a: a.reshape(shp[1], shp[2])
        d_, nm_, nv_ = _adamw(r2(w_), r2(grads[nm]), r2(m_), r2(v_), "adamw_" + nm)
        delta[nm], new_m[nm], new_v[nm] = d_.reshape(shp), nm_.reshape(shp), nv_.reshape(shp)

    return (loss, grad_x, *[grads[n] for n in _WEIGHTS], *[delta[n] for n in _WEIGHTS],
            *[new_m[n] for n in _WEIGHTS], *[new_v[n] for n in _WEIGHTS])
```

```python
import functools

import numpy as np
import jax
import jax.numpy as jnp
from jax import lax
from jax.experimental import pallas as pl
from jax.experimental.pallas import tpu as pltpu

F32 = jnp.float32
BF16 = jnp.bfloat16

D = 1024
NB = 2
SEQ = 2048
CT = 256
T = CT + SEQ
NTOK = NB * SEQ
CH = 64
NCH = T // CH
NCH_CTX = CT // CH
DNH, DNW = 4, 512
NAH, NAW = 8, 512
GRID_W = 64
ROWS = SEQ // GRID_W
WIN_ROWS = 8
WIN_COLS = 16
NWIN = WIN_ROWS * GRID_W
DFF = 2816
FFT = 1408
IN_COLS = 3600
PC = 3712
COL_Z, COL_NA, COL_G = 1536, 2048, 3584
EPS = 1e-6
NEG = -1e30
RT = 256
VMEM_BIG = 56 * 1024 * 1024
MESH = pl.DeviceIdType.MESH

ADAM_LR, ADAM_B1, ADAM_B2, ADAM_EPS, ADAM_WD, ADAM_STEP = 0.001, 0.9, 0.999, 1e-08, 0.01, 10


def _cp(vmem=None, sem=None):
    kw = {}
    if vmem is not None:
        kw["vmem_limit_bytes"] = vmem
    if sem is not None:
        kw["dimension_semantics"] = sem
    return pltpu.CompilerParams(**kw)


_NN = (((1,), (0,)), ((), ()))
_NT = (((1,), (1,)), ((), ()))
_TN = (((0,), (0,)), ((), ()))


def _dg(a, b, dims):
    return lax.dot_general(a.astype(BF16), b.astype(BF16), dims, preferred_element_type=F32)


@jax.custom_vjp
def _mm(a, b):
    return _dg(a, b, _NN)


_mm.defvjp(lambda a, b: (_dg(a, b, _NN), (a, b)),
           lambda r, g: (_dg(g, r[1], _NT), _dg(r[0], g, _TN)))


@jax.custom_vjp
def _mm_nt(a, b):
    return _dg(a, b, _NT)


_mm_nt.defvjp(lambda a, b: (_dg(a, b, _NT), (a, b)),
              lambda r, g: (_dg(g, r[1], _NN), _dg(g, r[0], _TN)))


@jax.custom_vjp
def _mm_tn(a, b):
    return _dg(a, b, _TN)


_mm_tn.defvjp(lambda a, b: (_dg(a, b, _TN), (a, b)),
              lambda r, g: (_dg(r[1], g, _NT), _dg(r[0], g, _NN)))


def _split_dot(a, b, dims):
    hi = b.astype(BF16)
    lo = (b - hi.astype(F32)).astype(BF16)
    ab = a.astype(BF16)
    return (lax.dot_general(ab, hi, dims, preferred_element_type=F32)
            + lax.dot_general(ab, lo, dims, preferred_element_type=F32))


@jax.custom_vjp
def _mask_mm(mask, b):
    return _split_dot(mask, b, _NN)


_mask_mm.defvjp(lambda mask, b: (_split_dot(mask, b, _NN), mask),
                lambda mask, g: (jnp.zeros_like(mask), _split_dot(mask, g, _TN)))


def _hmm(a, b):
    return jnp.dot(a, b, precision=lax.Precision.HIGHEST, preferred_element_type=F32)


def _sigmoid(x):
    return 1.0 / (1.0 + jnp.exp(-x))


def _silu(x):
    return x * _sigmoid(x)


def _norm_mod(x, w, sh, sc):
    r = lax.rsqrt(jnp.mean(x * x, axis=-1, keepdims=True) + EPS)
    return (x * r * w) * (1.0 + sc) + sh


def _my_pos():
    return lax.axis_index("x"), lax.axis_index("y"), lax.axis_index("c")


def _flip(pos, k):
    x, y, c = pos
    return (1 - x if k & 4 else x, 1 - y if k & 2 else y, 1 - c if k & 1 else c)


def _slot(pos):
    return 4 * pos[0] + 2 * pos[1] + pos[2]


def _allgather8(v, name, with_sum=False):
    R, W = v.shape

    def body(v_ref, out_ref, *rest):
        if with_sum:
            tot_ref, send_sems, recv_sems, lsem = rest
        else:
            send_sems, recv_sems, lsem = rest
        me = _my_pos()
        mine = pltpu.make_async_copy(v_ref, out_ref.at[_slot(me)], lsem)
        mine.start()
        sends = []
        for k in range(1, 8):
            cp = pltpu.make_async_remote_copy(
                src_ref=v_ref, dst_ref=out_ref.at[_slot(me)], send_sem=send_sems.at[k - 1],
                recv_sem=recv_sems.at[k - 1], device_id=_flip(me, k), device_id_type=MESH)
            cp.start()
            sends.append(cp)
        for k in range(1, 8):
            peer = _flip(me, k)
            pltpu.make_async_remote_copy(
                src_ref=v_ref, dst_ref=out_ref.at[_slot(peer)], send_sem=send_sems.at[k - 1],
                recv_sem=recv_sems.at[k - 1], device_id=peer, device_id_type=MESH).wait_recv()
        for cp in sends:
            cp.wait_send()
        mine.wait()
        if with_sum:
            acc = out_ref[0]
            for s in range(1, 8):
                acc = acc + out_ref[s]
            tot_ref[...] = acc

    out_shape = [jax.ShapeDtypeStruct((8, R, W), F32)]
    out_specs = [pl.BlockSpec(memory_space=pltpu.VMEM)]
    if with_sum:
        out_shape.append(jax.ShapeDtypeStruct((R, W), F32))
        out_specs.append(pl.BlockSpec(memory_space=pltpu.VMEM))
    res = pl.pallas_call(
        body, name=name, out_shape=out_shape,
        in_specs=[pl.BlockSpec(memory_space=pltpu.VMEM)], out_specs=out_specs,
        scratch_shapes=[pltpu.SemaphoreType.DMA((7,)), pltpu.SemaphoreType.DMA((7,)), pltpu.SemaphoreType.DMA],
        compiler_params=_cp(vmem=VMEM_BIG),
    )(v)
    return res if with_sum else res[0]


def _gather_weights(halves):
    n = len(halves)

    def body(*refs):
        ins, outs = refs[:n], refs[n:2 * n]
        send_sems, recv_sems, lsems = refs[2 * n:]
        me = _my_pos()
        x, y, c = me
        sib = (x, y, 1 - c)
        chips = [4, 2, 6]

        def cp(a, k, block, to, src=None):
            dst = outs[a].at[_slot(block)]
            return pltpu.make_async_remote_copy(
                src_ref=dst if src is None else src, dst_ref=dst, send_sem=send_sems.at[a, k],
                recv_sem=recv_sems.at[a, k], device_id=to, device_id_type=MESH)

        mine, first, passed = [], [], []
        for a in range(n):
            m = pltpu.make_async_copy(ins[a], outs[a].at[_slot(me)], lsems.at[a])
            m.start()
            mine.append(m)
            first.append(cp(a, 0, me, sib, src=ins[a]))
            for j, k in enumerate(chips):
                first.append(cp(a, 1 + j, me, _flip(me, k), src=ins[a]))
        for f in first:
            f.start()
        for j, k in enumerate(chips):
            for a in range(n):
                cp(a, 1 + j, _flip(me, k), me).wait_recv()
                p = cp(a, 4 + j, _flip(me, k), sib)
                p.start()
                passed.append(p)
        for a in range(n):
            cp(a, 0, sib, me).wait_recv()
            for j, k in enumerate(chips):
                cp(a, 4 + j, _flip(sib, k), me).wait_recv()
        for f in first + passed:
            f.wait_send()
        for m in mine:
            m.wait()

    hbm = pl.BlockSpec(memory_space=pltpu.HBM)
    return pl.pallas_call(
        body, name="gather_weights",
        out_shape=[jax.ShapeDtypeStruct((8,) + h.shape, h.dtype) for h in halves],
        in_specs=[hbm] * n, out_specs=[hbm] * n,
        scratch_shapes=[pltpu.SemaphoreType.DMA((n, 7)), pltpu.SemaphoreType.DMA((n, 7)),
                        pltpu.SemaphoreType.DMA((n,))],
    )(*halves)


def _exchange(name, ins, out_shapes, n_local, n_xfer, plan):
    n_in, n_out = len(ins), len(out_shapes)

    def body(*refs):
        i_refs, o_refs = refs[:n_in], refs[n_in:n_in + n_out]
        send_sems, recv_sems, lsems = refs[n_in + n_out:]
        locals_, xfers = plan(i_refs, o_refs, _my_pos())
        assert len(locals_) == n_local and len(xfers) == n_xfer
        mine = [pltpu.make_async_copy(src, dst, lsems.at[i]) for i, (src, dst) in enumerate(locals_)]
        for m in mine:
            m.start()
        sends = [pltpu.make_async_remote_copy(src_ref=src, dst_ref=dst, send_sem=send_sems.at[i], recv_sem=recv_sems.at[i],
                                              device_id=peer, device_id_type=MESH)
                 for i, (src, dst, _, peer) in enumerate(xfers)]
        for cp in sends:
            cp.start()
        for i, (src, _, landing, peer) in enumerate(xfers):
            pltpu.make_async_remote_copy(src_ref=src, dst_ref=landing, send_sem=send_sems.at[i], recv_sem=recv_sems.at[i],
                                         device_id=peer, device_id_type=MESH).wait_recv()
        for cp in sends:
            cp.wait_send()
        for m in mine:
            m.wait()

    hbm = pl.BlockSpec(memory_space=pltpu.HBM)
    return pl.pallas_call(
        body, name=name, out_shape=out_shapes, in_specs=[hbm] * n_in, out_specs=[hbm] * n_out,
        scratch_shapes=[pltpu.SemaphoreType.DMA((n_xfer,)), pltpu.SemaphoreType.DMA((n_xfer,)),
                        pltpu.SemaphoreType.DMA((n_local,))],
    )(*ins)


def _row_chunks(ref, n):
    rows = ref.shape[0] // n
    return [ref.at[pl.ds(i * rows, rows)] for i in range(n)]


def _presum_swap(grads):
    n, nck = len(grads), 2

    def plan(ins, outs, pos):
        x, y, c = pos
        sib = (x, y, 1 - c)
        locals_, xfers = [], []
        for a in range(n):
            for s in range(4):
                locals_.append((ins[a].at[2 * s + c], outs[n + a].at[s]))
                land = _row_chunks(outs[a].at[s], nck)
                for src, dst in zip(_row_chunks(ins[a].at[2 * s + 1 - c], nck), land):
                    xfers.append((src, dst, dst, sib))
        return locals_, xfers

    sh = [jax.ShapeDtypeStruct((4,) + g.shape[1:], F32) for g in grads]
    res = _exchange("presum_swap", grads, sh + sh, 4 * n, 4 * n * nck, plan)
    return res[:n], res[n:]


def _chip_scatter(parts):
    n, nck = len(parts), 2

    def plan(ins, outs, pos):
        x, y, c = pos
        mychip = 2 * x + y
        locals_, xfers = [], []
        for a in range(n):
            locals_.append((ins[a].at[mychip], outs[a].at[mychip]))
            for k in (4, 2, 6):
                peer = _flip(pos, k)
                pchip = 2 * peer[0] + peer[1]
                for src, dst, land in zip(_row_chunks(ins[a].at[pchip], nck), _row_chunks(outs[a].at[mychip], nck),
                                          _row_chunks(outs[a].at[pchip], nck)):
                    xfers.append((src, dst, land, peer))
        return locals_, xfers

    sh = [jax.ShapeDtypeStruct(p.shape, p.dtype) for p in parts]
    return _exchange("chip_scatter", parts, sh, n, 3 * n * nck, plan)


def _sibling_exchange(halves):
    n = len(halves)
    nck = [max(k for k in (1, 2, 4, 8) if h.shape[0] % (8 * k) == 0) for h in halves]

    def plan(ins, outs, pos):
        x, y, c = pos
        sib = (x, y, 1 - c)
        locals_, xfers = [], []
        for a in range(n):
            locals_.append((ins[a], outs[a].at[c]))
            for src, dst, land in zip(_row_chunks(ins[a], nck[a]), _row_chunks(outs[a].at[c], nck[a]),
                                      _row_chunks(outs[a].at[1 - c], nck[a])):
                xfers.append((src, dst, land, sib))
        return locals_, xfers

    sh = [jax.ShapeDtypeStruct((2,) + h.shape, h.dtype) for h in halves]
    return _exchange("sibling_exchange", halves, sh, n, sum(nck), plan)


def _row_tile(r, cap=512):
    for t in (cap, 256, 128, 64, 32, 16, 8):
        if t <= cap and r % t == 0:
            return t
    return r


def _cast_bf16(a, name):
    R, W = a.shape
    tr = _row_tile(R, 256)

    def body(a_ref, o_ref):
        o_ref[...] = a_ref[...].astype(BF16)

    return pl.pallas_call(
        body, name=name, grid=(R // tr,), out_shape=jax.ShapeDtypeStruct((R, W), BF16),
        in_specs=[pl.BlockSpec((tr, W), lambda i: (i, 0))], out_specs=pl.BlockSpec((tr, W), lambda i: (i, 0)),
    )(a)


def _add_cast(a, b, name):
    n, R, W = a.shape
    tr = _row_tile(R, 128)

    def body(a_ref, b_ref, o_ref):
        o_ref[...] = (a_ref[...] + b_ref[...]).astype(BF16)

    spec = pl.BlockSpec((n, tr, W), lambda i: (0, i, 0))
    return pl.pallas_call(
        body, name=name, grid=(R // tr,), out_shape=jax.ShapeDtypeStruct((n, R, W), BF16),
        in_specs=[spec, spec], out_specs=spec,
    )(a, b)


def _sum_chips(q, name):
    _, R, W = q.shape
    tr = _row_tile(R, 128)

    def body(q_ref, o_ref):
        acc = q_ref[0].astype(F32)
        for s_ in range(1, 4):
            acc = acc + q_ref[s_].astype(F32)
        o_ref[...] = acc

    return pl.pallas_call(
        body, name=name, grid=(R // tr,), out_shape=jax.ShapeDtypeStruct((R, W), F32),
        in_specs=[pl.BlockSpec((4, tr, W), lambda i: (0, i, 0))], out_specs=pl.BlockSpec((tr, W), lambda i: (i, 0)),
    )(q)


def _adamw(w, g, m, v, name):
    R, W = w.shape
    tr = _row_tile(R, 256)
    c1 = 1.0 / (1.0 - ADAM_B1 ** ADAM_STEP)
    c2 = 1.0 / (1.0 - ADAM_B2 ** ADAM_STEP)

    def body(w_ref, g_ref, m_ref, v_ref, d_ref, nm_ref, nv_ref):
        gg = g_ref[...]
        nm = ADAM_B1 * m_ref[...] + (1.0 - ADAM_B1) * gg
        nv = ADAM_B2 * v_ref[...] + (1.0 - ADAM_B2) * (gg * gg)
        d_ref[...] = -ADAM_LR * ((nm * c1) / (jnp.sqrt(nv * c2) + ADAM_EPS) + ADAM_WD * w_ref[...])
        nm_ref[...] = nm
        nv_ref[...] = nv

    spec = pl.BlockSpec((tr, W), lambda i: (i, 0))
    return pl.pallas_call(
        body, name=name, grid=(R // tr,), out_shape=[jax.ShapeDtypeStruct((R, W), F32)] * 3,
        in_specs=[spec] * 4, out_specs=[spec] * 3,
    )(w, g, m, v)


def _matmul_tn(a, b, tm, tn, tk, name, nn=1, b_map=None, out_shape=None, out_block=None, out_map=None,
               b_block=None):
    K, M = a.shape
    nk = K // tk
    nm = M // tm

    def body(a_ref, b_ref, o_ref, acc_ref):
        kk = pl.program_id(2)

        @pl.when(kk == 0)
        def _():
            acc_ref[...] = jnp.zeros_like(acc_ref)

        acc_ref[...] += lax.dot_general(a_ref[...], b_ref[...], _TN, preferred_element_type=F32)

        @pl.when(kk == nk - 1)
        def _():
            o_ref[...] = acc_ref[...]

    return pl.pallas_call(
        body, name=name, grid=(nm, nn, nk),
        out_shape=jax.ShapeDtypeStruct(out_shape or (M, b.shape[-1]), F32),
        in_specs=[pl.BlockSpec((tk, tm), lambda i, j, k: (k, i)),
                  pl.BlockSpec(b_block or (tk, tn), b_map or (lambda i, j, k: (k, j)))],
        out_specs=pl.BlockSpec(out_block or (tm, tn), out_map or (lambda i, j, k: (i, j))),
        scratch_shapes=[pltpu.VMEM((tm, tn), F32)],
        compiler_params=_cp(vmem=VMEM_BIG),
    )(a, b)


def _ada_fwd(cin, w_ada_s, b_s):
    tn = 512

    def body(c_ref, w_ref, b_ref, o_ref):
        o_ref[...] = _dg(_silu(c_ref[...]), w_ref[...], _NN) + b_ref[...]

    return pl.pallas_call(
        body, name="ada_fwd", grid=(1536 // tn,), out_shape=jax.ShapeDtypeStruct((64, 1536), F32),
        in_specs=[pl.BlockSpec((64, D), lambda j: (0, 0)), pl.BlockSpec((D, tn), lambda j: (0, j)),
                  pl.BlockSpec((1, tn), lambda j: (0, j))],
        out_specs=pl.BlockSpec((64, tn), lambda j: (0, j)),
        compiler_params=_cp(vmem=VMEM_BIG),
    )(cin, w_ada_s, b_s)


def _ada_bwd(cin, gmod_s, w_ada_s):
    tn = 512

    def body(c_ref, g_ref, w_ref, gw_ref, pc_ref):
        j = pl.program_id(0)
        g = g_ref[...]
        gw_ref[...] = _dg(_silu(c_ref[...]), g, _TN)
        rr = lax.broadcasted_iota(jnp.int32, (8, 64), 0)
        cc = lax.broadcasted_iota(jnp.int32, (8, 64), 1)
        sel = jnp.where((rr == 0) & ((cc & 7) == 2), 1.0, 0.0).astype(F32)
        dctx = _hmm(sel, g)
        part = _dg(dctx, w_ref[...], _NT)

        @pl.when(j == 0)
        def _():
            pc_ref[...] = jnp.zeros_like(pc_ref)

        pc_ref[...] += part

    return pl.pallas_call(
        body, name="ada_bwd", grid=(1536 // tn,),
        out_shape=[jax.ShapeDtypeStruct((D, 1536), F32), jax.ShapeDtypeStruct((8, D), F32)],
        in_specs=[pl.BlockSpec((64, D), lambda j: (0, 0)), pl.BlockSpec((64, tn), lambda j: (0, j)),
                  pl.BlockSpec((D, tn), lambda j: (0, j))],
        out_specs=[pl.BlockSpec((D, tn), lambda j: (0, j)), pl.BlockSpec((8, D), lambda j: (0, 0))],
        compiler_params=_cp(vmem=VMEM_BIG),
    )(cin, gmod_s, w_ada_s)


def _ada_small(gmod, parts, c_ctx):
    def body(g_ref, p_ref, c_ref, gb_ref, gc_ref):
        gb_ref[...] = jnp.sum(g_ref[...], axis=0, keepdims=True)
        ds = p_ref[0][0:1] + p_ref[2][0:1] + p_ref[4][0:1] + p_ref[6][0:1]
        cc = c_ref[...]
        s = _sigmoid(cc)
        gc_ref[...] = ds * (s * (1.0 + cc * (1.0 - s)))

    vm = pl.BlockSpec(memory_space=pltpu.VMEM)
    return pl.pallas_call(
        body, name="ada_small",
        out_shape=[jax.ShapeDtypeStruct((1, 6 * D), F32), jax.ShapeDtypeStruct((1, D), F32)],
        in_specs=[vm, vm, vm], out_specs=[vm, vm],
    )(gmod, parts, c_ctx)


def _tok_specs():
    xs = pl.BlockSpec((None, RT, D), lambda b, j: (b, jnp.maximum(j - 1, 0), 0))
    cs = pl.BlockSpec((None, RT, D), lambda b, j: (b, 0, 0))
    ms = pl.BlockSpec((None, 1, D), lambda b, j: (b, 0, 0))
    cms = pl.BlockSpec((None, 1, D), lambda b, j: (0, 0, 0))
    return xs, cs, ms, cms


def _inproj_fwd(x, ctx, sh1, sc1, csh1, csc1, n1w, w_in_p):
    def body(x_ref, c_ref, sh_ref, sc_ref, csh_ref, csc_ref, w1_ref, w_ref, p_ref, hm_ref):
        is_ctx = pl.program_id(1) == 0
        xt = jnp.where(is_ctx, c_ref[...], x_ref[...])
        sh = jnp.where(is_ctx, csh_ref[...], sh_ref[...])
        sc = jnp.where(is_ctx, csc_ref[...], sc_ref[...])
        hb = _norm_mod(xt, w1_ref[...], sh, sc).astype(BF16)
        hm_ref[...] = hb
        p_ref[...] = jnp.dot(hb, w_ref[...], preferred_element_type=F32)

    xs, cs, ms, cms = _tok_specs()
    return pl.pallas_call(
        body, name="inproj_fwd", grid=(NB, T // RT),
        out_shape=[jax.ShapeDtypeStruct((NB, T, PC), F32), jax.ShapeDtypeStruct((NB, T, D), BF16)],
        in_specs=[xs, cs, ms, ms, cms, cms, pl.BlockSpec((1, D), lambda b, j: (0, 0)),
                  pl.BlockSpec((D, PC), lambda b, j: (0, 0))],
        out_specs=[pl.BlockSpec((None, RT, PC), lambda b, j: (b, j, 0)),
                   pl.BlockSpec((None, RT, D), lambda b, j: (b, j, 0))],
        compiler_params=_cp(vmem=VMEM_BIG),
    )(x, ctx, sh1, sc1, csh1, csc1, n1w, w_in_p)


def _inproj_bwd(dp_dn, dz, dp_na, dgt, w_in_p, x, ctx, sh1, sc1, csh1, csc1, n1w, dx1):
    def body(d1_ref, d2_ref, d3_ref, d4_ref, w_ref, x_ref, c_ref, sh_ref, sc_ref, csh_ref, csc_ref, w1_ref,
             dx1_ref, dpb_ref, gx_ref, dsh_ref, dsc_ref, dcsh_ref, dcsc_ref, dw1_ref):
        b, j = pl.program_id(0), pl.program_id(1)
        is_ctx = j == 0
        dp = jnp.concatenate([d1_ref[...], d2_ref[...], d3_ref[...], d4_ref[...]], axis=1).astype(BF16)
        dpb_ref[...] = dp
        dhm = lax.dot_general(dp, w_ref[...], _NT, preferred_element_type=F32)
        xt = jnp.where(is_ctx, c_ref[...], x_ref[...])
        sh = jnp.where(is_ctx, csh_ref[...], sh_ref[...])
        sc = jnp.where(is_ctx, csc_ref[...], sc_ref[...])
        _, vjp = jax.vjp(_norm_mod, xt, w1_ref[...], sh, sc)
        dxt, dw1, dsh, dsc = vjp(dhm)

        @pl.when((b == 0) & (j == 0))
        def _():
            dw1_ref[...] = jnp.zeros_like(dw1_ref)
            dcsh_ref[...] = jnp.zeros_like(dcsh_ref)
            dcsc_ref[...] = jnp.zeros_like(dcsc_ref)

        dw1_ref[...] += dw1

        @pl.when(is_ctx)
        def _():
            dcsh_ref[...] += dsh
            dcsc_ref[...] += dsc

        @pl.when(j == 1)
        def _():
            dsh_ref[...] = jnp.zeros_like(dsh_ref)
            dsc_ref[...] = jnp.zeros_like(dsc_ref)

        @pl.when(j >= 1)
        def _():
            dsh_ref[...] += dsh
            dsc_ref[...] += dsc
            gx_ref[...] = dx1_ref[...] + dxt

    xs, cs, ms, cms = _tok_specs()
    tsp = lambda w: pl.BlockSpec((None, RT, w), lambda b, j: (b, j, 0))
    one = pl.BlockSpec((1, D), lambda b, j: (0, 0))
    return pl.pallas_call(
        body, name="inproj_bwd", grid=(NB, T // RT),
        out_shape=[jax.ShapeDtypeStruct((NB, T, PC), BF16), jax.ShapeDtypeStruct((NB, SEQ, D), F32),
                   jax.ShapeDtypeStruct((NB, 1, D), F32), jax.ShapeDtypeStruct((NB, 1, D), F32),
                   jax.ShapeDtypeStruct((1, 1, D), F32), jax.ShapeDtypeStruct((1, 1, D), F32),
                   jax.ShapeDtypeStruct((1, D), F32)],
        in_specs=[tsp(1536), tsp(512), tsp(1536), tsp(128), pl.BlockSpec((D, PC), lambda b, j: (0, 0)),
                  xs, cs, ms, ms, cms, cms, one, xs],
        out_specs=[tsp(PC), xs, ms, ms, cms, cms, one],
        compiler_params=_cp(vmem=VMEM_BIG),
    )(dp_dn, dz, dp_na, dgt, w_in_p, x, ctx, sh1, sc1, csh1, csc1, n1w, dx1)


def _seg_valid(t, off):
    lo = jnp.where(t < CT, 0, CT)
    hi = jnp.where(t < CT, CT, T)
    return (t + off >= lo) & (t + off < hi)


def _shifted(u, off):
    t = lax.broadcasted_iota(jnp.int32, (T, 1), 0)
    s = u if off == 0 else pltpu.roll(u, (-off) % T, 0)
    return jnp.where(_seg_valid(t, off), s, 0.0)


def _conv_pre(u, w):
    acc = _shifted(u, -2) * w[0:1, :]
    for kk in range(1, 5):
        acc = acc + _shifted(u, kk - 2) * w[kk:kk + 1, :]
    return acc


def _act_q(pre):
    a = _silu(pre)
    return a * lax.rsqrt(jnp.sum(a * a, axis=-1, keepdims=True) + EPS) * (128.0 ** -0.5)


def _act_k(pre):
    a = _silu(pre)
    return a * lax.rsqrt(jnp.sum(a * a, axis=-1, keepdims=True) + EPS)


def _dn_prep_fwd(p, cw):
    def body(pq_ref, pk_ref, pv_ref, wq_ref, wk_ref, wv_ref, q_ref, k_ref, v_ref):
        q_ref[...] = _act_q(_conv_pre(pq_ref[...], wq_ref[...]))
        k_ref[...] = _act_k(_conv_pre(pk_ref[...], wk_ref[...]))
        v_ref[...] = _silu(_conv_pre(pv_ref[...], wv_ref[...]))

    ps = lambda o: pl.BlockSpec((None, T, 128), lambda b, h: (b, 0, o + h))
    ws = lambda o: pl.BlockSpec((8, 128), lambda b, h: (0, o + h))
    os_ = pl.BlockSpec((None, T, 128), lambda b, h: (b, 0, h))
    return pl.pallas_call(
        body, name="dn_prep_fwd", grid=(NB, DNH), out_shape=[jax.ShapeDtypeStruct((NB, T, DNW), F32)] * 3,
        in_specs=[ps(0), ps(4), ps(8), ws(0), ws(4), ws(8)], out_specs=[os_] * 3,
        compiler_params=_cp(vmem=VMEM_BIG),
    )(p, p, p, cw, cw, cw)


def _dn_prep_bwd(p, cw, dqs, dks, dvs):
    def body(pq_ref, pk_ref, pv_ref, wq_ref, wk_ref, wv_ref, dqf_ref, dqb_ref, dkf_ref, dkb_ref, dvf_ref, dvb_ref,
             oq_ref, ok_ref, ov_ref, gq_ref, gk_ref, gv_ref):
        b = pl.program_id(1)

        def one(p_ref, w_ref, df_ref, db_ref, act, o_ref, g_ref):
            u = p_ref[...]
            w = w_ref[...]
            _, vjp = jax.vjp(act, _conv_pre(u, w))
            (dpre,) = vjp(df_ref[...] + db_ref[...])
            du = _shifted(dpre, 2) * w[0:1, :]
            rows = [jnp.sum(dpre * _shifted(u, -2), axis=0, keepdims=True)]
            for kk in range(1, 5):
                du = du + _shifted(dpre, 2 - kk) * w[kk:kk + 1, :]
                rows.append(jnp.sum(dpre * _shifted(u, kk - 2), axis=0, keepdims=True))
            o_ref[...] = du
            gw = jnp.concatenate(rows + [jnp.zeros((3, 128), F32)], axis=0)

            @pl.when(b == 0)
            def _():
                g_ref[...] = gw

            @pl.when(b > 0)
            def _():
                g_ref[...] += gw

        one(pq_ref, wq_ref, dqf_ref, dqb_ref, _act_q, oq_ref, gq_ref)
        one(pk_ref, wk_ref, dkf_ref, dkb_ref, _act_k, ok_ref, gk_ref)
        one(pv_ref, wv_ref, dvf_ref, dvb_ref, _silu, ov_ref, gv_ref)

    ps = lambda o: pl.BlockSpec((None, T, 128), lambda h, b: (b, 0, o + h))
    ws = lambda o: pl.BlockSpec((8, 128), lambda h, b: (0, o + h))
    ds = pl.BlockSpec((None, T, 128), lambda h, b: (b, 0, h))
    outs = pl.pallas_call(
        body, name="dn_prep_bwd", grid=(DNH, NB),
        out_shape=[jax.ShapeDtypeStruct((NB, T, DNW), F32)] * 3 + [jax.ShapeDtypeStruct((8, DNW), F32)] * 3,
        in_specs=[ps(0), ps(4), ps(8), ws(0), ws(4), ws(8)] + [ds] * 6,
        out_specs=[ds] * 3
        + [pl.BlockSpec((8, 128), lambda h, b: (0, h))] * 3,
        compiler_params=_cp(vmem=VMEM_BIG),
    )(p, p, p, cw, cw, cw, *dqs, *dks, *dvs)
    return outs


def _gate_fn(a, prm):
    lane = lax.broadcasted_iota(jnp.int32, a.shape, 1)
    z = a + prm[1:2, :]
    sp = jnp.maximum(z, 0.0) + jnp.log(1.0 + jnp.exp(-jnp.abs(z)))
    g = -jnp.exp(prm[0:1, :]) * sp
    return jnp.where(lane < 8, g, jnp.where(lane < 16, _sigmoid(a), 0.0))


def _gates_fwd(p, prm):
    def body(a_ref, prm_ref, o_ref):
        o_ref[...] = _gate_fn(a_ref[...], prm_ref[...])

    return pl.pallas_call(
        body, name="gates_fwd", grid=(NB,), out_shape=jax.ShapeDtypeStruct((NB, T, 128), F32),
        in_specs=[pl.BlockSpec((None, T, 128), lambda b: (b, 0, COL_G // 128)), pl.BlockSpec((8, 128), lambda b: (0, 0))],
        out_specs=pl.BlockSpec((None, T, 128), lambda b: (b, 0, 0)),
    )(p, prm)


def _gates_bwd(p, prm, dgf, dgb):
    def body(a_ref, prm_ref, dgf_ref, dgb_ref, o_ref, dprm_ref):
        b = pl.program_id(0)
        tot = dgf_ref[...] + dgb_ref[...]
        _, vjp = jax.vjp(_gate_fn, a_ref[...], prm_ref[...])
        da, dprm = vjp(tot)
        o_ref[...] = da

        @pl.when(b == 0)
        def _():
            dprm_ref[...] = dprm

        @pl.when(b > 0)
        def _():
            dprm_ref[...] += dprm

    return pl.pallas_call(
        body, name="gates_bwd", grid=(NB,),
        out_shape=[jax.ShapeDtypeStruct((NB, T, 128), F32), jax.ShapeDtypeStruct((8, 128), F32)],
        in_specs=[pl.BlockSpec((None, T, 128), lambda b: (b, 0, COL_G // 128)), pl.BlockSpec((8, 128), lambda b: (0, 0)),
                  pl.BlockSpec((None, T, 128), lambda b: (b, 0, 0)), pl.BlockSpec((None, T, 128), lambda b: (b, 0, 0))],
        out_specs=[pl.BlockSpec((None, T, 128), lambda b: (b, 0, 0)), pl.BlockSpec((8, 128), lambda b: (0, 0))],
        compiler_params=_cp(vmem=VMEM_BIG),
    )(p, prm, dgf, dgb)


def _tri_inv_impl(m, md, m1, m2, eye):
    n = -(m * md)
    t = eye + n
    pw = n
    for _ in range(3):
        pw = _dg(pw, pw, _NN)
        t = t + _dg(t, pw, _NN)
    t = t - _dg(_dg(t, m * m1, _NN), t, _NN)
    t = t - _dg(_dg(t, m * m2, _NN), t, _NN)
    return t


@jax.custom_vjp
def _tri_inv(m, md, m1, m2, eye):
    return _tri_inv_impl(m, md, m1, m2, eye)


def _tri_inv_f(m, md, m1, m2, eye):
    t = _tri_inv_impl(m, md, m1, m2, eye)
    return t, (t, md)


def _tri_inv_b(res, g):
    t, md = res
    tt = t.T
    z = jnp.zeros_like(md)
    return (-_dg(_dg(tt, g, _NN), tt, _NN), z, z, z, z)


_tri_inv.defvjp(_tri_inv_f, _tri_inv_b)


SR = DNH * CH


def _scan_masks():
    i, j = np.arange(SR)[:, None], np.arange(SR)[None, :]
    same = (i >> 6) == (j >> 6)
    cm = np.zeros((2, 6, SR, SR), np.float32)
    for d in range(2):
        diff = i - j if d == 0 else j - i
        cm[d, 0] = same & (diff >= 0)
        cm[d, 1] = same & (diff > 0)
        cm[d, 2] = (i >> 4) == (j >> 4)
        cm[d, 3] = ((i >> 5) == (j >> 5)) & ((i >> 4) != (j >> 4))
        cm[d, 4] = (i >> 5) != (j >> 5)
        cm[d, 5] = i == j
    t, u = np.arange(CH)[:, None], np.arange(CH)[None, :]
    c64 = np.stack([t >= u, t <= u]).astype(np.float32)
    bm = ((np.arange(SR)[:, None] >> 6) == (np.arange(DNW)[None, :] >> 7)).astype(np.float32)
    return jnp.asarray(cm), jnp.asarray(c64), jnp.asarray(bm)


@jax.custom_vjp
def _fold(x, bm):
    xm = x * bm
    return xm[:, 0:128] + xm[:, 128:256] + xm[:, 256:384] + xm[:, 384:512]


def _fold_f(x, bm):
    return _fold(x, bm), bm


def _fold_b(bm, g):
    return jnp.concatenate([g, g, g, g], axis=1) * bm, jnp.zeros_like(bm)


_fold.defvjp(_fold_f, _fold_b)


def _chunk_fn(q, k, v, gb, s, cm, c64, bm, d):
    incl, strict = cm[0], cm[1]
    lane = lax.broadcasted_iota(jnp.int32, (1, 128), 1)
    oh = lambda n: jnp.where(lane == n, 1.0, 0.0).astype(F32)
    col = lambda x, base: jnp.concatenate(
        [jnp.sum(x * oh(base + d * DNH + h), axis=1, keepdims=True) for h in range(DNH)], axis=0)
    beta = col(gb, 8)
    gam = col(_mask_mm(c64, gb), 0)
    tot = jnp.sum(gb, axis=0, keepdims=True)
    gtots = [jnp.sum(tot * oh(d * DNH + h), axis=1, keepdims=True) for h in range(DNH)]
    gtot = jnp.concatenate([jnp.broadcast_to(t, (CH, 1)) for t in gtots], axis=0)
    eg = jnp.concatenate([jnp.broadcast_to(jnp.exp(t), (1, 128)) for t in gtots], axis=1)
    row0 = jnp.where(lax.broadcasted_iota(jnp.int32, (128, 1), 0) == 0, 1.0, 0.0).astype(F32)
    gam_row = jnp.sum(jnp.broadcast_to(gam, (SR, 128)).T * row0, axis=0, keepdims=True)
    dec = jnp.exp(jnp.where(incl > 0.5, gam - gam_row, NEG))
    e_gam = jnp.exp(gam)
    kb = k * beta
    m = strict * (_mm_nt(kb, k) * dec)
    tm = _tri_inv(m, cm[2], cm[3], cm[4], cm[5])
    u = _mm(tm, v * beta)
    w = _mm(tm, kb * e_gam)
    qk = _mm_nt(q, k) * dec
    v_new = u - _fold(_mm(w, s), bm)
    o = _fold(_mm(q * e_gam, s), bm) + _mm(qk, v_new)
    vbd = jnp.concatenate([v_new, v_new, v_new, v_new], axis=1) * bm
    s_new = s * eg + _mm_tn(k * jnp.exp(gtot - gam), vbd)
    return o, s_new


def _chunk_rev(s):
    return jnp.where(s < NCH_CTX, NCH_CTX - 1 - s, NCH + NCH_CTX - 1 - s)


def _scan_specs(step):
    cf = lambda s: step(s)
    cb = lambda s: _chunk_rev(step(s))
    tok = lambda c: pl.BlockSpec((None, CH, DNW), lambda b, s: (b, c(s), 0))
    gat = lambda c: pl.BlockSpec((None, CH, 128), lambda b, s: (b, c(s), 0))
    sta = lambda c: pl.BlockSpec((None, None, 128, DNW), lambda b, s: (b, c(s), 0, 0))
    return (tok(cf), gat(cf), sta(cf)), (tok(cb), gat(cb), sta(cb))


def _const_specs():
    return [pl.BlockSpec((2, 6, SR, SR), lambda b, s: (0, 0, 0, 0)), pl.BlockSpec((2, CH, CH), lambda b, s: (0, 0, 0)),
            pl.BlockSpec((SR, DNW), lambda b, s: (0, 0))]


def _stack(ref):
    return jnp.concatenate([ref[:, 128 * h:128 * h + 128] for h in range(DNH)], axis=0)


def _unstack_to(ref, val):
    for h in range(DNH):
        ref[:, 128 * h:128 * h + 128] = val[CH * h:CH * h + CH]


def _scan_fwd(q, k, v, gates):
    def body(cm_ref, c64_ref, bm_ref, qf, kf, vf, gf, qb, kb, vb, gb, of_ref, sf_ref, ob_ref, sb_ref, s_ref):
        @pl.when(pl.program_id(1) == 0)
        def _():
            s_ref[...] = jnp.zeros_like(s_ref)

        bm = bm_ref[...]
        dirs = ((qf, kf, vf, gf, of_ref, sf_ref), (qb, kb, vb, gb, ob_ref, sb_ref))
        for d, (q_r, k_r, v_r, g_r, o_r, st_r) in enumerate(dirs):
            st = s_ref[d]
            st_r[...] = st
            o, s_new = _chunk_fn(_stack(q_r), _stack(k_r), _stack(v_r), g_r[...], st, cm_ref[d], c64_ref[d], bm, d)
            _unstack_to(o_r, o)
            s_ref[d] = s_new

    (tf, gf_, sf_), (tb, gb_, sb_) = _scan_specs(lambda s: s)
    o_sh = jax.ShapeDtypeStruct((NB, T, DNW), F32)
    st_sh = jax.ShapeDtypeStruct((NB, NCH, 128, DNW), F32)
    return pl.pallas_call(
        body, name="scan_fwd", grid=(NB, NCH), out_shape=[o_sh, st_sh, o_sh, st_sh],
        in_specs=_const_specs() + [tf, tf, tf, gf_, tb, tb, tb, gb_], out_specs=[tf, sf_, tb, sb_],
        scratch_shapes=[pltpu.VMEM((2, 128, DNW), F32)],
        compiler_params=_cp(vmem=VMEM_BIG),
    )(*_scan_masks(), q, k, v, gates, q, k, v, gates)


def _scan_bwd(q, k, v, gates, st_f, st_b, do):
    def body(cm_ref, c64_ref, bm_ref, qf, kf, vf, gf, sf, dof, qb, kb, vb, gb, sb, dob,
             dqf, dkf, dvf, dgf, dqb, dkb, dvb, dgb, ds_ref):
        @pl.when(pl.program_id(1) == 0)
        def _():
            ds_ref[...] = jnp.zeros_like(ds_ref)

        bm = bm_ref[...]
        dirs = ((qf, kf, vf, gf, sf, dof, dqf, dkf, dvf, dgf), (qb, kb, vb, gb, sb, dob, dqb, dkb, dvb, dgb))
        for d, (q_r, k_r, v_r, g_r, st_r, do_r, dq_r, dk_r, dv_r, dg_r) in enumerate(dirs):
            cm, c64 = cm_ref[d], c64_ref[d]
            _, vjp = jax.vjp(lambda a, b_, c, e, f, cm=cm, c64=c64, d=d: _chunk_fn(a, b_, c, e, f, cm, c64, bm, d),
                             _stack(q_r), _stack(k_r), _stack(v_r), g_r[...], st_r[...])
            dq, dk, dv, dg, ds = vjp((_stack(do_r), ds_ref[d]))
            _unstack_to(dq_r, dq)
            _unstack_to(dk_r, dk)
            _unstack_to(dv_r, dv)
            dg_r[...] = dg
            ds_ref[d] = ds

    (tf, gf_, sf_), (tb, gb_, sb_) = _scan_specs(lambda s: NCH - 1 - s)
    o_sh = jax.ShapeDtypeStruct((NB, T, DNW), F32)
    g_sh = jax.ShapeDtypeStruct((NB, T, 128), F32)
    return pl.pallas_call(
        body, name="scan_bwd", grid=(NB, NCH), out_shape=[o_sh, o_sh, o_sh, g_sh] * 2,
        in_specs=_const_specs() + [tf, tf, tf, gf_, sf_, tf, tb, tb, tb, gb_, sb_, tb],
        out_specs=[tf, tf, tf, gf_, tb, tb, tb, gb_],
        scratch_shapes=[pltpu.VMEM((2, 128, DNW), F32)],
        compiler_params=_cp(vmem=VMEM_BIG),
    )(*_scan_masks(), q, k, v, gates, st_f, do, q, k, v, gates, st_b, do)


def _segmean(a, bd):
    hi = a.astype(BF16)
    lo = (a - hi.astype(F32)).astype(BF16)
    return jnp.dot(hi, bd, preferred_element_type=F32) + jnp.dot(lo, bd, preferred_element_type=F32)


def _na_prep_fwd(p, qw, kw, bd):
    def body(q_ref, k_ref, v_ref, qw_ref, kw_ref, bd_ref, oq_ref, ok_ref, ov_ref):
        bdm = bd_ref[...]
        q = q_ref[...]
        k = k_ref[...]
        oq_ref[...] = (q * lax.rsqrt(_segmean(q * q, bdm) + EPS) * qw_ref[...] * 0.125).astype(BF16)
        ok_ref[...] = (k * lax.rsqrt(_segmean(k * k, bdm) + EPS) * kw_ref[...]).astype(BF16)
        ov_ref[...] = v_ref[...].astype(BF16)

    ps = lambda o: pl.BlockSpec((None, RT, NAW), lambda b, j: (b, j, COL_NA // NAW + o))
    one = pl.BlockSpec((1, NAW), lambda b, j: (0, 0))
    os_ = pl.BlockSpec((None, RT, NAW), lambda b, j: (b, j, 0))
    return pl.pallas_call(
        body, name="na_prep_fwd", grid=(NB, T // RT), out_shape=[jax.ShapeDtypeStruct((NB, T, NAW), BF16)] * 3,
        in_specs=[ps(0), ps(1), ps(2), one, one, pl.BlockSpec((NAW, NAW), lambda b, j: (0, 0))], out_specs=[os_] * 3,
    )(p, p, p, qw, kw, bd)


def _na_prep_bwd(p, qw, kw, bd, dq, dk, dv):
    def body(q_ref, k_ref, qw_ref, kw_ref, bd_ref, dq_ref, dk_ref, dv_ref, o_ref, gq_ref, gk_ref):
        b, j = pl.program_id(0), pl.program_id(1)
        bdm = bd_ref[...]

        def one(x, w, dy, scale):
            r = lax.rsqrt(_segmean(x * x, bdm) + EPS)
            g = dy * w * scale
            dx = r * g - x * (r * r * r) * _segmean(g * x, bdm)
            dw = jnp.sum(dy * x * r * scale, axis=0, keepdims=True)
            return dx, dw

        dqv = jnp.where(j == 0, 0.0, dq_ref[...])
        dxq, dwq = one(q_ref[...], qw_ref[...], dqv, 0.125)
        dxk, dwk = one(k_ref[...], kw_ref[...], dk_ref[...], 1.0)
        o_ref[...] = jnp.concatenate([dxq, dxk, dv_ref[...]], axis=1)
        pad = jnp.zeros((7, NAW), F32)

        @pl.when((b == 0) & (j == 0))
        def _():
            gq_ref[...] = jnp.zeros_like(gq_ref)
            gk_ref[...] = jnp.zeros_like(gk_ref)

        gq_ref[...] += jnp.concatenate([dwq, pad], axis=0)
        gk_ref[...] += jnp.concatenate([dwk, pad], axis=0)

    ps = lambda o: pl.BlockSpec((None, RT, NAW), lambda b, j: (b, j, COL_NA // NAW + o))
    one = pl.BlockSpec((1, NAW), lambda b, j: (0, 0))
    ts = pl.BlockSpec((None, RT, NAW), lambda b, j: (b, j, 0))
    acc = pl.BlockSpec((8, NAW), lambda b, j: (0, 0))
    return pl.pallas_call(
        body, name="na_prep_bwd", grid=(NB, T // RT),
        out_shape=[jax.ShapeDtypeStruct((NB, T, 1536), F32), jax.ShapeDtypeStruct((8, NAW), F32),
                   jax.ShapeDtypeStruct((8, NAW), F32)],
        in_specs=[ps(0), ps(1), one, one, pl.BlockSpec((NAW, NAW), lambda b, j: (0, 0)),
                  pl.BlockSpec((None, RT, NAW), lambda b, j: (b, jnp.maximum(j - 1, 0), 0)), ts, ts],
        out_specs=[pl.BlockSpec((None, RT, 1536), lambda b, j: (b, j, 0)), acc, acc],
        compiler_params=_cp(vmem=VMEM_BIG),
    )(p, p, qw, kw, bd, dq, dk, dv)


def _na_row0(r):
    return jnp.clip(r - WIN_ROWS // 2, 0, ROWS - WIN_ROWS)


def _na_cls(r):
    return _na_row0(r) - r + WIN_ROWS - 1


def _na_scores(q, kw, kc, bias, hh):
    lane = lax.broadcasted_iota(jnp.int32, (1, 128), 1)
    hm = ((lane >> 6) == hh)
    qh = jnp.where(hm, q, jnp.zeros_like(q))
    s_w = lax.dot_general(qh, kw, _NT, preferred_element_type=F32) + bias
    s_c = lax.dot_general(qh, kc, _NT, preferred_element_type=F32)
    mx = jnp.maximum(jnp.max(s_w, axis=1, keepdims=True), jnp.max(s_c, axis=1, keepdims=True))
    p_w = jnp.exp(s_w - mx)
    p_c = jnp.exp(s_c - mx)
    inv = 1.0 / (jnp.sum(p_w, axis=1, keepdims=True) + jnp.sum(p_c, axis=1, keepdims=True))
    return hm, qh, p_w, p_c, inv


def _na_fwd(qn, kn, vb, bias):
    def body(q_ref, k_ref, v_ref, b_ref, o_ref):
        r = pl.program_id(2)
        start = pl.multiple_of(CT + _na_row0(r) * GRID_W, GRID_W)
        q = q_ref[...]
        kw, vw = k_ref[pl.ds(start, NWIN), :], v_ref[pl.ds(start, NWIN), :]
        kc, vc = k_ref[0:CT, :], v_ref[0:CT, :]
        out = jnp.zeros((GRID_W, 128), F32)
        for hh in range(2):
            hm, _, p_w, p_c, inv = _na_scores(q, kw, kc, b_ref[hh], hh)
            o = (jnp.dot(p_w.astype(BF16), vw, preferred_element_type=F32)
                 + jnp.dot(p_c.astype(BF16), vc, preferred_element_type=F32)) * inv
            out = jnp.where(hm, o, out)
        o_ref[...] = out.astype(BF16)

    kv = pl.BlockSpec((None, T, 128), lambda g, b, r: (b, 0, g))
    return pl.pallas_call(
        body, name="na_fwd", grid=(NAH // 2, NB, ROWS), out_shape=jax.ShapeDtypeStruct((NB, SEQ, NAW), BF16),
        in_specs=[pl.BlockSpec((None, GRID_W, 128), lambda g, b, r: (b, CT // GRID_W + r, g)), kv, kv,
                  pl.BlockSpec((None, 2, GRID_W, NWIN), lambda g, b, r: (_na_cls(r), g, 0, 0))],
        out_specs=pl.BlockSpec((None, GRID_W, 128), lambda g, b, r: (b, r, g)),
    )(qn, kn, vb, bias)


def _na_bwd(qn, kn, vb, bias, do):
    def body(q_ref, k_ref, v_ref, b_ref, do_ref, dq_ref, dk_ref, dv_ref, db_ref):
        b, r = pl.program_id(1), pl.program_id(2)

        @pl.when((b == 0) & (r == 0))
        def _():
            db_ref[...] = jnp.zeros_like(db_ref)

        @pl.when(r == 0)
        def _():
            dk_ref[...] = jnp.zeros_like(dk_ref)
            dv_ref[...] = jnp.zeros_like(dv_ref)

        start = pl.multiple_of(CT + _na_row0(r) * GRID_W, GRID_W)
        cls = _na_cls(r)
        q = q_ref[...]
        kw, vw = k_ref[pl.ds(start, NWIN), :], v_ref[pl.ds(start, NWIN), :]
        kc, vc = k_ref[0:CT, :], v_ref[0:CT, :]
        dout = do_ref[...]
        dq = jnp.zeros((GRID_W, 128), F32)
        dkw = jnp.zeros((NWIN, 128), F32)
        dvw = jnp.zeros((NWIN, 128), F32)
        dkc = jnp.zeros((CT, 128), F32)
        dvc = jnp.zeros((CT, 128), F32)
        for hh in range(2):
            hm, qh, p_w, p_c, inv = _na_scores(q, kw, kc, b_ref[hh], hh)
            p_w = p_w * inv
            p_c = p_c * inv
            doh = jnp.where(hm, dout, 0.0).astype(BF16)
            dp_w = lax.dot_general(doh, vw, _NT, preferred_element_type=F32)
            dp_c = lax.dot_general(doh, vc, _NT, preferred_element_type=F32)
            delta = jnp.sum(p_w * dp_w, axis=1, keepdims=True) + jnp.sum(p_c * dp_c, axis=1, keepdims=True)
            ds_w = p_w * (dp_w - delta)
            ds_c = p_c * (dp_c - delta)
            db_ref[cls, hh] += ds_w
            ds_wb, ds_cb = ds_w.astype(BF16), ds_c.astype(BF16)
            dqh = jnp.dot(ds_wb, kw, preferred_element_type=F32) + jnp.dot(ds_cb, kc, preferred_element_type=F32)
            dq = jnp.where(hm, dqh, dq)
            dkw = dkw + lax.dot_general(ds_wb, qh, _TN, preferred_element_type=F32)
            dkc = dkc + lax.dot_general(ds_cb, qh, _TN, preferred_element_type=F32)
            dvw = dvw + lax.dot_general(p_w.astype(BF16), doh, _TN, preferred_element_type=F32)
            dvc = dvc + lax.dot_general(p_c.astype(BF16), doh, _TN, preferred_element_type=F32)
        dq_ref[...] = dq
        dk_ref[pl.ds(start, NWIN), :] += dkw
        dv_ref[pl.ds(start, NWIN), :] += dvw
        dk_ref[0:CT, :] += dkc
        dv_ref[0:CT, :] += dvc

    kv = pl.BlockSpec((None, T, 128), lambda g, b, r: (b, 0, g))
    return pl.pallas_call(
        body, name="na_bwd", grid=(NAH // 2, NB, ROWS),
        out_shape=[jax.ShapeDtypeStruct((NB, SEQ, NAW), F32)] + [jax.ShapeDtypeStruct((NB, T, NAW), F32)] * 2
        + [jax.ShapeDtypeStruct((WIN_ROWS, NAH, GRID_W, NWIN), F32)],
        in_specs=[pl.BlockSpec((None, GRID_W, 128), lambda g, b, r: (b, CT // GRID_W + r, g)), kv, kv,
                  pl.BlockSpec((None, 2, GRID_W, NWIN), lambda g, b, r: (_na_cls(r), g, 0, 0)),
                  pl.BlockSpec((None, GRID_W, 128), lambda g, b, r: (b, r, g))],
        out_specs=[pl.BlockSpec((None, GRID_W, 128), lambda g, b, r: (b, r, g)), kv, kv,
                   pl.BlockSpec((WIN_ROWS, 2, GRID_W, NWIN), lambda g, b, r: (0, g, 0, 0))],
        compiler_params=_cp(vmem=VMEM_BIG),
    )(qn, kn, vb, bias, do)


def _na_tables():
    cols = np.arange(GRID_W)
    win_start = np.clip(cols - WIN_COLS // 2, 0, GRID_W - WIN_COLS)
    kc = cols[None, :]
    valid = (kc >= win_start[:, None]) & (kc < win_start[:, None] + WIN_COLS)
    rel = np.clip(kc - cols[:, None] + WIN_COLS - 1, 0, 2 * WIN_COLS - 2)
    return valid, rel


def _bias_table(rpb):
    valid, rel = _na_tables()
    t15 = jnp.where(valid[None, None], rpb[:, :, rel], NEG)
    tab = jnp.stack([t15[:, c:c + WIN_ROWS] for c in range(WIN_ROWS)], axis=0)
    return tab.transpose(0, 1, 3, 2, 4).reshape(WIN_ROWS, NAH, GRID_W, NWIN)


def _rpb_fold(db):
    def body1(d_ref, o_ref):
        for dr in range(2 * WIN_ROWS - 1):
            acc = None
            for c in range(WIN_ROWS):
                i = dr - c
                if 0 <= i < WIN_ROWS:
                    blk = d_ref[c, :, i * GRID_W:(i + 1) * GRID_W]
                    acc = blk if acc is None else acc + blk
            o_ref[dr] = acc

    t15 = pl.pallas_call(
        body1, name="rpb_fold_rows", grid=(NAH,),
        out_shape=jax.ShapeDtypeStruct((NAH, 2 * WIN_ROWS - 1, GRID_W, GRID_W), F32),
        in_specs=[pl.BlockSpec((WIN_ROWS, None, GRID_W, NWIN), lambda h: (0, h, 0, 0))],
        out_specs=pl.BlockSpec((None, 2 * WIN_ROWS - 1, GRID_W, GRID_W), lambda h: (h, 0, 0, 0)),
        compiler_params=_cp(vmem=VMEM_BIG),
    )(db)
    valid, rel = _na_tables()
    onehot = np.zeros((GRID_W * GRID_W, 128), np.float32)
    flat_rel, flat_valid = rel.reshape(-1), valid.reshape(-1)
    onehot[np.arange(GRID_W * GRID_W)[flat_valid], flat_rel[flat_valid]] = 1.0

    def body2(a_ref, oh_ref, o_ref):
        o_ref[...] = _hmm(a_ref[...], oh_ref[...])

    vm = pl.BlockSpec(memory_space=pltpu.VMEM)
    out = pl.pallas_call(
        body2, name="rpb_fold_cols", out_shape=jax.ShapeDtypeStruct((NAH * 15, 128), F32),
        in_specs=[vm, vm], out_specs=vm, compiler_params=_cp(vmem=VMEM_BIG),
    )(t15.reshape(NAH * 15, GRID_W * GRID_W), jnp.asarray(onehot))
    return out[:, :31].reshape(NAH, 15, 31)


def _dn_out(o, z, w):
    outs = []
    for h in range(DNH):
        sl = slice(128 * h, 128 * h + 128)
        oh = o[:, sl]
        r = lax.rsqrt(jnp.mean(oh * oh, axis=-1, keepdims=True) + EPS)
        outs.append(oh * r * w[:, sl] * _silu(z[:, sl]))
    return jnp.concatenate(outs, axis=1)


def _outproj_fwd(of, ob, p, na, onw, x, g1, w_out):
    def body(of_ref, ob_ref, z_ref, na_ref, w_ref, x_ref, g_ref, wo_ref, x1_ref, ai_ref, ao_ref):
        dn = _dn_out(of_ref[...] + ob_ref[...], z_ref[...], w_ref[...])
        ai = jnp.concatenate([dn.astype(BF16), na_ref[...]], axis=1)
        ai_ref[...] = ai
        ao = jnp.dot(ai, wo_ref[...], preferred_element_type=F32)
        ao_ref[...] = ao
        x1_ref[...] = x_ref[...] + g_ref[...] * ao

    lat = lambda w: pl.BlockSpec((None, RT, w), lambda b, j: (b, j, 0))
    return pl.pallas_call(
        body, name="outproj_fwd", grid=(NB, SEQ // RT),
        out_shape=[jax.ShapeDtypeStruct((NB, SEQ, D), F32), jax.ShapeDtypeStruct((NB, SEQ, D), BF16),
                   jax.ShapeDtypeStruct((NB, SEQ, D), F32)],
        in_specs=[pl.BlockSpec((None, RT, DNW), lambda b, j: (b, j + 1, 0)),
                  pl.BlockSpec((None, RT, DNW), lambda b, j: (b, j + 1, 0)),
                  pl.BlockSpec((None, RT, DNW), lambda b, j: (b, j + 1, COL_Z // DNW)),
                  lat(NAW), pl.BlockSpec((1, DNW), lambda b, j: (0, 0)), lat(D),
                  pl.BlockSpec((None, 1, D), lambda b, j: (b, 0, 0)), pl.BlockSpec((D, D), lambda b, j: (0, 0))],
        out_specs=[lat(D), lat(D), lat(D)],
        compiler_params=_cp(vmem=VMEM_BIG),
    )(of, ob, p, na, onw, x, g1, w_out)


def _outproj_bwd(dx1, ao, g1, w_out, of, ob, p, onw):
    def body(dx_ref, ao_ref, g_ref, wo_ref, of_ref, ob_ref, z_ref, w_ref, dao_ref, do_ref, dz_ref, dna_ref, dg_ref, dw_ref):
        b, j = pl.program_id(0), pl.program_id(1)

        @pl.when((b == 0) & (j == 0))
        def _():
            dw_ref[...] = jnp.zeros_like(dw_ref)

        @pl.when(j == 0)
        def _():
            do_ref[...] = jnp.zeros_like(do_ref)
            dz_ref[...] = jnp.zeros_like(dz_ref)

        @pl.when(j == 1)
        def _():
            dg_ref[...] = jnp.zeros_like(dg_ref)

        @pl.when(j >= 1)
        def _():
            dx = dx_ref[...]
            dg_ref[...] += jnp.sum(dx * ao_ref[...], axis=0, keepdims=True)
            dao = (g_ref[...] * dx).astype(BF16)
            dao_ref[...] = dao
            dai = lax.dot_general(dao, wo_ref[...], _NT, preferred_element_type=F32)
            dna_ref[...] = dai[:, DNW:]
            _, vjp = jax.vjp(_dn_out, of_ref[...] + ob_ref[...], z_ref[...], w_ref[...])
            do, dz, dw = vjp(dai[:, :DNW])
            do_ref[...] = do
            dz_ref[...] = dz
            dw_ref[...] += jnp.concatenate([dw, jnp.zeros((7, DNW), F32)], axis=0)

    lat = lambda w: pl.BlockSpec((None, RT, w), lambda b, j: (b, jnp.maximum(j - 1, 0), 0))
    tok = lambda w: pl.BlockSpec((None, RT, w), lambda b, j: (b, j, 0))
    return pl.pallas_call(
        body, name="outproj_bwd", grid=(NB, T // RT),
        out_shape=[jax.ShapeDtypeStruct((NB, SEQ, D), BF16), jax.ShapeDtypeStruct((NB, T, DNW), F32),
                   jax.ShapeDtypeStruct((NB, T, DNW), F32), jax.ShapeDtypeStruct((NB, SEQ, NAW), F32),
                   jax.ShapeDtypeStruct((NB, 1, D), F32), jax.ShapeDtypeStruct((8, DNW), F32)],
        in_specs=[lat(D), lat(D), pl.BlockSpec((None, 1, D), lambda b, j: (b, 0, 0)),
                  pl.BlockSpec((D, D), lambda b, j: (0, 0)),
                  tok(DNW), tok(DNW),
                  pl.BlockSpec((None, RT, DNW), lambda b, j: (b, j, COL_Z // DNW)),
                  pl.BlockSpec((1, DNW), lambda b, j: (0, 0))],
        out_specs=[lat(D), tok(DNW), tok(DNW), lat(NAW), pl.BlockSpec((None, 1, D), lambda b, j: (b, 0, 0)),
                   pl.BlockSpec((8, DNW), lambda b, j: (0, 0))],
        compiler_params=_cp(vmem=VMEM_BIG),
    )(dx1, ao, g1, w_out, of, ob, p, onw)


FT = 256
NFT = NTOK // FT
TPE = SEQ // FT
NFF = DFF // FFT


def _ffn_specs():
    row = lambda w: pl.BlockSpec((FT, w), lambda i, f: (i, 0))
    ex = pl.BlockSpec((None, 1, D), lambda i, f: (i // TPE, 0, 0))
    one = pl.BlockSpec((1, D), lambda i, f: (0, 0))
    wg = pl.BlockSpec((None, D, FFT), lambda i, f: (f, 0, 0))
    wu = pl.BlockSpec((None, D, FFT), lambda i, f: (NFF + f, 0, 0))
    wo = pl.BlockSpec((FFT, D), lambda i, f: (f, 0))
    ff = pl.BlockSpec((FT, FFT), lambda i, f: (i, f))
    return row, ex, one, wg, wu, wo, ff


def _ffn_fwd(x1, sh2, sc2, g2, n2w, w_in4, w_out, target):
    def body(x_ref, sh_ref, sc_ref, g_ref, w2_ref, wg_ref, wu_ref, wo_ref, t_ref,
             h2_ref, gt_ref, up_ref, dy_ref, dg_ref, loss_ref, acc_ref):
        i, f = pl.program_id(0), pl.program_id(1)

        @pl.when(f == 0)
        def _():
            h2_ref[...] = _norm_mod(x_ref[...], w2_ref[...], sh_ref[...], sc_ref[...]).astype(BF16)
            acc_ref[...] = jnp.zeros_like(acc_ref)

        h2 = h2_ref[...]
        gate = jnp.dot(h2, wg_ref[...], preferred_element_type=F32)
        up = jnp.dot(h2, wu_ref[...], preferred_element_type=F32)
        gt_ref[...] = gate.astype(BF16)
        up_ref[...] = up.astype(BF16)
        acc_ref[...] += jnp.dot((_silu(gate) * up).astype(BF16), wo_ref[...], preferred_element_type=F32)

        @pl.when((i == 0) & (f == 0))
        def _():
            loss_ref[...] = jnp.zeros_like(loss_ref)

        @pl.when(f == NFF - 1)
        def _():
            ffn = acc_ref[...]
            err = x_ref[...] + g_ref[...] * ffn - t_ref[...]
            dy = err * (1.0 / D)
            dy_ref[...] = dy
            part = jnp.sum(dy * ffn, axis=0, keepdims=True)
            loss_ref[...] += jnp.sum(jnp.sum(err * err, axis=1, keepdims=True), axis=0, keepdims=True) * (0.5 / D)

            @pl.when(i % TPE == 0)
            def _():
                dg_ref[...] = part

            @pl.when(i % TPE != 0)
            def _():
                dg_ref[...] += part

    row, ex, one, wg, wu, wo, ff = _ffn_specs()
    return pl.pallas_call(
        body, name="ffn_fwd", grid=(NFT, NFF),
        out_shape=[jax.ShapeDtypeStruct((NTOK, D), BF16), jax.ShapeDtypeStruct((NTOK, DFF), BF16),
                   jax.ShapeDtypeStruct((NTOK, DFF), BF16), jax.ShapeDtypeStruct((NTOK, D), F32),
                   jax.ShapeDtypeStruct((NB, 1, D), F32), jax.ShapeDtypeStruct((8, 128), F32)],
        in_specs=[row(D), ex, ex, ex, one, wg, wu, wo, row(D)],
        out_specs=[row(D), ff, ff, row(D), ex, pl.BlockSpec((8, 128), lambda i, f: (0, 0))],
        scratch_shapes=[pltpu.VMEM((FT, D), F32)],
        compiler_params=_cp(vmem=VMEM_BIG),
    )(x1, sh2, sc2, g2, n2w, w_in4, w_in4, w_out, target)


def _ffn_bwd(dy, gate, up, x1, sh2, sc2, g2, n2w, w_in4, w_out):
    def body(dy_ref, gt_ref, up_ref, x_ref, sh_ref, sc_ref, g_ref, w2_ref, wg_ref, wu_ref, wo_ref,
             dff_ref, act_ref, dgu_ref, dx_ref, dsh_ref, dsc_ref, dw_ref, acc_ref):
        i, f = pl.program_id(0), pl.program_id(1)

        @pl.when(f == 0)
        def _():
            dff_ref[...] = (g_ref[...] * dy_ref[...]).astype(BF16)
            acc_ref[...] = jnp.zeros_like(acc_ref)

        dact = lax.dot_general(dff_ref[...], wo_ref[...], _NT, preferred_element_type=F32)
        gate, up = gt_ref[...].astype(F32), up_ref[...].astype(F32)
        sg = _sigmoid(gate)
        sl = gate * sg
        act_ref[...] = (sl * up).astype(BF16)
        dgate = (dact * up * (sg * (1.0 + gate * (1.0 - sg)))).astype(BF16)
        dup = (dact * sl).astype(BF16)
        dgu_ref[0] = dgate
        dgu_ref[1] = dup
        acc_ref[...] += (lax.dot_general(dgate, wg_ref[...], _NT, preferred_element_type=F32)
                         + lax.dot_general(dup, wu_ref[...], _NT, preferred_element_type=F32))

        @pl.when((i == 0) & (f == 0))
        def _():
            dw_ref[...] = jnp.zeros_like(dw_ref)

        @pl.when(f == NFF - 1)
        def _():
            _, vjp = jax.vjp(_norm_mod, x_ref[...], w2_ref[...], sh_ref[...], sc_ref[...])
            dx, dw, dsh, dsc = vjp(acc_ref[...])
            dx_ref[...] = dy_ref[...] + dx
            dw_ref[...] += dw

            @pl.when(i % TPE == 0)
            def _():
                dsh_ref[...] = dsh
                dsc_ref[...] = dsc

            @pl.when(i % TPE != 0)
            def _():
                dsh_ref[...] += dsh
                dsc_ref[...] += dsc

    row, ex, one, wg, wu, wo, ff = _ffn_specs()
    return pl.pallas_call(
        body, name="ffn_bwd", grid=(NFT, NFF),
        out_shape=[jax.ShapeDtypeStruct((NTOK, D), BF16), jax.ShapeDtypeStruct((NTOK, DFF), BF16),
                   jax.ShapeDtypeStruct((2, NTOK, DFF), BF16), jax.ShapeDtypeStruct((NTOK, D), F32),
                   jax.ShapeDtypeStruct((NB, 1, D), F32), jax.ShapeDtypeStruct((NB, 1, D), F32),
                   jax.ShapeDtypeStruct((1, D), F32)],
        in_specs=[row(D), ff, ff, row(D), ex, ex, ex, one, wg, wu, wo],
        out_specs=[row(D), ff, pl.BlockSpec((2, FT, FFT), lambda i, f: (0, i, f)), row(D), ex, ex, one],
        scratch_shapes=[pltpu.VMEM((FT, D), F32)],
        compiler_params=_cp(vmem=VMEM_BIG),
    )(dy, gate, up, x1, sh2, sc2, g2, n2w, w_in4, w_in4, w_out)


def _local_step(x, ctx, target, mod_x, mod_c, norm1_w, norm2_w, w_in_p, conv_w, a_log, dt_bias, out_norm_w,
                q_norm_w, k_norm_w, rpb, w_out_b, w_ffn_in4, w_ffn_out_b):
    sh1, sc1, g1, sh2, sc2, g2 = [mod_x[:, i * D:(i + 1) * D].reshape(NB, 1, D) for i in range(6)]
    csh1, csc1 = mod_c[0:D].reshape(1, 1, D), mod_c[D:2 * D].reshape(1, 1, D)
    cw = jnp.concatenate([conv_w, jnp.zeros((3, 1536), F32)], axis=0)
    prm = jnp.zeros((8, 128), F32).at[0, :8].set(a_log.reshape(8)).at[1, :8].set(dt_bias.reshape(8))
    onw = jnp.tile(out_norm_w.reshape(1, 128), (1, DNH))
    qw = jnp.tile(q_norm_w.reshape(1, 64), (1, NAH))
    kw = jnp.tile(k_norm_w.reshape(1, 64), (1, NAH))
    seg = np.arange(NAW) // 64
    bd = jnp.asarray((seg[:, None] == seg[None, :]).astype(np.float32) / 64.0, dtype=BF16)
    bias = _bias_table(rpb)

    p, hm = _inproj_fwd(x, ctx, sh1, sc1, csh1, csc1, norm1_w, w_in_p)
    dq_, dk_, dv_ = _dn_prep_fwd(p, cw)
    gates = _gates_fwd(p, prm)
    o_f, st_f, o_b, st_b = _scan_fwd(dq_, dk_, dv_, gates)
    qn, kn, vb = _na_prep_fwd(p, qw, kw, bd)
    na = _na_fwd(qn, kn, vb, bias)
    x1, ai, ao = _outproj_fwd(o_f, o_b, p, na, onw, x, g1, w_out_b)
    x1f = x1.reshape(NTOK, D)
    h2, gate, up, dy, dg2, loss8 = _ffn_fwd(x1f, sh2, sc2, g2, norm2_w, w_ffn_in4, w_ffn_out_b,
                                             target.reshape(NTOK, D))
    dff, act, dgu, dx1, dsh2, dsc2, dn2w = _ffn_bwd(dy, gate, up, x1f, sh2, sc2, g2, norm2_w, w_ffn_in4, w_ffn_out_b)
    g_w_ffn_out = _matmul_tn(act, dff, FFT, D, 512, "wgrad_ffn_out")
    g_w_ffn_in4 = _matmul_tn(
        h2, dgu, 512, FFT, 512, "wgrad_ffn_in", nn=2 * NFF,
        b_block=(None, 512, FFT), b_map=lambda i, j, k: (j // NFF, k, j % NFF),
        out_shape=(4, D, FFT), out_block=(None, 512, FFT), out_map=lambda i, j, k: (j, i, 0))
    dao, do, dz, dna, dg1, donw = _outproj_bwd(dx1.reshape(NB, SEQ, D), ao, g1, w_out_b, o_f, o_b, p, onw)
    g_w_out = _matmul_tn(ai.reshape(NTOK, D), dao.reshape(NTOK, D), 512, D, 512, "wgrad_out")
    dqn, dkn, dvn, dbias = _na_bwd(qn, kn, vb, bias, dna)
    dp_na, dqw, dkw = _na_prep_bwd(p, qw, kw, bd, dqn, dkn, dvn)
    dqf, dkf, dvf, dgf, dqb, dkb, dvb, dgb = _scan_bwd(dq_, dk_, dv_, gates, st_f, st_b, do)
    dpq, dpk, dpv, gcq, gck, gcv = _dn_prep_bwd(p, cw, (dqf, dqb), (dkf, dkb), (dvf, dvb))
    dgt, dprm = _gates_bwd(p, prm, dgf, dgb)
    dp_dn = jnp.concatenate([dpq, dpk, dpv], axis=2)
    dpb, grad_x, dsh1, dsc1, dcsh1, dcsc1, dn1w = _inproj_bwd(
        dp_dn, dz, dp_na, dgt, w_in_p, x, ctx, sh1, sc1, csh1, csc1, norm1_w, dx1.reshape(NB, SEQ, D))
    g_w_in_p = _matmul_tn(hm.reshape(NB * T, D), dpb.reshape(NB * T, PC), 256, PC, 512, "wgrad_in")
    g_rpb = _rpb_fold(dbias)

    dmod_x = jnp.concatenate([dsh1, dsc1, dg1, dsh2, dsc2, dg2], axis=2).reshape(NB, 6 * D)
    dmod_c = jnp.concatenate([dcsh1.reshape(D), dcsc1.reshape(D), jnp.zeros((4 * D,), F32)])
    g_conv = jnp.concatenate([gcq[:5], gck[:5], gcv[:5]], axis=1)
    small = dict(
        norm1_w=dn1w, norm2_w=dn2w, a_log=dprm[0, :8], dt_bias=dprm[1, :8],
        out_norm_w=donw[0].reshape(DNH, 128).sum(0), q_norm_w=dqw[0].reshape(NAH, 64).sum(0),
        k_norm_w=dkw[0].reshape(NAH, 64).sum(0), rpb=g_rpb, conv_w=g_conv)
    return loss8[0, 0], grad_x, dmod_x, dmod_c, small, g_w_in_p, g_w_out, g_w_ffn_in4, g_w_ffn_out


def _perm_w_in(w):
    return jnp.concatenate([w[:, :2048], w[:, 2064:3600], w[:, 2048:2064],
                            jnp.zeros((w.shape[0], PC - IN_COLS), w.dtype)], axis=1)


def _unperm_w_in(g):
    return jnp.concatenate([g[:, :2048], g[:, COL_G:COL_G + 16], g[:, 2048:COL_G]], axis=1)


def _pack_small(c_ctx, n1, n2, b_ada, a_log, dt_bias, onw, qnw, knw, rpb, conv_s):
    misc = jnp.zeros((D,), F32)
    misc = misc.at[0:8].set(a_log.reshape(8)).at[8:16].set(dt_bias.reshape(8))
    misc = misc.at[128:256].set(onw.reshape(128)).at[256:320].set(qnw.reshape(64)).at[320:384].set(knw.reshape(64))
    rp = jnp.concatenate([rpb.reshape(-1), jnp.zeros((4 * D - 3720,), F32)])
    cv = jnp.concatenate([conv_s.reshape(-1), jnp.zeros((2 * D - 1920,), F32)])
    return jnp.concatenate([c_ctx.reshape(-1), n1.reshape(-1), n2.reshape(-1), b_ada.reshape(-1), misc, rp, cv]).reshape(16, D)


def _unpack_small(a):
    f = a.reshape(-1)
    misc = f[9 * D:10 * D]
    return dict(
        c_ctx=f[0:D], norm1_w=f[D:2 * D].reshape(1, D), norm2_w=f[2 * D:3 * D].reshape(1, D),
        b_ada=f[3 * D:9 * D].reshape(1, 6 * D), dn_A_log=misc[0:8].reshape(1, 2, 4), dn_dt_bias=misc[8:16].reshape(1, 2, 4),
        dn_out_norm_w=misc[128:256].reshape(1, 128), na_q_norm_w=misc[256:320].reshape(1, 64),
        na_k_norm_w=misc[320:384].reshape(1, 64), na_rpb=f[10 * D:10 * D + 3720].reshape(1, 8, 15, 31),
        dn_conv_w=f[14 * D:14 * D + 1920].reshape(1, 5, 384))


_WEIGHTS = ['c_ctx', 'norm1_w', 'norm2_w', 'w_ada', 'b_ada', 'w_in', 'dn_conv_w', 'dn_A_log', 'dn_dt_bias',
            'dn_out_norm_w', 'na_q_norm_w', 'na_k_norm_w', 'na_rpb', 'w_out', 'w_ffn_in', 'w_ffn_out']


def kernel(x, c, ctx, c_ctx, norm1_w, norm2_w, w_ada, b_ada, w_in, dn_conv_w, dn_A_log, dn_dt_bias, dn_out_norm_w, na_q_norm_w, na_k_norm_w, na_rpb, w_out, w_ffn_in, w_ffn_out, loss_target, m_c_ctx, m_norm1_w, m_norm2_w, m_w_ada, m_b_ada, m_w_in, m_dn_conv_w, m_dn_A_log, m_dn_dt_bias, m_dn_out_norm_w, m_na_q_norm_w, m_na_k_norm_w, m_na_rpb, m_w_out, m_w_ffn_in, m_w_ffn_out, v_c_ctx, v_norm1_w, v_norm2_w, v_w_ada, v_b_ada, v_w_in, v_dn_conv_w, v_dn_A_log, v_dn_dt_bias, v_dn_out_norm_w, v_na_q_norm_w, v_na_k_norm_w, v_na_rpb, v_w_out, v_w_ffn_in, v_w_ffn_out):
    ix, iy, ic = lax.axis_index("x"), lax.axis_index("y"), lax.axis_index("c")
    chip = 2 * ix + iy
    me = 4 * ix + 2 * iy + ic

    crow = jnp.concatenate([c, c_ctx.reshape(1, D), jnp.zeros((5, D), F32)], axis=0)
    cvrow = jnp.concatenate([dn_conv_w[0].reshape(-1), jnp.zeros((2 * D - 1920,), F32)]).reshape(2, D)
    g0 = _allgather8(jnp.concatenate([crow, cvrow, jnp.zeros((6, D), F32)], axis=0), "gather_cond")
    cin = g0[:, :8].reshape(64, D)
    conv_full = jnp.concatenate([g0[2 * s, 8:10].reshape(-1)[:1920].reshape(5, 384) for s in range(4)], axis=1)

    w_ada_s = w_ada[0]
    b_s = lax.dynamic_slice(b_ada, (0, chip * 1536), (1, 1536))
    mod_s = _ada_fwd(cin, w_ada_s, b_s)
    mods = _allgather8(mod_s, "gather_mod")
    mod_all = jnp.concatenate([mods[0], mods[2], mods[4], mods[6]], axis=1)
    mod_mine = lax.dynamic_slice(mod_all, (8 * me, 0), (8, 6 * D))
    mod_x, mod_c = mod_mine[0:NB], mod_mine[2]

    def half(w2d):
        r = w2d.shape[0] // 2
        return lax.dynamic_slice(w2d, (ic * r, 0), (r, w2d.shape[1]))

    halves = [_cast_bf16(half(w_in[0]), "cast_w_in"), _cast_bf16(half(w_out[0]), "cast_w_out"),
              _cast_bf16(half(w_ffn_in[0]), "cast_w_ffn_in"), _cast_bf16(half(w_ffn_out[0]), "cast_w_ffn_out")]
    gw_in, gw_out, gw_ffn_in, gw_ffn_out = _gather_weights(halves)
    w_in_full = gw_in.reshape(4, D, 900).transpose(1, 0, 2).reshape(D, IN_COLS)
    w_in_p = _perm_w_in(w_in_full)
    w_out_b = gw_out.reshape(D, D)
    w_ffn_in4 = gw_ffn_in.reshape(4, D, FFT)
    w_ffn_out_b = gw_ffn_out.reshape(DFF, D)

    (loss_part, grad_x, dmod_x, dmod_c, small, g_w_in_p, g_w_out, g_w_ffn_in4, g_w_ffn_out) = _local_step(
        x, ctx, loss_target, mod_x, mod_c, norm1_w, norm2_w, w_in_p, conv_full, dn_A_log[0], dn_dt_bias[0],
        dn_out_norm_w[0], na_q_norm_w[0], na_k_norm_w[0], na_rpb[0], w_out_b, w_ffn_in4, w_ffn_out_b)
    loss = lax.psum(loss_part, ("x", "y", "c"))

    dm = jnp.concatenate([dmod_x, dmod_c.reshape(1, 6 * D), jnp.zeros((5, 6 * D), F32)], axis=0).reshape(48, D)
    misc = jnp.zeros((D,), F32)
    misc = misc.at[0:8].set(small["a_log"]).at[8:16].set(small["dt_bias"]).at[128:256].set(small["out_norm_w"])
    misc = misc.at[256:320].set(small["q_norm_w"]).at[320:384].set(small["k_norm_w"])
    sm = jnp.concatenate([
        small["norm1_w"].reshape(-1), small["norm2_w"].reshape(-1), misc,
        small["rpb"].reshape(-1), jnp.zeros((4 * D - 3720,), F32),
        small["conv_w"].reshape(-1), jnp.zeros((8 * D - 7680,), F32), jnp.zeros((D,), F32)]).reshape(16, D)
    gsl, gtot = _allgather8(jnp.concatenate([dm, sm], axis=0), "gather_small", with_sum=True)
    gmod = gsl[:, :48].reshape(64, 6 * D)
    st = gtot[48:64].reshape(-1)
    g_n1, g_n2, g_misc = st[0:D].reshape(1, D), st[D:2 * D].reshape(1, D), st[2 * D:3 * D]
    g_rpb = st[3 * D:3 * D + 3720].reshape(1, 8, 15, 31)
    g_conv_full = st[7 * D:7 * D + 7680].reshape(5, 1536)
    g_conv_s = lax.dynamic_slice(g_conv_full, (0, chip * 384), (5, 384)).reshape(1, 5, 384)

    gmod_s = lax.dynamic_slice(gmod, (0, chip * 1536), (64, 1536))
    g_w_ada_s, pctx = _ada_bwd(cin, gmod_s, w_ada_s)
    parts = _allgather8(pctx, "gather_cctx")
    g_b_ada, g_c_ctx = _ada_small(gmod, parts, c_ctx.reshape(1, D))

    g_w_in_full = _unperm_w_in(g_w_in_p)
    g_w_in8 = g_w_in_full.reshape(D, 4, 900).transpose(1, 0, 2).reshape(8, D // 2, 900)
    names = ["w_in", "w_out", "w_ffn_in", "w_ffn_out"]
    theirs, own = _presum_swap([g_w_in8, g_w_out.reshape(8, D // 8, D), g_w_ffn_in4.reshape(8, D // 2, FFT),
                                g_w_ffn_out.reshape(8, DFF // 8, D)])
    parts = [_add_cast(o_, t_, "chipsum_" + nm) for o_, t_, nm in zip(own, theirs, names)]
    sums = [_sum_chips(q_, "sum_" + nm) for q_, nm in zip(_chip_scatter(parts), names)]
    full = _sibling_exchange(sums)
    g_w_in_s = full[0].reshape(1, D, 900)
    g_w_out_s = full[1].reshape(1, D // 4, D)
    g_w_ffn_in_s = full[2].reshape(1, D, FFT)
    g_w_ffn_out_s = full[3].reshape(1, DFF // 4, D)

    grads = dict(
        c_ctx=g_c_ctx.reshape(D), norm1_w=g_n1, norm2_w=g_n2, w_ada=g_w_ada_s.reshape(1, D, 1536), b_ada=g_b_ada,
        w_in=g_w_in_s, dn_conv_w=g_conv_s, dn_A_log=g_misc[0:8].reshape(1, 2, 4), dn_dt_bias=g_misc[8:16].reshape(1, 2, 4),
        dn_out_norm_w=g_misc[128:256].reshape(1, 128), na_q_norm_w=g_misc[256:320].reshape(1, 64),
        na_k_norm_w=g_misc[320:384].reshape(1, 64), na_rpb=g_rpb, w_out=g_w_out_s, w_ffn_in=g_w_ffn_in_s,
        w_ffn_out=g_w_ffn_out_s)
    pk = lambda d_: _pack_small(d_["c_ctx"], d_["norm1_w"], d_["norm2_w"], d_["b_ada"], d_["dn_A_log"], d_["dn_dt_bias"],
                                d_["dn_out_norm_w"], d_["na_q_norm_w"], d_["na_k_norm_w"], d_["na_rpb"], d_["dn_conv_w"])
    wts = dict(c_ctx=c_ctx, norm1_w=norm1_w, norm2_w=norm2_w, b_ada=b_ada, dn_A_log=dn_A_log, dn_dt_bias=dn_dt_bias,
               dn_out_norm_w=dn_out_norm_w, na_q_norm_w=na_q_norm_w, na_k_norm_w=na_k_norm_w, na_rpb=na_rpb, dn_conv_w=dn_conv_w)
    ms = dict(c_ctx=m_c_ctx, norm1_w=m_norm1_w, norm2_w=m_norm2_w, b_ada=m_b_ada, dn_A_log=m_dn_A_log, dn_dt_bias=m_dn_dt_bias,
              dn_out_norm_w=m_dn_out_norm_w, na_q_norm_w=m_na_q_norm_w, na_k_norm_w=m_na_k_norm_w, na_rpb=m_na_rpb,
              dn_conv_w=m_dn_conv_w)
    vs = dict(c_ctx=v_c_ctx, norm1_w=v_norm1_w, norm2_w=v_norm2_w, b_ada=v_b_ada, dn_A_log=v_dn_A_log, dn_dt_bias=v_dn_dt_bias,
              dn_out_norm_w=v_dn_out_norm_w, na_q_norm_w=v_na_q_norm_w, na_k_norm_w=v_na_k_norm_w, na_rpb=v_na_rpb,
              dn_conv_w=v_dn_conv_w)
    sd, snm, snv = _adamw(pk(wts), pk(grads), pk(ms), pk(vs), "adamw_small")
    delta, new_m, new_v = _unpack_small(sd), _unpack_small(snm), _unpack_small(snv)
    big = dict(w_ada=(w_ada, m_w_ada, v_w_ada), w_in=(w_in, m_w_in, v_w_in), w_out=(w_out, m_w_out, v_w_out),
               w_ffn_in=(w_ffn_in, m_w_ffn_in, v_w_ffn_in), w_ffn_out=(w_ffn_out, m_w_ffn_out, v_w_ffn_out))
    for nm, (w_, m_, v_) in big.items():
        shp = w_.shape
        r2 = lambda a: a.reshape(shp[1], shp[2])
        d_, nm_, nv_ = _adamw(r2(w_), r2(grads[nm]), r2(m_), r2(v_), "adamw_" + nm)
        delta[nm], new_m[nm], new_v[nm] = d_.reshape(shp), nm_.reshape(shp), nv_.reshape(shp)

    return (loss, grad_x, *[grads[n] for n in _WEIGHTS], *[delta[n] for n in _WEIGHTS],
            *[new_m[n] for n in _WEIGHTS], *[new_v[n] for n in _WEIGHTS])
```

```python
import functools

import numpy as np
import jax
import jax.numpy as jnp
from jax import lax
from jax.experimental import pallas as pl
from jax.experimental.pallas import tpu as pltpu

F32 = jnp.float32
BF16 = jnp.bfloat16

D = 1024
NB = 2
SEQ = 2048
CT = 256
T = CT + SEQ
NTOK = NB * SEQ
CH = 64
NCH = T // CH
NCH_CTX = CT // CH
DNH, DNW = 4, 512
NAH, NAW = 8, 512
GRID_W = 64
ROWS = SEQ // GRID_W
WIN_ROWS = 8
WIN_COLS = 16
NWIN = WIN_ROWS * GRID_W
DFF = 2816
FFT = 1408
IN_COLS = 3600
PC = 3712
COL_Z, COL_NA, COL_G = 1536, 2048, 3584
EPS = 1e-6
NEG = -1e30
RT = 256
VMEM_BIG = 56 * 1024 * 1024
MESH = pl.DeviceIdType.MESH

ADAM_LR, ADAM_B1, ADAM_B2, ADAM_EPS, ADAM_WD, ADAM_STEP = 0.001, 0.9, 0.999, 1e-08, 0.01, 10


def _cp(vmem=None, sem=None):
    kw = {}
    if vmem is not None:
        kw["vmem_limit_bytes"] = vmem
    if sem is not None:
        kw["dimension_semantics"] = sem
    return pltpu.CompilerParams(**kw)


_NN = (((1,), (0,)), ((), ()))
_NT = (((1,), (1,)), ((), ()))
_TN = (((0,), (0,)), ((), ()))


def _dg(a, b, dims):
    return lax.dot_general(a.astype(BF16), b.astype(BF16), dims, preferred_element_type=F32)


@jax.custom_vjp
def _mm(a, b):
    return _dg(a, b, _NN)


_mm.defvjp(lambda a, b: (_dg(a, b, _NN), (a, b)),
           lambda r, g: (_dg(g, r[1], _NT), _dg(r[0], g, _TN)))


@jax.custom_vjp
def _mm_nt(a, b):
    return _dg(a, b, _NT)


_mm_nt.defvjp(lambda a, b: (_dg(a, b, _NT), (a, b)),
              lambda r, g: (_dg(g, r[1], _NN), _dg(g, r[0], _TN)))


@jax.custom_vjp
def _mm_tn(a, b):
    return _dg(a, b, _TN)


_mm_tn.defvjp(lambda a, b: (_dg(a, b, _TN), (a, b)),
              lambda r, g: (_dg(r[1], g, _NT), _dg(r[0], g, _NN)))


def _split_dot(a, b, dims):
    hi = b.astype(BF16)
    lo = (b - hi.astype(F32)).astype(BF16)
    ab = a.astype(BF16)
    return (lax.dot_general(ab, hi, dims, preferred_element_type=F32)
            + lax.dot_general(ab, lo, dims, preferred_element_type=F32))


@jax.custom_vjp
def _mask_mm(mask, b):
    return _split_dot(mask, b, _NN)


_mask_mm.defvjp(lambda mask, b: (_split_dot(mask, b, _NN), mask),
                lambda mask, g: (jnp.zeros_like(mask), _split_dot(mask, g, _TN)))


def _hmm(a, b):
    return jnp.dot(a, b, precision=lax.Precision.HIGHEST, preferred_element_type=F32)


def _sigmoid(x):
    return 1.0 / (1.0 + jnp.exp(-x))


def _silu(x):
    return x * _sigmoid(x)


def _norm_mod(x, w, sh, sc):
    r = lax.rsqrt(jnp.mean(x * x, axis=-1, keepdims=True) + EPS)
    return (x * r * w) * (1.0 + sc) + sh


def _my_pos():
    return lax.axis_index("x"), lax.axis_index("y"), lax.axis_index("c")


def _flip(pos, k):
    x, y, c = pos
    return (1 - x if k & 4 else x, 1 - y if k & 2 else y, 1 - c if k & 1 else c)


def _slot(pos):
    return 4 * pos[0] + 2 * pos[1] + pos[2]


def _allgather8(v, name, with_sum=False):
    R, W = v.shape

    def body(v_ref, out_ref, *rest):
        if with_sum:
            tot_ref, send_sems, recv_sems, lsem = rest
        else:
            send_sems, recv_sems, lsem = rest
        me = _my_pos()
        mine = pltpu.make_async_copy(v_ref, out_ref.at[_slot(me)], lsem)
        mine.start()
        sends = []
        for k in range(1, 8):
            cp = pltpu.make_async_remote_copy(
                src_ref=v_ref, dst_ref=out_ref.at[_slot(me)], send_sem=send_sems.at[k - 1],
                recv_sem=recv_sems.at[k - 1], device_id=_flip(me, k), device_id_type=MESH)
            cp.start()
            sends.append(cp)
        for k in range(1, 8):
            peer = _flip(me, k)
            pltpu.make_async_remote_copy(
                src_ref=v_ref, dst_ref=out_ref.at[_slot(peer)], send_sem=send_sems.at[k - 1],
                recv_sem=recv_sems.at[k - 1], device_id=peer, device_id_type=MESH).wait_recv()
        for cp in sends:
            cp.wait_send()
        mine.wait()
        if with_sum:
            acc = out_ref[0]
            for s in range(1, 8):
                acc = acc + out_ref[s]
            tot_ref[...] = acc

    out_shape = [jax.ShapeDtypeStruct((8, R, W), F32)]
    out_specs = [pl.BlockSpec(memory_space=pltpu.VMEM)]
    if with_sum:
        out_shape.append(jax.ShapeDtypeStruct((R, W), F32))
        out_specs.append(pl.BlockSpec(memory_space=pltpu.VMEM))
    res = pl.pallas_call(
        body, name=name, out_shape=out_shape,
        in_specs=[pl.BlockSpec(memory_space=pltpu.VMEM)], out_specs=out_specs,
        scratch_shapes=[pltpu.SemaphoreType.DMA((7,)), pltpu.SemaphoreType.DMA((7,)), pltpu.SemaphoreType.DMA],
        compiler_params=_cp(vmem=VMEM_BIG),
    )(v)
    return res if with_sum else res[0]


def _gather_weights(halves):
    n = len(halves)

    def body(*refs):
        ins, outs = refs[:n], refs[n:2 * n]
        send_sems, recv_sems, lsems = refs[2 * n:]
        me = _my_pos()
        x, y, c = me
        sib = (x, y, 1 - c)
        chips = [4, 2, 6]

        def cp(a, k, block, to, src=None):
            dst = outs[a].at[_slot(block)]
            return pltpu.make_async_remote_copy(
                src_ref=dst if src is None else src, dst_ref=dst, send_sem=send_sems.at[a, k],
                recv_sem=recv_sems.at[a, k], device_id=to, device_id_type=MESH)

        mine, first, passed = [], [], []
        for a in range(n):
            m = pltpu.make_async_copy(ins[a], outs[a].at[_slot(me)], lsems.at[a])
            m.start()
            mine.append(m)
            first.append(cp(a, 0, me, sib, src=ins[a]))
            for j, k in enumerate(chips):
                first.append(cp(a, 1 + j, me, _flip(me, k), src=ins[a]))
        for f in first:
            f.start()
        for j, k in enumerate(chips):
            for a in range(n):
                cp(a, 1 + j, _flip(me, k), me).wait_recv()
                p = cp(a, 4 + j, _flip(me, k), sib)
                p.start()
                passed.append(p)
        for a in range(n):
            cp(a, 0, sib, me).wait_recv()
            for j, k in enumerate(chips):
                cp(a, 4 + j, _flip(sib, k), me).wait_recv()
        for f in first + passed:
            f.wait_send()
        for m in mine:
            m.wait()

    hbm = pl.BlockSpec(memory_space=pltpu.HBM)
    return pl.pallas_call(
        body, name="gather_weights",
        out_shape=[jax.ShapeDtypeStruct((8,) + h.shape, h.dtype) for h in halves],
        in_specs=[hbm] * n, out_specs=[hbm] * n,
        scratch_shapes=[pltpu.SemaphoreType.DMA((n, 7)), pltpu.SemaphoreType.DMA((n, 7)),
                        pltpu.SemaphoreType.DMA((n,))],
    )(*halves)


def _exchange(name, ins, out_shapes, n_local, n_xfer, plan):
    n_in, n_out = len(ins), len(out_shapes)

    def body(*refs):
        i_refs, o_refs = refs[:n_in], refs[n_in:n_in + n_out]
        send_sems, recv_sems, lsems = refs[n_in + n_out:]
        locals_, xfers = plan(i_refs, o_refs, _my_pos())
        assert len(locals_) == n_local and len(xfers) == n_xfer
        mine = [pltpu.make_async_copy(src, dst, lsems.at[i]) for i, (src, dst) in enumerate(locals_)]
        for m in mine:
            m.start()
        sends = [pltpu.make_async_remote_copy(src_ref=src, dst_ref=dst, send_sem=send_sems.at[i], recv_sem=recv_sems.at[i],
                                              device_id=peer, device_id_type=MESH)
                 for i, (src, dst, _, peer) in enumerate(xfers)]
        for cp in sends:
            cp.start()
        for i, (src, _, landing, peer) in enumerate(xfers):
            pltpu.make_async_remote_copy(src_ref=src, dst_ref=landing, send_sem=send_sems.at[i], recv_sem=recv_sems.at[i],
                                         device_id=peer, device_id_type=MESH).wait_recv()
        for cp in sends:
            cp.wait_send()
        for m in mine:
            m.wait()

    hbm = pl.BlockSpec(memory_space=pltpu.HBM)
    return pl.pallas_call(
        body, name=name, out_shape=out_shapes, in_specs=[hbm] * n_in, out_specs=[hbm] * n_out,
        scratch_shapes=[pltpu.SemaphoreType.DMA((n_xfer,)), pltpu.SemaphoreType.DMA((n_xfer,)),
                        pltpu.SemaphoreType.DMA((n_local,))],
    )(*ins)


def _scatter_grads(grads):
    n = len(grads)

    def plan(ins, outs, pos):
        me = _slot(pos)
        locals_, xfers = [], []
        for a in range(n):
            locals_.append((ins[a].at[me], outs[a].at[me]))
            for k in range(1, 8):
                peer = _flip(pos, k)
                xfers.append((ins[a].at[_slot(peer)], outs[a].at[me], outs[a].at[_slot(peer)], peer))
        return locals_, xfers

    sh = [jax.ShapeDtypeStruct(g.shape, g.dtype) for g in grads]
    return _exchange("scatter_grads", grads, sh, n, 7 * n, plan)


def _sibling_exchange(halves):
    n = len(halves)

    def body(*refs):
        ins, outs = refs[:n], refs[n:2 * n]
        send_sems, recv_sems, lsems = refs[2 * n:]
        x, y, c = _my_pos()
        sib = (x, y, 1 - c)
        mine, sends = [], []
        for a in range(n):
            m = pltpu.make_async_copy(ins[a], outs[a].at[c], lsems.at[a])
            m.start()
            mine.append(m)
            cp = pltpu.make_async_remote_copy(
                src_ref=ins[a], dst_ref=outs[a].at[c], send_sem=send_sems.at[a], recv_sem=recv_sems.at[a],
                device_id=sib, device_id_type=MESH)
            cp.start()
            sends.append(cp)
        for a in range(n):
            pltpu.make_async_remote_copy(
                src_ref=ins[a], dst_ref=outs[a].at[1 - c], send_sem=send_sems.at[a], recv_sem=recv_sems.at[a],
                device_id=sib, device_id_type=MESH).wait_recv()
        for cp in sends:
            cp.wait_send()
        for m in mine:
            m.wait()

    vm = pl.BlockSpec(memory_space=pltpu.VMEM)
    return pl.pallas_call(
        body, name="sibling_exchange",
        out_shape=[jax.ShapeDtypeStruct((2,) + h.shape, h.dtype) for h in halves],
        in_specs=[vm] * n, out_specs=[vm] * n,
        scratch_shapes=[pltpu.SemaphoreType.DMA((n,)), pltpu.SemaphoreType.DMA((n,)), pltpu.SemaphoreType.DMA((n,))],
        compiler_params=_cp(vmem=VMEM_BIG),
    )(*halves)


def _row_tile(r, cap=512):
    for t in (cap, 256, 128, 64, 32, 16, 8):
        if t <= cap and r % t == 0:
            return t
    return r


def _cast_bf16(a, name):
    R, W = a.shape
    tr = _row_tile(R, 256)

    def body(a_ref, o_ref):
        o_ref[...] = a_ref[...].astype(BF16)

    return pl.pallas_call(
        body, name=name, grid=(R // tr,), out_shape=jax.ShapeDtypeStruct((R, W), BF16),
        in_specs=[pl.BlockSpec((tr, W), lambda i: (i, 0))], out_specs=pl.BlockSpec((tr, W), lambda i: (i, 0)),
    )(a)


def _sum_slots(r, name):
    _, R, W = r.shape
    tr = _row_tile(R, 128)

    def body(r_ref, o_ref):
        acc = r_ref[0].astype(F32)
        for s_ in range(1, 8):
            acc = acc + r_ref[s_].astype(F32)
        o_ref[...] = acc.astype(BF16)

    return pl.pallas_call(
        body, name=name, grid=(R // tr,), out_shape=jax.ShapeDtypeStruct((R, W), BF16),
        in_specs=[pl.BlockSpec((8, tr, W), lambda i: (0, i, 0))], out_specs=pl.BlockSpec((tr, W), lambda i: (i, 0)),
    )(r)


def _adamw(w, g, m, v, name):
    R, W = w.shape
    tr = _row_tile(R, 256)
    c1 = 1.0 / (1.0 - ADAM_B1 ** ADAM_STEP)
    c2 = 1.0 / (1.0 - ADAM_B2 ** ADAM_STEP)

    def body(w_ref, g_ref, m_ref, v_ref, d_ref, nm_ref, nv_ref, g32_ref):
        gg = g_ref[...].astype(F32)
        g32_ref[...] = gg
        nm = ADAM_B1 * m_ref[...] + (1.0 - ADAM_B1) * gg
        nv = ADAM_B2 * v_ref[...] + (1.0 - ADAM_B2) * (gg * gg)
        d_ref[...] = -ADAM_LR * ((nm * c1) / (jnp.sqrt(nv * c2) + ADAM_EPS) + ADAM_WD * w_ref[...])
        nm_ref[...] = nm
        nv_ref[...] = nv

    spec = pl.BlockSpec((tr, W), lambda i: (i, 0))
    return pl.pallas_call(
        body, name=name, grid=(R // tr,), out_shape=[jax.ShapeDtypeStruct((R, W), F32)] * 4,
        in_specs=[spec] * 4, out_specs=[spec] * 4,
    )(w, g, m, v)


def _matmul_tn(a, b, tm, tn, tk, name, nn=1, b_map=None, out_shape=None, out_block=None, out_map=None,
               b_block=None):
    K, M = a.shape
    nk = K // tk
    nm = M // tm

    def body(a_ref, b_ref, o_ref, acc_ref):
        kk = pl.program_id(2)

        @pl.when(kk == 0)
        def _():
            acc_ref[...] = jnp.zeros_like(acc_ref)

        acc_ref[...] += lax.dot_general(a_ref[...], b_ref[...], _TN, preferred_element_type=F32)

        @pl.when(kk == nk - 1)
        def _():
            o_ref[...] = acc_ref[...].astype(BF16)

    return pl.pallas_call(
        body, name=name, grid=(nm, nn, nk),
        out_shape=jax.ShapeDtypeStruct(out_shape or (M, b.shape[-1]), BF16),
        in_specs=[pl.BlockSpec((tk, tm), lambda i, j, k: (k, i)),
                  pl.BlockSpec(b_block or (tk, tn), b_map or (lambda i, j, k: (k, j)))],
        out_specs=pl.BlockSpec(out_block or (tm, tn), out_map or (lambda i, j, k: (i, j))),
        scratch_shapes=[pltpu.VMEM((tm, tn), F32)],
        compiler_params=_cp(vmem=VMEM_BIG),
    )(a, b)


def _ada_fwd(cin, w_ada_s, b_s):
    tn = 512

    def body(c_ref, w_ref, b_ref, o_ref):
        o_ref[...] = _dg(_silu(c_ref[...]), w_ref[...], _NN) + b_ref[...]

    return pl.pallas_call(
        body, name="ada_fwd", grid=(1536 // tn,), out_shape=jax.ShapeDtypeStruct((64, 1536), F32),
        in_specs=[pl.BlockSpec((64, D), lambda j: (0, 0)), pl.BlockSpec((D, tn), lambda j: (0, j)),
                  pl.BlockSpec((1, tn), lambda j: (0, j))],
        out_specs=pl.BlockSpec((64, tn), lambda j: (0, j)),
        compiler_params=_cp(vmem=VMEM_BIG),
    )(cin, w_ada_s, b_s)


def _ada_bwd(cin, gmod_s, w_ada_s):
    tn = 512

    def body(c_ref, g_ref, w_ref, gw_ref, pc_ref):
        j = pl.program_id(0)
        g = g_ref[...]
        gw_ref[...] = _dg(_silu(c_ref[...]), g, _TN)
        rr = lax.broadcasted_iota(jnp.int32, (8, 64), 0)
        cc = lax.broadcasted_iota(jnp.int32, (8, 64), 1)
        sel = jnp.where((rr == 0) & ((cc & 7) == 2), 1.0, 0.0).astype(F32)
        dctx = _hmm(sel, g)
        part = _dg(dctx, w_ref[...], _NT)

        @pl.when(j == 0)
        def _():
            pc_ref[...] = jnp.zeros_like(pc_ref)

        pc_ref[...] += part

    return pl.pallas_call(
        body, name="ada_bwd", grid=(1536 // tn,),
        out_shape=[jax.ShapeDtypeStruct((D, 1536), F32), jax.ShapeDtypeStruct((8, D), F32)],
        in_specs=[pl.BlockSpec((64, D), lambda j: (0, 0)), pl.BlockSpec((64, tn), lambda j: (0, j)),
                  pl.BlockSpec((D, tn), lambda j: (0, j))],
        out_specs=[pl.BlockSpec((D, tn), lambda j: (0, j)), pl.BlockSpec((8, D), lambda j: (0, 0))],
        compiler_params=_cp(vmem=VMEM_BIG),
    )(cin, gmod_s, w_ada_s)


def _ada_small(gmod, parts, c_ctx):
    def body(g_ref, p_ref, c_ref, gb_ref, gc_ref):
        gb_ref[...] = jnp.sum(g_ref[...], axis=0, keepdims=True)
        ds = p_ref[0][0:1] + p_ref[2][0:1] + p_ref[4][0:1] + p_ref[6][0:1]
        cc = c_ref[...]
        s = _sigmoid(cc)
        gc_ref[...] = ds * (s * (1.0 + cc * (1.0 - s)))

    vm = pl.BlockSpec(memory_space=pltpu.VMEM)
    return pl.pallas_call(
        body, name="ada_small",
        out_shape=[jax.ShapeDtypeStruct((1, 6 * D), F32), jax.ShapeDtypeStruct((1, D), F32)],
        in_specs=[vm, vm, vm], out_specs=[vm, vm],
    )(gmod, parts, c_ctx)


def _tok_specs():
    xs = pl.BlockSpec((None, RT, D), lambda b, j: (b, jnp.maximum(j - 1, 0), 0))
    cs = pl.BlockSpec((None, RT, D), lambda b, j: (b, 0, 0))
    ms = pl.BlockSpec((None, 1, D), lambda b, j: (b, 0, 0))
    cms = pl.BlockSpec((None, 1, D), lambda b, j: (0, 0, 0))
    return xs, cs, ms, cms


def _inproj_fwd(x, ctx, sh1, sc1, csh1, csc1, n1w, w_in_p):
    def body(x_ref, c_ref, sh_ref, sc_ref, csh_ref, csc_ref, w1_ref, w_ref, p_ref, hm_ref):
        is_ctx = pl.program_id(1) == 0
        xt = jnp.where(is_ctx, c_ref[...], x_ref[...])
        sh = jnp.where(is_ctx, csh_ref[...], sh_ref[...])
        sc = jnp.where(is_ctx, csc_ref[...], sc_ref[...])
        hb = _norm_mod(xt, w1_ref[...], sh, sc).astype(BF16)
        hm_ref[...] = hb
        p_ref[...] = jnp.dot(hb, w_ref[...], preferred_element_type=F32)

    xs, cs, ms, cms = _tok_specs()
    return pl.pallas_call(
        body, name="inproj_fwd", grid=(NB, T // RT),
        out_shape=[jax.ShapeDtypeStruct((NB, T, PC), F32), jax.ShapeDtypeStruct((NB, T, D), BF16)],
        in_specs=[xs, cs, ms, ms, cms, cms, pl.BlockSpec((1, D), lambda b, j: (0, 0)),
                  pl.BlockSpec((D, PC), lambda b, j: (0, 0))],
        out_specs=[pl.BlockSpec((None, RT, PC), lambda b, j: (b, j, 0)),
                   pl.BlockSpec((None, RT, D), lambda b, j: (b, j, 0))],
        compiler_params=_cp(vmem=VMEM_BIG),
    )(x, ctx, sh1, sc1, csh1, csc1, n1w, w_in_p)


def _inproj_bwd(dp_dn, dz, dp_na, dgt, w_in_p, x, ctx, sh1, sc1, csh1, csc1, n1w, dx1):
    def body(d1_ref, d2_ref, d3_ref, d4_ref, w_ref, x_ref, c_ref, sh_ref, sc_ref, csh_ref, csc_ref, w1_ref,
             dx1_ref, dpb_ref, gx_ref, dsh_ref, dsc_ref, dcsh_ref, dcsc_ref, dw1_ref):
        b, j = pl.program_id(0), pl.program_id(1)
        is_ctx = j == 0
        dp = jnp.concatenate([d1_ref[...], d2_ref[...], d3_ref[...], d4_ref[...]], axis=1).astype(BF16)
        dpb_ref[...] = dp
        dhm = lax.dot_general(dp, w_ref[...], _NT, preferred_element_type=F32)
        xt = jnp.where(is_ctx, c_ref[...], x_ref[...])
        sh = jnp.where(is_ctx, csh_ref[...], sh_ref[...])
        sc = jnp.where(is_ctx, csc_ref[...], sc_ref[...])
        _, vjp = jax.vjp(_norm_mod, xt, w1_ref[...], sh, sc)
        dxt, dw1, dsh, dsc = vjp(dhm)

        @pl.when((b == 0) & (j == 0))
        def _():
            dw1_ref[...] = jnp.zeros_like(dw1_ref)
            dcsh_ref[...] = jnp.zeros_like(dcsh_ref)
            dcsc_ref[...] = jnp.zeros_like(dcsc_ref)

        dw1_ref[...] += dw1

        @pl.when(is_ctx)
        def _():
            dcsh_ref[...] += dsh
            dcsc_ref[...] += dsc

        @pl.when(j == 1)
        def _():
            dsh_ref[...] = jnp.zeros_like(dsh_ref)
            dsc_ref[...] = jnp.zeros_like(dsc_ref)

        @pl.when(j >= 1)
        def _():
            dsh_ref[...] += dsh
            dsc_ref[...] += dsc
            gx_ref[...] = dx1_ref[...] + dxt

    xs, cs, ms, cms = _tok_specs()
    tsp = lambda w: pl.BlockSpec((None, RT, w), lambda b, j: (b, j, 0))
    one = pl.BlockSpec((1, D), lambda b, j: (0, 0))
    return pl.pallas_call(
        body, name="inproj_bwd", grid=(NB, T // RT),
        out_shape=[jax.ShapeDtypeStruct((NB, T, PC), BF16), jax.ShapeDtypeStruct((NB, SEQ, D), F32),
                   jax.ShapeDtypeStruct((NB, 1, D), F32), jax.ShapeDtypeStruct((NB, 1, D), F32),
                   jax.ShapeDtypeStruct((1, 1, D), F32), jax.ShapeDtypeStruct((1, 1, D), F32),
                   jax.ShapeDtypeStruct((1, D), F32)],
        in_specs=[tsp(1536), tsp(512), tsp(1536), tsp(128), pl.BlockSpec((D, PC), lambda b, j: (0, 0)),
                  xs, cs, ms, ms, cms, cms, one, xs],
        out_specs=[tsp(PC), xs, ms, ms, cms, cms, one],
        compiler_params=_cp(vmem=VMEM_BIG),
    )(dp_dn, dz, dp_na, dgt, w_in_p, x, ctx, sh1, sc1, csh1, csc1, n1w, dx1)


def _seg_valid(t, off):
    lo = jnp.where(t < CT, 0, CT)
    hi = jnp.where(t < CT, CT, T)
    return (t + off >= lo) & (t + off < hi)


def _shifted(u, off):
    t = lax.broadcasted_iota(jnp.int32, (T, 1), 0)
    s = u if off == 0 else pltpu.roll(u, (-off) % T, 0)
    return jnp.where(_seg_valid(t, off), s, 0.0)


def _conv_pre(u, w):
    acc = _shifted(u, -2) * w[0:1, :]
    for kk in range(1, 5):
        acc = acc + _shifted(u, kk - 2) * w[kk:kk + 1, :]
    return acc


def _act_q(pre):
    a = _silu(pre)
    return a * lax.rsqrt(jnp.sum(a * a, axis=-1, keepdims=True) + EPS) * (128.0 ** -0.5)


def _act_k(pre):
    a = _silu(pre)
    return a * lax.rsqrt(jnp.sum(a * a, axis=-1, keepdims=True) + EPS)


def _dn_prep_fwd(p, cw):
    def body(pq_ref, pk_ref, pv_ref, wq_ref, wk_ref, wv_ref, q_ref, k_ref, v_ref):
        q_ref[...] = _act_q(_conv_pre(pq_ref[...], wq_ref[...]))
        k_ref[...] = _act_k(_conv_pre(pk_ref[...], wk_ref[...]))
        v_ref[...] = _silu(_conv_pre(pv_ref[...], wv_ref[...]))

    ps = lambda o: pl.BlockSpec((None, T, 128), lambda b, h: (b, 0, o + h))
    ws = lambda o: pl.BlockSpec((8, 128), lambda b, h: (0, o + h))
    os_ = pl.BlockSpec((None, T, 128), lambda b, h: (b, 0, h))
    return pl.pallas_call(
        body, name="dn_prep_fwd", grid=(NB, DNH), out_shape=[jax.ShapeDtypeStruct((NB, T, DNW), F32)] * 3,
        in_specs=[ps(0), ps(4), ps(8), ws(0), ws(4), ws(8)], out_specs=[os_] * 3,
        compiler_params=_cp(vmem=VMEM_BIG),
    )(p, p, p, cw, cw, cw)


def _dn_prep_bwd(p, cw, dqs, dks, dvs):
    def body(pq_ref, pk_ref, pv_ref, wq_ref, wk_ref, wv_ref, dqf_ref, dqb_ref, dkf_ref, dkb_ref, dvf_ref, dvb_ref,
             oq_ref, ok_ref, ov_ref, gq_ref, gk_ref, gv_ref):
        b = pl.program_id(1)

        def one(p_ref, w_ref, df_ref, db_ref, act, o_ref, g_ref):
            u = p_ref[...]
            w = w_ref[...]
            _, vjp = jax.vjp(act, _conv_pre(u, w))
            (dpre,) = vjp(df_ref[...] + db_ref[...])
            du = _shifted(dpre, 2) * w[0:1, :]
            rows = [jnp.sum(dpre * _shifted(u, -2), axis=0, keepdims=True)]
            for kk in range(1, 5):
                du = du + _shifted(dpre, 2 - kk) * w[kk:kk + 1, :]
                rows.append(jnp.sum(dpre * _shifted(u, kk - 2), axis=0, keepdims=True))
            o_ref[...] = du
            gw = jnp.concatenate(rows + [jnp.zeros((3, 128), F32)], axis=0)

            @pl.when(b == 0)
            def _():
                g_ref[...] = gw

            @pl.when(b > 0)
            def _():
                g_ref[...] += gw

        one(pq_ref, wq_ref, dqf_ref, dqb_ref, _act_q, oq_ref, gq_ref)
        one(pk_ref, wk_ref, dkf_ref, dkb_ref, _act_k, ok_ref, gk_ref)
        one(pv_ref, wv_ref, dvf_ref, dvb_ref, _silu, ov_ref, gv_ref)

    ps = lambda o: pl.BlockSpec((None, T, 128), lambda h, b: (b, 0, o + h))
    ws = lambda o: pl.BlockSpec((8, 128), lambda h, b: (0, o + h))
    ds = pl.BlockSpec((None, T, 128), lambda h, b: (b, 0, h))
    outs = pl.pallas_call(
        body, name="dn_prep_bwd", grid=(DNH, NB),
        out_shape=[jax.ShapeDtypeStruct((NB, T, DNW), F32)] * 3 + [jax.ShapeDtypeStruct((8, DNW), F32)] * 3,
        in_specs=[ps(0), ps(4), ps(8), ws(0), ws(4), ws(8)] + [ds] * 6,
        out_specs=[ds] * 3
        + [pl.BlockSpec((8, 128), lambda h, b: (0, h))] * 3,
        compiler_params=_cp(vmem=VMEM_BIG),
    )(p, p, p, cw, cw, cw, *dqs, *dks, *dvs)
    return outs


def _gate_fn(a, prm):
    lane = lax.broadcasted_iota(jnp.int32, a.shape, 1)
    z = a + prm[1:2, :]
    sp = jnp.maximum(z, 0.0) + jnp.log(1.0 + jnp.exp(-jnp.abs(z)))
    g = -jnp.exp(prm[0:1, :]) * sp
    return jnp.where(lane < 8, g, jnp.where(lane < 16, _sigmoid(a), 0.0))


def _gates_fwd(p, prm):
    def body(a_ref, prm_ref, o_ref):
        o_ref[...] = _gate_fn(a_ref[...], prm_ref[...])

    return pl.pallas_call(
        body, name="gates_fwd", grid=(NB,), out_shape=jax.ShapeDtypeStruct((NB, T, 128), F32),
        in_specs=[pl.BlockSpec((None, T, 128), lambda b: (b, 0, COL_G // 128)), pl.BlockSpec((8, 128), lambda b: (0, 0))],
        out_specs=pl.BlockSpec((None, T, 128), lambda b: (b, 0, 0)),
    )(p, prm)


def _gates_bwd(p, prm, dgf, dgb):
    def body(a_ref, prm_ref, dgf_ref, dgb_ref, o_ref, dprm_ref):
        b = pl.program_id(0)
        tot = dgf_ref[...] + dgb_ref[...]
        _, vjp = jax.vjp(_gate_fn, a_ref[...], prm_ref[...])
        da, dprm = vjp(tot)
        o_ref[...] = da

        @pl.when(b == 0)
        def _():
            dprm_ref[...] = dprm

        @pl.when(b > 0)
        def _():
            dprm_ref[...] += dprm

    return pl.pallas_call(
        body, name="gates_bwd", grid=(NB,),
        out_shape=[jax.ShapeDtypeStruct((NB, T, 128), F32), jax.ShapeDtypeStruct((8, 128), F32)],
        in_specs=[pl.BlockSpec((None, T, 128), lambda b: (b, 0, COL_G // 128)), pl.BlockSpec((8, 128), lambda b: (0, 0)),
                  pl.BlockSpec((None, T, 128), lambda b: (b, 0, 0)), pl.BlockSpec((None, T, 128), lambda b: (b, 0, 0))],
        out_specs=[pl.BlockSpec((None, T, 128), lambda b: (b, 0, 0)), pl.BlockSpec((8, 128), lambda b: (0, 0))],
        compiler_params=_cp(vmem=VMEM_BIG),
    )(p, prm, dgf, dgb)


def _tri_inv_impl(m, md, m1, m2, eye):
    n = -(m * md)
    t = eye + n
    pw = n
    for _ in range(3):
        pw = _dg(pw, pw, _NN)
        t = t + _dg(t, pw, _NN)
    t = t - _dg(_dg(t, m * m1, _NN), t, _NN)
    t = t - _dg(_dg(t, m * m2, _NN), t, _NN)
    return t


@jax.custom_vjp
def _tri_inv(m, md, m1, m2, eye):
    return _tri_inv_impl(m, md, m1, m2, eye)


def _tri_inv_f(m, md, m1, m2, eye):
    t = _tri_inv_impl(m, md, m1, m2, eye)
    return t, (t, md)


def _tri_inv_b(res, g):
    t, md = res
    tt = t.T
    z = jnp.zeros_like(md)
    return (-_dg(_dg(tt, g, _NN), tt, _NN), z, z, z, z)


_tri_inv.defvjp(_tri_inv_f, _tri_inv_b)


SR = DNH * CH


def _scan_masks():
    i, j = np.arange(SR)[:, None], np.arange(SR)[None, :]
    same = (i >> 6) == (j >> 6)
    cm = np.zeros((2, 6, SR, SR), np.float32)
    for d in range(2):
        diff = i - j if d == 0 else j - i
        cm[d, 0] = same & (diff >= 0)
        cm[d, 1] = same & (diff > 0)
        cm[d, 2] = (i >> 4) == (j >> 4)
        cm[d, 3] = ((i >> 5) == (j >> 5)) & ((i >> 4) != (j >> 4))
        cm[d, 4] = (i >> 5) != (j >> 5)
        cm[d, 5] = i == j
    t, u = np.arange(CH)[:, None], np.arange(CH)[None, :]
    c64 = np.stack([t >= u, t <= u]).astype(np.float32)
    bm = ((np.arange(SR)[:, None] >> 6) == (np.arange(DNW)[None, :] >> 7)).astype(np.float32)
    return jnp.asarray(cm), jnp.asarray(c64), jnp.asarray(bm)


@jax.custom_vjp
def _fold(x, bm):
    xm = x * bm
    return xm[:, 0:128] + xm[:, 128:256] + xm[:, 256:384] + xm[:, 384:512]


def _fold_f(x, bm):
    return _fold(x, bm), bm


def _fold_b(bm, g):
    return jnp.concatenate([g, g, g, g], axis=1) * bm, jnp.zeros_like(bm)


_fold.defvjp(_fold_f, _fold_b)


def _chunk_fn(q, k, v, gb, s, cm, c64, bm, d):
    incl, strict = cm[0], cm[1]
    lane = lax.broadcasted_iota(jnp.int32, (1, 128), 1)
    oh = lambda n: jnp.where(lane == n, 1.0, 0.0).astype(F32)
    col = lambda x, base: jnp.concatenate(
        [jnp.sum(x * oh(base + d * DNH + h), axis=1, keepdims=True) for h in range(DNH)], axis=0)
    beta = col(gb, 8)
    gam = col(_mask_mm(c64, gb), 0)
    tot = jnp.sum(gb, axis=0, keepdims=True)
    gtots = [jnp.sum(tot * oh(d * DNH + h), axis=1, keepdims=True) for h in range(DNH)]
    gtot = jnp.concatenate([jnp.broadcast_to(t, (CH, 1)) for t in gtots], axis=0)
    eg = jnp.concatenate([jnp.broadcast_to(jnp.exp(t), (1, 128)) for t in gtots], axis=1)
    row0 = jnp.where(lax.broadcasted_iota(jnp.int32, (128, 1), 0) == 0, 1.0, 0.0).astype(F32)
    gam_row = jnp.sum(jnp.broadcast_to(gam, (SR, 128)).T * row0, axis=0, keepdims=True)
    dec = jnp.exp(jnp.where(incl > 0.5, gam - gam_row, NEG))
    e_gam = jnp.exp(gam)
    kb = k * beta
    m = strict * (_mm_nt(kb, k) * dec)
    tm = _tri_inv(m, cm[2], cm[3], cm[4], cm[5])
    u = _mm(tm, v * beta)
    w = _mm(tm, kb * e_gam)
    qk = _mm_nt(q, k) * dec
    v_new = u - _fold(_mm(w, s), bm)
    o = _fold(_mm(q * e_gam, s), bm) + _mm(qk, v_new)
    vbd = jnp.concatenate([v_new, v_new, v_new, v_new], axis=1) * bm
    s_new = s * eg + _mm_tn(k * jnp.exp(gtot - gam), vbd)
    return o, s_new


def _chunk_rev(s):
    return jnp.where(s < NCH_CTX, NCH_CTX - 1 - s, NCH + NCH_CTX - 1 - s)


def _scan_specs(step):
    cf = lambda s: step(s)
    cb = lambda s: _chunk_rev(step(s))
    tok = lambda c: pl.BlockSpec((None, CH, DNW), lambda b, s: (b, c(s), 0))
    gat = lambda c: pl.BlockSpec((None, CH, 128), lambda b, s: (b, c(s), 0))
    sta = lambda c: pl.BlockSpec((None, None, 128, DNW), lambda b, s: (b, c(s), 0, 0))
    return (tok(cf), gat(cf), sta(cf)), (tok(cb), gat(cb), sta(cb))


def _const_specs():
    return [pl.BlockSpec((2, 6, SR, SR), lambda b, s: (0, 0, 0, 0)), pl.BlockSpec((2, CH, CH), lambda b, s: (0, 0, 0)),
            pl.BlockSpec((SR, DNW), lambda b, s: (0, 0))]


def _stack(ref):
    return jnp.concatenate([ref[:, 128 * h:128 * h + 128] for h in range(DNH)], axis=0)


def _unstack_to(ref, val):
    for h in range(DNH):
        ref[:, 128 * h:128 * h + 128] = val[CH * h:CH * h + CH]


def _scan_fwd(q, k, v, gates):
    def body(cm_ref, c64_ref, bm_ref, qf, kf, vf, gf, qb, kb, vb, gb, of_ref, sf_ref, ob_ref, sb_ref, s_ref):
        @pl.when(pl.program_id(1) == 0)
        def _():
            s_ref[...] = jnp.zeros_like(s_ref)

        bm = bm_ref[...]
        dirs = ((qf, kf, vf, gf, of_ref, sf_ref), (qb, kb, vb, gb, ob_ref, sb_ref))
        for d, (q_r, k_r, v_r, g_r, o_r, st_r) in enumerate(dirs):
            st = s_ref[d]
            st_r[...] = st
            o, s_new = _chunk_fn(_stack(q_r), _stack(k_r), _stack(v_r), g_r[...], st, cm_ref[d], c64_ref[d], bm, d)
            _unstack_to(o_r, o)
            s_ref[d] = s_new

    (tf, gf_, sf_), (tb, gb_, sb_) = _scan_specs(lambda s: s)
    o_sh = jax.ShapeDtypeStruct((NB, T, DNW), F32)
    st_sh = jax.ShapeDtypeStruct((NB, NCH, 128, DNW), F32)
    return pl.pallas_call(
        body, name="scan_fwd", grid=(NB, NCH), out_shape=[o_sh, st_sh, o_sh, st_sh],
        in_specs=_const_specs() + [tf, tf, tf, gf_, tb, tb, tb, gb_], out_specs=[tf, sf_, tb, sb_],
        scratch_shapes=[pltpu.VMEM((2, 128, DNW), F32)],
        compiler_params=_cp(vmem=VMEM_BIG),
    )(*_scan_masks(), q, k, v, gates, q, k, v, gates)


def _scan_bwd(q, k, v, gates, st_f, st_b, do):
    def body(cm_ref, c64_ref, bm_ref, qf, kf, vf, gf, sf, dof, qb, kb, vb, gb, sb, dob,
             dqf, dkf, dvf, dgf, dqb, dkb, dvb, dgb, ds_ref):
        @pl.when(pl.program_id(1) == 0)
        def _():
            ds_ref[...] = jnp.zeros_like(ds_ref)

        bm = bm_ref[...]
        dirs = ((qf, kf, vf, gf, sf, dof, dqf, dkf, dvf, dgf), (qb, kb, vb, gb, sb, dob, dqb, dkb, dvb, dgb))
        for d, (q_r, k_r, v_r, g_r, st_r, do_r, dq_r, dk_r, dv_r, dg_r) in enumerate(dirs):
            cm, c64 = cm_ref[d], c64_ref[d]
            _, vjp = jax.vjp(lambda a, b_, c, e, f, cm=cm, c64=c64, d=d: _chunk_fn(a, b_, c, e, f, cm, c64, bm, d),
                             _stack(q_r), _stack(k_r), _stack(v_r), g_r[...], st_r[...])
            dq, dk, dv, dg, ds = vjp((_stack(do_r), ds_ref[d]))
            _unstack_to(dq_r, dq)
            _unstack_to(dk_r, dk)
            _unstack_to(dv_r, dv)
            dg_r[...] = dg
            ds_ref[d] = ds

    (tf, gf_, sf_), (tb, gb_, sb_) = _scan_specs(lambda s: NCH - 1 - s)
    o_sh = jax.ShapeDtypeStruct((NB, T, DNW), F32)
    g_sh = jax.ShapeDtypeStruct((NB, T, 128), F32)
    return pl.pallas_call(
        body, name="scan_bwd", grid=(NB, NCH), out_shape=[o_sh, o_sh, o_sh, g_sh] * 2,
        in_specs=_const_specs() + [tf, tf, tf, gf_, sf_, tf, tb, tb, tb, gb_, sb_, tb],
        out_specs=[tf, tf, tf, gf_, tb, tb, tb, gb_],
        scratch_shapes=[pltpu.VMEM((2, 128, DNW), F32)],
        compiler_params=_cp(vmem=VMEM_BIG),
    )(*_scan_masks(), q, k, v, gates, st_f, do, q, k, v, gates, st_b, do)


def _segmean(a, bd):
    hi = a.astype(BF16)
    lo = (a - hi.astype(F32)).astype(BF16)
    return jnp.dot(hi, bd, preferred_element_type=F32) + jnp.dot(lo, bd, preferred_element_type=F32)


def _na_prep_fwd(p, qw, kw, bd):
    def body(q_ref, k_ref, v_ref, qw_ref, kw_ref, bd_ref, oq_ref, ok_ref, ov_ref):
        bdm = bd_ref[...]
        q = q_ref[...]
        k = k_ref[...]
        oq_ref[...] = (q * lax.rsqrt(_segmean(q * q, bdm) + EPS) * qw_ref[...] * 0.125).astype(BF16)
        ok_ref[...] = (k * lax.rsqrt(_segmean(k * k, bdm) + EPS) * kw_ref[...]).astype(BF16)
        ov_ref[...] = v_ref[...].astype(BF16)

    ps = lambda o: pl.BlockSpec((None, RT, NAW), lambda b, j: (b, j, COL_NA // NAW + o))
    one = pl.BlockSpec((1, NAW), lambda b, j: (0, 0))
    os_ = pl.BlockSpec((None, RT, NAW), lambda b, j: (b, j, 0))
    return pl.pallas_call(
        body, name="na_prep_fwd", grid=(NB, T // RT), out_shape=[jax.ShapeDtypeStruct((NB, T, NAW), BF16)] * 3,
        in_specs=[ps(0), ps(1), ps(2), one, one, pl.BlockSpec((NAW, NAW), lambda b, j: (0, 0))], out_specs=[os_] * 3,
    )(p, p, p, qw, kw, bd)


def _na_prep_bwd(p, qw, kw, bd, dq, dk, dv):
    def body(q_ref, k_ref, qw_ref, kw_ref, bd_ref, dq_ref, dk_ref, dv_ref, o_ref, gq_ref, gk_ref):
        b, j = pl.program_id(0), pl.program_id(1)
        bdm = bd_ref[...]

        def one(x, w, dy, scale):
            r = lax.rsqrt(_segmean(x * x, bdm) + EPS)
            g = dy * w * scale
            dx = r * g - x * (r * r * r) * _segmean(g * x, bdm)
            dw = jnp.sum(dy * x * r * scale, axis=0, keepdims=True)
            return dx, dw

        dqv = jnp.where(j == 0, 0.0, dq_ref[...])
        dxq, dwq = one(q_ref[...], qw_ref[...], dqv, 0.125)
        dxk, dwk = one(k_ref[...], kw_ref[...], dk_ref[...], 1.0)
        o_ref[...] = jnp.concatenate([dxq, dxk, dv_ref[...]], axis=1)
        pad = jnp.zeros((7, NAW), F32)

        @pl.when((b == 0) & (j == 0))
        def _():
            gq_ref[...] = jnp.zeros_like(gq_ref)
            gk_ref[...] = jnp.zeros_like(gk_ref)

        gq_ref[...] += jnp.concatenate([dwq, pad], axis=0)
        gk_ref[...] += jnp.concatenate([dwk, pad], axis=0)

    ps = lambda o: pl.BlockSpec((None, RT, NAW), lambda b, j: (b, j, COL_NA // NAW + o))
    one = pl.BlockSpec((1, NAW), lambda b, j: (0, 0))
    ts = pl.BlockSpec((None, RT, NAW), lambda b, j: (b, j, 0))
    acc = pl.BlockSpec((8, NAW), lambda b, j: (0, 0))
    return pl.pallas_call(
        body, name="na_prep_bwd", grid=(NB, T // RT),
        out_shape=[jax.ShapeDtypeStruct((NB, T, 1536), F32), jax.ShapeDtypeStruct((8, NAW), F32),
                   jax.ShapeDtypeStruct((8, NAW), F32)],
        in_specs=[ps(0), ps(1), one, one, pl.BlockSpec((NAW, NAW), lambda b, j: (0, 0)),
                  pl.BlockSpec((None, RT, NAW), lambda b, j: (b, jnp.maximum(j - 1, 0), 0)), ts, ts],
        out_specs=[pl.BlockSpec((None, RT, 1536), lambda b, j: (b, j, 0)), acc, acc],
        compiler_params=_cp(vmem=VMEM_BIG),
    )(p, p, qw, kw, bd, dq, dk, dv)


def _na_row0(r):
    return jnp.clip(r - WIN_ROWS // 2, 0, ROWS - WIN_ROWS)


def _na_cls(r):
    return _na_row0(r) - r + WIN_ROWS - 1


def _na_scores(q, kw, kc, bias, hh):
    lane = lax.broadcasted_iota(jnp.int32, (1, 128), 1)
    hm = ((lane >> 6) == hh)
    qh = jnp.where(hm, q, jnp.zeros_like(q))
    s_w = lax.dot_general(qh, kw, _NT, preferred_element_type=F32) + bias
    s_c = lax.dot_general(qh, kc, _NT, preferred_element_type=F32)
    mx = jnp.maximum(jnp.max(s_w, axis=1, keepdims=True), jnp.max(s_c, axis=1, keepdims=True))
    p_w = jnp.exp(s_w - mx)
    p_c = jnp.exp(s_c - mx)
    inv = 1.0 / (jnp.sum(p_w, axis=1, keepdims=True) + jnp.sum(p_c, axis=1, keepdims=True))
    return hm, qh, p_w, p_c, inv


def _na_fwd(qn, kn, vb, bias):
    def body(q_ref, k_ref, v_ref, b_ref, o_ref):
        r = pl.program_id(2)
        start = pl.multiple_of(CT + _na_row0(r) * GRID_W, GRID_W)
        q = q_ref[...]
        kw, vw = k_ref[pl.ds(start, NWIN), :], v_ref[pl.ds(start, NWIN), :]
        kc, vc = k_ref[0:CT, :], v_ref[0:CT, :]
        out = jnp.zeros((GRID_W, 128), F32)
        for hh in range(2):
            hm, _, p_w, p_c, inv = _na_scores(q, kw, kc, b_ref[hh], hh)
            o = (jnp.dot(p_w.astype(BF16), vw, preferred_element_type=F32)
                 + jnp.dot(p_c.astype(BF16), vc, preferred_element_type=F32)) * inv
            out = jnp.where(hm, o, out)
        o_ref[...] = out.astype(BF16)

    kv = pl.BlockSpec((None, T, 128), lambda g, b, r: (b, 0, g))
    return pl.pallas_call(
        body, name="na_fwd", grid=(NAH // 2, NB, ROWS), out_shape=jax.ShapeDtypeStruct((NB, SEQ, NAW), BF16),
        in_specs=[pl.BlockSpec((None, GRID_W, 128), lambda g, b, r: (b, CT // GRID_W + r, g)), kv, kv,
                  pl.BlockSpec((None, 2, GRID_W, NWIN), lambda g, b, r: (_na_cls(r), g, 0, 0))],
        out_specs=pl.BlockSpec((None, GRID_W, 128), lambda g, b, r: (b, r, g)),
    )(qn, kn, vb, bias)


def _na_bwd(qn, kn, vb, bias, do):
    def body(q_ref, k_ref, v_ref, b_ref, do_ref, dq_ref, dk_ref, dv_ref, db_ref):
        b, r = pl.program_id(1), pl.program_id(2)

        @pl.when((b == 0) & (r == 0))
        def _():
            db_ref[...] = jnp.zeros_like(db_ref)

        @pl.when(r == 0)
        def _():
            dk_ref[...] = jnp.zeros_like(dk_ref)
            dv_ref[...] = jnp.zeros_like(dv_ref)

        start = pl.multiple_of(CT + _na_row0(r) * GRID_W, GRID_W)
        cls = _na_cls(r)
        q = q_ref[...]
        kw, vw = k_ref[pl.ds(start, NWIN), :], v_ref[pl.ds(start, NWIN), :]
        kc, vc = k_ref[0:CT, :], v_ref[0:CT, :]
        dout = do_ref[...]
        dq = jnp.zeros((GRID_W, 128), F32)
        dkw = jnp.zeros((NWIN, 128), F32)
        dvw = jnp.zeros((NWIN, 128), F32)
        dkc = jnp.zeros((CT, 128), F32)
        dvc = jnp.zeros((CT, 128), F32)
        for hh in range(2):
            hm, qh, p_w, p_c, inv = _na_scores(q, kw, kc, b_ref[hh], hh)
            p_w = p_w * inv
            p_c = p_c * inv
            doh = jnp.where(hm, dout, 0.0).astype(BF16)
            dp_w = lax.dot_general(doh, vw, _NT, preferred_element_type=F32)
            dp_c = lax.dot_general(doh, vc, _NT, preferred_element_type=F32)
            delta = jnp.sum(p_w * dp_w, axis=1, keepdims=True) + jnp.sum(p_c * dp_c, axis=1, keepdims=True)
            ds_w = p_w * (dp_w - delta)
            ds_c = p_c * (dp_c - delta)
            db_ref[cls, hh] += ds_w
            ds_wb, ds_cb = ds_w.astype(BF16), ds_c.astype(BF16)
            dqh = jnp.dot(ds_wb, kw, preferred_element_type=F32) + jnp.dot(ds_cb, kc, preferred_element_type=F32)
            dq = jnp.where(hm, dqh, dq)
            dkw = dkw + lax.dot_general(ds_wb, qh, _TN, preferred_element_type=F32)
            dkc = dkc + lax.dot_general(ds_cb, qh, _TN, preferred_element_type=F32)
            dvw = dvw + lax.dot_general(p_w.astype(BF16), doh, _TN, preferred_element_type=F32)
            dvc = dvc + lax.dot_general(p_c.astype(BF16), doh, _TN, preferred_element_type=F32)
        dq_ref[...] = dq
        dk_ref[pl.ds(start, NWIN), :] += dkw
        dv_ref[pl.ds(start, NWIN), :] += dvw
        dk_ref[0:CT, :] += dkc
        dv_ref[0:CT, :] += dvc

    kv = pl.BlockSpec((None, T, 128), lambda g, b, r: (b, 0, g))
    return pl.pallas_call(
        body, name="na_bwd", grid=(NAH // 2, NB, ROWS),
        out_shape=[jax.ShapeDtypeStruct((NB, SEQ, NAW), F32)] + [jax.ShapeDtypeStruct((NB, T, NAW), F32)] * 2
        + [jax.ShapeDtypeStruct((WIN_ROWS, NAH, GRID_W, NWIN), F32)],
        in_specs=[pl.BlockSpec((None, GRID_W, 128), lambda g, b, r: (b, CT // GRID_W + r, g)), kv, kv,
                  pl.BlockSpec((None, 2, GRID_W, NWIN), lambda g, b, r: (_na_cls(r), g, 0, 0)),
                  pl.BlockSpec((None, GRID_W, 128), lambda g, b, r: (b, r, g))],
        out_specs=[pl.BlockSpec((None, GRID_W, 128), lambda g, b, r: (b, r, g)), kv, kv,
                   pl.BlockSpec((WIN_ROWS, 2, GRID_W, NWIN), lambda g, b, r: (0, g, 0, 0))],
        compiler_params=_cp(vmem=VMEM_BIG),
    )(qn, kn, vb, bias, do)


def _na_tables():
    cols = np.arange(GRID_W)
    win_start = np.clip(cols - WIN_COLS // 2, 0, GRID_W - WIN_COLS)
    kc = cols[None, :]
    valid = (kc >= win_start[:, None]) & (kc < win_start[:, None] + WIN_COLS)
    rel = np.clip(kc - cols[:, None] + WIN_COLS - 1, 0, 2 * WIN_COLS - 2)
    return valid, rel


def _bias_table(rpb):
    valid, rel = _na_tables()
    t15 = jnp.where(valid[None, None], rpb[:, :, rel], NEG)
    tab = jnp.stack([t15[:, c:c + WIN_ROWS] for c in range(WIN_ROWS)], axis=0)
    return tab.transpose(0, 1, 3, 2, 4).reshape(WIN_ROWS, NAH, GRID_W, NWIN)


def _rpb_fold(db):
    def body1(d_ref, o_ref):
        for dr in range(2 * WIN_ROWS - 1):
            acc = None
            for c in range(WIN_ROWS):
                i = dr - c
                if 0 <= i < WIN_ROWS:
                    blk = d_ref[c, :, i * GRID_W:(i + 1) * GRID_W]
                    acc = blk if acc is None else acc + blk
            o_ref[dr] = acc

    t15 = pl.pallas_call(
        body1, name="rpb_fold_rows", grid=(NAH,),
        out_shape=jax.ShapeDtypeStruct((NAH, 2 * WIN_ROWS - 1, GRID_W, GRID_W), F32),
        in_specs=[pl.BlockSpec((WIN_ROWS, None, GRID_W, NWIN), lambda h: (0, h, 0, 0))],
        out_specs=pl.BlockSpec((None, 2 * WIN_ROWS - 1, GRID_W, GRID_W), lambda h: (h, 0, 0, 0)),
        compiler_params=_cp(vmem=VMEM_BIG),
    )(db)
    valid, rel = _na_tables()
    onehot = np.zeros((GRID_W * GRID_W, 128), np.float32)
    flat_rel, flat_valid = rel.reshape(-1), valid.reshape(-1)
    onehot[np.arange(GRID_W * GRID_W)[flat_valid], flat_rel[flat_valid]] = 1.0

    def body2(a_ref, oh_ref, o_ref):
        o_ref[...] = _hmm(a_ref[...], oh_ref[...])

    vm = pl.BlockSpec(memory_space=pltpu.VMEM)
    out = pl.pallas_call(
        body2, name="rpb_fold_cols", out_shape=jax.ShapeDtypeStruct((NAH * 15, 128), F32),
        in_specs=[vm, vm], out_specs=vm, compiler_params=_cp(vmem=VMEM_BIG),
    )(t15.reshape(NAH * 15, GRID_W * GRID_W), jnp.asarray(onehot))
    return out[:, :31].reshape(NAH, 15, 31)


def _dn_out(o, z, w):
    outs = []
    for h in range(DNH):
        sl = slice(128 * h, 128 * h + 128)
        oh = o[:, sl]
        r = lax.rsqrt(jnp.mean(oh * oh, axis=-1, keepdims=True) + EPS)
        outs.append(oh * r * w[:, sl] * _silu(z[:, sl]))
    return jnp.concatenate(outs, axis=1)


def _outproj_fwd(of, ob, p, na, onw, x, g1, w_out):
    def body(of_ref, ob_ref, z_ref, na_ref, w_ref, x_ref, g_ref, wo_ref, x1_ref, ai_ref, ao_ref):
        dn = _dn_out(of_ref[...] + ob_ref[...], z_ref[...], w_ref[...])
        ai = jnp.concatenate([dn.astype(BF16), na_ref[...]], axis=1)
        ai_ref[...] = ai
        ao = jnp.dot(ai, wo_ref[...], preferred_element_type=F32)
        ao_ref[...] = ao
        x1_ref[...] = x_ref[...] + g_ref[...] * ao

    lat = lambda w: pl.BlockSpec((None, RT, w), lambda b, j: (b, j, 0))
    return pl.pallas_call(
        body, name="outproj_fwd", grid=(NB, SEQ // RT),
        out_shape=[jax.ShapeDtypeStruct((NB, SEQ, D), F32), jax.ShapeDtypeStruct((NB, SEQ, D), BF16),
                   jax.ShapeDtypeStruct((NB, SEQ, D), F32)],
        in_specs=[pl.BlockSpec((None, RT, DNW), lambda b, j: (b, j + 1, 0)),
                  pl.BlockSpec((None, RT, DNW), lambda b, j: (b, j + 1, 0)),
                  pl.BlockSpec((None, RT, DNW), lambda b, j: (b, j + 1, COL_Z // DNW)),
                  lat(NAW), pl.BlockSpec((1, DNW), lambda b, j: (0, 0)), lat(D),
                  pl.BlockSpec((None, 1, D), lambda b, j: (b, 0, 0)), pl.BlockSpec((D, D), lambda b, j: (0, 0))],
        out_specs=[lat(D), lat(D), lat(D)],
        compiler_params=_cp(vmem=VMEM_BIG),
    )(of, ob, p, na, onw, x, g1, w_out)


def _outproj_bwd(dx1, ao, g1, w_out, of, ob, p, onw):
    def body(dx_ref, ao_ref, g_ref, wo_ref, of_ref, ob_ref, z_ref, w_ref, dao_ref, do_ref, dz_ref, dna_ref, dg_ref, dw_ref):
        b, j = pl.program_id(0), pl.program_id(1)

        @pl.when((b == 0) & (j == 0))
        def _():
            dw_ref[...] = jnp.zeros_like(dw_ref)

        @pl.when(j == 0)
        def _():
            do_ref[...] = jnp.zeros_like(do_ref)
            dz_ref[...] = jnp.zeros_like(dz_ref)

        @pl.when(j == 1)
        def _():
            dg_ref[...] = jnp.zeros_like(dg_ref)

        @pl.when(j >= 1)
        def _():
            dx = dx_ref[...]
            dg_ref[...] += jnp.sum(dx * ao_ref[...], axis=0, keepdims=True)
            dao = (g_ref[...] * dx).astype(BF16)
            dao_ref[...] = dao
            dai = lax.dot_general(dao, wo_ref[...], _NT, preferred_element_type=F32)
            dna_ref[...] = dai[:, DNW:]
            _, vjp = jax.vjp(_dn_out, of_ref[...] + ob_ref[...], z_ref[...], w_ref[...])
            do, dz, dw = vjp(dai[:, :DNW])
            do_ref[...] = do
            dz_ref[...] = dz
            dw_ref[...] += jnp.concatenate([dw, jnp.zeros((7, DNW), F32)], axis=0)

    lat = lambda w: pl.BlockSpec((None, RT, w), lambda b, j: (b, jnp.maximum(j - 1, 0), 0))
    tok = lambda w: pl.BlockSpec((None, RT, w), lambda b, j: (b, j, 0))
    return pl.pallas_call(
        body, name="outproj_bwd", grid=(NB, T // RT),
        out_shape=[jax.ShapeDtypeStruct((NB, SEQ, D), BF16), jax.ShapeDtypeStruct((NB, T, DNW), F32),
                   jax.ShapeDtypeStruct((NB, T, DNW), F32), jax.ShapeDtypeStruct((NB, SEQ, NAW), F32),
                   jax.ShapeDtypeStruct((NB, 1, D), F32), jax.ShapeDtypeStruct((8, DNW), F32)],
        in_specs=[lat(D), lat(D), pl.BlockSpec((None, 1, D), lambda b, j: (b, 0, 0)),
                  pl.BlockSpec((D, D), lambda b, j: (0, 0)),
                  tok(DNW), tok(DNW),
                  pl.BlockSpec((None, RT, DNW), lambda b, j: (b, j, COL_Z // DNW)),
                  pl.BlockSpec((1, DNW), lambda b, j: (0, 0))],
        out_specs=[lat(D), tok(DNW), tok(DNW), lat(NAW), pl.BlockSpec((None, 1, D), lambda b, j: (b, 0, 0)),
                   pl.BlockSpec((8, DNW), lambda b, j: (0, 0))],
        compiler_params=_cp(vmem=VMEM_BIG),
    )(dx1, ao, g1, w_out, of, ob, p, onw)


FT = 256
NFT = NTOK // FT
TPE = SEQ // FT
NFF = DFF // FFT


def _ffn_specs():
    row = lambda w: pl.BlockSpec((FT, w), lambda i, f: (i, 0))
    ex = pl.BlockSpec((None, 1, D), lambda i, f: (i // TPE, 0, 0))
    one = pl.BlockSpec((1, D), lambda i, f: (0, 0))
    wg = pl.BlockSpec((None, D, FFT), lambda i, f: (f, 0, 0))
    wu = pl.BlockSpec((None, D, FFT), lambda i, f: (NFF + f, 0, 0))
    wo = pl.BlockSpec((FFT, D), lambda i, f: (f, 0))
    ff = pl.BlockSpec((FT, FFT), lambda i, f: (i, f))
    return row, ex, one, wg, wu, wo, ff


def _ffn_fwd(x1, sh2, sc2, g2, n2w, w_in4, w_out, target):
    def body(x_ref, sh_ref, sc_ref, g_ref, w2_ref, wg_ref, wu_ref, wo_ref, t_ref,
             h2_ref, gt_ref, up_ref, dy_ref, dg_ref, loss_ref, acc_ref):
        i, f = pl.program_id(0), pl.program_id(1)

        @pl.when(f == 0)
        def _():
            h2_ref[...] = _norm_mod(x_ref[...], w2_ref[...], sh_ref[...], sc_ref[...]).astype(BF16)
            acc_ref[...] = jnp.zeros_like(acc_ref)

        h2 = h2_ref[...]
        gate = jnp.dot(h2, wg_ref[...], preferred_element_type=F32)
        up = jnp.dot(h2, wu_ref[...], preferred_element_type=F32)
        gt_ref[...] = gate.astype(BF16)
        up_ref[...] = up.astype(BF16)
        acc_ref[...] += jnp.dot((_silu(gate) * up).astype(BF16), wo_ref[...], preferred_element_type=F32)

        @pl.when((i == 0) & (f == 0))
        def _():
            loss_ref[...] = jnp.zeros_like(loss_ref)

        @pl.when(f == NFF - 1)
        def _():
            ffn = acc_ref[...]
            err = x_ref[...] + g_ref[...] * ffn - t_ref[...]
            dy = err * (1.0 / D)
            dy_ref[...] = dy
            part = jnp.sum(dy * ffn, axis=0, keepdims=True)
            loss_ref[...] += jnp.sum(jnp.sum(err * err, axis=1, keepdims=True), axis=0, keepdims=True) * (0.5 / D)

            @pl.when(i % TPE == 0)
            def _():
                dg_ref[...] = part

            @pl.when(i % TPE != 0)
            def _():
                dg_ref[...] += part

    row, ex, one, wg, wu, wo, ff = _ffn_specs()
    return pl.pallas_call(
        body, name="ffn_fwd", grid=(NFT, NFF),
        out_shape=[jax.ShapeDtypeStruct((NTOK, D), BF16), jax.ShapeDtypeStruct((NTOK, DFF), BF16),
                   jax.ShapeDtypeStruct((NTOK, DFF), BF16), jax.ShapeDtypeStruct((NTOK, D), F32),
                   jax.ShapeDtypeStruct((NB, 1, D), F32), jax.ShapeDtypeStruct((8, 128), F32)],
        in_specs=[row(D), ex, ex, ex, one, wg, wu, wo, row(D)],
        out_specs=[row(D), ff, ff, row(D), ex, pl.BlockSpec((8, 128), lambda i, f: (0, 0))],
        scratch_shapes=[pltpu.VMEM((FT, D), F32)],
        compiler_params=_cp(vmem=VMEM_BIG),
    )(x1, sh2, sc2, g2, n2w, w_in4, w_in4, w_out, target)


def _ffn_bwd(dy, gate, up, x1, sh2, sc2, g2, n2w, w_in4, w_out):
    def body(dy_ref, gt_ref, up_ref, x_ref, sh_ref, sc_ref, g_ref, w2_ref, wg_ref, wu_ref, wo_ref,
             dff_ref, act_ref, dgu_ref, dx_ref, dsh_ref, dsc_ref, dw_ref, acc_ref):
        i, f = pl.program_id(0), pl.program_id(1)

        @pl.when(f == 0)
        def _():
            dff_ref[...] = (g_ref[...] * dy_ref[...]).astype(BF16)
            acc_ref[...] = jnp.zeros_like(acc_ref)

        dact = lax.dot_general(dff_ref[...], wo_ref[...], _NT, preferred_element_type=F32)
        gate, up = gt_ref[...].astype(F32), up_ref[...].astype(F32)
        sg = _sigmoid(gate)
        sl = gate * sg
        act_ref[...] = (sl * up).astype(BF16)
        dgate = (dact * up * (sg * (1.0 + gate * (1.0 - sg)))).astype(BF16)
        dup = (dact * sl).astype(BF16)
        dgu_ref[0] = dgate
        dgu_ref[1] = dup
        acc_ref[...] += (lax.dot_general(dgate, wg_ref[...], _NT, preferred_element_type=F32)
                         + lax.dot_general(dup, wu_ref[...], _NT, preferred_element_type=F32))

        @pl.when((i == 0) & (f == 0))
        def _():
            dw_ref[...] = jnp.zeros_like(dw_ref)

        @pl.when(f == NFF - 1)
        def _():
            _, vjp = jax.vjp(_norm_mod, x_ref[...], w2_ref[...], sh_ref[...], sc_ref[...])
            dx, dw, dsh, dsc = vjp(acc_ref[...])
            dx_ref[...] = dy_ref[...] + dx
            dw_ref[...] += dw

            @pl.when(i % TPE == 0)
            def _():
                dsh_ref[...] = dsh
                dsc_ref[...] = dsc

            @pl.when(i % TPE != 0)
            def _():
                dsh_ref[...] += dsh
                dsc_ref[...] += dsc

    row, ex, one, wg, wu, wo, ff = _ffn_specs()
    return pl.pallas_call(
        body, name="ffn_bwd", grid=(NFT, NFF),
        out_shape=[jax.ShapeDtypeStruct((NTOK, D), BF16), jax.ShapeDtypeStruct((NTOK, DFF), BF16),
                   jax.ShapeDtypeStruct((2, NTOK, DFF), BF16), jax.ShapeDtypeStruct((NTOK, D), F32),
                   jax.ShapeDtypeStruct((NB, 1, D), F32), jax.ShapeDtypeStruct((NB, 1, D), F32),
                   jax.ShapeDtypeStruct((1, D), F32)],
        in_specs=[row(D), ff, ff, row(D), ex, ex, ex, one, wg, wu, wo],
        out_specs=[row(D), ff, pl.BlockSpec((2, FT, FFT), lambda i, f: (0, i, f)), row(D), ex, ex, one],
        scratch_shapes=[pltpu.VMEM((FT, D), F32)],
        compiler_params=_cp(vmem=VMEM_BIG),
    )(dy, gate, up, x1, sh2, sc2, g2, n2w, w_in4, w_in4, w_out)


def _local_step(x, ctx, target, mod_x, mod_c, norm1_w, norm2_w, w_in_p, conv_w, a_log, dt_bias, out_norm_w,
                q_norm_w, k_norm_w, rpb, w_out_b, w_ffn_in4, w_ffn_out_b):
    sh1, sc1, g1, sh2, sc2, g2 = [mod_x[:, i * D:(i + 1) * D].reshape(NB, 1, D) for i in range(6)]
    csh1, csc1 = mod_c[0:D].reshape(1, 1, D), mod_c[D:2 * D].reshape(1, 1, D)
    cw = jnp.concatenate([conv_w, jnp.zeros((3, 1536), F32)], axis=0)
    prm = jnp.zeros((8, 128), F32).at[0, :8].set(a_log.reshape(8)).at[1, :8].set(dt_bias.reshape(8))
    onw = jnp.tile(out_norm_w.reshape(1, 128), (1, DNH))
    qw = jnp.tile(q_norm_w.reshape(1, 64), (1, NAH))
    kw = jnp.tile(k_norm_w.reshape(1, 64), (1, NAH))
    seg = np.arange(NAW) // 64
    bd = jnp.asarray((seg[:, None] == seg[None, :]).astype(np.float32) / 64.0, dtype=BF16)
    bias = _bias_table(rpb)

    p, hm = _inproj_fwd(x, ctx, sh1, sc1, csh1, csc1, norm1_w, w_in_p)
    dq_, dk_, dv_ = _dn_prep_fwd(p, cw)
    gates = _gates_fwd(p, prm)
    o_f, st_f, o_b, st_b = _scan_fwd(dq_, dk_, dv_, gates)
    qn, kn, vb = _na_prep_fwd(p, qw, kw, bd)
    na = _na_fwd(qn, kn, vb, bias)
    x1, ai, ao = _outproj_fwd(o_f, o_b, p, na, onw, x, g1, w_out_b)
    x1f = x1.reshape(NTOK, D)
    h2, gate, up, dy, dg2, loss8 = _ffn_fwd(x1f, sh2, sc2, g2, norm2_w, w_ffn_in4, w_ffn_out_b,
                                             target.reshape(NTOK, D))
    dff, act, dgu, dx1, dsh2, dsc2, dn2w = _ffn_bwd(dy, gate, up, x1f, sh2, sc2, g2, norm2_w, w_ffn_in4, w_ffn_out_b)
    g_w_ffn_out = _matmul_tn(act, dff, FFT, D, 512, "wgrad_ffn_out")
    g_w_ffn_in4 = _matmul_tn(
        h2, dgu, 512, FFT, 512, "wgrad_ffn_in", nn=2 * NFF,
        b_block=(None, 512, FFT), b_map=lambda i, j, k: (j // NFF, k, j % NFF),
        out_shape=(4, D, FFT), out_block=(None, 512, FFT), out_map=lambda i, j, k: (j, i, 0))
    dao, do, dz, dna, dg1, donw = _outproj_bwd(dx1.reshape(NB, SEQ, D), ao, g1, w_out_b, o_f, o_b, p, onw)
    g_w_out = _matmul_tn(ai.reshape(NTOK, D), dao.reshape(NTOK, D), 512, D, 512, "wgrad_out")
    dqn, dkn, dvn, dbias = _na_bwd(qn, kn, vb, bias, dna)
    dp_na, dqw, dkw = _na_prep_bwd(p, qw, kw, bd, dqn, dkn, dvn)
    dqf, dkf, dvf, dgf, dqb, dkb, dvb, dgb = _scan_bwd(dq_, dk_, dv_, gates, st_f, st_b, do)
    dpq, dpk, dpv, gcq, gck, gcv = _dn_prep_bwd(p, cw, (dqf, dqb), (dkf, dkb), (dvf, dvb))
    dgt, dprm = _gates_bwd(p, prm, dgf, dgb)
    dp_dn = jnp.concatenate([dpq, dpk, dpv], axis=2)
    dpb, grad_x, dsh1, dsc1, dcsh1, dcsc1, dn1w = _inproj_bwd(
        dp_dn, dz, dp_na, dgt, w_in_p, x, ctx, sh1, sc1, csh1, csc1, norm1_w, dx1.reshape(NB, SEQ, D))
    g_w_in_p = _matmul_tn(hm.reshape(NB * T, D), dpb.reshape(NB * T, PC), 256, PC, 512, "wgrad_in")
    g_rpb = _rpb_fold(dbias)

    dmod_x = jnp.concatenate([dsh1, dsc1, dg1, dsh2, dsc2, dg2], axis=2).reshape(NB, 6 * D)
    dmod_c = jnp.concatenate([dcsh1.reshape(D), dcsc1.reshape(D), jnp.zeros((4 * D,), F32)])
    g_conv = jnp.concatenate([gcq[:5], gck[:5], gcv[:5]], axis=1)
    small = dict(
        norm1_w=dn1w, norm2_w=dn2w, a_log=dprm[0, :8], dt_bias=dprm[1, :8],
        out_norm_w=donw[0].reshape(DNH, 128).sum(0), q_norm_w=dqw[0].reshape(NAH, 64).sum(0),
        k_norm_w=dkw[0].reshape(NAH, 64).sum(0), rpb=g_rpb, conv_w=g_conv)
    return loss8[0, 0], grad_x, dmod_x, dmod_c, small, g_w_in_p, g_w_out, g_w_ffn_in4, g_w_ffn_out


def _perm_w_in(w):
    return jnp.concatenate([w[:, :2048], w[:, 2064:3600], w[:, 2048:2064],
                            jnp.zeros((w.shape[0], PC - IN_COLS), w.dtype)], axis=1)


def _unperm_w_in(g):
    return jnp.concatenate([g[:, :2048], g[:, COL_G:COL_G + 16], g[:, 2048:COL_G]], axis=1)


def _pack_small(c_ctx, n1, n2, b_ada, a_log, dt_bias, onw, qnw, knw, rpb, conv_s):
    misc = jnp.zeros((D,), F32)
    misc = misc.at[0:8].set(a_log.reshape(8)).at[8:16].set(dt_bias.reshape(8))
    misc = misc.at[128:256].set(onw.reshape(128)).at[256:320].set(qnw.reshape(64)).at[320:384].set(knw.reshape(64))
    rp = jnp.concatenate([rpb.reshape(-1), jnp.zeros((4 * D - 3720,), F32)])
    cv = jnp.concatenate([conv_s.reshape(-1), jnp.zeros((2 * D - 1920,), F32)])
    return jnp.concatenate([c_ctx.reshape(-1), n1.reshape(-1), n2.reshape(-1), b_ada.reshape(-1), misc, rp, cv]).reshape(16, D)


def _unpack_small(a):
    f = a.reshape(-1)
    misc = f[9 * D:10 * D]
    return dict(
        c_ctx=f[0:D], norm1_w=f[D:2 * D].reshape(1, D), norm2_w=f[2 * D:3 * D].reshape(1, D),
        b_ada=f[3 * D:9 * D].reshape(1, 6 * D), dn_A_log=misc[0:8].reshape(1, 2, 4), dn_dt_bias=misc[8:16].reshape(1, 2, 4),
        dn_out_norm_w=misc[128:256].reshape(1, 128), na_q_norm_w=misc[256:320].reshape(1, 64),
        na_k_norm_w=misc[320:384].reshape(1, 64), na_rpb=f[10 * D:10 * D + 3720].reshape(1, 8, 15, 31),
        dn_conv_w=f[14 * D:14 * D + 1920].reshape(1, 5, 384))


_WEIGHTS = ['c_ctx', 'norm1_w', 'norm2_w', 'w_ada', 'b_ada', 'w_in', 'dn_conv_w', 'dn_A_log', 'dn_dt_bias',
            'dn_out_norm_w', 'na_q_norm_w', 'na_k_norm_w', 'na_rpb', 'w_out', 'w_ffn_in', 'w_ffn_out']


def kernel(x, c, ctx, c_ctx, norm1_w, norm2_w, w_ada, b_ada, w_in, dn_conv_w, dn_A_log, dn_dt_bias, dn_out_norm_w, na_q_norm_w, na_k_norm_w, na_rpb, w_out, w_ffn_in, w_ffn_out, loss_target, m_c_ctx, m_norm1_w, m_norm2_w, m_w_ada, m_b_ada, m_w_in, m_dn_conv_w, m_dn_A_log, m_dn_dt_bias, m_dn_out_norm_w, m_na_q_norm_w, m_na_k_norm_w, m_na_rpb, m_w_out, m_w_ffn_in, m_w_ffn_out, v_c_ctx, v_norm1_w, v_norm2_w, v_w_ada, v_b_ada, v_w_in, v_dn_conv_w, v_dn_A_log, v_dn_dt_bias, v_dn_out_norm_w, v_na_q_norm_w, v_na_k_norm_w, v_na_rpb, v_w_out, v_w_ffn_in, v_w_ffn_out):
    ix, iy, ic = lax.axis_index("x"), lax.axis_index("y"), lax.axis_index("c")
    chip = 2 * ix + iy
    me = 4 * ix + 2 * iy + ic

    crow = jnp.concatenate([c, c_ctx.reshape(1, D), jnp.zeros((5, D), F32)], axis=0)
    cvrow = jnp.concatenate([dn_conv_w[0].reshape(-1), jnp.zeros((2 * D - 1920,), F32)]).reshape(2, D)
    g0 = _allgather8(jnp.concatenate([crow, cvrow, jnp.zeros((6, D), F32)], axis=0), "gather_cond")
    cin = g0[:, :8].reshape(64, D)
    conv_full = jnp.concatenate([g0[2 * s, 8:10].reshape(-1)[:1920].reshape(5, 384) for s in range(4)], axis=1)

    w_ada_s = w_ada[0]
    b_s = lax.dynamic_slice(b_ada, (0, chip * 1536), (1, 1536))
    mod_s = _ada_fwd(cin, w_ada_s, b_s)
    mods = _allgather8(mod_s, "gather_mod")
    mod_all = jnp.concatenate([mods[0], mods[2], mods[4], mods[6]], axis=1)
    mod_mine = lax.dynamic_slice(mod_all, (8 * me, 0), (8, 6 * D))
    mod_x, mod_c = mod_mine[0:NB], mod_mine[2]

    def half(w2d):
        r = w2d.shape[0] // 2
        return lax.dynamic_slice(w2d, (ic * r, 0), (r, w2d.shape[1]))

    halves = [_cast_bf16(half(w_in[0]), "cast_w_in"), _cast_bf16(half(w_out[0]), "cast_w_out"),
              _cast_bf16(half(w_ffn_in[0]), "cast_w_ffn_in"), _cast_bf16(half(w_ffn_out[0]), "cast_w_ffn_out")]
    gw_in, gw_out, gw_ffn_in, gw_ffn_out = _gather_weights(halves)
    w_in_full = gw_in.reshape(4, D, 900).transpose(1, 0, 2).reshape(D, IN_COLS)
    w_in_p = _perm_w_in(w_in_full)
    w_out_b = gw_out.reshape(D, D)
    w_ffn_in4 = gw_ffn_in.reshape(4, D, FFT)
    w_ffn_out_b = gw_ffn_out.reshape(DFF, D)

    (loss_part, grad_x, dmod_x, dmod_c, small, g_w_in_p, g_w_out, g_w_ffn_in4, g_w_ffn_out) = _local_step(
        x, ctx, loss_target, mod_x, mod_c, norm1_w, norm2_w, w_in_p, conv_full, dn_A_log[0], dn_dt_bias[0],
        dn_out_norm_w[0], na_q_norm_w[0], na_k_norm_w[0], na_rpb[0], w_out_b, w_ffn_in4, w_ffn_out_b)
    loss = lax.psum(loss_part, ("x", "y", "c"))

    dm = jnp.concatenate([dmod_x, dmod_c.reshape(1, 6 * D), jnp.zeros((5, 6 * D), F32)], axis=0).reshape(48, D)
    misc = jnp.zeros((D,), F32)
    misc = misc.at[0:8].set(small["a_log"]).at[8:16].set(small["dt_bias"]).at[128:256].set(small["out_norm_w"])
    misc = misc.at[256:320].set(small["q_norm_w"]).at[320:384].set(small["k_norm_w"])
    sm = jnp.concatenate([
        small["norm1_w"].reshape(-1), small["norm2_w"].reshape(-1), misc,
        small["rpb"].reshape(-1), jnp.zeros((4 * D - 3720,), F32),
        small["conv_w"].reshape(-1), jnp.zeros((8 * D - 7680,), F32), jnp.zeros((D,), F32)]).reshape(16, D)
    gsl, gtot = _allgather8(jnp.concatenate([dm, sm], axis=0), "gather_small", with_sum=True)
    gmod = gsl[:, :48].reshape(64, 6 * D)
    st = gtot[48:64].reshape(-1)
    g_n1, g_n2, g_misc = st[0:D].reshape(1, D), st[D:2 * D].reshape(1, D), st[2 * D:3 * D]
    g_rpb = st[3 * D:3 * D + 3720].reshape(1, 8, 15, 31)
    g_conv_full = st[7 * D:7 * D + 7680].reshape(5, 1536)
    g_conv_s = lax.dynamic_slice(g_conv_full, (0, chip * 384), (5, 384)).reshape(1, 5, 384)

    gmod_s = lax.dynamic_slice(gmod, (0, chip * 1536), (64, 1536))
    g_w_ada_s, pctx = _ada_bwd(cin, gmod_s, w_ada_s)
    parts = _allgather8(pctx, "gather_cctx")
    g_b_ada, g_c_ctx = _ada_small(gmod, parts, c_ctx.reshape(1, D))

    g_w_in_full = _unperm_w_in(g_w_in_p)
    g_w_in8 = g_w_in_full.reshape(D, 4, 900).transpose(1, 0, 2).reshape(8, D // 2, 900)
    names = ["w_in", "w_out", "w_ffn_in", "w_ffn_out"]
    rs = _scatter_grads([g_w_in8, g_w_out.reshape(8, D // 8, D), g_w_ffn_in4.reshape(8, D // 2, FFT),
                         g_w_ffn_out.reshape(8, DFF // 8, D)])
    full = _sibling_exchange([_sum_slots(r, "sum_" + nm) for r, nm in zip(rs, names)])
    g_w_in_s = full[0].reshape(1, D, 900)
    g_w_out_s = full[1].reshape(1, D // 4, D)
    g_w_ffn_in_s = full[2].reshape(1, D, FFT)
    g_w_ffn_out_s = full[3].reshape(1, DFF // 4, D)

    grads = dict(
        c_ctx=g_c_ctx.reshape(D), norm1_w=g_n1, norm2_w=g_n2, w_ada=g_w_ada_s.reshape(1, D, 1536), b_ada=g_b_ada,
        w_in=g_w_in_s, dn_conv_w=g_conv_s, dn_A_log=g_misc[0:8].reshape(1, 2, 4), dn_dt_bias=g_misc[8:16].reshape(1, 2, 4),
        dn_out_norm_w=g_misc[128:256].reshape(1, 128), na_q_norm_w=g_misc[256:320].reshape(1, 64),
        na_k_norm_w=g_misc[320:384].reshape(1, 64), na_rpb=g_rpb, w_out=g_w_out_s, w_ffn_in=g_w_ffn_in_s,
        w_ffn_out=g_w_ffn_out_s)
    pk = lambda d_: _pack_small(d_["c_ctx"], d_["norm1_w"], d_["norm2_w"], d_["b_ada"], d_["dn_A_log"], d_["dn_dt_bias"],
                                d_["dn_out_norm_w"], d_["na_q_norm_w"], d_["na_k_norm_w"], d_["na_rpb"], d_["dn_conv_w"])
    wts = dict(c_ctx=c_ctx, norm1_w=norm1_w, norm2_w=norm2_w, b_ada=b_ada, dn_A_log=dn_A_log, dn_dt_bias=dn_dt_bias,
               dn_out_norm_w=dn_out_norm_w, na_q_norm_w=na_q_norm_w, na_k_norm_w=na_k_norm_w, na_rpb=na_rpb, dn_conv_w=dn_conv_w)
    ms = dict(c_ctx=m_c_ctx, norm1_w=m_norm1_w, norm2_w=m_norm2_w, b_ada=m_b_ada, dn_A_log=m_dn_A_log, dn_dt_bias=m_dn_dt_bias,
              dn_out_norm_w=m_dn_out_norm_w, na_q_norm_w=m_na_q_norm_w, na_k_norm_w=m_na_k_norm_w, na_rpb=m_na_rpb,
              dn_conv_w=m_dn_conv_w)
    vs = dict(c_ctx=v_c_ctx, norm1_w=v_norm1_w, norm2_w=v_norm2_w, b_ada=v_b_ada, dn_A_log=v_dn_A_log, dn_dt_bias=v_dn_dt_bias,
              dn_out_norm_w=v_dn_out_norm_w, na_q_norm_w=v_na_q_norm_w, na_k_norm_w=v_na_k_norm_w, na_rpb=v_na_rpb,
              dn_conv_w=v_dn_conv_w)
    sd, snm, snv, _ = _adamw(pk(wts), pk(grads), pk(ms), pk(vs), "adamw_small")
    delta, new_m, new_v = _unpack_small(sd), _unpack_small(snm), _unpack_small(snv)
    big = dict(w_ada=(w_ada, m_w_ada, v_w_ada), w_in=(w_in, m_w_in, v_w_in), w_out=(w_out, m_w_out, v_w_out),
               w_ffn_in=(w_ffn_in, m_w_ffn_in, v_w_ffn_in), w_ffn_out=(w_ffn_out, m_w_ffn_out, v_w_ffn_out))
    for nm, (w_, m_, v_) in big.items():
        shp = w_.shape
        r2 = lambda a: a.reshape(shp[1], shp[2])
        d_, nm_, nv_, g_ = _adamw(r2(w_), r2(grads[nm]), r2(m_), r2(v_), "adamw_" + nm)
        delta[nm], new_m[nm], new_v[nm], grads[nm] = d_.reshape(shp), nm_.reshape(shp), nv_.reshape(shp), g_.reshape(shp)

    return (loss, grad_x, *[grads[n] for n in _WEIGHTS], *[delta[n] for n in _WEIGHTS],
            *[new_m[n] for n in _WEIGHTS], *[new_v[n] for n in _WEIGHTS])
```

```python
import functools

import numpy as np
import jax
import jax.numpy as jnp
from jax import lax
from jax.experimental import pallas as pl
from jax.experimental.pallas import tpu as pltpu

F32 = jnp.float32
BF16 = jnp.bfloat16

D = 1024
NB = 2
SEQ = 2048
CT = 256
T = CT + SEQ
NTOK = NB * SEQ
CH = 64
NCH = T // CH
NCH_CTX = CT // CH
DNH, DNW = 4, 512
NAH, NAW = 8, 512
GRID_W = 64
ROWS = SEQ // GRID_W
WIN_ROWS = 8
WIN_COLS = 16
NWIN = WIN_ROWS * GRID_W
DFF = 2816
FFT = 1408
IN_COLS = 3600
PC = 3712
COL_Z, COL_NA, COL_G = 1536, 2048, 3584
EPS = 1e-6
NEG = -1e30
RT = 256
VMEM_BIG = 56 * 1024 * 1024
MESH = pl.DeviceIdType.MESH

ADAM_LR, ADAM_B1, ADAM_B2, ADAM_EPS, ADAM_WD, ADAM_STEP = 0.001, 0.9, 0.999, 1e-08, 0.01, 10


def _cp(vmem=None, sem=None):
    kw = {}
    if vmem is not None:
        kw["vmem_limit_bytes"] = vmem
    if sem is not None:
        kw["dimension_semantics"] = sem
    return pltpu.CompilerParams(**kw)


_NN = (((1,), (0,)), ((), ()))
_NT = (((1,), (1,)), ((), ()))
_TN = (((0,), (0,)), ((), ()))


def _dg(a, b, dims):
    return lax.dot_general(a.astype(BF16), b.astype(BF16), dims, preferred_element_type=F32)


@jax.custom_vjp
def _mm(a, b):
    return _dg(a, b, _NN)


_mm.defvjp(lambda a, b: (_dg(a, b, _NN), (a, b)),
           lambda r, g: (_dg(g, r[1], _NT), _dg(r[0], g, _TN)))


@jax.custom_vjp
def _mm_nt(a, b):
    return _dg(a, b, _NT)


_mm_nt.defvjp(lambda a, b: (_dg(a, b, _NT), (a, b)),
              lambda r, g: (_dg(g, r[1], _NN), _dg(g, r[0], _TN)))


@jax.custom_vjp
def _mm_tn(a, b):
    return _dg(a, b, _TN)


_mm_tn.defvjp(lambda a, b: (_dg(a, b, _TN), (a, b)),
              lambda r, g: (_dg(r[1], g, _NT), _dg(r[0], g, _NN)))


def _split_dot(a, b, dims):
    hi = b.astype(BF16)
    lo = (b - hi.astype(F32)).astype(BF16)
    ab = a.astype(BF16)
    return (lax.dot_general(ab, hi, dims, preferred_element_type=F32)
            + lax.dot_general(ab, lo, dims, preferred_element_type=F32))


@jax.custom_vjp
def _mask_mm(mask, b):
    return _split_dot(mask, b, _NN)


_mask_mm.defvjp(lambda mask, b: (_split_dot(mask, b, _NN), mask),
                lambda mask, g: (jnp.zeros_like(mask), _split_dot(mask, g, _TN)))


def _hmm(a, b):
    return jnp.dot(a, b, precision=lax.Precision.HIGHEST, preferred_element_type=F32)


def _sigmoid(x):
    return 1.0 / (1.0 + jnp.exp(-x))


def _silu(x):
    return x * _sigmoid(x)


def _norm_mod(x, w, sh, sc):
    r = lax.rsqrt(jnp.mean(x * x, axis=-1, keepdims=True) + EPS)
    return (x * r * w) * (1.0 + sc) + sh


def _my_pos():
    return lax.axis_index("x"), lax.axis_index("y"), lax.axis_index("c")


def _flip(pos, k):
    x, y, c = pos
    return (1 - x if k & 4 else x, 1 - y if k & 2 else y, 1 - c if k & 1 else c)


def _slot(pos):
    return 4 * pos[0] + 2 * pos[1] + pos[2]


def _allgather8(v, name, with_sum=False):
    R, W = v.shape

    def body(v_ref, out_ref, *rest):
        if with_sum:
            tot_ref, send_sems, recv_sems, lsem = rest
        else:
            send_sems, recv_sems, lsem = rest
        me = _my_pos()
        mine = pltpu.make_async_copy(v_ref, out_ref.at[_slot(me)], lsem)
        mine.start()
        sends = []
        for k in range(1, 8):
            cp = pltpu.make_async_remote_copy(
                src_ref=v_ref, dst_ref=out_ref.at[_slot(me)], send_sem=send_sems.at[k - 1],
                recv_sem=recv_sems.at[k - 1], device_id=_flip(me, k), device_id_type=MESH)
            cp.start()
            sends.append(cp)
        for k in range(1, 8):
            peer = _flip(me, k)
            pltpu.make_async_remote_copy(
                src_ref=v_ref, dst_ref=out_ref.at[_slot(peer)], send_sem=send_sems.at[k - 1],
                recv_sem=recv_sems.at[k - 1], device_id=peer, device_id_type=MESH).wait_recv()
        for cp in sends:
            cp.wait_send()
        mine.wait()
        if with_sum:
            acc = out_ref[0]
            for s in range(1, 8):
                acc = acc + out_ref[s]
            tot_ref[...] = acc

    out_shape = [jax.ShapeDtypeStruct((8, R, W), F32)]
    out_specs = [pl.BlockSpec(memory_space=pltpu.VMEM)]
    if with_sum:
        out_shape.append(jax.ShapeDtypeStruct((R, W), F32))
        out_specs.append(pl.BlockSpec(memory_space=pltpu.VMEM))
    res = pl.pallas_call(
        body, name=name, out_shape=out_shape,
        in_specs=[pl.BlockSpec(memory_space=pltpu.VMEM)], out_specs=out_specs,
        scratch_shapes=[pltpu.SemaphoreType.DMA((7,)), pltpu.SemaphoreType.DMA((7,)), pltpu.SemaphoreType.DMA],
        compiler_params=_cp(vmem=VMEM_BIG),
    )(v)
    return res if with_sum else res[0]


def _gather_weights(halves):
    n = len(halves)

    def body(*refs):
        ins, outs = refs[:n], refs[n:2 * n]
        send_sems, recv_sems, lsems = refs[2 * n:]
        me = _my_pos()
        x, y, c = me
        sib = (x, y, 1 - c)
        chips = [4, 2, 6]

        def cp(a, k, block, to, src=None):
            dst = outs[a].at[_slot(block)]
            return pltpu.make_async_remote_copy(
                src_ref=dst if src is None else src, dst_ref=dst, send_sem=send_sems.at[a, k],
                recv_sem=recv_sems.at[a, k], device_id=to, device_id_type=MESH)

        mine, first, passed = [], [], []
        for a in range(n):
            m = pltpu.make_async_copy(ins[a], outs[a].at[_slot(me)], lsems.at[a])
            m.start()
            mine.append(m)
            first.append(cp(a, 0, me, sib, src=ins[a]))
            for j, k in enumerate(chips):
                first.append(cp(a, 1 + j, me, _flip(me, k), src=ins[a]))
        for f in first:
            f.start()
        for j, k in enumerate(chips):
            for a in range(n):
                cp(a, 1 + j, _flip(me, k), me).wait_recv()
                p = cp(a, 4 + j, _flip(me, k), sib)
                p.start()
                passed.append(p)
        for a in range(n):
            cp(a, 0, sib, me).wait_recv()
            for j, k in enumerate(chips):
                cp(a, 4 + j, _flip(sib, k), me).wait_recv()
        for f in first + passed:
            f.wait_send()
        for m in mine:
            m.wait()

    hbm = pl.BlockSpec(memory_space=pltpu.HBM)
    return pl.pallas_call(
        body, name="gather_weights",
        out_shape=[jax.ShapeDtypeStruct((8,) + h.shape, h.dtype) for h in halves],
        in_specs=[hbm] * n, out_specs=[hbm] * n,
        scratch_shapes=[pltpu.SemaphoreType.DMA((n, 7)), pltpu.SemaphoreType.DMA((n, 7)),
                        pltpu.SemaphoreType.DMA((n,))],
    )(*halves)


def _xfer_start(plan, sems):
    send_sems, recv_sems, lsems = sems
    locals_, xfers = plan
    for i, (src, dst) in enumerate(locals_):
        pltpu.make_async_copy(src, dst, lsems.at[i]).start()
    for i, (src, dst, _, peer) in enumerate(xfers):
        pltpu.make_async_remote_copy(src_ref=src, dst_ref=dst, send_sem=send_sems.at[i], recv_sem=recv_sems.at[i],
                                     device_id=peer, device_id_type=MESH).start()


def _xfer_wait(plan, sems):
    send_sems, recv_sems, lsems = sems
    locals_, xfers = plan
    for i, (src, _, landing, peer) in enumerate(xfers):
        pltpu.make_async_remote_copy(src_ref=src, dst_ref=landing, send_sem=send_sems.at[i], recv_sem=recv_sems.at[i],
                                     device_id=peer, device_id_type=MESH).wait_recv()
    for i, (src, dst, _, peer) in enumerate(xfers):
        pltpu.make_async_remote_copy(src_ref=src, dst_ref=dst, send_sem=send_sems.at[i], recv_sem=recv_sems.at[i],
                                     device_id=peer, device_id_type=MESH).wait_send()
    for i, (src, dst) in enumerate(locals_):
        pltpu.make_async_copy(src, dst, lsems.at[i]).wait()


def _xfer_sems(n_local, n_xfer):
    return [pltpu.SemaphoreType.DMA((n_xfer,)), pltpu.SemaphoreType.DMA((n_xfer,)), pltpu.SemaphoreType.DMA((n_local,))]


def _scatter_plan(ins, outs, pos):
    me = _slot(pos)
    locals_, xfers = [], []
    for src, dst in zip(ins, outs):
        locals_.append((src.at[me], dst.at[me]))
        for k in range(1, 8):
            peer = _flip(pos, k)
            xfers.append((src.at[_slot(peer)], dst.at[me], dst.at[_slot(peer)], peer))
    return locals_, xfers


def _scatter_grads(grads):
    n = len(grads)

    def body(*refs):
        plan = _scatter_plan(refs[:n], refs[n:2 * n], _my_pos())
        _xfer_start(plan, refs[2 * n:])
        _xfer_wait(plan, refs[2 * n:])

    hbm = pl.BlockSpec(memory_space=pltpu.HBM)
    return pl.pallas_call(
        body, name="scatter_grads", out_shape=[jax.ShapeDtypeStruct(g.shape, g.dtype) for g in grads],
        in_specs=[hbm] * n, out_specs=[hbm] * n, scratch_shapes=_xfer_sems(n, 7 * n),
    )(*grads)


def _sibling_exchange(halves):
    n = len(halves)

    def body(*refs):
        ins, outs = refs[:n], refs[n:2 * n]
        send_sems, recv_sems, lsems = refs[2 * n:]
        x, y, c = _my_pos()
        sib = (x, y, 1 - c)
        mine, sends = [], []
        for a in range(n):
            m = pltpu.make_async_copy(ins[a], outs[a].at[c], lsems.at[a])
            m.start()
            mine.append(m)
            cp = pltpu.make_async_remote_copy(
                src_ref=ins[a], dst_ref=outs[a].at[c], send_sem=send_sems.at[a], recv_sem=recv_sems.at[a],
                device_id=sib, device_id_type=MESH)
            cp.start()
            sends.append(cp)
        for a in range(n):
            pltpu.make_async_remote_copy(
                src_ref=ins[a], dst_ref=outs[a].at[1 - c], send_sem=send_sems.at[a], recv_sem=recv_sems.at[a],
                device_id=sib, device_id_type=MESH).wait_recv()
        for cp in sends:
            cp.wait_send()
        for m in mine:
            m.wait()

    vm = pl.BlockSpec(memory_space=pltpu.VMEM)
    return pl.pallas_call(
        body, name="sibling_exchange",
        out_shape=[jax.ShapeDtypeStruct((2,) + h.shape, h.dtype) for h in halves],
        in_specs=[vm] * n, out_specs=[vm] * n,
        scratch_shapes=[pltpu.SemaphoreType.DMA((n,)), pltpu.SemaphoreType.DMA((n,)), pltpu.SemaphoreType.DMA((n,))],
        compiler_params=_cp(vmem=VMEM_BIG),
    )(*halves)


def _row_tile(r, cap=512):
    for t in (cap, 256, 128, 64, 32, 16, 8):
        if t <= cap and r % t == 0:
            return t
    return r


def _cast_bf16(a, name):
    R, W = a.shape
    tr = _row_tile(R, 256)

    def body(a_ref, o_ref):
        o_ref[...] = a_ref[...].astype(BF16)

    return pl.pallas_call(
        body, name=name, grid=(R // tr,), out_shape=jax.ShapeDtypeStruct((R, W), BF16),
        in_specs=[pl.BlockSpec((tr, W), lambda i: (i, 0))], out_specs=pl.BlockSpec((tr, W), lambda i: (i, 0)),
    )(a)


def _sum_slots(r, name):
    _, R, W = r.shape
    tr = _row_tile(R, 128)

    def body(r_ref, o_ref):
        acc = r_ref[0].astype(F32)
        for s_ in range(1, 8):
            acc = acc + r_ref[s_].astype(F32)
        o_ref[...] = acc.astype(BF16)

    return pl.pallas_call(
        body, name=name, grid=(R // tr,), out_shape=jax.ShapeDtypeStruct((R, W), BF16),
        in_specs=[pl.BlockSpec((8, tr, W), lambda i: (0, i, 0))], out_specs=pl.BlockSpec((tr, W), lambda i: (i, 0)),
    )(r)


def _adamw(w, g, m, v, name):
    R, W = w.shape
    tr = _row_tile(R, 256)
    c1 = 1.0 / (1.0 - ADAM_B1 ** ADAM_STEP)
    c2 = 1.0 / (1.0 - ADAM_B2 ** ADAM_STEP)

    def body(w_ref, g_ref, m_ref, v_ref, d_ref, nm_ref, nv_ref, g32_ref):
        gg = g_ref[...].astype(F32)
        g32_ref[...] = gg
        nm = ADAM_B1 * m_ref[...] + (1.0 - ADAM_B1) * gg
        nv = ADAM_B2 * v_ref[...] + (1.0 - ADAM_B2) * (gg * gg)
        d_ref[...] = -ADAM_LR * ((nm * c1) / (jnp.sqrt(nv * c2) + ADAM_EPS) + ADAM_WD * w_ref[...])
        nm_ref[...] = nm
        nv_ref[...] = nv

    spec = pl.BlockSpec((tr, W), lambda i: (i, 0))
    return pl.pallas_call(
        body, name=name, grid=(R // tr,), out_shape=[jax.ShapeDtypeStruct((R, W), F32)] * 4,
        in_specs=[spec] * 4, out_specs=[spec] * 4,
    )(w, g, m, v)


def _matmul_tn(a, b, tm, tn, tk, name, nn=1, b_map=None, out_shape=None, out_block=None, out_map=None,
               b_block=None):
    K, M = a.shape
    nk = K // tk
    nm = M // tm

    def body(a_ref, b_ref, o_ref, acc_ref):
        kk = pl.program_id(2)

        @pl.when(kk == 0)
        def _():
            acc_ref[...] = jnp.zeros_like(acc_ref)

        acc_ref[...] += lax.dot_general(a_ref[...], b_ref[...], _TN, preferred_element_type=F32)

        @pl.when(kk == nk - 1)
        def _():
            o_ref[...] = acc_ref[...].astype(BF16)

    return pl.pallas_call(
        body, name=name, grid=(nm, nn, nk),
        out_shape=jax.ShapeDtypeStruct(out_shape or (M, b.shape[-1]), BF16),
        in_specs=[pl.BlockSpec((tk, tm), lambda i, j, k: (k, i)),
                  pl.BlockSpec(b_block or (tk, tn), b_map or (lambda i, j, k: (k, j)))],
        out_specs=pl.BlockSpec(out_block or (tm, tn), out_map or (lambda i, j, k: (i, j))),
        scratch_shapes=[pltpu.VMEM((tm, tn), F32)],
        compiler_params=_cp(vmem=VMEM_BIG),
    )(a, b)


def _ada_fwd(cin, w_ada_s, b_s):
    tn = 512

    def body(c_ref, w_ref, b_ref, o_ref):
        o_ref[...] = _dg(_silu(c_ref[...]), w_ref[...], _NN) + b_ref[...]

    return pl.pallas_call(
        body, name="ada_fwd", grid=(1536 // tn,), out_shape=jax.ShapeDtypeStruct((64, 1536), F32),
        in_specs=[pl.BlockSpec((64, D), lambda j: (0, 0)), pl.BlockSpec((D, tn), lambda j: (0, j)),
                  pl.BlockSpec((1, tn), lambda j: (0, j))],
        out_specs=pl.BlockSpec((64, tn), lambda j: (0, j)),
        compiler_params=_cp(vmem=VMEM_BIG),
    )(cin, w_ada_s, b_s)


def _ada_bwd(cin, gmod_s, w_ada_s):
    tn = 512

    def body(c_ref, g_ref, w_ref, gw_ref, pc_ref):
        j = pl.program_id(0)
        g = g_ref[...]
        gw_ref[...] = _dg(_silu(c_ref[...]), g, _TN)
        rr = lax.broadcasted_iota(jnp.int32, (8, 64), 0)
        cc = lax.broadcasted_iota(jnp.int32, (8, 64), 1)
        sel = jnp.where((rr == 0) & ((cc & 7) == 2), 1.0, 0.0).astype(F32)
        dctx = _hmm(sel, g)
        part = _dg(dctx, w_ref[...], _NT)

        @pl.when(j == 0)
        def _():
            pc_ref[...] = jnp.zeros_like(pc_ref)

        pc_ref[...] += part

    return pl.pallas_call(
        body, name="ada_bwd", grid=(1536 // tn,),
        out_shape=[jax.ShapeDtypeStruct((D, 1536), F32), jax.ShapeDtypeStruct((8, D), F32)],
        in_specs=[pl.BlockSpec((64, D), lambda j: (0, 0)), pl.BlockSpec((64, tn), lambda j: (0, j)),
                  pl.BlockSpec((D, tn), lambda j: (0, j))],
        out_specs=[pl.BlockSpec((D, tn), lambda j: (0, j)), pl.BlockSpec((8, D), lambda j: (0, 0))],
        compiler_params=_cp(vmem=VMEM_BIG),
    )(cin, gmod_s, w_ada_s)


def _ada_small(gmod, parts, c_ctx):
    def body(g_ref, p_ref, c_ref, gb_ref, gc_ref):
        gb_ref[...] = jnp.sum(g_ref[...], axis=0, keepdims=True)
        ds = p_ref[0][0:1] + p_ref[2][0:1] + p_ref[4][0:1] + p_ref[6][0:1]
        cc = c_ref[...]
        s = _sigmoid(cc)
        gc_ref[...] = ds * (s * (1.0 + cc * (1.0 - s)))

    vm = pl.BlockSpec(memory_space=pltpu.VMEM)
    return pl.pallas_call(
        body, name="ada_small",
        out_shape=[jax.ShapeDtypeStruct((1, 6 * D), F32), jax.ShapeDtypeStruct((1, D), F32)],
        in_specs=[vm, vm, vm], out_specs=[vm, vm],
    )(gmod, parts, c_ctx)


def _tok_specs():
    xs = pl.BlockSpec((None, RT, D), lambda b, j: (b, jnp.maximum(j - 1, 0), 0))
    cs = pl.BlockSpec((None, RT, D), lambda b, j: (b, 0, 0))
    ms = pl.BlockSpec((None, 1, D), lambda b, j: (b, 0, 0))
    cms = pl.BlockSpec((None, 1, D), lambda b, j: (0, 0, 0))
    return xs, cs, ms, cms


def _inproj_fwd(x, ctx, sh1, sc1, csh1, csc1, n1w, w_in_p):
    def body(x_ref, c_ref, sh_ref, sc_ref, csh_ref, csc_ref, w1_ref, w_ref, p_ref, hm_ref):
        is_ctx = pl.program_id(1) == 0
        xt = jnp.where(is_ctx, c_ref[...], x_ref[...])
        sh = jnp.where(is_ctx, csh_ref[...], sh_ref[...])
        sc = jnp.where(is_ctx, csc_ref[...], sc_ref[...])
        hb = _norm_mod(xt, w1_ref[...], sh, sc).astype(BF16)
        hm_ref[...] = hb
        p_ref[...] = jnp.dot(hb, w_ref[...], preferred_element_type=F32)

    xs, cs, ms, cms = _tok_specs()
    return pl.pallas_call(
        body, name="inproj_fwd", grid=(NB, T // RT),
        out_shape=[jax.ShapeDtypeStruct((NB, T, PC), F32), jax.ShapeDtypeStruct((NB, T, D), BF16)],
        in_specs=[xs, cs, ms, ms, cms, cms, pl.BlockSpec((1, D), lambda b, j: (0, 0)),
                  pl.BlockSpec((D, PC), lambda b, j: (0, 0))],
        out_specs=[pl.BlockSpec((None, RT, PC), lambda b, j: (b, j, 0)),
                   pl.BlockSpec((None, RT, D), lambda b, j: (b, j, 0))],
        compiler_params=_cp(vmem=VMEM_BIG),
    )(x, ctx, sh1, sc1, csh1, csc1, n1w, w_in_p)


def _inproj_bwd(dp_dn, dz, dp_na, dgt, w_in_p, x, ctx, sh1, sc1, csh1, csc1, n1w, dx1):
    def body(d1_ref, d2_ref, d3_ref, d4_ref, w_ref, x_ref, c_ref, sh_ref, sc_ref, csh_ref, csc_ref, w1_ref,
             dx1_ref, dpb_ref, gx_ref, dsh_ref, dsc_ref, dcsh_ref, dcsc_ref, dw1_ref):
        b, j = pl.program_id(0), pl.program_id(1)
        is_ctx = j == 0
        dp = jnp.concatenate([d1_ref[...], d2_ref[...], d3_ref[...], d4_ref[...]], axis=1).astype(BF16)
        dpb_ref[...] = dp
        dhm = lax.dot_general(dp, w_ref[...], _NT, preferred_element_type=F32)
        xt = jnp.where(is_ctx, c_ref[...], x_ref[...])
        sh = jnp.where(is_ctx, csh_ref[...], sh_ref[...])
        sc = jnp.where(is_ctx, csc_ref[...], sc_ref[...])
        _, vjp = jax.vjp(_norm_mod, xt, w1_ref[...], sh, sc)
        dxt, dw1, dsh, dsc = vjp(dhm)

        @pl.when((b == 0) & (j == 0))
        def _():
            dw1_ref[...] = jnp.zeros_like(dw1_ref)
            dcsh_ref[...] = jnp.zeros_like(dcsh_ref)
            dcsc_ref[...] = jnp.zeros_like(dcsc_ref)

        dw1_ref[...] += dw1

        @pl.when(is_ctx)
        def _():
            dcsh_ref[...] += dsh
            dcsc_ref[...] += dsc

        @pl.when(j == 1)
        def _():
            dsh_ref[...] = jnp.zeros_like(dsh_ref)
            dsc_ref[...] = jnp.zeros_like(dsc_ref)

        @pl.when(j >= 1)
        def _():
            dsh_ref[...] += dsh
            dsc_ref[...] += dsc
            gx_ref[...] = dx1_ref[...] + dxt

    xs, cs, ms, cms = _tok_specs()
    tsp = lambda w: pl.BlockSpec((None, RT, w), lambda b, j: (b, j, 0))
    one = pl.BlockSpec((1, D), lambda b, j: (0, 0))
    return pl.pallas_call(
        body, name="inproj_bwd", grid=(NB, T // RT),
        out_shape=[jax.ShapeDtypeStruct((NB, T, PC), BF16), jax.ShapeDtypeStruct((NB, SEQ, D), F32),
                   jax.ShapeDtypeStruct((NB, 1, D), F32), jax.ShapeDtypeStruct((NB, 1, D), F32),
                   jax.ShapeDtypeStruct((1, 1, D), F32), jax.ShapeDtypeStruct((1, 1, D), F32),
                   jax.ShapeDtypeStruct((1, D), F32)],
        in_specs=[tsp(1536), tsp(512), tsp(1536), tsp(128), pl.BlockSpec((D, PC), lambda b, j: (0, 0)),
                  xs, cs, ms, ms, cms, cms, one, xs],
        out_specs=[tsp(PC), xs, ms, ms, cms, cms, one],
        compiler_params=_cp(vmem=VMEM_BIG),
    )(dp_dn, dz, dp_na, dgt, w_in_p, x, ctx, sh1, sc1, csh1, csc1, n1w, dx1)


def _seg_valid(t, off):
    lo = jnp.where(t < CT, 0, CT)
    hi = jnp.where(t < CT, CT, T)
    return (t + off >= lo) & (t + off < hi)


def _shifted(u, off):
    t = lax.broadcasted_iota(jnp.int32, (T, 1), 0)
    s = u if off == 0 else pltpu.roll(u, (-off) % T, 0)
    return jnp.where(_seg_valid(t, off), s, 0.0)


def _conv_pre(u, w):
    acc = _shifted(u, -2) * w[0:1, :]
    for kk in range(1, 5):
        acc = acc + _shifted(u, kk - 2) * w[kk:kk + 1, :]
    return acc


def _act_q(pre):
    a = _silu(pre)
    return a * lax.rsqrt(jnp.sum(a * a, axis=-1, keepdims=True) + EPS) * (128.0 ** -0.5)


def _act_k(pre):
    a = _silu(pre)
    return a * lax.rsqrt(jnp.sum(a * a, axis=-1, keepdims=True) + EPS)


def _dn_prep_fwd(p, cw):
    def body(pq_ref, pk_ref, pv_ref, wq_ref, wk_ref, wv_ref, q_ref, k_ref, v_ref):
        q_ref[...] = _act_q(_conv_pre(pq_ref[...], wq_ref[...]))
        k_ref[...] = _act_k(_conv_pre(pk_ref[...], wk_ref[...]))
        v_ref[...] = _silu(_conv_pre(pv_ref[...], wv_ref[...]))

    ps = lambda o: pl.BlockSpec((None, T, 128), lambda b, h: (b, 0, o + h))
    ws = lambda o: pl.BlockSpec((8, 128), lambda b, h: (0, o + h))
    os_ = pl.BlockSpec((None, T, 128), lambda b, h: (b, 0, h))
    return pl.pallas_call(
        body, name="dn_prep_fwd", grid=(NB, DNH), out_shape=[jax.ShapeDtypeStruct((NB, T, DNW), F32)] * 3,
        in_specs=[ps(0), ps(4), ps(8), ws(0), ws(4), ws(8)], out_specs=[os_] * 3,
        compiler_params=_cp(vmem=VMEM_BIG),
    )(p, p, p, cw, cw, cw)


def _dn_prep_bwd(p, cw, dqs, dks, dvs):
    def body(pq_ref, pk_ref, pv_ref, wq_ref, wk_ref, wv_ref, dqf_ref, dqb_ref, dkf_ref, dkb_ref, dvf_ref, dvb_ref,
             oq_ref, ok_ref, ov_ref, gq_ref, gk_ref, gv_ref):
        b = pl.program_id(1)

        def one(p_ref, w_ref, df_ref, db_ref, act, o_ref, g_ref):
            u = p_ref[...]
            w = w_ref[...]
            _, vjp = jax.vjp(act, _conv_pre(u, w))
            (dpre,) = vjp(df_ref[...] + db_ref[...])
            du = _shifted(dpre, 2) * w[0:1, :]
            rows = [jnp.sum(dpre * _shifted(u, -2), axis=0, keepdims=True)]
            for kk in range(1, 5):
                du = du + _shifted(dpre, 2 - kk) * w[kk:kk + 1, :]
                rows.append(jnp.sum(dpre * _shifted(u, kk - 2), axis=0, keepdims=True))
            o_ref[...] = du
            gw = jnp.concatenate(rows + [jnp.zeros((3, 128), F32)], axis=0)

            @pl.when(b == 0)
            def _():
                g_ref[...] = gw

            @pl.when(b > 0)
            def _():
                g_ref[...] += gw

        one(pq_ref, wq_ref, dqf_ref, dqb_ref, _act_q, oq_ref, gq_ref)
        one(pk_ref, wk_ref, dkf_ref, dkb_ref, _act_k, ok_ref, gk_ref)
        one(pv_ref, wv_ref, dvf_ref, dvb_ref, _silu, ov_ref, gv_ref)

    ps = lambda o: pl.BlockSpec((None, T, 128), lambda h, b: (b, 0, o + h))
    ws = lambda o: pl.BlockSpec((8, 128), lambda h, b: (0, o + h))
    ds = pl.BlockSpec((None, T, 128), lambda h, b: (b, 0, h))
    outs = pl.pallas_call(
        body, name="dn_prep_bwd", grid=(DNH, NB),
        out_shape=[jax.ShapeDtypeStruct((NB, T, DNW), F32)] * 3 + [jax.ShapeDtypeStruct((8, DNW), F32)] * 3,
        in_specs=[ps(0), ps(4), ps(8), ws(0), ws(4), ws(8)] + [ds] * 6,
        out_specs=[ds] * 3
        + [pl.BlockSpec((8, 128), lambda h, b: (0, h))] * 3,
        compiler_params=_cp(vmem=VMEM_BIG),
    )(p, p, p, cw, cw, cw, *dqs, *dks, *dvs)
    return outs


def _gate_fn(a, prm):
    lane = lax.broadcasted_iota(jnp.int32, a.shape, 1)
    z = a + prm[1:2, :]
    sp = jnp.maximum(z, 0.0) + jnp.log(1.0 + jnp.exp(-jnp.abs(z)))
    g = -jnp.exp(prm[0:1, :]) * sp
    return jnp.where(lane < 8, g, jnp.where(lane < 16, _sigmoid(a), 0.0))


def _gates_fwd(p, prm):
    def body(a_ref, prm_ref, o_ref):
        o_ref[...] = _gate_fn(a_ref[...], prm_ref[...])

    return pl.pallas_call(
        body, name="gates_fwd", grid=(NB,), out_shape=jax.ShapeDtypeStruct((NB, T, 128), F32),
        in_specs=[pl.BlockSpec((None, T, 128), lambda b: (b, 0, COL_G // 128)), pl.BlockSpec((8, 128), lambda b: (0, 0))],
        out_specs=pl.BlockSpec((None, T, 128), lambda b: (b, 0, 0)),
    )(p, prm)


def _gates_bwd(p, prm, dgf, dgb):
    def body(a_ref, prm_ref, dgf_ref, dgb_ref, o_ref, dprm_ref):
        b = pl.program_id(0)
        tot = dgf_ref[...] + dgb_ref[...]
        _, vjp = jax.vjp(_gate_fn, a_ref[...], prm_ref[...])
        da, dprm = vjp(tot)
        o_ref[...] = da

        @pl.when(b == 0)
        def _():
            dprm_ref[...] = dprm

        @pl.when(b > 0)
        def _():
            dprm_ref[...] += dprm

    return pl.pallas_call(
        body, name="gates_bwd", grid=(NB,),
        out_shape=[jax.ShapeDtypeStruct((NB, T, 128), F32), jax.ShapeDtypeStruct((8, 128), F32)],
        in_specs=[pl.BlockSpec((None, T, 128), lambda b: (b, 0, COL_G // 128)), pl.BlockSpec((8, 128), lambda b: (0, 0)),
                  pl.BlockSpec((None, T, 128), lambda b: (b, 0, 0)), pl.BlockSpec((None, T, 128), lambda b: (b, 0, 0))],
        out_specs=[pl.BlockSpec((None, T, 128), lambda b: (b, 0, 0)), pl.BlockSpec((8, 128), lambda b: (0, 0))],
        compiler_params=_cp(vmem=VMEM_BIG),
    )(p, prm, dgf, dgb)


def _tri_inv_impl(m, md, m1, m2, eye):
    n = -(m * md)
    t = eye + n
    pw = n
    for _ in range(3):
        pw = _dg(pw, pw, _NN)
        t = t + _dg(t, pw, _NN)
    t = t - _dg(_dg(t, m * m1, _NN), t, _NN)
    t = t - _dg(_dg(t, m * m2, _NN), t, _NN)
    return t


@jax.custom_vjp
def _tri_inv(m, md, m1, m2, eye):
    return _tri_inv_impl(m, md, m1, m2, eye)


def _tri_inv_f(m, md, m1, m2, eye):
    t = _tri_inv_impl(m, md, m1, m2, eye)
    return t, (t, md)


def _tri_inv_b(res, g):
    t, md = res
    tt = t.T
    z = jnp.zeros_like(md)
    return (-_dg(_dg(tt, g, _NN), tt, _NN), z, z, z, z)


_tri_inv.defvjp(_tri_inv_f, _tri_inv_b)


SR = DNH * CH


def _scan_masks():
    i, j = np.arange(SR)[:, None], np.arange(SR)[None, :]
    same = (i >> 6) == (j >> 6)
    cm = np.zeros((2, 6, SR, SR), np.float32)
    for d in range(2):
        diff = i - j if d == 0 else j - i
        cm[d, 0] = same & (diff >= 0)
        cm[d, 1] = same & (diff > 0)
        cm[d, 2] = (i >> 4) == (j >> 4)
        cm[d, 3] = ((i >> 5) == (j >> 5)) & ((i >> 4) != (j >> 4))
        cm[d, 4] = (i >> 5) != (j >> 5)
        cm[d, 5] = i == j
    t, u = np.arange(CH)[:, None], np.arange(CH)[None, :]
    c64 = np.stack([t >= u, t <= u]).astype(np.float32)
    bm = ((np.arange(SR)[:, None] >> 6) == (np.arange(DNW)[None, :] >> 7)).astype(np.float32)
    return jnp.asarray(cm), jnp.asarray(c64), jnp.asarray(bm)


@jax.custom_vjp
def _fold(x, bm):
    xm = x * bm
    return xm[:, 0:128] + xm[:, 128:256] + xm[:, 256:384] + xm[:, 384:512]


def _fold_f(x, bm):
    return _fold(x, bm), bm


def _fold_b(bm, g):
    return jnp.concatenate([g, g, g, g], axis=1) * bm, jnp.zeros_like(bm)


_fold.defvjp(_fold_f, _fold_b)


def _chunk_fn(q, k, v, gb, s, cm, c64, bm, d):
    incl, strict = cm[0], cm[1]
    lane = lax.broadcasted_iota(jnp.int32, (1, 128), 1)
    oh = lambda n: jnp.where(lane == n, 1.0, 0.0).astype(F32)
    col = lambda x, base: jnp.concatenate(
        [jnp.sum(x * oh(base + d * DNH + h), axis=1, keepdims=True) for h in range(DNH)], axis=0)
    beta = col(gb, 8)
    gam = col(_mask_mm(c64, gb), 0)
    tot = jnp.sum(gb, axis=0, keepdims=True)
    gtots = [jnp.sum(tot * oh(d * DNH + h), axis=1, keepdims=True) for h in range(DNH)]
    gtot = jnp.concatenate([jnp.broadcast_to(t, (CH, 1)) for t in gtots], axis=0)
    eg = jnp.concatenate([jnp.broadcast_to(jnp.exp(t), (1, 128)) for t in gtots], axis=1)
    row0 = jnp.where(lax.broadcasted_iota(jnp.int32, (128, 1), 0) == 0, 1.0, 0.0).astype(F32)
    gam_row = jnp.sum(jnp.broadcast_to(gam, (SR, 128)).T * row0, axis=0, keepdims=True)
    dec = jnp.exp(jnp.where(incl > 0.5, gam - gam_row, NEG))
    e_gam = jnp.exp(gam)
    kb = k * beta
    m = strict * (_mm_nt(kb, k) * dec)
    tm = _tri_inv(m, cm[2], cm[3], cm[4], cm[5])
    u = _mm(tm, v * beta)
    w = _mm(tm, kb * e_gam)
    qk = _mm_nt(q, k) * dec
    v_new = u - _fold(_mm(w, s), bm)
    o = _fold(_mm(q * e_gam, s), bm) + _mm(qk, v_new)
    vbd = jnp.concatenate([v_new, v_new, v_new, v_new], axis=1) * bm
    s_new = s * eg + _mm_tn(k * jnp.exp(gtot - gam), vbd)
    return o, s_new


def _chunk_rev(s):
    return jnp.where(s < NCH_CTX, NCH_CTX - 1 - s, NCH + NCH_CTX - 1 - s)


def _scan_specs(step):
    cf = lambda s: step(s)
    cb = lambda s: _chunk_rev(step(s))
    tok = lambda c: pl.BlockSpec((None, CH, DNW), lambda b, s: (b, c(s), 0))
    gat = lambda c: pl.BlockSpec((None, CH, 128), lambda b, s: (b, c(s), 0))
    sta = lambda c: pl.BlockSpec((None, None, 128, DNW), lambda b, s: (b, c(s), 0, 0))
    return (tok(cf), gat(cf), sta(cf)), (tok(cb), gat(cb), sta(cb))


def _const_specs():
    return [pl.BlockSpec((2, 6, SR, SR), lambda b, s: (0, 0, 0, 0)), pl.BlockSpec((2, CH, CH), lambda b, s: (0, 0, 0)),
            pl.BlockSpec((SR, DNW), lambda b, s: (0, 0))]


def _stack(ref):
    return jnp.concatenate([ref[:, 128 * h:128 * h + 128] for h in range(DNH)], axis=0)


def _unstack_to(ref, val):
    for h in range(DNH):
        ref[:, 128 * h:128 * h + 128] = val[CH * h:CH * h + CH]


def _scan_fwd(q, k, v, gates):
    def body(cm_ref, c64_ref, bm_ref, qf, kf, vf, gf, qb, kb, vb, gb, of_ref, sf_ref, ob_ref, sb_ref, s_ref):
        @pl.when(pl.program_id(1) == 0)
        def _():
            s_ref[...] = jnp.zeros_like(s_ref)

        bm = bm_ref[...]
        dirs = ((qf, kf, vf, gf, of_ref, sf_ref), (qb, kb, vb, gb, ob_ref, sb_ref))
        for d, (q_r, k_r, v_r, g_r, o_r, st_r) in enumerate(dirs):
            st = s_ref[d]
            st_r[...] = st
            o, s_new = _chunk_fn(_stack(q_r), _stack(k_r), _stack(v_r), g_r[...], st, cm_ref[d], c64_ref[d], bm, d)
            _unstack_to(o_r, o)
            s_ref[d] = s_new

    (tf, gf_, sf_), (tb, gb_, sb_) = _scan_specs(lambda s: s)
    o_sh = jax.ShapeDtypeStruct((NB, T, DNW), F32)
    st_sh = jax.ShapeDtypeStruct((NB, NCH, 128, DNW), F32)
    return pl.pallas_call(
        body, name="scan_fwd", grid=(NB, NCH), out_shape=[o_sh, st_sh, o_sh, st_sh],
        in_specs=_const_specs() + [tf, tf, tf, gf_, tb, tb, tb, gb_], out_specs=[tf, sf_, tb, sb_],
        scratch_shapes=[pltpu.VMEM((2, 128, DNW), F32)],
        compiler_params=_cp(vmem=VMEM_BIG),
    )(*_scan_masks(), q, k, v, gates, q, k, v, gates)


def _scan_bwd(q, k, v, gates, st_f, st_b, do, carry=()):
    nc = len(carry)

    def body(*refs):
        (cm_ref, c64_ref, bm_ref, qf, kf, vf, gf, sf, dof, qb, kb, vb, gb, sb, dob) = refs[:15]
        c_in = refs[15:15 + nc]
        dqf, dkf, dvf, dgf, dqb, dkb, dvb, dgb = refs[15 + nc:23 + nc]
        c_out = refs[23 + nc:23 + 2 * nc]
        ds_ref = refs[23 + 2 * nc]
        sems = refs[24 + 2 * nc:]
        first = (pl.program_id(0) == 0) & (pl.program_id(1) == 0)
        last = (pl.program_id(0) == NB - 1) & (pl.program_id(1) == NCH - 1)
        if nc:
            @pl.when(first)
            def _():
                _xfer_start(_scatter_plan(c_in, c_out, _my_pos()), sems)

        @pl.when(pl.program_id(1) == 0)
        def _():
            ds_ref[...] = jnp.zeros_like(ds_ref)

        bm = bm_ref[...]
        dirs = ((qf, kf, vf, gf, sf, dof, dqf, dkf, dvf, dgf), (qb, kb, vb, gb, sb, dob, dqb, dkb, dvb, dgb))
        for d, (q_r, k_r, v_r, g_r, st_r, do_r, dq_r, dk_r, dv_r, dg_r) in enumerate(dirs):
            cm, c64 = cm_ref[d], c64_ref[d]
            _, vjp = jax.vjp(lambda a, b_, c, e, f, cm=cm, c64=c64, d=d: _chunk_fn(a, b_, c, e, f, cm, c64, bm, d),
                             _stack(q_r), _stack(k_r), _stack(v_r), g_r[...], st_r[...])
            dq, dk, dv, dg, ds = vjp((_stack(do_r), ds_ref[d]))
            _unstack_to(dq_r, dq)
            _unstack_to(dk_r, dk)
            _unstack_to(dv_r, dv)
            dg_r[...] = dg
            ds_ref[d] = ds

        if nc:
            @pl.when(last)
            def _():
                _xfer_wait(_scatter_plan(c_in, c_out, _my_pos()), sems)

    (tf, gf_, sf_), (tb, gb_, sb_) = _scan_specs(lambda s: NCH - 1 - s)
    o_sh = jax.ShapeDtypeStruct((NB, T, DNW), F32)
    g_sh = jax.ShapeDtypeStruct((NB, T, 128), F32)
    hbm = pl.BlockSpec(memory_space=pltpu.HBM)
    return pl.pallas_call(
        body, name="scan_bwd", grid=(NB, NCH),
        out_shape=[o_sh, o_sh, o_sh, g_sh] * 2 + [jax.ShapeDtypeStruct(g.shape, g.dtype) for g in carry],
        in_specs=_const_specs() + [tf, tf, tf, gf_, sf_, tf, tb, tb, tb, gb_, sb_, tb] + [hbm] * nc,
        out_specs=[tf, tf, tf, gf_, tb, tb, tb, gb_] + [hbm] * nc,
        scratch_shapes=[pltpu.VMEM((2, 128, DNW), F32)] + (_xfer_sems(nc, 7 * nc) if nc else []),
        compiler_params=_cp(vmem=VMEM_BIG),
    )(*_scan_masks(), q, k, v, gates, st_f, do, q, k, v, gates, st_b, do, *carry)


def _segmean(a, bd):
    hi = a.astype(BF16)
    lo = (a - hi.astype(F32)).astype(BF16)
    return jnp.dot(hi, bd, preferred_element_type=F32) + jnp.dot(lo, bd, preferred_element_type=F32)


def _na_prep_fwd(p, qw, kw, bd):
    def body(q_ref, k_ref, v_ref, qw_ref, kw_ref, bd_ref, oq_ref, ok_ref, ov_ref):
        bdm = bd_ref[...]
        q = q_ref[...]
        k = k_ref[...]
        oq_ref[...] = (q * lax.rsqrt(_segmean(q * q, bdm) + EPS) * qw_ref[...] * 0.125).astype(BF16)
        ok_ref[...] = (k * lax.rsqrt(_segmean(k * k, bdm) + EPS) * kw_ref[...]).astype(BF16)
        ov_ref[...] = v_ref[...].astype(BF16)

    ps = lambda o: pl.BlockSpec((None, RT, NAW), lambda b, j: (b, j, COL_NA // NAW + o))
    one = pl.BlockSpec((1, NAW), lambda b, j: (0, 0))
    os_ = pl.BlockSpec((None, RT, NAW), lambda b, j: (b, j, 0))
    return pl.pallas_call(
        body, name="na_prep_fwd", grid=(NB, T // RT), out_shape=[jax.ShapeDtypeStruct((NB, T, NAW), BF16)] * 3,
        in_specs=[ps(0), ps(1), ps(2), one, one, pl.BlockSpec((NAW, NAW), lambda b, j: (0, 0))], out_specs=[os_] * 3,
    )(p, p, p, qw, kw, bd)


def _na_prep_bwd(p, qw, kw, bd, dq, dk, dv):
    def body(q_ref, k_ref, qw_ref, kw_ref, bd_ref, dq_ref, dk_ref, dv_ref, o_ref, gq_ref, gk_ref):
        b, j = pl.program_id(0), pl.program_id(1)
        bdm = bd_ref[...]

        def one(x, w, dy, scale):
            r = lax.rsqrt(_segmean(x * x, bdm) + EPS)
            g = dy * w * scale
            dx = r * g - x * (r * r * r) * _segmean(g * x, bdm)
            dw = jnp.sum(dy * x * r * scale, axis=0, keepdims=True)
            return dx, dw

        dqv = jnp.where(j == 0, 0.0, dq_ref[...])
        dxq, dwq = one(q_ref[...], qw_ref[...], dqv, 0.125)
        dxk, dwk = one(k_ref[...], kw_ref[...], dk_ref[...], 1.0)
        o_ref[...] = jnp.concatenate([dxq, dxk, dv_ref[...]], axis=1)
        pad = jnp.zeros((7, NAW), F32)

        @pl.when((b == 0) & (j == 0))
        def _():
            gq_ref[...] = jnp.zeros_like(gq_ref)
            gk_ref[...] = jnp.zeros_like(gk_ref)

        gq_ref[...] += jnp.concatenate([dwq, pad], axis=0)
        gk_ref[...] += jnp.concatenate([dwk, pad], axis=0)

    ps = lambda o: pl.BlockSpec((None, RT, NAW), lambda b, j: (b, j, COL_NA // NAW + o))
    one = pl.BlockSpec((1, NAW), lambda b, j: (0, 0))
    ts = pl.BlockSpec((None, RT, NAW), lambda b, j: (b, j, 0))
    acc = pl.BlockSpec((8, NAW), lambda b, j: (0, 0))
    return pl.pallas_call(
        body, name="na_prep_bwd", grid=(NB, T // RT),
        out_shape=[jax.ShapeDtypeStruct((NB, T, 1536), F32), jax.ShapeDtypeStruct((8, NAW), F32),
                   jax.ShapeDtypeStruct((8, NAW), F32)],
        in_specs=[ps(0), ps(1), one, one, pl.BlockSpec((NAW, NAW), lambda b, j: (0, 0)),
                  pl.BlockSpec((None, RT, NAW), lambda b, j: (b, jnp.maximum(j - 1, 0), 0)), ts, ts],
        out_specs=[pl.BlockSpec((None, RT, 1536), lambda b, j: (b, j, 0)), acc, acc],
        compiler_params=_cp(vmem=VMEM_BIG),
    )(p, p, qw, kw, bd, dq, dk, dv)


def _na_row0(r):
    return jnp.clip(r - WIN_ROWS // 2, 0, ROWS - WIN_ROWS)


def _na_cls(r):
    return _na_row0(r) - r + WIN_ROWS - 1


NHS = 4
NLW = NHS * 64
NSR = NHS * GRID_W


def _na_stack(x):
    lane = lax.broadcasted_iota(jnp.int32, (1, NLW), 1)
    return jnp.concatenate([jnp.where((lane >> 6) == h, x, jnp.zeros_like(x)) for h in range(NHS)], axis=0)


def _na_unstack(y):
    lane = lax.broadcasted_iota(jnp.int32, (1, NLW), 1)
    out = jnp.zeros((GRID_W, NLW), y.dtype)
    for h in range(NHS):
        out = jnp.where((lane >> 6) == h, y[GRID_W * h:GRID_W * h + GRID_W], out)
    return out


def _na_probs(q4, kw, kc, bias):
    s_w = lax.dot_general(q4, kw, _NT, preferred_element_type=F32) + bias
    s_c = lax.dot_general(q4, kc, _NT, preferred_element_type=F32)
    mx = jnp.maximum(jnp.max(s_w, axis=1, keepdims=True), jnp.max(s_c, axis=1, keepdims=True))
    p_w = jnp.exp(s_w - mx)
    p_c = jnp.exp(s_c - mx)
    inv = 1.0 / (jnp.sum(p_w, axis=1, keepdims=True) + jnp.sum(p_c, axis=1, keepdims=True))
    return p_w, p_c, inv


def _na_specs():
    qs = pl.BlockSpec((None, GRID_W, NLW), lambda g, b, r: (b, CT // GRID_W + r, g))
    kv = pl.BlockSpec((None, T, NLW), lambda g, b, r: (b, 0, g))
    bs = pl.BlockSpec((None, None, NSR, NWIN), lambda g, b, r: (_na_cls(r), g, 0, 0))
    lat = pl.BlockSpec((None, GRID_W, NLW), lambda g, b, r: (b, r, g))
    return qs, kv, bs, lat


def _na_fwd(qn, kn, vb, bias):
    def body(q_ref, k_ref, v_ref, b_ref, o_ref):
        r = pl.program_id(2)
        start = pl.multiple_of(CT + _na_row0(r) * GRID_W, GRID_W)
        kw, vw = k_ref[pl.ds(start, NWIN), :], v_ref[pl.ds(start, NWIN), :]
        kc, vc = k_ref[0:CT, :], v_ref[0:CT, :]
        p_w, p_c, inv = _na_probs(_na_stack(q_ref[...]), kw, kc, b_ref[...])
        o4 = (jnp.dot(p_w.astype(BF16), vw, preferred_element_type=F32)
              + jnp.dot(p_c.astype(BF16), vc, preferred_element_type=F32)) * inv
        o_ref[...] = _na_unstack(o4).astype(BF16)

    qs, kv, bs, lat = _na_specs()
    return pl.pallas_call(
        body, name="na_fwd", grid=(NAH // NHS, NB, ROWS), out_shape=jax.ShapeDtypeStruct((NB, SEQ, NAW), BF16),
        in_specs=[qs, kv, kv, bs], out_specs=lat,
        compiler_params=_cp(vmem=VMEM_BIG),
    )(qn, kn, vb, bias)


def _na_bwd(qn, kn, vb, bias, do):
    def body(q_ref, k_ref, v_ref, b_ref, do_ref, dq_ref, dk_ref, dv_ref, db_ref):
        b, r = pl.program_id(1), pl.program_id(2)

        @pl.when((b == 0) & (r == 0))
        def _():
            db_ref[...] = jnp.zeros_like(db_ref)

        @pl.when(r == 0)
        def _():
            dk_ref[...] = jnp.zeros_like(dk_ref)
            dv_ref[...] = jnp.zeros_like(dv_ref)

        start = pl.multiple_of(CT + _na_row0(r) * GRID_W, GRID_W)
        cls = _na_cls(r)
        kw, vw = k_ref[pl.ds(start, NWIN), :], v_ref[pl.ds(start, NWIN), :]
        kc, vc = k_ref[0:CT, :], v_ref[0:CT, :]
        q4 = _na_stack(q_ref[...])
        p_w, p_c, inv = _na_probs(q4, kw, kc, b_ref[...])
        p_w = p_w * inv
        p_c = p_c * inv
        do4 = _na_stack(do_ref[...].astype(BF16))
        dp_w = lax.dot_general(do4, vw, _NT, preferred_element_type=F32)
        dp_c = lax.dot_general(do4, vc, _NT, preferred_element_type=F32)
        delta = jnp.sum(p_w * dp_w, axis=1, keepdims=True) + jnp.sum(p_c * dp_c, axis=1, keepdims=True)
        ds_w = p_w * (dp_w - delta)
        ds_c = p_c * (dp_c - delta)
        db_ref[cls] += ds_w
        ds_wb, ds_cb = ds_w.astype(BF16), ds_c.astype(BF16)
        dq_ref[...] = _na_unstack(jnp.dot(ds_wb, kw, preferred_element_type=F32)
                                  + jnp.dot(ds_cb, kc, preferred_element_type=F32))
        dk_ref[pl.ds(start, NWIN), :] += lax.dot_general(ds_wb, q4, _TN, preferred_element_type=F32)
        dv_ref[pl.ds(start, NWIN), :] += lax.dot_general(p_w.astype(BF16), do4, _TN, preferred_element_type=F32)
        dk_ref[0:CT, :] += lax.dot_general(ds_cb, q4, _TN, preferred_element_type=F32)
        dv_ref[0:CT, :] += lax.dot_general(p_c.astype(BF16), do4, _TN, preferred_element_type=F32)

    qs, kv, bs, lat = _na_specs()
    return pl.pallas_call(
        body, name="na_bwd", grid=(NAH // NHS, NB, ROWS),
        out_shape=[jax.ShapeDtypeStruct((NB, SEQ, NAW), F32)] + [jax.ShapeDtypeStruct((NB, T, NAW), F32)] * 2
        + [jax.ShapeDtypeStruct((WIN_ROWS, NAH // NHS, NSR, NWIN), F32)],
        in_specs=[qs, kv, kv, bs, lat],
        out_specs=[lat, kv, kv, pl.BlockSpec((WIN_ROWS, None, NSR, NWIN), lambda g, b, r: (0, g, 0, 0))],
        compiler_params=_cp(vmem=VMEM_BIG),
    )(qn, kn, vb, bias, do)


def _na_tables():
    cols = np.arange(GRID_W)
    win_start = np.clip(cols - WIN_COLS // 2, 0, GRID_W - WIN_COLS)
    kc = cols[None, :]
    valid = (kc >= win_start[:, None]) & (kc < win_start[:, None] + WIN_COLS)
    rel = np.clip(kc - cols[:, None] + WIN_COLS - 1, 0, 2 * WIN_COLS - 2)
    return valid, rel


def _bias_table(rpb):
    valid, rel = _na_tables()
    t15 = jnp.where(valid[None, None], rpb[:, :, rel], NEG)
    tab = jnp.stack([t15[:, c:c + WIN_ROWS] for c in range(WIN_ROWS)], axis=0)
    return tab.transpose(0, 1, 3, 2, 4).reshape(WIN_ROWS, NAH // NHS, NSR, NWIN)


def _rpb_fold(db):
    def body1(d_ref, o_ref):
        for dr in range(2 * WIN_ROWS - 1):
            acc = None
            for c in range(WIN_ROWS):
                i = dr - c
                if 0 <= i < WIN_ROWS:
                    blk = d_ref[c, :, i * GRID_W:(i + 1) * GRID_W]
                    acc = blk if acc is None else acc + blk
            o_ref[dr] = acc

    t15 = pl.pallas_call(
        body1, name="rpb_fold_rows", grid=(NAH,),
        out_shape=jax.ShapeDtypeStruct((NAH, 2 * WIN_ROWS - 1, GRID_W, GRID_W), F32),
        in_specs=[pl.BlockSpec((WIN_ROWS, None, GRID_W, NWIN), lambda h: (0, h, 0, 0))],
        out_specs=pl.BlockSpec((None, 2 * WIN_ROWS - 1, GRID_W, GRID_W), lambda h: (h, 0, 0, 0)),
        compiler_params=_cp(vmem=VMEM_BIG),
    )(db.reshape(WIN_ROWS, NAH, GRID_W, NWIN))
    valid, rel = _na_tables()
    onehot = np.zeros((GRID_W * GRID_W, 128), np.float32)
    flat_rel, flat_valid = rel.reshape(-1), valid.reshape(-1)
    onehot[np.arange(GRID_W * GRID_W)[flat_valid], flat_rel[flat_valid]] = 1.0

    def body2(a_ref, oh_ref, o_ref):
        o_ref[...] = _hmm(a_ref[...], oh_ref[...])

    vm = pl.BlockSpec(memory_space=pltpu.VMEM)
    out = pl.pallas_call(
        body2, name="rpb_fold_cols", out_shape=jax.ShapeDtypeStruct((NAH * 15, 128), F32),
        in_specs=[vm, vm], out_specs=vm, compiler_params=_cp(vmem=VMEM_BIG),
    )(t15.reshape(NAH * 15, GRID_W * GRID_W), jnp.asarray(onehot))
    return out[:, :31].reshape(NAH, 15, 31)


def _dn_out(o, z, w):
    outs = []
    for h in range(DNH):
        sl = slice(128 * h, 128 * h + 128)
        oh = o[:, sl]
        r = lax.rsqrt(jnp.mean(oh * oh, axis=-1, keepdims=True) + EPS)
        outs.append(oh * r * w[:, sl] * _silu(z[:, sl]))
    return jnp.concatenate(outs, axis=1)


def _outproj_fwd(of, ob, p, na, onw, x, g1, w_out):
    def body(of_ref, ob_ref, z_ref, na_ref, w_ref, x_ref, g_ref, wo_ref, x1_ref, ai_ref, ao_ref):
        dn = _dn_out(of_ref[...] + ob_ref[...], z_ref[...], w_ref[...])
        ai = jnp.concatenate([dn.astype(BF16), na_ref[...]], axis=1)
        ai_ref[...] = ai
        ao = jnp.dot(ai, wo_ref[...], preferred_element_type=F32)
        ao_ref[...] = ao
        x1_ref[...] = x_ref[...] + g_ref[...] * ao

    lat = lambda w: pl.BlockSpec((None, RT, w), lambda b, j: (b, j, 0))
    return pl.pallas_call(
        body, name="outproj_fwd", grid=(NB, SEQ // RT),
        out_shape=[jax.ShapeDtypeStruct((NB, SEQ, D), F32), jax.ShapeDtypeStruct((NB, SEQ, D), BF16),
                   jax.ShapeDtypeStruct((NB, SEQ, D), F32)],
        in_specs=[pl.BlockSpec((None, RT, DNW), lambda b, j: (b, j + 1, 0)),
                  pl.BlockSpec((None, RT, DNW), lambda b, j: (b, j + 1, 0)),
                  pl.BlockSpec((None, RT, DNW), lambda b, j: (b, j + 1, COL_Z // DNW)),
                  lat(NAW), pl.BlockSpec((1, DNW), lambda b, j: (0, 0)), lat(D),
                  pl.BlockSpec((None, 1, D), lambda b, j: (b, 0, 0)), pl.BlockSpec((D, D), lambda b, j: (0, 0))],
        out_specs=[lat(D), lat(D), lat(D)],
        compiler_params=_cp(vmem=VMEM_BIG),
    )(of, ob, p, na, onw, x, g1, w_out)


def _outproj_bwd(dx1, ao, g1, w_out, of, ob, p, onw):
    def body(dx_ref, ao_ref, g_ref, wo_ref, of_ref, ob_ref, z_ref, w_ref, dao_ref, do_ref, dz_ref, dna_ref, dg_ref, dw_ref):
        b, j = pl.program_id(0), pl.program_id(1)

        @pl.when((b == 0) & (j == 0))
        def _():
            dw_ref[...] = jnp.zeros_like(dw_ref)

        @pl.when(j == 0)
        def _():
            do_ref[...] = jnp.zeros_like(do_ref)
            dz_ref[...] = jnp.zeros_like(dz_ref)

        @pl.when(j == 1)
        def _():
            dg_ref[...] = jnp.zeros_like(dg_ref)

        @pl.when(j >= 1)
        def _():
            dx = dx_ref[...]
            dg_ref[...] += jnp.sum(dx * ao_ref[...], axis=0, keepdims=True)
            dao = (g_ref[...] * dx).astype(BF16)
            dao_ref[...] = dao
            dai = lax.dot_general(dao, wo_ref[...], _NT, preferred_element_type=F32)
            dna_ref[...] = dai[:, DNW:]
            _, vjp = jax.vjp(_dn_out, of_ref[...] + ob_ref[...], z_ref[...], w_ref[...])
            do, dz, dw = vjp(dai[:, :DNW])
            do_ref[...] = do
            dz_ref[...] = dz
            dw_ref[...] += jnp.concatenate([dw, jnp.zeros((7, DNW), F32)], axis=0)

    lat = lambda w: pl.BlockSpec((None, RT, w), lambda b, j: (b, jnp.maximum(j - 1, 0), 0))
    tok = lambda w: pl.BlockSpec((None, RT, w), lambda b, j: (b, j, 0))
    return pl.pallas_call(
        body, name="outproj_bwd", grid=(NB, T // RT),
        out_shape=[jax.ShapeDtypeStruct((NB, SEQ, D), BF16), jax.ShapeDtypeStruct((NB, T, DNW), F32),
                   jax.ShapeDtypeStruct((NB, T, DNW), F32), jax.ShapeDtypeStruct((NB, SEQ, NAW), F32),
                   jax.ShapeDtypeStruct((NB, 1, D), F32), jax.ShapeDtypeStruct((8, DNW), F32)],
        in_specs=[lat(D), lat(D), pl.BlockSpec((None, 1, D), lambda b, j: (b, 0, 0)),
                  pl.BlockSpec((D, D), lambda b, j: (0, 0)),
                  tok(DNW), tok(DNW),
                  pl.BlockSpec((None, RT, DNW), lambda b, j: (b, j, COL_Z // DNW)),
                  pl.BlockSpec((1, DNW), lambda b, j: (0, 0))],
        out_specs=[lat(D), tok(DNW), tok(DNW), lat(NAW), pl.BlockSpec((None, 1, D), lambda b, j: (b, 0, 0)),
                   pl.BlockSpec((8, DNW), lambda b, j: (0, 0))],
        compiler_params=_cp(vmem=VMEM_BIG),
    )(dx1, ao, g1, w_out, of, ob, p, onw)


FT = 256
NFT = NTOK // FT
TPE = SEQ // FT
NFF = DFF // FFT


def _ffn_specs():
    row = lambda w: pl.BlockSpec((FT, w), lambda i, f: (i, 0))
    ex = pl.BlockSpec((None, 1, D), lambda i, f: (i // TPE, 0, 0))
    one = pl.BlockSpec((1, D), lambda i, f: (0, 0))
    wg = pl.BlockSpec((None, D, FFT), lambda i, f: (f, 0, 0))
    wu = pl.BlockSpec((None, D, FFT), lambda i, f: (NFF + f, 0, 0))
    wo = pl.BlockSpec((FFT, D), lambda i, f: (f, 0))
    ff = pl.BlockSpec((FT, FFT), lambda i, f: (i, f))
    return row, ex, one, wg, wu, wo, ff


def _ffn_fwd(x1, sh2, sc2, g2, n2w, w_in4, w_out, target):
    def body(x_ref, sh_ref, sc_ref, g_ref, w2_ref, wg_ref, wu_ref, wo_ref, t_ref,
             h2_ref, gt_ref, up_ref, dy_ref, dg_ref, loss_ref, acc_ref):
        i, f = pl.program_id(0), pl.program_id(1)

        @pl.when(f == 0)
        def _():
            h2_ref[...] = _norm_mod(x_ref[...], w2_ref[...], sh_ref[...], sc_ref[...]).astype(BF16)
            acc_ref[...] = jnp.zeros_like(acc_ref)

        h2 = h2_ref[...]
        gate = jnp.dot(h2, wg_ref[...], preferred_element_type=F32)
        up = jnp.dot(h2, wu_ref[...], preferred_element_type=F32)
        gt_ref[...] = gate.astype(BF16)
        up_ref[...] = up.astype(BF16)
        acc_ref[...] += jnp.dot((_silu(gate) * up).astype(BF16), wo_ref[...], preferred_element_type=F32)

        @pl.when((i == 0) & (f == 0))
        def _():
            loss_ref[...] = jnp.zeros_like(loss_ref)

        @pl.when(f == NFF - 1)
        def _():
            ffn = acc_ref[...]
            err = x_ref[...] + g_ref[...] * ffn - t_ref[...]
            dy = err * (1.0 / D)
            dy_ref[...] = dy
            part = jnp.sum(dy * ffn, axis=0, keepdims=True)
            loss_ref[...] += jnp.sum(jnp.sum(err * err, axis=1, keepdims=True), axis=0, keepdims=True) * (0.5 / D)

            @pl.when(i % TPE == 0)
            def _():
                dg_ref[...] = part

            @pl.when(i % TPE != 0)
            def _():
                dg_ref[...] += part

    row, ex, one, wg, wu, wo, ff = _ffn_specs()
    return pl.pallas_call(
        body, name="ffn_fwd", grid=(NFT, NFF),
        out_shape=[jax.ShapeDtypeStruct((NTOK, D), BF16), jax.ShapeDtypeStruct((NTOK, DFF), BF16),
                   jax.ShapeDtypeStruct((NTOK, DFF), BF16), jax.ShapeDtypeStruct((NTOK, D), F32),
                   jax.ShapeDtypeStruct((NB, 1, D), F32), jax.ShapeDtypeStruct((8, 128), F32)],
        in_specs=[row(D), ex, ex, ex, one, wg, wu, wo, row(D)],
        out_specs=[row(D), ff, ff, row(D), ex, pl.BlockSpec((8, 128), lambda i, f: (0, 0))],
        scratch_shapes=[pltpu.VMEM((FT, D), F32)],
        compiler_params=_cp(vmem=VMEM_BIG),
    )(x1, sh2, sc2, g2, n2w, w_in4, w_in4, w_out, target)


def _ffn_bwd(dy, gate, up, x1, sh2, sc2, g2, n2w, w_in4, w_out):
    def body(dy_ref, gt_ref, up_ref, x_ref, sh_ref, sc_ref, g_ref, w2_ref, wg_ref, wu_ref, wo_ref,
             dff_ref, act_ref, dgu_ref, dx_ref, dsh_ref, dsc_ref, dw_ref, acc_ref):
        i, f = pl.program_id(0), pl.program_id(1)

        @pl.when(f == 0)
        def _():
            dff_ref[...] = (g_ref[...] * dy_ref[...]).astype(BF16)
            acc_ref[...] = jnp.zeros_like(acc_ref)

        dact = lax.dot_general(dff_ref[...], wo_ref[...], _NT, preferred_element_type=F32)
        gate, up = gt_ref[...].astype(F32), up_ref[...].astype(F32)
        sg = _sigmoid(gate)
        sl = gate * sg
        act_ref[...] = (sl * up).astype(BF16)
        dgate = (dact * up * (sg * (1.0 + gate * (1.0 - sg)))).astype(BF16)
        dup = (dact * sl).astype(BF16)
        dgu_ref[0] = dgate
        dgu_ref[1] = dup
        acc_ref[...] += (lax.dot_general(dgate, wg_ref[...], _NT, preferred_element_type=F32)
                         + lax.dot_general(dup, wu_ref[...], _NT, preferred_element_type=F32))

        @pl.when((i == 0) & (f == 0))
        def _():
            dw_ref[...] = jnp.zeros_like(dw_ref)

        @pl.when(f == NFF - 1)
        def _():
            _, vjp = jax.vjp(_norm_mod, x_ref[...], w2_ref[...], sh_ref[...], sc_ref[...])
            dx, dw, dsh, dsc = vjp(acc_ref[...])
            dx_ref[...] = dy_ref[...] + dx
            dw_ref[...] += dw

            @pl.when(i % TPE == 0)
            def _():
                dsh_ref[...] = dsh
                dsc_ref[...] = dsc

            @pl.when(i % TPE != 0)
            def _():
                dsh_ref[...] += dsh
                dsc_ref[...] += dsc

    row, ex, one, wg, wu, wo, ff = _ffn_specs()
    return pl.pallas_call(
        body, name="ffn_bwd", grid=(NFT, NFF),
        out_shape=[jax.ShapeDtypeStruct((NTOK, D), BF16), jax.ShapeDtypeStruct((NTOK, DFF), BF16),
                   jax.ShapeDtypeStruct((2, NTOK, DFF), BF16), jax.ShapeDtypeStruct((NTOK, D), F32),
                   jax.ShapeDtypeStruct((NB, 1, D), F32), jax.ShapeDtypeStruct((NB, 1, D), F32),
                   jax.ShapeDtypeStruct((1, D), F32)],
        in_specs=[row(D), ff, ff, row(D), ex, ex, ex, one, wg, wu, wo],
        out_specs=[row(D), ff, pl.BlockSpec((2, FT, FFT), lambda i, f: (0, i, f)), row(D), ex, ex, one],
        scratch_shapes=[pltpu.VMEM((FT, D), F32)],
        compiler_params=_cp(vmem=VMEM_BIG),
    )(dy, gate, up, x1, sh2, sc2, g2, n2w, w_in4, w_in4, w_out)


def _local_step(x, ctx, target, mod_x, mod_c, norm1_w, norm2_w, w_in_p, conv_w, a_log, dt_bias, out_norm_w,
                q_norm_w, k_norm_w, rpb, w_out_b, w_ffn_in4, w_ffn_out_b, scatter_early=False):
    sh1, sc1, g1, sh2, sc2, g2 = [mod_x[:, i * D:(i + 1) * D].reshape(NB, 1, D) for i in range(6)]
    csh1, csc1 = mod_c[0:D].reshape(1, 1, D), mod_c[D:2 * D].reshape(1, 1, D)
    cw = jnp.concatenate([conv_w, jnp.zeros((3, 1536), F32)], axis=0)
    prm = jnp.zeros((8, 128), F32).at[0, :8].set(a_log.reshape(8)).at[1, :8].set(dt_bias.reshape(8))
    onw = jnp.tile(out_norm_w.reshape(1, 128), (1, DNH))
    qw = jnp.tile(q_norm_w.reshape(1, 64), (1, NAH))
    kw = jnp.tile(k_norm_w.reshape(1, 64), (1, NAH))
    seg = np.arange(NAW) // 64
    bd = jnp.asarray((seg[:, None] == seg[None, :]).astype(np.float32) / 64.0, dtype=BF16)
    bias = _bias_table(rpb)

    p, hm = _inproj_fwd(x, ctx, sh1, sc1, csh1, csc1, norm1_w, w_in_p)
    dq_, dk_, dv_ = _dn_prep_fwd(p, cw)
    gates = _gates_fwd(p, prm)
    o_f, st_f, o_b, st_b = _scan_fwd(dq_, dk_, dv_, gates)
    qn, kn, vb = _na_prep_fwd(p, qw, kw, bd)
    na = _na_fwd(qn, kn, vb, bias)
    x1, ai, ao = _outproj_fwd(o_f, o_b, p, na, onw, x, g1, w_out_b)
    x1f = x1.reshape(NTOK, D)
    h2, gate, up, dy, dg2, loss8 = _ffn_fwd(x1f, sh2, sc2, g2, norm2_w, w_ffn_in4, w_ffn_out_b,
                                             target.reshape(NTOK, D))
    dff, act, dgu, dx1, dsh2, dsc2, dn2w = _ffn_bwd(dy, gate, up, x1f, sh2, sc2, g2, norm2_w, w_ffn_in4, w_ffn_out_b)
    g_w_ffn_out = _matmul_tn(act, dff, FFT, D, 512, "wgrad_ffn_out")
    g_w_ffn_in4 = _matmul_tn(
        h2, dgu, 512, FFT, 512, "wgrad_ffn_in", nn=2 * NFF,
        b_block=(None, 512, FFT), b_map=lambda i, j, k: (j // NFF, k, j % NFF),
        out_shape=(4, D, FFT), out_block=(None, 512, FFT), out_map=lambda i, j, k: (j, i, 0))
    dao, do, dz, dna, dg1, donw = _outproj_bwd(dx1.reshape(NB, SEQ, D), ao, g1, w_out_b, o_f, o_b, p, onw)
    g_w_out = _matmul_tn(ai.reshape(NTOK, D), dao.reshape(NTOK, D), 512, D, 512, "wgrad_out")
    dqn, dkn, dvn, dbias = _na_bwd(qn, kn, vb, bias, dna)
    dp_na, dqw, dkw = _na_prep_bwd(p, qw, kw, bd, dqn, dkn, dvn)
    early = [g_w_out.reshape(8, D // 8, D), g_w_ffn_in4.reshape(8, D // 2, FFT), g_w_ffn_out.reshape(8, DFF // 8, D)]
    res = _scan_bwd(dq_, dk_, dv_, gates, st_f, st_b, do, carry=early if scatter_early else ())
    dqf, dkf, dvf, dgf, dqb, dkb, dvb, dgb = res[:8]
    if scatter_early:
        g_w_out, g_w_ffn_in4, g_w_ffn_out = res[8:]
    dpq, dpk, dpv, gcq, gck, gcv = _dn_prep_bwd(p, cw, (dqf, dqb), (dkf, dkb), (dvf, dvb))
    dgt, dprm = _gates_bwd(p, prm, dgf, dgb)
    dp_dn = jnp.concatenate([dpq, dpk, dpv], axis=2)
    dpb, grad_x, dsh1, dsc1, dcsh1, dcsc1, dn1w = _inproj_bwd(
        dp_dn, dz, dp_na, dgt, w_in_p, x, ctx, sh1, sc1, csh1, csc1, norm1_w, dx1.reshape(NB, SEQ, D))
    g_w_in_p = _matmul_tn(hm.reshape(NB * T, D), dpb.reshape(NB * T, PC), 256, PC, 512, "wgrad_in")
    g_rpb = _rpb_fold(dbias)

    dmod_x = jnp.concatenate([dsh1, dsc1, dg1, dsh2, dsc2, dg2], axis=2).reshape(NB, 6 * D)
    dmod_c = jnp.concatenate([dcsh1.reshape(D), dcsc1.reshape(D), jnp.zeros((4 * D,), F32)])
    g_conv = jnp.concatenate([gcq[:5], gck[:5], gcv[:5]], axis=1)
    small = dict(
        norm1_w=dn1w, norm2_w=dn2w, a_log=dprm[0, :8], dt_bias=dprm[1, :8],
        out_norm_w=donw[0].reshape(DNH, 128).sum(0), q_norm_w=dqw[0].reshape(NAH, 64).sum(0),
        k_norm_w=dkw[0].reshape(NAH, 64).sum(0), rpb=g_rpb, conv_w=g_conv)
    return loss8[0, 0], grad_x, dmod_x, dmod_c, small, g_w_in_p, g_w_out, g_w_ffn_in4, g_w_ffn_out


def _perm_w_in(w):
    return jnp.concatenate([w[:, :2048], w[:, 2064:3600], w[:, 2048:2064],
                            jnp.zeros((w.shape[0], PC - IN_COLS), w.dtype)], axis=1)


def _unperm_w_in(g):
    return jnp.concatenate([g[:, :2048], g[:, COL_G:COL_G + 16], g[:, 2048:COL_G]], axis=1)


def _pack_small(c_ctx, n1, n2, b_ada, a_log, dt_bias, onw, qnw, knw, rpb, conv_s):
    misc = jnp.zeros((D,), F32)
    misc = misc.at[0:8].set(a_log.reshape(8)).at[8:16].set(dt_bias.reshape(8))
    misc = misc.at[128:256].set(onw.reshape(128)).at[256:320].set(qnw.reshape(64)).at[320:384].set(knw.reshape(64))
    rp = jnp.concatenate([rpb.reshape(-1), jnp.zeros((4 * D - 3720,), F32)])
    cv = jnp.concatenate([conv_s.reshape(-1), jnp.zeros((2 * D - 1920,), F32)])
    return jnp.concatenate([c_ctx.reshape(-1), n1.reshape(-1), n2.reshape(-1), b_ada.reshape(-1), misc, rp, cv]).reshape(16, D)


def _unpack_small(a):
    f = a.reshape(-1)
    misc = f[9 * D:10 * D]
    return dict(
        c_ctx=f[0:D], norm1_w=f[D:2 * D].reshape(1, D), norm2_w=f[2 * D:3 * D].reshape(1, D),
        b_ada=f[3 * D:9 * D].reshape(1, 6 * D), dn_A_log=misc[0:8].reshape(1, 2, 4), dn_dt_bias=misc[8:16].reshape(1, 2, 4),
        dn_out_norm_w=misc[128:256].reshape(1, 128), na_q_norm_w=misc[256:320].reshape(1, 64),
        na_k_norm_w=misc[320:384].reshape(1, 64), na_rpb=f[10 * D:10 * D + 3720].reshape(1, 8, 15, 31),
        dn_conv_w=f[14 * D:14 * D + 1920].reshape(1, 5, 384))


_WEIGHTS = ['c_ctx', 'norm1_w', 'norm2_w', 'w_ada', 'b_ada', 'w_in', 'dn_conv_w', 'dn_A_log', 'dn_dt_bias',
            'dn_out_norm_w', 'na_q_norm_w', 'na_k_norm_w', 'na_rpb', 'w_out', 'w_ffn_in', 'w_ffn_out']


def kernel(x, c, ctx, c_ctx, norm1_w, norm2_w, w_ada, b_ada, w_in, dn_conv_w, dn_A_log, dn_dt_bias, dn_out_norm_w, na_q_norm_w, na_k_norm_w, na_rpb, w_out, w_ffn_in, w_ffn_out, loss_target, m_c_ctx, m_norm1_w, m_norm2_w, m_w_ada, m_b_ada, m_w_in, m_dn_conv_w, m_dn_A_log, m_dn_dt_bias, m_dn_out_norm_w, m_na_q_norm_w, m_na_k_norm_w, m_na_rpb, m_w_out, m_w_ffn_in, m_w_ffn_out, v_c_ctx, v_norm1_w, v_norm2_w, v_w_ada, v_b_ada, v_w_in, v_dn_conv_w, v_dn_A_log, v_dn_dt_bias, v_dn_out_norm_w, v_na_q_norm_w, v_na_k_norm_w, v_na_rpb, v_w_out, v_w_ffn_in, v_w_ffn_out):
    ix, iy, ic = lax.axis_index("x"), lax.axis_index("y"), lax.axis_index("c")
    chip = 2 * ix + iy
    me = 4 * ix + 2 * iy + ic

    crow = jnp.concatenate([c, c_ctx.reshape(1, D), jnp.zeros((5, D), F32)], axis=0)
    cvrow = jnp.concatenate([dn_conv_w[0].reshape(-1), jnp.zeros((2 * D - 1920,), F32)]).reshape(2, D)
    g0 = _allgather8(jnp.concatenate([crow, cvrow, jnp.zeros((6, D), F32)], axis=0), "gather_cond")
    cin = g0[:, :8].reshape(64, D)
    conv_full = jnp.concatenate([g0[2 * s, 8:10].reshape(-1)[:1920].reshape(5, 384) for s in range(4)], axis=1)

    w_ada_s = w_ada[0]
    b_s = lax.dynamic_slice(b_ada, (0, chip * 1536), (1, 1536))
    mod_s = _ada_fwd(cin, w_ada_s, b_s)
    mods = _allgather8(mod_s, "gather_mod")
    mod_all = jnp.concatenate([mods[0], mods[2], mods[4], mods[6]], axis=1)
    mod_mine = lax.dynamic_slice(mod_all, (8 * me, 0), (8, 6 * D))
    mod_x, mod_c = mod_mine[0:NB], mod_mine[2]

    def half(w2d):
        r = w2d.shape[0] // 2
        return lax.dynamic_slice(w2d, (ic * r, 0), (r, w2d.shape[1]))

    halves = [_cast_bf16(half(w_in[0]), "cast_w_in"), _cast_bf16(half(w_out[0]), "cast_w_out"),
              _cast_bf16(half(w_ffn_in[0]), "cast_w_ffn_in"), _cast_bf16(half(w_ffn_out[0]), "cast_w_ffn_out")]
    gw_in, gw_out, gw_ffn_in, gw_ffn_out = _gather_weights(halves)
    w_in_full = gw_in.reshape(4, D, 900).transpose(1, 0, 2).reshape(D, IN_COLS)
    w_in_p = _perm_w_in(w_in_full)
    w_out_b = gw_out.reshape(D, D)
    w_ffn_in4 = gw_ffn_in.reshape(4, D, FFT)
    w_ffn_out_b = gw_ffn_out.reshape(DFF, D)

    (loss_part, grad_x, dmod_x, dmod_c, small, g_w_in_p, r_w_out, r_w_ffn_in, r_w_ffn_out) = _local_step(
        x, ctx, loss_target, mod_x, mod_c, norm1_w, norm2_w, w_in_p, conv_full, dn_A_log[0], dn_dt_bias[0],
        dn_out_norm_w[0], na_q_norm_w[0], na_k_norm_w[0], na_rpb[0], w_out_b, w_ffn_in4, w_ffn_out_b,
        scatter_early=True)
    loss = lax.psum(loss_part, ("x", "y", "c"))

    dm = jnp.concatenate([dmod_x, dmod_c.reshape(1, 6 * D), jnp.zeros((5, 6 * D), F32)], axis=0).reshape(48, D)
    misc = jnp.zeros((D,), F32)
    misc = misc.at[0:8].set(small["a_log"]).at[8:16].set(small["dt_bias"]).at[128:256].set(small["out_norm_w"])
    misc = misc.at[256:320].set(small["q_norm_w"]).at[320:384].set(small["k_norm_w"])
    sm = jnp.concatenate([
        small["norm1_w"].reshape(-1), small["norm2_w"].reshape(-1), misc,
        small["rpb"].reshape(-1), jnp.zeros((4 * D - 3720,), F32),
        small["conv_w"].reshape(-1), jnp.zeros((8 * D - 7680,), F32), jnp.zeros((D,), F32)]).reshape(16, D)
    gsl, gtot = _allgather8(jnp.concatenate([dm, sm], axis=0), "gather_small", with_sum=True)
    gmod = gsl[:, :48].reshape(64, 6 * D)
    st = gtot[48:64].reshape(-1)
    g_n1, g_n2, g_misc = st[0:D].reshape(1, D), st[D:2 * D].reshape(1, D), st[2 * D:3 * D]
    g_rpb = st[3 * D:3 * D + 3720].reshape(1, 8, 15, 31)
    g_conv_full = st[7 * D:7 * D + 7680].reshape(5, 1536)
    g_conv_s = lax.dynamic_slice(g_conv_full, (0, chip * 384), (5, 384)).reshape(1, 5, 384)

    gmod_s = lax.dynamic_slice(gmod, (0, chip * 1536), (64, 1536))
    g_w_ada_s, pctx = _ada_bwd(cin, gmod_s, w_ada_s)
    parts = _allgather8(pctx, "gather_cctx")
    g_b_ada, g_c_ctx = _ada_small(gmod, parts, c_ctx.reshape(1, D))

    g_w_in_full = _unperm_w_in(g_w_in_p)
    g_w_in8 = g_w_in_full.reshape(D, 4, 900).transpose(1, 0, 2).reshape(8, D // 2, 900)
    names = ["w_in", "w_out", "w_ffn_in", "w_ffn_out"]
    rs = list(_scatter_grads([g_w_in8])) + [r_w_out, r_w_ffn_in, r_w_ffn_out]
    full = _sibling_exchange([_sum_slots(r, "sum_" + nm) for r, nm in zip(rs, names)])
    g_w_in_s = full[0].reshape(1, D, 900)
    g_w_out_s = full[1].reshape(1, D // 4, D)
    g_w_ffn_in_s = full[2].reshape(1, D, FFT)
    g_w_ffn_out_s = full[3].reshape(1, DFF // 4, D)

    grads = dict(
        c_ctx=g_c_ctx.reshape(D), norm1_w=g_n1, norm2_w=g_n2, w_ada=g_w_ada_s.reshape(1, D, 1536), b_ada=g_b_ada,
        w_in=g_w_in_s, dn_conv_w=g_conv_s, dn_A_log=g_misc[0:8].reshape(1, 2, 4), dn_dt_bias=g_misc[8:16].reshape(1, 2, 4),
        dn_out_norm_w=g_misc[128:256].reshape(1, 128), na_q_norm_w=g_misc[256:320].reshape(1, 64),
        na_k_norm_w=g_misc[320:384].reshape(1, 64), na_rpb=g_rpb, w_out=g_w_out_s, w_ffn_in=g_w_ffn_in_s,
        w_ffn_out=g_w_ffn_out_s)
    pk = lambda d_: _pack_small(d_["c_ctx"], d_["norm1_w"], d_["norm2_w"], d_["b_ada"], d_["dn_A_log"], d_["dn_dt_bias"],
                                d_["dn_out_norm_w"], d_["na_q_norm_w"], d_["na_k_norm_w"], d_["na_rpb"], d_["dn_conv_w"])
    wts = dict(c_ctx=c_ctx, norm1_w=norm1_w, norm2_w=norm2_w, b_ada=b_ada, dn_A_log=dn_A_log, dn_dt_bias=dn_dt_bias,
               dn_out_norm_w=dn_out_norm_w, na_q_norm_w=na_q_norm_w, na_k_norm_w=na_k_norm_w, na_rpb=na_rpb, dn_conv_w=dn_conv_w)
    ms = dict(c_ctx=m_c_ctx, norm1_w=m_norm1_w, norm2_w=m_norm2_w, b_ada=m_b_ada, dn_A_log=m_dn_A_log, dn_dt_bias=m_dn_dt_bias,
              dn_out_norm_w=m_dn_out_norm_w, na_q_norm_w=m_na_q_norm_w, na_k_norm_w=m_na_k_norm_w, na_rpb=m_na_rpb,
              dn_conv_w=m_dn_conv_w)
    vs = dict(c_ctx=v_c_ctx, norm1_w=v_norm1_w, norm2_w=v_norm2_w, b_ada=v_b_ada, dn_A_log=v_dn_A_log, dn_dt_bias=v_dn_dt_bias,
              dn_out_norm_w=v_dn_out_norm_w, na_q_norm_w=v_na_q_norm_w, na_k_norm_w=v_na_k_norm_w, na_rpb=v_na_rpb,
              dn_conv_w=v_dn_conv_w)
    sd, snm, snv, _ = _adamw(pk(wts), pk(grads), pk(ms), pk(vs), "adamw_small")
    delta, new_m, new_v = _unpack_small(sd), _unpack_small(snm), _unpack_small(snv)
    big = dict(w_ada=(w_ada, m_w_ada, v_w_ada), w_in=(w_in, m_w_in, v_w_in), w_out=(w_out, m_w_out, v_w_out),
               w_ffn_in=(w_ffn_in, m_w_ffn_in, v_w_ffn_in), w_ffn_out=(w_ffn_out, m_w_ffn_out, v_w_ffn_out))
    for nm, (w_, m_, v_) in big.items():
        shp = w_.shape
        r2 = lambda a: a.reshape(shp[1], shp[2])
        d_, nm_, nv_, g_ = _adamw(r2(w_), r2(grads[nm]), r2(m_), r2(v_), "adamw_" + nm)
        delta[nm], new_m[nm], new_v[nm], grads[nm] = d_.reshape(shp), nm_.reshape(shp), nv_.reshape(shp), g_.reshape(shp)

    return (loss, grad_x, *[grads[n] for n in _WEIGHTS], *[delta[n] for n in _WEIGHTS],
            *[new_m[n] for n in _WEIGHTS], *[new_v[n] for n in _WEIGHTS])
```

```python
import functools

import numpy as np
import jax
import jax.numpy as jnp
from jax import lax
from jax.experimental import pallas as pl
from jax.experimental.pallas import tpu as pltpu

F32 = jnp.float32
BF16 = jnp.bfloat16

D = 1024
NB = 2
SEQ = 2048
CT = 256
T = CT + SEQ
NTOK = NB * SEQ
CH = 64
NCH = T // CH
NCH_CTX = CT // CH
DNH, DNW = 4, 512
NAH, NAW = 8, 512
GRID_W = 64
ROWS = SEQ // GRID_W
WIN_ROWS = 8
WIN_COLS = 16
NWIN = WIN_ROWS * GRID_W
DFF = 2816
FFT = 1408
IN_COLS = 3600
PC = 3712
COL_Z, COL_NA, COL_G = 1536, 2048, 3584
EPS = 1e-6
NEG = -1e30
RT = 256
VMEM_BIG = 56 * 1024 * 1024
MESH = pl.DeviceIdType.MESH

ADAM_LR, ADAM_B1, ADAM_B2, ADAM_EPS, ADAM_WD, ADAM_STEP = 0.001, 0.9, 0.999, 1e-08, 0.01, 10


def _cp(vmem=None, sem=None):
    kw = {}
    if vmem is not None:
        kw["vmem_limit_bytes"] = vmem
    if sem is not None:
        kw["dimension_semantics"] = sem
    return pltpu.CompilerParams(**kw)


_NN = (((1,), (0,)), ((), ()))
_NT = (((1,), (1,)), ((), ()))
_TN = (((0,), (0,)), ((), ()))


def _dg(a, b, dims):
    return lax.dot_general(a.astype(BF16), b.astype(BF16), dims, preferred_element_type=F32)


@jax.custom_vjp
def _mm(a, b):
    return _dg(a, b, _NN)


_mm.defvjp(lambda a, b: (_dg(a, b, _NN), (a, b)),
           lambda r, g: (_dg(g, r[1], _NT), _dg(r[0], g, _TN)))


@jax.custom_vjp
def _mm_nt(a, b):
    return _dg(a, b, _NT)


_mm_nt.defvjp(lambda a, b: (_dg(a, b, _NT), (a, b)),
              lambda r, g: (_dg(g, r[1], _NN), _dg(g, r[0], _TN)))


@jax.custom_vjp
def _mm_tn(a, b):
    return _dg(a, b, _TN)


_mm_tn.defvjp(lambda a, b: (_dg(a, b, _TN), (a, b)),
              lambda r, g: (_dg(r[1], g, _NT), _dg(r[0], g, _NN)))


def _split_dot(a, b, dims):
    hi = b.astype(BF16)
    lo = (b - hi.astype(F32)).astype(BF16)
    ab = a.astype(BF16)
    return (lax.dot_general(ab, hi, dims, preferred_element_type=F32)
            + lax.dot_general(ab, lo, dims, preferred_element_type=F32))


@jax.custom_vjp
def _mask_mm(mask, b):
    return _split_dot(mask, b, _NN)


_mask_mm.defvjp(lambda mask, b: (_split_dot(mask, b, _NN), mask),
                lambda mask, g: (jnp.zeros_like(mask), _split_dot(mask, g, _TN)))


def _hmm(a, b):
    return jnp.dot(a, b, precision=lax.Precision.HIGHEST, preferred_element_type=F32)


def _sigmoid(x):
    return 1.0 / (1.0 + jnp.exp(-x))


def _silu(x):
    return x * _sigmoid(x)


def _norm_mod(x, w, sh, sc):
    r = lax.rsqrt(jnp.mean(x * x, axis=-1, keepdims=True) + EPS)
    return (x * r * w) * (1.0 + sc) + sh


def _my_pos():
    return lax.axis_index("x"), lax.axis_index("y"), lax.axis_index("c")


def _flip(pos, k):
    x, y, c = pos
    return (1 - x if k & 4 else x, 1 - y if k & 2 else y, 1 - c if k & 1 else c)


def _slot(pos):
    return 4 * pos[0] + 2 * pos[1] + pos[2]


def _allgather8(v, name, with_sum=False):
    R, W = v.shape

    def body(v_ref, out_ref, *rest):
        if with_sum:
            tot_ref, send_sems, recv_sems, lsem = rest
        else:
            send_sems, recv_sems, lsem = rest
        me = _my_pos()
        mine = pltpu.make_async_copy(v_ref, out_ref.at[_slot(me)], lsem)
        mine.start()
        sends = []
        for k in range(1, 8):
            cp = pltpu.make_async_remote_copy(
                src_ref=v_ref, dst_ref=out_ref.at[_slot(me)], send_sem=send_sems.at[k - 1],
                recv_sem=recv_sems.at[k - 1], device_id=_flip(me, k), device_id_type=MESH)
            cp.start()
            sends.append(cp)
        for k in range(1, 8):
            peer = _flip(me, k)
            pltpu.make_async_remote_copy(
                src_ref=v_ref, dst_ref=out_ref.at[_slot(peer)], send_sem=send_sems.at[k - 1],
                recv_sem=recv_sems.at[k - 1], device_id=peer, device_id_type=MESH).wait_recv()
        for cp in sends:
            cp.wait_send()
        mine.wait()
        if with_sum:
            acc = out_ref[0]
            for s in range(1, 8):
                acc = acc + out_ref[s]
            tot_ref[...] = acc

    out_shape = [jax.ShapeDtypeStruct((8, R, W), F32)]
    out_specs = [pl.BlockSpec(memory_space=pltpu.VMEM)]
    if with_sum:
        out_shape.append(jax.ShapeDtypeStruct((R, W), F32))
        out_specs.append(pl.BlockSpec(memory_space=pltpu.VMEM))
    res = pl.pallas_call(
        body, name=name, out_shape=out_shape,
        in_specs=[pl.BlockSpec(memory_space=pltpu.VMEM)], out_specs=out_specs,
        scratch_shapes=[pltpu.SemaphoreType.DMA((7,)), pltpu.SemaphoreType.DMA((7,)), pltpu.SemaphoreType.DMA],
        compiler_params=_cp(vmem=VMEM_BIG),
    )(v)
    return res if with_sum else res[0]


def _gather_weights(halves):
    n = len(halves)

    def body(*refs):
        ins, outs = refs[:n], refs[n:2 * n]
        send_sems, recv_sems, lsems = refs[2 * n:]
        me = _my_pos()
        x, y, c = me
        sib = (x, y, 1 - c)
        chips = [4, 2, 6]

        def cp(a, k, block, to, src=None):
            dst = outs[a].at[_slot(block)]
            return pltpu.make_async_remote_copy(
                src_ref=dst if src is None else src, dst_ref=dst, send_sem=send_sems.at[a, k],
                recv_sem=recv_sems.at[a, k], device_id=to, device_id_type=MESH)

        mine, first, passed = [], [], []
        for a in range(n):
            m = pltpu.make_async_copy(ins[a], outs[a].at[_slot(me)], lsems.at[a])
            m.start()
            mine.append(m)
            first.append(cp(a, 0, me, sib, src=ins[a]))
            for j, k in enumerate(chips):
                first.append(cp(a, 1 + j, me, _flip(me, k), src=ins[a]))
        for f in first:
            f.start()
        for j, k in enumerate(chips):
            for a in range(n):
                cp(a, 1 + j, _flip(me, k), me).wait_recv()
                p = cp(a, 4 + j, _flip(me, k), sib)
                p.start()
                passed.append(p)
        for a in range(n):
            cp(a, 0, sib, me).wait_recv()
            for j, k in enumerate(chips):
                cp(a, 4 + j, _flip(sib, k), me).wait_recv()
        for f in first + passed:
            f.wait_send()
        for m in mine:
            m.wait()

    hbm = pl.BlockSpec(memory_space=pltpu.HBM)
    return pl.pallas_call(
        body, name="gather_weights",
        out_shape=[jax.ShapeDtypeStruct((8,) + h.shape, h.dtype) for h in halves],
        in_specs=[hbm] * n, out_specs=[hbm] * n,
        scratch_shapes=[pltpu.SemaphoreType.DMA((n, 7)), pltpu.SemaphoreType.DMA((n, 7)),
                        pltpu.SemaphoreType.DMA((n,))],
    )(*halves)


def _xfer_start(plan, sems):
    send_sems, recv_sems, lsems = sems
    locals_, xfers = plan
    for i, (src, dst) in enumerate(locals_):
        pltpu.make_async_copy(src, dst, lsems.at[i]).start()
    for i, (src, dst, _, peer) in enumerate(xfers):
        pltpu.make_async_remote_copy(src_ref=src, dst_ref=dst, send_sem=send_sems.at[i], recv_sem=recv_sems.at[i],
                                     device_id=peer, device_id_type=MESH).start()


def _xfer_wait(plan, sems):
    send_sems, recv_sems, lsems = sems
    locals_, xfers = plan
    for i, (src, _, landing, peer) in enumerate(xfers):
        pltpu.make_async_remote_copy(src_ref=src, dst_ref=landing, send_sem=send_sems.at[i], recv_sem=recv_sems.at[i],
                                     device_id=peer, device_id_type=MESH).wait_recv()
    for i, (src, dst, _, peer) in enumerate(xfers):
        pltpu.make_async_remote_copy(src_ref=src, dst_ref=dst, send_sem=send_sems.at[i], recv_sem=recv_sems.at[i],
                                     device_id=peer, device_id_type=MESH).wait_send()
    for i, (src, dst) in enumerate(locals_):
        pltpu.make_async_copy(src, dst, lsems.at[i]).wait()


def _xfer_sems(n_local, n_xfer):
    return [pltpu.SemaphoreType.DMA((n_xfer,)), pltpu.SemaphoreType.DMA((n_xfer,)), pltpu.SemaphoreType.DMA((n_local,))]


def _scatter_plan(ins, outs, pos):
    me = _slot(pos)
    locals_, xfers = [], []
    for src, dst in zip(ins, outs):
        locals_.append((src.at[me], dst.at[me]))
        for k in range(1, 8):
            peer = _flip(pos, k)
            xfers.append((src.at[_slot(peer)], dst.at[me], dst.at[_slot(peer)], peer))
    return locals_, xfers


def _scatter_grads(grads):
    n = len(grads)

    def body(*refs):
        plan = _scatter_plan(refs[:n], refs[n:2 * n], _my_pos())
        _xfer_start(plan, refs[2 * n:])
        _xfer_wait(plan, refs[2 * n:])

    hbm = pl.BlockSpec(memory_space=pltpu.HBM)
    return pl.pallas_call(
        body, name="scatter_grads", out_shape=[jax.ShapeDtypeStruct(g.shape, g.dtype) for g in grads],
        in_specs=[hbm] * n, out_specs=[hbm] * n, scratch_shapes=_xfer_sems(n, 7 * n),
    )(*grads)


def _sibling_exchange(halves):
    n = len(halves)

    def body(*refs):
        ins, outs = refs[:n], refs[n:2 * n]
        send_sems, recv_sems, lsems = refs[2 * n:]
        x, y, c = _my_pos()
        sib = (x, y, 1 - c)
        mine, sends = [], []
        for a in range(n):
            m = pltpu.make_async_copy(ins[a], outs[a].at[c], lsems.at[a])
            m.start()
            mine.append(m)
            cp = pltpu.make_async_remote_copy(
                src_ref=ins[a], dst_ref=outs[a].at[c], send_sem=send_sems.at[a], recv_sem=recv_sems.at[a],
                device_id=sib, device_id_type=MESH)
            cp.start()
            sends.append(cp)
        for a in range(n):
            pltpu.make_async_remote_copy(
                src_ref=ins[a], dst_ref=outs[a].at[1 - c], send_sem=send_sems.at[a], recv_sem=recv_sems.at[a],
                device_id=sib, device_id_type=MESH).wait_recv()
        for cp in sends:
            cp.wait_send()
        for m in mine:
            m.wait()

    vm = pl.BlockSpec(memory_space=pltpu.VMEM)
    return pl.pallas_call(
        body, name="sibling_exchange",
        out_shape=[jax.ShapeDtypeStruct((2,) + h.shape, h.dtype) for h in halves],
        in_specs=[vm] * n, out_specs=[vm] * n,
        scratch_shapes=[pltpu.SemaphoreType.DMA((n,)), pltpu.SemaphoreType.DMA((n,)), pltpu.SemaphoreType.DMA((n,))],
        compiler_params=_cp(vmem=VMEM_BIG),
    )(*halves)


def _row_tile(r, cap=512):
    for t in (cap, 256, 128, 64, 32, 16, 8):
        if t <= cap and r % t == 0:
            return t
    return r


def _cast_bf16(a, name):
    R, W = a.shape
    tr = _row_tile(R, 256)

    def body(a_ref, o_ref):
        o_ref[...] = a_ref[...].astype(BF16)

    return pl.pallas_call(
        body, name=name, grid=(R // tr,), out_shape=jax.ShapeDtypeStruct((R, W), BF16),
        in_specs=[pl.BlockSpec((tr, W), lambda i: (i, 0))], out_specs=pl.BlockSpec((tr, W), lambda i: (i, 0)),
    )(a)


def _sum_slots(r, name):
    _, R, W = r.shape
    tr = _row_tile(R, 128)

    def body(r_ref, o_ref):
        acc = r_ref[0].astype(F32)
        for s_ in range(1, 8):
            acc = acc + r_ref[s_].astype(F32)
        o_ref[...] = acc.astype(BF16)

    return pl.pallas_call(
        body, name=name, grid=(R // tr,), out_shape=jax.ShapeDtypeStruct((R, W), BF16),
        in_specs=[pl.BlockSpec((8, tr, W), lambda i: (0, i, 0))], out_specs=pl.BlockSpec((tr, W), lambda i: (i, 0)),
    )(r)


def _adamw(w, g, m, v, name):
    R, W = w.shape
    tr = _row_tile(R, 256)
    c1 = 1.0 / (1.0 - ADAM_B1 ** ADAM_STEP)
    c2 = 1.0 / (1.0 - ADAM_B2 ** ADAM_STEP)

    def body(w_ref, g_ref, m_ref, v_ref, d_ref, nm_ref, nv_ref, g32_ref):
        gg = g_ref[...].astype(F32)
        g32_ref[...] = gg
        nm = ADAM_B1 * m_ref[...] + (1.0 - ADAM_B1) * gg
        nv = ADAM_B2 * v_ref[...] + (1.0 - ADAM_B2) * (gg * gg)
        d_ref[...] = -ADAM_LR * ((nm * c1) / (jnp.sqrt(nv * c2) + ADAM_EPS) + ADAM_WD * w_ref[...])
        nm_ref[...] = nm
        nv_ref[...] = nv

    spec = pl.BlockSpec((tr, W), lambda i: (i, 0))
    return pl.pallas_call(
        body, name=name, grid=(R // tr,), out_shape=[jax.ShapeDtypeStruct((R, W), F32)] * 4,
        in_specs=[spec] * 4, out_specs=[spec] * 4,
    )(w, g, m, v)


def _matmul_tn(a, b, tm, tn, tk, name, nn=1, b_map=None, out_shape=None, out_block=None, out_map=None,
               b_block=None):
    K, M = a.shape
    nk = K // tk
    nm = M // tm

    def body(a_ref, b_ref, o_ref, acc_ref):
        kk = pl.program_id(2)

        @pl.when(kk == 0)
        def _():
            acc_ref[...] = jnp.zeros_like(acc_ref)

        acc_ref[...] += lax.dot_general(a_ref[...], b_ref[...], _TN, preferred_element_type=F32)

        @pl.when(kk == nk - 1)
        def _():
            o_ref[...] = acc_ref[...].astype(BF16)

    return pl.pallas_call(
        body, name=name, grid=(nm, nn, nk),
        out_shape=jax.ShapeDtypeStruct(out_shape or (M, b.shape[-1]), BF16),
        in_specs=[pl.BlockSpec((tk, tm), lambda i, j, k: (k, i)),
                  pl.BlockSpec(b_block or (tk, tn), b_map or (lambda i, j, k: (k, j)))],
        out_specs=pl.BlockSpec(out_block or (tm, tn), out_map or (lambda i, j, k: (i, j))),
        scratch_shapes=[pltpu.VMEM((tm, tn), F32)],
        compiler_params=_cp(vmem=VMEM_BIG),
    )(a, b)


def _ada_fwd(cin, w_ada_s, b_s):
    tn = 512

    def body(c_ref, w_ref, b_ref, o_ref):
        o_ref[...] = _dg(_silu(c_ref[...]), w_ref[...], _NN) + b_ref[...]

    return pl.pallas_call(
        body, name="ada_fwd", grid=(1536 // tn,), out_shape=jax.ShapeDtypeStruct((64, 1536), F32),
        in_specs=[pl.BlockSpec((64, D), lambda j: (0, 0)), pl.BlockSpec((D, tn), lambda j: (0, j)),
                  pl.BlockSpec((1, tn), lambda j: (0, j))],
        out_specs=pl.BlockSpec((64, tn), lambda j: (0, j)),
        compiler_params=_cp(vmem=VMEM_BIG),
    )(cin, w_ada_s, b_s)


def _ada_bwd(cin, gmod_s, w_ada_s):
    tn = 512

    def body(c_ref, g_ref, w_ref, gw_ref, pc_ref):
        j = pl.program_id(0)
        g = g_ref[...]
        gw_ref[...] = _dg(_silu(c_ref[...]), g, _TN)
        rr = lax.broadcasted_iota(jnp.int32, (8, 64), 0)
        cc = lax.broadcasted_iota(jnp.int32, (8, 64), 1)
        sel = jnp.where((rr == 0) & ((cc & 7) == 2), 1.0, 0.0).astype(F32)
        dctx = _hmm(sel, g)
        part = _dg(dctx, w_ref[...], _NT)

        @pl.when(j == 0)
        def _():
            pc_ref[...] = jnp.zeros_like(pc_ref)

        pc_ref[...] += part

    return pl.pallas_call(
        body, name="ada_bwd", grid=(1536 // tn,),
        out_shape=[jax.ShapeDtypeStruct((D, 1536), F32), jax.ShapeDtypeStruct((8, D), F32)],
        in_specs=[pl.BlockSpec((64, D), lambda j: (0, 0)), pl.BlockSpec((64, tn), lambda j: (0, j)),
                  pl.BlockSpec((D, tn), lambda j: (0, j))],
        out_specs=[pl.BlockSpec((D, tn), lambda j: (0, j)), pl.BlockSpec((8, D), lambda j: (0, 0))],
        compiler_params=_cp(vmem=VMEM_BIG),
    )(cin, gmod_s, w_ada_s)


def _ada_small(gmod, parts, c_ctx):
    def body(g_ref, p_ref, c_ref, gb_ref, gc_ref):
        gb_ref[...] = jnp.sum(g_ref[...], axis=0, keepdims=True)
        ds = p_ref[0][0:1] + p_ref[2][0:1] + p_ref[4][0:1] + p_ref[6][0:1]
        cc = c_ref[...]
        s = _sigmoid(cc)
        gc_ref[...] = ds * (s * (1.0 + cc * (1.0 - s)))

    vm = pl.BlockSpec(memory_space=pltpu.VMEM)
    return pl.pallas_call(
        body, name="ada_small",
        out_shape=[jax.ShapeDtypeStruct((1, 6 * D), F32), jax.ShapeDtypeStruct((1, D), F32)],
        in_specs=[vm, vm, vm], out_specs=[vm, vm],
    )(gmod, parts, c_ctx)


def _tok_specs():
    xs = pl.BlockSpec((None, RT, D), lambda b, j: (b, jnp.maximum(j - 1, 0), 0))
    cs = pl.BlockSpec((None, RT, D), lambda b, j: (b, 0, 0))
    ms = pl.BlockSpec((None, 1, D), lambda b, j: (b, 0, 0))
    cms = pl.BlockSpec((None, 1, D), lambda b, j: (0, 0, 0))
    return xs, cs, ms, cms


def _inproj_fwd(x, ctx, sh1, sc1, csh1, csc1, n1w, w_in_p):
    def body(x_ref, c_ref, sh_ref, sc_ref, csh_ref, csc_ref, w1_ref, w_ref, p_ref, hm_ref):
        is_ctx = pl.program_id(1) == 0
        xt = jnp.where(is_ctx, c_ref[...], x_ref[...])
        sh = jnp.where(is_ctx, csh_ref[...], sh_ref[...])
        sc = jnp.where(is_ctx, csc_ref[...], sc_ref[...])
        hb = _norm_mod(xt, w1_ref[...], sh, sc).astype(BF16)
        hm_ref[...] = hb
        p_ref[...] = jnp.dot(hb, w_ref[...], preferred_element_type=F32)

    xs, cs, ms, cms = _tok_specs()
    return pl.pallas_call(
        body, name="inproj_fwd", grid=(NB, T // RT),
        out_shape=[jax.ShapeDtypeStruct((NB, T, PC), F32), jax.ShapeDtypeStruct((NB, T, D), BF16)],
        in_specs=[xs, cs, ms, ms, cms, cms, pl.BlockSpec((1, D), lambda b, j: (0, 0)),
                  pl.BlockSpec((D, PC), lambda b, j: (0, 0))],
        out_specs=[pl.BlockSpec((None, RT, PC), lambda b, j: (b, j, 0)),
                   pl.BlockSpec((None, RT, D), lambda b, j: (b, j, 0))],
        compiler_params=_cp(vmem=VMEM_BIG),
    )(x, ctx, sh1, sc1, csh1, csc1, n1w, w_in_p)


def _inproj_bwd(dp_dn, dz, dp_na, dgt, w_in_p, x, ctx, sh1, sc1, csh1, csc1, n1w, dx1):
    def body(d1q_ref, d1k_ref, d1v_ref, d2_ref, d3_ref, d4_ref, w_ref, x_ref, c_ref, sh_ref, sc_ref, csh_ref, csc_ref,
             w1_ref, dx1_ref, dpb_ref, gx_ref, dsh_ref, dsc_ref, dcsh_ref, dcsc_ref, dw1_ref):
        b, j = pl.program_id(0), pl.program_id(1)
        is_ctx = j == 0
        dp = jnp.concatenate([d1q_ref[...], d1k_ref[...], d1v_ref[...], d2_ref[...], d3_ref[...], d4_ref[...]],
                             axis=1).astype(BF16)
        dpb_ref[...] = dp
        dhm = lax.dot_general(dp, w_ref[...], _NT, preferred_element_type=F32)
        xt = jnp.where(is_ctx, c_ref[...], x_ref[...])
        sh = jnp.where(is_ctx, csh_ref[...], sh_ref[...])
        sc = jnp.where(is_ctx, csc_ref[...], sc_ref[...])
        _, vjp = jax.vjp(_norm_mod, xt, w1_ref[...], sh, sc)
        dxt, dw1, dsh, dsc = vjp(dhm)

        @pl.when((b == 0) & (j == 0))
        def _():
            dw1_ref[...] = jnp.zeros_like(dw1_ref)
            dcsh_ref[...] = jnp.zeros_like(dcsh_ref)
            dcsc_ref[...] = jnp.zeros_like(dcsc_ref)

        dw1_ref[...] += dw1

        @pl.when(is_ctx)
        def _():
            dcsh_ref[...] += dsh
            dcsc_ref[...] += dsc

        @pl.when(j == 1)
        def _():
            dsh_ref[...] = jnp.zeros_like(dsh_ref)
            dsc_ref[...] = jnp.zeros_like(dsc_ref)

        @pl.when(j >= 1)
        def _():
            dsh_ref[...] += dsh
            dsc_ref[...] += dsc
            gx_ref[...] = dx1_ref[...] + dxt

    xs, cs, ms, cms = _tok_specs()
    tsp = lambda w: pl.BlockSpec((None, RT, w), lambda b, j: (b, j, 0))
    one = pl.BlockSpec((1, D), lambda b, j: (0, 0))
    return pl.pallas_call(
        body, name="inproj_bwd", grid=(NB, T // RT),
        out_shape=[jax.ShapeDtypeStruct((NB, T, PC), BF16), jax.ShapeDtypeStruct((NB, SEQ, D), F32),
                   jax.ShapeDtypeStruct((NB, 1, D), F32), jax.ShapeDtypeStruct((NB, 1, D), F32),
                   jax.ShapeDtypeStruct((1, 1, D), F32), jax.ShapeDtypeStruct((1, 1, D), F32),
                   jax.ShapeDtypeStruct((1, D), F32)],
        in_specs=[tsp(512), tsp(512), tsp(512), tsp(512), tsp(1536), tsp(128), pl.BlockSpec((D, PC), lambda b, j: (0, 0)),
                  xs, cs, ms, ms, cms, cms, one, xs],
        out_specs=[tsp(PC), xs, ms, ms, cms, cms, one],
        compiler_params=_cp(vmem=VMEM_BIG),
    )(*dp_dn, dz, dp_na, dgt, w_in_p, x, ctx, sh1, sc1, csh1, csc1, n1w, dx1)


def _seg_valid(t, off):
    lo = jnp.where(t < CT, 0, CT)
    hi = jnp.where(t < CT, CT, T)
    return (t + off >= lo) & (t + off < hi)


def _shifted(u, off):
    t = lax.broadcasted_iota(jnp.int32, (T, 1), 0)
    s = u if off == 0 else pltpu.roll(u, (-off) % T, 0)
    return jnp.where(_seg_valid(t, off), s, 0.0)


def _conv_pre(u, w):
    acc = _shifted(u, -2) * w[0:1, :]
    for kk in range(1, 5):
        acc = acc + _shifted(u, kk - 2) * w[kk:kk + 1, :]
    return acc


def _act_q(pre):
    a = _silu(pre)
    return a * lax.rsqrt(jnp.sum(a * a, axis=-1, keepdims=True) + EPS) * (128.0 ** -0.5)


def _act_k(pre):
    a = _silu(pre)
    return a * lax.rsqrt(jnp.sum(a * a, axis=-1, keepdims=True) + EPS)


def _dn_prep_fwd(p, cw):
    def body(pq_ref, pk_ref, pv_ref, wq_ref, wk_ref, wv_ref, q_ref, k_ref, v_ref):
        q_ref[...] = _act_q(_conv_pre(pq_ref[...], wq_ref[...]))
        k_ref[...] = _act_k(_conv_pre(pk_ref[...], wk_ref[...]))
        v_ref[...] = _silu(_conv_pre(pv_ref[...], wv_ref[...]))

    ps = lambda o: pl.BlockSpec((None, T, 128), lambda b, h: (b, 0, o + h))
    ws = lambda o: pl.BlockSpec((8, 128), lambda b, h: (0, o + h))
    os_ = pl.BlockSpec((None, T, 128), lambda b, h: (b, 0, h))
    return pl.pallas_call(
        body, name="dn_prep_fwd", grid=(NB, DNH), out_shape=[jax.ShapeDtypeStruct((NB, T, DNW), F32)] * 3,
        in_specs=[ps(0), ps(4), ps(8), ws(0), ws(4), ws(8)], out_specs=[os_] * 3,
        compiler_params=_cp(vmem=VMEM_BIG),
    )(p, p, p, cw, cw, cw)


def _dn_prep_bwd(p, cw, dqs, dks, dvs):
    def body(pq_ref, pk_ref, pv_ref, wq_ref, wk_ref, wv_ref, dqf_ref, dqb_ref, dkf_ref, dkb_ref, dvf_ref, dvb_ref,
             oq_ref, ok_ref, ov_ref, gq_ref, gk_ref, gv_ref):
        b = pl.program_id(1)

        def one(p_ref, w_ref, df_ref, db_ref, act, o_ref, g_ref):
            u = p_ref[...]
            w = w_ref[...]
            _, vjp = jax.vjp(act, _conv_pre(u, w))
            (dpre,) = vjp(df_ref[...] + db_ref[...])
            du = _shifted(dpre, 2) * w[0:1, :]
            rows = [jnp.sum(dpre * _shifted(u, -2), axis=0, keepdims=True)]
            for kk in range(1, 5):
                du = du + _shifted(dpre, 2 - kk) * w[kk:kk + 1, :]
                rows.append(jnp.sum(dpre * _shifted(u, kk - 2), axis=0, keepdims=True))
            o_ref[...] = du
            gw = jnp.concatenate(rows + [jnp.zeros((3, 128), F32)], axis=0)

            @pl.when(b == 0)
            def _():
                g_ref[...] = gw

            @pl.when(b > 0)
            def _():
                g_ref[...] += gw

        one(pq_ref, wq_ref, dqf_ref, dqb_ref, _act_q, oq_ref, gq_ref)
        one(pk_ref, wk_ref, dkf_ref, dkb_ref, _act_k, ok_ref, gk_ref)
        one(pv_ref, wv_ref, dvf_ref, dvb_ref, _silu, ov_ref, gv_ref)

    ps = lambda o: pl.BlockSpec((None, T, 128), lambda h, b: (b, 0, o + h))
    ws = lambda o: pl.BlockSpec((8, 128), lambda h, b: (0, o + h))
    ds = pl.BlockSpec((None, T, 128), lambda h, b: (b, 0, h))
    outs = pl.pallas_call(
        body, name="dn_prep_bwd", grid=(DNH, NB),
        out_shape=[jax.ShapeDtypeStruct((NB, T, DNW), F32)] * 3 + [jax.ShapeDtypeStruct((8, DNW), F32)] * 3,
        in_specs=[ps(0), ps(4), ps(8), ws(0), ws(4), ws(8)] + [ds] * 6,
        out_specs=[ds] * 3
        + [pl.BlockSpec((8, 128), lambda h, b: (0, h))] * 3,
        compiler_params=_cp(vmem=VMEM_BIG),
    )(p, p, p, cw, cw, cw, *dqs, *dks, *dvs)
    return outs


def _gate_fn(a, prm):
    lane = lax.broadcasted_iota(jnp.int32, a.shape, 1)
    z = a + prm[1:2, :]
    sp = jnp.maximum(z, 0.0) + jnp.log(1.0 + jnp.exp(-jnp.abs(z)))
    g = -jnp.exp(prm[0:1, :]) * sp
    return jnp.where(lane < 8, g, jnp.where(lane < 16, _sigmoid(a), 0.0))


def _gates_fwd(p, prm):
    def body(a_ref, prm_ref, o_ref):
        o_ref[...] = _gate_fn(a_ref[...], prm_ref[...])

    return pl.pallas_call(
        body, name="gates_fwd", grid=(NB,), out_shape=jax.ShapeDtypeStruct((NB, T, 128), F32),
        in_specs=[pl.BlockSpec((None, T, 128), lambda b: (b, 0, COL_G // 128)), pl.BlockSpec((8, 128), lambda b: (0, 0))],
        out_specs=pl.BlockSpec((None, T, 128), lambda b: (b, 0, 0)),
    )(p, prm)


def _gates_bwd(p, prm, dgf, dgb):
    def body(a_ref, prm_ref, dgf_ref, dgb_ref, o_ref, dprm_ref):
        b = pl.program_id(0)
        tot = dgf_ref[...] + dgb_ref[...]
        _, vjp = jax.vjp(_gate_fn, a_ref[...], prm_ref[...])
        da, dprm = vjp(tot)
        o_ref[...] = da

        @pl.when(b == 0)
        def _():
            dprm_ref[...] = dprm

        @pl.when(b > 0)
        def _():
            dprm_ref[...] += dprm

    return pl.pallas_call(
        body, name="gates_bwd", grid=(NB,),
        out_shape=[jax.ShapeDtypeStruct((NB, T, 128), F32), jax.ShapeDtypeStruct((8, 128), F32)],
        in_specs=[pl.BlockSpec((None, T, 128), lambda b: (b, 0, COL_G // 128)), pl.BlockSpec((8, 128), lambda b: (0, 0)),
                  pl.BlockSpec((None, T, 128), lambda b: (b, 0, 0)), pl.BlockSpec((None, T, 128), lambda b: (b, 0, 0))],
        out_specs=[pl.BlockSpec((None, T, 128), lambda b: (b, 0, 0)), pl.BlockSpec((8, 128), lambda b: (0, 0))],
        compiler_params=_cp(vmem=VMEM_BIG),
    )(p, prm, dgf, dgb)


def _tri_inv_impl(m, md, m1, m2, eye):
    n = -(m * md)
    t = eye + n
    pw = n
    for _ in range(3):
        pw = _dg(pw, pw, _NN)
        t = t + _dg(t, pw, _NN)
    t = t - _dg(_dg(t, m * m1, _NN), t, _NN)
    t = t - _dg(_dg(t, m * m2, _NN), t, _NN)
    return t


@jax.custom_vjp
def _tri_inv(m, md, m1, m2, eye):
    return _tri_inv_impl(m, md, m1, m2, eye)


def _tri_inv_f(m, md, m1, m2, eye):
    t = _tri_inv_impl(m, md, m1, m2, eye)
    return t, (t, md)


def _tri_inv_b(res, g):
    t, md = res
    tt = t.T
    z = jnp.zeros_like(md)
    return (-_dg(_dg(tt, g, _NN), tt, _NN), z, z, z, z)


_tri_inv.defvjp(_tri_inv_f, _tri_inv_b)


SR = DNH * CH


def _scan_masks():
    i, j = np.arange(SR)[:, None], np.arange(SR)[None, :]
    same = (i >> 6) == (j >> 6)
    cm = np.zeros((2, 6, SR, SR), np.float32)
    for d in range(2):
        diff = i - j if d == 0 else j - i
        cm[d, 0] = same & (diff >= 0)
        cm[d, 1] = same & (diff > 0)
        cm[d, 2] = (i >> 4) == (j >> 4)
        cm[d, 3] = ((i >> 5) == (j >> 5)) & ((i >> 4) != (j >> 4))
        cm[d, 4] = (i >> 5) != (j >> 5)
        cm[d, 5] = i == j
    t, u = np.arange(CH)[:, None], np.arange(CH)[None, :]
    c64 = np.stack([t >= u, t <= u]).astype(np.float32)
    bm = ((np.arange(SR)[:, None] >> 6) == (np.arange(DNW)[None, :] >> 7)).astype(np.float32)
    return jnp.asarray(cm), jnp.asarray(c64), jnp.asarray(bm)


@jax.custom_vjp
def _fold(x, bm):
    xm = x * bm
    return xm[:, 0:128] + xm[:, 128:256] + xm[:, 256:384] + xm[:, 384:512]


def _fold_f(x, bm):
    return _fold(x, bm), bm


def _fold_b(bm, g):
    return jnp.concatenate([g, g, g, g], axis=1) * bm, jnp.zeros_like(bm)


_fold.defvjp(_fold_f, _fold_b)


def _chunk_fn(q, k, v, gb, s, cm, c64, bm, d):
    incl, strict = cm[0], cm[1]
    lane = lax.broadcasted_iota(jnp.int32, (1, 128), 1)
    oh = lambda n: jnp.where(lane == n, 1.0, 0.0).astype(F32)
    col = lambda x, base: jnp.concatenate(
        [jnp.sum(x * oh(base + d * DNH + h), axis=1, keepdims=True) for h in range(DNH)], axis=0)
    beta = col(gb, 8)
    gam = col(_mask_mm(c64, gb), 0)
    tot = jnp.sum(gb, axis=0, keepdims=True)
    gtots = [jnp.sum(tot * oh(d * DNH + h), axis=1, keepdims=True) for h in range(DNH)]
    gtot = jnp.concatenate([jnp.broadcast_to(t, (CH, 1)) for t in gtots], axis=0)
    eg = jnp.concatenate([jnp.broadcast_to(jnp.exp(t), (1, 128)) for t in gtots], axis=1)
    row0 = jnp.where(lax.broadcasted_iota(jnp.int32, (128, 1), 0) == 0, 1.0, 0.0).astype(F32)
    gam_row = jnp.sum(jnp.broadcast_to(gam, (SR, 128)).T * row0, axis=0, keepdims=True)
    dec = jnp.exp(jnp.where(incl > 0.5, gam - gam_row, NEG))
    e_gam = jnp.exp(gam)
    kb = k * beta
    m = strict * (_mm_nt(kb, k) * dec)
    tm = _tri_inv(m, cm[2], cm[3], cm[4], cm[5])
    u = _mm(tm, v * beta)
    w = _mm(tm, kb * e_gam)
    qk = _mm_nt(q, k) * dec
    v_new = u - _fold(_mm(w, s), bm)
    o = _fold(_mm(q * e_gam, s), bm) + _mm(qk, v_new)
    vbd = jnp.concatenate([v_new, v_new, v_new, v_new], axis=1) * bm
    s_new = s * eg + _mm_tn(k * jnp.exp(gtot - gam), vbd)
    return o, s_new


def _chunk_rev(s):
    return jnp.where(s < NCH_CTX, NCH_CTX - 1 - s, NCH + NCH_CTX - 1 - s)


def _scan_specs(step):
    cf = lambda s: step(s)
    cb = lambda s: _chunk_rev(step(s))
    tok = lambda c: pl.BlockSpec((None, CH, DNW), lambda b, s: (b, c(s), 0))
    gat = lambda c: pl.BlockSpec((None, CH, 128), lambda b, s: (b, c(s), 0))
    sta = lambda c: pl.BlockSpec((None, None, 128, DNW), lambda b, s: (b, c(s), 0, 0))
    return (tok(cf), gat(cf), sta(cf)), (tok(cb), gat(cb), sta(cb))


def _const_specs():
    return [pl.BlockSpec((2, 6, SR, SR), lambda b, s: (0, 0, 0, 0)), pl.BlockSpec((2, CH, CH), lambda b, s: (0, 0, 0)),
            pl.BlockSpec((SR, DNW), lambda b, s: (0, 0))]


def _stack(ref):
    return jnp.concatenate([ref[:, 128 * h:128 * h + 128] for h in range(DNH)], axis=0)


def _unstack_to(ref, val):
    for h in range(DNH):
        ref[:, 128 * h:128 * h + 128] = val[CH * h:CH * h + CH]


def _gather_plan(ins, outs, pos):
    me = _slot(pos)
    locals_, xfers = [], []
    for src, dst in zip(ins, outs):
        locals_.append((src, dst.at[me]))
        for k in range(1, 8):
            peer = _flip(pos, k)
            xfers.append((src, dst.at[me], dst.at[_slot(peer)], peer))
    return locals_, xfers


def _scan_fwd(q, k, v, gates, carry=()):
    nc = len(carry)

    def body(*refs):
        cm_ref, c64_ref, bm_ref, qf, kf, vf, gf, qb, kb, vb, gb = refs[:11]
        c_in = refs[11:11 + nc]
        of_ref, sf_ref, ob_ref, sb_ref = refs[11 + nc:15 + nc]
        c_out = refs[15 + nc:15 + 2 * nc]
        s_ref = refs[15 + 2 * nc]
        sems = refs[16 + 2 * nc:]
        if nc:
            @pl.when((pl.program_id(0) == 0) & (pl.program_id(1) == 0))
            def _():
                _xfer_start(_gather_plan(c_in, c_out, _my_pos()), sems)

        @pl.when(pl.program_id(1) == 0)
        def _():
            s_ref[...] = jnp.zeros_like(s_ref)

        bm = bm_ref[...]
        dirs = ((qf, kf, vf, gf, of_ref, sf_ref), (qb, kb, vb, gb, ob_ref, sb_ref))
        for d, (q_r, k_r, v_r, g_r, o_r, st_r) in enumerate(dirs):
            st = s_ref[d]
            st_r[...] = st
            o, s_new = _chunk_fn(_stack(q_r), _stack(k_r), _stack(v_r), g_r[...], st, cm_ref[d], c64_ref[d], bm, d)
            _unstack_to(o_r, o)
            s_ref[d] = s_new

        if nc:
            @pl.when((pl.program_id(0) == NB - 1) & (pl.program_id(1) == NCH - 1))
            def _():
                _xfer_wait(_gather_plan(c_in, c_out, _my_pos()), sems)

    (tf, gf_, sf_), (tb, gb_, sb_) = _scan_specs(lambda s: s)
    o_sh = jax.ShapeDtypeStruct((NB, T, DNW), F32)
    st_sh = jax.ShapeDtypeStruct((NB, NCH, 128, DNW), F32)
    hbm = pl.BlockSpec(memory_space=pltpu.HBM)
    return pl.pallas_call(
        body, name="scan_fwd", grid=(NB, NCH),
        out_shape=[o_sh, st_sh, o_sh, st_sh] + [jax.ShapeDtypeStruct((8,) + h.shape, h.dtype) for h in carry],
        in_specs=_const_specs() + [tf, tf, tf, gf_, tb, tb, tb, gb_] + [hbm] * nc,
        out_specs=[tf, sf_, tb, sb_] + [hbm] * nc,
        scratch_shapes=[pltpu.VMEM((2, 128, DNW), F32)] + (_xfer_sems(nc, 7 * nc) if nc else []),
        compiler_params=_cp(vmem=VMEM_BIG),
    )(*_scan_masks(), q, k, v, gates, q, k, v, gates, *carry)


def _scan_bwd(q, k, v, gates, st_f, st_b, do, carry=()):
    nc = len(carry)

    def body(*refs):
        (cm_ref, c64_ref, bm_ref, qf, kf, vf, gf, sf, dof, qb, kb, vb, gb, sb, dob) = refs[:15]
        c_in = refs[15:15 + nc]
        dqf, dkf, dvf, dgf, dqb, dkb, dvb, dgb = refs[15 + nc:23 + nc]
        c_out = refs[23 + nc:23 + 2 * nc]
        ds_ref = refs[23 + 2 * nc]
        sems = refs[24 + 2 * nc:]
        first = (pl.program_id(0) == 0) & (pl.program_id(1) == 0)
        last = (pl.program_id(0) == NB - 1) & (pl.program_id(1) == NCH - 1)
        if nc:
            @pl.when(first)
            def _():
                _xfer_start(_scatter_plan(c_in, c_out, _my_pos()), sems)

        @pl.when(pl.program_id(1) == 0)
        def _():
            ds_ref[...] = jnp.zeros_like(ds_ref)

        bm = bm_ref[...]
        dirs = ((qf, kf, vf, gf, sf, dof, dqf, dkf, dvf, dgf), (qb, kb, vb, gb, sb, dob, dqb, dkb, dvb, dgb))
        for d, (q_r, k_r, v_r, g_r, st_r, do_r, dq_r, dk_r, dv_r, dg_r) in enumerate(dirs):
            cm, c64 = cm_ref[d], c64_ref[d]
            _, vjp = jax.vjp(lambda a, b_, c, e, f, cm=cm, c64=c64, d=d: _chunk_fn(a, b_, c, e, f, cm, c64, bm, d),
                             _stack(q_r), _stack(k_r), _stack(v_r), g_r[...], st_r[...])
            dq, dk, dv, dg, ds = vjp((_stack(do_r), ds_ref[d]))
            _unstack_to(dq_r, dq)
            _unstack_to(dk_r, dk)
            _unstack_to(dv_r, dv)
            dg_r[...] = dg
            ds_ref[d] = ds

        if nc:
            @pl.when(last)
            def _():
                _xfer_wait(_scatter_plan(c_in, c_out, _my_pos()), sems)

    (tf, gf_, sf_), (tb, gb_, sb_) = _scan_specs(lambda s: NCH - 1 - s)
    o_sh = jax.ShapeDtypeStruct((NB, T, DNW), F32)
    g_sh = jax.ShapeDtypeStruct((NB, T, 128), F32)
    hbm = pl.BlockSpec(memory_space=pltpu.HBM)
    return pl.pallas_call(
        body, name="scan_bwd", grid=(NB, NCH),
        out_shape=[o_sh, o_sh, o_sh, g_sh] * 2 + [jax.ShapeDtypeStruct(g.shape, g.dtype) for g in carry],
        in_specs=_const_specs() + [tf, tf, tf, gf_, sf_, tf, tb, tb, tb, gb_, sb_, tb] + [hbm] * nc,
        out_specs=[tf, tf, tf, gf_, tb, tb, tb, gb_] + [hbm] * nc,
        scratch_shapes=[pltpu.VMEM((2, 128, DNW), F32)] + (_xfer_sems(nc, 7 * nc) if nc else []),
        compiler_params=_cp(vmem=VMEM_BIG),
    )(*_scan_masks(), q, k, v, gates, st_f, do, q, k, v, gates, st_b, do, *carry)


def _segmean(a, bd):
    hi = a.astype(BF16)
    lo = (a - hi.astype(F32)).astype(BF16)
    return jnp.dot(hi, bd, preferred_element_type=F32) + jnp.dot(lo, bd, preferred_element_type=F32)


def _na_prep_fwd(p, qw, kw, bd):
    def body(q_ref, k_ref, v_ref, qw_ref, kw_ref, bd_ref, oq_ref, ok_ref, ov_ref):
        bdm = bd_ref[...]
        q = q_ref[...]
        k = k_ref[...]
        oq_ref[...] = (q * lax.rsqrt(_segmean(q * q, bdm) + EPS) * qw_ref[...] * 0.125).astype(BF16)
        ok_ref[...] = (k * lax.rsqrt(_segmean(k * k, bdm) + EPS) * kw_ref[...]).astype(BF16)
        ov_ref[...] = v_ref[...].astype(BF16)

    ps = lambda o: pl.BlockSpec((None, RT, NAW), lambda b, j: (b, j, COL_NA // NAW + o))
    one = pl.BlockSpec((1, NAW), lambda b, j: (0, 0))
    os_ = pl.BlockSpec((None, RT, NAW), lambda b, j: (b, j, 0))
    return pl.pallas_call(
        body, name="na_prep_fwd", grid=(NB, T // RT), out_shape=[jax.ShapeDtypeStruct((NB, T, NAW), BF16)] * 3,
        in_specs=[ps(0), ps(1), ps(2), one, one, pl.BlockSpec((NAW, NAW), lambda b, j: (0, 0))], out_specs=[os_] * 3,
    )(p, p, p, qw, kw, bd)


def _na_prep_bwd(p, qw, kw, bd, dq, dk, dv):
    def body(q_ref, k_ref, qw_ref, kw_ref, bd_ref, dq_ref, dk_ref, dv_ref, o_ref, gq_ref, gk_ref):
        b, j = pl.program_id(0), pl.program_id(1)
        bdm = bd_ref[...]

        def one(x, w, dy, scale):
            r = lax.rsqrt(_segmean(x * x, bdm) + EPS)
            g = dy * w * scale
            dx = r * g - x * (r * r * r) * _segmean(g * x, bdm)
            dw = jnp.sum(dy * x * r * scale, axis=0, keepdims=True)
            return dx, dw

        dqv = jnp.where(j == 0, 0.0, dq_ref[...])
        dxq, dwq = one(q_ref[...], qw_ref[...], dqv, 0.125)
        dxk, dwk = one(k_ref[...], kw_ref[...], dk_ref[...], 1.0)
        o_ref[...] = jnp.concatenate([dxq, dxk, dv_ref[...]], axis=1)
        pad = jnp.zeros((7, NAW), F32)

        @pl.when((b == 0) & (j == 0))
        def _():
            gq_ref[...] = jnp.zeros_like(gq_ref)
            gk_ref[...] = jnp.zeros_like(gk_ref)

        gq_ref[...] += jnp.concatenate([dwq, pad], axis=0)
        gk_ref[...] += jnp.concatenate([dwk, pad], axis=0)

    ps = lambda o: pl.BlockSpec((None, RT, NAW), lambda b, j: (b, j, COL_NA // NAW + o))
    one = pl.BlockSpec((1, NAW), lambda b, j: (0, 0))
    ts = pl.BlockSpec((None, RT, NAW), lambda b, j: (b, j, 0))
    acc = pl.BlockSpec((8, NAW), lambda b, j: (0, 0))
    return pl.pallas_call(
        body, name="na_prep_bwd", grid=(NB, T // RT),
        out_shape=[jax.ShapeDtypeStruct((NB, T, 1536), F32), jax.ShapeDtypeStruct((8, NAW), F32),
                   jax.ShapeDtypeStruct((8, NAW), F32)],
        in_specs=[ps(0), ps(1), one, one, pl.BlockSpec((NAW, NAW), lambda b, j: (0, 0)),
                  pl.BlockSpec((None, RT, NAW), lambda b, j: (b, jnp.maximum(j - 1, 0), 0)), ts, ts],
        out_specs=[pl.BlockSpec((None, RT, 1536), lambda b, j: (b, j, 0)), acc, acc],
        compiler_params=_cp(vmem=VMEM_BIG),
    )(p, p, qw, kw, bd, dq, dk, dv)


def _na_row0(r):
    return jnp.clip(r - WIN_ROWS // 2, 0, ROWS - WIN_ROWS)


def _na_cls(r):
    return _na_row0(r) - r + WIN_ROWS - 1


NHS = 4
NLW = NHS * 64
NSR = NHS * GRID_W


def _na_stack(x):
    lane = lax.broadcasted_iota(jnp.int32, (1, NLW), 1)
    return jnp.concatenate([jnp.where((lane >> 6) == h, x, jnp.zeros_like(x)) for h in range(NHS)], axis=0)


def _na_unstack(y):
    lane = lax.broadcasted_iota(jnp.int32, (1, NLW), 1)
    out = jnp.zeros((GRID_W, NLW), y.dtype)
    for h in range(NHS):
        out = jnp.where((lane >> 6) == h, y[GRID_W * h:GRID_W * h + GRID_W], out)
    return out


def _na_probs(q4, kw, kc, bias):
    s_w = lax.dot_general(q4, kw, _NT, preferred_element_type=F32) + bias
    s_c = lax.dot_general(q4, kc, _NT, preferred_element_type=F32)
    mx = jnp.maximum(jnp.max(s_w, axis=1, keepdims=True), jnp.max(s_c, axis=1, keepdims=True))
    p_w = jnp.exp(s_w - mx)
    p_c = jnp.exp(s_c - mx)
    inv = 1.0 / (jnp.sum(p_w, axis=1, keepdims=True) + jnp.sum(p_c, axis=1, keepdims=True))
    return p_w, p_c, inv


def _na_specs():
    qs = pl.BlockSpec((None, GRID_W, NLW), lambda g, b, r: (b, CT // GRID_W + r, g))
    kv = pl.BlockSpec((None, T, NLW), lambda g, b, r: (b, 0, g))
    bs = pl.BlockSpec((None, None, NSR, NWIN), lambda g, b, r: (_na_cls(r), g, 0, 0))
    lat = pl.BlockSpec((None, GRID_W, NLW), lambda g, b, r: (b, r, g))
    return qs, kv, bs, lat


def _na_fwd(qn, kn, vb, bias):
    def body(q_ref, k_ref, v_ref, b_ref, o_ref):
        r = pl.program_id(2)
        start = pl.multiple_of(CT + _na_row0(r) * GRID_W, GRID_W)
        kw, vw = k_ref[pl.ds(start, NWIN), :], v_ref[pl.ds(start, NWIN), :]
        kc, vc = k_ref[0:CT, :], v_ref[0:CT, :]
        p_w, p_c, inv = _na_probs(_na_stack(q_ref[...]), kw, kc, b_ref[...])
        o4 = (jnp.dot(p_w.astype(BF16), vw, preferred_element_type=F32)
              + jnp.dot(p_c.astype(BF16), vc, preferred_element_type=F32)) * inv
        o_ref[...] = _na_unstack(o4).astype(BF16)

    qs, kv, bs, lat = _na_specs()
    return pl.pallas_call(
        body, name="na_fwd", grid=(NAH // NHS, NB, ROWS), out_shape=jax.ShapeDtypeStruct((NB, SEQ, NAW), BF16),
        in_specs=[qs, kv, kv, bs], out_specs=lat,
        compiler_params=_cp(vmem=VMEM_BIG),
    )(qn, kn, vb, bias)


def _na_bwd(qn, kn, vb, bias, do):
    def body(q_ref, k_ref, v_ref, b_ref, do_ref, dq_ref, dk_ref, dv_ref, db_ref):
        b, r = pl.program_id(1), pl.program_id(2)

        @pl.when((b == 0) & (r == 0))
        def _():
            db_ref[...] = jnp.zeros_like(db_ref)

        @pl.when(r == 0)
        def _():
            dk_ref[...] = jnp.zeros_like(dk_ref)
            dv_ref[...] = jnp.zeros_like(dv_ref)

        start = pl.multiple_of(CT + _na_row0(r) * GRID_W, GRID_W)
        cls = _na_cls(r)
        kw, vw = k_ref[pl.ds(start, NWIN), :], v_ref[pl.ds(start, NWIN), :]
        kc, vc = k_ref[0:CT, :], v_ref[0:CT, :]
        q4 = _na_stack(q_ref[...])
        p_w, p_c, inv = _na_probs(q4, kw, kc, b_ref[...])
        p_w = p_w * inv
        p_c = p_c * inv
        do4 = _na_stack(do_ref[...].astype(BF16))
        dp_w = lax.dot_general(do4, vw, _NT, preferred_element_type=F32)
        dp_c = lax.dot_general(do4, vc, _NT, preferred_element_type=F32)
        delta = jnp.sum(p_w * dp_w, axis=1, keepdims=True) + jnp.sum(p_c * dp_c, axis=1, keepdims=True)
        ds_w = p_w * (dp_w - delta)
        ds_c = p_c * (dp_c - delta)
        db_ref[cls] += ds_w
        ds_wb, ds_cb = ds_w.astype(BF16), ds_c.astype(BF16)
        dq_ref[...] = _na_unstack(jnp.dot(ds_wb, kw, preferred_element_type=F32)
                                  + jnp.dot(ds_cb, kc, preferred_element_type=F32))
        dk_ref[pl.ds(start, NWIN), :] += lax.dot_general(ds_wb, q4, _TN, preferred_element_type=F32)
        dv_ref[pl.ds(start, NWIN), :] += lax.dot_general(p_w.astype(BF16), do4, _TN, preferred_element_type=F32)
        dk_ref[0:CT, :] += lax.dot_general(ds_cb, q4, _TN, preferred_element_type=F32)
        dv_ref[0:CT, :] += lax.dot_general(p_c.astype(BF16), do4, _TN, preferred_element_type=F32)

    qs, kv, bs, lat = _na_specs()
    return pl.pallas_call(
        body, name="na_bwd", grid=(NAH // NHS, NB, ROWS),
        out_shape=[jax.ShapeDtypeStruct((NB, SEQ, NAW), F32)] + [jax.ShapeDtypeStruct((NB, T, NAW), F32)] * 2
        + [jax.ShapeDtypeStruct((WIN_ROWS, NAH // NHS, NSR, NWIN), F32)],
        in_specs=[qs, kv, kv, bs, lat],
        out_specs=[lat, kv, kv, pl.BlockSpec((WIN_ROWS, None, NSR, NWIN), lambda g, b, r: (0, g, 0, 0))],
        compiler_params=_cp(vmem=VMEM_BIG),
    )(qn, kn, vb, bias, do)


def _na_tables():
    cols = np.arange(GRID_W)
    win_start = np.clip(cols - WIN_COLS // 2, 0, GRID_W - WIN_COLS)
    kc = cols[None, :]
    valid = (kc >= win_start[:, None]) & (kc < win_start[:, None] + WIN_COLS)
    rel = np.clip(kc - cols[:, None] + WIN_COLS - 1, 0, 2 * WIN_COLS - 2)
    return valid, rel


def _bias_table(rpb):
    valid, rel = _na_tables()
    t15 = jnp.where(valid[None, None], rpb[:, :, rel], NEG)
    tab = jnp.stack([t15[:, c:c + WIN_ROWS] for c in range(WIN_ROWS)], axis=0)
    return tab.transpose(0, 1, 3, 2, 4).reshape(WIN_ROWS, NAH // NHS, NSR, NWIN)


def _rpb_fold(db):
    def body1(d_ref, o_ref):
        for dr in range(2 * WIN_ROWS - 1):
            acc = None
            for c in range(WIN_ROWS):
                i = dr - c
                if 0 <= i < WIN_ROWS:
                    blk = d_ref[c, :, i * GRID_W:(i + 1) * GRID_W]
                    acc = blk if acc is None else acc + blk
            o_ref[dr] = acc

    t15 = pl.pallas_call(
        body1, name="rpb_fold_rows", grid=(NAH,),
        out_shape=jax.ShapeDtypeStruct((NAH, 2 * WIN_ROWS - 1, GRID_W, GRID_W), F32),
        in_specs=[pl.BlockSpec((WIN_ROWS, None, GRID_W, NWIN), lambda h: (0, h, 0, 0))],
        out_specs=pl.BlockSpec((None, 2 * WIN_ROWS - 1, GRID_W, GRID_W), lambda h: (h, 0, 0, 0)),
        compiler_params=_cp(vmem=VMEM_BIG),
    )(db.reshape(WIN_ROWS, NAH, GRID_W, NWIN))
    valid, rel = _na_tables()
    onehot = np.zeros((GRID_W * GRID_W, 128), np.float32)
    flat_rel, flat_valid = rel.reshape(-1), valid.reshape(-1)
    onehot[np.arange(GRID_W * GRID_W)[flat_valid], flat_rel[flat_valid]] = 1.0

    def body2(a_ref, oh_ref, o_ref):
        o_ref[...] = _hmm(a_ref[...], oh_ref[...])

    vm = pl.BlockSpec(memory_space=pltpu.VMEM)
    out = pl.pallas_call(
        body2, name="rpb_fold_cols", out_shape=jax.ShapeDtypeStruct((NAH * 15, 128), F32),
        in_specs=[vm, vm], out_specs=vm, compiler_params=_cp(vmem=VMEM_BIG),
    )(t15.reshape(NAH * 15, GRID_W * GRID_W), jnp.asarray(onehot))
    return out[:, :31].reshape(NAH, 15, 31)


def _dn_out(o, z, w):
    outs = []
    for h in range(DNH):
        sl = slice(128 * h, 128 * h + 128)
        oh = o[:, sl]
        r = lax.rsqrt(jnp.mean(oh * oh, axis=-1, keepdims=True) + EPS)
        outs.append(oh * r * w[:, sl] * _silu(z[:, sl]))
    return jnp.concatenate(outs, axis=1)


def _outproj_fwd(of, ob, p, na, onw, x, g1, w_out):
    def body(of_ref, ob_ref, z_ref, na_ref, w_ref, x_ref, g_ref, wo_ref, x1_ref, ai_ref, ao_ref):
        dn = _dn_out(of_ref[...] + ob_ref[...], z_ref[...], w_ref[...])
        ai = jnp.concatenate([dn.astype(BF16), na_ref[...]], axis=1)
        ai_ref[...] = ai
        ao = jnp.dot(ai, wo_ref[...], preferred_element_type=F32)
        ao_ref[...] = ao
        x1_ref[...] = x_ref[...] + g_ref[...] * ao

    lat = lambda w: pl.BlockSpec((None, RT, w), lambda b, j: (b, j, 0))
    return pl.pallas_call(
        body, name="outproj_fwd", grid=(NB, SEQ // RT),
        out_shape=[jax.ShapeDtypeStruct((NB, SEQ, D), F32), jax.ShapeDtypeStruct((NB, SEQ, D), BF16),
                   jax.ShapeDtypeStruct((NB, SEQ, D), F32)],
        in_specs=[pl.BlockSpec((None, RT, DNW), lambda b, j: (b, j + 1, 0)),
                  pl.BlockSpec((None, RT, DNW), lambda b, j: (b, j + 1, 0)),
                  pl.BlockSpec((None, RT, DNW), lambda b, j: (b, j + 1, COL_Z // DNW)),
                  lat(NAW), pl.BlockSpec((1, DNW), lambda b, j: (0, 0)), lat(D),
                  pl.BlockSpec((None, 1, D), lambda b, j: (b, 0, 0)), pl.BlockSpec((D, D), lambda b, j: (0, 0))],
        out_specs=[lat(D), lat(D), lat(D)],
        compiler_params=_cp(vmem=VMEM_BIG),
    )(of, ob, p, na, onw, x, g1, w_out)


def _outproj_bwd(dx1, ao, g1, w_out, of, ob, p, onw):
    def body(dx_ref, ao_ref, g_ref, wo_ref, of_ref, ob_ref, z_ref, w_ref, dao_ref, do_ref, dz_ref, dna_ref, dg_ref, dw_ref):
        b, j = pl.program_id(0), pl.program_id(1)

        @pl.when((b == 0) & (j == 0))
        def _():
            dw_ref[...] = jnp.zeros_like(dw_ref)

        @pl.when(j == 0)
        def _():
            do_ref[...] = jnp.zeros_like(do_ref)
            dz_ref[...] = jnp.zeros_like(dz_ref)

        @pl.when(j == 1)
        def _():
            dg_ref[...] = jnp.zeros_like(dg_ref)

        @pl.when(j >= 1)
        def _():
            dx = dx_ref[...]
            dg_ref[...] += jnp.sum(dx * ao_ref[...], axis=0, keepdims=True)
            dao = (g_ref[...] * dx).astype(BF16)
            dao_ref[...] = dao
            dai = lax.dot_general(dao, wo_ref[...], _NT, preferred_element_type=F32)
            dna_ref[...] = dai[:, DNW:]
            _, vjp = jax.vjp(_dn_out, of_ref[...] + ob_ref[...], z_ref[...], w_ref[...])
            do, dz, dw = vjp(dai[:, :DNW])
            do_ref[...] = do
            dz_ref[...] = dz
            dw_ref[...] += jnp.concatenate([dw, jnp.zeros((7, DNW), F32)], axis=0)

    lat = lambda w: pl.BlockSpec((None, RT, w), lambda b, j: (b, jnp.maximum(j - 1, 0), 0))
    tok = lambda w: pl.BlockSpec((None, RT, w), lambda b, j: (b, j, 0))
    return pl.pallas_call(
        body, name="outproj_bwd", grid=(NB, T // RT),
        out_shape=[jax.ShapeDtypeStruct((NB, SEQ, D), BF16), jax.ShapeDtypeStruct((NB, T, DNW), F32),
                   jax.ShapeDtypeStruct((NB, T, DNW), F32), jax.ShapeDtypeStruct((NB, SEQ, NAW), F32),
                   jax.ShapeDtypeStruct((NB, 1, D), F32), jax.ShapeDtypeStruct((8, DNW), F32)],
        in_specs=[lat(D), lat(D), pl.BlockSpec((None, 1, D), lambda b, j: (b, 0, 0)),
                  pl.BlockSpec((D, D), lambda b, j: (0, 0)),
                  tok(DNW), tok(DNW),
                  pl.BlockSpec((None, RT, DNW), lambda b, j: (b, j, COL_Z // DNW)),
                  pl.BlockSpec((1, DNW), lambda b, j: (0, 0))],
        out_specs=[lat(D), tok(DNW), tok(DNW), lat(NAW), pl.BlockSpec((None, 1, D), lambda b, j: (b, 0, 0)),
                   pl.BlockSpec((8, DNW), lambda b, j: (0, 0))],
        compiler_params=_cp(vmem=VMEM_BIG),
    )(dx1, ao, g1, w_out, of, ob, p, onw)


FT = 256
NFT = NTOK // FT
TPE = SEQ // FT
NFF = DFF // FFT


def _ffn_specs():
    row = lambda w: pl.BlockSpec((FT, w), lambda i, f: (i, 0))
    ex = pl.BlockSpec((None, 1, D), lambda i, f: (i // TPE, 0, 0))
    one = pl.BlockSpec((1, D), lambda i, f: (0, 0))
    wg = pl.BlockSpec((None, D, FFT), lambda i, f: (f, 0, 0))
    wu = pl.BlockSpec((None, D, FFT), lambda i, f: (NFF + f, 0, 0))
    wo = pl.BlockSpec((FFT, D), lambda i, f: (f, 0))
    ff = pl.BlockSpec((FT, FFT), lambda i, f: (i, f))
    return row, ex, one, wg, wu, wo, ff


def _ffn_fwd(x1, sh2, sc2, g2, n2w, w_in4, w_out, target):
    def body(x_ref, sh_ref, sc_ref, g_ref, w2_ref, wg_ref, wu_ref, wo_ref, t_ref,
             h2_ref, gt_ref, up_ref, dy_ref, dg_ref, loss_ref, acc_ref):
        i, f = pl.program_id(0), pl.program_id(1)

        @pl.when(f == 0)
        def _():
            h2_ref[...] = _norm_mod(x_ref[...], w2_ref[...], sh_ref[...], sc_ref[...]).astype(BF16)
            acc_ref[...] = jnp.zeros_like(acc_ref)

        h2 = h2_ref[...]
        gate = jnp.dot(h2, wg_ref[...], preferred_element_type=F32)
        up = jnp.dot(h2, wu_ref[...], preferred_element_type=F32)
        gt_ref[...] = gate.astype(BF16)
        up_ref[...] = up.astype(BF16)
        acc_ref[...] += jnp.dot((_silu(gate) * up).astype(BF16), wo_ref[...], preferred_element_type=F32)

        @pl.when((i == 0) & (f == 0))
        def _():
            loss_ref[...] = jnp.zeros_like(loss_ref)

        @pl.when(f == NFF - 1)
        def _():
            ffn = acc_ref[...]
            err = x_ref[...] + g_ref[...] * ffn - t_ref[...]
            dy = err * (1.0 / D)
            dy_ref[...] = dy
            part = jnp.sum(dy * ffn, axis=0, keepdims=True)
            loss_ref[...] += jnp.sum(jnp.sum(err * err, axis=1, keepdims=True), axis=0, keepdims=True) * (0.5 / D)

            @pl.when(i % TPE == 0)
            def _():
                dg_ref[...] = part

            @pl.when(i % TPE != 0)
            def _():
                dg_ref[...] += part

    row, ex, one, wg, wu, wo, ff = _ffn_specs()
    return pl.pallas_call(
        body, name="ffn_fwd", grid=(NFT, NFF),
        out_shape=[jax.ShapeDtypeStruct((NTOK, D), BF16), jax.ShapeDtypeStruct((NTOK, DFF), BF16),
                   jax.ShapeDtypeStruct((NTOK, DFF), BF16), jax.ShapeDtypeStruct((NTOK, D), F32),
                   jax.ShapeDtypeStruct((NB, 1, D), F32), jax.ShapeDtypeStruct((8, 128), F32)],
        in_specs=[row(D), ex, ex, ex, one, wg, wu, wo, row(D)],
        out_specs=[row(D), ff, ff, row(D), ex, pl.BlockSpec((8, 128), lambda i, f: (0, 0))],
        scratch_shapes=[pltpu.VMEM((FT, D), F32)],
        compiler_params=_cp(vmem=VMEM_BIG),
    )(x1, sh2, sc2, g2, n2w, w_in4, w_in4, w_out, target)


def _ffn_bwd(dy, gate, up, x1, sh2, sc2, g2, n2w, w_in4, w_out):
    def body(dy_ref, gt_ref, up_ref, x_ref, sh_ref, sc_ref, g_ref, w2_ref, wg_ref, wu_ref, wo_ref,
             dff_ref, act_ref, dgu_ref, dx_ref, dsh_ref, dsc_ref, dw_ref, acc_ref):
        i, f = pl.program_id(0), pl.program_id(1)

        @pl.when(f == 0)
        def _():
            dff_ref[...] = (g_ref[...] * dy_ref[...]).astype(BF16)
            acc_ref[...] = jnp.zeros_like(acc_ref)

        dact = lax.dot_general(dff_ref[...], wo_ref[...], _NT, preferred_element_type=F32)
        gate, up = gt_ref[...].astype(F32), up_ref[...].astype(F32)
        sg = _sigmoid(gate)
        sl = gate * sg
        act_ref[...] = (sl * up).astype(BF16)
        dgate = (dact * up * (sg * (1.0 + gate * (1.0 - sg)))).astype(BF16)
        dup = (dact * sl).astype(BF16)
        dgu_ref[0] = dgate
        dgu_ref[1] = dup
        acc_ref[...] += (lax.dot_general(dgate, wg_ref[...], _NT, preferred_element_type=F32)
                         + lax.dot_general(dup, wu_ref[...], _NT, preferred_element_type=F32))

        @pl.when((i == 0) & (f == 0))
        def _():
            dw_ref[...] = jnp.zeros_like(dw_ref)

        @pl.when(f == NFF - 1)
        def _():
            _, vjp = jax.vjp(_norm_mod, x_ref[...], w2_ref[...], sh_ref[...], sc_ref[...])
            dx, dw, dsh, dsc = vjp(acc_ref[...])
            dx_ref[...] = dy_ref[...] + dx
            dw_ref[...] += dw

            @pl.when(i % TPE == 0)
            def _():
                dsh_ref[...] = dsh
                dsc_ref[...] = dsc

            @pl.when(i % TPE != 0)
            def _():
                dsh_ref[...] += dsh
                dsc_ref[...] += dsc

    row, ex, one, wg, wu, wo, ff = _ffn_specs()
    return pl.pallas_call(
        body, name="ffn_bwd", grid=(NFT, NFF),
        out_shape=[jax.ShapeDtypeStruct((NTOK, D), BF16), jax.ShapeDtypeStruct((NTOK, DFF), BF16),
                   jax.ShapeDtypeStruct((2, NTOK, DFF), BF16), jax.ShapeDtypeStruct((NTOK, D), F32),
                   jax.ShapeDtypeStruct((NB, 1, D), F32), jax.ShapeDtypeStruct((NB, 1, D), F32),
                   jax.ShapeDtypeStruct((1, D), F32)],
        in_specs=[row(D), ff, ff, row(D), ex, ex, ex, one, wg, wu, wo],
        out_specs=[row(D), ff, pl.BlockSpec((2, FT, FFT), lambda i, f: (0, i, f)), row(D), ex, ex, one],
        scratch_shapes=[pltpu.VMEM((FT, D), F32)],
        compiler_params=_cp(vmem=VMEM_BIG),
    )(dy, gate, up, x1, sh2, sc2, g2, n2w, w_in4, w_in4, w_out)


def _local_step(x, ctx, target, mod_x, mod_c, norm1_w, norm2_w, w_in_p, conv_w, a_log, dt_bias, out_norm_w,
                q_norm_w, k_norm_w, rpb, w_out_b, w_ffn_in4, w_ffn_out_b, scatter_early=False, gather_late=None):
    sh1, sc1, g1, sh2, sc2, g2 = [mod_x[:, i * D:(i + 1) * D].reshape(NB, 1, D) for i in range(6)]
    csh1, csc1 = mod_c[0:D].reshape(1, 1, D), mod_c[D:2 * D].reshape(1, 1, D)
    cw = jnp.concatenate([conv_w, jnp.zeros((3, 1536), F32)], axis=0)
    prm = jnp.zeros((8, 128), F32).at[0, :8].set(a_log.reshape(8)).at[1, :8].set(dt_bias.reshape(8))
    onw = jnp.tile(out_norm_w.reshape(1, 128), (1, DNH))
    qw = jnp.tile(q_norm_w.reshape(1, 64), (1, NAH))
    kw = jnp.tile(k_norm_w.reshape(1, 64), (1, NAH))
    seg = np.arange(NAW) // 64
    bd = jnp.asarray((seg[:, None] == seg[None, :]).astype(np.float32) / 64.0, dtype=BF16)
    bias = _bias_table(rpb)

    p, hm = _inproj_fwd(x, ctx, sh1, sc1, csh1, csc1, norm1_w, w_in_p)
    dq_, dk_, dv_ = _dn_prep_fwd(p, cw)
    gates = _gates_fwd(p, prm)
    res = _scan_fwd(dq_, dk_, dv_, gates, carry=gather_late or ())
    o_f, st_f, o_b, st_b = res[:4]
    if gather_late:
        w_out_b = res[4].reshape(D, D)
        w_ffn_in4 = res[5].reshape(4, D, FFT)
        w_ffn_out_b = res[6].reshape(DFF, D)
    qn, kn, vb = _na_prep_fwd(p, qw, kw, bd)
    na = _na_fwd(qn, kn, vb, bias)
    x1, ai, ao = _outproj_fwd(o_f, o_b, p, na, onw, x, g1, w_out_b)
    x1f = x1.reshape(NTOK, D)
    h2, gate, up, dy, dg2, loss8 = _ffn_fwd(x1f, sh2, sc2, g2, norm2_w, w_ffn_in4, w_ffn_out_b,
                                             target.reshape(NTOK, D))
    dff, act, dgu, dx1, dsh2, dsc2, dn2w = _ffn_bwd(dy, gate, up, x1f, sh2, sc2, g2, norm2_w, w_ffn_in4, w_ffn_out_b)
    g_w_ffn_out = _matmul_tn(act, dff, FFT, D, 512, "wgrad_ffn_out")
    g_w_ffn_in4 = _matmul_tn(
        h2, dgu, 512, FFT, 512, "wgrad_ffn_in", nn=2 * NFF,
        b_block=(None, 512, FFT), b_map=lambda i, j, k: (j // NFF, k, j % NFF),
        out_shape=(4, D, FFT), out_block=(None, 512, FFT), out_map=lambda i, j, k: (j, i, 0))
    dao, do, dz, dna, dg1, donw = _outproj_bwd(dx1.reshape(NB, SEQ, D), ao, g1, w_out_b, o_f, o_b, p, onw)
    g_w_out = _matmul_tn(ai.reshape(NTOK, D), dao.reshape(NTOK, D), 512, D, 512, "wgrad_out")
    dqn, dkn, dvn, dbias = _na_bwd(qn, kn, vb, bias, dna)
    dp_na, dqw, dkw = _na_prep_bwd(p, qw, kw, bd, dqn, dkn, dvn)
    early = [g_w_out.reshape(8, D // 8, D), g_w_ffn_in4.reshape(8, D // 2, FFT), g_w_ffn_out.reshape(8, DFF // 8, D)]
    res = _scan_bwd(dq_, dk_, dv_, gates, st_f, st_b, do, carry=early if scatter_early else ())
    dqf, dkf, dvf, dgf, dqb, dkb, dvb, dgb = res[:8]
    if scatter_early:
        g_w_out, g_w_ffn_in4, g_w_ffn_out = res[8:]
    dpq, dpk, dpv, gcq, gck, gcv = _dn_prep_bwd(p, cw, (dqf, dqb), (dkf, dkb), (dvf, dvb))
    dgt, dprm = _gates_bwd(p, prm, dgf, dgb)
    dpb, grad_x, dsh1, dsc1, dcsh1, dcsc1, dn1w = _inproj_bwd(
        (dpq, dpk, dpv), dz, dp_na, dgt, w_in_p, x, ctx, sh1, sc1, csh1, csc1, norm1_w, dx1.reshape(NB, SEQ, D))
    g_w_in_p = _matmul_tn(hm.reshape(NB * T, D), dpb.reshape(NB * T, PC), 256, PC, 512, "wgrad_in")
    g_rpb = _rpb_fold(dbias)

    dmod_x = jnp.concatenate([dsh1, dsc1, dg1, dsh2, dsc2, dg2], axis=2).reshape(NB, 6 * D)
    dmod_c = jnp.concatenate([dcsh1.reshape(D), dcsc1.reshape(D), jnp.zeros((4 * D,), F32)])
    g_conv = jnp.concatenate([gcq[:5], gck[:5], gcv[:5]], axis=1)
    small = dict(
        norm1_w=dn1w, norm2_w=dn2w, a_log=dprm[0, :8], dt_bias=dprm[1, :8],
        out_norm_w=donw[0].reshape(DNH, 128).sum(0), q_norm_w=dqw[0].reshape(NAH, 64).sum(0),
        k_norm_w=dkw[0].reshape(NAH, 64).sum(0), rpb=g_rpb, conv_w=g_conv)
    return loss8[0, 0], grad_x, dmod_x, dmod_c, small, g_w_in_p, g_w_out, g_w_ffn_in4, g_w_ffn_out


def _perm_w_in(w):
    return jnp.concatenate([w[:, :2048], w[:, 2064:3600], w[:, 2048:2064],
                            jnp.zeros((w.shape[0], PC - IN_COLS), w.dtype)], axis=1)


def _unperm_w_in(g):
    return jnp.concatenate([g[:, :2048], g[:, COL_G:COL_G + 16], g[:, 2048:COL_G]], axis=1)


def _pack_small(c_ctx, n1, n2, b_ada, a_log, dt_bias, onw, qnw, knw, rpb, conv_s):
    misc = jnp.zeros((D,), F32)
    misc = misc.at[0:8].set(a_log.reshape(8)).at[8:16].set(dt_bias.reshape(8))
    misc = misc.at[128:256].set(onw.reshape(128)).at[256:320].set(qnw.reshape(64)).at[320:384].set(knw.reshape(64))
    rp = jnp.concatenate([rpb.reshape(-1), jnp.zeros((4 * D - 3720,), F32)])
    cv = jnp.concatenate([conv_s.reshape(-1), jnp.zeros((2 * D - 1920,), F32)])
    return jnp.concatenate([c_ctx.reshape(-1), n1.reshape(-1), n2.reshape(-1), b_ada.reshape(-1), misc, rp, cv]).reshape(16, D)


def _unpack_small(a):
    f = a.reshape(-1)
    misc = f[9 * D:10 * D]
    return dict(
        c_ctx=f[0:D], norm1_w=f[D:2 * D].reshape(1, D), norm2_w=f[2 * D:3 * D].reshape(1, D),
        b_ada=f[3 * D:9 * D].reshape(1, 6 * D), dn_A_log=misc[0:8].reshape(1, 2, 4), dn_dt_bias=misc[8:16].reshape(1, 2, 4),
        dn_out_norm_w=misc[128:256].reshape(1, 128), na_q_norm_w=misc[256:320].reshape(1, 64),
        na_k_norm_w=misc[320:384].reshape(1, 64), na_rpb=f[10 * D:10 * D + 3720].reshape(1, 8, 15, 31),
        dn_conv_w=f[14 * D:14 * D + 1920].reshape(1, 5, 384))


_WEIGHTS = ['c_ctx', 'norm1_w', 'norm2_w', 'w_ada', 'b_ada', 'w_in', 'dn_conv_w', 'dn_A_log', 'dn_dt_bias',
            'dn_out_norm_w', 'na_q_norm_w', 'na_k_norm_w', 'na_rpb', 'w_out', 'w_ffn_in', 'w_ffn_out']


def kernel(x, c, ctx, c_ctx, norm1_w, norm2_w, w_ada, b_ada, w_in, dn_conv_w, dn_A_log, dn_dt_bias, dn_out_norm_w, na_q_norm_w, na_k_norm_w, na_rpb, w_out, w_ffn_in, w_ffn_out, loss_target, m_c_ctx, m_norm1_w, m_norm2_w, m_w_ada, m_b_ada, m_w_in, m_dn_conv_w, m_dn_A_log, m_dn_dt_bias, m_dn_out_norm_w, m_na_q_norm_w, m_na_k_norm_w, m_na_rpb, m_w_out, m_w_ffn_in, m_w_ffn_out, v_c_ctx, v_norm1_w, v_norm2_w, v_w_ada, v_b_ada, v_w_in, v_dn_conv_w, v_dn_A_log, v_dn_dt_bias, v_dn_out_norm_w, v_na_q_norm_w, v_na_k_norm_w, v_na_rpb, v_w_out, v_w_ffn_in, v_w_ffn_out):
    ix, iy, ic = lax.axis_index("x"), lax.axis_index("y"), lax.axis_index("c")
    chip = 2 * ix + iy
    me = 4 * ix + 2 * iy + ic

    crow = jnp.concatenate([c, c_ctx.reshape(1, D), jnp.zeros((5, D), F32)], axis=0)
    cvrow = jnp.concatenate([dn_conv_w[0].reshape(-1), jnp.zeros((2 * D - 1920,), F32)]).reshape(2, D)
    g0 = _allgather8(jnp.concatenate([crow, cvrow, jnp.zeros((6, D), F32)], axis=0), "gather_cond")
    cin = g0[:, :8].reshape(64, D)
    conv_full = jnp.concatenate([g0[2 * s, 8:10].reshape(-1)[:1920].reshape(5, 384) for s in range(4)], axis=1)

    w_ada_s = w_ada[0]
    b_s = lax.dynamic_slice(b_ada, (0, chip * 1536), (1, 1536))
    mod_s = _ada_fwd(cin, w_ada_s, b_s)
    mods = _allgather8(mod_s, "gather_mod")
    mod_all = jnp.concatenate([mods[0], mods[2], mods[4], mods[6]], axis=1)
    mod_mine = lax.dynamic_slice(mod_all, (8 * me, 0), (8, 6 * D))
    mod_x, mod_c = mod_mine[0:NB], mod_mine[2]

    def half(w2d):
        r = w2d.shape[0] // 2
        return lax.dynamic_slice(w2d, (ic * r, 0), (r, w2d.shape[1]))

    (gw_in,) = _gather_weights([_cast_bf16(half(w_in[0]), "cast_w_in")])
    late = [_cast_bf16(half(w_out[0]), "cast_w_out"), _cast_bf16(half(w_ffn_in[0]), "cast_w_ffn_in"),
            _cast_bf16(half(w_ffn_out[0]), "cast_w_ffn_out")]
    w_in_full = gw_in.reshape(4, D, 900).transpose(1, 0, 2).reshape(D, IN_COLS)
    w_in_p = _perm_w_in(w_in_full)

    (loss_part, grad_x, dmod_x, dmod_c, small, g_w_in_p, r_w_out, r_w_ffn_in, r_w_ffn_out) = _local_step(
        x, ctx, loss_target, mod_x, mod_c, norm1_w, norm2_w, w_in_p, conv_full, dn_A_log[0], dn_dt_bias[0],
        dn_out_norm_w[0], na_q_norm_w[0], na_k_norm_w[0], na_rpb[0], None, None, None,
        scatter_early=True, gather_late=late)
    loss = lax.psum(loss_part, ("x", "y", "c"))

    dm = jnp.concatenate([dmod_x, dmod_c.reshape(1, 6 * D), jnp.zeros((5, 6 * D), F32)], axis=0).reshape(48, D)
    misc = jnp.zeros((D,), F32)
    misc = misc.at[0:8].set(small["a_log"]).at[8:16].set(small["dt_bias"]).at[128:256].set(small["out_norm_w"])
    misc = misc.at[256:320].set(small["q_norm_w"]).at[320:384].set(small["k_norm_w"])
    sm = jnp.concatenate([
        small["norm1_w"].reshape(-1), small["norm2_w"].reshape(-1), misc,
        small["rpb"].reshape(-1), jnp.zeros((4 * D - 3720,), F32),
        small["conv_w"].reshape(-1), jnp.zeros((8 * D - 7680,), F32), jnp.zeros((D,), F32)]).reshape(16, D)
    gsl, gtot = _allgather8(jnp.concatenate([dm, sm], axis=0), "gather_small", with_sum=True)
    gmod = gsl[:, :48].reshape(64, 6 * D)
    st = gtot[48:64].reshape(-1)
    g_n1, g_n2, g_misc = st[0:D].reshape(1, D), st[D:2 * D].reshape(1, D), st[2 * D:3 * D]
    g_rpb = st[3 * D:3 * D + 3720].reshape(1, 8, 15, 31)
    g_conv_full = st[7 * D:7 * D + 7680].reshape(5, 1536)
    g_conv_s = lax.dynamic_slice(g_conv_full, (0, chip * 384), (5, 384)).reshape(1, 5, 384)

    gmod_s = lax.dynamic_slice(gmod, (0, chip * 1536), (64, 1536))
    g_w_ada_s, pctx = _ada_bwd(cin, gmod_s, w_ada_s)
    parts = _allgather8(pctx, "gather_cctx")
    g_b_ada, g_c_ctx = _ada_small(gmod, parts, c_ctx.reshape(1, D))

    g_w_in_full = _unperm_w_in(g_w_in_p)
    g_w_in8 = g_w_in_full.reshape(D, 4, 900).transpose(1, 0, 2).reshape(8, D // 2, 900)
    names = ["w_in", "w_out", "w_ffn_in", "w_ffn_out"]
    rs = list(_scatter_grads([g_w_in8])) + [r_w_out, r_w_ffn_in, r_w_ffn_out]
    full = _sibling_exchange([_sum_slots(r, "sum_" + nm) for r, nm in zip(rs, names)])
    g_w_in_s = full[0].reshape(1, D, 900)
    g_w_out_s = full[1].reshape(1, D // 4, D)
    g_w_ffn_in_s = full[2].reshape(1, D, FFT)
    g_w_ffn_out_s = full[3].reshape(1, DFF // 4, D)

    grads = dict(
        c_ctx=g_c_ctx.reshape(D), norm1_w=g_n1, norm2_w=g_n2, w_ada=g_w_ada_s.reshape(1, D, 1536), b_ada=g_b_ada,
        w_in=g_w_in_s, dn_conv_w=g_conv_s, dn_A_log=g_misc[0:8].reshape(1, 2, 4), dn_dt_bias=g_misc[8:16].reshape(1, 2, 4),
        dn_out_norm_w=g_misc[128:256].reshape(1, 128), na_q_norm_w=g_misc[256:320].reshape(1, 64),
        na_k_norm_w=g_misc[320:384].reshape(1, 64), na_rpb=g_rpb, w_out=g_w_out_s, w_ffn_in=g_w_ffn_in_s,
        w_ffn_out=g_w_ffn_out_s)
    pk = lambda d_: _pack_small(d_["c_ctx"], d_["norm1_w"], d_["norm2_w"], d_["b_ada"], d_["dn_A_log"], d_["dn_dt_bias"],
                                d_["dn_out_norm_w"], d_["na_q_norm_w"], d_["na_k_norm_w"], d_["na_rpb"], d_["dn_conv_w"])
    wts = dict(c_ctx=c_ctx, norm1_w=norm1_w, norm2_w=norm2_w, b_ada=b_ada, dn_A_log=dn_A_log, dn_dt_bias=dn_dt_bias,
               dn_out_norm_w=dn_out_norm_w, na_q_norm_w=na_q_norm_w, na_k_norm_w=na_k_norm_w, na_rpb=na_rpb, dn_conv_w=dn_conv_w)
    ms = dict(c_ctx=m_c_ctx, norm1_w=m_norm1_w, norm2_w=m_norm2_w, b_ada=m_b_ada, dn_A_log=m_dn_A_log, dn_dt_bias=m_dn_dt_bias,
              dn_out_norm_w=m_dn_out_norm_w, na_q_norm_w=m_na_q_norm_w, na_k_norm_w=m_na_k_norm_w, na_rpb=m_na_rpb,
              dn_conv_w=m_dn_conv_w)
    vs = dict(c_ctx=v_c_ctx, norm1_w=v_norm1_w, norm2_w=v_norm2_w, b_ada=v_b_ada, dn_A_log=v_dn_A_log, dn_dt_bias=v_dn_dt_bias,
              dn_out_norm_w=v_dn_out_norm_w, na_q_norm_w=v_na_q_norm_w, na_k_norm_w=v_na_k_norm_w, na_rpb=v_na_rpb,
              dn_conv_w=v_dn_conv_w)
    sd, snm, snv, _ = _adamw(pk(wts), pk(grads), pk(ms), pk(vs), "adamw_small")
    delta, new_m, new_v = _unpack_small(sd), _unpack_small(snm), _unpack_small(snv)
    big = dict(w_ada=(w_ada, m_w_ada, v_w_ada), w_in=(w_in, m_w_in, v_w_in), w_out=(w_out, m_w_out, v_w_out),
               w_ffn_in=(w_ffn_in, m_w_ffn_in, v_w_ffn_in), w_ffn_out=(w_ffn_out, m_w_ffn_out, v_w_ffn_out))
    for nm, (w_, m_, v_) in big.items():
        shp = w_.shape
        r2 = lambda a: a.reshape(shp[1], shp[2])
        d_, nm_, nv_, g_ = _adamw(r2(w_), r2(grads[nm]), r2(m_), r2(v_), "adamw_" + nm)
        delta[nm], new_m[nm], new_v[nm], grads[nm] = d_.reshape(shp), nm_.reshape(shp), nv_.reshape(shp), g_.reshape(shp)

    return (loss, grad_x, *[grads[n] for n in _WEIGHTS], *[delta[n] for n in _WEIGHTS],
            *[new_m[n] for n in _WEIGHTS], *[new_v[n] for n in _WEIGHTS])
```

```python
import functools

import numpy as np
import jax
import jax.numpy as jnp
from jax import lax
from jax.experimental import pallas as pl
from jax.experimental.pallas import tpu as pltpu

F32 = jnp.float32
BF16 = jnp.bfloat16

D = 1024
NB = 2
SEQ = 2048
CT = 256
T = CT + SEQ
NTOK = NB * SEQ
CH = 64
NCH = T // CH
NCH_CTX = CT // CH
DNH, DNW = 4, 512
NAH, NAW = 8, 512
GRID_W = 64
ROWS = SEQ // GRID_W
WIN_ROWS = 8
WIN_COLS = 16
NWIN = WIN_ROWS * GRID_W
DFF = 2816
FFT = 1408
IN_COLS = 3600
PC = 3712
COL_Z, COL_NA, COL_G = 1536, 2048, 3584
EPS = 1e-6
NEG = -1e30
RT = 256
VMEM_BIG = 56 * 1024 * 1024
MESH = pl.DeviceIdType.MESH

ADAM_LR, ADAM_B1, ADAM_B2, ADAM_EPS, ADAM_WD, ADAM_STEP = 0.001, 0.9, 0.999, 1e-08, 0.01, 10


def _cp(vmem=None, sem=None):
    kw = {}
    if vmem is not None:
        kw["vmem_limit_bytes"] = vmem
    if sem is not None:
        kw["dimension_semantics"] = sem
    return pltpu.CompilerParams(**kw)


_NN = (((1,), (0,)), ((), ()))
_NT = (((1,), (1,)), ((), ()))
_TN = (((0,), (0,)), ((), ()))


def _dg(a, b, dims):
    return lax.dot_general(a.astype(BF16), b.astype(BF16), dims, preferred_element_type=F32)


@jax.custom_vjp
def _mm(a, b):
    return _dg(a, b, _NN)


_mm.defvjp(lambda a, b: (_dg(a, b, _NN), (a, b)),
           lambda r, g: (_dg(g, r[1], _NT), _dg(r[0], g, _TN)))


@jax.custom_vjp
def _mm_nt(a, b):
    return _dg(a, b, _NT)


_mm_nt.defvjp(lambda a, b: (_dg(a, b, _NT), (a, b)),
              lambda r, g: (_dg(g, r[1], _NN), _dg(g, r[0], _TN)))


@jax.custom_vjp
def _mm_tn(a, b):
    return _dg(a, b, _TN)


_mm_tn.defvjp(lambda a, b: (_dg(a, b, _TN), (a, b)),
              lambda r, g: (_dg(r[1], g, _NT), _dg(r[0], g, _NN)))


def _split_dot(a, b, dims):
    hi = b.astype(BF16)
    lo = (b - hi.astype(F32)).astype(BF16)
    ab = a.astype(BF16)
    return (lax.dot_general(ab, hi, dims, preferred_element_type=F32)
            + lax.dot_general(ab, lo, dims, preferred_element_type=F32))


@jax.custom_vjp
def _mask_mm(mask, b):
    return _split_dot(mask, b, _NN)


_mask_mm.defvjp(lambda mask, b: (_split_dot(mask, b, _NN), mask),
                lambda mask, g: (jnp.zeros_like(mask), _split_dot(mask, g, _TN)))


def _hmm(a, b):
    return jnp.dot(a, b, precision=lax.Precision.HIGHEST, preferred_element_type=F32)


def _sigmoid(x):
    return 1.0 / (1.0 + jnp.exp(-x))


def _silu(x):
    return x * _sigmoid(x)


def _norm_mod(x, w, sh, sc):
    r = lax.rsqrt(jnp.mean(x * x, axis=-1, keepdims=True) + EPS)
    return (x * r * w) * (1.0 + sc) + sh


def _my_pos():
    return lax.axis_index("x"), lax.axis_index("y"), lax.axis_index("c")


def _flip(pos, k):
    x, y, c = pos
    return (1 - x if k & 4 else x, 1 - y if k & 2 else y, 1 - c if k & 1 else c)


def _slot(pos):
    return 4 * pos[0] + 2 * pos[1] + pos[2]


def _allgather8(v, name, with_sum=False):
    R, W = v.shape

    def body(v_ref, out_ref, *rest):
        if with_sum:
            tot_ref, send_sems, recv_sems, lsem = rest
        else:
            send_sems, recv_sems, lsem = rest
        me = _my_pos()
        mine = pltpu.make_async_copy(v_ref, out_ref.at[_slot(me)], lsem)
        mine.start()
        sends = []
        for k in range(1, 8):
            cp = pltpu.make_async_remote_copy(
                src_ref=v_ref, dst_ref=out_ref.at[_slot(me)], send_sem=send_sems.at[k - 1],
                recv_sem=recv_sems.at[k - 1], device_id=_flip(me, k), device_id_type=MESH)
            cp.start()
            sends.append(cp)
        for k in range(1, 8):
            peer = _flip(me, k)
            pltpu.make_async_remote_copy(
                src_ref=v_ref, dst_ref=out_ref.at[_slot(peer)], send_sem=send_sems.at[k - 1],
                recv_sem=recv_sems.at[k - 1], device_id=peer, device_id_type=MESH).wait_recv()
        for cp in sends:
            cp.wait_send()
        mine.wait()
        if with_sum:
            acc = out_ref[0]
            for s in range(1, 8):
                acc = acc + out_ref[s]
            tot_ref[...] = acc

    out_shape = [jax.ShapeDtypeStruct((8, R, W), F32)]
    out_specs = [pl.BlockSpec(memory_space=pltpu.VMEM)]
    if with_sum:
        out_shape.append(jax.ShapeDtypeStruct((R, W), F32))
        out_specs.append(pl.BlockSpec(memory_space=pltpu.VMEM))
    res = pl.pallas_call(
        body, name=name, out_shape=out_shape,
        in_specs=[pl.BlockSpec(memory_space=pltpu.VMEM)], out_specs=out_specs,
        scratch_shapes=[pltpu.SemaphoreType.DMA((7,)), pltpu.SemaphoreType.DMA((7,)), pltpu.SemaphoreType.DMA],
        compiler_params=_cp(vmem=VMEM_BIG),
    )(v)
    return res if with_sum else res[0]


def _gather_weights(halves):
    n = len(halves)

    def body(*refs):
        ins, outs = refs[:n], refs[n:2 * n]
        send_sems, recv_sems, lsems = refs[2 * n:]
        me = _my_pos()
        x, y, c = me
        sib = (x, y, 1 - c)
        chips = [4, 2, 6]

        def cp(a, k, block, to, src=None):
            dst = outs[a].at[_slot(block)]
            return pltpu.make_async_remote_copy(
                src_ref=dst if src is None else src, dst_ref=dst, send_sem=send_sems.at[a, k],
                recv_sem=recv_sems.at[a, k], device_id=to, device_id_type=MESH)

        mine, first, passed = [], [], []
        for a in range(n):
            m = pltpu.make_async_copy(ins[a], outs[a].at[_slot(me)], lsems.at[a])
            m.start()
            mine.append(m)
            first.append(cp(a, 0, me, sib, src=ins[a]))
            for j, k in enumerate(chips):
                first.append(cp(a, 1 + j, me, _flip(me, k), src=ins[a]))
        for f in first:
            f.start()
        for j, k in enumerate(chips):
            for a in range(n):
                cp(a, 1 + j, _flip(me, k), me).wait_recv()
                p = cp(a, 4 + j, _flip(me, k), sib)
                p.start()
                passed.append(p)
        for a in range(n):
            cp(a, 0, sib, me).wait_recv()
            for j, k in enumerate(chips):
                cp(a, 4 + j, _flip(sib, k), me).wait_recv()
        for f in first + passed:
            f.wait_send()
        for m in mine:
            m.wait()

    vm = pl.BlockSpec(memory_space=pltpu.VMEM)
    return pl.pallas_call(
        body, name="gather_weights",
        out_shape=[jax.ShapeDtypeStruct((8,) + h.shape, h.dtype) for h in halves],
        in_specs=[vm] * n, out_specs=[vm] * n,
        scratch_shapes=[pltpu.SemaphoreType.DMA((n, 7)), pltpu.SemaphoreType.DMA((n, 7)),
                        pltpu.SemaphoreType.DMA((n,))],
        compiler_params=_cp(vmem=VMEM_BIG),
    )(*halves)


def _xfer_start(plan, sems):
    send_sems, recv_sems, lsems = sems
    locals_, xfers = plan
    for i, (src, dst) in enumerate(locals_):
        pltpu.make_async_copy(src, dst, lsems.at[i]).start()
    for i, (src, dst, _, peer) in enumerate(xfers):
        pltpu.make_async_remote_copy(src_ref=src, dst_ref=dst, send_sem=send_sems.at[i], recv_sem=recv_sems.at[i],
                                     device_id=peer, device_id_type=MESH).start()


def _xfer_wait(plan, sems):
    send_sems, recv_sems, lsems = sems
    locals_, xfers = plan
    for i, (src, _, landing, peer) in enumerate(xfers):
        pltpu.make_async_remote_copy(src_ref=src, dst_ref=landing, send_sem=send_sems.at[i], recv_sem=recv_sems.at[i],
                                     device_id=peer, device_id_type=MESH).wait_recv()
    for i, (src, dst, _, peer) in enumerate(xfers):
        pltpu.make_async_remote_copy(src_ref=src, dst_ref=dst, send_sem=send_sems.at[i], recv_sem=recv_sems.at[i],
                                     device_id=peer, device_id_type=MESH).wait_send()
    for i, (src, dst) in enumerate(locals_):
        pltpu.make_async_copy(src, dst, lsems.at[i]).wait()


def _xfer_sems(n_local, n_xfer):
    return [pltpu.SemaphoreType.DMA((n_xfer,)), pltpu.SemaphoreType.DMA((n_xfer,)), pltpu.SemaphoreType.DMA((n_local,))]


def _scatter_plan(ins, outs, pos):
    me = _slot(pos)
    locals_, xfers = [], []
    for src, dst in zip(ins, outs):
        locals_.append((src.at[me], dst.at[me]))
        for k in range(1, 8):
            peer = _flip(pos, k)
            xfers.append((src.at[_slot(peer)], dst.at[me], dst.at[_slot(peer)], peer))
    return locals_, xfers


def _scatter_grads(grads):
    n = len(grads)

    def body(*refs):
        plan = _scatter_plan(refs[:n], refs[n:2 * n], _my_pos())
        _xfer_start(plan, refs[2 * n:])
        _xfer_wait(plan, refs[2 * n:])

    hbm = pl.BlockSpec(memory_space=pltpu.HBM)
    return pl.pallas_call(
        body, name="scatter_grads", out_shape=[jax.ShapeDtypeStruct(g.shape, g.dtype) for g in grads],
        in_specs=[hbm] * n, out_specs=[hbm] * n, scratch_shapes=_xfer_sems(n, 7 * n),
    )(*grads)


def _sibling_exchange(halves):
    n = len(halves)

    def body(*refs):
        ins, outs = refs[:n], refs[n:2 * n]
        send_sems, recv_sems, lsems = refs[2 * n:]
        x, y, c = _my_pos()
        sib = (x, y, 1 - c)
        mine, sends = [], []
        for a in range(n):
            m = pltpu.make_async_copy(ins[a], outs[a].at[c], lsems.at[a])
            m.start()
            mine.append(m)
            cp = pltpu.make_async_remote_copy(
                src_ref=ins[a], dst_ref=outs[a].at[c], send_sem=send_sems.at[a], recv_sem=recv_sems.at[a],
                device_id=sib, device_id_type=MESH)
            cp.start()
            sends.append(cp)
        for a in range(n):
            pltpu.make_async_remote_copy(
                src_ref=ins[a], dst_ref=outs[a].at[1 - c], send_sem=send_sems.at[a], recv_sem=recv_sems.at[a],
                device_id=sib, device_id_type=MESH).wait_recv()
        for cp in sends:
            cp.wait_send()
        for m in mine:
            m.wait()

    vm = pl.BlockSpec(memory_space=pltpu.VMEM)
    return pl.pallas_call(
        body, name="sibling_exchange",
        out_shape=[jax.ShapeDtypeStruct((2,) + h.shape, h.dtype) for h in halves],
        in_specs=[vm] * n, out_specs=[vm] * n,
        scratch_shapes=[pltpu.SemaphoreType.DMA((n,)), pltpu.SemaphoreType.DMA((n,)), pltpu.SemaphoreType.DMA((n,))],
        compiler_params=_cp(vmem=VMEM_BIG),
    )(*halves)


def _row_tile(r, cap=512):
    for t in (cap, 256, 128, 64, 32, 16, 8):
        if t <= cap and r % t == 0:
            return t
    return r


def _cast_bf16(a, name):
    R, W = a.shape
    tr = _row_tile(R, 256)

    def body(a_ref, o_ref):
        o_ref[...] = a_ref[...].astype(BF16)

    return pl.pallas_call(
        body, name=name, grid=(R // tr,), out_shape=jax.ShapeDtypeStruct((R, W), BF16),
        in_specs=[pl.BlockSpec((tr, W), lambda i: (i, 0))], out_specs=pl.BlockSpec((tr, W), lambda i: (i, 0)),
    )(a)


def _sum_slots(r, name):
    _, R, W = r.shape
    tr = _row_tile(R, 128)

    def body(r_ref, o_ref):
        acc = r_ref[0].astype(F32)
        for s_ in range(1, 8):
            acc = acc + r_ref[s_].astype(F32)
        o_ref[...] = acc.astype(BF16)

    return pl.pallas_call(
        body, name=name, grid=(R // tr,), out_shape=jax.ShapeDtypeStruct((R, W), BF16),
        in_specs=[pl.BlockSpec((8, tr, W), lambda i: (0, i, 0))], out_specs=pl.BlockSpec((tr, W), lambda i: (i, 0)),
    )(r)


def _adamw(w, g, m, v, name):
    R, W = w.shape
    tr = _row_tile(R, 256)
    c1 = 1.0 / (1.0 - ADAM_B1 ** ADAM_STEP)
    c2 = 1.0 / (1.0 - ADAM_B2 ** ADAM_STEP)

    def body(w_ref, g_ref, m_ref, v_ref, d_ref, nm_ref, nv_ref, g32_ref):
        gg = g_ref[...].astype(F32)
        g32_ref[...] = gg
        nm = ADAM_B1 * m_ref[...] + (1.0 - ADAM_B1) * gg
        nv = ADAM_B2 * v_ref[...] + (1.0 - ADAM_B2) * (gg * gg)
        d_ref[...] = -ADAM_LR * ((nm * c1) / (jnp.sqrt(nv * c2) + ADAM_EPS) + ADAM_WD * w_ref[...])
        nm_ref[...] = nm
        nv_ref[...] = nv

    spec = pl.BlockSpec((tr, W), lambda i: (i, 0))
    return pl.pallas_call(
        body, name=name, grid=(R // tr,), out_shape=[jax.ShapeDtypeStruct((R, W), F32)] * 4,
        in_specs=[spec] * 4, out_specs=[spec] * 4,
    )(w, g, m, v)


def _matmul_tn(a, b, tm, tn, tk, name, nn=1, b_map=None, out_shape=None, out_block=None, out_map=None,
               b_block=None):
    K, M = a.shape
    nk = K // tk
    nm = M // tm

    def body(a_ref, b_ref, o_ref, acc_ref):
        kk = pl.program_id(2)

        @pl.when(kk == 0)
        def _():
            acc_ref[...] = jnp.zeros_like(acc_ref)

        acc_ref[...] += lax.dot_general(a_ref[...], b_ref[...], _TN, preferred_element_type=F32)

        @pl.when(kk == nk - 1)
        def _():
            o_ref[...] = acc_ref[...].astype(BF16)

    return pl.pallas_call(
        body, name=name, grid=(nm, nn, nk),
        out_shape=jax.ShapeDtypeStruct(out_shape or (M, b.shape[-1]), BF16),
        in_specs=[pl.BlockSpec((tk, tm), lambda i, j, k: (k, i)),
                  pl.BlockSpec(b_block or (tk, tn), b_map or (lambda i, j, k: (k, j)))],
        out_specs=pl.BlockSpec(out_block or (tm, tn), out_map or (lambda i, j, k: (i, j))),
        scratch_shapes=[pltpu.VMEM((tm, tn), F32)],
        compiler_params=_cp(vmem=VMEM_BIG),
    )(a, b)


def _ada_fwd(cin, w_ada_s, b_s):
    tn = 512

    def body(c_ref, w_ref, b_ref, o_ref):
        o_ref[...] = _dg(_silu(c_ref[...]), w_ref[...], _NN) + b_ref[...]

    return pl.pallas_call(
        body, name="ada_fwd", grid=(1536 // tn,), out_shape=jax.ShapeDtypeStruct((64, 1536), F32),
        in_specs=[pl.BlockSpec((64, D), lambda j: (0, 0)), pl.BlockSpec((D, tn), lambda j: (0, j)),
                  pl.BlockSpec((1, tn), lambda j: (0, j))],
        out_specs=pl.BlockSpec((64, tn), lambda j: (0, j)),
        compiler_params=_cp(vmem=VMEM_BIG),
    )(cin, w_ada_s, b_s)


def _ada_bwd(cin, gmod_s, w_ada_s):
    tn = 512

    def body(c_ref, g_ref, w_ref, gw_ref, pc_ref):
        j = pl.program_id(0)
        g = g_ref[...]
        gw_ref[...] = _dg(_silu(c_ref[...]), g, _TN)
        rr = lax.broadcasted_iota(jnp.int32, (8, 64), 0)
        cc = lax.broadcasted_iota(jnp.int32, (8, 64), 1)
        sel = jnp.where((rr == 0) & ((cc & 7) == 2), 1.0, 0.0).astype(F32)
        dctx = _hmm(sel, g)
        part = _dg(dctx, w_ref[...], _NT)

        @pl.when(j == 0)
        def _():
            pc_ref[...] = jnp.zeros_like(pc_ref)

        pc_ref[...] += part

    return pl.pallas_call(
        body, name="ada_bwd", grid=(1536 // tn,),
        out_shape=[jax.ShapeDtypeStruct((D, 1536), F32), jax.ShapeDtypeStruct((8, D), F32)],
        in_specs=[pl.BlockSpec((64, D), lambda j: (0, 0)), pl.BlockSpec((64, tn), lambda j: (0, j)),
                  pl.BlockSpec((D, tn), lambda j: (0, j))],
        out_specs=[pl.BlockSpec((D, tn), lambda j: (0, j)), pl.BlockSpec((8, D), lambda j: (0, 0))],
        compiler_params=_cp(vmem=VMEM_BIG),
    )(cin, gmod_s, w_ada_s)


def _ada_small(gmod, parts, c_ctx):
    def body(g_ref, p_ref, c_ref, gb_ref, gc_ref):
        gb_ref[...] = jnp.sum(g_ref[...], axis=0, keepdims=True)
        ds = p_ref[0][0:1] + p_ref[2][0:1] + p_ref[4][0:1] + p_ref[6][0:1]
        cc = c_ref[...]
        s = _sigmoid(cc)
        gc_ref[...] = ds * (s * (1.0 + cc * (1.0 - s)))

    vm = pl.BlockSpec(memory_space=pltpu.VMEM)
    return pl.pallas_call(
        body, name="ada_small",
        out_shape=[jax.ShapeDtypeStruct((1, 6 * D), F32), jax.ShapeDtypeStruct((1, D), F32)],
        in_specs=[vm, vm, vm], out_specs=[vm, vm],
    )(gmod, parts, c_ctx)


def _tok_specs():
    xs = pl.BlockSpec((None, RT, D), lambda b, j: (b, jnp.maximum(j - 1, 0), 0))
    cs = pl.BlockSpec((None, RT, D), lambda b, j: (b, 0, 0))
    ms = pl.BlockSpec((None, 1, D), lambda b, j: (b, 0, 0))
    cms = pl.BlockSpec((None, 1, D), lambda b, j: (0, 0, 0))
    return xs, cs, ms, cms


def _inproj_fwd(x, ctx, sh1, sc1, csh1, csc1, n1w, w_in_p):
    def body(x_ref, c_ref, sh_ref, sc_ref, csh_ref, csc_ref, w1_ref, w_ref, p_ref, hm_ref):
        is_ctx = pl.program_id(1) == 0
        xt = jnp.where(is_ctx, c_ref[...], x_ref[...])
        sh = jnp.where(is_ctx, csh_ref[...], sh_ref[...])
        sc = jnp.where(is_ctx, csc_ref[...], sc_ref[...])
        hb = _norm_mod(xt, w1_ref[...], sh, sc).astype(BF16)
        hm_ref[...] = hb
        p_ref[...] = jnp.dot(hb, w_ref[...], preferred_element_type=F32)

    xs, cs, ms, cms = _tok_specs()
    return pl.pallas_call(
        body, name="inproj_fwd", grid=(NB, T // RT),
        out_shape=[jax.ShapeDtypeStruct((NB, T, PC), F32), jax.ShapeDtypeStruct((NB, T, D), BF16)],
        in_specs=[xs, cs, ms, ms, cms, cms, pl.BlockSpec((1, D), lambda b, j: (0, 0)),
                  pl.BlockSpec((D, PC), lambda b, j: (0, 0))],
        out_specs=[pl.BlockSpec((None, RT, PC), lambda b, j: (b, j, 0)),
                   pl.BlockSpec((None, RT, D), lambda b, j: (b, j, 0))],
        compiler_params=_cp(vmem=VMEM_BIG),
    )(x, ctx, sh1, sc1, csh1, csc1, n1w, w_in_p)


def _inproj_bwd(dp_dn, dz, dp_na, dgt, w_in_p, x, ctx, sh1, sc1, csh1, csc1, n1w, dx1):
    def body(d1q_ref, d1k_ref, d1v_ref, d2_ref, d3_ref, d4_ref, w_ref, x_ref, c_ref, sh_ref, sc_ref, csh_ref, csc_ref,
             w1_ref, dx1_ref, dpb_ref, gx_ref, dsh_ref, dsc_ref, dcsh_ref, dcsc_ref, dw1_ref):
        b, j = pl.program_id(0), pl.program_id(1)
        is_ctx = j == 0
        dp = jnp.concatenate([d1q_ref[...], d1k_ref[...], d1v_ref[...], d2_ref[...], d3_ref[...], d4_ref[...]],
                             axis=1).astype(BF16)
        dpb_ref[...] = dp
        dhm = lax.dot_general(dp, w_ref[...], _NT, preferred_element_type=F32)
        xt = jnp.where(is_ctx, c_ref[...], x_ref[...])
        sh = jnp.where(is_ctx, csh_ref[...], sh_ref[...])
        sc = jnp.where(is_ctx, csc_ref[...], sc_ref[...])
        _, vjp = jax.vjp(_norm_mod, xt, w1_ref[...], sh, sc)
        dxt, dw1, dsh, dsc = vjp(dhm)

        @pl.when((b == 0) & (j == 0))
        def _():
            dw1_ref[...] = jnp.zeros_like(dw1_ref)
            dcsh_ref[...] = jnp.zeros_like(dcsh_ref)
            dcsc_ref[...] = jnp.zeros_like(dcsc_ref)

        dw1_ref[...] += dw1

        @pl.when(is_ctx)
        def _():
            dcsh_ref[...] += dsh
            dcsc_ref[...] += dsc

        @pl.when(j == 1)
        def _():
            dsh_ref[...] = jnp.zeros_like(dsh_ref)
            dsc_ref[...] = jnp.zeros_like(dsc_ref)

        @pl.when(j >= 1)
        def _():
            dsh_ref[...] += dsh
            dsc_ref[...] += dsc
            gx_ref[...] = dx1_ref[...] + dxt

    xs, cs, ms, cms = _tok_specs()
    tsp = lambda w: pl.BlockSpec((None, RT, w), lambda b, j: (b, j, 0))
    one = pl.BlockSpec((1, D), lambda b, j: (0, 0))
    return pl.pallas_call(
        body, name="inproj_bwd", grid=(NB, T // RT),
        out_shape=[jax.ShapeDtypeStruct((NB, T, PC), BF16), jax.ShapeDtypeStruct((NB, SEQ, D), F32),
                   jax.ShapeDtypeStruct((NB, 1, D), F32), jax.ShapeDtypeStruct((NB, 1, D), F32),
                   jax.ShapeDtypeStruct((1, 1, D), F32), jax.ShapeDtypeStruct((1, 1, D), F32),
                   jax.ShapeDtypeStruct((1, D), F32)],
        in_specs=[tsp(512), tsp(512), tsp(512), tsp(512), tsp(1536), tsp(128), pl.BlockSpec((D, PC), lambda b, j: (0, 0)),
                  xs, cs, ms, ms, cms, cms, one, xs],
        out_specs=[tsp(PC), xs, ms, ms, cms, cms, one],
        compiler_params=_cp(vmem=VMEM_BIG),
    )(*dp_dn, dz, dp_na, dgt, w_in_p, x, ctx, sh1, sc1, csh1, csc1, n1w, dx1)


def _seg_valid(t, off):
    lo = jnp.where(t < CT, 0, CT)
    hi = jnp.where(t < CT, CT, T)
    return (t + off >= lo) & (t + off < hi)


def _shifted(u, off):
    t = lax.broadcasted_iota(jnp.int32, (T, 1), 0)
    s = u if off == 0 else pltpu.roll(u, (-off) % T, 0)
    return jnp.where(_seg_valid(t, off), s, 0.0)


def _conv_pre(u, w):
    acc = _shifted(u, -2) * w[0:1, :]
    for kk in range(1, 5):
        acc = acc + _shifted(u, kk - 2) * w[kk:kk + 1, :]
    return acc


def _act_q(pre):
    a = _silu(pre)
    return a * lax.rsqrt(jnp.sum(a * a, axis=-1, keepdims=True) + EPS) * (128.0 ** -0.5)


def _act_k(pre):
    a = _silu(pre)
    return a * lax.rsqrt(jnp.sum(a * a, axis=-1, keepdims=True) + EPS)


def _dn_prep_fwd(p, cw):
    def body(pq_ref, pk_ref, pv_ref, wq_ref, wk_ref, wv_ref, q_ref, k_ref, v_ref):
        q_ref[...] = _act_q(_conv_pre(pq_ref[...], wq_ref[...]))
        k_ref[...] = _act_k(_conv_pre(pk_ref[...], wk_ref[...]))
        v_ref[...] = _silu(_conv_pre(pv_ref[...], wv_ref[...]))

    ps = lambda o: pl.BlockSpec((None, T, 128), lambda b, h: (b, 0, o + h))
    ws = lambda o: pl.BlockSpec((8, 128), lambda b, h: (0, o + h))
    os_ = pl.BlockSpec((None, T, 128), lambda b, h: (b, 0, h))
    return pl.pallas_call(
        body, name="dn_prep_fwd", grid=(NB, DNH), out_shape=[jax.ShapeDtypeStruct((NB, T, DNW), F32)] * 3,
        in_specs=[ps(0), ps(4), ps(8), ws(0), ws(4), ws(8)], out_specs=[os_] * 3,
        compiler_params=_cp(vmem=VMEM_BIG),
    )(p, p, p, cw, cw, cw)


def _dn_prep_bwd(p, cw, dqs, dks, dvs):
    def body(pq_ref, pk_ref, pv_ref, wq_ref, wk_ref, wv_ref, dqf_ref, dqb_ref, dkf_ref, dkb_ref, dvf_ref, dvb_ref,
             oq_ref, ok_ref, ov_ref, gq_ref, gk_ref, gv_ref):
        b = pl.program_id(1)

        def one(p_ref, w_ref, df_ref, db_ref, act, o_ref, g_ref):
            u = p_ref[...]
            w = w_ref[...]
            _, vjp = jax.vjp(act, _conv_pre(u, w))
            (dpre,) = vjp(df_ref[...] + db_ref[...])
            du = _shifted(dpre, 2) * w[0:1, :]
            rows = [jnp.sum(dpre * _shifted(u, -2), axis=0, keepdims=True)]
            for kk in range(1, 5):
                du = du + _shifted(dpre, 2 - kk) * w[kk:kk + 1, :]
                rows.append(jnp.sum(dpre * _shifted(u, kk - 2), axis=0, keepdims=True))
            o_ref[...] = du
            gw = jnp.concatenate(rows + [jnp.zeros((3, 128), F32)], axis=0)

            @pl.when(b == 0)
            def _():
                g_ref[...] = gw

            @pl.when(b > 0)
            def _():
                g_ref[...] += gw

        one(pq_ref, wq_ref, dqf_ref, dqb_ref, _act_q, oq_ref, gq_ref)
        one(pk_ref, wk_ref, dkf_ref, dkb_ref, _act_k, ok_ref, gk_ref)
        one(pv_ref, wv_ref, dvf_ref, dvb_ref, _silu, ov_ref, gv_ref)

    ps = lambda o: pl.BlockSpec((None, T, 128), lambda h, b: (b, 0, o + h))
    ws = lambda o: pl.BlockSpec((8, 128), lambda h, b: (0, o + h))
    ds = pl.BlockSpec((None, T, 128), lambda h, b: (b, 0, h))
    outs = pl.pallas_call(
        body, name="dn_prep_bwd", grid=(DNH, NB),
        out_shape=[jax.ShapeDtypeStruct((NB, T, DNW), F32)] * 3 + [jax.ShapeDtypeStruct((8, DNW), F32)] * 3,
        in_specs=[ps(0), ps(4), ps(8), ws(0), ws(4), ws(8)] + [ds] * 6,
        out_specs=[ds] * 3
        + [pl.BlockSpec((8, 128), lambda h, b: (0, h))] * 3,
        compiler_params=_cp(vmem=VMEM_BIG),
    )(p, p, p, cw, cw, cw, *dqs, *dks, *dvs)
    return outs


def _gate_fn(a, prm):
    lane = lax.broadcasted_iota(jnp.int32, a.shape, 1)
    z = a + prm[1:2, :]
    sp = jnp.maximum(z, 0.0) + jnp.log(1.0 + jnp.exp(-jnp.abs(z)))
    g = -jnp.exp(prm[0:1, :]) * sp
    return jnp.where(lane < 8, g, jnp.where(lane < 16, _sigmoid(a), 0.0))


def _gates_fwd(p, prm):
    def body(a_ref, prm_ref, o_ref):
        o_ref[...] = _gate_fn(a_ref[...], prm_ref[...])

    return pl.pallas_call(
        body, name="gates_fwd", grid=(NB,), out_shape=jax.ShapeDtypeStruct((NB, T, 128), F32),
        in_specs=[pl.BlockSpec((None, T, 128), lambda b: (b, 0, COL_G // 128)), pl.BlockSpec((8, 128), lambda b: (0, 0))],
        out_specs=pl.BlockSpec((None, T, 128), lambda b: (b, 0, 0)),
    )(p, prm)


def _gates_bwd(p, prm, dgf, dgb):
    def body(a_ref, prm_ref, dgf_ref, dgb_ref, o_ref, dprm_ref):
        b = pl.program_id(0)
        tot = dgf_ref[...] + dgb_ref[...]
        _, vjp = jax.vjp(_gate_fn, a_ref[...], prm_ref[...])
        da, dprm = vjp(tot)
        o_ref[...] = da

        @pl.when(b == 0)
        def _():
            dprm_ref[...] = dprm

        @pl.when(b > 0)
        def _():
            dprm_ref[...] += dprm

    return pl.pallas_call(
        body, name="gates_bwd", grid=(NB,),
        out_shape=[jax.ShapeDtypeStruct((NB, T, 128), F32), jax.ShapeDtypeStruct((8, 128), F32)],
        in_specs=[pl.BlockSpec((None, T, 128), lambda b: (b, 0, COL_G // 128)), pl.BlockSpec((8, 128), lambda b: (0, 0)),
                  pl.BlockSpec((None, T, 128), lambda b: (b, 0, 0)), pl.BlockSpec((None, T, 128), lambda b: (b, 0, 0))],
        out_specs=[pl.BlockSpec((None, T, 128), lambda b: (b, 0, 0)), pl.BlockSpec((8, 128), lambda b: (0, 0))],
        compiler_params=_cp(vmem=VMEM_BIG),
    )(p, prm, dgf, dgb)


def _lockstep(gens, joint=None):
    n = len(gens)
    res, vals = [None] * n, [None] * n
    while True:
        msgs = []
        for i, g in enumerate(gens):
            try:
                msgs.append((g.send(vals[i]), False))
            except StopIteration as e:
                res[i] = e.value
                msgs.append((None, True))
        if msgs[0][1]:
            return res
        if msgs[0][0] is not None:
            vals = list(joint(tuple(m[0] for m in msgs)))
        else:
            vals = [None] * n


def _tri_inv_gen(m, md, m1, m2, eye):
    n = -(m * md)
    t = eye + n
    pw = n
    for _ in range(3):
        pw = _dg(pw, pw, _NN)
        yield
        t = t + _dg(t, pw, _NN)
        yield
    a = _dg(t, m * m1, _NN)
    yield
    t = t - _dg(a, t, _NN)
    yield
    a = _dg(t, m * m2, _NN)
    yield
    return t - _dg(a, t, _NN)


def _tri_inv_bwd_gen(t, g):
    tt = t.T
    a = _dg(tt, g, _NN)
    yield
    return -_dg(a, tt, _NN)


@jax.custom_vjp
def _tri_inv(ms, md, m1, m2, eye):
    return tuple(_lockstep([_tri_inv_gen(m, md, m1, m2, eye) for m in ms]))


def _tri_inv_f(ms, md, m1, m2, eye):
    ts = _tri_inv(ms, md, m1, m2, eye)
    return ts, (ts, md)


def _tri_inv_b(res, gs):
    ts, md = res
    z = jnp.zeros_like(md)
    return (tuple(_lockstep([_tri_inv_bwd_gen(t, g) for t, g in zip(ts, gs)])), z, z, z, z)


_tri_inv.defvjp(_tri_inv_f, _tri_inv_b)


SR = DNH * CH


def _scan_masks():
    i, j = np.arange(SR)[:, None], np.arange(SR)[None, :]
    same = (i >> 6) == (j >> 6)
    cm = np.zeros((2, 6, SR, SR), np.float32)
    for d in range(2):
        diff = i - j if d == 0 else j - i
        cm[d, 0] = same & (diff >= 0)
        cm[d, 1] = same & (diff > 0)
        cm[d, 2] = (i >> 4) == (j >> 4)
        cm[d, 3] = ((i >> 5) == (j >> 5)) & ((i >> 4) != (j >> 4))
        cm[d, 4] = (i >> 5) != (j >> 5)
        cm[d, 5] = i == j
    t, u = np.arange(CH)[:, None], np.arange(CH)[None, :]
    c64 = np.stack([t >= u, t <= u]).astype(np.float32)
    bm = ((np.arange(SR)[:, None] >> 6) == (np.arange(DNW)[None, :] >> 7)).astype(np.float32)
    return jnp.asarray(cm), jnp.asarray(c64), jnp.asarray(bm)


@jax.custom_vjp
def _fold(x, bm):
    xm = x * bm
    return xm[:, 0:128] + xm[:, 128:256] + xm[:, 256:384] + xm[:, 384:512]


def _fold_f(x, bm):
    return _fold(x, bm), bm


def _fold_b(bm, g):
    return jnp.concatenate([g, g, g, g], axis=1) * bm, jnp.zeros_like(bm)


_fold.defvjp(_fold_f, _fold_b)


def _chunk_gen(q, k, v, gb, s, cm, c64, bm, d):
    incl, strict = cm[0], cm[1]
    lane = lax.broadcasted_iota(jnp.int32, (1, 128), 1)
    oh = lambda n: jnp.where(lane == n, 1.0, 0.0).astype(F32)
    col = lambda x, base: jnp.concatenate(
        [jnp.sum(x * oh(base + d * DNH + h), axis=1, keepdims=True) for h in range(DNH)], axis=0)
    beta = col(gb, 8)
    gam = col(_mask_mm(c64, gb), 0)
    yield
    tot = jnp.sum(gb, axis=0, keepdims=True)
    gtots = [jnp.sum(tot * oh(d * DNH + h), axis=1, keepdims=True) for h in range(DNH)]
    gtot = jnp.concatenate([jnp.broadcast_to(t, (CH, 1)) for t in gtots], axis=0)
    eg = jnp.concatenate([jnp.broadcast_to(jnp.exp(t), (1, 128)) for t in gtots], axis=1)
    row0 = jnp.where(lax.broadcasted_iota(jnp.int32, (128, 1), 0) == 0, 1.0, 0.0).astype(F32)
    gam_row = jnp.sum(jnp.broadcast_to(gam, (SR, 128)).T * row0, axis=0, keepdims=True)
    dec = jnp.exp(jnp.where(incl > 0.5, gam - gam_row, NEG))
    e_gam = jnp.exp(gam)
    kb = k * beta
    kk = _mm_nt(kb, k)
    yield
    tm = yield strict * (kk * dec)
    u = _mm(tm, v * beta)
    yield
    w = _mm(tm, kb * e_gam)
    yield
    qk = _mm_nt(q, k) * dec
    yield
    ws = _mm(w, s)
    yield
    v_new = u - _fold(ws, bm)
    qs = _mm(q * e_gam, s)
    yield
    o = _fold(qs, bm) + _mm(qk, v_new)
    yield
    vbd = jnp.concatenate([v_new, v_new, v_new, v_new], axis=1) * bm
    s_new = s * eg + _mm_tn(k * jnp.exp(gtot - gam), vbd)
    return o, s_new


def _chunks(args, cm_ref, c64_ref, bm):
    cm0 = cm_ref[0]
    inv = lambda ms: _tri_inv(ms, cm0[2], cm0[3], cm0[4], cm0[5])
    return _lockstep([_chunk_gen(q, k, v, gb, s, cm_ref[d], c64_ref[d], bm, d) for q, k, v, gb, s, d in args], inv)


def _chunk_rev(s):
    return jnp.where(s < NCH_CTX, NCH_CTX - 1 - s, NCH + NCH_CTX - 1 - s)


def _scan_specs(step):
    cf = lambda s: step(s)
    cb = lambda s: _chunk_rev(step(s))
    tok = lambda c: pl.BlockSpec((NB, CH, DNW), lambda s: (0, c(s), 0))
    gat = lambda c: pl.BlockSpec((NB, CH, 128), lambda s: (0, c(s), 0))
    sta = lambda c: pl.BlockSpec((NB, None, 128, DNW), lambda s: (0, c(s), 0, 0))
    return (tok(cf), gat(cf), sta(cf)), (tok(cb), gat(cb), sta(cb))


def _const_specs():
    return [pl.BlockSpec((2, 6, SR, SR), lambda s: (0, 0, 0, 0)), pl.BlockSpec((2, CH, CH), lambda s: (0, 0, 0)),
            pl.BlockSpec((SR, DNW), lambda s: (0, 0))]


SCAN_CHAINS = [(b, d) for b in range(NB) for d in range(2)]


def _stack(ref, b):
    return jnp.concatenate([ref[b, :, 128 * h:128 * h + 128] for h in range(DNH)], axis=0)


def _unstack_to(ref, b, val):
    for h in range(DNH):
        ref[b, :, 128 * h:128 * h + 128] = val[CH * h:CH * h + CH]


def _gather_plan(ins, outs, pos):
    me = _slot(pos)
    locals_, xfers = [], []
    for src, dst in zip(ins, outs):
        locals_.append((src, dst.at[me]))
        for k in range(1, 8):
            peer = _flip(pos, k)
            xfers.append((src, dst.at[me], dst.at[_slot(peer)], peer))
    return locals_, xfers


def _scan_fwd(q, k, v, gates, carry=()):
    nc = len(carry)

    def body(*refs):
        cm_ref, c64_ref, bm_ref, qf, kf, vf, gf, qb, kb, vb, gb = refs[:11]
        c_in = refs[11:11 + nc]
        of_ref, sf_ref, ob_ref, sb_ref = refs[11 + nc:15 + nc]
        c_out = refs[15 + nc:15 + 2 * nc]
        s_ref = refs[15 + 2 * nc]
        sems = refs[16 + 2 * nc:]
        if nc:
            @pl.when(pl.program_id(0) == 0)
            def _():
                _xfer_start(_gather_plan(c_in, c_out, _my_pos()), sems)

        @pl.when(pl.program_id(0) == 0)
        def _():
            s_ref[...] = jnp.zeros_like(s_ref)

        bm = bm_ref[...]
        dirs = ((qf, kf, vf, gf, of_ref, sf_ref), (qb, kb, vb, gb, ob_ref, sb_ref))
        args = [(_stack(dirs[d][0], b), _stack(dirs[d][1], b), _stack(dirs[d][2], b), dirs[d][3][b], s_ref[i], d)
                for i, (b, d) in enumerate(SCAN_CHAINS)]
        for i, (o, s_new) in enumerate(_chunks(args, cm_ref, c64_ref, bm)):
            b, d = SCAN_CHAINS[i]
            dirs[d][5][b] = args[i][4]
            _unstack_to(dirs[d][4], b, o)
            s_ref[i] = s_new

        if nc:
            @pl.when(pl.program_id(0) == NCH - 1)
            def _():
                _xfer_wait(_gather_plan(c_in, c_out, _my_pos()), sems)

    (tf, gf_, sf_), (tb, gb_, sb_) = _scan_specs(lambda s: s)
    o_sh = jax.ShapeDtypeStruct((NB, T, DNW), F32)
    st_sh = jax.ShapeDtypeStruct((NB, NCH, 128, DNW), F32)
    hbm = pl.BlockSpec(memory_space=pltpu.HBM)
    return pl.pallas_call(
        body, name="scan_fwd", grid=(NCH,),
        out_shape=[o_sh, st_sh, o_sh, st_sh] + [jax.ShapeDtypeStruct((8,) + h.shape, h.dtype) for h in carry],
        in_specs=_const_specs() + [tf, tf, tf, gf_, tb, tb, tb, gb_] + [hbm] * nc,
        out_specs=[tf, sf_, tb, sb_] + [hbm] * nc,
        scratch_shapes=[pltpu.VMEM((len(SCAN_CHAINS), 128, DNW), F32)] + (_xfer_sems(nc, 7 * nc) if nc else []),
        compiler_params=_cp(vmem=VMEM_BIG),
    )(*_scan_masks(), q, k, v, gates, q, k, v, gates, *carry)


def _scan_bwd(q, k, v, gates, st_f, st_b, do, carry=()):
    nc = len(carry)

    def body(*refs):
        (cm_ref, c64_ref, bm_ref, qf, kf, vf, gf, sf, dof, qb, kb, vb, gb, sb, dob) = refs[:15]
        c_in = refs[15:15 + nc]
        dqf, dkf, dvf, dgf, dqb, dkb, dvb, dgb = refs[15 + nc:23 + nc]
        c_out = refs[23 + nc:23 + 2 * nc]
        ds_ref = refs[23 + 2 * nc]
        sems = refs[24 + 2 * nc:]
        first = pl.program_id(0) == 0
        last = pl.program_id(0) == NCH - 1
        if nc:
            @pl.when(first)
            def _():
                _xfer_start(_scatter_plan(c_in, c_out, _my_pos()), sems)

        @pl.when(first)
        def _():
            ds_ref[...] = jnp.zeros_like(ds_ref)

        bm = bm_ref[...]
        dirs = ((qf, kf, vf, gf, sf, dof, dqf, dkf, dvf, dgf), (qb, kb, vb, gb, sb, dob, dqb, dkb, dvb, dgb))
        prim = tuple((_stack(dirs[d][0], b), _stack(dirs[d][1], b), _stack(dirs[d][2], b), dirs[d][3][b], dirs[d][4][b])
                     for b, d in SCAN_CHAINS)
        cots = tuple((_stack(dirs[d][5], b), ds_ref[i]) for i, (b, d) in enumerate(SCAN_CHAINS))
        _, vjp = jax.vjp(lambda ps: tuple(_chunks([p + (SCAN_CHAINS[i][1],) for i, p in enumerate(ps)],
                                                  cm_ref, c64_ref, bm)), prim)
        (grads,) = vjp(cots)
        for i, (dq, dk, dv, dg, ds) in enumerate(grads):
            b, d = SCAN_CHAINS[i]
            _unstack_to(dirs[d][6], b, dq)
            _unstack_to(dirs[d][7], b, dk)
            _unstack_to(dirs[d][8], b, dv)
            dirs[d][9][b] = dg
            ds_ref[i] = ds

        if nc:
            @pl.when(last)
            def _():
                _xfer_wait(_scatter_plan(c_in, c_out, _my_pos()), sems)

    (tf, gf_, sf_), (tb, gb_, sb_) = _scan_specs(lambda s: NCH - 1 - s)
    o_sh = jax.ShapeDtypeStruct((NB, T, DNW), F32)
    g_sh = jax.ShapeDtypeStruct((NB, T, 128), F32)
    hbm = pl.BlockSpec(memory_space=pltpu.HBM)
    return pl.pallas_call(
        body, name="scan_bwd", grid=(NCH,),
        out_shape=[o_sh, o_sh, o_sh, g_sh] * 2 + [jax.ShapeDtypeStruct(g.shape, g.dtype) for g in carry],
        in_specs=_const_specs() + [tf, tf, tf, gf_, sf_, tf, tb, tb, tb, gb_, sb_, tb] + [hbm] * nc,
        out_specs=[tf, tf, tf, gf_, tb, tb, tb, gb_] + [hbm] * nc,
        scratch_shapes=[pltpu.VMEM((len(SCAN_CHAINS), 128, DNW), F32)] + (_xfer_sems(nc, 7 * nc) if nc else []),
        compiler_params=_cp(vmem=VMEM_BIG),
    )(*_scan_masks(), q, k, v, gates, st_f, do, q, k, v, gates, st_b, do, *carry)


def _segmean(a, bd):
    hi = a.astype(BF16)
    lo = (a - hi.astype(F32)).astype(BF16)
    return jnp.dot(hi, bd, preferred_element_type=F32) + jnp.dot(lo, bd, preferred_element_type=F32)


def _na_prep_fwd(p, qw, kw, bd):
    def body(q_ref, k_ref, v_ref, qw_ref, kw_ref, bd_ref, oq_ref, ok_ref, ov_ref):
        bdm = bd_ref[...]
        q = q_ref[...]
        k = k_ref[...]
        oq_ref[...] = (q * lax.rsqrt(_segmean(q * q, bdm) + EPS) * qw_ref[...] * 0.125).astype(BF16)
        ok_ref[...] = (k * lax.rsqrt(_segmean(k * k, bdm) + EPS) * kw_ref[...]).astype(BF16)
        ov_ref[...] = v_ref[...].astype(BF16)

    ps = lambda o: pl.BlockSpec((None, RT, NAW), lambda b, j: (b, j, COL_NA // NAW + o))
    one = pl.BlockSpec((1, NAW), lambda b, j: (0, 0))
    os_ = pl.BlockSpec((None, RT, NAW), lambda b, j: (b, j, 0))
    return pl.pallas_call(
        body, name="na_prep_fwd", grid=(NB, T // RT), out_shape=[jax.ShapeDtypeStruct((NB, T, NAW), BF16)] * 3,
        in_specs=[ps(0), ps(1), ps(2), one, one, pl.BlockSpec((NAW, NAW), lambda b, j: (0, 0))], out_specs=[os_] * 3,
    )(p, p, p, qw, kw, bd)


def _na_prep_bwd(p, qw, kw, bd, dq, dk, dv):
    def body(q_ref, k_ref, qw_ref, kw_ref, bd_ref, dq_ref, dk_ref, dv_ref, o_ref, gq_ref, gk_ref):
        b, j = pl.program_id(0), pl.program_id(1)
        bdm = bd_ref[...]

        def one(x, w, dy, scale):
            r = lax.rsqrt(_segmean(x * x, bdm) + EPS)
            g = dy * w * scale
            dx = r * g - x * (r * r * r) * _segmean(g * x, bdm)
            dw = jnp.sum(dy * x * r * scale, axis=0, keepdims=True)
            return dx, dw

        dqv = jnp.where(j == 0, 0.0, dq_ref[...])
        dxq, dwq = one(q_ref[...], qw_ref[...], dqv, 0.125)
        dxk, dwk = one(k_ref[...], kw_ref[...], dk_ref[...], 1.0)
        o_ref[...] = jnp.concatenate([dxq, dxk, dv_ref[...]], axis=1)
        pad = jnp.zeros((7, NAW), F32)

        @pl.when((b == 0) & (j == 0))
        def _():
            gq_ref[...] = jnp.zeros_like(gq_ref)
            gk_ref[...] = jnp.zeros_like(gk_ref)

        gq_ref[...] += jnp.concatenate([dwq, pad], axis=0)
        gk_ref[...] += jnp.concatenate([dwk, pad], axis=0)

    ps = lambda o: pl.BlockSpec((None, RT, NAW), lambda b, j: (b, j, COL_NA // NAW + o))
    one = pl.BlockSpec((1, NAW), lambda b, j: (0, 0))
    ts = pl.BlockSpec((None, RT, NAW), lambda b, j: (b, j, 0))
    acc = pl.BlockSpec((8, NAW), lambda b, j: (0, 0))
    return pl.pallas_call(
        body, name="na_prep_bwd", grid=(NB, T // RT),
        out_shape=[jax.ShapeDtypeStruct((NB, T, 1536), F32), jax.ShapeDtypeStruct((8, NAW), F32),
                   jax.ShapeDtypeStruct((8, NAW), F32)],
        in_specs=[ps(0), ps(1), one, one, pl.BlockSpec((NAW, NAW), lambda b, j: (0, 0)),
                  pl.BlockSpec((None, RT, NAW), lambda b, j: (b, jnp.maximum(j - 1, 0), 0)), ts, ts],
        out_specs=[pl.BlockSpec((None, RT, 1536), lambda b, j: (b, j, 0)), acc, acc],
        compiler_params=_cp(vmem=VMEM_BIG),
    )(p, p, qw, kw, bd, dq, dk, dv)


def _na_row0(r):
    return jnp.clip(r - WIN_ROWS // 2, 0, ROWS - WIN_ROWS)


def _na_cls(r):
    return _na_row0(r) - r + WIN_ROWS - 1


NHS = 4
NLW = NHS * 64
NSR = NHS * GRID_W


def _na_stack(x):
    lane = lax.broadcasted_iota(jnp.int32, (1, NLW), 1)
    return jnp.concatenate([jnp.where((lane >> 6) == h, x, jnp.zeros_like(x)) for h in range(NHS)], axis=0)


def _na_unstack(y):
    lane = lax.broadcasted_iota(jnp.int32, (1, NLW), 1)
    out = jnp.zeros((GRID_W, NLW), y.dtype)
    for h in range(NHS):
        out = jnp.where((lane >> 6) == h, y[GRID_W * h:GRID_W * h + GRID_W], out)
    return out


def _na_probs(q4, kw, kc, bias):
    s_w = lax.dot_general(q4, kw, _NT, preferred_element_type=F32) + bias
    s_c = lax.dot_general(q4, kc, _NT, preferred_element_type=F32)
    mx = jnp.maximum(jnp.max(s_w, axis=1, keepdims=True), jnp.max(s_c, axis=1, keepdims=True))
    p_w = jnp.exp(s_w - mx)
    p_c = jnp.exp(s_c - mx)
    inv = 1.0 / (jnp.sum(p_w, axis=1, keepdims=True) + jnp.sum(p_c, axis=1, keepdims=True))
    return p_w, p_c, inv


def _na_specs():
    qs = pl.BlockSpec((None, GRID_W, NLW), lambda g, b, r: (b, CT // GRID_W + r, g))
    kv = pl.BlockSpec((None, T, NLW), lambda g, b, r: (b, 0, g))
    bs = pl.BlockSpec((None, None, NSR, NWIN), lambda g, b, r: (_na_cls(r), g, 0, 0))
    lat = pl.BlockSpec((None, GRID_W, NLW), lambda g, b, r: (b, r, g))
    return qs, kv, bs, lat


def _na_fwd(qn, kn, vb, bias, carry=()):
    nc = len(carry)
    grid = (NAH // NHS, NB, ROWS)

    def body(*refs):
        q_ref, k_ref, v_ref, b_ref = refs[:4]
        c_in, o_ref, c_out, sems = refs[4:4 + nc], refs[4 + nc], refs[5 + nc:5 + 2 * nc], refs[5 + 2 * nc:]
        ids = [pl.program_id(a) for a in range(3)]
        if nc:
            @pl.when((ids[0] == 0) & (ids[1] == 0) & (ids[2] == 0))
            def _():
                _xfer_start(_gather_plan(c_in, c_out, _my_pos()), sems)

        r = ids[2]
        start = pl.multiple_of(CT + _na_row0(r) * GRID_W, GRID_W)
        kw, vw = k_ref[pl.ds(start, NWIN), :], v_ref[pl.ds(start, NWIN), :]
        kc, vc = k_ref[0:CT, :], v_ref[0:CT, :]
        p_w, p_c, inv = _na_probs(_na_stack(q_ref[...]), kw, kc, b_ref[...])
        o4 = (jnp.dot(p_w.astype(BF16), vw, preferred_element_type=F32)
              + jnp.dot(p_c.astype(BF16), vc, preferred_element_type=F32)) * inv
        o_ref[...] = _na_unstack(o4).astype(BF16)

        if nc:
            @pl.when((ids[0] == grid[0] - 1) & (ids[1] == grid[1] - 1) & (ids[2] == grid[2] - 1))
            def _():
                _xfer_wait(_gather_plan(c_in, c_out, _my_pos()), sems)

    qs, kv, bs, lat = _na_specs()
    hbm = pl.BlockSpec(memory_space=pltpu.HBM)
    res = pl.pallas_call(
        body, name="na_fwd", grid=grid,
        out_shape=[jax.ShapeDtypeStruct((NB, SEQ, NAW), BF16)] + [jax.ShapeDtypeStruct((8,) + h.shape, h.dtype) for h in carry],
        in_specs=[qs, kv, kv, bs] + [hbm] * nc, out_specs=[lat] + [hbm] * nc,
        scratch_shapes=_xfer_sems(nc, 7 * nc) if nc else [],
        compiler_params=_cp(vmem=VMEM_BIG),
    )(qn, kn, vb, bias, *carry)
    return res[0], list(res[1:])


def _na_bwd(qn, kn, vb, bias, do):
    def body(q_ref, k_ref, v_ref, b_ref, do_ref, dq_ref, dk_ref, dv_ref, db_ref):
        b, r = pl.program_id(1), pl.program_id(2)

        @pl.when((b == 0) & (r == 0))
        def _():
            db_ref[...] = jnp.zeros_like(db_ref)

        @pl.when(r == 0)
        def _():
            dk_ref[...] = jnp.zeros_like(dk_ref)
            dv_ref[...] = jnp.zeros_like(dv_ref)

        start = pl.multiple_of(CT + _na_row0(r) * GRID_W, GRID_W)
        cls = _na_cls(r)
        kw, vw = k_ref[pl.ds(start, NWIN), :], v_ref[pl.ds(start, NWIN), :]
        kc, vc = k_ref[0:CT, :], v_ref[0:CT, :]
        q4 = _na_stack(q_ref[...])
        p_w, p_c, inv = _na_probs(q4, kw, kc, b_ref[...])
        p_w = p_w * inv
        p_c = p_c * inv
        do4 = _na_stack(do_ref[...].astype(BF16))
        dp_w = lax.dot_general(do4, vw, _NT, preferred_element_type=F32)
        dp_c = lax.dot_general(do4, vc, _NT, preferred_element_type=F32)
        delta = jnp.sum(p_w * dp_w, axis=1, keepdims=True) + jnp.sum(p_c * dp_c, axis=1, keepdims=True)
        ds_w = p_w * (dp_w - delta)
        ds_c = p_c * (dp_c - delta)
        db_ref[cls] += ds_w
        ds_wb, ds_cb = ds_w.astype(BF16), ds_c.astype(BF16)
        dq_ref[...] = _na_unstack(jnp.dot(ds_wb, kw, preferred_element_type=F32)
                                  + jnp.dot(ds_cb, kc, preferred_element_type=F32))
        dk_ref[pl.ds(start, NWIN), :] += lax.dot_general(ds_wb, q4, _TN, preferred_element_type=F32)
        dv_ref[pl.ds(start, NWIN), :] += lax.dot_general(p_w.astype(BF16), do4, _TN, preferred_element_type=F32)
        dk_ref[0:CT, :] += lax.dot_general(ds_cb, q4, _TN, preferred_element_type=F32)
        dv_ref[0:CT, :] += lax.dot_general(p_c.astype(BF16), do4, _TN, preferred_element_type=F32)

    qs, kv, bs, lat = _na_specs()
    return pl.pallas_call(
        body, name="na_bwd", grid=(NAH // NHS, NB, ROWS),
        out_shape=[jax.ShapeDtypeStruct((NB, SEQ, NAW), F32)] + [jax.ShapeDtypeStruct((NB, T, NAW), F32)] * 2
        + [jax.ShapeDtypeStruct((WIN_ROWS, NAH // NHS, NSR, NWIN), F32)],
        in_specs=[qs, kv, kv, bs, lat],
        out_specs=[lat, kv, kv, pl.BlockSpec((WIN_ROWS, None, NSR, NWIN), lambda g, b, r: (0, g, 0, 0))],
        compiler_params=_cp(vmem=VMEM_BIG),
    )(qn, kn, vb, bias, do)


def _na_tables():
    cols = np.arange(GRID_W)
    win_start = np.clip(cols - WIN_COLS // 2, 0, GRID_W - WIN_COLS)
    kc = cols[None, :]
    valid = (kc >= win_start[:, None]) & (kc < win_start[:, None] + WIN_COLS)
    rel = np.clip(kc - cols[:, None] + WIN_COLS - 1, 0, 2 * WIN_COLS - 2)
    return valid, rel


def _bias_table(rpb):
    valid, rel = _na_tables()
    t15 = jnp.where(valid[None, None], rpb[:, :, rel], NEG)
    tab = jnp.stack([t15[:, c:c + WIN_ROWS] for c in range(WIN_ROWS)], axis=0)
    return tab.transpose(0, 1, 3, 2, 4).reshape(WIN_ROWS, NAH // NHS, NSR, NWIN)


def _rpb_fold(db):
    def body1(d_ref, o_ref):
        for dr in range(2 * WIN_ROWS - 1):
            acc = None
            for c in range(WIN_ROWS):
                i = dr - c
                if 0 <= i < WIN_ROWS:
                    blk = d_ref[c, :, i * GRID_W:(i + 1) * GRID_W]
                    acc = blk if acc is None else acc + blk
            o_ref[dr] = acc

    t15 = pl.pallas_call(
        body1, name="rpb_fold_rows", grid=(NAH,),
        out_shape=jax.ShapeDtypeStruct((NAH, 2 * WIN_ROWS - 1, GRID_W, GRID_W), F32),
        in_specs=[pl.BlockSpec((WIN_ROWS, None, GRID_W, NWIN), lambda h: (0, h, 0, 0))],
        out_specs=pl.BlockSpec((None, 2 * WIN_ROWS - 1, GRID_W, GRID_W), lambda h: (h, 0, 0, 0)),
        compiler_params=_cp(vmem=VMEM_BIG),
    )(db.reshape(WIN_ROWS, NAH, GRID_W, NWIN))
    valid, rel = _na_tables()
    onehot = np.zeros((GRID_W * GRID_W, 128), np.float32)
    flat_rel, flat_valid = rel.reshape(-1), valid.reshape(-1)
    onehot[np.arange(GRID_W * GRID_W)[flat_valid], flat_rel[flat_valid]] = 1.0

    def body2(a_ref, oh_ref, o_ref):
        o_ref[...] = _hmm(a_ref[...], oh_ref[...])

    vm = pl.BlockSpec(memory_space=pltpu.VMEM)
    out = pl.pallas_call(
        body2, name="rpb_fold_cols", out_shape=jax.ShapeDtypeStruct((NAH * 15, 128), F32),
        in_specs=[vm, vm], out_specs=vm, compiler_params=_cp(vmem=VMEM_BIG),
    )(t15.reshape(NAH * 15, GRID_W * GRID_W), jnp.asarray(onehot))
    return out[:, :31].reshape(NAH, 15, 31)


def _dn_out(o, z, w):
    outs = []
    for h in range(DNH):
        sl = slice(128 * h, 128 * h + 128)
        oh = o[:, sl]
        r = lax.rsqrt(jnp.mean(oh * oh, axis=-1, keepdims=True) + EPS)
        outs.append(oh * r * w[:, sl] * _silu(z[:, sl]))
    return jnp.concatenate(outs, axis=1)


def _outproj_fwd(of, ob, p, na, onw, x, g1, w_out):
    def body(of_ref, ob_ref, z_ref, na_ref, w_ref, x_ref, g_ref, wo_ref, x1_ref, ai_ref, ao_ref):
        dn = _dn_out(of_ref[...] + ob_ref[...], z_ref[...], w_ref[...])
        ai = jnp.concatenate([dn.astype(BF16), na_ref[...]], axis=1)
        ai_ref[...] = ai
        ao = jnp.dot(ai, wo_ref[...], preferred_element_type=F32)
        ao_ref[...] = ao
        x1_ref[...] = x_ref[...] + g_ref[...] * ao

    lat = lambda w: pl.BlockSpec((None, RT, w), lambda b, j: (b, j, 0))
    return pl.pallas_call(
        body, name="outproj_fwd", grid=(NB, SEQ // RT),
        out_shape=[jax.ShapeDtypeStruct((NB, SEQ, D), F32), jax.ShapeDtypeStruct((NB, SEQ, D), BF16),
                   jax.ShapeDtypeStruct((NB, SEQ, D), F32)],
        in_specs=[pl.BlockSpec((None, RT, DNW), lambda b, j: (b, j + 1, 0)),
                  pl.BlockSpec((None, RT, DNW), lambda b, j: (b, j + 1, 0)),
                  pl.BlockSpec((None, RT, DNW), lambda b, j: (b, j + 1, COL_Z // DNW)),
                  lat(NAW), pl.BlockSpec((1, DNW), lambda b, j: (0, 0)), lat(D),
                  pl.BlockSpec((None, 1, D), lambda b, j: (b, 0, 0)), pl.BlockSpec((D, D), lambda b, j: (0, 0))],
        out_specs=[lat(D), lat(D), lat(D)],
        compiler_params=_cp(vmem=VMEM_BIG),
    )(of, ob, p, na, onw, x, g1, w_out)


def _outproj_bwd(dx1, ao, g1, w_out, of, ob, p, onw):
    def body(dx_ref, ao_ref, g_ref, wo_ref, of_ref, ob_ref, z_ref, w_ref, dao_ref, do_ref, dz_ref, dna_ref, dg_ref, dw_ref):
        b, j = pl.program_id(0), pl.program_id(1)

        @pl.when((b == 0) & (j == 0))
        def _():
            dw_ref[...] = jnp.zeros_like(dw_ref)

        @pl.when(j == 0)
        def _():
            do_ref[...] = jnp.zeros_like(do_ref)
            dz_ref[...] = jnp.zeros_like(dz_ref)

        @pl.when(j == 1)
        def _():
            dg_ref[...] = jnp.zeros_like(dg_ref)

        @pl.when(j >= 1)
        def _():
            dx = dx_ref[...]
            dg_ref[...] += jnp.sum(dx * ao_ref[...], axis=0, keepdims=True)
            dao = (g_ref[...] * dx).astype(BF16)
            dao_ref[...] = dao
            dai = lax.dot_general(dao, wo_ref[...], _NT, preferred_element_type=F32)
            dna_ref[...] = dai[:, DNW:]
            _, vjp = jax.vjp(_dn_out, of_ref[...] + ob_ref[...], z_ref[...], w_ref[...])
            do, dz, dw = vjp(dai[:, :DNW])
            do_ref[...] = do
            dz_ref[...] = dz
            dw_ref[...] += jnp.concatenate([dw, jnp.zeros((7, DNW), F32)], axis=0)

    lat = lambda w: pl.BlockSpec((None, RT, w), lambda b, j: (b, jnp.maximum(j - 1, 0), 0))
    tok = lambda w: pl.BlockSpec((None, RT, w), lambda b, j: (b, j, 0))
    return pl.pallas_call(
        body, name="outproj_bwd", grid=(NB, T // RT),
        out_shape=[jax.ShapeDtypeStruct((NB, SEQ, D), BF16), jax.ShapeDtypeStruct((NB, T, DNW), F32),
                   jax.ShapeDtypeStruct((NB, T, DNW), F32), jax.ShapeDtypeStruct((NB, SEQ, NAW), F32),
                   jax.ShapeDtypeStruct((NB, 1, D), F32), jax.ShapeDtypeStruct((8, DNW), F32)],
        in_specs=[lat(D), lat(D), pl.BlockSpec((None, 1, D), lambda b, j: (b, 0, 0)),
                  pl.BlockSpec((D, D), lambda b, j: (0, 0)),
                  tok(DNW), tok(DNW),
                  pl.BlockSpec((None, RT, DNW), lambda b, j: (b, j, COL_Z // DNW)),
                  pl.BlockSpec((1, DNW), lambda b, j: (0, 0))],
        out_specs=[lat(D), tok(DNW), tok(DNW), lat(NAW), pl.BlockSpec((None, 1, D), lambda b, j: (b, 0, 0)),
                   pl.BlockSpec((8, DNW), lambda b, j: (0, 0))],
        compiler_params=_cp(vmem=VMEM_BIG),
    )(dx1, ao, g1, w_out, of, ob, p, onw)


FT = 256
NFT = NTOK // FT
TPE = SEQ // FT
NFF = DFF // FFT


def _ffn_specs():
    row = lambda w: pl.BlockSpec((FT, w), lambda i, f: (i, 0))
    ex = pl.BlockSpec((None, 1, D), lambda i, f: (i // TPE, 0, 0))
    one = pl.BlockSpec((1, D), lambda i, f: (0, 0))
    wg = pl.BlockSpec((None, D, FFT), lambda i, f: (f, 0, 0))
    wu = pl.BlockSpec((None, D, FFT), lambda i, f: (NFF + f, 0, 0))
    wo = pl.BlockSpec((FFT, D), lambda i, f: (f, 0))
    ff = pl.BlockSpec((FT, FFT), lambda i, f: (i, f))
    return row, ex, one, wg, wu, wo, ff


def _ffn_fwd(x1, sh2, sc2, g2, n2w, w_in4, w_out, target):
    def body(x_ref, sh_ref, sc_ref, g_ref, w2_ref, wg_ref, wu_ref, wo_ref, t_ref,
             h2_ref, gt_ref, up_ref, dy_ref, dg_ref, loss_ref, acc_ref):
        i, f = pl.program_id(0), pl.program_id(1)

        @pl.when(f == 0)
        def _():
            h2_ref[...] = _norm_mod(x_ref[...], w2_ref[...], sh_ref[...], sc_ref[...]).astype(BF16)
            acc_ref[...] = jnp.zeros_like(acc_ref)

        h2 = h2_ref[...]
        gate = jnp.dot(h2, wg_ref[...], preferred_element_type=F32)
        up = jnp.dot(h2, wu_ref[...], preferred_element_type=F32)
        gt_ref[...] = gate.astype(BF16)
        up_ref[...] = up.astype(BF16)
        acc_ref[...] += jnp.dot((_silu(gate) * up).astype(BF16), wo_ref[...], preferred_element_type=F32)

        @pl.when((i == 0) & (f == 0))
        def _():
            loss_ref[...] = jnp.zeros_like(loss_ref)

        @pl.when(f == NFF - 1)
        def _():
            ffn = acc_ref[...]
            err = x_ref[...] + g_ref[...] * ffn - t_ref[...]
            dy = err * (1.0 / D)
            dy_ref[...] = dy
            part = jnp.sum(dy * ffn, axis=0, keepdims=True)
            loss_ref[...] += jnp.sum(jnp.sum(err * err, axis=1, keepdims=True), axis=0, keepdims=True) * (0.5 / D)

            @pl.when(i % TPE == 0)
            def _():
                dg_ref[...] = part

            @pl.when(i % TPE != 0)
            def _():
                dg_ref[...] += part

    row, ex, one, wg, wu, wo, ff = _ffn_specs()
    return pl.pallas_call(
        body, name="ffn_fwd", grid=(NFT, NFF),
        out_shape=[jax.ShapeDtypeStruct((NTOK, D), BF16), jax.ShapeDtypeStruct((NTOK, DFF), BF16),
                   jax.ShapeDtypeStruct((NTOK, DFF), BF16), jax.ShapeDtypeStruct((NTOK, D), F32),
                   jax.ShapeDtypeStruct((NB, 1, D), F32), jax.ShapeDtypeStruct((8, 128), F32)],
        in_specs=[row(D), ex, ex, ex, one, wg, wu, wo, row(D)],
        out_specs=[row(D), ff, ff, row(D), ex, pl.BlockSpec((8, 128), lambda i, f: (0, 0))],
        scratch_shapes=[pltpu.VMEM((FT, D), F32)],
        compiler_params=_cp(vmem=VMEM_BIG),
    )(x1, sh2, sc2, g2, n2w, w_in4, w_in4, w_out, target)


def _ffn_bwd(dy, gate, up, x1, sh2, sc2, g2, n2w, w_in4, w_out):
    def body(dy_ref, gt_ref, up_ref, x_ref, sh_ref, sc_ref, g_ref, w2_ref, wg_ref, wu_ref, wo_ref,
             dff_ref, act_ref, dgu_ref, dx_ref, dsh_ref, dsc_ref, dw_ref, acc_ref):
        i, f = pl.program_id(0), pl.program_id(1)

        @pl.when(f == 0)
        def _():
            dff_ref[...] = (g_ref[...] * dy_ref[...]).astype(BF16)
            acc_ref[...] = jnp.zeros_like(acc_ref)

        dact = lax.dot_general(dff_ref[...], wo_ref[...], _NT, preferred_element_type=F32)
        gate, up = gt_ref[...].astype(F32), up_ref[...].astype(F32)
        sg = _sigmoid(gate)
        sl = gate * sg
        act_ref[...] = (sl * up).astype(BF16)
        dgate = (dact * up * (sg * (1.0 + gate * (1.0 - sg)))).astype(BF16)
        dup = (dact * sl).astype(BF16)
        dgu_ref[0] = dgate
        dgu_ref[1] = dup
        acc_ref[...] += (lax.dot_general(dgate, wg_ref[...], _NT, preferred_element_type=F32)
                         + lax.dot_general(dup, wu_ref[...], _NT, preferred_element_type=F32))

        @pl.when((i == 0) & (f == 0))
        def _():
            dw_ref[...] = jnp.zeros_like(dw_ref)

        @pl.when(f == NFF - 1)
        def _():
            _, vjp = jax.vjp(_norm_mod, x_ref[...], w2_ref[...], sh_ref[...], sc_ref[...])
            dx, dw, dsh, dsc = vjp(acc_ref[...])
            dx_ref[...] = dy_ref[...] + dx
            dw_ref[...] += dw

            @pl.when(i % TPE == 0)
            def _():
                dsh_ref[...] = dsh
                dsc_ref[...] = dsc

            @pl.when(i % TPE != 0)
            def _():
                dsh_ref[...] += dsh
                dsc_ref[...] += dsc

    row, ex, one, wg, wu, wo, ff = _ffn_specs()
    return pl.pallas_call(
        body, name="ffn_bwd", grid=(NFT, NFF),
        out_shape=[jax.ShapeDtypeStruct((NTOK, D), BF16), jax.ShapeDtypeStruct((NTOK, DFF), BF16),
                   jax.ShapeDtypeStruct((2, NTOK, DFF), BF16), jax.ShapeDtypeStruct((NTOK, D), F32),
                   jax.ShapeDtypeStruct((NB, 1, D), F32), jax.ShapeDtypeStruct((NB, 1, D), F32),
                   jax.ShapeDtypeStruct((1, D), F32)],
        in_specs=[row(D), ff, ff, row(D), ex, ex, ex, one, wg, wu, wo],
        out_specs=[row(D), ff, pl.BlockSpec((2, FT, FFT), lambda i, f: (0, i, f)), row(D), ex, ex, one],
        scratch_shapes=[pltpu.VMEM((FT, D), F32)],
        compiler_params=_cp(vmem=VMEM_BIG),
    )(dy, gate, up, x1, sh2, sc2, g2, n2w, w_in4, w_in4, w_out)


def _local_step(x, ctx, target, mod_x, mod_c, norm1_w, norm2_w, w_in_p, conv_w, a_log, dt_bias, out_norm_w,
                q_norm_w, k_norm_w, rpb, w_out_b, w_ffn_in4, w_ffn_out_b, scatter_early=False, gather_late=None):
    sh1, sc1, g1, sh2, sc2, g2 = [mod_x[:, i * D:(i + 1) * D].reshape(NB, 1, D) for i in range(6)]
    csh1, csc1 = mod_c[0:D].reshape(1, 1, D), mod_c[D:2 * D].reshape(1, 1, D)
    cw = jnp.concatenate([conv_w, jnp.zeros((3, 1536), F32)], axis=0)
    prm = jnp.zeros((8, 128), F32).at[0, :8].set(a_log.reshape(8)).at[1, :8].set(dt_bias.reshape(8))
    onw = jnp.tile(out_norm_w.reshape(1, 128), (1, DNH))
    qw = jnp.tile(q_norm_w.reshape(1, 64), (1, NAH))
    kw = jnp.tile(k_norm_w.reshape(1, 64), (1, NAH))
    seg = np.arange(NAW) // 64
    bd = jnp.asarray((seg[:, None] == seg[None, :]).astype(np.float32) / 64.0, dtype=BF16)
    bias = _bias_table(rpb)

    p, hm = _inproj_fwd(x, ctx, sh1, sc1, csh1, csc1, norm1_w, w_in_p)
    dq_, dk_, dv_ = _dn_prep_fwd(p, cw)
    gates = _gates_fwd(p, prm)
    res = _scan_fwd(dq_, dk_, dv_, gates, carry=gather_late[:2] if gather_late else ())
    o_f, st_f, o_b, st_b = res[:4]
    qn, kn, vb = _na_prep_fwd(p, qw, kw, bd)
    na, got = _na_fwd(qn, kn, vb, bias, carry=gather_late[2:] if gather_late else ())
    if gather_late:
        w_out_b = res[4].reshape(D, D)
        w_ffn_in4 = res[5].reshape(4, D, FFT)
        w_ffn_out_b = got[0].reshape(DFF, D)
    x1, ai, ao = _outproj_fwd(o_f, o_b, p, na, onw, x, g1, w_out_b)
    x1f = x1.reshape(NTOK, D)
    h2, gate, up, dy, dg2, loss8 = _ffn_fwd(x1f, sh2, sc2, g2, norm2_w, w_ffn_in4, w_ffn_out_b,
                                             target.reshape(NTOK, D))
    dff, act, dgu, dx1, dsh2, dsc2, dn2w = _ffn_bwd(dy, gate, up, x1f, sh2, sc2, g2, norm2_w, w_ffn_in4, w_ffn_out_b)
    g_w_ffn_out = _matmul_tn(act, dff, FFT, D, 512, "wgrad_ffn_out")
    g_w_ffn_in4 = _matmul_tn(
        h2, dgu, 512, FFT, 512, "wgrad_ffn_in", nn=2 * NFF,
        b_block=(None, 512, FFT), b_map=lambda i, j, k: (j // NFF, k, j % NFF),
        out_shape=(4, D, FFT), out_block=(None, 512, FFT), out_map=lambda i, j, k: (j, i, 0))
    dao, do, dz, dna, dg1, donw = _outproj_bwd(dx1.reshape(NB, SEQ, D), ao, g1, w_out_b, o_f, o_b, p, onw)
    g_w_out = _matmul_tn(ai.reshape(NTOK, D), dao.reshape(NTOK, D), 512, D, 512, "wgrad_out")
    dqn, dkn, dvn, dbias = _na_bwd(qn, kn, vb, bias, dna)
    dp_na, dqw, dkw = _na_prep_bwd(p, qw, kw, bd, dqn, dkn, dvn)
    early = [g_w_out.reshape(8, D // 8, D), g_w_ffn_in4.reshape(8, D // 2, FFT), g_w_ffn_out.reshape(8, DFF // 8, D)]
    res = _scan_bwd(dq_, dk_, dv_, gates, st_f, st_b, do, carry=early if scatter_early else ())
    dqf, dkf, dvf, dgf, dqb, dkb, dvb, dgb = res[:8]
    if scatter_early:
        g_w_out, g_w_ffn_in4, g_w_ffn_out = res[8:]
    dpq, dpk, dpv, gcq, gck, gcv = _dn_prep_bwd(p, cw, (dqf, dqb), (dkf, dkb), (dvf, dvb))
    dgt, dprm = _gates_bwd(p, prm, dgf, dgb)
    dpb, grad_x, dsh1, dsc1, dcsh1, dcsc1, dn1w = _inproj_bwd(
        (dpq, dpk, dpv), dz, dp_na, dgt, w_in_p, x, ctx, sh1, sc1, csh1, csc1, norm1_w, dx1.reshape(NB, SEQ, D))
    g_w_in_p = _matmul_tn(hm.reshape(NB * T, D), dpb.reshape(NB * T, PC), 256, PC, 512, "wgrad_in")
    g_rpb = _rpb_fold(dbias)

    dmod_x = jnp.concatenate([dsh1, dsc1, dg1, dsh2, dsc2, dg2], axis=2).reshape(NB, 6 * D)
    dmod_c = jnp.concatenate([dcsh1.reshape(D), dcsc1.reshape(D), jnp.zeros((4 * D,), F32)])
    g_conv = jnp.concatenate([gcq[:5], gck[:5], gcv[:5]], axis=1)
    small = dict(
        norm1_w=dn1w, norm2_w=dn2w, a_log=dprm[0, :8], dt_bias=dprm[1, :8],
        out_norm_w=donw[0].reshape(DNH, 128).sum(0), q_norm_w=dqw[0].reshape(NAH, 64).sum(0),
        k_norm_w=dkw[0].reshape(NAH, 64).sum(0), rpb=g_rpb, conv_w=g_conv)
    return loss8[0, 0], grad_x, dmod_x, dmod_c, small, g_w_in_p, g_w_out, g_w_ffn_in4, g_w_ffn_out


def _perm_w_in(w):
    return jnp.concatenate([w[:, :2048], w[:, 2064:3600], w[:, 2048:2064],
                            jnp.zeros((w.shape[0], PC - IN_COLS), w.dtype)], axis=1)


def _unperm_w_in(g):
    return jnp.concatenate([g[:, :2048], g[:, COL_G:COL_G + 16], g[:, 2048:COL_G]], axis=1)


def _pack_small(c_ctx, n1, n2, b_ada, a_log, dt_bias, onw, qnw, knw, rpb, conv_s):
    misc = jnp.zeros((D,), F32)
    misc = misc.at[0:8].set(a_log.reshape(8)).at[8:16].set(dt_bias.reshape(8))
    misc = misc.at[128:256].set(onw.reshape(128)).at[256:320].set(qnw.reshape(64)).at[320:384].set(knw.reshape(64))
    rp = jnp.concatenate([rpb.reshape(-1), jnp.zeros((4 * D - 3720,), F32)])
    cv = jnp.concatenate([conv_s.reshape(-1), jnp.zeros((2 * D - 1920,), F32)])
    return jnp.concatenate([c_ctx.reshape(-1), n1.reshape(-1), n2.reshape(-1), b_ada.reshape(-1), misc, rp, cv]).reshape(16, D)


def _unpack_small(a):
    f = a.reshape(-1)
    misc = f[9 * D:10 * D]
    return dict(
        c_ctx=f[0:D], norm1_w=f[D:2 * D].reshape(1, D), norm2_w=f[2 * D:3 * D].reshape(1, D),
        b_ada=f[3 * D:9 * D].reshape(1, 6 * D), dn_A_log=misc[0:8].reshape(1, 2, 4), dn_dt_bias=misc[8:16].reshape(1, 2, 4),
        dn_out_norm_w=misc[128:256].reshape(1, 128), na_q_norm_w=misc[256:320].reshape(1, 64),
        na_k_norm_w=misc[320:384].reshape(1, 64), na_rpb=f[10 * D:10 * D + 3720].reshape(1, 8, 15, 31),
        dn_conv_w=f[14 * D:14 * D + 1920].reshape(1, 5, 384))


_WEIGHTS = ['c_ctx', 'norm1_w', 'norm2_w', 'w_ada', 'b_ada', 'w_in', 'dn_conv_w', 'dn_A_log', 'dn_dt_bias',
            'dn_out_norm_w', 'na_q_norm_w', 'na_k_norm_w', 'na_rpb', 'w_out', 'w_ffn_in', 'w_ffn_out']


def kernel(x, c, ctx, c_ctx, norm1_w, norm2_w, w_ada, b_ada, w_in, dn_conv_w, dn_A_log, dn_dt_bias, dn_out_norm_w, na_q_norm_w, na_k_norm_w, na_rpb, w_out, w_ffn_in, w_ffn_out, loss_target, m_c_ctx, m_norm1_w, m_norm2_w, m_w_ada, m_b_ada, m_w_in, m_dn_conv_w, m_dn_A_log, m_dn_dt_bias, m_dn_out_norm_w, m_na_q_norm_w, m_na_k_norm_w, m_na_rpb, m_w_out, m_w_ffn_in, m_w_ffn_out, v_c_ctx, v_norm1_w, v_norm2_w, v_w_ada, v_b_ada, v_w_in, v_dn_conv_w, v_dn_A_log, v_dn_dt_bias, v_dn_out_norm_w, v_na_q_norm_w, v_na_k_norm_w, v_na_rpb, v_w_out, v_w_ffn_in, v_w_ffn_out):
    ix, iy, ic = lax.axis_index("x"), lax.axis_index("y"), lax.axis_index("c")
    chip = 2 * ix + iy
    me = 4 * ix + 2 * iy + ic

    crow = jnp.concatenate([c, c_ctx.reshape(1, D), jnp.zeros((5, D), F32)], axis=0)
    cvrow = jnp.concatenate([dn_conv_w[0].reshape(-1), jnp.zeros((2 * D - 1920,), F32)]).reshape(2, D)
    g0 = _allgather8(jnp.concatenate([crow, cvrow, jnp.zeros((6, D), F32)], axis=0), "gather_cond")
    cin = g0[:, :8].reshape(64, D)
    conv_full = jnp.concatenate([g0[2 * s, 8:10].reshape(-1)[:1920].reshape(5, 384) for s in range(4)], axis=1)

    w_ada_s = w_ada[0]
    b_s = lax.dynamic_slice(b_ada, (0, chip * 1536), (1, 1536))
    mod_s = _ada_fwd(cin, w_ada_s, b_s)
    mods = _allgather8(mod_s, "gather_mod")
    mod_all = jnp.concatenate([mods[0], mods[2], mods[4], mods[6]], axis=1)
    mod_mine = lax.dynamic_slice(mod_all, (8 * me, 0), (8, 6 * D))
    mod_x, mod_c = mod_mine[0:NB], mod_mine[2]

    def half(w2d):
        r = w2d.shape[0] // 2
        return lax.dynamic_slice(w2d, (ic * r, 0), (r, w2d.shape[1]))

    (gw_in,) = _gather_weights([_cast_bf16(half(w_in[0]), "cast_w_in")])
    late = [_cast_bf16(half(w_out[0]), "cast_w_out"), _cast_bf16(half(w_ffn_in[0]), "cast_w_ffn_in"),
            _cast_bf16(half(w_ffn_out[0]), "cast_w_ffn_out")]
    w_in_full = gw_in.reshape(4, D, 900).transpose(1, 0, 2).reshape(D, IN_COLS)
    w_in_p = _perm_w_in(w_in_full)

    (loss_part, grad_x, dmod_x, dmod_c, small, g_w_in_p, r_w_out, r_w_ffn_in, r_w_ffn_out) = _local_step(
        x, ctx, loss_target, mod_x, mod_c, norm1_w, norm2_w, w_in_p, conv_full, dn_A_log[0], dn_dt_bias[0],
        dn_out_norm_w[0], na_q_norm_w[0], na_k_norm_w[0], na_rpb[0], None, None, None,
        scatter_early=True, gather_late=late)
    loss = lax.psum(loss_part, ("x", "y", "c"))

    dm = jnp.concatenate([dmod_x, dmod_c.reshape(1, 6 * D), jnp.zeros((5, 6 * D), F32)], axis=0).reshape(48, D)
    misc = jnp.zeros((D,), F32)
    misc = misc.at[0:8].set(small["a_log"]).at[8:16].set(small["dt_bias"]).at[128:256].set(small["out_norm_w"])
    misc = misc.at[256:320].set(small["q_norm_w"]).at[320:384].set(small["k_norm_w"])
    sm = jnp.concatenate([
        small["norm1_w"].reshape(-1), small["norm2_w"].reshape(-1), misc,
        small["rpb"].reshape(-1), jnp.zeros((4 * D - 3720,), F32),
        small["conv_w"].reshape(-1), jnp.zeros((8 * D - 7680,), F32), jnp.zeros((D,), F32)]).reshape(16, D)
    gsl, gtot = _allgather8(jnp.concatenate([dm, sm], axis=0), "gather_small", with_sum=True)
    gmod = gsl[:, :48].reshape(64, 6 * D)
    st = gtot[48:64].reshape(-1)
    g_n1, g_n2, g_misc = st[0:D].reshape(1, D), st[D:2 * D].reshape(1, D), st[2 * D:3 * D]
    g_rpb = st[3 * D:3 * D + 3720].reshape(1, 8, 15, 31)
    g_conv_full = st[7 * D:7 * D + 7680].reshape(5, 1536)
    g_conv_s = lax.dynamic_slice(g_conv_full, (0, chip * 384), (5, 384)).reshape(1, 5, 384)

    gmod_s = lax.dynamic_slice(gmod, (0, chip * 1536), (64, 1536))
    g_w_ada_s, pctx = _ada_bwd(cin, gmod_s, w_ada_s)
    parts = _allgather8(pctx, "gather_cctx")
    g_b_ada, g_c_ctx = _ada_small(gmod, parts, c_ctx.reshape(1, D))

    g_w_in_full = _unperm_w_in(g_w_in_p)
    g_w_in8 = g_w_in_full.reshape(D, 4, 900).transpose(1, 0, 2).reshape(8, D // 2, 900)
    names = ["w_in", "w_out", "w_ffn_in", "w_ffn_out"]
    rs = list(_scatter_grads([g_w_in8])) + [r_w_out, r_w_ffn_in, r_w_ffn_out]
    full = _sibling_exchange([_sum_slots(r, "sum_" + nm) for r, nm in zip(rs, names)])
    g_w_in_s = full[0].reshape(1, D, 900)
    g_w_out_s = full[1].reshape(1, D // 4, D)
    g_w_ffn_in_s = full[2].reshape(1, D, FFT)
    g_w_ffn_out_s = full[3].reshape(1, DFF // 4, D)

    grads = dict(
        c_ctx=g_c_ctx.reshape(D), norm1_w=g_n1, norm2_w=g_n2, w_ada=g_w_ada_s.reshape(1, D, 1536), b_ada=g_b_ada,
        w_in=g_w_in_s, dn_conv_w=g_conv_s, dn_A_log=g_misc[0:8].reshape(1, 2, 4), dn_dt_bias=g_misc[8:16].reshape(1, 2, 4),
        dn_out_norm_w=g_misc[128:256].reshape(1, 128), na_q_norm_w=g_misc[256:320].reshape(1, 64),
        na_k_norm_w=g_misc[320:384].reshape(1, 64), na_rpb=g_rpb, w_out=g_w_out_s, w_ffn_in=g_w_ffn_in_s,
        w_ffn_out=g_w_ffn_out_s)
    pk = lambda d_: _pack_small(d_["c_ctx"], d_["norm1_w"], d_["norm2_w"], d_["b_ada"], d_["dn_A_log"], d_["dn_dt_bias"],
                                d_["dn_out_norm_w"], d_["na_q_norm_w"], d_["na_k_norm_w"], d_["na_rpb"], d_["dn_conv_w"])
    wts = dict(c_ctx=c_ctx, norm1_w=norm1_w, norm2_w=norm2_w, b_ada=b_ada, dn_A_log=dn_A_log, dn_dt_bias=dn_dt_bias,
               dn_out_norm_w=dn_out_norm_w, na_q_norm_w=na_q_norm_w, na_k_norm_w=na_k_norm_w, na_rpb=na_rpb, dn_conv_w=dn_conv_w)
    ms = dict(c_ctx=m_c_ctx, norm1_w=m_norm1_w, norm2_w=m_norm2_w, b_ada=m_b_ada, dn_A_log=m_dn_A_log, dn_dt_bias=m_dn_dt_bias,
              dn_out_norm_w=m_dn_out_norm_w, na_q_norm_w=m_na_q_norm_w, na_k_norm_w=m_na_k_norm_w, na_rpb=m_na_rpb,
              dn_conv_w=m_dn_conv_w)
    vs = dict(c_ctx=v_c_ctx, norm1_w=v_norm1_w, norm2_w=v_norm2_w, b_ada=v_b_ada, dn_A_log=v_dn_A_log, dn_dt_bias=v_dn_dt_bias,
              dn_out_norm_w=v_dn_out_norm_w, na_q_norm_w=v_na_q_norm_w, na_k_norm_w=v_na_k_norm_w, na_rpb=v_na_rpb,
              dn_conv_w=v_dn_conv_w)
    sd, snm, snv, _ = _adamw(pk(wts), pk(grads), pk(ms), pk(vs), "adamw_small")
    delta, new_m, new_v = _unpack_small(sd), _unpack_small(snm), _unpack_small(snv)
    big = dict(w_ada=(w_ada, m_w_ada, v_w_ada), w_in=(w_in, m_w_in, v_w_in), w_out=(w_out, m_w_out, v_w_out),
               w_ffn_in=(w_ffn_in, m_w_ffn_in, v_w_ffn_in), w_ffn_out=(w_ffn_out, m_w_ffn_out, v_w_ffn_out))
    for nm, (w_, m_, v_) in big.items():
        shp = w_.shape
        r2 = lambda a: a.reshape(shp[1], shp[2])
        d_, nm_, nv_, g_ = _adamw(r2(w_), r2(grads[nm]), r2(m_), r2(v_), "adamw_" + nm)
        delta[nm], new_m[nm], new_v[nm], grads[nm] = d_.reshape(shp), nm_.reshape(shp), nv_.reshape(shp), g_.reshape(shp)

    return (loss, grad_x, *[grads[n] for n in _WEIGHTS], *[delta[n] for n in _WEIGHTS],
            *[new_m[n] for n in _WEIGHTS], *[new_v[n] for n in _WEIGHTS])
```

```python
import functools

import numpy as np
import jax
import jax.numpy as jnp
from jax import lax
from jax.experimental import pallas as pl
from jax.experimental.pallas import tpu as pltpu

F32 = jnp.float32
BF16 = jnp.bfloat16

D = 1024
NB = 2
SEQ = 2048
CT = 256
T = CT + SEQ
NTOK = NB * SEQ
CH = 64
NCH = T // CH
NCH_CTX = CT // CH
DNH, DNW = 4, 512
NAH, NAW = 8, 512
GRID_W = 64
ROWS = SEQ // GRID_W
WIN_ROWS = 8
WIN_COLS = 16
NWIN = WIN_ROWS * GRID_W
DFF = 2816
FFT = 1408
IN_COLS = 3600
PC = 3712
COL_Z, COL_NA, COL_G = 1536, 2048, 3584
EPS = 1e-6
NEG = -1e30
RT = 256
VMEM_BIG = 56 * 1024 * 1024
MESH = pl.DeviceIdType.MESH

ADAM_LR, ADAM_B1, ADAM_B2, ADAM_EPS, ADAM_WD, ADAM_STEP = 0.001, 0.9, 0.999, 1e-08, 0.01, 10


def _cp(vmem=None, sem=None):
    kw = {}
    if vmem is not None:
        kw["vmem_limit_bytes"] = vmem
    if sem is not None:
        kw["dimension_semantics"] = sem
    return pltpu.CompilerParams(**kw)


_NN = (((1,), (0,)), ((), ()))
_NT = (((1,), (1,)), ((), ()))
_TN = (((0,), (0,)), ((), ()))


def _dg(a, b, dims):
    return lax.dot_general(a.astype(BF16), b.astype(BF16), dims, preferred_element_type=F32)


@jax.custom_vjp
def _mm(a, b):
    return _dg(a, b, _NN)


_mm.defvjp(lambda a, b: (_dg(a, b, _NN), (a, b)),
           lambda r, g: (_dg(g, r[1], _NT), _dg(r[0], g, _TN)))


@jax.custom_vjp
def _mm_nt(a, b):
    return _dg(a, b, _NT)


_mm_nt.defvjp(lambda a, b: (_dg(a, b, _NT), (a, b)),
              lambda r, g: (_dg(g, r[1], _NN), _dg(g, r[0], _TN)))


@jax.custom_vjp
def _mm_tn(a, b):
    return _dg(a, b, _TN)


_mm_tn.defvjp(lambda a, b: (_dg(a, b, _TN), (a, b)),
              lambda r, g: (_dg(r[1], g, _NT), _dg(r[0], g, _NN)))


def _split_dot(a, b, dims):
    hi = b.astype(BF16)
    lo = (b - hi.astype(F32)).astype(BF16)
    ab = a.astype(BF16)
    return (lax.dot_general(ab, hi, dims, preferred_element_type=F32)
            + lax.dot_general(ab, lo, dims, preferred_element_type=F32))


@jax.custom_vjp
def _mask_mm(mask, b):
    return _split_dot(mask, b, _NN)


_mask_mm.defvjp(lambda mask, b: (_split_dot(mask, b, _NN), mask),
                lambda mask, g: (jnp.zeros_like(mask), _split_dot(mask, g, _TN)))


def _hmm(a, b):
    return jnp.dot(a, b, precision=lax.Precision.HIGHEST, preferred_element_type=F32)


def _sigmoid(x):
    return 1.0 / (1.0 + jnp.exp(-x))


def _silu(x):
    return x * _sigmoid(x)


def _norm_mod(x, w, sh, sc):
    r = lax.rsqrt(jnp.mean(x * x, axis=-1, keepdims=True) + EPS)
    return (x * r * w) * (1.0 + sc) + sh


def _my_pos():
    return lax.axis_index("x"), lax.axis_index("y"), lax.axis_index("c")


def _flip(pos, k):
    x, y, c = pos
    return (1 - x if k & 4 else x, 1 - y if k & 2 else y, 1 - c if k & 1 else c)


def _slot(pos):
    return 4 * pos[0] + 2 * pos[1] + pos[2]


def _allgather8(v, name, with_sum=False):
    R, W = v.shape

    def body(v_ref, out_ref, *rest):
        if with_sum:
            tot_ref, send_sems, recv_sems, lsem = rest
        else:
            send_sems, recv_sems, lsem = rest
        me = _my_pos()
        mine = pltpu.make_async_copy(v_ref, out_ref.at[_slot(me)], lsem)
        mine.start()
        sends = []
        for k in range(1, 8):
            cp = pltpu.make_async_remote_copy(
                src_ref=v_ref, dst_ref=out_ref.at[_slot(me)], send_sem=send_sems.at[k - 1],
                recv_sem=recv_sems.at[k - 1], device_id=_flip(me, k), device_id_type=MESH)
            cp.start()
            sends.append(cp)
        for k in range(1, 8):
            peer = _flip(me, k)
            pltpu.make_async_remote_copy(
                src_ref=v_ref, dst_ref=out_ref.at[_slot(peer)], send_sem=send_sems.at[k - 1],
                recv_sem=recv_sems.at[k - 1], device_id=peer, device_id_type=MESH).wait_recv()
        for cp in sends:
            cp.wait_send()
        mine.wait()
        if with_sum:
            acc = out_ref[0]
            for s in range(1, 8):
                acc = acc + out_ref[s]
            tot_ref[...] = acc

    out_shape = [jax.ShapeDtypeStruct((8, R, W), F32)]
    out_specs = [pl.BlockSpec(memory_space=pltpu.VMEM)]
    if with_sum:
        out_shape.append(jax.ShapeDtypeStruct((R, W), F32))
        out_specs.append(pl.BlockSpec(memory_space=pltpu.VMEM))
    res = pl.pallas_call(
        body, name=name, out_shape=out_shape,
        in_specs=[pl.BlockSpec(memory_space=pltpu.VMEM)], out_specs=out_specs,
        scratch_shapes=[pltpu.SemaphoreType.DMA((7,)), pltpu.SemaphoreType.DMA((7,)), pltpu.SemaphoreType.DMA],
        compiler_params=_cp(vmem=VMEM_BIG),
    )(v)
    return res if with_sum else res[0]


def _gather_rows(v, name):
    W = v.shape[1]

    def body(v_ref, out_ref, send_sems, recv_sems, lsem):
        pos = _my_pos()
        me = _slot(pos)
        rows = lambda p: v_ref.at[pl.ds(pl.multiple_of(8 * p, 8), 8)]
        locals_ = [(rows(me), out_ref.at[me])]
        xfers = [(rows(_slot(_flip(pos, k))), out_ref.at[me], out_ref.at[_slot(_flip(pos, k))], _flip(pos, k))
                 for k in range(1, 8)]
        _xfer_start((locals_, xfers), (send_sems, recv_sems, lsem))
        _xfer_wait((locals_, xfers), (send_sems, recv_sems, lsem))

    vm = pl.BlockSpec(memory_space=pltpu.VMEM)
    return pl.pallas_call(
        body, name=name, out_shape=jax.ShapeDtypeStruct((8, 8, W), F32), in_specs=[vm], out_specs=vm,
        scratch_shapes=_xfer_sems(1, 7),
    )(v)


def _gather_weights(halves):
    n = len(halves)

    def body(*refs):
        ins, outs = refs[:n], refs[n:2 * n]
        send_sems, recv_sems, lsems = refs[2 * n:]
        me = _my_pos()
        x, y, c = me
        sib = (x, y, 1 - c)
        chips = [4, 2, 6]

        def cp(a, k, block, to, src=None):
            dst = outs[a].at[_slot(block)]
            return pltpu.make_async_remote_copy(
                src_ref=dst if src is None else src, dst_ref=dst, send_sem=send_sems.at[a, k],
                recv_sem=recv_sems.at[a, k], device_id=to, device_id_type=MESH)

        mine, first, passed = [], [], []
        for a in range(n):
            m = pltpu.make_async_copy(ins[a], outs[a].at[_slot(me)], lsems.at[a])
            m.start()
            mine.append(m)
            first.append(cp(a, 0, me, sib, src=ins[a]))
            for j, k in enumerate(chips):
                first.append(cp(a, 1 + j, me, _flip(me, k), src=ins[a]))
        for f in first:
            f.start()
        for j, k in enumerate(chips):
            for a in range(n):
                cp(a, 1 + j, _flip(me, k), me).wait_recv()
                p = cp(a, 4 + j, _flip(me, k), sib)
                p.start()
                passed.append(p)
        for a in range(n):
            cp(a, 0, sib, me).wait_recv()
            for j, k in enumerate(chips):
                cp(a, 4 + j, _flip(sib, k), me).wait_recv()
        for f in first + passed:
            f.wait_send()
        for m in mine:
            m.wait()

    vm = pl.BlockSpec(memory_space=pltpu.VMEM)
    return pl.pallas_call(
        body, name="gather_weights",
        out_shape=[jax.ShapeDtypeStruct((8,) + h.shape, h.dtype) for h in halves],
        in_specs=[vm] * n, out_specs=[vm] * n,
        scratch_shapes=[pltpu.SemaphoreType.DMA((n, 7)), pltpu.SemaphoreType.DMA((n, 7)),
                        pltpu.SemaphoreType.DMA((n,))],
        compiler_params=_cp(vmem=VMEM_BIG),
    )(*halves)


def _xfer_start(plan, sems):
    send_sems, recv_sems, lsems = sems
    locals_, xfers = plan
    for i, (src, dst) in enumerate(locals_):
        pltpu.make_async_copy(src, dst, lsems.at[i]).start()
    for i, (src, dst, _, peer) in enumerate(xfers):
        pltpu.make_async_remote_copy(src_ref=src, dst_ref=dst, send_sem=send_sems.at[i], recv_sem=recv_sems.at[i],
                                     device_id=peer, device_id_type=MESH).start()


def _xfer_wait(plan, sems):
    send_sems, recv_sems, lsems = sems
    locals_, xfers = plan
    for i, (src, _, landing, peer) in enumerate(xfers):
        pltpu.make_async_remote_copy(src_ref=src, dst_ref=landing, send_sem=send_sems.at[i], recv_sem=recv_sems.at[i],
                                     device_id=peer, device_id_type=MESH).wait_recv()
    for i, (src, dst, _, peer) in enumerate(xfers):
        pltpu.make_async_remote_copy(src_ref=src, dst_ref=dst, send_sem=send_sems.at[i], recv_sem=recv_sems.at[i],
                                     device_id=peer, device_id_type=MESH).wait_send()
    for i, (src, dst) in enumerate(locals_):
        pltpu.make_async_copy(src, dst, lsems.at[i]).wait()


def _xfer_sems(n_local, n_xfer):
    return [pltpu.SemaphoreType.DMA((n_xfer,)), pltpu.SemaphoreType.DMA((n_xfer,)), pltpu.SemaphoreType.DMA((n_local,))]


def _scatter_plan(ins, outs, pos):
    me = _slot(pos)
    locals_, xfers = [], []
    for src, dst in zip(ins, outs):
        locals_.append((src.at[me], dst.at[me]))
        for k in range(1, 8):
            peer = _flip(pos, k)
            xfers.append((src.at[_slot(peer)], dst.at[me], dst.at[_slot(peer)], peer))
    return locals_, xfers


def _scatter_grads(grads):
    n = len(grads)

    def body(*refs):
        plan = _scatter_plan(refs[:n], refs[n:2 * n], _my_pos())
        _xfer_start(plan, refs[2 * n:])
        _xfer_wait(plan, refs[2 * n:])

    hbm = pl.BlockSpec(memory_space=pltpu.HBM)
    return pl.pallas_call(
        body, name="scatter_grads", out_shape=[jax.ShapeDtypeStruct(g.shape, g.dtype) for g in grads],
        in_specs=[hbm] * n, out_specs=[hbm] * n, scratch_shapes=_xfer_sems(n, 7 * n),
    )(*grads)


def _sibling_exchange(halves):
    n = len(halves)

    def body(*refs):
        ins, outs = refs[:n], refs[n:2 * n]
        send_sems, recv_sems, lsems = refs[2 * n:]
        x, y, c = _my_pos()
        sib = (x, y, 1 - c)
        mine, sends = [], []
        for a in range(n):
            m = pltpu.make_async_copy(ins[a], outs[a].at[c], lsems.at[a])
            m.start()
            mine.append(m)
            cp = pltpu.make_async_remote_copy(
                src_ref=ins[a], dst_ref=outs[a].at[c], send_sem=send_sems.at[a], recv_sem=recv_sems.at[a],
                device_id=sib, device_id_type=MESH)
            cp.start()
            sends.append(cp)
        for a in range(n):
            pltpu.make_async_remote_copy(
                src_ref=ins[a], dst_ref=outs[a].at[1 - c], send_sem=send_sems.at[a], recv_sem=recv_sems.at[a],
                device_id=sib, device_id_type=MESH).wait_recv()
        for cp in sends:
            cp.wait_send()
        for m in mine:
            m.wait()

    vm = pl.BlockSpec(memory_space=pltpu.VMEM)
    return pl.pallas_call(
        body, name="sibling_exchange",
        out_shape=[jax.ShapeDtypeStruct((2,) + h.shape, h.dtype) for h in halves],
        in_specs=[vm] * n, out_specs=[vm] * n,
        scratch_shapes=[pltpu.SemaphoreType.DMA((n,)), pltpu.SemaphoreType.DMA((n,)), pltpu.SemaphoreType.DMA((n,))],
        compiler_params=_cp(vmem=VMEM_BIG),
    )(*halves)


def _row_tile(r, cap=512):
    for t in (cap, 256, 128, 64, 32, 16, 8):
        if t <= cap and r % t == 0:
            return t
    return r


def _cast_bf16(a, name):
    R, W = a.shape
    tr = _row_tile(R, 256)

    def body(a_ref, o_ref):
        o_ref[...] = a_ref[...].astype(BF16)

    return pl.pallas_call(
        body, name=name, grid=(R // tr,), out_shape=jax.ShapeDtypeStruct((R, W), BF16),
        in_specs=[pl.BlockSpec((tr, W), lambda i: (i, 0))], out_specs=pl.BlockSpec((tr, W), lambda i: (i, 0)),
    )(a)


def _sum_slots(r, name):
    _, R, W = r.shape
    tr = _row_tile(R, 128)

    def body(r_ref, o_ref):
        acc = r_ref[0].astype(F32)
        for s_ in range(1, 8):
            acc = acc + r_ref[s_].astype(F32)
        o_ref[...] = acc.astype(BF16)

    return pl.pallas_call(
        body, name=name, grid=(R // tr,), out_shape=jax.ShapeDtypeStruct((R, W), BF16),
        in_specs=[pl.BlockSpec((8, tr, W), lambda i: (0, i, 0))], out_specs=pl.BlockSpec((tr, W), lambda i: (i, 0)),
        compiler_params=_cp(vmem=VMEM_BIG),
    )(r)


def _adamw(w, g, m, v, name):
    R, W = w.shape
    tr = _row_tile(R, 256)
    tc = 256 if (tr == R and R * W * 4 > (1 << 21) and W % 256 == 0) else W
    c1 = 1.0 / (1.0 - ADAM_B1 ** ADAM_STEP)
    c2 = 1.0 / (1.0 - ADAM_B2 ** ADAM_STEP)

    def body(w_ref, g_ref, m_ref, v_ref, d_ref, nm_ref, nv_ref, g32_ref):
        gg = g_ref[...].astype(F32)
        g32_ref[...] = gg
        nm = ADAM_B1 * m_ref[...] + (1.0 - ADAM_B1) * gg
        nv = ADAM_B2 * v_ref[...] + (1.0 - ADAM_B2) * (gg * gg)
        d_ref[...] = -ADAM_LR * ((nm * c1) / (jnp.sqrt(nv * c2) + ADAM_EPS) + ADAM_WD * w_ref[...])
        nm_ref[...] = nm
        nv_ref[...] = nv

    spec = pl.BlockSpec((tr, tc), lambda i, j: (i, j))
    return pl.pallas_call(
        body, name=name, grid=(R // tr, W // tc), out_shape=[jax.ShapeDtypeStruct((R, W), F32)] * 4,
        in_specs=[spec] * 4, out_specs=[spec] * 4, compiler_params=_cp(vmem=VMEM_BIG),
    )(w, g, m, v)


def _matmul_tn(a, b, tm, tn, tk, name, nn=1, b_map=None, out_shape=None, out_block=None, out_map=None,
               b_block=None):
    K, M = a.shape
    nk = K // tk
    nm = M // tm

    def body(a_ref, b_ref, o_ref, acc_ref):
        kk = pl.program_id(2)

        @pl.when(kk == 0)
        def _():
            acc_ref[...] = jnp.zeros_like(acc_ref)

        acc_ref[...] += lax.dot_general(a_ref[...], b_ref[...], _TN, preferred_element_type=F32)

        @pl.when(kk == nk - 1)
        def _():
            o_ref[...] = acc_ref[...].astype(BF16)

    return pl.pallas_call(
        body, name=name, grid=(nm, nn, nk),
        out_shape=jax.ShapeDtypeStruct(out_shape or (M, b.shape[-1]), BF16),
        in_specs=[pl.BlockSpec((tk, tm), lambda i, j, k: (k, i)),
                  pl.BlockSpec(b_block or (tk, tn), b_map or (lambda i, j, k: (k, j)))],
        out_specs=pl.BlockSpec(out_block or (tm, tn), out_map or (lambda i, j, k: (i, j))),
        scratch_shapes=[pltpu.VMEM((tm, tn), F32)],
        compiler_params=_cp(vmem=VMEM_BIG),
    )(a, b)


def _ada_fwd(cin, w_ada_s, b_s):
    tn = 512

    def body(c_ref, w_ref, b_ref, o_ref):
        o_ref[...] = _dg(_silu(c_ref[...]), w_ref[...], _NN) + b_ref[...]

    return pl.pallas_call(
        body, name="ada_fwd", grid=(1536 // tn,), out_shape=jax.ShapeDtypeStruct((64, 1536), F32),
        in_specs=[pl.BlockSpec((64, D), lambda j: (0, 0)), pl.BlockSpec((D, tn), lambda j: (0, j)),
                  pl.BlockSpec((1, tn), lambda j: (0, j))],
        out_specs=pl.BlockSpec((64, tn), lambda j: (0, j)),
        compiler_params=_cp(vmem=VMEM_BIG),
    )(cin, w_ada_s, b_s)


def _ada_bwd(cin, gmod_s, w_ada_s):
    tn = 512

    def body(c_ref, g_ref, w_ref, gw_ref, pc_ref):
        j = pl.program_id(0)
        g = g_ref[...]
        gw_ref[...] = _dg(_silu(c_ref[...]), g, _TN)
        rr = lax.broadcasted_iota(jnp.int32, (8, 64), 0)
        cc = lax.broadcasted_iota(jnp.int32, (8, 64), 1)
        sel = jnp.where((rr == 0) & ((cc & 7) == 2), 1.0, 0.0).astype(F32)
        dctx = _hmm(sel, g)
        part = _dg(dctx, w_ref[...], _NT)

        @pl.when(j == 0)
        def _():
            pc_ref[...] = jnp.zeros_like(pc_ref)

        pc_ref[...] += part

    return pl.pallas_call(
        body, name="ada_bwd", grid=(1536 // tn,),
        out_shape=[jax.ShapeDtypeStruct((D, 1536), F32), jax.ShapeDtypeStruct((8, D), F32)],
        in_specs=[pl.BlockSpec((64, D), lambda j: (0, 0)), pl.BlockSpec((64, tn), lambda j: (0, j)),
                  pl.BlockSpec((D, tn), lambda j: (0, j))],
        out_specs=[pl.BlockSpec((D, tn), lambda j: (0, j)), pl.BlockSpec((8, D), lambda j: (0, 0))],
        compiler_params=_cp(vmem=VMEM_BIG),
    )(cin, gmod_s, w_ada_s)


def _ada_small(gmod, parts, c_ctx):
    def body(g_ref, p_ref, c_ref, gb_ref, gc_ref):
        gb_ref[...] = jnp.sum(g_ref[...], axis=0, keepdims=True)
        ds = p_ref[0][0:1] + p_ref[2][0:1] + p_ref[4][0:1] + p_ref[6][0:1]
        cc = c_ref[...]
        s = _sigmoid(cc)
        gc_ref[...] = ds * (s * (1.0 + cc * (1.0 - s)))

    vm = pl.BlockSpec(memory_space=pltpu.VMEM)
    return pl.pallas_call(
        body, name="ada_small",
        out_shape=[jax.ShapeDtypeStruct((1, 6 * D), F32), jax.ShapeDtypeStruct((1, D), F32)],
        in_specs=[vm, vm, vm], out_specs=[vm, vm],
    )(gmod, parts, c_ctx)


def _tok_specs():
    xs = pl.BlockSpec((None, RT, D), lambda b, j: (b, jnp.maximum(j - 1, 0), 0))
    cs = pl.BlockSpec((None, RT, D), lambda b, j: (b, 0, 0))
    ms = pl.BlockSpec((None, 1, D), lambda b, j: (b, 0, 0))
    cms = pl.BlockSpec((None, 1, D), lambda b, j: (0, 0, 0))
    return xs, cs, ms, cms


def _inproj_fwd(x, ctx, sh1, sc1, csh1, csc1, n1w, w_in_p):
    def body(x_ref, c_ref, sh_ref, sc_ref, csh_ref, csc_ref, w1_ref, w_ref, p_ref, hm_ref):
        is_ctx = pl.program_id(1) == 0
        xt = jnp.where(is_ctx, c_ref[...], x_ref[...])
        sh = jnp.where(is_ctx, csh_ref[...], sh_ref[...])
        sc = jnp.where(is_ctx, csc_ref[...], sc_ref[...])
        hb = _norm_mod(xt, w1_ref[...], sh, sc).astype(BF16)
        hm_ref[...] = hb
        p_ref[...] = lax.dot_general(hb, w_ref[...], _NT, preferred_element_type=F32)

    xs, cs, ms, cms = _tok_specs()
    return pl.pallas_call(
        body, name="inproj_fwd", grid=(NB, T // RT),
        out_shape=[jax.ShapeDtypeStruct((NB, T, PC), F32), jax.ShapeDtypeStruct((NB, T, D), BF16)],
        in_specs=[xs, cs, ms, ms, cms, cms, pl.BlockSpec((1, D), lambda b, j: (0, 0)),
                  pl.BlockSpec((PC, D), lambda b, j: (0, 0))],
        out_specs=[pl.BlockSpec((None, RT, PC), lambda b, j: (b, j, 0)),
                   pl.BlockSpec((None, RT, D), lambda b, j: (b, j, 0))],
        compiler_params=_cp(vmem=VMEM_BIG),
    )(x, ctx, sh1, sc1, csh1, csc1, n1w, w_in_p)


def _inproj_bwd(dp_dn, dz, dp_na, dgt, w_in_p, x, ctx, sh1, sc1, csh1, csc1, n1w, dx1):
    def body(d1q_ref, d1k_ref, d1v_ref, d2_ref, d3_ref, d4_ref, w_ref, x_ref, c_ref, sh_ref, sc_ref, csh_ref, csc_ref,
             w1_ref, dx1_ref, dpb_ref, gx_ref, dsh_ref, dsc_ref, dcsh_ref, dcsc_ref, dw1_ref):
        b, j = pl.program_id(0), pl.program_id(1)
        is_ctx = j == 0
        dp = jnp.concatenate([d1q_ref[...], d1k_ref[...], d1v_ref[...], d2_ref[...], d3_ref[...], d4_ref[...]],
                             axis=1).astype(BF16)
        dpb_ref[...] = dp
        dhm = jnp.dot(dp, w_ref[...], preferred_element_type=F32)
        xt = jnp.where(is_ctx, c_ref[...], x_ref[...])
        sh = jnp.where(is_ctx, csh_ref[...], sh_ref[...])
        sc = jnp.where(is_ctx, csc_ref[...], sc_ref[...])
        _, vjp = jax.vjp(_norm_mod, xt, w1_ref[...], sh, sc)
        dxt, dw1, dsh, dsc = vjp(dhm)

        @pl.when((b == 0) & (j == 0))
        def _():
            dw1_ref[...] = jnp.zeros_like(dw1_ref)
            dcsh_ref[...] = jnp.zeros_like(dcsh_ref)
            dcsc_ref[...] = jnp.zeros_like(dcsc_ref)

        dw1_ref[...] += dw1

        @pl.when(is_ctx)
        def _():
            dcsh_ref[...] += dsh
            dcsc_ref[...] += dsc

        @pl.when(j == 1)
        def _():
            dsh_ref[...] = jnp.zeros_like(dsh_ref)
            dsc_ref[...] = jnp.zeros_like(dsc_ref)

        @pl.when(j >= 1)
        def _():
            dsh_ref[...] += dsh
            dsc_ref[...] += dsc
            gx_ref[...] = dx1_ref[...] + dxt

    xs, cs, ms, cms = _tok_specs()
    tsp = lambda w: pl.BlockSpec((None, RT, w), lambda b, j: (b, j, 0))
    one = pl.BlockSpec((1, D), lambda b, j: (0, 0))
    return pl.pallas_call(
        body, name="inproj_bwd", grid=(NB, T // RT),
        out_shape=[jax.ShapeDtypeStruct((NB, T, PC), BF16), jax.ShapeDtypeStruct((NB, SEQ, D), F32),
                   jax.ShapeDtypeStruct((NB, 1, D), F32), jax.ShapeDtypeStruct((NB, 1, D), F32),
                   jax.ShapeDtypeStruct((1, 1, D), F32), jax.ShapeDtypeStruct((1, 1, D), F32),
                   jax.ShapeDtypeStruct((1, D), F32)],
        in_specs=[tsp(512), tsp(512), tsp(512), tsp(512), tsp(1536), tsp(128), pl.BlockSpec((PC, D), lambda b, j: (0, 0)),
                  xs, cs, ms, ms, cms, cms, one, xs],
        out_specs=[tsp(PC), xs, ms, ms, cms, cms, one],
        compiler_params=_cp(vmem=VMEM_BIG),
    )(*dp_dn, dz, dp_na, dgt, w_in_p, x, ctx, sh1, sc1, csh1, csc1, n1w, dx1)


def _seg_valid(t, off):
    lo = jnp.where(t < CT, 0, CT)
    hi = jnp.where(t < CT, CT, T)
    return (t + off >= lo) & (t + off < hi)


def _shifted(u, off):
    t = lax.broadcasted_iota(jnp.int32, (T, 1), 0)
    s = u if off == 0 else pltpu.roll(u, (-off) % T, 0)
    return jnp.where(_seg_valid(t, off), s, 0.0)


def _conv_pre(u, w):
    acc = _shifted(u, -2) * w[0:1, :]
    for kk in range(1, 5):
        acc = acc + _shifted(u, kk - 2) * w[kk:kk + 1, :]
    return acc


def _act_q(pre):
    a = _silu(pre)
    return a * lax.rsqrt(jnp.sum(a * a, axis=-1, keepdims=True) + EPS) * (128.0 ** -0.5)


def _act_k(pre):
    a = _silu(pre)
    return a * lax.rsqrt(jnp.sum(a * a, axis=-1, keepdims=True) + EPS)


def _dn_prep_fwd(p, cw):
    def body(pq_ref, pk_ref, pv_ref, wq_ref, wk_ref, wv_ref, q_ref, k_ref, v_ref):
        q_ref[...] = _act_q(_conv_pre(pq_ref[...], wq_ref[...]))
        k_ref[...] = _act_k(_conv_pre(pk_ref[...], wk_ref[...]))
        v_ref[...] = _silu(_conv_pre(pv_ref[...], wv_ref[...]))

    ps = lambda o: pl.BlockSpec((None, T, 128), lambda b, h: (b, 0, o + h))
    ws = lambda o: pl.BlockSpec((8, 128), lambda b, h: (0, o + h))
    os_ = pl.BlockSpec((None, T, 128), lambda b, h: (b, 0, h))
    return pl.pallas_call(
        body, name="dn_prep_fwd", grid=(NB, DNH), out_shape=[jax.ShapeDtypeStruct((NB, T, DNW), F32)] * 3,
        in_specs=[ps(0), ps(4), ps(8), ws(0), ws(4), ws(8)], out_specs=[os_] * 3,
        compiler_params=_cp(vmem=VMEM_BIG),
    )(p, p, p, cw, cw, cw)


def _dn_prep_bwd(p, cw, dqs, dks, dvs):
    def body(pq_ref, pk_ref, pv_ref, wq_ref, wk_ref, wv_ref, dqf_ref, dqb_ref, dkf_ref, dkb_ref, dvf_ref, dvb_ref,
             oq_ref, ok_ref, ov_ref, gq_ref, gk_ref, gv_ref):
        b = pl.program_id(1)

        def one(p_ref, w_ref, df_ref, db_ref, act, o_ref, g_ref):
            u = p_ref[...]
            w = w_ref[...]
            _, vjp = jax.vjp(act, _conv_pre(u, w))
            (dpre,) = vjp(df_ref[...] + db_ref[...])
            du = _shifted(dpre, 2) * w[0:1, :]
            rows = [jnp.sum(dpre * _shifted(u, -2), axis=0, keepdims=True)]
            for kk in range(1, 5):
                du = du + _shifted(dpre, 2 - kk) * w[kk:kk + 1, :]
                rows.append(jnp.sum(dpre * _shifted(u, kk - 2), axis=0, keepdims=True))
            o_ref[...] = du
            gw = jnp.concatenate(rows + [jnp.zeros((3, 128), F32)], axis=0)

            @pl.when(b == 0)
            def _():
                g_ref[...] = gw

            @pl.when(b > 0)
            def _():
                g_ref[...] += gw

        one(pq_ref, wq_ref, dqf_ref, dqb_ref, _act_q, oq_ref, gq_ref)
        one(pk_ref, wk_ref, dkf_ref, dkb_ref, _act_k, ok_ref, gk_ref)
        one(pv_ref, wv_ref, dvf_ref, dvb_ref, _silu, ov_ref, gv_ref)

    ps = lambda o: pl.BlockSpec((None, T, 128), lambda h, b: (b, 0, o + h))
    ws = lambda o: pl.BlockSpec((8, 128), lambda h, b: (0, o + h))
    ds = pl.BlockSpec((None, T, 128), lambda h, b: (b, 0, h))
    outs = pl.pallas_call(
        body, name="dn_prep_bwd", grid=(DNH, NB),
        out_shape=[jax.ShapeDtypeStruct((NB, T, DNW), F32)] * 3 + [jax.ShapeDtypeStruct((8, DNW), F32)] * 3,
        in_specs=[ps(0), ps(4), ps(8), ws(0), ws(4), ws(8)] + [ds] * 6,
        out_specs=[ds] * 3
        + [pl.BlockSpec((8, 128), lambda h, b: (0, h))] * 3,
        compiler_params=_cp(vmem=VMEM_BIG),
    )(p, p, p, cw, cw, cw, *dqs, *dks, *dvs)
    return outs


def _gate_fn(a, prm):
    lane = lax.broadcasted_iota(jnp.int32, a.shape, 1)
    z = a + prm[1:2, :]
    sp = jnp.maximum(z, 0.0) + jnp.log(1.0 + jnp.exp(-jnp.abs(z)))
    g = -jnp.exp(prm[0:1, :]) * sp
    return jnp.where(lane < 8, g, jnp.where(lane < 16, _sigmoid(a), 0.0))


def _gates_fwd(p, prm):
    def body(a_ref, prm_ref, o_ref):
        o_ref[...] = _gate_fn(a_ref[...], prm_ref[...])

    return pl.pallas_call(
        body, name="gates_fwd", grid=(NB,), out_shape=jax.ShapeDtypeStruct((NB, T, 128), F32),
        in_specs=[pl.BlockSpec((None, T, 128), lambda b: (b, 0, COL_G // 128)), pl.BlockSpec((8, 128), lambda b: (0, 0))],
        out_specs=pl.BlockSpec((None, T, 128), lambda b: (b, 0, 0)),
    )(p, prm)


def _gates_bwd(p, prm, dgf, dgb):
    def body(a_ref, prm_ref, dgf_ref, dgb_ref, o_ref, dprm_ref):
        b = pl.program_id(0)
        tot = dgf_ref[...] + dgb_ref[...]
        _, vjp = jax.vjp(_gate_fn, a_ref[...], prm_ref[...])
        da, dprm = vjp(tot)
        o_ref[...] = da

        @pl.when(b == 0)
        def _():
            dprm_ref[...] = dprm

        @pl.when(b > 0)
        def _():
            dprm_ref[...] += dprm

    return pl.pallas_call(
        body, name="gates_bwd", grid=(NB,),
        out_shape=[jax.ShapeDtypeStruct((NB, T, 128), F32), jax.ShapeDtypeStruct((8, 128), F32)],
        in_specs=[pl.BlockSpec((None, T, 128), lambda b: (b, 0, COL_G // 128)), pl.BlockSpec((8, 128), lambda b: (0, 0)),
                  pl.BlockSpec((None, T, 128), lambda b: (b, 0, 0)), pl.BlockSpec((None, T, 128), lambda b: (b, 0, 0))],
        out_specs=[pl.BlockSpec((None, T, 128), lambda b: (b, 0, 0)), pl.BlockSpec((8, 128), lambda b: (0, 0))],
        compiler_params=_cp(vmem=VMEM_BIG),
    )(p, prm, dgf, dgb)


def _lockstep(gens, joint=None):
    n = len(gens)
    res, vals = [None] * n, [None] * n
    while True:
        msgs = []
        for i, g in enumerate(gens):
            try:
                msgs.append((g.send(vals[i]), False))
            except StopIteration as e:
                res[i] = e.value
                msgs.append((None, True))
        if msgs[0][1]:
            return res
        if msgs[0][0] is not None:
            vals = list(joint(tuple(m[0] for m in msgs)))
        else:
            vals = [None] * n


def _tri_inv_gen(m, md, m1, m2, eye):
    n = -(m * md)
    t = eye + n
    pw = n
    for _ in range(3):
        pw = _dg(pw, pw, _NN)
        yield
        t = t + _dg(t, pw, _NN)
        yield
    a = _dg(t, m * m1, _NN)
    yield
    t = t - _dg(a, t, _NN)
    yield
    a = _dg(t, m * m2, _NN)
    yield
    return t - _dg(a, t, _NN)


def _tri_inv_bwd_gen(t, g):
    tt = t.T
    a = _dg(tt, g, _NN)
    yield
    return -_dg(a, tt, _NN)


@jax.custom_vjp
def _tri_inv(ms, md, m1, m2, eye):
    return tuple(_lockstep([_tri_inv_gen(m, md, m1, m2, eye) for m in ms]))


def _tri_inv_f(ms, md, m1, m2, eye):
    ts = _tri_inv(ms, md, m1, m2, eye)
    return ts, (ts, md)


def _tri_inv_b(res, gs):
    ts, md = res
    z = jnp.zeros_like(md)
    return (tuple(_lockstep([_tri_inv_bwd_gen(t, g) for t, g in zip(ts, gs)])), z, z, z, z)


_tri_inv.defvjp(_tri_inv_f, _tri_inv_b)


SR = DNH * CH


def _scan_masks():
    i, j = np.arange(SR)[:, None], np.arange(SR)[None, :]
    same = (i >> 6) == (j >> 6)
    cm = np.zeros((2, 6, SR, SR), np.float32)
    for d in range(2):
        diff = i - j if d == 0 else j - i
        cm[d, 0] = same & (diff >= 0)
        cm[d, 1] = same & (diff > 0)
        cm[d, 2] = (i >> 4) == (j >> 4)
        cm[d, 3] = ((i >> 5) == (j >> 5)) & ((i >> 4) != (j >> 4))
        cm[d, 4] = (i >> 5) != (j >> 5)
        cm[d, 5] = i == j
    t, u = np.arange(CH)[:, None], np.arange(CH)[None, :]
    c64 = np.stack([t >= u, t <= u]).astype(np.float32)
    bm = ((np.arange(SR)[:, None] >> 6) == (np.arange(DNW)[None, :] >> 7)).astype(np.float32)
    return jnp.asarray(cm), jnp.asarray(c64), jnp.asarray(bm)


@jax.custom_vjp
def _fold(x, bm):
    xm = x * bm
    return xm[:, 0:128] + xm[:, 128:256] + xm[:, 256:384] + xm[:, 384:512]


def _fold_f(x, bm):
    return _fold(x, bm), bm


def _fold_b(bm, g):
    return jnp.concatenate([g, g, g, g], axis=1) * bm, jnp.zeros_like(bm)


_fold.defvjp(_fold_f, _fold_b)


def _chunk_gen(q, k, v, gb, s, cm, c64, bm, d):
    incl, strict = cm[0], cm[1]
    lane = lax.broadcasted_iota(jnp.int32, (1, 128), 1)
    oh = lambda n: jnp.where(lane == n, 1.0, 0.0).astype(F32)
    col = lambda x, base: jnp.concatenate(
        [jnp.sum(x * oh(base + d * DNH + h), axis=1, keepdims=True) for h in range(DNH)], axis=0)
    beta = col(gb, 8)
    gam = col(_mask_mm(c64, gb), 0)
    yield
    tot = jnp.sum(gb, axis=0, keepdims=True)
    gtots = [jnp.sum(tot * oh(d * DNH + h), axis=1, keepdims=True) for h in range(DNH)]
    gtot = jnp.concatenate([jnp.broadcast_to(t, (CH, 1)) for t in gtots], axis=0)
    eg = jnp.concatenate([jnp.broadcast_to(jnp.exp(t), (1, 128)) for t in gtots], axis=1)
    row0 = jnp.where(lax.broadcasted_iota(jnp.int32, (128, 1), 0) == 0, 1.0, 0.0).astype(F32)
    gam_row = jnp.sum(jnp.broadcast_to(gam, (SR, 128)).T * row0, axis=0, keepdims=True)
    dec = jnp.exp(jnp.where(incl > 0.5, gam - gam_row, NEG))
    e_gam = jnp.exp(gam)
    kb = k * beta
    kk = _mm_nt(kb, k)
    yield
    tm = yield strict * (kk * dec)
    u = _mm(tm, v * beta)
    yield
    w = _mm(tm, kb * e_gam)
    yield
    qk = _mm_nt(q, k) * dec
    yield
    ws = _mm(w, s)
    yield
    v_new = u - _fold(ws, bm)
    qs = _mm(q * e_gam, s)
    yield
    o = _fold(qs, bm) + _mm(qk, v_new)
    yield
    vbd = jnp.concatenate([v_new, v_new, v_new, v_new], axis=1) * bm
    s_new = s * eg + _mm_tn(k * jnp.exp(gtot - gam), vbd)
    return o, s_new


def _chunks(args, cm_ref, c64_ref, bm):
    cm0 = cm_ref[0]
    inv = lambda ms: _tri_inv(ms, cm0[2], cm0[3], cm0[4], cm0[5])
    return _lockstep([_chunk_gen(q, k, v, gb, s, cm_ref[d], c64_ref[d], bm, d) for q, k, v, gb, s, d in args], inv)


def _chunk_rev(s):
    return jnp.where(s < NCH_CTX, NCH_CTX - 1 - s, NCH + NCH_CTX - 1 - s)


def _scan_specs(step):
    cf = lambda s: step(s)
    cb = lambda s: _chunk_rev(step(s))
    tok = lambda c: pl.BlockSpec((NB, CH, DNW), lambda s: (0, c(s), 0))
    gat = lambda c: pl.BlockSpec((NB, CH, 128), lambda s: (0, c(s), 0))
    sta = lambda c: pl.BlockSpec((NB, None, 128, DNW), lambda s: (0, c(s), 0, 0))
    return (tok(cf), gat(cf), sta(cf)), (tok(cb), gat(cb), sta(cb))


def _const_specs():
    return [pl.BlockSpec((2, 6, SR, SR), lambda s: (0, 0, 0, 0)), pl.BlockSpec((2, CH, CH), lambda s: (0, 0, 0)),
            pl.BlockSpec((SR, DNW), lambda s: (0, 0))]


SCAN_CHAINS = [(b, d) for b in range(NB) for d in range(2)]


def _stack(ref, b):
    return jnp.concatenate([ref[b, :, 128 * h:128 * h + 128] for h in range(DNH)], axis=0)


def _unstack_to(ref, b, val):
    for h in range(DNH):
        ref[b, :, 128 * h:128 * h + 128] = val[CH * h:CH * h + CH]


def _gather_plan(ins, outs, pos):
    me = _slot(pos)
    locals_, xfers = [], []
    for src, dst in zip(ins, outs):
        locals_.append((src, dst.at[me]))
        for k in range(1, 8):
            peer = _flip(pos, k)
            xfers.append((src, dst.at[me], dst.at[_slot(peer)], peer))
    return locals_, xfers


def _scan_fwd(q, k, v, gates, carry=()):
    nc = len(carry)

    def body(*refs):
        cm_ref, c64_ref, bm_ref, qf, kf, vf, gf, qb, kb, vb, gb = refs[:11]
        c_in = refs[11:11 + nc]
        of_ref, sf_ref, ob_ref, sb_ref = refs[11 + nc:15 + nc]
        c_out = refs[15 + nc:15 + 2 * nc]
        s_ref = refs[15 + 2 * nc]
        sems = refs[16 + 2 * nc:]
        if nc:
            @pl.when(pl.program_id(0) == 0)
            def _():
                _xfer_start(_gather_plan(c_in, c_out, _my_pos()), sems)

        @pl.when(pl.program_id(0) == 0)
        def _():
            s_ref[...] = jnp.zeros_like(s_ref)

        bm = bm_ref[...]
        dirs = ((qf, kf, vf, gf, of_ref, sf_ref), (qb, kb, vb, gb, ob_ref, sb_ref))
        args = [(_stack(dirs[d][0], b), _stack(dirs[d][1], b), _stack(dirs[d][2], b), dirs[d][3][b], s_ref[i], d)
                for i, (b, d) in enumerate(SCAN_CHAINS)]
        for i, (o, s_new) in enumerate(_chunks(args, cm_ref, c64_ref, bm)):
            b, d = SCAN_CHAINS[i]
            dirs[d][5][b] = args[i][4]
            _unstack_to(dirs[d][4], b, o)
            s_ref[i] = s_new

        if nc:
            @pl.when(pl.program_id(0) == NCH - 1)
            def _():
                _xfer_wait(_gather_plan(c_in, c_out, _my_pos()), sems)

    (tf, gf_, sf_), (tb, gb_, sb_) = _scan_specs(lambda s: s)
    o_sh = jax.ShapeDtypeStruct((NB, T, DNW), F32)
    st_sh = jax.ShapeDtypeStruct((NB, NCH, 128, DNW), F32)
    hbm = pl.BlockSpec(memory_space=pltpu.HBM)
    return pl.pallas_call(
        body, name="scan_fwd", grid=(NCH,),
        out_shape=[o_sh, st_sh, o_sh, st_sh] + [jax.ShapeDtypeStruct((8,) + h.shape, h.dtype) for h in carry],
        in_specs=_const_specs() + [tf, tf, tf, gf_, tb, tb, tb, gb_] + [hbm] * nc,
        out_specs=[tf, sf_, tb, sb_] + [hbm] * nc,
        scratch_shapes=[pltpu.VMEM((len(SCAN_CHAINS), 128, DNW), F32)] + (_xfer_sems(nc, 7 * nc) if nc else []),
        compiler_params=_cp(vmem=VMEM_BIG),
    )(*_scan_masks(), q, k, v, gates, q, k, v, gates, *carry)


def _scan_bwd(q, k, v, gates, st_f, st_b, do, carry=()):
    nc = len(carry)

    def body(*refs):
        (cm_ref, c64_ref, bm_ref, qf, kf, vf, gf, sf, dof, qb, kb, vb, gb, sb, dob) = refs[:15]
        c_in = refs[15:15 + nc]
        dqf, dkf, dvf, dgf, dqb, dkb, dvb, dgb = refs[15 + nc:23 + nc]
        c_out = refs[23 + nc:23 + 2 * nc]
        ds_ref = refs[23 + 2 * nc]
        sems = refs[24 + 2 * nc:]
        first = pl.program_id(0) == 0
        last = pl.program_id(0) == NCH - 1
        if nc:
            @pl.when(first)
            def _():
                _xfer_start(_scatter_plan(c_in, c_out, _my_pos()), sems)

        @pl.when(first)
        def _():
            ds_ref[...] = jnp.zeros_like(ds_ref)

        bm = bm_ref[...]
        dirs = ((qf, kf, vf, gf, sf, dof, dqf, dkf, dvf, dgf), (qb, kb, vb, gb, sb, dob, dqb, dkb, dvb, dgb))
        prim = tuple((_stack(dirs[d][0], b), _stack(dirs[d][1], b), _stack(dirs[d][2], b), dirs[d][3][b], dirs[d][4][b])
                     for b, d in SCAN_CHAINS)
        cots = tuple((_stack(dirs[d][5], b), ds_ref[i]) for i, (b, d) in enumerate(SCAN_CHAINS))
        _, vjp = jax.vjp(lambda ps: tuple(_chunks([p + (SCAN_CHAINS[i][1],) for i, p in enumerate(ps)],
                                                  cm_ref, c64_ref, bm)), prim)
        (grads,) = vjp(cots)
        for i, (dq, dk, dv, dg, ds) in enumerate(grads):
            b, d = SCAN_CHAINS[i]
            _unstack_to(dirs[d][6], b, dq)
            _unstack_to(dirs[d][7], b, dk)
            _unstack_to(dirs[d][8], b, dv)
            dirs[d][9][b] = dg
            ds_ref[i] = ds

        if nc:
            @pl.when(last)
            def _():
                _xfer_wait(_scatter_plan(c_in, c_out, _my_pos()), sems)

    (tf, gf_, sf_), (tb, gb_, sb_) = _scan_specs(lambda s: NCH - 1 - s)
    o_sh = jax.ShapeDtypeStruct((NB, T, DNW), F32)
    g_sh = jax.ShapeDtypeStruct((NB, T, 128), F32)
    hbm = pl.BlockSpec(memory_space=pltpu.HBM)
    return pl.pallas_call(
        body, name="scan_bwd", grid=(NCH,),
        out_shape=[o_sh, o_sh, o_sh, g_sh] * 2 + [jax.ShapeDtypeStruct(g.shape, g.dtype) for g in carry],
        in_specs=_const_specs() + [tf, tf, tf, gf_, sf_, tf, tb, tb, tb, gb_, sb_, tb] + [hbm] * nc,
        out_specs=[tf, tf, tf, gf_, tb, tb, tb, gb_] + [hbm] * nc,
        scratch_shapes=[pltpu.VMEM((len(SCAN_CHAINS), 128, DNW), F32)] + (_xfer_sems(nc, 7 * nc) if nc else []),
        compiler_params=_cp(vmem=VMEM_BIG),
    )(*_scan_masks(), q, k, v, gates, st_f, do, q, k, v, gates, st_b, do, *carry)


def _segmean(a, bd):
    hi = a.astype(BF16)
    lo = (a - hi.astype(F32)).astype(BF16)
    return jnp.dot(hi, bd, preferred_element_type=F32) + jnp.dot(lo, bd, preferred_element_type=F32)


def _na_prep_fwd(p, qw, kw, bd):
    def body(q_ref, k_ref, v_ref, qw_ref, kw_ref, bd_ref, oq_ref, ok_ref, ov_ref):
        bdm = bd_ref[...]
        q = q_ref[...]
        k = k_ref[...]
        oq_ref[...] = (q * lax.rsqrt(_segmean(q * q, bdm) + EPS) * qw_ref[...] * 0.125).astype(BF16)
        ok_ref[...] = (k * lax.rsqrt(_segmean(k * k, bdm) + EPS) * kw_ref[...]).astype(BF16)
        ov_ref[...] = v_ref[...].astype(BF16)

    ps = lambda o: pl.BlockSpec((None, RT, NAW), lambda b, j: (b, j, COL_NA // NAW + o))
    one = pl.BlockSpec((1, NAW), lambda b, j: (0, 0))
    os_ = pl.BlockSpec((None, RT, NAW), lambda b, j: (b, j, 0))
    return pl.pallas_call(
        body, name="na_prep_fwd", grid=(NB, T // RT), out_shape=[jax.ShapeDtypeStruct((NB, T, NAW), BF16)] * 3,
        in_specs=[ps(0), ps(1), ps(2), one, one, pl.BlockSpec((NAW, NAW), lambda b, j: (0, 0))], out_specs=[os_] * 3,
    )(p, p, p, qw, kw, bd)


def _na_prep_bwd(p, qw, kw, bd, dq, dk, dv):
    def body(q_ref, k_ref, qw_ref, kw_ref, bd_ref, dq_ref, dk_ref, dv_ref, o_ref, gq_ref, gk_ref):
        b, j = pl.program_id(0), pl.program_id(1)
        bdm = bd_ref[...]

        def one(x, w, dy, scale):
            r = lax.rsqrt(_segmean(x * x, bdm) + EPS)
            g = dy * w * scale
            dx = r * g - x * (r * r * r) * _segmean(g * x, bdm)
            dw = jnp.sum(dy * x * r * scale, axis=0, keepdims=True)
            return dx, dw

        dqv = jnp.where(j == 0, 0.0, dq_ref[...])
        dxq, dwq = one(q_ref[...], qw_ref[...], dqv, 0.125)
        dxk, dwk = one(k_ref[...], kw_ref[...], dk_ref[...], 1.0)
        o_ref[...] = jnp.concatenate([dxq, dxk, dv_ref[...]], axis=1)
        pad = jnp.zeros((7, NAW), F32)

        @pl.when((b == 0) & (j == 0))
        def _():
            gq_ref[...] = jnp.zeros_like(gq_ref)
            gk_ref[...] = jnp.zeros_like(gk_ref)

        gq_ref[...] += jnp.concatenate([dwq, pad], axis=0)
        gk_ref[...] += jnp.concatenate([dwk, pad], axis=0)

    ps = lambda o: pl.BlockSpec((None, RT, NAW), lambda b, j: (b, j, COL_NA // NAW + o))
    one = pl.BlockSpec((1, NAW), lambda b, j: (0, 0))
    ts = pl.BlockSpec((None, RT, NAW), lambda b, j: (b, j, 0))
    acc = pl.BlockSpec((8, NAW), lambda b, j: (0, 0))
    return pl.pallas_call(
        body, name="na_prep_bwd", grid=(NB, T // RT),
        out_shape=[jax.ShapeDtypeStruct((NB, T, 1536), F32), jax.ShapeDtypeStruct((8, NAW), F32),
                   jax.ShapeDtypeStruct((8, NAW), F32)],
        in_specs=[ps(0), ps(1), one, one, pl.BlockSpec((NAW, NAW), lambda b, j: (0, 0)),
                  pl.BlockSpec((None, RT, NAW), lambda b, j: (b, jnp.maximum(j - 1, 0), 0)), ts, ts],
        out_specs=[pl.BlockSpec((None, RT, 1536), lambda b, j: (b, j, 0)), acc, acc],
        compiler_params=_cp(vmem=VMEM_BIG),
    )(p, p, qw, kw, bd, dq, dk, dv)


def _na_row0(r):
    return jnp.clip(r - WIN_ROWS // 2, 0, ROWS - WIN_ROWS)


def _na_cls(r):
    return _na_row0(r) - r + WIN_ROWS - 1


NHS = 4
NLW = NHS * 64
NSR = NHS * GRID_W


def _na_stack(x):
    lane = lax.broadcasted_iota(jnp.int32, (1, NLW), 1)
    return jnp.concatenate([jnp.where((lane >> 6) == h, x, jnp.zeros_like(x)) for h in range(NHS)], axis=0)


def _na_unstack(y):
    lane = lax.broadcasted_iota(jnp.int32, (1, NLW), 1)
    out = jnp.zeros((GRID_W, NLW), y.dtype)
    for h in range(NHS):
        out = jnp.where((lane >> 6) == h, y[GRID_W * h:GRID_W * h + GRID_W], out)
    return out


def _na_probs(q4, kw, kc, bias):
    s_w = lax.dot_general(q4, kw, _NT, preferred_element_type=F32) + bias
    s_c = lax.dot_general(q4, kc, _NT, preferred_element_type=F32)
    mx = jnp.maximum(jnp.max(s_w, axis=1, keepdims=True), jnp.max(s_c, axis=1, keepdims=True))
    p_w = jnp.exp(s_w - mx)
    p_c = jnp.exp(s_c - mx)
    inv = 1.0 / (jnp.sum(p_w, axis=1, keepdims=True) + jnp.sum(p_c, axis=1, keepdims=True))
    return p_w, p_c, inv


def _na_specs():
    qs = pl.BlockSpec((None, GRID_W, NLW), lambda g, b, r: (b, CT // GRID_W + r, g))
    kv = pl.BlockSpec((None, T, NLW), lambda g, b, r: (b, 0, g))
    bs = pl.BlockSpec((None, None, NSR, NWIN), lambda g, b, r: (_na_cls(r), g, 0, 0))
    lat = pl.BlockSpec((None, GRID_W, NLW), lambda g, b, r: (b, r, g))
    return qs, kv, bs, lat


def _na_fwd(qn, kn, vb, bias, carry=()):
    nc = len(carry)
    grid = (NAH // NHS, NB, ROWS)

    def body(*refs):
        q_ref, k_ref, v_ref, b_ref = refs[:4]
        c_in, o_ref, c_out, sems = refs[4:4 + nc], refs[4 + nc], refs[5 + nc:5 + 2 * nc], refs[5 + 2 * nc:]
        ids = [pl.program_id(a) for a in range(3)]
        if nc:
            @pl.when((ids[0] == 0) & (ids[1] == 0) & (ids[2] == 0))
            def _():
                _xfer_start(_gather_plan(c_in, c_out, _my_pos()), sems)

        r = ids[2]
        start = pl.multiple_of(CT + _na_row0(r) * GRID_W, GRID_W)
        kw, vw = k_ref[pl.ds(start, NWIN), :], v_ref[pl.ds(start, NWIN), :]
        kc, vc = k_ref[0:CT, :], v_ref[0:CT, :]
        p_w, p_c, inv = _na_probs(_na_stack(q_ref[...]), kw, kc, b_ref[...])
        o4 = (jnp.dot(p_w.astype(BF16), vw, preferred_element_type=F32)
              + jnp.dot(p_c.astype(BF16), vc, preferred_element_type=F32)) * inv
        o_ref[...] = _na_unstack(o4).astype(BF16)

        if nc:
            @pl.when((ids[0] == grid[0] - 1) & (ids[1] == grid[1] - 1) & (ids[2] == grid[2] - 1))
            def _():
                _xfer_wait(_gather_plan(c_in, c_out, _my_pos()), sems)

    qs, kv, bs, lat = _na_specs()
    hbm = pl.BlockSpec(memory_space=pltpu.HBM)
    res = pl.pallas_call(
        body, name="na_fwd", grid=grid,
        out_shape=[jax.ShapeDtypeStruct((NB, SEQ, NAW), BF16)] + [jax.ShapeDtypeStruct((8,) + h.shape, h.dtype) for h in carry],
        in_specs=[qs, kv, kv, bs] + [hbm] * nc, out_specs=[lat] + [hbm] * nc,
        scratch_shapes=_xfer_sems(nc, 7 * nc) if nc else [],
        compiler_params=_cp(vmem=VMEM_BIG),
    )(qn, kn, vb, bias, *carry)
    return res[0], list(res[1:])


def _na_bwd(qn, kn, vb, bias, do):
    def body(q_ref, k_ref, v_ref, b_ref, do_ref, dq_ref, dk_ref, dv_ref, db_ref):
        b, r = pl.program_id(1), pl.program_id(2)

        @pl.when((b == 0) & (r == 0))
        def _():
            db_ref[...] = jnp.zeros_like(db_ref)

        @pl.when(r == 0)
        def _():
            dk_ref[...] = jnp.zeros_like(dk_ref)
            dv_ref[...] = jnp.zeros_like(dv_ref)

        start = pl.multiple_of(CT + _na_row0(r) * GRID_W, GRID_W)
        cls = _na_cls(r)
        kw, vw = k_ref[pl.ds(start, NWIN), :], v_ref[pl.ds(start, NWIN), :]
        kc, vc = k_ref[0:CT, :], v_ref[0:CT, :]
        q4 = _na_stack(q_ref[...])
        p_w, p_c, inv = _na_probs(q4, kw, kc, b_ref[...])
        p_w = p_w * inv
        p_c = p_c * inv
        do4 = _na_stack(do_ref[...].astype(BF16))
        dp_w = lax.dot_general(do4, vw, _NT, preferred_element_type=F32)
        dp_c = lax.dot_general(do4, vc, _NT, preferred_element_type=F32)
        delta = jnp.sum(p_w * dp_w, axis=1, keepdims=True) + jnp.sum(p_c * dp_c, axis=1, keepdims=True)
        ds_w = p_w * (dp_w - delta)
        ds_c = p_c * (dp_c - delta)
        db_ref[cls] += ds_w
        ds_wb, ds_cb = ds_w.astype(BF16), ds_c.astype(BF16)
        dq_ref[...] = _na_unstack(jnp.dot(ds_wb, kw, preferred_element_type=F32)
                                  + jnp.dot(ds_cb, kc, preferred_element_type=F32))
        dk_ref[pl.ds(start, NWIN), :] += lax.dot_general(ds_wb, q4, _TN, preferred_element_type=F32)
        dv_ref[pl.ds(start, NWIN), :] += lax.dot_general(p_w.astype(BF16), do4, _TN, preferred_element_type=F32)
        dk_ref[0:CT, :] += lax.dot_general(ds_cb, q4, _TN, preferred_element_type=F32)
        dv_ref[0:CT, :] += lax.dot_general(p_c.astype(BF16), do4, _TN, preferred_element_type=F32)

    qs, kv, bs, lat = _na_specs()
    return pl.pallas_call(
        body, name="na_bwd", grid=(NAH // NHS, NB, ROWS),
        out_shape=[jax.ShapeDtypeStruct((NB, SEQ, NAW), F32)] + [jax.ShapeDtypeStruct((NB, T, NAW), F32)] * 2
        + [jax.ShapeDtypeStruct((WIN_ROWS, NAH // NHS, NSR, NWIN), F32)],
        in_specs=[qs, kv, kv, bs, lat],
        out_specs=[lat, kv, kv, pl.BlockSpec((WIN_ROWS, None, NSR, NWIN), lambda g, b, r: (0, g, 0, 0))],
        compiler_params=_cp(vmem=VMEM_BIG),
    )(qn, kn, vb, bias, do)


def _na_tables():
    cols = np.arange(GRID_W)
    win_start = np.clip(cols - WIN_COLS // 2, 0, GRID_W - WIN_COLS)
    kc = cols[None, :]
    valid = (kc >= win_start[:, None]) & (kc < win_start[:, None] + WIN_COLS)
    rel = np.clip(kc - cols[:, None] + WIN_COLS - 1, 0, 2 * WIN_COLS - 2)
    return valid, rel


def _bias_table(rpb):
    valid, rel = _na_tables()
    t15 = jnp.where(valid[None, None], rpb[:, :, rel], NEG)
    tab = jnp.stack([t15[:, c:c + WIN_ROWS] for c in range(WIN_ROWS)], axis=0)
    return tab.transpose(0, 1, 3, 2, 4).reshape(WIN_ROWS, NAH // NHS, NSR, NWIN)


def _rpb_fold(db):
    def body1(d_ref, o_ref):
        for dr in range(2 * WIN_ROWS - 1):
            acc = None
            for c in range(WIN_ROWS):
                i = dr - c
                if 0 <= i < WIN_ROWS:
                    blk = d_ref[c, :, i * GRID_W:(i + 1) * GRID_W]
                    acc = blk if acc is None else acc + blk
            o_ref[dr] = acc

    t15 = pl.pallas_call(
        body1, name="rpb_fold_rows", grid=(NAH,),
        out_shape=jax.ShapeDtypeStruct((NAH, 2 * WIN_ROWS - 1, GRID_W, GRID_W), F32),
        in_specs=[pl.BlockSpec((WIN_ROWS, None, GRID_W, NWIN), lambda h: (0, h, 0, 0))],
        out_specs=pl.BlockSpec((None, 2 * WIN_ROWS - 1, GRID_W, GRID_W), lambda h: (h, 0, 0, 0)),
        compiler_params=_cp(vmem=VMEM_BIG),
    )(db.reshape(WIN_ROWS, NAH, GRID_W, NWIN))
    valid, rel = _na_tables()
    onehot = np.zeros((GRID_W * GRID_W, 128), np.float32)
    flat_rel, flat_valid = rel.reshape(-1), valid.reshape(-1)
    onehot[np.arange(GRID_W * GRID_W)[flat_valid], flat_rel[flat_valid]] = 1.0

    def body2(a_ref, oh_ref, o_ref):
        o_ref[...] = _hmm(a_ref[...], oh_ref[...])

    vm = pl.BlockSpec(memory_space=pltpu.VMEM)
    out = pl.pallas_call(
        body2, name="rpb_fold_cols", out_shape=jax.ShapeDtypeStruct((NAH * 15, 128), F32),
        in_specs=[vm, vm], out_specs=vm, compiler_params=_cp(vmem=VMEM_BIG),
    )(t15.reshape(NAH * 15, GRID_W * GRID_W), jnp.asarray(onehot))
    return out[:, :31].reshape(NAH, 15, 31)


def _dn_out(o, z, w):
    outs = []
    for h in range(DNH):
        sl = slice(128 * h, 128 * h + 128)
        oh = o[:, sl]
        r = lax.rsqrt(jnp.mean(oh * oh, axis=-1, keepdims=True) + EPS)
        outs.append(oh * r * w[:, sl] * _silu(z[:, sl]))
    return jnp.concatenate(outs, axis=1)


def _outproj_fwd(of, ob, p, na, onw, x, g1, w_out):
    def body(of_ref, ob_ref, z_ref, na_ref, w_ref, x_ref, g_ref, wo_ref, x1_ref, ai_ref, ao_ref):
        dn = _dn_out(of_ref[...] + ob_ref[...], z_ref[...], w_ref[...])
        ai = jnp.concatenate([dn.astype(BF16), na_ref[...]], axis=1)
        ai_ref[...] = ai
        ao = jnp.dot(ai, wo_ref[...], preferred_element_type=F32)
        ao_ref[...] = ao
        x1_ref[...] = x_ref[...] + g_ref[...] * ao

    lat = lambda w: pl.BlockSpec((None, RT, w), lambda b, j: (b, j, 0))
    return pl.pallas_call(
        body, name="outproj_fwd", grid=(NB, SEQ // RT),
        out_shape=[jax.ShapeDtypeStruct((NB, SEQ, D), F32), jax.ShapeDtypeStruct((NB, SEQ, D), BF16),
                   jax.ShapeDtypeStruct((NB, SEQ, D), F32)],
        in_specs=[pl.BlockSpec((None, RT, DNW), lambda b, j: (b, j + 1, 0)),
                  pl.BlockSpec((None, RT, DNW), lambda b, j: (b, j + 1, 0)),
                  pl.BlockSpec((None, RT, DNW), lambda b, j: (b, j + 1, COL_Z // DNW)),
                  lat(NAW), pl.BlockSpec((1, DNW), lambda b, j: (0, 0)), lat(D),
                  pl.BlockSpec((None, 1, D), lambda b, j: (b, 0, 0)), pl.BlockSpec((D, D), lambda b, j: (0, 0))],
        out_specs=[lat(D), lat(D), lat(D)],
        compiler_params=_cp(vmem=VMEM_BIG),
    )(of, ob, p, na, onw, x, g1, w_out)


def _outproj_bwd(dx1, ao, g1, w_out, of, ob, p, onw):
    def body(dx_ref, ao_ref, g_ref, wo_ref, of_ref, ob_ref, z_ref, w_ref, dao_ref, do_ref, dz_ref, dna_ref, dg_ref, dw_ref):
        b, j = pl.program_id(0), pl.program_id(1)

        @pl.when((b == 0) & (j == 0))
        def _():
            dw_ref[...] = jnp.zeros_like(dw_ref)

        @pl.when(j == 0)
        def _():
            do_ref[...] = jnp.zeros_like(do_ref)
            dz_ref[...] = jnp.zeros_like(dz_ref)

        @pl.when(j == 1)
        def _():
            dg_ref[...] = jnp.zeros_like(dg_ref)

        @pl.when(j >= 1)
        def _():
            dx = dx_ref[...]
            dg_ref[...] += jnp.sum(dx * ao_ref[...], axis=0, keepdims=True)
            dao = (g_ref[...] * dx).astype(BF16)
            dao_ref[...] = dao
            dai = lax.dot_general(dao, wo_ref[...], _NT, preferred_element_type=F32)
            dna_ref[...] = dai[:, DNW:]
            _, vjp = jax.vjp(_dn_out, of_ref[...] + ob_ref[...], z_ref[...], w_ref[...])
            do, dz, dw = vjp(dai[:, :DNW])
            do_ref[...] = do
            dz_ref[...] = dz
            dw_ref[...] += jnp.concatenate([dw, jnp.zeros((7, DNW), F32)], axis=0)

    lat = lambda w: pl.BlockSpec((None, RT, w), lambda b, j: (b, jnp.maximum(j - 1, 0), 0))
    tok = lambda w: pl.BlockSpec((None, RT, w), lambda b, j: (b, j, 0))
    return pl.pallas_call(
        body, name="outproj_bwd", grid=(NB, T // RT),
        out_shape=[jax.ShapeDtypeStruct((NB, SEQ, D), BF16), jax.ShapeDtypeStruct((NB, T, DNW), F32),
                   jax.ShapeDtypeStruct((NB, T, DNW), F32), jax.ShapeDtypeStruct((NB, SEQ, NAW), F32),
                   jax.ShapeDtypeStruct((NB, 1, D), F32), jax.ShapeDtypeStruct((8, DNW), F32)],
        in_specs=[lat(D), lat(D), pl.BlockSpec((None, 1, D), lambda b, j: (b, 0, 0)),
                  pl.BlockSpec((D, D), lambda b, j: (0, 0)),
                  tok(DNW), tok(DNW),
                  pl.BlockSpec((None, RT, DNW), lambda b, j: (b, j, COL_Z // DNW)),
                  pl.BlockSpec((1, DNW), lambda b, j: (0, 0))],
        out_specs=[lat(D), tok(DNW), tok(DNW), lat(NAW), pl.BlockSpec((None, 1, D), lambda b, j: (b, 0, 0)),
                   pl.BlockSpec((8, DNW), lambda b, j: (0, 0))],
        compiler_params=_cp(vmem=VMEM_BIG),
    )(dx1, ao, g1, w_out, of, ob, p, onw)


FT = 256
NFT = NTOK // FT
TPE = SEQ // FT
NFF = DFF // FFT


def _ffn_specs():
    row = lambda w: pl.BlockSpec((FT, w), lambda i, f: (i, 0))
    ex = pl.BlockSpec((None, 1, D), lambda i, f: (i // TPE, 0, 0))
    one = pl.BlockSpec((1, D), lambda i, f: (0, 0))
    wg = pl.BlockSpec((None, D, FFT), lambda i, f: (f, 0, 0))
    wu = pl.BlockSpec((None, D, FFT), lambda i, f: (NFF + f, 0, 0))
    wo = pl.BlockSpec((FFT, D), lambda i, f: (f, 0))
    ff = pl.BlockSpec((FT, FFT), lambda i, f: (i, f))
    return row, ex, one, wg, wu, wo, ff


def _ffn_fwd(x1, sh2, sc2, g2, n2w, w_in4, w_out, target):
    def body(x_ref, sh_ref, sc_ref, g_ref, w2_ref, wg_ref, wu_ref, wo_ref, t_ref,
             h2_ref, gt_ref, up_ref, dy_ref, dg_ref, loss_ref, acc_ref):
        i, f = pl.program_id(0), pl.program_id(1)

        @pl.when(f == 0)
        def _():
            h2_ref[...] = _norm_mod(x_ref[...], w2_ref[...], sh_ref[...], sc_ref[...]).astype(BF16)
            acc_ref[...] = jnp.zeros_like(acc_ref)

        h2 = h2_ref[...]
        gate = jnp.dot(h2, wg_ref[...], preferred_element_type=F32)
        up = jnp.dot(h2, wu_ref[...], preferred_element_type=F32)
        gt_ref[...] = gate.astype(BF16)
        up_ref[...] = up.astype(BF16)
        acc_ref[...] += jnp.dot((_silu(gate) * up).astype(BF16), wo_ref[...], preferred_element_type=F32)

        @pl.when((i == 0) & (f == 0))
        def _():
            loss_ref[...] = jnp.zeros_like(loss_ref)

        @pl.when(f == NFF - 1)
        def _():
            ffn = acc_ref[...]
            err = x_ref[...] + g_ref[...] * ffn - t_ref[...]
            dy = err * (1.0 / D)
            dy_ref[...] = dy
            part = jnp.sum(dy * ffn, axis=0, keepdims=True)
            loss_ref[...] += jnp.sum(jnp.sum(err * err, axis=1, keepdims=True), axis=0, keepdims=True) * (0.5 / D)

            @pl.when(i % TPE == 0)
            def _():
                dg_ref[...] = part

            @pl.when(i % TPE != 0)
            def _():
                dg_ref[...] += part

    row, ex, one, wg, wu, wo, ff = _ffn_specs()
    return pl.pallas_call(
        body, name="ffn_fwd", grid=(NFT, NFF),
        out_shape=[jax.ShapeDtypeStruct((NTOK, D), BF16), jax.ShapeDtypeStruct((NTOK, DFF), BF16),
                   jax.ShapeDtypeStruct((NTOK, DFF), BF16), jax.ShapeDtypeStruct((NTOK, D), F32),
                   jax.ShapeDtypeStruct((NB, 1, D), F32), jax.ShapeDtypeStruct((8, 128), F32)],
        in_specs=[row(D), ex, ex, ex, one, wg, wu, wo, row(D)],
        out_specs=[row(D), ff, ff, row(D), ex, pl.BlockSpec((8, 128), lambda i, f: (0, 0))],
        scratch_shapes=[pltpu.VMEM((FT, D), F32)],
        compiler_params=_cp(vmem=VMEM_BIG),
    )(x1, sh2, sc2, g2, n2w, w_in4, w_in4, w_out, target)


def _ffn_bwd(dy, gate, up, x1, sh2, sc2, g2, n2w, w_in4, w_out):
    def body(dy_ref, gt_ref, up_ref, x_ref, sh_ref, sc_ref, g_ref, w2_ref, wg_ref, wu_ref, wo_ref,
             dff_ref, act_ref, dgu_ref, dx_ref, dsh_ref, dsc_ref, dw_ref, acc_ref):
        i, f = pl.program_id(0), pl.program_id(1)

        @pl.when(f == 0)
        def _():
            dff_ref[...] = (g_ref[...] * dy_ref[...]).astype(BF16)
            acc_ref[...] = jnp.zeros_like(acc_ref)

        dact = lax.dot_general(dff_ref[...], wo_ref[...], _NT, preferred_element_type=F32)
        gate, up = gt_ref[...].astype(F32), up_ref[...].astype(F32)
        sg = _sigmoid(gate)
        sl = gate * sg
        act_ref[...] = (sl * up).astype(BF16)
        dgate = (dact * up * (sg * (1.0 + gate * (1.0 - sg)))).astype(BF16)
        dup = (dact * sl).astype(BF16)
        dgu_ref[0] = dgate
        dgu_ref[1] = dup
        acc_ref[...] += (lax.dot_general(dgate, wg_ref[...], _NT, preferred_element_type=F32)
                         + lax.dot_general(dup, wu_ref[...], _NT, preferred_element_type=F32))

        @pl.when((i == 0) & (f == 0))
        def _():
            dw_ref[...] = jnp.zeros_like(dw_ref)

        @pl.when(f == NFF - 1)
        def _():
            _, vjp = jax.vjp(_norm_mod, x_ref[...], w2_ref[...], sh_ref[...], sc_ref[...])
            dx, dw, dsh, dsc = vjp(acc_ref[...])
            dx_ref[...] = dy_ref[...] + dx
            dw_ref[...] += dw

            @pl.when(i % TPE == 0)
            def _():
                dsh_ref[...] = dsh
                dsc_ref[...] = dsc

            @pl.when(i % TPE != 0)
            def _():
                dsh_ref[...] += dsh
                dsc_ref[...] += dsc

    row, ex, one, wg, wu, wo, ff = _ffn_specs()
    return pl.pallas_call(
        body, name="ffn_bwd", grid=(NFT, NFF),
        out_shape=[jax.ShapeDtypeStruct((NTOK, D), BF16), jax.ShapeDtypeStruct((NTOK, DFF), BF16),
                   jax.ShapeDtypeStruct((2, NTOK, DFF), BF16), jax.ShapeDtypeStruct((NTOK, D), F32),
                   jax.ShapeDtypeStruct((NB, 1, D), F32), jax.ShapeDtypeStruct((NB, 1, D), F32),
                   jax.ShapeDtypeStruct((1, D), F32)],
        in_specs=[row(D), ff, ff, row(D), ex, ex, ex, one, wg, wu, wo],
        out_specs=[row(D), ff, pl.BlockSpec((2, FT, FFT), lambda i, f: (0, i, f)), row(D), ex, ex, one],
        scratch_shapes=[pltpu.VMEM((FT, D), F32)],
        compiler_params=_cp(vmem=VMEM_BIG),
    )(dy, gate, up, x1, sh2, sc2, g2, n2w, w_in4, w_in4, w_out)


def _local_step(x, ctx, target, mod_x, mod_c, norm1_w, norm2_w, w_in_p, conv_w, a_log, dt_bias, out_norm_w,
                q_norm_w, k_norm_w, rpb, w_out_b, w_ffn_in4, w_ffn_out_b, scatter_early=False, gather_late=None):
    sh1, sc1, g1, sh2, sc2, g2 = [mod_x[:, i * D:(i + 1) * D].reshape(NB, 1, D) for i in range(6)]
    csh1, csc1 = mod_c[0:D].reshape(1, 1, D), mod_c[D:2 * D].reshape(1, 1, D)
    cw = jnp.concatenate([conv_w, jnp.zeros((3, 1536), F32)], axis=0)
    prm = jnp.zeros((8, 128), F32).at[0, :8].set(a_log.reshape(8)).at[1, :8].set(dt_bias.reshape(8))
    onw = jnp.tile(out_norm_w.reshape(1, 128), (1, DNH))
    qw = jnp.tile(q_norm_w.reshape(1, 64), (1, NAH))
    kw = jnp.tile(k_norm_w.reshape(1, 64), (1, NAH))
    seg = np.arange(NAW) // 64
    bd = jnp.asarray((seg[:, None] == seg[None, :]).astype(np.float32) / 64.0, dtype=BF16)
    bias = _bias_table(rpb)

    p, hm = _inproj_fwd(x, ctx, sh1, sc1, csh1, csc1, norm1_w, w_in_p)
    dq_, dk_, dv_ = _dn_prep_fwd(p, cw)
    gates = _gates_fwd(p, prm)
    res = _scan_fwd(dq_, dk_, dv_, gates, carry=gather_late[:2] if gather_late else ())
    o_f, st_f, o_b, st_b = res[:4]
    qn, kn, vb = _na_prep_fwd(p, qw, kw, bd)
    na, got = _na_fwd(qn, kn, vb, bias, carry=gather_late[2:] if gather_late else ())
    if gather_late:
        w_out_b = res[4].reshape(D, D)
        w_ffn_in4 = res[5].reshape(4, D, FFT)
        w_ffn_out_b = got[0].reshape(DFF, D)
    x1, ai, ao = _outproj_fwd(o_f, o_b, p, na, onw, x, g1, w_out_b)
    x1f = x1.reshape(NTOK, D)
    h2, gate, up, dy, dg2, loss8 = _ffn_fwd(x1f, sh2, sc2, g2, norm2_w, w_ffn_in4, w_ffn_out_b,
                                             target.reshape(NTOK, D))
    dff, act, dgu, dx1, dsh2, dsc2, dn2w = _ffn_bwd(dy, gate, up, x1f, sh2, sc2, g2, norm2_w, w_ffn_in4, w_ffn_out_b)
    g_w_ffn_out = _matmul_tn(act, dff, FFT, D, 512, "wgrad_ffn_out")
    g_w_ffn_in4 = _matmul_tn(
        h2, dgu, 512, FFT, 512, "wgrad_ffn_in", nn=2 * NFF,
        b_block=(None, 512, FFT), b_map=lambda i, j, k: (j // NFF, k, j % NFF),
        out_shape=(4, D, FFT), out_block=(None, 512, FFT), out_map=lambda i, j, k: (j, i, 0))
    dao, do, dz, dna, dg1, donw = _outproj_bwd(dx1.reshape(NB, SEQ, D), ao, g1, w_out_b, o_f, o_b, p, onw)
    g_w_out = _matmul_tn(ai.reshape(NTOK, D), dao.reshape(NTOK, D), 512, D, 512, "wgrad_out")
    dqn, dkn, dvn, dbias = _na_bwd(qn, kn, vb, bias, dna)
    dp_na, dqw, dkw = _na_prep_bwd(p, qw, kw, bd, dqn, dkn, dvn)
    early = [g_w_out.reshape(8, D // 8, D), g_w_ffn_in4.reshape(8, D // 2, FFT), g_w_ffn_out.reshape(8, DFF // 8, D)]
    res = _scan_bwd(dq_, dk_, dv_, gates, st_f, st_b, do, carry=early if scatter_early else ())
    dqf, dkf, dvf, dgf, dqb, dkb, dvb, dgb = res[:8]
    if scatter_early:
        g_w_out, g_w_ffn_in4, g_w_ffn_out = res[8:]
    dpq, dpk, dpv, gcq, gck, gcv = _dn_prep_bwd(p, cw, (dqf, dqb), (dkf, dkb), (dvf, dvb))
    dgt, dprm = _gates_bwd(p, prm, dgf, dgb)
    dpb, grad_x, dsh1, dsc1, dcsh1, dcsc1, dn1w = _inproj_bwd(
        (dpq, dpk, dpv), dz, dp_na, dgt, w_in_p, x, ctx, sh1, sc1, csh1, csc1, norm1_w, dx1.reshape(NB, SEQ, D))
    g_w_in_p = _matmul_tn(dpb.reshape(NB * T, PC), hm.reshape(NB * T, D), PC, D, 512, "wgrad_in")
    g_rpb = _rpb_fold(dbias)

    dmod_x = jnp.concatenate([dsh1, dsc1, dg1, dsh2, dsc2, dg2], axis=2).reshape(NB, 6 * D)
    dmod_c = jnp.concatenate([dcsh1.reshape(D), dcsc1.reshape(D), jnp.zeros((4 * D,), F32)])
    g_conv = jnp.concatenate([gcq[:5], gck[:5], gcv[:5]], axis=1)
    small = dict(
        norm1_w=dn1w, norm2_w=dn2w, a_log=dprm[0, :8], dt_bias=dprm[1, :8],
        out_norm_w=donw[0].reshape(DNH, 128).sum(0), q_norm_w=dqw[0].reshape(NAH, 64).sum(0),
        k_norm_w=dkw[0].reshape(NAH, 64).sum(0), rpb=g_rpb, conv_w=g_conv)
    return loss8[0, 0], grad_x, dmod_x, dmod_c, small, g_w_in_p, g_w_out, g_w_ffn_in4, g_w_ffn_out


def _perm_w_in(wt):
    return jnp.concatenate([wt[:2048], wt[2064:3600], wt[2048:2064], jnp.zeros((PC - IN_COLS, wt.shape[1]), wt.dtype)],
                           axis=0)


def _unperm_w_in(gt):
    return jnp.concatenate([gt[:2048], gt[COL_G:COL_G + 16], gt[2048:COL_G]], axis=0)


def _pack_small(c_ctx, n1, n2, b_ada, a_log, dt_bias, onw, qnw, knw, rpb, conv_s):
    misc = jnp.zeros((D,), F32)
    misc = misc.at[0:8].set(a_log.reshape(8)).at[8:16].set(dt_bias.reshape(8))
    misc = misc.at[128:256].set(onw.reshape(128)).at[256:320].set(qnw.reshape(64)).at[320:384].set(knw.reshape(64))
    rp = jnp.concatenate([rpb.reshape(-1), jnp.zeros((4 * D - 3720,), F32)])
    cv = jnp.concatenate([conv_s.reshape(-1), jnp.zeros((2 * D - 1920,), F32)])
    return jnp.concatenate([c_ctx.reshape(-1), n1.reshape(-1), n2.reshape(-1), b_ada.reshape(-1), misc, rp, cv]).reshape(16, D)


def _unpack_small(a):
    f = a.reshape(-1)
    misc = f[9 * D:10 * D]
    return dict(
        c_ctx=f[0:D], norm1_w=f[D:2 * D].reshape(1, D), norm2_w=f[2 * D:3 * D].reshape(1, D),
        b_ada=f[3 * D:9 * D].reshape(1, 6 * D), dn_A_log=misc[0:8].reshape(1, 2, 4), dn_dt_bias=misc[8:16].reshape(1, 2, 4),
        dn_out_norm_w=misc[128:256].reshape(1, 128), na_q_norm_w=misc[256:320].reshape(1, 64),
        na_k_norm_w=misc[320:384].reshape(1, 64), na_rpb=f[10 * D:10 * D + 3720].reshape(1, 8, 15, 31),
        dn_conv_w=f[14 * D:14 * D + 1920].reshape(1, 5, 384))


_WEIGHTS = ['c_ctx', 'norm1_w', 'norm2_w', 'w_ada', 'b_ada', 'w_in', 'dn_conv_w', 'dn_A_log', 'dn_dt_bias',
            'dn_out_norm_w', 'na_q_norm_w', 'na_k_norm_w', 'na_rpb', 'w_out', 'w_ffn_in', 'w_ffn_out']


def kernel(x, c, ctx, c_ctx, norm1_w, norm2_w, w_ada, b_ada, w_in, dn_conv_w, dn_A_log, dn_dt_bias, dn_out_norm_w, na_q_norm_w, na_k_norm_w, na_rpb, w_out, w_ffn_in, w_ffn_out, loss_target, m_c_ctx, m_norm1_w, m_norm2_w, m_w_ada, m_b_ada, m_w_in, m_dn_conv_w, m_dn_A_log, m_dn_dt_bias, m_dn_out_norm_w, m_na_q_norm_w, m_na_k_norm_w, m_na_rpb, m_w_out, m_w_ffn_in, m_w_ffn_out, v_c_ctx, v_norm1_w, v_norm2_w, v_w_ada, v_b_ada, v_w_in, v_dn_conv_w, v_dn_A_log, v_dn_dt_bias, v_dn_out_norm_w, v_na_q_norm_w, v_na_k_norm_w, v_na_rpb, v_w_out, v_w_ffn_in, v_w_ffn_out):
    ix, iy, ic = lax.axis_index("x"), lax.axis_index("y"), lax.axis_index("c")
    chip = 2 * ix + iy
    me = 4 * ix + 2 * iy + ic

    crow = jnp.concatenate([c, c_ctx.reshape(1, D), jnp.zeros((5, D), F32)], axis=0)
    cvrow = jnp.concatenate([dn_conv_w[0].reshape(-1), jnp.zeros((2 * D - 1920,), F32)]).reshape(2, D)
    g0 = _allgather8(jnp.concatenate([crow, cvrow, jnp.zeros((6, D), F32)], axis=0), "gather_cond")
    cin = g0[:, :8].reshape(64, D)
    conv_full = jnp.concatenate([g0[2 * s, 8:10].reshape(-1)[:1920].reshape(5, 384) for s in range(4)], axis=1)

    w_ada_s = w_ada[0]
    b_s = lax.dynamic_slice(b_ada, (0, chip * 1536), (1, 1536))
    mod_s = _ada_fwd(cin, w_ada_s, b_s)
    mods = _gather_rows(mod_s, "gather_mod")
    mod_mine = jnp.concatenate([mods[0], mods[2], mods[4], mods[6]], axis=1)
    mod_x, mod_c = mod_mine[0:NB], mod_mine[2]

    def half(w2d):
        r = w2d.shape[0] // 2
        return lax.dynamic_slice(w2d, (ic * r, 0), (r, w2d.shape[1]))

    w_in_t = w_in[0].T
    (gw_in,) = _gather_weights([_cast_bf16(lax.dynamic_slice(w_in_t, (0, ic * (D // 2)), (900, D // 2)), "cast_w_in")])
    late = [_cast_bf16(half(w_out[0]), "cast_w_out"), _cast_bf16(half(w_ffn_in[0]), "cast_w_ffn_in"),
            _cast_bf16(half(w_ffn_out[0]), "cast_w_ffn_out")]
    w_in_p = _perm_w_in(gw_in.reshape(4, 2, 900, D // 2).transpose(0, 2, 1, 3).reshape(IN_COLS, D))

    (loss_part, grad_x, dmod_x, dmod_c, small, g_w_in_p, r_w_out, r_w_ffn_in, r_w_ffn_out) = _local_step(
        x, ctx, loss_target, mod_x, mod_c, norm1_w, norm2_w, w_in_p, conv_full, dn_A_log[0], dn_dt_bias[0],
        dn_out_norm_w[0], na_q_norm_w[0], na_k_norm_w[0], na_rpb[0], None, None, None,
        scatter_early=True, gather_late=late)
    loss = lax.psum(loss_part, ("x", "y", "c"))

    dm = jnp.concatenate([dmod_x, dmod_c.reshape(1, 6 * D), jnp.zeros((5, 6 * D), F32)], axis=0).reshape(48, D)
    misc = jnp.zeros((D,), F32)
    misc = misc.at[0:8].set(small["a_log"]).at[8:16].set(small["dt_bias"]).at[128:256].set(small["out_norm_w"])
    misc = misc.at[256:320].set(small["q_norm_w"]).at[320:384].set(small["k_norm_w"])
    sm = jnp.concatenate([
        small["norm1_w"].reshape(-1), small["norm2_w"].reshape(-1), misc,
        small["rpb"].reshape(-1), jnp.zeros((4 * D - 3720,), F32),
        small["conv_w"].reshape(-1), jnp.zeros((8 * D - 7680,), F32), jnp.zeros((D,), F32)]).reshape(16, D)
    gsl, gtot = _allgather8(jnp.concatenate([dm, sm], axis=0), "gather_small", with_sum=True)
    gmod = gsl[:, :48].reshape(64, 6 * D)
    st = gtot[48:64].reshape(-1)
    g_n1, g_n2, g_misc = st[0:D].reshape(1, D), st[D:2 * D].reshape(1, D), st[2 * D:3 * D]
    g_rpb = st[3 * D:3 * D + 3720].reshape(1, 8, 15, 31)
    g_conv_full = st[7 * D:7 * D + 7680].reshape(5, 1536)
    g_conv_s = lax.dynamic_slice(g_conv_full, (0, chip * 384), (5, 384)).reshape(1, 5, 384)

    gmod_s = lax.dynamic_slice(gmod, (0, chip * 1536), (64, 1536))
    g_w_ada_s, pctx = _ada_bwd(cin, gmod_s, w_ada_s)
    parts = _allgather8(pctx, "gather_cctx")
    g_b_ada, g_c_ctx = _ada_small(gmod, parts, c_ctx.reshape(1, D))

    g_w_in8 = _unperm_w_in(g_w_in_p).reshape(4, 900, 2, D // 2).transpose(0, 2, 1, 3).reshape(8, 900, D // 2)
    names = ["w_in", "w_out", "w_ffn_in", "w_ffn_out"]
    rs = list(_scatter_grads([g_w_in8])) + [r_w_out, r_w_ffn_in, r_w_ffn_out]
    full = _sibling_exchange([_sum_slots(r, "sum_" + nm) for r, nm in zip(rs, names)])
    g_w_in_t = full[0].transpose(1, 0, 2).reshape(900, D)
    g_w_out_s = full[1].reshape(1, D // 4, D)
    g_w_ffn_in_s = full[2].reshape(1, D, FFT)
    g_w_ffn_out_s = full[3].reshape(1, DFF // 4, D)

    grads = dict(
        c_ctx=g_c_ctx.reshape(D), norm1_w=g_n1, norm2_w=g_n2, w_ada=g_w_ada_s.reshape(1, D, 1536), b_ada=g_b_ada,
        w_in=g_w_in_t, dn_conv_w=g_conv_s, dn_A_log=g_misc[0:8].reshape(1, 2, 4), dn_dt_bias=g_misc[8:16].reshape(1, 2, 4),
        dn_out_norm_w=g_misc[128:256].reshape(1, 128), na_q_norm_w=g_misc[256:320].reshape(1, 64),
        na_k_norm_w=g_misc[320:384].reshape(1, 64), na_rpb=g_rpb, w_out=g_w_out_s, w_ffn_in=g_w_ffn_in_s,
        w_ffn_out=g_w_ffn_out_s)
    pk = lambda d_: _pack_small(d_["c_ctx"], d_["norm1_w"], d_["norm2_w"], d_["b_ada"], d_["dn_A_log"], d_["dn_dt_bias"],
                                d_["dn_out_norm_w"], d_["na_q_norm_w"], d_["na_k_norm_w"], d_["na_rpb"], d_["dn_conv_w"])
    wts = dict(c_ctx=c_ctx, norm1_w=norm1_w, norm2_w=norm2_w, b_ada=b_ada, dn_A_log=dn_A_log, dn_dt_bias=dn_dt_bias,
               dn_out_norm_w=dn_out_norm_w, na_q_norm_w=na_q_norm_w, na_k_norm_w=na_k_norm_w, na_rpb=na_rpb, dn_conv_w=dn_conv_w)
    ms = dict(c_ctx=m_c_ctx, norm1_w=m_norm1_w, norm2_w=m_norm2_w, b_ada=m_b_ada, dn_A_log=m_dn_A_log, dn_dt_bias=m_dn_dt_bias,
              dn_out_norm_w=m_dn_out_norm_w, na_q_norm_w=m_na_q_norm_w, na_k_norm_w=m_na_k_norm_w, na_rpb=m_na_rpb,
              dn_conv_w=m_dn_conv_w)
    vs = dict(c_ctx=v_c_ctx, norm1_w=v_norm1_w, norm2_w=v_norm2_w, b_ada=v_b_ada, dn_A_log=v_dn_A_log, dn_dt_bias=v_dn_dt_bias,
              dn_out_norm_w=v_dn_out_norm_w, na_q_norm_w=v_na_q_norm_w, na_k_norm_w=v_na_k_norm_w, na_rpb=v_na_rpb,
              dn_conv_w=v_dn_conv_w)
    sd, snm, snv, _ = _adamw(pk(wts), pk(grads), pk(ms), pk(vs), "adamw_small")
    delta, new_m, new_v = _unpack_small(sd), _unpack_small(snm), _unpack_small(snv)
    big = dict(w_ada=(w_ada, m_w_ada, v_w_ada), w_in=(w_in, m_w_in, v_w_in), w_out=(w_out, m_w_out, v_w_out),
               w_ffn_in=(w_ffn_in, m_w_ffn_in, v_w_ffn_in), w_ffn_out=(w_ffn_out, m_w_ffn_out, v_w_ffn_out))
    for nm, (w_, m_, v_) in big.items():
        shp = w_.shape
        if nm == "w_in":
            r2, back = (lambda a: a[0].T), (lambda a: a.T.reshape(shp))
            res = _adamw(w_in_t, grads[nm], r2(m_), r2(v_), "adamw_" + nm)
        else:
            r2, back = (lambda a: a.reshape(shp[1], shp[2])), (lambda a: a.reshape(shp))
            res = _adamw(r2(w_), r2(grads[nm]), r2(m_), r2(v_), "adamw_" + nm)
        delta[nm], new_m[nm], new_v[nm], grads[nm] = [back(a) for a in res]

    return (loss, grad_x, *[grads[n] for n in _WEIGHTS], *[delta[n] for n in _WEIGHTS],
            *[new_m[n] for n in _WEIGHTS], *[new_v[n] for n in _WEIGHTS])
```

```python
import functools

import numpy as np
import jax
import jax.numpy as jnp
from jax import lax
from jax.experimental import pallas as pl
from jax.experimental.pallas import tpu as pltpu

F32 = jnp.float32
BF16 = jnp.bfloat16

D = 1024
NB = 2
SEQ = 2048
CT = 256
T = CT + SEQ
NTOK = NB * SEQ
CH = 64
NCH = T // CH
NCH_CTX = CT // CH
DNH, DNW = 4, 512
NAH, NAW = 8, 512
GRID_W = 64
ROWS = SEQ // GRID_W
WIN_ROWS = 8
WIN_COLS = 16
NWIN = WIN_ROWS * GRID_W
DFF = 2816
FFT = 1408
IN_COLS = 3600
PC = 3712
COL_Z, COL_NA, COL_G = 1536, 2048, 3584
EPS = 1e-6
NEG = -1e30
RT = 256
VMEM_BIG = 56 * 1024 * 1024
MESH = pl.DeviceIdType.MESH

ADAM_LR, ADAM_B1, ADAM_B2, ADAM_EPS, ADAM_WD, ADAM_STEP = 0.001, 0.9, 0.999, 1e-08, 0.01, 10


def _cp(vmem=None, sem=None):
    kw = {}
    if vmem is not None:
        kw["vmem_limit_bytes"] = vmem
    if sem is not None:
        kw["dimension_semantics"] = sem
    return pltpu.CompilerParams(**kw)


_NN = (((1,), (0,)), ((), ()))
_NT = (((1,), (1,)), ((), ()))
_TN = (((0,), (0,)), ((), ()))


def _dg(a, b, dims):
    return lax.dot_general(a.astype(BF16), b.astype(BF16), dims, preferred_element_type=F32)


@jax.custom_vjp
def _mm(a, b):
    return _dg(a, b, _NN)


_mm.defvjp(lambda a, b: (_dg(a, b, _NN), (a, b)),
           lambda r, g: (_dg(g, r[1], _NT), _dg(r[0], g, _TN)))


@jax.custom_vjp
def _mm_nt(a, b):
    return _dg(a, b, _NT)


_mm_nt.defvjp(lambda a, b: (_dg(a, b, _NT), (a, b)),
              lambda r, g: (_dg(g, r[1], _NN), _dg(g, r[0], _TN)))


@jax.custom_vjp
def _mm_tn(a, b):
    return _dg(a, b, _TN)


_mm_tn.defvjp(lambda a, b: (_dg(a, b, _TN), (a, b)),
              lambda r, g: (_dg(r[1], g, _NT), _dg(r[0], g, _NN)))


def _split_dot(a, b, dims):
    hi = b.astype(BF16)
    lo = (b - hi.astype(F32)).astype(BF16)
    ab = a.astype(BF16)
    return (lax.dot_general(ab, hi, dims, preferred_element_type=F32)
            + lax.dot_general(ab, lo, dims, preferred_element_type=F32))


@jax.custom_vjp
def _mask_mm(mask, b):
    return _split_dot(mask, b, _NN)


_mask_mm.defvjp(lambda mask, b: (_split_dot(mask, b, _NN), mask),
                lambda mask, g: (jnp.zeros_like(mask), _split_dot(mask, g, _TN)))


def _hmm(a, b):
    return jnp.dot(a, b, precision=lax.Precision.HIGHEST, preferred_element_type=F32)


def _sigmoid(x):
    return 1.0 / (1.0 + jnp.exp(-x))


def _silu(x):
    return x * _sigmoid(x)


def _norm_mod(x, w, sh, sc):
    r = lax.rsqrt(jnp.mean(x * x, axis=-1, keepdims=True) + EPS)
    return (x * r * w) * (1.0 + sc) + sh


def _my_pos():
    return lax.axis_index("x"), lax.axis_index("y"), lax.axis_index("c")


def _flip(pos, k):
    x, y, c = pos
    return (1 - x if k & 4 else x, 1 - y if k & 2 else y, 1 - c if k & 1 else c)


def _slot(pos):
    return 4 * pos[0] + 2 * pos[1] + pos[2]


def _allgather8(v, name, with_sum=False):
    R, W = v.shape

    def body(v_ref, out_ref, *rest):
        if with_sum:
            tot_ref, send_sems, recv_sems, lsem = rest
        else:
            send_sems, recv_sems, lsem = rest
        me = _my_pos()
        mine = pltpu.make_async_copy(v_ref, out_ref.at[_slot(me)], lsem)
        mine.start()
        sends = []
        for k in range(1, 8):
            cp = pltpu.make_async_remote_copy(
                src_ref=v_ref, dst_ref=out_ref.at[_slot(me)], send_sem=send_sems.at[k - 1],
                recv_sem=recv_sems.at[k - 1], device_id=_flip(me, k), device_id_type=MESH)
            cp.start()
            sends.append(cp)
        for k in range(1, 8):
            peer = _flip(me, k)
            pltpu.make_async_remote_copy(
                src_ref=v_ref, dst_ref=out_ref.at[_slot(peer)], send_sem=send_sems.at[k - 1],
                recv_sem=recv_sems.at[k - 1], device_id=peer, device_id_type=MESH).wait_recv()
        for cp in sends:
            cp.wait_send()
        mine.wait()
        if with_sum:
            acc = out_ref[0]
            for s in range(1, 8):
                acc = acc + out_ref[s]
            tot_ref[...] = acc

    out_shape = [jax.ShapeDtypeStruct((8, R, W), F32)]
    out_specs = [pl.BlockSpec(memory_space=pltpu.VMEM)]
    if with_sum:
        out_shape.append(jax.ShapeDtypeStruct((R, W), F32))
        out_specs.append(pl.BlockSpec(memory_space=pltpu.VMEM))
    res = pl.pallas_call(
        body, name=name, out_shape=out_shape,
        in_specs=[pl.BlockSpec(memory_space=pltpu.VMEM)], out_specs=out_specs,
        scratch_shapes=[pltpu.SemaphoreType.DMA((7,)), pltpu.SemaphoreType.DMA((7,)), pltpu.SemaphoreType.DMA],
        compiler_params=_cp(vmem=VMEM_BIG),
    )(v)
    return res if with_sum else res[0]


def _gather_rows(v, name):
    W = v.shape[1]

    def body(v_ref, out_ref, send_sems, recv_sems, lsem):
        pos = _my_pos()
        me = _slot(pos)
        rows = lambda p: v_ref.at[pl.ds(pl.multiple_of(8 * p, 8), 8)]
        locals_ = [(rows(me), out_ref.at[me])]
        xfers = [(rows(_slot(_flip(pos, k))), out_ref.at[me], out_ref.at[_slot(_flip(pos, k))], _flip(pos, k))
                 for k in range(1, 8)]
        _xfer_start((locals_, xfers), (send_sems, recv_sems, lsem))
        _xfer_wait((locals_, xfers), (send_sems, recv_sems, lsem))

    vm = pl.BlockSpec(memory_space=pltpu.VMEM)
    return pl.pallas_call(
        body, name=name, out_shape=jax.ShapeDtypeStruct((8, 8, W), F32), in_specs=[vm], out_specs=vm,
        scratch_shapes=_xfer_sems(1, 7),
    )(v)


def _gather_weights(halves):
    n = len(halves)

    def body(*refs):
        ins, outs = refs[:n], refs[n:2 * n]
        send_sems, recv_sems, lsems = refs[2 * n:]
        me = _my_pos()
        x, y, c = me
        sib = (x, y, 1 - c)
        chips = [4, 2, 6]

        def cp(a, k, block, to, src=None):
            dst = outs[a].at[_slot(block)]
            return pltpu.make_async_remote_copy(
                src_ref=dst if src is None else src, dst_ref=dst, send_sem=send_sems.at[a, k],
                recv_sem=recv_sems.at[a, k], device_id=to, device_id_type=MESH)

        mine, first, passed = [], [], []
        for a in range(n):
            m = pltpu.make_async_copy(ins[a], outs[a].at[_slot(me)], lsems.at[a])
            m.start()
            mine.append(m)
            first.append(cp(a, 0, me, sib, src=ins[a]))
            for j, k in enumerate(chips):
                first.append(cp(a, 1 + j, me, _flip(me, k), src=ins[a]))
        for f in first:
            f.start()
        for j, k in enumerate(chips):
            for a in range(n):
                cp(a, 1 + j, _flip(me, k), me).wait_recv()
                p = cp(a, 4 + j, _flip(me, k), sib)
                p.start()
                passed.append(p)
        for a in range(n):
            cp(a, 0, sib, me).wait_recv()
            for j, k in enumerate(chips):
                cp(a, 4 + j, _flip(sib, k), me).wait_recv()
        for f in first + passed:
            f.wait_send()
        for m in mine:
            m.wait()

    vm = pl.BlockSpec(memory_space=pltpu.VMEM)
    return pl.pallas_call(
        body, name="gather_weights",
        out_shape=[jax.ShapeDtypeStruct((8,) + h.shape, h.dtype) for h in halves],
        in_specs=[vm] * n, out_specs=[vm] * n,
        scratch_shapes=[pltpu.SemaphoreType.DMA((n, 7)), pltpu.SemaphoreType.DMA((n, 7)),
                        pltpu.SemaphoreType.DMA((n,))],
        compiler_params=_cp(vmem=VMEM_BIG),
    )(*halves)


def _xfer_start(plan, sems):
    send_sems, recv_sems, lsems = sems
    locals_, xfers = plan
    for i, (src, dst) in enumerate(locals_):
        pltpu.make_async_copy(src, dst, lsems.at[i]).start()
    for i, (src, dst, _, peer) in enumerate(xfers):
        pltpu.make_async_remote_copy(src_ref=src, dst_ref=dst, send_sem=send_sems.at[i], recv_sem=recv_sems.at[i],
                                     device_id=peer, device_id_type=MESH).start()


def _xfer_wait(plan, sems):
    send_sems, recv_sems, lsems = sems
    locals_, xfers = plan
    for i, (src, _, landing, peer) in enumerate(xfers):
        pltpu.make_async_remote_copy(src_ref=src, dst_ref=landing, send_sem=send_sems.at[i], recv_sem=recv_sems.at[i],
                                     device_id=peer, device_id_type=MESH).wait_recv()
    for i, (src, dst, _, peer) in enumerate(xfers):
        pltpu.make_async_remote_copy(src_ref=src, dst_ref=dst, send_sem=send_sems.at[i], recv_sem=recv_sems.at[i],
                                     device_id=peer, device_id_type=MESH).wait_send()
    for i, (src, dst) in enumerate(locals_):
        pltpu.make_async_copy(src, dst, lsems.at[i]).wait()


def _xfer_sems(n_local, n_xfer):
    return [pltpu.SemaphoreType.DMA((n_xfer,)), pltpu.SemaphoreType.DMA((n_xfer,)), pltpu.SemaphoreType.DMA((n_local,))]


def _scatter_plan(ins, outs, pos):
    me = _slot(pos)
    locals_, xfers = [], []
    for src, dst in zip(ins, outs):
        locals_.append((src.at[me], dst.at[me]))
        for k in range(1, 8):
            peer = _flip(pos, k)
            xfers.append((src.at[_slot(peer)], dst.at[me], dst.at[_slot(peer)], peer))
    return locals_, xfers


def _scatter_grads(grads):
    n = len(grads)

    def body(*refs):
        plan = _scatter_plan(refs[:n], refs[n:2 * n], _my_pos())
        _xfer_start(plan, refs[2 * n:])
        _xfer_wait(plan, refs[2 * n:])

    hbm = pl.BlockSpec(memory_space=pltpu.HBM)
    return pl.pallas_call(
        body, name="scatter_grads", out_shape=[jax.ShapeDtypeStruct(g.shape, g.dtype) for g in grads],
        in_specs=[hbm] * n, out_specs=[hbm] * n, scratch_shapes=_xfer_sems(n, 7 * n),
    )(*grads)


def _sibling_exchange(halves):
    n = len(halves)

    def body(*refs):
        ins, outs = refs[:n], refs[n:2 * n]
        send_sems, recv_sems, lsems = refs[2 * n:]
        x, y, c = _my_pos()
        sib = (x, y, 1 - c)
        mine, sends = [], []
        for a in range(n):
            m = pltpu.make_async_copy(ins[a], outs[a].at[c], lsems.at[a])
            m.start()
            mine.append(m)
            cp = pltpu.make_async_remote_copy(
                src_ref=ins[a], dst_ref=outs[a].at[c], send_sem=send_sems.at[a], recv_sem=recv_sems.at[a],
                device_id=sib, device_id_type=MESH)
            cp.start()
            sends.append(cp)
        for a in range(n):
            pltpu.make_async_remote_copy(
                src_ref=ins[a], dst_ref=outs[a].at[1 - c], send_sem=send_sems.at[a], recv_sem=recv_sems.at[a],
                device_id=sib, device_id_type=MESH).wait_recv()
        for cp in sends:
            cp.wait_send()
        for m in mine:
            m.wait()

    vm = pl.BlockSpec(memory_space=pltpu.VMEM)
    return pl.pallas_call(
        body, name="sibling_exchange",
        out_shape=[jax.ShapeDtypeStruct((2,) + h.shape, h.dtype) for h in halves],
        in_specs=[vm] * n, out_specs=[vm] * n,
        scratch_shapes=[pltpu.SemaphoreType.DMA((n,)), pltpu.SemaphoreType.DMA((n,)), pltpu.SemaphoreType.DMA((n,))],
        compiler_params=_cp(vmem=VMEM_BIG),
    )(*halves)


def _row_tile(r, cap=512):
    for t in (cap, 256, 128, 64, 32, 16, 8):
        if t <= cap and r % t == 0:
            return t
    return r


def _cast_bf16(a, name):
    R, W = a.shape
    tr = _row_tile(R, 256)

    def body(a_ref, o_ref):
        o_ref[...] = a_ref[...].astype(BF16)

    return pl.pallas_call(
        body, name=name, grid=(R // tr,), out_shape=jax.ShapeDtypeStruct((R, W), BF16),
        in_specs=[pl.BlockSpec((tr, W), lambda i: (i, 0))], out_specs=pl.BlockSpec((tr, W), lambda i: (i, 0)),
    )(a)


def _sum_slots(r, name):
    _, R, W = r.shape
    tr = _row_tile(R, 128)

    def body(r_ref, o_ref):
        acc = r_ref[0].astype(F32)
        for s_ in range(1, 8):
            acc = acc + r_ref[s_].astype(F32)
        o_ref[...] = acc.astype(BF16)

    return pl.pallas_call(
        body, name=name, grid=(R // tr,), out_shape=jax.ShapeDtypeStruct((R, W), BF16),
        in_specs=[pl.BlockSpec((8, tr, W), lambda i: (0, i, 0))], out_specs=pl.BlockSpec((tr, W), lambda i: (i, 0)),
        compiler_params=_cp(vmem=VMEM_BIG),
    )(r)


def _adamw(w, g, m, v, name):
    R, W = w.shape
    tr = _row_tile(R, 256)
    tc = 256 if (tr == R and R * W * 4 > (1 << 21) and W % 256 == 0) else W
    c1 = 1.0 / (1.0 - ADAM_B1 ** ADAM_STEP)
    c2 = 1.0 / (1.0 - ADAM_B2 ** ADAM_STEP)

    def body(w_ref, g_ref, m_ref, v_ref, d_ref, nm_ref, nv_ref, g32_ref):
        gg = g_ref[...].astype(F32)
        g32_ref[...] = gg
        nm = ADAM_B1 * m_ref[...] + (1.0 - ADAM_B1) * gg
        nv = ADAM_B2 * v_ref[...] + (1.0 - ADAM_B2) * (gg * gg)
        d_ref[...] = -ADAM_LR * ((nm * c1) / (jnp.sqrt(nv * c2) + ADAM_EPS) + ADAM_WD * w_ref[...])
        nm_ref[...] = nm
        nv_ref[...] = nv

    spec = pl.BlockSpec((tr, tc), lambda i, j: (i, j))
    return pl.pallas_call(
        body, name=name, grid=(R // tr, W // tc), out_shape=[jax.ShapeDtypeStruct((R, W), F32)] * 4,
        in_specs=[spec] * 4, out_specs=[spec] * 4, compiler_params=_cp(vmem=VMEM_BIG),
    )(w, g, m, v)


def _matmul_tn(a, b, tm, tn, tk, name, nn=1, b_map=None, out_shape=None, out_block=None, out_map=None,
               b_block=None):
    K, M = a.shape
    nk = K // tk
    nm = M // tm

    def body(a_ref, b_ref, o_ref, acc_ref):
        kk = pl.program_id(2)

        @pl.when(kk == 0)
        def _():
            acc_ref[...] = jnp.zeros_like(acc_ref)

        acc_ref[...] += lax.dot_general(a_ref[...], b_ref[...], _TN, preferred_element_type=F32)

        @pl.when(kk == nk - 1)
        def _():
            o_ref[...] = acc_ref[...].astype(BF16)

    return pl.pallas_call(
        body, name=name, grid=(nm, nn, nk),
        out_shape=jax.ShapeDtypeStruct(out_shape or (M, b.shape[-1]), BF16),
        in_specs=[pl.BlockSpec((tk, tm), lambda i, j, k: (k, i)),
                  pl.BlockSpec(b_block or (tk, tn), b_map or (lambda i, j, k: (k, j)))],
        out_specs=pl.BlockSpec(out_block or (tm, tn), out_map or (lambda i, j, k: (i, j))),
        scratch_shapes=[pltpu.VMEM((tm, tn), F32)],
        compiler_params=_cp(vmem=VMEM_BIG),
    )(a, b)


def _ada_fwd(cin, w_ada_s, b_s):
    tn = 512

    def body(c_ref, w_ref, b_ref, o_ref):
        o_ref[...] = _dg(_silu(c_ref[...]), w_ref[...], _NN) + b_ref[...]

    return pl.pallas_call(
        body, name="ada_fwd", grid=(1536 // tn,), out_shape=jax.ShapeDtypeStruct((64, 1536), F32),
        in_specs=[pl.BlockSpec((64, D), lambda j: (0, 0)), pl.BlockSpec((D, tn), lambda j: (0, j)),
                  pl.BlockSpec((1, tn), lambda j: (0, j))],
        out_specs=pl.BlockSpec((64, tn), lambda j: (0, j)),
        compiler_params=_cp(vmem=VMEM_BIG),
    )(cin, w_ada_s, b_s)


def _ada_bwd(cin, gmod_s, w_ada_s):
    tn = 512

    def body(c_ref, g_ref, w_ref, gw_ref, pc_ref):
        j = pl.program_id(0)
        g = g_ref[...]
        gw_ref[...] = _dg(_silu(c_ref[...]), g, _TN)
        rr = lax.broadcasted_iota(jnp.int32, (8, 64), 0)
        cc = lax.broadcasted_iota(jnp.int32, (8, 64), 1)
        sel = jnp.where((rr == 0) & ((cc & 7) == 2), 1.0, 0.0).astype(F32)
        dctx = _hmm(sel, g)
        part = _dg(dctx, w_ref[...], _NT)

        @pl.when(j == 0)
        def _():
            pc_ref[...] = jnp.zeros_like(pc_ref)

        pc_ref[...] += part

    return pl.pallas_call(
        body, name="ada_bwd", grid=(1536 // tn,),
        out_shape=[jax.ShapeDtypeStruct((D, 1536), F32), jax.ShapeDtypeStruct((8, D), F32)],
        in_specs=[pl.BlockSpec((64, D), lambda j: (0, 0)), pl.BlockSpec((64, tn), lambda j: (0, j)),
                  pl.BlockSpec((D, tn), lambda j: (0, j))],
        out_specs=[pl.BlockSpec((D, tn), lambda j: (0, j)), pl.BlockSpec((8, D), lambda j: (0, 0))],
        compiler_params=_cp(vmem=VMEM_BIG),
    )(cin, gmod_s, w_ada_s)


def _ada_small(gmod, parts, c_ctx):
    def body(g_ref, p_ref, c_ref, gb_ref, gc_ref):
        gb_ref[...] = jnp.sum(g_ref[...], axis=0, keepdims=True)
        ds = p_ref[0][0:1] + p_ref[2][0:1] + p_ref[4][0:1] + p_ref[6][0:1]
        cc = c_ref[...]
        s = _sigmoid(cc)
        gc_ref[...] = ds * (s * (1.0 + cc * (1.0 - s)))

    vm = pl.BlockSpec(memory_space=pltpu.VMEM)
    return pl.pallas_call(
        body, name="ada_small",
        out_shape=[jax.ShapeDtypeStruct((1, 6 * D), F32), jax.ShapeDtypeStruct((1, D), F32)],
        in_specs=[vm, vm, vm], out_specs=[vm, vm],
    )(gmod, parts, c_ctx)


def _tok_specs():
    xs = pl.BlockSpec((None, RT, D), lambda b, j: (b, jnp.maximum(j - 1, 0), 0))
    cs = pl.BlockSpec((None, RT, D), lambda b, j: (b, 0, 0))
    ms = pl.BlockSpec((None, 1, D), lambda b, j: (b, 0, 0))
    cms = pl.BlockSpec((None, 1, D), lambda b, j: (0, 0, 0))
    return xs, cs, ms, cms


def _inproj_fwd(x, ctx, sh1, sc1, csh1, csc1, n1w, w_in_p):
    def body(x_ref, c_ref, sh_ref, sc_ref, csh_ref, csc_ref, w1_ref, w_ref, p_ref, hm_ref):
        is_ctx = pl.program_id(1) == 0
        xt = jnp.where(is_ctx, c_ref[...], x_ref[...])
        sh = jnp.where(is_ctx, csh_ref[...], sh_ref[...])
        sc = jnp.where(is_ctx, csc_ref[...], sc_ref[...])
        hb = _norm_mod(xt, w1_ref[...], sh, sc).astype(BF16)
        hm_ref[...] = hb
        p_ref[...] = lax.dot_general(hb, w_ref[...], _NT, preferred_element_type=F32)

    xs, cs, ms, cms = _tok_specs()
    return pl.pallas_call(
        body, name="inproj_fwd", grid=(NB, T // RT),
        out_shape=[jax.ShapeDtypeStruct((NB, T, PC), F32), jax.ShapeDtypeStruct((NB, T, D), BF16)],
        in_specs=[xs, cs, ms, ms, cms, cms, pl.BlockSpec((1, D), lambda b, j: (0, 0)),
                  pl.BlockSpec((PC, D), lambda b, j: (0, 0))],
        out_specs=[pl.BlockSpec((None, RT, PC), lambda b, j: (b, j, 0)),
                   pl.BlockSpec((None, RT, D), lambda b, j: (b, j, 0))],
        compiler_params=_cp(vmem=VMEM_BIG),
    )(x, ctx, sh1, sc1, csh1, csc1, n1w, w_in_p)


def _inproj_bwd(dp_dn, dz, dp_na, dgt, w_in_p, x, ctx, sh1, sc1, csh1, csc1, n1w, dx1):
    def body(d1q_ref, d1k_ref, d1v_ref, d2_ref, d3_ref, d4_ref, w_ref, x_ref, c_ref, sh_ref, sc_ref, csh_ref, csc_ref,
             w1_ref, dx1_ref, dpb_ref, gx_ref, dsh_ref, dsc_ref, dcsh_ref, dcsc_ref, dw1_ref):
        b, j = pl.program_id(0), pl.program_id(1)
        is_ctx = j == 0
        dp = jnp.concatenate([d1q_ref[...], d1k_ref[...], d1v_ref[...], d2_ref[...], d3_ref[...], d4_ref[...]],
                             axis=1).astype(BF16)
        dpb_ref[...] = dp
        dhm = jnp.dot(dp, w_ref[...], preferred_element_type=F32)
        xt = jnp.where(is_ctx, c_ref[...], x_ref[...])
        sh = jnp.where(is_ctx, csh_ref[...], sh_ref[...])
        sc = jnp.where(is_ctx, csc_ref[...], sc_ref[...])
        _, vjp = jax.vjp(_norm_mod, xt, w1_ref[...], sh, sc)
        dxt, dw1, dsh, dsc = vjp(dhm)

        @pl.when((b == 0) & (j == 0))
        def _():
            dw1_ref[...] = jnp.zeros_like(dw1_ref)
            dcsh_ref[...] = jnp.zeros_like(dcsh_ref)
            dcsc_ref[...] = jnp.zeros_like(dcsc_ref)

        dw1_ref[...] += dw1

        @pl.when(is_ctx)
        def _():
            dcsh_ref[...] += dsh
            dcsc_ref[...] += dsc

        @pl.when(j == 1)
        def _():
            dsh_ref[...] = jnp.zeros_like(dsh_ref)
            dsc_ref[...] = jnp.zeros_like(dsc_ref)

        @pl.when(j >= 1)
        def _():
            dsh_ref[...] += dsh
            dsc_ref[...] += dsc
            gx_ref[...] = dx1_ref[...] + dxt

    xs, cs, ms, cms = _tok_specs()
    tsp = lambda w: pl.BlockSpec((None, RT, w), lambda b, j: (b, j, 0))
    one = pl.BlockSpec((1, D), lambda b, j: (0, 0))
    return pl.pallas_call(
        body, name="inproj_bwd", grid=(NB, T // RT),
        out_shape=[jax.ShapeDtypeStruct((NB, T, PC), BF16), jax.ShapeDtypeStruct((NB, SEQ, D), F32),
                   jax.ShapeDtypeStruct((NB, 1, D), F32), jax.ShapeDtypeStruct((NB, 1, D), F32),
                   jax.ShapeDtypeStruct((1, 1, D), F32), jax.ShapeDtypeStruct((1, 1, D), F32),
                   jax.ShapeDtypeStruct((1, D), F32)],
        in_specs=[tsp(512), tsp(512), tsp(512), tsp(512), tsp(1536), tsp(128), pl.BlockSpec((PC, D), lambda b, j: (0, 0)),
                  xs, cs, ms, ms, cms, cms, one, xs],
        out_specs=[tsp(PC), xs, ms, ms, cms, cms, one],
        compiler_params=_cp(vmem=VMEM_BIG),
    )(*dp_dn, dz, dp_na, dgt, w_in_p, x, ctx, sh1, sc1, csh1, csc1, n1w, dx1)


def _seg_valid(t, off):
    lo = jnp.where(t < CT, 0, CT)
    hi = jnp.where(t < CT, CT, T)
    return (t + off >= lo) & (t + off < hi)


def _shifted(u, off):
    t = lax.broadcasted_iota(jnp.int32, (T, 1), 0)
    s = u if off == 0 else pltpu.roll(u, (-off) % T, 0)
    return jnp.where(_seg_valid(t, off), s, 0.0)


def _conv_pre(u, w):
    acc = _shifted(u, -2) * w[0:1, :]
    for kk in range(1, 5):
        acc = acc + _shifted(u, kk - 2) * w[kk:kk + 1, :]
    return acc


def _act_q(pre):
    a = _silu(pre)
    return a * lax.rsqrt(jnp.sum(a * a, axis=-1, keepdims=True) + EPS) * (128.0 ** -0.5)


def _act_k(pre):
    a = _silu(pre)
    return a * lax.rsqrt(jnp.sum(a * a, axis=-1, keepdims=True) + EPS)


def _dn_prep_fwd(p, cw):
    def body(pq_ref, pk_ref, pv_ref, wq_ref, wk_ref, wv_ref, q_ref, k_ref, v_ref):
        q_ref[...] = _act_q(_conv_pre(pq_ref[...], wq_ref[...]))
        k_ref[...] = _act_k(_conv_pre(pk_ref[...], wk_ref[...]))
        v_ref[...] = _silu(_conv_pre(pv_ref[...], wv_ref[...]))

    ps = lambda o: pl.BlockSpec((None, T, 128), lambda b, h: (b, 0, o + h))
    ws = lambda o: pl.BlockSpec((8, 128), lambda b, h: (0, o + h))
    os_ = pl.BlockSpec((None, T, 128), lambda b, h: (b, 0, h))
    return pl.pallas_call(
        body, name="dn_prep_fwd", grid=(NB, DNH), out_shape=[jax.ShapeDtypeStruct((NB, T, DNW), F32)] * 3,
        in_specs=[ps(0), ps(4), ps(8), ws(0), ws(4), ws(8)], out_specs=[os_] * 3,
        compiler_params=_cp(vmem=VMEM_BIG),
    )(p, p, p, cw, cw, cw)


def _dn_prep_bwd(p, cw, dqs, dks, dvs):
    def body(pq_ref, pk_ref, pv_ref, wq_ref, wk_ref, wv_ref, dqf_ref, dqb_ref, dkf_ref, dkb_ref, dvf_ref, dvb_ref,
             oq_ref, ok_ref, ov_ref, gq_ref, gk_ref, gv_ref):
        b = pl.program_id(1)

        def one(p_ref, w_ref, df_ref, db_ref, act, o_ref, g_ref):
            u = p_ref[...]
            w = w_ref[...]
            _, vjp = jax.vjp(act, _conv_pre(u, w))
            (dpre,) = vjp(df_ref[...] + db_ref[...])
            du = _shifted(dpre, 2) * w[0:1, :]
            rows = [jnp.sum(dpre * _shifted(u, -2), axis=0, keepdims=True)]
            for kk in range(1, 5):
                du = du + _shifted(dpre, 2 - kk) * w[kk:kk + 1, :]
                rows.append(jnp.sum(dpre * _shifted(u, kk - 2), axis=0, keepdims=True))
            o_ref[...] = du
            gw = jnp.concatenate(rows + [jnp.zeros((3, 128), F32)], axis=0)

            @pl.when(b == 0)
            def _():
                g_ref[...] = gw

            @pl.when(b > 0)
            def _():
                g_ref[...] += gw

        one(pq_ref, wq_ref, dqf_ref, dqb_ref, _act_q, oq_ref, gq_ref)
        one(pk_ref, wk_ref, dkf_ref, dkb_ref, _act_k, ok_ref, gk_ref)
        one(pv_ref, wv_ref, dvf_ref, dvb_ref, _silu, ov_ref, gv_ref)

    ps = lambda o: pl.BlockSpec((None, T, 128), lambda h, b: (b, 0, o + h))
    ws = lambda o: pl.BlockSpec((8, 128), lambda h, b: (0, o + h))
    ds = pl.BlockSpec((None, T, 128), lambda h, b: (b, 0, h))
    outs = pl.pallas_call(
        body, name="dn_prep_bwd", grid=(DNH, NB),
        out_shape=[jax.ShapeDtypeStruct((NB, T, DNW), F32)] * 3 + [jax.ShapeDtypeStruct((8, DNW), F32)] * 3,
        in_specs=[ps(0), ps(4), ps(8), ws(0), ws(4), ws(8)] + [ds] * 6,
        out_specs=[ds] * 3
        + [pl.BlockSpec((8, 128), lambda h, b: (0, h))] * 3,
        compiler_params=_cp(vmem=VMEM_BIG),
    )(p, p, p, cw, cw, cw, *dqs, *dks, *dvs)
    return outs


def _gate_fn(a, prm):
    lane = lax.broadcasted_iota(jnp.int32, a.shape, 1)
    z = a + prm[1:2, :]
    sp = jnp.maximum(z, 0.0) + jnp.log(1.0 + jnp.exp(-jnp.abs(z)))
    g = -jnp.exp(prm[0:1, :]) * sp
    return jnp.where(lane < 8, g, jnp.where(lane < 16, _sigmoid(a), 0.0))


def _gates_fwd(p, prm):
    def body(a_ref, prm_ref, o_ref):
        o_ref[...] = _gate_fn(a_ref[...], prm_ref[...])

    return pl.pallas_call(
        body, name="gates_fwd", grid=(NB,), out_shape=jax.ShapeDtypeStruct((NB, T, 128), F32),
        in_specs=[pl.BlockSpec((None, T, 128), lambda b: (b, 0, COL_G // 128)), pl.BlockSpec((8, 128), lambda b: (0, 0))],
        out_specs=pl.BlockSpec((None, T, 128), lambda b: (b, 0, 0)),
    )(p, prm)


def _gates_bwd(p, prm, dgf, dgb):
    def body(a_ref, prm_ref, dgf_ref, dgb_ref, o_ref, dprm_ref):
        b = pl.program_id(0)
        tot = dgf_ref[...] + dgb_ref[...]
        _, vjp = jax.vjp(_gate_fn, a_ref[...], prm_ref[...])
        da, dprm = vjp(tot)
        o_ref[...] = da

        @pl.when(b == 0)
        def _():
            dprm_ref[...] = dprm

        @pl.when(b > 0)
        def _():
            dprm_ref[...] += dprm

    return pl.pallas_call(
        body, name="gates_bwd", grid=(NB,),
        out_shape=[jax.ShapeDtypeStruct((NB, T, 128), F32), jax.ShapeDtypeStruct((8, 128), F32)],
        in_specs=[pl.BlockSpec((None, T, 128), lambda b: (b, 0, COL_G // 128)), pl.BlockSpec((8, 128), lambda b: (0, 0)),
                  pl.BlockSpec((None, T, 128), lambda b: (b, 0, 0)), pl.BlockSpec((None, T, 128), lambda b: (b, 0, 0))],
        out_specs=[pl.BlockSpec((None, T, 128), lambda b: (b, 0, 0)), pl.BlockSpec((8, 128), lambda b: (0, 0))],
        compiler_params=_cp(vmem=VMEM_BIG),
    )(p, prm, dgf, dgb)


def _lockstep(gens, joint=None):
    n = len(gens)
    res, vals = [None] * n, [None] * n
    while True:
        msgs = []
        for i, g in enumerate(gens):
            try:
                msgs.append((g.send(vals[i]), False))
            except StopIteration as e:
                res[i] = e.value
                msgs.append((None, True))
        if msgs[0][1]:
            return res
        if msgs[0][0] is not None:
            vals = list(joint(tuple(m[0] for m in msgs)))
        else:
            vals = [None] * n


def _tri_inv_gen(m, md, m1, m2, eye):
    n = -(m * md)
    t = eye + n
    pw = n
    for _ in range(3):
        pw = _dg(pw, pw, _NN)
        yield
        t = t + _dg(t, pw, _NN)
        yield
    a = _dg(t, m * m1, _NN)
    yield
    t = t - _dg(a, t, _NN)
    yield
    a = _dg(t, m * m2, _NN)
    yield
    return t - _dg(a, t, _NN)


def _tri_inv_bwd_gen(t, g):
    tt = t.T
    a = _dg(tt, g, _NN)
    yield
    return -_dg(a, tt, _NN)


@jax.custom_vjp
def _tri_inv(ms, md, m1, m2, eye):
    return tuple(_lockstep([_tri_inv_gen(m, md, m1, m2, eye) for m in ms]))


def _tri_inv_f(ms, md, m1, m2, eye):
    ts = _tri_inv(ms, md, m1, m2, eye)
    return ts, (ts, md)


def _tri_inv_b(res, gs):
    ts, md = res
    z = jnp.zeros_like(md)
    return (tuple(_lockstep([_tri_inv_bwd_gen(t, g) for t, g in zip(ts, gs)])), z, z, z, z)


_tri_inv.defvjp(_tri_inv_f, _tri_inv_b)


@jax.custom_vjp
def _tri_inv_saved(ms, ts):
    return tuple(t.astype(F32) for t in ts)


def _tri_inv_saved_f(ms, ts):
    return _tri_inv_saved(ms, ts), ts


def _tri_inv_saved_b(ts, gs):
    dms = tuple(_lockstep([_tri_inv_bwd_gen(t, g) for t, g in zip(ts, gs)]))
    return dms, tuple(jnp.zeros_like(t) for t in ts)


_tri_inv_saved.defvjp(_tri_inv_saved_f, _tri_inv_saved_b)


SR = DNH * CH


def _scan_masks():
    i, j = np.arange(SR)[:, None], np.arange(SR)[None, :]
    same = (i >> 6) == (j >> 6)
    cm = np.zeros((2, 6, SR, SR), np.float32)
    for d in range(2):
        diff = i - j if d == 0 else j - i
        cm[d, 0] = same & (diff >= 0)
        cm[d, 1] = same & (diff > 0)
        cm[d, 2] = (i >> 4) == (j >> 4)
        cm[d, 3] = ((i >> 5) == (j >> 5)) & ((i >> 4) != (j >> 4))
        cm[d, 4] = (i >> 5) != (j >> 5)
        cm[d, 5] = i == j
    t, u = np.arange(CH)[:, None], np.arange(CH)[None, :]
    c64 = np.stack([t >= u, t <= u]).astype(np.float32)
    bm = ((np.arange(SR)[:, None] >> 6) == (np.arange(DNW)[None, :] >> 7)).astype(np.float32)
    return jnp.asarray(cm), jnp.asarray(c64), jnp.asarray(bm)


@jax.custom_vjp
def _fold(x, bm):
    xm = x * bm
    return xm[:, 0:128] + xm[:, 128:256] + xm[:, 256:384] + xm[:, 384:512]


def _fold_f(x, bm):
    return _fold(x, bm), bm


def _fold_b(bm, g):
    return jnp.concatenate([g, g, g, g], axis=1) * bm, jnp.zeros_like(bm)


_fold.defvjp(_fold_f, _fold_b)


def _chunk_gen(q, k, v, gb, s, cm, c64, bm, d):
    incl, strict = cm[0], cm[1]
    lane = lax.broadcasted_iota(jnp.int32, (1, 128), 1)
    oh = lambda n: jnp.where(lane == n, 1.0, 0.0).astype(F32)
    col = lambda x, base: jnp.concatenate(
        [jnp.sum(x * oh(base + d * DNH + h), axis=1, keepdims=True) for h in range(DNH)], axis=0)
    beta = col(gb, 8)
    gam = col(_mask_mm(c64, gb), 0)
    yield
    tot = jnp.sum(gb, axis=0, keepdims=True)
    gtots = [jnp.sum(tot * oh(d * DNH + h), axis=1, keepdims=True) for h in range(DNH)]
    gtot = jnp.concatenate([jnp.broadcast_to(t, (CH, 1)) for t in gtots], axis=0)
    eg = jnp.concatenate([jnp.broadcast_to(jnp.exp(t), (1, 128)) for t in gtots], axis=1)
    row0 = jnp.where(lax.broadcasted_iota(jnp.int32, (128, 1), 0) == 0, 1.0, 0.0).astype(F32)
    gam_row = jnp.sum(jnp.broadcast_to(gam, (SR, 128)).T * row0, axis=0, keepdims=True)
    dec = jnp.exp(jnp.where(incl > 0.5, gam - gam_row, NEG))
    e_gam = jnp.exp(gam)
    kb = k * beta
    kk = _mm_nt(kb, k)
    yield
    tm = yield strict * (kk * dec)
    u = _mm(tm, v * beta)
    yield
    w = _mm(tm, kb * e_gam)
    yield
    qk = _mm_nt(q, k) * dec
    yield
    ws = _mm(w, s)
    yield
    v_new = u - _fold(ws, bm)
    qs = _mm(q * e_gam, s)
    yield
    o = _fold(qs, bm) + _mm(qk, v_new)
    yield
    vbd = jnp.concatenate([v_new, v_new, v_new, v_new], axis=1) * bm
    s_new = s * eg + _mm_tn(k * jnp.exp(gtot - gam), vbd)
    return o, s_new, tm


def _chunks(args, cm_ref, c64_ref, bm, saved=None):
    cm0 = cm_ref[0]
    if saved is None:
        inv = lambda ms: _tri_inv(ms, cm0[2], cm0[3], cm0[4], cm0[5])
    else:
        inv = lambda ms: _tri_inv_saved(ms, saved)
    return _lockstep([_chunk_gen(q, k, v, gb, s, cm_ref[d], c64_ref[d], bm, d) for q, k, v, gb, s, d in args], inv)


def _chunk_rev(s):
    return jnp.where(s < NCH_CTX, NCH_CTX - 1 - s, NCH + NCH_CTX - 1 - s)


def _scan_specs(step):
    cf = lambda s: step(s)
    cb = lambda s: _chunk_rev(step(s))
    tok = lambda c: pl.BlockSpec((NB, CH, DNW), lambda s: (0, c(s), 0))
    gat = lambda c: pl.BlockSpec((NB, CH, 128), lambda s: (0, c(s), 0))
    sta = lambda c: pl.BlockSpec((NB, None, 128, DNW), lambda s: (0, c(s), 0, 0))
    inv = lambda c: pl.BlockSpec((NB, None, SR, SR), lambda s: (0, c(s), 0, 0))
    return (tok(cf), gat(cf), sta(cf), inv(cf)), (tok(cb), gat(cb), sta(cb), inv(cb))


def _const_specs():
    return [pl.BlockSpec((2, 6, SR, SR), lambda s: (0, 0, 0, 0)), pl.BlockSpec((2, CH, CH), lambda s: (0, 0, 0)),
            pl.BlockSpec((SR, DNW), lambda s: (0, 0))]


SCAN_CHAINS = [(b, d) for b in range(NB) for d in range(2)]


def _stack(ref, b):
    return jnp.concatenate([ref[b, :, 128 * h:128 * h + 128] for h in range(DNH)], axis=0)


def _unstack_to(ref, b, val):
    for h in range(DNH):
        ref[b, :, 128 * h:128 * h + 128] = val[CH * h:CH * h + CH]


def _gather_plan(ins, outs, pos):
    me = _slot(pos)
    locals_, xfers = [], []
    for src, dst in zip(ins, outs):
        locals_.append((src, dst.at[me]))
        for k in range(1, 8):
            peer = _flip(pos, k)
            xfers.append((src, dst.at[me], dst.at[_slot(peer)], peer))
    return locals_, xfers


def _scan_fwd(q, k, v, gates, carry=()):
    nc = len(carry)

    def body(*refs):
        cm_ref, c64_ref, bm_ref, qf, kf, vf, gf, qb, kb, vb, gb = refs[:11]
        c_in = refs[11:11 + nc]
        of_ref, sf_ref, tf_ref, ob_ref, sb_ref, tb_ref = refs[11 + nc:17 + nc]
        c_out = refs[17 + nc:17 + 2 * nc]
        s_ref = refs[17 + 2 * nc]
        sems = refs[18 + 2 * nc:]
        if nc:
            @pl.when(pl.program_id(0) == 0)
            def _():
                _xfer_start(_gather_plan(c_in, c_out, _my_pos()), sems)

        @pl.when(pl.program_id(0) == 0)
        def _():
            s_ref[...] = jnp.zeros_like(s_ref)

        bm = bm_ref[...]
        dirs = ((qf, kf, vf, gf, of_ref, sf_ref, tf_ref), (qb, kb, vb, gb, ob_ref, sb_ref, tb_ref))
        args = [(_stack(dirs[d][0], b), _stack(dirs[d][1], b), _stack(dirs[d][2], b), dirs[d][3][b], s_ref[i], d)
                for i, (b, d) in enumerate(SCAN_CHAINS)]
        for i, (o, s_new, tm) in enumerate(_chunks(args, cm_ref, c64_ref, bm)):
            b, d = SCAN_CHAINS[i]
            dirs[d][5][b] = args[i][4]
            dirs[d][6][b] = tm.astype(BF16)
            _unstack_to(dirs[d][4], b, o)
            s_ref[i] = s_new

        if nc:
            @pl.when(pl.program_id(0) == NCH - 1)
            def _():
                _xfer_wait(_gather_plan(c_in, c_out, _my_pos()), sems)

    (tf, gf_, sf_, if_), (tb, gb_, sb_, ib_) = _scan_specs(lambda s: s)
    o_sh = jax.ShapeDtypeStruct((NB, T, DNW), F32)
    st_sh = jax.ShapeDtypeStruct((NB, NCH, 128, DNW), F32)
    inv_sh = jax.ShapeDtypeStruct((NB, NCH, SR, SR), BF16)
    hbm = pl.BlockSpec(memory_space=pltpu.HBM)
    return pl.pallas_call(
        body, name="scan_fwd", grid=(NCH,),
        out_shape=[o_sh, st_sh, inv_sh, o_sh, st_sh, inv_sh] + [jax.ShapeDtypeStruct((8,) + h.shape, h.dtype) for h in carry],
        in_specs=_const_specs() + [tf, tf, tf, gf_, tb, tb, tb, gb_] + [hbm] * nc,
        out_specs=[tf, sf_, if_, tb, sb_, ib_] + [hbm] * nc,
        scratch_shapes=[pltpu.VMEM((len(SCAN_CHAINS), 128, DNW), F32)] + (_xfer_sems(nc, 7 * nc) if nc else []),
        compiler_params=_cp(vmem=VMEM_BIG),
    )(*_scan_masks(), q, k, v, gates, q, k, v, gates, *carry)


def _scan_bwd(q, k, v, gates, st_f, st_b, inv_f, inv_b, do, carry=()):
    nc = len(carry)

    def body(*refs):
        (cm_ref, c64_ref, bm_ref, qf, kf, vf, gf, sf, tf_, dof, qb, kb, vb, gb, sb, tb_, dob) = refs[:17]
        c_in = refs[17:17 + nc]
        dqf, dkf, dvf, dgf, dqb, dkb, dvb, dgb = refs[17 + nc:25 + nc]
        c_out = refs[25 + nc:25 + 2 * nc]
        ds_ref = refs[25 + 2 * nc]
        sems = refs[26 + 2 * nc:]
        first = pl.program_id(0) == 0
        last = pl.program_id(0) == NCH - 1
        if nc:
            @pl.when(first)
            def _():
                _xfer_start(_scatter_plan(c_in, c_out, _my_pos()), sems)

        @pl.when(first)
        def _():
            ds_ref[...] = jnp.zeros_like(ds_ref)

        bm = bm_ref[...]
        dirs = ((qf, kf, vf, gf, sf, dof, dqf, dkf, dvf, dgf), (qb, kb, vb, gb, sb, dob, dqb, dkb, dvb, dgb))
        prim = tuple((_stack(dirs[d][0], b), _stack(dirs[d][1], b), _stack(dirs[d][2], b), dirs[d][3][b], dirs[d][4][b])
                     for b, d in SCAN_CHAINS)
        cots = tuple((_stack(dirs[d][5], b), ds_ref[i]) for i, (b, d) in enumerate(SCAN_CHAINS))
        saved = tuple((tf_, tb_)[d][b] for b, d in SCAN_CHAINS)
        _, vjp = jax.vjp(lambda ps: tuple(r[:2] for r in _chunks([p + (SCAN_CHAINS[i][1],) for i, p in enumerate(ps)],
                                                                cm_ref, c64_ref, bm, saved)), prim)
        (grads,) = vjp(cots)
        for i, (dq, dk, dv, dg, ds) in enumerate(grads):
            b, d = SCAN_CHAINS[i]
            _unstack_to(dirs[d][6], b, dq)
            _unstack_to(dirs[d][7], b, dk)
            _unstack_to(dirs[d][8], b, dv)
            dirs[d][9][b] = dg
            ds_ref[i] = ds

        if nc:
            @pl.when(last)
            def _():
                _xfer_wait(_scatter_plan(c_in, c_out, _my_pos()), sems)

    (tf, gf_, sf_, if_), (tb, gb_, sb_, ib_) = _scan_specs(lambda s: NCH - 1 - s)
    o_sh = jax.ShapeDtypeStruct((NB, T, DNW), F32)
    g_sh = jax.ShapeDtypeStruct((NB, T, 128), F32)
    hbm = pl.BlockSpec(memory_space=pltpu.HBM)
    return pl.pallas_call(
        body, name="scan_bwd", grid=(NCH,),
        out_shape=[o_sh, o_sh, o_sh, g_sh] * 2 + [jax.ShapeDtypeStruct(g.shape, g.dtype) for g in carry],
        in_specs=_const_specs() + [tf, tf, tf, gf_, sf_, if_, tf, tb, tb, tb, gb_, sb_, ib_, tb] + [hbm] * nc,
        out_specs=[tf, tf, tf, gf_, tb, tb, tb, gb_] + [hbm] * nc,
        scratch_shapes=[pltpu.VMEM((len(SCAN_CHAINS), 128, DNW), F32)] + (_xfer_sems(nc, 7 * nc) if nc else []),
        compiler_params=_cp(vmem=VMEM_BIG),
    )(*_scan_masks(), q, k, v, gates, st_f, inv_f, do, q, k, v, gates, st_b, inv_b, do, *carry)


def _segmean(a, bd):
    hi = a.astype(BF16)
    lo = (a - hi.astype(F32)).astype(BF16)
    return jnp.dot(hi, bd, preferred_element_type=F32) + jnp.dot(lo, bd, preferred_element_type=F32)


def _na_prep_fwd(p, qw, kw, bd):
    def body(q_ref, k_ref, v_ref, qw_ref, kw_ref, bd_ref, oq_ref, ok_ref, ov_ref):
        bdm = bd_ref[...]
        q = q_ref[...]
        k = k_ref[...]
        oq_ref[...] = (q * lax.rsqrt(_segmean(q * q, bdm) + EPS) * qw_ref[...] * 0.125).astype(BF16)
        ok_ref[...] = (k * lax.rsqrt(_segmean(k * k, bdm) + EPS) * kw_ref[...]).astype(BF16)
        ov_ref[...] = v_ref[...].astype(BF16)

    ps = lambda o: pl.BlockSpec((None, RT, NAW), lambda b, j: (b, j, COL_NA // NAW + o))
    one = pl.BlockSpec((1, NAW), lambda b, j: (0, 0))
    os_ = pl.BlockSpec((None, RT, NAW), lambda b, j: (b, j, 0))
    return pl.pallas_call(
        body, name="na_prep_fwd", grid=(NB, T // RT), out_shape=[jax.ShapeDtypeStruct((NB, T, NAW), BF16)] * 3,
        in_specs=[ps(0), ps(1), ps(2), one, one, pl.BlockSpec((NAW, NAW), lambda b, j: (0, 0))], out_specs=[os_] * 3,
    )(p, p, p, qw, kw, bd)


def _na_prep_bwd(p, qw, kw, bd, dq, dk, dv):
    def body(q_ref, k_ref, qw_ref, kw_ref, bd_ref, dq_ref, dk_ref, dv_ref, o_ref, gq_ref, gk_ref):
        b, j = pl.program_id(0), pl.program_id(1)
        bdm = bd_ref[...]

        def one(x, w, dy, scale):
            r = lax.rsqrt(_segmean(x * x, bdm) + EPS)
            g = dy * w * scale
            dx = r * g - x * (r * r * r) * _segmean(g * x, bdm)
            dw = jnp.sum(dy * x * r * scale, axis=0, keepdims=True)
            return dx, dw

        dqv = jnp.where(j == 0, 0.0, dq_ref[...])
        dxq, dwq = one(q_ref[...], qw_ref[...], dqv, 0.125)
        dxk, dwk = one(k_ref[...], kw_ref[...], dk_ref[...], 1.0)
        o_ref[...] = jnp.concatenate([dxq, dxk, dv_ref[...]], axis=1)
        pad = jnp.zeros((7, NAW), F32)

        @pl.when((b == 0) & (j == 0))
        def _():
            gq_ref[...] = jnp.zeros_like(gq_ref)
            gk_ref[...] = jnp.zeros_like(gk_ref)

        gq_ref[...] += jnp.concatenate([dwq, pad], axis=0)
        gk_ref[...] += jnp.concatenate([dwk, pad], axis=0)

    ps = lambda o: pl.BlockSpec((None, RT, NAW), lambda b, j: (b, j, COL_NA // NAW + o))
    one = pl.BlockSpec((1, NAW), lambda b, j: (0, 0))
    ts = pl.BlockSpec((None, RT, NAW), lambda b, j: (b, j, 0))
    acc = pl.BlockSpec((8, NAW), lambda b, j: (0, 0))
    return pl.pallas_call(
        body, name="na_prep_bwd", grid=(NB, T // RT),
        out_shape=[jax.ShapeDtypeStruct((NB, T, 1536), F32), jax.ShapeDtypeStruct((8, NAW), F32),
                   jax.ShapeDtypeStruct((8, NAW), F32)],
        in_specs=[ps(0), ps(1), one, one, pl.BlockSpec((NAW, NAW), lambda b, j: (0, 0)),
                  pl.BlockSpec((None, RT, NAW), lambda b, j: (b, jnp.maximum(j - 1, 0), 0)), ts, ts],
        out_specs=[pl.BlockSpec((None, RT, 1536), lambda b, j: (b, j, 0)), acc, acc],
        compiler_params=_cp(vmem=VMEM_BIG),
    )(p, p, qw, kw, bd, dq, dk, dv)


def _na_row0(r):
    return jnp.clip(r - WIN_ROWS // 2, 0, ROWS - WIN_ROWS)


def _na_cls(r):
    return _na_row0(r) - r + WIN_ROWS - 1


NHS = 4
NLW = NHS * 64
NSR = NHS * GRID_W


def _na_stack(x):
    lane = lax.broadcasted_iota(jnp.int32, (1, NLW), 1)
    return jnp.concatenate([jnp.where((lane >> 6) == h, x, jnp.zeros_like(x)) for h in range(NHS)], axis=0)


def _na_unstack(y):
    lane = lax.broadcasted_iota(jnp.int32, (1, NLW), 1)
    out = jnp.zeros((GRID_W, NLW), y.dtype)
    for h in range(NHS):
        out = jnp.where((lane >> 6) == h, y[GRID_W * h:GRID_W * h + GRID_W], out)
    return out


def _na_probs(q4, kw, kc, bias):
    s_w = lax.dot_general(q4, kw, _NT, preferred_element_type=F32) + bias
    s_c = lax.dot_general(q4, kc, _NT, preferred_element_type=F32)
    mx = jnp.maximum(jnp.max(s_w, axis=1, keepdims=True), jnp.max(s_c, axis=1, keepdims=True))
    p_w = jnp.exp(s_w - mx)
    p_c = jnp.exp(s_c - mx)
    inv = 1.0 / (jnp.sum(p_w, axis=1, keepdims=True) + jnp.sum(p_c, axis=1, keepdims=True))
    return p_w, p_c, inv


def _na_specs():
    qs = pl.BlockSpec((None, GRID_W, NLW), lambda g, b, r: (b, CT // GRID_W + r, g))
    kv = pl.BlockSpec((None, T, NLW), lambda g, b, r: (b, 0, g))
    bs = pl.BlockSpec((None, None, NSR, NWIN), lambda g, b, r: (_na_cls(r), g, 0, 0))
    lat = pl.BlockSpec((None, GRID_W, NLW), lambda g, b, r: (b, r, g))
    return qs, kv, bs, lat


def _na_fwd(qn, kn, vb, bias, carry=()):
    nc = len(carry)
    grid = (NAH // NHS, NB, ROWS)

    def body(*refs):
        q_ref, k_ref, v_ref, b_ref = refs[:4]
        c_in, o_ref, c_out, sems = refs[4:4 + nc], refs[4 + nc], refs[5 + nc:5 + 2 * nc], refs[5 + 2 * nc:]
        ids = [pl.program_id(a) for a in range(3)]
        if nc:
            @pl.when((ids[0] == 0) & (ids[1] == 0) & (ids[2] == 0))
            def _():
                _xfer_start(_gather_plan(c_in, c_out, _my_pos()), sems)

        r = ids[2]
        start = pl.multiple_of(CT + _na_row0(r) * GRID_W, GRID_W)
        kw, vw = k_ref[pl.ds(start, NWIN), :], v_ref[pl.ds(start, NWIN), :]
        kc, vc = k_ref[0:CT, :], v_ref[0:CT, :]
        p_w, p_c, inv = _na_probs(_na_stack(q_ref[...]), kw, kc, b_ref[...])
        o4 = (jnp.dot(p_w.astype(BF16), vw, preferred_element_type=F32)
              + jnp.dot(p_c.astype(BF16), vc, preferred_element_type=F32)) * inv
        o_ref[...] = _na_unstack(o4).astype(BF16)

        if nc:
            @pl.when((ids[0] == grid[0] - 1) & (ids[1] == grid[1] - 1) & (ids[2] == grid[2] - 1))
            def _():
                _xfer_wait(_gather_plan(c_in, c_out, _my_pos()), sems)

    qs, kv, bs, lat = _na_specs()
    hbm = pl.BlockSpec(memory_space=pltpu.HBM)
    res = pl.pallas_call(
        body, name="na_fwd", grid=grid,
        out_shape=[jax.ShapeDtypeStruct((NB, SEQ, NAW), BF16)] + [jax.ShapeDtypeStruct((8,) + h.shape, h.dtype) for h in carry],
        in_specs=[qs, kv, kv, bs] + [hbm] * nc, out_specs=[lat] + [hbm] * nc,
        scratch_shapes=_xfer_sems(nc, 7 * nc) if nc else [],
        compiler_params=_cp(vmem=VMEM_BIG),
    )(qn, kn, vb, bias, *carry)
    return res[0], list(res[1:])


def _na_bwd(qn, kn, vb, bias, do):
    def body(q_ref, k_ref, v_ref, b_ref, do_ref, dq_ref, dk_ref, dv_ref, db_ref):
        b, r = pl.program_id(1), pl.program_id(2)

        @pl.when((b == 0) & (r == 0))
        def _():
            db_ref[...] = jnp.zeros_like(db_ref)

        @pl.when(r == 0)
        def _():
            dk_ref[...] = jnp.zeros_like(dk_ref)
            dv_ref[...] = jnp.zeros_like(dv_ref)

        start = pl.multiple_of(CT + _na_row0(r) * GRID_W, GRID_W)
        cls = _na_cls(r)
        kw, vw = k_ref[pl.ds(start, NWIN), :], v_ref[pl.ds(start, NWIN), :]
        kc, vc = k_ref[0:CT, :], v_ref[0:CT, :]
        q4 = _na_stack(q_ref[...])
        p_w, p_c, inv = _na_probs(q4, kw, kc, b_ref[...])
        p_w = p_w * inv
        p_c = p_c * inv
        do4 = _na_stack(do_ref[...].astype(BF16))
        dp_w = lax.dot_general(do4, vw, _NT, preferred_element_type=F32)
        dp_c = lax.dot_general(do4, vc, _NT, preferred_element_type=F32)
        delta = jnp.sum(p_w * dp_w, axis=1, keepdims=True) + jnp.sum(p_c * dp_c, axis=1, keepdims=True)
        ds_w = p_w * (dp_w - delta)
        ds_c = p_c * (dp_c - delta)
        db_ref[cls] += ds_w
        ds_wb, ds_cb = ds_w.astype(BF16), ds_c.astype(BF16)
        dq_ref[...] = _na_unstack(jnp.dot(ds_wb, kw, preferred_element_type=F32)
                                  + jnp.dot(ds_cb, kc, preferred_element_type=F32))
        dk_ref[pl.ds(start, NWIN), :] += lax.dot_general(ds_wb, q4, _TN, preferred_element_type=F32)
        dv_ref[pl.ds(start, NWIN), :] += lax.dot_general(p_w.astype(BF16), do4, _TN, preferred_element_type=F32)
        dk_ref[0:CT, :] += lax.dot_general(ds_cb, q4, _TN, preferred_element_type=F32)
        dv_ref[0:CT, :] += lax.dot_general(p_c.astype(BF16), do4, _TN, preferred_element_type=F32)

    qs, kv, bs, lat = _na_specs()
    return pl.pallas_call(
        body, name="na_bwd", grid=(NAH // NHS, NB, ROWS),
        out_shape=[jax.ShapeDtypeStruct((NB, SEQ, NAW), F32)] + [jax.ShapeDtypeStruct((NB, T, NAW), F32)] * 2
        + [jax.ShapeDtypeStruct((WIN_ROWS, NAH // NHS, NSR, NWIN), F32)],
        in_specs=[qs, kv, kv, bs, lat],
        out_specs=[lat, kv, kv, pl.BlockSpec((WIN_ROWS, None, NSR, NWIN), lambda g, b, r: (0, g, 0, 0))],
        compiler_params=_cp(vmem=VMEM_BIG),
    )(qn, kn, vb, bias, do)


def _na_tables():
    cols = np.arange(GRID_W)
    win_start = np.clip(cols - WIN_COLS // 2, 0, GRID_W - WIN_COLS)
    kc = cols[None, :]
    valid = (kc >= win_start[:, None]) & (kc < win_start[:, None] + WIN_COLS)
    rel = np.clip(kc - cols[:, None] + WIN_COLS - 1, 0, 2 * WIN_COLS - 2)
    return valid, rel


def _bias_table(rpb):
    valid, rel = _na_tables()
    nrel = 2 * WIN_COLS - 1
    shifts = np.stack([(rel == r) & valid for r in range(nrel)]).astype(np.float32)
    neg = np.where(valid, 0.0, NEG).astype(np.float32)

    def body(rpb_ref, sh_ref, neg_ref, o_ref):
        h = pl.program_id(0)
        for dr in range(2 * WIN_ROWS - 1):
            tile = lax.fori_loop(0, nrel, lambda r, acc: acc + rpb_ref[h, dr, r] * sh_ref[r], neg_ref[...])
            for cls in range(WIN_ROWS):
                i = dr - cls
                if 0 <= i < WIN_ROWS:
                    o_ref[cls, :, i * GRID_W:(i + 1) * GRID_W] = tile

    return pl.pallas_call(
        body, name="bias_table", grid=(NAH,),
        out_shape=jax.ShapeDtypeStruct((WIN_ROWS, NAH // NHS, NSR, NWIN), F32),
        in_specs=[pl.BlockSpec(memory_space=pltpu.SMEM), pl.BlockSpec((nrel, GRID_W, GRID_W), lambda h: (0, 0, 0)),
                  pl.BlockSpec((GRID_W, GRID_W), lambda h: (0, 0))],
        out_specs=pl.BlockSpec((WIN_ROWS, None, GRID_W, NWIN), lambda h: (0, h // NHS, h % NHS, 0)),
    )(rpb, jnp.asarray(shifts), jnp.asarray(neg))


def _rpb_fold(db):
    def body1(d_ref, o_ref):
        for dr in range(2 * WIN_ROWS - 1):
            acc = None
            for c in range(WIN_ROWS):
                i = dr - c
                if 0 <= i < WIN_ROWS:
                    blk = d_ref[c, :, i * GRID_W:(i + 1) * GRID_W]
                    acc = blk if acc is None else acc + blk
            o_ref[dr] = acc

    t15 = pl.pallas_call(
        body1, name="rpb_fold_rows", grid=(NAH,),
        out_shape=jax.ShapeDtypeStruct((NAH, 2 * WIN_ROWS - 1, GRID_W, GRID_W), F32),
        in_specs=[pl.BlockSpec((WIN_ROWS, None, GRID_W, NWIN), lambda h: (0, h, 0, 0))],
        out_specs=pl.BlockSpec((None, 2 * WIN_ROWS - 1, GRID_W, GRID_W), lambda h: (h, 0, 0, 0)),
        compiler_params=_cp(vmem=VMEM_BIG),
    )(db.reshape(WIN_ROWS, NAH, GRID_W, NWIN))
    valid, rel = _na_tables()
    onehot = np.zeros((GRID_W * GRID_W, 128), np.float32)
    flat_rel, flat_valid = rel.reshape(-1), valid.reshape(-1)
    onehot[np.arange(GRID_W * GRID_W)[flat_valid], flat_rel[flat_valid]] = 1.0

    def body2(a_ref, oh_ref, o_ref):
        o_ref[...] = _hmm(a_ref[...], oh_ref[...])

    vm = pl.BlockSpec(memory_space=pltpu.VMEM)
    out = pl.pallas_call(
        body2, name="rpb_fold_cols", out_shape=jax.ShapeDtypeStruct((NAH * 15, 128), F32),
        in_specs=[vm, vm], out_specs=vm, compiler_params=_cp(vmem=VMEM_BIG),
    )(t15.reshape(NAH * 15, GRID_W * GRID_W), jnp.asarray(onehot))
    return out[:, :31].reshape(NAH, 15, 31)


def _dn_out(o, z, w):
    outs = []
    for h in range(DNH):
        sl = slice(128 * h, 128 * h + 128)
        oh = o[:, sl]
        r = lax.rsqrt(jnp.mean(oh * oh, axis=-1, keepdims=True) + EPS)
        outs.append(oh * r * w[:, sl] * _silu(z[:, sl]))
    return jnp.concatenate(outs, axis=1)


def _outproj_fwd(of, ob, p, na, onw, x, g1, w_out):
    def body(of_ref, ob_ref, z_ref, na_ref, w_ref, x_ref, g_ref, wo_ref, x1_ref, ai_ref, ao_ref):
        dn = _dn_out(of_ref[...] + ob_ref[...], z_ref[...], w_ref[...])
        ai = jnp.concatenate([dn.astype(BF16), na_ref[...]], axis=1)
        ai_ref[...] = ai
        ao = jnp.dot(ai, wo_ref[...], preferred_element_type=F32)
        ao_ref[...] = ao
        x1_ref[...] = x_ref[...] + g_ref[...] * ao

    lat = lambda w: pl.BlockSpec((None, RT, w), lambda b, j: (b, j, 0))
    return pl.pallas_call(
        body, name="outproj_fwd", grid=(NB, SEQ // RT),
        out_shape=[jax.ShapeDtypeStruct((NB, SEQ, D), F32), jax.ShapeDtypeStruct((NB, SEQ, D), BF16),
                   jax.ShapeDtypeStruct((NB, SEQ, D), F32)],
        in_specs=[pl.BlockSpec((None, RT, DNW), lambda b, j: (b, j + 1, 0)),
                  pl.BlockSpec((None, RT, DNW), lambda b, j: (b, j + 1, 0)),
                  pl.BlockSpec((None, RT, DNW), lambda b, j: (b, j + 1, COL_Z // DNW)),
                  lat(NAW), pl.BlockSpec((1, DNW), lambda b, j: (0, 0)), lat(D),
                  pl.BlockSpec((None, 1, D), lambda b, j: (b, 0, 0)), pl.BlockSpec((D, D), lambda b, j: (0, 0))],
        out_specs=[lat(D), lat(D), lat(D)],
        compiler_params=_cp(vmem=VMEM_BIG),
    )(of, ob, p, na, onw, x, g1, w_out)


def _outproj_bwd(dx1, ao, g1, w_out, of, ob, p, onw):
    def body(dx_ref, ao_ref, g_ref, wo_ref, of_ref, ob_ref, z_ref, w_ref, dao_ref, do_ref, dz_ref, dna_ref, dg_ref, dw_ref):
        b, j = pl.program_id(0), pl.program_id(1)

        @pl.when((b == 0) & (j == 0))
        def _():
            dw_ref[...] = jnp.zeros_like(dw_ref)

        @pl.when(j == 0)
        def _():
            do_ref[...] = jnp.zeros_like(do_ref)
            dz_ref[...] = jnp.zeros_like(dz_ref)

        @pl.when(j == 1)
        def _():
            dg_ref[...] = jnp.zeros_like(dg_ref)

        @pl.when(j >= 1)
        def _():
            dx = dx_ref[...]
            dg_ref[...] += jnp.sum(dx * ao_ref[...], axis=0, keepdims=True)
            dao = (g_ref[...] * dx).astype(BF16)
            dao_ref[...] = dao
            dai = lax.dot_general(dao, wo_ref[...], _NT, preferred_element_type=F32)
            dna_ref[...] = dai[:, DNW:]
            _, vjp = jax.vjp(_dn_out, of_ref[...] + ob_ref[...], z_ref[...], w_ref[...])
            do, dz, dw = vjp(dai[:, :DNW])
            do_ref[...] = do
            dz_ref[...] = dz
            dw_ref[...] += jnp.concatenate([dw, jnp.zeros((7, DNW), F32)], axis=0)

    lat = lambda w: pl.BlockSpec((None, RT, w), lambda b, j: (b, jnp.maximum(j - 1, 0), 0))
    tok = lambda w: pl.BlockSpec((None, RT, w), lambda b, j: (b, j, 0))
    return pl.pallas_call(
        body, name="outproj_bwd", grid=(NB, T // RT),
        out_shape=[jax.ShapeDtypeStruct((NB, SEQ, D), BF16), jax.ShapeDtypeStruct((NB, T, DNW), F32),
                   jax.ShapeDtypeStruct((NB, T, DNW), F32), jax.ShapeDtypeStruct((NB, SEQ, NAW), F32),
                   jax.ShapeDtypeStruct((NB, 1, D), F32), jax.ShapeDtypeStruct((8, DNW), F32)],
        in_specs=[lat(D), lat(D), pl.BlockSpec((None, 1, D), lambda b, j: (b, 0, 0)),
                  pl.BlockSpec((D, D), lambda b, j: (0, 0)),
                  tok(DNW), tok(DNW),
                  pl.BlockSpec((None, RT, DNW), lambda b, j: (b, j, COL_Z // DNW)),
                  pl.BlockSpec((1, DNW), lambda b, j: (0, 0))],
        out_specs=[lat(D), tok(DNW), tok(DNW), lat(NAW), pl.BlockSpec((None, 1, D), lambda b, j: (b, 0, 0)),
                   pl.BlockSpec((8, DNW), lambda b, j: (0, 0))],
        compiler_params=_cp(vmem=VMEM_BIG),
    )(dx1, ao, g1, w_out, of, ob, p, onw)


FT = 256
NFT = NTOK // FT
TPE = SEQ // FT
NFF = DFF // FFT


def _ffn_specs():
    row = lambda w: pl.BlockSpec((FT, w), lambda i, f: (i, 0))
    ex = pl.BlockSpec((None, 1, D), lambda i, f: (i // TPE, 0, 0))
    one = pl.BlockSpec((1, D), lambda i, f: (0, 0))
    wg = pl.BlockSpec((None, D, FFT), lambda i, f: (f, 0, 0))
    wu = pl.BlockSpec((None, D, FFT), lambda i, f: (NFF + f, 0, 0))
    wo = pl.BlockSpec((FFT, D), lambda i, f: (f, 0))
    ff = pl.BlockSpec((FT, FFT), lambda i, f: (i, f))
    return row, ex, one, wg, wu, wo, ff


def _ffn_fwd(x1, sh2, sc2, g2, n2w, w_in4, w_out, target):
    def body(x_ref, sh_ref, sc_ref, g_ref, w2_ref, wg_ref, wu_ref, wo_ref, t_ref,
             h2_ref, gt_ref, up_ref, dy_ref, dg_ref, loss_ref, acc_ref):
        i, f = pl.program_id(0), pl.program_id(1)

        @pl.when(f == 0)
        def _():
            h2_ref[...] = _norm_mod(x_ref[...], w2_ref[...], sh_ref[...], sc_ref[...]).astype(BF16)
            acc_ref[...] = jnp.zeros_like(acc_ref)

        h2 = h2_ref[...]
        gate = jnp.dot(h2, wg_ref[...], preferred_element_type=F32)
        up = jnp.dot(h2, wu_ref[...], preferred_element_type=F32)
        gt_ref[...] = gate.astype(BF16)
        up_ref[...] = up.astype(BF16)
        acc_ref[...] += jnp.dot((_silu(gate) * up).astype(BF16), wo_ref[...], preferred_element_type=F32)

        @pl.when((i == 0) & (f == 0))
        def _():
            loss_ref[...] = jnp.zeros_like(loss_ref)

        @pl.when(f == NFF - 1)
        def _():
            ffn = acc_ref[...]
            err = x_ref[...] + g_ref[...] * ffn - t_ref[...]
            dy = err * (1.0 / D)
            dy_ref[...] = dy
            part = jnp.sum(dy * ffn, axis=0, keepdims=True)
            loss_ref[...] += jnp.sum(jnp.sum(err * err, axis=1, keepdims=True), axis=0, keepdims=True) * (0.5 / D)

            @pl.when(i % TPE == 0)
            def _():
                dg_ref[...] = part

            @pl.when(i % TPE != 0)
            def _():
                dg_ref[...] += part

    row, ex, one, wg, wu, wo, ff = _ffn_specs()
    return pl.pallas_call(
        body, name="ffn_fwd", grid=(NFT, NFF),
        out_shape=[jax.ShapeDtypeStruct((NTOK, D), BF16), jax.ShapeDtypeStruct((NTOK, DFF), BF16),
                   jax.ShapeDtypeStruct((NTOK, DFF), BF16), jax.ShapeDtypeStruct((NTOK, D), F32),
                   jax.ShapeDtypeStruct((NB, 1, D), F32), jax.ShapeDtypeStruct((8, 128), F32)],
        in_specs=[row(D), ex, ex, ex, one, wg, wu, wo, row(D)],
        out_specs=[row(D), ff, ff, row(D), ex, pl.BlockSpec((8, 128), lambda i, f: (0, 0))],
        scratch_shapes=[pltpu.VMEM((FT, D), F32)],
        compiler_params=_cp(vmem=VMEM_BIG),
    )(x1, sh2, sc2, g2, n2w, w_in4, w_in4, w_out, target)


def _ffn_bwd(dy, gate, up, x1, sh2, sc2, g2, n2w, w_in4, w_out):
    def body(dy_ref, gt_ref, up_ref, x_ref, sh_ref, sc_ref, g_ref, w2_ref, wg_ref, wu_ref, wo_ref,
             dff_ref, act_ref, dgu_ref, dx_ref, dsh_ref, dsc_ref, dw_ref, acc_ref):
        i, f = pl.program_id(0), pl.program_id(1)

        @pl.when(f == 0)
        def _():
            dff_ref[...] = (g_ref[...] * dy_ref[...]).astype(BF16)
            acc_ref[...] = jnp.zeros_like(acc_ref)

        dact = lax.dot_general(dff_ref[...], wo_ref[...], _NT, preferred_element_type=F32)
        gate, up = gt_ref[...].astype(F32), up_ref[...].astype(F32)
        sg = _sigmoid(gate)
        sl = gate * sg
        act_ref[...] = (sl * up).astype(BF16)
        dgate = (dact * up * (sg * (1.0 + gate * (1.0 - sg)))).astype(BF16)
        dup = (dact * sl).astype(BF16)
        dgu_ref[0] = dgate
        dgu_ref[1] = dup
        acc_ref[...] += (lax.dot_general(dgate, wg_ref[...], _NT, preferred_element_type=F32)
                         + lax.dot_general(dup, wu_ref[...], _NT, preferred_element_type=F32))

        @pl.when((i == 0) & (f == 0))
        def _():
            dw_ref[...] = jnp.zeros_like(dw_ref)

        @pl.when(f == NFF - 1)
        def _():
            _, vjp = jax.vjp(_norm_mod, x_ref[...], w2_ref[...], sh_ref[...], sc_ref[...])
            dx, dw, dsh, dsc = vjp(acc_ref[...])
            dx_ref[...] = dy_ref[...] + dx
            dw_ref[...] += dw

            @pl.when(i % TPE == 0)
            def _():
                dsh_ref[...] = dsh
                dsc_ref[...] = dsc

            @pl.when(i % TPE != 0)
            def _():
                dsh_ref[...] += dsh
                dsc_ref[...] += dsc

    row, ex, one, wg, wu, wo, ff = _ffn_specs()
    return pl.pallas_call(
        body, name="ffn_bwd", grid=(NFT, NFF),
        out_shape=[jax.ShapeDtypeStruct((NTOK, D), BF16), jax.ShapeDtypeStruct((NTOK, DFF), BF16),
                   jax.ShapeDtypeStruct((2, NTOK, DFF), BF16), jax.ShapeDtypeStruct((NTOK, D), F32),
                   jax.ShapeDtypeStruct((NB, 1, D), F32), jax.ShapeDtypeStruct((NB, 1, D), F32),
                   jax.ShapeDtypeStruct((1, D), F32)],
        in_specs=[row(D), ff, ff, row(D), ex, ex, ex, one, wg, wu, wo],
        out_specs=[row(D), ff, pl.BlockSpec((2, FT, FFT), lambda i, f: (0, i, f)), row(D), ex, ex, one],
        scratch_shapes=[pltpu.VMEM((FT, D), F32)],
        compiler_params=_cp(vmem=VMEM_BIG),
    )(dy, gate, up, x1, sh2, sc2, g2, n2w, w_in4, w_in4, w_out)


def _local_step(x, ctx, target, mod_x, mod_c, norm1_w, norm2_w, w_in_p, conv_w, a_log, dt_bias, out_norm_w,
                q_norm_w, k_norm_w, rpb, w_out_b, w_ffn_in4, w_ffn_out_b, scatter_early=False, gather_late=None):
    sh1, sc1, g1, sh2, sc2, g2 = [mod_x[:, i * D:(i + 1) * D].reshape(NB, 1, D) for i in range(6)]
    csh1, csc1 = mod_c[0:D].reshape(1, 1, D), mod_c[D:2 * D].reshape(1, 1, D)
    cw = jnp.concatenate([conv_w, jnp.zeros((3, 1536), F32)], axis=0)
    prm = jnp.zeros((8, 128), F32).at[0, :8].set(a_log.reshape(8)).at[1, :8].set(dt_bias.reshape(8))
    onw = jnp.tile(out_norm_w.reshape(1, 128), (1, DNH))
    qw = jnp.tile(q_norm_w.reshape(1, 64), (1, NAH))
    kw = jnp.tile(k_norm_w.reshape(1, 64), (1, NAH))
    seg = np.arange(NAW) // 64
    bd = jnp.asarray((seg[:, None] == seg[None, :]).astype(np.float32) / 64.0, dtype=BF16)
    bias = _bias_table(rpb)

    p, hm = _inproj_fwd(x, ctx, sh1, sc1, csh1, csc1, norm1_w, w_in_p)
    dq_, dk_, dv_ = _dn_prep_fwd(p, cw)
    gates = _gates_fwd(p, prm)
    res = _scan_fwd(dq_, dk_, dv_, gates, carry=gather_late[:2] if gather_late else ())
    o_f, st_f, inv_f, o_b, st_b, inv_b = res[:6]
    qn, kn, vb = _na_prep_fwd(p, qw, kw, bd)
    na, got = _na_fwd(qn, kn, vb, bias, carry=gather_late[2:] if gather_late else ())
    if gather_late:
        w_out_b = res[6].reshape(D, D)
        w_ffn_in4 = res[7].reshape(4, D, FFT)
        w_ffn_out_b = got[0].reshape(DFF, D)
    x1, ai, ao = _outproj_fwd(o_f, o_b, p, na, onw, x, g1, w_out_b)
    x1f = x1.reshape(NTOK, D)
    h2, gate, up, dy, dg2, loss8 = _ffn_fwd(x1f, sh2, sc2, g2, norm2_w, w_ffn_in4, w_ffn_out_b,
                                             target.reshape(NTOK, D))
    dff, act, dgu, dx1, dsh2, dsc2, dn2w = _ffn_bwd(dy, gate, up, x1f, sh2, sc2, g2, norm2_w, w_ffn_in4, w_ffn_out_b)
    g_w_ffn_out = _matmul_tn(act, dff, FFT, D, 512, "wgrad_ffn_out")
    g_w_ffn_in4 = _matmul_tn(
        h2, dgu, 512, FFT, 512, "wgrad_ffn_in", nn=2 * NFF,
        b_block=(None, 512, FFT), b_map=lambda i, j, k: (j // NFF, k, j % NFF),
        out_shape=(4, D, FFT), out_block=(None, 512, FFT), out_map=lambda i, j, k: (j, i, 0))
    dao, do, dz, dna, dg1, donw = _outproj_bwd(dx1.reshape(NB, SEQ, D), ao, g1, w_out_b, o_f, o_b, p, onw)
    g_w_out = _matmul_tn(ai.reshape(NTOK, D), dao.reshape(NTOK, D), 512, D, 512, "wgrad_out")
    dqn, dkn, dvn, dbias = _na_bwd(qn, kn, vb, bias, dna)
    dp_na, dqw, dkw = _na_prep_bwd(p, qw, kw, bd, dqn, dkn, dvn)
    early = [g_w_out.reshape(8, D // 8, D), g_w_ffn_in4.reshape(8, D // 2, FFT), g_w_ffn_out.reshape(8, DFF // 8, D)]
    res = _scan_bwd(dq_, dk_, dv_, gates, st_f, st_b, inv_f, inv_b, do, carry=early if scatter_early else ())
    dqf, dkf, dvf, dgf, dqb, dkb, dvb, dgb = res[:8]
    if scatter_early:
        g_w_out, g_w_ffn_in4, g_w_ffn_out = res[8:]
    dpq, dpk, dpv, gcq, gck, gcv = _dn_prep_bwd(p, cw, (dqf, dqb), (dkf, dkb), (dvf, dvb))
    dgt, dprm = _gates_bwd(p, prm, dgf, dgb)
    dpb, grad_x, dsh1, dsc1, dcsh1, dcsc1, dn1w = _inproj_bwd(
        (dpq, dpk, dpv), dz, dp_na, dgt, w_in_p, x, ctx, sh1, sc1, csh1, csc1, norm1_w, dx1.reshape(NB, SEQ, D))
    g_w_in_p = _matmul_tn(dpb.reshape(NB * T, PC), hm.reshape(NB * T, D), PC, D, 512, "wgrad_in")
    g_rpb = _rpb_fold(dbias)

    dmod_x = jnp.concatenate([dsh1, dsc1, dg1, dsh2, dsc2, dg2], axis=2).reshape(NB, 6 * D)
    dmod_c = jnp.concatenate([dcsh1.reshape(D), dcsc1.reshape(D), jnp.zeros((4 * D,), F32)])
    g_conv = jnp.concatenate([gcq[:5], gck[:5], gcv[:5]], axis=1)
    small = dict(
        norm1_w=dn1w, norm2_w=dn2w, a_log=dprm[0, :8], dt_bias=dprm[1, :8],
        out_norm_w=donw[0].reshape(DNH, 128).sum(0), q_norm_w=dqw[0].reshape(NAH, 64).sum(0),
        k_norm_w=dkw[0].reshape(NAH, 64).sum(0), rpb=g_rpb, conv_w=g_conv)
    return loss8[0, 0], grad_x, dmod_x, dmod_c, small, g_w_in_p, g_w_out, g_w_ffn_in4, g_w_ffn_out


def _perm_w_in(wt):
    return jnp.concatenate([wt[:2048], wt[2064:3600], wt[2048:2064], jnp.zeros((PC - IN_COLS, wt.shape[1]), wt.dtype)],
                           axis=0)


def _unperm_w_in(gt):
    return jnp.concatenate([gt[:2048], gt[COL_G:COL_G + 16], gt[2048:COL_G]], axis=0)


def _pack_small(c_ctx, n1, n2, b_ada, a_log, dt_bias, onw, qnw, knw, rpb, conv_s):
    misc = jnp.zeros((D,), F32)
    misc = misc.at[0:8].set(a_log.reshape(8)).at[8:16].set(dt_bias.reshape(8))
    misc = misc.at[128:256].set(onw.reshape(128)).at[256:320].set(qnw.reshape(64)).at[320:384].set(knw.reshape(64))
    rp = jnp.concatenate([rpb.reshape(-1), jnp.zeros((4 * D - 3720,), F32)])
    cv = jnp.concatenate([conv_s.reshape(-1), jnp.zeros((2 * D - 1920,), F32)])
    return jnp.concatenate([c_ctx.reshape(-1), n1.reshape(-1), n2.reshape(-1), b_ada.reshape(-1), misc, rp, cv]).reshape(16, D)


def _unpack_small(a):
    f = a.reshape(-1)
    misc = f[9 * D:10 * D]
    return dict(
        c_ctx=f[0:D], norm1_w=f[D:2 * D].reshape(1, D), norm2_w=f[2 * D:3 * D].reshape(1, D),
        b_ada=f[3 * D:9 * D].reshape(1, 6 * D), dn_A_log=misc[0:8].reshape(1, 2, 4), dn_dt_bias=misc[8:16].reshape(1, 2, 4),
        dn_out_norm_w=misc[128:256].reshape(1, 128), na_q_norm_w=misc[256:320].reshape(1, 64),
        na_k_norm_w=misc[320:384].reshape(1, 64), na_rpb=f[10 * D:10 * D + 3720].reshape(1, 8, 15, 31),
        dn_conv_w=f[14 * D:14 * D + 1920].reshape(1, 5, 384))


_WEIGHTS = ['c_ctx', 'norm1_w', 'norm2_w', 'w_ada', 'b_ada', 'w_in', 'dn_conv_w', 'dn_A_log', 'dn_dt_bias',
            'dn_out_norm_w', 'na_q_norm_w', 'na_k_norm_w', 'na_rpb', 'w_out', 'w_ffn_in', 'w_ffn_out']


def kernel(x, c, ctx, c_ctx, norm1_w, norm2_w, w_ada, b_ada, w_in, dn_conv_w, dn_A_log, dn_dt_bias, dn_out_norm_w, na_q_norm_w, na_k_norm_w, na_rpb, w_out, w_ffn_in, w_ffn_out, loss_target, m_c_ctx, m_norm1_w, m_norm2_w, m_w_ada, m_b_ada, m_w_in, m_dn_conv_w, m_dn_A_log, m_dn_dt_bias, m_dn_out_norm_w, m_na_q_norm_w, m_na_k_norm_w, m_na_rpb, m_w_out, m_w_ffn_in, m_w_ffn_out, v_c_ctx, v_norm1_w, v_norm2_w, v_w_ada, v_b_ada, v_w_in, v_dn_conv_w, v_dn_A_log, v_dn_dt_bias, v_dn_out_norm_w, v_na_q_norm_w, v_na_k_norm_w, v_na_rpb, v_w_out, v_w_ffn_in, v_w_ffn_out):
    ix, iy, ic = lax.axis_index("x"), lax.axis_index("y"), lax.axis_index("c")
    chip = 2 * ix + iy
    me = 4 * ix + 2 * iy + ic

    crow = jnp.concatenate([c, c_ctx.reshape(1, D), jnp.zeros((5, D), F32)], axis=0)
    cvrow = jnp.concatenate([dn_conv_w[0].reshape(-1), jnp.zeros((2 * D - 1920,), F32)]).reshape(2, D)
    g0 = _allgather8(jnp.concatenate([crow, cvrow, jnp.zeros((6, D), F32)], axis=0), "gather_cond")
    cin = g0[:, :8].reshape(64, D)
    conv_full = jnp.concatenate([g0[2 * s, 8:10].reshape(-1)[:1920].reshape(5, 384) for s in range(4)], axis=1)

    w_ada_s = w_ada[0]
    b_s = lax.dynamic_slice(b_ada, (0, chip * 1536), (1, 1536))
    mod_s = _ada_fwd(cin, w_ada_s, b_s)
    mods = _gather_rows(mod_s, "gather_mod")
    mod_mine = jnp.concatenate([mods[0], mods[2], mods[4], mods[6]], axis=1)
    mod_x, mod_c = mod_mine[0:NB], mod_mine[2]

    def half(w2d):
        r = w2d.shape[0] // 2
        return lax.dynamic_slice(w2d, (ic * r, 0), (r, w2d.shape[1]))

    w_in_t = w_in[0].T
    (gw_in,) = _gather_weights([_cast_bf16(lax.dynamic_slice(w_in_t, (0, ic * (D // 2)), (900, D // 2)), "cast_w_in")])
    late = [_cast_bf16(half(w_out[0]), "cast_w_out"), _cast_bf16(half(w_ffn_in[0]), "cast_w_ffn_in"),
            _cast_bf16(half(w_ffn_out[0]), "cast_w_ffn_out")]
    w_in_p = _perm_w_in(gw_in.reshape(4, 2, 900, D // 2).transpose(0, 2, 1, 3).reshape(IN_COLS, D))

    (loss_part, grad_x, dmod_x, dmod_c, small, g_w_in_p, r_w_out, r_w_ffn_in, r_w_ffn_out) = _local_step(
        x, ctx, loss_target, mod_x, mod_c, norm1_w, norm2_w, w_in_p, conv_full, dn_A_log[0], dn_dt_bias[0],
        dn_out_norm_w[0], na_q_norm_w[0], na_k_norm_w[0], na_rpb[0], None, None, None,
        scatter_early=True, gather_late=late)
    loss = lax.psum(loss_part, ("x", "y", "c"))

    dm = jnp.concatenate([dmod_x, dmod_c.reshape(1, 6 * D), jnp.zeros((5, 6 * D), F32)], axis=0).reshape(48, D)
    misc = jnp.zeros((D,), F32)
    misc = misc.at[0:8].set(small["a_log"]).at[8:16].set(small["dt_bias"]).at[128:256].set(small["out_norm_w"])
    misc = misc.at[256:320].set(small["q_norm_w"]).at[320:384].set(small["k_norm_w"])
    sm = jnp.concatenate([
        small["norm1_w"].reshape(-1), small["norm2_w"].reshape(-1), misc,
        small["rpb"].reshape(-1), jnp.zeros((4 * D - 3720,), F32),
        small["conv_w"].reshape(-1), jnp.zeros((8 * D - 7680,), F32), jnp.zeros((D,), F32)]).reshape(16, D)
    gsl, gtot = _allgather8(jnp.concatenate([dm, sm], axis=0), "gather_small", with_sum=True)
    gmod = gsl[:, :48].reshape(64, 6 * D)
    st = gtot[48:64].reshape(-1)
    g_n1, g_n2, g_misc = st[0:D].reshape(1, D), st[D:2 * D].reshape(1, D), st[2 * D:3 * D]
    g_rpb = st[3 * D:3 * D + 3720].reshape(1, 8, 15, 31)
    g_conv_full = st[7 * D:7 * D + 7680].reshape(5, 1536)
    g_conv_s = lax.dynamic_slice(g_conv_full, (0, chip * 384), (5, 384)).reshape(1, 5, 384)

    gmod_s = lax.dynamic_slice(gmod, (0, chip * 1536), (64, 1536))
    g_w_ada_s, pctx = _ada_bwd(cin, gmod_s, w_ada_s)
    parts = _allgather8(pctx, "gather_cctx")
    g_b_ada, g_c_ctx = _ada_small(gmod, parts, c_ctx.reshape(1, D))

    g_w_in8 = _unperm_w_in(g_w_in_p).reshape(4, 900, 2, D // 2).transpose(0, 2, 1, 3).reshape(8, 900, D // 2)
    names = ["w_in", "w_out", "w_ffn_in", "w_ffn_out"]
    rs = list(_scatter_grads([g_w_in8])) + [r_w_out, r_w_ffn_in, r_w_ffn_out]
    full = _sibling_exchange([_sum_slots(r, "sum_" + nm) for r, nm in zip(rs, names)])
    g_w_in_t = full[0].transpose(1, 0, 2).reshape(900, D)
    g_w_out_s = full[1].reshape(1, D // 4, D)
    g_w_ffn_in_s = full[2].reshape(1, D, FFT)
    g_w_ffn_out_s = full[3].reshape(1, DFF // 4, D)

    grads = dict(
        c_ctx=g_c_ctx.reshape(D), norm1_w=g_n1, norm2_w=g_n2, w_ada=g_w_ada_s.reshape(1, D, 1536), b_ada=g_b_ada,
        w_in=g_w_in_t, dn_conv_w=g_conv_s, dn_A_log=g_misc[0:8].reshape(1, 2, 4), dn_dt_bias=g_misc[8:16].reshape(1, 2, 4),
        dn_out_norm_w=g_misc[128:256].reshape(1, 128), na_q_norm_w=g_misc[256:320].reshape(1, 64),
        na_k_norm_w=g_misc[320:384].reshape(1, 64), na_rpb=g_rpb, w_out=g_w_out_s, w_ffn_in=g_w_ffn_in_s,
        w_ffn_out=g_w_ffn_out_s)
    pk = lambda d_: _pack_small(d_["c_ctx"], d_["norm1_w"], d_["norm2_w"], d_["b_ada"], d_["dn_A_log"], d_["dn_dt_bias"],
                                d_["dn_out_norm_w"], d_["na_q_norm_w"], d_["na_k_norm_w"], d_["na_rpb"], d_["dn_conv_w"])
    wts = dict(c_ctx=c_ctx, norm1_w=norm1_w, norm2_w=norm2_w, b_ada=b_ada, dn_A_log=dn_A_log, dn_dt_bias=dn_dt_bias,
               dn_out_norm_w=dn_out_norm_w, na_q_norm_w=na_q_norm_w, na_k_norm_w=na_k_norm_w, na_rpb=na_rpb, dn_conv_w=dn_conv_w)
    ms = dict(c_ctx=m_c_ctx, norm1_w=m_norm1_w, norm2_w=m_norm2_w, b_ada=m_b_ada, dn_A_log=m_dn_A_log, dn_dt_bias=m_dn_dt_bias,
              dn_out_norm_w=m_dn_out_norm_w, na_q_norm_w=m_na_q_norm_w, na_k_norm_w=m_na_k_norm_w, na_rpb=m_na_rpb,
              dn_conv_w=m_dn_conv_w)
    vs = dict(c_ctx=v_c_ctx, norm1_w=v_norm1_w, norm2_w=v_norm2_w, b_ada=v_b_ada, dn_A_log=v_dn_A_log, dn_dt_bias=v_dn_dt_bias,
              dn_out_norm_w=v_dn_out_norm_w, na_q_norm_w=v_na_q_norm_w, na_k_norm_w=v_na_k_norm_w, na_rpb=v_na_rpb,
              dn_conv_w=v_dn_conv_w)
    sd, snm, snv, _ = _adamw(pk(wts), pk(grads), pk(ms), pk(vs), "adamw_small")
    delta, new_m, new_v = _unpack_small(sd), _unpack_small(snm), _unpack_small(snv)
    big = dict(w_ada=(w_ada, m_w_ada, v_w_ada), w_in=(w_in, m_w_in, v_w_in), w_out=(w_out, m_w_out, v_w_out),
               w_ffn_in=(w_ffn_in, m_w_ffn_in, v_w_ffn_in), w_ffn_out=(w_ffn_out, m_w_ffn_out, v_w_ffn_out))
    for nm, (w_, m_, v_) in big.items():
        shp = w_.shape
        if nm == "w_in":
            r2, back = (lambda a: a[0].T), (lambda a: a.T.reshape(shp))
            res = _adamw(w_in_t, grads[nm], r2(m_), r2(v_), "adamw_" + nm)
        else:
            r2, back = (lambda a: a.reshape(shp[1], shp[2])), (lambda a: a.reshape(shp))
            res = _adamw(r2(w_), r2(grads[nm]), r2(m_), r2(v_), "adamw_" + nm)
        delta[nm], new_m[nm], new_v[nm], grads[nm] = [back(a) for a in res]

    return (loss, grad_x, *[grads[n] for n in _WEIGHTS], *[delta[n] for n in _WEIGHTS],
            *[new_m[n] for n in _WEIGHTS], *[new_v[n] for n in _WEIGHTS])
```

```python
import functools

import numpy as np
import jax
import jax.numpy as jnp
from jax import lax
from jax.experimental import pallas as pl
from jax.experimental.pallas import tpu as pltpu

F32 = jnp.float32
BF16 = jnp.bfloat16

D = 1024
NB = 2
SEQ = 2048
CT = 256
T = CT + SEQ
NTOK = NB * SEQ
CH = 64
NCH = T // CH
NCH_CTX = CT // CH
DNH, DNW = 4, 512
NAH, NAW = 8, 512
GRID_W = 64
ROWS = SEQ // GRID_W
WIN_ROWS = 8
WIN_COLS = 16
NWIN = WIN_ROWS * GRID_W
DFF = 2816
FFT = 1408
IN_COLS = 3600
PC = 3712
COL_Z, COL_NA, COL_G = 1536, 2048, 3584
EPS = 1e-6
NEG = -1e30
RT = 256
VMEM_BIG = 56 * 1024 * 1024
MESH = pl.DeviceIdType.MESH

ADAM_LR, ADAM_B1, ADAM_B2, ADAM_EPS, ADAM_WD, ADAM_STEP = 0.001, 0.9, 0.999, 1e-08, 0.01, 10


def _cp(vmem=None, sem=None):
    kw = {}
    if vmem is not None:
        kw["vmem_limit_bytes"] = vmem
    if sem is not None:
        kw["dimension_semantics"] = sem
    return pltpu.CompilerParams(**kw)


_NN = (((1,), (0,)), ((), ()))
_NT = (((1,), (1,)), ((), ()))
_TN = (((0,), (0,)), ((), ()))


def _dg(a, b, dims):
    return lax.dot_general(a.astype(BF16), b.astype(BF16), dims, preferred_element_type=F32)


@jax.custom_vjp
def _mm(a, b):
    return _dg(a, b, _NN)


_mm.defvjp(lambda a, b: (_dg(a, b, _NN), (a, b)),
           lambda r, g: (_dg(g, r[1], _NT), _dg(r[0], g, _TN)))


@jax.custom_vjp
def _mm_nt(a, b):
    return _dg(a, b, _NT)


_mm_nt.defvjp(lambda a, b: (_dg(a, b, _NT), (a, b)),
              lambda r, g: (_dg(g, r[1], _NN), _dg(g, r[0], _TN)))


@jax.custom_vjp
def _mm_tn(a, b):
    return _dg(a, b, _TN)


_mm_tn.defvjp(lambda a, b: (_dg(a, b, _TN), (a, b)),
              lambda r, g: (_dg(r[1], g, _NT), _dg(r[0], g, _NN)))


def _split_dot(a, b, dims):
    hi = b.astype(BF16)
    lo = (b - hi.astype(F32)).astype(BF16)
    ab = a.astype(BF16)
    return (lax.dot_general(ab, hi, dims, preferred_element_type=F32)
            + lax.dot_general(ab, lo, dims, preferred_element_type=F32))


@jax.custom_vjp
def _mask_mm(mask, b):
    return _split_dot(mask, b, _NN)


_mask_mm.defvjp(lambda mask, b: (_split_dot(mask, b, _NN), mask),
                lambda mask, g: (jnp.zeros_like(mask), _split_dot(mask, g, _TN)))


def _hmm(a, b):
    return jnp.dot(a, b, precision=lax.Precision.HIGHEST, preferred_element_type=F32)


def _sigmoid(x):
    return 1.0 / (1.0 + jnp.exp(-x))


def _silu(x):
    return x * _sigmoid(x)


def _norm_mod(x, w, sh, sc):
    r = lax.rsqrt(jnp.mean(x * x, axis=-1, keepdims=True) + EPS)
    return (x * r * w) * (1.0 + sc) + sh


def _my_pos():
    return lax.axis_index("x"), lax.axis_index("y"), lax.axis_index("c")


def _flip(pos, k):
    x, y, c = pos
    return (1 - x if k & 4 else x, 1 - y if k & 2 else y, 1 - c if k & 1 else c)


def _slot(pos):
    return 4 * pos[0] + 2 * pos[1] + pos[2]


def _allgather8(v, name, with_sum=False):
    R, W = v.shape

    def body(v_ref, out_ref, *rest):
        if with_sum:
            tot_ref, send_sems, recv_sems, lsem = rest
        else:
            send_sems, recv_sems, lsem = rest
        me = _my_pos()
        mine = pltpu.make_async_copy(v_ref, out_ref.at[_slot(me)], lsem)
        mine.start()
        sends = []
        for k in range(1, 8):
            cp = pltpu.make_async_remote_copy(
                src_ref=v_ref, dst_ref=out_ref.at[_slot(me)], send_sem=send_sems.at[k - 1],
                recv_sem=recv_sems.at[k - 1], device_id=_flip(me, k), device_id_type=MESH)
            cp.start()
            sends.append(cp)
        for k in range(1, 8):
            peer = _flip(me, k)
            pltpu.make_async_remote_copy(
                src_ref=v_ref, dst_ref=out_ref.at[_slot(peer)], send_sem=send_sems.at[k - 1],
                recv_sem=recv_sems.at[k - 1], device_id=peer, device_id_type=MESH).wait_recv()
        for cp in sends:
            cp.wait_send()
        mine.wait()
        if with_sum:
            acc = out_ref[0]
            for s in range(1, 8):
                acc = acc + out_ref[s]
            tot_ref[...] = acc

    out_shape = [jax.ShapeDtypeStruct((8, R, W), F32)]
    out_specs = [pl.BlockSpec(memory_space=pltpu.VMEM)]
    if with_sum:
        out_shape.append(jax.ShapeDtypeStruct((R, W), F32))
        out_specs.append(pl.BlockSpec(memory_space=pltpu.VMEM))
    res = pl.pallas_call(
        body, name=name, out_shape=out_shape,
        in_specs=[pl.BlockSpec(memory_space=pltpu.VMEM)], out_specs=out_specs,
        scratch_shapes=[pltpu.SemaphoreType.DMA((7,)), pltpu.SemaphoreType.DMA((7,)), pltpu.SemaphoreType.DMA],
        compiler_params=_cp(vmem=VMEM_BIG),
    )(v)
    return res if with_sum else res[0]


def _gather_rows(v, name):
    W = v.shape[1]

    def body(v_ref, out_ref, send_sems, recv_sems, lsem):
        pos = _my_pos()
        me = _slot(pos)
        rows = lambda p: v_ref.at[pl.ds(pl.multiple_of(8 * p, 8), 8)]
        locals_ = [(rows(me), out_ref.at[me])]
        xfers = [(rows(_slot(_flip(pos, k))), out_ref.at[me], out_ref.at[_slot(_flip(pos, k))], _flip(pos, k))
                 for k in range(1, 8)]
        _xfer_start((locals_, xfers), (send_sems, recv_sems, lsem))
        _xfer_wait((locals_, xfers), (send_sems, recv_sems, lsem))

    vm = pl.BlockSpec(memory_space=pltpu.VMEM)
    return pl.pallas_call(
        body, name=name, out_shape=jax.ShapeDtypeStruct((8, 8, W), F32), in_specs=[vm], out_specs=vm,
        scratch_shapes=_xfer_sems(1, 7),
    )(v)


def _gather_small_vmem(h):
    def body(h_ref, out_ref, send_sems, recv_sems, lsem):
        plan = _gather_plan([h_ref], [out_ref], _my_pos())
        _xfer_start(plan, (send_sems, recv_sems, lsem))
        _xfer_wait(plan, (send_sems, recv_sems, lsem))

    vm = pl.BlockSpec(memory_space=pltpu.VMEM)
    return pl.pallas_call(
        body, name="gather_w_in", out_shape=jax.ShapeDtypeStruct((8,) + h.shape, h.dtype), in_specs=[vm], out_specs=vm,
        scratch_shapes=_xfer_sems(1, 7), compiler_params=_cp(vmem=VMEM_BIG),
    )(h)


def _gather_weights(halves):
    n = len(halves)

    def body(*refs):
        ins, outs = refs[:n], refs[n:2 * n]
        send_sems, recv_sems, lsems = refs[2 * n:]
        me = _my_pos()
        x, y, c = me
        sib = (x, y, 1 - c)
        chips = [4, 2, 6]

        def cp(a, k, block, to, src=None):
            dst = outs[a].at[_slot(block)]
            return pltpu.make_async_remote_copy(
                src_ref=dst if src is None else src, dst_ref=dst, send_sem=send_sems.at[a, k],
                recv_sem=recv_sems.at[a, k], device_id=to, device_id_type=MESH)

        mine, first, passed = [], [], []
        for a in range(n):
            m = pltpu.make_async_copy(ins[a], outs[a].at[_slot(me)], lsems.at[a])
            m.start()
            mine.append(m)
            first.append(cp(a, 0, me, sib, src=ins[a]))
            for j, k in enumerate(chips):
                first.append(cp(a, 1 + j, me, _flip(me, k), src=ins[a]))
        for f in first:
            f.start()
        for j, k in enumerate(chips):
            for a in range(n):
                cp(a, 1 + j, _flip(me, k), me).wait_recv()
                p = cp(a, 4 + j, _flip(me, k), sib)
                p.start()
                passed.append(p)
        for a in range(n):
            cp(a, 0, sib, me).wait_recv()
            for j, k in enumerate(chips):
                cp(a, 4 + j, _flip(sib, k), me).wait_recv()
        for f in first + passed:
            f.wait_send()
        for m in mine:
            m.wait()

    vm = pl.BlockSpec(memory_space=pltpu.VMEM)
    return pl.pallas_call(
        body, name="gather_weights",
        out_shape=[jax.ShapeDtypeStruct((8,) + h.shape, h.dtype) for h in halves],
        in_specs=[vm] * n, out_specs=[vm] * n,
        scratch_shapes=[pltpu.SemaphoreType.DMA((n, 7)), pltpu.SemaphoreType.DMA((n, 7)),
                        pltpu.SemaphoreType.DMA((n,))],
        compiler_params=_cp(vmem=VMEM_BIG),
    )(*halves)


def _xfer_start(plan, sems):
    send_sems, recv_sems, lsems = sems
    locals_, xfers = plan
    for i, (src, dst) in enumerate(locals_):
        pltpu.make_async_copy(src, dst, lsems.at[i]).start()
    for i, (src, dst, _, peer) in enumerate(xfers):
        pltpu.make_async_remote_copy(src_ref=src, dst_ref=dst, send_sem=send_sems.at[i], recv_sem=recv_sems.at[i],
                                     device_id=peer, device_id_type=MESH).start()


def _xfer_wait(plan, sems):
    send_sems, recv_sems, lsems = sems
    locals_, xfers = plan
    for i, (src, _, landing, peer) in enumerate(xfers):
        pltpu.make_async_remote_copy(src_ref=src, dst_ref=landing, send_sem=send_sems.at[i], recv_sem=recv_sems.at[i],
                                     device_id=peer, device_id_type=MESH).wait_recv()
    for i, (src, dst, _, peer) in enumerate(xfers):
        pltpu.make_async_remote_copy(src_ref=src, dst_ref=dst, send_sem=send_sems.at[i], recv_sem=recv_sems.at[i],
                                     device_id=peer, device_id_type=MESH).wait_send()
    for i, (src, dst) in enumerate(locals_):
        pltpu.make_async_copy(src, dst, lsems.at[i]).wait()


def _xfer_sems(n_local, n_xfer):
    return [pltpu.SemaphoreType.DMA((n_xfer,)), pltpu.SemaphoreType.DMA((n_xfer,)), pltpu.SemaphoreType.DMA((n_local,))]


def _scatter_plan(ins, outs, pos):
    me = _slot(pos)
    locals_, xfers = [], []
    for src, dst in zip(ins, outs):
        locals_.append((src.at[me], dst.at[me]))
        for k in range(1, 8):
            peer = _flip(pos, k)
            xfers.append((src.at[_slot(peer)], dst.at[me], dst.at[_slot(peer)], peer))
    return locals_, xfers


def _scatter_grads(grads):
    n = len(grads)

    def body(*refs):
        plan = _scatter_plan(refs[:n], refs[n:2 * n], _my_pos())
        _xfer_start(plan, refs[2 * n:])
        _xfer_wait(plan, refs[2 * n:])

    hbm = pl.BlockSpec(memory_space=pltpu.HBM)
    return pl.pallas_call(
        body, name="scatter_grads", out_shape=[jax.ShapeDtypeStruct(g.shape, g.dtype) for g in grads],
        in_specs=[hbm] * n, out_specs=[hbm] * n, scratch_shapes=_xfer_sems(n, 7 * n),
    )(*grads)


def _sibling_exchange(halves):
    n = len(halves)

    def body(*refs):
        ins, outs = refs[:n], refs[n:2 * n]
        send_sems, recv_sems, lsems = refs[2 * n:]
        x, y, c = _my_pos()
        sib = (x, y, 1 - c)
        mine, sends = [], []
        for a in range(n):
            m = pltpu.make_async_copy(ins[a], outs[a].at[c], lsems.at[a])
            m.start()
            mine.append(m)
            cp = pltpu.make_async_remote_copy(
                src_ref=ins[a], dst_ref=outs[a].at[c], send_sem=send_sems.at[a], recv_sem=recv_sems.at[a],
                device_id=sib, device_id_type=MESH)
            cp.start()
            sends.append(cp)
        for a in range(n):
            pltpu.make_async_remote_copy(
                src_ref=ins[a], dst_ref=outs[a].at[1 - c], send_sem=send_sems.at[a], recv_sem=recv_sems.at[a],
                device_id=sib, device_id_type=MESH).wait_recv()
        for cp in sends:
            cp.wait_send()
        for m in mine:
            m.wait()

    vm = pl.BlockSpec(memory_space=pltpu.VMEM)
    return pl.pallas_call(
        body, name="sibling_exchange",
        out_shape=[jax.ShapeDtypeStruct((2,) + h.shape, h.dtype) for h in halves],
        in_specs=[vm] * n, out_specs=[vm] * n,
        scratch_shapes=[pltpu.SemaphoreType.DMA((n,)), pltpu.SemaphoreType.DMA((n,)), pltpu.SemaphoreType.DMA((n,))],
        compiler_params=_cp(vmem=VMEM_BIG),
    )(*halves)


def _row_tile(r, cap=512):
    for t in (cap, 256, 128, 64, 32, 16, 8):
        if t <= cap and r % t == 0:
            return t
    return r


def _cast_bf16(a, name):
    R, W = a.shape
    tr = _row_tile(R, 256)

    def body(a_ref, o_ref):
        o_ref[...] = a_ref[...].astype(BF16)

    return pl.pallas_call(
        body, name=name, grid=(R // tr,), out_shape=jax.ShapeDtypeStruct((R, W), BF16),
        in_specs=[pl.BlockSpec((tr, W), lambda i: (i, 0))], out_specs=pl.BlockSpec((tr, W), lambda i: (i, 0)),
    )(a)


def _sum_slots(r, name):
    _, R, W = r.shape
    tr = _row_tile(R, 128)

    def body(r_ref, o_ref):
        acc = r_ref[0].astype(F32)
        for s_ in range(1, 8):
            acc = acc + r_ref[s_].astype(F32)
        o_ref[...] = acc.astype(BF16)

    return pl.pallas_call(
        body, name=name, grid=(R // tr,), out_shape=jax.ShapeDtypeStruct((R, W), BF16),
        in_specs=[pl.BlockSpec((8, tr, W), lambda i: (0, i, 0))], out_specs=pl.BlockSpec((tr, W), lambda i: (i, 0)),
        compiler_params=_cp(vmem=VMEM_BIG),
    )(r)


def _adamw(w, g, m, v, name):
    R, W = w.shape
    tr = _row_tile(R, 256)
    tc = 256 if (tr == R and R * W * 4 > (1 << 21) and W % 256 == 0) else W
    c1 = 1.0 / (1.0 - ADAM_B1 ** ADAM_STEP)
    c2 = 1.0 / (1.0 - ADAM_B2 ** ADAM_STEP)

    def body(w_ref, g_ref, m_ref, v_ref, d_ref, nm_ref, nv_ref, g32_ref):
        gg = g_ref[...].astype(F32)
        g32_ref[...] = gg
        nm = ADAM_B1 * m_ref[...] + (1.0 - ADAM_B1) * gg
        nv = ADAM_B2 * v_ref[...] + (1.0 - ADAM_B2) * (gg * gg)
        d_ref[...] = -ADAM_LR * ((nm * c1) / (jnp.sqrt(nv * c2) + ADAM_EPS) + ADAM_WD * w_ref[...])
        nm_ref[...] = nm
        nv_ref[...] = nv

    spec = pl.BlockSpec((tr, tc), lambda i, j: (i, j))
    return pl.pallas_call(
        body, name=name, grid=(R // tr, W // tc), out_shape=[jax.ShapeDtypeStruct((R, W), F32)] * 4,
        in_specs=[spec] * 4, out_specs=[spec] * 4, compiler_params=_cp(vmem=VMEM_BIG),
    )(w, g, m, v)


def _matmul_tn(a, b, tm, tn, tk, name, nn=1, b_map=None, out_shape=None, out_block=None, out_map=None,
               b_block=None):
    K, M = a.shape
    nk = K // tk
    nm = M // tm

    def body(a_ref, b_ref, o_ref, acc_ref):
        kk = pl.program_id(2)

        @pl.when(kk == 0)
        def _():
            acc_ref[...] = jnp.zeros_like(acc_ref)

        acc_ref[...] += lax.dot_general(a_ref[...], b_ref[...], _TN, preferred_element_type=F32)

        @pl.when(kk == nk - 1)
        def _():
            o_ref[...] = acc_ref[...].astype(BF16)

    return pl.pallas_call(
        body, name=name, grid=(nm, nn, nk),
        out_shape=jax.ShapeDtypeStruct(out_shape or (M, b.shape[-1]), BF16),
        in_specs=[pl.BlockSpec((tk, tm), lambda i, j, k: (k, i)),
                  pl.BlockSpec(b_block or (tk, tn), b_map or (lambda i, j, k: (k, j)))],
        out_specs=pl.BlockSpec(out_block or (tm, tn), out_map or (lambda i, j, k: (i, j))),
        scratch_shapes=[pltpu.VMEM((tm, tn), F32)],
        compiler_params=_cp(vmem=VMEM_BIG),
    )(a, b)


def _ada_fwd(cin, w_ada_s, b_s):
    tn = 512

    def body(c_ref, w_ref, b_ref, o_ref):
        o_ref[...] = _dg(_silu(c_ref[...]), w_ref[...], _NN) + b_ref[...]

    return pl.pallas_call(
        body, name="ada_fwd", grid=(1536 // tn,), out_shape=jax.ShapeDtypeStruct((64, 1536), F32),
        in_specs=[pl.BlockSpec((64, D), lambda j: (0, 0)), pl.BlockSpec((D, tn), lambda j: (0, j)),
                  pl.BlockSpec((1, tn), lambda j: (0, j))],
        out_specs=pl.BlockSpec((64, tn), lambda j: (0, j)),
        compiler_params=_cp(vmem=VMEM_BIG),
    )(cin, w_ada_s, b_s)


def _ada_bwd(cin, gmod_s, w_ada_s):
    tn = 512

    def body(c_ref, g_ref, w_ref, gw_ref, pc_ref):
        j = pl.program_id(0)
        g = g_ref[...]
        gw_ref[...] = _dg(_silu(c_ref[...]), g, _TN)
        rr = lax.broadcasted_iota(jnp.int32, (8, 64), 0)
        cc = lax.broadcasted_iota(jnp.int32, (8, 64), 1)
        sel = jnp.where((rr == 0) & ((cc & 7) == 2), 1.0, 0.0).astype(F32)
        dctx = _hmm(sel, g)
        part = _dg(dctx, w_ref[...], _NT)

        @pl.when(j == 0)
        def _():
            pc_ref[...] = jnp.zeros_like(pc_ref)

        pc_ref[...] += part

    return pl.pallas_call(
        body, name="ada_bwd", grid=(1536 // tn,),
        out_shape=[jax.ShapeDtypeStruct((D, 1536), F32), jax.ShapeDtypeStruct((8, D), F32)],
        in_specs=[pl.BlockSpec((64, D), lambda j: (0, 0)), pl.BlockSpec((64, tn), lambda j: (0, j)),
                  pl.BlockSpec((D, tn), lambda j: (0, j))],
        out_specs=[pl.BlockSpec((D, tn), lambda j: (0, j)), pl.BlockSpec((8, D), lambda j: (0, 0))],
        compiler_params=_cp(vmem=VMEM_BIG),
    )(cin, gmod_s, w_ada_s)


def _ada_small(gmod, parts, c_ctx):
    def body(g_ref, p_ref, c_ref, gb_ref, gc_ref):
        gb_ref[...] = jnp.sum(g_ref[...], axis=0, keepdims=True)
        ds = p_ref[0][0:1] + p_ref[2][0:1] + p_ref[4][0:1] + p_ref[6][0:1]
        cc = c_ref[...]
        s = _sigmoid(cc)
        gc_ref[...] = ds * (s * (1.0 + cc * (1.0 - s)))

    vm = pl.BlockSpec(memory_space=pltpu.VMEM)
    return pl.pallas_call(
        body, name="ada_small",
        out_shape=[jax.ShapeDtypeStruct((1, 6 * D), F32), jax.ShapeDtypeStruct((1, D), F32)],
        in_specs=[vm, vm, vm], out_specs=[vm, vm],
    )(gmod, parts, c_ctx)


def _tok_specs():
    xs = pl.BlockSpec((None, RT, D), lambda b, j: (b, jnp.maximum(j - 1, 0), 0))
    cs = pl.BlockSpec((None, RT, D), lambda b, j: (b, 0, 0))
    ms = pl.BlockSpec((None, 1, D), lambda b, j: (b, 0, 0))
    cms = pl.BlockSpec((None, 1, D), lambda b, j: (0, 0, 0))
    return xs, cs, ms, cms


def _inproj_fwd(x, ctx, sh1, sc1, csh1, csc1, n1w, w_in_p):
    def body(x_ref, c_ref, sh_ref, sc_ref, csh_ref, csc_ref, w1_ref, w_ref, p_ref, hm_ref):
        is_ctx = pl.program_id(1) == 0
        xt = jnp.where(is_ctx, c_ref[...], x_ref[...])
        sh = jnp.where(is_ctx, csh_ref[...], sh_ref[...])
        sc = jnp.where(is_ctx, csc_ref[...], sc_ref[...])
        hb = _norm_mod(xt, w1_ref[...], sh, sc).astype(BF16)
        hm_ref[...] = hb
        p_ref[...] = lax.dot_general(hb, w_ref[...], _NT, preferred_element_type=F32)

    xs, cs, ms, cms = _tok_specs()
    return pl.pallas_call(
        body, name="inproj_fwd", grid=(NB, T // RT),
        out_shape=[jax.ShapeDtypeStruct((NB, T, PC), F32), jax.ShapeDtypeStruct((NB, T, D), BF16)],
        in_specs=[xs, cs, ms, ms, cms, cms, pl.BlockSpec((1, D), lambda b, j: (0, 0)),
                  pl.BlockSpec((PC, D), lambda b, j: (0, 0))],
        out_specs=[pl.BlockSpec((None, RT, PC), lambda b, j: (b, j, 0)),
                   pl.BlockSpec((None, RT, D), lambda b, j: (b, j, 0))],
        compiler_params=_cp(vmem=VMEM_BIG),
    )(x, ctx, sh1, sc1, csh1, csc1, n1w, w_in_p)


def _inproj_bwd(dp_dn, dz, dp_na, dgt, w_in_p, x, ctx, sh1, sc1, csh1, csc1, n1w, dx1):
    def body(d1q_ref, d1k_ref, d1v_ref, d2_ref, d3_ref, d4_ref, w_ref, x_ref, c_ref, sh_ref, sc_ref, csh_ref, csc_ref,
             w1_ref, dx1_ref, dpb_ref, gx_ref, dsh_ref, dsc_ref, dcsh_ref, dcsc_ref, dw1_ref):
        b, j = pl.program_id(0), pl.program_id(1)
        is_ctx = j == 0
        dp = jnp.concatenate([d1q_ref[...], d1k_ref[...], d1v_ref[...], d2_ref[...], d3_ref[...], d4_ref[...]],
                             axis=1).astype(BF16)
        dpb_ref[...] = dp
        dhm = jnp.dot(dp, w_ref[...], preferred_element_type=F32)
        xt = jnp.where(is_ctx, c_ref[...], x_ref[...])
        sh = jnp.where(is_ctx, csh_ref[...], sh_ref[...])
        sc = jnp.where(is_ctx, csc_ref[...], sc_ref[...])
        _, vjp = jax.vjp(_norm_mod, xt, w1_ref[...], sh, sc)
        dxt, dw1, dsh, dsc = vjp(dhm)

        @pl.when((b == 0) & (j == 0))
        def _():
            dw1_ref[...] = jnp.zeros_like(dw1_ref)
            dcsh_ref[...] = jnp.zeros_like(dcsh_ref)
            dcsc_ref[...] = jnp.zeros_like(dcsc_ref)

        dw1_ref[...] += dw1

        @pl.when(is_ctx)
        def _():
            dcsh_ref[...] += dsh
            dcsc_ref[...] += dsc

        @pl.when(j == 1)
        def _():
            dsh_ref[...] = jnp.zeros_like(dsh_ref)
            dsc_ref[...] = jnp.zeros_like(dsc_ref)

        @pl.when(j >= 1)
        def _():
            dsh_ref[...] += dsh
            dsc_ref[...] += dsc
            gx_ref[...] = dx1_ref[...] + dxt

    xs, cs, ms, cms = _tok_specs()
    tsp = lambda w: pl.BlockSpec((None, RT, w), lambda b, j: (b, j, 0))
    one = pl.BlockSpec((1, D), lambda b, j: (0, 0))
    return pl.pallas_call(
        body, name="inproj_bwd", grid=(NB, T // RT),
        out_shape=[jax.ShapeDtypeStruct((NB, T, PC), BF16), jax.ShapeDtypeStruct((NB, SEQ, D), F32),
                   jax.ShapeDtypeStruct((NB, 1, D), F32), jax.ShapeDtypeStruct((NB, 1, D), F32),
                   jax.ShapeDtypeStruct((1, 1, D), F32), jax.ShapeDtypeStruct((1, 1, D), F32),
                   jax.ShapeDtypeStruct((1, D), F32)],
        in_specs=[tsp(512), tsp(512), tsp(512), tsp(512), tsp(1536), tsp(128), pl.BlockSpec((PC, D), lambda b, j: (0, 0)),
                  xs, cs, ms, ms, cms, cms, one, xs],
        out_specs=[tsp(PC), xs, ms, ms, cms, cms, one],
        compiler_params=_cp(vmem=VMEM_BIG),
    )(*dp_dn, dz, dp_na, dgt, w_in_p, x, ctx, sh1, sc1, csh1, csc1, n1w, dx1)


def _seg_valid(t, off):
    lo = jnp.where(t < CT, 0, CT)
    hi = jnp.where(t < CT, CT, T)
    return (t + off >= lo) & (t + off < hi)


def _shifted(u, off):
    t = lax.broadcasted_iota(jnp.int32, (T, 1), 0)
    s = u if off == 0 else pltpu.roll(u, (-off) % T, 0)
    return jnp.where(_seg_valid(t, off), s, 0.0)


def _conv_pre(u, w):
    acc = _shifted(u, -2) * w[0:1, :]
    for kk in range(1, 5):
        acc = acc + _shifted(u, kk - 2) * w[kk:kk + 1, :]
    return acc


def _act_q(pre):
    a = _silu(pre)
    return a * lax.rsqrt(jnp.sum(a * a, axis=-1, keepdims=True) + EPS) * (128.0 ** -0.5)


def _act_k(pre):
    a = _silu(pre)
    return a * lax.rsqrt(jnp.sum(a * a, axis=-1, keepdims=True) + EPS)


def _dn_prep_fwd(p, cw):
    def body(pq_ref, pk_ref, pv_ref, wq_ref, wk_ref, wv_ref, q_ref, k_ref, v_ref):
        q_ref[...] = _act_q(_conv_pre(pq_ref[...], wq_ref[...]))
        k_ref[...] = _act_k(_conv_pre(pk_ref[...], wk_ref[...]))
        v_ref[...] = _silu(_conv_pre(pv_ref[...], wv_ref[...]))

    ps = lambda o: pl.BlockSpec((None, T, 128), lambda b, h: (b, 0, o + h))
    ws = lambda o: pl.BlockSpec((8, 128), lambda b, h: (0, o + h))
    os_ = pl.BlockSpec((None, T, 128), lambda b, h: (b, 0, h))
    return pl.pallas_call(
        body, name="dn_prep_fwd", grid=(NB, DNH), out_shape=[jax.ShapeDtypeStruct((NB, T, DNW), F32)] * 3,
        in_specs=[ps(0), ps(4), ps(8), ws(0), ws(4), ws(8)], out_specs=[os_] * 3,
        compiler_params=_cp(vmem=VMEM_BIG),
    )(p, p, p, cw, cw, cw)


def _dn_prep_bwd(p, cw, dqs, dks, dvs):
    def body(pq_ref, pk_ref, pv_ref, wq_ref, wk_ref, wv_ref, dqf_ref, dqb_ref, dkf_ref, dkb_ref, dvf_ref, dvb_ref,
             oq_ref, ok_ref, ov_ref, gq_ref, gk_ref, gv_ref):
        b = pl.program_id(1)

        def one(p_ref, w_ref, df_ref, db_ref, act, o_ref, g_ref):
            u = p_ref[...]
            w = w_ref[...]
            _, vjp = jax.vjp(act, _conv_pre(u, w))
            (dpre,) = vjp(df_ref[...] + db_ref[...])
            du = _shifted(dpre, 2) * w[0:1, :]
            rows = [jnp.sum(dpre * _shifted(u, -2), axis=0, keepdims=True)]
            for kk in range(1, 5):
                du = du + _shifted(dpre, 2 - kk) * w[kk:kk + 1, :]
                rows.append(jnp.sum(dpre * _shifted(u, kk - 2), axis=0, keepdims=True))
            o_ref[...] = du
            gw = jnp.concatenate(rows + [jnp.zeros((3, 128), F32)], axis=0)

            @pl.when(b == 0)
            def _():
                g_ref[...] = gw

            @pl.when(b > 0)
            def _():
                g_ref[...] += gw

        one(pq_ref, wq_ref, dqf_ref, dqb_ref, _act_q, oq_ref, gq_ref)
        one(pk_ref, wk_ref, dkf_ref, dkb_ref, _act_k, ok_ref, gk_ref)
        one(pv_ref, wv_ref, dvf_ref, dvb_ref, _silu, ov_ref, gv_ref)

    ps = lambda o: pl.BlockSpec((None, T, 128), lambda h, b: (b, 0, o + h))
    ws = lambda o: pl.BlockSpec((8, 128), lambda h, b: (0, o + h))
    ds = pl.BlockSpec((None, T, 128), lambda h, b: (b, 0, h))
    outs = pl.pallas_call(
        body, name="dn_prep_bwd", grid=(DNH, NB),
        out_shape=[jax.ShapeDtypeStruct((NB, T, DNW), F32)] * 3 + [jax.ShapeDtypeStruct((8, DNW), F32)] * 3,
        in_specs=[ps(0), ps(4), ps(8), ws(0), ws(4), ws(8)] + [ds] * 6,
        out_specs=[ds] * 3
        + [pl.BlockSpec((8, 128), lambda h, b: (0, h))] * 3,
        compiler_params=_cp(vmem=VMEM_BIG),
    )(p, p, p, cw, cw, cw, *dqs, *dks, *dvs)
    return outs


def _gate_fn(a, prm):
    lane = lax.broadcasted_iota(jnp.int32, a.shape, 1)
    z = a + prm[1:2, :]
    sp = jnp.maximum(z, 0.0) + jnp.log(1.0 + jnp.exp(-jnp.abs(z)))
    g = -jnp.exp(prm[0:1, :]) * sp
    return jnp.where(lane < 8, g, jnp.where(lane < 16, _sigmoid(a), 0.0))


def _gates_fwd(p, prm):
    def body(a_ref, prm_ref, o_ref):
        o_ref[...] = _gate_fn(a_ref[...], prm_ref[...])

    return pl.pallas_call(
        body, name="gates_fwd", grid=(NB,), out_shape=jax.ShapeDtypeStruct((NB, T, 128), F32),
        in_specs=[pl.BlockSpec((None, T, 128), lambda b: (b, 0, COL_G // 128)), pl.BlockSpec((8, 128), lambda b: (0, 0))],
        out_specs=pl.BlockSpec((None, T, 128), lambda b: (b, 0, 0)),
    )(p, prm)


def _gates_bwd(p, prm, dgf, dgb):
    def body(a_ref, prm_ref, dgf_ref, dgb_ref, o_ref, dprm_ref):
        b = pl.program_id(0)
        tot = dgf_ref[...] + dgb_ref[...]
        _, vjp = jax.vjp(_gate_fn, a_ref[...], prm_ref[...])
        da, dprm = vjp(tot)
        o_ref[...] = da

        @pl.when(b == 0)
        def _():
            dprm_ref[...] = dprm

        @pl.when(b > 0)
        def _():
            dprm_ref[...] += dprm

    return pl.pallas_call(
        body, name="gates_bwd", grid=(NB,),
        out_shape=[jax.ShapeDtypeStruct((NB, T, 128), F32), jax.ShapeDtypeStruct((8, 128), F32)],
        in_specs=[pl.BlockSpec((None, T, 128), lambda b: (b, 0, COL_G // 128)), pl.BlockSpec((8, 128), lambda b: (0, 0)),
                  pl.BlockSpec((None, T, 128), lambda b: (b, 0, 0)), pl.BlockSpec((None, T, 128), lambda b: (b, 0, 0))],
        out_specs=[pl.BlockSpec((None, T, 128), lambda b: (b, 0, 0)), pl.BlockSpec((8, 128), lambda b: (0, 0))],
        compiler_params=_cp(vmem=VMEM_BIG),
    )(p, prm, dgf, dgb)


def _lockstep(gens, joint=None):
    n = len(gens)
    res, vals = [None] * n, [None] * n
    while True:
        msgs = []
        for i, g in enumerate(gens):
            try:
                msgs.append((g.send(vals[i]), False))
            except StopIteration as e:
                res[i] = e.value
                msgs.append((None, True))
        if msgs[0][1]:
            return res
        if msgs[0][0] is not None:
            vals = list(joint(tuple(m[0] for m in msgs)))
        else:
            vals = [None] * n


def _tri_inv_gen(m, md, m1, m2, eye):
    n = -(m * md)
    t = eye + n
    pw = n
    for _ in range(3):
        pw = _dg(pw, pw, _NN)
        yield
        t = t + _dg(t, pw, _NN)
        yield
    a = _dg(t, m * m1, _NN)
    yield
    t = t - _dg(a, t, _NN)
    yield
    a = _dg(t, m * m2, _NN)
    yield
    return t - _dg(a, t, _NN)


def _tri_inv_bwd_gen(t, g):
    tt = t.T
    a = _dg(tt, g, _NN)
    yield
    return -_dg(a, tt, _NN)


@jax.custom_vjp
def _tri_inv(ms, md, m1, m2, eye):
    return tuple(_lockstep([_tri_inv_gen(m, md, m1, m2, eye) for m in ms]))


def _tri_inv_f(ms, md, m1, m2, eye):
    ts = _tri_inv(ms, md, m1, m2, eye)
    return ts, (ts, md)


def _tri_inv_b(res, gs):
    ts, md = res
    z = jnp.zeros_like(md)
    return (tuple(_lockstep([_tri_inv_bwd_gen(t, g) for t, g in zip(ts, gs)])), z, z, z, z)


_tri_inv.defvjp(_tri_inv_f, _tri_inv_b)


@jax.custom_vjp
def _tri_inv_saved(ms, ts):
    return tuple(t.astype(F32) for t in ts)


def _tri_inv_saved_f(ms, ts):
    return _tri_inv_saved(ms, ts), ts


def _tri_inv_saved_b(ts, gs):
    dms = tuple(_lockstep([_tri_inv_bwd_gen(t, g) for t, g in zip(ts, gs)]))
    return dms, tuple(jnp.zeros_like(t) for t in ts)


_tri_inv_saved.defvjp(_tri_inv_saved_f, _tri_inv_saved_b)


SR = DNH * CH


def _scan_masks():
    i, j = np.arange(SR)[:, None], np.arange(SR)[None, :]
    same = (i >> 6) == (j >> 6)
    cm = np.zeros((2, 6, SR, SR), np.float32)
    for d in range(2):
        diff = i - j if d == 0 else j - i
        cm[d, 0] = same & (diff >= 0)
        cm[d, 1] = same & (diff > 0)
        cm[d, 2] = (i >> 4) == (j >> 4)
        cm[d, 3] = ((i >> 5) == (j >> 5)) & ((i >> 4) != (j >> 4))
        cm[d, 4] = (i >> 5) != (j >> 5)
        cm[d, 5] = i == j
    t, u = np.arange(CH)[:, None], np.arange(CH)[None, :]
    c64 = np.stack([t >= u, t <= u]).astype(np.float32)
    bm = ((np.arange(SR)[:, None] >> 6) == (np.arange(DNW)[None, :] >> 7)).astype(np.float32)
    return jnp.asarray(cm), jnp.asarray(c64), jnp.asarray(bm)


@jax.custom_vjp
def _fold(x, bm):
    xm = x * bm
    return xm[:, 0:128] + xm[:, 128:256] + xm[:, 256:384] + xm[:, 384:512]


def _fold_f(x, bm):
    return _fold(x, bm), bm


def _fold_b(bm, g):
    return jnp.concatenate([g, g, g, g], axis=1) * bm, jnp.zeros_like(bm)


_fold.defvjp(_fold_f, _fold_b)


def _chunk_gen(q, k, v, gb, s, cm, c64, bm, d):
    incl, strict = cm[0], cm[1]
    lane = lax.broadcasted_iota(jnp.int32, (1, 128), 1)
    oh = lambda n: jnp.where(lane == n, 1.0, 0.0).astype(F32)
    col = lambda x, base: jnp.concatenate(
        [jnp.sum(x * oh(base + d * DNH + h), axis=1, keepdims=True) for h in range(DNH)], axis=0)
    beta = col(gb, 8)
    gam = col(_mask_mm(c64, gb), 0)
    yield
    tot = jnp.sum(gb, axis=0, keepdims=True)
    gtots = [jnp.sum(tot * oh(d * DNH + h), axis=1, keepdims=True) for h in range(DNH)]
    gtot = jnp.concatenate([jnp.broadcast_to(t, (CH, 1)) for t in gtots], axis=0)
    eg = jnp.concatenate([jnp.broadcast_to(jnp.exp(t), (1, 128)) for t in gtots], axis=1)
    row0 = jnp.where(lax.broadcasted_iota(jnp.int32, (128, 1), 0) == 0, 1.0, 0.0).astype(F32)
    gam_row = jnp.sum(jnp.broadcast_to(gam, (SR, 128)).T * row0, axis=0, keepdims=True)
    dec = jnp.exp(jnp.where(incl > 0.5, gam - gam_row, NEG))
    e_gam = jnp.exp(gam)
    kb = k * beta
    kk = _mm_nt(kb, k)
    yield
    tm = yield strict * (kk * dec)
    u = _mm(tm, v * beta)
    yield
    w = _mm(tm, kb * e_gam)
    yield
    qk = _mm_nt(q, k) * dec
    yield
    ws = _mm(w, s)
    yield
    v_new = u - _fold(ws, bm)
    qs = _mm(q * e_gam, s)
    yield
    o = _fold(qs, bm) + _mm(qk, v_new)
    yield
    vbd = jnp.concatenate([v_new, v_new, v_new, v_new], axis=1) * bm
    s_new = s * eg + _mm_tn(k * jnp.exp(gtot - gam), vbd)
    return o, s_new, tm


def _chunks(args, cm_ref, c64_ref, bm, saved=None):
    cm0 = cm_ref[0]
    if saved is None:
        inv = lambda ms: _tri_inv(ms, cm0[2], cm0[3], cm0[4], cm0[5])
    else:
        inv = lambda ms: _tri_inv_saved(ms, saved)
    return _lockstep([_chunk_gen(q, k, v, gb, s, cm_ref[d], c64_ref[d], bm, d) for q, k, v, gb, s, d in args], inv)


def _chunk_rev(s):
    return jnp.where(s < NCH_CTX, NCH_CTX - 1 - s, NCH + NCH_CTX - 1 - s)


def _scan_specs(step):
    cf = lambda s: step(s)
    cb = lambda s: _chunk_rev(step(s))
    tok = lambda c: pl.BlockSpec((NB, CH, DNW), lambda s: (0, c(s), 0))
    gat = lambda c: pl.BlockSpec((NB, CH, 128), lambda s: (0, c(s), 0))
    sta = lambda c: pl.BlockSpec((NB, None, 128, DNW), lambda s: (0, c(s), 0, 0))
    inv = lambda c: pl.BlockSpec((NB, None, SR, SR), lambda s: (0, c(s), 0, 0))
    return (tok(cf), gat(cf), sta(cf), inv(cf)), (tok(cb), gat(cb), sta(cb), inv(cb))


def _const_specs():
    return [pl.BlockSpec((2, 6, SR, SR), lambda s: (0, 0, 0, 0)), pl.BlockSpec((2, CH, CH), lambda s: (0, 0, 0)),
            pl.BlockSpec((SR, DNW), lambda s: (0, 0))]


SCAN_CHAINS = [(b, d) for b in range(NB) for d in range(2)]


def _stack(ref, b):
    return jnp.concatenate([ref[b, :, 128 * h:128 * h + 128] for h in range(DNH)], axis=0)


def _unstack_to(ref, b, val):
    for h in range(DNH):
        ref[b, :, 128 * h:128 * h + 128] = val[CH * h:CH * h + CH]


def _gather_plan(ins, outs, pos):
    me = _slot(pos)
    locals_, xfers = [], []
    for src, dst in zip(ins, outs):
        locals_.append((src, dst.at[me]))
        for k in range(1, 8):
            peer = _flip(pos, k)
            xfers.append((src, dst.at[me], dst.at[_slot(peer)], peer))
    return locals_, xfers


def _scan_fwd(q, k, v, gates, carry=()):
    nc = len(carry)

    def body(*refs):
        cm_ref, c64_ref, bm_ref, qf, kf, vf, gf, qb, kb, vb, gb = refs[:11]
        c_in = refs[11:11 + nc]
        of_ref, sf_ref, tf_ref, ob_ref, sb_ref, tb_ref = refs[11 + nc:17 + nc]
        c_out = refs[17 + nc:17 + 2 * nc]
        s_ref = refs[17 + 2 * nc]
        sems = refs[18 + 2 * nc:]
        if nc:
            @pl.when(pl.program_id(0) == 0)
            def _():
                _xfer_start(_gather_plan(c_in, c_out, _my_pos()), sems)

        @pl.when(pl.program_id(0) == 0)
        def _():
            s_ref[...] = jnp.zeros_like(s_ref)

        bm = bm_ref[...]
        dirs = ((qf, kf, vf, gf, of_ref, sf_ref, tf_ref), (qb, kb, vb, gb, ob_ref, sb_ref, tb_ref))
        args = [(_stack(dirs[d][0], b), _stack(dirs[d][1], b), _stack(dirs[d][2], b), dirs[d][3][b], s_ref[i], d)
                for i, (b, d) in enumerate(SCAN_CHAINS)]
        for i, (o, s_new, tm) in enumerate(_chunks(args, cm_ref, c64_ref, bm)):
            b, d = SCAN_CHAINS[i]
            dirs[d][5][b] = args[i][4]
            dirs[d][6][b] = tm.astype(BF16)
            _unstack_to(dirs[d][4], b, o)
            s_ref[i] = s_new

        if nc:
            @pl.when(pl.program_id(0) == NCH - 1)
            def _():
                _xfer_wait(_gather_plan(c_in, c_out, _my_pos()), sems)

    (tf, gf_, sf_, if_), (tb, gb_, sb_, ib_) = _scan_specs(lambda s: s)
    o_sh = jax.ShapeDtypeStruct((NB, T, DNW), F32)
    st_sh = jax.ShapeDtypeStruct((NB, NCH, 128, DNW), F32)
    inv_sh = jax.ShapeDtypeStruct((NB, NCH, SR, SR), BF16)
    hbm = pl.BlockSpec(memory_space=pltpu.HBM)
    return pl.pallas_call(
        body, name="scan_fwd", grid=(NCH,),
        out_shape=[o_sh, st_sh, inv_sh, o_sh, st_sh, inv_sh] + [jax.ShapeDtypeStruct((8,) + h.shape, h.dtype) for h in carry],
        in_specs=_const_specs() + [tf, tf, tf, gf_, tb, tb, tb, gb_] + [hbm] * nc,
        out_specs=[tf, sf_, if_, tb, sb_, ib_] + [hbm] * nc,
        scratch_shapes=[pltpu.VMEM((len(SCAN_CHAINS), 128, DNW), F32)] + (_xfer_sems(nc, 7 * nc) if nc else []),
        compiler_params=_cp(vmem=VMEM_BIG),
    )(*_scan_masks(), q, k, v, gates, q, k, v, gates, *carry)


def _scan_bwd(q, k, v, gates, st_f, st_b, inv_f, inv_b, do, carry=()):
    nc = len(carry)

    def body(*refs):
        (cm_ref, c64_ref, bm_ref, qf, kf, vf, gf, sf, tf_, dof, qb, kb, vb, gb, sb, tb_, dob) = refs[:17]
        c_in = refs[17:17 + nc]
        dqf, dkf, dvf, dgf, dqb, dkb, dvb, dgb = refs[17 + nc:25 + nc]
        c_out = refs[25 + nc:25 + 2 * nc]
        ds_ref = refs[25 + 2 * nc]
        sems = refs[26 + 2 * nc:]
        first = pl.program_id(0) == 0
        last = pl.program_id(0) == NCH - 1
        if nc:
            @pl.when(first)
            def _():
                _xfer_start(_scatter_plan(c_in, c_out, _my_pos()), sems)

        @pl.when(first)
        def _():
            ds_ref[...] = jnp.zeros_like(ds_ref)

        bm = bm_ref[...]
        dirs = ((qf, kf, vf, gf, sf, dof, dqf, dkf, dvf, dgf), (qb, kb, vb, gb, sb, dob, dqb, dkb, dvb, dgb))
        prim = tuple((_stack(dirs[d][0], b), _stack(dirs[d][1], b), _stack(dirs[d][2], b), dirs[d][3][b], dirs[d][4][b])
                     for b, d in SCAN_CHAINS)
        cots = tuple((_stack(dirs[d][5], b), ds_ref[i]) for i, (b, d) in enumerate(SCAN_CHAINS))
        saved = tuple((tf_, tb_)[d][b] for b, d in SCAN_CHAINS)
        _, vjp = jax.vjp(lambda ps: tuple(r[:2] for r in _chunks([p + (SCAN_CHAINS[i][1],) for i, p in enumerate(ps)],
                                                                cm_ref, c64_ref, bm, saved)), prim)
        (grads,) = vjp(cots)
        for i, (dq, dk, dv, dg, ds) in enumerate(grads):
            b, d = SCAN_CHAINS[i]
            _unstack_to(dirs[d][6], b, dq)
            _unstack_to(dirs[d][7], b, dk)
            _unstack_to(dirs[d][8], b, dv)
            dirs[d][9][b] = dg
            ds_ref[i] = ds

        if nc:
            @pl.when(last)
            def _():
                _xfer_wait(_scatter_plan(c_in, c_out, _my_pos()), sems)

    (tf, gf_, sf_, if_), (tb, gb_, sb_, ib_) = _scan_specs(lambda s: NCH - 1 - s)
    o_sh = jax.ShapeDtypeStruct((NB, T, DNW), F32)
    g_sh = jax.ShapeDtypeStruct((NB, T, 128), F32)
    hbm = pl.BlockSpec(memory_space=pltpu.HBM)
    return pl.pallas_call(
        body, name="scan_bwd", grid=(NCH,),
        out_shape=[o_sh, o_sh, o_sh, g_sh] * 2 + [jax.ShapeDtypeStruct(g.shape, g.dtype) for g in carry],
        in_specs=_const_specs() + [tf, tf, tf, gf_, sf_, if_, tf, tb, tb, tb, gb_, sb_, ib_, tb] + [hbm] * nc,
        out_specs=[tf, tf, tf, gf_, tb, tb, tb, gb_] + [hbm] * nc,
        scratch_shapes=[pltpu.VMEM((len(SCAN_CHAINS), 128, DNW), F32)] + (_xfer_sems(nc, 7 * nc) if nc else []),
        compiler_params=_cp(vmem=VMEM_BIG),
    )(*_scan_masks(), q, k, v, gates, st_f, inv_f, do, q, k, v, gates, st_b, inv_b, do, *carry)


def _segmean(a, bd):
    hi = a.astype(BF16)
    lo = (a - hi.astype(F32)).astype(BF16)
    return jnp.dot(hi, bd, preferred_element_type=F32) + jnp.dot(lo, bd, preferred_element_type=F32)


def _na_prep_fwd(p, qw, kw, bd):
    def body(q_ref, k_ref, v_ref, qw_ref, kw_ref, bd_ref, oq_ref, ok_ref, ov_ref):
        bdm = bd_ref[...]
        q = q_ref[...]
        k = k_ref[...]
        oq_ref[...] = (q * lax.rsqrt(_segmean(q * q, bdm) + EPS) * qw_ref[...] * 0.125).astype(BF16)
        ok_ref[...] = (k * lax.rsqrt(_segmean(k * k, bdm) + EPS) * kw_ref[...]).astype(BF16)
        ov_ref[...] = v_ref[...].astype(BF16)

    ps = lambda o: pl.BlockSpec((None, RT, NAW), lambda b, j: (b, j, COL_NA // NAW + o))
    one = pl.BlockSpec((1, NAW), lambda b, j: (0, 0))
    os_ = pl.BlockSpec((None, RT, NAW), lambda b, j: (b, j, 0))
    return pl.pallas_call(
        body, name="na_prep_fwd", grid=(NB, T // RT), out_shape=[jax.ShapeDtypeStruct((NB, T, NAW), BF16)] * 3,
        in_specs=[ps(0), ps(1), ps(2), one, one, pl.BlockSpec((NAW, NAW), lambda b, j: (0, 0))], out_specs=[os_] * 3,
    )(p, p, p, qw, kw, bd)


def _na_prep_bwd(p, qw, kw, bd, dq, dk, dv):
    def body(q_ref, k_ref, qw_ref, kw_ref, bd_ref, dq_ref, dk_ref, dv_ref, o_ref, gq_ref, gk_ref):
        b, j = pl.program_id(0), pl.program_id(1)
        bdm = bd_ref[...]

        def one(x, w, dy, scale):
            r = lax.rsqrt(_segmean(x * x, bdm) + EPS)
            g = dy * w * scale
            dx = r * g - x * (r * r * r) * _segmean(g * x, bdm)
            dw = jnp.sum(dy * x * r * scale, axis=0, keepdims=True)
            return dx, dw

        dqv = jnp.where(j == 0, 0.0, dq_ref[...])
        dxq, dwq = one(q_ref[...], qw_ref[...], dqv, 0.125)
        dxk, dwk = one(k_ref[...], kw_ref[...], dk_ref[...], 1.0)
        o_ref[...] = jnp.concatenate([dxq, dxk, dv_ref[...]], axis=1)
        pad = jnp.zeros((7, NAW), F32)

        @pl.when((b == 0) & (j == 0))
        def _():
            gq_ref[...] = jnp.zeros_like(gq_ref)
            gk_ref[...] = jnp.zeros_like(gk_ref)

        gq_ref[...] += jnp.concatenate([dwq, pad], axis=0)
        gk_ref[...] += jnp.concatenate([dwk, pad], axis=0)

    ps = lambda o: pl.BlockSpec((None, RT, NAW), lambda b, j: (b, j, COL_NA // NAW + o))
    one = pl.BlockSpec((1, NAW), lambda b, j: (0, 0))
    ts = pl.BlockSpec((None, RT, NAW), lambda b, j: (b, j, 0))
    acc = pl.BlockSpec((8, NAW), lambda b, j: (0, 0))
    return pl.pallas_call(
        body, name="na_prep_bwd", grid=(NB, T // RT),
        out_shape=[jax.ShapeDtypeStruct((NB, T, 1536), F32), jax.ShapeDtypeStruct((8, NAW), F32),
                   jax.ShapeDtypeStruct((8, NAW), F32)],
        in_specs=[ps(0), ps(1), one, one, pl.BlockSpec((NAW, NAW), lambda b, j: (0, 0)),
                  pl.BlockSpec((None, RT, NAW), lambda b, j: (b, jnp.maximum(j - 1, 0), 0)), ts, ts],
        out_specs=[pl.BlockSpec((None, RT, 1536), lambda b, j: (b, j, 0)), acc, acc],
        compiler_params=_cp(vmem=VMEM_BIG),
    )(p, p, qw, kw, bd, dq, dk, dv)


def _na_row0(r):
    return jnp.clip(r - WIN_ROWS // 2, 0, ROWS - WIN_ROWS)


def _na_cls(r):
    return _na_row0(r) - r + WIN_ROWS - 1


NHS = 4
NLW = NHS * 64
NSR = NHS * GRID_W


def _na_stack(x):
    lane = lax.broadcasted_iota(jnp.int32, (1, NLW), 1)
    return jnp.concatenate([jnp.where((lane >> 6) == h, x, jnp.zeros_like(x)) for h in range(NHS)], axis=0)


def _na_unstack(y):
    lane = lax.broadcasted_iota(jnp.int32, (1, NLW), 1)
    out = jnp.zeros((GRID_W, NLW), y.dtype)
    for h in range(NHS):
        out = jnp.where((lane >> 6) == h, y[GRID_W * h:GRID_W * h + GRID_W], out)
    return out


def _na_probs(q4, kw, kc, bias):
    s_w = lax.dot_general(q4, kw, _NT, preferred_element_type=F32) + bias
    s_c = lax.dot_general(q4, kc, _NT, preferred_element_type=F32)
    mx = jnp.maximum(jnp.max(s_w, axis=1, keepdims=True), jnp.max(s_c, axis=1, keepdims=True))
    p_w = jnp.exp(s_w - mx)
    p_c = jnp.exp(s_c - mx)
    inv = 1.0 / (jnp.sum(p_w, axis=1, keepdims=True) + jnp.sum(p_c, axis=1, keepdims=True))
    return p_w, p_c, inv


def _na_specs():
    qs = pl.BlockSpec((None, GRID_W, NLW), lambda g, b, r: (b, CT // GRID_W + r, g))
    kv = pl.BlockSpec((None, T, NLW), lambda g, b, r: (b, 0, g))
    bs = pl.BlockSpec((None, None, NSR, NWIN), lambda g, b, r: (_na_cls(r), g, 0, 0))
    lat = pl.BlockSpec((None, GRID_W, NLW), lambda g, b, r: (b, r, g))
    return qs, kv, bs, lat


def _na_fwd(qn, kn, vb, bias, carry=()):
    nc = len(carry)
    grid = (NAH // NHS, NB, ROWS)

    def body(*refs):
        q_ref, k_ref, v_ref, b_ref = refs[:4]
        c_in, o_ref, c_out, sems = refs[4:4 + nc], refs[4 + nc], refs[5 + nc:5 + 2 * nc], refs[5 + 2 * nc:]
        ids = [pl.program_id(a) for a in range(3)]
        if nc:
            @pl.when((ids[0] == 0) & (ids[1] == 0) & (ids[2] == 0))
            def _():
                _xfer_start(_gather_plan(c_in, c_out, _my_pos()), sems)

        r = ids[2]
        start = pl.multiple_of(CT + _na_row0(r) * GRID_W, GRID_W)
        kw, vw = k_ref[pl.ds(start, NWIN), :], v_ref[pl.ds(start, NWIN), :]
        kc, vc = k_ref[0:CT, :], v_ref[0:CT, :]
        p_w, p_c, inv = _na_probs(_na_stack(q_ref[...]), kw, kc, b_ref[...])
        o4 = (jnp.dot(p_w.astype(BF16), vw, preferred_element_type=F32)
              + jnp.dot(p_c.astype(BF16), vc, preferred_element_type=F32)) * inv
        o_ref[...] = _na_unstack(o4).astype(BF16)

        if nc:
            @pl.when((ids[0] == grid[0] - 1) & (ids[1] == grid[1] - 1) & (ids[2] == grid[2] - 1))
            def _():
                _xfer_wait(_gather_plan(c_in, c_out, _my_pos()), sems)

    qs, kv, bs, lat = _na_specs()
    hbm = pl.BlockSpec(memory_space=pltpu.HBM)
    res = pl.pallas_call(
        body, name="na_fwd", grid=grid,
        out_shape=[jax.ShapeDtypeStruct((NB, SEQ, NAW), BF16)] + [jax.ShapeDtypeStruct((8,) + h.shape, h.dtype) for h in carry],
        in_specs=[qs, kv, kv, bs] + [hbm] * nc, out_specs=[lat] + [hbm] * nc,
        scratch_shapes=_xfer_sems(nc, 7 * nc) if nc else [],
        compiler_params=_cp(vmem=VMEM_BIG),
    )(qn, kn, vb, bias, *carry)
    return res[0], list(res[1:])


def _na_bwd(qn, kn, vb, bias, do):
    def body(q_ref, k_ref, v_ref, b_ref, do_ref, dq_ref, dk_ref, dv_ref, db_ref):
        b, r = pl.program_id(1), pl.program_id(2)

        @pl.when((b == 0) & (r == 0))
        def _():
            db_ref[...] = jnp.zeros_like(db_ref)

        @pl.when(r == 0)
        def _():
            dk_ref[...] = jnp.zeros_like(dk_ref)
            dv_ref[...] = jnp.zeros_like(dv_ref)

        start = pl.multiple_of(CT + _na_row0(r) * GRID_W, GRID_W)
        cls = _na_cls(r)
        kw, vw = k_ref[pl.ds(start, NWIN), :], v_ref[pl.ds(start, NWIN), :]
        kc, vc = k_ref[0:CT, :], v_ref[0:CT, :]
        q4 = _na_stack(q_ref[...])
        p_w, p_c, inv = _na_probs(q4, kw, kc, b_ref[...])
        p_w = p_w * inv
        p_c = p_c * inv
        do4 = _na_stack(do_ref[...].astype(BF16))
        dp_w = lax.dot_general(do4, vw, _NT, preferred_element_type=F32)
        dp_c = lax.dot_general(do4, vc, _NT, preferred_element_type=F32)
        delta = jnp.sum(p_w * dp_w, axis=1, keepdims=True) + jnp.sum(p_c * dp_c, axis=1, keepdims=True)
        ds_w = p_w * (dp_w - delta)
        ds_c = p_c * (dp_c - delta)
        db_ref[cls] += ds_w
        ds_wb, ds_cb = ds_w.astype(BF16), ds_c.astype(BF16)
        dq_ref[...] = _na_unstack(jnp.dot(ds_wb, kw, preferred_element_type=F32)
                                  + jnp.dot(ds_cb, kc, preferred_element_type=F32))
        dk_ref[pl.ds(start, NWIN), :] += lax.dot_general(ds_wb, q4, _TN, preferred_element_type=F32)
        dv_ref[pl.ds(start, NWIN), :] += lax.dot_general(p_w.astype(BF16), do4, _TN, preferred_element_type=F32)
        dk_ref[0:CT, :] += lax.dot_general(ds_cb, q4, _TN, preferred_element_type=F32)
        dv_ref[0:CT, :] += lax.dot_general(p_c.astype(BF16), do4, _TN, preferred_element_type=F32)

    qs, kv, bs, lat = _na_specs()
    return pl.pallas_call(
        body, name="na_bwd", grid=(NAH // NHS, NB, ROWS),
        out_shape=[jax.ShapeDtypeStruct((NB, SEQ, NAW), F32)] + [jax.ShapeDtypeStruct((NB, T, NAW), F32)] * 2
        + [jax.ShapeDtypeStruct((WIN_ROWS, NAH // NHS, NSR, NWIN), F32)],
        in_specs=[qs, kv, kv, bs, lat],
        out_specs=[lat, kv, kv, pl.BlockSpec((WIN_ROWS, None, NSR, NWIN), lambda g, b, r: (0, g, 0, 0))],
        compiler_params=_cp(vmem=VMEM_BIG),
    )(qn, kn, vb, bias, do)


def _na_tables():
    cols = np.arange(GRID_W)
    win_start = np.clip(cols - WIN_COLS // 2, 0, GRID_W - WIN_COLS)
    kc = cols[None, :]
    valid = (kc >= win_start[:, None]) & (kc < win_start[:, None] + WIN_COLS)
    rel = np.clip(kc - cols[:, None] + WIN_COLS - 1, 0, 2 * WIN_COLS - 2)
    return valid, rel


def _bias_table(rpb):
    valid, rel = _na_tables()
    nrel = 2 * WIN_COLS - 1
    shifts = np.stack([(rel == r) & valid for r in range(nrel)]).astype(np.float32)
    neg = np.where(valid, 0.0, NEG).astype(np.float32)

    def body(rpb_ref, sh_ref, neg_ref, o_ref):
        h = pl.program_id(0)
        for dr in range(2 * WIN_ROWS - 1):
            tile = neg_ref[...]
            for r in range(nrel):
                tile = tile + rpb_ref[h, dr, r] * sh_ref[r]
            for cls in range(WIN_ROWS):
                i = dr - cls
                if 0 <= i < WIN_ROWS:
                    o_ref[cls, :, i * GRID_W:(i + 1) * GRID_W] = tile

    return pl.pallas_call(
        body, name="bias_table", grid=(NAH,),
        out_shape=jax.ShapeDtypeStruct((WIN_ROWS, NAH // NHS, NSR, NWIN), F32),
        in_specs=[pl.BlockSpec(memory_space=pltpu.SMEM), pl.BlockSpec((nrel, GRID_W, GRID_W), lambda h: (0, 0, 0)),
                  pl.BlockSpec((GRID_W, GRID_W), lambda h: (0, 0))],
        out_specs=pl.BlockSpec((WIN_ROWS, None, GRID_W, NWIN), lambda h: (0, h // NHS, h % NHS, 0)),
    )(rpb, jnp.asarray(shifts), jnp.asarray(neg))


def _rpb_fold(db):
    def body1(d_ref, o_ref):
        for dr in range(2 * WIN_ROWS - 1):
            acc = None
            for c in range(WIN_ROWS):
                i = dr - c
                if 0 <= i < WIN_ROWS:
                    blk = d_ref[c, :, i * GRID_W:(i + 1) * GRID_W]
                    acc = blk if acc is None else acc + blk
            o_ref[dr] = acc

    t15 = pl.pallas_call(
        body1, name="rpb_fold_rows", grid=(NAH,),
        out_shape=jax.ShapeDtypeStruct((NAH, 2 * WIN_ROWS - 1, GRID_W, GRID_W), F32),
        in_specs=[pl.BlockSpec((WIN_ROWS, None, GRID_W, NWIN), lambda h: (0, h, 0, 0))],
        out_specs=pl.BlockSpec((None, 2 * WIN_ROWS - 1, GRID_W, GRID_W), lambda h: (h, 0, 0, 0)),
        compiler_params=_cp(vmem=VMEM_BIG),
    )(db.reshape(WIN_ROWS, NAH, GRID_W, NWIN))
    valid, rel = _na_tables()
    onehot = np.zeros((GRID_W * GRID_W, 128), np.float32)
    flat_rel, flat_valid = rel.reshape(-1), valid.reshape(-1)
    onehot[np.arange(GRID_W * GRID_W)[flat_valid], flat_rel[flat_valid]] = 1.0

    def body2(a_ref, oh_ref, o_ref):
        o_ref[...] = _hmm(a_ref[...], oh_ref[...])

    vm = pl.BlockSpec(memory_space=pltpu.VMEM)
    out = pl.pallas_call(
        body2, name="rpb_fold_cols", out_shape=jax.ShapeDtypeStruct((NAH * 15, 128), F32),
        in_specs=[vm, vm], out_specs=vm, compiler_params=_cp(vmem=VMEM_BIG),
    )(t15.reshape(NAH * 15, GRID_W * GRID_W), jnp.asarray(onehot))
    return out[:, :31].reshape(NAH, 15, 31)


def _dn_out(o, z, w):
    outs = []
    for h in range(DNH):
        sl = slice(128 * h, 128 * h + 128)
        oh = o[:, sl]
        r = lax.rsqrt(jnp.mean(oh * oh, axis=-1, keepdims=True) + EPS)
        outs.append(oh * r * w[:, sl] * _silu(z[:, sl]))
    return jnp.concatenate(outs, axis=1)


def _outproj_fwd(of, ob, p, na, onw, x, g1, w_out):
    def body(of_ref, ob_ref, z_ref, na_ref, w_ref, x_ref, g_ref, wo_ref, x1_ref, ai_ref, ao_ref):
        dn = _dn_out(of_ref[...] + ob_ref[...], z_ref[...], w_ref[...])
        ai = jnp.concatenate([dn.astype(BF16), na_ref[...]], axis=1)
        ai_ref[...] = ai
        ao = jnp.dot(ai, wo_ref[...], preferred_element_type=F32)
        ao_ref[...] = ao
        x1_ref[...] = x_ref[...] + g_ref[...] * ao

    lat = lambda w: pl.BlockSpec((None, RT, w), lambda b, j: (b, j, 0))
    return pl.pallas_call(
        body, name="outproj_fwd", grid=(NB, SEQ // RT),
        out_shape=[jax.ShapeDtypeStruct((NB, SEQ, D), F32), jax.ShapeDtypeStruct((NB, SEQ, D), BF16),
                   jax.ShapeDtypeStruct((NB, SEQ, D), F32)],
        in_specs=[pl.BlockSpec((None, RT, DNW), lambda b, j: (b, j + 1, 0)),
                  pl.BlockSpec((None, RT, DNW), lambda b, j: (b, j + 1, 0)),
                  pl.BlockSpec((None, RT, DNW), lambda b, j: (b, j + 1, COL_Z // DNW)),
                  lat(NAW), pl.BlockSpec((1, DNW), lambda b, j: (0, 0)), lat(D),
                  pl.BlockSpec((None, 1, D), lambda b, j: (b, 0, 0)), pl.BlockSpec((D, D), lambda b, j: (0, 0))],
        out_specs=[lat(D), lat(D), lat(D)],
        compiler_params=_cp(vmem=VMEM_BIG),
    )(of, ob, p, na, onw, x, g1, w_out)


def _outproj_bwd(dx1, ao, g1, w_out, of, ob, p, onw):
    def body(dx_ref, ao_ref, g_ref, wo_ref, of_ref, ob_ref, z_ref, w_ref, dao_ref, do_ref, dz_ref, dna_ref, dg_ref, dw_ref):
        b, j = pl.program_id(0), pl.program_id(1)

        @pl.when((b == 0) & (j == 0))
        def _():
            dw_ref[...] = jnp.zeros_like(dw_ref)

        @pl.when(j == 0)
        def _():
            do_ref[...] = jnp.zeros_like(do_ref)
            dz_ref[...] = jnp.zeros_like(dz_ref)

        @pl.when(j == 1)
        def _():
            dg_ref[...] = jnp.zeros_like(dg_ref)

        @pl.when(j >= 1)
        def _():
            dx = dx_ref[...]
            dg_ref[...] += jnp.sum(dx * ao_ref[...], axis=0, keepdims=True)
            dao = (g_ref[...] * dx).astype(BF16)
            dao_ref[...] = dao
            dai = lax.dot_general(dao, wo_ref[...], _NT, preferred_element_type=F32)
            dna_ref[...] = dai[:, DNW:]
            _, vjp = jax.vjp(_dn_out, of_ref[...] + ob_ref[...], z_ref[...], w_ref[...])
            do, dz, dw = vjp(dai[:, :DNW])
            do_ref[...] = do
            dz_ref[...] = dz
            dw_ref[...] += jnp.concatenate([dw, jnp.zeros((7, DNW), F32)], axis=0)

    lat = lambda w: pl.BlockSpec((None, RT, w), lambda b, j: (b, jnp.maximum(j - 1, 0), 0))
    tok = lambda w: pl.BlockSpec((None, RT, w), lambda b, j: (b, j, 0))
    return pl.pallas_call(
        body, name="outproj_bwd", grid=(NB, T // RT),
        out_shape=[jax.ShapeDtypeStruct((NB, SEQ, D), BF16), jax.ShapeDtypeStruct((NB, T, DNW), F32),
                   jax.ShapeDtypeStruct((NB, T, DNW), F32), jax.ShapeDtypeStruct((NB, SEQ, NAW), F32),
                   jax.ShapeDtypeStruct((NB, 1, D), F32), jax.ShapeDtypeStruct((8, DNW), F32)],
        in_specs=[lat(D), lat(D), pl.BlockSpec((None, 1, D), lambda b, j: (b, 0, 0)),
                  pl.BlockSpec((D, D), lambda b, j: (0, 0)),
                  tok(DNW), tok(DNW),
                  pl.BlockSpec((None, RT, DNW), lambda b, j: (b, j, COL_Z // DNW)),
                  pl.BlockSpec((1, DNW), lambda b, j: (0, 0))],
        out_specs=[lat(D), tok(DNW), tok(DNW), lat(NAW), pl.BlockSpec((None, 1, D), lambda b, j: (b, 0, 0)),
                   pl.BlockSpec((8, DNW), lambda b, j: (0, 0))],
        compiler_params=_cp(vmem=VMEM_BIG),
    )(dx1, ao, g1, w_out, of, ob, p, onw)


FT_FWD, FT_BWD = 512, 256
NFF = DFF // FFT


def _ffn_specs(ft):
    row = lambda w: pl.BlockSpec((ft, w), lambda i, f: (i, 0))
    ex = pl.BlockSpec((None, 1, D), lambda i, f: (i // (SEQ // ft), 0, 0))
    one = pl.BlockSpec((1, D), lambda i, f: (0, 0))
    wg = pl.BlockSpec((None, D, FFT), lambda i, f: (f, 0, 0))
    wu = pl.BlockSpec((None, D, FFT), lambda i, f: (NFF + f, 0, 0))
    wo = pl.BlockSpec((FFT, D), lambda i, f: (f, 0))
    ff = pl.BlockSpec((ft, FFT), lambda i, f: (i, f))
    return row, ex, one, wg, wu, wo, ff


def _ffn_fwd(x1, sh2, sc2, g2, n2w, w_in4, w_out, target):
    ft, tpe = FT_FWD, SEQ // FT_FWD

    def body(x_ref, sh_ref, sc_ref, g_ref, w2_ref, wg_ref, wu_ref, wo_ref, t_ref,
             h2_ref, gt_ref, up_ref, dy_ref, dg_ref, loss_ref, acc_ref):
        i, f = pl.program_id(0), pl.program_id(1)

        @pl.when(f == 0)
        def _():
            h2_ref[...] = _norm_mod(x_ref[...], w2_ref[...], sh_ref[...], sc_ref[...]).astype(BF16)
            acc_ref[...] = jnp.zeros_like(acc_ref)

        h2 = h2_ref[...]
        gate = jnp.dot(h2, wg_ref[...], preferred_element_type=F32)
        up = jnp.dot(h2, wu_ref[...], preferred_element_type=F32)
        gt_ref[...] = gate.astype(BF16)
        up_ref[...] = up.astype(BF16)
        acc_ref[...] += jnp.dot((_silu(gate) * up).astype(BF16), wo_ref[...], preferred_element_type=F32)

        @pl.when((i == 0) & (f == 0))
        def _():
            loss_ref[...] = jnp.zeros_like(loss_ref)

        @pl.when(f == NFF - 1)
        def _():
            ffn = acc_ref[...]
            err = x_ref[...] + g_ref[...] * ffn - t_ref[...]
            dy = err * (1.0 / D)
            dy_ref[...] = dy
            part = jnp.sum(dy * ffn, axis=0, keepdims=True)
            loss_ref[...] += jnp.sum(jnp.sum(err * err, axis=1, keepdims=True), axis=0, keepdims=True) * (0.5 / D)

            @pl.when(i % tpe == 0)
            def _():
                dg_ref[...] = part

            @pl.when(i % tpe != 0)
            def _():
                dg_ref[...] += part

    row, ex, one, wg, wu, wo, ff = _ffn_specs(ft)
    return pl.pallas_call(
        body, name="ffn_fwd", grid=(NTOK // ft, NFF),
        out_shape=[jax.ShapeDtypeStruct((NTOK, D), BF16), jax.ShapeDtypeStruct((NTOK, DFF), BF16),
                   jax.ShapeDtypeStruct((NTOK, DFF), BF16), jax.ShapeDtypeStruct((NTOK, D), F32),
                   jax.ShapeDtypeStruct((NB, 1, D), F32), jax.ShapeDtypeStruct((8, 128), F32)],
        in_specs=[row(D), ex, ex, ex, one, wg, wu, wo, row(D)],
        out_specs=[row(D), ff, ff, row(D), ex, pl.BlockSpec((8, 128), lambda i, f: (0, 0))],
        scratch_shapes=[pltpu.VMEM((ft, D), F32)],
        compiler_params=_cp(vmem=VMEM_BIG),
    )(x1, sh2, sc2, g2, n2w, w_in4, w_in4, w_out, target)


def _ffn_bwd(dy, gate, up, x1, sh2, sc2, g2, n2w, w_in4, w_out):
    ft, tpe = FT_BWD, SEQ // FT_BWD

    def body(dy_ref, gt_ref, up_ref, x_ref, sh_ref, sc_ref, g_ref, w2_ref, wg_ref, wu_ref, wo_ref,
             dff_ref, act_ref, dgu_ref, dx_ref, dsh_ref, dsc_ref, dw_ref, acc_ref):
        i, f = pl.program_id(0), pl.program_id(1)

        @pl.when(f == 0)
        def _():
            dff_ref[...] = (g_ref[...] * dy_ref[...]).astype(BF16)
            acc_ref[...] = jnp.zeros_like(acc_ref)

        dact = lax.dot_general(dff_ref[...], wo_ref[...], _NT, preferred_element_type=F32)
        gate, up = gt_ref[...].astype(F32), up_ref[...].astype(F32)
        sg = _sigmoid(gate)
        sl = gate * sg
        act_ref[...] = (sl * up).astype(BF16)
        dgate = (dact * up * (sg * (1.0 + gate * (1.0 - sg)))).astype(BF16)
        dup = (dact * sl).astype(BF16)
        dgu_ref[0] = dgate
        dgu_ref[1] = dup
        acc_ref[...] += (lax.dot_general(dgate, wg_ref[...], _NT, preferred_element_type=F32)
                         + lax.dot_general(dup, wu_ref[...], _NT, preferred_element_type=F32))

        @pl.when((i == 0) & (f == 0))
        def _():
            dw_ref[...] = jnp.zeros_like(dw_ref)

        @pl.when(f == NFF - 1)
        def _():
            _, vjp = jax.vjp(_norm_mod, x_ref[...], w2_ref[...], sh_ref[...], sc_ref[...])
            dx, dw, dsh, dsc = vjp(acc_ref[...])
            dx_ref[...] = dy_ref[...] + dx
            dw_ref[...] += dw

            @pl.when(i % tpe == 0)
            def _():
                dsh_ref[...] = dsh
                dsc_ref[...] = dsc

            @pl.when(i % tpe != 0)
            def _():
                dsh_ref[...] += dsh
                dsc_ref[...] += dsc

    row, ex, one, wg, wu, wo, ff = _ffn_specs(ft)
    return pl.pallas_call(
        body, name="ffn_bwd", grid=(NTOK // ft, NFF),
        out_shape=[jax.ShapeDtypeStruct((NTOK, D), BF16), jax.ShapeDtypeStruct((NTOK, DFF), BF16),
                   jax.ShapeDtypeStruct((2, NTOK, DFF), BF16), jax.ShapeDtypeStruct((NTOK, D), F32),
                   jax.ShapeDtypeStruct((NB, 1, D), F32), jax.ShapeDtypeStruct((NB, 1, D), F32),
                   jax.ShapeDtypeStruct((1, D), F32)],
        in_specs=[row(D), ff, ff, row(D), ex, ex, ex, one, wg, wu, wo],
        out_specs=[row(D), ff, pl.BlockSpec((2, ft, FFT), lambda i, f: (0, i, f)), row(D), ex, ex, one],
        scratch_shapes=[pltpu.VMEM((ft, D), F32)],
        compiler_params=_cp(vmem=VMEM_BIG),
    )(dy, gate, up, x1, sh2, sc2, g2, n2w, w_in4, w_in4, w_out)


def _local_step(x, ctx, target, mod_x, mod_c, norm1_w, norm2_w, w_in_p, conv_w, a_log, dt_bias, out_norm_w,
                q_norm_w, k_norm_w, rpb, w_out_b, w_ffn_in4, w_ffn_out_b, scatter_early=False, gather_late=None):
    sh1, sc1, g1, sh2, sc2, g2 = [mod_x[:, i * D:(i + 1) * D].reshape(NB, 1, D) for i in range(6)]
    csh1, csc1 = mod_c[0:D].reshape(1, 1, D), mod_c[D:2 * D].reshape(1, 1, D)
    cw = jnp.concatenate([conv_w, jnp.zeros((3, 1536), F32)], axis=0)
    prm = jnp.zeros((8, 128), F32).at[0, :8].set(a_log.reshape(8)).at[1, :8].set(dt_bias.reshape(8))
    onw = jnp.tile(out_norm_w.reshape(1, 128), (1, DNH))
    qw = jnp.tile(q_norm_w.reshape(1, 64), (1, NAH))
    kw = jnp.tile(k_norm_w.reshape(1, 64), (1, NAH))
    seg = np.arange(NAW) // 64
    bd = jnp.asarray((seg[:, None] == seg[None, :]).astype(np.float32) / 64.0, dtype=BF16)
    bias = _bias_table(rpb)

    p, hm = _inproj_fwd(x, ctx, sh1, sc1, csh1, csc1, norm1_w, w_in_p)
    dq_, dk_, dv_ = _dn_prep_fwd(p, cw)
    gates = _gates_fwd(p, prm)
    res = _scan_fwd(dq_, dk_, dv_, gates, carry=gather_late[:2] if gather_late else ())
    o_f, st_f, inv_f, o_b, st_b, inv_b = res[:6]
    qn, kn, vb = _na_prep_fwd(p, qw, kw, bd)
    na, got = _na_fwd(qn, kn, vb, bias, carry=gather_late[2:] if gather_late else ())
    if gather_late:
        w_out_b = res[6].reshape(D, D)
        w_ffn_in4 = res[7].reshape(4, D, FFT)
        w_ffn_out_b = got[0].reshape(DFF, D)
    x1, ai, ao = _outproj_fwd(o_f, o_b, p, na, onw, x, g1, w_out_b)
    x1f = x1.reshape(NTOK, D)
    h2, gate, up, dy, dg2, loss8 = _ffn_fwd(x1f, sh2, sc2, g2, norm2_w, w_ffn_in4, w_ffn_out_b,
                                             target.reshape(NTOK, D))
    dff, act, dgu, dx1, dsh2, dsc2, dn2w = _ffn_bwd(dy, gate, up, x1f, sh2, sc2, g2, norm2_w, w_ffn_in4, w_ffn_out_b)
    g_w_ffn_out = _matmul_tn(act, dff, FFT, D, 512, "wgrad_ffn_out")
    g_w_ffn_in4 = _matmul_tn(
        h2, dgu, 512, FFT, 512, "wgrad_ffn_in", nn=2 * NFF,
        b_block=(None, 512, FFT), b_map=lambda i, j, k: (j // NFF, k, j % NFF),
        out_shape=(4, D, FFT), out_block=(None, 512, FFT), out_map=lambda i, j, k: (j, i, 0))
    dao, do, dz, dna, dg1, donw = _outproj_bwd(dx1.reshape(NB, SEQ, D), ao, g1, w_out_b, o_f, o_b, p, onw)
    g_w_out = _matmul_tn(ai.reshape(NTOK, D), dao.reshape(NTOK, D), 512, D, 512, "wgrad_out")
    dqn, dkn, dvn, dbias = _na_bwd(qn, kn, vb, bias, dna)
    dp_na, dqw, dkw = _na_prep_bwd(p, qw, kw, bd, dqn, dkn, dvn)
    early = [g_w_out.reshape(8, D // 8, D), g_w_ffn_in4.reshape(8, D // 2, FFT), g_w_ffn_out.reshape(8, DFF // 8, D)]
    res = _scan_bwd(dq_, dk_, dv_, gates, st_f, st_b, inv_f, inv_b, do, carry=early if scatter_early else ())
    dqf, dkf, dvf, dgf, dqb, dkb, dvb, dgb = res[:8]
    if scatter_early:
        g_w_out, g_w_ffn_in4, g_w_ffn_out = res[8:]
    dpq, dpk, dpv, gcq, gck, gcv = _dn_prep_bwd(p, cw, (dqf, dqb), (dkf, dkb), (dvf, dvb))
    dgt, dprm = _gates_bwd(p, prm, dgf, dgb)
    dpb, grad_x, dsh1, dsc1, dcsh1, dcsc1, dn1w = _inproj_bwd(
        (dpq, dpk, dpv), dz, dp_na, dgt, w_in_p, x, ctx, sh1, sc1, csh1, csc1, norm1_w, dx1.reshape(NB, SEQ, D))
    g_w_in_p = _matmul_tn(dpb.reshape(NB * T, PC), hm.reshape(NB * T, D), PC, D, 512, "wgrad_in")
    g_rpb = _rpb_fold(dbias)

    dmod_x = jnp.concatenate([dsh1, dsc1, dg1, dsh2, dsc2, dg2], axis=2).reshape(NB, 6 * D)
    dmod_c = jnp.concatenate([dcsh1.reshape(D), dcsc1.reshape(D), jnp.zeros((4 * D,), F32)])
    g_conv = jnp.concatenate([gcq[:5], gck[:5], gcv[:5]], axis=1)
    small = dict(
        norm1_w=dn1w, norm2_w=dn2w, a_log=dprm[0, :8], dt_bias=dprm[1, :8],
        out_norm_w=donw[0].reshape(DNH, 128).sum(0), q_norm_w=dqw[0].reshape(NAH, 64).sum(0),
        k_norm_w=dkw[0].reshape(NAH, 64).sum(0), rpb=g_rpb, conv_w=g_conv)
    return loss8[0, 0], grad_x, dmod_x, dmod_c, small, g_w_in_p, g_w_out, g_w_ffn_in4, g_w_ffn_out


def _perm_w_in(wt):
    return jnp.concatenate([wt[:2048], wt[2064:3600], wt[2048:2064], jnp.zeros((PC - IN_COLS, wt.shape[1]), wt.dtype)],
                           axis=0)


def _unperm_w_in(gt):
    return jnp.concatenate([gt[:2048], gt[COL_G:COL_G + 16], gt[2048:COL_G]], axis=0)


def _pack_small(c_ctx, n1, n2, b_ada, a_log, dt_bias, onw, qnw, knw, rpb, conv_s):
    misc = jnp.zeros((D,), F32)
    misc = misc.at[0:8].set(a_log.reshape(8)).at[8:16].set(dt_bias.reshape(8))
    misc = misc.at[128:256].set(onw.reshape(128)).at[256:320].set(qnw.reshape(64)).at[320:384].set(knw.reshape(64))
    rp = jnp.concatenate([rpb.reshape(-1), jnp.zeros((4 * D - 3720,), F32)])
    cv = jnp.concatenate([conv_s.reshape(-1), jnp.zeros((2 * D - 1920,), F32)])
    return jnp.concatenate([c_ctx.reshape(-1), n1.reshape(-1), n2.reshape(-1), b_ada.reshape(-1), misc, rp, cv]).reshape(16, D)


def _unpack_small(a):
    f = a.reshape(-1)
    misc = f[9 * D:10 * D]
    return dict(
        c_ctx=f[0:D], norm1_w=f[D:2 * D].reshape(1, D), norm2_w=f[2 * D:3 * D].reshape(1, D),
        b_ada=f[3 * D:9 * D].reshape(1, 6 * D), dn_A_log=misc[0:8].reshape(1, 2, 4), dn_dt_bias=misc[8:16].reshape(1, 2, 4),
        dn_out_norm_w=misc[128:256].reshape(1, 128), na_q_norm_w=misc[256:320].reshape(1, 64),
        na_k_norm_w=misc[320:384].reshape(1, 64), na_rpb=f[10 * D:10 * D + 3720].reshape(1, 8, 15, 31),
        dn_conv_w=f[14 * D:14 * D + 1920].reshape(1, 5, 384))


_WEIGHTS = ['c_ctx', 'norm1_w', 'norm2_w', 'w_ada', 'b_ada', 'w_in', 'dn_conv_w', 'dn_A_log', 'dn_dt_bias',
            'dn_out_norm_w', 'na_q_norm_w', 'na_k_norm_w', 'na_rpb', 'w_out', 'w_ffn_in', 'w_ffn_out']


def kernel(x, c, ctx, c_ctx, norm1_w, norm2_w, w_ada, b_ada, w_in, dn_conv_w, dn_A_log, dn_dt_bias, dn_out_norm_w, na_q_norm_w, na_k_norm_w, na_rpb, w_out, w_ffn_in, w_ffn_out, loss_target, m_c_ctx, m_norm1_w, m_norm2_w, m_w_ada, m_b_ada, m_w_in, m_dn_conv_w, m_dn_A_log, m_dn_dt_bias, m_dn_out_norm_w, m_na_q_norm_w, m_na_k_norm_w, m_na_rpb, m_w_out, m_w_ffn_in, m_w_ffn_out, v_c_ctx, v_norm1_w, v_norm2_w, v_w_ada, v_b_ada, v_w_in, v_dn_conv_w, v_dn_A_log, v_dn_dt_bias, v_dn_out_norm_w, v_na_q_norm_w, v_na_k_norm_w, v_na_rpb, v_w_out, v_w_ffn_in, v_w_ffn_out):
    ix, iy, ic = lax.axis_index("x"), lax.axis_index("y"), lax.axis_index("c")
    chip = 2 * ix + iy
    me = 4 * ix + 2 * iy + ic

    crow = jnp.concatenate([c, c_ctx.reshape(1, D), jnp.zeros((5, D), F32)], axis=0)
    cvrow = jnp.concatenate([dn_conv_w[0].reshape(-1), jnp.zeros((2 * D - 1920,), F32)]).reshape(2, D)
    g0 = _allgather8(jnp.concatenate([crow, cvrow, jnp.zeros((6, D), F32)], axis=0), "gather_cond")
    cin = g0[:, :8].reshape(64, D)
    conv_full = jnp.concatenate([g0[2 * s, 8:10].reshape(-1)[:1920].reshape(5, 384) for s in range(4)], axis=1)

    w_ada_s = w_ada[0]
    b_s = lax.dynamic_slice(b_ada, (0, chip * 1536), (1, 1536))
    mod_s = _ada_fwd(cin, w_ada_s, b_s)
    mods = _gather_rows(mod_s, "gather_mod")
    mod_mine = jnp.concatenate([mods[0], mods[2], mods[4], mods[6]], axis=1)
    mod_x, mod_c = mod_mine[0:NB], mod_mine[2]

    def half(w2d):
        r = w2d.shape[0] // 2
        return lax.dynamic_slice(w2d, (ic * r, 0), (r, w2d.shape[1]))

    w_in_t = w_in[0].T
    gw_in = _gather_small_vmem(_cast_bf16(lax.dynamic_slice(w_in_t, (0, ic * (D // 2)), (900, D // 2)), "cast_w_in"))
    late = [_cast_bf16(half(w_out[0]), "cast_w_out"), _cast_bf16(half(w_ffn_in[0]), "cast_w_ffn_in"),
            _cast_bf16(half(w_ffn_out[0]), "cast_w_ffn_out")]
    w_in_p = _perm_w_in(gw_in.reshape(4, 2, 900, D // 2).transpose(0, 2, 1, 3).reshape(IN_COLS, D))

    (loss_part, grad_x, dmod_x, dmod_c, small, g_w_in_p, r_w_out, r_w_ffn_in, r_w_ffn_out) = _local_step(
        x, ctx, loss_target, mod_x, mod_c, norm1_w, norm2_w, w_in_p, conv_full, dn_A_log[0], dn_dt_bias[0],
        dn_out_norm_w[0], na_q_norm_w[0], na_k_norm_w[0], na_rpb[0], None, None, None,
        scatter_early=True, gather_late=late)
    loss = lax.psum(loss_part, ("x", "y", "c"))

    dm = jnp.concatenate([dmod_x, dmod_c.reshape(1, 6 * D), jnp.zeros((5, 6 * D), F32)], axis=0).reshape(48, D)
    misc = jnp.zeros((D,), F32)
    misc = misc.at[0:8].set(small["a_log"]).at[8:16].set(small["dt_bias"]).at[128:256].set(small["out_norm_w"])
    misc = misc.at[256:320].set(small["q_norm_w"]).at[320:384].set(small["k_norm_w"])
    sm = jnp.concatenate([
        small["norm1_w"].reshape(-1), small["norm2_w"].reshape(-1), misc,
        small["rpb"].reshape(-1), jnp.zeros((4 * D - 3720,), F32),
        small["conv_w"].reshape(-1), jnp.zeros((8 * D - 7680,), F32), jnp.zeros((D,), F32)]).reshape(16, D)
    gsl, gtot = _allgather8(jnp.concatenate([dm, sm], axis=0), "gather_small", with_sum=True)
    gmod = gsl[:, :48].reshape(64, 6 * D)
    st = gtot[48:64].reshape(-1)
    g_n1, g_n2, g_misc = st[0:D].reshape(1, D), st[D:2 * D].reshape(1, D), st[2 * D:3 * D]
    g_rpb = st[3 * D:3 * D + 3720].reshape(1, 8, 15, 31)
    g_conv_full = st[7 * D:7 * D + 7680].reshape(5, 1536)
    g_conv_s = lax.dynamic_slice(g_conv_full, (0, chip * 384), (5, 384)).reshape(1, 5, 384)

    gmod_s = lax.dynamic_slice(gmod, (0, chip * 1536), (64, 1536))
    g_w_ada_s, pctx = _ada_bwd(cin, gmod_s, w_ada_s)
    parts = _allgather8(pctx, "gather_cctx")
    g_b_ada, g_c_ctx = _ada_small(gmod, parts, c_ctx.reshape(1, D))

    g_w_in8 = _unperm_w_in(g_w_in_p).reshape(4, 900, 2, D // 2).transpose(0, 2, 1, 3).reshape(8, 900, D // 2)
    names = ["w_in", "w_out", "w_ffn_in", "w_ffn_out"]
    rs = list(_scatter_grads([g_w_in8])) + [r_w_out, r_w_ffn_in, r_w_ffn_out]
    full = _sibling_exchange([_sum_slots(r, "sum_" + nm) for r, nm in zip(rs, names)])
    g_w_in_t = full[0].transpose(1, 0, 2).reshape(900, D)
    g_w_out_s = full[1].reshape(1, D // 4, D)
    g_w_ffn_in_s = full[2].reshape(1, D, FFT)
    g_w_ffn_out_s = full[3].reshape(1, DFF // 4, D)

    grads = dict(
        c_ctx=g_c_ctx.reshape(D), norm1_w=g_n1, norm2_w=g_n2, w_ada=g_w_ada_s.reshape(1, D, 1536), b_ada=g_b_ada,
        w_in=g_w_in_t, dn_conv_w=g_conv_s, dn_A_log=g_misc[0:8].reshape(1, 2, 4), dn_dt_bias=g_misc[8:16].reshape(1, 2, 4),
        dn_out_norm_w=g_misc[128:256].reshape(1, 128), na_q_norm_w=g_misc[256:320].reshape(1, 64),
        na_k_norm_w=g_misc[320:384].reshape(1, 64), na_rpb=g_rpb, w_out=g_w_out_s, w_ffn_in=g_w_ffn_in_s,
        w_ffn_out=g_w_ffn_out_s)
    pk = lambda d_: _pack_small(d_["c_ctx"], d_["norm1_w"], d_["norm2_w"], d_["b_ada"], d_["dn_A_log"], d_["dn_dt_bias"],
                                d_["dn_out_norm_w"], d_["na_q_norm_w"], d_["na_k_norm_w"], d_["na_rpb"], d_["dn_conv_w"])
    wts = dict(c_ctx=c_ctx, norm1_w=norm1_w, norm2_w=norm2_w, b_ada=b_ada, dn_A_log=dn_A_log, dn_dt_bias=dn_dt_bias,
               dn_out_norm_w=dn_out_norm_w, na_q_norm_w=na_q_norm_w, na_k_norm_w=na_k_norm_w, na_rpb=na_rpb, dn_conv_w=dn_conv_w)
    ms = dict(c_ctx=m_c_ctx, norm1_w=m_norm1_w, norm2_w=m_norm2_w, b_ada=m_b_ada, dn_A_log=m_dn_A_log, dn_dt_bias=m_dn_dt_bias,
              dn_out_norm_w=m_dn_out_norm_w, na_q_norm_w=m_na_q_norm_w, na_k_norm_w=m_na_k_norm_w, na_rpb=m_na_rpb,
              dn_conv_w=m_dn_conv_w)
    vs = dict(c_ctx=v_c_ctx, norm1_w=v_norm1_w, norm2_w=v_norm2_w, b_ada=v_b_ada, dn_A_log=v_dn_A_log, dn_dt_bias=v_dn_dt_bias,
              dn_out_norm_w=v_dn_out_norm_w, na_q_norm_w=v_na_q_norm_w, na_k_norm_w=v_na_k_norm_w, na_rpb=v_na_rpb,
              dn_conv_w=v_dn_conv_w)
    sd, snm, snv, _ = _adamw(pk(wts), pk(grads), pk(ms), pk(vs), "adamw_small")
    delta, new_m, new_v = _unpack_small(sd), _unpack_small(snm), _unpack_small(snv)
    big = dict(w_ada=(w_ada, m_w_ada, v_w_ada), w_in=(w_in, m_w_in, v_w_in), w_out=(w_out, m_w_out, v_w_out),
               w_ffn_in=(w_ffn_in, m_w_ffn_in, v_w_ffn_in), w_ffn_out=(w_ffn_out, m_w_ffn_out, v_w_ffn_out))
    for nm, (w_, m_, v_) in big.items():
        shp = w_.shape
        if nm == "w_in":
            r2, back = (lambda a: a[0].T), (lambda a: a.T.reshape(shp))
            res = _adamw(w_in_t, grads[nm], r2(m_), r2(v_), "adamw_" + nm)
        else:
            r2, back = (lambda a: a.reshape(shp[1], shp[2])), (lambda a: a.reshape(shp))
            res = _adamw(r2(w_), r2(grads[nm]), r2(m_), r2(v_), "adamw_" + nm)
        delta[nm], new_m[nm], new_v[nm], grads[nm] = [back(a) for a in res]

    return (loss, grad_x, *[grads[n] for n in _WEIGHTS], *[delta[n] for n in _WEIGHTS],
            *[new_m[n] for n in _WEIGHTS], *[new_v[n] for n in _WEIGHTS])
```

```python
import functools

import numpy as np
import jax
import jax.numpy as jnp
from jax import lax
from jax.experimental import pallas as pl
from jax.experimental.pallas import tpu as pltpu

F32 = jnp.float32
BF16 = jnp.bfloat16

D = 1024
NB = 2
SEQ = 2048
CT = 256
T = CT + SEQ
NTOK = NB * SEQ
CH = 64
NCH = T // CH
NCH_CTX = CT // CH
DNH, DNW = 4, 512
NAH, NAW = 8, 512
GRID_W = 64
ROWS = SEQ // GRID_W
WIN_ROWS = 8
WIN_COLS = 16
NWIN = WIN_ROWS * GRID_W
DFF = 2816
FFT = 1408
IN_COLS = 3600
PC = 3712
COL_Z, COL_NA, COL_G = 1536, 2048, 3584
EPS = 1e-6
NEG = -1e30
RT = 256
VMEM_BIG = 56 * 1024 * 1024
MESH = pl.DeviceIdType.MESH

ADAM_LR, ADAM_B1, ADAM_B2, ADAM_EPS, ADAM_WD, ADAM_STEP = 0.001, 0.9, 0.999, 1e-08, 0.01, 10


def _cp(vmem=None, sem=None):
    kw = {}
    if vmem is not None:
        kw["vmem_limit_bytes"] = vmem
    if sem is not None:
        kw["dimension_semantics"] = sem
    return pltpu.CompilerParams(**kw)


_NN = (((1,), (0,)), ((), ()))
_NT = (((1,), (1,)), ((), ()))
_TN = (((0,), (0,)), ((), ()))


def _dg(a, b, dims):
    return lax.dot_general(a.astype(BF16), b.astype(BF16), dims, preferred_element_type=F32)


@jax.custom_vjp
def _mm(a, b):
    return _dg(a, b, _NN)


_mm.defvjp(lambda a, b: (_dg(a, b, _NN), (a, b)),
           lambda r, g: (_dg(g, r[1], _NT), _dg(r[0], g, _TN)))


@jax.custom_vjp
def _mm_nt(a, b):
    return _dg(a, b, _NT)


_mm_nt.defvjp(lambda a, b: (_dg(a, b, _NT), (a, b)),
              lambda r, g: (_dg(g, r[1], _NN), _dg(g, r[0], _TN)))


@jax.custom_vjp
def _mm_tn(a, b):
    return _dg(a, b, _TN)


_mm_tn.defvjp(lambda a, b: (_dg(a, b, _TN), (a, b)),
              lambda r, g: (_dg(r[1], g, _NT), _dg(r[0], g, _NN)))


def _split_dot(a, b, dims):
    hi = b.astype(BF16)
    lo = (b - hi.astype(F32)).astype(BF16)
    ab = a.astype(BF16)
    return (lax.dot_general(ab, hi, dims, preferred_element_type=F32)
            + lax.dot_general(ab, lo, dims, preferred_element_type=F32))


@jax.custom_vjp
def _mask_mm(mask, b):
    return _split_dot(mask, b, _NN)


_mask_mm.defvjp(lambda mask, b: (_split_dot(mask, b, _NN), mask),
                lambda mask, g: (jnp.zeros_like(mask), _split_dot(mask, g, _TN)))


def _hmm(a, b):
    return jnp.dot(a, b, precision=lax.Precision.HIGHEST, preferred_element_type=F32)


def _sigmoid(x):
    return 1.0 / (1.0 + jnp.exp(-x))


def _silu(x):
    return x * _sigmoid(x)


def _norm_mod(x, w, sh, sc):
    r = lax.rsqrt(jnp.mean(x * x, axis=-1, keepdims=True) + EPS)
    return (x * r * w) * (1.0 + sc) + sh


def _my_pos():
    return lax.axis_index("x"), lax.axis_index("y"), lax.axis_index("c")


def _flip(pos, k):
    x, y, c = pos
    return (1 - x if k & 4 else x, 1 - y if k & 2 else y, 1 - c if k & 1 else c)


def _slot(pos):
    return 4 * pos[0] + 2 * pos[1] + pos[2]


def _allgather8(v, name, with_sum=False):
    R, W = v.shape

    def body(v_ref, out_ref, *rest):
        if with_sum:
            tot_ref, send_sems, recv_sems, lsem = rest
        else:
            send_sems, recv_sems, lsem = rest
        me = _my_pos()
        mine = pltpu.make_async_copy(v_ref, out_ref.at[_slot(me)], lsem)
        mine.start()
        sends = []
        for k in range(1, 8):
            cp = pltpu.make_async_remote_copy(
                src_ref=v_ref, dst_ref=out_ref.at[_slot(me)], send_sem=send_sems.at[k - 1],
                recv_sem=recv_sems.at[k - 1], device_id=_flip(me, k), device_id_type=MESH)
            cp.start()
            sends.append(cp)
        for k in range(1, 8):
            peer = _flip(me, k)
            pltpu.make_async_remote_copy(
                src_ref=v_ref, dst_ref=out_ref.at[_slot(peer)], send_sem=send_sems.at[k - 1],
                recv_sem=recv_sems.at[k - 1], device_id=peer, device_id_type=MESH).wait_recv()
        for cp in sends:
            cp.wait_send()
        mine.wait()
        if with_sum:
            acc = out_ref[0]
            for s in range(1, 8):
                acc = acc + out_ref[s]
            tot_ref[...] = acc

    out_shape = [jax.ShapeDtypeStruct((8, R, W), F32)]
    out_specs = [pl.BlockSpec(memory_space=pltpu.VMEM)]
    if with_sum:
        out_shape.append(jax.ShapeDtypeStruct((R, W), F32))
        out_specs.append(pl.BlockSpec(memory_space=pltpu.VMEM))
    res = pl.pallas_call(
        body, name=name, out_shape=out_shape,
        in_specs=[pl.BlockSpec(memory_space=pltpu.VMEM)], out_specs=out_specs,
        scratch_shapes=[pltpu.SemaphoreType.DMA((7,)), pltpu.SemaphoreType.DMA((7,)), pltpu.SemaphoreType.DMA],
        compiler_params=_cp(vmem=VMEM_BIG),
    )(v)
    return res if with_sum else res[0]


def _gather_rows(v, name):
    W = v.shape[1]

    def body(v_ref, out_ref, send_sems, recv_sems, lsem):
        pos = _my_pos()
        me = _slot(pos)
        rows = lambda p: v_ref.at[pl.ds(pl.multiple_of(8 * p, 8), 8)]
        locals_ = [(rows(me), out_ref.at[me])]
        xfers = [(rows(_slot(_flip(pos, k))), out_ref.at[me], out_ref.at[_slot(_flip(pos, k))], _flip(pos, k))
                 for k in range(1, 8)]
        _xfer_start((locals_, xfers), (send_sems, recv_sems, lsem))
        _xfer_wait((locals_, xfers), (send_sems, recv_sems, lsem))

    vm = pl.BlockSpec(memory_space=pltpu.VMEM)
    return pl.pallas_call(
        body, name=name, out_shape=jax.ShapeDtypeStruct((8, 8, W), F32), in_specs=[vm], out_specs=vm,
        scratch_shapes=_xfer_sems(1, 7),
    )(v)


def _gather_weights(halves):
    n = len(halves)

    def body(*refs):
        ins, outs = refs[:n], refs[n:2 * n]
        send_sems, recv_sems, lsems = refs[2 * n:]
        me = _my_pos()
        x, y, c = me
        sib = (x, y, 1 - c)
        chips = [4, 2, 6]

        def cp(a, k, block, to, src=None):
            dst = outs[a].at[_slot(block)]
            return pltpu.make_async_remote_copy(
                src_ref=dst if src is None else src, dst_ref=dst, send_sem=send_sems.at[a, k],
                recv_sem=recv_sems.at[a, k], device_id=to, device_id_type=MESH)

        mine, first, passed = [], [], []
        for a in range(n):
            m = pltpu.make_async_copy(ins[a], outs[a].at[_slot(me)], lsems.at[a])
            m.start()
            mine.append(m)
            first.append(cp(a, 0, me, sib, src=ins[a]))
            for j, k in enumerate(chips):
                first.append(cp(a, 1 + j, me, _flip(me, k), src=ins[a]))
        for f in first:
            f.start()
        for j, k in enumerate(chips):
            for a in range(n):
                cp(a, 1 + j, _flip(me, k), me).wait_recv()
                p = cp(a, 4 + j, _flip(me, k), sib)
                p.start()
                passed.append(p)
        for a in range(n):
            cp(a, 0, sib, me).wait_recv()
            for j, k in enumerate(chips):
                cp(a, 4 + j, _flip(sib, k), me).wait_recv()
        for f in first + passed:
            f.wait_send()
        for m in mine:
            m.wait()

    vm = pl.BlockSpec(memory_space=pltpu.VMEM)
    return pl.pallas_call(
        body, name="gather_weights",
        out_shape=[jax.ShapeDtypeStruct((8,) + h.shape, h.dtype) for h in halves],
        in_specs=[vm] * n, out_specs=[vm] * n,
        scratch_shapes=[pltpu.SemaphoreType.DMA((n, 7)), pltpu.SemaphoreType.DMA((n, 7)),
                        pltpu.SemaphoreType.DMA((n,))],
        compiler_params=_cp(vmem=VMEM_BIG),
    )(*halves)


def _xfer_start(plan, sems):
    send_sems, recv_sems, lsems = sems
    locals_, xfers = plan
    for i, (src, dst) in enumerate(locals_):
        pltpu.make_async_copy(src, dst, lsems.at[i]).start()
    for i, (src, dst, _, peer) in enumerate(xfers):
        pltpu.make_async_remote_copy(src_ref=src, dst_ref=dst, send_sem=send_sems.at[i], recv_sem=recv_sems.at[i],
                                     device_id=peer, device_id_type=MESH).start()


def _xfer_wait(plan, sems):
    send_sems, recv_sems, lsems = sems
    locals_, xfers = plan
    for i, (src, _, landing, peer) in enumerate(xfers):
        pltpu.make_async_remote_copy(src_ref=src, dst_ref=landing, send_sem=send_sems.at[i], recv_sem=recv_sems.at[i],
                                     device_id=peer, device_id_type=MESH).wait_recv()
    for i, (src, dst, _, peer) in enumerate(xfers):
        pltpu.make_async_remote_copy(src_ref=src, dst_ref=dst, send_sem=send_sems.at[i], recv_sem=recv_sems.at[i],
                                     device_id=peer, device_id_type=MESH).wait_send()
    for i, (src, dst) in enumerate(locals_):
        pltpu.make_async_copy(src, dst, lsems.at[i]).wait()


def _xfer_sems(n_local, n_xfer):
    return [pltpu.SemaphoreType.DMA((n_xfer,)), pltpu.SemaphoreType.DMA((n_xfer,)), pltpu.SemaphoreType.DMA((n_local,))]


def _scatter_plan(ins, outs, pos):
    me = _slot(pos)
    locals_, xfers = [], []
    for src, dst in zip(ins, outs):
        locals_.append((src.at[me], dst.at[me]))
        for k in range(1, 8):
            peer = _flip(pos, k)
            xfers.append((src.at[_slot(peer)], dst.at[me], dst.at[_slot(peer)], peer))
    return locals_, xfers


def _scatter_grads(grads):
    n = len(grads)

    def body(*refs):
        plan = _scatter_plan(refs[:n], refs[n:2 * n], _my_pos())
        _xfer_start(plan, refs[2 * n:])
        _xfer_wait(plan, refs[2 * n:])

    hbm = pl.BlockSpec(memory_space=pltpu.HBM)
    return pl.pallas_call(
        body, name="scatter_grads", out_shape=[jax.ShapeDtypeStruct(g.shape, g.dtype) for g in grads],
        in_specs=[hbm] * n, out_specs=[hbm] * n, scratch_shapes=_xfer_sems(n, 7 * n),
    )(*grads)


def _sibling_exchange(halves):
    n = len(halves)

    def body(*refs):
        ins, outs = refs[:n], refs[n:2 * n]
        send_sems, recv_sems, lsems = refs[2 * n:]
        x, y, c = _my_pos()
        sib = (x, y, 1 - c)
        mine, sends = [], []
        for a in range(n):
            m = pltpu.make_async_copy(ins[a], outs[a].at[c], lsems.at[a])
            m.start()
            mine.append(m)
            cp = pltpu.make_async_remote_copy(
                src_ref=ins[a], dst_ref=outs[a].at[c], send_sem=send_sems.at[a], recv_sem=recv_sems.at[a],
                device_id=sib, device_id_type=MESH)
            cp.start()
            sends.append(cp)
        for a in range(n):
            pltpu.make_async_remote_copy(
                src_ref=ins[a], dst_ref=outs[a].at[1 - c], send_sem=send_sems.at[a], recv_sem=recv_sems.at[a],
                device_id=sib, device_id_type=MESH).wait_recv()
        for cp in sends:
            cp.wait_send()
        for m in mine:
            m.wait()

    vm = pl.BlockSpec(memory_space=pltpu.VMEM)
    return pl.pallas_call(
        body, name="sibling_exchange",
        out_shape=[jax.ShapeDtypeStruct((2,) + h.shape, h.dtype) for h in halves],
        in_specs=[vm] * n, out_specs=[vm] * n,
        scratch_shapes=[pltpu.SemaphoreType.DMA((n,)), pltpu.SemaphoreType.DMA((n,)), pltpu.SemaphoreType.DMA((n,))],
        compiler_params=_cp(vmem=VMEM_BIG),
    )(*halves)


def _row_tile(r, cap=512):
    for t in (cap, 256, 128, 64, 32, 16, 8):
        if t <= cap and r % t == 0:
            return t
    return r


def _cast_bf16(a, name):
    R, W = a.shape
    tr = _row_tile(R, 256)

    def body(a_ref, o_ref):
        o_ref[...] = a_ref[...].astype(BF16)

    return pl.pallas_call(
        body, name=name, grid=(R // tr,), out_shape=jax.ShapeDtypeStruct((R, W), BF16),
        in_specs=[pl.BlockSpec((tr, W), lambda i: (i, 0))], out_specs=pl.BlockSpec((tr, W), lambda i: (i, 0)),
    )(a)


def _sum_slots(r, name):
    _, R, W = r.shape
    tr = _row_tile(R, 128)

    def body(r_ref, o_ref):
        acc = r_ref[0].astype(F32)
        for s_ in range(1, 8):
            acc = acc + r_ref[s_].astype(F32)
        o_ref[...] = acc.astype(BF16)

    return pl.pallas_call(
        body, name=name, grid=(R // tr,), out_shape=jax.ShapeDtypeStruct((R, W), BF16),
        in_specs=[pl.BlockSpec((8, tr, W), lambda i: (0, i, 0))], out_specs=pl.BlockSpec((tr, W), lambda i: (i, 0)),
        compiler_params=_cp(vmem=VMEM_BIG),
    )(r)


def _adamw(w, g, m, v, name):
    R, W = w.shape
    tr = _row_tile(R, 256)
    tc = 256 if (tr == R and R * W * 4 > (1 << 21) and W % 256 == 0) else W
    c1 = 1.0 / (1.0 - ADAM_B1 ** ADAM_STEP)
    c2 = 1.0 / (1.0 - ADAM_B2 ** ADAM_STEP)

    def body(w_ref, g_ref, m_ref, v_ref, d_ref, nm_ref, nv_ref, g32_ref):
        gg = g_ref[...].astype(F32)
        g32_ref[...] = gg
        nm = ADAM_B1 * m_ref[...] + (1.0 - ADAM_B1) * gg
        nv = ADAM_B2 * v_ref[...] + (1.0 - ADAM_B2) * (gg * gg)
        d_ref[...] = -ADAM_LR * ((nm * c1) / (jnp.sqrt(nv * c2) + ADAM_EPS) + ADAM_WD * w_ref[...])
        nm_ref[...] = nm
        nv_ref[...] = nv

    spec = pl.BlockSpec((tr, tc), lambda i, j: (i, j))
    return pl.pallas_call(
        body, name=name, grid=(R // tr, W // tc), out_shape=[jax.ShapeDtypeStruct((R, W), F32)] * 4,
        in_specs=[spec] * 4, out_specs=[spec] * 4, compiler_params=_cp(vmem=VMEM_BIG),
    )(w, g, m, v)


def _matmul_tn(a, b, tm, tn, tk, name, nn=1, b_map=None, out_shape=None, out_block=None, out_map=None,
               b_block=None):
    K, M = a.shape
    nk = K // tk
    nm = M // tm

    def body(a_ref, b_ref, o_ref, acc_ref):
        kk = pl.program_id(2)

        @pl.when(kk == 0)
        def _():
            acc_ref[...] = jnp.zeros_like(acc_ref)

        acc_ref[...] += lax.dot_general(a_ref[...], b_ref[...], _TN, preferred_element_type=F32)

        @pl.when(kk == nk - 1)
        def _():
            o_ref[...] = acc_ref[...].astype(BF16)

    return pl.pallas_call(
        body, name=name, grid=(nm, nn, nk),
        out_shape=jax.ShapeDtypeStruct(out_shape or (M, b.shape[-1]), BF16),
        in_specs=[pl.BlockSpec((tk, tm), lambda i, j, k: (k, i)),
                  pl.BlockSpec(b_block or (tk, tn), b_map or (lambda i, j, k: (k, j)))],
        out_specs=pl.BlockSpec(out_block or (tm, tn), out_map or (lambda i, j, k: (i, j))),
        scratch_shapes=[pltpu.VMEM((tm, tn), F32)],
        compiler_params=_cp(vmem=VMEM_BIG),
    )(a, b)


def _ada_fwd(cin, w_ada_s, b_s):
    tn = 512

    def body(c_ref, w_ref, b_ref, o_ref):
        o_ref[...] = _dg(_silu(c_ref[...]), w_ref[...], _NN) + b_ref[...]

    return pl.pallas_call(
        body, name="ada_fwd", grid=(1536 // tn,), out_shape=jax.ShapeDtypeStruct((64, 1536), F32),
        in_specs=[pl.BlockSpec((64, D), lambda j: (0, 0)), pl.BlockSpec((D, tn), lambda j: (0, j)),
                  pl.BlockSpec((1, tn), lambda j: (0, j))],
        out_specs=pl.BlockSpec((64, tn), lambda j: (0, j)),
        compiler_params=_cp(vmem=VMEM_BIG),
    )(cin, w_ada_s, b_s)


def _ada_bwd(cin, gmod_s, w_ada_s):
    tn = 512

    def body(c_ref, g_ref, w_ref, gw_ref, pc_ref):
        j = pl.program_id(0)
        g = g_ref[...]
        gw_ref[...] = _dg(_silu(c_ref[...]), g, _TN)
        rr = lax.broadcasted_iota(jnp.int32, (8, 64), 0)
        cc = lax.broadcasted_iota(jnp.int32, (8, 64), 1)
        sel = jnp.where((rr == 0) & ((cc & 7) == 2), 1.0, 0.0).astype(F32)
        dctx = _hmm(sel, g)
        part = _dg(dctx, w_ref[...], _NT)

        @pl.when(j == 0)
        def _():
            pc_ref[...] = jnp.zeros_like(pc_ref)

        pc_ref[...] += part

    return pl.pallas_call(
        body, name="ada_bwd", grid=(1536 // tn,),
        out_shape=[jax.ShapeDtypeStruct((D, 1536), F32), jax.ShapeDtypeStruct((8, D), F32)],
        in_specs=[pl.BlockSpec((64, D), lambda j: (0, 0)), pl.BlockSpec((64, tn), lambda j: (0, j)),
                  pl.BlockSpec((D, tn), lambda j: (0, j))],
        out_specs=[pl.BlockSpec((D, tn), lambda j: (0, j)), pl.BlockSpec((8, D), lambda j: (0, 0))],
        compiler_params=_cp(vmem=VMEM_BIG),
    )(cin, gmod_s, w_ada_s)


def _ada_small(gmod, parts, c_ctx):
    def body(g_ref, p_ref, c_ref, gb_ref, gc_ref):
        gb_ref[...] = jnp.sum(g_ref[...], axis=0, keepdims=True)
        ds = p_ref[0][0:1] + p_ref[2][0:1] + p_ref[4][0:1] + p_ref[6][0:1]
        cc = c_ref[...]
        s = _sigmoid(cc)
        gc_ref[...] = ds * (s * (1.0 + cc * (1.0 - s)))

    vm = pl.BlockSpec(memory_space=pltpu.VMEM)
    return pl.pallas_call(
        body, name="ada_small",
        out_shape=[jax.ShapeDtypeStruct((1, 6 * D), F32), jax.ShapeDtypeStruct((1, D), F32)],
        in_specs=[vm, vm, vm], out_specs=[vm, vm],
    )(gmod, parts, c_ctx)


def _tok_specs():
    xs = pl.BlockSpec((None, RT, D), lambda b, j: (b, jnp.maximum(j - 1, 0), 0))
    cs = pl.BlockSpec((None, RT, D), lambda b, j: (b, 0, 0))
    ms = pl.BlockSpec((None, 1, D), lambda b, j: (b, 0, 0))
    cms = pl.BlockSpec((None, 1, D), lambda b, j: (0, 0, 0))
    return xs, cs, ms, cms


def _inproj_fwd(x, ctx, sh1, sc1, csh1, csc1, n1w, w_in_p):
    def body(x_ref, c_ref, sh_ref, sc_ref, csh_ref, csc_ref, w1_ref, w_ref, p_ref, hm_ref):
        is_ctx = pl.program_id(1) == 0
        xt = jnp.where(is_ctx, c_ref[...], x_ref[...])
        sh = jnp.where(is_ctx, csh_ref[...], sh_ref[...])
        sc = jnp.where(is_ctx, csc_ref[...], sc_ref[...])
        hb = _norm_mod(xt, w1_ref[...], sh, sc).astype(BF16)
        hm_ref[...] = hb
        p_ref[...] = lax.dot_general(hb, w_ref[...], _NT, preferred_element_type=F32)

    xs, cs, ms, cms = _tok_specs()
    return pl.pallas_call(
        body, name="inproj_fwd", grid=(NB, T // RT),
        out_shape=[jax.ShapeDtypeStruct((NB, T, PC), F32), jax.ShapeDtypeStruct((NB, T, D), BF16)],
        in_specs=[xs, cs, ms, ms, cms, cms, pl.BlockSpec((1, D), lambda b, j: (0, 0)),
                  pl.BlockSpec((PC, D), lambda b, j: (0, 0))],
        out_specs=[pl.BlockSpec((None, RT, PC), lambda b, j: (b, j, 0)),
                   pl.BlockSpec((None, RT, D), lambda b, j: (b, j, 0))],
        compiler_params=_cp(vmem=VMEM_BIG),
    )(x, ctx, sh1, sc1, csh1, csc1, n1w, w_in_p)


def _inproj_bwd(dp_dn, dz, dp_na, dgt, w_in_p, x, ctx, sh1, sc1, csh1, csc1, n1w, dx1):
    def body(d1q_ref, d1k_ref, d1v_ref, d2_ref, d3_ref, d4_ref, w_ref, x_ref, c_ref, sh_ref, sc_ref, csh_ref, csc_ref,
             w1_ref, dx1_ref, dpb_ref, gx_ref, dsh_ref, dsc_ref, dcsh_ref, dcsc_ref, dw1_ref):
        b, j = pl.program_id(0), pl.program_id(1)
        is_ctx = j == 0
        dp = jnp.concatenate([d1q_ref[...], d1k_ref[...], d1v_ref[...], d2_ref[...], d3_ref[...], d4_ref[...]],
                             axis=1).astype(BF16)
        dpb_ref[...] = dp
        dhm = jnp.dot(dp, w_ref[...], preferred_element_type=F32)
        xt = jnp.where(is_ctx, c_ref[...], x_ref[...])
        sh = jnp.where(is_ctx, csh_ref[...], sh_ref[...])
        sc = jnp.where(is_ctx, csc_ref[...], sc_ref[...])
        _, vjp = jax.vjp(_norm_mod, xt, w1_ref[...], sh, sc)
        dxt, dw1, dsh, dsc = vjp(dhm)

        @pl.when((b == 0) & (j == 0))
        def _():
            dw1_ref[...] = jnp.zeros_like(dw1_ref)
            dcsh_ref[...] = jnp.zeros_like(dcsh_ref)
            dcsc_ref[...] = jnp.zeros_like(dcsc_ref)

        dw1_ref[...] += dw1

        @pl.when(is_ctx)
        def _():
            dcsh_ref[...] += dsh
            dcsc_ref[...] += dsc

        @pl.when(j == 1)
        def _():
            dsh_ref[...] = jnp.zeros_like(dsh_ref)
            dsc_ref[...] = jnp.zeros_like(dsc_ref)

        @pl.when(j >= 1)
        def _():
            dsh_ref[...] += dsh
            dsc_ref[...] += dsc
            gx_ref[...] = dx1_ref[...] + dxt

    xs, cs, ms, cms = _tok_specs()
    tsp = lambda w: pl.BlockSpec((None, RT, w), lambda b, j: (b, j, 0))
    one = pl.BlockSpec((1, D), lambda b, j: (0, 0))
    return pl.pallas_call(
        body, name="inproj_bwd", grid=(NB, T // RT),
        out_shape=[jax.ShapeDtypeStruct((NB, T, PC), BF16), jax.ShapeDtypeStruct((NB, SEQ, D), F32),
                   jax.ShapeDtypeStruct((NB, 1, D), F32), jax.ShapeDtypeStruct((NB, 1, D), F32),
                   jax.ShapeDtypeStruct((1, 1, D), F32), jax.ShapeDtypeStruct((1, 1, D), F32),
                   jax.ShapeDtypeStruct((1, D), F32)],
        in_specs=[tsp(512), tsp(512), tsp(512), tsp(512), tsp(1536), tsp(128), pl.BlockSpec((PC, D), lambda b, j: (0, 0)),
                  xs, cs, ms, ms, cms, cms, one, _rows2d_spec(D)],
        out_specs=[tsp(PC), xs, ms, ms, cms, cms, one],
        compiler_params=_cp(vmem=VMEM_BIG),
    )(*dp_dn, dz, dp_na, dgt, w_in_p, x, ctx, sh1, sc1, csh1, csc1, n1w, dx1)


def _seg_valid(t, off):
    lo = jnp.where(t < CT, 0, CT)
    hi = jnp.where(t < CT, CT, T)
    return (t + off >= lo) & (t + off < hi)


def _shifted(u, off):
    t = lax.broadcasted_iota(jnp.int32, (T, 1), 0)
    s = u if off == 0 else pltpu.roll(u, (-off) % T, 0)
    return jnp.where(_seg_valid(t, off), s, 0.0)


def _conv_pre(u, w):
    acc = _shifted(u, -2) * w[0:1, :]
    for kk in range(1, 5):
        acc = acc + _shifted(u, kk - 2) * w[kk:kk + 1, :]
    return acc


def _act_q(pre):
    a = _silu(pre)
    return a * lax.rsqrt(jnp.sum(a * a, axis=-1, keepdims=True) + EPS) * (128.0 ** -0.5)


def _act_k(pre):
    a = _silu(pre)
    return a * lax.rsqrt(jnp.sum(a * a, axis=-1, keepdims=True) + EPS)


def _dn_prep_fwd(p, cw):
    def body(pq_ref, pk_ref, pv_ref, wq_ref, wk_ref, wv_ref, q_ref, k_ref, v_ref):
        q_ref[...] = _act_q(_conv_pre(pq_ref[...], wq_ref[...]))
        k_ref[...] = _act_k(_conv_pre(pk_ref[...], wk_ref[...]))
        v_ref[...] = _silu(_conv_pre(pv_ref[...], wv_ref[...]))

    ps = lambda o: pl.BlockSpec((None, T, 128), lambda b, h: (b, 0, o + h))
    ws = lambda o: pl.BlockSpec((8, 128), lambda b, h: (0, o + h))
    os_ = pl.BlockSpec((None, T, 128), lambda b, h: (b, 0, h))
    return pl.pallas_call(
        body, name="dn_prep_fwd", grid=(NB, DNH), out_shape=[jax.ShapeDtypeStruct((NB, T, DNW), F32)] * 3,
        in_specs=[ps(0), ps(4), ps(8), ws(0), ws(4), ws(8)], out_specs=[os_] * 3,
        compiler_params=_cp(vmem=VMEM_BIG),
    )(p, p, p, cw, cw, cw)


def _dn_prep_bwd(p, cw, dqs, dks, dvs):
    def body(pq_ref, pk_ref, pv_ref, wq_ref, wk_ref, wv_ref, dqf_ref, dqb_ref, dkf_ref, dkb_ref, dvf_ref, dvb_ref,
             oq_ref, ok_ref, ov_ref, gq_ref, gk_ref, gv_ref):
        b = pl.program_id(1)

        def one(p_ref, w_ref, df_ref, db_ref, act, o_ref, g_ref):
            u = p_ref[...]
            w = w_ref[...]
            _, vjp = jax.vjp(act, _conv_pre(u, w))
            (dpre,) = vjp(df_ref[...] + db_ref[...])
            du = _shifted(dpre, 2) * w[0:1, :]
            rows = [jnp.sum(dpre * _shifted(u, -2), axis=0, keepdims=True)]
            for kk in range(1, 5):
                du = du + _shifted(dpre, 2 - kk) * w[kk:kk + 1, :]
                rows.append(jnp.sum(dpre * _shifted(u, kk - 2), axis=0, keepdims=True))
            o_ref[...] = du
            gw = jnp.concatenate(rows + [jnp.zeros((3, 128), F32)], axis=0)

            @pl.when(b == 0)
            def _():
                g_ref[...] = gw

            @pl.when(b > 0)
            def _():
                g_ref[...] += gw

        one(pq_ref, wq_ref, dqf_ref, dqb_ref, _act_q, oq_ref, gq_ref)
        one(pk_ref, wk_ref, dkf_ref, dkb_ref, _act_k, ok_ref, gk_ref)
        one(pv_ref, wv_ref, dvf_ref, dvb_ref, _silu, ov_ref, gv_ref)

    ps = lambda o: pl.BlockSpec((None, T, 128), lambda h, b: (b, 0, o + h))
    ws = lambda o: pl.BlockSpec((8, 128), lambda h, b: (0, o + h))
    ds = pl.BlockSpec((None, T, 128), lambda h, b: (b, 0, h))
    outs = pl.pallas_call(
        body, name="dn_prep_bwd", grid=(DNH, NB),
        out_shape=[jax.ShapeDtypeStruct((NB, T, DNW), F32)] * 3 + [jax.ShapeDtypeStruct((8, DNW), F32)] * 3,
        in_specs=[ps(0), ps(4), ps(8), ws(0), ws(4), ws(8)] + [ds] * 6,
        out_specs=[ds] * 3
        + [pl.BlockSpec((8, 128), lambda h, b: (0, h))] * 3,
        compiler_params=_cp(vmem=VMEM_BIG),
    )(p, p, p, cw, cw, cw, *dqs, *dks, *dvs)
    return outs


def _gate_fn(a, prm):
    lane = lax.broadcasted_iota(jnp.int32, a.shape, 1)
    z = a + prm[1:2, :]
    sp = jnp.maximum(z, 0.0) + jnp.log(1.0 + jnp.exp(-jnp.abs(z)))
    g = -jnp.exp(prm[0:1, :]) * sp
    return jnp.where(lane < 8, g, jnp.where(lane < 16, _sigmoid(a), 0.0))


def _gates_fwd(p, prm):
    def body(a_ref, prm_ref, o_ref):
        o_ref[...] = _gate_fn(a_ref[...], prm_ref[...])

    return pl.pallas_call(
        body, name="gates_fwd", grid=(NB,), out_shape=jax.ShapeDtypeStruct((NB, T, 128), F32),
        in_specs=[pl.BlockSpec((None, T, 128), lambda b: (b, 0, COL_G // 128)), pl.BlockSpec((8, 128), lambda b: (0, 0))],
        out_specs=pl.BlockSpec((None, T, 128), lambda b: (b, 0, 0)),
    )(p, prm)


def _gates_bwd(p, prm, dgf, dgb):
    def body(a_ref, prm_ref, dgf_ref, dgb_ref, o_ref, dprm_ref):
        b = pl.program_id(0)
        tot = dgf_ref[...] + dgb_ref[...]
        _, vjp = jax.vjp(_gate_fn, a_ref[...], prm_ref[...])
        da, dprm = vjp(tot)
        o_ref[...] = da

        @pl.when(b == 0)
        def _():
            dprm_ref[...] = dprm

        @pl.when(b > 0)
        def _():
            dprm_ref[...] += dprm

    return pl.pallas_call(
        body, name="gates_bwd", grid=(NB,),
        out_shape=[jax.ShapeDtypeStruct((NB, T, 128), F32), jax.ShapeDtypeStruct((8, 128), F32)],
        in_specs=[pl.BlockSpec((None, T, 128), lambda b: (b, 0, COL_G // 128)), pl.BlockSpec((8, 128), lambda b: (0, 0)),
                  pl.BlockSpec((None, T, 128), lambda b: (b, 0, 0)), pl.BlockSpec((None, T, 128), lambda b: (b, 0, 0))],
        out_specs=[pl.BlockSpec((None, T, 128), lambda b: (b, 0, 0)), pl.BlockSpec((8, 128), lambda b: (0, 0))],
        compiler_params=_cp(vmem=VMEM_BIG),
    )(p, prm, dgf, dgb)


def _lockstep(gens, joint=None):
    n = len(gens)
    res, vals = [None] * n, [None] * n
    while True:
        msgs = []
        for i, g in enumerate(gens):
            try:
                msgs.append((g.send(vals[i]), False))
            except StopIteration as e:
                res[i] = e.value
                msgs.append((None, True))
        if msgs[0][1]:
            return res
        if msgs[0][0] is not None:
            vals = list(joint(tuple(m[0] for m in msgs)))
        else:
            vals = [None] * n


def _tri_inv_gen(m, md, m1, m2, eye):
    n = -(m * md)
    t = eye + n
    pw = n
    for _ in range(3):
        pw = _dg(pw, pw, _NN)
        yield
        t = t + _dg(t, pw, _NN)
        yield
    a = _dg(t, m * m1, _NN)
    yield
    t = t - _dg(a, t, _NN)
    yield
    a = _dg(t, m * m2, _NN)
    yield
    return t - _dg(a, t, _NN)


def _tri_inv_bwd_gen(t, g):
    tt = t.T
    a = _dg(tt, g, _NN)
    yield
    return -_dg(a, tt, _NN)


@jax.custom_vjp
def _tri_inv(ms, md, m1, m2, eye):
    return tuple(_lockstep([_tri_inv_gen(m, md, m1, m2, eye) for m in ms]))


def _tri_inv_f(ms, md, m1, m2, eye):
    ts = _tri_inv(ms, md, m1, m2, eye)
    return ts, (ts, md)


def _tri_inv_b(res, gs):
    ts, md = res
    z = jnp.zeros_like(md)
    return (tuple(_lockstep([_tri_inv_bwd_gen(t, g) for t, g in zip(ts, gs)])), z, z, z, z)


_tri_inv.defvjp(_tri_inv_f, _tri_inv_b)


@jax.custom_vjp
def _tri_inv_saved(ms, ts):
    return tuple(t.astype(F32) for t in ts)


def _tri_inv_saved_f(ms, ts):
    return _tri_inv_saved(ms, ts), ts


def _tri_inv_saved_b(ts, gs):
    dms = tuple(_lockstep([_tri_inv_bwd_gen(t, g) for t, g in zip(ts, gs)]))
    return dms, tuple(jnp.zeros_like(t) for t in ts)


_tri_inv_saved.defvjp(_tri_inv_saved_f, _tri_inv_saved_b)


SR = DNH * CH


def _scan_masks():
    i, j = np.arange(SR)[:, None], np.arange(SR)[None, :]
    same = (i >> 6) == (j >> 6)
    cm = np.zeros((2, 6, SR, SR), np.float32)
    for d in range(2):
        diff = i - j if d == 0 else j - i
        cm[d, 0] = same & (diff >= 0)
        cm[d, 1] = same & (diff > 0)
        cm[d, 2] = (i >> 4) == (j >> 4)
        cm[d, 3] = ((i >> 5) == (j >> 5)) & ((i >> 4) != (j >> 4))
        cm[d, 4] = (i >> 5) != (j >> 5)
        cm[d, 5] = i == j
    t, u = np.arange(CH)[:, None], np.arange(CH)[None, :]
    c64 = np.stack([t >= u, t <= u]).astype(np.float32)
    bm = ((np.arange(SR)[:, None] >> 6) == (np.arange(DNW)[None, :] >> 7)).astype(np.float32)
    return jnp.asarray(cm), jnp.asarray(c64), jnp.asarray(bm)


@jax.custom_vjp
def _fold(x, bm):
    xm = x * bm
    return xm[:, 0:128] + xm[:, 128:256] + xm[:, 256:384] + xm[:, 384:512]


def _fold_f(x, bm):
    return _fold(x, bm), bm


def _fold_b(bm, g):
    return jnp.concatenate([g, g, g, g], axis=1) * bm, jnp.zeros_like(bm)


_fold.defvjp(_fold_f, _fold_b)


def _chunk_gen(q, k, v, gb, s, cm, c64, bm, d):
    incl, strict = cm[0], cm[1]
    lane = lax.broadcasted_iota(jnp.int32, (1, 128), 1)
    oh = lambda n: jnp.where(lane == n, 1.0, 0.0).astype(F32)
    col = lambda x, base: jnp.concatenate(
        [jnp.sum(x * oh(base + d * DNH + h), axis=1, keepdims=True) for h in range(DNH)], axis=0)
    beta = col(gb, 8)
    gam = col(_mask_mm(c64, gb), 0)
    yield
    tot = jnp.sum(gb, axis=0, keepdims=True)
    gtots = [jnp.sum(tot * oh(d * DNH + h), axis=1, keepdims=True) for h in range(DNH)]
    gtot = jnp.concatenate([jnp.broadcast_to(t, (CH, 1)) for t in gtots], axis=0)
    eg = jnp.concatenate([jnp.broadcast_to(jnp.exp(t), (1, 128)) for t in gtots], axis=1)
    row0 = jnp.where(lax.broadcasted_iota(jnp.int32, (128, 1), 0) == 0, 1.0, 0.0).astype(F32)
    gam_row = jnp.sum(jnp.broadcast_to(gam, (SR, 128)).T * row0, axis=0, keepdims=True)
    dec = jnp.exp(jnp.where(incl > 0.5, gam - gam_row, NEG))
    e_gam = jnp.exp(gam)
    kb = k * beta
    kk = _mm_nt(kb, k)
    yield
    tm = yield strict * (kk * dec)
    u = _mm(tm, v * beta)
    yield
    w = _mm(tm, kb * e_gam)
    yield
    qk = _mm_nt(q, k) * dec
    yield
    ws = _mm(w, s)
    yield
    v_new = u - _fold(ws, bm)
    qs = _mm(q * e_gam, s)
    yield
    o = _fold(qs, bm) + _mm(qk, v_new)
    yield
    vbd = jnp.concatenate([v_new, v_new, v_new, v_new], axis=1) * bm
    s_new = s * eg + _mm_tn(k * jnp.exp(gtot - gam), vbd)
    return o, s_new, tm


def _chunks(args, cm_ref, c64_ref, bm, saved=None):
    cm0 = cm_ref[0]
    if saved is None:
        inv = lambda ms: _tri_inv(ms, cm0[2], cm0[3], cm0[4], cm0[5])
    else:
        inv = lambda ms: _tri_inv_saved(ms, saved)
    return _lockstep([_chunk_gen(q, k, v, gb, s, cm_ref[d], c64_ref[d], bm, d) for q, k, v, gb, s, d in args], inv)


def _chunk_rev(s):
    return jnp.where(s < NCH_CTX, NCH_CTX - 1 - s, NCH + NCH_CTX - 1 - s)


def _scan_specs(step):
    cf = lambda s: step(s)
    cb = lambda s: _chunk_rev(step(s))
    tok = lambda c: pl.BlockSpec((NB, CH, DNW), lambda s: (0, c(s), 0))
    gat = lambda c: pl.BlockSpec((NB, CH, 128), lambda s: (0, c(s), 0))
    sta = lambda c: pl.BlockSpec((NB, None, 128, DNW), lambda s: (0, c(s), 0, 0))
    inv = lambda c: pl.BlockSpec((NB, None, SR, SR), lambda s: (0, c(s), 0, 0))
    return (tok(cf), gat(cf), sta(cf), inv(cf)), (tok(cb), gat(cb), sta(cb), inv(cb))


def _const_specs():
    return [pl.BlockSpec((2, 6, SR, SR), lambda s: (0, 0, 0, 0)), pl.BlockSpec((2, CH, CH), lambda s: (0, 0, 0)),
            pl.BlockSpec((SR, DNW), lambda s: (0, 0))]


SCAN_CHAINS = [(b, d) for b in range(NB) for d in range(2)]


def _stack(ref, b):
    return jnp.concatenate([ref[b, :, 128 * h:128 * h + 128] for h in range(DNH)], axis=0)


def _unstack_to(ref, b, val):
    for h in range(DNH):
        ref[b, :, 128 * h:128 * h + 128] = val[CH * h:CH * h + CH]


def _gather_plan(ins, outs, pos):
    me = _slot(pos)
    locals_, xfers = [], []
    for src, dst in zip(ins, outs):
        locals_.append((src, dst.at[me]))
        for k in range(1, 8):
            peer = _flip(pos, k)
            xfers.append((src, dst.at[me], dst.at[_slot(peer)], peer))
    return locals_, xfers


def _scan_fwd(q, k, v, gates, carry=()):
    nc = len(carry)

    def body(*refs):
        cm_ref, c64_ref, bm_ref, qf, kf, vf, gf, qb, kb, vb, gb = refs[:11]
        c_in = refs[11:11 + nc]
        of_ref, sf_ref, tf_ref, ob_ref, sb_ref, tb_ref = refs[11 + nc:17 + nc]
        c_out = refs[17 + nc:17 + 2 * nc]
        s_ref = refs[17 + 2 * nc]
        sems = refs[18 + 2 * nc:]
        if nc:
            @pl.when(pl.program_id(0) == 0)
            def _():
                _xfer_start(_gather_plan(c_in, c_out, _my_pos()), sems)

        @pl.when(pl.program_id(0) == 0)
        def _():
            s_ref[...] = jnp.zeros_like(s_ref)

        bm = bm_ref[...]
        dirs = ((qf, kf, vf, gf, of_ref, sf_ref, tf_ref), (qb, kb, vb, gb, ob_ref, sb_ref, tb_ref))
        args = [(_stack(dirs[d][0], b), _stack(dirs[d][1], b), _stack(dirs[d][2], b), dirs[d][3][b], s_ref[i], d)
                for i, (b, d) in enumerate(SCAN_CHAINS)]
        for i, (o, s_new, tm) in enumerate(_chunks(args, cm_ref, c64_ref, bm)):
            b, d = SCAN_CHAINS[i]
            dirs[d][5][b] = args[i][4]
            dirs[d][6][b] = tm.astype(BF16)
            _unstack_to(dirs[d][4], b, o)
            s_ref[i] = s_new

        if nc:
            @pl.when(pl.program_id(0) == NCH - 1)
            def _():
                _xfer_wait(_gather_plan(c_in, c_out, _my_pos()), sems)

    (tf, gf_, sf_, if_), (tb, gb_, sb_, ib_) = _scan_specs(lambda s: s)
    o_sh = jax.ShapeDtypeStruct((NB, T, DNW), F32)
    st_sh = jax.ShapeDtypeStruct((NB, NCH, 128, DNW), F32)
    inv_sh = jax.ShapeDtypeStruct((NB, NCH, SR, SR), BF16)
    hbm = pl.BlockSpec(memory_space=pltpu.HBM)
    return pl.pallas_call(
        body, name="scan_fwd", grid=(NCH,),
        out_shape=[o_sh, st_sh, inv_sh, o_sh, st_sh, inv_sh] + [jax.ShapeDtypeStruct((8,) + h.shape, h.dtype) for h in carry],
        in_specs=_const_specs() + [tf, tf, tf, gf_, tb, tb, tb, gb_] + [hbm] * nc,
        out_specs=[tf, sf_, if_, tb, sb_, ib_] + [hbm] * nc,
        scratch_shapes=[pltpu.VMEM((len(SCAN_CHAINS), 128, DNW), F32)] + (_xfer_sems(nc, 7 * nc) if nc else []),
        compiler_params=_cp(vmem=VMEM_BIG),
    )(*_scan_masks(), q, k, v, gates, q, k, v, gates, *carry)


def _scan_bwd(q, k, v, gates, st_f, st_b, inv_f, inv_b, do, carry=()):
    nc = len(carry)

    def body(*refs):
        (cm_ref, c64_ref, bm_ref, qf, kf, vf, gf, sf, tf_, dof, qb, kb, vb, gb, sb, tb_, dob) = refs[:17]
        c_in = refs[17:17 + nc]
        dqf, dkf, dvf, dgf, dqb, dkb, dvb, dgb = refs[17 + nc:25 + nc]
        c_out = refs[25 + nc:25 + 2 * nc]
        ds_ref = refs[25 + 2 * nc]
        sems = refs[26 + 2 * nc:]
        first = pl.program_id(0) == 0
        last = pl.program_id(0) == NCH - 1
        if nc:
            @pl.when(first)
            def _():
                _xfer_start(_scatter_plan(c_in, c_out, _my_pos()), sems)

        @pl.when(first)
        def _():
            ds_ref[...] = jnp.zeros_like(ds_ref)

        bm = bm_ref[...]
        dirs = ((qf, kf, vf, gf, sf, dof, dqf, dkf, dvf, dgf), (qb, kb, vb, gb, sb, dob, dqb, dkb, dvb, dgb))
        prim = tuple((_stack(dirs[d][0], b), _stack(dirs[d][1], b), _stack(dirs[d][2], b), dirs[d][3][b], dirs[d][4][b])
                     for b, d in SCAN_CHAINS)
        cots = tuple((_stack(dirs[d][5], b), ds_ref[i]) for i, (b, d) in enumerate(SCAN_CHAINS))
        saved = tuple((tf_, tb_)[d][b] for b, d in SCAN_CHAINS)
        _, vjp = jax.vjp(lambda ps: tuple(r[:2] for r in _chunks([p + (SCAN_CHAINS[i][1],) for i, p in enumerate(ps)],
                                                                cm_ref, c64_ref, bm, saved)), prim)
        (grads,) = vjp(cots)
        for i, (dq, dk, dv, dg, ds) in enumerate(grads):
            b, d = SCAN_CHAINS[i]
            _unstack_to(dirs[d][6], b, dq)
            _unstack_to(dirs[d][7], b, dk)
            _unstack_to(dirs[d][8], b, dv)
            dirs[d][9][b] = dg
            ds_ref[i] = ds

        if nc:
            @pl.when(last)
            def _():
                _xfer_wait(_scatter_plan(c_in, c_out, _my_pos()), sems)

    (tf, gf_, sf_, if_), (tb, gb_, sb_, ib_) = _scan_specs(lambda s: NCH - 1 - s)
    o_sh = jax.ShapeDtypeStruct((NB, T, DNW), F32)
    g_sh = jax.ShapeDtypeStruct((NB, T, 128), F32)
    hbm = pl.BlockSpec(memory_space=pltpu.HBM)
    return pl.pallas_call(
        body, name="scan_bwd", grid=(NCH,),
        out_shape=[o_sh, o_sh, o_sh, g_sh] * 2 + [jax.ShapeDtypeStruct(g.shape, g.dtype) for g in carry],
        in_specs=_const_specs() + [tf, tf, tf, gf_, sf_, if_, tf, tb, tb, tb, gb_, sb_, ib_, tb] + [hbm] * nc,
        out_specs=[tf, tf, tf, gf_, tb, tb, tb, gb_] + [hbm] * nc,
        scratch_shapes=[pltpu.VMEM((len(SCAN_CHAINS), 128, DNW), F32)] + (_xfer_sems(nc, 7 * nc) if nc else []),
        compiler_params=_cp(vmem=VMEM_BIG),
    )(*_scan_masks(), q, k, v, gates, st_f, inv_f, do, q, k, v, gates, st_b, inv_b, do, *carry)


def _segmean(a, bd):
    hi = a.astype(BF16)
    lo = (a - hi.astype(F32)).astype(BF16)
    return jnp.dot(hi, bd, preferred_element_type=F32) + jnp.dot(lo, bd, preferred_element_type=F32)


def _na_prep_fwd(p, qw, kw, bd):
    def body(q_ref, k_ref, v_ref, qw_ref, kw_ref, bd_ref, oq_ref, ok_ref, ov_ref):
        bdm = bd_ref[...]
        q = q_ref[...]
        k = k_ref[...]
        oq_ref[...] = (q * lax.rsqrt(_segmean(q * q, bdm) + EPS) * qw_ref[...] * 0.125).astype(BF16)
        ok_ref[...] = (k * lax.rsqrt(_segmean(k * k, bdm) + EPS) * kw_ref[...]).astype(BF16)
        ov_ref[...] = v_ref[...].astype(BF16)

    ps = lambda o: pl.BlockSpec((None, RT, NAW), lambda b, j: (b, j, COL_NA // NAW + o))
    one = pl.BlockSpec((1, NAW), lambda b, j: (0, 0))
    os_ = pl.BlockSpec((None, RT, NAW), lambda b, j: (b, j, 0))
    return pl.pallas_call(
        body, name="na_prep_fwd", grid=(NB, T // RT), out_shape=[jax.ShapeDtypeStruct((NB, T, NAW), BF16)] * 3,
        in_specs=[ps(0), ps(1), ps(2), one, one, pl.BlockSpec((NAW, NAW), lambda b, j: (0, 0))], out_specs=[os_] * 3,
    )(p, p, p, qw, kw, bd)


def _na_prep_bwd(p, qw, kw, bd, dq, dk, dv):
    def body(q_ref, k_ref, qw_ref, kw_ref, bd_ref, dq_ref, dk_ref, dv_ref, o_ref, gq_ref, gk_ref):
        b, j = pl.program_id(0), pl.program_id(1)
        bdm = bd_ref[...]

        def one(x, w, dy, scale):
            r = lax.rsqrt(_segmean(x * x, bdm) + EPS)
            g = dy * w * scale
            dx = r * g - x * (r * r * r) * _segmean(g * x, bdm)
            dw = jnp.sum(dy * x * r * scale, axis=0, keepdims=True)
            return dx, dw

        dqv = jnp.where(j == 0, 0.0, dq_ref[...])
        dxq, dwq = one(q_ref[...], qw_ref[...], dqv, 0.125)
        dxk, dwk = one(k_ref[...], kw_ref[...], dk_ref[...], 1.0)
        o_ref[...] = jnp.concatenate([dxq, dxk, dv_ref[...]], axis=1)
        pad = jnp.zeros((7, NAW), F32)

        @pl.when((b == 0) & (j == 0))
        def _():
            gq_ref[...] = jnp.zeros_like(gq_ref)
            gk_ref[...] = jnp.zeros_like(gk_ref)

        gq_ref[...] += jnp.concatenate([dwq, pad], axis=0)
        gk_ref[...] += jnp.concatenate([dwk, pad], axis=0)

    ps = lambda o: pl.BlockSpec((None, RT, NAW), lambda b, j: (b, j, COL_NA // NAW + o))
    one = pl.BlockSpec((1, NAW), lambda b, j: (0, 0))
    ts = pl.BlockSpec((None, RT, NAW), lambda b, j: (b, j, 0))
    acc = pl.BlockSpec((8, NAW), lambda b, j: (0, 0))
    return pl.pallas_call(
        body, name="na_prep_bwd", grid=(NB, T // RT),
        out_shape=[jax.ShapeDtypeStruct((NB, T, 1536), F32), jax.ShapeDtypeStruct((8, NAW), F32),
                   jax.ShapeDtypeStruct((8, NAW), F32)],
        in_specs=[ps(0), ps(1), one, one, pl.BlockSpec((NAW, NAW), lambda b, j: (0, 0)),
                  pl.BlockSpec((None, RT, NAW), lambda b, j: (b, jnp.maximum(j - 1, 0), 0)), ts, ts],
        out_specs=[pl.BlockSpec((None, RT, 1536), lambda b, j: (b, j, 0)), acc, acc],
        compiler_params=_cp(vmem=VMEM_BIG),
    )(p, p, qw, kw, bd, dq, dk, dv)


def _na_row0(r):
    return jnp.clip(r - WIN_ROWS // 2, 0, ROWS - WIN_ROWS)


def _na_cls(r):
    return _na_row0(r) - r + WIN_ROWS - 1


NHS = 4
NLW = NHS * 64
NSR = NHS * GRID_W


def _na_stack(x):
    lane = lax.broadcasted_iota(jnp.int32, (1, NLW), 1)
    return jnp.concatenate([jnp.where((lane >> 6) == h, x, jnp.zeros_like(x)) for h in range(NHS)], axis=0)


def _na_unstack(y):
    lane = lax.broadcasted_iota(jnp.int32, (1, NLW), 1)
    out = jnp.zeros((GRID_W, NLW), y.dtype)
    for h in range(NHS):
        out = jnp.where((lane >> 6) == h, y[GRID_W * h:GRID_W * h + GRID_W], out)
    return out


def _na_probs(q4, kw, kc, bias):
    s_w = lax.dot_general(q4, kw, _NT, preferred_element_type=F32) + bias
    s_c = lax.dot_general(q4, kc, _NT, preferred_element_type=F32)
    mx = jnp.maximum(jnp.max(s_w, axis=1, keepdims=True), jnp.max(s_c, axis=1, keepdims=True))
    p_w = jnp.exp(s_w - mx)
    p_c = jnp.exp(s_c - mx)
    inv = 1.0 / (jnp.sum(p_w, axis=1, keepdims=True) + jnp.sum(p_c, axis=1, keepdims=True))
    return p_w, p_c, inv


def _na_specs():
    qs = pl.BlockSpec((None, GRID_W, NLW), lambda g, b, r: (b, CT // GRID_W + r, g))
    kv = pl.BlockSpec((None, T, NLW), lambda g, b, r: (b, 0, g))
    bs = pl.BlockSpec((None, None, NSR, NWIN), lambda g, b, r: (_na_cls(r), g, 0, 0))
    lat = pl.BlockSpec((None, GRID_W, NLW), lambda g, b, r: (b, r, g))
    return qs, kv, bs, lat


def _na_fwd(qn, kn, vb, bias, carry=()):
    nc = len(carry)
    grid = (NAH // NHS, NB, ROWS)

    def body(*refs):
        q_ref, k_ref, v_ref, b_ref = refs[:4]
        c_in, o_ref, c_out, sems = refs[4:4 + nc], refs[4 + nc], refs[5 + nc:5 + 2 * nc], refs[5 + 2 * nc:]
        ids = [pl.program_id(a) for a in range(3)]
        if nc:
            @pl.when((ids[0] == 0) & (ids[1] == 0) & (ids[2] == 0))
            def _():
                _xfer_start(_gather_plan(c_in, c_out, _my_pos()), sems)

        r = ids[2]
        start = pl.multiple_of(CT + _na_row0(r) * GRID_W, GRID_W)
        kw, vw = k_ref[pl.ds(start, NWIN), :], v_ref[pl.ds(start, NWIN), :]
        kc, vc = k_ref[0:CT, :], v_ref[0:CT, :]
        p_w, p_c, inv = _na_probs(_na_stack(q_ref[...]), kw, kc, b_ref[...])
        o4 = (jnp.dot(p_w.astype(BF16), vw, preferred_element_type=F32)
              + jnp.dot(p_c.astype(BF16), vc, preferred_element_type=F32)) * inv
        o_ref[...] = _na_unstack(o4).astype(BF16)

        if nc:
            @pl.when((ids[0] == grid[0] - 1) & (ids[1] == grid[1] - 1) & (ids[2] == grid[2] - 1))
            def _():
                _xfer_wait(_gather_plan(c_in, c_out, _my_pos()), sems)

    qs, kv, bs, lat = _na_specs()
    hbm = pl.BlockSpec(memory_space=pltpu.HBM)
    res = pl.pallas_call(
        body, name="na_fwd", grid=grid,
        out_shape=[jax.ShapeDtypeStruct((NB, SEQ, NAW), BF16)] + [jax.ShapeDtypeStruct((8,) + h.shape, h.dtype) for h in carry],
        in_specs=[qs, kv, kv, bs] + [hbm] * nc, out_specs=[lat] + [hbm] * nc,
        scratch_shapes=_xfer_sems(nc, 7 * nc) if nc else [],
        compiler_params=_cp(vmem=VMEM_BIG),
    )(qn, kn, vb, bias, *carry)
    return res[0], list(res[1:])


def _na_bwd(qn, kn, vb, bias, do):
    def body(q_ref, k_ref, v_ref, b_ref, do_ref, dq_ref, dk_ref, dv_ref, db_ref):
        b, r = pl.program_id(1), pl.program_id(2)

        @pl.when((b == 0) & (r == 0))
        def _():
            db_ref[...] = jnp.zeros_like(db_ref)

        @pl.when(r == 0)
        def _():
            dk_ref[...] = jnp.zeros_like(dk_ref)
            dv_ref[...] = jnp.zeros_like(dv_ref)

        start = pl.multiple_of(CT + _na_row0(r) * GRID_W, GRID_W)
        cls = _na_cls(r)
        kw, vw = k_ref[pl.ds(start, NWIN), :], v_ref[pl.ds(start, NWIN), :]
        kc, vc = k_ref[0:CT, :], v_ref[0:CT, :]
        q4 = _na_stack(q_ref[...])
        p_w, p_c, inv = _na_probs(q4, kw, kc, b_ref[...])
        p_w = p_w * inv
        p_c = p_c * inv
        do4 = _na_stack(do_ref[...].astype(BF16))
        dp_w = lax.dot_general(do4, vw, _NT, preferred_element_type=F32)
        dp_c = lax.dot_general(do4, vc, _NT, preferred_element_type=F32)
        delta = jnp.sum(p_w * dp_w, axis=1, keepdims=True) + jnp.sum(p_c * dp_c, axis=1, keepdims=True)
        ds_w = p_w * (dp_w - delta)
        ds_c = p_c * (dp_c - delta)
        db_ref[cls] += ds_w
        ds_wb, ds_cb = ds_w.astype(BF16), ds_c.astype(BF16)
        dq_ref[...] = _na_unstack(jnp.dot(ds_wb, kw, preferred_element_type=F32)
                                  + jnp.dot(ds_cb, kc, preferred_element_type=F32))
        dk_ref[pl.ds(start, NWIN), :] += lax.dot_general(ds_wb, q4, _TN, preferred_element_type=F32)
        dv_ref[pl.ds(start, NWIN), :] += lax.dot_general(p_w.astype(BF16), do4, _TN, preferred_element_type=F32)
        dk_ref[0:CT, :] += lax.dot_general(ds_cb, q4, _TN, preferred_element_type=F32)
        dv_ref[0:CT, :] += lax.dot_general(p_c.astype(BF16), do4, _TN, preferred_element_type=F32)

    qs, kv, bs, lat = _na_specs()
    return pl.pallas_call(
        body, name="na_bwd", grid=(NAH // NHS, NB, ROWS),
        out_shape=[jax.ShapeDtypeStruct((NB, SEQ, NAW), F32)] + [jax.ShapeDtypeStruct((NB, T, NAW), F32)] * 2
        + [jax.ShapeDtypeStruct((WIN_ROWS, NAH // NHS, NSR, NWIN), F32)],
        in_specs=[qs, kv, kv, bs, lat],
        out_specs=[lat, kv, kv, pl.BlockSpec((WIN_ROWS, None, NSR, NWIN), lambda g, b, r: (0, g, 0, 0))],
        compiler_params=_cp(vmem=VMEM_BIG),
    )(qn, kn, vb, bias, do)


def _na_tables():
    cols = np.arange(GRID_W)
    win_start = np.clip(cols - WIN_COLS // 2, 0, GRID_W - WIN_COLS)
    kc = cols[None, :]
    valid = (kc >= win_start[:, None]) & (kc < win_start[:, None] + WIN_COLS)
    rel = np.clip(kc - cols[:, None] + WIN_COLS - 1, 0, 2 * WIN_COLS - 2)
    return valid, rel


def _bias_table(rpb):
    valid, rel = _na_tables()
    nrel = 2 * WIN_COLS - 1
    shifts = np.stack([(rel == r) & valid for r in range(nrel)]).astype(np.float32)
    neg = np.where(valid, 0.0, NEG).astype(np.float32)

    def body(rpb_ref, sh_ref, neg_ref, o_ref):
        h = pl.program_id(0)
        for dr in range(2 * WIN_ROWS - 1):
            tile = neg_ref[...]
            for r in range(nrel):
                tile = tile + rpb_ref[h, dr, r] * sh_ref[r]
            for cls in range(WIN_ROWS):
                i = dr - cls
                if 0 <= i < WIN_ROWS:
                    o_ref[cls, :, i * GRID_W:(i + 1) * GRID_W] = tile

    return pl.pallas_call(
        body, name="bias_table", grid=(NAH,),
        out_shape=jax.ShapeDtypeStruct((WIN_ROWS, NAH // NHS, NSR, NWIN), F32),
        in_specs=[pl.BlockSpec(memory_space=pltpu.SMEM), pl.BlockSpec((nrel, GRID_W, GRID_W), lambda h: (0, 0, 0)),
                  pl.BlockSpec((GRID_W, GRID_W), lambda h: (0, 0))],
        out_specs=pl.BlockSpec((WIN_ROWS, None, GRID_W, NWIN), lambda h: (0, h // NHS, h % NHS, 0)),
    )(rpb, jnp.asarray(shifts), jnp.asarray(neg))


def _rpb_fold(db):
    def body1(d_ref, o_ref):
        for dr in range(2 * WIN_ROWS - 1):
            acc = None
            for c in range(WIN_ROWS):
                i = dr - c
                if 0 <= i < WIN_ROWS:
                    blk = d_ref[c, :, i * GRID_W:(i + 1) * GRID_W]
                    acc = blk if acc is None else acc + blk
            o_ref[dr] = acc

    t15 = pl.pallas_call(
        body1, name="rpb_fold_rows", grid=(NAH,),
        out_shape=jax.ShapeDtypeStruct((NAH, 2 * WIN_ROWS - 1, GRID_W, GRID_W), F32),
        in_specs=[pl.BlockSpec((WIN_ROWS, None, GRID_W, NWIN), lambda h: (0, h, 0, 0))],
        out_specs=pl.BlockSpec((None, 2 * WIN_ROWS - 1, GRID_W, GRID_W), lambda h: (h, 0, 0, 0)),
        compiler_params=_cp(vmem=VMEM_BIG),
    )(db.reshape(WIN_ROWS, NAH, GRID_W, NWIN))
    valid, rel = _na_tables()
    onehot = np.zeros((GRID_W * GRID_W, 128), np.float32)
    flat_rel, flat_valid = rel.reshape(-1), valid.reshape(-1)
    onehot[np.arange(GRID_W * GRID_W)[flat_valid], flat_rel[flat_valid]] = 1.0

    def body2(a_ref, oh_ref, o_ref):
        o_ref[...] = _hmm(a_ref[...], oh_ref[...])

    vm = pl.BlockSpec(memory_space=pltpu.VMEM)
    out = pl.pallas_call(
        body2, name="rpb_fold_cols", out_shape=jax.ShapeDtypeStruct((NAH * 15, 128), F32),
        in_specs=[vm, vm], out_specs=vm, compiler_params=_cp(vmem=VMEM_BIG),
    )(t15.reshape(NAH * 15, GRID_W * GRID_W), jnp.asarray(onehot))
    return out[:, :31].reshape(NAH, 15, 31)


def _dn_out(o, z, w):
    outs = []
    for h in range(DNH):
        sl = slice(128 * h, 128 * h + 128)
        oh = o[:, sl]
        r = lax.rsqrt(jnp.mean(oh * oh, axis=-1, keepdims=True) + EPS)
        outs.append(oh * r * w[:, sl] * _silu(z[:, sl]))
    return jnp.concatenate(outs, axis=1)


def _outproj_fwd(of, ob, p, na, onw, x, g1, w_out):
    def body(of_ref, ob_ref, z_ref, na_ref, w_ref, x_ref, g_ref, wo_ref, x1_ref, ai_ref, ao_ref):
        dn = _dn_out(of_ref[...] + ob_ref[...], z_ref[...], w_ref[...])
        ai = jnp.concatenate([dn.astype(BF16), na_ref[...]], axis=1)
        ai_ref[...] = ai
        ao = jnp.dot(ai, wo_ref[...], preferred_element_type=F32)
        ao_ref[...] = ao
        x1_ref[...] = x_ref[...] + g_ref[...] * ao

    lat = lambda w: pl.BlockSpec((None, RT, w), lambda b, j: (b, j, 0))
    return pl.pallas_call(
        body, name="outproj_fwd", grid=(NB, SEQ // RT),
        out_shape=[jax.ShapeDtypeStruct((NB, SEQ, D), F32), jax.ShapeDtypeStruct((NB, SEQ, D), BF16),
                   jax.ShapeDtypeStruct((NB, SEQ, D), F32)],
        in_specs=[pl.BlockSpec((None, RT, DNW), lambda b, j: (b, j + 1, 0)),
                  pl.BlockSpec((None, RT, DNW), lambda b, j: (b, j + 1, 0)),
                  pl.BlockSpec((None, RT, DNW), lambda b, j: (b, j + 1, COL_Z // DNW)),
                  lat(NAW), pl.BlockSpec((1, DNW), lambda b, j: (0, 0)), lat(D),
                  pl.BlockSpec((None, 1, D), lambda b, j: (b, 0, 0)), pl.BlockSpec((D, D), lambda b, j: (0, 0))],
        out_specs=[lat(D), lat(D), lat(D)],
        compiler_params=_cp(vmem=VMEM_BIG),
    )(of, ob, p, na, onw, x, g1, w_out)


def _outproj_bwd(dx1, ao, g1, w_out, of, ob, p, onw):
    def body(dx_ref, ao_ref, g_ref, wo_ref, of_ref, ob_ref, z_ref, w_ref, dao_ref, do_ref, dz_ref, dna_ref, dg_ref, dw_ref):
        b, j = pl.program_id(0), pl.program_id(1)

        @pl.when((b == 0) & (j == 0))
        def _():
            dw_ref[...] = jnp.zeros_like(dw_ref)

        @pl.when(j == 0)
        def _():
            do_ref[...] = jnp.zeros_like(do_ref)
            dz_ref[...] = jnp.zeros_like(dz_ref)

        @pl.when(j == 1)
        def _():
            dg_ref[...] = jnp.zeros_like(dg_ref)

        @pl.when(j >= 1)
        def _():
            dx = dx_ref[...]
            dg_ref[...] += jnp.sum(dx * ao_ref[...], axis=0, keepdims=True)
            dao = (g_ref[...] * dx).astype(BF16)
            dao_ref[...] = dao
            dai = lax.dot_general(dao, wo_ref[...], _NT, preferred_element_type=F32)
            dna_ref[...] = dai[:, DNW:]
            _, vjp = jax.vjp(_dn_out, of_ref[...] + ob_ref[...], z_ref[...], w_ref[...])
            do, dz, dw = vjp(dai[:, :DNW])
            do_ref[...] = do
            dz_ref[...] = dz
            dw_ref[...] += jnp.concatenate([dw, jnp.zeros((7, DNW), F32)], axis=0)

    lat = lambda w: pl.BlockSpec((None, RT, w), lambda b, j: (b, jnp.maximum(j - 1, 0), 0))
    tok = lambda w: pl.BlockSpec((None, RT, w), lambda b, j: (b, j, 0))
    return pl.pallas_call(
        body, name="outproj_bwd", grid=(NB, T // RT),
        out_shape=[jax.ShapeDtypeStruct((NB, SEQ, D), BF16), jax.ShapeDtypeStruct((NB, T, DNW), F32),
                   jax.ShapeDtypeStruct((NB, T, DNW), F32), jax.ShapeDtypeStruct((NB, SEQ, NAW), F32),
                   jax.ShapeDtypeStruct((NB, 1, D), F32), jax.ShapeDtypeStruct((8, DNW), F32)],
        in_specs=[_rows2d_spec(D), lat(D), pl.BlockSpec((None, 1, D), lambda b, j: (b, 0, 0)),
                  pl.BlockSpec((D, D), lambda b, j: (0, 0)),
                  tok(DNW), tok(DNW),
                  pl.BlockSpec((None, RT, DNW), lambda b, j: (b, j, COL_Z // DNW)),
                  pl.BlockSpec((1, DNW), lambda b, j: (0, 0))],
        out_specs=[lat(D), tok(DNW), tok(DNW), lat(NAW), pl.BlockSpec((None, 1, D), lambda b, j: (b, 0, 0)),
                   pl.BlockSpec((8, DNW), lambda b, j: (0, 0))],
        compiler_params=_cp(vmem=VMEM_BIG),
    )(dx1, ao, g1, w_out, of, ob, p, onw)


FT_FWD, FT_BWD = 512, 256
NFF = DFF // FFT


def _ffn_specs(ft):
    row = lambda w: pl.BlockSpec((ft, w), lambda i, f: (i, 0))
    ex = pl.BlockSpec((None, 1, D), lambda i, f: (i // (SEQ // ft), 0, 0))
    one = pl.BlockSpec((1, D), lambda i, f: (0, 0))
    wg = pl.BlockSpec((None, D, FFT), lambda i, f: (f, 0, 0))
    wu = pl.BlockSpec((None, D, FFT), lambda i, f: (NFF + f, 0, 0))
    wo = pl.BlockSpec((FFT, D), lambda i, f: (f, 0))
    ff = pl.BlockSpec((ft, FFT), lambda i, f: (i, f))
    return row, ex, one, wg, wu, wo, ff


def _ffn_fwd(x1, sh2, sc2, g2, n2w, w_in4, w_out, target):
    ft, tpe = FT_FWD, SEQ // FT_FWD

    def body(x_ref, sh_ref, sc_ref, g_ref, w2_ref, wg_ref, wu_ref, wo_ref, t_ref,
             h2_ref, gt_ref, up_ref, dy_ref, dg_ref, loss_ref, acc_ref):
        i, f = pl.program_id(0), pl.program_id(1)

        @pl.when(f == 0)
        def _():
            h2_ref[...] = _norm_mod(x_ref[...], w2_ref[...], sh_ref[...], sc_ref[...]).astype(BF16)
            acc_ref[...] = jnp.zeros_like(acc_ref)

        h2 = h2_ref[...]
        gate = jnp.dot(h2, wg_ref[...], preferred_element_type=F32)
        up = jnp.dot(h2, wu_ref[...], preferred_element_type=F32)
        gt_ref[...] = gate.astype(BF16)
        up_ref[...] = up.astype(BF16)
        acc_ref[...] += jnp.dot((_silu(gate) * up).astype(BF16), wo_ref[...], preferred_element_type=F32)

        @pl.when((i == 0) & (f == 0))
        def _():
            loss_ref[...] = jnp.zeros_like(loss_ref)

        @pl.when(f == NFF - 1)
        def _():
            ffn = acc_ref[...]
            err = x_ref[...] + g_ref[...] * ffn - t_ref[...]
            dy = err * (1.0 / D)
            dy_ref[...] = dy
            part = jnp.sum(dy * ffn, axis=0, keepdims=True)
            loss_ref[...] += jnp.sum(jnp.sum(err * err, axis=1, keepdims=True), axis=0, keepdims=True) * (0.5 / D)

            @pl.when(i % tpe == 0)
            def _():
                dg_ref[...] = part

            @pl.when(i % tpe != 0)
            def _():
                dg_ref[...] += part

    row, ex, one, wg, wu, wo, ff = _ffn_specs(ft)
    return pl.pallas_call(
        body, name="ffn_fwd", grid=(NTOK // ft, NFF),
        out_shape=[jax.ShapeDtypeStruct((NTOK, D), BF16), jax.ShapeDtypeStruct((NTOK, DFF), BF16),
                   jax.ShapeDtypeStruct((NTOK, DFF), BF16), jax.ShapeDtypeStruct((NTOK, D), F32),
                   jax.ShapeDtypeStruct((NB, 1, D), F32), jax.ShapeDtypeStruct((8, 128), F32)],
        in_specs=[row(D), ex, ex, ex, one, wg, wu, wo, row(D)],
        out_specs=[row(D), ff, ff, row(D), ex, pl.BlockSpec((8, 128), lambda i, f: (0, 0))],
        scratch_shapes=[pltpu.VMEM((ft, D), F32)],
        compiler_params=_cp(vmem=VMEM_BIG),
    )(x1, sh2, sc2, g2, n2w, w_in4, w_in4, w_out, target)


def _rows2d_spec(width):
    return pl.BlockSpec((RT, width), lambda b, j: (b * (SEQ // RT) + jnp.maximum(j - 1, 0), 0))


def _ffn_bwd(dy, gate, up, x1, sh2, sc2, g2, n2w, w_in4, w_out):
    ft, tpe = FT_BWD, SEQ // FT_BWD
    ni = NTOK // ft

    def body(dy_ref, gt_ref, up_ref, x_ref, sh_ref, sc_ref, g_ref, w2_ref, wg_ref, wu_ref, wo_ref,
             dff_ref, act_ref, dgu_ref, dx_ref, dsh_ref, dsc_ref, dw_ref, acc_ref):
        f, i = pl.program_id(0), pl.program_id(1)
        rows = pl.ds(pl.multiple_of(i * ft, ft), ft)
        dff = (g_ref[...] * dy_ref[...]).astype(BF16)
        dff_ref[...] = dff
        dact = lax.dot_general(dff, wo_ref[...], _NT, preferred_element_type=F32)
        gate, up = gt_ref[...].astype(F32), up_ref[...].astype(F32)
        sg = _sigmoid(gate)
        sl = gate * sg
        act_ref[...] = (sl * up).astype(BF16)
        dgate = (dact * up * (sg * (1.0 + gate * (1.0 - sg)))).astype(BF16)
        dup = (dact * sl).astype(BF16)
        dgu_ref[0] = dgate
        dgu_ref[1] = dup
        part = (lax.dot_general(dgate, wg_ref[...], _NT, preferred_element_type=F32)
                + lax.dot_general(dup, wu_ref[...], _NT, preferred_element_type=F32))

        @pl.when(f == 0)
        def _():
            acc_ref[rows, :] = part

        @pl.when(f > 0)
        def _():
            acc_ref[rows, :] += part

        @pl.when((i == 0) & (f == 0))
        def _():
            dw_ref[...] = jnp.zeros_like(dw_ref)

        @pl.when(f == NFF - 1)
        def _():
            _, vjp = jax.vjp(_norm_mod, x_ref[...], w2_ref[...], sh_ref[...], sc_ref[...])
            dx, dw, dsh, dsc = vjp(acc_ref[rows, :])
            dx_ref[...] = dy_ref[...] + dx
            dw_ref[...] += dw

            @pl.when(i % tpe == 0)
            def _():
                dsh_ref[...] = dsh
                dsc_ref[...] = dsc

            @pl.when(i % tpe != 0)
            def _():
                dsh_ref[...] += dsh
                dsc_ref[...] += dsc

    last = lambda f, real, spare: jnp.where(f == NFF - 1, real, spare)
    row = lambda w: pl.BlockSpec((ft, w), lambda f, i: (i, 0))
    ex = pl.BlockSpec((None, 1, D), lambda f, i: (i // tpe, 0, 0))
    one = pl.BlockSpec((1, D), lambda f, i: (0, 0))
    wg = pl.BlockSpec((None, D, FFT), lambda f, i: (f, 0, 0))
    wu = pl.BlockSpec((None, D, FFT), lambda f, i: (NFF + f, 0, 0))
    wo = pl.BlockSpec((FFT, D), lambda f, i: (f, 0))
    ff = pl.BlockSpec((ft, FFT), lambda f, i: (i, f))
    row_once = lambda w: pl.BlockSpec((ft, w), lambda f, i: (last(f, i, ni), 0))
    ex_once = pl.BlockSpec((None, 1, D), lambda f, i: (last(f, i // tpe, NB), 0, 0))
    return pl.pallas_call(
        body, name="ffn_bwd", grid=(NFF, ni),
        out_shape=[jax.ShapeDtypeStruct((NTOK + ft, D), BF16), jax.ShapeDtypeStruct((NTOK, DFF), BF16),
                   jax.ShapeDtypeStruct((2, NTOK, DFF), BF16), jax.ShapeDtypeStruct((NTOK + ft, D), F32),
                   jax.ShapeDtypeStruct((NB + 1, 1, D), F32), jax.ShapeDtypeStruct((NB + 1, 1, D), F32),
                   jax.ShapeDtypeStruct((1, D), F32)],
        in_specs=[row(D), ff, ff, row(D), ex, ex, ex, one, wg, wu, wo],
        out_specs=[row_once(D), ff, pl.BlockSpec((2, ft, FFT), lambda f, i: (0, i, f)), row_once(D), ex_once, ex_once, one],
        scratch_shapes=[pltpu.VMEM((NTOK, D), F32)],
        compiler_params=_cp(vmem=VMEM_BIG),
    )(dy, gate, up, x1, sh2, sc2, g2, n2w, w_in4, w_in4, w_out)


def _local_step(x, ctx, target, mod_x, mod_c, norm1_w, norm2_w, w_in_p, conv_w, a_log, dt_bias, out_norm_w,
                q_norm_w, k_norm_w, rpb, w_out_b, w_ffn_in4, w_ffn_out_b, scatter_early=False, gather_late=None):
    sh1, sc1, g1, sh2, sc2, g2 = [mod_x[:, i * D:(i + 1) * D].reshape(NB, 1, D) for i in range(6)]
    csh1, csc1 = mod_c[0:D].reshape(1, 1, D), mod_c[D:2 * D].reshape(1, 1, D)
    cw = jnp.concatenate([conv_w, jnp.zeros((3, 1536), F32)], axis=0)
    prm = jnp.zeros((8, 128), F32).at[0, :8].set(a_log.reshape(8)).at[1, :8].set(dt_bias.reshape(8))
    onw = jnp.tile(out_norm_w.reshape(1, 128), (1, DNH))
    qw = jnp.tile(q_norm_w.reshape(1, 64), (1, NAH))
    kw = jnp.tile(k_norm_w.reshape(1, 64), (1, NAH))
    seg = np.arange(NAW) // 64
    bd = jnp.asarray((seg[:, None] == seg[None, :]).astype(np.float32) / 64.0, dtype=BF16)
    bias = _bias_table(rpb)

    p, hm = _inproj_fwd(x, ctx, sh1, sc1, csh1, csc1, norm1_w, w_in_p)
    dq_, dk_, dv_ = _dn_prep_fwd(p, cw)
    gates = _gates_fwd(p, prm)
    res = _scan_fwd(dq_, dk_, dv_, gates, carry=gather_late[:2] if gather_late else ())
    o_f, st_f, inv_f, o_b, st_b, inv_b = res[:6]
    qn, kn, vb = _na_prep_fwd(p, qw, kw, bd)
    na, got = _na_fwd(qn, kn, vb, bias, carry=gather_late[2:] if gather_late else ())
    if gather_late:
        w_out_b = res[6].reshape(D, D)
        w_ffn_in4 = res[7].reshape(4, D, FFT)
        w_ffn_out_b = got[0].reshape(DFF, D)
    x1, ai, ao = _outproj_fwd(o_f, o_b, p, na, onw, x, g1, w_out_b)
    x1f = x1.reshape(NTOK, D)
    h2, gate, up, dy, dg2, loss8 = _ffn_fwd(x1f, sh2, sc2, g2, norm2_w, w_ffn_in4, w_ffn_out_b,
                                             target.reshape(NTOK, D))
    dff, act, dgu, dx1, dsh2, dsc2, dn2w = _ffn_bwd(dy, gate, up, x1f, sh2, sc2, g2, norm2_w, w_ffn_in4, w_ffn_out_b)
    g_w_ffn_out = _matmul_tn(act, dff, FFT, D, 512, "wgrad_ffn_out")
    g_w_ffn_in4 = _matmul_tn(
        h2, dgu, D, FFT, 1024, "wgrad_ffn_in", nn=2 * NFF,
        b_block=(None, 1024, FFT), b_map=lambda i, j, k: (j // NFF, k, j % NFF),
        out_shape=(4, D, FFT), out_block=(None, D, FFT), out_map=lambda i, j, k: (j, i, 0))
    dsh2, dsc2 = dsh2[:NB], dsc2[:NB]
    dao, do, dz, dna, dg1, donw = _outproj_bwd(dx1, ao, g1, w_out_b, o_f, o_b, p, onw)
    g_w_out = _matmul_tn(ai.reshape(NTOK, D), dao.reshape(NTOK, D), D, D, 1024, "wgrad_out")
    dqn, dkn, dvn, dbias = _na_bwd(qn, kn, vb, bias, dna)
    dp_na, dqw, dkw = _na_prep_bwd(p, qw, kw, bd, dqn, dkn, dvn)
    early = [g_w_out.reshape(8, D // 8, D), g_w_ffn_in4.reshape(8, D // 2, FFT), g_w_ffn_out.reshape(8, DFF // 8, D)]
    res = _scan_bwd(dq_, dk_, dv_, gates, st_f, st_b, inv_f, inv_b, do, carry=early if scatter_early else ())
    dqf, dkf, dvf, dgf, dqb, dkb, dvb, dgb = res[:8]
    if scatter_early:
        g_w_out, g_w_ffn_in4, g_w_ffn_out = res[8:]
    dpq, dpk, dpv, gcq, gck, gcv = _dn_prep_bwd(p, cw, (dqf, dqb), (dkf, dkb), (dvf, dvb))
    dgt, dprm = _gates_bwd(p, prm, dgf, dgb)
    dpb, grad_x, dsh1, dsc1, dcsh1, dcsc1, dn1w = _inproj_bwd(
        (dpq, dpk, dpv), dz, dp_na, dgt, w_in_p, x, ctx, sh1, sc1, csh1, csc1, norm1_w, dx1)
    g_w_in_p = _matmul_tn(dpb.reshape(NB * T, PC), hm.reshape(NB * T, D), PC, D, 512, "wgrad_in")
    g_rpb = _rpb_fold(dbias)

    dmod_x = jnp.concatenate([dsh1, dsc1, dg1, dsh2, dsc2, dg2], axis=2).reshape(NB, 6 * D)
    dmod_c = jnp.concatenate([dcsh1.reshape(D), dcsc1.reshape(D), jnp.zeros((4 * D,), F32)])
    g_conv = jnp.concatenate([gcq[:5], gck[:5], gcv[:5]], axis=1)
    small = dict(
        norm1_w=dn1w, norm2_w=dn2w, a_log=dprm[0, :8], dt_bias=dprm[1, :8],
        out_norm_w=donw[0].reshape(DNH, 128).sum(0), q_norm_w=dqw[0].reshape(NAH, 64).sum(0),
        k_norm_w=dkw[0].reshape(NAH, 64).sum(0), rpb=g_rpb, conv_w=g_conv)
    return loss8[0, 0], grad_x, dmod_x, dmod_c, small, g_w_in_p, g_w_out, g_w_ffn_in4, g_w_ffn_out


def _perm_w_in(wt):
    return jnp.concatenate([wt[:2048], wt[2064:3600], wt[2048:2064], jnp.zeros((PC - IN_COLS, wt.shape[1]), wt.dtype)],
                           axis=0)


def _unperm_w_in(gt):
    return jnp.concatenate([gt[:2048], gt[COL_G:COL_G + 16], gt[2048:COL_G]], axis=0)


def _pack_small(c_ctx, n1, n2, b_ada, a_log, dt_bias, onw, qnw, knw, rpb, conv_s):
    misc = jnp.zeros((D,), F32)
    misc = misc.at[0:8].set(a_log.reshape(8)).at[8:16].set(dt_bias.reshape(8))
    misc = misc.at[128:256].set(onw.reshape(128)).at[256:320].set(qnw.reshape(64)).at[320:384].set(knw.reshape(64))
    rp = jnp.concatenate([rpb.reshape(-1), jnp.zeros((4 * D - 3720,), F32)])
    cv = jnp.concatenate([conv_s.reshape(-1), jnp.zeros((2 * D - 1920,), F32)])
    return jnp.concatenate([c_ctx.reshape(-1), n1.reshape(-1), n2.reshape(-1), b_ada.reshape(-1), misc, rp, cv]).reshape(16, D)


def _unpack_small(a):
    f = a.reshape(-1)
    misc = f[9 * D:10 * D]
    return dict(
        c_ctx=f[0:D], norm1_w=f[D:2 * D].reshape(1, D), norm2_w=f[2 * D:3 * D].reshape(1, D),
        b_ada=f[3 * D:9 * D].reshape(1, 6 * D), dn_A_log=misc[0:8].reshape(1, 2, 4), dn_dt_bias=misc[8:16].reshape(1, 2, 4),
        dn_out_norm_w=misc[128:256].reshape(1, 128), na_q_norm_w=misc[256:320].reshape(1, 64),
        na_k_norm_w=misc[320:384].reshape(1, 64), na_rpb=f[10 * D:10 * D + 3720].reshape(1, 8, 15, 31),
        dn_conv_w=f[14 * D:14 * D + 1920].reshape(1, 5, 384))


_WEIGHTS = ['c_ctx', 'norm1_w', 'norm2_w', 'w_ada', 'b_ada', 'w_in', 'dn_conv_w', 'dn_A_log', 'dn_dt_bias',
            'dn_out_norm_w', 'na_q_norm_w', 'na_k_norm_w', 'na_rpb', 'w_out', 'w_ffn_in', 'w_ffn_out']


def kernel(x, c, ctx, c_ctx, norm1_w, norm2_w, w_ada, b_ada, w_in, dn_conv_w, dn_A_log, dn_dt_bias, dn_out_norm_w, na_q_norm_w, na_k_norm_w, na_rpb, w_out, w_ffn_in, w_ffn_out, loss_target, m_c_ctx, m_norm1_w, m_norm2_w, m_w_ada, m_b_ada, m_w_in, m_dn_conv_w, m_dn_A_log, m_dn_dt_bias, m_dn_out_norm_w, m_na_q_norm_w, m_na_k_norm_w, m_na_rpb, m_w_out, m_w_ffn_in, m_w_ffn_out, v_c_ctx, v_norm1_w, v_norm2_w, v_w_ada, v_b_ada, v_w_in, v_dn_conv_w, v_dn_A_log, v_dn_dt_bias, v_dn_out_norm_w, v_na_q_norm_w, v_na_k_norm_w, v_na_rpb, v_w_out, v_w_ffn_in, v_w_ffn_out):
    ix, iy, ic = lax.axis_index("x"), lax.axis_index("y"), lax.axis_index("c")
    chip = 2 * ix + iy
    me = 4 * ix + 2 * iy + ic

    crow = jnp.concatenate([c, c_ctx.reshape(1, D), jnp.zeros((5, D), F32)], axis=0)
    cvrow = jnp.concatenate([dn_conv_w[0].reshape(-1), jnp.zeros((2 * D - 1920,), F32)]).reshape(2, D)
    g0 = _allgather8(jnp.concatenate([crow, cvrow, jnp.zeros((6, D), F32)], axis=0), "gather_cond")
    cin = g0[:, :8].reshape(64, D)
    conv_full = jnp.concatenate([g0[2 * s, 8:10].reshape(-1)[:1920].reshape(5, 384) for s in range(4)], axis=1)

    w_ada_s = w_ada[0]
    b_s = lax.dynamic_slice(b_ada, (0, chip * 1536), (1, 1536))
    mod_s = _ada_fwd(cin, w_ada_s, b_s)
    mods = _gather_rows(mod_s, "gather_mod")
    mod_mine = jnp.concatenate([mods[0], mods[2], mods[4], mods[6]], axis=1)
    mod_x, mod_c = mod_mine[0:NB], mod_mine[2]

    def half(w2d):
        r = w2d.shape[0] // 2
        return lax.dynamic_slice(w2d, (ic * r, 0), (r, w2d.shape[1]))

    w_in_t = w_in[0].T
    (gw_in,) = _gather_weights([_cast_bf16(lax.dynamic_slice(w_in_t, (0, ic * (D // 2)), (900, D // 2)), "cast_w_in")])
    late = [_cast_bf16(half(w_out[0]), "cast_w_out"), _cast_bf16(half(w_ffn_in[0]), "cast_w_ffn_in"),
            _cast_bf16(half(w_ffn_out[0]), "cast_w_ffn_out")]
    w_in_p = _perm_w_in(gw_in.reshape(4, 2, 900, D // 2).transpose(0, 2, 1, 3).reshape(IN_COLS, D))

    (loss_part, grad_x, dmod_x, dmod_c, small, g_w_in_p, r_w_out, r_w_ffn_in, r_w_ffn_out) = _local_step(
        x, ctx, loss_target, mod_x, mod_c, norm1_w, norm2_w, w_in_p, conv_full, dn_A_log[0], dn_dt_bias[0],
        dn_out_norm_w[0], na_q_norm_w[0], na_k_norm_w[0], na_rpb[0], None, None, None,
        scatter_early=True, gather_late=late)
    loss = lax.psum(loss_part, ("x", "y", "c"))

    dm = jnp.concatenate([dmod_x, dmod_c.reshape(1, 6 * D), jnp.zeros((5, 6 * D), F32)], axis=0).reshape(48, D)
    misc = jnp.zeros((D,), F32)
    misc = misc.at[0:8].set(small["a_log"]).at[8:16].set(small["dt_bias"]).at[128:256].set(small["out_norm_w"])
    misc = misc.at[256:320].set(small["q_norm_w"]).at[320:384].set(small["k_norm_w"])
    sm = jnp.concatenate([
        small["norm1_w"].reshape(-1), small["norm2_w"].reshape(-1), misc,
        small["rpb"].reshape(-1), jnp.zeros((4 * D - 3720,), F32),
        small["conv_w"].reshape(-1), jnp.zeros((8 * D - 7680,), F32), jnp.zeros((D,), F32)]).reshape(16, D)
    gsl, gtot = _allgather8(jnp.concatenate([dm, sm], axis=0), "gather_small", with_sum=True)
    gmod = gsl[:, :48].reshape(64, 6 * D)
    st = gtot[48:64].reshape(-1)
    g_n1, g_n2, g_misc = st[0:D].reshape(1, D), st[D:2 * D].reshape(1, D), st[2 * D:3 * D]
    g_rpb = st[3 * D:3 * D + 3720].reshape(1, 8, 15, 31)
    g_conv_full = st[7 * D:7 * D + 7680].reshape(5, 1536)
    g_conv_s = lax.dynamic_slice(g_conv_full, (0, chip * 384), (5, 384)).reshape(1, 5, 384)

    gmod_s = lax.dynamic_slice(gmod, (0, chip * 1536), (64, 1536))
    g_w_ada_s, pctx = _ada_bwd(cin, gmod_s, w_ada_s)
    parts = _allgather8(pctx, "gather_cctx")
    g_b_ada, g_c_ctx = _ada_small(gmod, parts, c_ctx.reshape(1, D))

    g_w_in8 = _unperm_w_in(g_w_in_p).reshape(4, 900, 2, D // 2).transpose(0, 2, 1, 3).reshape(8, 900, D // 2)
    names = ["w_in", "w_out", "w_ffn_in", "w_ffn_out"]
    rs = list(_scatter_grads([g_w_in8])) + [r_w_out, r_w_ffn_in, r_w_ffn_out]
    full = _sibling_exchange([_sum_slots(r, "sum_" + nm) for r, nm in zip(rs, names)])
    g_w_in_t = full[0].transpose(1, 0, 2).reshape(900, D)
    g_w_out_s = full[1].reshape(1, D // 4, D)
    g_w_ffn_in_s = full[2].reshape(1, D, FFT)
    g_w_ffn_out_s = full[3].reshape(1, DFF // 4, D)

    grads = dict(
        c_ctx=g_c_ctx.reshape(D), norm1_w=g_n1, norm2_w=g_n2, w_ada=g_w_ada_s.reshape(1, D, 1536), b_ada=g_b_ada,
        w_in=g_w_in_t, dn_conv_w=g_conv_s, dn_A_log=g_misc[0:8].reshape(1, 2, 4), dn_dt_bias=g_misc[8:16].reshape(1, 2, 4),
        dn_out_norm_w=g_misc[128:256].reshape(1, 128), na_q_norm_w=g_misc[256:320].reshape(1, 64),
        na_k_norm_w=g_misc[320:384].reshape(1, 64), na_rpb=g_rpb, w_out=g_w_out_s, w_ffn_in=g_w_ffn_in_s,
        w_ffn_out=g_w_ffn_out_s)
    pk = lambda d_: _pack_small(d_["c_ctx"], d_["norm1_w"], d_["norm2_w"], d_["b_ada"], d_["dn_A_log"], d_["dn_dt_bias"],
                                d_["dn_out_norm_w"], d_["na_q_norm_w"], d_["na_k_norm_w"], d_["na_rpb"], d_["dn_conv_w"])
    wts = dict(c_ctx=c_ctx, norm1_w=norm1_w, norm2_w=norm2_w, b_ada=b_ada, dn_A_log=dn_A_log, dn_dt_bias=dn_dt_bias,
               dn_out_norm_w=dn_out_norm_w, na_q_norm_w=na_q_norm_w, na_k_norm_w=na_k_norm_w, na_rpb=na_rpb, dn_conv_w=dn_conv_w)
    ms = dict(c_ctx=m_c_ctx, norm1_w=m_norm1_w, norm2_w=m_norm2_w, b_ada=m_b_ada, dn_A_log=m_dn_A_log, dn_dt_bias=m_dn_dt_bias,
              dn_out_norm_w=m_dn_out_norm_w, na_q_norm_w=m_na_q_norm_w, na_k_norm_w=m_na_k_norm_w, na_rpb=m_na_rpb,
              dn_conv_w=m_dn_conv_w)
    vs = dict(c_ctx=v_c_ctx, norm1_w=v_norm1_w, norm2_w=v_norm2_w, b_ada=v_b_ada, dn_A_log=v_dn_A_log, dn_dt_bias=v_dn_dt_bias,
              dn_out_norm_w=v_dn_out_norm_w, na_q_norm_w=v_na_q_norm_w, na_k_norm_w=v_na_k_norm_w, na_rpb=v_na_rpb,
              dn_conv_w=v_dn_conv_w)
    sd, snm, snv, _ = _adamw(pk(wts), pk(grads), pk(ms), pk(vs), "adamw_small")
    delta, new_m, new_v = _unpack_small(sd), _unpack_small(snm), _unpack_small(snv)
    big = dict(w_ada=(w_ada, m_w_ada, v_w_ada), w_in=(w_in, m_w_in, v_w_in), w_out=(w_out, m_w_out, v_w_out),
               w_ffn_in=(w_ffn_in, m_w_ffn_in, v_w_ffn_in), w_ffn_out=(w_ffn_out, m_w_ffn_out, v_w_ffn_out))
    for nm, (w_, m_, v_) in big.items():
        shp = w_.shape
        if nm == "w_in":
            r2, back = (lambda a: a[0].T), (lambda a: a.T.reshape(shp))
            res = _adamw(w_in_t, grads[nm], r2(m_), r2(v_), "adamw_" + nm)
        else:
            r2, back = (lambda a: a.reshape(shp[1], shp[2])), (lambda a: a.reshape(shp))
            res = _adamw(r2(w_), r2(grads[nm]), r2(m_), r2(v_), "adamw_" + nm)
        delta[nm], new_m[nm], new_v[nm], grads[nm] = [back(a) for a in res]

    return (loss, grad_x, *[grads[n] for n in _WEIGHTS], *[delta[n] for n in _WEIGHTS],
            *[new_m[n] for n in _WEIGHTS], *[new_v[n] for n in _WEIGHTS])
```

```python
import functools

import numpy as np
import jax
import jax.numpy as jnp
from jax import lax
from jax.experimental import pallas as pl
from jax.experimental.pallas import tpu as pltpu

F32 = jnp.float32
BF16 = jnp.bfloat16

D = 1024
NB = 2
SEQ = 2048
CT = 256
T = CT + SEQ
NTOK = NB * SEQ
CH = 64
NCH = T // CH
NCH_CTX = CT // CH
DNH, DNW = 4, 512
NAH, NAW = 8, 512
GRID_W = 64
ROWS = SEQ // GRID_W
WIN_ROWS = 8
WIN_COLS = 16
NWIN = WIN_ROWS * GRID_W
DFF = 2816
FFT = 1408
IN_COLS = 3600
PC = 3712
COL_Z, COL_NA, COL_G = 1536, 2048, 3584
EPS = 1e-6
NEG = -1e30
RT = 256
VMEM_BIG = 56 * 1024 * 1024
MESH = pl.DeviceIdType.MESH

ADAM_LR, ADAM_B1, ADAM_B2, ADAM_EPS, ADAM_WD, ADAM_STEP = 0.001, 0.9, 0.999, 1e-08, 0.01, 10


def _cp(vmem=None, sem=None):
    kw = {}
    if vmem is not None:
        kw["vmem_limit_bytes"] = vmem
    if sem is not None:
        kw["dimension_semantics"] = sem
    return pltpu.CompilerParams(**kw)


_NN = (((1,), (0,)), ((), ()))
_NT = (((1,), (1,)), ((), ()))
_TN = (((0,), (0,)), ((), ()))


def _dg(a, b, dims):
    return lax.dot_general(a.astype(BF16), b.astype(BF16), dims, preferred_element_type=F32)


@jax.custom_vjp
def _mm(a, b):
    return _dg(a, b, _NN)


_mm.defvjp(lambda a, b: (_dg(a, b, _NN), (a, b)),
           lambda r, g: (_dg(g, r[1], _NT), _dg(r[0], g, _TN)))


@jax.custom_vjp
def _mm_nt(a, b):
    return _dg(a, b, _NT)


_mm_nt.defvjp(lambda a, b: (_dg(a, b, _NT), (a, b)),
              lambda r, g: (_dg(g, r[1], _NN), _dg(g, r[0], _TN)))


@jax.custom_vjp
def _mm_tn(a, b):
    return _dg(a, b, _TN)


_mm_tn.defvjp(lambda a, b: (_dg(a, b, _TN), (a, b)),
              lambda r, g: (_dg(r[1], g, _NT), _dg(r[0], g, _NN)))


def _split_dot(a, b, dims):
    hi = b.astype(BF16)
    lo = (b - hi.astype(F32)).astype(BF16)
    ab = a.astype(BF16)
    return (lax.dot_general(ab, hi, dims, preferred_element_type=F32)
            + lax.dot_general(ab, lo, dims, preferred_element_type=F32))


@jax.custom_vjp
def _mask_mm(mask, b):
    return _split_dot(mask, b, _NN)


_mask_mm.defvjp(lambda mask, b: (_split_dot(mask, b, _NN), mask),
                lambda mask, g: (jnp.zeros_like(mask), _split_dot(mask, g, _TN)))


def _hmm(a, b):
    return jnp.dot(a, b, precision=lax.Precision.HIGHEST, preferred_element_type=F32)


def _sigmoid(x):
    return 1.0 / (1.0 + jnp.exp(-x))


def _silu(x):
    return x * _sigmoid(x)


def _norm_mod(x, w, sh, sc):
    r = lax.rsqrt(jnp.mean(x * x, axis=-1, keepdims=True) + EPS)
    return (x * r * w) * (1.0 + sc) + sh


def _my_pos():
    return lax.axis_index("x"), lax.axis_index("y"), lax.axis_index("c")


def _flip(pos, k):
    x, y, c = pos
    return (1 - x if k & 4 else x, 1 - y if k & 2 else y, 1 - c if k & 1 else c)


def _slot(pos):
    return 4 * pos[0] + 2 * pos[1] + pos[2]


def _allgather8(v, name, with_sum=False):
    R, W = v.shape

    def body(v_ref, out_ref, *rest):
        if with_sum:
            tot_ref, send_sems, recv_sems, lsem = rest
        else:
            send_sems, recv_sems, lsem = rest
        me = _my_pos()
        mine = pltpu.make_async_copy(v_ref, out_ref.at[_slot(me)], lsem)
        mine.start()
        sends = []
        for k in range(1, 8):
            cp = pltpu.make_async_remote_copy(
                src_ref=v_ref, dst_ref=out_ref.at[_slot(me)], send_sem=send_sems.at[k - 1],
                recv_sem=recv_sems.at[k - 1], device_id=_flip(me, k), device_id_type=MESH)
            cp.start()
            sends.append(cp)
        for k in range(1, 8):
            peer = _flip(me, k)
            pltpu.make_async_remote_copy(
                src_ref=v_ref, dst_ref=out_ref.at[_slot(peer)], send_sem=send_sems.at[k - 1],
                recv_sem=recv_sems.at[k - 1], device_id=peer, device_id_type=MESH).wait_recv()
        for cp in sends:
            cp.wait_send()
        mine.wait()
        if with_sum:
            acc = out_ref[0]
            for s in range(1, 8):
                acc = acc + out_ref[s]
            tot_ref[...] = acc

    out_shape = [jax.ShapeDtypeStruct((8, R, W), F32)]
    out_specs = [pl.BlockSpec(memory_space=pltpu.VMEM)]
    if with_sum:
        out_shape.append(jax.ShapeDtypeStruct((R, W), F32))
        out_specs.append(pl.BlockSpec(memory_space=pltpu.VMEM))
    res = pl.pallas_call(
        body, name=name, out_shape=out_shape,
        in_specs=[pl.BlockSpec(memory_space=pltpu.VMEM)], out_specs=out_specs,
        scratch_shapes=[pltpu.SemaphoreType.DMA((7,)), pltpu.SemaphoreType.DMA((7,)), pltpu.SemaphoreType.DMA],
        compiler_params=_cp(vmem=VMEM_BIG),
    )(v)
    return res if with_sum else res[0]


def _gather_rows(v, name):
    W = v.shape[1]

    def body(v_ref, out_ref, send_sems, recv_sems, lsem):
        pos = _my_pos()
        me = _slot(pos)
        rows = lambda p: v_ref.at[pl.ds(pl.multiple_of(8 * p, 8), 8)]
        locals_ = [(rows(me), out_ref.at[me])]
        xfers = [(rows(_slot(_flip(pos, k))), out_ref.at[me], out_ref.at[_slot(_flip(pos, k))], _flip(pos, k))
                 for k in range(1, 8)]
        _xfer_start((locals_, xfers), (send_sems, recv_sems, lsem))
        _xfer_wait((locals_, xfers), (send_sems, recv_sems, lsem))

    vm = pl.BlockSpec(memory_space=pltpu.VMEM)
    return pl.pallas_call(
        body, name=name, out_shape=jax.ShapeDtypeStruct((8, 8, W), F32), in_specs=[vm], out_specs=vm,
        scratch_shapes=_xfer_sems(1, 7),
    )(v)


def _gather_weights(halves):
    n = len(halves)

    def body(*refs):
        ins, outs = refs[:n], refs[n:2 * n]
        send_sems, recv_sems, lsems = refs[2 * n:]
        me = _my_pos()
        x, y, c = me
        sib = (x, y, 1 - c)
        chips = [4, 2, 6]

        def cp(a, k, block, to, src=None):
            dst = outs[a].at[_slot(block)]
            return pltpu.make_async_remote_copy(
                src_ref=dst if src is None else src, dst_ref=dst, send_sem=send_sems.at[a, k],
                recv_sem=recv_sems.at[a, k], device_id=to, device_id_type=MESH)

        mine, first, passed = [], [], []
        for a in range(n):
            m = pltpu.make_async_copy(ins[a], outs[a].at[_slot(me)], lsems.at[a])
            m.start()
            mine.append(m)
            first.append(cp(a, 0, me, sib, src=ins[a]))
            for j, k in enumerate(chips):
                first.append(cp(a, 1 + j, me, _flip(me, k), src=ins[a]))
        for f in first:
            f.start()
        for j, k in enumerate(chips):
            for a in range(n):
                cp(a, 1 + j, _flip(me, k), me).wait_recv()
                p = cp(a, 4 + j, _flip(me, k), sib)
                p.start()
                passed.append(p)
        for a in range(n):
            cp(a, 0, sib, me).wait_recv()
            for j, k in enumerate(chips):
                cp(a, 4 + j, _flip(sib, k), me).wait_recv()
        for f in first + passed:
            f.wait_send()
        for m in mine:
            m.wait()

    vm = pl.BlockSpec(memory_space=pltpu.VMEM)
    return pl.pallas_call(
        body, name="gather_weights",
        out_shape=[jax.ShapeDtypeStruct((8,) + h.shape, h.dtype) for h in halves],
        in_specs=[vm] * n, out_specs=[vm] * n,
        scratch_shapes=[pltpu.SemaphoreType.DMA((n, 7)), pltpu.SemaphoreType.DMA((n, 7)),
                        pltpu.SemaphoreType.DMA((n,))],
        compiler_params=_cp(vmem=VMEM_BIG),
    )(*halves)


def _xfer_start(plan, sems):
    send_sems, recv_sems, lsems = sems
    locals_, xfers = plan
    for i, (src, dst) in enumerate(locals_):
        pltpu.make_async_copy(src, dst, lsems.at[i]).start()
    for i, (src, dst, _, peer) in enumerate(xfers):
        pltpu.make_async_remote_copy(src_ref=src, dst_ref=dst, send_sem=send_sems.at[i], recv_sem=recv_sems.at[i],
                                     device_id=peer, device_id_type=MESH).start()


def _xfer_wait(plan, sems):
    send_sems, recv_sems, lsems = sems
    locals_, xfers = plan
    for i, (src, _, landing, peer) in enumerate(xfers):
        pltpu.make_async_remote_copy(src_ref=src, dst_ref=landing, send_sem=send_sems.at[i], recv_sem=recv_sems.at[i],
                                     device_id=peer, device_id_type=MESH).wait_recv()
    for i, (src, dst, _, peer) in enumerate(xfers):
        pltpu.make_async_remote_copy(src_ref=src, dst_ref=dst, send_sem=send_sems.at[i], recv_sem=recv_sems.at[i],
                                     device_id=peer, device_id_type=MESH).wait_send()
    for i, (src, dst) in enumerate(locals_):
        pltpu.make_async_copy(src, dst, lsems.at[i]).wait()


def _xfer_sems(n_local, n_xfer):
    return [pltpu.SemaphoreType.DMA((n_xfer,)), pltpu.SemaphoreType.DMA((n_xfer,)), pltpu.SemaphoreType.DMA((n_local,))]


def _scatter_plan(ins, outs, pos):
    me = _slot(pos)
    locals_, xfers = [], []
    for src, dst in zip(ins, outs):
        locals_.append((src.at[me], dst.at[me]))
        for k in range(1, 8):
            peer = _flip(pos, k)
            xfers.append((src.at[_slot(peer)], dst.at[me], dst.at[_slot(peer)], peer))
    return locals_, xfers


def _scatter_grads(grads):
    n = len(grads)

    def body(*refs):
        plan = _scatter_plan(refs[:n], refs[n:2 * n], _my_pos())
        _xfer_start(plan, refs[2 * n:])
        _xfer_wait(plan, refs[2 * n:])

    hbm = pl.BlockSpec(memory_space=pltpu.HBM)
    return pl.pallas_call(
        body, name="scatter_grads", out_shape=[jax.ShapeDtypeStruct(g.shape, g.dtype) for g in grads],
        in_specs=[hbm] * n, out_specs=[hbm] * n, scratch_shapes=_xfer_sems(n, 7 * n),
    )(*grads)


def _sibling_exchange(halves):
    n = len(halves)

    def body(*refs):
        ins, outs = refs[:n], refs[n:2 * n]
        send_sems, recv_sems, lsems = refs[2 * n:]
        x, y, c = _my_pos()
        sib = (x, y, 1 - c)
        mine, sends = [], []
        for a in range(n):
            m = pltpu.make_async_copy(ins[a], outs[a].at[c], lsems.at[a])
            m.start()
            mine.append(m)
            cp = pltpu.make_async_remote_copy(
                src_ref=ins[a], dst_ref=outs[a].at[c], send_sem=send_sems.at[a], recv_sem=recv_sems.at[a],
                device_id=sib, device_id_type=MESH)
            cp.start()
            sends.append(cp)
        for a in range(n):
            pltpu.make_async_remote_copy(
                src_ref=ins[a], dst_ref=outs[a].at[1 - c], send_sem=send_sems.at[a], recv_sem=recv_sems.at[a],
                device_id=sib, device_id_type=MESH).wait_recv()
        for cp in sends:
            cp.wait_send()
        for m in mine:
            m.wait()

    vm = pl.BlockSpec(memory_space=pltpu.VMEM)
    return pl.pallas_call(
        body, name="sibling_exchange",
        out_shape=[jax.ShapeDtypeStruct((2,) + h.shape, h.dtype) for h in halves],
        in_specs=[vm] * n, out_specs=[vm] * n,
        scratch_shapes=[pltpu.SemaphoreType.DMA((n,)), pltpu.SemaphoreType.DMA((n,)), pltpu.SemaphoreType.DMA((n,))],
        compiler_params=_cp(vmem=VMEM_BIG),
    )(*halves)


def _row_tile(r, cap=512):
    for t in (cap, 256, 128, 64, 32, 16, 8):
        if t <= cap and r % t == 0:
            return t
    return r


def _cast_bf16(a, name):
    R, W = a.shape
    tr = _row_tile(R, 256)

    def body(a_ref, o_ref):
        o_ref[...] = a_ref[...].astype(BF16)

    return pl.pallas_call(
        body, name=name, grid=(R // tr,), out_shape=jax.ShapeDtypeStruct((R, W), BF16),
        in_specs=[pl.BlockSpec((tr, W), lambda i: (i, 0))], out_specs=pl.BlockSpec((tr, W), lambda i: (i, 0)),
    )(a)


def _sum_slots(r, name):
    _, R, W = r.shape
    tr = _row_tile(R, 128)

    def body(r_ref, o_ref):
        acc = r_ref[0].astype(F32)
        for s_ in range(1, 8):
            acc = acc + r_ref[s_].astype(F32)
        o_ref[...] = acc.astype(BF16)

    return pl.pallas_call(
        body, name=name, grid=(R // tr,), out_shape=jax.ShapeDtypeStruct((R, W), BF16),
        in_specs=[pl.BlockSpec((8, tr, W), lambda i: (0, i, 0))], out_specs=pl.BlockSpec((tr, W), lambda i: (i, 0)),
        compiler_params=_cp(vmem=VMEM_BIG),
    )(r)


def _adamw(w, g, m, v, name):
    R, W = w.shape
    tr = _row_tile(R, 256)
    tc = 256 if (tr == R and R * W * 4 > (1 << 21) and W % 256 == 0) else W
    c1 = 1.0 / (1.0 - ADAM_B1 ** ADAM_STEP)
    c2 = 1.0 / (1.0 - ADAM_B2 ** ADAM_STEP)

    def body(w_ref, g_ref, m_ref, v_ref, d_ref, nm_ref, nv_ref, g32_ref):
        gg = g_ref[...].astype(F32)
        g32_ref[...] = gg
        nm = ADAM_B1 * m_ref[...] + (1.0 - ADAM_B1) * gg
        nv = ADAM_B2 * v_ref[...] + (1.0 - ADAM_B2) * (gg * gg)
        d_ref[...] = -ADAM_LR * ((nm * c1) / (jnp.sqrt(nv * c2) + ADAM_EPS) + ADAM_WD * w_ref[...])
        nm_ref[...] = nm
        nv_ref[...] = nv

    spec = pl.BlockSpec((tr, tc), lambda i, j: (i, j))
    return pl.pallas_call(
        body, name=name, grid=(R // tr, W // tc), out_shape=[jax.ShapeDtypeStruct((R, W), F32)] * 4,
        in_specs=[spec] * 4, out_specs=[spec] * 4, compiler_params=_cp(vmem=VMEM_BIG),
    )(w, g, m, v)


def _matmul_tn(a, b, tm, tn, tk, name, nn=1, b_map=None, out_shape=None, out_block=None, out_map=None,
               b_block=None):
    K, M = a.shape
    nk = K // tk
    nm = M // tm

    def body(a_ref, b_ref, o_ref, acc_ref):
        kk = pl.program_id(2)

        @pl.when(kk == 0)
        def _():
            acc_ref[...] = jnp.zeros_like(acc_ref)

        acc_ref[...] += lax.dot_general(a_ref[...], b_ref[...], _TN, preferred_element_type=F32)

        @pl.when(kk == nk - 1)
        def _():
            o_ref[...] = acc_ref[...].astype(BF16)

    return pl.pallas_call(
        body, name=name, grid=(nm, nn, nk),
        out_shape=jax.ShapeDtypeStruct(out_shape or (M, b.shape[-1]), BF16),
        in_specs=[pl.BlockSpec((tk, tm), lambda i, j, k: (k, i)),
                  pl.BlockSpec(b_block or (tk, tn), b_map or (lambda i, j, k: (k, j)))],
        out_specs=pl.BlockSpec(out_block or (tm, tn), out_map or (lambda i, j, k: (i, j))),
        scratch_shapes=[pltpu.VMEM((tm, tn), F32)],
        compiler_params=_cp(vmem=VMEM_BIG),
    )(a, b)


def _ada_fwd(cin, w_ada_s, b_s):
    tn = 512

    def body(c_ref, w_ref, b_ref, o_ref):
        o_ref[...] = _dg(_silu(c_ref[...]), w_ref[...], _NN) + b_ref[...]

    return pl.pallas_call(
        body, name="ada_fwd", grid=(1536 // tn,), out_shape=jax.ShapeDtypeStruct((64, 1536), F32),
        in_specs=[pl.BlockSpec((64, D), lambda j: (0, 0)), pl.BlockSpec((D, tn), lambda j: (0, j)),
                  pl.BlockSpec((1, tn), lambda j: (0, j))],
        out_specs=pl.BlockSpec((64, tn), lambda j: (0, j)),
        compiler_params=_cp(vmem=VMEM_BIG),
    )(cin, w_ada_s, b_s)


def _ada_bwd(cin, gmod_s, w_ada_s):
    tn = 512

    def body(c_ref, g_ref, w_ref, gw_ref, pc_ref):
        j = pl.program_id(0)
        g = g_ref[...]
        gw_ref[...] = _dg(_silu(c_ref[...]), g, _TN)
        rr = lax.broadcasted_iota(jnp.int32, (8, 64), 0)
        cc = lax.broadcasted_iota(jnp.int32, (8, 64), 1)
        sel = jnp.where((rr == 0) & ((cc & 7) == 2), 1.0, 0.0).astype(F32)
        dctx = _hmm(sel, g)
        part = _dg(dctx, w_ref[...], _NT)

        @pl.when(j == 0)
        def _():
            pc_ref[...] = jnp.zeros_like(pc_ref)

        pc_ref[...] += part

    return pl.pallas_call(
        body, name="ada_bwd", grid=(1536 // tn,),
        out_shape=[jax.ShapeDtypeStruct((D, 1536), F32), jax.ShapeDtypeStruct((8, D), F32)],
        in_specs=[pl.BlockSpec((64, D), lambda j: (0, 0)), pl.BlockSpec((64, tn), lambda j: (0, j)),
                  pl.BlockSpec((D, tn), lambda j: (0, j))],
        out_specs=[pl.BlockSpec((D, tn), lambda j: (0, j)), pl.BlockSpec((8, D), lambda j: (0, 0))],
        compiler_params=_cp(vmem=VMEM_BIG),
    )(cin, gmod_s, w_ada_s)


def _ada_small(gmod, parts, c_ctx):
    def body(g_ref, p_ref, c_ref, gb_ref, gc_ref):
        gb_ref[...] = jnp.sum(g_ref[...], axis=0, keepdims=True)
        ds = p_ref[0][0:1] + p_ref[2][0:1] + p_ref[4][0:1] + p_ref[6][0:1]
        cc = c_ref[...]
        s = _sigmoid(cc)
        gc_ref[...] = ds * (s * (1.0 + cc * (1.0 - s)))

    vm = pl.BlockSpec(memory_space=pltpu.VMEM)
    return pl.pallas_call(
        body, name="ada_small",
        out_shape=[jax.ShapeDtypeStruct((1, 6 * D), F32), jax.ShapeDtypeStruct((1, D), F32)],
        in_specs=[vm, vm, vm], out_specs=[vm, vm],
    )(gmod, parts, c_ctx)


def _tok_specs():
    xs = pl.BlockSpec((None, RT, D), lambda b, j: (b, jnp.maximum(j - 1, 0), 0))
    cs = pl.BlockSpec((None, RT, D), lambda b, j: (b, 0, 0))
    ms = pl.BlockSpec((None, 1, D), lambda b, j: (b, 0, 0))
    cms = pl.BlockSpec((None, 1, D), lambda b, j: (0, 0, 0))
    return xs, cs, ms, cms


def _inproj_fwd(x, ctx, sh1, sc1, csh1, csc1, n1w, w_in_p):
    def body(x_ref, c_ref, sh_ref, sc_ref, csh_ref, csc_ref, w1_ref, w_ref, p_ref, hm_ref):
        is_ctx = pl.program_id(1) == 0
        xt = jnp.where(is_ctx, c_ref[...], x_ref[...])
        sh = jnp.where(is_ctx, csh_ref[...], sh_ref[...])
        sc = jnp.where(is_ctx, csc_ref[...], sc_ref[...])
        hb = _norm_mod(xt, w1_ref[...], sh, sc).astype(BF16)
        hm_ref[...] = hb
        p_ref[...] = lax.dot_general(hb, w_ref[...], _NT, preferred_element_type=F32)

    xs, cs, ms, cms = _tok_specs()
    return pl.pallas_call(
        body, name="inproj_fwd", grid=(NB, T // RT),
        out_shape=[jax.ShapeDtypeStruct((NB, T, PC), F32), jax.ShapeDtypeStruct((NB, T, D), BF16)],
        in_specs=[xs, cs, ms, ms, cms, cms, pl.BlockSpec((1, D), lambda b, j: (0, 0)),
                  pl.BlockSpec((PC, D), lambda b, j: (0, 0))],
        out_specs=[pl.BlockSpec((None, RT, PC), lambda b, j: (b, j, 0)),
                   pl.BlockSpec((None, RT, D), lambda b, j: (b, j, 0))],
        compiler_params=_cp(vmem=VMEM_BIG),
    )(x, ctx, sh1, sc1, csh1, csc1, n1w, w_in_p)


def _inproj_bwd(dp_dn, dz, dp_na, dgt, w_in_p, x, ctx, sh1, sc1, csh1, csc1, n1w, dx1):
    def body(d1q_ref, d1k_ref, d1v_ref, d2_ref, d3_ref, d4_ref, w_ref, x_ref, c_ref, sh_ref, sc_ref, csh_ref, csc_ref,
             w1_ref, dx1_ref, dpb_ref, gx_ref, dsh_ref, dsc_ref, dcsh_ref, dcsc_ref, dw1_ref):
        b, j = pl.program_id(0), pl.program_id(1)
        is_ctx = j == 0
        dp = jnp.concatenate([d1q_ref[...], d1k_ref[...], d1v_ref[...], d2_ref[...], d3_ref[...], d4_ref[...]],
                             axis=1).astype(BF16)
        dpb_ref[...] = dp
        dhm = jnp.dot(dp, w_ref[...], preferred_element_type=F32)
        xt = jnp.where(is_ctx, c_ref[...], x_ref[...])
        sh = jnp.where(is_ctx, csh_ref[...], sh_ref[...])
        sc = jnp.where(is_ctx, csc_ref[...], sc_ref[...])
        _, vjp = jax.vjp(_norm_mod, xt, w1_ref[...], sh, sc)
        dxt, dw1, dsh, dsc = vjp(dhm)

        @pl.when((b == 0) & (j == 0))
        def _():
            dw1_ref[...] = jnp.zeros_like(dw1_ref)
            dcsh_ref[...] = jnp.zeros_like(dcsh_ref)
            dcsc_ref[...] = jnp.zeros_like(dcsc_ref)

        dw1_ref[...] += dw1

        @pl.when(is_ctx)
        def _():
            dcsh_ref[...] += dsh
            dcsc_ref[...] += dsc

        @pl.when(j == 1)
        def _():
            dsh_ref[...] = jnp.zeros_like(dsh_ref)
            dsc_ref[...] = jnp.zeros_like(dsc_ref)

        @pl.when(j >= 1)
        def _():
            dsh_ref[...] += dsh
            dsc_ref[...] += dsc
            gx_ref[...] = dx1_ref[...] + dxt

    xs, cs, ms, cms = _tok_specs()
    tsp = lambda w: pl.BlockSpec((None, RT, w), lambda b, j: (b, j, 0))
    one = pl.BlockSpec((1, D), lambda b, j: (0, 0))
    return pl.pallas_call(
        body, name="inproj_bwd", grid=(NB, T // RT),
        out_shape=[jax.ShapeDtypeStruct((NB, T, PC), BF16), jax.ShapeDtypeStruct((NB, SEQ, D), F32),
                   jax.ShapeDtypeStruct((NB, 1, D), F32), jax.ShapeDtypeStruct((NB, 1, D), F32),
                   jax.ShapeDtypeStruct((1, 1, D), F32), jax.ShapeDtypeStruct((1, 1, D), F32),
                   jax.ShapeDtypeStruct((1, D), F32)],
        in_specs=[tsp(512), tsp(512), tsp(512), tsp(512), tsp(1536), tsp(128), pl.BlockSpec((PC, D), lambda b, j: (0, 0)),
                  xs, cs, ms, ms, cms, cms, one, _rows2d_spec(D)],
        out_specs=[tsp(PC), xs, ms, ms, cms, cms, one],
        compiler_params=_cp(vmem=VMEM_BIG),
    )(*dp_dn, dz, dp_na, dgt, w_in_p, x, ctx, sh1, sc1, csh1, csc1, n1w, dx1)


def _shift_masks():
    t = lax.broadcasted_iota(jnp.int32, (T, 1), 0)
    lo = jnp.where(t < CT, 0, CT)
    hi = jnp.where(t < CT, CT, T)
    return {off: (t + off >= lo) & (t + off < hi) for off in range(-2, 3)}


def _shifted(u, off, masks):
    s = u if off == 0 else pltpu.roll(u, (-off) % T, 0)
    return jnp.where(masks[off], s, 0.0)


def _conv_pre(u, w, masks):
    acc = _shifted(u, -2, masks) * w[0:1, :]
    for kk in range(1, 5):
        acc = acc + _shifted(u, kk - 2, masks) * w[kk:kk + 1, :]
    return acc


def _act_q(pre):
    a = _silu(pre)
    return a * lax.rsqrt(jnp.sum(a * a, axis=-1, keepdims=True) + EPS) * (128.0 ** -0.5)


def _act_k(pre):
    a = _silu(pre)
    return a * lax.rsqrt(jnp.sum(a * a, axis=-1, keepdims=True) + EPS)


def _dn_prep_fwd(p, cw):
    def body(pq_ref, pk_ref, pv_ref, wq_ref, wk_ref, wv_ref, q_ref, k_ref, v_ref):
        masks = _shift_masks()
        q_ref[...] = _act_q(_conv_pre(pq_ref[...], wq_ref[...], masks))
        k_ref[...] = _act_k(_conv_pre(pk_ref[...], wk_ref[...], masks))
        v_ref[...] = _silu(_conv_pre(pv_ref[...], wv_ref[...], masks))

    ps = lambda o: pl.BlockSpec((None, T, 128), lambda b, h: (b, 0, o + h))
    ws = lambda o: pl.BlockSpec((8, 128), lambda b, h: (0, o + h))
    os_ = pl.BlockSpec((None, T, 128), lambda b, h: (b, 0, h))
    return pl.pallas_call(
        body, name="dn_prep_fwd", grid=(NB, DNH), out_shape=[jax.ShapeDtypeStruct((NB, T, DNW), F32)] * 3,
        in_specs=[ps(0), ps(4), ps(8), ws(0), ws(4), ws(8)], out_specs=[os_] * 3,
        compiler_params=_cp(vmem=VMEM_BIG),
    )(p, p, p, cw, cw, cw)


def _dn_prep_bwd(p, cw, dqs, dks, dvs):
    def body(pq_ref, pk_ref, pv_ref, wq_ref, wk_ref, wv_ref, dqf_ref, dqb_ref, dkf_ref, dkb_ref, dvf_ref, dvb_ref,
             oq_ref, ok_ref, ov_ref, gq_ref, gk_ref, gv_ref):
        b = pl.program_id(1)
        masks = _shift_masks()

        def one(p_ref, w_ref, df_ref, db_ref, act, o_ref, g_ref):
            u = p_ref[...]
            w = w_ref[...]
            _, vjp = jax.vjp(act, _conv_pre(u, w, masks))
            (dpre,) = vjp(df_ref[...] + db_ref[...])
            du = _shifted(dpre, 2, masks) * w[0:1, :]
            rows = [jnp.sum(dpre * _shifted(u, -2, masks), axis=0, keepdims=True)]
            for kk in range(1, 5):
                du = du + _shifted(dpre, 2 - kk, masks) * w[kk:kk + 1, :]
                rows.append(jnp.sum(dpre * _shifted(u, kk - 2, masks), axis=0, keepdims=True))
            o_ref[...] = du
            gw = jnp.concatenate(rows + [jnp.zeros((3, 128), F32)], axis=0)

            @pl.when(b == 0)
            def _():
                g_ref[...] = gw

            @pl.when(b > 0)
            def _():
                g_ref[...] += gw

        one(pq_ref, wq_ref, dqf_ref, dqb_ref, _act_q, oq_ref, gq_ref)
        one(pk_ref, wk_ref, dkf_ref, dkb_ref, _act_k, ok_ref, gk_ref)
        one(pv_ref, wv_ref, dvf_ref, dvb_ref, _silu, ov_ref, gv_ref)

    ps = lambda o: pl.BlockSpec((None, T, 128), lambda h, b: (b, 0, o + h))
    ws = lambda o: pl.BlockSpec((8, 128), lambda h, b: (0, o + h))
    ds = pl.BlockSpec((None, T, 128), lambda h, b: (b, 0, h))
    outs = pl.pallas_call(
        body, name="dn_prep_bwd", grid=(DNH, NB),
        out_shape=[jax.ShapeDtypeStruct((NB, T, DNW), F32)] * 3 + [jax.ShapeDtypeStruct((8, DNW), F32)] * 3,
        in_specs=[ps(0), ps(4), ps(8), ws(0), ws(4), ws(8)] + [ds] * 6,
        out_specs=[ds] * 3
        + [pl.BlockSpec((8, 128), lambda h, b: (0, h))] * 3,
        compiler_params=_cp(vmem=VMEM_BIG),
    )(p, p, p, cw, cw, cw, *dqs, *dks, *dvs)
    return outs


def _gate_fn(a, prm):
    lane = lax.broadcasted_iota(jnp.int32, a.shape, 1)
    z = a + prm[1:2, :]
    sp = jnp.maximum(z, 0.0) + jnp.log(1.0 + jnp.exp(-jnp.abs(z)))
    g = -jnp.exp(prm[0:1, :]) * sp
    return jnp.where(lane < 8, g, jnp.where(lane < 16, _sigmoid(a), 0.0))


def _gates_fwd(p, prm):
    def body(a_ref, prm_ref, o_ref):
        o_ref[...] = _gate_fn(a_ref[...], prm_ref[...])

    return pl.pallas_call(
        body, name="gates_fwd", grid=(NB,), out_shape=jax.ShapeDtypeStruct((NB, T, 128), F32),
        in_specs=[pl.BlockSpec((None, T, 128), lambda b: (b, 0, COL_G // 128)), pl.BlockSpec((8, 128), lambda b: (0, 0))],
        out_specs=pl.BlockSpec((None, T, 128), lambda b: (b, 0, 0)),
    )(p, prm)


def _gates_bwd(p, prm, dgf, dgb):
    def body(a_ref, prm_ref, dgf_ref, dgb_ref, o_ref, dprm_ref):
        b = pl.program_id(0)
        tot = dgf_ref[...] + dgb_ref[...]
        _, vjp = jax.vjp(_gate_fn, a_ref[...], prm_ref[...])
        da, dprm = vjp(tot)
        o_ref[...] = da

        @pl.when(b == 0)
        def _():
            dprm_ref[...] = dprm

        @pl.when(b > 0)
        def _():
            dprm_ref[...] += dprm

    return pl.pallas_call(
        body, name="gates_bwd", grid=(NB,),
        out_shape=[jax.ShapeDtypeStruct((NB, T, 128), F32), jax.ShapeDtypeStruct((8, 128), F32)],
        in_specs=[pl.BlockSpec((None, T, 128), lambda b: (b, 0, COL_G // 128)), pl.BlockSpec((8, 128), lambda b: (0, 0)),
                  pl.BlockSpec((None, T, 128), lambda b: (b, 0, 0)), pl.BlockSpec((None, T, 128), lambda b: (b, 0, 0))],
        out_specs=[pl.BlockSpec((None, T, 128), lambda b: (b, 0, 0)), pl.BlockSpec((8, 128), lambda b: (0, 0))],
        compiler_params=_cp(vmem=VMEM_BIG),
    )(p, prm, dgf, dgb)


def _lockstep(gens, joint=None):
    n = len(gens)
    res, vals = [None] * n, [None] * n
    while True:
        msgs = []
        for i, g in enumerate(gens):
            try:
                msgs.append((g.send(vals[i]), False))
            except StopIteration as e:
                res[i] = e.value
                msgs.append((None, True))
        if msgs[0][1]:
            return res
        if msgs[0][0] is not None:
            vals = list(joint(tuple(m[0] for m in msgs)))
        else:
            vals = [None] * n


def _tri_inv_gen(m, md, m1, m2, eye):
    n = -(m * md)
    t = eye + n
    pw = n
    for _ in range(3):
        pw = _dg(pw, pw, _NN)
        yield
        t = t + _dg(t, pw, _NN)
        yield
    a = _dg(t, m * m1, _NN)
    yield
    t = t - _dg(a, t, _NN)
    yield
    a = _dg(t, m * m2, _NN)
    yield
    return t - _dg(a, t, _NN)


def _tri_inv_bwd_gen(t, g):
    tt = t.T
    a = _dg(tt, g, _NN)
    yield
    return -_dg(a, tt, _NN)


@jax.custom_vjp
def _tri_inv(ms, md, m1, m2, eye):
    return tuple(_lockstep([_tri_inv_gen(m, md, m1, m2, eye) for m in ms]))


def _tri_inv_f(ms, md, m1, m2, eye):
    ts = _tri_inv(ms, md, m1, m2, eye)
    return ts, (ts, md)


def _tri_inv_b(res, gs):
    ts, md = res
    z = jnp.zeros_like(md)
    return (tuple(_lockstep([_tri_inv_bwd_gen(t, g) for t, g in zip(ts, gs)])), z, z, z, z)


_tri_inv.defvjp(_tri_inv_f, _tri_inv_b)


@jax.custom_vjp
def _tri_inv_saved(ms, ts):
    return tuple(t.astype(F32) for t in ts)


def _tri_inv_saved_f(ms, ts):
    return _tri_inv_saved(ms, ts), ts


def _tri_inv_saved_b(ts, gs):
    dms = tuple(_lockstep([_tri_inv_bwd_gen(t, g) for t, g in zip(ts, gs)]))
    return dms, tuple(jnp.zeros_like(t) for t in ts)


_tri_inv_saved.defvjp(_tri_inv_saved_f, _tri_inv_saved_b)


SR = DNH * CH


def _scan_masks():
    i, j = np.arange(SR)[:, None], np.arange(SR)[None, :]
    same = (i >> 6) == (j >> 6)
    cm = np.zeros((2, 6, SR, SR), np.float32)
    for d in range(2):
        diff = i - j if d == 0 else j - i
        cm[d, 0] = same & (diff >= 0)
        cm[d, 1] = same & (diff > 0)
        cm[d, 2] = (i >> 4) == (j >> 4)
        cm[d, 3] = ((i >> 5) == (j >> 5)) & ((i >> 4) != (j >> 4))
        cm[d, 4] = (i >> 5) != (j >> 5)
        cm[d, 5] = i == j
    t, u = np.arange(CH)[:, None], np.arange(CH)[None, :]
    c64 = np.stack([t >= u, t <= u]).astype(np.float32)
    bm = ((np.arange(SR)[:, None] >> 6) == (np.arange(DNW)[None, :] >> 7)).astype(np.float32)
    return jnp.asarray(cm), jnp.asarray(c64), jnp.asarray(bm)


@jax.custom_vjp
def _fold(x, bm):
    xm = x * bm
    return xm[:, 0:128] + xm[:, 128:256] + xm[:, 256:384] + xm[:, 384:512]


def _fold_f(x, bm):
    return _fold(x, bm), bm


def _fold_b(bm, g):
    return jnp.concatenate([g, g, g, g], axis=1) * bm, jnp.zeros_like(bm)


_fold.defvjp(_fold_f, _fold_b)


def _chunk_gen(q, k, v, gb, s, cm, c64, bm, d):
    incl, strict = cm[0], cm[1]
    lane = lax.broadcasted_iota(jnp.int32, (1, 128), 1)
    oh = lambda n: jnp.where(lane == n, 1.0, 0.0).astype(F32)
    col = lambda x, base: jnp.concatenate(
        [jnp.sum(x * oh(base + d * DNH + h), axis=1, keepdims=True) for h in range(DNH)], axis=0)
    beta = col(gb, 8)
    gam = col(_mask_mm(c64, gb), 0)
    yield
    tot = jnp.sum(gb, axis=0, keepdims=True)
    gtots = [jnp.sum(tot * oh(d * DNH + h), axis=1, keepdims=True) for h in range(DNH)]
    gtot = jnp.concatenate([jnp.broadcast_to(t, (CH, 1)) for t in gtots], axis=0)
    eg = jnp.concatenate([jnp.broadcast_to(jnp.exp(t), (1, 128)) for t in gtots], axis=1)
    row0 = jnp.where(lax.broadcasted_iota(jnp.int32, (128, 1), 0) == 0, 1.0, 0.0).astype(F32)
    gam_row = jnp.sum(jnp.broadcast_to(gam, (SR, 128)).T * row0, axis=0, keepdims=True)
    dec = jnp.exp(jnp.where(incl > 0.5, gam - gam_row, NEG))
    e_gam = jnp.exp(gam)
    kb = k * beta
    kk = _mm_nt(kb, k)
    yield
    tm = yield strict * (kk * dec)
    u = _mm(tm, v * beta)
    yield
    w = _mm(tm, kb * e_gam)
    yield
    qk = _mm_nt(q, k) * dec
    yield
    ws = _mm(w, s)
    yield
    v_new = u - _fold(ws, bm)
    qs = _mm(q * e_gam, s)
    yield
    o = _fold(qs, bm) + _mm(qk, v_new)
    yield
    vbd = jnp.concatenate([v_new, v_new, v_new, v_new], axis=1) * bm
    s_new = s * eg + _mm_tn(k * jnp.exp(gtot - gam), vbd)
    return o, s_new, tm


def _chunks(args, cm_ref, c64_ref, bm, saved=None):
    cm0 = cm_ref[0]
    if saved is None:
        inv = lambda ms: _tri_inv(ms, cm0[2], cm0[3], cm0[4], cm0[5])
    else:
        inv = lambda ms: _tri_inv_saved(ms, saved)
    return _lockstep([_chunk_gen(q, k, v, gb, s, cm_ref[d], c64_ref[d], bm, d) for q, k, v, gb, s, d in args], inv)


def _chunk_rev(s):
    return jnp.where(s < NCH_CTX, NCH_CTX - 1 - s, NCH + NCH_CTX - 1 - s)


def _scan_specs(step):
    cf = lambda s: step(s)
    cb = lambda s: _chunk_rev(step(s))
    tok = lambda c: pl.BlockSpec((NB, CH, DNW), lambda s: (0, c(s), 0))
    gat = lambda c: pl.BlockSpec((NB, CH, 128), lambda s: (0, c(s), 0))
    sta = lambda c: pl.BlockSpec((NB, None, 128, DNW), lambda s: (0, c(s), 0, 0))
    inv = lambda c: pl.BlockSpec((NB, None, SR, SR), lambda s: (0, c(s), 0, 0))
    return (tok(cf), gat(cf), sta(cf), inv(cf)), (tok(cb), gat(cb), sta(cb), inv(cb))


def _const_specs():
    return [pl.BlockSpec((2, 6, SR, SR), lambda s: (0, 0, 0, 0)), pl.BlockSpec((2, CH, CH), lambda s: (0, 0, 0)),
            pl.BlockSpec((SR, DNW), lambda s: (0, 0))]


SCAN_CHAINS = [(b, d) for b in range(NB) for d in range(2)]


def _stack(ref, b):
    return jnp.concatenate([ref[b, :, 128 * h:128 * h + 128] for h in range(DNH)], axis=0)


def _unstack_to(ref, b, val):
    for h in range(DNH):
        ref[b, :, 128 * h:128 * h + 128] = val[CH * h:CH * h + CH]


def _gather_plan(ins, outs, pos):
    me = _slot(pos)
    locals_, xfers = [], []
    for src, dst in zip(ins, outs):
        locals_.append((src, dst.at[me]))
        for k in range(1, 8):
            peer = _flip(pos, k)
            xfers.append((src, dst.at[me], dst.at[_slot(peer)], peer))
    return locals_, xfers


def _scan_fwd(q, k, v, gates, carry=()):
    nc = len(carry)

    def body(*refs):
        cm_ref, c64_ref, bm_ref, qf, kf, vf, gf, qb, kb, vb, gb = refs[:11]
        c_in = refs[11:11 + nc]
        of_ref, sf_ref, tf_ref, ob_ref, sb_ref, tb_ref = refs[11 + nc:17 + nc]
        c_out = refs[17 + nc:17 + 2 * nc]
        s_ref = refs[17 + 2 * nc]
        sems = refs[18 + 2 * nc:]
        if nc:
            @pl.when(pl.program_id(0) == 0)
            def _():
                _xfer_start(_gather_plan(c_in, c_out, _my_pos()), sems)

        @pl.when(pl.program_id(0) == 0)
        def _():
            s_ref[...] = jnp.zeros_like(s_ref)

        bm = bm_ref[...]
        dirs = ((qf, kf, vf, gf, of_ref, sf_ref, tf_ref), (qb, kb, vb, gb, ob_ref, sb_ref, tb_ref))
        args = [(_stack(dirs[d][0], b), _stack(dirs[d][1], b), _stack(dirs[d][2], b), dirs[d][3][b], s_ref[i], d)
                for i, (b, d) in enumerate(SCAN_CHAINS)]
        for i, (o, s_new, tm) in enumerate(_chunks(args, cm_ref, c64_ref, bm)):
            b, d = SCAN_CHAINS[i]
            dirs[d][5][b] = args[i][4]
            dirs[d][6][b] = tm.astype(BF16)
            _unstack_to(dirs[d][4], b, o)
            s_ref[i] = s_new

        if nc:
            @pl.when(pl.program_id(0) == NCH - 1)
            def _():
                _xfer_wait(_gather_plan(c_in, c_out, _my_pos()), sems)

    (tf, gf_, sf_, if_), (tb, gb_, sb_, ib_) = _scan_specs(lambda s: s)
    o_sh = jax.ShapeDtypeStruct((NB, T, DNW), F32)
    st_sh = jax.ShapeDtypeStruct((NB, NCH, 128, DNW), F32)
    inv_sh = jax.ShapeDtypeStruct((NB, NCH, SR, SR), BF16)
    hbm = pl.BlockSpec(memory_space=pltpu.HBM)
    return pl.pallas_call(
        body, name="scan_fwd", grid=(NCH,),
        out_shape=[o_sh, st_sh, inv_sh, o_sh, st_sh, inv_sh] + [jax.ShapeDtypeStruct((8,) + h.shape, h.dtype) for h in carry],
        in_specs=_const_specs() + [tf, tf, tf, gf_, tb, tb, tb, gb_] + [hbm] * nc,
        out_specs=[tf, sf_, if_, tb, sb_, ib_] + [hbm] * nc,
        scratch_shapes=[pltpu.VMEM((len(SCAN_CHAINS), 128, DNW), F32)] + (_xfer_sems(nc, 7 * nc) if nc else []),
        compiler_params=_cp(vmem=VMEM_BIG),
    )(*_scan_masks(), q, k, v, gates, q, k, v, gates, *carry)


def _scan_bwd(q, k, v, gates, st_f, st_b, inv_f, inv_b, do, carry=()):
    nc = len(carry)

    def body(*refs):
        (cm_ref, c64_ref, bm_ref, qf, kf, vf, gf, sf, tf_, dof, qb, kb, vb, gb, sb, tb_, dob) = refs[:17]
        c_in = refs[17:17 + nc]
        dqf, dkf, dvf, dgf, dqb, dkb, dvb, dgb = refs[17 + nc:25 + nc]
        c_out = refs[25 + nc:25 + 2 * nc]
        ds_ref = refs[25 + 2 * nc]
        sems = refs[26 + 2 * nc:]
        first = pl.program_id(0) == 0
        last = pl.program_id(0) == NCH - 1
        if nc:
            @pl.when(first)
            def _():
                _xfer_start(_scatter_plan(c_in, c_out, _my_pos()), sems)

        @pl.when(first)
        def _():
            ds_ref[...] = jnp.zeros_like(ds_ref)

        bm = bm_ref[...]
        dirs = ((qf, kf, vf, gf, sf, dof, dqf, dkf, dvf, dgf), (qb, kb, vb, gb, sb, dob, dqb, dkb, dvb, dgb))
        prim = tuple((_stack(dirs[d][0], b), _stack(dirs[d][1], b), _stack(dirs[d][2], b), dirs[d][3][b], dirs[d][4][b])
                     for b, d in SCAN_CHAINS)
        cots = tuple((_stack(dirs[d][5], b), ds_ref[i]) for i, (b, d) in enumerate(SCAN_CHAINS))
        saved = tuple((tf_, tb_)[d][b] for b, d in SCAN_CHAINS)
        _, vjp = jax.vjp(lambda ps: tuple(r[:2] for r in _chunks([p + (SCAN_CHAINS[i][1],) for i, p in enumerate(ps)],
                                                                cm_ref, c64_ref, bm, saved)), prim)
        (grads,) = vjp(cots)
        for i, (dq, dk, dv, dg, ds) in enumerate(grads):
            b, d = SCAN_CHAINS[i]
            _unstack_to(dirs[d][6], b, dq)
            _unstack_to(dirs[d][7], b, dk)
            _unstack_to(dirs[d][8], b, dv)
            dirs[d][9][b] = dg
            ds_ref[i] = ds

        if nc:
            @pl.when(last)
            def _():
                _xfer_wait(_scatter_plan(c_in, c_out, _my_pos()), sems)

    (tf, gf_, sf_, if_), (tb, gb_, sb_, ib_) = _scan_specs(lambda s: NCH - 1 - s)
    o_sh = jax.ShapeDtypeStruct((NB, T, DNW), F32)
    g_sh = jax.ShapeDtypeStruct((NB, T, 128), F32)
    hbm = pl.BlockSpec(memory_space=pltpu.HBM)
    return pl.pallas_call(
        body, name="scan_bwd", grid=(NCH,),
        out_shape=[o_sh, o_sh, o_sh, g_sh] * 2 + [jax.ShapeDtypeStruct(g.shape, g.dtype) for g in carry],
        in_specs=_const_specs() + [tf, tf, tf, gf_, sf_, if_, tf, tb, tb, tb, gb_, sb_, ib_, tb] + [hbm] * nc,
        out_specs=[tf, tf, tf, gf_, tb, tb, tb, gb_] + [hbm] * nc,
        scratch_shapes=[pltpu.VMEM((len(SCAN_CHAINS), 128, DNW), F32)] + (_xfer_sems(nc, 7 * nc) if nc else []),
        compiler_params=_cp(vmem=VMEM_BIG),
    )(*_scan_masks(), q, k, v, gates, st_f, inv_f, do, q, k, v, gates, st_b, inv_b, do, *carry)


def _segmean(a, bd):
    hi = a.astype(BF16)
    lo = (a - hi.astype(F32)).astype(BF16)
    return jnp.dot(hi, bd, preferred_element_type=F32) + jnp.dot(lo, bd, preferred_element_type=F32)


def _na_prep_fwd(p, qw, kw, bd):
    def body(q_ref, k_ref, v_ref, qw_ref, kw_ref, bd_ref, oq_ref, ok_ref, ov_ref):
        bdm = bd_ref[...]
        q = q_ref[...]
        k = k_ref[...]
        oq_ref[...] = (q * lax.rsqrt(_segmean(q * q, bdm) + EPS) * qw_ref[...] * 0.125).astype(BF16)
        ok_ref[...] = (k * lax.rsqrt(_segmean(k * k, bdm) + EPS) * kw_ref[...]).astype(BF16)
        ov_ref[...] = v_ref[...].astype(BF16)

    ps = lambda o: pl.BlockSpec((None, RT, NAW), lambda b, j: (b, j, COL_NA // NAW + o))
    one = pl.BlockSpec((1, NAW), lambda b, j: (0, 0))
    os_ = pl.BlockSpec((None, RT, NAW), lambda b, j: (b, j, 0))
    return pl.pallas_call(
        body, name="na_prep_fwd", grid=(NB, T // RT), out_shape=[jax.ShapeDtypeStruct((NB, T, NAW), BF16)] * 3,
        in_specs=[ps(0), ps(1), ps(2), one, one, pl.BlockSpec((NAW, NAW), lambda b, j: (0, 0))], out_specs=[os_] * 3,
    )(p, p, p, qw, kw, bd)


def _na_prep_bwd(p, qw, kw, bd, dq, dk, dv):
    def body(q_ref, k_ref, qw_ref, kw_ref, bd_ref, dq_ref, dk_ref, dv_ref, o_ref, gq_ref, gk_ref):
        b, j = pl.program_id(0), pl.program_id(1)
        bdm = bd_ref[...]

        def one(x, w, dy, scale):
            r = lax.rsqrt(_segmean(x * x, bdm) + EPS)
            g = dy * w * scale
            dx = r * g - x * (r * r * r) * _segmean(g * x, bdm)
            dw = jnp.sum(dy * x * r * scale, axis=0, keepdims=True)
            return dx, dw

        dqv = jnp.where(j == 0, 0.0, dq_ref[...])
        dxq, dwq = one(q_ref[...], qw_ref[...], dqv, 0.125)
        dxk, dwk = one(k_ref[...], kw_ref[...], dk_ref[...], 1.0)
        o_ref[...] = jnp.concatenate([dxq, dxk, dv_ref[...]], axis=1)
        pad = jnp.zeros((7, NAW), F32)

        @pl.when((b == 0) & (j == 0))
        def _():
            gq_ref[...] = jnp.zeros_like(gq_ref)
            gk_ref[...] = jnp.zeros_like(gk_ref)

        gq_ref[...] += jnp.concatenate([dwq, pad], axis=0)
        gk_ref[...] += jnp.concatenate([dwk, pad], axis=0)

    ps = lambda o: pl.BlockSpec((None, RT, NAW), lambda b, j: (b, j, COL_NA // NAW + o))
    one = pl.BlockSpec((1, NAW), lambda b, j: (0, 0))
    ts = pl.BlockSpec((None, RT, NAW), lambda b, j: (b, j, 0))
    acc = pl.BlockSpec((8, NAW), lambda b, j: (0, 0))
    return pl.pallas_call(
        body, name="na_prep_bwd", grid=(NB, T // RT),
        out_shape=[jax.ShapeDtypeStruct((NB, T, 1536), F32), jax.ShapeDtypeStruct((8, NAW), F32),
                   jax.ShapeDtypeStruct((8, NAW), F32)],
        in_specs=[ps(0), ps(1), one, one, pl.BlockSpec((NAW, NAW), lambda b, j: (0, 0)),
                  pl.BlockSpec((None, RT, NAW), lambda b, j: (b, jnp.maximum(j - 1, 0), 0)), ts, ts],
        out_specs=[pl.BlockSpec((None, RT, 1536), lambda b, j: (b, j, 0)), acc, acc],
        compiler_params=_cp(vmem=VMEM_BIG),
    )(p, p, qw, kw, bd, dq, dk, dv)


def _na_row0(r):
    return jnp.clip(r - WIN_ROWS // 2, 0, ROWS - WIN_ROWS)


def _na_cls(r):
    return _na_row0(r) - r + WIN_ROWS - 1


NHS = 4
NLW = NHS * 64
NSR = NHS * GRID_W


def _na_stack(x):
    lane = lax.broadcasted_iota(jnp.int32, (1, NLW), 1)
    return jnp.concatenate([jnp.where((lane >> 6) == h, x, jnp.zeros_like(x)) for h in range(NHS)], axis=0)


def _na_unstack(y):
    lane = lax.broadcasted_iota(jnp.int32, (1, NLW), 1)
    out = jnp.zeros((GRID_W, NLW), y.dtype)
    for h in range(NHS):
        out = jnp.where((lane >> 6) == h, y[GRID_W * h:GRID_W * h + GRID_W], out)
    return out


def _na_probs(q4, kw, kc, bias):
    s_w = lax.dot_general(q4, kw, _NT, preferred_element_type=F32) + bias
    s_c = lax.dot_general(q4, kc, _NT, preferred_element_type=F32)
    mx = jnp.maximum(jnp.max(s_w, axis=1, keepdims=True), jnp.max(s_c, axis=1, keepdims=True))
    p_w = jnp.exp(s_w - mx)
    p_c = jnp.exp(s_c - mx)
    inv = 1.0 / (jnp.sum(p_w, axis=1, keepdims=True) + jnp.sum(p_c, axis=1, keepdims=True))
    return p_w, p_c, inv


def _na_specs():
    qs = pl.BlockSpec((None, GRID_W, NLW), lambda g, b, r: (b, CT // GRID_W + r, g))
    kv = pl.BlockSpec((None, T, NLW), lambda g, b, r: (b, 0, g))
    bs = pl.BlockSpec((None, None, NSR, NWIN), lambda g, b, r: (_na_cls(r), g, 0, 0))
    lat = pl.BlockSpec((None, GRID_W, NLW), lambda g, b, r: (b, r, g))
    return qs, kv, bs, lat


def _na_fwd(qn, kn, vb, bias, carry=()):
    nc = len(carry)
    grid = (NAH // NHS, NB, ROWS)

    def body(*refs):
        q_ref, k_ref, v_ref, b_ref = refs[:4]
        c_in, o_ref, c_out, sems = refs[4:4 + nc], refs[4 + nc], refs[5 + nc:5 + 2 * nc], refs[5 + 2 * nc:]
        ids = [pl.program_id(a) for a in range(3)]
        if nc:
            @pl.when((ids[0] == 0) & (ids[1] == 0) & (ids[2] == 0))
            def _():
                _xfer_start(_gather_plan(c_in, c_out, _my_pos()), sems)

        r = ids[2]
        start = pl.multiple_of(CT + _na_row0(r) * GRID_W, GRID_W)
        kw, vw = k_ref[pl.ds(start, NWIN), :], v_ref[pl.ds(start, NWIN), :]
        kc, vc = k_ref[0:CT, :], v_ref[0:CT, :]
        p_w, p_c, inv = _na_probs(_na_stack(q_ref[...]), kw, kc, b_ref[...])
        o4 = (jnp.dot(p_w.astype(BF16), vw, preferred_element_type=F32)
              + jnp.dot(p_c.astype(BF16), vc, preferred_element_type=F32)) * inv
        o_ref[...] = _na_unstack(o4).astype(BF16)

        if nc:
            @pl.when((ids[0] == grid[0] - 1) & (ids[1] == grid[1] - 1) & (ids[2] == grid[2] - 1))
            def _():
                _xfer_wait(_gather_plan(c_in, c_out, _my_pos()), sems)

    qs, kv, bs, lat = _na_specs()
    hbm = pl.BlockSpec(memory_space=pltpu.HBM)
    res = pl.pallas_call(
        body, name="na_fwd", grid=grid,
        out_shape=[jax.ShapeDtypeStruct((NB, SEQ, NAW), BF16)] + [jax.ShapeDtypeStruct((8,) + h.shape, h.dtype) for h in carry],
        in_specs=[qs, kv, kv, bs] + [hbm] * nc, out_specs=[lat] + [hbm] * nc,
        scratch_shapes=_xfer_sems(nc, 7 * nc) if nc else [],
        compiler_params=_cp(vmem=VMEM_BIG),
    )(qn, kn, vb, bias, *carry)
    return res[0], list(res[1:])


def _na_bwd(qn, kn, vb, bias, do):
    def body(q_ref, k_ref, v_ref, b_ref, do_ref, dq_ref, dk_ref, dv_ref, db_ref):
        b, r = pl.program_id(1), pl.program_id(2)

        @pl.when((b == 0) & (r == 0))
        def _():
            db_ref[...] = jnp.zeros_like(db_ref)

        @pl.when(r == 0)
        def _():
            dk_ref[...] = jnp.zeros_like(dk_ref)
            dv_ref[...] = jnp.zeros_like(dv_ref)

        start = pl.multiple_of(CT + _na_row0(r) * GRID_W, GRID_W)
        cls = _na_cls(r)
        kw, vw = k_ref[pl.ds(start, NWIN), :], v_ref[pl.ds(start, NWIN), :]
        kc, vc = k_ref[0:CT, :], v_ref[0:CT, :]
        q4 = _na_stack(q_ref[...])
        p_w, p_c, inv = _na_probs(q4, kw, kc, b_ref[...])
        p_w = p_w * inv
        p_c = p_c * inv
        do4 = _na_stack(do_ref[...].astype(BF16))
        dp_w = lax.dot_general(do4, vw, _NT, preferred_element_type=F32)
        dp_c = lax.dot_general(do4, vc, _NT, preferred_element_type=F32)
        delta = jnp.sum(p_w * dp_w, axis=1, keepdims=True) + jnp.sum(p_c * dp_c, axis=1, keepdims=True)
        ds_w = p_w * (dp_w - delta)
        ds_c = p_c * (dp_c - delta)
        db_ref[cls] += ds_w
        ds_wb, ds_cb = ds_w.astype(BF16), ds_c.astype(BF16)
        dq_ref[...] = _na_unstack(jnp.dot(ds_wb, kw, preferred_element_type=F32)
                                  + jnp.dot(ds_cb, kc, preferred_element_type=F32))
        dk_ref[pl.ds(start, NWIN), :] += lax.dot_general(ds_wb, q4, _TN, preferred_element_type=F32)
        dv_ref[pl.ds(start, NWIN), :] += lax.dot_general(p_w.astype(BF16), do4, _TN, preferred_element_type=F32)
        dk_ref[0:CT, :] += lax.dot_general(ds_cb, q4, _TN, preferred_element_type=F32)
        dv_ref[0:CT, :] += lax.dot_general(p_c.astype(BF16), do4, _TN, preferred_element_type=F32)

    qs, kv, bs, lat = _na_specs()
    return pl.pallas_call(
        body, name="na_bwd", grid=(NAH // NHS, NB, ROWS),
        out_shape=[jax.ShapeDtypeStruct((NB, SEQ, NAW), F32)] + [jax.ShapeDtypeStruct((NB, T, NAW), F32)] * 2
        + [jax.ShapeDtypeStruct((WIN_ROWS, NAH // NHS, NSR, NWIN), F32)],
        in_specs=[qs, kv, kv, bs, lat],
        out_specs=[lat, kv, kv, pl.BlockSpec((WIN_ROWS, None, NSR, NWIN), lambda g, b, r: (0, g, 0, 0))],
        compiler_params=_cp(vmem=VMEM_BIG),
    )(qn, kn, vb, bias, do)


def _na_tables():
    cols = np.arange(GRID_W)
    win_start = np.clip(cols - WIN_COLS // 2, 0, GRID_W - WIN_COLS)
    kc = cols[None, :]
    valid = (kc >= win_start[:, None]) & (kc < win_start[:, None] + WIN_COLS)
    rel = np.clip(kc - cols[:, None] + WIN_COLS - 1, 0, 2 * WIN_COLS - 2)
    return valid, rel


def _bias_table(rpb):
    valid, rel = _na_tables()
    nrel = 2 * WIN_COLS - 1
    shifts = np.stack([(rel == r) & valid for r in range(nrel)]).astype(np.float32)
    neg = np.where(valid, 0.0, NEG).astype(np.float32)

    def body(rpb_ref, sh_ref, neg_ref, o_ref):
        h = pl.program_id(0)
        for dr in range(2 * WIN_ROWS - 1):
            tile = neg_ref[...]
            for r in range(nrel):
                tile = tile + rpb_ref[h, dr, r] * sh_ref[r]
            for cls in range(WIN_ROWS):
                i = dr - cls
                if 0 <= i < WIN_ROWS:
                    o_ref[cls, :, i * GRID_W:(i + 1) * GRID_W] = tile

    return pl.pallas_call(
        body, name="bias_table", grid=(NAH,),
        out_shape=jax.ShapeDtypeStruct((WIN_ROWS, NAH // NHS, NSR, NWIN), F32),
        in_specs=[pl.BlockSpec(memory_space=pltpu.SMEM), pl.BlockSpec((nrel, GRID_W, GRID_W), lambda h: (0, 0, 0)),
                  pl.BlockSpec((GRID_W, GRID_W), lambda h: (0, 0))],
        out_specs=pl.BlockSpec((WIN_ROWS, None, GRID_W, NWIN), lambda h: (0, h // NHS, h % NHS, 0)),
    )(rpb, jnp.asarray(shifts), jnp.asarray(neg))


def _rpb_fold(db):
    def body1(d_ref, o_ref):
        for dr in range(2 * WIN_ROWS - 1):
            acc = None
            for c in range(WIN_ROWS):
                i = dr - c
                if 0 <= i < WIN_ROWS:
                    blk = d_ref[c, :, i * GRID_W:(i + 1) * GRID_W]
                    acc = blk if acc is None else acc + blk
            o_ref[dr] = acc

    t15 = pl.pallas_call(
        body1, name="rpb_fold_rows", grid=(NAH,),
        out_shape=jax.ShapeDtypeStruct((NAH, 2 * WIN_ROWS - 1, GRID_W, GRID_W), F32),
        in_specs=[pl.BlockSpec((WIN_ROWS, None, GRID_W, NWIN), lambda h: (0, h, 0, 0))],
        out_specs=pl.BlockSpec((None, 2 * WIN_ROWS - 1, GRID_W, GRID_W), lambda h: (h, 0, 0, 0)),
        compiler_params=_cp(vmem=VMEM_BIG),
    )(db.reshape(WIN_ROWS, NAH, GRID_W, NWIN))
    valid, rel = _na_tables()
    onehot = np.zeros((GRID_W * GRID_W, 128), np.float32)
    flat_rel, flat_valid = rel.reshape(-1), valid.reshape(-1)
    onehot[np.arange(GRID_W * GRID_W)[flat_valid], flat_rel[flat_valid]] = 1.0

    def body2(a_ref, oh_ref, o_ref):
        o_ref[...] = _hmm(a_ref[...], oh_ref[...])

    vm = pl.BlockSpec(memory_space=pltpu.VMEM)
    out = pl.pallas_call(
        body2, name="rpb_fold_cols", out_shape=jax.ShapeDtypeStruct((NAH * 15, 128), F32),
        in_specs=[vm, vm], out_specs=vm, compiler_params=_cp(vmem=VMEM_BIG),
    )(t15.reshape(NAH * 15, GRID_W * GRID_W), jnp.asarray(onehot))
    return out[:, :31].reshape(NAH, 15, 31)


def _dn_out(o, z, w):
    outs = []
    for h in range(DNH):
        sl = slice(128 * h, 128 * h + 128)
        oh = o[:, sl]
        r = lax.rsqrt(jnp.mean(oh * oh, axis=-1, keepdims=True) + EPS)
        outs.append(oh * r * w[:, sl] * _silu(z[:, sl]))
    return jnp.concatenate(outs, axis=1)


def _outproj_fwd(of, ob, p, na, onw, x, g1, w_out):
    def body(of_ref, ob_ref, z_ref, na_ref, w_ref, x_ref, g_ref, wo_ref, x1_ref, ai_ref, ao_ref):
        dn = _dn_out(of_ref[...] + ob_ref[...], z_ref[...], w_ref[...])
        ai = jnp.concatenate([dn.astype(BF16), na_ref[...]], axis=1)
        ai_ref[...] = ai
        ao = jnp.dot(ai, wo_ref[...], preferred_element_type=F32)
        ao_ref[...] = ao
        x1_ref[...] = x_ref[...] + g_ref[...] * ao

    lat = lambda w: pl.BlockSpec((None, RT, w), lambda b, j: (b, j, 0))
    return pl.pallas_call(
        body, name="outproj_fwd", grid=(NB, SEQ // RT),
        out_shape=[jax.ShapeDtypeStruct((NB, SEQ, D), F32), jax.ShapeDtypeStruct((NB, SEQ, D), BF16),
                   jax.ShapeDtypeStruct((NB, SEQ, D), F32)],
        in_specs=[pl.BlockSpec((None, RT, DNW), lambda b, j: (b, j + 1, 0)),
                  pl.BlockSpec((None, RT, DNW), lambda b, j: (b, j + 1, 0)),
                  pl.BlockSpec((None, RT, DNW), lambda b, j: (b, j + 1, COL_Z // DNW)),
                  lat(NAW), pl.BlockSpec((1, DNW), lambda b, j: (0, 0)), lat(D),
                  pl.BlockSpec((None, 1, D), lambda b, j: (b, 0, 0)), pl.BlockSpec((D, D), lambda b, j: (0, 0))],
        out_specs=[lat(D), lat(D), lat(D)],
        compiler_params=_cp(vmem=VMEM_BIG),
    )(of, ob, p, na, onw, x, g1, w_out)


def _outproj_bwd(dx1, ao, g1, w_out, of, ob, p, onw):
    def body(dx_ref, ao_ref, g_ref, wo_ref, of_ref, ob_ref, z_ref, w_ref, dao_ref, do_ref, dz_ref, dna_ref, dg_ref, dw_ref):
        b, j = pl.program_id(0), pl.program_id(1)

        @pl.when((b == 0) & (j == 0))
        def _():
            dw_ref[...] = jnp.zeros_like(dw_ref)

        @pl.when(j == 0)
        def _():
            do_ref[...] = jnp.zeros_like(do_ref)
            dz_ref[...] = jnp.zeros_like(dz_ref)

        @pl.when(j == 1)
        def _():
            dg_ref[...] = jnp.zeros_like(dg_ref)

        @pl.when(j >= 1)
        def _():
            dx = dx_ref[...]
            dg_ref[...] += jnp.sum(dx * ao_ref[...], axis=0, keepdims=True)
            dao = (g_ref[...] * dx).astype(BF16)
            dao_ref[...] = dao
            dai = lax.dot_general(dao, wo_ref[...], _NT, preferred_element_type=F32)
            dna_ref[...] = dai[:, DNW:]
            _, vjp = jax.vjp(_dn_out, of_ref[...] + ob_ref[...], z_ref[...], w_ref[...])
            do, dz, dw = vjp(dai[:, :DNW])
            do_ref[...] = do
            dz_ref[...] = dz
            dw_ref[...] += jnp.concatenate([dw, jnp.zeros((7, DNW), F32)], axis=0)

    lat = lambda w: pl.BlockSpec((None, RT, w), lambda b, j: (b, jnp.maximum(j - 1, 0), 0))
    tok = lambda w: pl.BlockSpec((None, RT, w), lambda b, j: (b, j, 0))
    return pl.pallas_call(
        body, name="outproj_bwd", grid=(NB, T // RT),
        out_shape=[jax.ShapeDtypeStruct((NB, SEQ, D), BF16), jax.ShapeDtypeStruct((NB, T, DNW), F32),
                   jax.ShapeDtypeStruct((NB, T, DNW), F32), jax.ShapeDtypeStruct((NB, SEQ, NAW), F32),
                   jax.ShapeDtypeStruct((NB, 1, D), F32), jax.ShapeDtypeStruct((8, DNW), F32)],
        in_specs=[_rows2d_spec(D), lat(D), pl.BlockSpec((None, 1, D), lambda b, j: (b, 0, 0)),
                  pl.BlockSpec((D, D), lambda b, j: (0, 0)),
                  tok(DNW), tok(DNW),
                  pl.BlockSpec((None, RT, DNW), lambda b, j: (b, j, COL_Z // DNW)),
                  pl.BlockSpec((1, DNW), lambda b, j: (0, 0))],
        out_specs=[lat(D), tok(DNW), tok(DNW), lat(NAW), pl.BlockSpec((None, 1, D), lambda b, j: (b, 0, 0)),
                   pl.BlockSpec((8, DNW), lambda b, j: (0, 0))],
        compiler_params=_cp(vmem=VMEM_BIG),
    )(dx1, ao, g1, w_out, of, ob, p, onw)


FT_FWD, FT_BWD = 512, 256
NFF = DFF // FFT


def _ffn_specs(ft):
    row = lambda w: pl.BlockSpec((ft, w), lambda i, f: (i, 0))
    ex = pl.BlockSpec((None, 1, D), lambda i, f: (i // (SEQ // ft), 0, 0))
    one = pl.BlockSpec((1, D), lambda i, f: (0, 0))
    wg = pl.BlockSpec((None, D, FFT), lambda i, f: (f, 0, 0))
    wu = pl.BlockSpec((None, D, FFT), lambda i, f: (NFF + f, 0, 0))
    wo = pl.BlockSpec((FFT, D), lambda i, f: (f, 0))
    ff = pl.BlockSpec((ft, FFT), lambda i, f: (i, f))
    return row, ex, one, wg, wu, wo, ff


def _ffn_fwd(x1, sh2, sc2, g2, n2w, w_in4, w_out, target):
    ft, tpe = FT_FWD, SEQ // FT_FWD

    def body(x_ref, sh_ref, sc_ref, g_ref, w2_ref, wg_ref, wu_ref, wo_ref, t_ref,
             h2_ref, gt_ref, up_ref, dy_ref, dg_ref, loss_ref, acc_ref):
        i, f = pl.program_id(0), pl.program_id(1)

        @pl.when(f == 0)
        def _():
            h2_ref[...] = _norm_mod(x_ref[...], w2_ref[...], sh_ref[...], sc_ref[...]).astype(BF16)
            acc_ref[...] = jnp.zeros_like(acc_ref)

        h2 = h2_ref[...]
        gate = jnp.dot(h2, wg_ref[...], preferred_element_type=F32)
        up = jnp.dot(h2, wu_ref[...], preferred_element_type=F32)
        gt_ref[...] = gate.astype(BF16)
        up_ref[...] = up.astype(BF16)
        acc_ref[...] += jnp.dot((_silu(gate) * up).astype(BF16), wo_ref[...], preferred_element_type=F32)

        @pl.when((i == 0) & (f == 0))
        def _():
            loss_ref[...] = jnp.zeros_like(loss_ref)

        @pl.when(f == NFF - 1)
        def _():
            ffn = acc_ref[...]
            err = x_ref[...] + g_ref[...] * ffn - t_ref[...]
            dy = err * (1.0 / D)
            dy_ref[...] = dy
            part = jnp.sum(dy * ffn, axis=0, keepdims=True)
            loss_ref[...] += jnp.sum(jnp.sum(err * err, axis=1, keepdims=True), axis=0, keepdims=True) * (0.5 / D)

            @pl.when(i % tpe == 0)
            def _():
                dg_ref[...] = part

            @pl.when(i % tpe != 0)
            def _():
                dg_ref[...] += part

    row, ex, one, wg, wu, wo, ff = _ffn_specs(ft)
    return pl.pallas_call(
        body, name="ffn_fwd", grid=(NTOK // ft, NFF),
        out_shape=[jax.ShapeDtypeStruct((NTOK, D), BF16), jax.ShapeDtypeStruct((NTOK, DFF), BF16),
                   jax.ShapeDtypeStruct((NTOK, DFF), BF16), jax.ShapeDtypeStruct((NTOK, D), F32),
                   jax.ShapeDtypeStruct((NB, 1, D), F32), jax.ShapeDtypeStruct((8, 128), F32)],
        in_specs=[row(D), ex, ex, ex, one, wg, wu, wo, row(D)],
        out_specs=[row(D), ff, ff, row(D), ex, pl.BlockSpec((8, 128), lambda i, f: (0, 0))],
        scratch_shapes=[pltpu.VMEM((ft, D), F32)],
        compiler_params=_cp(vmem=VMEM_BIG),
    )(x1, sh2, sc2, g2, n2w, w_in4, w_in4, w_out, target)


def _rows2d_spec(width):
    return pl.BlockSpec((RT, width), lambda b, j: (b * (SEQ // RT) + jnp.maximum(j - 1, 0), 0))


def _ffn_bwd(dy, gate, up, x1, sh2, sc2, g2, n2w, w_in4, w_out):
    ft, tpe = FT_BWD, SEQ // FT_BWD
    ni = NTOK // ft

    def body(dy_ref, gt_ref, up_ref, x_ref, sh_ref, sc_ref, g_ref, w2_ref, wg_ref, wu_ref, wo_ref,
             dff_ref, act_ref, dgu_ref, dx_ref, dsh_ref, dsc_ref, dw_ref, acc_ref):
        f, i = pl.program_id(0), pl.program_id(1)
        rows = pl.ds(pl.multiple_of(i * ft, ft), ft)
        dff = (g_ref[...] * dy_ref[...]).astype(BF16)
        dff_ref[...] = dff
        dact = lax.dot_general(dff, wo_ref[...], _NT, preferred_element_type=F32)
        gate, up = gt_ref[...].astype(F32), up_ref[...].astype(F32)
        sg = _sigmoid(gate)
        sl = gate * sg
        act_ref[...] = (sl * up).astype(BF16)
        dgate = (dact * up * (sg * (1.0 + gate * (1.0 - sg)))).astype(BF16)
        dup = (dact * sl).astype(BF16)
        dgu_ref[0] = dgate
        dgu_ref[1] = dup
        part = (lax.dot_general(dgate, wg_ref[...], _NT, preferred_element_type=F32)
                + lax.dot_general(dup, wu_ref[...], _NT, preferred_element_type=F32))

        @pl.when(f == 0)
        def _():
            acc_ref[rows, :] = part

        @pl.when(f > 0)
        def _():
            acc_ref[rows, :] += part

        @pl.when((i == 0) & (f == 0))
        def _():
            dw_ref[...] = jnp.zeros_like(dw_ref)

        @pl.when(f == NFF - 1)
        def _():
            _, vjp = jax.vjp(_norm_mod, x_ref[...], w2_ref[...], sh_ref[...], sc_ref[...])
            dx, dw, dsh, dsc = vjp(acc_ref[rows, :])
            dx_ref[...] = dy_ref[...] + dx
            dw_ref[...] += dw

            @pl.when(i % tpe == 0)
            def _():
                dsh_ref[...] = dsh
                dsc_ref[...] = dsc

            @pl.when(i % tpe != 0)
            def _():
                dsh_ref[...] += dsh
                dsc_ref[...] += dsc

    last = lambda f, real, spare: jnp.where(f == NFF - 1, real, spare)
    row = lambda w: pl.BlockSpec((ft, w), lambda f, i: (i, 0))
    ex = pl.BlockSpec((None, 1, D), lambda f, i: (i // tpe, 0, 0))
    one = pl.BlockSpec((1, D), lambda f, i: (0, 0))
    wg = pl.BlockSpec((None, D, FFT), lambda f, i: (f, 0, 0))
    wu = pl.BlockSpec((None, D, FFT), lambda f, i: (NFF + f, 0, 0))
    wo = pl.BlockSpec((FFT, D), lambda f, i: (f, 0))
    ff = pl.BlockSpec((ft, FFT), lambda f, i: (i, f))
    row_once = lambda w: pl.BlockSpec((ft, w), lambda f, i: (last(f, i, ni), 0))
    ex_once = pl.BlockSpec((None, 1, D), lambda f, i: (last(f, i // tpe, NB), 0, 0))
    return pl.pallas_call(
        body, name="ffn_bwd", grid=(NFF, ni),
        out_shape=[jax.ShapeDtypeStruct((NTOK + ft, D), BF16), jax.ShapeDtypeStruct((NTOK, DFF), BF16),
                   jax.ShapeDtypeStruct((2, NTOK, DFF), BF16), jax.ShapeDtypeStruct((NTOK + ft, D), F32),
                   jax.ShapeDtypeStruct((NB + 1, 1, D), F32), jax.ShapeDtypeStruct((NB + 1, 1, D), F32),
                   jax.ShapeDtypeStruct((1, D), F32)],
        in_specs=[row(D), ff, ff, row(D), ex, ex, ex, one, wg, wu, wo],
        out_specs=[row_once(D), ff, pl.BlockSpec((2, ft, FFT), lambda f, i: (0, i, f)), row_once(D), ex_once, ex_once, one],
        scratch_shapes=[pltpu.VMEM((NTOK, D), F32)],
        compiler_params=_cp(vmem=VMEM_BIG),
    )(dy, gate, up, x1, sh2, sc2, g2, n2w, w_in4, w_in4, w_out)


def _local_step(x, ctx, target, mod_x, mod_c, norm1_w, norm2_w, w_in_p, conv_w, a_log, dt_bias, out_norm_w,
                q_norm_w, k_norm_w, rpb, w_out_b, w_ffn_in4, w_ffn_out_b, scatter_early=False, gather_late=None):
    sh1, sc1, g1, sh2, sc2, g2 = [mod_x[:, i * D:(i + 1) * D].reshape(NB, 1, D) for i in range(6)]
    csh1, csc1 = mod_c[0:D].reshape(1, 1, D), mod_c[D:2 * D].reshape(1, 1, D)
    cw = jnp.concatenate([conv_w, jnp.zeros((3, 1536), F32)], axis=0)
    prm = jnp.pad(jnp.stack([a_log.reshape(8), dt_bias.reshape(8)]), ((0, 6), (0, 120)))
    onw = jnp.tile(out_norm_w.reshape(1, 128), (1, DNH))
    qw = jnp.tile(q_norm_w.reshape(1, 64), (1, NAH))
    kw = jnp.tile(k_norm_w.reshape(1, 64), (1, NAH))
    seg = np.arange(NAW) // 64
    bd = jnp.asarray((seg[:, None] == seg[None, :]).astype(np.float32) / 64.0, dtype=BF16)
    bias = _bias_table(rpb)

    p, hm = _inproj_fwd(x, ctx, sh1, sc1, csh1, csc1, norm1_w, w_in_p)
    dq_, dk_, dv_ = _dn_prep_fwd(p, cw)
    gates = _gates_fwd(p, prm)
    res = _scan_fwd(dq_, dk_, dv_, gates, carry=gather_late[:2] if gather_late else ())
    o_f, st_f, inv_f, o_b, st_b, inv_b = res[:6]
    qn, kn, vb = _na_prep_fwd(p, qw, kw, bd)
    na, got = _na_fwd(qn, kn, vb, bias, carry=gather_late[2:] if gather_late else ())
    if gather_late:
        w_out_b = res[6].reshape(D, D)
        w_ffn_in4 = res[7].reshape(4, D, FFT)
        w_ffn_out_b = got[0].reshape(DFF, D)
    x1, ai, ao = _outproj_fwd(o_f, o_b, p, na, onw, x, g1, w_out_b)
    x1f = x1.reshape(NTOK, D)
    h2, gate, up, dy, dg2, loss8 = _ffn_fwd(x1f, sh2, sc2, g2, norm2_w, w_ffn_in4, w_ffn_out_b,
                                             target.reshape(NTOK, D))
    dff, act, dgu, dx1, dsh2, dsc2, dn2w = _ffn_bwd(dy, gate, up, x1f, sh2, sc2, g2, norm2_w, w_ffn_in4, w_ffn_out_b)
    g_w_ffn_out = _matmul_tn(act, dff, FFT, D, 512, "wgrad_ffn_out")
    g_w_ffn_in4 = _matmul_tn(
        h2, dgu, D, FFT, 1024, "wgrad_ffn_in", nn=2 * NFF,
        b_block=(None, 1024, FFT), b_map=lambda i, j, k: (j // NFF, k, j % NFF),
        out_shape=(4, D, FFT), out_block=(None, D, FFT), out_map=lambda i, j, k: (j, i, 0))
    dsh2, dsc2 = dsh2[:NB], dsc2[:NB]
    dao, do, dz, dna, dg1, donw = _outproj_bwd(dx1, ao, g1, w_out_b, o_f, o_b, p, onw)
    g_w_out = _matmul_tn(ai.reshape(NTOK, D), dao.reshape(NTOK, D), D, D, 1024, "wgrad_out")
    dqn, dkn, dvn, dbias = _na_bwd(qn, kn, vb, bias, dna)
    dp_na, dqw, dkw = _na_prep_bwd(p, qw, kw, bd, dqn, dkn, dvn)
    early = [g_w_out.reshape(8, D // 8, D), g_w_ffn_in4.reshape(8, D // 2, FFT), g_w_ffn_out.reshape(8, DFF // 8, D)]
    res = _scan_bwd(dq_, dk_, dv_, gates, st_f, st_b, inv_f, inv_b, do, carry=early if scatter_early else ())
    dqf, dkf, dvf, dgf, dqb, dkb, dvb, dgb = res[:8]
    if scatter_early:
        g_w_out, g_w_ffn_in4, g_w_ffn_out = res[8:]
    dpq, dpk, dpv, gcq, gck, gcv = _dn_prep_bwd(p, cw, (dqf, dqb), (dkf, dkb), (dvf, dvb))
    dgt, dprm = _gates_bwd(p, prm, dgf, dgb)
    dpb, grad_x, dsh1, dsc1, dcsh1, dcsc1, dn1w = _inproj_bwd(
        (dpq, dpk, dpv), dz, dp_na, dgt, w_in_p, x, ctx, sh1, sc1, csh1, csc1, norm1_w, dx1)
    g_w_in_p = _matmul_tn(dpb.reshape(NB * T, PC), hm.reshape(NB * T, D), PC, D, 512, "wgrad_in")
    g_rpb = _rpb_fold(dbias)

    dmod_x = jnp.concatenate([dsh1, dsc1, dg1, dsh2, dsc2, dg2], axis=2).reshape(NB, 6 * D)
    dmod_c = jnp.concatenate([dcsh1.reshape(D), dcsc1.reshape(D), jnp.zeros((4 * D,), F32)])
    g_conv = jnp.concatenate([gcq[:5], gck[:5], gcv[:5]], axis=1)
    small = dict(
        norm1_w=dn1w, norm2_w=dn2w, a_log=dprm[0, :8], dt_bias=dprm[1, :8],
        out_norm_w=donw[0].reshape(DNH, 128).sum(0), q_norm_w=dqw[0].reshape(NAH, 64).sum(0),
        k_norm_w=dkw[0].reshape(NAH, 64).sum(0), rpb=g_rpb, conv_w=g_conv)
    return loss8[0, 0], grad_x, dmod_x, dmod_c, small, g_w_in_p, g_w_out, g_w_ffn_in4, g_w_ffn_out


def _perm_w_in(wt):
    return jnp.concatenate([wt[:2048], wt[2064:3600], wt[2048:2064], jnp.zeros((PC - IN_COLS, wt.shape[1]), wt.dtype)],
                           axis=0)


def _unperm_w_in(gt):
    return jnp.concatenate([gt[:2048], gt[COL_G:COL_G + 16], gt[2048:COL_G]], axis=0)


def _misc_row(a_log, dt_bias, onw, qnw, knw):
    return jnp.concatenate([a_log.reshape(8), dt_bias.reshape(8), jnp.zeros((112,), F32), onw.reshape(128),
                            qnw.reshape(64), knw.reshape(64), jnp.zeros((D - 384,), F32)])


def _pack_small(c_ctx, n1, n2, b_ada, a_log, dt_bias, onw, qnw, knw, rpb, conv_s):
    misc = _misc_row(a_log, dt_bias, onw, qnw, knw)
    rp = jnp.concatenate([rpb.reshape(-1), jnp.zeros((4 * D - 3720,), F32)])
    cv = jnp.concatenate([conv_s.reshape(-1), jnp.zeros((2 * D - 1920,), F32)])
    return jnp.concatenate([c_ctx.reshape(-1), n1.reshape(-1), n2.reshape(-1), b_ada.reshape(-1), misc, rp, cv]).reshape(16, D)


def _unpack_small(a):
    f = a.reshape(-1)
    misc = f[9 * D:10 * D]
    return dict(
        c_ctx=f[0:D], norm1_w=f[D:2 * D].reshape(1, D), norm2_w=f[2 * D:3 * D].reshape(1, D),
        b_ada=f[3 * D:9 * D].reshape(1, 6 * D), dn_A_log=misc[0:8].reshape(1, 2, 4), dn_dt_bias=misc[8:16].reshape(1, 2, 4),
        dn_out_norm_w=misc[128:256].reshape(1, 128), na_q_norm_w=misc[256:320].reshape(1, 64),
        na_k_norm_w=misc[320:384].reshape(1, 64), na_rpb=f[10 * D:10 * D + 3720].reshape(1, 8, 15, 31),
        dn_conv_w=f[14 * D:14 * D + 1920].reshape(1, 5, 384))


_WEIGHTS = ['c_ctx', 'norm1_w', 'norm2_w', 'w_ada', 'b_ada', 'w_in', 'dn_conv_w', 'dn_A_log', 'dn_dt_bias',
            'dn_out_norm_w', 'na_q_norm_w', 'na_k_norm_w', 'na_rpb', 'w_out', 'w_ffn_in', 'w_ffn_out']


def kernel(x, c, ctx, c_ctx, norm1_w, norm2_w, w_ada, b_ada, w_in, dn_conv_w, dn_A_log, dn_dt_bias, dn_out_norm_w, na_q_norm_w, na_k_norm_w, na_rpb, w_out, w_ffn_in, w_ffn_out, loss_target, m_c_ctx, m_norm1_w, m_norm2_w, m_w_ada, m_b_ada, m_w_in, m_dn_conv_w, m_dn_A_log, m_dn_dt_bias, m_dn_out_norm_w, m_na_q_norm_w, m_na_k_norm_w, m_na_rpb, m_w_out, m_w_ffn_in, m_w_ffn_out, v_c_ctx, v_norm1_w, v_norm2_w, v_w_ada, v_b_ada, v_w_in, v_dn_conv_w, v_dn_A_log, v_dn_dt_bias, v_dn_out_norm_w, v_na_q_norm_w, v_na_k_norm_w, v_na_rpb, v_w_out, v_w_ffn_in, v_w_ffn_out):
    ix, iy, ic = lax.axis_index("x"), lax.axis_index("y"), lax.axis_index("c")
    chip = 2 * ix + iy
    me = 4 * ix + 2 * iy + ic

    crow = jnp.concatenate([c, c_ctx.reshape(1, D), jnp.zeros((5, D), F32)], axis=0)
    cvrow = jnp.concatenate([dn_conv_w[0].reshape(-1), jnp.zeros((2 * D - 1920,), F32)]).reshape(2, D)
    g0 = _allgather8(jnp.concatenate([crow, cvrow, jnp.zeros((6, D), F32)], axis=0), "gather_cond")
    cin = g0[:, :8].reshape(64, D)
    conv_full = jnp.concatenate([g0[2 * s, 8:10].reshape(-1)[:1920].reshape(5, 384) for s in range(4)], axis=1)

    w_ada_s = w_ada[0]
    b_s = lax.dynamic_slice(b_ada, (0, chip * 1536), (1, 1536))
    mod_s = _ada_fwd(cin, w_ada_s, b_s)
    mods = _gather_rows(mod_s, "gather_mod")
    mod_mine = jnp.concatenate([mods[0], mods[2], mods[4], mods[6]], axis=1)
    mod_x, mod_c = mod_mine[0:NB], mod_mine[2]

    def half(w2d):
        r = w2d.shape[0] // 2
        return lax.dynamic_slice(w2d, (ic * r, 0), (r, w2d.shape[1]))

    w_in_t = w_in[0].T
    (gw_in,) = _gather_weights([_cast_bf16(lax.dynamic_slice(w_in_t, (0, ic * (D // 2)), (900, D // 2)), "cast_w_in")])
    late = [_cast_bf16(half(w_out[0]), "cast_w_out"), _cast_bf16(half(w_ffn_in[0]), "cast_w_ffn_in"),
            _cast_bf16(half(w_ffn_out[0]), "cast_w_ffn_out")]
    w_in_p = _perm_w_in(gw_in.reshape(4, 2, 900, D // 2).transpose(0, 2, 1, 3).reshape(IN_COLS, D))

    (loss_part, grad_x, dmod_x, dmod_c, small, g_w_in_p, r_w_out, r_w_ffn_in, r_w_ffn_out) = _local_step(
        x, ctx, loss_target, mod_x, mod_c, norm1_w, norm2_w, w_in_p, conv_full, dn_A_log[0], dn_dt_bias[0],
        dn_out_norm_w[0], na_q_norm_w[0], na_k_norm_w[0], na_rpb[0], None, None, None,
        scatter_early=True, gather_late=late)
    loss = lax.psum(loss_part, ("x", "y", "c"))

    dm = jnp.concatenate([dmod_x, dmod_c.reshape(1, 6 * D), jnp.zeros((5, 6 * D), F32)], axis=0).reshape(48, D)
    misc = _misc_row(small["a_log"], small["dt_bias"], small["out_norm_w"], small["q_norm_w"], small["k_norm_w"])
    sm = jnp.concatenate([
        small["norm1_w"].reshape(-1), small["norm2_w"].reshape(-1), misc,
        small["rpb"].reshape(-1), jnp.zeros((4 * D - 3720,), F32),
        small["conv_w"].reshape(-1), jnp.zeros((8 * D - 7680,), F32), jnp.zeros((D,), F32)]).reshape(16, D)
    gsl, gtot = _allgather8(jnp.concatenate([dm, sm], axis=0), "gather_small", with_sum=True)
    gmod = gsl[:, :48].reshape(64, 6 * D)
    st = gtot[48:64].reshape(-1)
    g_n1, g_n2, g_misc = st[0:D].reshape(1, D), st[D:2 * D].reshape(1, D), st[2 * D:3 * D]
    g_rpb = st[3 * D:3 * D + 3720].reshape(1, 8, 15, 31)
    g_conv_full = st[7 * D:7 * D + 7680].reshape(5, 1536)
    g_conv_s = lax.dynamic_slice(g_conv_full, (0, chip * 384), (5, 384)).reshape(1, 5, 384)

    gmod_s = lax.dynamic_slice(gmod, (0, chip * 1536), (64, 1536))
    g_w_ada_s, pctx = _ada_bwd(cin, gmod_s, w_ada_s)
    parts = _allgather8(pctx, "gather_cctx")
    g_b_ada, g_c_ctx = _ada_small(gmod, parts, c_ctx.reshape(1, D))

    g_w_in8 = _unperm_w_in(g_w_in_p).reshape(4, 900, 2, D // 2).transpose(0, 2, 1, 3).reshape(8, 900, D // 2)
    names = ["w_in", "w_out", "w_ffn_in", "w_ffn_out"]
    rs = list(_scatter_grads([g_w_in8])) + [r_w_out, r_w_ffn_in, r_w_ffn_out]
    full = _sibling_exchange([_sum_slots(r, "sum_" + nm) for r, nm in zip(rs, names)])
    g_w_in_t = full[0].transpose(1, 0, 2).reshape(900, D)
    g_w_out_s = full[1].reshape(1, D // 4, D)
    g_w_ffn_in_s = full[2].reshape(1, D, FFT)
    g_w_ffn_out_s = full[3].reshape(1, DFF // 4, D)

    grads = dict(
        c_ctx=g_c_ctx.reshape(D), norm1_w=g_n1, norm2_w=g_n2, w_ada=g_w_ada_s.reshape(1, D, 1536), b_ada=g_b_ada,
        w_in=g_w_in_t, dn_conv_w=g_conv_s, dn_A_log=g_misc[0:8].reshape(1, 2, 4), dn_dt_bias=g_misc[8:16].reshape(1, 2, 4),
        dn_out_norm_w=g_misc[128:256].reshape(1, 128), na_q_norm_w=g_misc[256:320].reshape(1, 64),
        na_k_norm_w=g_misc[320:384].reshape(1, 64), na_rpb=g_rpb, w_out=g_w_out_s, w_ffn_in=g_w_ffn_in_s,
        w_ffn_out=g_w_ffn_out_s)
    pk = lambda d_: _pack_small(d_["c_ctx"], d_["norm1_w"], d_["norm2_w"], d_["b_ada"], d_["dn_A_log"], d_["dn_dt_bias"],
                                d_["dn_out_norm_w"], d_["na_q_norm_w"], d_["na_k_norm_w"], d_["na_rpb"], d_["dn_conv_w"])
    wts = dict(c_ctx=c_ctx, norm1_w=norm1_w, norm2_w=norm2_w, b_ada=b_ada, dn_A_log=dn_A_log, dn_dt_bias=dn_dt_bias,
               dn_out_norm_w=dn_out_norm_w, na_q_norm_w=na_q_norm_w, na_k_norm_w=na_k_norm_w, na_rpb=na_rpb, dn_conv_w=dn_conv_w)
    ms = dict(c_ctx=m_c_ctx, norm1_w=m_norm1_w, norm2_w=m_norm2_w, b_ada=m_b_ada, dn_A_log=m_dn_A_log, dn_dt_bias=m_dn_dt_bias,
              dn_out_norm_w=m_dn_out_norm_w, na_q_norm_w=m_na_q_norm_w, na_k_norm_w=m_na_k_norm_w, na_rpb=m_na_rpb,
              dn_conv_w=m_dn_conv_w)
    vs = dict(c_ctx=v_c_ctx, norm1_w=v_norm1_w, norm2_w=v_norm2_w, b_ada=v_b_ada, dn_A_log=v_dn_A_log, dn_dt_bias=v_dn_dt_bias,
              dn_out_norm_w=v_dn_out_norm_w, na_q_norm_w=v_na_q_norm_w, na_k_norm_w=v_na_k_norm_w, na_rpb=v_na_rpb,
              dn_conv_w=v_dn_conv_w)
    sd, snm, snv, _ = _adamw(pk(wts), pk(grads), pk(ms), pk(vs), "adamw_small")
    delta, new_m, new_v = _unpack_small(sd), _unpack_small(snm), _unpack_small(snv)
    big = dict(w_ada=(w_ada, m_w_ada, v_w_ada), w_in=(w_in, m_w_in, v_w_in), w_out=(w_out, m_w_out, v_w_out),
               w_ffn_in=(w_ffn_in, m_w_ffn_in, v_w_ffn_in), w_ffn_out=(w_ffn_out, m_w_ffn_out, v_w_ffn_out))
    for nm, (w_, m_, v_) in big.items():
        shp = w_.shape
        if nm == "w_in":
            r2, back = (lambda a: a[0].T), (lambda a: a.T.reshape(shp))
            res = _adamw(w_in_t, grads[nm], r2(m_), r2(v_), "adamw_" + nm)
        else:
            r2, back = (lambda a: a.reshape(shp[1], shp[2])), (lambda a: a.reshape(shp))
            res = _adamw(r2(w_), r2(grads[nm]), r2(m_), r2(v_), "adamw_" + nm)
        delta[nm], new_m[nm], new_v[nm], grads[nm] = [back(a) for a in res]

    return (loss, grad_x, *[grads[n] for n in _WEIGHTS], *[delta[n] for n in _WEIGHTS],
            *[new_m[n] for n in _WEIGHTS], *[new_v[n] for n in _WEIGHTS])
```

```python
import functools

import numpy as np
import jax
import jax.numpy as jnp
from jax import lax
from jax.experimental import pallas as pl
from jax.experimental.pallas import tpu as pltpu

F32 = jnp.float32
BF16 = jnp.bfloat16

D = 1024
NB = 2
SEQ = 2048
CT = 256
T = CT + SEQ
NTOK = NB * SEQ
CH = 64
NCH = T // CH
NCH_CTX = CT // CH
DNH, DNW = 4, 512
NAH, NAW = 8, 512
GRID_W = 64
ROWS = SEQ // GRID_W
WIN_ROWS = 8
WIN_COLS = 16
NWIN = WIN_ROWS * GRID_W
DFF = 2816
FFT = 1408
IN_COLS = 3600
PC = 3712
COL_Z, COL_NA, COL_G = 1536, 2048, 3584
EPS = 1e-6
NEG = -1e30
RT = 256
VMEM_BIG = 56 * 1024 * 1024
MESH = pl.DeviceIdType.MESH

ADAM_LR, ADAM_B1, ADAM_B2, ADAM_EPS, ADAM_WD, ADAM_STEP = 0.001, 0.9, 0.999, 1e-08, 0.01, 10


def _cp(vmem=None, sem=None):
    kw = {}
    if vmem is not None:
        kw["vmem_limit_bytes"] = vmem
    if sem is not None:
        kw["dimension_semantics"] = sem
    return pltpu.CompilerParams(**kw)


_NN = (((1,), (0,)), ((), ()))
_NT = (((1,), (1,)), ((), ()))
_TN = (((0,), (0,)), ((), ()))


def _dg(a, b, dims):
    return lax.dot_general(a.astype(BF16), b.astype(BF16), dims, preferred_element_type=F32)


@jax.custom_vjp
def _mm(a, b):
    return _dg(a, b, _NN)


_mm.defvjp(lambda a, b: (_dg(a, b, _NN), (a, b)),
           lambda r, g: (_dg(g, r[1], _NT), _dg(r[0], g, _TN)))


@jax.custom_vjp
def _mm_nt(a, b):
    return _dg(a, b, _NT)


_mm_nt.defvjp(lambda a, b: (_dg(a, b, _NT), (a, b)),
              lambda r, g: (_dg(g, r[1], _NN), _dg(g, r[0], _TN)))


@jax.custom_vjp
def _mm_tn(a, b):
    return _dg(a, b, _TN)


_mm_tn.defvjp(lambda a, b: (_dg(a, b, _TN), (a, b)),
              lambda r, g: (_dg(r[1], g, _NT), _dg(r[0], g, _NN)))


def _split_dot(a, b, dims):
    hi = b.astype(BF16)
    lo = (b - hi.astype(F32)).astype(BF16)
    ab = a.astype(BF16)
    return (lax.dot_general(ab, hi, dims, preferred_element_type=F32)
            + lax.dot_general(ab, lo, dims, preferred_element_type=F32))


@jax.custom_vjp
def _mask_mm(mask, b):
    return _split_dot(mask, b, _NN)


_mask_mm.defvjp(lambda mask, b: (_split_dot(mask, b, _NN), mask),
                lambda mask, g: (jnp.zeros_like(mask), _split_dot(mask, g, _TN)))


def _hmm(a, b):
    return jnp.dot(a, b, precision=lax.Precision.HIGHEST, preferred_element_type=F32)


def _sigmoid(x):
    return 1.0 / (1.0 + jnp.exp(-x))


def _silu(x):
    return x * _sigmoid(x)


def _norm_mod(x, w, sh, sc):
    r = lax.rsqrt(jnp.mean(x * x, axis=-1, keepdims=True) + EPS)
    return (x * r * w) * (1.0 + sc) + sh


def _my_pos():
    return lax.axis_index("x"), lax.axis_index("y"), lax.axis_index("c")


def _flip(pos, k):
    x, y, c = pos
    return (1 - x if k & 4 else x, 1 - y if k & 2 else y, 1 - c if k & 1 else c)


def _slot(pos):
    return 4 * pos[0] + 2 * pos[1] + pos[2]


def _allgather8(v, name, with_sum=False):
    R, W = v.shape

    def body(v_ref, out_ref, *rest):
        if with_sum:
            tot_ref, send_sems, recv_sems, lsem = rest
        else:
            send_sems, recv_sems, lsem = rest
        me = _my_pos()
        mine = pltpu.make_async_copy(v_ref, out_ref.at[_slot(me)], lsem)
        mine.start()
        sends = []
        for k in range(1, 8):
            cp = pltpu.make_async_remote_copy(
                src_ref=v_ref, dst_ref=out_ref.at[_slot(me)], send_sem=send_sems.at[k - 1],
                recv_sem=recv_sems.at[k - 1], device_id=_flip(me, k), device_id_type=MESH)
            cp.start()
            sends.append(cp)
        for k in range(1, 8):
            peer = _flip(me, k)
            pltpu.make_async_remote_copy(
                src_ref=v_ref, dst_ref=out_ref.at[_slot(peer)], send_sem=send_sems.at[k - 1],
                recv_sem=recv_sems.at[k - 1], device_id=peer, device_id_type=MESH).wait_recv()
        for cp in sends:
            cp.wait_send()
        mine.wait()
        if with_sum:
            acc = out_ref[0]
            for s in range(1, 8):
                acc = acc + out_ref[s]
            tot_ref[...] = acc

    out_shape = [jax.ShapeDtypeStruct((8, R, W), F32)]
    out_specs = [pl.BlockSpec(memory_space=pltpu.VMEM)]
    if with_sum:
        out_shape.append(jax.ShapeDtypeStruct((R, W), F32))
        out_specs.append(pl.BlockSpec(memory_space=pltpu.VMEM))
    res = pl.pallas_call(
        body, name=name, out_shape=out_shape,
        in_specs=[pl.BlockSpec(memory_space=pltpu.VMEM)], out_specs=out_specs,
        scratch_shapes=[pltpu.SemaphoreType.DMA((7,)), pltpu.SemaphoreType.DMA((7,)), pltpu.SemaphoreType.DMA],
        compiler_params=_cp(vmem=VMEM_BIG),
    )(v)
    return res if with_sum else res[0]


def _gather_rows(v, name):
    W = v.shape[1]

    def body(v_ref, out_ref, send_sems, recv_sems, lsem):
        pos = _my_pos()
        me = _slot(pos)
        rows = lambda p: v_ref.at[pl.ds(pl.multiple_of(8 * p, 8), 8)]
        locals_ = [(rows(me), out_ref.at[me])]
        xfers = [(rows(_slot(_flip(pos, k))), out_ref.at[me], out_ref.at[_slot(_flip(pos, k))], _flip(pos, k))
                 for k in range(1, 8)]
        _xfer_start((locals_, xfers), (send_sems, recv_sems, lsem))
        _xfer_wait((locals_, xfers), (send_sems, recv_sems, lsem))

    vm = pl.BlockSpec(memory_space=pltpu.VMEM)
    return pl.pallas_call(
        body, name=name, out_shape=jax.ShapeDtypeStruct((8, 8, W), F32), in_specs=[vm], out_specs=vm,
        scratch_shapes=_xfer_sems(1, 7),
    )(v)


def _gather_weights(halves):
    n = len(halves)

    def body(*refs):
        ins, outs = refs[:n], refs[n:2 * n]
        send_sems, recv_sems, lsems = refs[2 * n:]
        me = _my_pos()
        x, y, c = me
        sib = (x, y, 1 - c)
        chips = [4, 2, 6]

        def cp(a, k, block, to, src=None):
            dst = outs[a].at[_slot(block)]
            return pltpu.make_async_remote_copy(
                src_ref=dst if src is None else src, dst_ref=dst, send_sem=send_sems.at[a, k],
                recv_sem=recv_sems.at[a, k], device_id=to, device_id_type=MESH)

        mine, first, passed = [], [], []
        for a in range(n):
            m = pltpu.make_async_copy(ins[a], outs[a].at[_slot(me)], lsems.at[a])
            m.start()
            mine.append(m)
            first.append(cp(a, 0, me, sib, src=ins[a]))
            for j, k in enumerate(chips):
                first.append(cp(a, 1 + j, me, _flip(me, k), src=ins[a]))
        for f in first:
            f.start()
        for j, k in enumerate(chips):
            for a in range(n):
                cp(a, 1 + j, _flip(me, k), me).wait_recv()
                p = cp(a, 4 + j, _flip(me, k), sib)
                p.start()
                passed.append(p)
        for a in range(n):
            cp(a, 0, sib, me).wait_recv()
            for j, k in enumerate(chips):
                cp(a, 4 + j, _flip(sib, k), me).wait_recv()
        for f in first + passed:
            f.wait_send()
        for m in mine:
            m.wait()

    vm = pl.BlockSpec(memory_space=pltpu.VMEM)
    return pl.pallas_call(
        body, name="gather_weights",
        out_shape=[jax.ShapeDtypeStruct((8,) + h.shape, h.dtype) for h in halves],
        in_specs=[vm] * n, out_specs=[vm] * n,
        scratch_shapes=[pltpu.SemaphoreType.DMA((n, 7)), pltpu.SemaphoreType.DMA((n, 7)),
                        pltpu.SemaphoreType.DMA((n,))],
        compiler_params=_cp(vmem=VMEM_BIG),
    )(*halves)


def _xfer_start(plan, sems):
    send_sems, recv_sems, lsems = sems
    locals_, xfers = plan
    for i, (src, dst) in enumerate(locals_):
        pltpu.make_async_copy(src, dst, lsems.at[i]).start()
    for i, (src, dst, _, peer) in enumerate(xfers):
        pltpu.make_async_remote_copy(src_ref=src, dst_ref=dst, send_sem=send_sems.at[i], recv_sem=recv_sems.at[i],
                                     device_id=peer, device_id_type=MESH).start()


def _xfer_wait(plan, sems):
    send_sems, recv_sems, lsems = sems
    locals_, xfers = plan
    for i, (src, _, landing, peer) in enumerate(xfers):
        pltpu.make_async_remote_copy(src_ref=src, dst_ref=landing, send_sem=send_sems.at[i], recv_sem=recv_sems.at[i],
                                     device_id=peer, device_id_type=MESH).wait_recv()
    for i, (src, dst, _, peer) in enumerate(xfers):
        pltpu.make_async_remote_copy(src_ref=src, dst_ref=dst, send_sem=send_sems.at[i], recv_sem=recv_sems.at[i],
                                     device_id=peer, device_id_type=MESH).wait_send()
    for i, (src, dst) in enumerate(locals_):
        pltpu.make_async_copy(src, dst, lsems.at[i]).wait()


def _xfer_sems(n_local, n_xfer):
    return [pltpu.SemaphoreType.DMA((n_xfer,)), pltpu.SemaphoreType.DMA((n_xfer,)), pltpu.SemaphoreType.DMA((n_local,))]


def _scatter_plan(ins, outs, pos):
    me = _slot(pos)
    locals_, xfers = [], []
    for src, dst in zip(ins, outs):
        locals_.append((src.at[me], dst.at[me]))
        for k in range(1, 8):
            peer = _flip(pos, k)
            xfers.append((src.at[_slot(peer)], dst.at[me], dst.at[_slot(peer)], peer))
    return locals_, xfers


def _scatter_grads(grads):
    n = len(grads)

    def body(*refs):
        plan = _scatter_plan(refs[:n], refs[n:2 * n], _my_pos())
        _xfer_start(plan, refs[2 * n:])
        _xfer_wait(plan, refs[2 * n:])

    hbm = pl.BlockSpec(memory_space=pltpu.HBM)
    return pl.pallas_call(
        body, name="scatter_grads", out_shape=[jax.ShapeDtypeStruct(g.shape, g.dtype) for g in grads],
        in_specs=[hbm] * n, out_specs=[hbm] * n, scratch_shapes=_xfer_sems(n, 7 * n),
    )(*grads)


def _sibling_exchange(halves):
    n = len(halves)

    def body(*refs):
        ins, outs = refs[:n], refs[n:2 * n]
        send_sems, recv_sems, lsems = refs[2 * n:]
        x, y, c = _my_pos()
        sib = (x, y, 1 - c)
        mine, sends = [], []
        for a in range(n):
            m = pltpu.make_async_copy(ins[a], outs[a].at[c], lsems.at[a])
            m.start()
            mine.append(m)
            cp = pltpu.make_async_remote_copy(
                src_ref=ins[a], dst_ref=outs[a].at[c], send_sem=send_sems.at[a], recv_sem=recv_sems.at[a],
                device_id=sib, device_id_type=MESH)
            cp.start()
            sends.append(cp)
        for a in range(n):
            pltpu.make_async_remote_copy(
                src_ref=ins[a], dst_ref=outs[a].at[1 - c], send_sem=send_sems.at[a], recv_sem=recv_sems.at[a],
                device_id=sib, device_id_type=MESH).wait_recv()
        for cp in sends:
            cp.wait_send()
        for m in mine:
            m.wait()

    vm = pl.BlockSpec(memory_space=pltpu.VMEM)
    return pl.pallas_call(
        body, name="sibling_exchange",
        out_shape=[jax.ShapeDtypeStruct((2,) + h.shape, h.dtype) for h in halves],
        in_specs=[vm] * n, out_specs=[vm] * n,
        scratch_shapes=[pltpu.SemaphoreType.DMA((n,)), pltpu.SemaphoreType.DMA((n,)), pltpu.SemaphoreType.DMA((n,))],
        compiler_params=_cp(vmem=VMEM_BIG),
    )(*halves)


def _row_tile(r, cap=512):
    for t in (cap, 256, 128, 64, 32, 16, 8):
        if t <= cap and r % t == 0:
            return t
    return r


def _cast_bf16(a, name):
    R, W = a.shape
    tr = _row_tile(R, 256)

    def body(a_ref, o_ref):
        o_ref[...] = a_ref[...].astype(BF16)

    return pl.pallas_call(
        body, name=name, grid=(R // tr,), out_shape=jax.ShapeDtypeStruct((R, W), BF16),
        in_specs=[pl.BlockSpec((tr, W), lambda i: (i, 0))], out_specs=pl.BlockSpec((tr, W), lambda i: (i, 0)),
    )(a)


def _sum_slots(r, name):
    _, R, W = r.shape
    tr = _row_tile(R, 128)

    def body(r_ref, o_ref):
        acc = r_ref[0].astype(F32)
        for s_ in range(1, 8):
            acc = acc + r_ref[s_].astype(F32)
        o_ref[...] = acc.astype(BF16)

    return pl.pallas_call(
        body, name=name, grid=(R // tr,), out_shape=jax.ShapeDtypeStruct((R, W), BF16),
        in_specs=[pl.BlockSpec((8, tr, W), lambda i: (0, i, 0))], out_specs=pl.BlockSpec((tr, W), lambda i: (i, 0)),
        compiler_params=_cp(vmem=VMEM_BIG),
    )(r)


def _adamw(w, g, m, v, name):
    R, W = w.shape
    tr = _row_tile(R, 256)
    tc = 256 if (tr == R and R * W * 4 > (1 << 21) and W % 256 == 0) else W
    c1 = 1.0 / (1.0 - ADAM_B1 ** ADAM_STEP)
    c2 = 1.0 / (1.0 - ADAM_B2 ** ADAM_STEP)

    def body(w_ref, g_ref, m_ref, v_ref, d_ref, nm_ref, nv_ref, g32_ref):
        gg = g_ref[...].astype(F32)
        g32_ref[...] = gg
        nm = ADAM_B1 * m_ref[...] + (1.0 - ADAM_B1) * gg
        nv = ADAM_B2 * v_ref[...] + (1.0 - ADAM_B2) * (gg * gg)
        d_ref[...] = -ADAM_LR * ((nm * c1) / (jnp.sqrt(nv * c2) + ADAM_EPS) + ADAM_WD * w_ref[...])
        nm_ref[...] = nm
        nv_ref[...] = nv

    spec = pl.BlockSpec((tr, tc), lambda i, j: (i, j))
    return pl.pallas_call(
        body, name=name, grid=(R // tr, W // tc), out_shape=[jax.ShapeDtypeStruct((R, W), F32)] * 4,
        in_specs=[spec] * 4, out_specs=[spec] * 4, compiler_params=_cp(vmem=VMEM_BIG),
    )(w, g, m, v)


def _matmul_tn(a, b, tm, tn, tk, name, nn=1, b_map=None, out_shape=None, out_block=None, out_map=None,
               b_block=None):
    K, M = a.shape
    nk = K // tk
    nm = M // tm

    def body(a_ref, b_ref, o_ref, acc_ref):
        kk = pl.program_id(2)

        @pl.when(kk == 0)
        def _():
            acc_ref[...] = jnp.zeros_like(acc_ref)

        acc_ref[...] += lax.dot_general(a_ref[...], b_ref[...], _TN, preferred_element_type=F32)

        @pl.when(kk == nk - 1)
        def _():
            o_ref[...] = acc_ref[...].astype(BF16)

    return pl.pallas_call(
        body, name=name, grid=(nm, nn, nk),
        out_shape=jax.ShapeDtypeStruct(out_shape or (M, b.shape[-1]), BF16),
        in_specs=[pl.BlockSpec((tk, tm), lambda i, j, k: (k, i)),
                  pl.BlockSpec(b_block or (tk, tn), b_map or (lambda i, j, k: (k, j)))],
        out_specs=pl.BlockSpec(out_block or (tm, tn), out_map or (lambda i, j, k: (i, j))),
        scratch_shapes=[pltpu.VMEM((tm, tn), F32)],
        compiler_params=_cp(vmem=VMEM_BIG),
    )(a, b)


def _ada_fwd(cin, w_ada_s, b_s):
    tn = 512

    def body(c_ref, w_ref, b_ref, o_ref):
        o_ref[...] = _dg(_silu(c_ref[...]), w_ref[...], _NN) + b_ref[...]

    return pl.pallas_call(
        body, name="ada_fwd", grid=(1536 // tn,), out_shape=jax.ShapeDtypeStruct((64, 1536), F32),
        in_specs=[pl.BlockSpec((64, D), lambda j: (0, 0)), pl.BlockSpec((D, tn), lambda j: (0, j)),
                  pl.BlockSpec((1, tn), lambda j: (0, j))],
        out_specs=pl.BlockSpec((64, tn), lambda j: (0, j)),
        compiler_params=_cp(vmem=VMEM_BIG),
    )(cin, w_ada_s, b_s)


def _ada_bwd(cin, gmod_s, w_ada_s):
    tn = 512

    def body(c_ref, g_ref, w_ref, gw_ref, pc_ref):
        j = pl.program_id(0)
        g = g_ref[...]
        gw_ref[...] = _dg(_silu(c_ref[...]), g, _TN)
        rr = lax.broadcasted_iota(jnp.int32, (8, 64), 0)
        cc = lax.broadcasted_iota(jnp.int32, (8, 64), 1)
        sel = jnp.where((rr == 0) & ((cc & 7) == 2), 1.0, 0.0).astype(F32)
        dctx = _hmm(sel, g)
        part = _dg(dctx, w_ref[...], _NT)

        @pl.when(j == 0)
        def _():
            pc_ref[...] = jnp.zeros_like(pc_ref)

        pc_ref[...] += part

    return pl.pallas_call(
        body, name="ada_bwd", grid=(1536 // tn,),
        out_shape=[jax.ShapeDtypeStruct((D, 1536), F32), jax.ShapeDtypeStruct((8, D), F32)],
        in_specs=[pl.BlockSpec((64, D), lambda j: (0, 0)), pl.BlockSpec((64, tn), lambda j: (0, j)),
                  pl.BlockSpec((D, tn), lambda j: (0, j))],
        out_specs=[pl.BlockSpec((D, tn), lambda j: (0, j)), pl.BlockSpec((8, D), lambda j: (0, 0))],
        compiler_params=_cp(vmem=VMEM_BIG),
    )(cin, gmod_s, w_ada_s)


def _ada_small(gmod, parts, c_ctx):
    def body(g_ref, p_ref, c_ref, gb_ref, gc_ref):
        gb_ref[...] = jnp.sum(g_ref[...], axis=0, keepdims=True)
        ds = p_ref[0][0:1] + p_ref[2][0:1] + p_ref[4][0:1] + p_ref[6][0:1]
        cc = c_ref[...]
        s = _sigmoid(cc)
        gc_ref[...] = ds * (s * (1.0 + cc * (1.0 - s)))

    vm = pl.BlockSpec(memory_space=pltpu.VMEM)
    return pl.pallas_call(
        body, name="ada_small",
        out_shape=[jax.ShapeDtypeStruct((1, 6 * D), F32), jax.ShapeDtypeStruct((1, D), F32)],
        in_specs=[vm, vm, vm], out_specs=[vm, vm],
    )(gmod, parts, c_ctx)


def _tok_specs():
    xs = pl.BlockSpec((None, RT, D), lambda b, j: (b, jnp.maximum(j - 1, 0), 0))
    cs = pl.BlockSpec((None, RT, D), lambda b, j: (b, 0, 0))
    ms = pl.BlockSpec((None, 1, D), lambda b, j: (b, 0, 0))
    cms = pl.BlockSpec((None, 1, D), lambda b, j: (0, 0, 0))
    return xs, cs, ms, cms


def _inproj_fwd(x, ctx, sh1, sc1, csh1, csc1, n1w, w_in_p):
    def body(x_ref, c_ref, sh_ref, sc_ref, csh_ref, csc_ref, w1_ref, w_ref, p_ref, hm_ref):
        is_ctx = pl.program_id(1) == 0
        xt = jnp.where(is_ctx, c_ref[...], x_ref[...])
        sh = jnp.where(is_ctx, csh_ref[...], sh_ref[...])
        sc = jnp.where(is_ctx, csc_ref[...], sc_ref[...])
        hb = _norm_mod(xt, w1_ref[...], sh, sc).astype(BF16)
        hm_ref[...] = hb
        p_ref[...] = lax.dot_general(hb, w_ref[...], _NT, preferred_element_type=F32)

    xs, cs, ms, cms = _tok_specs()
    return pl.pallas_call(
        body, name="inproj_fwd", grid=(NB, T // RT),
        out_shape=[jax.ShapeDtypeStruct((NB, T, PC), F32), jax.ShapeDtypeStruct((NB, T, D), BF16)],
        in_specs=[xs, cs, ms, ms, cms, cms, pl.BlockSpec((1, D), lambda b, j: (0, 0)),
                  pl.BlockSpec((PC, D), lambda b, j: (0, 0))],
        out_specs=[pl.BlockSpec((None, RT, PC), lambda b, j: (b, j, 0)),
                   pl.BlockSpec((None, RT, D), lambda b, j: (b, j, 0))],
        compiler_params=_cp(vmem=VMEM_BIG),
    )(x, ctx, sh1, sc1, csh1, csc1, n1w, w_in_p)


def _inproj_bwd(dp_dn, dz, dp_na, dgt, w_in_p, x, ctx, sh1, sc1, csh1, csc1, n1w, dx1):
    def body(d1q_ref, d1k_ref, d1v_ref, d2_ref, d3_ref, d4_ref, w_ref, x_ref, c_ref, sh_ref, sc_ref, csh_ref, csc_ref,
             w1_ref, dx1_ref, dpb_ref, gx_ref, dsh_ref, dsc_ref, dcsh_ref, dcsc_ref, dw1_ref):
        b, j = pl.program_id(0), pl.program_id(1)
        is_ctx = j == 0
        dp = jnp.concatenate([d1q_ref[...], d1k_ref[...], d1v_ref[...], d2_ref[...], d3_ref[...], d4_ref[...]],
                             axis=1).astype(BF16)
        dpb_ref[...] = dp
        dhm = jnp.dot(dp, w_ref[...], preferred_element_type=F32)
        xt = jnp.where(is_ctx, c_ref[...], x_ref[...])
        sh = jnp.where(is_ctx, csh_ref[...], sh_ref[...])
        sc = jnp.where(is_ctx, csc_ref[...], sc_ref[...])
        _, vjp = jax.vjp(_norm_mod, xt, w1_ref[...], sh, sc)
        dxt, dw1, dsh, dsc = vjp(dhm)

        @pl.when((b == 0) & (j == 0))
        def _():
            dw1_ref[...] = jnp.zeros_like(dw1_ref)
            dcsh_ref[...] = jnp.zeros_like(dcsh_ref)
            dcsc_ref[...] = jnp.zeros_like(dcsc_ref)

        dw1_ref[...] += dw1

        @pl.when(is_ctx)
        def _():
            dcsh_ref[...] += dsh
            dcsc_ref[...] += dsc

        @pl.when(j == 1)
        def _():
            dsh_ref[...] = jnp.zeros_like(dsh_ref)
            dsc_ref[...] = jnp.zeros_like(dsc_ref)

        @pl.when(j >= 1)
        def _():
            dsh_ref[...] += dsh
            dsc_ref[...] += dsc
            gx_ref[...] = dx1_ref[...] + dxt

    xs, cs, ms, cms = _tok_specs()
    tsp = lambda w: pl.BlockSpec((None, RT, w), lambda b, j: (b, j, 0))
    one = pl.BlockSpec((1, D), lambda b, j: (0, 0))
    return pl.pallas_call(
        body, name="inproj_bwd", grid=(NB, T // RT),
        out_shape=[jax.ShapeDtypeStruct((NB, T, PC), BF16), jax.ShapeDtypeStruct((NB, SEQ, D), F32),
                   jax.ShapeDtypeStruct((NB, 1, D), F32), jax.ShapeDtypeStruct((NB, 1, D), F32),
                   jax.ShapeDtypeStruct((1, 1, D), F32), jax.ShapeDtypeStruct((1, 1, D), F32),
                   jax.ShapeDtypeStruct((1, D), F32)],
        in_specs=[tsp(512), tsp(512), tsp(512), tsp(512), tsp(1536), tsp(128), pl.BlockSpec((PC, D), lambda b, j: (0, 0)),
                  xs, cs, ms, ms, cms, cms, one, _rows2d_spec(D)],
        out_specs=[tsp(PC), xs, ms, ms, cms, cms, one],
        compiler_params=_cp(vmem=VMEM_BIG),
    )(*dp_dn, dz, dp_na, dgt, w_in_p, x, ctx, sh1, sc1, csh1, csc1, n1w, dx1)


def _shift_masks():
    t = lax.broadcasted_iota(jnp.int32, (T, 1), 0)
    lo = jnp.where(t < CT, 0, CT)
    hi = jnp.where(t < CT, CT, T)
    return {off: (t + off >= lo) & (t + off < hi) for off in range(-2, 3)}


def _shifted(u, off, masks):
    s = u if off == 0 else pltpu.roll(u, (-off) % T, 0)
    return jnp.where(masks[off], s, 0.0)


def _conv_pre(u, w, masks):
    acc = _shifted(u, -2, masks) * w[0:1, :]
    for kk in range(1, 5):
        acc = acc + _shifted(u, kk - 2, masks) * w[kk:kk + 1, :]
    return acc


def _act_q(pre):
    a = _silu(pre)
    return a * lax.rsqrt(jnp.sum(a * a, axis=-1, keepdims=True) + EPS) * (128.0 ** -0.5)


def _act_k(pre):
    a = _silu(pre)
    return a * lax.rsqrt(jnp.sum(a * a, axis=-1, keepdims=True) + EPS)


def _dn_prep_fwd(p, cw):
    def body(pq_ref, pk_ref, pv_ref, wq_ref, wk_ref, wv_ref, q_ref, k_ref, v_ref):
        masks = _shift_masks()
        q_ref[...] = _act_q(_conv_pre(pq_ref[...], wq_ref[...], masks))
        k_ref[...] = _act_k(_conv_pre(pk_ref[...], wk_ref[...], masks))
        v_ref[...] = _silu(_conv_pre(pv_ref[...], wv_ref[...], masks))

    ps = lambda o: pl.BlockSpec((None, T, 128), lambda b, h: (b, 0, o + h))
    ws = lambda o: pl.BlockSpec((8, 128), lambda b, h: (0, o + h))
    os_ = pl.BlockSpec((None, T, 128), lambda b, h: (b, 0, h))
    return pl.pallas_call(
        body, name="dn_prep_fwd", grid=(NB, DNH), out_shape=[jax.ShapeDtypeStruct((NB, T, DNW), F32)] * 3,
        in_specs=[ps(0), ps(4), ps(8), ws(0), ws(4), ws(8)], out_specs=[os_] * 3,
        compiler_params=_cp(vmem=VMEM_BIG),
    )(p, p, p, cw, cw, cw)


def _dn_prep_bwd(p, cw, dqs, dks, dvs):
    def body(pq_ref, pk_ref, pv_ref, wq_ref, wk_ref, wv_ref, dqf_ref, dqb_ref, dkf_ref, dkb_ref, dvf_ref, dvb_ref,
             oq_ref, ok_ref, ov_ref, gq_ref, gk_ref, gv_ref):
        b = pl.program_id(1)
        masks = _shift_masks()

        def one(p_ref, w_ref, df_ref, db_ref, act, o_ref, g_ref):
            u = p_ref[...]
            w = w_ref[...]
            _, vjp = jax.vjp(act, _conv_pre(u, w, masks))
            (dpre,) = vjp(df_ref[...] + db_ref[...])
            du = _shifted(dpre, 2, masks) * w[0:1, :]
            rows = [jnp.sum(dpre * _shifted(u, -2, masks), axis=0, keepdims=True)]
            for kk in range(1, 5):
                du = du + _shifted(dpre, 2 - kk, masks) * w[kk:kk + 1, :]
                rows.append(jnp.sum(dpre * _shifted(u, kk - 2, masks), axis=0, keepdims=True))
            o_ref[...] = du
            gw = jnp.concatenate(rows + [jnp.zeros((3, 128), F32)], axis=0)

            @pl.when(b == 0)
            def _():
                g_ref[...] = gw

            @pl.when(b > 0)
            def _():
                g_ref[...] += gw

        one(pq_ref, wq_ref, dqf_ref, dqb_ref, _act_q, oq_ref, gq_ref)
        one(pk_ref, wk_ref, dkf_ref, dkb_ref, _act_k, ok_ref, gk_ref)
        one(pv_ref, wv_ref, dvf_ref, dvb_ref, _silu, ov_ref, gv_ref)

    ps = lambda o: pl.BlockSpec((None, T, 128), lambda h, b: (b, 0, o + h))
    ws = lambda o: pl.BlockSpec((8, 128), lambda h, b: (0, o + h))
    ds = pl.BlockSpec((None, T, 128), lambda h, b: (b, 0, h))
    outs = pl.pallas_call(
        body, name="dn_prep_bwd", grid=(DNH, NB),
        out_shape=[jax.ShapeDtypeStruct((NB, T, DNW), F32)] * 3 + [jax.ShapeDtypeStruct((8, DNW), F32)] * 3,
        in_specs=[ps(0), ps(4), ps(8), ws(0), ws(4), ws(8)] + [ds] * 6,
        out_specs=[ds] * 3
        + [pl.BlockSpec((8, 128), lambda h, b: (0, h))] * 3,
        compiler_params=_cp(vmem=VMEM_BIG),
    )(p, p, p, cw, cw, cw, *dqs, *dks, *dvs)
    return outs


def _gate_fn(a, prm):
    lane = lax.broadcasted_iota(jnp.int32, a.shape, 1)
    z = a + prm[1:2, :]
    sp = jnp.maximum(z, 0.0) + jnp.log(1.0 + jnp.exp(-jnp.abs(z)))
    g = -jnp.exp(prm[0:1, :]) * sp
    return jnp.where(lane < 8, g, jnp.where(lane < 16, _sigmoid(a), 0.0))


def _gates_fwd(p, prm):
    def body(a_ref, prm_ref, o_ref):
        o_ref[...] = _gate_fn(a_ref[...], prm_ref[...])

    return pl.pallas_call(
        body, name="gates_fwd", grid=(NB,), out_shape=jax.ShapeDtypeStruct((NB, T, 128), F32),
        in_specs=[pl.BlockSpec((None, T, 128), lambda b: (b, 0, COL_G // 128)), pl.BlockSpec((8, 128), lambda b: (0, 0))],
        out_specs=pl.BlockSpec((None, T, 128), lambda b: (b, 0, 0)),
    )(p, prm)


def _gates_bwd(p, prm, dgf, dgb):
    def body(a_ref, prm_ref, dgf_ref, dgb_ref, o_ref, dprm_ref):
        b = pl.program_id(0)
        tot = dgf_ref[...] + dgb_ref[...]
        _, vjp = jax.vjp(_gate_fn, a_ref[...], prm_ref[...])
        da, dprm = vjp(tot)
        o_ref[...] = da

        @pl.when(b == 0)
        def _():
            dprm_ref[...] = dprm

        @pl.when(b > 0)
        def _():
            dprm_ref[...] += dprm

    return pl.pallas_call(
        body, name="gates_bwd", grid=(NB,),
        out_shape=[jax.ShapeDtypeStruct((NB, T, 128), F32), jax.ShapeDtypeStruct((8, 128), F32)],
        in_specs=[pl.BlockSpec((None, T, 128), lambda b: (b, 0, COL_G // 128)), pl.BlockSpec((8, 128), lambda b: (0, 0)),
                  pl.BlockSpec((None, T, 128), lambda b: (b, 0, 0)), pl.BlockSpec((None, T, 128), lambda b: (b, 0, 0))],
        out_specs=[pl.BlockSpec((None, T, 128), lambda b: (b, 0, 0)), pl.BlockSpec((8, 128), lambda b: (0, 0))],
        compiler_params=_cp(vmem=VMEM_BIG),
    )(p, prm, dgf, dgb)


def _lockstep(gens, joint=None):
    n = len(gens)
    res, vals = [None] * n, [None] * n
    while True:
        msgs = []
        for i, g in enumerate(gens):
            try:
                msgs.append((g.send(vals[i]), False))
            except StopIteration as e:
                res[i] = e.value
                msgs.append((None, True))
        if msgs[0][1]:
            return res
        if msgs[0][0] is not None:
            vals = list(joint(tuple(m[0] for m in msgs)))
        else:
            vals = [None] * n


def _tri_inv_gen(m, md, m1, m2, eye):
    n = -(m * md)
    t = eye + n
    pw = n
    for _ in range(3):
        pw = _dg(pw, pw, _NN)
        yield
        t = t + _dg(t, pw, _NN)
        yield
    a = _dg(t, m * m1, _NN)
    yield
    t = t - _dg(a, t, _NN)
    yield
    a = _dg(t, m * m2, _NN)
    yield
    return t - _dg(a, t, _NN)


def _tri_inv_bwd_gen(t, g):
    tt = t.T
    a = _dg(tt, g, _NN)
    yield
    return -_dg(a, tt, _NN)


@jax.custom_vjp
def _tri_inv(ms, md, m1, m2, eye):
    return tuple(_lockstep([_tri_inv_gen(m, md, m1, m2, eye) for m in ms]))


def _tri_inv_f(ms, md, m1, m2, eye):
    ts = _tri_inv(ms, md, m1, m2, eye)
    return ts, (ts, md)


def _tri_inv_b(res, gs):
    ts, md = res
    z = jnp.zeros_like(md)
    return (tuple(_lockstep([_tri_inv_bwd_gen(t, g) for t, g in zip(ts, gs)])), z, z, z, z)


_tri_inv.defvjp(_tri_inv_f, _tri_inv_b)


@jax.custom_vjp
def _tri_inv_saved(ms, ts):
    return tuple(t.astype(F32) for t in ts)


def _tri_inv_saved_f(ms, ts):
    return _tri_inv_saved(ms, ts), ts


def _tri_inv_saved_b(ts, gs):
    dms = tuple(_lockstep([_tri_inv_bwd_gen(t, g) for t, g in zip(ts, gs)]))
    return dms, tuple(jnp.zeros_like(t) for t in ts)


_tri_inv_saved.defvjp(_tri_inv_saved_f, _tri_inv_saved_b)


SR = DNH * CH


def _scan_masks():
    i, j = np.arange(SR)[:, None], np.arange(SR)[None, :]
    same = (i >> 6) == (j >> 6)
    cm = np.zeros((2, 6, SR, SR), np.float32)
    for d in range(2):
        diff = i - j if d == 0 else j - i
        cm[d, 0] = same & (diff >= 0)
        cm[d, 1] = same & (diff > 0)
        cm[d, 2] = (i >> 4) == (j >> 4)
        cm[d, 3] = ((i >> 5) == (j >> 5)) & ((i >> 4) != (j >> 4))
        cm[d, 4] = (i >> 5) != (j >> 5)
        cm[d, 5] = i == j
    t, u = np.arange(CH)[:, None], np.arange(CH)[None, :]
    c64 = np.stack([t >= u, t <= u]).astype(np.float32)
    bm = ((np.arange(SR)[:, None] >> 6) == (np.arange(DNW)[None, :] >> 7)).astype(np.float32)
    return jnp.asarray(cm), jnp.asarray(c64), jnp.asarray(bm)


@jax.custom_vjp
def _fold(x, bm):
    xm = x * bm
    return xm[:, 0:128] + xm[:, 128:256] + xm[:, 256:384] + xm[:, 384:512]


def _fold_f(x, bm):
    return _fold(x, bm), bm


def _fold_b(bm, g):
    return jnp.concatenate([g, g, g, g], axis=1) * bm, jnp.zeros_like(bm)


_fold.defvjp(_fold_f, _fold_b)


def _chunk_gen(q, k, v, gb, s, cm, c64, bm, d):
    incl, strict = cm[0], cm[1]
    lane = lax.broadcasted_iota(jnp.int32, (1, 128), 1)
    oh = lambda n: jnp.where(lane == n, 1.0, 0.0).astype(F32)
    col = lambda x, base: jnp.concatenate(
        [jnp.sum(x * oh(base + d * DNH + h), axis=1, keepdims=True) for h in range(DNH)], axis=0)
    beta = col(gb, 8)
    gam = col(_mask_mm(c64, gb), 0)
    yield
    tot = jnp.sum(gb, axis=0, keepdims=True)
    gtots = [jnp.sum(tot * oh(d * DNH + h), axis=1, keepdims=True) for h in range(DNH)]
    gtot = jnp.concatenate([jnp.broadcast_to(t, (CH, 1)) for t in gtots], axis=0)
    eg = jnp.concatenate([jnp.broadcast_to(jnp.exp(t), (1, 128)) for t in gtots], axis=1)
    row0 = jnp.where(lax.broadcasted_iota(jnp.int32, (128, 1), 0) == 0, 1.0, 0.0).astype(F32)
    gam_row = jnp.sum(jnp.broadcast_to(gam, (SR, 128)).T * row0, axis=0, keepdims=True)
    dec = jnp.exp(jnp.where(incl > 0.5, gam - gam_row, NEG))
    e_gam = jnp.exp(gam)
    kb = k * beta
    kk = _mm_nt(kb, k)
    yield
    tm = yield strict * (kk * dec)
    u = _mm(tm, v * beta)
    yield
    w = _mm(tm, kb * e_gam)
    yield
    qk = _mm_nt(q, k) * dec
    yield
    ws = _mm(w, s)
    yield
    v_new = u - _fold(ws, bm)
    qs = _mm(q * e_gam, s)
    yield
    o = _fold(qs, bm) + _mm(qk, v_new)
    yield
    vbd = jnp.concatenate([v_new, v_new, v_new, v_new], axis=1) * bm
    s_new = s * eg + _mm_tn(k * jnp.exp(gtot - gam), vbd)
    return o, s_new, tm


def _chunks(args, cm_ref, c64_ref, bm, saved=None):
    cm0 = cm_ref[0]
    if saved is None:
        inv = lambda ms: _tri_inv(ms, cm0[2], cm0[3], cm0[4], cm0[5])
    else:
        inv = lambda ms: _tri_inv_saved(ms, saved)
    return _lockstep([_chunk_gen(q, k, v, gb, s, cm_ref[d], c64_ref[d], bm, d) for q, k, v, gb, s, d in args], inv)


def _chunk_rev(s):
    return jnp.where(s < NCH_CTX, NCH_CTX - 1 - s, NCH + NCH_CTX - 1 - s)


def _scan_specs(step):
    cf = lambda s: step(s)
    cb = lambda s: _chunk_rev(step(s))
    tok = lambda c: pl.BlockSpec((NB, CH, DNW), lambda s: (0, c(s), 0))
    gat = lambda c: pl.BlockSpec((NB, CH, 128), lambda s: (0, c(s), 0))
    sta = lambda c: pl.BlockSpec((NB, None, 128, DNW), lambda s: (0, c(s), 0, 0))
    inv = lambda c: pl.BlockSpec((NB, None, SR, SR), lambda s: (0, c(s), 0, 0))
    return (tok(cf), gat(cf), sta(cf), inv(cf)), (tok(cb), gat(cb), sta(cb), inv(cb))


def _const_specs():
    return [pl.BlockSpec((2, 6, SR, SR), lambda s: (0, 0, 0, 0)), pl.BlockSpec((2, CH, CH), lambda s: (0, 0, 0)),
            pl.BlockSpec((SR, DNW), lambda s: (0, 0))]


SCAN_CHAINS = [(b, d) for b in range(NB) for d in range(2)]


def _stack(ref, b):
    return jnp.concatenate([ref[b, :, 128 * h:128 * h + 128] for h in range(DNH)], axis=0)


def _unstack_to(ref, b, val):
    for h in range(DNH):
        ref[b, :, 128 * h:128 * h + 128] = val[CH * h:CH * h + CH]


def _gather_plan(ins, outs, pos):
    me = _slot(pos)
    locals_, xfers = [], []
    for src, dst in zip(ins, outs):
        locals_.append((src, dst.at[me]))
        for k in range(1, 8):
            peer = _flip(pos, k)
            xfers.append((src, dst.at[me], dst.at[_slot(peer)], peer))
    return locals_, xfers


def _scan_fwd(q, k, v, gates, carry=()):
    nc = len(carry)

    def body(*refs):
        cm_ref, c64_ref, bm_ref, qf, kf, vf, gf, qb, kb, vb, gb = refs[:11]
        c_in = refs[11:11 + nc]
        of_ref, sf_ref, tf_ref, ob_ref, sb_ref, tb_ref = refs[11 + nc:17 + nc]
        c_out = refs[17 + nc:17 + 2 * nc]
        s_ref = refs[17 + 2 * nc]
        sems = refs[18 + 2 * nc:]
        if nc:
            @pl.when(pl.program_id(0) == 0)
            def _():
                _xfer_start(_gather_plan(c_in, c_out, _my_pos()), sems)

        @pl.when(pl.program_id(0) == 0)
        def _():
            s_ref[...] = jnp.zeros_like(s_ref)

        bm = bm_ref[...]
        dirs = ((qf, kf, vf, gf, of_ref, sf_ref, tf_ref), (qb, kb, vb, gb, ob_ref, sb_ref, tb_ref))
        args = [(_stack(dirs[d][0], b), _stack(dirs[d][1], b), _stack(dirs[d][2], b), dirs[d][3][b], s_ref[i], d)
                for i, (b, d) in enumerate(SCAN_CHAINS)]
        for i, (o, s_new, tm) in enumerate(_chunks(args, cm_ref, c64_ref, bm)):
            b, d = SCAN_CHAINS[i]
            dirs[d][5][b] = args[i][4]
            dirs[d][6][b] = tm.astype(BF16)
            _unstack_to(dirs[d][4], b, o)
            s_ref[i] = s_new

        if nc:
            @pl.when(pl.program_id(0) == NCH - 1)
            def _():
                _xfer_wait(_gather_plan(c_in, c_out, _my_pos()), sems)

    (tf, gf_, sf_, if_), (tb, gb_, sb_, ib_) = _scan_specs(lambda s: s)
    o_sh = jax.ShapeDtypeStruct((NB, T, DNW), F32)
    st_sh = jax.ShapeDtypeStruct((NB, NCH, 128, DNW), F32)
    inv_sh = jax.ShapeDtypeStruct((NB, NCH, SR, SR), BF16)
    hbm = pl.BlockSpec(memory_space=pltpu.HBM)
    return pl.pallas_call(
        body, name="scan_fwd", grid=(NCH,),
        out_shape=[o_sh, st_sh, inv_sh, o_sh, st_sh, inv_sh] + [jax.ShapeDtypeStruct((8,) + h.shape, h.dtype) for h in carry],
        in_specs=_const_specs() + [tf, tf, tf, gf_, tb, tb, tb, gb_] + [hbm] * nc,
        out_specs=[tf, sf_, if_, tb, sb_, ib_] + [hbm] * nc,
        scratch_shapes=[pltpu.VMEM((len(SCAN_CHAINS), 128, DNW), F32)] + (_xfer_sems(nc, 7 * nc) if nc else []),
        compiler_params=_cp(vmem=VMEM_BIG),
    )(*_scan_masks(), q, k, v, gates, q, k, v, gates, *carry)


def _scan_bwd(q, k, v, gates, st_f, st_b, inv_f, inv_b, do, carry=()):
    nc = len(carry)

    def body(*refs):
        (cm_ref, c64_ref, bm_ref, qf, kf, vf, gf, sf, tf_, dof, qb, kb, vb, gb, sb, tb_, dob) = refs[:17]
        c_in = refs[17:17 + nc]
        dqf, dkf, dvf, dgf, dqb, dkb, dvb, dgb = refs[17 + nc:25 + nc]
        c_out = refs[25 + nc:25 + 2 * nc]
        ds_ref = refs[25 + 2 * nc]
        sems = refs[26 + 2 * nc:]
        first = pl.program_id(0) == 0
        last = pl.program_id(0) == NCH - 1
        if nc:
            @pl.when(first)
            def _():
                _xfer_start(_scatter_plan(c_in, c_out, _my_pos()), sems)

        @pl.when(first)
        def _():
            ds_ref[...] = jnp.zeros_like(ds_ref)

        bm = bm_ref[...]
        dirs = ((qf, kf, vf, gf, sf, dof, dqf, dkf, dvf, dgf), (qb, kb, vb, gb, sb, dob, dqb, dkb, dvb, dgb))
        prim = tuple((_stack(dirs[d][0], b), _stack(dirs[d][1], b), _stack(dirs[d][2], b), dirs[d][3][b], dirs[d][4][b])
                     for b, d in SCAN_CHAINS)
        cots = tuple((_stack(dirs[d][5], b), ds_ref[i]) for i, (b, d) in enumerate(SCAN_CHAINS))
        saved = tuple((tf_, tb_)[d][b] for b, d in SCAN_CHAINS)
        _, vjp = jax.vjp(lambda ps: tuple(r[:2] for r in _chunks([p + (SCAN_CHAINS[i][1],) for i, p in enumerate(ps)],
                                                                cm_ref, c64_ref, bm, saved)), prim)
        (grads,) = vjp(cots)
        for i, (dq, dk, dv, dg, ds) in enumerate(grads):
            b, d = SCAN_CHAINS[i]
            _unstack_to(dirs[d][6], b, dq)
            _unstack_to(dirs[d][7], b, dk)
            _unstack_to(dirs[d][8], b, dv)
            dirs[d][9][b] = dg
            ds_ref[i] = ds

        if nc:
            @pl.when(last)
            def _():
                _xfer_wait(_scatter_plan(c_in, c_out, _my_pos()), sems)

    (tf, gf_, sf_, if_), (tb, gb_, sb_, ib_) = _scan_specs(lambda s: NCH - 1 - s)
    o_sh = jax.ShapeDtypeStruct((NB, T, DNW), F32)
    g_sh = jax.ShapeDtypeStruct((NB, T, 128), F32)
    hbm = pl.BlockSpec(memory_space=pltpu.HBM)
    return pl.pallas_call(
        body, name="scan_bwd", grid=(NCH,),
        out_shape=[o_sh, o_sh, o_sh, g_sh] * 2 + [jax.ShapeDtypeStruct(g.shape, g.dtype) for g in carry],
        in_specs=_const_specs() + [tf, tf, tf, gf_, sf_, if_, tf, tb, tb, tb, gb_, sb_, ib_, tb] + [hbm] * nc,
        out_specs=[tf, tf, tf, gf_, tb, tb, tb, gb_] + [hbm] * nc,
        scratch_shapes=[pltpu.VMEM((len(SCAN_CHAINS), 128, DNW), F32)] + (_xfer_sems(nc, 7 * nc) if nc else []),
        compiler_params=_cp(vmem=VMEM_BIG),
    )(*_scan_masks(), q, k, v, gates, st_f, inv_f, do, q, k, v, gates, st_b, inv_b, do, *carry)


def _segmean(a, bd):
    hi = a.astype(BF16)
    lo = (a - hi.astype(F32)).astype(BF16)
    return jnp.dot(hi, bd, preferred_element_type=F32) + jnp.dot(lo, bd, preferred_element_type=F32)


def _na_prep_fwd(p, qw, kw, bd):
    def body(q_ref, k_ref, v_ref, qw_ref, kw_ref, bd_ref, oq_ref, ok_ref, ov_ref):
        bdm = bd_ref[...]
        q = q_ref[...]
        k = k_ref[...]
        oq_ref[...] = (q * lax.rsqrt(_segmean(q * q, bdm) + EPS) * qw_ref[...] * 0.125).astype(BF16)
        ok_ref[...] = (k * lax.rsqrt(_segmean(k * k, bdm) + EPS) * kw_ref[...]).astype(BF16)
        ov_ref[...] = v_ref[...].astype(BF16)

    ps = lambda o: pl.BlockSpec((None, RT, NAW), lambda b, j: (b, j, COL_NA // NAW + o))
    one = pl.BlockSpec((1, NAW), lambda b, j: (0, 0))
    os_ = pl.BlockSpec((None, RT, NAW), lambda b, j: (b, j, 0))
    return pl.pallas_call(
        body, name="na_prep_fwd", grid=(NB, T // RT), out_shape=[jax.ShapeDtypeStruct((NB, T, NAW), BF16)] * 3,
        in_specs=[ps(0), ps(1), ps(2), one, one, pl.BlockSpec((NAW, NAW), lambda b, j: (0, 0))], out_specs=[os_] * 3,
    )(p, p, p, qw, kw, bd)


def _na_prep_bwd(p, qw, kw, bd, dq, dk, dv):
    def body(q_ref, k_ref, qw_ref, kw_ref, bd_ref, dq_ref, dk_ref, dv_ref, o_ref, gq_ref, gk_ref):
        b, j = pl.program_id(0), pl.program_id(1)
        bdm = bd_ref[...]

        def one(x, w, dy, scale):
            r = lax.rsqrt(_segmean(x * x, bdm) + EPS)
            g = dy * w * scale
            dx = r * g - x * (r * r * r) * _segmean(g * x, bdm)
            dw = jnp.sum(dy * x * r * scale, axis=0, keepdims=True)
            return dx, dw

        dqv = jnp.where(j == 0, 0.0, dq_ref[...])
        dxq, dwq = one(q_ref[...], qw_ref[...], dqv, 0.125)
        dxk, dwk = one(k_ref[...], kw_ref[...], dk_ref[...], 1.0)
        o_ref[...] = jnp.concatenate([dxq, dxk, dv_ref[...]], axis=1)
        pad = jnp.zeros((7, NAW), F32)

        @pl.when((b == 0) & (j == 0))
        def _():
            gq_ref[...] = jnp.zeros_like(gq_ref)
            gk_ref[...] = jnp.zeros_like(gk_ref)

        gq_ref[...] += jnp.concatenate([dwq, pad], axis=0)
        gk_ref[...] += jnp.concatenate([dwk, pad], axis=0)

    ps = lambda o: pl.BlockSpec((None, RT, NAW), lambda b, j: (b, j, COL_NA // NAW + o))
    one = pl.BlockSpec((1, NAW), lambda b, j: (0, 0))
    ts = pl.BlockSpec((None, RT, NAW), lambda b, j: (b, j, 0))
    acc = pl.BlockSpec((8, NAW), lambda b, j: (0, 0))
    return pl.pallas_call(
        body, name="na_prep_bwd", grid=(NB, T // RT),
        out_shape=[jax.ShapeDtypeStruct((NB, T, 1536), F32), jax.ShapeDtypeStruct((8, NAW), F32),
                   jax.ShapeDtypeStruct((8, NAW), F32)],
        in_specs=[ps(0), ps(1), one, one, pl.BlockSpec((NAW, NAW), lambda b, j: (0, 0)),
                  pl.BlockSpec((None, RT, NAW), lambda b, j: (b, jnp.maximum(j - 1, 0), 0)), ts, ts],
        out_specs=[pl.BlockSpec((None, RT, 1536), lambda b, j: (b, j, 0)), acc, acc],
        compiler_params=_cp(vmem=VMEM_BIG),
    )(p, p, qw, kw, bd, dq, dk, dv)


def _na_row0(r):
    return jnp.clip(r - WIN_ROWS // 2, 0, ROWS - WIN_ROWS)


def _na_cls(r):
    return _na_row0(r) - r + WIN_ROWS - 1


NHS = 4
NLW = NHS * 64
NSR = NHS * GRID_W


def _na_stack(x):
    lane = lax.broadcasted_iota(jnp.int32, (1, NLW), 1)
    return jnp.concatenate([jnp.where((lane >> 6) == h, x, jnp.zeros_like(x)) for h in range(NHS)], axis=0)


def _na_unstack(y):
    lane = lax.broadcasted_iota(jnp.int32, (1, NLW), 1)
    out = jnp.zeros((GRID_W, NLW), y.dtype)
    for h in range(NHS):
        out = jnp.where((lane >> 6) == h, y[GRID_W * h:GRID_W * h + GRID_W], out)
    return out


def _na_probs(q4, kw, kc, bias):
    s_w = lax.dot_general(q4, kw, _NT, preferred_element_type=F32) + bias
    s_c = lax.dot_general(q4, kc, _NT, preferred_element_type=F32)
    mx = jnp.maximum(jnp.max(s_w, axis=1, keepdims=True), jnp.max(s_c, axis=1, keepdims=True))
    p_w = jnp.exp(s_w - mx)
    p_c = jnp.exp(s_c - mx)
    inv = 1.0 / (jnp.sum(p_w, axis=1, keepdims=True) + jnp.sum(p_c, axis=1, keepdims=True))
    return p_w, p_c, inv


def _na_specs():
    qs = pl.BlockSpec((None, GRID_W, NLW), lambda g, b, r: (b, CT // GRID_W + r, g))
    kv = pl.BlockSpec((None, T, NLW), lambda g, b, r: (b, 0, g))
    bs = pl.BlockSpec((None, None, NSR, NWIN), lambda g, b, r: (_na_cls(r), g, 0, 0))
    lat = pl.BlockSpec((None, GRID_W, NLW), lambda g, b, r: (b, r, g))
    return qs, kv, bs, lat


def _na_fwd_gen(q, kw, vw, kc, vc, bias):
    q4 = _na_stack(q)
    s_w = lax.dot_general(q4, kw, _NT, preferred_element_type=F32) + bias
    yield
    s_c = lax.dot_general(q4, kc, _NT, preferred_element_type=F32)
    yield
    mx = jnp.maximum(jnp.max(s_w, axis=1, keepdims=True), jnp.max(s_c, axis=1, keepdims=True))
    p_w = jnp.exp(s_w - mx)
    p_c = jnp.exp(s_c - mx)
    inv = 1.0 / (jnp.sum(p_w, axis=1, keepdims=True) + jnp.sum(p_c, axis=1, keepdims=True))
    yield
    o_w = jnp.dot(p_w.astype(BF16), vw, preferred_element_type=F32)
    yield
    o4 = (o_w + jnp.dot(p_c.astype(BF16), vc, preferred_element_type=F32)) * inv
    return _na_unstack(o4)


def _na_fwd(qn, kn, vb, bias, carry=()):
    nc = len(carry)
    grid = (NB, ROWS)
    ng = NAH // NHS

    def body(*refs):
        q_ref, k_ref, v_ref, b_ref = refs[:4]
        c_in, o_ref, c_out, sems = refs[4:4 + nc], refs[4 + nc], refs[5 + nc:5 + 2 * nc], refs[5 + 2 * nc:]
        ids = [pl.program_id(a) for a in range(2)]
        if nc:
            @pl.when((ids[0] == 0) & (ids[1] == 0))
            def _():
                _xfer_start(_gather_plan(c_in, c_out, _my_pos()), sems)

        r = ids[1]
        start = pl.multiple_of(CT + _na_row0(r) * GRID_W, GRID_W)
        gens = []
        for g in range(ng):
            ln = slice(NLW * g, NLW * g + NLW)
            gens.append(_na_fwd_gen(q_ref[:, ln], k_ref[pl.ds(start, NWIN), ln], v_ref[pl.ds(start, NWIN), ln],
                                    k_ref[0:CT, ln], v_ref[0:CT, ln], b_ref[g]))
        for g, o in enumerate(_lockstep(gens)):
            o_ref[:, NLW * g:NLW * g + NLW] = o.astype(BF16)

        if nc:
            @pl.when((ids[0] == grid[0] - 1) & (ids[1] == grid[1] - 1))
            def _():
                _xfer_wait(_gather_plan(c_in, c_out, _my_pos()), sems)

    qs = pl.BlockSpec((None, GRID_W, NAW), lambda b, r: (b, CT // GRID_W + r, 0))
    kv = pl.BlockSpec((None, T, NAW), lambda b, r: (b, 0, 0))
    bs = pl.BlockSpec((None, ng, NSR, NWIN), lambda b, r: (_na_cls(r), 0, 0, 0))
    lat = pl.BlockSpec((None, GRID_W, NAW), lambda b, r: (b, r, 0))
    hbm = pl.BlockSpec(memory_space=pltpu.HBM)
    res = pl.pallas_call(
        body, name="na_fwd", grid=grid,
        out_shape=[jax.ShapeDtypeStruct((NB, SEQ, NAW), BF16)] + [jax.ShapeDtypeStruct((8,) + h.shape, h.dtype) for h in carry],
        in_specs=[qs, kv, kv, bs] + [hbm] * nc, out_specs=[lat] + [hbm] * nc,
        scratch_shapes=_xfer_sems(nc, 7 * nc) if nc else [],
        compiler_params=_cp(vmem=VMEM_BIG),
    )(qn, kn, vb, bias, *carry)
    return res[0], list(res[1:])


def _na_bwd(qn, kn, vb, bias, do):
    def body(q_ref, k_ref, v_ref, b_ref, do_ref, dq_ref, dk_ref, dv_ref, db_ref):
        b, r = pl.program_id(1), pl.program_id(2)

        @pl.when((b == 0) & (r == 0))
        def _():
            db_ref[...] = jnp.zeros_like(db_ref)

        @pl.when(r == 0)
        def _():
            dk_ref[...] = jnp.zeros_like(dk_ref)
            dv_ref[...] = jnp.zeros_like(dv_ref)

        start = pl.multiple_of(CT + _na_row0(r) * GRID_W, GRID_W)
        cls = _na_cls(r)
        kw, vw = k_ref[pl.ds(start, NWIN), :], v_ref[pl.ds(start, NWIN), :]
        kc, vc = k_ref[0:CT, :], v_ref[0:CT, :]
        q4 = _na_stack(q_ref[...])
        p_w, p_c, inv = _na_probs(q4, kw, kc, b_ref[...])
        p_w = p_w * inv
        p_c = p_c * inv
        do4 = _na_stack(do_ref[...].astype(BF16))
        dp_w = lax.dot_general(do4, vw, _NT, preferred_element_type=F32)
        dp_c = lax.dot_general(do4, vc, _NT, preferred_element_type=F32)
        delta = jnp.sum(p_w * dp_w, axis=1, keepdims=True) + jnp.sum(p_c * dp_c, axis=1, keepdims=True)
        ds_w = p_w * (dp_w - delta)
        ds_c = p_c * (dp_c - delta)
        db_ref[cls] += ds_w
        ds_wb, ds_cb = ds_w.astype(BF16), ds_c.astype(BF16)
        dq_ref[...] = _na_unstack(jnp.dot(ds_wb, kw, preferred_element_type=F32)
                                  + jnp.dot(ds_cb, kc, preferred_element_type=F32))
        dk_ref[pl.ds(start, NWIN), :] += lax.dot_general(ds_wb, q4, _TN, preferred_element_type=F32)
        dv_ref[pl.ds(start, NWIN), :] += lax.dot_general(p_w.astype(BF16), do4, _TN, preferred_element_type=F32)
        dk_ref[0:CT, :] += lax.dot_general(ds_cb, q4, _TN, preferred_element_type=F32)
        dv_ref[0:CT, :] += lax.dot_general(p_c.astype(BF16), do4, _TN, preferred_element_type=F32)

    qs, kv, bs, lat = _na_specs()
    return pl.pallas_call(
        body, name="na_bwd", grid=(NAH // NHS, NB, ROWS),
        out_shape=[jax.ShapeDtypeStruct((NB, SEQ, NAW), F32)] + [jax.ShapeDtypeStruct((NB, T, NAW), F32)] * 2
        + [jax.ShapeDtypeStruct((WIN_ROWS, NAH // NHS, NSR, NWIN), F32)],
        in_specs=[qs, kv, kv, bs, lat],
        out_specs=[lat, kv, kv, pl.BlockSpec((WIN_ROWS, None, NSR, NWIN), lambda g, b, r: (0, g, 0, 0))],
        compiler_params=_cp(vmem=VMEM_BIG),
    )(qn, kn, vb, bias, do)


def _na_tables():
    cols = np.arange(GRID_W)
    win_start = np.clip(cols - WIN_COLS // 2, 0, GRID_W - WIN_COLS)
    kc = cols[None, :]
    valid = (kc >= win_start[:, None]) & (kc < win_start[:, None] + WIN_COLS)
    rel = np.clip(kc - cols[:, None] + WIN_COLS - 1, 0, 2 * WIN_COLS - 2)
    return valid, rel


def _bias_table(rpb):
    valid, rel = _na_tables()
    nrel = 2 * WIN_COLS - 1
    shifts = np.stack([(rel == r) & valid for r in range(nrel)]).astype(np.float32)
    neg = np.where(valid, 0.0, NEG).astype(np.float32)

    def body(rpb_ref, sh_ref, neg_ref, o_ref):
        h = pl.program_id(0)
        for dr in range(2 * WIN_ROWS - 1):
            tile = neg_ref[...]
            for r in range(nrel):
                tile = tile + rpb_ref[h, dr, r] * sh_ref[r]
            for cls in range(WIN_ROWS):
                i = dr - cls
                if 0 <= i < WIN_ROWS:
                    o_ref[cls, :, i * GRID_W:(i + 1) * GRID_W] = tile

    return pl.pallas_call(
        body, name="bias_table", grid=(NAH,),
        out_shape=jax.ShapeDtypeStruct((WIN_ROWS, NAH // NHS, NSR, NWIN), F32),
        in_specs=[pl.BlockSpec(memory_space=pltpu.SMEM), pl.BlockSpec((nrel, GRID_W, GRID_W), lambda h: (0, 0, 0)),
                  pl.BlockSpec((GRID_W, GRID_W), lambda h: (0, 0))],
        out_specs=pl.BlockSpec((WIN_ROWS, None, GRID_W, NWIN), lambda h: (0, h // NHS, h % NHS, 0)),
    )(rpb, jnp.asarray(shifts), jnp.asarray(neg))


def _rpb_fold(db):
    def body1(d_ref, o_ref):
        for dr in range(2 * WIN_ROWS - 1):
            acc = None
            for c in range(WIN_ROWS):
                i = dr - c
                if 0 <= i < WIN_ROWS:
                    blk = d_ref[c, :, i * GRID_W:(i + 1) * GRID_W]
                    acc = blk if acc is None else acc + blk
            o_ref[dr] = acc

    t15 = pl.pallas_call(
        body1, name="rpb_fold_rows", grid=(NAH,),
        out_shape=jax.ShapeDtypeStruct((NAH, 2 * WIN_ROWS - 1, GRID_W, GRID_W), F32),
        in_specs=[pl.BlockSpec((WIN_ROWS, None, GRID_W, NWIN), lambda h: (0, h, 0, 0))],
        out_specs=pl.BlockSpec((None, 2 * WIN_ROWS - 1, GRID_W, GRID_W), lambda h: (h, 0, 0, 0)),
        compiler_params=_cp(vmem=VMEM_BIG),
    )(db.reshape(WIN_ROWS, NAH, GRID_W, NWIN))
    valid, rel = _na_tables()
    onehot = np.zeros((GRID_W * GRID_W, 128), np.float32)
    flat_rel, flat_valid = rel.reshape(-1), valid.reshape(-1)
    onehot[np.arange(GRID_W * GRID_W)[flat_valid], flat_rel[flat_valid]] = 1.0

    def body2(a_ref, oh_ref, o_ref):
        o_ref[...] = _hmm(a_ref[...], oh_ref[...])

    vm = pl.BlockSpec(memory_space=pltpu.VMEM)
    out = pl.pallas_call(
        body2, name="rpb_fold_cols", out_shape=jax.ShapeDtypeStruct((NAH * 15, 128), F32),
        in_specs=[vm, vm], out_specs=vm, compiler_params=_cp(vmem=VMEM_BIG),
    )(t15.reshape(NAH * 15, GRID_W * GRID_W), jnp.asarray(onehot))
    return out[:, :31].reshape(NAH, 15, 31)


def _dn_out(o, z, w):
    outs = []
    for h in range(DNH):
        sl = slice(128 * h, 128 * h + 128)
        oh = o[:, sl]
        r = lax.rsqrt(jnp.mean(oh * oh, axis=-1, keepdims=True) + EPS)
        outs.append(oh * r * w[:, sl] * _silu(z[:, sl]))
    return jnp.concatenate(outs, axis=1)


def _outproj_fwd(of, ob, p, na, onw, x, g1, w_out):
    def body(of_ref, ob_ref, z_ref, na_ref, w_ref, x_ref, g_ref, wo_ref, x1_ref, ai_ref, ao_ref):
        dn = _dn_out(of_ref[...] + ob_ref[...], z_ref[...], w_ref[...])
        ai = jnp.concatenate([dn.astype(BF16), na_ref[...]], axis=1)
        ai_ref[...] = ai
        ao = jnp.dot(ai, wo_ref[...], preferred_element_type=F32)
        ao_ref[...] = ao
        x1_ref[...] = x_ref[...] + g_ref[...] * ao

    lat = lambda w: pl.BlockSpec((None, RT, w), lambda b, j: (b, j, 0))
    return pl.pallas_call(
        body, name="outproj_fwd", grid=(NB, SEQ // RT),
        out_shape=[jax.ShapeDtypeStruct((NB, SEQ, D), F32), jax.ShapeDtypeStruct((NB, SEQ, D), BF16),
                   jax.ShapeDtypeStruct((NB, SEQ, D), F32)],
        in_specs=[pl.BlockSpec((None, RT, DNW), lambda b, j: (b, j + 1, 0)),
                  pl.BlockSpec((None, RT, DNW), lambda b, j: (b, j + 1, 0)),
                  pl.BlockSpec((None, RT, DNW), lambda b, j: (b, j + 1, COL_Z // DNW)),
                  lat(NAW), pl.BlockSpec((1, DNW), lambda b, j: (0, 0)), lat(D),
                  pl.BlockSpec((None, 1, D), lambda b, j: (b, 0, 0)), pl.BlockSpec((D, D), lambda b, j: (0, 0))],
        out_specs=[lat(D), lat(D), lat(D)],
        compiler_params=_cp(vmem=VMEM_BIG),
    )(of, ob, p, na, onw, x, g1, w_out)


def _outproj_bwd(dx1, ao, g1, w_out, of, ob, p, onw):
    def body(dx_ref, ao_ref, g_ref, wo_ref, of_ref, ob_ref, z_ref, w_ref, dao_ref, do_ref, dz_ref, dna_ref, dg_ref, dw_ref):
        b, j = pl.program_id(0), pl.program_id(1)

        @pl.when((b == 0) & (j == 0))
        def _():
            dw_ref[...] = jnp.zeros_like(dw_ref)

        @pl.when(j == 0)
        def _():
            do_ref[...] = jnp.zeros_like(do_ref)
            dz_ref[...] = jnp.zeros_like(dz_ref)

        @pl.when(j == 1)
        def _():
            dg_ref[...] = jnp.zeros_like(dg_ref)

        @pl.when(j >= 1)
        def _():
            dx = dx_ref[...]
            dg_ref[...] += jnp.sum(dx * ao_ref[...], axis=0, keepdims=True)
            dao = (g_ref[...] * dx).astype(BF16)
            dao_ref[...] = dao
            dai = lax.dot_general(dao, wo_ref[...], _NT, preferred_element_type=F32)
            dna_ref[...] = dai[:, DNW:]
            _, vjp = jax.vjp(_dn_out, of_ref[...] + ob_ref[...], z_ref[...], w_ref[...])
            do, dz, dw = vjp(dai[:, :DNW])
            do_ref[...] = do
            dz_ref[...] = dz
            dw_ref[...] += jnp.concatenate([dw, jnp.zeros((7, DNW), F32)], axis=0)

    lat = lambda w: pl.BlockSpec((None, RT, w), lambda b, j: (b, jnp.maximum(j - 1, 0), 0))
    tok = lambda w: pl.BlockSpec((None, RT, w), lambda b, j: (b, j, 0))
    return pl.pallas_call(
        body, name="outproj_bwd", grid=(NB, T // RT),
        out_shape=[jax.ShapeDtypeStruct((NB, SEQ, D), BF16), jax.ShapeDtypeStruct((NB, T, DNW), F32),
                   jax.ShapeDtypeStruct((NB, T, DNW), F32), jax.ShapeDtypeStruct((NB, SEQ, NAW), F32),
                   jax.ShapeDtypeStruct((NB, 1, D), F32), jax.ShapeDtypeStruct((8, DNW), F32)],
        in_specs=[_rows2d_spec(D), lat(D), pl.BlockSpec((None, 1, D), lambda b, j: (b, 0, 0)),
                  pl.BlockSpec((D, D), lambda b, j: (0, 0)),
                  tok(DNW), tok(DNW),
                  pl.BlockSpec((None, RT, DNW), lambda b, j: (b, j, COL_Z // DNW)),
                  pl.BlockSpec((1, DNW), lambda b, j: (0, 0))],
        out_specs=[lat(D), tok(DNW), tok(DNW), lat(NAW), pl.BlockSpec((None, 1, D), lambda b, j: (b, 0, 0)),
                   pl.BlockSpec((8, DNW), lambda b, j: (0, 0))],
        compiler_params=_cp(vmem=VMEM_BIG),
    )(dx1, ao, g1, w_out, of, ob, p, onw)


FT_FWD, FT_BWD = 512, 256
NFF = DFF // FFT


def _ffn_specs(ft):
    row = lambda w: pl.BlockSpec((ft, w), lambda i, f: (i, 0))
    ex = pl.BlockSpec((None, 1, D), lambda i, f: (i // (SEQ // ft), 0, 0))
    one = pl.BlockSpec((1, D), lambda i, f: (0, 0))
    wg = pl.BlockSpec((None, D, FFT), lambda i, f: (f, 0, 0))
    wu = pl.BlockSpec((None, D, FFT), lambda i, f: (NFF + f, 0, 0))
    wo = pl.BlockSpec((FFT, D), lambda i, f: (f, 0))
    ff = pl.BlockSpec((ft, FFT), lambda i, f: (i, f))
    return row, ex, one, wg, wu, wo, ff


def _ffn_fwd(x1, sh2, sc2, g2, n2w, w_in4, w_out, target):
    ft, tpe = FT_FWD, SEQ // FT_FWD

    def body(x_ref, sh_ref, sc_ref, g_ref, w2_ref, wg_ref, wu_ref, wo_ref, t_ref,
             h2_ref, gt_ref, up_ref, dy_ref, dg_ref, loss_ref, acc_ref):
        i, f = pl.program_id(0), pl.program_id(1)

        @pl.when(f == 0)
        def _():
            h2_ref[...] = _norm_mod(x_ref[...], w2_ref[...], sh_ref[...], sc_ref[...]).astype(BF16)
            acc_ref[...] = jnp.zeros_like(acc_ref)

        h2 = h2_ref[...]
        gate = jnp.dot(h2, wg_ref[...], preferred_element_type=F32)
        up = jnp.dot(h2, wu_ref[...], preferred_element_type=F32)
        gt_ref[...] = gate.astype(BF16)
        up_ref[...] = up.astype(BF16)
        acc_ref[...] += jnp.dot((_silu(gate) * up).astype(BF16), wo_ref[...], preferred_element_type=F32)

        @pl.when((i == 0) & (f == 0))
        def _():
            loss_ref[...] = jnp.zeros_like(loss_ref)

        @pl.when(f == NFF - 1)
        def _():
            ffn = acc_ref[...]
            err = x_ref[...] + g_ref[...] * ffn - t_ref[...]
            dy = err * (1.0 / D)
            dy_ref[...] = dy
            part = jnp.sum(dy * ffn, axis=0, keepdims=True)
            loss_ref[...] += jnp.sum(jnp.sum(err * err, axis=1, keepdims=True), axis=0, keepdims=True) * (0.5 / D)

            @pl.when(i % tpe == 0)
            def _():
                dg_ref[...] = part

            @pl.when(i % tpe != 0)
            def _():
                dg_ref[...] += part

    row, ex, one, wg, wu, wo, ff = _ffn_specs(ft)
    return pl.pallas_call(
        body, name="ffn_fwd", grid=(NTOK // ft, NFF),
        out_shape=[jax.ShapeDtypeStruct((NTOK, D), BF16), jax.ShapeDtypeStruct((NTOK, DFF), BF16),
                   jax.ShapeDtypeStruct((NTOK, DFF), BF16), jax.ShapeDtypeStruct((NTOK, D), F32),
                   jax.ShapeDtypeStruct((NB, 1, D), F32), jax.ShapeDtypeStruct((8, 128), F32)],
        in_specs=[row(D), ex, ex, ex, one, wg, wu, wo, row(D)],
        out_specs=[row(D), ff, ff, row(D), ex, pl.BlockSpec((8, 128), lambda i, f: (0, 0))],
        scratch_shapes=[pltpu.VMEM((ft, D), F32)],
        compiler_params=_cp(vmem=VMEM_BIG),
    )(x1, sh2, sc2, g2, n2w, w_in4, w_in4, w_out, target)


def _rows2d_spec(width):
    return pl.BlockSpec((RT, width), lambda b, j: (b * (SEQ // RT) + jnp.maximum(j - 1, 0), 0))


def _ffn_bwd(dy, gate, up, x1, sh2, sc2, g2, n2w, w_in4, w_out):
    ft, tpe = FT_BWD, SEQ // FT_BWD
    ni = NTOK // ft

    def body(dy_ref, gt_ref, up_ref, x_ref, sh_ref, sc_ref, g_ref, w2_ref, wg_ref, wu_ref, wo_ref,
             dff_ref, act_ref, dgu_ref, dx_ref, dsh_ref, dsc_ref, dw_ref, acc_ref):
        f, i = pl.program_id(0), pl.program_id(1)
        rows = pl.ds(pl.multiple_of(i * ft, ft), ft)
        dff = (g_ref[...] * dy_ref[...]).astype(BF16)
        dff_ref[...] = dff
        dact = lax.dot_general(dff, wo_ref[...], _NT, preferred_element_type=F32)
        gate, up = gt_ref[...].astype(F32), up_ref[...].astype(F32)
        sg = _sigmoid(gate)
        sl = gate * sg
        act_ref[...] = (sl * up).astype(BF16)
        dgate = (dact * up * (sg * (1.0 + gate * (1.0 - sg)))).astype(BF16)
        dup = (dact * sl).astype(BF16)
        dgu_ref[0] = dgate
        dgu_ref[1] = dup
        part = (lax.dot_general(dgate, wg_ref[...], _NT, preferred_element_type=F32)
                + lax.dot_general(dup, wu_ref[...], _NT, preferred_element_type=F32))

        @pl.when(f == 0)
        def _():
            acc_ref[rows, :] = part

        @pl.when(f > 0)
        def _():
            acc_ref[rows, :] += part

        @pl.when((i == 0) & (f == 0))
        def _():
            dw_ref[...] = jnp.zeros_like(dw_ref)

        @pl.when(f == NFF - 1)
        def _():
            _, vjp = jax.vjp(_norm_mod, x_ref[...], w2_ref[...], sh_ref[...], sc_ref[...])
            dx, dw, dsh, dsc = vjp(acc_ref[rows, :])
            dx_ref[...] = dy_ref[...] + dx
            dw_ref[...] += dw

            @pl.when(i % tpe == 0)
            def _():
                dsh_ref[...] = dsh
                dsc_ref[...] = dsc

            @pl.when(i % tpe != 0)
            def _():
                dsh_ref[...] += dsh
                dsc_ref[...] += dsc

    last = lambda f, real, spare: jnp.where(f == NFF - 1, real, spare)
    row = lambda w: pl.BlockSpec((ft, w), lambda f, i: (i, 0))
    ex = pl.BlockSpec((None, 1, D), lambda f, i: (i // tpe, 0, 0))
    one = pl.BlockSpec((1, D), lambda f, i: (0, 0))
    wg = pl.BlockSpec((None, D, FFT), lambda f, i: (f, 0, 0))
    wu = pl.BlockSpec((None, D, FFT), lambda f, i: (NFF + f, 0, 0))
    wo = pl.BlockSpec((FFT, D), lambda f, i: (f, 0))
    ff = pl.BlockSpec((ft, FFT), lambda f, i: (i, f))
    row_once = lambda w: pl.BlockSpec((ft, w), lambda f, i: (last(f, i, ni), 0))
    ex_once = pl.BlockSpec((None, 1, D), lambda f, i: (last(f, i // tpe, NB), 0, 0))
    return pl.pallas_call(
        body, name="ffn_bwd", grid=(NFF, ni),
        out_shape=[jax.ShapeDtypeStruct((NTOK + ft, D), BF16), jax.ShapeDtypeStruct((NTOK, DFF), BF16),
                   jax.ShapeDtypeStruct((2, NTOK, DFF), BF16), jax.ShapeDtypeStruct((NTOK + ft, D), F32),
                   jax.ShapeDtypeStruct((NB + 1, 1, D), F32), jax.ShapeDtypeStruct((NB + 1, 1, D), F32),
                   jax.ShapeDtypeStruct((1, D), F32)],
        in_specs=[row(D), ff, ff, row(D), ex, ex, ex, one, wg, wu, wo],
        out_specs=[row_once(D), ff, pl.BlockSpec((2, ft, FFT), lambda f, i: (0, i, f)), row_once(D), ex_once, ex_once, one],
        scratch_shapes=[pltpu.VMEM((NTOK, D), F32)],
        compiler_params=_cp(vmem=VMEM_BIG),
    )(dy, gate, up, x1, sh2, sc2, g2, n2w, w_in4, w_in4, w_out)


def _local_step(x, ctx, target, mod_x, mod_c, norm1_w, norm2_w, w_in_p, conv_w, a_log, dt_bias, out_norm_w,
                q_norm_w, k_norm_w, rpb, w_out_b, w_ffn_in4, w_ffn_out_b, scatter_early=False, gather_late=None):
    sh1, sc1, g1, sh2, sc2, g2 = [mod_x[:, i * D:(i + 1) * D].reshape(NB, 1, D) for i in range(6)]
    csh1, csc1 = mod_c[0:D].reshape(1, 1, D), mod_c[D:2 * D].reshape(1, 1, D)
    cw = jnp.concatenate([conv_w, jnp.zeros((3, 1536), F32)], axis=0)
    prm = jnp.pad(jnp.stack([a_log.reshape(8), dt_bias.reshape(8)]), ((0, 6), (0, 120)))
    onw = jnp.tile(out_norm_w.reshape(1, 128), (1, DNH))
    qw = jnp.tile(q_norm_w.reshape(1, 64), (1, NAH))
    kw = jnp.tile(k_norm_w.reshape(1, 64), (1, NAH))
    seg = np.arange(NAW) // 64
    bd = jnp.asarray((seg[:, None] == seg[None, :]).astype(np.float32) / 64.0, dtype=BF16)
    bias = _bias_table(rpb)

    p, hm = _inproj_fwd(x, ctx, sh1, sc1, csh1, csc1, norm1_w, w_in_p)
    dq_, dk_, dv_ = _dn_prep_fwd(p, cw)
    gates = _gates_fwd(p, prm)
    res = _scan_fwd(dq_, dk_, dv_, gates, carry=gather_late[:2] if gather_late else ())
    o_f, st_f, inv_f, o_b, st_b, inv_b = res[:6]
    qn, kn, vb = _na_prep_fwd(p, qw, kw, bd)
    na, got = _na_fwd(qn, kn, vb, bias, carry=gather_late[2:] if gather_late else ())
    if gather_late:
        w_out_b = res[6].reshape(D, D)
        w_ffn_in4 = res[7].reshape(4, D, FFT)
        w_ffn_out_b = got[0].reshape(DFF, D)
    x1, ai, ao = _outproj_fwd(o_f, o_b, p, na, onw, x, g1, w_out_b)
    x1f = x1.reshape(NTOK, D)
    h2, gate, up, dy, dg2, loss8 = _ffn_fwd(x1f, sh2, sc2, g2, norm2_w, w_ffn_in4, w_ffn_out_b,
                                             target.reshape(NTOK, D))
    dff, act, dgu, dx1, dsh2, dsc2, dn2w = _ffn_bwd(dy, gate, up, x1f, sh2, sc2, g2, norm2_w, w_ffn_in4, w_ffn_out_b)
    g_w_ffn_out = _matmul_tn(act, dff, FFT, D, 512, "wgrad_ffn_out")
    g_w_ffn_in4 = _matmul_tn(
        h2, dgu, D, FFT, 1024, "wgrad_ffn_in", nn=2 * NFF,
        b_block=(None, 1024, FFT), b_map=lambda i, j, k: (j // NFF, k, j % NFF),
        out_shape=(4, D, FFT), out_block=(None, D, FFT), out_map=lambda i, j, k: (j, i, 0))
    dsh2, dsc2 = dsh2[:NB], dsc2[:NB]
    dao, do, dz, dna, dg1, donw = _outproj_bwd(dx1, ao, g1, w_out_b, o_f, o_b, p, onw)
    g_w_out = _matmul_tn(ai.reshape(NTOK, D), dao.reshape(NTOK, D), D, D, 1024, "wgrad_out")
    dqn, dkn, dvn, dbias = _na_bwd(qn, kn, vb, bias, dna)
    dp_na, dqw, dkw = _na_prep_bwd(p, qw, kw, bd, dqn, dkn, dvn)
    early = [g_w_out.reshape(8, D // 8, D), g_w_ffn_in4.reshape(8, D // 2, FFT), g_w_ffn_out.reshape(8, DFF // 8, D)]
    res = _scan_bwd(dq_, dk_, dv_, gates, st_f, st_b, inv_f, inv_b, do, carry=early if scatter_early else ())
    dqf, dkf, dvf, dgf, dqb, dkb, dvb, dgb = res[:8]
    if scatter_early:
        g_w_out, g_w_ffn_in4, g_w_ffn_out = res[8:]
    dpq, dpk, dpv, gcq, gck, gcv = _dn_prep_bwd(p, cw, (dqf, dqb), (dkf, dkb), (dvf, dvb))
    dgt, dprm = _gates_bwd(p, prm, dgf, dgb)
    dpb, grad_x, dsh1, dsc1, dcsh1, dcsc1, dn1w = _inproj_bwd(
        (dpq, dpk, dpv), dz, dp_na, dgt, w_in_p, x, ctx, sh1, sc1, csh1, csc1, norm1_w, dx1)
    g_w_in_p = _matmul_tn(dpb.reshape(NB * T, PC), hm.reshape(NB * T, D), PC, D, 512, "wgrad_in")
    g_rpb = _rpb_fold(dbias)

    dmod_x = jnp.concatenate([dsh1, dsc1, dg1, dsh2, dsc2, dg2], axis=2).reshape(NB, 6 * D)
    dmod_c = jnp.concatenate([dcsh1.reshape(D), dcsc1.reshape(D), jnp.zeros((4 * D,), F32)])
    g_conv = jnp.concatenate([gcq[:5], gck[:5], gcv[:5]], axis=1)
    small = dict(
        norm1_w=dn1w, norm2_w=dn2w, a_log=dprm[0, :8], dt_bias=dprm[1, :8],
        out_norm_w=donw[0].reshape(DNH, 128).sum(0), q_norm_w=dqw[0].reshape(NAH, 64).sum(0),
        k_norm_w=dkw[0].reshape(NAH, 64).sum(0), rpb=g_rpb, conv_w=g_conv)
    return loss8[0, 0], grad_x, dmod_x, dmod_c, small, g_w_in_p, g_w_out, g_w_ffn_in4, g_w_ffn_out


def _perm_w_in(wt):
    return jnp.concatenate([wt[:2048], wt[2064:3600], wt[2048:2064], jnp.zeros((PC - IN_COLS, wt.shape[1]), wt.dtype)],
                           axis=0)


def _unperm_w_in(gt):
    return jnp.concatenate([gt[:2048], gt[COL_G:COL_G + 16], gt[2048:COL_G]], axis=0)


def _misc_row(a_log, dt_bias, onw, qnw, knw):
    return jnp.concatenate([a_log.reshape(8), dt_bias.reshape(8), jnp.zeros((112,), F32), onw.reshape(128),
                            qnw.reshape(64), knw.reshape(64), jnp.zeros((D - 384,), F32)])


def _pack_small(c_ctx, n1, n2, b_ada, a_log, dt_bias, onw, qnw, knw, rpb, conv_s):
    misc = _misc_row(a_log, dt_bias, onw, qnw, knw)
    rp = jnp.concatenate([rpb.reshape(-1), jnp.zeros((4 * D - 3720,), F32)])
    cv = jnp.concatenate([conv_s.reshape(-1), jnp.zeros((2 * D - 1920,), F32)])
    return jnp.concatenate([c_ctx.reshape(-1), n1.reshape(-1), n2.reshape(-1), b_ada.reshape(-1), misc, rp, cv]).reshape(16, D)


def _unpack_small(a):
    f = a.reshape(-1)
    misc = f[9 * D:10 * D]
    return dict(
        c_ctx=f[0:D], norm1_w=f[D:2 * D].reshape(1, D), norm2_w=f[2 * D:3 * D].reshape(1, D),
        b_ada=f[3 * D:9 * D].reshape(1, 6 * D), dn_A_log=misc[0:8].reshape(1, 2, 4), dn_dt_bias=misc[8:16].reshape(1, 2, 4),
        dn_out_norm_w=misc[128:256].reshape(1, 128), na_q_norm_w=misc[256:320].reshape(1, 64),
        na_k_norm_w=misc[320:384].reshape(1, 64), na_rpb=f[10 * D:10 * D + 3720].reshape(1, 8, 15, 31),
        dn_conv_w=f[14 * D:14 * D + 1920].reshape(1, 5, 384))


_WEIGHTS = ['c_ctx', 'norm1_w', 'norm2_w', 'w_ada', 'b_ada', 'w_in', 'dn_conv_w', 'dn_A_log', 'dn_dt_bias',
            'dn_out_norm_w', 'na_q_norm_w', 'na_k_norm_w', 'na_rpb', 'w_out', 'w_ffn_in', 'w_ffn_out']


def kernel(x, c, ctx, c_ctx, norm1_w, norm2_w, w_ada, b_ada, w_in, dn_conv_w, dn_A_log, dn_dt_bias, dn_out_norm_w, na_q_norm_w, na_k_norm_w, na_rpb, w_out, w_ffn_in, w_ffn_out, loss_target, m_c_ctx, m_norm1_w, m_norm2_w, m_w_ada, m_b_ada, m_w_in, m_dn_conv_w, m_dn_A_log, m_dn_dt_bias, m_dn_out_norm_w, m_na_q_norm_w, m_na_k_norm_w, m_na_rpb, m_w_out, m_w_ffn_in, m_w_ffn_out, v_c_ctx, v_norm1_w, v_norm2_w, v_w_ada, v_b_ada, v_w_in, v_dn_conv_w, v_dn_A_log, v_dn_dt_bias, v_dn_out_norm_w, v_na_q_norm_w, v_na_k_norm_w, v_na_rpb, v_w_out, v_w_ffn_in, v_w_ffn_out):
    ix, iy, ic = lax.axis_index("x"), lax.axis_index("y"), lax.axis_index("c")
    chip = 2 * ix + iy
    me = 4 * ix + 2 * iy + ic

    crow = jnp.concatenate([c, c_ctx.reshape(1, D), jnp.zeros((5, D), F32)], axis=0)
    cvrow = jnp.concatenate([dn_conv_w[0].reshape(-1), jnp.zeros((2 * D - 1920,), F32)]).reshape(2, D)
    g0 = _allgather8(jnp.concatenate([crow, cvrow, jnp.zeros((6, D), F32)], axis=0), "gather_cond")
    cin = g0[:, :8].reshape(64, D)
    conv_full = jnp.concatenate([g0[2 * s, 8:10].reshape(-1)[:1920].reshape(5, 384) for s in range(4)], axis=1)

    w_ada_s = w_ada[0]
    b_s = lax.dynamic_slice(b_ada, (0, chip * 1536), (1, 1536))
    mod_s = _ada_fwd(cin, w_ada_s, b_s)
    mods = _gather_rows(mod_s, "gather_mod")
    mod_mine = jnp.concatenate([mods[0], mods[2], mods[4], mods[6]], axis=1)
    mod_x, mod_c = mod_mine[0:NB], mod_mine[2]

    def half(w2d):
        r = w2d.shape[0] // 2
        return lax.dynamic_slice(w2d, (ic * r, 0), (r, w2d.shape[1]))

    w_in_t = w_in[0].T
    (gw_in,) = _gather_weights([_cast_bf16(lax.dynamic_slice(w_in_t, (0, ic * (D // 2)), (900, D // 2)), "cast_w_in")])
    late = [_cast_bf16(half(w_out[0]), "cast_w_out"), _cast_bf16(half(w_ffn_in[0]), "cast_w_ffn_in"),
            _cast_bf16(half(w_ffn_out[0]), "cast_w_ffn_out")]
    w_in_p = _perm_w_in(gw_in.reshape(4, 2, 900, D // 2).transpose(0, 2, 1, 3).reshape(IN_COLS, D))

    (loss_part, grad_x, dmod_x, dmod_c, small, g_w_in_p, r_w_out, r_w_ffn_in, r_w_ffn_out) = _local_step(
        x, ctx, loss_target, mod_x, mod_c, norm1_w, norm2_w, w_in_p, conv_full, dn_A_log[0], dn_dt_bias[0],
        dn_out_norm_w[0], na_q_norm_w[0], na_k_norm_w[0], na_rpb[0], None, None, None,
        scatter_early=True, gather_late=late)
    loss = lax.psum(loss_part, ("x", "y", "c"))

    dm = jnp.concatenate([dmod_x, dmod_c.reshape(1, 6 * D), jnp.zeros((5, 6 * D), F32)], axis=0).reshape(48, D)
    misc = _misc_row(small["a_log"], small["dt_bias"], small["out_norm_w"], small["q_norm_w"], small["k_norm_w"])
    sm = jnp.concatenate([
        small["norm1_w"].reshape(-1), small["norm2_w"].reshape(-1), misc,
        small["rpb"].reshape(-1), jnp.zeros((4 * D - 3720,), F32),
        small["conv_w"].reshape(-1), jnp.zeros((8 * D - 7680,), F32), jnp.zeros((D,), F32)]).reshape(16, D)
    gsl, gtot = _allgather8(jnp.concatenate([dm, sm], axis=0), "gather_small", with_sum=True)
    gmod = gsl[:, :48].reshape(64, 6 * D)
    st = gtot[48:64].reshape(-1)
    g_n1, g_n2, g_misc = st[0:D].reshape(1, D), st[D:2 * D].reshape(1, D), st[2 * D:3 * D]
    g_rpb = st[3 * D:3 * D + 3720].reshape(1, 8, 15, 31)
    g_conv_full = st[7 * D:7 * D + 7680].reshape(5, 1536)
    g_conv_s = lax.dynamic_slice(g_conv_full, (0, chip * 384), (5, 384)).reshape(1, 5, 384)

    gmod_s = lax.dynamic_slice(gmod, (0, chip * 1536), (64, 1536))
    g_w_ada_s, pctx = _ada_bwd(cin, gmod_s, w_ada_s)
    parts = _allgather8(pctx, "gather_cctx")
    g_b_ada, g_c_ctx = _ada_small(gmod, parts, c_ctx.reshape(1, D))

    g_w_in8 = _unperm_w_in(g_w_in_p).reshape(4, 900, 2, D // 2).transpose(0, 2, 1, 3).reshape(8, 900, D // 2)
    names = ["w_in", "w_out", "w_ffn_in", "w_ffn_out"]
    rs = list(_scatter_grads([g_w_in8])) + [r_w_out, r_w_ffn_in, r_w_ffn_out]
    full = _sibling_exchange([_sum_slots(r, "sum_" + nm) for r, nm in zip(rs, names)])
    g_w_in_t = full[0].transpose(1, 0, 2).reshape(900, D)
    g_w_out_s = full[1].reshape(1, D // 4, D)
    g_w_ffn_in_s = full[2].reshape(1, D, FFT)
    g_w_ffn_out_s = full[3].reshape(1, DFF // 4, D)

    grads = dict(
        c_ctx=g_c_ctx.reshape(D), norm1_w=g_n1, norm2_w=g_n2, w_ada=g_w_ada_s.reshape(1, D, 1536), b_ada=g_b_ada,
        w_in=g_w_in_t, dn_conv_w=g_conv_s, dn_A_log=g_misc[0:8].reshape(1, 2, 4), dn_dt_bias=g_misc[8:16].reshape(1, 2, 4),
        dn_out_norm_w=g_misc[128:256].reshape(1, 128), na_q_norm_w=g_misc[256:320].reshape(1, 64),
        na_k_norm_w=g_misc[320:384].reshape(1, 64), na_rpb=g_rpb, w_out=g_w_out_s, w_ffn_in=g_w_ffn_in_s,
        w_ffn_out=g_w_ffn_out_s)
    pk = lambda d_: _pack_small(d_["c_ctx"], d_["norm1_w"], d_["norm2_w"], d_["b_ada"], d_["dn_A_log"], d_["dn_dt_bias"],
                                d_["dn_out_norm_w"], d_["na_q_norm_w"], d_["na_k_norm_w"], d_["na_rpb"], d_["dn_conv_w"])
    wts = dict(c_ctx=c_ctx, norm1_w=norm1_w, norm2_w=norm2_w, b_ada=b_ada, dn_A_log=dn_A_log, dn_dt_bias=dn_dt_bias,
               dn_out_norm_w=dn_out_norm_w, na_q_norm_w=na_q_norm_w, na_k_norm_w=na_k_norm_w, na_rpb=na_rpb, dn_conv_w=dn_conv_w)
    ms = dict(c_ctx=m_c_ctx, norm1_w=m_norm1_w, norm2_w=m_norm2_w, b_ada=m_b_ada, dn_A_log=m_dn_A_log, dn_dt_bias=m_dn_dt_bias,
              dn_out_norm_w=m_dn_out_norm_w, na_q_norm_w=m_na_q_norm_w, na_k_norm_w=m_na_k_norm_w, na_rpb=m_na_rpb,
              dn_conv_w=m_dn_conv_w)
    vs = dict(c_ctx=v_c_ctx, norm1_w=v_norm1_w, norm2_w=v_norm2_w, b_ada=v_b_ada, dn_A_log=v_dn_A_log, dn_dt_bias=v_dn_dt_bias,
              dn_out_norm_w=v_dn_out_norm_w, na_q_norm_w=v_na_q_norm_w, na_k_norm_w=v_na_k_norm_w, na_rpb=v_na_rpb,
              dn_conv_w=v_dn_conv_w)
    sd, snm, snv, _ = _adamw(pk(wts), pk(grads), pk(ms), pk(vs), "adamw_small")
    delta, new_m, new_v = _unpack_small(sd), _unpack_small(snm), _unpack_small(snv)
    big = dict(w_ada=(w_ada, m_w_ada, v_w_ada), w_in=(w_in, m_w_in, v_w_in), w_out=(w_out, m_w_out, v_w_out),
               w_ffn_in=(w_ffn_in, m_w_ffn_in, v_w_ffn_in), w_ffn_out=(w_ffn_out, m_w_ffn_out, v_w_ffn_out))
    for nm, (w_, m_, v_) in big.items():
        shp = w_.shape
        if nm == "w_in":
            r2, back = (lambda a: a[0].T), (lambda a: a.T.reshape(shp))
            res = _adamw(w_in_t, grads[nm], r2(m_), r2(v_), "adamw_" + nm)
        else:
            r2, back = (lambda a: a.reshape(shp[1], shp[2])), (lambda a: a.reshape(shp))
            res = _adamw(r2(w_), r2(grads[nm]), r2(m_), r2(v_), "adamw_" + nm)
        delta[nm], new_m[nm], new_v[nm], grads[nm] = [back(a) for a in res]

    return (loss, grad_x, *[grads[n] for n in _WEIGHTS], *[delta[n] for n in _WEIGHTS],
            *[new_m[n] for n in _WEIGHTS], *[new_v[n] for n in _WEIGHTS])
```

```python
import functools

import numpy as np
import jax
import jax.numpy as jnp
from jax import lax
from jax.experimental import pallas as pl
from jax.experimental.pallas import tpu as pltpu

F32 = jnp.float32
BF16 = jnp.bfloat16

D = 1024
NB = 2
SEQ = 2048
CT = 256
T = CT + SEQ
NTOK = NB * SEQ
CH = 64
NCH = T // CH
NCH_CTX = CT // CH
DNH, DNW = 4, 512
NAH, NAW = 8, 512
GRID_W = 64
ROWS = SEQ // GRID_W
WIN_ROWS = 8
WIN_COLS = 16
NWIN = WIN_ROWS * GRID_W
DFF = 2816
FFT = 1408
IN_COLS = 3600
PC = 3712
COL_Z, COL_NA, COL_G = 1536, 2048, 3584
EPS = 1e-6
NEG = -1e30
RT = 256
VMEM_BIG = 56 * 1024 * 1024
MESH = pl.DeviceIdType.MESH

ADAM_LR, ADAM_B1, ADAM_B2, ADAM_EPS, ADAM_WD, ADAM_STEP = 0.001, 0.9, 0.999, 1e-08, 0.01, 10


def _cp(vmem=None, sem=None):
    kw = {}
    if vmem is not None:
        kw["vmem_limit_bytes"] = vmem
    if sem is not None:
        kw["dimension_semantics"] = sem
    return pltpu.CompilerParams(**kw)


_NN = (((1,), (0,)), ((), ()))
_NT = (((1,), (1,)), ((), ()))
_TN = (((0,), (0,)), ((), ()))


def _dg(a, b, dims):
    return lax.dot_general(a.astype(BF16), b.astype(BF16), dims, preferred_element_type=F32)


@jax.custom_vjp
def _mm(a, b):
    return _dg(a, b, _NN)


_mm.defvjp(lambda a, b: (_dg(a, b, _NN), (a, b)),
           lambda r, g: (_dg(g, r[1], _NT), _dg(r[0], g, _TN)))


@jax.custom_vjp
def _mm_nt(a, b):
    return _dg(a, b, _NT)


_mm_nt.defvjp(lambda a, b: (_dg(a, b, _NT), (a, b)),
              lambda r, g: (_dg(g, r[1], _NN), _dg(g, r[0], _TN)))


@jax.custom_vjp
def _mm_tn(a, b):
    return _dg(a, b, _TN)


_mm_tn.defvjp(lambda a, b: (_dg(a, b, _TN), (a, b)),
              lambda r, g: (_dg(r[1], g, _NT), _dg(r[0], g, _NN)))


def _split_dot(a, b, dims):
    hi = b.astype(BF16)
    lo = (b - hi.astype(F32)).astype(BF16)
    ab = a.astype(BF16)
    return (lax.dot_general(ab, hi, dims, preferred_element_type=F32)
            + lax.dot_general(ab, lo, dims, preferred_element_type=F32))


@jax.custom_vjp
def _mask_mm(mask, b):
    return _split_dot(mask, b, _NN)


_mask_mm.defvjp(lambda mask, b: (_split_dot(mask, b, _NN), mask),
                lambda mask, g: (jnp.zeros_like(mask), _split_dot(mask, g, _TN)))


def _hmm(a, b):
    return jnp.dot(a, b, precision=lax.Precision.HIGHEST, preferred_element_type=F32)


def _sigmoid(x):
    return 1.0 / (1.0 + jnp.exp(-x))


def _silu(x):
    return x * _sigmoid(x)


def _norm_mod(x, w, sh, sc):
    r = lax.rsqrt(jnp.mean(x * x, axis=-1, keepdims=True) + EPS)
    return (x * r * w) * (1.0 + sc) + sh


def _my_pos():
    return lax.axis_index("x"), lax.axis_index("y"), lax.axis_index("c")


def _flip(pos, k):
    x, y, c = pos
    return (1 - x if k & 4 else x, 1 - y if k & 2 else y, 1 - c if k & 1 else c)


def _slot(pos):
    return 4 * pos[0] + 2 * pos[1] + pos[2]


def _allgather8(v, name, with_sum=False):
    R, W = v.shape

    def body(v_ref, out_ref, *rest):
        if with_sum:
            tot_ref, send_sems, recv_sems, lsem = rest
        else:
            send_sems, recv_sems, lsem = rest
        me = _my_pos()
        mine = pltpu.make_async_copy(v_ref, out_ref.at[_slot(me)], lsem)
        mine.start()
        sends = []
        for k in range(1, 8):
            cp = pltpu.make_async_remote_copy(
                src_ref=v_ref, dst_ref=out_ref.at[_slot(me)], send_sem=send_sems.at[k - 1],
                recv_sem=recv_sems.at[k - 1], device_id=_flip(me, k), device_id_type=MESH)
            cp.start()
            sends.append(cp)
        for k in range(1, 8):
            peer = _flip(me, k)
            pltpu.make_async_remote_copy(
                src_ref=v_ref, dst_ref=out_ref.at[_slot(peer)], send_sem=send_sems.at[k - 1],
                recv_sem=recv_sems.at[k - 1], device_id=peer, device_id_type=MESH).wait_recv()
        for cp in sends:
            cp.wait_send()
        mine.wait()
        if with_sum:
            acc = out_ref[0]
            for s in range(1, 8):
                acc = acc + out_ref[s]
            tot_ref[...] = acc

    out_shape = [jax.ShapeDtypeStruct((8, R, W), F32)]
    out_specs = [pl.BlockSpec(memory_space=pltpu.VMEM)]
    if with_sum:
        out_shape.append(jax.ShapeDtypeStruct((R, W), F32))
        out_specs.append(pl.BlockSpec(memory_space=pltpu.VMEM))
    res = pl.pallas_call(
        body, name=name, out_shape=out_shape,
        in_specs=[pl.BlockSpec(memory_space=pltpu.VMEM)], out_specs=out_specs,
        scratch_shapes=[pltpu.SemaphoreType.DMA((7,)), pltpu.SemaphoreType.DMA((7,)), pltpu.SemaphoreType.DMA],
        compiler_params=_cp(vmem=VMEM_BIG),
    )(v)
    return res if with_sum else res[0]


def _gather_rows(v, name):
    W = v.shape[1]

    def body(v_ref, out_ref, send_sems, recv_sems, lsem):
        pos = _my_pos()
        me = _slot(pos)
        rows = lambda p: v_ref.at[pl.ds(pl.multiple_of(8 * p, 8), 8)]
        locals_ = [(rows(me), out_ref.at[me])]
        xfers = [(rows(_slot(_flip(pos, k))), out_ref.at[me], out_ref.at[_slot(_flip(pos, k))], _flip(pos, k))
                 for k in range(1, 8)]
        _xfer_start((locals_, xfers), (send_sems, recv_sems, lsem))
        _xfer_wait((locals_, xfers), (send_sems, recv_sems, lsem))

    vm = pl.BlockSpec(memory_space=pltpu.VMEM)
    return pl.pallas_call(
        body, name=name, out_shape=jax.ShapeDtypeStruct((8, 8, W), F32), in_specs=[vm], out_specs=vm,
        scratch_shapes=_xfer_sems(1, 7),
    )(v)


def _gather_weights(halves):
    n = len(halves)

    def body(*refs):
        ins, outs = refs[:n], refs[n:2 * n]
        send_sems, recv_sems, lsems = refs[2 * n:]
        me = _my_pos()
        x, y, c = me
        sib = (x, y, 1 - c)
        chips = [4, 2, 6]

        def cp(a, k, block, to, src=None):
            dst = outs[a].at[_slot(block)]
            return pltpu.make_async_remote_copy(
                src_ref=dst if src is None else src, dst_ref=dst, send_sem=send_sems.at[a, k],
                recv_sem=recv_sems.at[a, k], device_id=to, device_id_type=MESH)

        mine, first, passed = [], [], []
        for a in range(n):
            m = pltpu.make_async_copy(ins[a], outs[a].at[_slot(me)], lsems.at[a])
            m.start()
            mine.append(m)
            first.append(cp(a, 0, me, sib, src=ins[a]))
            for j, k in enumerate(chips):
                first.append(cp(a, 1 + j, me, _flip(me, k), src=ins[a]))
        for f in first:
            f.start()
        for j, k in enumerate(chips):
            for a in range(n):
                cp(a, 1 + j, _flip(me, k), me).wait_recv()
                p = cp(a, 4 + j, _flip(me, k), sib)
                p.start()
                passed.append(p)
        for a in range(n):
            cp(a, 0, sib, me).wait_recv()
            for j, k in enumerate(chips):
                cp(a, 4 + j, _flip(sib, k), me).wait_recv()
        for f in first + passed:
            f.wait_send()
        for m in mine:
            m.wait()

    vm = pl.BlockSpec(memory_space=pltpu.VMEM)
    return pl.pallas_call(
        body, name="gather_weights",
        out_shape=[jax.ShapeDtypeStruct((8,) + h.shape, h.dtype) for h in halves],
        in_specs=[vm] * n, out_specs=[vm] * n,
        scratch_shapes=[pltpu.SemaphoreType.DMA((n, 7)), pltpu.SemaphoreType.DMA((n, 7)),
                        pltpu.SemaphoreType.DMA((n,))],
        compiler_params=_cp(vmem=VMEM_BIG),
    )(*halves)


def _xfer_start(plan, sems):
    send_sems, recv_sems, lsems = sems
    locals_, xfers = plan
    for i, (src, dst) in enumerate(locals_):
        pltpu.make_async_copy(src, dst, lsems.at[i]).start()
    for i, (src, dst, _, peer) in enumerate(xfers):
        pltpu.make_async_remote_copy(src_ref=src, dst_ref=dst, send_sem=send_sems.at[i], recv_sem=recv_sems.at[i],
                                     device_id=peer, device_id_type=MESH).start()


def _xfer_wait(plan, sems):
    send_sems, recv_sems, lsems = sems
    locals_, xfers = plan
    for i, (src, _, landing, peer) in enumerate(xfers):
        pltpu.make_async_remote_copy(src_ref=src, dst_ref=landing, send_sem=send_sems.at[i], recv_sem=recv_sems.at[i],
                                     device_id=peer, device_id_type=MESH).wait_recv()
    for i, (src, dst, _, peer) in enumerate(xfers):
        pltpu.make_async_remote_copy(src_ref=src, dst_ref=dst, send_sem=send_sems.at[i], recv_sem=recv_sems.at[i],
                                     device_id=peer, device_id_type=MESH).wait_send()
    for i, (src, dst) in enumerate(locals_):
        pltpu.make_async_copy(src, dst, lsems.at[i]).wait()


def _xfer_sems(n_local, n_xfer):
    return [pltpu.SemaphoreType.DMA((n_xfer,)), pltpu.SemaphoreType.DMA((n_xfer,)), pltpu.SemaphoreType.DMA((n_local,))]


def _scatter_plan(ins, outs, pos):
    me = _slot(pos)
    locals_, xfers = [], []
    for src, dst in zip(ins, outs):
        locals_.append((src.at[me], dst.at[me]))
        for k in range(1, 8):
            peer = _flip(pos, k)
            xfers.append((src.at[_slot(peer)], dst.at[me], dst.at[_slot(peer)], peer))
    return locals_, xfers


def _scatter_grads(grads):
    n = len(grads)

    def body(*refs):
        plan = _scatter_plan(refs[:n], refs[n:2 * n], _my_pos())
        _xfer_start(plan, refs[2 * n:])
        _xfer_wait(plan, refs[2 * n:])

    hbm = pl.BlockSpec(memory_space=pltpu.HBM)
    return pl.pallas_call(
        body, name="scatter_grads", out_shape=[jax.ShapeDtypeStruct(g.shape, g.dtype) for g in grads],
        in_specs=[hbm] * n, out_specs=[hbm] * n, scratch_shapes=_xfer_sems(n, 7 * n),
    )(*grads)


def _sibling_exchange(halves):
    n = len(halves)

    def body(*refs):
        ins, outs = refs[:n], refs[n:2 * n]
        send_sems, recv_sems, lsems = refs[2 * n:]
        x, y, c = _my_pos()
        sib = (x, y, 1 - c)
        mine, sends = [], []
        for a in range(n):
            m = pltpu.make_async_copy(ins[a], outs[a].at[c], lsems.at[a])
            m.start()
            mine.append(m)
            cp = pltpu.make_async_remote_copy(
                src_ref=ins[a], dst_ref=outs[a].at[c], send_sem=send_sems.at[a], recv_sem=recv_sems.at[a],
                device_id=sib, device_id_type=MESH)
            cp.start()
            sends.append(cp)
        for a in range(n):
            pltpu.make_async_remote_copy(
                src_ref=ins[a], dst_ref=outs[a].at[1 - c], send_sem=send_sems.at[a], recv_sem=recv_sems.at[a],
                device_id=sib, device_id_type=MESH).wait_recv()
        for cp in sends:
            cp.wait_send()
        for m in mine:
            m.wait()

    vm = pl.BlockSpec(memory_space=pltpu.VMEM)
    return pl.pallas_call(
        body, name="sibling_exchange",
        out_shape=[jax.ShapeDtypeStruct((2,) + h.shape, h.dtype) for h in halves],
        in_specs=[vm] * n, out_specs=[vm] * n,
        scratch_shapes=[pltpu.SemaphoreType.DMA((n,)), pltpu.SemaphoreType.DMA((n,)), pltpu.SemaphoreType.DMA((n,))],
        compiler_params=_cp(vmem=VMEM_BIG),
    )(*halves)


def _row_tile(r, cap=512):
    for t in (cap, 256, 128, 64, 32, 16, 8):
        if t <= cap and r % t == 0:
            return t
    return r


def _cast_bf16(a, name):
    R, W = a.shape
    tr = _row_tile(R, 256)

    def body(a_ref, o_ref):
        o_ref[...] = a_ref[...].astype(BF16)

    return pl.pallas_call(
        body, name=name, grid=(R // tr,), out_shape=jax.ShapeDtypeStruct((R, W), BF16),
        in_specs=[pl.BlockSpec((tr, W), lambda i: (i, 0))], out_specs=pl.BlockSpec((tr, W), lambda i: (i, 0)),
    )(a)


def _sum_slots(r, name):
    _, R, W = r.shape
    tr = _row_tile(R, 128)

    def body(r_ref, o_ref):
        acc = r_ref[0].astype(F32)
        for s_ in range(1, 8):
            acc = acc + r_ref[s_].astype(F32)
        o_ref[...] = acc.astype(BF16)

    return pl.pallas_call(
        body, name=name, grid=(R // tr,), out_shape=jax.ShapeDtypeStruct((R, W), BF16),
        in_specs=[pl.BlockSpec((8, tr, W), lambda i: (0, i, 0))], out_specs=pl.BlockSpec((tr, W), lambda i: (i, 0)),
        compiler_params=_cp(vmem=VMEM_BIG),
    )(r)


def _adamw(w, g, m, v, name):
    R, W = w.shape
    tr = _row_tile(R, 256)
    tc = 256 if (tr == R and R * W * 4 > (1 << 21) and W % 256 == 0) else W
    c1 = 1.0 / (1.0 - ADAM_B1 ** ADAM_STEP)
    c2 = 1.0 / (1.0 - ADAM_B2 ** ADAM_STEP)

    def body(w_ref, g_ref, m_ref, v_ref, d_ref, nm_ref, nv_ref, g32_ref):
        gg = g_ref[...].astype(F32)
        g32_ref[...] = gg
        nm = ADAM_B1 * m_ref[...] + (1.0 - ADAM_B1) * gg
        nv = ADAM_B2 * v_ref[...] + (1.0 - ADAM_B2) * (gg * gg)
        d_ref[...] = -ADAM_LR * ((nm * c1) / (jnp.sqrt(nv * c2) + ADAM_EPS) + ADAM_WD * w_ref[...])
        nm_ref[...] = nm
        nv_ref[...] = nv

    spec = pl.BlockSpec((tr, tc), lambda i, j: (i, j))
    return pl.pallas_call(
        body, name=name, grid=(R // tr, W // tc), out_shape=[jax.ShapeDtypeStruct((R, W), F32)] * 4,
        in_specs=[spec] * 4, out_specs=[spec] * 4, compiler_params=_cp(vmem=VMEM_BIG),
    )(w, g, m, v)


def _matmul_tn(a, b, tm, tn, tk, name, nn=1, b_map=None, out_shape=None, out_block=None, out_map=None,
               b_block=None):
    K, M = a.shape
    nk = K // tk
    nm = M // tm

    def body(a_ref, b_ref, o_ref, acc_ref):
        kk = pl.program_id(2)

        @pl.when(kk == 0)
        def _():
            acc_ref[...] = jnp.zeros_like(acc_ref)

        acc_ref[...] += lax.dot_general(a_ref[...], b_ref[...], _TN, preferred_element_type=F32)

        @pl.when(kk == nk - 1)
        def _():
            o_ref[...] = acc_ref[...].astype(BF16)

    return pl.pallas_call(
        body, name=name, grid=(nm, nn, nk),
        out_shape=jax.ShapeDtypeStruct(out_shape or (M, b.shape[-1]), BF16),
        in_specs=[pl.BlockSpec((tk, tm), lambda i, j, k: (k, i)),
                  pl.BlockSpec(b_block or (tk, tn), b_map or (lambda i, j, k: (k, j)))],
        out_specs=pl.BlockSpec(out_block or (tm, tn), out_map or (lambda i, j, k: (i, j))),
        scratch_shapes=[pltpu.VMEM((tm, tn), F32)],
        compiler_params=_cp(vmem=VMEM_BIG),
    )(a, b)


def _ada_fwd(cin, w_ada_s, b_s):
    tn = 512

    def body(c_ref, w_ref, b_ref, o_ref):
        o_ref[...] = _dg(_silu(c_ref[...]), w_ref[...], _NN) + b_ref[...]

    return pl.pallas_call(
        body, name="ada_fwd", grid=(1536 // tn,), out_shape=jax.ShapeDtypeStruct((64, 1536), F32),
        in_specs=[pl.BlockSpec((64, D), lambda j: (0, 0)), pl.BlockSpec((D, tn), lambda j: (0, j)),
                  pl.BlockSpec((1, tn), lambda j: (0, j))],
        out_specs=pl.BlockSpec((64, tn), lambda j: (0, j)),
        compiler_params=_cp(vmem=VMEM_BIG),
    )(cin, w_ada_s, b_s)


def _ada_bwd(cin, gmod_s, w_ada_s):
    tn = 512

    def body(c_ref, g_ref, w_ref, gw_ref, pc_ref):
        j = pl.program_id(0)
        g = g_ref[...]
        gw_ref[...] = _dg(_silu(c_ref[...]), g, _TN)
        rr = lax.broadcasted_iota(jnp.int32, (8, 64), 0)
        cc = lax.broadcasted_iota(jnp.int32, (8, 64), 1)
        sel = jnp.where((rr == 0) & ((cc & 7) == 2), 1.0, 0.0).astype(F32)
        dctx = _hmm(sel, g)
        part = _dg(dctx, w_ref[...], _NT)

        @pl.when(j == 0)
        def _():
            pc_ref[...] = jnp.zeros_like(pc_ref)

        pc_ref[...] += part

    return pl.pallas_call(
        body, name="ada_bwd", grid=(1536 // tn,),
        out_shape=[jax.ShapeDtypeStruct((D, 1536), F32), jax.ShapeDtypeStruct((8, D), F32)],
        in_specs=[pl.BlockSpec((64, D), lambda j: (0, 0)), pl.BlockSpec((64, tn), lambda j: (0, j)),
                  pl.BlockSpec((D, tn), lambda j: (0, j))],
        out_specs=[pl.BlockSpec((D, tn), lambda j: (0, j)), pl.BlockSpec((8, D), lambda j: (0, 0))],
        compiler_params=_cp(vmem=VMEM_BIG),
    )(cin, gmod_s, w_ada_s)


def _ada_small(gmod, parts, c_ctx):
    def body(g_ref, p_ref, c_ref, gb_ref, gc_ref):
        gb_ref[...] = jnp.sum(g_ref[...], axis=0, keepdims=True)
        ds = p_ref[0][0:1] + p_ref[2][0:1] + p_ref[4][0:1] + p_ref[6][0:1]
        cc = c_ref[...]
        s = _sigmoid(cc)
        gc_ref[...] = ds * (s * (1.0 + cc * (1.0 - s)))

    vm = pl.BlockSpec(memory_space=pltpu.VMEM)
    return pl.pallas_call(
        body, name="ada_small",
        out_shape=[jax.ShapeDtypeStruct((1, 6 * D), F32), jax.ShapeDtypeStruct((1, D), F32)],
        in_specs=[vm, vm, vm], out_specs=[vm, vm],
    )(gmod, parts, c_ctx)


def _tok_specs():
    xs = pl.BlockSpec((None, RT, D), lambda b, j: (b, jnp.maximum(j - 1, 0), 0))
    cs = pl.BlockSpec((None, RT, D), lambda b, j: (b, 0, 0))
    ms = pl.BlockSpec((None, 1, D), lambda b, j: (b, 0, 0))
    cms = pl.BlockSpec((None, 1, D), lambda b, j: (0, 0, 0))
    return xs, cs, ms, cms


def _inproj_fwd(x, ctx, sh1, sc1, csh1, csc1, n1w, w_in_p):
    def body(x_ref, c_ref, sh_ref, sc_ref, csh_ref, csc_ref, w1_ref, w_ref, p_ref, hm_ref):
        is_ctx = pl.program_id(1) == 0
        xt = jnp.where(is_ctx, c_ref[...], x_ref[...])
        sh = jnp.where(is_ctx, csh_ref[...], sh_ref[...])
        sc = jnp.where(is_ctx, csc_ref[...], sc_ref[...])
        hb = _norm_mod(xt, w1_ref[...], sh, sc).astype(BF16)
        hm_ref[...] = hb
        p_ref[...] = lax.dot_general(hb, w_ref[...], _NT, preferred_element_type=F32)

    xs, cs, ms, cms = _tok_specs()
    return pl.pallas_call(
        body, name="inproj_fwd", grid=(NB, T // RT),
        out_shape=[jax.ShapeDtypeStruct((NB, T, PC), F32), jax.ShapeDtypeStruct((NB, T, D), BF16)],
        in_specs=[xs, cs, ms, ms, cms, cms, pl.BlockSpec((1, D), lambda b, j: (0, 0)),
                  pl.BlockSpec((PC, D), lambda b, j: (0, 0))],
        out_specs=[pl.BlockSpec((None, RT, PC), lambda b, j: (b, j, 0)),
                   pl.BlockSpec((None, RT, D), lambda b, j: (b, j, 0))],
        compiler_params=_cp(vmem=VMEM_BIG),
    )(x, ctx, sh1, sc1, csh1, csc1, n1w, w_in_p)


def _inproj_bwd(dp_dn, dz, dp_na, dgt, w_in_p, x, ctx, sh1, sc1, csh1, csc1, n1w, dx1):
    def body(d1q_ref, d1k_ref, d1v_ref, d2_ref, d3_ref, d4_ref, w_ref, x_ref, c_ref, sh_ref, sc_ref, csh_ref, csc_ref,
             w1_ref, dx1_ref, dpb_ref, gx_ref, dsh_ref, dsc_ref, dcsh_ref, dcsc_ref, dw1_ref):
        b, j = pl.program_id(0), pl.program_id(1)
        is_ctx = j == 0
        dp = jnp.concatenate([d1q_ref[...], d1k_ref[...], d1v_ref[...], d2_ref[...], d3_ref[...], d4_ref[...]],
                             axis=1).astype(BF16)
        dpb_ref[...] = dp
        dhm = jnp.dot(dp, w_ref[...], preferred_element_type=F32)
        xt = jnp.where(is_ctx, c_ref[...], x_ref[...])
        sh = jnp.where(is_ctx, csh_ref[...], sh_ref[...])
        sc = jnp.where(is_ctx, csc_ref[...], sc_ref[...])
        _, vjp = jax.vjp(_norm_mod, xt, w1_ref[...], sh, sc)
        dxt, dw1, dsh, dsc = vjp(dhm)

        @pl.when((b == 0) & (j == 0))
        def _():
            dw1_ref[...] = jnp.zeros_like(dw1_ref)
            dcsh_ref[...] = jnp.zeros_like(dcsh_ref)
            dcsc_ref[...] = jnp.zeros_like(dcsc_ref)

        dw1_ref[...] += dw1

        @pl.when(is_ctx)
        def _():
            dcsh_ref[...] += dsh
            dcsc_ref[...] += dsc

        @pl.when(j == 1)
        def _():
            dsh_ref[...] = jnp.zeros_like(dsh_ref)
            dsc_ref[...] = jnp.zeros_like(dsc_ref)

        @pl.when(j >= 1)
        def _():
            dsh_ref[...] += dsh
            dsc_ref[...] += dsc
            gx_ref[...] = dx1_ref[...] + dxt

    xs, cs, ms, cms = _tok_specs()
    tsp = lambda w: pl.BlockSpec((None, RT, w), lambda b, j: (b, j, 0))
    one = pl.BlockSpec((1, D), lambda b, j: (0, 0))
    return pl.pallas_call(
        body, name="inproj_bwd", grid=(NB, T // RT),
        out_shape=[jax.ShapeDtypeStruct((NB, T, PC), BF16), jax.ShapeDtypeStruct((NB, SEQ, D), F32),
                   jax.ShapeDtypeStruct((NB, 1, D), F32), jax.ShapeDtypeStruct((NB, 1, D), F32),
                   jax.ShapeDtypeStruct((1, 1, D), F32), jax.ShapeDtypeStruct((1, 1, D), F32),
                   jax.ShapeDtypeStruct((1, D), F32)],
        in_specs=[tsp(512), tsp(512), tsp(512), tsp(512), tsp(1536), tsp(128), pl.BlockSpec((PC, D), lambda b, j: (0, 0)),
                  xs, cs, ms, ms, cms, cms, one, _rows2d_spec(D)],
        out_specs=[tsp(PC), xs, ms, ms, cms, cms, one],
        compiler_params=_cp(vmem=VMEM_BIG),
    )(*dp_dn, dz, dp_na, dgt, w_in_p, x, ctx, sh1, sc1, csh1, csc1, n1w, dx1)


def _shift_masks():
    t = lax.broadcasted_iota(jnp.int32, (T, 1), 0)
    lo = jnp.where(t < CT, 0, CT)
    hi = jnp.where(t < CT, CT, T)
    return {off: (t + off >= lo) & (t + off < hi) for off in range(-2, 3)}


def _shifted(u, off, masks):
    s = u if off == 0 else pltpu.roll(u, (-off) % T, 0)
    return jnp.where(masks[off], s, 0.0)


def _conv_pre(u, w, masks):
    acc = _shifted(u, -2, masks) * w[0:1, :]
    for kk in range(1, 5):
        acc = acc + _shifted(u, kk - 2, masks) * w[kk:kk + 1, :]
    return acc


def _act_q(pre):
    a = _silu(pre)
    return a * lax.rsqrt(jnp.sum(a * a, axis=-1, keepdims=True) + EPS) * (128.0 ** -0.5)


def _act_k(pre):
    a = _silu(pre)
    return a * lax.rsqrt(jnp.sum(a * a, axis=-1, keepdims=True) + EPS)


def _dn_prep_fwd(p, cw):
    def body(pq_ref, pk_ref, pv_ref, wq_ref, wk_ref, wv_ref, q_ref, k_ref, v_ref):
        masks = _shift_masks()
        q_ref[...] = _act_q(_conv_pre(pq_ref[...], wq_ref[...], masks))
        k_ref[...] = _act_k(_conv_pre(pk_ref[...], wk_ref[...], masks))
        v_ref[...] = _silu(_conv_pre(pv_ref[...], wv_ref[...], masks))

    ps = lambda o: pl.BlockSpec((None, T, 128), lambda b, h: (b, 0, o + h))
    ws = lambda o: pl.BlockSpec((8, 128), lambda b, h: (0, o + h))
    os_ = pl.BlockSpec((None, T, 128), lambda b, h: (b, 0, h))
    return pl.pallas_call(
        body, name="dn_prep_fwd", grid=(NB, DNH), out_shape=[jax.ShapeDtypeStruct((NB, T, DNW), F32)] * 3,
        in_specs=[ps(0), ps(4), ps(8), ws(0), ws(4), ws(8)], out_specs=[os_] * 3,
        compiler_params=_cp(vmem=VMEM_BIG),
    )(p, p, p, cw, cw, cw)


def _dn_prep_bwd(p, cw, dqs, dks, dvs):
    def body(pq_ref, pk_ref, pv_ref, wq_ref, wk_ref, wv_ref, dqf_ref, dqb_ref, dkf_ref, dkb_ref, dvf_ref, dvb_ref,
             oq_ref, ok_ref, ov_ref, gq_ref, gk_ref, gv_ref):
        b = pl.program_id(1)
        masks = _shift_masks()

        def one(p_ref, w_ref, df_ref, db_ref, act, o_ref, g_ref):
            u = p_ref[...]
            w = w_ref[...]
            _, vjp = jax.vjp(act, _conv_pre(u, w, masks))
            (dpre,) = vjp(df_ref[...] + db_ref[...])
            du = _shifted(dpre, 2, masks) * w[0:1, :]
            rows = [jnp.sum(dpre * _shifted(u, -2, masks), axis=0, keepdims=True)]
            for kk in range(1, 5):
                du = du + _shifted(dpre, 2 - kk, masks) * w[kk:kk + 1, :]
                rows.append(jnp.sum(dpre * _shifted(u, kk - 2, masks), axis=0, keepdims=True))
            o_ref[...] = du
            gw = jnp.concatenate(rows + [jnp.zeros((3, 128), F32)], axis=0)

            @pl.when(b == 0)
            def _():
                g_ref[...] = gw

            @pl.when(b > 0)
            def _():
                g_ref[...] += gw

        one(pq_ref, wq_ref, dqf_ref, dqb_ref, _act_q, oq_ref, gq_ref)
        one(pk_ref, wk_ref, dkf_ref, dkb_ref, _act_k, ok_ref, gk_ref)
        one(pv_ref, wv_ref, dvf_ref, dvb_ref, _silu, ov_ref, gv_ref)

    ps = lambda o: pl.BlockSpec((None, T, 128), lambda h, b: (b, 0, o + h))
    ws = lambda o: pl.BlockSpec((8, 128), lambda h, b: (0, o + h))
    ds = pl.BlockSpec((None, T, 128), lambda h, b: (b, 0, h))
    outs = pl.pallas_call(
        body, name="dn_prep_bwd", grid=(DNH, NB),
        out_shape=[jax.ShapeDtypeStruct((NB, T, DNW), F32)] * 3 + [jax.ShapeDtypeStruct((8, DNW), F32)] * 3,
        in_specs=[ps(0), ps(4), ps(8), ws(0), ws(4), ws(8)] + [ds] * 6,
        out_specs=[ds] * 3
        + [pl.BlockSpec((8, 128), lambda h, b: (0, h))] * 3,
        compiler_params=_cp(vmem=VMEM_BIG),
    )(p, p, p, cw, cw, cw, *dqs, *dks, *dvs)
    return outs


def _gate_fn(a, prm):
    lane = lax.broadcasted_iota(jnp.int32, a.shape, 1)
    z = a + prm[1:2, :]
    sp = jnp.maximum(z, 0.0) + jnp.log(1.0 + jnp.exp(-jnp.abs(z)))
    g = -jnp.exp(prm[0:1, :]) * sp
    return jnp.where(lane < 8, g, jnp.where(lane < 16, _sigmoid(a), 0.0))


def _gates_fwd(p, prm):
    def body(a_ref, prm_ref, o_ref):
        o_ref[...] = _gate_fn(a_ref[...], prm_ref[...])

    return pl.pallas_call(
        body, name="gates_fwd", grid=(NB,), out_shape=jax.ShapeDtypeStruct((NB, T, 128), F32),
        in_specs=[pl.BlockSpec((None, T, 128), lambda b: (b, 0, COL_G // 128)), pl.BlockSpec((8, 128), lambda b: (0, 0))],
        out_specs=pl.BlockSpec((None, T, 128), lambda b: (b, 0, 0)),
    )(p, prm)


def _gates_bwd(p, prm, dgf, dgb):
    def body(a_ref, prm_ref, dgf_ref, dgb_ref, o_ref, dprm_ref):
        b = pl.program_id(0)
        tot = dgf_ref[...] + dgb_ref[...]
        _, vjp = jax.vjp(_gate_fn, a_ref[...], prm_ref[...])
        da, dprm = vjp(tot)
        o_ref[...] = da

        @pl.when(b == 0)
        def _():
            dprm_ref[...] = dprm

        @pl.when(b > 0)
        def _():
            dprm_ref[...] += dprm

    return pl.pallas_call(
        body, name="gates_bwd", grid=(NB,),
        out_shape=[jax.ShapeDtypeStruct((NB, T, 128), F32), jax.ShapeDtypeStruct((8, 128), F32)],
        in_specs=[pl.BlockSpec((None, T, 128), lambda b: (b, 0, COL_G // 128)), pl.BlockSpec((8, 128), lambda b: (0, 0)),
                  pl.BlockSpec((None, T, 128), lambda b: (b, 0, 0)), pl.BlockSpec((None, T, 128), lambda b: (b, 0, 0))],
        out_specs=[pl.BlockSpec((None, T, 128), lambda b: (b, 0, 0)), pl.BlockSpec((8, 128), lambda b: (0, 0))],
        compiler_params=_cp(vmem=VMEM_BIG),
    )(p, prm, dgf, dgb)


def _lockstep(gens, joint=None):
    n = len(gens)
    res, vals = [None] * n, [None] * n
    while True:
        msgs = []
        for i, g in enumerate(gens):
            try:
                msgs.append((g.send(vals[i]), False))
            except StopIteration as e:
                res[i] = e.value
                msgs.append((None, True))
        if msgs[0][1]:
            return res
        if msgs[0][0] is not None:
            vals = list(joint(tuple(m[0] for m in msgs)))
        else:
            vals = [None] * n


def _tri_inv_gen(m, md, m1, m2, eye):
    n = -(m * md)
    t = eye + n
    pw = n
    for _ in range(3):
        pw = _dg(pw, pw, _NN)
        yield
        t = t + _dg(t, pw, _NN)
        yield
    a = _dg(t, m * m1, _NN)
    yield
    t = t - _dg(a, t, _NN)
    yield
    a = _dg(t, m * m2, _NN)
    yield
    return t - _dg(a, t, _NN)


def _tri_inv_bwd_gen(t, g):
    tt = t.T
    a = _dg(tt, g, _NN)
    yield
    return -_dg(a, tt, _NN)


@jax.custom_vjp
def _tri_inv(ms, md, m1, m2, eye):
    return tuple(_lockstep([_tri_inv_gen(m, md, m1, m2, eye) for m in ms]))


def _tri_inv_f(ms, md, m1, m2, eye):
    ts = _tri_inv(ms, md, m1, m2, eye)
    return ts, (ts, md)


def _tri_inv_b(res, gs):
    ts, md = res
    z = jnp.zeros_like(md)
    return (tuple(_lockstep([_tri_inv_bwd_gen(t, g) for t, g in zip(ts, gs)])), z, z, z, z)


_tri_inv.defvjp(_tri_inv_f, _tri_inv_b)


@jax.custom_vjp
def _tri_inv_saved(ms, ts):
    return tuple(t.astype(F32) for t in ts)


def _tri_inv_saved_f(ms, ts):
    return _tri_inv_saved(ms, ts), ts


def _tri_inv_saved_b(ts, gs):
    dms = tuple(_lockstep([_tri_inv_bwd_gen(t, g) for t, g in zip(ts, gs)]))
    return dms, tuple(jnp.zeros_like(t) for t in ts)


_tri_inv_saved.defvjp(_tri_inv_saved_f, _tri_inv_saved_b)


SR = DNH * CH


def _scan_masks():
    i, j = np.arange(SR)[:, None], np.arange(SR)[None, :]
    same = (i >> 6) == (j >> 6)
    cm = np.zeros((2, 6, SR, SR), np.float32)
    for d in range(2):
        diff = i - j if d == 0 else j - i
        cm[d, 0] = same & (diff >= 0)
        cm[d, 1] = same & (diff > 0)
        cm[d, 2] = (i >> 4) == (j >> 4)
        cm[d, 3] = ((i >> 5) == (j >> 5)) & ((i >> 4) != (j >> 4))
        cm[d, 4] = (i >> 5) != (j >> 5)
        cm[d, 5] = i == j
    t, u = np.arange(CH)[:, None], np.arange(CH)[None, :]
    c64 = np.stack([t >= u, t <= u]).astype(np.float32)
    bm = ((np.arange(SR)[:, None] >> 6) == (np.arange(DNW)[None, :] >> 7)).astype(np.float32)
    return jnp.asarray(cm), jnp.asarray(c64), jnp.asarray(bm)


@jax.custom_vjp
def _fold(x, bm):
    xm = x * bm
    return xm[:, 0:128] + xm[:, 128:256] + xm[:, 256:384] + xm[:, 384:512]


def _fold_f(x, bm):
    return _fold(x, bm), bm


def _fold_b(bm, g):
    return jnp.concatenate([g, g, g, g], axis=1) * bm, jnp.zeros_like(bm)


_fold.defvjp(_fold_f, _fold_b)


def _chunk_gen(q, k, v, gb, s, cm, c64, bm, d):
    incl, strict = cm[0], cm[1]
    lane = lax.broadcasted_iota(jnp.int32, (1, 128), 1)
    oh = lambda n: jnp.where(lane == n, 1.0, 0.0).astype(F32)
    col = lambda x, base: jnp.concatenate(
        [jnp.sum(x * oh(base + d * DNH + h), axis=1, keepdims=True) for h in range(DNH)], axis=0)
    beta = col(gb, 8)
    gam = col(_mask_mm(c64, gb), 0)
    yield
    tot = jnp.sum(gb, axis=0, keepdims=True)
    gtots = [jnp.sum(tot * oh(d * DNH + h), axis=1, keepdims=True) for h in range(DNH)]
    gtot = jnp.concatenate([jnp.broadcast_to(t, (CH, 1)) for t in gtots], axis=0)
    eg = jnp.concatenate([jnp.broadcast_to(jnp.exp(t), (1, 128)) for t in gtots], axis=1)
    row0 = jnp.where(lax.broadcasted_iota(jnp.int32, (128, 1), 0) == 0, 1.0, 0.0).astype(F32)
    gam_row = jnp.sum(jnp.broadcast_to(gam, (SR, 128)).T * row0, axis=0, keepdims=True)
    dec = jnp.exp(jnp.where(incl > 0.5, gam - gam_row, NEG))
    e_gam = jnp.exp(gam)
    kb = k * beta
    kk = _mm_nt(kb, k)
    yield
    tm = yield strict * (kk * dec)
    u = _mm(tm, v * beta)
    yield
    w = _mm(tm, kb * e_gam)
    yield
    qk = _mm_nt(q, k) * dec
    yield
    ws = _mm(w, s)
    yield
    v_new = u - _fold(ws, bm)
    qs = _mm(q * e_gam, s)
    yield
    o = _fold(qs, bm) + _mm(qk, v_new)
    yield
    vbd = jnp.concatenate([v_new, v_new, v_new, v_new], axis=1) * bm
    s_new = s * eg + _mm_tn(k * jnp.exp(gtot - gam), vbd)
    return o, s_new, tm


def _chunks(args, cm_ref, c64_ref, bm, saved=None):
    cm0 = cm_ref[0]
    if saved is None:
        inv = lambda ms: _tri_inv(ms, cm0[2], cm0[3], cm0[4], cm0[5])
    else:
        inv = lambda ms: _tri_inv_saved(ms, saved)
    return _lockstep([_chunk_gen(q, k, v, gb, s, cm_ref[d], c64_ref[d], bm, d) for q, k, v, gb, s, d in args], inv)


def _chunk_rev(s):
    return jnp.where(s < NCH_CTX, NCH_CTX - 1 - s, NCH + NCH_CTX - 1 - s)


def _scan_specs(step):
    cf = lambda s: step(s)
    cb = lambda s: _chunk_rev(step(s))
    tok = lambda c: pl.BlockSpec((NB, CH, DNW), lambda s: (0, c(s), 0))
    gat = lambda c: pl.BlockSpec((NB, CH, 128), lambda s: (0, c(s), 0))
    sta = lambda c: pl.BlockSpec((NB, None, 128, DNW), lambda s: (0, c(s), 0, 0))
    inv = lambda c: pl.BlockSpec((NB, None, SR, SR), lambda s: (0, c(s), 0, 0))
    return (tok(cf), gat(cf), sta(cf), inv(cf)), (tok(cb), gat(cb), sta(cb), inv(cb))


def _const_specs():
    return [pl.BlockSpec((2, 6, SR, SR), lambda s: (0, 0, 0, 0)), pl.BlockSpec((2, CH, CH), lambda s: (0, 0, 0)),
            pl.BlockSpec((SR, DNW), lambda s: (0, 0))]


SCAN_CHAINS = [(b, d) for b in range(NB) for d in range(2)]


def _stack(ref, b):
    return jnp.concatenate([ref[b, :, 128 * h:128 * h + 128] for h in range(DNH)], axis=0)


def _unstack_to(ref, b, val):
    for h in range(DNH):
        ref[b, :, 128 * h:128 * h + 128] = val[CH * h:CH * h + CH]


def _gather_plan(ins, outs, pos):
    me = _slot(pos)
    locals_, xfers = [], []
    for src, dst in zip(ins, outs):
        locals_.append((src, dst.at[me]))
        for k in range(1, 8):
            peer = _flip(pos, k)
            xfers.append((src, dst.at[me], dst.at[_slot(peer)], peer))
    return locals_, xfers


def _scan_fwd(q, k, v, gates, carry=()):
    nc = len(carry)

    def body(*refs):
        cm_ref, c64_ref, bm_ref, qf, kf, vf, gf, qb, kb, vb, gb = refs[:11]
        c_in = refs[11:11 + nc]
        of_ref, sf_ref, tf_ref, ob_ref, sb_ref, tb_ref = refs[11 + nc:17 + nc]
        c_out = refs[17 + nc:17 + 2 * nc]
        s_ref = refs[17 + 2 * nc]
        sems = refs[18 + 2 * nc:]
        if nc:
            @pl.when(pl.program_id(0) == 0)
            def _():
                _xfer_start(_gather_plan(c_in, c_out, _my_pos()), sems)

        @pl.when(pl.program_id(0) == 0)
        def _():
            s_ref[...] = jnp.zeros_like(s_ref)

        bm = bm_ref[...]
        dirs = ((qf, kf, vf, gf, of_ref, sf_ref, tf_ref), (qb, kb, vb, gb, ob_ref, sb_ref, tb_ref))
        args = [(_stack(dirs[d][0], b), _stack(dirs[d][1], b), _stack(dirs[d][2], b), dirs[d][3][b], s_ref[i], d)
                for i, (b, d) in enumerate(SCAN_CHAINS)]
        for i, (o, s_new, tm) in enumerate(_chunks(args, cm_ref, c64_ref, bm)):
            b, d = SCAN_CHAINS[i]
            dirs[d][5][b] = args[i][4]
            dirs[d][6][b] = tm.astype(BF16)
            _unstack_to(dirs[d][4], b, o)
            s_ref[i] = s_new

        if nc:
            @pl.when(pl.program_id(0) == NCH - 1)
            def _():
                _xfer_wait(_gather_plan(c_in, c_out, _my_pos()), sems)

    (tf, gf_, sf_, if_), (tb, gb_, sb_, ib_) = _scan_specs(lambda s: s)
    o_sh = jax.ShapeDtypeStruct((NB, T, DNW), F32)
    st_sh = jax.ShapeDtypeStruct((NB, NCH, 128, DNW), F32)
    inv_sh = jax.ShapeDtypeStruct((NB, NCH, SR, SR), BF16)
    hbm = pl.BlockSpec(memory_space=pltpu.HBM)
    return pl.pallas_call(
        body, name="scan_fwd", grid=(NCH,),
        out_shape=[o_sh, st_sh, inv_sh, o_sh, st_sh, inv_sh] + [jax.ShapeDtypeStruct((8,) + h.shape, h.dtype) for h in carry],
        in_specs=_const_specs() + [tf, tf, tf, gf_, tb, tb, tb, gb_] + [hbm] * nc,
        out_specs=[tf, sf_, if_, tb, sb_, ib_] + [hbm] * nc,
        scratch_shapes=[pltpu.VMEM((len(SCAN_CHAINS), 128, DNW), F32)] + (_xfer_sems(nc, 7 * nc) if nc else []),
        compiler_params=_cp(vmem=VMEM_BIG),
    )(*_scan_masks(), q, k, v, gates, q, k, v, gates, *carry)


def _scan_bwd(q, k, v, gates, st_f, st_b, inv_f, inv_b, do, carry=()):
    nc = len(carry)

    def body(*refs):
        (cm_ref, c64_ref, bm_ref, qf, kf, vf, gf, sf, tf_, dof, qb, kb, vb, gb, sb, tb_, dob) = refs[:17]
        c_in = refs[17:17 + nc]
        dqf, dkf, dvf, dgf, dqb, dkb, dvb, dgb = refs[17 + nc:25 + nc]
        c_out = refs[25 + nc:25 + 2 * nc]
        ds_ref = refs[25 + 2 * nc]
        sems = refs[26 + 2 * nc:]
        first = pl.program_id(0) == 0
        last = pl.program_id(0) == NCH - 1
        if nc:
            @pl.when(first)
            def _():
                _xfer_start(_scatter_plan(c_in, c_out, _my_pos()), sems)

        @pl.when(first)
        def _():
            ds_ref[...] = jnp.zeros_like(ds_ref)

        bm = bm_ref[...]
        dirs = ((qf, kf, vf, gf, sf, dof, dqf, dkf, dvf, dgf), (qb, kb, vb, gb, sb, dob, dqb, dkb, dvb, dgb))
        prim = tuple((_stack(dirs[d][0], b), _stack(dirs[d][1], b), _stack(dirs[d][2], b), dirs[d][3][b], dirs[d][4][b])
                     for b, d in SCAN_CHAINS)
        cots = tuple((_stack(dirs[d][5], b), ds_ref[i]) for i, (b, d) in enumerate(SCAN_CHAINS))
        saved = tuple((tf_, tb_)[d][b] for b, d in SCAN_CHAINS)
        _, vjp = jax.vjp(lambda ps: tuple(r[:2] for r in _chunks([p + (SCAN_CHAINS[i][1],) for i, p in enumerate(ps)],
                                                                cm_ref, c64_ref, bm, saved)), prim)
        (grads,) = vjp(cots)
        for i, (dq, dk, dv, dg, ds) in enumerate(grads):
            b, d = SCAN_CHAINS[i]
            _unstack_to(dirs[d][6], b, dq)
            _unstack_to(dirs[d][7], b, dk)
            _unstack_to(dirs[d][8], b, dv)
            dirs[d][9][b] = dg
            ds_ref[i] = ds

        if nc:
            @pl.when(last)
            def _():
                _xfer_wait(_scatter_plan(c_in, c_out, _my_pos()), sems)

    (tf, gf_, sf_, if_), (tb, gb_, sb_, ib_) = _scan_specs(lambda s: NCH - 1 - s)
    o_sh = jax.ShapeDtypeStruct((NB, T, DNW), F32)
    g_sh = jax.ShapeDtypeStruct((NB, T, 128), F32)
    hbm = pl.BlockSpec(memory_space=pltpu.HBM)
    return pl.pallas_call(
        body, name="scan_bwd", grid=(NCH,),
        out_shape=[o_sh, o_sh, o_sh, g_sh] * 2 + [jax.ShapeDtypeStruct(g.shape, g.dtype) for g in carry],
        in_specs=_const_specs() + [tf, tf, tf, gf_, sf_, if_, tf, tb, tb, tb, gb_, sb_, ib_, tb] + [hbm] * nc,
        out_specs=[tf, tf, tf, gf_, tb, tb, tb, gb_] + [hbm] * nc,
        scratch_shapes=[pltpu.VMEM((len(SCAN_CHAINS), 128, DNW), F32)] + (_xfer_sems(nc, 7 * nc) if nc else []),
        compiler_params=_cp(vmem=VMEM_BIG),
    )(*_scan_masks(), q, k, v, gates, st_f, inv_f, do, q, k, v, gates, st_b, inv_b, do, *carry)


def _segmean(a, bd):
    hi = a.astype(BF16)
    lo = (a - hi.astype(F32)).astype(BF16)
    return jnp.dot(hi, bd, preferred_element_type=F32) + jnp.dot(lo, bd, preferred_element_type=F32)


def _na_prep_fwd(p, qw, kw, bd):
    def body(q_ref, k_ref, v_ref, qw_ref, kw_ref, bd_ref, oq_ref, ok_ref, ov_ref):
        bdm = bd_ref[...]
        q = q_ref[...]
        k = k_ref[...]
        oq_ref[...] = (q * lax.rsqrt(_segmean(q * q, bdm) + EPS) * qw_ref[...] * 0.125).astype(BF16)
        ok_ref[...] = (k * lax.rsqrt(_segmean(k * k, bdm) + EPS) * kw_ref[...]).astype(BF16)
        ov_ref[...] = v_ref[...].astype(BF16)

    ps = lambda o: pl.BlockSpec((None, RT, NAW), lambda b, j: (b, j, COL_NA // NAW + o))
    one = pl.BlockSpec((1, NAW), lambda b, j: (0, 0))
    os_ = pl.BlockSpec((None, RT, NAW), lambda b, j: (b, j, 0))
    return pl.pallas_call(
        body, name="na_prep_fwd", grid=(NB, T // RT), out_shape=[jax.ShapeDtypeStruct((NB, T, NAW), BF16)] * 3,
        in_specs=[ps(0), ps(1), ps(2), one, one, pl.BlockSpec((NAW, NAW), lambda b, j: (0, 0))], out_specs=[os_] * 3,
    )(p, p, p, qw, kw, bd)


def _na_prep_bwd(p, qw, kw, bd, dq, dk, dv):
    def body(q_ref, k_ref, qw_ref, kw_ref, bd_ref, dq_ref, dk_ref, dv_ref, o_ref, gq_ref, gk_ref):
        b, j = pl.program_id(0), pl.program_id(1)
        bdm = bd_ref[...]

        def one(x, w, dy, scale):
            r = lax.rsqrt(_segmean(x * x, bdm) + EPS)
            g = dy * w * scale
            dx = r * g - x * (r * r * r) * _segmean(g * x, bdm)
            dw = jnp.sum(dy * x * r * scale, axis=0, keepdims=True)
            return dx, dw

        dqv = jnp.where(j == 0, 0.0, dq_ref[...])
        dxq, dwq = one(q_ref[...], qw_ref[...], dqv, 0.125)
        dxk, dwk = one(k_ref[...], kw_ref[...], dk_ref[...], 1.0)
        o_ref[...] = jnp.concatenate([dxq, dxk, dv_ref[...]], axis=1)
        pad = jnp.zeros((7, NAW), F32)

        @pl.when((b == 0) & (j == 0))
        def _():
            gq_ref[...] = jnp.zeros_like(gq_ref)
            gk_ref[...] = jnp.zeros_like(gk_ref)

        gq_ref[...] += jnp.concatenate([dwq, pad], axis=0)
        gk_ref[...] += jnp.concatenate([dwk, pad], axis=0)

    ps = lambda o: pl.BlockSpec((None, RT, NAW), lambda b, j: (b, j, COL_NA // NAW + o))
    one = pl.BlockSpec((1, NAW), lambda b, j: (0, 0))
    ts = pl.BlockSpec((None, RT, NAW), lambda b, j: (b, j, 0))
    acc = pl.BlockSpec((8, NAW), lambda b, j: (0, 0))
    return pl.pallas_call(
        body, name="na_prep_bwd", grid=(NB, T // RT),
        out_shape=[jax.ShapeDtypeStruct((NB, T, 1536), F32), jax.ShapeDtypeStruct((8, NAW), F32),
                   jax.ShapeDtypeStruct((8, NAW), F32)],
        in_specs=[ps(0), ps(1), one, one, pl.BlockSpec((NAW, NAW), lambda b, j: (0, 0)),
                  pl.BlockSpec((None, RT, NAW), lambda b, j: (b, jnp.maximum(j - 1, 0), 0)), ts, ts],
        out_specs=[pl.BlockSpec((None, RT, 1536), lambda b, j: (b, j, 0)), acc, acc],
        compiler_params=_cp(vmem=VMEM_BIG),
    )(p, p, qw, kw, bd, dq, dk, dv)


def _na_row0(r):
    return jnp.clip(r - WIN_ROWS // 2, 0, ROWS - WIN_ROWS)


def _na_cls(r):
    return _na_row0(r) - r + WIN_ROWS - 1


NHS = 4
NLW = NHS * 64
NSR = NHS * GRID_W


def _na_stack(x):
    lane = lax.broadcasted_iota(jnp.int32, (1, NLW), 1)
    return jnp.concatenate([jnp.where((lane >> 6) == h, x, jnp.zeros_like(x)) for h in range(NHS)], axis=0)


def _na_unstack(y):
    lane = lax.broadcasted_iota(jnp.int32, (1, NLW), 1)
    out = jnp.zeros((GRID_W, NLW), y.dtype)
    for h in range(NHS):
        out = jnp.where((lane >> 6) == h, y[GRID_W * h:GRID_W * h + GRID_W], out)
    return out


def _na_fwd_gen(q, kw, vw, kc, vc, bias):
    q4 = _na_stack(q)
    s_w = lax.dot_general(q4, kw, _NT, preferred_element_type=F32) + bias
    yield
    s_c = lax.dot_general(q4, kc, _NT, preferred_element_type=F32)
    yield
    mx = jnp.maximum(jnp.max(s_w, axis=1, keepdims=True), jnp.max(s_c, axis=1, keepdims=True))
    p_w = jnp.exp(s_w - mx)
    p_c = jnp.exp(s_c - mx)
    inv = 1.0 / (jnp.sum(p_w, axis=1, keepdims=True) + jnp.sum(p_c, axis=1, keepdims=True))
    yield
    o_w = jnp.dot(p_w.astype(BF16), vw, preferred_element_type=F32)
    yield
    o4 = (o_w + jnp.dot(p_c.astype(BF16), vc, preferred_element_type=F32)) * inv
    return _na_unstack(o4)


def _na_fwd(qn, kn, vb, bias, carry=()):
    nc = len(carry)
    grid = (NB, ROWS)
    ng = NAH // NHS

    def body(*refs):
        q_ref, k_ref, v_ref, b_ref = refs[:4]
        c_in, o_ref, c_out, sems = refs[4:4 + nc], refs[4 + nc], refs[5 + nc:5 + 2 * nc], refs[5 + 2 * nc:]
        ids = [pl.program_id(a) for a in range(2)]
        if nc:
            @pl.when((ids[0] == 0) & (ids[1] == 0))
            def _():
                _xfer_start(_gather_plan(c_in, c_out, _my_pos()), sems)

        r = ids[1]
        start = pl.multiple_of(CT + _na_row0(r) * GRID_W, GRID_W)
        gens = []
        for g in range(ng):
            ln = slice(NLW * g, NLW * g + NLW)
            gens.append(_na_fwd_gen(q_ref[:, ln], k_ref[pl.ds(start, NWIN), ln], v_ref[pl.ds(start, NWIN), ln],
                                    k_ref[0:CT, ln], v_ref[0:CT, ln], b_ref[g]))
        for g, o in enumerate(_lockstep(gens)):
            o_ref[:, NLW * g:NLW * g + NLW] = o.astype(BF16)

        if nc:
            @pl.when((ids[0] == grid[0] - 1) & (ids[1] == grid[1] - 1))
            def _():
                _xfer_wait(_gather_plan(c_in, c_out, _my_pos()), sems)

    qs = pl.BlockSpec((None, GRID_W, NAW), lambda b, r: (b, CT // GRID_W + r, 0))
    kv = pl.BlockSpec((None, T, NAW), lambda b, r: (b, 0, 0))
    bs = pl.BlockSpec((None, ng, NSR, NWIN), lambda b, r: (_na_cls(r), 0, 0, 0))
    lat = pl.BlockSpec((None, GRID_W, NAW), lambda b, r: (b, r, 0))
    hbm = pl.BlockSpec(memory_space=pltpu.HBM)
    res = pl.pallas_call(
        body, name="na_fwd", grid=grid,
        out_shape=[jax.ShapeDtypeStruct((NB, SEQ, NAW), BF16)] + [jax.ShapeDtypeStruct((8,) + h.shape, h.dtype) for h in carry],
        in_specs=[qs, kv, kv, bs] + [hbm] * nc, out_specs=[lat] + [hbm] * nc,
        scratch_shapes=_xfer_sems(nc, 7 * nc) if nc else [],
        compiler_params=_cp(vmem=VMEM_BIG),
    )(qn, kn, vb, bias, *carry)
    return res[0], list(res[1:])


def _na_bwd_gen(q, kw, vw, kc, vc, bias, do):
    q4 = _na_stack(q)
    s_w = lax.dot_general(q4, kw, _NT, preferred_element_type=F32) + bias
    yield
    s_c = lax.dot_general(q4, kc, _NT, preferred_element_type=F32)
    yield
    mx = jnp.maximum(jnp.max(s_w, axis=1, keepdims=True), jnp.max(s_c, axis=1, keepdims=True))
    p_w = jnp.exp(s_w - mx)
    p_c = jnp.exp(s_c - mx)
    inv = 1.0 / (jnp.sum(p_w, axis=1, keepdims=True) + jnp.sum(p_c, axis=1, keepdims=True))
    p_w = p_w * inv
    p_c = p_c * inv
    do4 = _na_stack(do.astype(BF16))
    dp_w = lax.dot_general(do4, vw, _NT, preferred_element_type=F32)
    yield
    dp_c = lax.dot_general(do4, vc, _NT, preferred_element_type=F32)
    yield
    delta = jnp.sum(p_w * dp_w, axis=1, keepdims=True) + jnp.sum(p_c * dp_c, axis=1, keepdims=True)
    ds_w = p_w * (dp_w - delta)
    ds_c = p_c * (dp_c - delta)
    ds_wb, ds_cb = ds_w.astype(BF16), ds_c.astype(BF16)
    dq4 = jnp.dot(ds_wb, kw, preferred_element_type=F32)
    yield
    dq4 = dq4 + jnp.dot(ds_cb, kc, preferred_element_type=F32)
    yield
    dkw = lax.dot_general(ds_wb, q4, _TN, preferred_element_type=F32)
    yield
    dvw = lax.dot_general(p_w.astype(BF16), do4, _TN, preferred_element_type=F32)
    yield
    dkc = lax.dot_general(ds_cb, q4, _TN, preferred_element_type=F32)
    yield
    dvc = lax.dot_general(p_c.astype(BF16), do4, _TN, preferred_element_type=F32)
    return ds_w, _na_unstack(dq4), dkw, dvw, dkc, dvc


NRS = 2


def _na_bwd(qn, kn, vb, bias, do):
    def body(q_ref, k_ref, v_ref, *rest):
        b_refs, (do_ref, dq_ref, dk_ref, dv_ref, db_ref) = rest[:NRS], rest[NRS:]
        b, j = pl.program_id(1), pl.program_id(2)

        @pl.when((b == 0) & (j == 0))
        def _():
            db_ref[...] = jnp.zeros_like(db_ref)

        @pl.when(j == 0)
        def _():
            dk_ref[...] = jnp.zeros_like(dk_ref)
            dv_ref[...] = jnp.zeros_like(dv_ref)

        kc, vc = k_ref[0:CT, :], v_ref[0:CT, :]
        rows = [NRS * j + i for i in range(NRS)]
        starts = [pl.multiple_of(CT + _na_row0(r) * GRID_W, GRID_W) for r in rows]
        gens = [_na_bwd_gen(q_ref[GRID_W * i:GRID_W * (i + 1), :], k_ref[pl.ds(starts[i], NWIN), :],
                            v_ref[pl.ds(starts[i], NWIN), :], kc, vc, b_refs[i][...],
                            do_ref[GRID_W * i:GRID_W * (i + 1), :]) for i in range(NRS)]
        for i, (ds_w, dq, dkw, dvw, dkc, dvc) in enumerate(_lockstep(gens)):
            db_ref[_na_cls(rows[i])] += ds_w
            dq_ref[GRID_W * i:GRID_W * (i + 1), :] = dq
            dk_ref[pl.ds(starts[i], NWIN), :] += dkw
            dv_ref[pl.ds(starts[i], NWIN), :] += dvw
            dk_ref[0:CT, :] += dkc
            dv_ref[0:CT, :] += dvc

    qs = pl.BlockSpec((None, NRS * GRID_W, NLW), lambda g, b, j: (b, CT // (NRS * GRID_W) + j, g))
    kv = pl.BlockSpec((None, T, NLW), lambda g, b, j: (b, 0, g))
    bss = [pl.BlockSpec((None, None, NSR, NWIN), lambda g, b, j, i=i: (_na_cls(NRS * j + i), g, 0, 0)) for i in range(NRS)]
    lat = pl.BlockSpec((None, NRS * GRID_W, NLW), lambda g, b, j: (b, j, g))
    return pl.pallas_call(
        body, name="na_bwd", grid=(NAH // NHS, NB, ROWS // NRS),
        out_shape=[jax.ShapeDtypeStruct((NB, SEQ, NAW), F32)] + [jax.ShapeDtypeStruct((NB, T, NAW), F32)] * 2
        + [jax.ShapeDtypeStruct((WIN_ROWS, NAH // NHS, NSR, NWIN), F32)],
        in_specs=[qs, kv, kv] + bss + [lat],
        out_specs=[lat, kv, kv, pl.BlockSpec((WIN_ROWS, None, NSR, NWIN), lambda g, b, j: (0, g, 0, 0))],
        compiler_params=_cp(vmem=VMEM_BIG),
    )(qn, kn, vb, *([bias] * NRS), do)


def _na_tables():
    cols = np.arange(GRID_W)
    win_start = np.clip(cols - WIN_COLS // 2, 0, GRID_W - WIN_COLS)
    kc = cols[None, :]
    valid = (kc >= win_start[:, None]) & (kc < win_start[:, None] + WIN_COLS)
    rel = np.clip(kc - cols[:, None] + WIN_COLS - 1, 0, 2 * WIN_COLS - 2)
    return valid, rel


def _bias_table(rpb):
    valid, rel = _na_tables()
    nrel = 2 * WIN_COLS - 1
    shifts = np.stack([(rel == r) & valid for r in range(nrel)]).astype(np.float32)
    neg = np.where(valid, 0.0, NEG).astype(np.float32)

    def body(rpb_ref, sh_ref, neg_ref, o_ref):
        h = pl.program_id(0)
        for dr in range(2 * WIN_ROWS - 1):
            tile = neg_ref[...]
            for r in range(nrel):
                tile = tile + rpb_ref[h, dr, r] * sh_ref[r]
            for cls in range(WIN_ROWS):
                i = dr - cls
                if 0 <= i < WIN_ROWS:
                    o_ref[cls, :, i * GRID_W:(i + 1) * GRID_W] = tile

    return pl.pallas_call(
        body, name="bias_table", grid=(NAH,),
        out_shape=jax.ShapeDtypeStruct((WIN_ROWS, NAH // NHS, NSR, NWIN), F32),
        in_specs=[pl.BlockSpec(memory_space=pltpu.SMEM), pl.BlockSpec((nrel, GRID_W, GRID_W), lambda h: (0, 0, 0)),
                  pl.BlockSpec((GRID_W, GRID_W), lambda h: (0, 0))],
        out_specs=pl.BlockSpec((WIN_ROWS, None, GRID_W, NWIN), lambda h: (0, h // NHS, h % NHS, 0)),
    )(rpb, jnp.asarray(shifts), jnp.asarray(neg))


def _rpb_fold(db):
    def body1(d_ref, o_ref):
        for dr in range(2 * WIN_ROWS - 1):
            acc = None
            for c in range(WIN_ROWS):
                i = dr - c
                if 0 <= i < WIN_ROWS:
                    blk = d_ref[c, :, i * GRID_W:(i + 1) * GRID_W]
                    acc = blk if acc is None else acc + blk
            o_ref[dr] = acc

    t15 = pl.pallas_call(
        body1, name="rpb_fold_rows", grid=(NAH,),
        out_shape=jax.ShapeDtypeStruct((NAH, 2 * WIN_ROWS - 1, GRID_W, GRID_W), F32),
        in_specs=[pl.BlockSpec((WIN_ROWS, None, GRID_W, NWIN), lambda h: (0, h, 0, 0))],
        out_specs=pl.BlockSpec((None, 2 * WIN_ROWS - 1, GRID_W, GRID_W), lambda h: (h, 0, 0, 0)),
        compiler_params=_cp(vmem=VMEM_BIG),
    )(db.reshape(WIN_ROWS, NAH, GRID_W, NWIN))
    valid, rel = _na_tables()
    onehot = np.zeros((GRID_W * GRID_W, 128), np.float32)
    flat_rel, flat_valid = rel.reshape(-1), valid.reshape(-1)
    onehot[np.arange(GRID_W * GRID_W)[flat_valid], flat_rel[flat_valid]] = 1.0

    def body2(a_ref, oh_ref, o_ref):
        o_ref[...] = _hmm(a_ref[...], oh_ref[...])

    vm = pl.BlockSpec(memory_space=pltpu.VMEM)
    out = pl.pallas_call(
        body2, name="rpb_fold_cols", out_shape=jax.ShapeDtypeStruct((NAH * 15, 128), F32),
        in_specs=[vm, vm], out_specs=vm, compiler_params=_cp(vmem=VMEM_BIG),
    )(t15.reshape(NAH * 15, GRID_W * GRID_W), jnp.asarray(onehot))
    return out[:, :31].reshape(NAH, 15, 31)


def _dn_out(o, z, w):
    outs = []
    for h in range(DNH):
        sl = slice(128 * h, 128 * h + 128)
        oh = o[:, sl]
        r = lax.rsqrt(jnp.mean(oh * oh, axis=-1, keepdims=True) + EPS)
        outs.append(oh * r * w[:, sl] * _silu(z[:, sl]))
    return jnp.concatenate(outs, axis=1)


def _outproj_fwd(of, ob, p, na, onw, x, g1, w_out):
    def body(of_ref, ob_ref, z_ref, na_ref, w_ref, x_ref, g_ref, wo_ref, x1_ref, ai_ref, ao_ref):
        dn = _dn_out(of_ref[...] + ob_ref[...], z_ref[...], w_ref[...])
        ai = jnp.concatenate([dn.astype(BF16), na_ref[...]], axis=1)
        ai_ref[...] = ai
        ao = jnp.dot(ai, wo_ref[...], preferred_element_type=F32)
        ao_ref[...] = ao
        x1_ref[...] = x_ref[...] + g_ref[...] * ao

    lat = lambda w: pl.BlockSpec((None, RT, w), lambda b, j: (b, j, 0))
    return pl.pallas_call(
        body, name="outproj_fwd", grid=(NB, SEQ // RT),
        out_shape=[jax.ShapeDtypeStruct((NB, SEQ, D), F32), jax.ShapeDtypeStruct((NB, SEQ, D), BF16),
                   jax.ShapeDtypeStruct((NB, SEQ, D), F32)],
        in_specs=[pl.BlockSpec((None, RT, DNW), lambda b, j: (b, j + 1, 0)),
                  pl.BlockSpec((None, RT, DNW), lambda b, j: (b, j + 1, 0)),
                  pl.BlockSpec((None, RT, DNW), lambda b, j: (b, j + 1, COL_Z // DNW)),
                  lat(NAW), pl.BlockSpec((1, DNW), lambda b, j: (0, 0)), lat(D),
                  pl.BlockSpec((None, 1, D), lambda b, j: (b, 0, 0)), pl.BlockSpec((D, D), lambda b, j: (0, 0))],
        out_specs=[lat(D), lat(D), lat(D)],
        compiler_params=_cp(vmem=VMEM_BIG),
    )(of, ob, p, na, onw, x, g1, w_out)


def _outproj_bwd(dx1, ao, g1, w_out, of, ob, p, onw):
    def body(dx_ref, ao_ref, g_ref, wo_ref, of_ref, ob_ref, z_ref, w_ref, dao_ref, do_ref, dz_ref, dna_ref, dg_ref, dw_ref):
        b, j = pl.program_id(0), pl.program_id(1)

        @pl.when((b == 0) & (j == 0))
        def _():
            dw_ref[...] = jnp.zeros_like(dw_ref)

        @pl.when(j == 0)
        def _():
            do_ref[...] = jnp.zeros_like(do_ref)
            dz_ref[...] = jnp.zeros_like(dz_ref)

        @pl.when(j == 1)
        def _():
            dg_ref[...] = jnp.zeros_like(dg_ref)

        @pl.when(j >= 1)
        def _():
            dx = dx_ref[...]
            dg_ref[...] += jnp.sum(dx * ao_ref[...], axis=0, keepdims=True)
            dao = (g_ref[...] * dx).astype(BF16)
            dao_ref[...] = dao
            dai = lax.dot_general(dao, wo_ref[...], _NT, preferred_element_type=F32)
            dna_ref[...] = dai[:, DNW:]
            _, vjp = jax.vjp(_dn_out, of_ref[...] + ob_ref[...], z_ref[...], w_ref[...])
            do, dz, dw = vjp(dai[:, :DNW])
            do_ref[...] = do
            dz_ref[...] = dz
            dw_ref[...] += jnp.concatenate([dw, jnp.zeros((7, DNW), F32)], axis=0)

    lat = lambda w: pl.BlockSpec((None, RT, w), lambda b, j: (b, jnp.maximum(j - 1, 0), 0))
    tok = lambda w: pl.BlockSpec((None, RT, w), lambda b, j: (b, j, 0))
    return pl.pallas_call(
        body, name="outproj_bwd", grid=(NB, T // RT),
        out_shape=[jax.ShapeDtypeStruct((NB, SEQ, D), BF16), jax.ShapeDtypeStruct((NB, T, DNW), F32),
                   jax.ShapeDtypeStruct((NB, T, DNW), F32), jax.ShapeDtypeStruct((NB, SEQ, NAW), F32),
                   jax.ShapeDtypeStruct((NB, 1, D), F32), jax.ShapeDtypeStruct((8, DNW), F32)],
        in_specs=[_rows2d_spec(D), lat(D), pl.BlockSpec((None, 1, D), lambda b, j: (b, 0, 0)),
                  pl.BlockSpec((D, D), lambda b, j: (0, 0)),
                  tok(DNW), tok(DNW),
                  pl.BlockSpec((None, RT, DNW), lambda b, j: (b, j, COL_Z // DNW)),
                  pl.BlockSpec((1, DNW), lambda b, j: (0, 0))],
        out_specs=[lat(D), tok(DNW), tok(DNW), lat(NAW), pl.BlockSpec((None, 1, D), lambda b, j: (b, 0, 0)),
                   pl.BlockSpec((8, DNW), lambda b, j: (0, 0))],
        compiler_params=_cp(vmem=VMEM_BIG),
    )(dx1, ao, g1, w_out, of, ob, p, onw)


FT_FWD, FT_BWD = 512, 256
NFF = DFF // FFT


def _ffn_specs(ft):
    row = lambda w: pl.BlockSpec((ft, w), lambda i, f: (i, 0))
    ex = pl.BlockSpec((None, 1, D), lambda i, f: (i // (SEQ // ft), 0, 0))
    one = pl.BlockSpec((1, D), lambda i, f: (0, 0))
    wg = pl.BlockSpec((None, D, FFT), lambda i, f: (f, 0, 0))
    wu = pl.BlockSpec((None, D, FFT), lambda i, f: (NFF + f, 0, 0))
    wo = pl.BlockSpec((FFT, D), lambda i, f: (f, 0))
    ff = pl.BlockSpec((ft, FFT), lambda i, f: (i, f))
    return row, ex, one, wg, wu, wo, ff


def _ffn_fwd(x1, sh2, sc2, g2, n2w, w_in4, w_out, target):
    ft, tpe = FT_FWD, SEQ // FT_FWD

    def body(x_ref, sh_ref, sc_ref, g_ref, w2_ref, wg_ref, wu_ref, wo_ref, t_ref,
             h2_ref, gt_ref, up_ref, dy_ref, dg_ref, loss_ref, acc_ref):
        i, f = pl.program_id(0), pl.program_id(1)

        @pl.when(f == 0)
        def _():
            h2_ref[...] = _norm_mod(x_ref[...], w2_ref[...], sh_ref[...], sc_ref[...]).astype(BF16)
            acc_ref[...] = jnp.zeros_like(acc_ref)

        h2 = h2_ref[...]
        gate = jnp.dot(h2, wg_ref[...], preferred_element_type=F32)
        up = jnp.dot(h2, wu_ref[...], preferred_element_type=F32)
        gt_ref[...] = gate.astype(BF16)
        up_ref[...] = up.astype(BF16)
        acc_ref[...] += jnp.dot((_silu(gate) * up).astype(BF16), wo_ref[...], preferred_element_type=F32)

        @pl.when((i == 0) & (f == 0))
        def _():
            loss_ref[...] = jnp.zeros_like(loss_ref)

        @pl.when(f == NFF - 1)
        def _():
            ffn = acc_ref[...]
            err = x_ref[...] + g_ref[...] * ffn - t_ref[...]
            dy = err * (1.0 / D)
            dy_ref[...] = dy
            part = jnp.sum(dy * ffn, axis=0, keepdims=True)
            loss_ref[...] += jnp.sum(jnp.sum(err * err, axis=1, keepdims=True), axis=0, keepdims=True) * (0.5 / D)

            @pl.when(i % tpe == 0)
            def _():
                dg_ref[...] = part

            @pl.when(i % tpe != 0)
            def _():
                dg_ref[...] += part

    row, ex, one, wg, wu, wo, ff = _ffn_specs(ft)
    return pl.pallas_call(
        body, name="ffn_fwd", grid=(NTOK // ft, NFF),
        out_shape=[jax.ShapeDtypeStruct((NTOK, D), BF16), jax.ShapeDtypeStruct((NTOK, DFF), BF16),
                   jax.ShapeDtypeStruct((NTOK, DFF), BF16), jax.ShapeDtypeStruct((NTOK, D), F32),
                   jax.ShapeDtypeStruct((NB, 1, D), F32), jax.ShapeDtypeStruct((8, 128), F32)],
        in_specs=[row(D), ex, ex, ex, one, wg, wu, wo, row(D)],
        out_specs=[row(D), ff, ff, row(D), ex, pl.BlockSpec((8, 128), lambda i, f: (0, 0))],
        scratch_shapes=[pltpu.VMEM((ft, D), F32)],
        compiler_params=_cp(vmem=VMEM_BIG),
    )(x1, sh2, sc2, g2, n2w, w_in4, w_in4, w_out, target)


def _rows2d_spec(width):
    return pl.BlockSpec((RT, width), lambda b, j: (b * (SEQ // RT) + jnp.maximum(j - 1, 0), 0))


def _ffn_bwd(dy, gate, up, x1, sh2, sc2, g2, n2w, w_in4, w_out):
    ft, tpe = FT_BWD, SEQ // FT_BWD
    ni = NTOK // ft

    def body(dy_ref, gt_ref, up_ref, x_ref, sh_ref, sc_ref, g_ref, w2_ref, wg_ref, wu_ref, wo_ref,
             dff_ref, act_ref, dgu_ref, dx_ref, dsh_ref, dsc_ref, dw_ref, acc_ref):
        f, i = pl.program_id(0), pl.program_id(1)
        rows = pl.ds(pl.multiple_of(i * ft, ft), ft)
        dff = (g_ref[...] * dy_ref[...]).astype(BF16)
        dff_ref[...] = dff
        dact = lax.dot_general(dff, wo_ref[...], _NT, preferred_element_type=F32)
        gate, up = gt_ref[...].astype(F32), up_ref[...].astype(F32)
        sg = _sigmoid(gate)
        sl = gate * sg
        act_ref[...] = (sl * up).astype(BF16)
        dgate = (dact * up * (sg * (1.0 + gate * (1.0 - sg)))).astype(BF16)
        dup = (dact * sl).astype(BF16)
        dgu_ref[0] = dgate
        dgu_ref[1] = dup
        part = (lax.dot_general(dgate, wg_ref[...], _NT, preferred_element_type=F32)
                + lax.dot_general(dup, wu_ref[...], _NT, preferred_element_type=F32))

        @pl.when(f == 0)
        def _():
            acc_ref[rows, :] = part

        @pl.when(f > 0)
        def _():
            acc_ref[rows, :] += part

        @pl.when((i == 0) & (f == 0))
        def _():
            dw_ref[...] = jnp.zeros_like(dw_ref)

        @pl.when(f == NFF - 1)
        def _():
            _, vjp = jax.vjp(_norm_mod, x_ref[...], w2_ref[...], sh_ref[...], sc_ref[...])
            dx, dw, dsh, dsc = vjp(acc_ref[rows, :])
            dx_ref[...] = dy_ref[...] + dx
            dw_ref[...] += dw

            @pl.when(i % tpe == 0)
            def _():
                dsh_ref[...] = dsh
                dsc_ref[...] = dsc

            @pl.when(i % tpe != 0)
            def _():
                dsh_ref[...] += dsh
                dsc_ref[...] += dsc

    last = lambda f, real, spare: jnp.where(f == NFF - 1, real, spare)
    row = lambda w: pl.BlockSpec((ft, w), lambda f, i: (i, 0))
    ex = pl.BlockSpec((None, 1, D), lambda f, i: (i // tpe, 0, 0))
    one = pl.BlockSpec((1, D), lambda f, i: (0, 0))
    wg = pl.BlockSpec((None, D, FFT), lambda f, i: (f, 0, 0))
    wu = pl.BlockSpec((None, D, FFT), lambda f, i: (NFF + f, 0, 0))
    wo = pl.BlockSpec((FFT, D), lambda f, i: (f, 0))
    ff = pl.BlockSpec((ft, FFT), lambda f, i: (i, f))
    row_once = lambda w: pl.BlockSpec((ft, w), lambda f, i: (last(f, i, ni), 0))
    ex_once = pl.BlockSpec((None, 1, D), lambda f, i: (last(f, i // tpe, NB), 0, 0))
    return pl.pallas_call(
        body, name="ffn_bwd", grid=(NFF, ni),
        out_shape=[jax.ShapeDtypeStruct((NTOK + ft, D), BF16), jax.ShapeDtypeStruct((NTOK, DFF), BF16),
                   jax.ShapeDtypeStruct((2, NTOK, DFF), BF16), jax.ShapeDtypeStruct((NTOK + ft, D), F32),
                   jax.ShapeDtypeStruct((NB + 1, 1, D), F32), jax.ShapeDtypeStruct((NB + 1, 1, D), F32),
                   jax.ShapeDtypeStruct((1, D), F32)],
        in_specs=[row(D), ff, ff, row(D), ex, ex, ex, one, wg, wu, wo],
        out_specs=[row_once(D), ff, pl.BlockSpec((2, ft, FFT), lambda f, i: (0, i, f)), row_once(D), ex_once, ex_once, one],
        scratch_shapes=[pltpu.VMEM((NTOK, D), F32)],
        compiler_params=_cp(vmem=VMEM_BIG),
    )(dy, gate, up, x1, sh2, sc2, g2, n2w, w_in4, w_in4, w_out)


def _local_step(x, ctx, target, mod_x, mod_c, norm1_w, norm2_w, w_in_p, conv_w, a_log, dt_bias, out_norm_w,
                q_norm_w, k_norm_w, rpb, w_out_b, w_ffn_in4, w_ffn_out_b, scatter_early=False, gather_late=None):
    sh1, sc1, g1, sh2, sc2, g2 = [mod_x[:, i * D:(i + 1) * D].reshape(NB, 1, D) for i in range(6)]
    csh1, csc1 = mod_c[0:D].reshape(1, 1, D), mod_c[D:2 * D].reshape(1, 1, D)
    cw = jnp.concatenate([conv_w, jnp.zeros((3, 1536), F32)], axis=0)
    prm = jnp.pad(jnp.stack([a_log.reshape(8), dt_bias.reshape(8)]), ((0, 6), (0, 120)))
    onw = jnp.tile(out_norm_w.reshape(1, 128), (1, DNH))
    qw = jnp.tile(q_norm_w.reshape(1, 64), (1, NAH))
    kw = jnp.tile(k_norm_w.reshape(1, 64), (1, NAH))
    seg = np.arange(NAW) // 64
    bd = jnp.asarray((seg[:, None] == seg[None, :]).astype(np.float32) / 64.0, dtype=BF16)
    bias = _bias_table(rpb)

    p, hm = _inproj_fwd(x, ctx, sh1, sc1, csh1, csc1, norm1_w, w_in_p)
    dq_, dk_, dv_ = _dn_prep_fwd(p, cw)
    gates = _gates_fwd(p, prm)
    res = _scan_fwd(dq_, dk_, dv_, gates, carry=gather_late[:2] if gather_late else ())
    o_f, st_f, inv_f, o_b, st_b, inv_b = res[:6]
    qn, kn, vb = _na_prep_fwd(p, qw, kw, bd)
    na, got = _na_fwd(qn, kn, vb, bias, carry=gather_late[2:] if gather_late else ())
    if gather_late:
        w_out_b = res[6].reshape(D, D)
        w_ffn_in4 = res[7].reshape(4, D, FFT)
        w_ffn_out_b = got[0].reshape(DFF, D)
    x1, ai, ao = _outproj_fwd(o_f, o_b, p, na, onw, x, g1, w_out_b)
    x1f = x1.reshape(NTOK, D)
    h2, gate, up, dy, dg2, loss8 = _ffn_fwd(x1f, sh2, sc2, g2, norm2_w, w_ffn_in4, w_ffn_out_b,
                                             target.reshape(NTOK, D))
    dff, act, dgu, dx1, dsh2, dsc2, dn2w = _ffn_bwd(dy, gate, up, x1f, sh2, sc2, g2, norm2_w, w_ffn_in4, w_ffn_out_b)
    g_w_ffn_out = _matmul_tn(act, dff, FFT, D, 512, "wgrad_ffn_out")
    g_w_ffn_in4 = _matmul_tn(
        h2, dgu, D, FFT, 1024, "wgrad_ffn_in", nn=2 * NFF,
        b_block=(None, 1024, FFT), b_map=lambda i, j, k: (j // NFF, k, j % NFF),
        out_shape=(4, D, FFT), out_block=(None, D, FFT), out_map=lambda i, j, k: (j, i, 0))
    dsh2, dsc2 = dsh2[:NB], dsc2[:NB]
    dao, do, dz, dna, dg1, donw = _outproj_bwd(dx1, ao, g1, w_out_b, o_f, o_b, p, onw)
    g_w_out = _matmul_tn(ai.reshape(NTOK, D), dao.reshape(NTOK, D), D, D, 1024, "wgrad_out")
    dqn, dkn, dvn, dbias = _na_bwd(qn, kn, vb, bias, dna)
    dp_na, dqw, dkw = _na_prep_bwd(p, qw, kw, bd, dqn, dkn, dvn)
    early = [g_w_out.reshape(8, D // 8, D), g_w_ffn_in4.reshape(8, D // 2, FFT), g_w_ffn_out.reshape(8, DFF // 8, D)]
    res = _scan_bwd(dq_, dk_, dv_, gates, st_f, st_b, inv_f, inv_b, do, carry=early if scatter_early else ())
    dqf, dkf, dvf, dgf, dqb, dkb, dvb, dgb = res[:8]
    if scatter_early:
        g_w_out, g_w_ffn_in4, g_w_ffn_out = res[8:]
    dpq, dpk, dpv, gcq, gck, gcv = _dn_prep_bwd(p, cw, (dqf, dqb), (dkf, dkb), (dvf, dvb))
    dgt, dprm = _gates_bwd(p, prm, dgf, dgb)
    dpb, grad_x, dsh1, dsc1, dcsh1, dcsc1, dn1w = _inproj_bwd(
        (dpq, dpk, dpv), dz, dp_na, dgt, w_in_p, x, ctx, sh1, sc1, csh1, csc1, norm1_w, dx1)
    g_w_in_p = _matmul_tn(dpb.reshape(NB * T, PC), hm.reshape(NB * T, D), PC, D, 512, "wgrad_in")
    g_rpb = _rpb_fold(dbias)

    dmod_x = jnp.concatenate([dsh1, dsc1, dg1, dsh2, dsc2, dg2], axis=2).reshape(NB, 6 * D)
    dmod_c = jnp.concatenate([dcsh1.reshape(D), dcsc1.reshape(D), jnp.zeros((4 * D,), F32)])
    g_conv = jnp.concatenate([gcq[:5], gck[:5], gcv[:5]], axis=1)
    small = dict(
        norm1_w=dn1w, norm2_w=dn2w, a_log=dprm[0, :8], dt_bias=dprm[1, :8],
        out_norm_w=donw[0].reshape(DNH, 128).sum(0), q_norm_w=dqw[0].reshape(NAH, 64).sum(0),
        k_norm_w=dkw[0].reshape(NAH, 64).sum(0), rpb=g_rpb, conv_w=g_conv)
    return loss8[0, 0], grad_x, dmod_x, dmod_c, small, g_w_in_p, g_w_out, g_w_ffn_in4, g_w_ffn_out


def _perm_w_in(wt):
    return jnp.concatenate([wt[:2048], wt[2064:3600], wt[2048:2064], jnp.zeros((PC - IN_COLS, wt.shape[1]), wt.dtype)],
                           axis=0)


def _unperm_w_in(gt):
    return jnp.concatenate([gt[:2048], gt[COL_G:COL_G + 16], gt[2048:COL_G]], axis=0)


def _misc_row(a_log, dt_bias, onw, qnw, knw):
    return jnp.concatenate([a_log.reshape(8), dt_bias.reshape(8), jnp.zeros((112,), F32), onw.reshape(128),
                            qnw.reshape(64), knw.reshape(64), jnp.zeros((D - 384,), F32)])


def _pack_small(c_ctx, n1, n2, b_ada, a_log, dt_bias, onw, qnw, knw, rpb, conv_s):
    misc = _misc_row(a_log, dt_bias, onw, qnw, knw)
    rp = jnp.concatenate([rpb.reshape(-1), jnp.zeros((4 * D - 3720,), F32)])
    cv = jnp.concatenate([conv_s.reshape(-1), jnp.zeros((2 * D - 1920,), F32)])
    return jnp.concatenate([c_ctx.reshape(-1), n1.reshape(-1), n2.reshape(-1), b_ada.reshape(-1), misc, rp, cv]).reshape(16, D)


def _unpack_small(a):
    f = a.reshape(-1)
    misc = f[9 * D:10 * D]
    return dict(
        c_ctx=f[0:D], norm1_w=f[D:2 * D].reshape(1, D), norm2_w=f[2 * D:3 * D].reshape(1, D),
        b_ada=f[3 * D:9 * D].reshape(1, 6 * D), dn_A_log=misc[0:8].reshape(1, 2, 4), dn_dt_bias=misc[8:16].reshape(1, 2, 4),
        dn_out_norm_w=misc[128:256].reshape(1, 128), na_q_norm_w=misc[256:320].reshape(1, 64),
        na_k_norm_w=misc[320:384].reshape(1, 64), na_rpb=f[10 * D:10 * D + 3720].reshape(1, 8, 15, 31),
        dn_conv_w=f[14 * D:14 * D + 1920].reshape(1, 5, 384))


_WEIGHTS = ['c_ctx', 'norm1_w', 'norm2_w', 'w_ada', 'b_ada', 'w_in', 'dn_conv_w', 'dn_A_log', 'dn_dt_bias',
            'dn_out_norm_w', 'na_q_norm_w', 'na_k_norm_w', 'na_rpb', 'w_out', 'w_ffn_in', 'w_ffn_out']


def kernel(x, c, ctx, c_ctx, norm1_w, norm2_w, w_ada, b_ada, w_in, dn_conv_w, dn_A_log, dn_dt_bias, dn_out_norm_w, na_q_norm_w, na_k_norm_w, na_rpb, w_out, w_ffn_in, w_ffn_out, loss_target, m_c_ctx, m_norm1_w, m_norm2_w, m_w_ada, m_b_ada, m_w_in, m_dn_conv_w, m_dn_A_log, m_dn_dt_bias, m_dn_out_norm_w, m_na_q_norm_w, m_na_k_norm_w, m_na_rpb, m_w_out, m_w_ffn_in, m_w_ffn_out, v_c_ctx, v_norm1_w, v_norm2_w, v_w_ada, v_b_ada, v_w_in, v_dn_conv_w, v_dn_A_log, v_dn_dt_bias, v_dn_out_norm_w, v_na_q_norm_w, v_na_k_norm_w, v_na_rpb, v_w_out, v_w_ffn_in, v_w_ffn_out):
    ix, iy, ic = lax.axis_index("x"), lax.axis_index("y"), lax.axis_index("c")
    chip = 2 * ix + iy
    me = 4 * ix + 2 * iy + ic

    crow = jnp.concatenate([c, c_ctx.reshape(1, D), jnp.zeros((5, D), F32)], axis=0)
    cvrow = jnp.concatenate([dn_conv_w[0].reshape(-1), jnp.zeros((2 * D - 1920,), F32)]).reshape(2, D)
    g0 = _allgather8(jnp.concatenate([crow, cvrow, jnp.zeros((6, D), F32)], axis=0), "gather_cond")
    cin = g0[:, :8].reshape(64, D)
    conv_full = jnp.concatenate([g0[2 * s, 8:10].reshape(-1)[:1920].reshape(5, 384) for s in range(4)], axis=1)

    w_ada_s = w_ada[0]
    b_s = lax.dynamic_slice(b_ada, (0, chip * 1536), (1, 1536))
    mod_s = _ada_fwd(cin, w_ada_s, b_s)
    mods = _gather_rows(mod_s, "gather_mod")
    mod_mine = jnp.concatenate([mods[0], mods[2], mods[4], mods[6]], axis=1)
    mod_x, mod_c = mod_mine[0:NB], mod_mine[2]

    def half(w2d):
        r = w2d.shape[0] // 2
        return lax.dynamic_slice(w2d, (ic * r, 0), (r, w2d.shape[1]))

    w_in_t = w_in[0].T
    (gw_in,) = _gather_weights([_cast_bf16(lax.dynamic_slice(w_in_t, (0, ic * (D // 2)), (900, D // 2)), "cast_w_in")])
    late = [_cast_bf16(half(w_out[0]), "cast_w_out"), _cast_bf16(half(w_ffn_in[0]), "cast_w_ffn_in"),
            _cast_bf16(half(w_ffn_out[0]), "cast_w_ffn_out")]
    w_in_p = _perm_w_in(gw_in.reshape(4, 2, 900, D // 2).transpose(0, 2, 1, 3).reshape(IN_COLS, D))

    (loss_part, grad_x, dmod_x, dmod_c, small, g_w_in_p, r_w_out, r_w_ffn_in, r_w_ffn_out) = _local_step(
        x, ctx, loss_target, mod_x, mod_c, norm1_w, norm2_w, w_in_p, conv_full, dn_A_log[0], dn_dt_bias[0],
        dn_out_norm_w[0], na_q_norm_w[0], na_k_norm_w[0], na_rpb[0], None, None, None,
        scatter_early=True, gather_late=late)
    loss = lax.psum(loss_part, ("x", "y", "c"))

    dm = jnp.concatenate([dmod_x, dmod_c.reshape(1, 6 * D), jnp.zeros((5, 6 * D), F32)], axis=0).reshape(48, D)
    misc = _misc_row(small["a_log"], small["dt_bias"], small["out_norm_w"], small["q_norm_w"], small["k_norm_w"])
    sm = jnp.concatenate([
        small["norm1_w"].reshape(-1), small["norm2_w"].reshape(-1), misc,
        small["rpb"].reshape(-1), jnp.zeros((4 * D - 3720,), F32),
        small["conv_w"].reshape(-1), jnp.zeros((8 * D - 7680,), F32), jnp.zeros((D,), F32)]).reshape(16, D)
    gsl, gtot = _allgather8(jnp.concatenate([dm, sm], axis=0), "gather_small", with_sum=True)
    gmod = gsl[:, :48].reshape(64, 6 * D)
    st = gtot[48:64].reshape(-1)
    g_n1, g_n2, g_misc = st[0:D].reshape(1, D), st[D:2 * D].reshape(1, D), st[2 * D:3 * D]
    g_rpb = st[3 * D:3 * D + 3720].reshape(1, 8, 15, 31)
    g_conv_full = st[7 * D:7 * D + 7680].reshape(5, 1536)
    g_conv_s = lax.dynamic_slice(g_conv_full, (0, chip * 384), (5, 384)).reshape(1, 5, 384)

    gmod_s = lax.dynamic_slice(gmod, (0, chip * 1536), (64, 1536))
    g_w_ada_s, pctx = _ada_bwd(cin, gmod_s, w_ada_s)
    parts = _allgather8(pctx, "gather_cctx")
    g_b_ada, g_c_ctx = _ada_small(gmod, parts, c_ctx.reshape(1, D))

    g_w_in8 = _unperm_w_in(g_w_in_p).reshape(4, 900, 2, D // 2).transpose(0, 2, 1, 3).reshape(8, 900, D // 2)
    names = ["w_in", "w_out", "w_ffn_in", "w_ffn_out"]
    rs = list(_scatter_grads([g_w_in8])) + [r_w_out, r_w_ffn_in, r_w_ffn_out]
    full = _sibling_exchange([_sum_slots(r, "sum_" + nm) for r, nm in zip(rs, names)])
    g_w_in_t = full[0].transpose(1, 0, 2).reshape(900, D)
    g_w_out_s = full[1].reshape(1, D // 4, D)
    g_w_ffn_in_s = full[2].reshape(1, D, FFT)
    g_w_ffn_out_s = full[3].reshape(1, DFF // 4, D)

    grads = dict(
        c_ctx=g_c_ctx.reshape(D), norm1_w=g_n1, norm2_w=g_n2, w_ada=g_w_ada_s.reshape(1, D, 1536), b_ada=g_b_ada,
        w_in=g_w_in_t, dn_conv_w=g_conv_s, dn_A_log=g_misc[0:8].reshape(1, 2, 4), dn_dt_bias=g_misc[8:16].reshape(1, 2, 4),
        dn_out_norm_w=g_misc[128:256].reshape(1, 128), na_q_norm_w=g_misc[256:320].reshape(1, 64),
        na_k_norm_w=g_misc[320:384].reshape(1, 64), na_rpb=g_rpb, w_out=g_w_out_s, w_ffn_in=g_w_ffn_in_s,
        w_ffn_out=g_w_ffn_out_s)
    pk = lambda d_: _pack_small(d_["c_ctx"], d_["norm1_w"], d_["norm2_w"], d_["b_ada"], d_["dn_A_log"], d_["dn_dt_bias"],
                                d_["dn_out_norm_w"], d_["na_q_norm_w"], d_["na_k_norm_w"], d_["na_rpb"], d_["dn_conv_w"])
    wts = dict(c_ctx=c_ctx, norm1_w=norm1_w, norm2_w=norm2_w, b_ada=b_ada, dn_A_log=dn_A_log, dn_dt_bias=dn_dt_bias,
               dn_out_norm_w=dn_out_norm_w, na_q_norm_w=na_q_norm_w, na_k_norm_w=na_k_norm_w, na_rpb=na_rpb, dn_conv_w=dn_conv_w)
    ms = dict(c_ctx=m_c_ctx, norm1_w=m_norm1_w, norm2_w=m_norm2_w, b_ada=m_b_ada, dn_A_log=m_dn_A_log, dn_dt_bias=m_dn_dt_bias,
              dn_out_norm_w=m_dn_out_norm_w, na_q_norm_w=m_na_q_norm_w, na_k_norm_w=m_na_k_norm_w, na_rpb=m_na_rpb,
              dn_conv_w=m_dn_conv_w)
    vs = dict(c_ctx=v_c_ctx, norm1_w=v_norm1_w, norm2_w=v_norm2_w, b_ada=v_b_ada, dn_A_log=v_dn_A_log, dn_dt_bias=v_dn_dt_bias,
              dn_out_norm_w=v_dn_out_norm_w, na_q_norm_w=v_na_q_norm_w, na_k_norm_w=v_na_k_norm_w, na_rpb=v_na_rpb,
              dn_conv_w=v_dn_conv_w)
    sd, snm, snv, _ = _adamw(pk(wts), pk(grads), pk(ms), pk(vs), "adamw_small")
    delta, new_m, new_v = _unpack_small(sd), _unpack_small(snm), _unpack_small(snv)
    big = dict(w_ada=(w_ada, m_w_ada, v_w_ada), w_in=(w_in, m_w_in, v_w_in), w_out=(w_out, m_w_out, v_w_out),
               w_ffn_in=(w_ffn_in, m_w_ffn_in, v_w_ffn_in), w_ffn_out=(w_ffn_out, m_w_ffn_out, v_w_ffn_out))
    for nm, (w_, m_, v_) in big.items():
        shp = w_.shape
        if nm == "w_in":
            r2, back = (lambda a: a[0].T), (lambda a: a.T.reshape(shp))
            res = _adamw(w_in_t, grads[nm], r2(m_), r2(v_), "adamw_" + nm)
        else:
            r2, back = (lambda a: a.reshape(shp[1], shp[2])), (lambda a: a.reshape(shp))
            res = _adamw(r2(w_), r2(grads[nm]), r2(m_), r2(v_), "adamw_" + nm)
        delta[nm], new_m[nm], new_v[nm], grads[nm] = [back(a) for a in res]

    return (loss, grad_x, *[grads[n] for n in _WEIGHTS], *[delta[n] for n in _WEIGHTS],
            *[new_m[n] for n in _WEIGHTS], *[new_v[n] for n in _WEIGHTS])
```

```python
import functools

import numpy as np
import jax
import jax.numpy as jnp
from jax import lax
from jax.experimental import pallas as pl
from jax.experimental.pallas import tpu as pltpu

F32 = jnp.float32
BF16 = jnp.bfloat16

D = 1024
NB = 2
SEQ = 2048
CT = 256
T = CT + SEQ
NTOK = NB * SEQ
CH = 64
NCH = T // CH
NCH_CTX = CT // CH
DNH, DNW = 4, 512
NAH, NAW = 8, 512
GRID_W = 64
ROWS = SEQ // GRID_W
WIN_ROWS = 8
WIN_COLS = 16
NWIN = WIN_ROWS * GRID_W
DFF = 2816
FFT = 1408
IN_COLS = 3600
PC = 3712
COL_Z, COL_NA, COL_G = 1536, 2048, 3584
EPS = 1e-6
NEG = -1e30
RT = 256
VMEM_BIG = 56 * 1024 * 1024
MESH = pl.DeviceIdType.MESH

ADAM_LR, ADAM_B1, ADAM_B2, ADAM_EPS, ADAM_WD, ADAM_STEP = 0.001, 0.9, 0.999, 1e-08, 0.01, 10


def _cp(vmem=None, sem=None):
    kw = {}
    if vmem is not None:
        kw["vmem_limit_bytes"] = vmem
    if sem is not None:
        kw["dimension_semantics"] = sem
    return pltpu.CompilerParams(**kw)


_NN = (((1,), (0,)), ((), ()))
_NT = (((1,), (1,)), ((), ()))
_TN = (((0,), (0,)), ((), ()))


def _dg(a, b, dims):
    return lax.dot_general(a.astype(BF16), b.astype(BF16), dims, preferred_element_type=F32)


@jax.custom_vjp
def _mm(a, b):
    return _dg(a, b, _NN)


_mm.defvjp(lambda a, b: (_dg(a, b, _NN), (a, b)),
           lambda r, g: (_dg(g, r[1], _NT), _dg(r[0], g, _TN)))


@jax.custom_vjp
def _mm_nt(a, b):
    return _dg(a, b, _NT)


_mm_nt.defvjp(lambda a, b: (_dg(a, b, _NT), (a, b)),
              lambda r, g: (_dg(g, r[1], _NN), _dg(g, r[0], _TN)))


@jax.custom_vjp
def _mm_tn(a, b):
    return _dg(a, b, _TN)


_mm_tn.defvjp(lambda a, b: (_dg(a, b, _TN), (a, b)),
              lambda r, g: (_dg(r[1], g, _NT), _dg(r[0], g, _NN)))


def _split_dot(a, b, dims):
    hi = b.astype(BF16)
    lo = (b - hi.astype(F32)).astype(BF16)
    ab = a.astype(BF16)
    return (lax.dot_general(ab, hi, dims, preferred_element_type=F32)
            + lax.dot_general(ab, lo, dims, preferred_element_type=F32))


@jax.custom_vjp
def _mask_mm(mask, b):
    return _split_dot(mask, b, _NN)


_mask_mm.defvjp(lambda mask, b: (_split_dot(mask, b, _NN), mask),
                lambda mask, g: (jnp.zeros_like(mask), _split_dot(mask, g, _TN)))


def _hmm(a, b):
    return jnp.dot(a, b, precision=lax.Precision.HIGHEST, preferred_element_type=F32)


def _sigmoid(x):
    return 1.0 / (1.0 + jnp.exp(-x))


def _silu(x):
    return x * _sigmoid(x)


def _norm_mod(x, w, sh, sc):
    r = lax.rsqrt(jnp.mean(x * x, axis=-1, keepdims=True) + EPS)
    return (x * r * w) * (1.0 + sc) + sh


def _my_pos():
    return lax.axis_index("x"), lax.axis_index("y"), lax.axis_index("c")


def _flip(pos, k):
    x, y, c = pos
    return (1 - x if k & 4 else x, 1 - y if k & 2 else y, 1 - c if k & 1 else c)


def _slot(pos):
    return 4 * pos[0] + 2 * pos[1] + pos[2]


def _allgather8(v, name, with_sum=False):
    R, W = v.shape

    def body(v_ref, out_ref, *rest):
        if with_sum:
            tot_ref, send_sems, recv_sems, lsem = rest
        else:
            send_sems, recv_sems, lsem = rest
        me = _my_pos()
        mine = pltpu.make_async_copy(v_ref, out_ref.at[_slot(me)], lsem)
        mine.start()
        sends = []
        for k in range(1, 8):
            cp = pltpu.make_async_remote_copy(
                src_ref=v_ref, dst_ref=out_ref.at[_slot(me)], send_sem=send_sems.at[k - 1],
                recv_sem=recv_sems.at[k - 1], device_id=_flip(me, k), device_id_type=MESH)
            cp.start()
            sends.append(cp)
        for k in range(1, 8):
            peer = _flip(me, k)
            pltpu.make_async_remote_copy(
                src_ref=v_ref, dst_ref=out_ref.at[_slot(peer)], send_sem=send_sems.at[k - 1],
                recv_sem=recv_sems.at[k - 1], device_id=peer, device_id_type=MESH).wait_recv()
        for cp in sends:
            cp.wait_send()
        mine.wait()
        if with_sum:
            acc = out_ref[0]
            for s in range(1, 8):
                acc = acc + out_ref[s]
            tot_ref[...] = acc

    out_shape = [jax.ShapeDtypeStruct((8, R, W), F32)]
    out_specs = [pl.BlockSpec(memory_space=pltpu.VMEM)]
    if with_sum:
        out_shape.append(jax.ShapeDtypeStruct((R, W), F32))
        out_specs.append(pl.BlockSpec(memory_space=pltpu.VMEM))
    res = pl.pallas_call(
        body, name=name, out_shape=out_shape,
        in_specs=[pl.BlockSpec(memory_space=pltpu.VMEM)], out_specs=out_specs,
        scratch_shapes=[pltpu.SemaphoreType.DMA((7,)), pltpu.SemaphoreType.DMA((7,)), pltpu.SemaphoreType.DMA],
        compiler_params=_cp(vmem=VMEM_BIG),
    )(v)
    return res if with_sum else res[0]


def _gather_rows(v, name):
    W = v.shape[1]

    def body(v_ref, out_ref, send_sems, recv_sems, lsem):
        pos = _my_pos()
        me = _slot(pos)
        rows = lambda p: v_ref.at[pl.ds(pl.multiple_of(8 * p, 8), 8)]
        locals_ = [(rows(me), out_ref.at[me])]
        xfers = [(rows(_slot(_flip(pos, k))), out_ref.at[me], out_ref.at[_slot(_flip(pos, k))], _flip(pos, k))
                 for k in range(1, 8)]
        _xfer_start((locals_, xfers), (send_sems, recv_sems, lsem))
        _xfer_wait((locals_, xfers), (send_sems, recv_sems, lsem))

    vm = pl.BlockSpec(memory_space=pltpu.VMEM)
    return pl.pallas_call(
        body, name=name, out_shape=jax.ShapeDtypeStruct((8, 8, W), F32), in_specs=[vm], out_specs=vm,
        scratch_shapes=_xfer_sems(1, 7),
    )(v)


def _gather_weights(halves):
    n = len(halves)

    def body(*refs):
        ins, outs = refs[:n], refs[n:2 * n]
        send_sems, recv_sems, lsems = refs[2 * n:]
        me = _my_pos()
        x, y, c = me
        sib = (x, y, 1 - c)
        chips = [4, 2, 6]

        def cp(a, k, block, to, src=None):
            dst = outs[a].at[_slot(block)]
            return pltpu.make_async_remote_copy(
                src_ref=dst if src is None else src, dst_ref=dst, send_sem=send_sems.at[a, k],
                recv_sem=recv_sems.at[a, k], device_id=to, device_id_type=MESH)

        mine, first, passed = [], [], []
        for a in range(n):
            m = pltpu.make_async_copy(ins[a], outs[a].at[_slot(me)], lsems.at[a])
            m.start()
            mine.append(m)
            first.append(cp(a, 0, me, sib, src=ins[a]))
            for j, k in enumerate(chips):
                first.append(cp(a, 1 + j, me, _flip(me, k), src=ins[a]))
        for f in first:
            f.start()
        for j, k in enumerate(chips):
            for a in range(n):
                cp(a, 1 + j, _flip(me, k), me).wait_recv()
                p = cp(a, 4 + j, _flip(me, k), sib)
                p.start()
                passed.append(p)
        for a in range(n):
            cp(a, 0, sib, me).wait_recv()
            for j, k in enumerate(chips):
                cp(a, 4 + j, _flip(sib, k), me).wait_recv()
        for f in first + passed:
            f.wait_send()
        for m in mine:
            m.wait()

    vm = pl.BlockSpec(memory_space=pltpu.VMEM)
    return pl.pallas_call(
        body, name="gather_weights",
        out_shape=[jax.ShapeDtypeStruct((8,) + h.shape, h.dtype) for h in halves],
        in_specs=[vm] * n, out_specs=[vm] * n,
        scratch_shapes=[pltpu.SemaphoreType.DMA((n, 7)), pltpu.SemaphoreType.DMA((n, 7)),
                        pltpu.SemaphoreType.DMA((n,))],
        compiler_params=_cp(vmem=VMEM_BIG),
    )(*halves)


def _xfer_start(plan, sems):
    send_sems, recv_sems, lsems = sems
    locals_, xfers = plan
    for i, (src, dst) in enumerate(locals_):
        pltpu.make_async_copy(src, dst, lsems.at[i]).start()
    for i, (src, dst, _, peer) in enumerate(xfers):
        pltpu.make_async_remote_copy(src_ref=src, dst_ref=dst, send_sem=send_sems.at[i], recv_sem=recv_sems.at[i],
                                     device_id=peer, device_id_type=MESH).start()


def _xfer_wait(plan, sems):
    send_sems, recv_sems, lsems = sems
    locals_, xfers = plan
    for i, (src, _, landing, peer) in enumerate(xfers):
        pltpu.make_async_remote_copy(src_ref=src, dst_ref=landing, send_sem=send_sems.at[i], recv_sem=recv_sems.at[i],
                                     device_id=peer, device_id_type=MESH).wait_recv()
    for i, (src, dst, _, peer) in enumerate(xfers):
        pltpu.make_async_remote_copy(src_ref=src, dst_ref=dst, send_sem=send_sems.at[i], recv_sem=recv_sems.at[i],
                                     device_id=peer, device_id_type=MESH).wait_send()
    for i, (src, dst) in enumerate(locals_):
        pltpu.make_async_copy(src, dst, lsems.at[i]).wait()


def _xfer_sems(n_local, n_xfer):
    return [pltpu.SemaphoreType.DMA((n_xfer,)), pltpu.SemaphoreType.DMA((n_xfer,)), pltpu.SemaphoreType.DMA((n_local,))]


def _scatter_plan(ins, outs, pos):
    me = _slot(pos)
    locals_, xfers = [], []
    for src, dst in zip(ins, outs):
        locals_.append((src.at[me], dst.at[me]))
        for k in range(1, 8):
            peer = _flip(pos, k)
            xfers.append((src.at[_slot(peer)], dst.at[me], dst.at[_slot(peer)], peer))
    return locals_, xfers


def _scatter_grads(grads):
    n = len(grads)

    def body(*refs):
        plan = _scatter_plan(refs[:n], refs[n:2 * n], _my_pos())
        _xfer_start(plan, refs[2 * n:])
        _xfer_wait(plan, refs[2 * n:])

    hbm = pl.BlockSpec(memory_space=pltpu.HBM)
    return pl.pallas_call(
        body, name="scatter_grads", out_shape=[jax.ShapeDtypeStruct(g.shape, g.dtype) for g in grads],
        in_specs=[hbm] * n, out_specs=[hbm] * n, scratch_shapes=_xfer_sems(n, 7 * n),
    )(*grads)


def _sibling_exchange(halves):
    n = len(halves)

    def body(*refs):
        ins, outs = refs[:n], refs[n:2 * n]
        send_sems, recv_sems, lsems = refs[2 * n:]
        x, y, c = _my_pos()
        sib = (x, y, 1 - c)
        mine, sends = [], []
        for a in range(n):
            m = pltpu.make_async_copy(ins[a], outs[a].at[c], lsems.at[a])
            m.start()
            mine.append(m)
            cp = pltpu.make_async_remote_copy(
                src_ref=ins[a], dst_ref=outs[a].at[c], send_sem=send_sems.at[a], recv_sem=recv_sems.at[a],
                device_id=sib, device_id_type=MESH)
            cp.start()
            sends.append(cp)
        for a in range(n):
            pltpu.make_async_remote_copy(
                src_ref=ins[a], dst_ref=outs[a].at[1 - c], send_sem=send_sems.at[a], recv_sem=recv_sems.at[a],
                device_id=sib, device_id_type=MESH).wait_recv()
        for cp in sends:
            cp.wait_send()
        for m in mine:
            m.wait()

    vm = pl.BlockSpec(memory_space=pltpu.VMEM)
    return pl.pallas_call(
        body, name="sibling_exchange",
        out_shape=[jax.ShapeDtypeStruct((2,) + h.shape, h.dtype) for h in halves],
        in_specs=[vm] * n, out_specs=[vm] * n,
        scratch_shapes=[pltpu.SemaphoreType.DMA((n,)), pltpu.SemaphoreType.DMA((n,)), pltpu.SemaphoreType.DMA((n,))],
        compiler_params=_cp(vmem=VMEM_BIG),
    )(*halves)


def _row_tile(r, cap=512):
    for t in (cap, 256, 128, 64, 32, 16, 8):
        if t <= cap and r % t == 0:
            return t
    return r


def _cast_bf16(a, name):
    R, W = a.shape
    tr = _row_tile(R, 256)

    def body(a_ref, o_ref):
        o_ref[...] = a_ref[...].astype(BF16)

    return pl.pallas_call(
        body, name=name, grid=(R // tr,), out_shape=jax.ShapeDtypeStruct((R, W), BF16),
        in_specs=[pl.BlockSpec((tr, W), lambda i: (i, 0))], out_specs=pl.BlockSpec((tr, W), lambda i: (i, 0)),
    )(a)


def _sum_slots(r, name):
    _, R, W = r.shape
    tr = _row_tile(R, 128)

    def body(r_ref, o_ref):
        acc = r_ref[0].astype(F32)
        for s_ in range(1, 8):
            acc = acc + r_ref[s_].astype(F32)
        o_ref[...] = acc.astype(BF16)

    return pl.pallas_call(
        body, name=name, grid=(R // tr,), out_shape=jax.ShapeDtypeStruct((R, W), BF16),
        in_specs=[pl.BlockSpec((8, tr, W), lambda i: (0, i, 0))], out_specs=pl.BlockSpec((tr, W), lambda i: (i, 0)),
        compiler_params=_cp(vmem=VMEM_BIG),
    )(r)


def _adamw(w, g, m, v, name):
    R, W = w.shape
    tr = _row_tile(R, 256)
    tc = 256 if (tr == R and R * W * 4 > (1 << 21) and W % 256 == 0) else W
    c1 = 1.0 / (1.0 - ADAM_B1 ** ADAM_STEP)
    c2 = 1.0 / (1.0 - ADAM_B2 ** ADAM_STEP)

    def body(w_ref, g_ref, m_ref, v_ref, d_ref, nm_ref, nv_ref, g32_ref):
        gg = g_ref[...].astype(F32)
        g32_ref[...] = gg
        nm = ADAM_B1 * m_ref[...] + (1.0 - ADAM_B1) * gg
        nv = ADAM_B2 * v_ref[...] + (1.0 - ADAM_B2) * (gg * gg)
        d_ref[...] = -ADAM_LR * ((nm * c1) / (jnp.sqrt(nv * c2) + ADAM_EPS) + ADAM_WD * w_ref[...])
        nm_ref[...] = nm
        nv_ref[...] = nv

    spec = pl.BlockSpec((tr, tc), lambda i, j: (i, j))
    return pl.pallas_call(
        body, name=name, grid=(R // tr, W // tc), out_shape=[jax.ShapeDtypeStruct((R, W), F32)] * 4,
        in_specs=[spec] * 4, out_specs=[spec] * 4, compiler_params=_cp(vmem=VMEM_BIG),
    )(w, g, m, v)


def _matmul_tn(a, b, tm, tn, tk, name, nn=1, b_map=None, out_shape=None, out_block=None, out_map=None,
               b_block=None):
    K, M = a.shape
    nk = K // tk
    nm = M // tm

    def body(a_ref, b_ref, o_ref, acc_ref):
        kk = pl.program_id(2)

        @pl.when(kk == 0)
        def _():
            acc_ref[...] = jnp.zeros_like(acc_ref)

        acc_ref[...] += lax.dot_general(a_ref[...], b_ref[...], _TN, preferred_element_type=F32)

        @pl.when(kk == nk - 1)
        def _():
            o_ref[...] = acc_ref[...].astype(BF16)

    return pl.pallas_call(
        body, name=name, grid=(nm, nn, nk),
        out_shape=jax.ShapeDtypeStruct(out_shape or (M, b.shape[-1]), BF16),
        in_specs=[pl.BlockSpec((tk, tm), lambda i, j, k: (k, i)),
                  pl.BlockSpec(b_block or (tk, tn), b_map or (lambda i, j, k: (k, j)))],
        out_specs=pl.BlockSpec(out_block or (tm, tn), out_map or (lambda i, j, k: (i, j))),
        scratch_shapes=[pltpu.VMEM((tm, tn), F32)],
        compiler_params=_cp(vmem=VMEM_BIG),
    )(a, b)


def _ada_fwd(cin, w_ada_s, b_s):
    tn = 512

    def body(c_ref, w_ref, b_ref, o_ref):
        o_ref[...] = _dg(_silu(c_ref[...]), w_ref[...], _NN) + b_ref[...]

    return pl.pallas_call(
        body, name="ada_fwd", grid=(1536 // tn,), out_shape=jax.ShapeDtypeStruct((64, 1536), F32),
        in_specs=[pl.BlockSpec((64, D), lambda j: (0, 0)), pl.BlockSpec((D, tn), lambda j: (0, j)),
                  pl.BlockSpec((1, tn), lambda j: (0, j))],
        out_specs=pl.BlockSpec((64, tn), lambda j: (0, j)),
        compiler_params=_cp(vmem=VMEM_BIG),
    )(cin, w_ada_s, b_s)


def _ada_bwd(cin, gmod_s, w_ada_s):
    tn = 512

    def body(c_ref, g_ref, w_ref, gw_ref, pc_ref):
        j = pl.program_id(0)
        g = g_ref[...]
        gw_ref[...] = _dg(_silu(c_ref[...]), g, _TN)
        rr = lax.broadcasted_iota(jnp.int32, (8, 64), 0)
        cc = lax.broadcasted_iota(jnp.int32, (8, 64), 1)
        sel = jnp.where((rr == 0) & ((cc & 7) == 2), 1.0, 0.0).astype(F32)
        dctx = _hmm(sel, g)
        part = _dg(dctx, w_ref[...], _NT)

        @pl.when(j == 0)
        def _():
            pc_ref[...] = jnp.zeros_like(pc_ref)

        pc_ref[...] += part

    return pl.pallas_call(
        body, name="ada_bwd", grid=(1536 // tn,),
        out_shape=[jax.ShapeDtypeStruct((D, 1536), F32), jax.ShapeDtypeStruct((8, D), F32)],
        in_specs=[pl.BlockSpec((64, D), lambda j: (0, 0)), pl.BlockSpec((64, tn), lambda j: (0, j)),
                  pl.BlockSpec((D, tn), lambda j: (0, j))],
        out_specs=[pl.BlockSpec((D, tn), lambda j: (0, j)), pl.BlockSpec((8, D), lambda j: (0, 0))],
        compiler_params=_cp(vmem=VMEM_BIG),
    )(cin, gmod_s, w_ada_s)


def _ada_small(gmod, parts, c_ctx):
    def body(g_ref, p_ref, c_ref, gb_ref, gc_ref):
        gb_ref[...] = jnp.sum(g_ref[...], axis=0, keepdims=True)
        ds = p_ref[0][0:1] + p_ref[2][0:1] + p_ref[4][0:1] + p_ref[6][0:1]
        cc = c_ref[...]
        s = _sigmoid(cc)
        gc_ref[...] = ds * (s * (1.0 + cc * (1.0 - s)))

    vm = pl.BlockSpec(memory_space=pltpu.VMEM)
    return pl.pallas_call(
        body, name="ada_small",
        out_shape=[jax.ShapeDtypeStruct((1, 6 * D), F32), jax.ShapeDtypeStruct((1, D), F32)],
        in_specs=[vm, vm, vm], out_specs=[vm, vm],
    )(gmod, parts, c_ctx)


def _tok_specs():
    xs = pl.BlockSpec((None, RT, D), lambda b, j: (b, jnp.maximum(j - 1, 0), 0))
    cs = pl.BlockSpec((None, RT, D), lambda b, j: (b, 0, 0))
    ms = pl.BlockSpec((None, 1, D), lambda b, j: (b, 0, 0))
    cms = pl.BlockSpec((None, 1, D), lambda b, j: (0, 0, 0))
    return xs, cs, ms, cms


def _inproj_fwd(x, ctx, sh1, sc1, csh1, csc1, n1w, w_in_p):
    def body(x_ref, c_ref, sh_ref, sc_ref, csh_ref, csc_ref, w1_ref, w_ref, p_ref, hm_ref):
        is_ctx = pl.program_id(1) == 0
        xt = jnp.where(is_ctx, c_ref[...], x_ref[...])
        sh = jnp.where(is_ctx, csh_ref[...], sh_ref[...])
        sc = jnp.where(is_ctx, csc_ref[...], sc_ref[...])
        hb = _norm_mod(xt, w1_ref[...], sh, sc).astype(BF16)
        hm_ref[...] = hb
        p_ref[...] = lax.dot_general(hb, w_ref[...], _NT, preferred_element_type=F32)

    xs, cs, ms, cms = _tok_specs()
    return pl.pallas_call(
        body, name="inproj_fwd", grid=(NB, T // RT),
        out_shape=[jax.ShapeDtypeStruct((NB, T, PC), F32), jax.ShapeDtypeStruct((NB, T, D), BF16)],
        in_specs=[xs, cs, ms, ms, cms, cms, pl.BlockSpec((1, D), lambda b, j: (0, 0)),
                  pl.BlockSpec((PC, D), lambda b, j: (0, 0))],
        out_specs=[pl.BlockSpec((None, RT, PC), lambda b, j: (b, j, 0)),
                   pl.BlockSpec((None, RT, D), lambda b, j: (b, j, 0))],
        compiler_params=_cp(vmem=VMEM_BIG),
    )(x, ctx, sh1, sc1, csh1, csc1, n1w, w_in_p)


def _inproj_bwd(dp_dn, dz, dp_na, dgt, w_in_p, x, ctx, sh1, sc1, csh1, csc1, n1w, dx1):
    def body(d1q_ref, d1k_ref, d1v_ref, d2_ref, d3_ref, d4_ref, w_ref, x_ref, c_ref, sh_ref, sc_ref, csh_ref, csc_ref,
             w1_ref, dx1_ref, dpb_ref, gx_ref, dsh_ref, dsc_ref, dcsh_ref, dcsc_ref, dw1_ref):
        b, j = pl.program_id(0), pl.program_id(1)
        is_ctx = j == 0
        dp = jnp.concatenate([d1q_ref[...], d1k_ref[...], d1v_ref[...], d2_ref[...], d3_ref[...], d4_ref[...]],
                             axis=1).astype(BF16)
        dpb_ref[...] = dp
        dhm = jnp.dot(dp, w_ref[...], preferred_element_type=F32)
        xt = jnp.where(is_ctx, c_ref[...], x_ref[...])
        sh = jnp.where(is_ctx, csh_ref[...], sh_ref[...])
        sc = jnp.where(is_ctx, csc_ref[...], sc_ref[...])
        _, vjp = jax.vjp(_norm_mod, xt, w1_ref[...], sh, sc)
        dxt, dw1, dsh, dsc = vjp(dhm)

        @pl.when((b == 0) & (j == 0))
        def _():
            dw1_ref[...] = jnp.zeros_like(dw1_ref)
            dcsh_ref[...] = jnp.zeros_like(dcsh_ref)
            dcsc_ref[...] = jnp.zeros_like(dcsc_ref)

        dw1_ref[...] += dw1

        @pl.when(is_ctx)
        def _():
            dcsh_ref[...] += dsh
            dcsc_ref[...] += dsc

        @pl.when(j == 1)
        def _():
            dsh_ref[...] = jnp.zeros_like(dsh_ref)
            dsc_ref[...] = jnp.zeros_like(dsc_ref)

        @pl.when(j >= 1)
        def _():
            dsh_ref[...] += dsh
            dsc_ref[...] += dsc
            gx_ref[...] = dx1_ref[...] + dxt

    xs, cs, ms, cms = _tok_specs()
    tsp = lambda w: pl.BlockSpec((None, RT, w), lambda b, j: (b, j, 0))
    one = pl.BlockSpec((1, D), lambda b, j: (0, 0))
    return pl.pallas_call(
        body, name="inproj_bwd", grid=(NB, T // RT),
        out_shape=[jax.ShapeDtypeStruct((NB, T, PC), BF16), jax.ShapeDtypeStruct((NB, SEQ, D), F32),
                   jax.ShapeDtypeStruct((NB, 1, D), F32), jax.ShapeDtypeStruct((NB, 1, D), F32),
                   jax.ShapeDtypeStruct((1, 1, D), F32), jax.ShapeDtypeStruct((1, 1, D), F32),
                   jax.ShapeDtypeStruct((1, D), F32)],
        in_specs=[tsp(512), tsp(512), tsp(512), tsp(512), tsp(1536), tsp(128), pl.BlockSpec((PC, D), lambda b, j: (0, 0)),
                  xs, cs, ms, ms, cms, cms, one, _rows2d_spec(D)],
        out_specs=[tsp(PC), xs, ms, ms, cms, cms, one],
        compiler_params=_cp(vmem=VMEM_BIG),
    )(*dp_dn, dz, dp_na, dgt, w_in_p, x, ctx, sh1, sc1, csh1, csc1, n1w, dx1)


def _shift_masks():
    t = lax.broadcasted_iota(jnp.int32, (T, 1), 0)
    lo = jnp.where(t < CT, 0, CT)
    hi = jnp.where(t < CT, CT, T)
    return {off: (t + off >= lo) & (t + off < hi) for off in range(-2, 3)}


def _shifted(u, off, masks):
    s = u if off == 0 else pltpu.roll(u, (-off) % T, 0)
    return jnp.where(masks[off], s, 0.0)


def _conv_pre(u, w, masks):
    acc = _shifted(u, -2, masks) * w[0:1, :]
    for kk in range(1, 5):
        acc = acc + _shifted(u, kk - 2, masks) * w[kk:kk + 1, :]
    return acc


def _act_q(pre):
    a = _silu(pre)
    return a * lax.rsqrt(jnp.sum(a * a, axis=-1, keepdims=True) + EPS) * (128.0 ** -0.5)


def _act_k(pre):
    a = _silu(pre)
    return a * lax.rsqrt(jnp.sum(a * a, axis=-1, keepdims=True) + EPS)


def _dn_prep_fwd(p, cw):
    def body(pq_ref, pk_ref, pv_ref, wq_ref, wk_ref, wv_ref, q_ref, k_ref, v_ref):
        masks = _shift_masks()
        q_ref[...] = _act_q(_conv_pre(pq_ref[...], wq_ref[...], masks))
        k_ref[...] = _act_k(_conv_pre(pk_ref[...], wk_ref[...], masks))
        v_ref[...] = _silu(_conv_pre(pv_ref[...], wv_ref[...], masks))

    ps = lambda o: pl.BlockSpec((None, T, 128), lambda b, h: (b, 0, o + h))
    ws = lambda o: pl.BlockSpec((8, 128), lambda b, h: (0, o + h))
    os_ = pl.BlockSpec((None, T, 128), lambda b, h: (b, 0, h))
    return pl.pallas_call(
        body, name="dn_prep_fwd", grid=(NB, DNH), out_shape=[jax.ShapeDtypeStruct((NB, T, DNW), F32)] * 3,
        in_specs=[ps(0), ps(4), ps(8), ws(0), ws(4), ws(8)], out_specs=[os_] * 3,
        compiler_params=_cp(vmem=VMEM_BIG),
    )(p, p, p, cw, cw, cw)


def _dn_prep_bwd(p, cw, dqs, dks, dvs):
    def body(pq_ref, pk_ref, pv_ref, wq_ref, wk_ref, wv_ref, dqf_ref, dqb_ref, dkf_ref, dkb_ref, dvf_ref, dvb_ref,
             oq_ref, ok_ref, ov_ref, gq_ref, gk_ref, gv_ref):
        b = pl.program_id(1)
        masks = _shift_masks()

        def one(p_ref, w_ref, df_ref, db_ref, act, o_ref, g_ref):
            u = p_ref[...]
            w = w_ref[...]
            _, vjp = jax.vjp(act, _conv_pre(u, w, masks))
            (dpre,) = vjp(df_ref[...] + db_ref[...])
            du = _shifted(dpre, 2, masks) * w[0:1, :]
            rows = [jnp.sum(dpre * _shifted(u, -2, masks), axis=0, keepdims=True)]
            for kk in range(1, 5):
                du = du + _shifted(dpre, 2 - kk, masks) * w[kk:kk + 1, :]
                rows.append(jnp.sum(dpre * _shifted(u, kk - 2, masks), axis=0, keepdims=True))
            o_ref[...] = du
            gw = jnp.concatenate(rows + [jnp.zeros((3, 128), F32)], axis=0)

            @pl.when(b == 0)
            def _():
                g_ref[...] = gw

            @pl.when(b > 0)
            def _():
                g_ref[...] += gw

        one(pq_ref, wq_ref, dqf_ref, dqb_ref, _act_q, oq_ref, gq_ref)
        one(pk_ref, wk_ref, dkf_ref, dkb_ref, _act_k, ok_ref, gk_ref)
        one(pv_ref, wv_ref, dvf_ref, dvb_ref, _silu, ov_ref, gv_ref)

    ps = lambda o: pl.BlockSpec((None, T, 128), lambda h, b: (b, 0, o + h))
    ws = lambda o: pl.BlockSpec((8, 128), lambda h, b: (0, o + h))
    ds = pl.BlockSpec((None, T, 128), lambda h, b: (b, 0, h))
    outs = pl.pallas_call(
        body, name="dn_prep_bwd", grid=(DNH, NB),
        out_shape=[jax.ShapeDtypeStruct((NB, T, DNW), F32)] * 3 + [jax.ShapeDtypeStruct((8, DNW), F32)] * 3,
        in_specs=[ps(0), ps(4), ps(8), ws(0), ws(4), ws(8)] + [ds] * 6,
        out_specs=[ds] * 3
        + [pl.BlockSpec((8, 128), lambda h, b: (0, h))] * 3,
        compiler_params=_cp(vmem=VMEM_BIG),
    )(p, p, p, cw, cw, cw, *dqs, *dks, *dvs)
    return outs


def _gate_fn(a, prm):
    lane = lax.broadcasted_iota(jnp.int32, a.shape, 1)
    z = a + prm[1:2, :]
    sp = jnp.maximum(z, 0.0) + jnp.log(1.0 + jnp.exp(-jnp.abs(z)))
    g = -jnp.exp(prm[0:1, :]) * sp
    return jnp.where(lane < 8, g, jnp.where(lane < 16, _sigmoid(a), 0.0))


def _gates_fwd(p, prm):
    def body(a_ref, prm_ref, o_ref):
        o_ref[...] = _gate_fn(a_ref[...], prm_ref[...])

    return pl.pallas_call(
        body, name="gates_fwd", grid=(NB,), out_shape=jax.ShapeDtypeStruct((NB, T, 128), F32),
        in_specs=[pl.BlockSpec((None, T, 128), lambda b: (b, 0, COL_G // 128)), pl.BlockSpec((8, 128), lambda b: (0, 0))],
        out_specs=pl.BlockSpec((None, T, 128), lambda b: (b, 0, 0)),
    )(p, prm)


def _gates_bwd(p, prm, dgf, dgb):
    def body(a_ref, prm_ref, dgf_ref, dgb_ref, o_ref, dprm_ref):
        b = pl.program_id(0)
        tot = dgf_ref[...] + dgb_ref[...]
        _, vjp = jax.vjp(_gate_fn, a_ref[...], prm_ref[...])
        da, dprm = vjp(tot)
        o_ref[...] = da

        @pl.when(b == 0)
        def _():
            dprm_ref[...] = dprm

        @pl.when(b > 0)
        def _():
            dprm_ref[...] += dprm

    return pl.pallas_call(
        body, name="gates_bwd", grid=(NB,),
        out_shape=[jax.ShapeDtypeStruct((NB, T, 128), F32), jax.ShapeDtypeStruct((8, 128), F32)],
        in_specs=[pl.BlockSpec((None, T, 128), lambda b: (b, 0, COL_G // 128)), pl.BlockSpec((8, 128), lambda b: (0, 0)),
                  pl.BlockSpec((None, T, 128), lambda b: (b, 0, 0)), pl.BlockSpec((None, T, 128), lambda b: (b, 0, 0))],
        out_specs=[pl.BlockSpec((None, T, 128), lambda b: (b, 0, 0)), pl.BlockSpec((8, 128), lambda b: (0, 0))],
        compiler_params=_cp(vmem=VMEM_BIG),
    )(p, prm, dgf, dgb)


def _lockstep(gens, joint=None):
    n = len(gens)
    res, vals = [None] * n, [None] * n
    while True:
        msgs = []
        for i, g in enumerate(gens):
            try:
                msgs.append((g.send(vals[i]), False))
            except StopIteration as e:
                res[i] = e.value
                msgs.append((None, True))
        if msgs[0][1]:
            return res
        if msgs[0][0] is not None:
            vals = list(joint(tuple(m[0] for m in msgs)))
        else:
            vals = [None] * n


def _tri_inv_gen(m, md, m1, m2, eye):
    n = -(m * md)
    t = eye + n
    pw = n
    for _ in range(3):
        pw = _dg(pw, pw, _NN)
        yield
        t = t + _dg(t, pw, _NN)
        yield
    a = _dg(t, m * m1, _NN)
    yield
    t = t - _dg(a, t, _NN)
    yield
    a = _dg(t, m * m2, _NN)
    yield
    return t - _dg(a, t, _NN)


def _tri_inv_bwd_gen(t, g):
    tt = t.T
    a = _dg(tt, g, _NN)
    yield
    return -_dg(a, tt, _NN)


@jax.custom_vjp
def _tri_inv(ms, md, m1, m2, eye):
    return tuple(_lockstep([_tri_inv_gen(m, md, m1, m2, eye) for m in ms]))


def _tri_inv_f(ms, md, m1, m2, eye):
    ts = _tri_inv(ms, md, m1, m2, eye)
    return ts, (ts, md)


def _tri_inv_b(res, gs):
    ts, md = res
    z = jnp.zeros_like(md)
    return (tuple(_lockstep([_tri_inv_bwd_gen(t, g) for t, g in zip(ts, gs)])), z, z, z, z)


_tri_inv.defvjp(_tri_inv_f, _tri_inv_b)


@jax.custom_vjp
def _tri_inv_saved(ms, ts):
    return tuple(t.astype(F32) for t in ts)


def _tri_inv_saved_f(ms, ts):
    return _tri_inv_saved(ms, ts), ts


def _tri_inv_saved_b(ts, gs):
    dms = tuple(_lockstep([_tri_inv_bwd_gen(t, g) for t, g in zip(ts, gs)]))
    return dms, tuple(jnp.zeros_like(t) for t in ts)


_tri_inv_saved.defvjp(_tri_inv_saved_f, _tri_inv_saved_b)


SR = DNH * CH


def _scan_masks():
    i, j = np.arange(SR)[:, None], np.arange(SR)[None, :]
    same = (i >> 6) == (j >> 6)
    cm = np.zeros((2, 6, SR, SR), np.float32)
    for d in range(2):
        diff = i - j if d == 0 else j - i
        cm[d, 0] = same & (diff >= 0)
        cm[d, 1] = same & (diff > 0)
        cm[d, 2] = (i >> 4) == (j >> 4)
        cm[d, 3] = ((i >> 5) == (j >> 5)) & ((i >> 4) != (j >> 4))
        cm[d, 4] = (i >> 5) != (j >> 5)
        cm[d, 5] = i == j
    t, u = np.arange(CH)[:, None], np.arange(CH)[None, :]
    c64 = np.stack([t >= u, t <= u]).astype(np.float32)
    bm = ((np.arange(SR)[:, None] >> 6) == (np.arange(DNW)[None, :] >> 7)).astype(np.float32)
    return jnp.asarray(cm), jnp.asarray(c64), jnp.asarray(bm)


@jax.custom_vjp
def _fold(x, bm):
    xm = x * bm
    return xm[:, 0:128] + xm[:, 128:256] + xm[:, 256:384] + xm[:, 384:512]


def _fold_f(x, bm):
    return _fold(x, bm), bm


def _fold_b(bm, g):
    return jnp.concatenate([g, g, g, g], axis=1) * bm, jnp.zeros_like(bm)


_fold.defvjp(_fold_f, _fold_b)


def _chunk_gen(q, k, v, gb, s, cm, c64, bm, d):
    incl, strict = cm[0], cm[1]
    lane = lax.broadcasted_iota(jnp.int32, (1, 128), 1)
    oh = lambda n: jnp.where(lane == n, 1.0, 0.0).astype(F32)
    col = lambda x, base: jnp.concatenate(
        [jnp.sum(x * oh(base + d * DNH + h), axis=1, keepdims=True) for h in range(DNH)], axis=0)
    beta = col(gb, 8)
    gam = col(_mask_mm(c64, gb), 0)
    yield
    tot = jnp.sum(gb, axis=0, keepdims=True)
    gtots = [jnp.sum(tot * oh(d * DNH + h), axis=1, keepdims=True) for h in range(DNH)]
    gtot = jnp.concatenate([jnp.broadcast_to(t, (CH, 1)) for t in gtots], axis=0)
    eg = jnp.concatenate([jnp.broadcast_to(jnp.exp(t), (1, 128)) for t in gtots], axis=1)
    row0 = jnp.where(lax.broadcasted_iota(jnp.int32, (128, 1), 0) == 0, 1.0, 0.0).astype(F32)
    gam_row = jnp.sum(jnp.broadcast_to(gam, (SR, 128)).T * row0, axis=0, keepdims=True)
    dec = jnp.exp(jnp.where(incl > 0.5, gam - gam_row, NEG))
    e_gam = jnp.exp(gam)
    kb = k * beta
    kk = _mm_nt(kb, k)
    yield
    tm = yield strict * (kk * dec)
    u = _mm(tm, v * beta)
    yield
    w = _mm(tm, kb * e_gam)
    yield
    qk = _mm_nt(q, k) * dec
    yield
    ws = _mm(w, s)
    yield
    v_new = u - _fold(ws, bm)
    qs = _mm(q * e_gam, s)
    yield
    o = _fold(qs, bm) + _mm(qk, v_new)
    yield
    vbd = jnp.concatenate([v_new, v_new, v_new, v_new], axis=1) * bm
    s_new = s * eg + _mm_tn(k * jnp.exp(gtot - gam), vbd)
    return o, s_new, tm


def _chunks(args, cm_ref, c64_ref, bm, saved=None):
    cm0 = cm_ref[0]
    if saved is None:
        inv = lambda ms: _tri_inv(ms, cm0[2], cm0[3], cm0[4], cm0[5])
    else:
        inv = lambda ms: _tri_inv_saved(ms, saved)
    return _lockstep([_chunk_gen(q, k, v, gb, s, cm_ref[d], c64_ref[d], bm, d) for q, k, v, gb, s, d in args], inv)


def _chunk_rev(s):
    return jnp.where(s < NCH_CTX, NCH_CTX - 1 - s, NCH + NCH_CTX - 1 - s)


def _scan_specs(step):
    cf = lambda s: step(s)
    cb = lambda s: _chunk_rev(step(s))
    tok = lambda c: pl.BlockSpec((NB, CH, DNW), lambda s: (0, c(s), 0))
    gat = lambda c: pl.BlockSpec((NB, CH, 128), lambda s: (0, c(s), 0))
    sta = lambda c: pl.BlockSpec((NB, None, 128, DNW), lambda s: (0, c(s), 0, 0))
    inv = lambda c: pl.BlockSpec((NB, None, SR, SR), lambda s: (0, c(s), 0, 0))
    return (tok(cf), gat(cf), sta(cf), inv(cf)), (tok(cb), gat(cb), sta(cb), inv(cb))


def _const_specs():
    return [pl.BlockSpec((2, 6, SR, SR), lambda s: (0, 0, 0, 0)), pl.BlockSpec((2, CH, CH), lambda s: (0, 0, 0)),
            pl.BlockSpec((SR, DNW), lambda s: (0, 0))]


SCAN_CHAINS = [(b, d) for b in range(NB) for d in range(2)]


def _stack(ref, b):
    return jnp.concatenate([ref[b, :, 128 * h:128 * h + 128] for h in range(DNH)], axis=0)


def _unstack_to(ref, b, val):
    for h in range(DNH):
        ref[b, :, 128 * h:128 * h + 128] = val[CH * h:CH * h + CH]


def _gather_plan(ins, outs, pos):
    me = _slot(pos)
    locals_, xfers = [], []
    for src, dst in zip(ins, outs):
        locals_.append((src, dst.at[me]))
        for k in range(1, 8):
            peer = _flip(pos, k)
            xfers.append((src, dst.at[me], dst.at[_slot(peer)], peer))
    return locals_, xfers


def _scan_fwd(q, k, v, gates, carry=()):
    nc = len(carry)

    def body(*refs):
        cm_ref, c64_ref, bm_ref, qf, kf, vf, gf, qb, kb, vb, gb = refs[:11]
        c_in = refs[11:11 + nc]
        of_ref, sf_ref, tf_ref, ob_ref, sb_ref, tb_ref = refs[11 + nc:17 + nc]
        c_out = refs[17 + nc:17 + 2 * nc]
        s_ref = refs[17 + 2 * nc]
        sems = refs[18 + 2 * nc:]
        if nc:
            @pl.when(pl.program_id(0) == 0)
            def _():
                _xfer_start(_gather_plan(c_in, c_out, _my_pos()), sems)

        @pl.when(pl.program_id(0) == 0)
        def _():
            s_ref[...] = jnp.zeros_like(s_ref)

        bm = bm_ref[...]
        dirs = ((qf, kf, vf, gf, of_ref, sf_ref, tf_ref), (qb, kb, vb, gb, ob_ref, sb_ref, tb_ref))
        args = [(_stack(dirs[d][0], b), _stack(dirs[d][1], b), _stack(dirs[d][2], b), dirs[d][3][b], s_ref[i], d)
                for i, (b, d) in enumerate(SCAN_CHAINS)]
        for i, (o, s_new, tm) in enumerate(_chunks(args, cm_ref, c64_ref, bm)):
            b, d = SCAN_CHAINS[i]
            dirs[d][5][b] = args[i][4]
            dirs[d][6][b] = tm.astype(BF16)
            _unstack_to(dirs[d][4], b, o)
            s_ref[i] = s_new

        if nc:
            @pl.when(pl.program_id(0) == NCH - 1)
            def _():
                _xfer_wait(_gather_plan(c_in, c_out, _my_pos()), sems)

    (tf, gf_, sf_, if_), (tb, gb_, sb_, ib_) = _scan_specs(lambda s: s)
    o_sh = jax.ShapeDtypeStruct((NB, T, DNW), F32)
    st_sh = jax.ShapeDtypeStruct((NB, NCH, 128, DNW), F32)
    inv_sh = jax.ShapeDtypeStruct((NB, NCH, SR, SR), BF16)
    hbm = pl.BlockSpec(memory_space=pltpu.HBM)
    return pl.pallas_call(
        body, name="scan_fwd", grid=(NCH,),
        out_shape=[o_sh, st_sh, inv_sh, o_sh, st_sh, inv_sh] + [jax.ShapeDtypeStruct((8,) + h.shape, h.dtype) for h in carry],
        in_specs=_const_specs() + [tf, tf, tf, gf_, tb, tb, tb, gb_] + [hbm] * nc,
        out_specs=[tf, sf_, if_, tb, sb_, ib_] + [hbm] * nc,
        scratch_shapes=[pltpu.VMEM((len(SCAN_CHAINS), 128, DNW), F32)] + (_xfer_sems(nc, 7 * nc) if nc else []),
        compiler_params=_cp(vmem=VMEM_BIG),
    )(*_scan_masks(), q, k, v, gates, q, k, v, gates, *carry)


def _scan_bwd(q, k, v, gates, st_f, st_b, inv_f, inv_b, do, carry=()):
    nc = len(carry)

    def body(*refs):
        (cm_ref, c64_ref, bm_ref, qf, kf, vf, gf, sf, tf_, dof, qb, kb, vb, gb, sb, tb_, dob) = refs[:17]
        c_in = refs[17:17 + nc]
        dqf, dkf, dvf, dgf, dqb, dkb, dvb, dgb = refs[17 + nc:25 + nc]
        c_out = refs[25 + nc:25 + 2 * nc]
        ds_ref = refs[25 + 2 * nc]
        sems = refs[26 + 2 * nc:]
        first = pl.program_id(0) == 0
        last = pl.program_id(0) == NCH - 1
        if nc:
            @pl.when(first)
            def _():
                _xfer_start(_scatter_plan(c_in, c_out, _my_pos()), sems)

        @pl.when(first)
        def _():
            ds_ref[...] = jnp.zeros_like(ds_ref)

        bm = bm_ref[...]
        dirs = ((qf, kf, vf, gf, sf, dof, dqf, dkf, dvf, dgf), (qb, kb, vb, gb, sb, dob, dqb, dkb, dvb, dgb))
        prim = tuple((_stack(dirs[d][0], b), _stack(dirs[d][1], b), _stack(dirs[d][2], b), dirs[d][3][b], dirs[d][4][b])
                     for b, d in SCAN_CHAINS)
        cots = tuple((_stack(dirs[d][5], b), ds_ref[i]) for i, (b, d) in enumerate(SCAN_CHAINS))
        saved = tuple((tf_, tb_)[d][b] for b, d in SCAN_CHAINS)
        _, vjp = jax.vjp(lambda ps: tuple(r[:2] for r in _chunks([p + (SCAN_CHAINS[i][1],) for i, p in enumerate(ps)],
                                                                cm_ref, c64_ref, bm, saved)), prim)
        (grads,) = vjp(cots)
        for i, (dq, dk, dv, dg, ds) in enumerate(grads):
            b, d = SCAN_CHAINS[i]
            _unstack_to(dirs[d][6], b, dq)
            _unstack_to(dirs[d][7], b, dk)
            _unstack_to(dirs[d][8], b, dv)
            dirs[d][9][b] = dg
            ds_ref[i] = ds

        if nc:
            @pl.when(last)
            def _():
                _xfer_wait(_scatter_plan(c_in, c_out, _my_pos()), sems)

    (tf, gf_, sf_, if_), (tb, gb_, sb_, ib_) = _scan_specs(lambda s: NCH - 1 - s)
    o_sh = jax.ShapeDtypeStruct((NB, T, DNW), F32)
    g_sh = jax.ShapeDtypeStruct((NB, T, 128), F32)
    hbm = pl.BlockSpec(memory_space=pltpu.HBM)
    return pl.pallas_call(
        body, name="scan_bwd", grid=(NCH,),
        out_shape=[o_sh, o_sh, o_sh, g_sh] * 2 + [jax.ShapeDtypeStruct(g.shape, g.dtype) for g in carry],
        in_specs=_const_specs() + [tf, tf, tf, gf_, sf_, if_, tf, tb, tb, tb, gb_, sb_, ib_, tb] + [hbm] * nc,
        out_specs=[tf, tf, tf, gf_, tb, tb, tb, gb_] + [hbm] * nc,
        scratch_shapes=[pltpu.VMEM((len(SCAN_CHAINS), 128, DNW), F32)] + (_xfer_sems(nc, 7 * nc) if nc else []),
        compiler_params=_cp(vmem=VMEM_BIG),
    )(*_scan_masks(), q, k, v, gates, st_f, inv_f, do, q, k, v, gates, st_b, inv_b, do, *carry)


def _segmean(a, bd):
    hi = a.astype(BF16)
    lo = (a - hi.astype(F32)).astype(BF16)
    return jnp.dot(hi, bd, preferred_element_type=F32) + jnp.dot(lo, bd, preferred_element_type=F32)


def _na_prep_fwd(p, qw, kw, bd):
    def body(q_ref, k_ref, v_ref, qw_ref, kw_ref, bd_ref, oq_ref, ok_ref, ov_ref):
        bdm = bd_ref[...]
        q = q_ref[...]
        k = k_ref[...]
        oq_ref[...] = (q * lax.rsqrt(_segmean(q * q, bdm) + EPS) * qw_ref[...] * 0.125).astype(BF16)
        ok_ref[...] = (k * lax.rsqrt(_segmean(k * k, bdm) + EPS) * kw_ref[...]).astype(BF16)
        ov_ref[...] = v_ref[...].astype(BF16)

    ps = lambda o: pl.BlockSpec((None, RT, NAW), lambda b, j: (b, j, COL_NA // NAW + o))
    one = pl.BlockSpec((1, NAW), lambda b, j: (0, 0))
    os_ = pl.BlockSpec((None, RT, NAW), lambda b, j: (b, j, 0))
    return pl.pallas_call(
        body, name="na_prep_fwd", grid=(NB, T // RT), out_shape=[jax.ShapeDtypeStruct((NB, T, NAW), BF16)] * 3,
        in_specs=[ps(0), ps(1), ps(2), one, one, pl.BlockSpec((NAW, NAW), lambda b, j: (0, 0))], out_specs=[os_] * 3,
    )(p, p, p, qw, kw, bd)


def _na_prep_bwd(p, qw, kw, bd, dq, dk, dv):
    def body(q_ref, k_ref, qw_ref, kw_ref, bd_ref, dq_ref, dk_ref, dv_ref, o_ref, gq_ref, gk_ref):
        b, j = pl.program_id(0), pl.program_id(1)
        bdm = bd_ref[...]

        def one(x, w, dy, scale):
            r = lax.rsqrt(_segmean(x * x, bdm) + EPS)
            g = dy * w * scale
            dx = r * g - x * (r * r * r) * _segmean(g * x, bdm)
            dw = jnp.sum(dy * x * r * scale, axis=0, keepdims=True)
            return dx, dw

        dqv = jnp.where(j == 0, 0.0, dq_ref[...])
        dxq, dwq = one(q_ref[...], qw_ref[...], dqv, 0.125)
        dxk, dwk = one(k_ref[...], kw_ref[...], dk_ref[...], 1.0)
        o_ref[...] = jnp.concatenate([dxq, dxk, dv_ref[...]], axis=1)
        pad = jnp.zeros((7, NAW), F32)

        @pl.when((b == 0) & (j == 0))
        def _():
            gq_ref[...] = jnp.zeros_like(gq_ref)
            gk_ref[...] = jnp.zeros_like(gk_ref)

        gq_ref[...] += jnp.concatenate([dwq, pad], axis=0)
        gk_ref[...] += jnp.concatenate([dwk, pad], axis=0)

    ps = lambda o: pl.BlockSpec((None, RT, NAW), lambda b, j: (b, j, COL_NA // NAW + o))
    one = pl.BlockSpec((1, NAW), lambda b, j: (0, 0))
    ts = pl.BlockSpec((None, RT, NAW), lambda b, j: (b, j, 0))
    acc = pl.BlockSpec((8, NAW), lambda b, j: (0, 0))
    return pl.pallas_call(
        body, name="na_prep_bwd", grid=(NB, T // RT),
        out_shape=[jax.ShapeDtypeStruct((NB, T, 1536), F32), jax.ShapeDtypeStruct((8, NAW), F32),
                   jax.ShapeDtypeStruct((8, NAW), F32)],
        in_specs=[ps(0), ps(1), one, one, pl.BlockSpec((NAW, NAW), lambda b, j: (0, 0)),
                  pl.BlockSpec((None, RT, NAW), lambda b, j: (b, jnp.maximum(j - 1, 0), 0)), ts, ts],
        out_specs=[pl.BlockSpec((None, RT, 1536), lambda b, j: (b, j, 0)), acc, acc],
        compiler_params=_cp(vmem=VMEM_BIG),
    )(p, p, qw, kw, bd, dq, dk, dv)


def _na_row0(r):
    return jnp.clip(r - WIN_ROWS // 2, 0, ROWS - WIN_ROWS)


def _na_cls(r):
    return _na_row0(r) - r + WIN_ROWS - 1


NHS = 4
NLW = NHS * 64
NSR = NHS * GRID_W


def _na_stack(x):
    lane = lax.broadcasted_iota(jnp.int32, (1, NLW), 1)
    return jnp.concatenate([jnp.where((lane >> 6) == h, x, jnp.zeros_like(x)) for h in range(NHS)], axis=0)


def _na_unstack(y):
    lane = lax.broadcasted_iota(jnp.int32, (1, NLW), 1)
    out = jnp.zeros((GRID_W, NLW), y.dtype)
    for h in range(NHS):
        out = jnp.where((lane >> 6) == h, y[GRID_W * h:GRID_W * h + GRID_W], out)
    return out


def _na_fwd_gen(q, kw, vw, kc, vc, bias):
    q4 = _na_stack(q)
    s_w = lax.dot_general(q4, kw, _NT, preferred_element_type=F32) + bias
    yield
    s_c = lax.dot_general(q4, kc, _NT, preferred_element_type=F32)
    yield
    mx = jnp.maximum(jnp.max(s_w, axis=1, keepdims=True), jnp.max(s_c, axis=1, keepdims=True))
    p_w = jnp.exp(s_w - mx)
    p_c = jnp.exp(s_c - mx)
    inv = 1.0 / (jnp.sum(p_w, axis=1, keepdims=True) + jnp.sum(p_c, axis=1, keepdims=True))
    yield
    o_w = jnp.dot(p_w.astype(BF16), vw, preferred_element_type=F32)
    yield
    o4 = (o_w + jnp.dot(p_c.astype(BF16), vc, preferred_element_type=F32)) * inv
    return _na_unstack(o4)


NRF = 2


def _na_fwd(qn, kn, vb, bias, carry=()):
    nc = len(carry)
    grid = (NB, ROWS // NRF)
    ng = NAH // NHS

    def body(*refs):
        q_ref, k_ref, v_ref = refs[:3]
        b_refs, rest = refs[3:3 + NRF], refs[3 + NRF:]
        c_in, o_ref, c_out, sems = rest[:nc], rest[nc], rest[nc + 1:2 * nc + 1], rest[2 * nc + 1:]
        ids = [pl.program_id(a) for a in range(2)]
        if nc:
            @pl.when((ids[0] == 0) & (ids[1] == 0))
            def _():
                _xfer_start(_gather_plan(c_in, c_out, _my_pos()), sems)

        gens = []
        for i in range(NRF):
            start = pl.multiple_of(CT + _na_row0(NRF * ids[1] + i) * GRID_W, GRID_W)
            rw = slice(GRID_W * i, GRID_W * (i + 1))
            for g in range(ng):
                ln = slice(NLW * g, NLW * g + NLW)
                gens.append(_na_fwd_gen(q_ref[rw, ln], k_ref[pl.ds(start, NWIN), ln], v_ref[pl.ds(start, NWIN), ln],
                                        k_ref[0:CT, ln], v_ref[0:CT, ln], b_refs[i][g]))
        for n, o in enumerate(_lockstep(gens)):
            i, g = n // ng, n % ng
            o_ref[GRID_W * i:GRID_W * (i + 1), NLW * g:NLW * g + NLW] = o.astype(BF16)

        if nc:
            @pl.when((ids[0] == grid[0] - 1) & (ids[1] == grid[1] - 1))
            def _():
                _xfer_wait(_gather_plan(c_in, c_out, _my_pos()), sems)

    qs = pl.BlockSpec((None, NRF * GRID_W, NAW), lambda b, j: (b, CT // (NRF * GRID_W) + j, 0))
    kv = pl.BlockSpec((None, T, NAW), lambda b, j: (b, 0, 0))
    bss = [pl.BlockSpec((None, ng, NSR, NWIN), lambda b, j, i=i: (_na_cls(NRF * j + i), 0, 0, 0)) for i in range(NRF)]
    lat = pl.BlockSpec((None, NRF * GRID_W, NAW), lambda b, j: (b, j, 0))
    hbm = pl.BlockSpec(memory_space=pltpu.HBM)
    res = pl.pallas_call(
        body, name="na_fwd", grid=grid,
        out_shape=[jax.ShapeDtypeStruct((NB, SEQ, NAW), BF16)] + [jax.ShapeDtypeStruct((8,) + h.shape, h.dtype) for h in carry],
        in_specs=[qs, kv, kv] + bss + [hbm] * nc, out_specs=[lat] + [hbm] * nc,
        scratch_shapes=_xfer_sems(nc, 7 * nc) if nc else [],
        compiler_params=_cp(vmem=VMEM_BIG),
    )(qn, kn, vb, *([bias] * NRF), *carry)
    return res[0], list(res[1:])


def _na_bwd_gen(q, kw, vw, kc, vc, bias, do):
    q4 = _na_stack(q)
    s_w = lax.dot_general(q4, kw, _NT, preferred_element_type=F32) + bias
    yield
    s_c = lax.dot_general(q4, kc, _NT, preferred_element_type=F32)
    yield
    mx = jnp.maximum(jnp.max(s_w, axis=1, keepdims=True), jnp.max(s_c, axis=1, keepdims=True))
    p_w = jnp.exp(s_w - mx)
    p_c = jnp.exp(s_c - mx)
    inv = 1.0 / (jnp.sum(p_w, axis=1, keepdims=True) + jnp.sum(p_c, axis=1, keepdims=True))
    p_w = p_w * inv
    p_c = p_c * inv
    do4 = _na_stack(do.astype(BF16))
    dp_w = lax.dot_general(do4, vw, _NT, preferred_element_type=F32)
    yield
    dp_c = lax.dot_general(do4, vc, _NT, preferred_element_type=F32)
    yield
    delta = jnp.sum(p_w * dp_w, axis=1, keepdims=True) + jnp.sum(p_c * dp_c, axis=1, keepdims=True)
    ds_w = p_w * (dp_w - delta)
    ds_c = p_c * (dp_c - delta)
    ds_wb, ds_cb = ds_w.astype(BF16), ds_c.astype(BF16)
    dq4 = jnp.dot(ds_wb, kw, preferred_element_type=F32)
    yield
    dq4 = dq4 + jnp.dot(ds_cb, kc, preferred_element_type=F32)
    yield
    dkw = lax.dot_general(ds_wb, q4, _TN, preferred_element_type=F32)
    yield
    dvw = lax.dot_general(p_w.astype(BF16), do4, _TN, preferred_element_type=F32)
    yield
    dkc = lax.dot_general(ds_cb, q4, _TN, preferred_element_type=F32)
    yield
    dvc = lax.dot_general(p_c.astype(BF16), do4, _TN, preferred_element_type=F32)
    return ds_w, _na_unstack(dq4), dkw, dvw, dkc, dvc


NRS = 4


def _na_bwd(qn, kn, vb, bias, do):
    def body(q_ref, k_ref, v_ref, *rest):
        b_refs, (do_ref, dq_ref, dk_ref, dv_ref, db_ref) = rest[:NRS], rest[NRS:]
        b, j = pl.program_id(1), pl.program_id(2)

        @pl.when((b == 0) & (j == 0))
        def _():
            db_ref[...] = jnp.zeros_like(db_ref)

        @pl.when(j == 0)
        def _():
            dk_ref[...] = jnp.zeros_like(dk_ref)
            dv_ref[...] = jnp.zeros_like(dv_ref)

        kc, vc = k_ref[0:CT, :], v_ref[0:CT, :]
        rows = [NRS * j + i for i in range(NRS)]
        starts = [pl.multiple_of(CT + _na_row0(r) * GRID_W, GRID_W) for r in rows]
        gens = [_na_bwd_gen(q_ref[GRID_W * i:GRID_W * (i + 1), :], k_ref[pl.ds(starts[i], NWIN), :],
                            v_ref[pl.ds(starts[i], NWIN), :], kc, vc, b_refs[i][...],
                            do_ref[GRID_W * i:GRID_W * (i + 1), :]) for i in range(NRS)]
        for i, (ds_w, dq, dkw, dvw, dkc, dvc) in enumerate(_lockstep(gens)):
            db_ref[_na_cls(rows[i])] += ds_w
            dq_ref[GRID_W * i:GRID_W * (i + 1), :] = dq
            dk_ref[pl.ds(starts[i], NWIN), :] += dkw
            dv_ref[pl.ds(starts[i], NWIN), :] += dvw
            dk_ref[0:CT, :] += dkc
            dv_ref[0:CT, :] += dvc

    qs = pl.BlockSpec((None, NRS * GRID_W, NLW), lambda g, b, j: (b, CT // (NRS * GRID_W) + j, g))
    kv = pl.BlockSpec((None, T, NLW), lambda g, b, j: (b, 0, g))
    bss = [pl.BlockSpec((None, None, NSR, NWIN), lambda g, b, j, i=i: (_na_cls(NRS * j + i), g, 0, 0)) for i in range(NRS)]
    lat = pl.BlockSpec((None, NRS * GRID_W, NLW), lambda g, b, j: (b, j, g))
    return pl.pallas_call(
        body, name="na_bwd", grid=(NAH // NHS, NB, ROWS // NRS),
        out_shape=[jax.ShapeDtypeStruct((NB, SEQ, NAW), F32)] + [jax.ShapeDtypeStruct((NB, T, NAW), F32)] * 2
        + [jax.ShapeDtypeStruct((WIN_ROWS, NAH // NHS, NSR, NWIN), F32)],
        in_specs=[qs, kv, kv] + bss + [lat],
        out_specs=[lat, kv, kv, pl.BlockSpec((WIN_ROWS, None, NSR, NWIN), lambda g, b, j: (0, g, 0, 0))],
        compiler_params=_cp(vmem=VMEM_BIG),
    )(qn, kn, vb, *([bias] * NRS), do)


def _na_tables():
    cols = np.arange(GRID_W)
    win_start = np.clip(cols - WIN_COLS // 2, 0, GRID_W - WIN_COLS)
    kc = cols[None, :]
    valid = (kc >= win_start[:, None]) & (kc < win_start[:, None] + WIN_COLS)
    rel = np.clip(kc - cols[:, None] + WIN_COLS - 1, 0, 2 * WIN_COLS - 2)
    return valid, rel


def _bias_table(rpb):
    valid, rel = _na_tables()
    nrel = 2 * WIN_COLS - 1
    shifts = np.stack([(rel == r) & valid for r in range(nrel)]).astype(np.float32)
    neg = np.where(valid, 0.0, NEG).astype(np.float32)

    def body(rpb_ref, sh_ref, neg_ref, o_ref):
        h = pl.program_id(0)
        for dr in range(2 * WIN_ROWS - 1):
            tile = neg_ref[...]
            for r in range(nrel):
                tile = tile + rpb_ref[h, dr, r] * sh_ref[r]
            for cls in range(WIN_ROWS):
                i = dr - cls
                if 0 <= i < WIN_ROWS:
                    o_ref[cls, :, i * GRID_W:(i + 1) * GRID_W] = tile

    return pl.pallas_call(
        body, name="bias_table", grid=(NAH,),
        out_shape=jax.ShapeDtypeStruct((WIN_ROWS, NAH // NHS, NSR, NWIN), F32),
        in_specs=[pl.BlockSpec(memory_space=pltpu.SMEM), pl.BlockSpec((nrel, GRID_W, GRID_W), lambda h: (0, 0, 0)),
                  pl.BlockSpec((GRID_W, GRID_W), lambda h: (0, 0))],
        out_specs=pl.BlockSpec((WIN_ROWS, None, GRID_W, NWIN), lambda h: (0, h // NHS, h % NHS, 0)),
    )(rpb, jnp.asarray(shifts), jnp.asarray(neg))


def _rpb_fold(db):
    def body1(d_ref, o_ref):
        for dr in range(2 * WIN_ROWS - 1):
            acc = None
            for c in range(WIN_ROWS):
                i = dr - c
                if 0 <= i < WIN_ROWS:
                    blk = d_ref[c, :, i * GRID_W:(i + 1) * GRID_W]
                    acc = blk if acc is None else acc + blk
            o_ref[dr] = acc

    t15 = pl.pallas_call(
        body1, name="rpb_fold_rows", grid=(NAH,),
        out_shape=jax.ShapeDtypeStruct((NAH, 2 * WIN_ROWS - 1, GRID_W, GRID_W), F32),
        in_specs=[pl.BlockSpec((WIN_ROWS, None, GRID_W, NWIN), lambda h: (0, h, 0, 0))],
        out_specs=pl.BlockSpec((None, 2 * WIN_ROWS - 1, GRID_W, GRID_W), lambda h: (h, 0, 0, 0)),
        compiler_params=_cp(vmem=VMEM_BIG),
    )(db.reshape(WIN_ROWS, NAH, GRID_W, NWIN))
    valid, rel = _na_tables()
    onehot = np.zeros((GRID_W * GRID_W, 128), np.float32)
    flat_rel, flat_valid = rel.reshape(-1), valid.reshape(-1)
    onehot[np.arange(GRID_W * GRID_W)[flat_valid], flat_rel[flat_valid]] = 1.0

    def body2(a_ref, oh_ref, o_ref):
        o_ref[...] = _hmm(a_ref[...], oh_ref[...])

    vm = pl.BlockSpec(memory_space=pltpu.VMEM)
    out = pl.pallas_call(
        body2, name="rpb_fold_cols", out_shape=jax.ShapeDtypeStruct((NAH * 15, 128), F32),
        in_specs=[vm, vm], out_specs=vm, compiler_params=_cp(vmem=VMEM_BIG),
    )(t15.reshape(NAH * 15, GRID_W * GRID_W), jnp.asarray(onehot))
    return out[:, :31].reshape(NAH, 15, 31)


def _dn_out(o, z, w):
    outs = []
    for h in range(DNH):
        sl = slice(128 * h, 128 * h + 128)
        oh = o[:, sl]
        r = lax.rsqrt(jnp.mean(oh * oh, axis=-1, keepdims=True) + EPS)
        outs.append(oh * r * w[:, sl] * _silu(z[:, sl]))
    return jnp.concatenate(outs, axis=1)


def _outproj_fwd(of, ob, p, na, onw, x, g1, w_out):
    def body(of_ref, ob_ref, z_ref, na_ref, w_ref, x_ref, g_ref, wo_ref, x1_ref, ai_ref, ao_ref):
        dn = _dn_out(of_ref[...] + ob_ref[...], z_ref[...], w_ref[...])
        ai = jnp.concatenate([dn.astype(BF16), na_ref[...]], axis=1)
        ai_ref[...] = ai
        ao = jnp.dot(ai, wo_ref[...], preferred_element_type=F32)
        ao_ref[...] = ao
        x1_ref[...] = x_ref[...] + g_ref[...] * ao

    lat = lambda w: pl.BlockSpec((None, RT, w), lambda b, j: (b, j, 0))
    return pl.pallas_call(
        body, name="outproj_fwd", grid=(NB, SEQ // RT),
        out_shape=[jax.ShapeDtypeStruct((NB, SEQ, D), F32), jax.ShapeDtypeStruct((NB, SEQ, D), BF16),
                   jax.ShapeDtypeStruct((NB, SEQ, D), F32)],
        in_specs=[pl.BlockSpec((None, RT, DNW), lambda b, j: (b, j + 1, 0)),
                  pl.BlockSpec((None, RT, DNW), lambda b, j: (b, j + 1, 0)),
                  pl.BlockSpec((None, RT, DNW), lambda b, j: (b, j + 1, COL_Z // DNW)),
                  lat(NAW), pl.BlockSpec((1, DNW), lambda b, j: (0, 0)), lat(D),
                  pl.BlockSpec((None, 1, D), lambda b, j: (b, 0, 0)), pl.BlockSpec((D, D), lambda b, j: (0, 0))],
        out_specs=[lat(D), lat(D), lat(D)],
        compiler_params=_cp(vmem=VMEM_BIG),
    )(of, ob, p, na, onw, x, g1, w_out)


def _outproj_bwd(dx1, ao, g1, w_out, of, ob, p, onw):
    def body(dx_ref, ao_ref, g_ref, wo_ref, of_ref, ob_ref, z_ref, w_ref, dao_ref, do_ref, dz_ref, dna_ref, dg_ref, dw_ref):
        b, j = pl.program_id(0), pl.program_id(1)

        @pl.when((b == 0) & (j == 0))
        def _():
            dw_ref[...] = jnp.zeros_like(dw_ref)

        @pl.when(j == 0)
        def _():
            do_ref[...] = jnp.zeros_like(do_ref)
            dz_ref[...] = jnp.zeros_like(dz_ref)

        @pl.when(j == 1)
        def _():
            dg_ref[...] = jnp.zeros_like(dg_ref)

        @pl.when(j >= 1)
        def _():
            dx = dx_ref[...]
            dg_ref[...] += jnp.sum(dx * ao_ref[...], axis=0, keepdims=True)
            dao = (g_ref[...] * dx).astype(BF16)
            dao_ref[...] = dao
            dai = lax.dot_general(dao, wo_ref[...], _NT, preferred_element_type=F32)
            dna_ref[...] = dai[:, DNW:]
            _, vjp = jax.vjp(_dn_out, of_ref[...] + ob_ref[...], z_ref[...], w_ref[...])
            do, dz, dw = vjp(dai[:, :DNW])
            do_ref[...] = do
            dz_ref[...] = dz
            dw_ref[...] += jnp.concatenate([dw, jnp.zeros((7, DNW), F32)], axis=0)

    lat = lambda w: pl.BlockSpec((None, RT, w), lambda b, j: (b, jnp.maximum(j - 1, 0), 0))
    tok = lambda w: pl.BlockSpec((None, RT, w), lambda b, j: (b, j, 0))
    return pl.pallas_call(
        body, name="outproj_bwd", grid=(NB, T // RT),
        out_shape=[jax.ShapeDtypeStruct((NB, SEQ, D), BF16), jax.ShapeDtypeStruct((NB, T, DNW), F32),
                   jax.ShapeDtypeStruct((NB, T, DNW), F32), jax.ShapeDtypeStruct((NB, SEQ, NAW), F32),
                   jax.ShapeDtypeStruct((NB, 1, D), F32), jax.ShapeDtypeStruct((8, DNW), F32)],
        in_specs=[_rows2d_spec(D), lat(D), pl.BlockSpec((None, 1, D), lambda b, j: (b, 0, 0)),
                  pl.BlockSpec((D, D), lambda b, j: (0, 0)),
                  tok(DNW), tok(DNW),
                  pl.BlockSpec((None, RT, DNW), lambda b, j: (b, j, COL_Z // DNW)),
                  pl.BlockSpec((1, DNW), lambda b, j: (0, 0))],
        out_specs=[lat(D), tok(DNW), tok(DNW), lat(NAW), pl.BlockSpec((None, 1, D), lambda b, j: (b, 0, 0)),
                   pl.BlockSpec((8, DNW), lambda b, j: (0, 0))],
        compiler_params=_cp(vmem=VMEM_BIG),
    )(dx1, ao, g1, w_out, of, ob, p, onw)


FT_FWD, FT_BWD = 512, 256
NFF = DFF // FFT


def _ffn_specs(ft):
    row = lambda w: pl.BlockSpec((ft, w), lambda i, f: (i, 0))
    ex = pl.BlockSpec((None, 1, D), lambda i, f: (i // (SEQ // ft), 0, 0))
    one = pl.BlockSpec((1, D), lambda i, f: (0, 0))
    wg = pl.BlockSpec((None, D, FFT), lambda i, f: (f, 0, 0))
    wu = pl.BlockSpec((None, D, FFT), lambda i, f: (NFF + f, 0, 0))
    wo = pl.BlockSpec((FFT, D), lambda i, f: (f, 0))
    ff = pl.BlockSpec((ft, FFT), lambda i, f: (i, f))
    return row, ex, one, wg, wu, wo, ff


def _ffn_fwd(x1, sh2, sc2, g2, n2w, w_in4, w_out, target):
    ft, tpe = FT_FWD, SEQ // FT_FWD

    def body(x_ref, sh_ref, sc_ref, g_ref, w2_ref, wg_ref, wu_ref, wo_ref, t_ref,
             h2_ref, gt_ref, up_ref, dy_ref, dg_ref, loss_ref, acc_ref):
        i, f = pl.program_id(0), pl.program_id(1)

        @pl.when(f == 0)
        def _():
            h2_ref[...] = _norm_mod(x_ref[...], w2_ref[...], sh_ref[...], sc_ref[...]).astype(BF16)
            acc_ref[...] = jnp.zeros_like(acc_ref)

        h2 = h2_ref[...]
        gate = jnp.dot(h2, wg_ref[...], preferred_element_type=F32)
        up = jnp.dot(h2, wu_ref[...], preferred_element_type=F32)
        gt_ref[...] = gate.astype(BF16)
        up_ref[...] = up.astype(BF16)
        acc_ref[...] += jnp.dot((_silu(gate) * up).astype(BF16), wo_ref[...], preferred_element_type=F32)

        @pl.when((i == 0) & (f == 0))
        def _():
            loss_ref[...] = jnp.zeros_like(loss_ref)

        @pl.when(f == NFF - 1)
        def _():
            ffn = acc_ref[...]
            err = x_ref[...] + g_ref[...] * ffn - t_ref[...]
            dy = err * (1.0 / D)
            dy_ref[...] = dy
            part = jnp.sum(dy * ffn, axis=0, keepdims=True)
            loss_ref[...] += jnp.sum(jnp.sum(err * err, axis=1, keepdims=True), axis=0, keepdims=True) * (0.5 / D)

            @pl.when(i % tpe == 0)
            def _():
                dg_ref[...] = part

            @pl.when(i % tpe != 0)
            def _():
                dg_ref[...] += part

    row, ex, one, wg, wu, wo, ff = _ffn_specs(ft)
    return pl.pallas_call(
        body, name="ffn_fwd", grid=(NTOK // ft, NFF),
        out_shape=[jax.ShapeDtypeStruct((NTOK, D), BF16), jax.ShapeDtypeStruct((NTOK, DFF), BF16),
                   jax.ShapeDtypeStruct((NTOK, DFF), BF16), jax.ShapeDtypeStruct((NTOK, D), F32),
                   jax.ShapeDtypeStruct((NB, 1, D), F32), jax.ShapeDtypeStruct((8, 128), F32)],
        in_specs=[row(D), ex, ex, ex, one, wg, wu, wo, row(D)],
        out_specs=[row(D), ff, ff, row(D), ex, pl.BlockSpec((8, 128), lambda i, f: (0, 0))],
        scratch_shapes=[pltpu.VMEM((ft, D), F32)],
        compiler_params=_cp(vmem=VMEM_BIG),
    )(x1, sh2, sc2, g2, n2w, w_in4, w_in4, w_out, target)


def _rows2d_spec(width):
    return pl.BlockSpec((RT, width), lambda b, j: (b * (SEQ // RT) + jnp.maximum(j - 1, 0), 0))


def _ffn_bwd(dy, gate, up, x1, sh2, sc2, g2, n2w, w_in4, w_out):
    ft, tpe = FT_BWD, SEQ // FT_BWD
    ni = NTOK // ft

    def body(dy_ref, gt_ref, up_ref, x_ref, sh_ref, sc_ref, g_ref, w2_ref, wg_ref, wu_ref, wo_ref,
             dff_ref, act_ref, dgu_ref, dx_ref, dsh_ref, dsc_ref, dw_ref, acc_ref):
        f, i = pl.program_id(0), pl.program_id(1)
        rows = pl.ds(pl.multiple_of(i * ft, ft), ft)
        dff = (g_ref[...] * dy_ref[...]).astype(BF16)
        dff_ref[...] = dff
        dact = lax.dot_general(dff, wo_ref[...], _NT, preferred_element_type=F32)
        gate, up = gt_ref[...].astype(F32), up_ref[...].astype(F32)
        sg = _sigmoid(gate)
        sl = gate * sg
        act_ref[...] = (sl * up).astype(BF16)
        dgate = (dact * up * (sg * (1.0 + gate * (1.0 - sg)))).astype(BF16)
        dup = (dact * sl).astype(BF16)
        dgu_ref[0] = dgate
        dgu_ref[1] = dup
        part = (lax.dot_general(dgate, wg_ref[...], _NT, preferred_element_type=F32)
                + lax.dot_general(dup, wu_ref[...], _NT, preferred_element_type=F32))

        @pl.when(f == 0)
        def _():
            acc_ref[rows, :] = part

        @pl.when(f > 0)
        def _():
            acc_ref[rows, :] += part

        @pl.when((i == 0) & (f == 0))
        def _():
            dw_ref[...] = jnp.zeros_like(dw_ref)

        @pl.when(f == NFF - 1)
        def _():
            _, vjp = jax.vjp(_norm_mod, x_ref[...], w2_ref[...], sh_ref[...], sc_ref[...])
            dx, dw, dsh, dsc = vjp(acc_ref[rows, :])
            dx_ref[...] = dy_ref[...] + dx
            dw_ref[...] += dw

            @pl.when(i % tpe == 0)
            def _():
                dsh_ref[...] = dsh
                dsc_ref[...] = dsc

            @pl.when(i % tpe != 0)
            def _():
                dsh_ref[...] += dsh
                dsc_ref[...] += dsc

    last = lambda f, real, spare: jnp.where(f == NFF - 1, real, spare)
    row = lambda w: pl.BlockSpec((ft, w), lambda f, i: (i, 0))
    ex = pl.BlockSpec((None, 1, D), lambda f, i: (i // tpe, 0, 0))
    one = pl.BlockSpec((1, D), lambda f, i: (0, 0))
    wg = pl.BlockSpec((None, D, FFT), lambda f, i: (f, 0, 0))
    wu = pl.BlockSpec((None, D, FFT), lambda f, i: (NFF + f, 0, 0))
    wo = pl.BlockSpec((FFT, D), lambda f, i: (f, 0))
    ff = pl.BlockSpec((ft, FFT), lambda f, i: (i, f))
    row_once = lambda w: pl.BlockSpec((ft, w), lambda f, i: (last(f, i, ni), 0))
    ex_once = pl.BlockSpec((None, 1, D), lambda f, i: (last(f, i // tpe, NB), 0, 0))
    return pl.pallas_call(
        body, name="ffn_bwd", grid=(NFF, ni),
        out_shape=[jax.ShapeDtypeStruct((NTOK + ft, D), BF16), jax.ShapeDtypeStruct((NTOK, DFF), BF16),
                   jax.ShapeDtypeStruct((2, NTOK, DFF), BF16), jax.ShapeDtypeStruct((NTOK + ft, D), F32),
                   jax.ShapeDtypeStruct((NB + 1, 1, D), F32), jax.ShapeDtypeStruct((NB + 1, 1, D), F32),
                   jax.ShapeDtypeStruct((1, D), F32)],
        in_specs=[row(D), ff, ff, row(D), ex, ex, ex, one, wg, wu, wo],
        out_specs=[row_once(D), ff, pl.BlockSpec((2, ft, FFT), lambda f, i: (0, i, f)), row_once(D), ex_once, ex_once, one],
        scratch_shapes=[pltpu.VMEM((NTOK, D), F32)],
        compiler_params=_cp(vmem=VMEM_BIG),
    )(dy, gate, up, x1, sh2, sc2, g2, n2w, w_in4, w_in4, w_out)


def _local_step(x, ctx, target, mod_x, mod_c, norm1_w, norm2_w, w_in_p, conv_w, a_log, dt_bias, out_norm_w,
                q_norm_w, k_norm_w, rpb, w_out_b, w_ffn_in4, w_ffn_out_b, scatter_early=False, gather_late=None):
    sh1, sc1, g1, sh2, sc2, g2 = [mod_x[:, i * D:(i + 1) * D].reshape(NB, 1, D) for i in range(6)]
    csh1, csc1 = mod_c[0:D].reshape(1, 1, D), mod_c[D:2 * D].reshape(1, 1, D)
    cw = jnp.concatenate([conv_w, jnp.zeros((3, 1536), F32)], axis=0)
    prm = jnp.pad(jnp.stack([a_log.reshape(8), dt_bias.reshape(8)]), ((0, 6), (0, 120)))
    onw = jnp.tile(out_norm_w.reshape(1, 128), (1, DNH))
    qw = jnp.tile(q_norm_w.reshape(1, 64), (1, NAH))
    kw = jnp.tile(k_norm_w.reshape(1, 64), (1, NAH))
    seg = np.arange(NAW) // 64
    bd = jnp.asarray((seg[:, None] == seg[None, :]).astype(np.float32) / 64.0, dtype=BF16)
    bias = _bias_table(rpb)

    p, hm = _inproj_fwd(x, ctx, sh1, sc1, csh1, csc1, norm1_w, w_in_p)
    dq_, dk_, dv_ = _dn_prep_fwd(p, cw)
    gates = _gates_fwd(p, prm)
    res = _scan_fwd(dq_, dk_, dv_, gates, carry=gather_late[:2] if gather_late else ())
    o_f, st_f, inv_f, o_b, st_b, inv_b = res[:6]
    qn, kn, vb = _na_prep_fwd(p, qw, kw, bd)
    na, got = _na_fwd(qn, kn, vb, bias, carry=gather_late[2:] if gather_late else ())
    if gather_late:
        w_out_b = res[6].reshape(D, D)
        w_ffn_in4 = res[7].reshape(4, D, FFT)
        w_ffn_out_b = got[0].reshape(DFF, D)
    x1, ai, ao = _outproj_fwd(o_f, o_b, p, na, onw, x, g1, w_out_b)
    x1f = x1.reshape(NTOK, D)
    h2, gate, up, dy, dg2, loss8 = _ffn_fwd(x1f, sh2, sc2, g2, norm2_w, w_ffn_in4, w_ffn_out_b,
                                             target.reshape(NTOK, D))
    dff, act, dgu, dx1, dsh2, dsc2, dn2w = _ffn_bwd(dy, gate, up, x1f, sh2, sc2, g2, norm2_w, w_ffn_in4, w_ffn_out_b)
    g_w_ffn_out = _matmul_tn(act, dff, FFT, D, 512, "wgrad_ffn_out")
    g_w_ffn_in4 = _matmul_tn(
        h2, dgu, D, FFT, 1024, "wgrad_ffn_in", nn=2 * NFF,
        b_block=(None, 1024, FFT), b_map=lambda i, j, k: (j // NFF, k, j % NFF),
        out_shape=(4, D, FFT), out_block=(None, D, FFT), out_map=lambda i, j, k: (j, i, 0))
    dsh2, dsc2 = dsh2[:NB], dsc2[:NB]
    dao, do, dz, dna, dg1, donw = _outproj_bwd(dx1, ao, g1, w_out_b, o_f, o_b, p, onw)
    g_w_out = _matmul_tn(ai.reshape(NTOK, D), dao.reshape(NTOK, D), D, D, 1024, "wgrad_out")
    dqn, dkn, dvn, dbias = _na_bwd(qn, kn, vb, bias, dna)
    dp_na, dqw, dkw = _na_prep_bwd(p, qw, kw, bd, dqn, dkn, dvn)
    early = [g_w_out.reshape(8, D // 8, D), g_w_ffn_in4.reshape(8, D // 2, FFT), g_w_ffn_out.reshape(8, DFF // 8, D)]
    res = _scan_bwd(dq_, dk_, dv_, gates, st_f, st_b, inv_f, inv_b, do, carry=early if scatter_early else ())
    dqf, dkf, dvf, dgf, dqb, dkb, dvb, dgb = res[:8]
    if scatter_early:
        g_w_out, g_w_ffn_in4, g_w_ffn_out = res[8:]
    dpq, dpk, dpv, gcq, gck, gcv = _dn_prep_bwd(p, cw, (dqf, dqb), (dkf, dkb), (dvf, dvb))
    dgt, dprm = _gates_bwd(p, prm, dgf, dgb)
    dpb, grad_x, dsh1, dsc1, dcsh1, dcsc1, dn1w = _inproj_bwd(
        (dpq, dpk, dpv), dz, dp_na, dgt, w_in_p, x, ctx, sh1, sc1, csh1, csc1, norm1_w, dx1)
    g_w_in_p = _matmul_tn(dpb.reshape(NB * T, PC), hm.reshape(NB * T, D), PC, D, 512, "wgrad_in")
    g_rpb = _rpb_fold(dbias)

    dmod_x = jnp.concatenate([dsh1, dsc1, dg1, dsh2, dsc2, dg2], axis=2).reshape(NB, 6 * D)
    dmod_c = jnp.concatenate([dcsh1.reshape(D), dcsc1.reshape(D), jnp.zeros((4 * D,), F32)])
    g_conv = jnp.concatenate([gcq[:5], gck[:5], gcv[:5]], axis=1)
    small = dict(
        norm1_w=dn1w, norm2_w=dn2w, a_log=dprm[0, :8], dt_bias=dprm[1, :8],
        out_norm_w=donw[0].reshape(DNH, 128).sum(0), q_norm_w=dqw[0].reshape(NAH, 64).sum(0),
        k_norm_w=dkw[0].reshape(NAH, 64).sum(0), rpb=g_rpb, conv_w=g_conv)
    return loss8[0, 0], grad_x, dmod_x, dmod_c, small, g_w_in_p, g_w_out, g_w_ffn_in4, g_w_ffn_out


def _perm_w_in(wt):
    return jnp.concatenate([wt[:2048], wt[2064:3600], wt[2048:2064], jnp.zeros((PC - IN_COLS, wt.shape[1]), wt.dtype)],
                           axis=0)


def _unperm_w_in(gt):
    return jnp.concatenate([gt[:2048], gt[COL_G:COL_G + 16], gt[2048:COL_G]], axis=0)


def _misc_row(a_log, dt_bias, onw, qnw, knw):
    return jnp.concatenate([a_log.reshape(8), dt_bias.reshape(8), jnp.zeros((112,), F32), onw.reshape(128),
                            qnw.reshape(64), knw.reshape(64), jnp.zeros((D - 384,), F32)])


def _pack_small(c_ctx, n1, n2, b_ada, a_log, dt_bias, onw, qnw, knw, rpb, conv_s):
    misc = _misc_row(a_log, dt_bias, onw, qnw, knw)
    rp = jnp.concatenate([rpb.reshape(-1), jnp.zeros((4 * D - 3720,), F32)])
    cv = jnp.concatenate([conv_s.reshape(-1), jnp.zeros((2 * D - 1920,), F32)])
    return jnp.concatenate([c_ctx.reshape(-1), n1.reshape(-1), n2.reshape(-1), b_ada.reshape(-1), misc, rp, cv]).reshape(16, D)


def _unpack_small(a):
    f = a.reshape(-1)
    misc = f[9 * D:10 * D]
    return dict(
        c_ctx=f[0:D], norm1_w=f[D:2 * D].reshape(1, D), norm2_w=f[2 * D:3 * D].reshape(1, D),
        b_ada=f[3 * D:9 * D].reshape(1, 6 * D), dn_A_log=misc[0:8].reshape(1, 2, 4), dn_dt_bias=misc[8:16].reshape(1, 2, 4),
        dn_out_norm_w=misc[128:256].reshape(1, 128), na_q_norm_w=misc[256:320].reshape(1, 64),
        na_k_norm_w=misc[320:384].reshape(1, 64), na_rpb=f[10 * D:10 * D + 3720].reshape(1, 8, 15, 31),
        dn_conv_w=f[14 * D:14 * D + 1920].reshape(1, 5, 384))


_WEIGHTS = ['c_ctx', 'norm1_w', 'norm2_w', 'w_ada', 'b_ada', 'w_in', 'dn_conv_w', 'dn_A_log', 'dn_dt_bias',
            'dn_out_norm_w', 'na_q_norm_w', 'na_k_norm_w', 'na_rpb', 'w_out', 'w_ffn_in', 'w_ffn_out']


def kernel(x, c, ctx, c_ctx, norm1_w, norm2_w, w_ada, b_ada, w_in, dn_conv_w, dn_A_log, dn_dt_bias, dn_out_norm_w, na_q_norm_w, na_k_norm_w, na_rpb, w_out, w_ffn_in, w_ffn_out, loss_target, m_c_ctx, m_norm1_w, m_norm2_w, m_w_ada, m_b_ada, m_w_in, m_dn_conv_w, m_dn_A_log, m_dn_dt_bias, m_dn_out_norm_w, m_na_q_norm_w, m_na_k_norm_w, m_na_rpb, m_w_out, m_w_ffn_in, m_w_ffn_out, v_c_ctx, v_norm1_w, v_norm2_w, v_w_ada, v_b_ada, v_w_in, v_dn_conv_w, v_dn_A_log, v_dn_dt_bias, v_dn_out_norm_w, v_na_q_norm_w, v_na_k_norm_w, v_na_rpb, v_w_out, v_w_ffn_in, v_w_ffn_out):
    ix, iy, ic = lax.axis_index("x"), lax.axis_index("y"), lax.axis_index("c")
    chip = 2 * ix + iy
    me = 4 * ix + 2 * iy + ic

    crow = jnp.concatenate([c, c_ctx.reshape(1, D), jnp.zeros((5, D), F32)], axis=0)
    cvrow = jnp.concatenate([dn_conv_w[0].reshape(-1), jnp.zeros((2 * D - 1920,), F32)]).reshape(2, D)
    g0 = _allgather8(jnp.concatenate([crow, cvrow, jnp.zeros((6, D), F32)], axis=0), "gather_cond")
    cin = g0[:, :8].reshape(64, D)
    conv_full = jnp.concatenate([g0[2 * s, 8:10].reshape(-1)[:1920].reshape(5, 384) for s in range(4)], axis=1)

    w_ada_s = w_ada[0]
    b_s = lax.dynamic_slice(b_ada, (0, chip * 1536), (1, 1536))
    mod_s = _ada_fwd(cin, w_ada_s, b_s)
    mods = _gather_rows(mod_s, "gather_mod")
    mod_mine = jnp.concatenate([mods[0], mods[2], mods[4], mods[6]], axis=1)
    mod_x, mod_c = mod_mine[0:NB], mod_mine[2]

    def half(w2d):
        r = w2d.shape[0] // 2
        return lax.dynamic_slice(w2d, (ic * r, 0), (r, w2d.shape[1]))

    w_in_t = w_in[0].T
    (gw_in,) = _gather_weights([_cast_bf16(lax.dynamic_slice(w_in_t, (0, ic * (D // 2)), (900, D // 2)), "cast_w_in")])
    late = [_cast_bf16(half(w_out[0]), "cast_w_out"), _cast_bf16(half(w_ffn_in[0]), "cast_w_ffn_in"),
            _cast_bf16(half(w_ffn_out[0]), "cast_w_ffn_out")]
    w_in_p = _perm_w_in(gw_in.reshape(4, 2, 900, D // 2).transpose(0, 2, 1, 3).reshape(IN_COLS, D))

    (loss_part, grad_x, dmod_x, dmod_c, small, g_w_in_p, r_w_out, r_w_ffn_in, r_w_ffn_out) = _local_step(
        x, ctx, loss_target, mod_x, mod_c, norm1_w, norm2_w, w_in_p, conv_full, dn_A_log[0], dn_dt_bias[0],
        dn_out_norm_w[0], na_q_norm_w[0], na_k_norm_w[0], na_rpb[0], None, None, None,
        scatter_early=True, gather_late=late)
    loss = lax.psum(loss_part, ("x", "y", "c"))

    dm = jnp.concatenate([dmod_x, dmod_c.reshape(1, 6 * D), jnp.zeros((5, 6 * D), F32)], axis=0).reshape(48, D)
    misc = _misc_row(small["a_log"], small["dt_bias"], small["out_norm_w"], small["q_norm_w"], small["k_norm_w"])
    sm = jnp.concatenate([
        small["norm1_w"].reshape(-1), small["norm2_w"].reshape(-1), misc,
        small["rpb"].reshape(-1), jnp.zeros((4 * D - 3720,), F32),
        small["conv_w"].reshape(-1), jnp.zeros((8 * D - 7680,), F32), jnp.zeros((D,), F32)]).reshape(16, D)
    gsl, gtot = _allgather8(jnp.concatenate([dm, sm], axis=0), "gather_small", with_sum=True)
    gmod = gsl[:, :48].reshape(64, 6 * D)
    st = gtot[48:64].reshape(-1)
    g_n1, g_n2, g_misc = st[0:D].reshape(1, D), st[D:2 * D].reshape(1, D), st[2 * D:3 * D]
    g_rpb = st[3 * D:3 * D + 3720].reshape(1, 8, 15, 31)
    g_conv_full = st[7 * D:7 * D + 7680].reshape(5, 1536)
    g_conv_s = lax.dynamic_slice(g_conv_full, (0, chip * 384), (5, 384)).reshape(1, 5, 384)

    gmod_s = lax.dynamic_slice(gmod, (0, chip * 1536), (64, 1536))
    g_w_ada_s, pctx = _ada_bwd(cin, gmod_s, w_ada_s)
    parts = _allgather8(pctx, "gather_cctx")
    g_b_ada, g_c_ctx = _ada_small(gmod, parts, c_ctx.reshape(1, D))

    g_w_in8 = _unperm_w_in(g_w_in_p).reshape(4, 900, 2, D // 2).transpose(0, 2, 1, 3).reshape(8, 900, D // 2)
    names = ["w_in", "w_out", "w_ffn_in", "w_ffn_out"]
    rs = list(_scatter_grads([g_w_in8])) + [r_w_out, r_w_ffn_in, r_w_ffn_out]
    full = _sibling_exchange([_sum_slots(r, "sum_" + nm) for r, nm in zip(rs, names)])
    g_w_in_t = full[0].transpose(1, 0, 2).reshape(900, D)
    g_w_out_s = full[1].reshape(1, D // 4, D)
    g_w_ffn_in_s = full[2].reshape(1, D, FFT)
    g_w_ffn_out_s = full[3].reshape(1, DFF // 4, D)

    grads = dict(
        c_ctx=g_c_ctx.reshape(D), norm1_w=g_n1, norm2_w=g_n2, w_ada=g_w_ada_s.reshape(1, D, 1536), b_ada=g_b_ada,
        w_in=g_w_in_t, dn_conv_w=g_conv_s, dn_A_log=g_misc[0:8].reshape(1, 2, 4), dn_dt_bias=g_misc[8:16].reshape(1, 2, 4),
        dn_out_norm_w=g_misc[128:256].reshape(1, 128), na_q_norm_w=g_misc[256:320].reshape(1, 64),
        na_k_norm_w=g_misc[320:384].reshape(1, 64), na_rpb=g_rpb, w_out=g_w_out_s, w_ffn_in=g_w_ffn_in_s,
        w_ffn_out=g_w_ffn_out_s)
    pk = lambda d_: _pack_small(d_["c_ctx"], d_["norm1_w"], d_["norm2_w"], d_["b_ada"], d_["dn_A_log"], d_["dn_dt_bias"],
                                d_["dn_out_norm_w"], d_["na_q_norm_w"], d_["na_k_norm_w"], d_["na_rpb"], d_["dn_conv_w"])
    wts = dict(c_ctx=c_ctx, norm1_w=norm1_w, norm2_w=norm2_w, b_ada=b_ada, dn_A_log=dn_A_log, dn_dt_bias=dn_dt_bias,
               dn_out_norm_w=dn_out_norm_w, na_q_norm_w=na_q_norm_w, na_k_norm_w=na_k_norm_w, na_rpb=na_rpb, dn_conv_w=dn_conv_w)
    ms = dict(c_ctx=m_c_ctx, norm1_w=m_norm1_w, norm2_w=m_norm2_w, b_ada=m_b_ada, dn_A_log=m_dn_A_log, dn_dt_bias=m_dn_dt_bias,
              dn_out_norm_w=m_dn_out_norm_w, na_q_norm_w=m_na_q_norm_w, na_k_norm_w=m_na_k_norm_w, na_rpb=m_na_rpb,
              dn_conv_w=m_dn_conv_w)
    vs = dict(c_ctx=v_c_ctx, norm1_w=v_norm1_w, norm2_w=v_norm2_w, b_ada=v_b_ada, dn_A_log=v_dn_A_log, dn_dt_bias=v_dn_dt_bias,
              dn_out_norm_w=v_dn_out_norm_w, na_q_norm_w=v_na_q_norm_w, na_k_norm_w=v_na_k_norm_w, na_rpb=v_na_rpb,
              dn_conv_w=v_dn_conv_w)
    sd, snm, snv, _ = _adamw(pk(wts), pk(grads), pk(ms), pk(vs), "adamw_small")
    delta, new_m, new_v = _unpack_small(sd), _unpack_small(snm), _unpack_small(snv)
    big = dict(w_ada=(w_ada, m_w_ada, v_w_ada), w_in=(w_in, m_w_in, v_w_in), w_out=(w_out, m_w_out, v_w_out),
               w_ffn_in=(w_ffn_in, m_w_ffn_in, v_w_ffn_in), w_ffn_out=(w_ffn_out, m_w_ffn_out, v_w_ffn_out))
    for nm, (w_, m_, v_) in big.items():
        shp = w_.shape
        if nm == "w_in":
            r2, back = (lambda a: a[0].T), (lambda a: a.T.reshape(shp))
            res = _adamw(w_in_t, grads[nm], r2(m_), r2(v_), "adamw_" + nm)
        else:
            r2, back = (lambda a: a.reshape(shp[1], shp[2])), (lambda a: a.reshape(shp))
            res = _adamw(r2(w_), r2(grads[nm]), r2(m_), r2(v_), "adamw_" + nm)
        delta[nm], new_m[nm], new_v[nm], grads[nm] = [back(a) for a in res]

    return (loss, grad_x, *[grads[n] for n in _WEIGHTS], *[delta[n] for n in _WEIGHTS],
            *[new_m[n] for n in _WEIGHTS], *[new_v[n] for n in _WEIGHTS])
```

```python
import functools

import numpy as np
import jax
import jax.numpy as jnp
from jax import lax
from jax.experimental import pallas as pl
from jax.experimental.pallas import tpu as pltpu

F32 = jnp.float32
BF16 = jnp.bfloat16

D = 1024
NB = 2
SEQ = 2048
CT = 256
T = CT + SEQ
NTOK = NB * SEQ
CH = 64
NCH = T // CH
NCH_CTX = CT // CH
DNH, DNW = 4, 512
NAH, NAW = 8, 512
GRID_W = 64
ROWS = SEQ // GRID_W
WIN_ROWS = 8
WIN_COLS = 16
NWIN = WIN_ROWS * GRID_W
DFF = 2816
FFT = 1408
IN_COLS = 3600
PC = 3712
COL_Z, COL_NA, COL_G = 1536, 2048, 3584
EPS = 1e-6
NEG = -1e30
RT = 256
VMEM_BIG = 56 * 1024 * 1024
MESH = pl.DeviceIdType.MESH

ADAM_LR, ADAM_B1, ADAM_B2, ADAM_EPS, ADAM_WD, ADAM_STEP = 0.001, 0.9, 0.999, 1e-08, 0.01, 10


def _cp(vmem=None, sem=None):
    kw = {}
    if vmem is not None:
        kw["vmem_limit_bytes"] = vmem
    if sem is not None:
        kw["dimension_semantics"] = sem
    return pltpu.CompilerParams(**kw)


_NN = (((1,), (0,)), ((), ()))
_NT = (((1,), (1,)), ((), ()))
_TN = (((0,), (0,)), ((), ()))


def _dg(a, b, dims):
    return lax.dot_general(a.astype(BF16), b.astype(BF16), dims, preferred_element_type=F32)


@jax.custom_vjp
def _mm(a, b):
    return _dg(a, b, _NN)


_mm.defvjp(lambda a, b: (_dg(a, b, _NN), (a, b)),
           lambda r, g: (_dg(g, r[1], _NT), _dg(r[0], g, _TN)))


@jax.custom_vjp
def _mm_nt(a, b):
    return _dg(a, b, _NT)


_mm_nt.defvjp(lambda a, b: (_dg(a, b, _NT), (a, b)),
              lambda r, g: (_dg(g, r[1], _NN), _dg(g, r[0], _TN)))


@jax.custom_vjp
def _mm_tn(a, b):
    return _dg(a, b, _TN)


_mm_tn.defvjp(lambda a, b: (_dg(a, b, _TN), (a, b)),
              lambda r, g: (_dg(r[1], g, _NT), _dg(r[0], g, _NN)))


def _split_dot(a, b, dims):
    hi = b.astype(BF16)
    lo = (b - hi.astype(F32)).astype(BF16)
    ab = a.astype(BF16)
    return (lax.dot_general(ab, hi, dims, preferred_element_type=F32)
            + lax.dot_general(ab, lo, dims, preferred_element_type=F32))


@jax.custom_vjp
def _mask_mm(mask, b):
    return _split_dot(mask, b, _NN)


_mask_mm.defvjp(lambda mask, b: (_split_dot(mask, b, _NN), mask),
                lambda mask, g: (jnp.zeros_like(mask), _split_dot(mask, g, _TN)))


def _hmm(a, b):
    return jnp.dot(a, b, precision=lax.Precision.HIGHEST, preferred_element_type=F32)


def _sigmoid(x):
    return 1.0 / (1.0 + jnp.exp(-x))


def _silu(x):
    return x * _sigmoid(x)


def _norm_mod(x, w, sh, sc):
    r = lax.rsqrt(jnp.mean(x * x, axis=-1, keepdims=True) + EPS)
    return (x * r * w) * (1.0 + sc) + sh


def _my_pos():
    return lax.axis_index("x"), lax.axis_index("y"), lax.axis_index("c")


def _flip(pos, k):
    x, y, c = pos
    return (1 - x if k & 4 else x, 1 - y if k & 2 else y, 1 - c if k & 1 else c)


def _slot(pos):
    return 4 * pos[0] + 2 * pos[1] + pos[2]


def _allgather8(v, name, with_sum=False):
    R, W = v.shape

    def body(v_ref, out_ref, *rest):
        if with_sum:
            tot_ref, send_sems, recv_sems, lsem = rest
        else:
            send_sems, recv_sems, lsem = rest
        me = _my_pos()
        mine = pltpu.make_async_copy(v_ref, out_ref.at[_slot(me)], lsem)
        mine.start()
        sends = []
        for k in range(1, 8):
            cp = pltpu.make_async_remote_copy(
                src_ref=v_ref, dst_ref=out_ref.at[_slot(me)], send_sem=send_sems.at[k - 1],
                recv_sem=recv_sems.at[k - 1], device_id=_flip(me, k), device_id_type=MESH)
            cp.start()
            sends.append(cp)
        for k in range(1, 8):
            peer = _flip(me, k)
            pltpu.make_async_remote_copy(
                src_ref=v_ref, dst_ref=out_ref.at[_slot(peer)], send_sem=send_sems.at[k - 1],
                recv_sem=recv_sems.at[k - 1], device_id=peer, device_id_type=MESH).wait_recv()
        for cp in sends:
            cp.wait_send()
        mine.wait()
        if with_sum:
            acc = out_ref[0]
            for s in range(1, 8):
                acc = acc + out_ref[s]
            tot_ref[...] = acc

    out_shape = [jax.ShapeDtypeStruct((8, R, W), F32)]
    out_specs = [pl.BlockSpec(memory_space=pltpu.VMEM)]
    if with_sum:
        out_shape.append(jax.ShapeDtypeStruct((R, W), F32))
        out_specs.append(pl.BlockSpec(memory_space=pltpu.VMEM))
    res = pl.pallas_call(
        body, name=name, out_shape=out_shape,
        in_specs=[pl.BlockSpec(memory_space=pltpu.VMEM)], out_specs=out_specs,
        scratch_shapes=[pltpu.SemaphoreType.DMA((7,)), pltpu.SemaphoreType.DMA((7,)), pltpu.SemaphoreType.DMA],
        compiler_params=_cp(vmem=VMEM_BIG),
    )(v)
    return res if with_sum else res[0]


def _gather_rows(v, name):
    W = v.shape[1]

    def body(v_ref, out_ref, send_sems, recv_sems, lsem):
        pos = _my_pos()
        me = _slot(pos)
        rows = lambda p: v_ref.at[pl.ds(pl.multiple_of(8 * p, 8), 8)]
        locals_ = [(rows(me), out_ref.at[me])]
        xfers = [(rows(_slot(_flip(pos, k))), out_ref.at[me], out_ref.at[_slot(_flip(pos, k))], _flip(pos, k))
                 for k in range(1, 8)]
        _xfer_start((locals_, xfers), (send_sems, recv_sems, lsem))
        _xfer_wait((locals_, xfers), (send_sems, recv_sems, lsem))

    vm = pl.BlockSpec(memory_space=pltpu.VMEM)
    return pl.pallas_call(
        body, name=name, out_shape=jax.ShapeDtypeStruct((8, 8, W), F32), in_specs=[vm], out_specs=vm,
        scratch_shapes=_xfer_sems(1, 7),
    )(v)


def _gather_weights(halves):
    n = len(halves)

    def body(*refs):
        ins, outs = refs[:n], refs[n:2 * n]
        send_sems, recv_sems, lsems = refs[2 * n:]
        me = _my_pos()
        x, y, c = me
        sib = (x, y, 1 - c)
        chips = [4, 2, 6]

        def cp(a, k, block, to, src=None):
            dst = outs[a].at[_slot(block)]
            return pltpu.make_async_remote_copy(
                src_ref=dst if src is None else src, dst_ref=dst, send_sem=send_sems.at[a, k],
                recv_sem=recv_sems.at[a, k], device_id=to, device_id_type=MESH)

        mine, first, passed = [], [], []
        for a in range(n):
            m = pltpu.make_async_copy(ins[a], outs[a].at[_slot(me)], lsems.at[a])
            m.start()
            mine.append(m)
            first.append(cp(a, 0, me, sib, src=ins[a]))
            for j, k in enumerate(chips):
                first.append(cp(a, 1 + j, me, _flip(me, k), src=ins[a]))
        for f in first:
            f.start()
        for j, k in enumerate(chips):
            for a in range(n):
                cp(a, 1 + j, _flip(me, k), me).wait_recv()
                p = cp(a, 4 + j, _flip(me, k), sib)
                p.start()
                passed.append(p)
        for a in range(n):
            cp(a, 0, sib, me).wait_recv()
            for j, k in enumerate(chips):
                cp(a, 4 + j, _flip(sib, k), me).wait_recv()
        for f in first + passed:
            f.wait_send()
        for m in mine:
            m.wait()

    vm = pl.BlockSpec(memory_space=pltpu.VMEM)
    return pl.pallas_call(
        body, name="gather_weights",
        out_shape=[jax.ShapeDtypeStruct((8,) + h.shape, h.dtype) for h in halves],
        in_specs=[vm] * n, out_specs=[vm] * n,
        scratch_shapes=[pltpu.SemaphoreType.DMA((n, 7)), pltpu.SemaphoreType.DMA((n, 7)),
                        pltpu.SemaphoreType.DMA((n,))],
        compiler_params=_cp(vmem=VMEM_BIG),
    )(*halves)


def _xfer_start(plan, sems):
    send_sems, recv_sems, lsems = sems
    locals_, xfers = plan
    for i, (src, dst) in enumerate(locals_):
        pltpu.make_async_copy(src, dst, lsems.at[i]).start()
    for i, (src, dst, _, peer) in enumerate(xfers):
        pltpu.make_async_remote_copy(src_ref=src, dst_ref=dst, send_sem=send_sems.at[i], recv_sem=recv_sems.at[i],
                                     device_id=peer, device_id_type=MESH).start()


def _xfer_wait(plan, sems):
    send_sems, recv_sems, lsems = sems
    locals_, xfers = plan
    for i, (src, _, landing, peer) in enumerate(xfers):
        pltpu.make_async_remote_copy(src_ref=src, dst_ref=landing, send_sem=send_sems.at[i], recv_sem=recv_sems.at[i],
                                     device_id=peer, device_id_type=MESH).wait_recv()
    for i, (src, dst, _, peer) in enumerate(xfers):
        pltpu.make_async_remote_copy(src_ref=src, dst_ref=dst, send_sem=send_sems.at[i], recv_sem=recv_sems.at[i],
                                     device_id=peer, device_id_type=MESH).wait_send()
    for i, (src, dst) in enumerate(locals_):
        pltpu.make_async_copy(src, dst, lsems.at[i]).wait()


def _xfer_sems(n_local, n_xfer):
    return [pltpu.SemaphoreType.DMA((n_xfer,)), pltpu.SemaphoreType.DMA((n_xfer,)), pltpu.SemaphoreType.DMA((n_local,))]


def _scatter_plan(ins, outs, pos):
    me = _slot(pos)
    locals_, xfers = [], []
    for src, dst in zip(ins, outs):
        locals_.append((src.at[me], dst.at[me]))
        for k in range(1, 8):
            peer = _flip(pos, k)
            xfers.append((src.at[_slot(peer)], dst.at[me], dst.at[_slot(peer)], peer))
    return locals_, xfers


def _scatter_grads(grads):
    n = len(grads)

    def body(*refs):
        plan = _scatter_plan(refs[:n], refs[n:2 * n], _my_pos())
        _xfer_start(plan, refs[2 * n:])
        _xfer_wait(plan, refs[2 * n:])

    hbm = pl.BlockSpec(memory_space=pltpu.HBM)
    return pl.pallas_call(
        body, name="scatter_grads", out_shape=[jax.ShapeDtypeStruct(g.shape, g.dtype) for g in grads],
        in_specs=[hbm] * n, out_specs=[hbm] * n, scratch_shapes=_xfer_sems(n, 7 * n),
    )(*grads)


def _sibling_exchange(halves):
    n = len(halves)

    def body(*refs):
        ins, outs = refs[:n], refs[n:2 * n]
        send_sems, recv_sems, lsems = refs[2 * n:]
        x, y, c = _my_pos()
        sib = (x, y, 1 - c)
        mine, sends = [], []
        for a in range(n):
            m = pltpu.make_async_copy(ins[a], outs[a].at[c], lsems.at[a])
            m.start()
            mine.append(m)
            cp = pltpu.make_async_remote_copy(
                src_ref=ins[a], dst_ref=outs[a].at[c], send_sem=send_sems.at[a], recv_sem=recv_sems.at[a],
                device_id=sib, device_id_type=MESH)
            cp.start()
            sends.append(cp)
        for a in range(n):
            pltpu.make_async_remote_copy(
                src_ref=ins[a], dst_ref=outs[a].at[1 - c], send_sem=send_sems.at[a], recv_sem=recv_sems.at[a],
                device_id=sib, device_id_type=MESH).wait_recv()
        for cp in sends:
            cp.wait_send()
        for m in mine:
            m.wait()

    vm = pl.BlockSpec(memory_space=pltpu.VMEM)
    return pl.pallas_call(
        body, name="sibling_exchange",
        out_shape=[jax.ShapeDtypeStruct((2,) + h.shape, h.dtype) for h in halves],
        in_specs=[vm] * n, out_specs=[vm] * n,
        scratch_shapes=[pltpu.SemaphoreType.DMA((n,)), pltpu.SemaphoreType.DMA((n,)), pltpu.SemaphoreType.DMA((n,))],
        compiler_params=_cp(vmem=VMEM_BIG),
    )(*halves)


def _row_tile(r, cap=512):
    for t in (cap, 256, 128, 64, 32, 16, 8):
        if t <= cap and r % t == 0:
            return t
    return r


def _cast_bf16(a, name):
    R, W = a.shape
    tr = _row_tile(R, 256)

    def body(a_ref, o_ref):
        o_ref[...] = a_ref[...].astype(BF16)

    return pl.pallas_call(
        body, name=name, grid=(R // tr,), out_shape=jax.ShapeDtypeStruct((R, W), BF16),
        in_specs=[pl.BlockSpec((tr, W), lambda i: (i, 0))], out_specs=pl.BlockSpec((tr, W), lambda i: (i, 0)),
    )(a)


def _sum_slots(r, name):
    _, R, W = r.shape
    tr = _row_tile(R, 128)

    def body(r_ref, o_ref):
        acc = r_ref[0].astype(F32)
        for s_ in range(1, 8):
            acc = acc + r_ref[s_].astype(F32)
        o_ref[...] = acc.astype(BF16)

    return pl.pallas_call(
        body, name=name, grid=(R // tr,), out_shape=jax.ShapeDtypeStruct((R, W), BF16),
        in_specs=[pl.BlockSpec((8, tr, W), lambda i: (0, i, 0))], out_specs=pl.BlockSpec((tr, W), lambda i: (i, 0)),
        compiler_params=_cp(vmem=VMEM_BIG),
    )(r)


def _adamw(w, g, m, v, name):
    R, W = w.shape
    tr = _row_tile(R, 256)
    tc = 256 if (tr == R and R * W * 4 > (1 << 21) and W % 256 == 0) else W
    c1 = 1.0 / (1.0 - ADAM_B1 ** ADAM_STEP)
    c2 = 1.0 / (1.0 - ADAM_B2 ** ADAM_STEP)

    def body(w_ref, g_ref, m_ref, v_ref, d_ref, nm_ref, nv_ref, g32_ref):
        gg = g_ref[...].astype(F32)
        g32_ref[...] = gg
        nm = ADAM_B1 * m_ref[...] + (1.0 - ADAM_B1) * gg
        nv = ADAM_B2 * v_ref[...] + (1.0 - ADAM_B2) * (gg * gg)
        d_ref[...] = -ADAM_LR * ((nm * c1) / (jnp.sqrt(nv * c2) + ADAM_EPS) + ADAM_WD * w_ref[...])
        nm_ref[...] = nm
        nv_ref[...] = nv

    spec = pl.BlockSpec((tr, tc), lambda i, j: (i, j))
    return pl.pallas_call(
        body, name=name, grid=(R // tr, W // tc), out_shape=[jax.ShapeDtypeStruct((R, W), F32)] * 4,
        in_specs=[spec] * 4, out_specs=[spec] * 4, compiler_params=_cp(vmem=VMEM_BIG),
    )(w, g, m, v)


def _matmul_tn(a, b, tm, tn, tk, name, nn=1, b_map=None, out_shape=None, out_block=None, out_map=None,
               b_block=None, k_rows=None):
    K, M = (k_rows or a.shape[0]), a.shape[1]
    nk = K // tk
    nm = M // tm

    def body(a_ref, b_ref, o_ref, acc_ref):
        kk = pl.program_id(2)

        @pl.when(kk == 0)
        def _():
            acc_ref[...] = jnp.zeros_like(acc_ref)

        acc_ref[...] += lax.dot_general(a_ref[...], b_ref[...], _TN, preferred_element_type=F32)

        @pl.when(kk == nk - 1)
        def _():
            o_ref[...] = acc_ref[...].astype(BF16)

    return pl.pallas_call(
        body, name=name, grid=(nm, nn, nk),
        out_shape=jax.ShapeDtypeStruct(out_shape or (M, b.shape[-1]), BF16),
        in_specs=[pl.BlockSpec((tk, tm), lambda i, j, k: (k, i)),
                  pl.BlockSpec(b_block or (tk, tn), b_map or (lambda i, j, k: (k, j)))],
        out_specs=pl.BlockSpec(out_block or (tm, tn), out_map or (lambda i, j, k: (i, j))),
        scratch_shapes=[pltpu.VMEM((tm, tn), F32)],
        compiler_params=_cp(vmem=VMEM_BIG),
    )(a, b)


def _ada_fwd(cin, w_ada_s, b_s):
    tn = 512

    def body(c_ref, w_ref, b_ref, o_ref):
        o_ref[...] = _dg(_silu(c_ref[...]), w_ref[...], _NN) + b_ref[...]

    return pl.pallas_call(
        body, name="ada_fwd", grid=(1536 // tn,), out_shape=jax.ShapeDtypeStruct((64, 1536), F32),
        in_specs=[pl.BlockSpec((64, D), lambda j: (0, 0)), pl.BlockSpec((D, tn), lambda j: (0, j)),
                  pl.BlockSpec((1, tn), lambda j: (0, j))],
        out_specs=pl.BlockSpec((64, tn), lambda j: (0, j)),
        compiler_params=_cp(vmem=VMEM_BIG),
    )(cin, w_ada_s, b_s)


def _ada_bwd(cin, gmod_s, w_ada_s):
    tn = 512

    def body(c_ref, g_ref, w_ref, gw_ref, pc_ref):
        j = pl.program_id(0)
        g = g_ref[...]
        gw_ref[...] = _dg(_silu(c_ref[...]), g, _TN)
        rr = lax.broadcasted_iota(jnp.int32, (8, 64), 0)
        cc = lax.broadcasted_iota(jnp.int32, (8, 64), 1)
        sel = jnp.where((rr == 0) & ((cc & 7) == 2), 1.0, 0.0).astype(F32)
        dctx = _hmm(sel, g)
        part = _dg(dctx, w_ref[...], _NT)

        @pl.when(j == 0)
        def _():
            pc_ref[...] = jnp.zeros_like(pc_ref)

        pc_ref[...] += part

    return pl.pallas_call(
        body, name="ada_bwd", grid=(1536 // tn,),
        out_shape=[jax.ShapeDtypeStruct((D, 1536), F32), jax.ShapeDtypeStruct((8, D), F32)],
        in_specs=[pl.BlockSpec((64, D), lambda j: (0, 0)), pl.BlockSpec((64, tn), lambda j: (0, j)),
                  pl.BlockSpec((D, tn), lambda j: (0, j))],
        out_specs=[pl.BlockSpec((D, tn), lambda j: (0, j)), pl.BlockSpec((8, D), lambda j: (0, 0))],
        compiler_params=_cp(vmem=VMEM_BIG),
    )(cin, gmod_s, w_ada_s)


def _ada_small(gmod, parts, c_ctx):
    def body(g_ref, p_ref, c_ref, gb_ref, gc_ref):
        gb_ref[...] = jnp.sum(g_ref[...], axis=0, keepdims=True)
        ds = p_ref[0][0:1] + p_ref[2][0:1] + p_ref[4][0:1] + p_ref[6][0:1]
        cc = c_ref[...]
        s = _sigmoid(cc)
        gc_ref[...] = ds * (s * (1.0 + cc * (1.0 - s)))

    vm = pl.BlockSpec(memory_space=pltpu.VMEM)
    return pl.pallas_call(
        body, name="ada_small",
        out_shape=[jax.ShapeDtypeStruct((1, 6 * D), F32), jax.ShapeDtypeStruct((1, D), F32)],
        in_specs=[vm, vm, vm], out_specs=[vm, vm],
    )(gmod, parts, c_ctx)


def _tok_specs():
    xs = pl.BlockSpec((None, RT, D), lambda b, j: (b, jnp.maximum(j - 1, 0), 0))
    cs = pl.BlockSpec((None, RT, D), lambda b, j: (b, 0, 0))
    ms = pl.BlockSpec((None, 1, D), lambda b, j: (b, 0, 0))
    cms = pl.BlockSpec((None, 1, D), lambda b, j: (0, 0, 0))
    return xs, cs, ms, cms


def _inproj_fwd(x, ctx, sh1, sc1, csh1, csc1, n1w, w_in_p):
    def body(x_ref, c_ref, sh_ref, sc_ref, csh_ref, csc_ref, w1_ref, w_ref, p_ref, hm_ref):
        is_ctx = pl.program_id(1) == 0
        xt = jnp.where(is_ctx, c_ref[...], x_ref[...])
        sh = jnp.where(is_ctx, csh_ref[...], sh_ref[...])
        sc = jnp.where(is_ctx, csc_ref[...], sc_ref[...])
        hb = _norm_mod(xt, w1_ref[...], sh, sc).astype(BF16)
        hm_ref[...] = hb
        p_ref[...] = lax.dot_general(hb, w_ref[...], _NT, preferred_element_type=F32)

    xs, cs, ms, cms = _tok_specs()
    return pl.pallas_call(
        body, name="inproj_fwd", grid=(NB, T // RT),
        out_shape=[jax.ShapeDtypeStruct((NB, T, PC), F32), jax.ShapeDtypeStruct((NB, T, D), BF16)],
        in_specs=[xs, cs, ms, ms, cms, cms, pl.BlockSpec((1, D), lambda b, j: (0, 0)),
                  pl.BlockSpec((PC, D), lambda b, j: (0, 0))],
        out_specs=[pl.BlockSpec((None, RT, PC), lambda b, j: (b, j, 0)),
                   pl.BlockSpec((None, RT, D), lambda b, j: (b, j, 0))],
        compiler_params=_cp(vmem=VMEM_BIG),
    )(x, ctx, sh1, sc1, csh1, csc1, n1w, w_in_p)


def _inproj_bwd(dp_dn, dz, dp_na, dgt, w_in_p, x, ctx, sh1, sc1, csh1, csc1, n1w, dx1):
    def body(d1q_ref, d1k_ref, d1v_ref, d2_ref, d3_ref, d4_ref, w_ref, x_ref, c_ref, sh_ref, sc_ref, csh_ref, csc_ref,
             w1_ref, dx1_ref, dpb_ref, gx_ref, dsh_ref, dsc_ref, dcsh_ref, dcsc_ref, dw1_ref):
        b, j = pl.program_id(0), pl.program_id(1)
        is_ctx = j == 0
        dp = jnp.concatenate([d1q_ref[...], d1k_ref[...], d1v_ref[...], d2_ref[...], d3_ref[...], d4_ref[...]],
                             axis=1).astype(BF16)
        dpb_ref[...] = dp
        dhm = jnp.dot(dp, w_ref[...], preferred_element_type=F32)
        xt = jnp.where(is_ctx, c_ref[...], x_ref[...])
        sh = jnp.where(is_ctx, csh_ref[...], sh_ref[...])
        sc = jnp.where(is_ctx, csc_ref[...], sc_ref[...])
        _, vjp = jax.vjp(_norm_mod, xt, w1_ref[...], sh, sc)
        dxt, dw1, dsh, dsc = vjp(dhm)

        @pl.when((b == 0) & (j == 0))
        def _():
            dw1_ref[...] = jnp.zeros_like(dw1_ref)
            dcsh_ref[...] = jnp.zeros_like(dcsh_ref)
            dcsc_ref[...] = jnp.zeros_like(dcsc_ref)

        dw1_ref[...] += dw1

        @pl.when(is_ctx)
        def _():
            dcsh_ref[...] += dsh
            dcsc_ref[...] += dsc

        @pl.when(j == 1)
        def _():
            dsh_ref[...] = jnp.zeros_like(dsh_ref)
            dsc_ref[...] = jnp.zeros_like(dsc_ref)

        @pl.when(j >= 1)
        def _():
            dsh_ref[...] += dsh
            dsc_ref[...] += dsc
            gx_ref[...] = dx1_ref[...] + dxt

    xs, cs, ms, cms = _tok_specs()
    tsp = lambda w: pl.BlockSpec((None, RT, w), lambda b, j: (b, j, 0))
    one = pl.BlockSpec((1, D), lambda b, j: (0, 0))
    return pl.pallas_call(
        body, name="inproj_bwd", grid=(NB, T // RT),
        out_shape=[jax.ShapeDtypeStruct((NB, T, PC), BF16), jax.ShapeDtypeStruct((NB, SEQ, D), F32),
                   jax.ShapeDtypeStruct((NB, 1, D), F32), jax.ShapeDtypeStruct((NB, 1, D), F32),
                   jax.ShapeDtypeStruct((1, 1, D), F32), jax.ShapeDtypeStruct((1, 1, D), F32),
                   jax.ShapeDtypeStruct((1, D), F32)],
        in_specs=[tsp(512), tsp(512), tsp(512), tsp(512), tsp(1536), tsp(128), pl.BlockSpec((PC, D), lambda b, j: (0, 0)),
                  xs, cs, ms, ms, cms, cms, one, _rows2d_spec(D)],
        out_specs=[tsp(PC), xs, ms, ms, cms, cms, one],
        compiler_params=_cp(vmem=VMEM_BIG),
    )(*dp_dn, dz, dp_na, dgt, w_in_p, x, ctx, sh1, sc1, csh1, csc1, n1w, dx1)


def _shift_masks():
    t = lax.broadcasted_iota(jnp.int32, (T, 1), 0)
    lo = jnp.where(t < CT, 0, CT)
    hi = jnp.where(t < CT, CT, T)
    return {off: (t + off >= lo) & (t + off < hi) for off in range(-2, 3)}


def _shifted(u, off, masks):
    s = u if off == 0 else pltpu.roll(u, (-off) % T, 0)
    return jnp.where(masks[off], s, 0.0)


def _conv_pre(u, w, masks):
    acc = _shifted(u, -2, masks) * w[0:1, :]
    for kk in range(1, 5):
        acc = acc + _shifted(u, kk - 2, masks) * w[kk:kk + 1, :]
    return acc


def _act_q(pre):
    a = _silu(pre)
    return a * lax.rsqrt(jnp.sum(a * a, axis=-1, keepdims=True) + EPS) * (128.0 ** -0.5)


def _act_k(pre):
    a = _silu(pre)
    return a * lax.rsqrt(jnp.sum(a * a, axis=-1, keepdims=True) + EPS)


def _dn_prep_fwd(p, cw):
    def body(pq_ref, pk_ref, pv_ref, wq_ref, wk_ref, wv_ref, q_ref, k_ref, v_ref):
        masks = _shift_masks()
        q_ref[...] = _act_q(_conv_pre(pq_ref[...], wq_ref[...], masks))
        k_ref[...] = _act_k(_conv_pre(pk_ref[...], wk_ref[...], masks))
        v_ref[...] = _silu(_conv_pre(pv_ref[...], wv_ref[...], masks))

    ps = lambda o: pl.BlockSpec((None, T, 128), lambda b, h: (b, 0, o + h))
    ws = lambda o: pl.BlockSpec((8, 128), lambda b, h: (0, o + h))
    os_ = pl.BlockSpec((None, T, 128), lambda b, h: (b, 0, h))
    return pl.pallas_call(
        body, name="dn_prep_fwd", grid=(NB, DNH), out_shape=[jax.ShapeDtypeStruct((NB, T, DNW), F32)] * 3,
        in_specs=[ps(0), ps(4), ps(8), ws(0), ws(4), ws(8)], out_specs=[os_] * 3,
        compiler_params=_cp(vmem=VMEM_BIG),
    )(p, p, p, cw, cw, cw)


def _dn_prep_bwd(p, cw, dqs, dks, dvs):
    def body(pq_ref, pk_ref, pv_ref, wq_ref, wk_ref, wv_ref, dqf_ref, dqb_ref, dkf_ref, dkb_ref, dvf_ref, dvb_ref,
             oq_ref, ok_ref, ov_ref, gq_ref, gk_ref, gv_ref):
        b = pl.program_id(1)
        masks = _shift_masks()

        def one(p_ref, w_ref, df_ref, db_ref, act, o_ref, g_ref):
            u = p_ref[...]
            w = w_ref[...]
            _, vjp = jax.vjp(act, _conv_pre(u, w, masks))
            (dpre,) = vjp(df_ref[...] + db_ref[...])
            du = _shifted(dpre, 2, masks) * w[0:1, :]
            rows = [jnp.sum(dpre * _shifted(u, -2, masks), axis=0, keepdims=True)]
            for kk in range(1, 5):
                du = du + _shifted(dpre, 2 - kk, masks) * w[kk:kk + 1, :]
                rows.append(jnp.sum(dpre * _shifted(u, kk - 2, masks), axis=0, keepdims=True))
            o_ref[...] = du
            gw = jnp.concatenate(rows + [jnp.zeros((3, 128), F32)], axis=0)

            @pl.when(b == 0)
            def _():
                g_ref[...] = gw

            @pl.when(b > 0)
            def _():
                g_ref[...] += gw

        one(pq_ref, wq_ref, dqf_ref, dqb_ref, _act_q, oq_ref, gq_ref)
        one(pk_ref, wk_ref, dkf_ref, dkb_ref, _act_k, ok_ref, gk_ref)
        one(pv_ref, wv_ref, dvf_ref, dvb_ref, _silu, ov_ref, gv_ref)

    ps = lambda o: pl.BlockSpec((None, T, 128), lambda h, b: (b, 0, o + h))
    ws = lambda o: pl.BlockSpec((8, 128), lambda h, b: (0, o + h))
    ds = pl.BlockSpec((None, T, 128), lambda h, b: (b, 0, h))
    outs = pl.pallas_call(
        body, name="dn_prep_bwd", grid=(DNH, NB),
        out_shape=[jax.ShapeDtypeStruct((NB, T, DNW), F32)] * 3 + [jax.ShapeDtypeStruct((8, DNW), F32)] * 3,
        in_specs=[ps(0), ps(4), ps(8), ws(0), ws(4), ws(8)] + [ds] * 6,
        out_specs=[ds] * 3
        + [pl.BlockSpec((8, 128), lambda h, b: (0, h))] * 3,
        compiler_params=_cp(vmem=VMEM_BIG),
    )(p, p, p, cw, cw, cw, *dqs, *dks, *dvs)
    return outs


def _gate_fn(a, prm):
    lane = lax.broadcasted_iota(jnp.int32, a.shape, 1)
    z = a + prm[1:2, :]
    sp = jnp.maximum(z, 0.0) + jnp.log(1.0 + jnp.exp(-jnp.abs(z)))
    g = -jnp.exp(prm[0:1, :]) * sp
    return jnp.where(lane < 8, g, jnp.where(lane < 16, _sigmoid(a), 0.0))


def _gates_fwd(p, prm):
    def body(a_ref, prm_ref, o_ref):
        o_ref[...] = _gate_fn(a_ref[...], prm_ref[...])

    return pl.pallas_call(
        body, name="gates_fwd", grid=(NB,), out_shape=jax.ShapeDtypeStruct((NB, T, 128), F32),
        in_specs=[pl.BlockSpec((None, T, 128), lambda b: (b, 0, COL_G // 128)), pl.BlockSpec((8, 128), lambda b: (0, 0))],
        out_specs=pl.BlockSpec((None, T, 128), lambda b: (b, 0, 0)),
    )(p, prm)


def _gates_bwd(p, prm, dgf, dgb):
    def body(a_ref, prm_ref, dgf_ref, dgb_ref, o_ref, dprm_ref):
        b = pl.program_id(0)
        tot = dgf_ref[...] + dgb_ref[...]
        _, vjp = jax.vjp(_gate_fn, a_ref[...], prm_ref[...])
        da, dprm = vjp(tot)
        o_ref[...] = da

        @pl.when(b == 0)
        def _():
            dprm_ref[...] = dprm

        @pl.when(b > 0)
        def _():
            dprm_ref[...] += dprm

    return pl.pallas_call(
        body, name="gates_bwd", grid=(NB,),
        out_shape=[jax.ShapeDtypeStruct((NB, T, 128), F32), jax.ShapeDtypeStruct((8, 128), F32)],
        in_specs=[pl.BlockSpec((None, T, 128), lambda b: (b, 0, COL_G // 128)), pl.BlockSpec((8, 128), lambda b: (0, 0)),
                  pl.BlockSpec((None, T, 128), lambda b: (b, 0, 0)), pl.BlockSpec((None, T, 128), lambda b: (b, 0, 0))],
        out_specs=[pl.BlockSpec((None, T, 128), lambda b: (b, 0, 0)), pl.BlockSpec((8, 128), lambda b: (0, 0))],
        compiler_params=_cp(vmem=VMEM_BIG),
    )(p, prm, dgf, dgb)


def _lockstep(gens, joint=None):
    n = len(gens)
    res, vals = [None] * n, [None] * n
    while True:
        msgs = []
        for i, g in enumerate(gens):
            try:
                msgs.append((g.send(vals[i]), False))
            except StopIteration as e:
                res[i] = e.value
                msgs.append((None, True))
        if msgs[0][1]:
            return res
        if msgs[0][0] is not None:
            vals = list(joint(tuple(m[0] for m in msgs)))
        else:
            vals = [None] * n


def _tri_inv_gen(m, md, m1, m2, eye):
    n = -(m * md)
    t = eye + n
    pw = n
    for _ in range(3):
        pw = _dg(pw, pw, _NN)
        yield
        t = t + _dg(t, pw, _NN)
        yield
    a = _dg(t, m * m1, _NN)
    yield
    t = t - _dg(a, t, _NN)
    yield
    a = _dg(t, m * m2, _NN)
    yield
    return t - _dg(a, t, _NN)


def _tri_inv_bwd_gen(t, g):
    tt = t.T
    a = _dg(tt, g, _NN)
    yield
    return -_dg(a, tt, _NN)


@jax.custom_vjp
def _tri_inv(ms, md, m1, m2, eye):
    return tuple(_lockstep([_tri_inv_gen(m, md, m1, m2, eye) for m in ms]))


def _tri_inv_f(ms, md, m1, m2, eye):
    ts = _tri_inv(ms, md, m1, m2, eye)
    return ts, (ts, md)


def _tri_inv_b(res, gs):
    ts, md = res
    z = jnp.zeros_like(md)
    return (tuple(_lockstep([_tri_inv_bwd_gen(t, g) for t, g in zip(ts, gs)])), z, z, z, z)


_tri_inv.defvjp(_tri_inv_f, _tri_inv_b)


@jax.custom_vjp
def _tri_inv_saved(ms, ts):
    return tuple(t.astype(F32) for t in ts)


def _tri_inv_saved_f(ms, ts):
    return _tri_inv_saved(ms, ts), ts


def _tri_inv_saved_b(ts, gs):
    dms = tuple(_lockstep([_tri_inv_bwd_gen(t, g) for t, g in zip(ts, gs)]))
    return dms, tuple(jnp.zeros_like(t) for t in ts)


_tri_inv_saved.defvjp(_tri_inv_saved_f, _tri_inv_saved_b)


SR = DNH * CH


def _scan_masks():
    i, j = np.arange(SR)[:, None], np.arange(SR)[None, :]
    same = (i >> 6) == (j >> 6)
    cm = np.zeros((2, 6, SR, SR), np.float32)
    for d in range(2):
        diff = i - j if d == 0 else j - i
        cm[d, 0] = same & (diff >= 0)
        cm[d, 1] = same & (diff > 0)
        cm[d, 2] = (i >> 4) == (j >> 4)
        cm[d, 3] = ((i >> 5) == (j >> 5)) & ((i >> 4) != (j >> 4))
        cm[d, 4] = (i >> 5) != (j >> 5)
        cm[d, 5] = i == j
    t, u = np.arange(CH)[:, None], np.arange(CH)[None, :]
    c64 = np.stack([t >= u, t <= u]).astype(np.float32)
    bm = ((np.arange(SR)[:, None] >> 6) == (np.arange(DNW)[None, :] >> 7)).astype(np.float32)
    return jnp.asarray(cm), jnp.asarray(c64), jnp.asarray(bm)


@jax.custom_vjp
def _fold(x, bm):
    xm = x * bm
    return xm[:, 0:128] + xm[:, 128:256] + xm[:, 256:384] + xm[:, 384:512]


def _fold_f(x, bm):
    return _fold(x, bm), bm


def _fold_b(bm, g):
    return jnp.concatenate([g, g, g, g], axis=1) * bm, jnp.zeros_like(bm)


_fold.defvjp(_fold_f, _fold_b)


def _chunk_gen(q, k, v, gb, s, cm, c64, bm, d):
    incl, strict = cm[0], cm[1]
    lane = lax.broadcasted_iota(jnp.int32, (1, 128), 1)
    oh = lambda n: jnp.where(lane == n, 1.0, 0.0).astype(F32)
    col = lambda x, base: jnp.concatenate(
        [jnp.sum(x * oh(base + d * DNH + h), axis=1, keepdims=True) for h in range(DNH)], axis=0)
    beta = col(gb, 8)
    gam = col(_mask_mm(c64, gb), 0)
    yield
    tot = jnp.sum(gb, axis=0, keepdims=True)
    gtots = [jnp.sum(tot * oh(d * DNH + h), axis=1, keepdims=True) for h in range(DNH)]
    gtot = jnp.concatenate([jnp.broadcast_to(t, (CH, 1)) for t in gtots], axis=0)
    eg = jnp.concatenate([jnp.broadcast_to(jnp.exp(t), (1, 128)) for t in gtots], axis=1)
    row0 = jnp.where(lax.broadcasted_iota(jnp.int32, (128, 1), 0) == 0, 1.0, 0.0).astype(F32)
    gam_row = jnp.sum(jnp.broadcast_to(gam, (SR, 128)).T * row0, axis=0, keepdims=True)
    dec = jnp.exp(jnp.where(incl > 0.5, gam - gam_row, NEG))
    e_gam = jnp.exp(gam)
    kb = k * beta
    kk = _mm_nt(kb, k)
    yield
    tm = yield strict * (kk * dec)
    u = _mm(tm, v * beta)
    yield
    w = _mm(tm, kb * e_gam)
    yield
    qk = _mm_nt(q, k) * dec
    yield
    ws = _mm(w, s)
    yield
    v_new = u - _fold(ws, bm)
    qs = _mm(q * e_gam, s)
    yield
    o = _fold(qs, bm) + _mm(qk, v_new)
    yield
    vbd = jnp.concatenate([v_new, v_new, v_new, v_new], axis=1) * bm
    s_new = s * eg + _mm_tn(k * jnp.exp(gtot - gam), vbd)
    return o, s_new, tm


def _chunks(args, cm_ref, c64_ref, bm, saved=None):
    cm0 = cm_ref[0]
    if saved is None:
        inv = lambda ms: _tri_inv(ms, cm0[2], cm0[3], cm0[4], cm0[5])
    else:
        inv = lambda ms: _tri_inv_saved(ms, saved)
    return _lockstep([_chunk_gen(q, k, v, gb, s, cm_ref[d], c64_ref[d], bm, d) for q, k, v, gb, s, d in args], inv)


def _chunk_rev(s):
    return jnp.where(s < NCH_CTX, NCH_CTX - 1 - s, NCH + NCH_CTX - 1 - s)


def _scan_specs(step):
    cf = lambda s: step(s)
    cb = lambda s: _chunk_rev(step(s))
    tok = lambda c: pl.BlockSpec((NB, CH, DNW), lambda s: (0, c(s), 0))
    gat = lambda c: pl.BlockSpec((NB, CH, 128), lambda s: (0, c(s), 0))
    sta = lambda c: pl.BlockSpec((NB, None, 128, DNW), lambda s: (0, c(s), 0, 0))
    inv = lambda c: pl.BlockSpec((NB, None, SR, SR), lambda s: (0, c(s), 0, 0))
    return (tok(cf), gat(cf), sta(cf), inv(cf)), (tok(cb), gat(cb), sta(cb), inv(cb))


def _const_specs():
    return [pl.BlockSpec((2, 6, SR, SR), lambda s: (0, 0, 0, 0)), pl.BlockSpec((2, CH, CH), lambda s: (0, 0, 0)),
            pl.BlockSpec((SR, DNW), lambda s: (0, 0))]


SCAN_CHAINS = [(b, d) for b in range(NB) for d in range(2)]


def _stack(ref, b):
    return jnp.concatenate([ref[b, :, 128 * h:128 * h + 128] for h in range(DNH)], axis=0)


def _unstack_to(ref, b, val):
    for h in range(DNH):
        ref[b, :, 128 * h:128 * h + 128] = val[CH * h:CH * h + CH]


def _gather_plan(ins, outs, pos):
    me = _slot(pos)
    locals_, xfers = [], []
    for src, dst in zip(ins, outs):
        locals_.append((src, dst.at[me]))
        for k in range(1, 8):
            peer = _flip(pos, k)
            xfers.append((src, dst.at[me], dst.at[_slot(peer)], peer))
    return locals_, xfers


def _scan_fwd(q, k, v, gates, carry=()):
    nc = len(carry)

    def body(*refs):
        cm_ref, c64_ref, bm_ref, qf, kf, vf, gf, qb, kb, vb, gb = refs[:11]
        c_in = refs[11:11 + nc]
        of_ref, sf_ref, tf_ref, ob_ref, sb_ref, tb_ref = refs[11 + nc:17 + nc]
        c_out = refs[17 + nc:17 + 2 * nc]
        s_ref = refs[17 + 2 * nc]
        sems = refs[18 + 2 * nc:]
        if nc:
            @pl.when(pl.program_id(0) == 0)
            def _():
                _xfer_start(_gather_plan(c_in, c_out, _my_pos()), sems)

        @pl.when(pl.program_id(0) == 0)
        def _():
            s_ref[...] = jnp.zeros_like(s_ref)

        bm = bm_ref[...]
        dirs = ((qf, kf, vf, gf, of_ref, sf_ref, tf_ref), (qb, kb, vb, gb, ob_ref, sb_ref, tb_ref))
        args = [(_stack(dirs[d][0], b), _stack(dirs[d][1], b), _stack(dirs[d][2], b), dirs[d][3][b], s_ref[i], d)
                for i, (b, d) in enumerate(SCAN_CHAINS)]
        for i, (o, s_new, tm) in enumerate(_chunks(args, cm_ref, c64_ref, bm)):
            b, d = SCAN_CHAINS[i]
            dirs[d][5][b] = args[i][4]
            dirs[d][6][b] = tm.astype(BF16)
            _unstack_to(dirs[d][4], b, o)
            s_ref[i] = s_new

        if nc:
            @pl.when(pl.program_id(0) == NCH - 1)
            def _():
                _xfer_wait(_gather_plan(c_in, c_out, _my_pos()), sems)

    (tf, gf_, sf_, if_), (tb, gb_, sb_, ib_) = _scan_specs(lambda s: s)
    o_sh = jax.ShapeDtypeStruct((NB, T, DNW), F32)
    st_sh = jax.ShapeDtypeStruct((NB, NCH, 128, DNW), F32)
    inv_sh = jax.ShapeDtypeStruct((NB, NCH, SR, SR), BF16)
    hbm = pl.BlockSpec(memory_space=pltpu.HBM)
    return pl.pallas_call(
        body, name="scan_fwd", grid=(NCH,),
        out_shape=[o_sh, st_sh, inv_sh, o_sh, st_sh, inv_sh] + [jax.ShapeDtypeStruct((8,) + h.shape, h.dtype) for h in carry],
        in_specs=_const_specs() + [tf, tf, tf, gf_, tb, tb, tb, gb_] + [hbm] * nc,
        out_specs=[tf, sf_, if_, tb, sb_, ib_] + [hbm] * nc,
        scratch_shapes=[pltpu.VMEM((len(SCAN_CHAINS), 128, DNW), F32)] + (_xfer_sems(nc, 7 * nc) if nc else []),
        compiler_params=_cp(vmem=VMEM_BIG),
    )(*_scan_masks(), q, k, v, gates, q, k, v, gates, *carry)


def _scan_bwd(q, k, v, gates, st_f, st_b, inv_f, inv_b, do, carry=()):
    nc = len(carry)

    def body(*refs):
        (cm_ref, c64_ref, bm_ref, qf, kf, vf, gf, sf, tf_, dof, qb, kb, vb, gb, sb, tb_, dob) = refs[:17]
        c_in = refs[17:17 + nc]
        dqf, dkf, dvf, dgf, dqb, dkb, dvb, dgb = refs[17 + nc:25 + nc]
        c_out = refs[25 + nc:25 + 2 * nc]
        ds_ref = refs[25 + 2 * nc]
        sems = refs[26 + 2 * nc:]
        first = pl.program_id(0) == 0
        last = pl.program_id(0) == NCH - 1
        if nc:
            @pl.when(first)
            def _():
                _xfer_start(_scatter_plan(c_in, c_out, _my_pos()), sems)

        @pl.when(first)
        def _():
            ds_ref[...] = jnp.zeros_like(ds_ref)

        bm = bm_ref[...]
        dirs = ((qf, kf, vf, gf, sf, dof, dqf, dkf, dvf, dgf), (qb, kb, vb, gb, sb, dob, dqb, dkb, dvb, dgb))
        prim = tuple((_stack(dirs[d][0], b), _stack(dirs[d][1], b), _stack(dirs[d][2], b), dirs[d][3][b], dirs[d][4][b])
                     for b, d in SCAN_CHAINS)
        cots = tuple((_stack(dirs[d][5], b), ds_ref[i]) for i, (b, d) in enumerate(SCAN_CHAINS))
        saved = tuple((tf_, tb_)[d][b] for b, d in SCAN_CHAINS)
        _, vjp = jax.vjp(lambda ps: tuple(r[:2] for r in _chunks([p + (SCAN_CHAINS[i][1],) for i, p in enumerate(ps)],
                                                                cm_ref, c64_ref, bm, saved)), prim)
        (grads,) = vjp(cots)
        for i, (dq, dk, dv, dg, ds) in enumerate(grads):
            b, d = SCAN_CHAINS[i]
            _unstack_to(dirs[d][6], b, dq)
            _unstack_to(dirs[d][7], b, dk)
            _unstack_to(dirs[d][8], b, dv)
            dirs[d][9][b] = dg
            ds_ref[i] = ds

        if nc:
            @pl.when(last)
            def _():
                _xfer_wait(_scatter_plan(c_in, c_out, _my_pos()), sems)

    (tf, gf_, sf_, if_), (tb, gb_, sb_, ib_) = _scan_specs(lambda s: NCH - 1 - s)
    o_sh = jax.ShapeDtypeStruct((NB, T, DNW), F32)
    g_sh = jax.ShapeDtypeStruct((NB, T, 128), F32)
    hbm = pl.BlockSpec(memory_space=pltpu.HBM)
    return pl.pallas_call(
        body, name="scan_bwd", grid=(NCH,),
        out_shape=[o_sh, o_sh, o_sh, g_sh] * 2 + [jax.ShapeDtypeStruct(g.shape, g.dtype) for g in carry],
        in_specs=_const_specs() + [tf, tf, tf, gf_, sf_, if_, tf, tb, tb, tb, gb_, sb_, ib_, tb] + [hbm] * nc,
        out_specs=[tf, tf, tf, gf_, tb, tb, tb, gb_] + [hbm] * nc,
        scratch_shapes=[pltpu.VMEM((len(SCAN_CHAINS), 128, DNW), F32)] + (_xfer_sems(nc, 7 * nc) if nc else []),
        compiler_params=_cp(vmem=VMEM_BIG),
    )(*_scan_masks(), q, k, v, gates, st_f, inv_f, do, q, k, v, gates, st_b, inv_b, do, *carry)


def _segmean(a, bd):
    hi = a.astype(BF16)
    lo = (a - hi.astype(F32)).astype(BF16)
    return jnp.dot(hi, bd, preferred_element_type=F32) + jnp.dot(lo, bd, preferred_element_type=F32)


def _na_prep_fwd(p, qw, kw, bd):
    def body(q_ref, k_ref, v_ref, qw_ref, kw_ref, bd_ref, oq_ref, ok_ref, ov_ref):
        bdm = bd_ref[...]
        q = q_ref[...]
        k = k_ref[...]
        oq_ref[...] = (q * lax.rsqrt(_segmean(q * q, bdm) + EPS) * qw_ref[...] * 0.125).astype(BF16)
        ok_ref[...] = (k * lax.rsqrt(_segmean(k * k, bdm) + EPS) * kw_ref[...]).astype(BF16)
        ov_ref[...] = v_ref[...].astype(BF16)

    ps = lambda o: pl.BlockSpec((None, RT, NAW), lambda b, j: (b, j, COL_NA // NAW + o))
    one = pl.BlockSpec((1, NAW), lambda b, j: (0, 0))
    os_ = pl.BlockSpec((None, RT, NAW), lambda b, j: (b, j, 0))
    return pl.pallas_call(
        body, name="na_prep_fwd", grid=(NB, T // RT), out_shape=[jax.ShapeDtypeStruct((NB, T, NAW), BF16)] * 3,
        in_specs=[ps(0), ps(1), ps(2), one, one, pl.BlockSpec((NAW, NAW), lambda b, j: (0, 0))], out_specs=[os_] * 3,
    )(p, p, p, qw, kw, bd)


def _na_prep_bwd(p, qw, kw, bd, dq, dk, dv):
    def body(q_ref, k_ref, qw_ref, kw_ref, bd_ref, dq_ref, dk_ref, dv_ref, o_ref, gq_ref, gk_ref):
        b, j = pl.program_id(0), pl.program_id(1)
        bdm = bd_ref[...]

        def one(x, w, dy, scale):
            r = lax.rsqrt(_segmean(x * x, bdm) + EPS)
            g = dy * w * scale
            dx = r * g - x * (r * r * r) * _segmean(g * x, bdm)
            dw = jnp.sum(dy * x * r * scale, axis=0, keepdims=True)
            return dx, dw

        dqv = jnp.where(j == 0, 0.0, dq_ref[...])
        dxq, dwq = one(q_ref[...], qw_ref[...], dqv, 0.125)
        dxk, dwk = one(k_ref[...], kw_ref[...], dk_ref[...], 1.0)
        o_ref[...] = jnp.concatenate([dxq, dxk, dv_ref[...]], axis=1)
        pad = jnp.zeros((7, NAW), F32)

        @pl.when((b == 0) & (j == 0))
        def _():
            gq_ref[...] = jnp.zeros_like(gq_ref)
            gk_ref[...] = jnp.zeros_like(gk_ref)

        gq_ref[...] += jnp.concatenate([dwq, pad], axis=0)
        gk_ref[...] += jnp.concatenate([dwk, pad], axis=0)

    ps = lambda o: pl.BlockSpec((None, RT, NAW), lambda b, j: (b, j, COL_NA // NAW + o))
    one = pl.BlockSpec((1, NAW), lambda b, j: (0, 0))
    ts = pl.BlockSpec((None, RT, NAW), lambda b, j: (b, j, 0))
    acc = pl.BlockSpec((8, NAW), lambda b, j: (0, 0))
    return pl.pallas_call(
        body, name="na_prep_bwd", grid=(NB, T // RT),
        out_shape=[jax.ShapeDtypeStruct((NB, T, 1536), F32), jax.ShapeDtypeStruct((8, NAW), F32),
                   jax.ShapeDtypeStruct((8, NAW), F32)],
        in_specs=[ps(0), ps(1), one, one, pl.BlockSpec((NAW, NAW), lambda b, j: (0, 0)),
                  pl.BlockSpec((None, RT, NAW), lambda b, j: (b, jnp.maximum(j - 1, 0), 0)), ts, ts],
        out_specs=[pl.BlockSpec((None, RT, 1536), lambda b, j: (b, j, 0)), acc, acc],
        compiler_params=_cp(vmem=VMEM_BIG),
    )(p, p, qw, kw, bd, dq, dk, dv)


def _na_row0(r):
    return jnp.clip(r - WIN_ROWS // 2, 0, ROWS - WIN_ROWS)


def _na_cls(r):
    return _na_row0(r) - r + WIN_ROWS - 1


NHS = 4
NLW = NHS * 64
NSR = NHS * GRID_W


def _na_stack(x):
    lane = lax.broadcasted_iota(jnp.int32, (1, NLW), 1)
    return jnp.concatenate([jnp.where((lane >> 6) == h, x, jnp.zeros_like(x)) for h in range(NHS)], axis=0)


def _na_unstack(y):
    lane = lax.broadcasted_iota(jnp.int32, (1, NLW), 1)
    out = jnp.zeros((GRID_W, NLW), y.dtype)
    for h in range(NHS):
        out = jnp.where((lane >> 6) == h, y[GRID_W * h:GRID_W * h + GRID_W], out)
    return out


def _na_fwd_gen(q, kw, vw, kc, vc, bias):
    q4 = _na_stack(q)
    s_w = lax.dot_general(q4, kw, _NT, preferred_element_type=F32) + bias
    yield
    s_c = lax.dot_general(q4, kc, _NT, preferred_element_type=F32)
    yield
    mx = jnp.maximum(jnp.max(s_w, axis=1, keepdims=True), jnp.max(s_c, axis=1, keepdims=True))
    p_w = jnp.exp(s_w - mx)
    p_c = jnp.exp(s_c - mx)
    inv = 1.0 / (jnp.sum(p_w, axis=1, keepdims=True) + jnp.sum(p_c, axis=1, keepdims=True))
    yield
    o_w = jnp.dot(p_w.astype(BF16), vw, preferred_element_type=F32)
    yield
    o4 = (o_w + jnp.dot(p_c.astype(BF16), vc, preferred_element_type=F32)) * inv
    return _na_unstack(o4)


NRF = 2


def _na_fwd(qn, kn, vb, bias, carry=()):
    nc = len(carry)
    grid = (NB, ROWS // NRF)
    ng = NAH // NHS

    def body(*refs):
        q_ref, k_ref, v_ref = refs[:3]
        b_refs, rest = refs[3:3 + NRF], refs[3 + NRF:]
        c_in, o_ref, c_out, sems = rest[:nc], rest[nc], rest[nc + 1:2 * nc + 1], rest[2 * nc + 1:]
        ids = [pl.program_id(a) for a in range(2)]
        if nc:
            @pl.when((ids[0] == 0) & (ids[1] == 0))
            def _():
                _xfer_start(_gather_plan(c_in, c_out, _my_pos()), sems)

        gens = []
        for i in range(NRF):
            start = pl.multiple_of(CT + _na_row0(NRF * ids[1] + i) * GRID_W, GRID_W)
            rw = slice(GRID_W * i, GRID_W * (i + 1))
            for g in range(ng):
                ln = slice(NLW * g, NLW * g + NLW)
                gens.append(_na_fwd_gen(q_ref[rw, ln], k_ref[pl.ds(start, NWIN), ln], v_ref[pl.ds(start, NWIN), ln],
                                        k_ref[0:CT, ln], v_ref[0:CT, ln], b_refs[i][g]))
        for n, o in enumerate(_lockstep(gens)):
            i, g = n // ng, n % ng
            o_ref[GRID_W * i:GRID_W * (i + 1), NLW * g:NLW * g + NLW] = o.astype(BF16)

        if nc:
            @pl.when((ids[0] == grid[0] - 1) & (ids[1] == grid[1] - 1))
            def _():
                _xfer_wait(_gather_plan(c_in, c_out, _my_pos()), sems)

    qs = pl.BlockSpec((None, NRF * GRID_W, NAW), lambda b, j: (b, CT // (NRF * GRID_W) + j, 0))
    kv = pl.BlockSpec((None, T, NAW), lambda b, j: (b, 0, 0))
    bss = [pl.BlockSpec((None, ng, NSR, NWIN), lambda b, j, i=i: (_na_cls(NRF * j + i), 0, 0, 0)) for i in range(NRF)]
    lat = pl.BlockSpec((None, NRF * GRID_W, NAW), lambda b, j: (b, j, 0))
    hbm = pl.BlockSpec(memory_space=pltpu.HBM)
    res = pl.pallas_call(
        body, name="na_fwd", grid=grid,
        out_shape=[jax.ShapeDtypeStruct((NB, SEQ, NAW), BF16)] + [jax.ShapeDtypeStruct((8,) + h.shape, h.dtype) for h in carry],
        in_specs=[qs, kv, kv] + bss + [hbm] * nc, out_specs=[lat] + [hbm] * nc,
        scratch_shapes=_xfer_sems(nc, 7 * nc) if nc else [],
        compiler_params=_cp(vmem=VMEM_BIG),
    )(qn, kn, vb, *([bias] * NRF), *carry)
    return res[0], list(res[1:])


def _na_bwd_gen(q, kw, vw, kc, vc, bias, do):
    q4 = _na_stack(q)
    s_w = lax.dot_general(q4, kw, _NT, preferred_element_type=F32) + bias
    yield
    s_c = lax.dot_general(q4, kc, _NT, preferred_element_type=F32)
    yield
    mx = jnp.maximum(jnp.max(s_w, axis=1, keepdims=True), jnp.max(s_c, axis=1, keepdims=True))
    p_w = jnp.exp(s_w - mx)
    p_c = jnp.exp(s_c - mx)
    inv = 1.0 / (jnp.sum(p_w, axis=1, keepdims=True) + jnp.sum(p_c, axis=1, keepdims=True))
    p_w = p_w * inv
    p_c = p_c * inv
    do4 = _na_stack(do.astype(BF16))
    dp_w = lax.dot_general(do4, vw, _NT, preferred_element_type=F32)
    yield
    dp_c = lax.dot_general(do4, vc, _NT, preferred_element_type=F32)
    yield
    delta = jnp.sum(p_w * dp_w, axis=1, keepdims=True) + jnp.sum(p_c * dp_c, axis=1, keepdims=True)
    ds_w = p_w * (dp_w - delta)
    ds_c = p_c * (dp_c - delta)
    ds_wb, ds_cb = ds_w.astype(BF16), ds_c.astype(BF16)
    dq4 = jnp.dot(ds_wb, kw, preferred_element_type=F32)
    yield
    dq4 = dq4 + jnp.dot(ds_cb, kc, preferred_element_type=F32)
    yield
    dkw = lax.dot_general(ds_wb, q4, _TN, preferred_element_type=F32)
    yield
    dvw = lax.dot_general(p_w.astype(BF16), do4, _TN, preferred_element_type=F32)
    yield
    dkc = lax.dot_general(ds_cb, q4, _TN, preferred_element_type=F32)
    yield
    dvc = lax.dot_general(p_c.astype(BF16), do4, _TN, preferred_element_type=F32)
    return ds_w, _na_unstack(dq4), dkw, dvw, dkc, dvc


NRS = 4


def _na_bwd(qn, kn, vb, bias, do):
    def body(q_ref, k_ref, v_ref, *rest):
        b_refs, (do_ref, dq_ref, dk_ref, dv_ref, db_ref) = rest[:NRS], rest[NRS:]
        b, j = pl.program_id(1), pl.program_id(2)

        @pl.when((b == 0) & (j == 0))
        def _():
            db_ref[...] = jnp.zeros_like(db_ref)

        @pl.when(j == 0)
        def _():
            dk_ref[...] = jnp.zeros_like(dk_ref)
            dv_ref[...] = jnp.zeros_like(dv_ref)

        kc, vc = k_ref[0:CT, :], v_ref[0:CT, :]
        rows = [NRS * j + i for i in range(NRS)]
        starts = [pl.multiple_of(CT + _na_row0(r) * GRID_W, GRID_W) for r in rows]
        gens = [_na_bwd_gen(q_ref[GRID_W * i:GRID_W * (i + 1), :], k_ref[pl.ds(starts[i], NWIN), :],
                            v_ref[pl.ds(starts[i], NWIN), :], kc, vc, b_refs[i][...],
                            do_ref[GRID_W * i:GRID_W * (i + 1), :]) for i in range(NRS)]
        for i, (ds_w, dq, dkw, dvw, dkc, dvc) in enumerate(_lockstep(gens)):
            db_ref[_na_cls(rows[i])] += ds_w
            dq_ref[GRID_W * i:GRID_W * (i + 1), :] = dq
            dk_ref[pl.ds(starts[i], NWIN), :] += dkw
            dv_ref[pl.ds(starts[i], NWIN), :] += dvw
            dk_ref[0:CT, :] += dkc
            dv_ref[0:CT, :] += dvc

    qs = pl.BlockSpec((None, NRS * GRID_W, NLW), lambda g, b, j: (b, CT // (NRS * GRID_W) + j, g))
    kv = pl.BlockSpec((None, T, NLW), lambda g, b, j: (b, 0, g))
    bss = [pl.BlockSpec((None, None, NSR, NWIN), lambda g, b, j, i=i: (_na_cls(NRS * j + i), g, 0, 0)) for i in range(NRS)]
    lat = pl.BlockSpec((None, NRS * GRID_W, NLW), lambda g, b, j: (b, j, g))
    return pl.pallas_call(
        body, name="na_bwd", grid=(NAH // NHS, NB, ROWS // NRS),
        out_shape=[jax.ShapeDtypeStruct((NB, SEQ, NAW), F32)] + [jax.ShapeDtypeStruct((NB, T, NAW), F32)] * 2
        + [jax.ShapeDtypeStruct((WIN_ROWS, NAH // NHS, NSR, NWIN), F32)],
        in_specs=[qs, kv, kv] + bss + [lat],
        out_specs=[lat, kv, kv, pl.BlockSpec((WIN_ROWS, None, NSR, NWIN), lambda g, b, j: (0, g, 0, 0))],
        compiler_params=_cp(vmem=VMEM_BIG),
    )(qn, kn, vb, *([bias] * NRS), do)


def _na_tables():
    cols = np.arange(GRID_W)
    win_start = np.clip(cols - WIN_COLS // 2, 0, GRID_W - WIN_COLS)
    kc = cols[None, :]
    valid = (kc >= win_start[:, None]) & (kc < win_start[:, None] + WIN_COLS)
    rel = np.clip(kc - cols[:, None] + WIN_COLS - 1, 0, 2 * WIN_COLS - 2)
    return valid, rel


def _bias_table(rpb):
    valid, rel = _na_tables()
    nrel = 2 * WIN_COLS - 1
    shifts = np.stack([(rel == r) & valid for r in range(nrel)]).astype(np.float32)
    neg = np.where(valid, 0.0, NEG).astype(np.float32)

    def body(rpb_ref, sh_ref, neg_ref, o_ref):
        h = pl.program_id(0)
        for dr in range(2 * WIN_ROWS - 1):
            tile = neg_ref[...]
            for r in range(nrel):
                tile = tile + rpb_ref[h, dr, r] * sh_ref[r]
            for cls in range(WIN_ROWS):
                i = dr - cls
                if 0 <= i < WIN_ROWS:
                    o_ref[cls, :, i * GRID_W:(i + 1) * GRID_W] = tile

    return pl.pallas_call(
        body, name="bias_table", grid=(NAH,),
        out_shape=jax.ShapeDtypeStruct((WIN_ROWS, NAH // NHS, NSR, NWIN), F32),
        in_specs=[pl.BlockSpec(memory_space=pltpu.SMEM), pl.BlockSpec((nrel, GRID_W, GRID_W), lambda h: (0, 0, 0)),
                  pl.BlockSpec((GRID_W, GRID_W), lambda h: (0, 0))],
        out_specs=pl.BlockSpec((WIN_ROWS, None, GRID_W, NWIN), lambda h: (0, h // NHS, h % NHS, 0)),
    )(rpb, jnp.asarray(shifts), jnp.asarray(neg))


def _rpb_fold(db):
    def body1(d_ref, o_ref):
        for dr in range(2 * WIN_ROWS - 1):
            acc = None
            for c in range(WIN_ROWS):
                i = dr - c
                if 0 <= i < WIN_ROWS:
                    blk = d_ref[c, :, i * GRID_W:(i + 1) * GRID_W]
                    acc = blk if acc is None else acc + blk
            o_ref[dr] = acc

    t15 = pl.pallas_call(
        body1, name="rpb_fold_rows", grid=(NAH,),
        out_shape=jax.ShapeDtypeStruct((NAH, 2 * WIN_ROWS - 1, GRID_W, GRID_W), F32),
        in_specs=[pl.BlockSpec((WIN_ROWS, None, GRID_W, NWIN), lambda h: (0, h, 0, 0))],
        out_specs=pl.BlockSpec((None, 2 * WIN_ROWS - 1, GRID_W, GRID_W), lambda h: (h, 0, 0, 0)),
        compiler_params=_cp(vmem=VMEM_BIG),
    )(db.reshape(WIN_ROWS, NAH, GRID_W, NWIN))
    valid, rel = _na_tables()
    onehot = np.zeros((GRID_W * GRID_W, 128), np.float32)
    flat_rel, flat_valid = rel.reshape(-1), valid.reshape(-1)
    onehot[np.arange(GRID_W * GRID_W)[flat_valid], flat_rel[flat_valid]] = 1.0

    def body2(a_ref, oh_ref, o_ref):
        o_ref[...] = _hmm(a_ref[...], oh_ref[...])

    vm = pl.BlockSpec(memory_space=pltpu.VMEM)
    out = pl.pallas_call(
        body2, name="rpb_fold_cols", out_shape=jax.ShapeDtypeStruct((NAH * 15, 128), F32),
        in_specs=[vm, vm], out_specs=vm, compiler_params=_cp(vmem=VMEM_BIG),
    )(t15.reshape(NAH * 15, GRID_W * GRID_W), jnp.asarray(onehot))
    return out[:, :31].reshape(NAH, 15, 31)


def _dn_out(o, z, w):
    outs = []
    for h in range(DNH):
        sl = slice(128 * h, 128 * h + 128)
        oh = o[:, sl]
        r = lax.rsqrt(jnp.mean(oh * oh, axis=-1, keepdims=True) + EPS)
        outs.append(oh * r * w[:, sl] * _silu(z[:, sl]))
    return jnp.concatenate(outs, axis=1)


def _outproj_fwd(of, ob, p, na, onw, x, g1, w_out):
    def body(of_ref, ob_ref, z_ref, na_ref, w_ref, x_ref, g_ref, wo_ref, x1_ref, ai_ref, ao_ref):
        dn = _dn_out(of_ref[...] + ob_ref[...], z_ref[...], w_ref[...])
        ai = jnp.concatenate([dn.astype(BF16), na_ref[...]], axis=1)
        ai_ref[...] = ai
        ao = jnp.dot(ai, wo_ref[...], preferred_element_type=F32)
        ao_ref[...] = ao
        x1_ref[...] = x_ref[...] + g_ref[...] * ao

    lat = lambda w: pl.BlockSpec((None, RT, w), lambda b, j: (b, j, 0))
    return pl.pallas_call(
        body, name="outproj_fwd", grid=(NB, SEQ // RT),
        out_shape=[jax.ShapeDtypeStruct((NB, SEQ, D), F32), jax.ShapeDtypeStruct((NB, SEQ, D), BF16),
                   jax.ShapeDtypeStruct((NB, SEQ, D), F32)],
        in_specs=[pl.BlockSpec((None, RT, DNW), lambda b, j: (b, j + 1, 0)),
                  pl.BlockSpec((None, RT, DNW), lambda b, j: (b, j + 1, 0)),
                  pl.BlockSpec((None, RT, DNW), lambda b, j: (b, j + 1, COL_Z // DNW)),
                  lat(NAW), pl.BlockSpec((1, DNW), lambda b, j: (0, 0)), lat(D),
                  pl.BlockSpec((None, 1, D), lambda b, j: (b, 0, 0)), pl.BlockSpec((D, D), lambda b, j: (0, 0))],
        out_specs=[lat(D), lat(D), lat(D)],
        compiler_params=_cp(vmem=VMEM_BIG),
    )(of, ob, p, na, onw, x, g1, w_out)


def _outproj_bwd(dx1, ao, g1, w_out, of, ob, p, onw):
    def body(dx_ref, ao_ref, g_ref, wo_ref, of_ref, ob_ref, z_ref, w_ref, dao_ref, do_ref, dz_ref, dna_ref, dg_ref, dw_ref):
        b, j = pl.program_id(0), pl.program_id(1)

        @pl.when((b == 0) & (j == 0))
        def _():
            dw_ref[...] = jnp.zeros_like(dw_ref)

        @pl.when(j == 0)
        def _():
            do_ref[...] = jnp.zeros_like(do_ref)
            dz_ref[...] = jnp.zeros_like(dz_ref)

        @pl.when(j == 1)
        def _():
            dg_ref[...] = jnp.zeros_like(dg_ref)

        @pl.when(j >= 1)
        def _():
            dx = dx_ref[...]
            dg_ref[...] += jnp.sum(dx * ao_ref[...], axis=0, keepdims=True)
            dao = (g_ref[...] * dx).astype(BF16)
            dao_ref[...] = dao
            dai = lax.dot_general(dao, wo_ref[...], _NT, preferred_element_type=F32)
            dna_ref[...] = dai[:, DNW:]
            _, vjp = jax.vjp(_dn_out, of_ref[...] + ob_ref[...], z_ref[...], w_ref[...])
            do, dz, dw = vjp(dai[:, :DNW])
            do_ref[...] = do
            dz_ref[...] = dz
            dw_ref[...] += jnp.concatenate([dw, jnp.zeros((7, DNW), F32)], axis=0)

    lat = lambda w: pl.BlockSpec((None, RT, w), lambda b, j: (b, jnp.maximum(j - 1, 0), 0))
    tok = lambda w: pl.BlockSpec((None, RT, w), lambda b, j: (b, j, 0))
    return pl.pallas_call(
        body, name="outproj_bwd", grid=(NB, T // RT),
        out_shape=[jax.ShapeDtypeStruct((NB, SEQ, D), BF16), jax.ShapeDtypeStruct((NB, T, DNW), F32),
                   jax.ShapeDtypeStruct((NB, T, DNW), F32), jax.ShapeDtypeStruct((NB, SEQ, NAW), F32),
                   jax.ShapeDtypeStruct((NB, 1, D), F32), jax.ShapeDtypeStruct((8, DNW), F32)],
        in_specs=[_rows2d_spec(D), lat(D), pl.BlockSpec((None, 1, D), lambda b, j: (b, 0, 0)),
                  pl.BlockSpec((D, D), lambda b, j: (0, 0)),
                  tok(DNW), tok(DNW),
                  pl.BlockSpec((None, RT, DNW), lambda b, j: (b, j, COL_Z // DNW)),
                  pl.BlockSpec((1, DNW), lambda b, j: (0, 0))],
        out_specs=[lat(D), tok(DNW), tok(DNW), lat(NAW), pl.BlockSpec((None, 1, D), lambda b, j: (b, 0, 0)),
                   pl.BlockSpec((8, DNW), lambda b, j: (0, 0))],
        compiler_params=_cp(vmem=VMEM_BIG),
    )(dx1, ao, g1, w_out, of, ob, p, onw)


FT_FWD, FT_BWD = 256, 256
NFF = DFF // FFT


def _ffn_fwd(x1, sh2, sc2, g2, n2w, w_in4, w_out, target):
    ft, tpe = FT_FWD, SEQ // FT_FWD
    ni = NTOK // ft

    def body(x_ref, sh_ref, sc_ref, g_ref, w2_ref, wg_ref, wu_ref, wo_ref, t_ref,
             h2_ref, gt_ref, up_ref, dy_ref, dg_ref, loss_ref, acc_ref):
        f, i = pl.program_id(0), pl.program_id(1)
        rows = pl.ds(pl.multiple_of(i * ft, ft), ft)
        h2 = _norm_mod(x_ref[...], w2_ref[...], sh_ref[...], sc_ref[...]).astype(BF16)
        h2_ref[...] = h2
        gate = jnp.dot(h2, wg_ref[...], preferred_element_type=F32)
        up = jnp.dot(h2, wu_ref[...], preferred_element_type=F32)
        gt_ref[...] = gate.astype(BF16)
        up_ref[...] = up.astype(BF16)
        part = jnp.dot((_silu(gate) * up).astype(BF16), wo_ref[...], preferred_element_type=F32)

        @pl.when(f == 0)
        def _():
            acc_ref[rows, :] = part

        @pl.when(f > 0)
        def _():
            acc_ref[rows, :] += part

        @pl.when((i == 0) & (f == 0))
        def _():
            loss_ref[...] = jnp.zeros_like(loss_ref)

        @pl.when(f == NFF - 1)
        def _():
            ffn = acc_ref[rows, :]
            err = x_ref[...] + g_ref[...] * ffn - t_ref[...]
            dy = err * (1.0 / D)
            dy_ref[...] = dy
            dgp = jnp.sum(dy * ffn, axis=0, keepdims=True)
            loss_ref[...] += jnp.sum(jnp.sum(err * err, axis=1, keepdims=True), axis=0, keepdims=True) * (0.5 / D)

            @pl.when(i % tpe == 0)
            def _():
                dg_ref[...] = dgp

            @pl.when(i % tpe != 0)
            def _():
                dg_ref[...] += dgp

    row = lambda w: pl.BlockSpec((ft, w), lambda f, i: (i, 0))
    ex = pl.BlockSpec((None, 1, D), lambda f, i: (i // tpe, 0, 0))
    one = pl.BlockSpec((1, D), lambda f, i: (0, 0))
    wg = pl.BlockSpec((None, D, FFT), lambda f, i: (f, 0, 0))
    wu = pl.BlockSpec((None, D, FFT), lambda f, i: (NFF + f, 0, 0))
    wo = pl.BlockSpec((FFT, D), lambda f, i: (f, 0))
    ff = pl.BlockSpec((ft, FFT), lambda f, i: (i, f))
    row_first = pl.BlockSpec((ft, D), lambda f, i: (jnp.where(f == 0, i, ni), 0))
    row_last = pl.BlockSpec((ft, D), lambda f, i: (jnp.where(f == NFF - 1, i, ni), 0))
    ex_last = pl.BlockSpec((None, 1, D), lambda f, i: (jnp.where(f == NFF - 1, i // tpe, NB), 0, 0))
    return pl.pallas_call(
        body, name="ffn_fwd", grid=(NFF, ni),
        out_shape=[jax.ShapeDtypeStruct((NTOK + ft, D), BF16), jax.ShapeDtypeStruct((NTOK, DFF), BF16),
                   jax.ShapeDtypeStruct((NTOK, DFF), BF16), jax.ShapeDtypeStruct((NTOK + ft, D), F32),
                   jax.ShapeDtypeStruct((NB + 1, 1, D), F32), jax.ShapeDtypeStruct((8, 128), F32)],
        in_specs=[row(D), ex, ex, ex, one, wg, wu, wo, row(D)],
        out_specs=[row_first, ff, ff, row_last, ex_last, pl.BlockSpec((8, 128), lambda f, i: (0, 0))],
        scratch_shapes=[pltpu.VMEM((NTOK, D), F32)],
        compiler_params=_cp(vmem=VMEM_BIG),
    )(x1, sh2, sc2, g2, n2w, w_in4, w_in4, w_out, target)


def _rows2d_spec(width):
    return pl.BlockSpec((RT, width), lambda b, j: (b * (SEQ // RT) + jnp.maximum(j - 1, 0), 0))


def _ffn_bwd(dy, gate, up, x1, sh2, sc2, g2, n2w, w_in4, w_out):
    ft, tpe = FT_BWD, SEQ // FT_BWD
    ni = NTOK // ft

    def body(dy_ref, gt_ref, up_ref, x_ref, sh_ref, sc_ref, g_ref, w2_ref, wg_ref, wu_ref, wo_ref,
             dff_ref, act_ref, dgu_ref, dx_ref, dsh_ref, dsc_ref, dw_ref, acc_ref):
        f, i = pl.program_id(0), pl.program_id(1)
        rows = pl.ds(pl.multiple_of(i * ft, ft), ft)
        dff = (g_ref[...] * dy_ref[...]).astype(BF16)
        dff_ref[...] = dff
        dact = lax.dot_general(dff, wo_ref[...], _NT, preferred_element_type=F32)
        gate, up = gt_ref[...].astype(F32), up_ref[...].astype(F32)
        sg = _sigmoid(gate)
        sl = gate * sg
        act_ref[...] = (sl * up).astype(BF16)
        dgate = (dact * up * (sg * (1.0 + gate * (1.0 - sg)))).astype(BF16)
        dup = (dact * sl).astype(BF16)
        dgu_ref[0] = dgate
        dgu_ref[1] = dup
        part = (lax.dot_general(dgate, wg_ref[...], _NT, preferred_element_type=F32)
                + lax.dot_general(dup, wu_ref[...], _NT, preferred_element_type=F32))

        @pl.when(f == 0)
        def _():
            acc_ref[rows, :] = part

        @pl.when(f > 0)
        def _():
            acc_ref[rows, :] += part

        @pl.when((i == 0) & (f == 0))
        def _():
            dw_ref[...] = jnp.zeros_like(dw_ref)

        @pl.when(f == NFF - 1)
        def _():
            _, vjp = jax.vjp(_norm_mod, x_ref[...], w2_ref[...], sh_ref[...], sc_ref[...])
            dx, dw, dsh, dsc = vjp(acc_ref[rows, :])
            dx_ref[...] = dy_ref[...] + dx
            dw_ref[...] += dw

            @pl.when(i % tpe == 0)
            def _():
                dsh_ref[...] = dsh
                dsc_ref[...] = dsc

            @pl.when(i % tpe != 0)
            def _():
                dsh_ref[...] += dsh
                dsc_ref[...] += dsc

    last = lambda f, real, spare: jnp.where(f == NFF - 1, real, spare)
    row = lambda w: pl.BlockSpec((ft, w), lambda f, i: (i, 0))
    ex = pl.BlockSpec((None, 1, D), lambda f, i: (i // tpe, 0, 0))
    one = pl.BlockSpec((1, D), lambda f, i: (0, 0))
    wg = pl.BlockSpec((None, D, FFT), lambda f, i: (f, 0, 0))
    wu = pl.BlockSpec((None, D, FFT), lambda f, i: (NFF + f, 0, 0))
    wo = pl.BlockSpec((FFT, D), lambda f, i: (f, 0))
    ff = pl.BlockSpec((ft, FFT), lambda f, i: (i, f))
    row_once = lambda w: pl.BlockSpec((ft, w), lambda f, i: (last(f, i, ni), 0))
    ex_once = pl.BlockSpec((None, 1, D), lambda f, i: (last(f, i // tpe, NB), 0, 0))
    return pl.pallas_call(
        body, name="ffn_bwd", grid=(NFF, ni),
        out_shape=[jax.ShapeDtypeStruct((NTOK + ft, D), BF16), jax.ShapeDtypeStruct((NTOK, DFF), BF16),
                   jax.ShapeDtypeStruct((2, NTOK, DFF), BF16), jax.ShapeDtypeStruct((NTOK + ft, D), F32),
                   jax.ShapeDtypeStruct((NB + 1, 1, D), F32), jax.ShapeDtypeStruct((NB + 1, 1, D), F32),
                   jax.ShapeDtypeStruct((1, D), F32)],
        in_specs=[row(D), ff, ff, row(D), ex, ex, ex, one, wg, wu, wo],
        out_specs=[row_once(D), ff, pl.BlockSpec((2, ft, FFT), lambda f, i: (0, i, f)), row_once(D), ex_once, ex_once, one],
        scratch_shapes=[pltpu.VMEM((NTOK, D), F32)],
        compiler_params=_cp(vmem=VMEM_BIG),
    )(dy, gate, up, x1, sh2, sc2, g2, n2w, w_in4, w_in4, w_out)


def _local_step(x, ctx, target, mod_x, mod_c, norm1_w, norm2_w, w_in_p, conv_w, a_log, dt_bias, out_norm_w,
                q_norm_w, k_norm_w, rpb, w_out_b, w_ffn_in4, w_ffn_out_b, scatter_early=False, gather_late=None):
    sh1, sc1, g1, sh2, sc2, g2 = [mod_x[:, i * D:(i + 1) * D].reshape(NB, 1, D) for i in range(6)]
    csh1, csc1 = mod_c[0:D].reshape(1, 1, D), mod_c[D:2 * D].reshape(1, 1, D)
    cw = jnp.concatenate([conv_w, jnp.zeros((3, 1536), F32)], axis=0)
    prm = jnp.pad(jnp.stack([a_log.reshape(8), dt_bias.reshape(8)]), ((0, 6), (0, 120)))
    onw = jnp.tile(out_norm_w.reshape(1, 128), (1, DNH))
    qw = jnp.tile(q_norm_w.reshape(1, 64), (1, NAH))
    kw = jnp.tile(k_norm_w.reshape(1, 64), (1, NAH))
    seg = np.arange(NAW) // 64
    bd = jnp.asarray((seg[:, None] == seg[None, :]).astype(np.float32) / 64.0, dtype=BF16)
    bias = _bias_table(rpb)

    p, hm = _inproj_fwd(x, ctx, sh1, sc1, csh1, csc1, norm1_w, w_in_p)
    dq_, dk_, dv_ = _dn_prep_fwd(p, cw)
    gates = _gates_fwd(p, prm)
    res = _scan_fwd(dq_, dk_, dv_, gates, carry=gather_late[:2] if gather_late else ())
    o_f, st_f, inv_f, o_b, st_b, inv_b = res[:6]
    qn, kn, vb = _na_prep_fwd(p, qw, kw, bd)
    na, got = _na_fwd(qn, kn, vb, bias, carry=gather_late[2:] if gather_late else ())
    if gather_late:
        w_out_b = res[6].reshape(D, D)
        w_ffn_in4 = res[7].reshape(4, D, FFT)
        w_ffn_out_b = got[0].reshape(DFF, D)
    x1, ai, ao = _outproj_fwd(o_f, o_b, p, na, onw, x, g1, w_out_b)
    x1f = x1.reshape(NTOK, D)
    h2, gate, up, dy, dg2, loss8 = _ffn_fwd(x1f, sh2, sc2, g2, norm2_w, w_ffn_in4, w_ffn_out_b,
                                             target.reshape(NTOK, D))
    dg2 = dg2[:NB]
    dff, act, dgu, dx1, dsh2, dsc2, dn2w = _ffn_bwd(dy, gate, up, x1f, sh2, sc2, g2, norm2_w, w_ffn_in4, w_ffn_out_b)
    g_w_ffn_out = _matmul_tn(act, dff, FFT, D, 512, "wgrad_ffn_out", k_rows=NTOK)
    g_w_ffn_in4 = _matmul_tn(
        h2, dgu, D, FFT, 1024, "wgrad_ffn_in", nn=2 * NFF, k_rows=NTOK,
        b_block=(None, 1024, FFT), b_map=lambda i, j, k: (j // NFF, k, j % NFF),
        out_shape=(4, D, FFT), out_block=(None, D, FFT), out_map=lambda i, j, k: (j, i, 0))
    dsh2, dsc2 = dsh2[:NB], dsc2[:NB]
    dao, do, dz, dna, dg1, donw = _outproj_bwd(dx1, ao, g1, w_out_b, o_f, o_b, p, onw)
    g_w_out = _matmul_tn(ai.reshape(NTOK, D), dao.reshape(NTOK, D), D, D, 1024, "wgrad_out")
    dqn, dkn, dvn, dbias = _na_bwd(qn, kn, vb, bias, dna)
    dp_na, dqw, dkw = _na_prep_bwd(p, qw, kw, bd, dqn, dkn, dvn)
    early = [g_w_out.reshape(8, D // 8, D), g_w_ffn_in4.reshape(8, D // 2, FFT), g_w_ffn_out.reshape(8, DFF // 8, D)]
    res = _scan_bwd(dq_, dk_, dv_, gates, st_f, st_b, inv_f, inv_b, do, carry=early if scatter_early else ())
    dqf, dkf, dvf, dgf, dqb, dkb, dvb, dgb = res[:8]
    if scatter_early:
        g_w_out, g_w_ffn_in4, g_w_ffn_out = res[8:]
    dpq, dpk, dpv, gcq, gck, gcv = _dn_prep_bwd(p, cw, (dqf, dqb), (dkf, dkb), (dvf, dvb))
    dgt, dprm = _gates_bwd(p, prm, dgf, dgb)
    dpb, grad_x, dsh1, dsc1, dcsh1, dcsc1, dn1w = _inproj_bwd(
        (dpq, dpk, dpv), dz, dp_na, dgt, w_in_p, x, ctx, sh1, sc1, csh1, csc1, norm1_w, dx1)
    g_w_in_p = _matmul_tn(dpb.reshape(NB * T, PC), hm.reshape(NB * T, D), PC, D, 512, "wgrad_in")
    g_rpb = _rpb_fold(dbias)

    dmod_x = jnp.concatenate([dsh1, dsc1, dg1, dsh2, dsc2, dg2], axis=2).reshape(NB, 6 * D)
    dmod_c = jnp.concatenate([dcsh1.reshape(D), dcsc1.reshape(D), jnp.zeros((4 * D,), F32)])
    g_conv = jnp.concatenate([gcq[:5], gck[:5], gcv[:5]], axis=1)
    small = dict(
        norm1_w=dn1w, norm2_w=dn2w, a_log=dprm[0, :8], dt_bias=dprm[1, :8],
        out_norm_w=donw[0].reshape(DNH, 128).sum(0), q_norm_w=dqw[0].reshape(NAH, 64).sum(0),
        k_norm_w=dkw[0].reshape(NAH, 64).sum(0), rpb=g_rpb, conv_w=g_conv)
    return loss8[0, 0], grad_x, dmod_x, dmod_c, small, g_w_in_p, g_w_out, g_w_ffn_in4, g_w_ffn_out


def _perm_w_in(wt):
    return jnp.concatenate([wt[:2048], wt[2064:3600], wt[2048:2064], jnp.zeros((PC - IN_COLS, wt.shape[1]), wt.dtype)],
                           axis=0)


def _unperm_w_in(gt):
    return jnp.concatenate([gt[:2048], gt[COL_G:COL_G + 16], gt[2048:COL_G]], axis=0)


def _misc_row(a_log, dt_bias, onw, qnw, knw):
    return jnp.concatenate([a_log.reshape(8), dt_bias.reshape(8), jnp.zeros((112,), F32), onw.reshape(128),
                            qnw.reshape(64), knw.reshape(64), jnp.zeros((D - 384,), F32)])


def _pack_small(c_ctx, n1, n2, b_ada, a_log, dt_bias, onw, qnw, knw, rpb, conv_s):
    misc = _misc_row(a_log, dt_bias, onw, qnw, knw)
    rp = jnp.concatenate([rpb.reshape(-1), jnp.zeros((4 * D - 3720,), F32)])
    cv = jnp.concatenate([conv_s.reshape(-1), jnp.zeros((2 * D - 1920,), F32)])
    return jnp.concatenate([c_ctx.reshape(-1), n1.reshape(-1), n2.reshape(-1), b_ada.reshape(-1), misc, rp, cv]).reshape(16, D)


def _unpack_small(a):
    f = a.reshape(-1)
    misc = f[9 * D:10 * D]
    return dict(
        c_ctx=f[0:D], norm1_w=f[D:2 * D].reshape(1, D), norm2_w=f[2 * D:3 * D].reshape(1, D),
        b_ada=f[3 * D:9 * D].reshape(1, 6 * D), dn_A_log=misc[0:8].reshape(1, 2, 4), dn_dt_bias=misc[8:16].reshape(1, 2, 4),
        dn_out_norm_w=misc[128:256].reshape(1, 128), na_q_norm_w=misc[256:320].reshape(1, 64),
        na_k_norm_w=misc[320:384].reshape(1, 64), na_rpb=f[10 * D:10 * D + 3720].reshape(1, 8, 15, 31),
        dn_conv_w=f[14 * D:14 * D + 1920].reshape(1, 5, 384))


_WEIGHTS = ['c_ctx', 'norm1_w', 'norm2_w', 'w_ada', 'b_ada', 'w_in', 'dn_conv_w', 'dn_A_log', 'dn_dt_bias',
            'dn_out_norm_w', 'na_q_norm_w', 'na_k_norm_w', 'na_rpb', 'w_out', 'w_ffn_in', 'w_ffn_out']


def kernel(x, c, ctx, c_ctx, norm1_w, norm2_w, w_ada, b_ada, w_in, dn_conv_w, dn_A_log, dn_dt_bias, dn_out_norm_w, na_q_norm_w, na_k_norm_w, na_rpb, w_out, w_ffn_in, w_ffn_out, loss_target, m_c_ctx, m_norm1_w, m_norm2_w, m_w_ada, m_b_ada, m_w_in, m_dn_conv_w, m_dn_A_log, m_dn_dt_bias, m_dn_out_norm_w, m_na_q_norm_w, m_na_k_norm_w, m_na_rpb, m_w_out, m_w_ffn_in, m_w_ffn_out, v_c_ctx, v_norm1_w, v_norm2_w, v_w_ada, v_b_ada, v_w_in, v_dn_conv_w, v_dn_A_log, v_dn_dt_bias, v_dn_out_norm_w, v_na_q_norm_w, v_na_k_norm_w, v_na_rpb, v_w_out, v_w_ffn_in, v_w_ffn_out):
    ix, iy, ic = lax.axis_index("x"), lax.axis_index("y"), lax.axis_index("c")
    chip = 2 * ix + iy
    me = 4 * ix + 2 * iy + ic

    crow = jnp.concatenate([c, c_ctx.reshape(1, D), jnp.zeros((5, D), F32)], axis=0)
    cvrow = jnp.concatenate([dn_conv_w[0].reshape(-1), jnp.zeros((2 * D - 1920,), F32)]).reshape(2, D)
    g0 = _allgather8(jnp.concatenate([crow, cvrow, jnp.zeros((6, D), F32)], axis=0), "gather_cond")
    cin = g0[:, :8].reshape(64, D)
    conv_full = jnp.concatenate([g0[2 * s, 8:10].reshape(-1)[:1920].reshape(5, 384) for s in range(4)], axis=1)

    w_ada_s = w_ada[0]
    b_s = lax.dynamic_slice(b_ada, (0, chip * 1536), (1, 1536))
    mod_s = _ada_fwd(cin, w_ada_s, b_s)
    mods = _gather_rows(mod_s, "gather_mod")
    mod_mine = jnp.concatenate([mods[0], mods[2], mods[4], mods[6]], axis=1)
    mod_x, mod_c = mod_mine[0:NB], mod_mine[2]

    def half(w2d):
        r = w2d.shape[0] // 2
        return lax.dynamic_slice(w2d, (ic * r, 0), (r, w2d.shape[1]))

    w_in_t = w_in[0].T
    (gw_in,) = _gather_weights([_cast_bf16(lax.dynamic_slice(w_in_t, (0, ic * (D // 2)), (900, D // 2)), "cast_w_in")])
    late = [_cast_bf16(half(w_out[0]), "cast_w_out"), _cast_bf16(half(w_ffn_in[0]), "cast_w_ffn_in"),
            _cast_bf16(half(w_ffn_out[0]), "cast_w_ffn_out")]
    w_in_p = _perm_w_in(gw_in.reshape(4, 2, 900, D // 2).transpose(0, 2, 1, 3).reshape(IN_COLS, D))

    (loss_part, grad_x, dmod_x, dmod_c, small, g_w_in_p, r_w_out, r_w_ffn_in, r_w_ffn_out) = _local_step(
        x, ctx, loss_target, mod_x, mod_c, norm1_w, norm2_w, w_in_p, conv_full, dn_A_log[0], dn_dt_bias[0],
        dn_out_norm_w[0], na_q_norm_w[0], na_k_norm_w[0], na_rpb[0], None, None, None,
        scatter_early=True, gather_late=late)
    loss = lax.psum(loss_part, ("x", "y", "c"))

    dm = jnp.concatenate([dmod_x, dmod_c.reshape(1, 6 * D), jnp.zeros((5, 6 * D), F32)], axis=0).reshape(48, D)
    misc = _misc_row(small["a_log"], small["dt_bias"], small["out_norm_w"], small["q_norm_w"], small["k_norm_w"])
    sm = jnp.concatenate([
        small["norm1_w"].reshape(-1), small["norm2_w"].reshape(-1), misc,
        small["rpb"].reshape(-1), jnp.zeros((4 * D - 3720,), F32),
        small["conv_w"].reshape(-1), jnp.zeros((8 * D - 7680,), F32), jnp.zeros((D,), F32)]).reshape(16, D)
    gsl, gtot = _allgather8(jnp.concatenate([dm, sm], axis=0), "gather_small", with_sum=True)
    gmod = gsl[:, :48].reshape(64, 6 * D)
    st = gtot[48:64].reshape(-1)
    g_n1, g_n2, g_misc = st[0:D].reshape(1, D), st[D:2 * D].reshape(1, D), st[2 * D:3 * D]
    g_rpb = st[3 * D:3 * D + 3720].reshape(1, 8, 15, 31)
    g_conv_full = st[7 * D:7 * D + 7680].reshape(5, 1536)
    g_conv_s = lax.dynamic_slice(g_conv_full, (0, chip * 384), (5, 384)).reshape(1, 5, 384)

    gmod_s = lax.dynamic_slice(gmod, (0, chip * 1536), (64, 1536))
    g_w_ada_s, pctx = _ada_bwd(cin, gmod_s, w_ada_s)
    parts = _allgather8(pctx, "gather_cctx")
    g_b_ada, g_c_ctx = _ada_small(gmod, parts, c_ctx.reshape(1, D))

    g_w_in8 = _unperm_w_in(g_w_in_p).reshape(4, 900, 2, D // 2).transpose(0, 2, 1, 3).reshape(8, 900, D // 2)
    names = ["w_in", "w_out", "w_ffn_in", "w_ffn_out"]
    rs = list(_scatter_grads([g_w_in8])) + [r_w_out, r_w_ffn_in, r_w_ffn_out]
    full = _sibling_exchange([_sum_slots(r, "sum_" + nm) for r, nm in zip(rs, names)])
    g_w_in_t = full[0].transpose(1, 0, 2).reshape(900, D)
    g_w_out_s = full[1].reshape(1, D // 4, D)
    g_w_ffn_in_s = full[2].reshape(1, D, FFT)
    g_w_ffn_out_s = full[3].reshape(1, DFF // 4, D)

    grads = dict(
        c_ctx=g_c_ctx.reshape(D), norm1_w=g_n1, norm2_w=g_n2, w_ada=g_w_ada_s.reshape(1, D, 1536), b_ada=g_b_ada,
        w_in=g_w_in_t, dn_conv_w=g_conv_s, dn_A_log=g_misc[0:8].reshape(1, 2, 4), dn_dt_bias=g_misc[8:16].reshape(1, 2, 4),
        dn_out_norm_w=g_misc[128:256].reshape(1, 128), na_q_norm_w=g_misc[256:320].reshape(1, 64),
        na_k_norm_w=g_misc[320:384].reshape(1, 64), na_rpb=g_rpb, w_out=g_w_out_s, w_ffn_in=g_w_ffn_in_s,
        w_ffn_out=g_w_ffn_out_s)
    pk = lambda d_: _pack_small(d_["c_ctx"], d_["norm1_w"], d_["norm2_w"], d_["b_ada"], d_["dn_A_log"], d_["dn_dt_bias"],
                                d_["dn_out_norm_w"], d_["na_q_norm_w"], d_["na_k_norm_w"], d_["na_rpb"], d_["dn_conv_w"])
    wts = dict(c_ctx=c_ctx, norm1_w=norm1_w, norm2_w=norm2_w, b_ada=b_ada, dn_A_log=dn_A_log, dn_dt_bias=dn_dt_bias,
               dn_out_norm_w=dn_out_norm_w, na_q_norm_w=na_q_norm_w, na_k_norm_w=na_k_norm_w, na_rpb=na_rpb, dn_conv_w=dn_conv_w)
    ms = dict(c_ctx=m_c_ctx, norm1_w=m_norm1_w, norm2_w=m_norm2_w, b_ada=m_b_ada, dn_A_log=m_dn_A_log, dn_dt_bias=m_dn_dt_bias,
              dn_out_norm_w=m_dn_out_norm_w, na_q_norm_w=m_na_q_norm_w, na_k_norm_w=m_na_k_norm_w, na_rpb=m_na_rpb,
              dn_conv_w=m_dn_conv_w)
    vs = dict(c_ctx=v_c_ctx, norm1_w=v_norm1_w, norm2_w=v_norm2_w, b_ada=v_b_ada, dn_A_log=v_dn_A_log, dn_dt_bias=v_dn_dt_bias,
              dn_out_norm_w=v_dn_out_norm_w, na_q_norm_w=v_na_q_norm_w, na_k_norm_w=v_na_k_norm_w, na_rpb=v_na_rpb,
              dn_conv_w=v_dn_conv_w)
    sd, snm, snv, _ = _adamw(pk(wts), pk(grads), pk(ms), pk(vs), "adamw_small")
    delta, new_m, new_v = _unpack_small(sd), _unpack_small(snm), _unpack_small(snv)
    big = dict(w_ada=(w_ada, m_w_ada, v_w_ada), w_in=(w_in, m_w_in, v_w_in), w_out=(w_out, m_w_out, v_w_out),
               w_ffn_in=(w_ffn_in, m_w_ffn_in, v_w_ffn_in), w_ffn_out=(w_ffn_out, m_w_ffn_out, v_w_ffn_out))
    for nm, (w_, m_, v_) in big.items():
        shp = w_.shape
        if nm == "w_in":
            r2, back = (lambda a: a[0].T), (lambda a: a.T.reshape(shp))
            res = _adamw(w_in_t, grads[nm], r2(m_), r2(v_), "adamw_" + nm)
        else:
            r2, back = (lambda a: a.reshape(shp[1], shp[2])), (lambda a: a.reshape(shp))
            res = _adamw(r2(w_), r2(grads[nm]), r2(m_), r2(v_), "adamw_" + nm)
        delta[nm], new_m[nm], new_v[nm], grads[nm] = [back(a) for a in res]

    return (loss, grad_x, *[grads[n] for n in _WEIGHTS], *[delta[n] for n in _WEIGHTS],
            *[new_m[n] for n in _WEIGHTS], *[new_v[n] for n in _WEIGHTS])
```

```python
import functools

import numpy as np
import jax
import jax.numpy as jnp
from jax import lax
from jax.experimental import pallas as pl
from jax.experimental.pallas import tpu as pltpu

F32 = jnp.float32
BF16 = jnp.bfloat16

D = 1024
NB = 2
SEQ = 2048
CT = 256
T = CT + SEQ
NTOK = NB * SEQ
CH = 64
NCH = T // CH
NCH_CTX = CT // CH
DNH, DNW = 4, 512
NAH, NAW = 8, 512
GRID_W = 64
ROWS = SEQ // GRID_W
WIN_ROWS = 8
WIN_COLS = 16
NWIN = WIN_ROWS * GRID_W
DFF = 2816
FFT = 1408
IN_COLS = 3600
PC = 3712
COL_Z, COL_NA, COL_G = 1536, 2048, 3584
EPS = 1e-6
NEG = -1e30
RT = 256
VMEM_BIG = 56 * 1024 * 1024
MESH = pl.DeviceIdType.MESH

ADAM_LR, ADAM_B1, ADAM_B2, ADAM_EPS, ADAM_WD, ADAM_STEP = 0.001, 0.9, 0.999, 1e-08, 0.01, 10


def _cp(vmem=None, sem=None):
    kw = {}
    if vmem is not None:
        kw["vmem_limit_bytes"] = vmem
    if sem is not None:
        kw["dimension_semantics"] = sem
    return pltpu.CompilerParams(**kw)


_NN = (((1,), (0,)), ((), ()))
_NT = (((1,), (1,)), ((), ()))
_TN = (((0,), (0,)), ((), ()))


def _dg(a, b, dims):
    return lax.dot_general(a.astype(BF16), b.astype(BF16), dims, preferred_element_type=F32)


@jax.custom_vjp
def _mm(a, b):
    return _dg(a, b, _NN)


_mm.defvjp(lambda a, b: (_dg(a, b, _NN), (a, b)),
           lambda r, g: (_dg(g, r[1], _NT), _dg(r[0], g, _TN)))


@jax.custom_vjp
def _mm_nt(a, b):
    return _dg(a, b, _NT)


_mm_nt.defvjp(lambda a, b: (_dg(a, b, _NT), (a, b)),
              lambda r, g: (_dg(g, r[1], _NN), _dg(g, r[0], _TN)))


@jax.custom_vjp
def _mm_tn(a, b):
    return _dg(a, b, _TN)


_mm_tn.defvjp(lambda a, b: (_dg(a, b, _TN), (a, b)),
              lambda r, g: (_dg(r[1], g, _NT), _dg(r[0], g, _NN)))


def _split_dot(a, b, dims):
    hi = b.astype(BF16)
    lo = (b - hi.astype(F32)).astype(BF16)
    ab = a.astype(BF16)
    return (lax.dot_general(ab, hi, dims, preferred_element_type=F32)
            + lax.dot_general(ab, lo, dims, preferred_element_type=F32))


@jax.custom_vjp
def _mask_mm(mask, b):
    return _split_dot(mask, b, _NN)


_mask_mm.defvjp(lambda mask, b: (_split_dot(mask, b, _NN), mask),
                lambda mask, g: (jnp.zeros_like(mask), _split_dot(mask, g, _TN)))


def _hmm(a, b):
    return jnp.dot(a, b, precision=lax.Precision.HIGHEST, preferred_element_type=F32)


def _sigmoid(x):
    return 1.0 / (1.0 + jnp.exp(-x))


def _silu(x):
    return x * _sigmoid(x)


def _norm_mod(x, w, sh, sc):
    r = lax.rsqrt(jnp.mean(x * x, axis=-1, keepdims=True) + EPS)
    return (x * r * w) * (1.0 + sc) + sh


def _my_pos():
    return lax.axis_index("x"), lax.axis_index("y"), lax.axis_index("c")


def _flip(pos, k):
    x, y, c = pos
    return (1 - x if k & 4 else x, 1 - y if k & 2 else y, 1 - c if k & 1 else c)


def _slot(pos):
    return 4 * pos[0] + 2 * pos[1] + pos[2]


def _allgather8(v, name, with_sum=False):
    R, W = v.shape

    def body(v_ref, out_ref, *rest):
        if with_sum:
            tot_ref, send_sems, recv_sems, lsem = rest
        else:
            send_sems, recv_sems, lsem = rest
        me = _my_pos()
        mine = pltpu.make_async_copy(v_ref, out_ref.at[_slot(me)], lsem)
        mine.start()
        sends = []
        for k in range(1, 8):
            cp = pltpu.make_async_remote_copy(
                src_ref=v_ref, dst_ref=out_ref.at[_slot(me)], send_sem=send_sems.at[k - 1],
                recv_sem=recv_sems.at[k - 1], device_id=_flip(me, k), device_id_type=MESH)
            cp.start()
            sends.append(cp)
        for k in range(1, 8):
            peer = _flip(me, k)
            pltpu.make_async_remote_copy(
                src_ref=v_ref, dst_ref=out_ref.at[_slot(peer)], send_sem=send_sems.at[k - 1],
                recv_sem=recv_sems.at[k - 1], device_id=peer, device_id_type=MESH).wait_recv()
        for cp in sends:
            cp.wait_send()
        mine.wait()
        if with_sum:
            acc = out_ref[0]
            for s in range(1, 8):
                acc = acc + out_ref[s]
            tot_ref[...] = acc

    out_shape = [jax.ShapeDtypeStruct((8, R, W), F32)]
    out_specs = [pl.BlockSpec(memory_space=pltpu.VMEM)]
    if with_sum:
        out_shape.append(jax.ShapeDtypeStruct((R, W), F32))
        out_specs.append(pl.BlockSpec(memory_space=pltpu.VMEM))
    res = pl.pallas_call(
        body, name=name, out_shape=out_shape,
        in_specs=[pl.BlockSpec(memory_space=pltpu.VMEM)], out_specs=out_specs,
        scratch_shapes=[pltpu.SemaphoreType.DMA((7,)), pltpu.SemaphoreType.DMA((7,)), pltpu.SemaphoreType.DMA],
        compiler_params=_cp(vmem=VMEM_BIG),
    )(v)
    return res if with_sum else res[0]


def _gather_rows(v, name):
    W = v.shape[1]

    def body(v_ref, out_ref, send_sems, recv_sems, lsem):
        pos = _my_pos()
        me = _slot(pos)
        rows = lambda p: v_ref.at[pl.ds(pl.multiple_of(8 * p, 8), 8)]
        locals_ = [(rows(me), out_ref.at[me])]
        xfers = [(rows(_slot(_flip(pos, k))), out_ref.at[me], out_ref.at[_slot(_flip(pos, k))], _flip(pos, k))
                 for k in range(1, 8)]
        _xfer_start((locals_, xfers), (send_sems, recv_sems, lsem))
        _xfer_wait((locals_, xfers), (send_sems, recv_sems, lsem))

    vm = pl.BlockSpec(memory_space=pltpu.VMEM)
    return pl.pallas_call(
        body, name=name, out_shape=jax.ShapeDtypeStruct((8, 8, W), F32), in_specs=[vm], out_specs=vm,
        scratch_shapes=_xfer_sems(1, 7),
    )(v)


def _gather_weights(halves):
    n = len(halves)

    def body(*refs):
        ins, outs = refs[:n], refs[n:2 * n]
        send_sems, recv_sems, lsems = refs[2 * n:]
        me = _my_pos()
        x, y, c = me
        sib = (x, y, 1 - c)
        chips = [4, 2, 6]

        def cp(a, k, block, to, src=None):
            dst = outs[a].at[_slot(block)]
            return pltpu.make_async_remote_copy(
                src_ref=dst if src is None else src, dst_ref=dst, send_sem=send_sems.at[a, k],
                recv_sem=recv_sems.at[a, k], device_id=to, device_id_type=MESH)

        mine, first, passed = [], [], []
        for a in range(n):
            m = pltpu.make_async_copy(ins[a], outs[a].at[_slot(me)], lsems.at[a])
            m.start()
            mine.append(m)
            first.append(cp(a, 0, me, sib, src=ins[a]))
            for j, k in enumerate(chips):
                first.append(cp(a, 1 + j, me, _flip(me, k), src=ins[a]))
        for f in first:
            f.start()
        for j, k in enumerate(chips):
            for a in range(n):
                cp(a, 1 + j, _flip(me, k), me).wait_recv()
                p = cp(a, 4 + j, _flip(me, k), sib)
                p.start()
                passed.append(p)
        for a in range(n):
            cp(a, 0, sib, me).wait_recv()
            for j, k in enumerate(chips):
                cp(a, 4 + j, _flip(sib, k), me).wait_recv()
        for f in first + passed:
            f.wait_send()
        for m in mine:
            m.wait()

    vm = pl.BlockSpec(memory_space=pltpu.VMEM)
    return pl.pallas_call(
        body, name="gather_weights",
        out_shape=[jax.ShapeDtypeStruct((8,) + h.shape, h.dtype) for h in halves],
        in_specs=[vm] * n, out_specs=[vm] * n,
        scratch_shapes=[pltpu.SemaphoreType.DMA((n, 7)), pltpu.SemaphoreType.DMA((n, 7)),
                        pltpu.SemaphoreType.DMA((n,))],
        compiler_params=_cp(vmem=VMEM_BIG),
    )(*halves)


def _xfer_start(plan, sems):
    send_sems, recv_sems, lsems = sems
    locals_, xfers = plan
    for i, (src, dst) in enumerate(locals_):
        pltpu.make_async_copy(src, dst, lsems.at[i]).start()
    for i, (src, dst, _, peer) in enumerate(xfers):
        pltpu.make_async_remote_copy(src_ref=src, dst_ref=dst, send_sem=send_sems.at[i], recv_sem=recv_sems.at[i],
                                     device_id=peer, device_id_type=MESH).start()


def _xfer_wait(plan, sems):
    send_sems, recv_sems, lsems = sems
    locals_, xfers = plan
    for i, (src, _, landing, peer) in enumerate(xfers):
        pltpu.make_async_remote_copy(src_ref=src, dst_ref=landing, send_sem=send_sems.at[i], recv_sem=recv_sems.at[i],
                                     device_id=peer, device_id_type=MESH).wait_recv()
    for i, (src, dst, _, peer) in enumerate(xfers):
        pltpu.make_async_remote_copy(src_ref=src, dst_ref=dst, send_sem=send_sems.at[i], recv_sem=recv_sems.at[i],
                                     device_id=peer, device_id_type=MESH).wait_send()
    for i, (src, dst) in enumerate(locals_):
        pltpu.make_async_copy(src, dst, lsems.at[i]).wait()


def _xfer_sems(n_local, n_xfer):
    return [pltpu.SemaphoreType.DMA((n_xfer,)), pltpu.SemaphoreType.DMA((n_xfer,)), pltpu.SemaphoreType.DMA((n_local,))]


def _scatter_plan(ins, outs, pos):
    me = _slot(pos)
    locals_, xfers = [], []
    for src, dst in zip(ins, outs):
        locals_.append((src.at[me], dst.at[me]))
        for k in range(1, 8):
            peer = _flip(pos, k)
            xfers.append((src.at[_slot(peer)], dst.at[me], dst.at[_slot(peer)], peer))
    return locals_, xfers


def _scatter_grads(grads):
    n = len(grads)

    def body(*refs):
        plan = _scatter_plan(refs[:n], refs[n:2 * n], _my_pos())
        _xfer_start(plan, refs[2 * n:])
        _xfer_wait(plan, refs[2 * n:])

    hbm = pl.BlockSpec(memory_space=pltpu.HBM)
    return pl.pallas_call(
        body, name="scatter_grads", out_shape=[jax.ShapeDtypeStruct(g.shape, g.dtype) for g in grads],
        in_specs=[hbm] * n, out_specs=[hbm] * n, scratch_shapes=_xfer_sems(n, 7 * n),
    )(*grads)


def _sibling_exchange(halves, name):
    n = len(halves)

    def body(*refs):
        ins, outs = refs[:n], refs[n:2 * n]
        send_sems, recv_sems, lsems = refs[2 * n:]
        x, y, c = _my_pos()
        sib = (x, y, 1 - c)
        mine, sends = [], []
        for a in range(n):
            m = pltpu.make_async_copy(ins[a], outs[a].at[c], lsems.at[a])
            m.start()
            mine.append(m)
            cp = pltpu.make_async_remote_copy(
                src_ref=ins[a], dst_ref=outs[a].at[c], send_sem=send_sems.at[a], recv_sem=recv_sems.at[a],
                device_id=sib, device_id_type=MESH)
            cp.start()
            sends.append(cp)
        for a in range(n):
            pltpu.make_async_remote_copy(
                src_ref=ins[a], dst_ref=outs[a].at[1 - c], send_sem=send_sems.at[a], recv_sem=recv_sems.at[a],
                device_id=sib, device_id_type=MESH).wait_recv()
        for cp in sends:
            cp.wait_send()
        for m in mine:
            m.wait()

    vm = pl.BlockSpec(memory_space=pltpu.VMEM)
    return pl.pallas_call(
        body, name=name,
        out_shape=[jax.ShapeDtypeStruct((2,) + h.shape, h.dtype) for h in halves],
        in_specs=[vm] * n, out_specs=[vm] * n,
        scratch_shapes=[pltpu.SemaphoreType.DMA((n,)), pltpu.SemaphoreType.DMA((n,)), pltpu.SemaphoreType.DMA((n,))],
        compiler_params=_cp(vmem=VMEM_BIG),
    )(*halves)


def _row_tile(r, cap=512):
    for t in (cap, 256, 128, 64, 32, 16, 8):
        if t <= cap and r % t == 0:
            return t
    return r


def _cast_bf16(a, name):
    R, W = a.shape
    tr = _row_tile(R, 256)

    def body(a_ref, o_ref):
        o_ref[...] = a_ref[...].astype(BF16)

    return pl.pallas_call(
        body, name=name, grid=(R // tr,), out_shape=jax.ShapeDtypeStruct((R, W), BF16),
        in_specs=[pl.BlockSpec((tr, W), lambda i: (i, 0))], out_specs=pl.BlockSpec((tr, W), lambda i: (i, 0)),
    )(a)


def _sum_slots(r, name):
    _, R, W = r.shape
    tr = _row_tile(R, 128)

    def body(r_ref, o_ref):
        acc = r_ref[0].astype(F32)
        for s_ in range(1, 8):
            acc = acc + r_ref[s_].astype(F32)
        o_ref[...] = acc.astype(BF16)

    return pl.pallas_call(
        body, name=name, grid=(R // tr,), out_shape=jax.ShapeDtypeStruct((R, W), BF16),
        in_specs=[pl.BlockSpec((8, tr, W), lambda i: (0, i, 0))], out_specs=pl.BlockSpec((tr, W), lambda i: (i, 0)),
        compiler_params=_cp(vmem=VMEM_BIG),
    )(r)


def _adamw(w, g, m, v, name):
    R, W = w.shape
    tr = _row_tile(R, 256)
    tc = 256 if (tr == R and R * W * 4 > (1 << 21) and W % 256 == 0) else W
    c1 = 1.0 / (1.0 - ADAM_B1 ** ADAM_STEP)
    c2 = 1.0 / (1.0 - ADAM_B2 ** ADAM_STEP)

    def body(w_ref, g_ref, m_ref, v_ref, d_ref, nm_ref, nv_ref, g32_ref):
        gg = g_ref[...].astype(F32)
        g32_ref[...] = gg
        nm = ADAM_B1 * m_ref[...] + (1.0 - ADAM_B1) * gg
        nv = ADAM_B2 * v_ref[...] + (1.0 - ADAM_B2) * (gg * gg)
        d_ref[...] = -ADAM_LR * ((nm * c1) / (jnp.sqrt(nv * c2) + ADAM_EPS) + ADAM_WD * w_ref[...])
        nm_ref[...] = nm
        nv_ref[...] = nv

    spec = pl.BlockSpec((tr, tc), lambda i, j: (i, j))
    return pl.pallas_call(
        body, name=name, grid=(R // tr, W // tc), out_shape=[jax.ShapeDtypeStruct((R, W), F32)] * 4,
        in_specs=[spec] * 4, out_specs=[spec] * 4, compiler_params=_cp(vmem=VMEM_BIG),
    )(w, g, m, v)


def _adamw_hosting(items, carry):
    n, nc, steps = len(items), len(carry), 8
    c1 = 1.0 / (1.0 - ADAM_B1 ** ADAM_STEP)
    c2 = 1.0 / (1.0 - ADAM_B2 ** ADAM_STEP)

    def body(*refs):
        ins, c_in = refs[:4 * n], refs[4 * n:4 * n + nc]
        outs, c_out = refs[4 * n + nc:8 * n + nc], refs[8 * n + nc:8 * n + 2 * nc]
        sems = refs[8 * n + 2 * nc:]
        if nc:
            @pl.when(pl.program_id(0) == 0)
            def _():
                _xfer_start(_scatter_plan(c_in, c_out, _my_pos()), sems)

        for a in range(n):
            w_ref, g_ref, m_ref, v_ref = ins[4 * a:4 * a + 4]
            d_ref, nm_ref, nv_ref, g32_ref = outs[4 * a:4 * a + 4]
            gg = g_ref[...].astype(F32)
            g32_ref[...] = gg
            nm = ADAM_B1 * m_ref[...] + (1.0 - ADAM_B1) * gg
            nv = ADAM_B2 * v_ref[...] + (1.0 - ADAM_B2) * (gg * gg)
            d_ref[...] = -ADAM_LR * ((nm * c1) / (jnp.sqrt(nv * c2) + ADAM_EPS) + ADAM_WD * w_ref[...])
            nm_ref[...] = nm
            nv_ref[...] = nv

        if nc:
            @pl.when(pl.program_id(0) == steps - 1)
            def _():
                _xfer_wait(_scatter_plan(c_in, c_out, _my_pos()), sems)

    specs = [pl.BlockSpec((w.shape[0] // steps, w.shape[1]), lambda i: (i, 0)) for w, _, _, _ in items]
    hbm = pl.BlockSpec(memory_space=pltpu.HBM)
    res = pl.pallas_call(
        body, name="adamw_hosting", grid=(steps,),
        out_shape=[jax.ShapeDtypeStruct(w.shape, F32) for w, _, _, _ in items for _ in range(4)]
        + [jax.ShapeDtypeStruct(g.shape, g.dtype) for g in carry],
        in_specs=[sp for sp in specs for _ in range(4)] + [hbm] * nc,
        out_specs=[sp for sp in specs for _ in range(4)] + [hbm] * nc,
        scratch_shapes=_xfer_sems(nc, 7 * nc) if nc else [],
        compiler_params=_cp(vmem=VMEM_BIG),
    )(*[a for it in items for a in it], *carry)
    return [res[4 * a:4 * a + 4] for a in range(n)], list(res[4 * n:])


def _matmul_tn(a, b, tm, tn, tk, name, nn=1, b_map=None, out_shape=None, out_block=None, out_map=None,
               b_block=None):
    K, M = a.shape
    nk = K // tk
    nm = M // tm

    def body(a_ref, b_ref, o_ref, acc_ref):
        kk = pl.program_id(2)

        @pl.when(kk == 0)
        def _():
            acc_ref[...] = jnp.zeros_like(acc_ref)

        acc_ref[...] += lax.dot_general(a_ref[...], b_ref[...], _TN, preferred_element_type=F32)

        @pl.when(kk == nk - 1)
        def _():
            o_ref[...] = acc_ref[...].astype(BF16)

    return pl.pallas_call(
        body, name=name, grid=(nm, nn, nk),
        out_shape=jax.ShapeDtypeStruct(out_shape or (M, b.shape[-1]), BF16),
        in_specs=[pl.BlockSpec((tk, tm), lambda i, j, k: (k, i)),
                  pl.BlockSpec(b_block or (tk, tn), b_map or (lambda i, j, k: (k, j)))],
        out_specs=pl.BlockSpec(out_block or (tm, tn), out_map or (lambda i, j, k: (i, j))),
        scratch_shapes=[pltpu.VMEM((tm, tn), F32)],
        compiler_params=_cp(vmem=VMEM_BIG),
    )(a, b)


def _ada_fwd(cin, w_ada_s, b_s):
    tn = 512

    def body(c_ref, w_ref, b_ref, o_ref):
        o_ref[...] = _dg(_silu(c_ref[...]), w_ref[...], _NN) + b_ref[...]

    return pl.pallas_call(
        body, name="ada_fwd", grid=(1536 // tn,), out_shape=jax.ShapeDtypeStruct((64, 1536), F32),
        in_specs=[pl.BlockSpec((64, D), lambda j: (0, 0)), pl.BlockSpec((D, tn), lambda j: (0, j)),
                  pl.BlockSpec((1, tn), lambda j: (0, j))],
        out_specs=pl.BlockSpec((64, tn), lambda j: (0, j)),
        compiler_params=_cp(vmem=VMEM_BIG),
    )(cin, w_ada_s, b_s)


def _ada_bwd(cin, gmod_s, w_ada_s):
    tn = 512

    def body(c_ref, g_ref, w_ref, gw_ref, pc_ref):
        j = pl.program_id(0)
        g = g_ref[...]
        gw_ref[...] = _dg(_silu(c_ref[...]), g, _TN)
        rr = lax.broadcasted_iota(jnp.int32, (8, 64), 0)
        cc = lax.broadcasted_iota(jnp.int32, (8, 64), 1)
        sel = jnp.where((rr == 0) & ((cc & 7) == 2), 1.0, 0.0).astype(F32)
        dctx = _hmm(sel, g)
        part = _dg(dctx, w_ref[...], _NT)

        @pl.when(j == 0)
        def _():
            pc_ref[...] = jnp.zeros_like(pc_ref)

        pc_ref[...] += part

    return pl.pallas_call(
        body, name="ada_bwd", grid=(1536 // tn,),
        out_shape=[jax.ShapeDtypeStruct((D, 1536), F32), jax.ShapeDtypeStruct((8, D), F32)],
        in_specs=[pl.BlockSpec((64, D), lambda j: (0, 0)), pl.BlockSpec((64, tn), lambda j: (0, j)),
                  pl.BlockSpec((D, tn), lambda j: (0, j))],
        out_specs=[pl.BlockSpec((D, tn), lambda j: (0, j)), pl.BlockSpec((8, D), lambda j: (0, 0))],
        compiler_params=_cp(vmem=VMEM_BIG),
    )(cin, gmod_s, w_ada_s)


def _ada_small(gmod, parts, c_ctx):
    def body(g_ref, p_ref, c_ref, gb_ref, gc_ref):
        gb_ref[...] = jnp.sum(g_ref[...], axis=0, keepdims=True)
        ds = p_ref[0][0:1] + p_ref[2][0:1] + p_ref[4][0:1] + p_ref[6][0:1]
        cc = c_ref[...]
        s = _sigmoid(cc)
        gc_ref[...] = ds * (s * (1.0 + cc * (1.0 - s)))

    vm = pl.BlockSpec(memory_space=pltpu.VMEM)
    return pl.pallas_call(
        body, name="ada_small",
        out_shape=[jax.ShapeDtypeStruct((1, 6 * D), F32), jax.ShapeDtypeStruct((1, D), F32)],
        in_specs=[vm, vm, vm], out_specs=[vm, vm],
    )(gmod, parts, c_ctx)


def _tok_specs():
    xs = pl.BlockSpec((None, RT, D), lambda b, j: (b, jnp.maximum(j - 1, 0), 0))
    cs = pl.BlockSpec((None, RT, D), lambda b, j: (b, 0, 0))
    ms = pl.BlockSpec((None, 1, D), lambda b, j: (b, 0, 0))
    cms = pl.BlockSpec((None, 1, D), lambda b, j: (0, 0, 0))
    return xs, cs, ms, cms


def _inproj_fwd(x, ctx, sh1, sc1, csh1, csc1, n1w, w_in_p):
    def body(x_ref, c_ref, sh_ref, sc_ref, csh_ref, csc_ref, w1_ref, w_ref, p_ref, hm_ref):
        is_ctx = pl.program_id(1) == 0
        xt = jnp.where(is_ctx, c_ref[...], x_ref[...])
        sh = jnp.where(is_ctx, csh_ref[...], sh_ref[...])
        sc = jnp.where(is_ctx, csc_ref[...], sc_ref[...])
        hb = _norm_mod(xt, w1_ref[...], sh, sc).astype(BF16)
        hm_ref[...] = hb
        p_ref[...] = lax.dot_general(hb, w_ref[...], _NT, preferred_element_type=F32)

    xs, cs, ms, cms = _tok_specs()
    return pl.pallas_call(
        body, name="inproj_fwd", grid=(NB, T // RT),
        out_shape=[jax.ShapeDtypeStruct((NB, T, PC), F32), jax.ShapeDtypeStruct((NB, T, D), BF16)],
        in_specs=[xs, cs, ms, ms, cms, cms, pl.BlockSpec((1, D), lambda b, j: (0, 0)),
                  pl.BlockSpec((PC, D), lambda b, j: (0, 0))],
        out_specs=[pl.BlockSpec((None, RT, PC), lambda b, j: (b, j, 0)),
                   pl.BlockSpec((None, RT, D), lambda b, j: (b, j, 0))],
        compiler_params=_cp(vmem=VMEM_BIG),
    )(x, ctx, sh1, sc1, csh1, csc1, n1w, w_in_p)


def _inproj_bwd(dp_dn, dz, dp_na, dgt, w_in_p, x, ctx, sh1, sc1, csh1, csc1, n1w, dx1):
    def body(d1q_ref, d1k_ref, d1v_ref, d2_ref, d3_ref, d4_ref, w_ref, x_ref, c_ref, sh_ref, sc_ref, csh_ref, csc_ref,
             w1_ref, dx1_ref, dpb_ref, gx_ref, dsh_ref, dsc_ref, dcsh_ref, dcsc_ref, dw1_ref):
        b, j = pl.program_id(0), pl.program_id(1)
        is_ctx = j == 0
        dp = jnp.concatenate([d1q_ref[...], d1k_ref[...], d1v_ref[...], d2_ref[...], d3_ref[...], d4_ref[...]],
                             axis=1).astype(BF16)
        dpb_ref[...] = dp
        dhm = jnp.dot(dp, w_ref[...], preferred_element_type=F32)
        xt = jnp.where(is_ctx, c_ref[...], x_ref[...])
        sh = jnp.where(is_ctx, csh_ref[...], sh_ref[...])
        sc = jnp.where(is_ctx, csc_ref[...], sc_ref[...])
        _, vjp = jax.vjp(_norm_mod, xt, w1_ref[...], sh, sc)
        dxt, dw1, dsh, dsc = vjp(dhm)

        @pl.when((b == 0) & (j == 0))
        def _():
            dw1_ref[...] = jnp.zeros_like(dw1_ref)
            dcsh_ref[...] = jnp.zeros_like(dcsh_ref)
            dcsc_ref[...] = jnp.zeros_like(dcsc_ref)

        dw1_ref[...] += dw1

        @pl.when(is_ctx)
        def _():
            dcsh_ref[...] += dsh
            dcsc_ref[...] += dsc

        @pl.when(j == 1)
        def _():
            dsh_ref[...] = jnp.zeros_like(dsh_ref)
            dsc_ref[...] = jnp.zeros_like(dsc_ref)

        @pl.when(j >= 1)
        def _():
            dsh_ref[...] += dsh
            dsc_ref[...] += dsc
            gx_ref[...] = dx1_ref[...] + dxt

    xs, cs, ms, cms = _tok_specs()
    tsp = lambda w: pl.BlockSpec((None, RT, w), lambda b, j: (b, j, 0))
    one = pl.BlockSpec((1, D), lambda b, j: (0, 0))
    return pl.pallas_call(
        body, name="inproj_bwd", grid=(NB, T // RT),
        out_shape=[jax.ShapeDtypeStruct((NB, T, PC), BF16), jax.ShapeDtypeStruct((NB, SEQ, D), F32),
                   jax.ShapeDtypeStruct((NB, 1, D), F32), jax.ShapeDtypeStruct((NB, 1, D), F32),
                   jax.ShapeDtypeStruct((1, 1, D), F32), jax.ShapeDtypeStruct((1, 1, D), F32),
                   jax.ShapeDtypeStruct((1, D), F32)],
        in_specs=[tsp(512), tsp(512), tsp(512), tsp(512), tsp(1536), tsp(128), pl.BlockSpec((PC, D), lambda b, j: (0, 0)),
                  xs, cs, ms, ms, cms, cms, one, _rows2d_spec(D)],
        out_specs=[tsp(PC), xs, ms, ms, cms, cms, one],
        compiler_params=_cp(vmem=VMEM_BIG),
    )(*dp_dn, dz, dp_na, dgt, w_in_p, x, ctx, sh1, sc1, csh1, csc1, n1w, dx1)


def _shift_masks():
    t = lax.broadcasted_iota(jnp.int32, (T, 1), 0)
    lo = jnp.where(t < CT, 0, CT)
    hi = jnp.where(t < CT, CT, T)
    return {off: (t + off >= lo) & (t + off < hi) for off in range(-2, 3)}


def _shifted(u, off, masks):
    s = u if off == 0 else pltpu.roll(u, (-off) % T, 0)
    return jnp.where(masks[off], s, 0.0)


def _conv_pre(u, w, masks):
    acc = _shifted(u, -2, masks) * w[0:1, :]
    for kk in range(1, 5):
        acc = acc + _shifted(u, kk - 2, masks) * w[kk:kk + 1, :]
    return acc


def _act_q(pre):
    a = _silu(pre)
    return a * lax.rsqrt(jnp.sum(a * a, axis=-1, keepdims=True) + EPS) * (128.0 ** -0.5)


def _act_k(pre):
    a = _silu(pre)
    return a * lax.rsqrt(jnp.sum(a * a, axis=-1, keepdims=True) + EPS)


def _dn_prep_fwd(p, cw):
    def body(pq_ref, pk_ref, pv_ref, wq_ref, wk_ref, wv_ref, q_ref, k_ref, v_ref):
        masks = _shift_masks()
        q_ref[...] = _act_q(_conv_pre(pq_ref[...], wq_ref[...], masks))
        k_ref[...] = _act_k(_conv_pre(pk_ref[...], wk_ref[...], masks))
        v_ref[...] = _silu(_conv_pre(pv_ref[...], wv_ref[...], masks))

    ps = lambda o: pl.BlockSpec((None, T, 128), lambda b, h: (b, 0, o + h))
    ws = lambda o: pl.BlockSpec((8, 128), lambda b, h: (0, o + h))
    os_ = pl.BlockSpec((None, T, 128), lambda b, h: (b, 0, h))
    return pl.pallas_call(
        body, name="dn_prep_fwd", grid=(NB, DNH), out_shape=[jax.ShapeDtypeStruct((NB, T, DNW), F32)] * 3,
        in_specs=[ps(0), ps(4), ps(8), ws(0), ws(4), ws(8)], out_specs=[os_] * 3,
        compiler_params=_cp(vmem=VMEM_BIG),
    )(p, p, p, cw, cw, cw)


def _dn_prep_bwd(p, cw, dqs, dks, dvs):
    def body(pq_ref, pk_ref, pv_ref, wq_ref, wk_ref, wv_ref, dqf_ref, dqb_ref, dkf_ref, dkb_ref, dvf_ref, dvb_ref,
             oq_ref, ok_ref, ov_ref, gq_ref, gk_ref, gv_ref):
        b = pl.program_id(1)
        masks = _shift_masks()

        def one(p_ref, w_ref, df_ref, db_ref, act, o_ref, g_ref):
            u = p_ref[...]
            w = w_ref[...]
            _, vjp = jax.vjp(act, _conv_pre(u, w, masks))
            (dpre,) = vjp(df_ref[...] + db_ref[...])
            du = _shifted(dpre, 2, masks) * w[0:1, :]
            rows = [jnp.sum(dpre * _shifted(u, -2, masks), axis=0, keepdims=True)]
            for kk in range(1, 5):
                du = du + _shifted(dpre, 2 - kk, masks) * w[kk:kk + 1, :]
                rows.append(jnp.sum(dpre * _shifted(u, kk - 2, masks), axis=0, keepdims=True))
            o_ref[...] = du
            gw = jnp.concatenate(rows + [jnp.zeros((3, 128), F32)], axis=0)

            @pl.when(b == 0)
            def _():
                g_ref[...] = gw

            @pl.when(b > 0)
            def _():
                g_ref[...] += gw

        one(pq_ref, wq_ref, dqf_ref, dqb_ref, _act_q, oq_ref, gq_ref)
        one(pk_ref, wk_ref, dkf_ref, dkb_ref, _act_k, ok_ref, gk_ref)
        one(pv_ref, wv_ref, dvf_ref, dvb_ref, _silu, ov_ref, gv_ref)

    ps = lambda o: pl.BlockSpec((None, T, 128), lambda h, b: (b, 0, o + h))
    ws = lambda o: pl.BlockSpec((8, 128), lambda h, b: (0, o + h))
    ds = pl.BlockSpec((None, T, 128), lambda h, b: (b, 0, h))
    outs = pl.pallas_call(
        body, name="dn_prep_bwd", grid=(DNH, NB),
        out_shape=[jax.ShapeDtypeStruct((NB, T, DNW), F32)] * 3 + [jax.ShapeDtypeStruct((8, DNW), F32)] * 3,
        in_specs=[ps(0), ps(4), ps(8), ws(0), ws(4), ws(8)] + [ds] * 6,
        out_specs=[ds] * 3
        + [pl.BlockSpec((8, 128), lambda h, b: (0, h))] * 3,
        compiler_params=_cp(vmem=VMEM_BIG),
    )(p, p, p, cw, cw, cw, *dqs, *dks, *dvs)
    return outs


def _gate_fn(a, prm):
    lane = lax.broadcasted_iota(jnp.int32, a.shape, 1)
    z = a + prm[1:2, :]
    sp = jnp.maximum(z, 0.0) + jnp.log(1.0 + jnp.exp(-jnp.abs(z)))
    g = -jnp.exp(prm[0:1, :]) * sp
    return jnp.where(lane < 8, g, jnp.where(lane < 16, _sigmoid(a), 0.0))


def _gates_fwd(p, prm):
    def body(a_ref, prm_ref, o_ref):
        o_ref[...] = _gate_fn(a_ref[...], prm_ref[...])

    return pl.pallas_call(
        body, name="gates_fwd", grid=(NB,), out_shape=jax.ShapeDtypeStruct((NB, T, 128), F32),
        in_specs=[pl.BlockSpec((None, T, 128), lambda b: (b, 0, COL_G // 128)), pl.BlockSpec((8, 128), lambda b: (0, 0))],
        out_specs=pl.BlockSpec((None, T, 128), lambda b: (b, 0, 0)),
    )(p, prm)


def _gates_bwd(p, prm, dgf, dgb):
    def body(a_ref, prm_ref, dgf_ref, dgb_ref, o_ref, dprm_ref):
        b = pl.program_id(0)
        tot = dgf_ref[...] + dgb_ref[...]
        _, vjp = jax.vjp(_gate_fn, a_ref[...], prm_ref[...])
        da, dprm = vjp(tot)
        o_ref[...] = da

        @pl.when(b == 0)
        def _():
            dprm_ref[...] = dprm

        @pl.when(b > 0)
        def _():
            dprm_ref[...] += dprm

    return pl.pallas_call(
        body, name="gates_bwd", grid=(NB,),
        out_shape=[jax.ShapeDtypeStruct((NB, T, 128), F32), jax.ShapeDtypeStruct((8, 128), F32)],
        in_specs=[pl.BlockSpec((None, T, 128), lambda b: (b, 0, COL_G // 128)), pl.BlockSpec((8, 128), lambda b: (0, 0)),
                  pl.BlockSpec((None, T, 128), lambda b: (b, 0, 0)), pl.BlockSpec((None, T, 128), lambda b: (b, 0, 0))],
        out_specs=[pl.BlockSpec((None, T, 128), lambda b: (b, 0, 0)), pl.BlockSpec((8, 128), lambda b: (0, 0))],
        compiler_params=_cp(vmem=VMEM_BIG),
    )(p, prm, dgf, dgb)


def _lockstep(gens, joint=None):
    n = len(gens)
    res, vals = [None] * n, [None] * n
    while True:
        msgs = []
        for i, g in enumerate(gens):
            try:
                msgs.append((g.send(vals[i]), False))
            except StopIteration as e:
                res[i] = e.value
                msgs.append((None, True))
        if msgs[0][1]:
            return res
        if msgs[0][0] is not None:
            vals = list(joint(tuple(m[0] for m in msgs)))
        else:
            vals = [None] * n


def _tri_inv_gen(m, md, m1, m2, eye):
    n = -(m * md)
    t = eye + n
    pw = n
    for _ in range(3):
        pw = _dg(pw, pw, _NN)
        yield
        t = t + _dg(t, pw, _NN)
        yield
    a = _dg(t, m * m1, _NN)
    yield
    t = t - _dg(a, t, _NN)
    yield
    a = _dg(t, m * m2, _NN)
    yield
    return t - _dg(a, t, _NN)


def _tri_inv_bwd_gen(t, g):
    tt = t.T
    a = _dg(tt, g, _NN)
    yield
    return -_dg(a, tt, _NN)


@jax.custom_vjp
def _tri_inv(ms, md, m1, m2, eye):
    return tuple(_lockstep([_tri_inv_gen(m, md, m1, m2, eye) for m in ms]))


def _tri_inv_f(ms, md, m1, m2, eye):
    ts = _tri_inv(ms, md, m1, m2, eye)
    return ts, (ts, md)


def _tri_inv_b(res, gs):
    ts, md = res
    z = jnp.zeros_like(md)
    return (tuple(_lockstep([_tri_inv_bwd_gen(t, g) for t, g in zip(ts, gs)])), z, z, z, z)


_tri_inv.defvjp(_tri_inv_f, _tri_inv_b)


@jax.custom_vjp
def _tri_inv_saved(ms, ts):
    return tuple(t.astype(F32) for t in ts)


def _tri_inv_saved_f(ms, ts):
    return _tri_inv_saved(ms, ts), ts


def _tri_inv_saved_b(ts, gs):
    dms = tuple(_lockstep([_tri_inv_bwd_gen(t, g) for t, g in zip(ts, gs)]))
    return dms, tuple(jnp.zeros_like(t) for t in ts)


_tri_inv_saved.defvjp(_tri_inv_saved_f, _tri_inv_saved_b)


SR = DNH * CH


def _scan_masks():
    i, j = np.arange(SR)[:, None], np.arange(SR)[None, :]
    same = (i >> 6) == (j >> 6)
    cm = np.zeros((2, 6, SR, SR), np.float32)
    for d in range(2):
        diff = i - j if d == 0 else j - i
        cm[d, 0] = same & (diff >= 0)
        cm[d, 1] = same & (diff > 0)
        cm[d, 2] = (i >> 4) == (j >> 4)
        cm[d, 3] = ((i >> 5) == (j >> 5)) & ((i >> 4) != (j >> 4))
        cm[d, 4] = (i >> 5) != (j >> 5)
        cm[d, 5] = i == j
    t, u = np.arange(CH)[:, None], np.arange(CH)[None, :]
    c64 = np.stack([t >= u, t <= u]).astype(np.float32)
    bm = ((np.arange(SR)[:, None] >> 6) == (np.arange(DNW)[None, :] >> 7)).astype(np.float32)
    return jnp.asarray(cm), jnp.asarray(c64), jnp.asarray(bm)


@jax.custom_vjp
def _fold(x, bm):
    xm = x * bm
    return xm[:, 0:128] + xm[:, 128:256] + xm[:, 256:384] + xm[:, 384:512]


def _fold_f(x, bm):
    return _fold(x, bm), bm


def _fold_b(bm, g):
    return jnp.concatenate([g, g, g, g], axis=1) * bm, jnp.zeros_like(bm)


_fold.defvjp(_fold_f, _fold_b)


def _chunk_gen(q, k, v, gb, s, cm, c64, bm, d):
    incl, strict = cm[0], cm[1]
    lane = lax.broadcasted_iota(jnp.int32, (1, 128), 1)
    oh = lambda n: jnp.where(lane == n, 1.0, 0.0).astype(F32)
    col = lambda x, base: jnp.concatenate(
        [jnp.sum(x * oh(base + d * DNH + h), axis=1, keepdims=True) for h in range(DNH)], axis=0)
    beta = col(gb, 8)
    gam = col(_mask_mm(c64, gb), 0)
    yield
    tot = jnp.sum(gb, axis=0, keepdims=True)
    gtots = [jnp.sum(tot * oh(d * DNH + h), axis=1, keepdims=True) for h in range(DNH)]
    gtot = jnp.concatenate([jnp.broadcast_to(t, (CH, 1)) for t in gtots], axis=0)
    eg = jnp.concatenate([jnp.broadcast_to(jnp.exp(t), (1, 128)) for t in gtots], axis=1)
    row0 = jnp.where(lax.broadcasted_iota(jnp.int32, (128, 1), 0) == 0, 1.0, 0.0).astype(F32)
    gam_row = jnp.sum(jnp.broadcast_to(gam, (SR, 128)).T * row0, axis=0, keepdims=True)
    dec = jnp.exp(jnp.where(incl > 0.5, gam - gam_row, NEG))
    e_gam = jnp.exp(gam)
    kb = k * beta
    kk = _mm_nt(kb, k)
    yield
    tm = yield strict * (kk * dec)
    u = _mm(tm, v * beta)
    yield
    w = _mm(tm, kb * e_gam)
    yield
    qk = _mm_nt(q, k) * dec
    yield
    ws = _mm(w, s)
    yield
    v_new = u - _fold(ws, bm)
    qs = _mm(q * e_gam, s)
    yield
    o = _fold(qs, bm) + _mm(qk, v_new)
    yield
    vbd = jnp.concatenate([v_new, v_new, v_new, v_new], axis=1) * bm
    s_new = s * eg + _mm_tn(k * jnp.exp(gtot - gam), vbd)
    return o, s_new, tm


def _chunks(args, cm_ref, c64_ref, bm, saved=None):
    cm0 = cm_ref[0]
    if saved is None:
        inv = lambda ms: _tri_inv(ms, cm0[2], cm0[3], cm0[4], cm0[5])
    else:
        inv = lambda ms: _tri_inv_saved(ms, saved)
    return _lockstep([_chunk_gen(q, k, v, gb, s, cm_ref[d], c64_ref[d], bm, d) for q, k, v, gb, s, d in args], inv)


def _chunk_rev(s):
    return jnp.where(s < NCH_CTX, NCH_CTX - 1 - s, NCH + NCH_CTX - 1 - s)


def _scan_specs(step):
    cf = lambda s: step(s)
    cb = lambda s: _chunk_rev(step(s))
    tok = lambda c: pl.BlockSpec((NB, CH, DNW), lambda s: (0, c(s), 0))
    gat = lambda c: pl.BlockSpec((NB, CH, 128), lambda s: (0, c(s), 0))
    sta = lambda c: pl.BlockSpec((NB, None, 128, DNW), lambda s: (0, c(s), 0, 0))
    inv = lambda c: pl.BlockSpec((NB, None, SR, SR), lambda s: (0, c(s), 0, 0))
    return (tok(cf), gat(cf), sta(cf), inv(cf)), (tok(cb), gat(cb), sta(cb), inv(cb))


def _const_specs():
    return [pl.BlockSpec((2, 6, SR, SR), lambda s: (0, 0, 0, 0)), pl.BlockSpec((2, CH, CH), lambda s: (0, 0, 0)),
            pl.BlockSpec((SR, DNW), lambda s: (0, 0))]


SCAN_CHAINS = [(b, d) for b in range(NB) for d in range(2)]


def _stack(ref, b):
    return jnp.concatenate([ref[b, :, 128 * h:128 * h + 128] for h in range(DNH)], axis=0)


def _unstack_to(ref, b, val):
    for h in range(DNH):
        ref[b, :, 128 * h:128 * h + 128] = val[CH * h:CH * h + CH]


def _gather_plan(ins, outs, pos):
    me = _slot(pos)
    locals_, xfers = [], []
    for src, dst in zip(ins, outs):
        locals_.append((src, dst.at[me]))
        for k in range(1, 8):
            peer = _flip(pos, k)
            xfers.append((src, dst.at[me], dst.at[_slot(peer)], peer))
    return locals_, xfers


def _scan_fwd(q, k, v, gates, carry=()):
    nc = len(carry)

    def body(*refs):
        cm_ref, c64_ref, bm_ref, qf, kf, vf, gf, qb, kb, vb, gb = refs[:11]
        c_in = refs[11:11 + nc]
        of_ref, sf_ref, tf_ref, ob_ref, sb_ref, tb_ref = refs[11 + nc:17 + nc]
        c_out = refs[17 + nc:17 + 2 * nc]
        s_ref = refs[17 + 2 * nc]
        sems = refs[18 + 2 * nc:]
        if nc:
            @pl.when(pl.program_id(0) == 0)
            def _():
                _xfer_start(_gather_plan(c_in, c_out, _my_pos()), sems)

        @pl.when(pl.program_id(0) == 0)
        def _():
            s_ref[...] = jnp.zeros_like(s_ref)

        bm = bm_ref[...]
        dirs = ((qf, kf, vf, gf, of_ref, sf_ref, tf_ref), (qb, kb, vb, gb, ob_ref, sb_ref, tb_ref))
        args = [(_stack(dirs[d][0], b), _stack(dirs[d][1], b), _stack(dirs[d][2], b), dirs[d][3][b], s_ref[i], d)
                for i, (b, d) in enumerate(SCAN_CHAINS)]
        for i, (o, s_new, tm) in enumerate(_chunks(args, cm_ref, c64_ref, bm)):
            b, d = SCAN_CHAINS[i]
            dirs[d][5][b] = args[i][4]
            dirs[d][6][b] = tm.astype(BF16)
            _unstack_to(dirs[d][4], b, o)
            s_ref[i] = s_new

        if nc:
            @pl.when(pl.program_id(0) == NCH - 1)
            def _():
                _xfer_wait(_gather_plan(c_in, c_out, _my_pos()), sems)

    (tf, gf_, sf_, if_), (tb, gb_, sb_, ib_) = _scan_specs(lambda s: s)
    o_sh = jax.ShapeDtypeStruct((NB, T, DNW), F32)
    st_sh = jax.ShapeDtypeStruct((NB, NCH, 128, DNW), F32)
    inv_sh = jax.ShapeDtypeStruct((NB, NCH, SR, SR), BF16)
    hbm = pl.BlockSpec(memory_space=pltpu.HBM)
    return pl.pallas_call(
        body, name="scan_fwd", grid=(NCH,),
        out_shape=[o_sh, st_sh, inv_sh, o_sh, st_sh, inv_sh] + [jax.ShapeDtypeStruct((8,) + h.shape, h.dtype) for h in carry],
        in_specs=_const_specs() + [tf, tf, tf, gf_, tb, tb, tb, gb_] + [hbm] * nc,
        out_specs=[tf, sf_, if_, tb, sb_, ib_] + [hbm] * nc,
        scratch_shapes=[pltpu.VMEM((len(SCAN_CHAINS), 128, DNW), F32)] + (_xfer_sems(nc, 7 * nc) if nc else []),
        compiler_params=_cp(vmem=VMEM_BIG),
    )(*_scan_masks(), q, k, v, gates, q, k, v, gates, *carry)


def _scan_bwd(q, k, v, gates, st_f, st_b, inv_f, inv_b, do, carry=()):
    nc = len(carry)

    def body(*refs):
        (cm_ref, c64_ref, bm_ref, qf, kf, vf, gf, sf, tf_, dof, qb, kb, vb, gb, sb, tb_, dob) = refs[:17]
        c_in = refs[17:17 + nc]
        dqf, dkf, dvf, dgf, dqb, dkb, dvb, dgb = refs[17 + nc:25 + nc]
        c_out = refs[25 + nc:25 + 2 * nc]
        ds_ref = refs[25 + 2 * nc]
        sems = refs[26 + 2 * nc:]
        first = pl.program_id(0) == 0
        last = pl.program_id(0) == NCH - 1
        if nc:
            @pl.when(first)
            def _():
                _xfer_start(_scatter_plan(c_in, c_out, _my_pos()), sems)

        @pl.when(first)
        def _():
            ds_ref[...] = jnp.zeros_like(ds_ref)

        bm = bm_ref[...]
        dirs = ((qf, kf, vf, gf, sf, dof, dqf, dkf, dvf, dgf), (qb, kb, vb, gb, sb, dob, dqb, dkb, dvb, dgb))
        prim = tuple((_stack(dirs[d][0], b), _stack(dirs[d][1], b), _stack(dirs[d][2], b), dirs[d][3][b], dirs[d][4][b])
                     for b, d in SCAN_CHAINS)
        cots = tuple((_stack(dirs[d][5], b), ds_ref[i]) for i, (b, d) in enumerate(SCAN_CHAINS))
        saved = tuple((tf_, tb_)[d][b] for b, d in SCAN_CHAINS)
        _, vjp = jax.vjp(lambda ps: tuple(r[:2] for r in _chunks([p + (SCAN_CHAINS[i][1],) for i, p in enumerate(ps)],
                                                                cm_ref, c64_ref, bm, saved)), prim)
        (grads,) = vjp(cots)
        for i, (dq, dk, dv, dg, ds) in enumerate(grads):
            b, d = SCAN_CHAINS[i]
            _unstack_to(dirs[d][6], b, dq)
            _unstack_to(dirs[d][7], b, dk)
            _unstack_to(dirs[d][8], b, dv)
            dirs[d][9][b] = dg
            ds_ref[i] = ds

        if nc:
            @pl.when(last)
            def _():
                _xfer_wait(_scatter_plan(c_in, c_out, _my_pos()), sems)

    (tf, gf_, sf_, if_), (tb, gb_, sb_, ib_) = _scan_specs(lambda s: NCH - 1 - s)
    o_sh = jax.ShapeDtypeStruct((NB, T, DNW), F32)
    g_sh = jax.ShapeDtypeStruct((NB, T, 128), F32)
    hbm = pl.BlockSpec(memory_space=pltpu.HBM)
    return pl.pallas_call(
        body, name="scan_bwd", grid=(NCH,),
        out_shape=[o_sh, o_sh, o_sh, g_sh] * 2 + [jax.ShapeDtypeStruct(g.shape, g.dtype) for g in carry],
        in_specs=_const_specs() + [tf, tf, tf, gf_, sf_, if_, tf, tb, tb, tb, gb_, sb_, ib_, tb] + [hbm] * nc,
        out_specs=[tf, tf, tf, gf_, tb, tb, tb, gb_] + [hbm] * nc,
        scratch_shapes=[pltpu.VMEM((len(SCAN_CHAINS), 128, DNW), F32)] + (_xfer_sems(nc, 7 * nc) if nc else []),
        compiler_params=_cp(vmem=VMEM_BIG),
    )(*_scan_masks(), q, k, v, gates, st_f, inv_f, do, q, k, v, gates, st_b, inv_b, do, *carry)


def _segmean(a, bd):
    hi = a.astype(BF16)
    lo = (a - hi.astype(F32)).astype(BF16)
    return jnp.dot(hi, bd, preferred_element_type=F32) + jnp.dot(lo, bd, preferred_element_type=F32)


def _na_prep_fwd(p, qw, kw, bd):
    def body(q_ref, k_ref, v_ref, qw_ref, kw_ref, bd_ref, oq_ref, ok_ref, ov_ref):
        bdm = bd_ref[...]
        q = q_ref[...]
        k = k_ref[...]
        oq_ref[...] = (q * lax.rsqrt(_segmean(q * q, bdm) + EPS) * qw_ref[...] * 0.125).astype(BF16)
        ok_ref[...] = (k * lax.rsqrt(_segmean(k * k, bdm) + EPS) * kw_ref[...]).astype(BF16)
        ov_ref[...] = v_ref[...].astype(BF16)

    ps = lambda o: pl.BlockSpec((None, RT, NAW), lambda b, j: (b, j, COL_NA // NAW + o))
    one = pl.BlockSpec((1, NAW), lambda b, j: (0, 0))
    os_ = pl.BlockSpec((None, RT, NAW), lambda b, j: (b, j, 0))
    return pl.pallas_call(
        body, name="na_prep_fwd", grid=(NB, T // RT), out_shape=[jax.ShapeDtypeStruct((NB, T, NAW), BF16)] * 3,
        in_specs=[ps(0), ps(1), ps(2), one, one, pl.BlockSpec((NAW, NAW), lambda b, j: (0, 0))], out_specs=[os_] * 3,
    )(p, p, p, qw, kw, bd)


def _na_prep_bwd(p, qw, kw, bd, dq, dk, dv):
    def body(q_ref, k_ref, qw_ref, kw_ref, bd_ref, dq_ref, dk_ref, dv_ref, o_ref, gq_ref, gk_ref):
        b, j = pl.program_id(0), pl.program_id(1)
        bdm = bd_ref[...]

        def one(x, w, dy, scale):
            r = lax.rsqrt(_segmean(x * x, bdm) + EPS)
            g = dy * w * scale
            dx = r * g - x * (r * r * r) * _segmean(g * x, bdm)
            dw = jnp.sum(dy * x * r * scale, axis=0, keepdims=True)
            return dx, dw

        dqv = jnp.where(j == 0, 0.0, dq_ref[...])
        dxq, dwq = one(q_ref[...], qw_ref[...], dqv, 0.125)
        dxk, dwk = one(k_ref[...], kw_ref[...], dk_ref[...], 1.0)
        o_ref[...] = jnp.concatenate([dxq, dxk, dv_ref[...]], axis=1)
        pad = jnp.zeros((7, NAW), F32)

        @pl.when((b == 0) & (j == 0))
        def _():
            gq_ref[...] = jnp.zeros_like(gq_ref)
            gk_ref[...] = jnp.zeros_like(gk_ref)

        gq_ref[...] += jnp.concatenate([dwq, pad], axis=0)
        gk_ref[...] += jnp.concatenate([dwk, pad], axis=0)

    ps = lambda o: pl.BlockSpec((None, RT, NAW), lambda b, j: (b, j, COL_NA // NAW + o))
    one = pl.BlockSpec((1, NAW), lambda b, j: (0, 0))
    ts = pl.BlockSpec((None, RT, NAW), lambda b, j: (b, j, 0))
    acc = pl.BlockSpec((8, NAW), lambda b, j: (0, 0))
    return pl.pallas_call(
        body, name="na_prep_bwd", grid=(NB, T // RT),
        out_shape=[jax.ShapeDtypeStruct((NB, T, 1536), F32), jax.ShapeDtypeStruct((8, NAW), F32),
                   jax.ShapeDtypeStruct((8, NAW), F32)],
        in_specs=[ps(0), ps(1), one, one, pl.BlockSpec((NAW, NAW), lambda b, j: (0, 0)),
                  pl.BlockSpec((None, RT, NAW), lambda b, j: (b, jnp.maximum(j - 1, 0), 0)), ts, ts],
        out_specs=[pl.BlockSpec((None, RT, 1536), lambda b, j: (b, j, 0)), acc, acc],
        compiler_params=_cp(vmem=VMEM_BIG),
    )(p, p, qw, kw, bd, dq, dk, dv)


def _na_row0(r):
    return jnp.clip(r - WIN_ROWS // 2, 0, ROWS - WIN_ROWS)


def _na_cls(r):
    return _na_row0(r) - r + WIN_ROWS - 1


NHS = 4
NLW = NHS * 64
NSR = NHS * GRID_W


def _na_stack(x):
    lane = lax.broadcasted_iota(jnp.int32, (1, NLW), 1)
    return jnp.concatenate([jnp.where((lane >> 6) == h, x, jnp.zeros_like(x)) for h in range(NHS)], axis=0)


def _na_unstack(y):
    lane = lax.broadcasted_iota(jnp.int32, (1, NLW), 1)
    out = jnp.zeros((GRID_W, NLW), y.dtype)
    for h in range(NHS):
        out = jnp.where((lane >> 6) == h, y[GRID_W * h:GRID_W * h + GRID_W], out)
    return out


def _na_fwd_gen(q, kw, vw, kc, vc, bias):
    q4 = _na_stack(q)
    s_w = lax.dot_general(q4, kw, _NT, preferred_element_type=F32) + bias
    yield
    s_c = lax.dot_general(q4, kc, _NT, preferred_element_type=F32)
    yield
    mx = jnp.maximum(jnp.max(s_w, axis=1, keepdims=True), jnp.max(s_c, axis=1, keepdims=True))
    p_w = jnp.exp(s_w - mx)
    p_c = jnp.exp(s_c - mx)
    inv = 1.0 / (jnp.sum(p_w, axis=1, keepdims=True) + jnp.sum(p_c, axis=1, keepdims=True))
    yield
    o_w = jnp.dot(p_w.astype(BF16), vw, preferred_element_type=F32)
    yield
    o4 = (o_w + jnp.dot(p_c.astype(BF16), vc, preferred_element_type=F32)) * inv
    return _na_unstack(o4)


NRF = 2


def _na_fwd(qn, kn, vb, bias, carry=()):
    nc = len(carry)
    grid = (NB, ROWS // NRF)
    ng = NAH // NHS

    def body(*refs):
        q_ref, k_ref, v_ref = refs[:3]
        b_refs, rest = refs[3:3 + NRF], refs[3 + NRF:]
        c_in, o_ref, c_out, sems = rest[:nc], rest[nc], rest[nc + 1:2 * nc + 1], rest[2 * nc + 1:]
        ids = [pl.program_id(a) for a in range(2)]
        if nc:
            @pl.when((ids[0] == 0) & (ids[1] == 0))
            def _():
                _xfer_start(_gather_plan(c_in, c_out, _my_pos()), sems)

        gens = []
        for i in range(NRF):
            start = pl.multiple_of(CT + _na_row0(NRF * ids[1] + i) * GRID_W, GRID_W)
            rw = slice(GRID_W * i, GRID_W * (i + 1))
            for g in range(ng):
                ln = slice(NLW * g, NLW * g + NLW)
                gens.append(_na_fwd_gen(q_ref[rw, ln], k_ref[pl.ds(start, NWIN), ln], v_ref[pl.ds(start, NWIN), ln],
                                        k_ref[0:CT, ln], v_ref[0:CT, ln], b_refs[i][g]))
        for n, o in enumerate(_lockstep(gens)):
            i, g = n // ng, n % ng
            o_ref[GRID_W * i:GRID_W * (i + 1), NLW * g:NLW * g + NLW] = o.astype(BF16)

        if nc:
            @pl.when((ids[0] == grid[0] - 1) & (ids[1] == grid[1] - 1))
            def _():
                _xfer_wait(_gather_plan(c_in, c_out, _my_pos()), sems)

    qs = pl.BlockSpec((None, NRF * GRID_W, NAW), lambda b, j: (b, CT // (NRF * GRID_W) + j, 0))
    kv = pl.BlockSpec((None, T, NAW), lambda b, j: (b, 0, 0))
    bss = [pl.BlockSpec((None, ng, NSR, NWIN), lambda b, j, i=i: (_na_cls(NRF * j + i), 0, 0, 0)) for i in range(NRF)]
    lat = pl.BlockSpec((None, NRF * GRID_W, NAW), lambda b, j: (b, j, 0))
    hbm = pl.BlockSpec(memory_space=pltpu.HBM)
    res = pl.pallas_call(
        body, name="na_fwd", grid=grid,
        out_shape=[jax.ShapeDtypeStruct((NB, SEQ, NAW), BF16)] + [jax.ShapeDtypeStruct((8,) + h.shape, h.dtype) for h in carry],
        in_specs=[qs, kv, kv] + bss + [hbm] * nc, out_specs=[lat] + [hbm] * nc,
        scratch_shapes=_xfer_sems(nc, 7 * nc) if nc else [],
        compiler_params=_cp(vmem=VMEM_BIG),
    )(qn, kn, vb, *([bias] * NRF), *carry)
    return res[0], list(res[1:])


def _na_bwd_gen(q, kw, vw, kc, vc, bias, do):
    q4 = _na_stack(q)
    s_w = lax.dot_general(q4, kw, _NT, preferred_element_type=F32) + bias
    yield
    s_c = lax.dot_general(q4, kc, _NT, preferred_element_type=F32)
    yield
    mx = jnp.maximum(jnp.max(s_w, axis=1, keepdims=True), jnp.max(s_c, axis=1, keepdims=True))
    p_w = jnp.exp(s_w - mx)
    p_c = jnp.exp(s_c - mx)
    inv = 1.0 / (jnp.sum(p_w, axis=1, keepdims=True) + jnp.sum(p_c, axis=1, keepdims=True))
    p_w = p_w * inv
    p_c = p_c * inv
    do4 = _na_stack(do.astype(BF16))
    dp_w = lax.dot_general(do4, vw, _NT, preferred_element_type=F32)
    yield
    dp_c = lax.dot_general(do4, vc, _NT, preferred_element_type=F32)
    yield
    delta = jnp.sum(p_w * dp_w, axis=1, keepdims=True) + jnp.sum(p_c * dp_c, axis=1, keepdims=True)
    ds_w = p_w * (dp_w - delta)
    ds_c = p_c * (dp_c - delta)
    ds_wb, ds_cb = ds_w.astype(BF16), ds_c.astype(BF16)
    dq4 = jnp.dot(ds_wb, kw, preferred_element_type=F32)
    yield
    dq4 = dq4 + jnp.dot(ds_cb, kc, preferred_element_type=F32)
    yield
    dkw = lax.dot_general(ds_wb, q4, _TN, preferred_element_type=F32)
    yield
    dvw = lax.dot_general(p_w.astype(BF16), do4, _TN, preferred_element_type=F32)
    yield
    dkc = lax.dot_general(ds_cb, q4, _TN, preferred_element_type=F32)
    yield
    dvc = lax.dot_general(p_c.astype(BF16), do4, _TN, preferred_element_type=F32)
    return ds_w, _na_unstack(dq4), dkw, dvw, dkc, dvc


NRS = 4


def _na_bwd(qn, kn, vb, bias, do):
    def body(q_ref, k_ref, v_ref, *rest):
        b_refs, (do_ref, dq_ref, dk_ref, dv_ref, db_ref) = rest[:NRS], rest[NRS:]
        b, j = pl.program_id(1), pl.program_id(2)

        @pl.when((b == 0) & (j == 0))
        def _():
            db_ref[...] = jnp.zeros_like(db_ref)

        @pl.when(j == 0)
        def _():
            dk_ref[...] = jnp.zeros_like(dk_ref)
            dv_ref[...] = jnp.zeros_like(dv_ref)

        kc, vc = k_ref[0:CT, :], v_ref[0:CT, :]
        rows = [NRS * j + i for i in range(NRS)]
        starts = [pl.multiple_of(CT + _na_row0(r) * GRID_W, GRID_W) for r in rows]
        gens = [_na_bwd_gen(q_ref[GRID_W * i:GRID_W * (i + 1), :], k_ref[pl.ds(starts[i], NWIN), :],
                            v_ref[pl.ds(starts[i], NWIN), :], kc, vc, b_refs[i][...],
                            do_ref[GRID_W * i:GRID_W * (i + 1), :]) for i in range(NRS)]
        for i, (ds_w, dq, dkw, dvw, dkc, dvc) in enumerate(_lockstep(gens)):
            db_ref[_na_cls(rows[i])] += ds_w
            dq_ref[GRID_W * i:GRID_W * (i + 1), :] = dq
            dk_ref[pl.ds(starts[i], NWIN), :] += dkw
            dv_ref[pl.ds(starts[i], NWIN), :] += dvw
            dk_ref[0:CT, :] += dkc
            dv_ref[0:CT, :] += dvc

    qs = pl.BlockSpec((None, NRS * GRID_W, NLW), lambda g, b, j: (b, CT // (NRS * GRID_W) + j, g))
    kv = pl.BlockSpec((None, T, NLW), lambda g, b, j: (b, 0, g))
    bss = [pl.BlockSpec((None, None, NSR, NWIN), lambda g, b, j, i=i: (_na_cls(NRS * j + i), g, 0, 0)) for i in range(NRS)]
    lat = pl.BlockSpec((None, NRS * GRID_W, NLW), lambda g, b, j: (b, j, g))
    return pl.pallas_call(
        body, name="na_bwd", grid=(NAH // NHS, NB, ROWS // NRS),
        out_shape=[jax.ShapeDtypeStruct((NB, SEQ, NAW), F32)] + [jax.ShapeDtypeStruct((NB, T, NAW), F32)] * 2
        + [jax.ShapeDtypeStruct((WIN_ROWS, NAH // NHS, NSR, NWIN), F32)],
        in_specs=[qs, kv, kv] + bss + [lat],
        out_specs=[lat, kv, kv, pl.BlockSpec((WIN_ROWS, None, NSR, NWIN), lambda g, b, j: (0, g, 0, 0))],
        compiler_params=_cp(vmem=VMEM_BIG),
    )(qn, kn, vb, *([bias] * NRS), do)


def _na_tables():
    cols = np.arange(GRID_W)
    win_start = np.clip(cols - WIN_COLS // 2, 0, GRID_W - WIN_COLS)
    kc = cols[None, :]
    valid = (kc >= win_start[:, None]) & (kc < win_start[:, None] + WIN_COLS)
    rel = np.clip(kc - cols[:, None] + WIN_COLS - 1, 0, 2 * WIN_COLS - 2)
    return valid, rel


def _bias_table(rpb):
    valid, rel = _na_tables()
    nrel = 2 * WIN_COLS - 1
    shifts = np.stack([(rel == r) & valid for r in range(nrel)]).astype(np.float32)
    neg = np.where(valid, 0.0, NEG).astype(np.float32)

    def body(rpb_ref, sh_ref, neg_ref, o_ref):
        h = pl.program_id(0)
        for dr in range(2 * WIN_ROWS - 1):
            tile = neg_ref[...]
            for r in range(nrel):
                tile = tile + rpb_ref[h, dr, r] * sh_ref[r]
            for cls in range(WIN_ROWS):
                i = dr - cls
                if 0 <= i < WIN_ROWS:
                    o_ref[cls, :, i * GRID_W:(i + 1) * GRID_W] = tile

    return pl.pallas_call(
        body, name="bias_table", grid=(NAH,),
        out_shape=jax.ShapeDtypeStruct((WIN_ROWS, NAH // NHS, NSR, NWIN), F32),
        in_specs=[pl.BlockSpec(memory_space=pltpu.SMEM), pl.BlockSpec((nrel, GRID_W, GRID_W), lambda h: (0, 0, 0)),
                  pl.BlockSpec((GRID_W, GRID_W), lambda h: (0, 0))],
        out_specs=pl.BlockSpec((WIN_ROWS, None, GRID_W, NWIN), lambda h: (0, h // NHS, h % NHS, 0)),
    )(rpb, jnp.asarray(shifts), jnp.asarray(neg))


def _rpb_fold(db):
    def body1(d_ref, o_ref):
        for dr in range(2 * WIN_ROWS - 1):
            acc = None
            for c in range(WIN_ROWS):
                i = dr - c
                if 0 <= i < WIN_ROWS:
                    blk = d_ref[c, :, i * GRID_W:(i + 1) * GRID_W]
                    acc = blk if acc is None else acc + blk
            o_ref[dr] = acc

    t15 = pl.pallas_call(
        body1, name="rpb_fold_rows", grid=(NAH,),
        out_shape=jax.ShapeDtypeStruct((NAH, 2 * WIN_ROWS - 1, GRID_W, GRID_W), F32),
        in_specs=[pl.BlockSpec((WIN_ROWS, None, GRID_W, NWIN), lambda h: (0, h, 0, 0))],
        out_specs=pl.BlockSpec((None, 2 * WIN_ROWS - 1, GRID_W, GRID_W), lambda h: (h, 0, 0, 0)),
        compiler_params=_cp(vmem=VMEM_BIG),
    )(db.reshape(WIN_ROWS, NAH, GRID_W, NWIN))
    valid, rel = _na_tables()
    onehot = np.zeros((GRID_W * GRID_W, 128), np.float32)
    flat_rel, flat_valid = rel.reshape(-1), valid.reshape(-1)
    onehot[np.arange(GRID_W * GRID_W)[flat_valid], flat_rel[flat_valid]] = 1.0

    def body2(a_ref, oh_ref, o_ref):
        o_ref[...] = _hmm(a_ref[...], oh_ref[...])

    vm = pl.BlockSpec(memory_space=pltpu.VMEM)
    out = pl.pallas_call(
        body2, name="rpb_fold_cols", out_shape=jax.ShapeDtypeStruct((NAH * 15, 128), F32),
        in_specs=[vm, vm], out_specs=vm, compiler_params=_cp(vmem=VMEM_BIG),
    )(t15.reshape(NAH * 15, GRID_W * GRID_W), jnp.asarray(onehot))
    return out[:, :31].reshape(NAH, 15, 31)


def _dn_out(o, z, w):
    outs = []
    for h in range(DNH):
        sl = slice(128 * h, 128 * h + 128)
        oh = o[:, sl]
        r = lax.rsqrt(jnp.mean(oh * oh, axis=-1, keepdims=True) + EPS)
        outs.append(oh * r * w[:, sl] * _silu(z[:, sl]))
    return jnp.concatenate(outs, axis=1)


def _outproj_fwd(of, ob, p, na, onw, x, g1, w_out):
    def body(of_ref, ob_ref, z_ref, na_ref, w_ref, x_ref, g_ref, wo_ref, x1_ref, ai_ref, ao_ref):
        dn = _dn_out(of_ref[...] + ob_ref[...], z_ref[...], w_ref[...])
        ai = jnp.concatenate([dn.astype(BF16), na_ref[...]], axis=1)
        ai_ref[...] = ai
        ao = jnp.dot(ai, wo_ref[...], preferred_element_type=F32)
        ao_ref[...] = ao
        x1_ref[...] = x_ref[...] + g_ref[...] * ao

    lat = lambda w: pl.BlockSpec((None, RT, w), lambda b, j: (b, j, 0))
    return pl.pallas_call(
        body, name="outproj_fwd", grid=(NB, SEQ // RT),
        out_shape=[jax.ShapeDtypeStruct((NB, SEQ, D), F32), jax.ShapeDtypeStruct((NB, SEQ, D), BF16),
                   jax.ShapeDtypeStruct((NB, SEQ, D), F32)],
        in_specs=[pl.BlockSpec((None, RT, DNW), lambda b, j: (b, j + 1, 0)),
                  pl.BlockSpec((None, RT, DNW), lambda b, j: (b, j + 1, 0)),
                  pl.BlockSpec((None, RT, DNW), lambda b, j: (b, j + 1, COL_Z // DNW)),
                  lat(NAW), pl.BlockSpec((1, DNW), lambda b, j: (0, 0)), lat(D),
                  pl.BlockSpec((None, 1, D), lambda b, j: (b, 0, 0)), pl.BlockSpec((D, D), lambda b, j: (0, 0))],
        out_specs=[lat(D), lat(D), lat(D)],
        compiler_params=_cp(vmem=VMEM_BIG),
    )(of, ob, p, na, onw, x, g1, w_out)


def _outproj_bwd(dx1, ao, g1, w_out, of, ob, p, onw):
    def body(dx_ref, ao_ref, g_ref, wo_ref, of_ref, ob_ref, z_ref, w_ref, dao_ref, do_ref, dz_ref, dna_ref, dg_ref, dw_ref):
        b, j = pl.program_id(0), pl.program_id(1)

        @pl.when((b == 0) & (j == 0))
        def _():
            dw_ref[...] = jnp.zeros_like(dw_ref)

        @pl.when(j == 0)
        def _():
            do_ref[...] = jnp.zeros_like(do_ref)
            dz_ref[...] = jnp.zeros_like(dz_ref)

        @pl.when(j == 1)
        def _():
            dg_ref[...] = jnp.zeros_like(dg_ref)

        @pl.when(j >= 1)
        def _():
            dx = dx_ref[...]
            dg_ref[...] += jnp.sum(dx * ao_ref[...], axis=0, keepdims=True)
            dao = (g_ref[...] * dx).astype(BF16)
            dao_ref[...] = dao
            dai = lax.dot_general(dao, wo_ref[...], _NT, preferred_element_type=F32)
            dna_ref[...] = dai[:, DNW:]
            _, vjp = jax.vjp(_dn_out, of_ref[...] + ob_ref[...], z_ref[...], w_ref[...])
            do, dz, dw = vjp(dai[:, :DNW])
            do_ref[...] = do
            dz_ref[...] = dz
            dw_ref[...] += jnp.concatenate([dw, jnp.zeros((7, DNW), F32)], axis=0)

    lat = lambda w: pl.BlockSpec((None, RT, w), lambda b, j: (b, jnp.maximum(j - 1, 0), 0))
    tok = lambda w: pl.BlockSpec((None, RT, w), lambda b, j: (b, j, 0))
    return pl.pallas_call(
        body, name="outproj_bwd", grid=(NB, T // RT),
        out_shape=[jax.ShapeDtypeStruct((NB, SEQ, D), BF16), jax.ShapeDtypeStruct((NB, T, DNW), F32),
                   jax.ShapeDtypeStruct((NB, T, DNW), F32), jax.ShapeDtypeStruct((NB, SEQ, NAW), F32),
                   jax.ShapeDtypeStruct((NB, 1, D), F32), jax.ShapeDtypeStruct((8, DNW), F32)],
        in_specs=[_rows2d_spec(D), lat(D), pl.BlockSpec((None, 1, D), lambda b, j: (b, 0, 0)),
                  pl.BlockSpec((D, D), lambda b, j: (0, 0)),
                  tok(DNW), tok(DNW),
                  pl.BlockSpec((None, RT, DNW), lambda b, j: (b, j, COL_Z // DNW)),
                  pl.BlockSpec((1, DNW), lambda b, j: (0, 0))],
        out_specs=[lat(D), tok(DNW), tok(DNW), lat(NAW), pl.BlockSpec((None, 1, D), lambda b, j: (b, 0, 0)),
                   pl.BlockSpec((8, DNW), lambda b, j: (0, 0))],
        compiler_params=_cp(vmem=VMEM_BIG),
    )(dx1, ao, g1, w_out, of, ob, p, onw)


FT_FWD, FT_BWD = 512, 256
NFF = DFF // FFT


def _ffn_specs(ft):
    row = lambda w: pl.BlockSpec((ft, w), lambda i, f: (i, 0))
    ex = pl.BlockSpec((None, 1, D), lambda i, f: (i // (SEQ // ft), 0, 0))
    one = pl.BlockSpec((1, D), lambda i, f: (0, 0))
    wg = pl.BlockSpec((None, D, FFT), lambda i, f: (f, 0, 0))
    wu = pl.BlockSpec((None, D, FFT), lambda i, f: (NFF + f, 0, 0))
    wo = pl.BlockSpec((FFT, D), lambda i, f: (f, 0))
    ff = pl.BlockSpec((ft, FFT), lambda i, f: (i, f))
    return row, ex, one, wg, wu, wo, ff


def _ffn_fwd(x1, sh2, sc2, g2, n2w, w_in4, w_out, target):
    ft, tpe = FT_FWD, SEQ // FT_FWD

    def body(x_ref, sh_ref, sc_ref, g_ref, w2_ref, wg_ref, wu_ref, wo_ref, t_ref,
             h2_ref, gt_ref, up_ref, dy_ref, dg_ref, loss_ref, acc_ref):
        i, f = pl.program_id(0), pl.program_id(1)

        @pl.when(f == 0)
        def _():
            h2_ref[...] = _norm_mod(x_ref[...], w2_ref[...], sh_ref[...], sc_ref[...]).astype(BF16)
            acc_ref[...] = jnp.zeros_like(acc_ref)

        h2 = h2_ref[...]
        gate = jnp.dot(h2, wg_ref[...], preferred_element_type=F32)
        up = jnp.dot(h2, wu_ref[...], preferred_element_type=F32)
        gt_ref[...] = gate.astype(BF16)
        up_ref[...] = up.astype(BF16)
        acc_ref[...] += jnp.dot((_silu(gate) * up).astype(BF16), wo_ref[...], preferred_element_type=F32)

        @pl.when((i == 0) & (f == 0))
        def _():
            loss_ref[...] = jnp.zeros_like(loss_ref)

        @pl.when(f == NFF - 1)
        def _():
            ffn = acc_ref[...]
            err = x_ref[...] + g_ref[...] * ffn - t_ref[...]
            dy = err * (1.0 / D)
            dy_ref[...] = dy
            part = jnp.sum(dy * ffn, axis=0, keepdims=True)
            loss_ref[...] += jnp.sum(jnp.sum(err * err, axis=1, keepdims=True), axis=0, keepdims=True) * (0.5 / D)

            @pl.when(i % tpe == 0)
            def _():
                dg_ref[...] = part

            @pl.when(i % tpe != 0)
            def _():
                dg_ref[...] += part

    row, ex, one, wg, wu, wo, ff = _ffn_specs(ft)
    return pl.pallas_call(
        body, name="ffn_fwd", grid=(NTOK // ft, NFF),
        out_shape=[jax.ShapeDtypeStruct((NTOK, D), BF16), jax.ShapeDtypeStruct((NTOK, DFF), BF16),
                   jax.ShapeDtypeStruct((NTOK, DFF), BF16), jax.ShapeDtypeStruct((NTOK, D), F32),
                   jax.ShapeDtypeStruct((NB, 1, D), F32), jax.ShapeDtypeStruct((8, 128), F32)],
        in_specs=[row(D), ex, ex, ex, one, wg, wu, wo, row(D)],
        out_specs=[row(D), ff, ff, row(D), ex, pl.BlockSpec((8, 128), lambda i, f: (0, 0))],
        scratch_shapes=[pltpu.VMEM((ft, D), F32)],
        compiler_params=_cp(vmem=VMEM_BIG),
    )(x1, sh2, sc2, g2, n2w, w_in4, w_in4, w_out, target)


def _rows2d_spec(width):
    return pl.BlockSpec((RT, width), lambda b, j: (b * (SEQ // RT) + jnp.maximum(j - 1, 0), 0))


def _ffn_bwd(dy, gate, up, x1, sh2, sc2, g2, n2w, w_in4, w_out):
    ft, tpe = FT_BWD, SEQ // FT_BWD
    ni = NTOK // ft

    def body(dy_ref, gt_ref, up_ref, x_ref, sh_ref, sc_ref, g_ref, w2_ref, wg_ref, wu_ref, wo_ref,
             dff_ref, act_ref, dgu_ref, dx_ref, dsh_ref, dsc_ref, dw_ref, acc_ref):
        f, i = pl.program_id(0), pl.program_id(1)
        rows = pl.ds(pl.multiple_of(i * ft, ft), ft)
        dff = (g_ref[...] * dy_ref[...]).astype(BF16)
        dff_ref[...] = dff
        dact = lax.dot_general(dff, wo_ref[...], _NT, preferred_element_type=F32)
        gate, up = gt_ref[...].astype(F32), up_ref[...].astype(F32)
        sg = _sigmoid(gate)
        sl = gate * sg
        act_ref[...] = (sl * up).astype(BF16)
        dgate = (dact * up * (sg * (1.0 + gate * (1.0 - sg)))).astype(BF16)
        dup = (dact * sl).astype(BF16)
        dgu_ref[0] = dgate
        dgu_ref[1] = dup
        part = (lax.dot_general(dgate, wg_ref[...], _NT, preferred_element_type=F32)
                + lax.dot_general(dup, wu_ref[...], _NT, preferred_element_type=F32))

        @pl.when(f == 0)
        def _():
            acc_ref[rows, :] = part

        @pl.when(f > 0)
        def _():
            acc_ref[rows, :] += part

        @pl.when((i == 0) & (f == 0))
        def _():
            dw_ref[...] = jnp.zeros_like(dw_ref)

        @pl.when(f == NFF - 1)
        def _():
            _, vjp = jax.vjp(_norm_mod, x_ref[...], w2_ref[...], sh_ref[...], sc_ref[...])
            dx, dw, dsh, dsc = vjp(acc_ref[rows, :])
            dx_ref[...] = dy_ref[...] + dx
            dw_ref[...] += dw

            @pl.when(i % tpe == 0)
            def _():
                dsh_ref[...] = dsh
                dsc_ref[...] = dsc

            @pl.when(i % tpe != 0)
            def _():
                dsh_ref[...] += dsh
                dsc_ref[...] += dsc

    last = lambda f, real, spare: jnp.where(f == NFF - 1, real, spare)
    row = lambda w: pl.BlockSpec((ft, w), lambda f, i: (i, 0))
    ex = pl.BlockSpec((None, 1, D), lambda f, i: (i // tpe, 0, 0))
    one = pl.BlockSpec((1, D), lambda f, i: (0, 0))
    wg = pl.BlockSpec((None, D, FFT), lambda f, i: (f, 0, 0))
    wu = pl.BlockSpec((None, D, FFT), lambda f, i: (NFF + f, 0, 0))
    wo = pl.BlockSpec((FFT, D), lambda f, i: (f, 0))
    ff = pl.BlockSpec((ft, FFT), lambda f, i: (i, f))
    row_once = lambda w: pl.BlockSpec((ft, w), lambda f, i: (last(f, i, ni), 0))
    ex_once = pl.BlockSpec((None, 1, D), lambda f, i: (last(f, i // tpe, NB), 0, 0))
    return pl.pallas_call(
        body, name="ffn_bwd", grid=(NFF, ni),
        out_shape=[jax.ShapeDtypeStruct((NTOK + ft, D), BF16), jax.ShapeDtypeStruct((NTOK, DFF), BF16),
                   jax.ShapeDtypeStruct((2, NTOK, DFF), BF16), jax.ShapeDtypeStruct((NTOK + ft, D), F32),
                   jax.ShapeDtypeStruct((NB + 1, 1, D), F32), jax.ShapeDtypeStruct((NB + 1, 1, D), F32),
                   jax.ShapeDtypeStruct((1, D), F32)],
        in_specs=[row(D), ff, ff, row(D), ex, ex, ex, one, wg, wu, wo],
        out_specs=[row_once(D), ff, pl.BlockSpec((2, ft, FFT), lambda f, i: (0, i, f)), row_once(D), ex_once, ex_once, one],
        scratch_shapes=[pltpu.VMEM((NTOK, D), F32)],
        compiler_params=_cp(vmem=VMEM_BIG),
    )(dy, gate, up, x1, sh2, sc2, g2, n2w, w_in4, w_in4, w_out)


def _local_step(x, ctx, target, mod_x, mod_c, norm1_w, norm2_w, w_in_p, conv_w, a_log, dt_bias, out_norm_w,
                q_norm_w, k_norm_w, rpb, w_out_b, w_ffn_in4, w_ffn_out_b, scatter_early=False, gather_late=None):
    sh1, sc1, g1, sh2, sc2, g2 = [mod_x[:, i * D:(i + 1) * D].reshape(NB, 1, D) for i in range(6)]
    csh1, csc1 = mod_c[0:D].reshape(1, 1, D), mod_c[D:2 * D].reshape(1, 1, D)
    cw = jnp.concatenate([conv_w, jnp.zeros((3, 1536), F32)], axis=0)
    prm = jnp.pad(jnp.stack([a_log.reshape(8), dt_bias.reshape(8)]), ((0, 6), (0, 120)))
    onw = jnp.tile(out_norm_w.reshape(1, 128), (1, DNH))
    qw = jnp.tile(q_norm_w.reshape(1, 64), (1, NAH))
    kw = jnp.tile(k_norm_w.reshape(1, 64), (1, NAH))
    seg = np.arange(NAW) // 64
    bd = jnp.asarray((seg[:, None] == seg[None, :]).astype(np.float32) / 64.0, dtype=BF16)
    bias = _bias_table(rpb)

    p, hm = _inproj_fwd(x, ctx, sh1, sc1, csh1, csc1, norm1_w, w_in_p)
    dq_, dk_, dv_ = _dn_prep_fwd(p, cw)
    gates = _gates_fwd(p, prm)
    res = _scan_fwd(dq_, dk_, dv_, gates, carry=gather_late[:2] if gather_late else ())
    o_f, st_f, inv_f, o_b, st_b, inv_b = res[:6]
    qn, kn, vb = _na_prep_fwd(p, qw, kw, bd)
    na, got = _na_fwd(qn, kn, vb, bias, carry=gather_late[2:] if gather_late else ())
    if gather_late:
        w_out_b = res[6].reshape(D, D)
        w_ffn_in4 = res[7].reshape(4, D, FFT)
        w_ffn_out_b = got[0].reshape(DFF, D)
    x1, ai, ao = _outproj_fwd(o_f, o_b, p, na, onw, x, g1, w_out_b)
    x1f = x1.reshape(NTOK, D)
    h2, gate, up, dy, dg2, loss8 = _ffn_fwd(x1f, sh2, sc2, g2, norm2_w, w_ffn_in4, w_ffn_out_b,
                                             target.reshape(NTOK, D))
    dff, act, dgu, dx1, dsh2, dsc2, dn2w = _ffn_bwd(dy, gate, up, x1f, sh2, sc2, g2, norm2_w, w_ffn_in4, w_ffn_out_b)
    g_w_ffn_out = _matmul_tn(act, dff, FFT, D, 512, "wgrad_ffn_out")
    g_w_ffn_in4 = _matmul_tn(
        h2, dgu, D, FFT, 1024, "wgrad_ffn_in", nn=2 * NFF,
        b_block=(None, 1024, FFT), b_map=lambda i, j, k: (j // NFF, k, j % NFF),
        out_shape=(4, D, FFT), out_block=(None, D, FFT), out_map=lambda i, j, k: (j, i, 0))
    dsh2, dsc2 = dsh2[:NB], dsc2[:NB]
    dao, do, dz, dna, dg1, donw = _outproj_bwd(dx1, ao, g1, w_out_b, o_f, o_b, p, onw)
    g_w_out = _matmul_tn(ai.reshape(NTOK, D), dao.reshape(NTOK, D), D, D, 1024, "wgrad_out")
    dqn, dkn, dvn, dbias = _na_bwd(qn, kn, vb, bias, dna)
    dp_na, dqw, dkw = _na_prep_bwd(p, qw, kw, bd, dqn, dkn, dvn)
    early = [g_w_out.reshape(8, D // 8, D), g_w_ffn_in4.reshape(8, D // 2, FFT), g_w_ffn_out.reshape(8, DFF // 8, D)]
    res = _scan_bwd(dq_, dk_, dv_, gates, st_f, st_b, inv_f, inv_b, do, carry=early if scatter_early else ())
    dqf, dkf, dvf, dgf, dqb, dkb, dvb, dgb = res[:8]
    if scatter_early:
        g_w_out, g_w_ffn_in4, g_w_ffn_out = res[8:]
    dpq, dpk, dpv, gcq, gck, gcv = _dn_prep_bwd(p, cw, (dqf, dqb), (dkf, dkb), (dvf, dvb))
    dgt, dprm = _gates_bwd(p, prm, dgf, dgb)
    dpb, grad_x, dsh1, dsc1, dcsh1, dcsc1, dn1w = _inproj_bwd(
        (dpq, dpk, dpv), dz, dp_na, dgt, w_in_p, x, ctx, sh1, sc1, csh1, csc1, norm1_w, dx1)
    g_w_in_p = _matmul_tn(dpb.reshape(NB * T, PC), hm.reshape(NB * T, D), PC, D, 512, "wgrad_in")
    g_rpb = _rpb_fold(dbias)

    dmod_x = jnp.concatenate([dsh1, dsc1, dg1, dsh2, dsc2, dg2], axis=2).reshape(NB, 6 * D)
    dmod_c = jnp.concatenate([dcsh1.reshape(D), dcsc1.reshape(D), jnp.zeros((4 * D,), F32)])
    g_conv = jnp.concatenate([gcq[:5], gck[:5], gcv[:5]], axis=1)
    small = dict(
        norm1_w=dn1w, norm2_w=dn2w, a_log=dprm[0, :8], dt_bias=dprm[1, :8],
        out_norm_w=donw[0].reshape(DNH, 128).sum(0), q_norm_w=dqw[0].reshape(NAH, 64).sum(0),
        k_norm_w=dkw[0].reshape(NAH, 64).sum(0), rpb=g_rpb, conv_w=g_conv)
    return loss8[0, 0], grad_x, dmod_x, dmod_c, small, g_w_in_p, g_w_out, g_w_ffn_in4, g_w_ffn_out


def _perm_w_in(wt):
    return jnp.concatenate([wt[:2048], wt[2064:3600], wt[2048:2064], jnp.zeros((PC - IN_COLS, wt.shape[1]), wt.dtype)],
                           axis=0)


def _unperm_w_in(gt):
    return jnp.concatenate([gt[:2048], gt[COL_G:COL_G + 16], gt[2048:COL_G]], axis=0)


def _misc_row(a_log, dt_bias, onw, qnw, knw):
    return jnp.concatenate([a_log.reshape(8), dt_bias.reshape(8), jnp.zeros((112,), F32), onw.reshape(128),
                            qnw.reshape(64), knw.reshape(64), jnp.zeros((D - 384,), F32)])


def _pack_small(c_ctx, n1, n2, b_ada, a_log, dt_bias, onw, qnw, knw, rpb, conv_s):
    misc = _misc_row(a_log, dt_bias, onw, qnw, knw)
    rp = jnp.concatenate([rpb.reshape(-1), jnp.zeros((4 * D - 3720,), F32)])
    cv = jnp.concatenate([conv_s.reshape(-1), jnp.zeros((2 * D - 1920,), F32)])
    return jnp.concatenate([c_ctx.reshape(-1), n1.reshape(-1), n2.reshape(-1), b_ada.reshape(-1), misc, rp, cv]).reshape(16, D)


def _unpack_small(a):
    f = a.reshape(-1)
    misc = f[9 * D:10 * D]
    return dict(
        c_ctx=f[0:D], norm1_w=f[D:2 * D].reshape(1, D), norm2_w=f[2 * D:3 * D].reshape(1, D),
        b_ada=f[3 * D:9 * D].reshape(1, 6 * D), dn_A_log=misc[0:8].reshape(1, 2, 4), dn_dt_bias=misc[8:16].reshape(1, 2, 4),
        dn_out_norm_w=misc[128:256].reshape(1, 128), na_q_norm_w=misc[256:320].reshape(1, 64),
        na_k_norm_w=misc[320:384].reshape(1, 64), na_rpb=f[10 * D:10 * D + 3720].reshape(1, 8, 15, 31),
        dn_conv_w=f[14 * D:14 * D + 1920].reshape(1, 5, 384))


_WEIGHTS = ['c_ctx', 'norm1_w', 'norm2_w', 'w_ada', 'b_ada', 'w_in', 'dn_conv_w', 'dn_A_log', 'dn_dt_bias',
            'dn_out_norm_w', 'na_q_norm_w', 'na_k_norm_w', 'na_rpb', 'w_out', 'w_ffn_in', 'w_ffn_out']


def kernel(x, c, ctx, c_ctx, norm1_w, norm2_w, w_ada, b_ada, w_in, dn_conv_w, dn_A_log, dn_dt_bias, dn_out_norm_w, na_q_norm_w, na_k_norm_w, na_rpb, w_out, w_ffn_in, w_ffn_out, loss_target, m_c_ctx, m_norm1_w, m_norm2_w, m_w_ada, m_b_ada, m_w_in, m_dn_conv_w, m_dn_A_log, m_dn_dt_bias, m_dn_out_norm_w, m_na_q_norm_w, m_na_k_norm_w, m_na_rpb, m_w_out, m_w_ffn_in, m_w_ffn_out, v_c_ctx, v_norm1_w, v_norm2_w, v_w_ada, v_b_ada, v_w_in, v_dn_conv_w, v_dn_A_log, v_dn_dt_bias, v_dn_out_norm_w, v_na_q_norm_w, v_na_k_norm_w, v_na_rpb, v_w_out, v_w_ffn_in, v_w_ffn_out):
    ix, iy, ic = lax.axis_index("x"), lax.axis_index("y"), lax.axis_index("c")
    chip = 2 * ix + iy
    me = 4 * ix + 2 * iy + ic

    crow = jnp.concatenate([c, c_ctx.reshape(1, D), jnp.zeros((5, D), F32)], axis=0)
    cvrow = jnp.concatenate([dn_conv_w[0].reshape(-1), jnp.zeros((2 * D - 1920,), F32)]).reshape(2, D)
    g0 = _allgather8(jnp.concatenate([crow, cvrow, jnp.zeros((6, D), F32)], axis=0), "gather_cond")
    cin = g0[:, :8].reshape(64, D)
    conv_full = jnp.concatenate([g0[2 * s, 8:10].reshape(-1)[:1920].reshape(5, 384) for s in range(4)], axis=1)

    w_ada_s = w_ada[0]
    b_s = lax.dynamic_slice(b_ada, (0, chip * 1536), (1, 1536))
    mod_s = _ada_fwd(cin, w_ada_s, b_s)
    mods = _gather_rows(mod_s, "gather_mod")
    mod_mine = jnp.concatenate([mods[0], mods[2], mods[4], mods[6]], axis=1)
    mod_x, mod_c = mod_mine[0:NB], mod_mine[2]

    def half(w2d):
        r = w2d.shape[0] // 2
        return lax.dynamic_slice(w2d, (ic * r, 0), (r, w2d.shape[1]))

    w_in_t = w_in[0].T
    (gw_in,) = _gather_weights([_cast_bf16(lax.dynamic_slice(w_in_t, (0, ic * (D // 2)), (900, D // 2)), "cast_w_in")])
    late = [_cast_bf16(half(w_out[0]), "cast_w_out"), _cast_bf16(half(w_ffn_in[0]), "cast_w_ffn_in"),
            _cast_bf16(half(w_ffn_out[0]), "cast_w_ffn_out")]
    w_in_p = _perm_w_in(gw_in.reshape(4, 2, 900, D // 2).transpose(0, 2, 1, 3).reshape(IN_COLS, D))

    (loss_part, grad_x, dmod_x, dmod_c, small, g_w_in_p, r_w_out, r_w_ffn_in, r_w_ffn_out) = _local_step(
        x, ctx, loss_target, mod_x, mod_c, norm1_w, norm2_w, w_in_p, conv_full, dn_A_log[0], dn_dt_bias[0],
        dn_out_norm_w[0], na_q_norm_w[0], na_k_norm_w[0], na_rpb[0], None, None, None,
        scatter_early=True, gather_late=late)
    loss = lax.psum(loss_part, ("x", "y", "c"))

    dm = jnp.concatenate([dmod_x, dmod_c.reshape(1, 6 * D), jnp.zeros((5, 6 * D), F32)], axis=0).reshape(48, D)
    misc = _misc_row(small["a_log"], small["dt_bias"], small["out_norm_w"], small["q_norm_w"], small["k_norm_w"])
    sm = jnp.concatenate([
        small["norm1_w"].reshape(-1), small["norm2_w"].reshape(-1), misc,
        small["rpb"].reshape(-1), jnp.zeros((4 * D - 3720,), F32),
        small["conv_w"].reshape(-1), jnp.zeros((8 * D - 7680,), F32), jnp.zeros((D,), F32)]).reshape(16, D)
    gsl, gtot = _allgather8(jnp.concatenate([dm, sm], axis=0), "gather_small", with_sum=True)
    gmod = gsl[:, :48].reshape(64, 6 * D)
    st = gtot[48:64].reshape(-1)
    g_n1, g_n2, g_misc = st[0:D].reshape(1, D), st[D:2 * D].reshape(1, D), st[2 * D:3 * D]
    g_rpb = st[3 * D:3 * D + 3720].reshape(1, 8, 15, 31)
    g_conv_full = st[7 * D:7 * D + 7680].reshape(5, 1536)
    g_conv_s = lax.dynamic_slice(g_conv_full, (0, chip * 384), (5, 384)).reshape(1, 5, 384)

    gmod_s = lax.dynamic_slice(gmod, (0, chip * 1536), (64, 1536))
    g_w_ada_s, pctx = _ada_bwd(cin, gmod_s, w_ada_s)
    parts = _allgather8(pctx, "gather_cctx")
    g_b_ada, g_c_ctx = _ada_small(gmod, parts, c_ctx.reshape(1, D))

    g_w_in8 = _unperm_w_in(g_w_in_p).reshape(4, 900, 2, D // 2).transpose(0, 2, 1, 3).reshape(8, 900, D // 2)
    names = ["w_in", "w_out", "w_ffn_in", "w_ffn_out"]
    early = _sibling_exchange([_sum_slots(r, "sum_" + nm)
                               for r, nm in zip([r_w_out, r_w_ffn_in, r_w_ffn_out], names[1:])], "sibling_early")
    g_w_out_s = early[0].reshape(1, D // 4, D)
    g_w_ffn_in_s = early[1].reshape(1, D, FFT)
    g_w_ffn_out_s = early[2].reshape(1, DFF // 4, D)
    two = lambda a: a.reshape(a.shape[1], a.shape[2])
    hosted_names = ["w_ada", "w_out", "w_ffn_in"]
    hosted_g = dict(w_ada=g_w_ada_s, w_out=two(g_w_out_s), w_ffn_in=two(g_w_ffn_in_s))
    hosted_w = dict(w_ada=(w_ada, m_w_ada, v_w_ada), w_out=(w_out, m_w_out, v_w_out), w_ffn_in=(w_ffn_in, m_w_ffn_in, v_w_ffn_in))
    hosted_res, (r_w_in,) = _adamw_hosting(
        [(two(hosted_w[k][0]), hosted_g[k], two(hosted_w[k][1]), two(hosted_w[k][2])) for k in hosted_names], [g_w_in8])
    (g_w_in_half,) = _sibling_exchange([_sum_slots(r_w_in, "sum_w_in")], "sibling_w_in")
    g_w_in_t = g_w_in_half.transpose(1, 0, 2).reshape(900, D)

    grads = dict(
        c_ctx=g_c_ctx.reshape(D), norm1_w=g_n1, norm2_w=g_n2, w_ada=g_w_ada_s.reshape(1, D, 1536), b_ada=g_b_ada,
        w_in=g_w_in_t, dn_conv_w=g_conv_s, dn_A_log=g_misc[0:8].reshape(1, 2, 4), dn_dt_bias=g_misc[8:16].reshape(1, 2, 4),
        dn_out_norm_w=g_misc[128:256].reshape(1, 128), na_q_norm_w=g_misc[256:320].reshape(1, 64),
        na_k_norm_w=g_misc[320:384].reshape(1, 64), na_rpb=g_rpb, w_out=g_w_out_s, w_ffn_in=g_w_ffn_in_s,
        w_ffn_out=g_w_ffn_out_s)
    pk = lambda d_: _pack_small(d_["c_ctx"], d_["norm1_w"], d_["norm2_w"], d_["b_ada"], d_["dn_A_log"], d_["dn_dt_bias"],
                                d_["dn_out_norm_w"], d_["na_q_norm_w"], d_["na_k_norm_w"], d_["na_rpb"], d_["dn_conv_w"])
    wts = dict(c_ctx=c_ctx, norm1_w=norm1_w, norm2_w=norm2_w, b_ada=b_ada, dn_A_log=dn_A_log, dn_dt_bias=dn_dt_bias,
               dn_out_norm_w=dn_out_norm_w, na_q_norm_w=na_q_norm_w, na_k_norm_w=na_k_norm_w, na_rpb=na_rpb, dn_conv_w=dn_conv_w)
    ms = dict(c_ctx=m_c_ctx, norm1_w=m_norm1_w, norm2_w=m_norm2_w, b_ada=m_b_ada, dn_A_log=m_dn_A_log, dn_dt_bias=m_dn_dt_bias,
              dn_out_norm_w=m_dn_out_norm_w, na_q_norm_w=m_na_q_norm_w, na_k_norm_w=m_na_k_norm_w, na_rpb=m_na_rpb,
              dn_conv_w=m_dn_conv_w)
    vs = dict(c_ctx=v_c_ctx, norm1_w=v_norm1_w, norm2_w=v_norm2_w, b_ada=v_b_ada, dn_A_log=v_dn_A_log, dn_dt_bias=v_dn_dt_bias,
              dn_out_norm_w=v_dn_out_norm_w, na_q_norm_w=v_na_q_norm_w, na_k_norm_w=v_na_k_norm_w, na_rpb=v_na_rpb,
              dn_conv_w=v_dn_conv_w)
    sd, snm, snv, _ = _adamw(pk(wts), pk(grads), pk(ms), pk(vs), "adamw_small")
    delta, new_m, new_v = _unpack_small(sd), _unpack_small(snm), _unpack_small(snv)
    for k, res in zip(hosted_names, hosted_res):
        shp = hosted_w[k][0].shape
        delta[k], new_m[k], new_v[k], grads[k] = [a.reshape(shp) for a in res]
    big = dict(w_in=(w_in, m_w_in, v_w_in), w_ffn_out=(w_ffn_out, m_w_ffn_out, v_w_ffn_out))
    for nm, (w_, m_, v_) in big.items():
        shp = w_.shape
        if nm == "w_in":
            r2, back = (lambda a: a[0].T), (lambda a: a.T.reshape(shp))
            res = _adamw(w_in_t, grads[nm], r2(m_), r2(v_), "adamw_" + nm)
        else:
            r2, back = (lambda a: a.reshape(shp[1], shp[2])), (lambda a: a.reshape(shp))
            res = _adamw(r2(w_), r2(grads[nm]), r2(m_), r2(v_), "adamw_" + nm)
        delta[nm], new_m[nm], new_v[nm], grads[nm] = [back(a) for a in res]

    return (loss, grad_x, *[grads[n] for n in _WEIGHTS], *[delta[n] for n in _WEIGHTS],
            *[new_m[n] for n in _WEIGHTS], *[new_v[n] for n in _WEIGHTS])
```

```python
import functools

import numpy as np
import jax
import jax.numpy as jnp
from jax import lax
from jax.experimental import pallas as pl
from jax.experimental.pallas import tpu as pltpu

F32 = jnp.float32
BF16 = jnp.bfloat16

D = 1024
NB = 2
SEQ = 2048
CT = 256
T = CT + SEQ
NTOK = NB * SEQ
CH = 64
NCH = T // CH
NCH_CTX = CT // CH
DNH, DNW = 4, 512
NAH, NAW = 8, 512
GRID_W = 64
ROWS = SEQ // GRID_W
WIN_ROWS = 8
WIN_COLS = 16
NWIN = WIN_ROWS * GRID_W
DFF = 2816
FFT = 1408
IN_COLS = 3600
PC = 3712
COL_Z, COL_NA, COL_G = 1536, 2048, 3584
EPS = 1e-6
NEG = -1e30
RT = 256
VMEM_BIG = 56 * 1024 * 1024
MESH = pl.DeviceIdType.MESH

ADAM_LR, ADAM_B1, ADAM_B2, ADAM_EPS, ADAM_WD, ADAM_STEP = 0.001, 0.9, 0.999, 1e-08, 0.01, 10


def _cp(vmem=None, sem=None):
    kw = {}
    if vmem is not None:
        kw["vmem_limit_bytes"] = vmem
    if sem is not None:
        kw["dimension_semantics"] = sem
    return pltpu.CompilerParams(**kw)


_NN = (((1,), (0,)), ((), ()))
_NT = (((1,), (1,)), ((), ()))
_TN = (((0,), (0,)), ((), ()))


def _dg(a, b, dims):
    return lax.dot_general(a.astype(BF16), b.astype(BF16), dims, preferred_element_type=F32)


@jax.custom_vjp
def _mm(a, b):
    return _dg(a, b, _NN)


_mm.defvjp(lambda a, b: (_dg(a, b, _NN), (a, b)),
           lambda r, g: (_dg(g, r[1], _NT), _dg(r[0], g, _TN)))


@jax.custom_vjp
def _mm_nt(a, b):
    return _dg(a, b, _NT)


_mm_nt.defvjp(lambda a, b: (_dg(a, b, _NT), (a, b)),
              lambda r, g: (_dg(g, r[1], _NN), _dg(g, r[0], _TN)))


@jax.custom_vjp
def _mm_tn(a, b):
    return _dg(a, b, _TN)


_mm_tn.defvjp(lambda a, b: (_dg(a, b, _TN), (a, b)),
              lambda r, g: (_dg(r[1], g, _NT), _dg(r[0], g, _NN)))


def _split_dot(a, b, dims):
    hi = b.astype(BF16)
    lo = (b - hi.astype(F32)).astype(BF16)
    ab = a.astype(BF16)
    return (lax.dot_general(ab, hi, dims, preferred_element_type=F32)
            + lax.dot_general(ab, lo, dims, preferred_element_type=F32))


@jax.custom_vjp
def _mask_mm(mask, b):
    return _split_dot(mask, b, _NN)


_mask_mm.defvjp(lambda mask, b: (_split_dot(mask, b, _NN), mask),
                lambda mask, g: (jnp.zeros_like(mask), _split_dot(mask, g, _TN)))


def _hmm(a, b):
    return jnp.dot(a, b, precision=lax.Precision.HIGHEST, preferred_element_type=F32)


def _sigmoid(x):
    return 1.0 / (1.0 + jnp.exp(-x))


def _silu(x):
    return x * _sigmoid(x)


def _norm_mod(x, w, sh, sc):
    r = lax.rsqrt(jnp.mean(x * x, axis=-1, keepdims=True) + EPS)
    return (x * r * w) * (1.0 + sc) + sh


def _my_pos():
    return lax.axis_index("x"), lax.axis_index("y"), lax.axis_index("c")


def _flip(pos, k):
    x, y, c = pos
    return (1 - x if k & 4 else x, 1 - y if k & 2 else y, 1 - c if k & 1 else c)


def _slot(pos):
    return 4 * pos[0] + 2 * pos[1] + pos[2]


def _allgather8(v, name, with_sum=False):
    R, W = v.shape

    def body(v_ref, out_ref, *rest):
        if with_sum:
            tot_ref, send_sems, recv_sems, lsem = rest
        else:
            send_sems, recv_sems, lsem = rest
        me = _my_pos()
        mine = pltpu.make_async_copy(v_ref, out_ref.at[_slot(me)], lsem)
        mine.start()
        sends = []
        for k in range(1, 8):
            cp = pltpu.make_async_remote_copy(
                src_ref=v_ref, dst_ref=out_ref.at[_slot(me)], send_sem=send_sems.at[k - 1],
                recv_sem=recv_sems.at[k - 1], device_id=_flip(me, k), device_id_type=MESH)
            cp.start()
            sends.append(cp)
        for k in range(1, 8):
            peer = _flip(me, k)
            pltpu.make_async_remote_copy(
                src_ref=v_ref, dst_ref=out_ref.at[_slot(peer)], send_sem=send_sems.at[k - 1],
                recv_sem=recv_sems.at[k - 1], device_id=peer, device_id_type=MESH).wait_recv()
        for cp in sends:
            cp.wait_send()
        mine.wait()
        if with_sum:
            acc = out_ref[0]
            for s in range(1, 8):
                acc = acc + out_ref[s]
            tot_ref[...] = acc

    out_shape = [jax.ShapeDtypeStruct((8, R, W), F32)]
    out_specs = [pl.BlockSpec(memory_space=pltpu.VMEM)]
    if with_sum:
        out_shape.append(jax.ShapeDtypeStruct((R, W), F32))
        out_specs.append(pl.BlockSpec(memory_space=pltpu.VMEM))
    res = pl.pallas_call(
        body, name=name, out_shape=out_shape,
        in_specs=[pl.BlockSpec(memory_space=pltpu.VMEM)], out_specs=out_specs,
        scratch_shapes=[pltpu.SemaphoreType.DMA((7,)), pltpu.SemaphoreType.DMA((7,)), pltpu.SemaphoreType.DMA],
        compiler_params=_cp(vmem=VMEM_BIG),
    )(v)
    return res if with_sum else res[0]


def _gather_rows(v, name):
    W = v.shape[1]

    def body(v_ref, out_ref, send_sems, recv_sems, lsem):
        pos = _my_pos()
        me = _slot(pos)
        rows = lambda p: v_ref.at[pl.ds(pl.multiple_of(8 * p, 8), 8)]
        locals_ = [(rows(me), out_ref.at[me])]
        xfers = [(rows(_slot(_flip(pos, k))), out_ref.at[me], out_ref.at[_slot(_flip(pos, k))], _flip(pos, k))
                 for k in range(1, 8)]
        _xfer_start((locals_, xfers), (send_sems, recv_sems, lsem))
        _xfer_wait((locals_, xfers), (send_sems, recv_sems, lsem))

    vm = pl.BlockSpec(memory_space=pltpu.VMEM)
    return pl.pallas_call(
        body, name=name, out_shape=jax.ShapeDtypeStruct((8, 8, W), F32), in_specs=[vm], out_specs=vm,
        scratch_shapes=_xfer_sems(1, 7),
    )(v)


def _gather_weights(halves):
    n = len(halves)

    def body(*refs):
        ins, outs = refs[:n], refs[n:2 * n]
        send_sems, recv_sems, lsems = refs[2 * n:]
        me = _my_pos()
        x, y, c = me
        sib = (x, y, 1 - c)
        chips = [4, 2, 6]

        def cp(a, k, block, to, src=None):
            dst = outs[a].at[_slot(block)]
            return pltpu.make_async_remote_copy(
                src_ref=dst if src is None else src, dst_ref=dst, send_sem=send_sems.at[a, k],
                recv_sem=recv_sems.at[a, k], device_id=to, device_id_type=MESH)

        mine, first, passed = [], [], []
        for a in range(n):
            m = pltpu.make_async_copy(ins[a], outs[a].at[_slot(me)], lsems.at[a])
            m.start()
            mine.append(m)
            first.append(cp(a, 0, me, sib, src=ins[a]))
            for j, k in enumerate(chips):
                first.append(cp(a, 1 + j, me, _flip(me, k), src=ins[a]))
        for f in first:
            f.start()
        for j, k in enumerate(chips):
            for a in range(n):
                cp(a, 1 + j, _flip(me, k), me).wait_recv()
                p = cp(a, 4 + j, _flip(me, k), sib)
                p.start()
                passed.append(p)
        for a in range(n):
            cp(a, 0, sib, me).wait_recv()
            for j, k in enumerate(chips):
                cp(a, 4 + j, _flip(sib, k), me).wait_recv()
        for f in first + passed:
            f.wait_send()
        for m in mine:
            m.wait()

    vm = pl.BlockSpec(memory_space=pltpu.VMEM)
    return pl.pallas_call(
        body, name="gather_weights",
        out_shape=[jax.ShapeDtypeStruct((8,) + h.shape, h.dtype) for h in halves],
        in_specs=[vm] * n, out_specs=[vm] * n,
        scratch_shapes=[pltpu.SemaphoreType.DMA((n, 7)), pltpu.SemaphoreType.DMA((n, 7)),
                        pltpu.SemaphoreType.DMA((n,))],
        compiler_params=_cp(vmem=VMEM_BIG),
    )(*halves)


def _xfer_start(plan, sems):
    send_sems, recv_sems, lsems = sems
    locals_, xfers = plan
    for i, (src, dst) in enumerate(locals_):
        pltpu.make_async_copy(src, dst, lsems.at[i]).start()
    for i, (src, dst, _, peer) in enumerate(xfers):
        pltpu.make_async_remote_copy(src_ref=src, dst_ref=dst, send_sem=send_sems.at[i], recv_sem=recv_sems.at[i],
                                     device_id=peer, device_id_type=MESH).start()


def _xfer_wait(plan, sems):
    send_sems, recv_sems, lsems = sems
    locals_, xfers = plan
    for i, (src, _, landing, peer) in enumerate(xfers):
        pltpu.make_async_remote_copy(src_ref=src, dst_ref=landing, send_sem=send_sems.at[i], recv_sem=recv_sems.at[i],
                                     device_id=peer, device_id_type=MESH).wait_recv()
    for i, (src, dst, _, peer) in enumerate(xfers):
        pltpu.make_async_remote_copy(src_ref=src, dst_ref=dst, send_sem=send_sems.at[i], recv_sem=recv_sems.at[i],
                                     device_id=peer, device_id_type=MESH).wait_send()
    for i, (src, dst) in enumerate(locals_):
        pltpu.make_async_copy(src, dst, lsems.at[i]).wait()


def _xfer_sems(n_local, n_xfer):
    return [pltpu.SemaphoreType.DMA((n_xfer,)), pltpu.SemaphoreType.DMA((n_xfer,)), pltpu.SemaphoreType.DMA((n_local,))]


def _scatter_plan(ins, outs, pos):
    me = _slot(pos)
    locals_, xfers = [], []
    for src, dst in zip(ins, outs):
        locals_.append((src.at[me], dst.at[me]))
        for k in range(1, 8):
            peer = _flip(pos, k)
            xfers.append((src.at[_slot(peer)], dst.at[me], dst.at[_slot(peer)], peer))
    return locals_, xfers


def _sibling_exchange(halves, name):
    n = len(halves)

    def body(*refs):
        ins, outs = refs[:n], refs[n:2 * n]
        send_sems, recv_sems, lsems = refs[2 * n:]
        x, y, c = _my_pos()
        sib = (x, y, 1 - c)
        mine, sends = [], []
        for a in range(n):
            m = pltpu.make_async_copy(ins[a], outs[a].at[c], lsems.at[a])
            m.start()
            mine.append(m)
            cp = pltpu.make_async_remote_copy(
                src_ref=ins[a], dst_ref=outs[a].at[c], send_sem=send_sems.at[a], recv_sem=recv_sems.at[a],
                device_id=sib, device_id_type=MESH)
            cp.start()
            sends.append(cp)
        for a in range(n):
            pltpu.make_async_remote_copy(
                src_ref=ins[a], dst_ref=outs[a].at[1 - c], send_sem=send_sems.at[a], recv_sem=recv_sems.at[a],
                device_id=sib, device_id_type=MESH).wait_recv()
        for cp in sends:
            cp.wait_send()
        for m in mine:
            m.wait()

    vm = pl.BlockSpec(memory_space=pltpu.VMEM)
    return pl.pallas_call(
        body, name=name,
        out_shape=[jax.ShapeDtypeStruct((2,) + h.shape, h.dtype) for h in halves],
        in_specs=[vm] * n, out_specs=[vm] * n,
        scratch_shapes=[pltpu.SemaphoreType.DMA((n,)), pltpu.SemaphoreType.DMA((n,)), pltpu.SemaphoreType.DMA((n,))],
        compiler_params=_cp(vmem=VMEM_BIG),
    )(*halves)


def _row_tile(r, cap=512):
    for t in (cap, 256, 128, 64, 32, 16, 8):
        if t <= cap and r % t == 0:
            return t
    return r


def _cast_bf16(a, name):
    R, W = a.shape
    tr = _row_tile(R, 256)

    def body(a_ref, o_ref):
        o_ref[...] = a_ref[...].astype(BF16)

    return pl.pallas_call(
        body, name=name, grid=(R // tr,), out_shape=jax.ShapeDtypeStruct((R, W), BF16),
        in_specs=[pl.BlockSpec((tr, W), lambda i: (i, 0))], out_specs=pl.BlockSpec((tr, W), lambda i: (i, 0)),
    )(a)


def _sum_slots(r, name):
    _, R, W = r.shape
    tr = _row_tile(R, 128)

    def body(r_ref, o_ref):
        acc = r_ref[0].astype(F32)
        for s_ in range(1, 8):
            acc = acc + r_ref[s_].astype(F32)
        o_ref[...] = acc.astype(BF16)

    return pl.pallas_call(
        body, name=name, grid=(R // tr,), out_shape=jax.ShapeDtypeStruct((R, W), BF16),
        in_specs=[pl.BlockSpec((8, tr, W), lambda i: (0, i, 0))], out_specs=pl.BlockSpec((tr, W), lambda i: (i, 0)),
        compiler_params=_cp(vmem=VMEM_BIG),
    )(r)


def _adamw(w, g, m, v, name):
    R, W = w.shape
    tr = _row_tile(R, 256)
    tc = 256 if (tr == R and R * W * 4 > (1 << 21) and W % 256 == 0) else W
    c1 = 1.0 / (1.0 - ADAM_B1 ** ADAM_STEP)
    c2 = 1.0 / (1.0 - ADAM_B2 ** ADAM_STEP)

    def body(w_ref, g_ref, m_ref, v_ref, d_ref, nm_ref, nv_ref, g32_ref):
        gg = g_ref[...].astype(F32)
        g32_ref[...] = gg
        nm = ADAM_B1 * m_ref[...] + (1.0 - ADAM_B1) * gg
        nv = ADAM_B2 * v_ref[...] + (1.0 - ADAM_B2) * (gg * gg)
        d_ref[...] = -ADAM_LR * ((nm * c1) / (jnp.sqrt(nv * c2) + ADAM_EPS) + ADAM_WD * w_ref[...])
        nm_ref[...] = nm
        nv_ref[...] = nv

    spec = pl.BlockSpec((tr, tc), lambda i, j: (i, j))
    return pl.pallas_call(
        body, name=name, grid=(R // tr, W // tc), out_shape=[jax.ShapeDtypeStruct((R, W), F32)] * 4,
        in_specs=[spec] * 4, out_specs=[spec] * 4, compiler_params=_cp(vmem=VMEM_BIG),
    )(w, g, m, v)


def _adamw_hosting(items, carry):
    n, nc, steps = len(items), len(carry), 8
    c1 = 1.0 / (1.0 - ADAM_B1 ** ADAM_STEP)
    c2 = 1.0 / (1.0 - ADAM_B2 ** ADAM_STEP)

    def body(*refs):
        ins, c_in = refs[:4 * n], refs[4 * n:4 * n + nc]
        outs, c_out = refs[4 * n + nc:8 * n + nc], refs[8 * n + nc:8 * n + 2 * nc]
        sems = refs[8 * n + 2 * nc:]
        if nc:
            @pl.when(pl.program_id(0) == 0)
            def _():
                _xfer_start(_scatter_plan(c_in, c_out, _my_pos()), sems)

        for a in range(n):
            w_ref, g_ref, m_ref, v_ref = ins[4 * a:4 * a + 4]
            d_ref, nm_ref, nv_ref, g32_ref = outs[4 * a:4 * a + 4]
            gg = g_ref[...].astype(F32)
            g32_ref[...] = gg
            nm = ADAM_B1 * m_ref[...] + (1.0 - ADAM_B1) * gg
            nv = ADAM_B2 * v_ref[...] + (1.0 - ADAM_B2) * (gg * gg)
            d_ref[...] = -ADAM_LR * ((nm * c1) / (jnp.sqrt(nv * c2) + ADAM_EPS) + ADAM_WD * w_ref[...])
            nm_ref[...] = nm
            nv_ref[...] = nv

        if nc:
            @pl.when(pl.program_id(0) == steps - 1)
            def _():
                _xfer_wait(_scatter_plan(c_in, c_out, _my_pos()), sems)

    specs = [pl.BlockSpec((w.shape[0] // steps, w.shape[1]), lambda i: (i, 0)) for w, _, _, _ in items]
    hbm = pl.BlockSpec(memory_space=pltpu.HBM)
    res = pl.pallas_call(
        body, name="adamw_hosting", grid=(steps,),
        out_shape=[jax.ShapeDtypeStruct(w.shape, F32) for w, _, _, _ in items for _ in range(4)]
        + [jax.ShapeDtypeStruct(g.shape, g.dtype) for g in carry],
        in_specs=[sp for sp in specs for _ in range(4)] + [hbm] * nc,
        out_specs=[sp for sp in specs for _ in range(4)] + [hbm] * nc,
        scratch_shapes=_xfer_sems(nc, 7 * nc) if nc else [],
        compiler_params=_cp(vmem=VMEM_BIG),
    )(*[a for it in items for a in it], *carry)
    return [res[4 * a:4 * a + 4] for a in range(n)], list(res[4 * n:])


def _matmul_tn(a, b, tm, tn, tk, name, nn=1, b_map=None, out_shape=None, out_block=None, out_map=None,
               b_block=None):
    K, M = a.shape
    nk = K // tk
    nm = M // tm

    def body(a_ref, b_ref, o_ref, acc_ref):
        kk = pl.program_id(2)

        @pl.when(kk == 0)
        def _():
            acc_ref[...] = jnp.zeros_like(acc_ref)

        acc_ref[...] += lax.dot_general(a_ref[...], b_ref[...], _TN, preferred_element_type=F32)

        @pl.when(kk == nk - 1)
        def _():
            o_ref[...] = acc_ref[...].astype(BF16)

    return pl.pallas_call(
        body, name=name, grid=(nm, nn, nk),
        out_shape=jax.ShapeDtypeStruct(out_shape or (M, b.shape[-1]), BF16),
        in_specs=[pl.BlockSpec((tk, tm), lambda i, j, k: (k, i)),
                  pl.BlockSpec(b_block or (tk, tn), b_map or (lambda i, j, k: (k, j)))],
        out_specs=pl.BlockSpec(out_block or (tm, tn), out_map or (lambda i, j, k: (i, j))),
        scratch_shapes=[pltpu.VMEM((tm, tn), F32)],
        compiler_params=_cp(vmem=VMEM_BIG),
    )(a, b)


def _ada_fwd(cin, w_ada_s, b_s):
    tn = 512

    def body(c_ref, w_ref, b_ref, o_ref):
        o_ref[...] = _dg(_silu(c_ref[...]), w_ref[...], _NN) + b_ref[...]

    return pl.pallas_call(
        body, name="ada_fwd", grid=(1536 // tn,), out_shape=jax.ShapeDtypeStruct((64, 1536), F32),
        in_specs=[pl.BlockSpec((64, D), lambda j: (0, 0)), pl.BlockSpec((D, tn), lambda j: (0, j)),
                  pl.BlockSpec((1, tn), lambda j: (0, j))],
        out_specs=pl.BlockSpec((64, tn), lambda j: (0, j)),
        compiler_params=_cp(vmem=VMEM_BIG),
    )(cin, w_ada_s, b_s)


def _ada_bwd(cin, gmod_s, w_ada_s):
    tn = 512

    def body(c_ref, g_ref, w_ref, gw_ref, pc_ref):
        j = pl.program_id(0)
        g = g_ref[...]
        gw_ref[...] = _dg(_silu(c_ref[...]), g, _TN)
        rr = lax.broadcasted_iota(jnp.int32, (8, 64), 0)
        cc = lax.broadcasted_iota(jnp.int32, (8, 64), 1)
        sel = jnp.where((rr == 0) & ((cc & 7) == 2), 1.0, 0.0).astype(F32)
        dctx = _hmm(sel, g)
        part = _dg(dctx, w_ref[...], _NT)

        @pl.when(j == 0)
        def _():
            pc_ref[...] = jnp.zeros_like(pc_ref)

        pc_ref[...] += part

    return pl.pallas_call(
        body, name="ada_bwd", grid=(1536 // tn,),
        out_shape=[jax.ShapeDtypeStruct((D, 1536), F32), jax.ShapeDtypeStruct((8, D), F32)],
        in_specs=[pl.BlockSpec((64, D), lambda j: (0, 0)), pl.BlockSpec((64, tn), lambda j: (0, j)),
                  pl.BlockSpec((D, tn), lambda j: (0, j))],
        out_specs=[pl.BlockSpec((D, tn), lambda j: (0, j)), pl.BlockSpec((8, D), lambda j: (0, 0))],
        compiler_params=_cp(vmem=VMEM_BIG),
    )(cin, gmod_s, w_ada_s)


def _ada_small(gmod, parts, c_ctx):
    def body(g_ref, p_ref, c_ref, gb_ref, gc_ref):
        gb_ref[...] = jnp.sum(g_ref[...], axis=0, keepdims=True)
        ds = p_ref[0][0:1] + p_ref[2][0:1] + p_ref[4][0:1] + p_ref[6][0:1]
        cc = c_ref[...]
        s = _sigmoid(cc)
        gc_ref[...] = ds * (s * (1.0 + cc * (1.0 - s)))

    vm = pl.BlockSpec(memory_space=pltpu.VMEM)
    return pl.pallas_call(
        body, name="ada_small",
        out_shape=[jax.ShapeDtypeStruct((1, 6 * D), F32), jax.ShapeDtypeStruct((1, D), F32)],
        in_specs=[vm, vm, vm], out_specs=[vm, vm],
    )(gmod, parts, c_ctx)


def _tok_specs():
    xs = pl.BlockSpec((None, RT, D), lambda b, j: (b, jnp.maximum(j - 1, 0), 0))
    cs = pl.BlockSpec((None, RT, D), lambda b, j: (b, 0, 0))
    ms = pl.BlockSpec((None, 1, D), lambda b, j: (b, 0, 0))
    cms = pl.BlockSpec((None, 1, D), lambda b, j: (0, 0, 0))
    return xs, cs, ms, cms


def _inproj_fwd(x, ctx, sh1, sc1, csh1, csc1, n1w, w_in_p):
    def body(x_ref, c_ref, sh_ref, sc_ref, csh_ref, csc_ref, w1_ref, w_ref, p_ref, hm_ref):
        is_ctx = pl.program_id(1) == 0
        xt = jnp.where(is_ctx, c_ref[...], x_ref[...])
        sh = jnp.where(is_ctx, csh_ref[...], sh_ref[...])
        sc = jnp.where(is_ctx, csc_ref[...], sc_ref[...])
        hb = _norm_mod(xt, w1_ref[...], sh, sc).astype(BF16)
        hm_ref[...] = hb
        p_ref[...] = lax.dot_general(hb, w_ref[...], _NT, preferred_element_type=F32)

    xs, cs, ms, cms = _tok_specs()
    return pl.pallas_call(
        body, name="inproj_fwd", grid=(NB, T // RT),
        out_shape=[jax.ShapeDtypeStruct((NB, T, PC), F32), jax.ShapeDtypeStruct((NB, T, D), BF16)],
        in_specs=[xs, cs, ms, ms, cms, cms, pl.BlockSpec((1, D), lambda b, j: (0, 0)),
                  pl.BlockSpec((PC, D), lambda b, j: (0, 0))],
        out_specs=[pl.BlockSpec((None, RT, PC), lambda b, j: (b, j, 0)),
                   pl.BlockSpec((None, RT, D), lambda b, j: (b, j, 0))],
        compiler_params=_cp(vmem=VMEM_BIG),
    )(x, ctx, sh1, sc1, csh1, csc1, n1w, w_in_p)


def _inproj_bwd(dp_dn, dz, dp_na, dgt, w_in_p, x, ctx, sh1, sc1, csh1, csc1, n1w, dx1):
    def body(d1q_ref, d1k_ref, d1v_ref, d2_ref, d3_ref, d4_ref, w_ref, x_ref, c_ref, sh_ref, sc_ref, csh_ref, csc_ref,
             w1_ref, dx1_ref, dpb_ref, gx_ref, dsh_ref, dsc_ref, dcsh_ref, dcsc_ref, dw1_ref):
        b, j = pl.program_id(0), pl.program_id(1)
        is_ctx = j == 0
        dp = jnp.concatenate([d1q_ref[...], d1k_ref[...], d1v_ref[...], d2_ref[...], d3_ref[...], d4_ref[...]],
                             axis=1).astype(BF16)
        dpb_ref[...] = dp
        dhm = jnp.dot(dp, w_ref[...], preferred_element_type=F32)
        xt = jnp.where(is_ctx, c_ref[...], x_ref[...])
        sh = jnp.where(is_ctx, csh_ref[...], sh_ref[...])
        sc = jnp.where(is_ctx, csc_ref[...], sc_ref[...])
        _, vjp = jax.vjp(_norm_mod, xt, w1_ref[...], sh, sc)
        dxt, dw1, dsh, dsc = vjp(dhm)

        @pl.when((b == 0) & (j == 0))
        def _():
            dw1_ref[...] = jnp.zeros_like(dw1_ref)
            dcsh_ref[...] = jnp.zeros_like(dcsh_ref)
            dcsc_ref[...] = jnp.zeros_like(dcsc_ref)

        dw1_ref[...] += dw1

        @pl.when(is_ctx)
        def _():
            dcsh_ref[...] += dsh
            dcsc_ref[...] += dsc

        @pl.when(j == 1)
        def _():
            dsh_ref[...] = jnp.zeros_like(dsh_ref)
            dsc_ref[...] = jnp.zeros_like(dsc_ref)

        @pl.when(j >= 1)
        def _():
            dsh_ref[...] += dsh
            dsc_ref[...] += dsc
            gx_ref[...] = dx1_ref[...] + dxt

    xs, cs, ms, cms = _tok_specs()
    tsp = lambda w: pl.BlockSpec((None, RT, w), lambda b, j: (b, j, 0))
    one = pl.BlockSpec((1, D), lambda b, j: (0, 0))
    return pl.pallas_call(
        body, name="inproj_bwd", grid=(NB, T // RT),
        out_shape=[jax.ShapeDtypeStruct((NB, T, PC), BF16), jax.ShapeDtypeStruct((NB, SEQ, D), F32),
                   jax.ShapeDtypeStruct((NB, 1, D), F32), jax.ShapeDtypeStruct((NB, 1, D), F32),
                   jax.ShapeDtypeStruct((1, 1, D), F32), jax.ShapeDtypeStruct((1, 1, D), F32),
                   jax.ShapeDtypeStruct((1, D), F32)],
        in_specs=[tsp(512), tsp(512), tsp(512), tsp(512), tsp(1536), tsp(128), pl.BlockSpec((PC, D), lambda b, j: (0, 0)),
                  xs, cs, ms, ms, cms, cms, one, _rows2d_spec(D)],
        out_specs=[tsp(PC), xs, ms, ms, cms, cms, one],
        compiler_params=_cp(vmem=VMEM_BIG),
    )(*dp_dn, dz, dp_na, dgt, w_in_p, x, ctx, sh1, sc1, csh1, csc1, n1w, dx1)


def _shift_masks():
    t = lax.broadcasted_iota(jnp.int32, (T, 1), 0)
    lo = jnp.where(t < CT, 0, CT)
    hi = jnp.where(t < CT, CT, T)
    return {off: (t + off >= lo) & (t + off < hi) for off in range(-2, 3)}


def _shifted(u, off, masks):
    s = u if off == 0 else pltpu.roll(u, (-off) % T, 0)
    return jnp.where(masks[off], s, 0.0)


def _conv_pre(u, w, masks):
    acc = _shifted(u, -2, masks) * w[0:1, :]
    for kk in range(1, 5):
        acc = acc + _shifted(u, kk - 2, masks) * w[kk:kk + 1, :]
    return acc


def _act_q(pre):
    a = _silu(pre)
    return a * lax.rsqrt(jnp.sum(a * a, axis=-1, keepdims=True) + EPS) * (128.0 ** -0.5)


def _act_k(pre):
    a = _silu(pre)
    return a * lax.rsqrt(jnp.sum(a * a, axis=-1, keepdims=True) + EPS)


def _dn_prep_fwd(p, cw):
    def body(pq_ref, pk_ref, pv_ref, wq_ref, wk_ref, wv_ref, q_ref, k_ref, v_ref):
        masks = _shift_masks()
        q_ref[...] = _act_q(_conv_pre(pq_ref[...], wq_ref[...], masks))
        k_ref[...] = _act_k(_conv_pre(pk_ref[...], wk_ref[...], masks))
        v_ref[...] = _silu(_conv_pre(pv_ref[...], wv_ref[...], masks))

    ps = lambda o: pl.BlockSpec((None, T, 128), lambda b, h: (b, 0, o + h))
    ws = lambda o: pl.BlockSpec((8, 128), lambda b, h: (0, o + h))
    os_ = pl.BlockSpec((None, T, 128), lambda b, h: (b, 0, h))
    return pl.pallas_call(
        body, name="dn_prep_fwd", grid=(NB, DNH), out_shape=[jax.ShapeDtypeStruct((NB, T, DNW), F32)] * 3,
        in_specs=[ps(0), ps(4), ps(8), ws(0), ws(4), ws(8)], out_specs=[os_] * 3,
        compiler_params=_cp(vmem=VMEM_BIG),
    )(p, p, p, cw, cw, cw)


def _dn_prep_bwd(p, cw, dqs, dks, dvs):
    def body(pq_ref, pk_ref, pv_ref, wq_ref, wk_ref, wv_ref, dqf_ref, dqb_ref, dkf_ref, dkb_ref, dvf_ref, dvb_ref,
             oq_ref, ok_ref, ov_ref, gq_ref, gk_ref, gv_ref):
        b = pl.program_id(1)
        masks = _shift_masks()

        def one(p_ref, w_ref, df_ref, db_ref, act, o_ref, g_ref):
            u = p_ref[...]
            w = w_ref[...]
            _, vjp = jax.vjp(act, _conv_pre(u, w, masks))
            (dpre,) = vjp(df_ref[...] + db_ref[...])
            du = _shifted(dpre, 2, masks) * w[0:1, :]
            rows = [jnp.sum(dpre * _shifted(u, -2, masks), axis=0, keepdims=True)]
            for kk in range(1, 5):
                du = du + _shifted(dpre, 2 - kk, masks) * w[kk:kk + 1, :]
                rows.append(jnp.sum(dpre * _shifted(u, kk - 2, masks), axis=0, keepdims=True))
            o_ref[...] = du
            gw = jnp.concatenate(rows + [jnp.zeros((3, 128), F32)], axis=0)

            @pl.when(b == 0)
            def _():
                g_ref[...] = gw

            @pl.when(b > 0)
            def _():
                g_ref[...] += gw

        one(pq_ref, wq_ref, dqf_ref, dqb_ref, _act_q, oq_ref, gq_ref)
        one(pk_ref, wk_ref, dkf_ref, dkb_ref, _act_k, ok_ref, gk_ref)
        one(pv_ref, wv_ref, dvf_ref, dvb_ref, _silu, ov_ref, gv_ref)

    ps = lambda o: pl.BlockSpec((None, T, 128), lambda h, b: (b, 0, o + h))
    ws = lambda o: pl.BlockSpec((8, 128), lambda h, b: (0, o + h))
    ds = pl.BlockSpec((None, T, 128), lambda h, b: (b, 0, h))
    outs = pl.pallas_call(
        body, name="dn_prep_bwd", grid=(DNH, NB),
        out_shape=[jax.ShapeDtypeStruct((NB, T, DNW), F32)] * 3 + [jax.ShapeDtypeStruct((8, DNW), F32)] * 3,
        in_specs=[ps(0), ps(4), ps(8), ws(0), ws(4), ws(8)] + [ds] * 6,
        out_specs=[ds] * 3
        + [pl.BlockSpec((8, 128), lambda h, b: (0, h))] * 3,
        compiler_params=_cp(vmem=VMEM_BIG),
    )(p, p, p, cw, cw, cw, *dqs, *dks, *dvs)
    return outs


def _gate_fn(a, prm):
    lane = lax.broadcasted_iota(jnp.int32, a.shape, 1)
    z = a + prm[1:2, :]
    sp = jnp.maximum(z, 0.0) + jnp.log(1.0 + jnp.exp(-jnp.abs(z)))
    g = -jnp.exp(prm[0:1, :]) * sp
    return jnp.where(lane < 8, g, jnp.where(lane < 16, _sigmoid(a), 0.0))


def _gates_fwd(p, prm):
    def body(a_ref, prm_ref, o_ref):
        o_ref[...] = _gate_fn(a_ref[...], prm_ref[...])

    return pl.pallas_call(
        body, name="gates_fwd", grid=(NB,), out_shape=jax.ShapeDtypeStruct((NB, T, 128), F32),
        in_specs=[pl.BlockSpec((None, T, 128), lambda b: (b, 0, COL_G // 128)), pl.BlockSpec((8, 128), lambda b: (0, 0))],
        out_specs=pl.BlockSpec((None, T, 128), lambda b: (b, 0, 0)),
    )(p, prm)


def _gates_bwd(p, prm, dgf, dgb):
    def body(a_ref, prm_ref, dgf_ref, dgb_ref, o_ref, dprm_ref):
        b = pl.program_id(0)
        tot = dgf_ref[...] + dgb_ref[...]
        _, vjp = jax.vjp(_gate_fn, a_ref[...], prm_ref[...])
        da, dprm = vjp(tot)
        o_ref[...] = da

        @pl.when(b == 0)
        def _():
            dprm_ref[...] = dprm

        @pl.when(b > 0)
        def _():
            dprm_ref[...] += dprm

    return pl.pallas_call(
        body, name="gates_bwd", grid=(NB,),
        out_shape=[jax.ShapeDtypeStruct((NB, T, 128), F32), jax.ShapeDtypeStruct((8, 128), F32)],
        in_specs=[pl.BlockSpec((None, T, 128), lambda b: (b, 0, COL_G // 128)), pl.BlockSpec((8, 128), lambda b: (0, 0)),
                  pl.BlockSpec((None, T, 128), lambda b: (b, 0, 0)), pl.BlockSpec((None, T, 128), lambda b: (b, 0, 0))],
        out_specs=[pl.BlockSpec((None, T, 128), lambda b: (b, 0, 0)), pl.BlockSpec((8, 128), lambda b: (0, 0))],
        compiler_params=_cp(vmem=VMEM_BIG),
    )(p, prm, dgf, dgb)


def _lockstep(gens, joint=None):
    n = len(gens)
    res, vals = [None] * n, [None] * n
    while True:
        msgs = []
        for i, g in enumerate(gens):
            try:
                msgs.append((g.send(vals[i]), False))
            except StopIteration as e:
                res[i] = e.value
                msgs.append((None, True))
        if msgs[0][1]:
            return res
        if msgs[0][0] is not None:
            vals = list(joint(tuple(m[0] for m in msgs)))
        else:
            vals = [None] * n


def _tri_inv_gen(m, md, m1, m2, eye):
    n = -(m * md)
    t = eye + n
    pw = n
    for _ in range(3):
        pw = _dg(pw, pw, _NN)
        yield
        t = t + _dg(t, pw, _NN)
        yield
    a = _dg(t, m * m1, _NN)
    yield
    t = t - _dg(a, t, _NN)
    yield
    a = _dg(t, m * m2, _NN)
    yield
    return t - _dg(a, t, _NN)


def _tri_inv_bwd_gen(t, g):
    tt = t.T
    a = _dg(tt, g, _NN)
    yield
    return -_dg(a, tt, _NN)


@jax.custom_vjp
def _tri_inv(ms, md, m1, m2, eye):
    return tuple(_lockstep([_tri_inv_gen(m, md, m1, m2, eye) for m in ms]))


def _tri_inv_f(ms, md, m1, m2, eye):
    ts = _tri_inv(ms, md, m1, m2, eye)
    return ts, (ts, md)


def _tri_inv_b(res, gs):
    ts, md = res
    z = jnp.zeros_like(md)
    return (tuple(_lockstep([_tri_inv_bwd_gen(t, g) for t, g in zip(ts, gs)])), z, z, z, z)


_tri_inv.defvjp(_tri_inv_f, _tri_inv_b)


@jax.custom_vjp
def _tri_inv_saved(ms, ts):
    return tuple(t.astype(F32) for t in ts)


def _tri_inv_saved_f(ms, ts):
    return _tri_inv_saved(ms, ts), ts


def _tri_inv_saved_b(ts, gs):
    dms = tuple(_lockstep([_tri_inv_bwd_gen(t, g) for t, g in zip(ts, gs)]))
    return dms, tuple(jnp.zeros_like(t) for t in ts)


_tri_inv_saved.defvjp(_tri_inv_saved_f, _tri_inv_saved_b)


SR = DNH * CH


def _scan_masks():
    i, j = np.arange(SR)[:, None], np.arange(SR)[None, :]
    same = (i >> 6) == (j >> 6)
    cm = np.zeros((2, 6, SR, SR), np.float32)
    for d in range(2):
        diff = i - j if d == 0 else j - i
        cm[d, 0] = same & (diff >= 0)
        cm[d, 1] = same & (diff > 0)
        cm[d, 2] = (i >> 4) == (j >> 4)
        cm[d, 3] = ((i >> 5) == (j >> 5)) & ((i >> 4) != (j >> 4))
        cm[d, 4] = (i >> 5) != (j >> 5)
        cm[d, 5] = i == j
    t, u = np.arange(CH)[:, None], np.arange(CH)[None, :]
    c64 = np.stack([t >= u, t <= u]).astype(np.float32)
    return jnp.asarray(cm), jnp.asarray(c64)


def _hsl(h):
    return slice(CH * h, CH * h + CH), slice(128 * h, 128 * h + 128)


@jax.custom_vjp
def _heads_mm(x, s):
    return jnp.concatenate([_dg(x[_hsl(h)[0]], s[:, _hsl(h)[1]], _NN) for h in range(DNH)], axis=0)


def _heads_mm_b(res, g):
    x, s = res
    dx = jnp.concatenate([_dg(g[_hsl(h)[0]], s[:, _hsl(h)[1]], _NT) for h in range(DNH)], axis=0)
    ds = jnp.concatenate([_dg(x[_hsl(h)[0]], g[_hsl(h)[0]], _TN) for h in range(DNH)], axis=1)
    return dx, ds


_heads_mm.defvjp(lambda x, s: (_heads_mm(x, s), (x, s)), _heads_mm_b)


@jax.custom_vjp
def _heads_outer(k, v):
    return jnp.concatenate([_dg(k[_hsl(h)[0]], v[_hsl(h)[0]], _TN) for h in range(DNH)], axis=1)


def _heads_outer_b(res, g):
    k, v = res
    dk = jnp.concatenate([_dg(v[_hsl(h)[0]], g[:, _hsl(h)[1]], _NT) for h in range(DNH)], axis=0)
    dv = jnp.concatenate([_dg(k[_hsl(h)[0]], g[:, _hsl(h)[1]], _NN) for h in range(DNH)], axis=0)
    return dk, dv


_heads_outer.defvjp(lambda k, v: (_heads_outer(k, v), (k, v)), _heads_outer_b)


def _chunk_gen(q, k, v, gb, s, cm, c64, d):
    incl, strict = cm[0], cm[1]
    lane = lax.broadcasted_iota(jnp.int32, (1, 128), 1)
    oh = lambda n: jnp.where(lane == n, 1.0, 0.0).astype(F32)
    col = lambda x, base: jnp.concatenate(
        [jnp.sum(x * oh(base + d * DNH + h), axis=1, keepdims=True) for h in range(DNH)], axis=0)
    beta = col(gb, 8)
    gam = col(_mask_mm(c64, gb), 0)
    yield
    tot = jnp.sum(gb, axis=0, keepdims=True)
    gtots = [jnp.sum(tot * oh(d * DNH + h), axis=1, keepdims=True) for h in range(DNH)]
    gtot = jnp.concatenate([jnp.broadcast_to(t, (CH, 1)) for t in gtots], axis=0)
    eg = jnp.concatenate([jnp.broadcast_to(jnp.exp(t), (1, 128)) for t in gtots], axis=1)
    row0 = jnp.where(lax.broadcasted_iota(jnp.int32, (128, 1), 0) == 0, 1.0, 0.0).astype(F32)
    gam_row = jnp.sum(jnp.broadcast_to(gam, (SR, 128)).T * row0, axis=0, keepdims=True)
    dec = jnp.exp(jnp.where(incl > 0.5, gam - gam_row, NEG))
    e_gam = jnp.exp(gam)
    kb = k * beta
    kk = _mm_nt(kb, k)
    yield
    tm = yield strict * (kk * dec)
    u = _mm(tm, v * beta)
    yield
    w = _mm(tm, kb * e_gam)
    yield
    qk = _mm_nt(q, k) * dec
    yield
    ws = _heads_mm(w, s)
    yield
    v_new = u - ws
    qs = _heads_mm(q * e_gam, s)
    yield
    o = qs + _mm(qk, v_new)
    yield
    s_new = s * eg + _heads_outer(k * jnp.exp(gtot - gam), v_new)
    return o, s_new, tm


def _chunks(args, cm_ref, c64_ref, saved=None):
    cm0 = cm_ref[0]
    if saved is None:
        inv = lambda ms: _tri_inv(ms, cm0[2], cm0[3], cm0[4], cm0[5])
    else:
        inv = lambda ms: _tri_inv_saved(ms, saved)
    return _lockstep([_chunk_gen(q, k, v, gb, s, cm_ref[d], c64_ref[d], d) for q, k, v, gb, s, d in args], inv)


def _chunk_rev(s):
    return jnp.where(s < NCH_CTX, NCH_CTX - 1 - s, NCH + NCH_CTX - 1 - s)


def _scan_specs(step):
    cf = lambda s: step(s)
    cb = lambda s: _chunk_rev(step(s))
    tok = lambda c: pl.BlockSpec((NB, CH, DNW), lambda s: (0, c(s), 0))
    gat = lambda c: pl.BlockSpec((NB, CH, 128), lambda s: (0, c(s), 0))
    sta = lambda c: pl.BlockSpec((NB, None, 128, DNW), lambda s: (0, c(s), 0, 0))
    inv = lambda c: pl.BlockSpec((NB, None, SR, SR), lambda s: (0, c(s), 0, 0))
    return (tok(cf), gat(cf), sta(cf), inv(cf)), (tok(cb), gat(cb), sta(cb), inv(cb))


def _const_specs():
    return [pl.BlockSpec((2, 6, SR, SR), lambda s: (0, 0, 0, 0)), pl.BlockSpec((2, CH, CH), lambda s: (0, 0, 0))]


SCAN_CHAINS = [(b, d) for b in range(NB) for d in range(2)]


def _stack(ref, b):
    return jnp.concatenate([ref[b, :, 128 * h:128 * h + 128] for h in range(DNH)], axis=0)


def _unstack_to(ref, b, val):
    for h in range(DNH):
        ref[b, :, 128 * h:128 * h + 128] = val[CH * h:CH * h + CH]


def _gather_plan(ins, outs, pos):
    me = _slot(pos)
    locals_, xfers = [], []
    for src, dst in zip(ins, outs):
        locals_.append((src, dst.at[me]))
        for k in range(1, 8):
            peer = _flip(pos, k)
            xfers.append((src, dst.at[me], dst.at[_slot(peer)], peer))
    return locals_, xfers


def _scan_fwd(q, k, v, gates, carry=()):
    nc = len(carry)

    def body(*refs):
        cm_ref, c64_ref, qf, kf, vf, gf, qb, kb, vb, gb = refs[:10]
        c_in = refs[10:10 + nc]
        of_ref, sf_ref, tf_ref, ob_ref, sb_ref, tb_ref = refs[10 + nc:16 + nc]
        c_out = refs[16 + nc:16 + 2 * nc]
        s_ref = refs[16 + 2 * nc]
        sems = refs[17 + 2 * nc:]
        if nc:
            @pl.when(pl.program_id(0) == 0)
            def _():
                _xfer_start(_gather_plan(c_in, c_out, _my_pos()), sems)

        @pl.when(pl.program_id(0) == 0)
        def _():
            s_ref[...] = jnp.zeros_like(s_ref)

        dirs = ((qf, kf, vf, gf, of_ref, sf_ref, tf_ref), (qb, kb, vb, gb, ob_ref, sb_ref, tb_ref))
        args = [(_stack(dirs[d][0], b), _stack(dirs[d][1], b), _stack(dirs[d][2], b), dirs[d][3][b], s_ref[i], d)
                for i, (b, d) in enumerate(SCAN_CHAINS)]
        for i, (o, s_new, tm) in enumerate(_chunks(args, cm_ref, c64_ref)):
            b, d = SCAN_CHAINS[i]
            dirs[d][5][b] = args[i][4]
            dirs[d][6][b] = tm.astype(BF16)
            _unstack_to(dirs[d][4], b, o)
            s_ref[i] = s_new

        if nc:
            @pl.when(pl.program_id(0) == NCH - 1)
            def _():
                _xfer_wait(_gather_plan(c_in, c_out, _my_pos()), sems)

    (tf, gf_, sf_, if_), (tb, gb_, sb_, ib_) = _scan_specs(lambda s: s)
    o_sh = jax.ShapeDtypeStruct((NB, T, DNW), F32)
    st_sh = jax.ShapeDtypeStruct((NB, NCH, 128, DNW), F32)
    inv_sh = jax.ShapeDtypeStruct((NB, NCH, SR, SR), BF16)
    hbm = pl.BlockSpec(memory_space=pltpu.HBM)
    return pl.pallas_call(
        body, name="scan_fwd", grid=(NCH,),
        out_shape=[o_sh, st_sh, inv_sh, o_sh, st_sh, inv_sh] + [jax.ShapeDtypeStruct((8,) + h.shape, h.dtype) for h in carry],
        in_specs=_const_specs() + [tf, tf, tf, gf_, tb, tb, tb, gb_] + [hbm] * nc,
        out_specs=[tf, sf_, if_, tb, sb_, ib_] + [hbm] * nc,
        scratch_shapes=[pltpu.VMEM((len(SCAN_CHAINS), 128, DNW), F32)] + (_xfer_sems(nc, 7 * nc) if nc else []),
        compiler_params=_cp(vmem=VMEM_BIG),
    )(*_scan_masks(), q, k, v, gates, q, k, v, gates, *carry)


def _scan_bwd(q, k, v, gates, st_f, st_b, inv_f, inv_b, do, carry=()):
    nc = len(carry)

    def body(*refs):
        (cm_ref, c64_ref, qf, kf, vf, gf, sf, tf_, dof, qb, kb, vb, gb, sb, tb_, dob) = refs[:16]
        c_in = refs[16:16 + nc]
        dqf, dkf, dvf, dgf, dqb, dkb, dvb, dgb = refs[16 + nc:24 + nc]
        c_out = refs[24 + nc:24 + 2 * nc]
        ds_ref = refs[24 + 2 * nc]
        sems = refs[25 + 2 * nc:]
        first = pl.program_id(0) == 0
        last = pl.program_id(0) == NCH - 1
        if nc:
            @pl.when(first)
            def _():
                _xfer_start(_scatter_plan(c_in, c_out, _my_pos()), sems)

        @pl.when(first)
        def _():
            ds_ref[...] = jnp.zeros_like(ds_ref)

        dirs = ((qf, kf, vf, gf, sf, dof, dqf, dkf, dvf, dgf), (qb, kb, vb, gb, sb, dob, dqb, dkb, dvb, dgb))
        prim = tuple((_stack(dirs[d][0], b), _stack(dirs[d][1], b), _stack(dirs[d][2], b), dirs[d][3][b], dirs[d][4][b])
                     for b, d in SCAN_CHAINS)
        cots = tuple((_stack(dirs[d][5], b), ds_ref[i]) for i, (b, d) in enumerate(SCAN_CHAINS))
        saved = tuple((tf_, tb_)[d][b] for b, d in SCAN_CHAINS)
        _, vjp = jax.vjp(lambda ps: tuple(r[:2] for r in _chunks([p + (SCAN_CHAINS[i][1],) for i, p in enumerate(ps)],
                                                                cm_ref, c64_ref, saved)), prim)
        (grads,) = vjp(cots)
        for i, (dq, dk, dv, dg, ds) in enumerate(grads):
            b, d = SCAN_CHAINS[i]
            _unstack_to(dirs[d][6], b, dq)
            _unstack_to(dirs[d][7], b, dk)
            _unstack_to(dirs[d][8], b, dv)
            dirs[d][9][b] = dg
            ds_ref[i] = ds

        if nc:
            @pl.when(last)
            def _():
                _xfer_wait(_scatter_plan(c_in, c_out, _my_pos()), sems)

    (tf, gf_, sf_, if_), (tb, gb_, sb_, ib_) = _scan_specs(lambda s: NCH - 1 - s)
    o_sh = jax.ShapeDtypeStruct((NB, T, DNW), F32)
    g_sh = jax.ShapeDtypeStruct((NB, T, 128), F32)
    hbm = pl.BlockSpec(memory_space=pltpu.HBM)
    return pl.pallas_call(
        body, name="scan_bwd", grid=(NCH,),
        out_shape=[o_sh, o_sh, o_sh, g_sh] * 2 + [jax.ShapeDtypeStruct(g.shape, g.dtype) for g in carry],
        in_specs=_const_specs() + [tf, tf, tf, gf_, sf_, if_, tf, tb, tb, tb, gb_, sb_, ib_, tb] + [hbm] * nc,
        out_specs=[tf, tf, tf, gf_, tb, tb, tb, gb_] + [hbm] * nc,
        scratch_shapes=[pltpu.VMEM((len(SCAN_CHAINS), 128, DNW), F32)] + (_xfer_sems(nc, 7 * nc) if nc else []),
        compiler_params=_cp(vmem=VMEM_BIG),
    )(*_scan_masks(), q, k, v, gates, st_f, inv_f, do, q, k, v, gates, st_b, inv_b, do, *carry)


def _segmean(a, bd):
    hi = a.astype(BF16)
    lo = (a - hi.astype(F32)).astype(BF16)
    return jnp.dot(hi, bd, preferred_element_type=F32) + jnp.dot(lo, bd, preferred_element_type=F32)


def _na_prep_fwd(p, qw, kw, bd):
    def body(q_ref, k_ref, v_ref, qw_ref, kw_ref, bd_ref, oq_ref, ok_ref, ov_ref):
        bdm = bd_ref[...]
        q = q_ref[...]
        k = k_ref[...]
        oq_ref[...] = (q * lax.rsqrt(_segmean(q * q, bdm) + EPS) * qw_ref[...] * 0.125).astype(BF16)
        ok_ref[...] = (k * lax.rsqrt(_segmean(k * k, bdm) + EPS) * kw_ref[...]).astype(BF16)
        ov_ref[...] = v_ref[...].astype(BF16)

    ps = lambda o: pl.BlockSpec((None, RT, NAW), lambda b, j: (b, j, COL_NA // NAW + o))
    one = pl.BlockSpec((1, NAW), lambda b, j: (0, 0))
    os_ = pl.BlockSpec((None, RT, NAW), lambda b, j: (b, j, 0))
    return pl.pallas_call(
        body, name="na_prep_fwd", grid=(NB, T // RT), out_shape=[jax.ShapeDtypeStruct((NB, T, NAW), BF16)] * 3,
        in_specs=[ps(0), ps(1), ps(2), one, one, pl.BlockSpec((NAW, NAW), lambda b, j: (0, 0))], out_specs=[os_] * 3,
    )(p, p, p, qw, kw, bd)


def _na_prep_bwd(p, qw, kw, bd, dq, dk, dv):
    def body(q_ref, k_ref, qw_ref, kw_ref, bd_ref, dq_ref, dk_ref, dv_ref, o_ref, gq_ref, gk_ref):
        b, j = pl.program_id(0), pl.program_id(1)
        bdm = bd_ref[...]

        def one(x, w, dy, scale):
            r = lax.rsqrt(_segmean(x * x, bdm) + EPS)
            g = dy * w * scale
            dx = r * g - x * (r * r * r) * _segmean(g * x, bdm)
            dw = jnp.sum(dy * x * r * scale, axis=0, keepdims=True)
            return dx, dw

        dqv = jnp.where(j == 0, 0.0, dq_ref[...])
        dxq, dwq = one(q_ref[...], qw_ref[...], dqv, 0.125)
        dxk, dwk = one(k_ref[...], kw_ref[...], dk_ref[...], 1.0)
        o_ref[...] = jnp.concatenate([dxq, dxk, dv_ref[...]], axis=1)
        pad = jnp.zeros((7, NAW), F32)

        @pl.when((b == 0) & (j == 0))
        def _():
            gq_ref[...] = jnp.zeros_like(gq_ref)
            gk_ref[...] = jnp.zeros_like(gk_ref)

        gq_ref[...] += jnp.concatenate([dwq, pad], axis=0)
        gk_ref[...] += jnp.concatenate([dwk, pad], axis=0)

    ps = lambda o: pl.BlockSpec((None, RT, NAW), lambda b, j: (b, j, COL_NA // NAW + o))
    one = pl.BlockSpec((1, NAW), lambda b, j: (0, 0))
    ts = pl.BlockSpec((None, RT, NAW), lambda b, j: (b, j, 0))
    acc = pl.BlockSpec((8, NAW), lambda b, j: (0, 0))
    return pl.pallas_call(
        body, name="na_prep_bwd", grid=(NB, T // RT),
        out_shape=[jax.ShapeDtypeStruct((NB, T, 1536), F32), jax.ShapeDtypeStruct((8, NAW), F32),
                   jax.ShapeDtypeStruct((8, NAW), F32)],
        in_specs=[ps(0), ps(1), one, one, pl.BlockSpec((NAW, NAW), lambda b, j: (0, 0)),
                  pl.BlockSpec((None, RT, NAW), lambda b, j: (b, jnp.maximum(j - 1, 0), 0)), ts, ts],
        out_specs=[pl.BlockSpec((None, RT, 1536), lambda b, j: (b, j, 0)), acc, acc],
        compiler_params=_cp(vmem=VMEM_BIG),
    )(p, p, qw, kw, bd, dq, dk, dv)


def _na_row0(r):
    return jnp.clip(r - WIN_ROWS // 2, 0, ROWS - WIN_ROWS)


def _na_cls(r):
    return _na_row0(r) - r + WIN_ROWS - 1


NHS = 4
NLW = NHS * 64
NSR = NHS * GRID_W


def _na_stack(x):
    lane = lax.broadcasted_iota(jnp.int32, (1, NLW), 1)
    return jnp.concatenate([jnp.where((lane >> 6) == h, x, jnp.zeros_like(x)) for h in range(NHS)], axis=0)


def _na_unstack(y):
    lane = lax.broadcasted_iota(jnp.int32, (1, NLW), 1)
    out = jnp.zeros((GRID_W, NLW), y.dtype)
    for h in range(NHS):
        out = jnp.where((lane >> 6) == h, y[GRID_W * h:GRID_W * h + GRID_W], out)
    return out


def _na_fwd_gen(q, kw, vw, kc, vc, bias):
    q4 = _na_stack(q)
    s_w = lax.dot_general(q4, kw, _NT, preferred_element_type=F32) + bias
    yield
    s_c = lax.dot_general(q4, kc, _NT, preferred_element_type=F32)
    yield
    mx = jnp.maximum(jnp.max(s_w, axis=1, keepdims=True), jnp.max(s_c, axis=1, keepdims=True))
    p_w = jnp.exp(s_w - mx)
    p_c = jnp.exp(s_c - mx)
    inv = 1.0 / (jnp.sum(p_w, axis=1, keepdims=True) + jnp.sum(p_c, axis=1, keepdims=True))
    yield
    o_w = jnp.dot(p_w.astype(BF16), vw, preferred_element_type=F32)
    yield
    o4 = (o_w + jnp.dot(p_c.astype(BF16), vc, preferred_element_type=F32)) * inv
    return _na_unstack(o4)


NRF = 2


def _na_fwd(qn, kn, vb, bias, carry=()):
    nc = len(carry)
    grid = (NB, ROWS // NRF)
    ng = NAH // NHS

    def body(*refs):
        q_ref, k_ref, v_ref = refs[:3]
        b_refs, rest = refs[3:3 + NRF], refs[3 + NRF:]
        c_in, o_ref, c_out, sems = rest[:nc], rest[nc], rest[nc + 1:2 * nc + 1], rest[2 * nc + 1:]
        ids = [pl.program_id(a) for a in range(2)]
        if nc:
            @pl.when((ids[0] == 0) & (ids[1] == 0))
            def _():
                _xfer_start(_gather_plan(c_in, c_out, _my_pos()), sems)

        gens = []
        for i in range(NRF):
            start = pl.multiple_of(CT + _na_row0(NRF * ids[1] + i) * GRID_W, GRID_W)
            rw = slice(GRID_W * i, GRID_W * (i + 1))
            for g in range(ng):
                ln = slice(NLW * g, NLW * g + NLW)
                gens.append(_na_fwd_gen(q_ref[rw, ln], k_ref[pl.ds(start, NWIN), ln], v_ref[pl.ds(start, NWIN), ln],
                                        k_ref[0:CT, ln], v_ref[0:CT, ln], b_refs[i][g]))
        for n, o in enumerate(_lockstep(gens)):
            i, g = n // ng, n % ng
            o_ref[GRID_W * i:GRID_W * (i + 1), NLW * g:NLW * g + NLW] = o.astype(BF16)

        if nc:
            @pl.when((ids[0] == grid[0] - 1) & (ids[1] == grid[1] - 1))
            def _():
                _xfer_wait(_gather_plan(c_in, c_out, _my_pos()), sems)

    qs = pl.BlockSpec((None, NRF * GRID_W, NAW), lambda b, j: (b, CT // (NRF * GRID_W) + j, 0))
    kv = pl.BlockSpec((None, T, NAW), lambda b, j: (b, 0, 0))
    bss = [pl.BlockSpec((None, ng, NSR, NWIN), lambda b, j, i=i: (_na_cls(NRF * j + i), 0, 0, 0)) for i in range(NRF)]
    lat = pl.BlockSpec((None, NRF * GRID_W, NAW), lambda b, j: (b, j, 0))
    hbm = pl.BlockSpec(memory_space=pltpu.HBM)
    res = pl.pallas_call(
        body, name="na_fwd", grid=grid,
        out_shape=[jax.ShapeDtypeStruct((NB, SEQ, NAW), BF16)] + [jax.ShapeDtypeStruct((8,) + h.shape, h.dtype) for h in carry],
        in_specs=[qs, kv, kv] + bss + [hbm] * nc, out_specs=[lat] + [hbm] * nc,
        scratch_shapes=_xfer_sems(nc, 7 * nc) if nc else [],
        compiler_params=_cp(vmem=VMEM_BIG),
    )(qn, kn, vb, *([bias] * NRF), *carry)
    return res[0], list(res[1:])


def _na_bwd_gen(q, kw, vw, kc, vc, bias, do):
    q4 = _na_stack(q)
    s_w = lax.dot_general(q4, kw, _NT, preferred_element_type=F32) + bias
    yield
    s_c = lax.dot_general(q4, kc, _NT, preferred_element_type=F32)
    yield
    mx = jnp.maximum(jnp.max(s_w, axis=1, keepdims=True), jnp.max(s_c, axis=1, keepdims=True))
    p_w = jnp.exp(s_w - mx)
    p_c = jnp.exp(s_c - mx)
    inv = 1.0 / (jnp.sum(p_w, axis=1, keepdims=True) + jnp.sum(p_c, axis=1, keepdims=True))
    p_w = p_w * inv
    p_c = p_c * inv
    do4 = _na_stack(do.astype(BF16))
    dp_w = lax.dot_general(do4, vw, _NT, preferred_element_type=F32)
    yield
    dp_c = lax.dot_general(do4, vc, _NT, preferred_element_type=F32)
    yield
    delta = jnp.sum(p_w * dp_w, axis=1, keepdims=True) + jnp.sum(p_c * dp_c, axis=1, keepdims=True)
    ds_w = p_w * (dp_w - delta)
    ds_c = p_c * (dp_c - delta)
    ds_wb, ds_cb = ds_w.astype(BF16), ds_c.astype(BF16)
    dq4 = jnp.dot(ds_wb, kw, preferred_element_type=F32)
    yield
    dq4 = dq4 + jnp.dot(ds_cb, kc, preferred_element_type=F32)
    yield
    dkw = lax.dot_general(ds_wb, q4, _TN, preferred_element_type=F32)
    yield
    dvw = lax.dot_general(p_w.astype(BF16), do4, _TN, preferred_element_type=F32)
    yield
    dkc = lax.dot_general(ds_cb, q4, _TN, preferred_element_type=F32)
    yield
    dvc = lax.dot_general(p_c.astype(BF16), do4, _TN, preferred_element_type=F32)
    return ds_w, _na_unstack(dq4), dkw, dvw, dkc, dvc


NRS = 4


def _na_bwd(qn, kn, vb, bias, do):
    def body(q_ref, k_ref, v_ref, *rest):
        b_refs, (do_ref, dq_ref, dk_ref, dv_ref, db_ref) = rest[:NRS], rest[NRS:]
        b, j = pl.program_id(1), pl.program_id(2)

        @pl.when((b == 0) & (j == 0))
        def _():
            db_ref[...] = jnp.zeros_like(db_ref)

        @pl.when(j == 0)
        def _():
            dk_ref[...] = jnp.zeros_like(dk_ref)
            dv_ref[...] = jnp.zeros_like(dv_ref)

        kc, vc = k_ref[0:CT, :], v_ref[0:CT, :]
        rows = [NRS * j + i for i in range(NRS)]
        starts = [pl.multiple_of(CT + _na_row0(r) * GRID_W, GRID_W) for r in rows]
        gens = [_na_bwd_gen(q_ref[GRID_W * i:GRID_W * (i + 1), :], k_ref[pl.ds(starts[i], NWIN), :],
                            v_ref[pl.ds(starts[i], NWIN), :], kc, vc, b_refs[i][...],
                            do_ref[GRID_W * i:GRID_W * (i + 1), :]) for i in range(NRS)]
        for i, (ds_w, dq, dkw, dvw, dkc, dvc) in enumerate(_lockstep(gens)):
            db_ref[_na_cls(rows[i])] += ds_w
            dq_ref[GRID_W * i:GRID_W * (i + 1), :] = dq
            dk_ref[pl.ds(starts[i], NWIN), :] += dkw
            dv_ref[pl.ds(starts[i], NWIN), :] += dvw
            dk_ref[0:CT, :] += dkc
            dv_ref[0:CT, :] += dvc

    qs = pl.BlockSpec((None, NRS * GRID_W, NLW), lambda g, b, j: (b, CT // (NRS * GRID_W) + j, g))
    kv = pl.BlockSpec((None, T, NLW), lambda g, b, j: (b, 0, g))
    bss = [pl.BlockSpec((None, None, NSR, NWIN), lambda g, b, j, i=i: (_na_cls(NRS * j + i), g, 0, 0)) for i in range(NRS)]
    lat = pl.BlockSpec((None, NRS * GRID_W, NLW), lambda g, b, j: (b, j, g))
    return pl.pallas_call(
        body, name="na_bwd", grid=(NAH // NHS, NB, ROWS // NRS),
        out_shape=[jax.ShapeDtypeStruct((NB, SEQ, NAW), F32)] + [jax.ShapeDtypeStruct((NB, T, NAW), F32)] * 2
        + [jax.ShapeDtypeStruct((WIN_ROWS, NAH // NHS, NSR, NWIN), F32)],
        in_specs=[qs, kv, kv] + bss + [lat],
        out_specs=[lat, kv, kv, pl.BlockSpec((WIN_ROWS, None, NSR, NWIN), lambda g, b, j: (0, g, 0, 0))],
        compiler_params=_cp(vmem=VMEM_BIG),
    )(qn, kn, vb, *([bias] * NRS), do)


def _na_tables():
    cols = np.arange(GRID_W)
    win_start = np.clip(cols - WIN_COLS // 2, 0, GRID_W - WIN_COLS)
    kc = cols[None, :]
    valid = (kc >= win_start[:, None]) & (kc < win_start[:, None] + WIN_COLS)
    rel = np.clip(kc - cols[:, None] + WIN_COLS - 1, 0, 2 * WIN_COLS - 2)
    return valid, rel


def _bias_table(rpb):
    valid, rel = _na_tables()
    nrel = 2 * WIN_COLS - 1
    shifts = np.stack([(rel == r) & valid for r in range(nrel)]).astype(np.float32)
    neg = np.where(valid, 0.0, NEG).astype(np.float32)

    def body(rpb_ref, sh_ref, neg_ref, o_ref):
        h = pl.program_id(0)
        for dr in range(2 * WIN_ROWS - 1):
            tile = neg_ref[...]
            for r in range(nrel):
                tile = tile + rpb_ref[h, dr, r] * sh_ref[r]
            for cls in range(WIN_ROWS):
                i = dr - cls
                if 0 <= i < WIN_ROWS:
                    o_ref[cls, :, i * GRID_W:(i + 1) * GRID_W] = tile

    return pl.pallas_call(
        body, name="bias_table", grid=(NAH,),
        out_shape=jax.ShapeDtypeStruct((WIN_ROWS, NAH // NHS, NSR, NWIN), F32),
        in_specs=[pl.BlockSpec(memory_space=pltpu.SMEM), pl.BlockSpec((nrel, GRID_W, GRID_W), lambda h: (0, 0, 0)),
                  pl.BlockSpec((GRID_W, GRID_W), lambda h: (0, 0))],
        out_specs=pl.BlockSpec((WIN_ROWS, None, GRID_W, NWIN), lambda h: (0, h // NHS, h % NHS, 0)),
    )(rpb, jnp.asarray(shifts), jnp.asarray(neg))


def _rpb_fold(db):
    def body1(d_ref, o_ref):
        for dr in range(2 * WIN_ROWS - 1):
            acc = None
            for c in range(WIN_ROWS):
                i = dr - c
                if 0 <= i < WIN_ROWS:
                    blk = d_ref[c, :, i * GRID_W:(i + 1) * GRID_W]
                    acc = blk if acc is None else acc + blk
            o_ref[dr] = acc

    t15 = pl.pallas_call(
        body1, name="rpb_fold_rows", grid=(NAH,),
        out_shape=jax.ShapeDtypeStruct((NAH, 2 * WIN_ROWS - 1, GRID_W, GRID_W), F32),
        in_specs=[pl.BlockSpec((WIN_ROWS, None, GRID_W, NWIN), lambda h: (0, h, 0, 0))],
        out_specs=pl.BlockSpec((None, 2 * WIN_ROWS - 1, GRID_W, GRID_W), lambda h: (h, 0, 0, 0)),
        compiler_params=_cp(vmem=VMEM_BIG),
    )(db.reshape(WIN_ROWS, NAH, GRID_W, NWIN))
    valid, rel = _na_tables()
    onehot = np.zeros((GRID_W * GRID_W, 128), np.float32)
    flat_rel, flat_valid = rel.reshape(-1), valid.reshape(-1)
    onehot[np.arange(GRID_W * GRID_W)[flat_valid], flat_rel[flat_valid]] = 1.0

    def body2(a_ref, oh_ref, o_ref):
        o_ref[...] = _hmm(a_ref[...], oh_ref[...])

    vm = pl.BlockSpec(memory_space=pltpu.VMEM)
    out = pl.pallas_call(
        body2, name="rpb_fold_cols", out_shape=jax.ShapeDtypeStruct((NAH * 15, 128), F32),
        in_specs=[vm, vm], out_specs=vm, compiler_params=_cp(vmem=VMEM_BIG),
    )(t15.reshape(NAH * 15, GRID_W * GRID_W), jnp.asarray(onehot))
    return out[:, :31].reshape(NAH, 15, 31)


def _dn_out(o, z, w):
    outs = []
    for h in range(DNH):
        sl = slice(128 * h, 128 * h + 128)
        oh = o[:, sl]
        r = lax.rsqrt(jnp.mean(oh * oh, axis=-1, keepdims=True) + EPS)
        outs.append(oh * r * w[:, sl] * _silu(z[:, sl]))
    return jnp.concatenate(outs, axis=1)


def _outproj_fwd(of, ob, p, na, onw, x, g1, w_out):
    def body(of_ref, ob_ref, z_ref, na_ref, w_ref, x_ref, g_ref, wo_ref, x1_ref, ai_ref, ao_ref):
        dn = _dn_out(of_ref[...] + ob_ref[...], z_ref[...], w_ref[...])
        ai = jnp.concatenate([dn.astype(BF16), na_ref[...]], axis=1)
        ai_ref[...] = ai
        ao = jnp.dot(ai, wo_ref[...], preferred_element_type=F32)
        ao_ref[...] = ao
        x1_ref[...] = x_ref[...] + g_ref[...] * ao

    lat = lambda w: pl.BlockSpec((None, RT, w), lambda b, j: (b, j, 0))
    return pl.pallas_call(
        body, name="outproj_fwd", grid=(NB, SEQ // RT),
        out_shape=[jax.ShapeDtypeStruct((NB, SEQ, D), F32), jax.ShapeDtypeStruct((NB, SEQ, D), BF16),
                   jax.ShapeDtypeStruct((NB, SEQ, D), F32)],
        in_specs=[pl.BlockSpec((None, RT, DNW), lambda b, j: (b, j + 1, 0)),
                  pl.BlockSpec((None, RT, DNW), lambda b, j: (b, j + 1, 0)),
                  pl.BlockSpec((None, RT, DNW), lambda b, j: (b, j + 1, COL_Z // DNW)),
                  lat(NAW), pl.BlockSpec((1, DNW), lambda b, j: (0, 0)), lat(D),
                  pl.BlockSpec((None, 1, D), lambda b, j: (b, 0, 0)), pl.BlockSpec((D, D), lambda b, j: (0, 0))],
        out_specs=[lat(D), lat(D), lat(D)],
        compiler_params=_cp(vmem=VMEM_BIG),
    )(of, ob, p, na, onw, x, g1, w_out)


def _outproj_bwd(dx1, ao, g1, w_out, of, ob, p, onw):
    def body(dx_ref, ao_ref, g_ref, wo_ref, of_ref, ob_ref, z_ref, w_ref, dao_ref, do_ref, dz_ref, dna_ref, dg_ref, dw_ref):
        b, j = pl.program_id(0), pl.program_id(1)

        @pl.when((b == 0) & (j == 0))
        def _():
            dw_ref[...] = jnp.zeros_like(dw_ref)

        @pl.when(j == 0)
        def _():
            do_ref[...] = jnp.zeros_like(do_ref)
            dz_ref[...] = jnp.zeros_like(dz_ref)

        @pl.when(j == 1)
        def _():
            dg_ref[...] = jnp.zeros_like(dg_ref)

        @pl.when(j >= 1)
        def _():
            dx = dx_ref[...]
            dg_ref[...] += jnp.sum(dx * ao_ref[...], axis=0, keepdims=True)
            dao = (g_ref[...] * dx).astype(BF16)
            dao_ref[...] = dao
            dai = lax.dot_general(dao, wo_ref[...], _NT, preferred_element_type=F32)
            dna_ref[...] = dai[:, DNW:]
            _, vjp = jax.vjp(_dn_out, of_ref[...] + ob_ref[...], z_ref[...], w_ref[...])
            do, dz, dw = vjp(dai[:, :DNW])
            do_ref[...] = do
            dz_ref[...] = dz
            dw_ref[...] += jnp.concatenate([dw, jnp.zeros((7, DNW), F32)], axis=0)

    lat = lambda w: pl.BlockSpec((None, RT, w), lambda b, j: (b, jnp.maximum(j - 1, 0), 0))
    tok = lambda w: pl.BlockSpec((None, RT, w), lambda b, j: (b, j, 0))
    return pl.pallas_call(
        body, name="outproj_bwd", grid=(NB, T // RT),
        out_shape=[jax.ShapeDtypeStruct((NB, SEQ, D), BF16), jax.ShapeDtypeStruct((NB, T, DNW), F32),
                   jax.ShapeDtypeStruct((NB, T, DNW), F32), jax.ShapeDtypeStruct((NB, SEQ, NAW), F32),
                   jax.ShapeDtypeStruct((NB, 1, D), F32), jax.ShapeDtypeStruct((8, DNW), F32)],
        in_specs=[_rows2d_spec(D), lat(D), pl.BlockSpec((None, 1, D), lambda b, j: (b, 0, 0)),
                  pl.BlockSpec((D, D), lambda b, j: (0, 0)),
                  tok(DNW), tok(DNW),
                  pl.BlockSpec((None, RT, DNW), lambda b, j: (b, j, COL_Z // DNW)),
                  pl.BlockSpec((1, DNW), lambda b, j: (0, 0))],
        out_specs=[lat(D), tok(DNW), tok(DNW), lat(NAW), pl.BlockSpec((None, 1, D), lambda b, j: (b, 0, 0)),
                   pl.BlockSpec((8, DNW), lambda b, j: (0, 0))],
        compiler_params=_cp(vmem=VMEM_BIG),
    )(dx1, ao, g1, w_out, of, ob, p, onw)


FT_FWD, FT_BWD = 512, 256
NFF = DFF // FFT


def _ffn_specs(ft):
    row = lambda w: pl.BlockSpec((ft, w), lambda i, f: (i, 0))
    ex = pl.BlockSpec((None, 1, D), lambda i, f: (i // (SEQ // ft), 0, 0))
    one = pl.BlockSpec((1, D), lambda i, f: (0, 0))
    wg = pl.BlockSpec((None, D, FFT), lambda i, f: (f, 0, 0))
    wu = pl.BlockSpec((None, D, FFT), lambda i, f: (NFF + f, 0, 0))
    wo = pl.BlockSpec((FFT, D), lambda i, f: (f, 0))
    ff = pl.BlockSpec((ft, FFT), lambda i, f: (i, f))
    return row, ex, one, wg, wu, wo, ff


def _ffn_fwd(x1, sh2, sc2, g2, n2w, w_in4, w_out, target):
    ft, tpe = FT_FWD, SEQ // FT_FWD

    def body(x_ref, sh_ref, sc_ref, g_ref, w2_ref, wg_ref, wu_ref, wo_ref, t_ref,
             h2_ref, gt_ref, up_ref, dy_ref, dg_ref, loss_ref, acc_ref):
        i, f = pl.program_id(0), pl.program_id(1)

        @pl.when(f == 0)
        def _():
            h2_ref[...] = _norm_mod(x_ref[...], w2_ref[...], sh_ref[...], sc_ref[...]).astype(BF16)
            acc_ref[...] = jnp.zeros_like(acc_ref)

        h2 = h2_ref[...]
        gate = jnp.dot(h2, wg_ref[...], preferred_element_type=F32)
        up = jnp.dot(h2, wu_ref[...], preferred_element_type=F32)
        gt_ref[...] = gate.astype(BF16)
        up_ref[...] = up.astype(BF16)
        acc_ref[...] += jnp.dot((_silu(gate) * up).astype(BF16), wo_ref[...], preferred_element_type=F32)

        @pl.when((i == 0) & (f == 0))
        def _():
            loss_ref[...] = jnp.zeros_like(loss_ref)

        @pl.when(f == NFF - 1)
        def _():
            ffn = acc_ref[...]
            err = x_ref[...] + g_ref[...] * ffn - t_ref[...]
            dy = err * (1.0 / D)
            dy_ref[...] = dy
            part = jnp.sum(dy * ffn, axis=0, keepdims=True)
            loss_ref[...] += jnp.sum(jnp.sum(err * err, axis=1, keepdims=True), axis=0, keepdims=True) * (0.5 / D)

            @pl.when(i % tpe == 0)
            def _():
                dg_ref[...] = part

            @pl.when(i % tpe != 0)
            def _():
                dg_ref[...] += part

    row, ex, one, wg, wu, wo, ff = _ffn_specs(ft)
    return pl.pallas_call(
        body, name="ffn_fwd", grid=(NTOK // ft, NFF),
        out_shape=[jax.ShapeDtypeStruct((NTOK, D), BF16), jax.ShapeDtypeStruct((NTOK, DFF), BF16),
                   jax.ShapeDtypeStruct((NTOK, DFF), BF16), jax.ShapeDtypeStruct((NTOK, D), F32),
                   jax.ShapeDtypeStruct((NB, 1, D), F32), jax.ShapeDtypeStruct((8, 128), F32)],
        in_specs=[row(D), ex, ex, ex, one, wg, wu, wo, row(D)],
        out_specs=[row(D), ff, ff, row(D), ex, pl.BlockSpec((8, 128), lambda i, f: (0, 0))],
        scratch_shapes=[pltpu.VMEM((ft, D), F32)],
        compiler_params=_cp(vmem=VMEM_BIG),
    )(x1, sh2, sc2, g2, n2w, w_in4, w_in4, w_out, target)


def _rows2d_spec(width):
    return pl.BlockSpec((RT, width), lambda b, j: (b * (SEQ // RT) + jnp.maximum(j - 1, 0), 0))


def _ffn_bwd(dy, gate, up, x1, sh2, sc2, g2, n2w, w_in4, w_out):
    ft, tpe = FT_BWD, SEQ // FT_BWD
    ni = NTOK // ft

    def body(dy_ref, gt_ref, up_ref, x_ref, sh_ref, sc_ref, g_ref, w2_ref, wg_ref, wu_ref, wo_ref,
             dff_ref, act_ref, dgu_ref, dx_ref, dsh_ref, dsc_ref, dw_ref, acc_ref):
        f, i = pl.program_id(0), pl.program_id(1)
        rows = pl.ds(pl.multiple_of(i * ft, ft), ft)
        dff = (g_ref[...] * dy_ref[...]).astype(BF16)
        dff_ref[...] = dff
        dact = lax.dot_general(dff, wo_ref[...], _NT, preferred_element_type=F32)
        gate, up = gt_ref[...].astype(F32), up_ref[...].astype(F32)
        sg = _sigmoid(gate)
        sl = gate * sg
        act_ref[...] = (sl * up).astype(BF16)
        dgate = (dact * up * (sg * (1.0 + gate * (1.0 - sg)))).astype(BF16)
        dup = (dact * sl).astype(BF16)
        dgu_ref[0] = dgate
        dgu_ref[1] = dup
        part = (lax.dot_general(dgate, wg_ref[...], _NT, preferred_element_type=F32)
                + lax.dot_general(dup, wu_ref[...], _NT, preferred_element_type=F32))

        @pl.when(f == 0)
        def _():
            acc_ref[rows, :] = part

        @pl.when(f > 0)
        def _():
            acc_ref[rows, :] += part

        @pl.when((i == 0) & (f == 0))
        def _():
            dw_ref[...] = jnp.zeros_like(dw_ref)

        @pl.when(f == NFF - 1)
        def _():
            _, vjp = jax.vjp(_norm_mod, x_ref[...], w2_ref[...], sh_ref[...], sc_ref[...])
            dx, dw, dsh, dsc = vjp(acc_ref[rows, :])
            dx_ref[...] = dy_ref[...] + dx
            dw_ref[...] += dw

            @pl.when(i % tpe == 0)
            def _():
                dsh_ref[...] = dsh
                dsc_ref[...] = dsc

            @pl.when(i % tpe != 0)
            def _():
                dsh_ref[...] += dsh
                dsc_ref[...] += dsc

    last = lambda f, real, spare: jnp.where(f == NFF - 1, real, spare)
    row = lambda w: pl.BlockSpec((ft, w), lambda f, i: (i, 0))
    ex = pl.BlockSpec((None, 1, D), lambda f, i: (i // tpe, 0, 0))
    one = pl.BlockSpec((1, D), lambda f, i: (0, 0))
    wg = pl.BlockSpec((None, D, FFT), lambda f, i: (f, 0, 0))
    wu = pl.BlockSpec((None, D, FFT), lambda f, i: (NFF + f, 0, 0))
    wo = pl.BlockSpec((FFT, D), lambda f, i: (f, 0))
    ff = pl.BlockSpec((ft, FFT), lambda f, i: (i, f))
    row_once = lambda w: pl.BlockSpec((ft, w), lambda f, i: (last(f, i, ni), 0))
    ex_once = pl.BlockSpec((None, 1, D), lambda f, i: (last(f, i // tpe, NB), 0, 0))
    return pl.pallas_call(
        body, name="ffn_bwd", grid=(NFF, ni),
        out_shape=[jax.ShapeDtypeStruct((NTOK + ft, D), BF16), jax.ShapeDtypeStruct((NTOK, DFF), BF16),
                   jax.ShapeDtypeStruct((2, NTOK, DFF), BF16), jax.ShapeDtypeStruct((NTOK + ft, D), F32),
                   jax.ShapeDtypeStruct((NB + 1, 1, D), F32), jax.ShapeDtypeStruct((NB + 1, 1, D), F32),
                   jax.ShapeDtypeStruct((1, D), F32)],
        in_specs=[row(D), ff, ff, row(D), ex, ex, ex, one, wg, wu, wo],
        out_specs=[row_once(D), ff, pl.BlockSpec((2, ft, FFT), lambda f, i: (0, i, f)), row_once(D), ex_once, ex_once, one],
        scratch_shapes=[pltpu.VMEM((NTOK, D), F32)],
        compiler_params=_cp(vmem=VMEM_BIG),
    )(dy, gate, up, x1, sh2, sc2, g2, n2w, w_in4, w_in4, w_out)


def _local_step(x, ctx, target, mod_x, mod_c, norm1_w, norm2_w, w_in_p, conv_w, a_log, dt_bias, out_norm_w,
                q_norm_w, k_norm_w, rpb, w_out_b, w_ffn_in4, w_ffn_out_b, scatter_early=False, gather_late=None):
    sh1, sc1, g1, sh2, sc2, g2 = [mod_x[:, i * D:(i + 1) * D].reshape(NB, 1, D) for i in range(6)]
    csh1, csc1 = mod_c[0:D].reshape(1, 1, D), mod_c[D:2 * D].reshape(1, 1, D)
    cw = jnp.concatenate([conv_w, jnp.zeros((3, 1536), F32)], axis=0)
    prm = jnp.pad(jnp.stack([a_log.reshape(8), dt_bias.reshape(8)]), ((0, 6), (0, 120)))
    onw = jnp.tile(out_norm_w.reshape(1, 128), (1, DNH))
    qw = jnp.tile(q_norm_w.reshape(1, 64), (1, NAH))
    kw = jnp.tile(k_norm_w.reshape(1, 64), (1, NAH))
    seg = np.arange(NAW) // 64
    bd = jnp.asarray((seg[:, None] == seg[None, :]).astype(np.float32) / 64.0, dtype=BF16)
    bias = _bias_table(rpb)

    p, hm = _inproj_fwd(x, ctx, sh1, sc1, csh1, csc1, norm1_w, w_in_p)
    dq_, dk_, dv_ = _dn_prep_fwd(p, cw)
    gates = _gates_fwd(p, prm)
    res = _scan_fwd(dq_, dk_, dv_, gates, carry=gather_late[:2] if gather_late else ())
    o_f, st_f, inv_f, o_b, st_b, inv_b = res[:6]
    qn, kn, vb = _na_prep_fwd(p, qw, kw, bd)
    na, got = _na_fwd(qn, kn, vb, bias, carry=gather_late[2:] if gather_late else ())
    if gather_late:
        w_out_b = res[6].reshape(D, D)
        w_ffn_in4 = res[7].reshape(4, D, FFT)
        w_ffn_out_b = got[0].reshape(DFF, D)
    x1, ai, ao = _outproj_fwd(o_f, o_b, p, na, onw, x, g1, w_out_b)
    x1f = x1.reshape(NTOK, D)
    h2, gate, up, dy, dg2, loss8 = _ffn_fwd(x1f, sh2, sc2, g2, norm2_w, w_ffn_in4, w_ffn_out_b,
                                             target.reshape(NTOK, D))
    dff, act, dgu, dx1, dsh2, dsc2, dn2w = _ffn_bwd(dy, gate, up, x1f, sh2, sc2, g2, norm2_w, w_ffn_in4, w_ffn_out_b)
    g_w_ffn_out = _matmul_tn(act, dff, FFT, D, 512, "wgrad_ffn_out")
    g_w_ffn_in4 = _matmul_tn(
        h2, dgu, D, FFT, 1024, "wgrad_ffn_in", nn=2 * NFF,
        b_block=(None, 1024, FFT), b_map=lambda i, j, k: (j // NFF, k, j % NFF),
        out_shape=(4, D, FFT), out_block=(None, D, FFT), out_map=lambda i, j, k: (j, i, 0))
    dsh2, dsc2 = dsh2[:NB], dsc2[:NB]
    dao, do, dz, dna, dg1, donw = _outproj_bwd(dx1, ao, g1, w_out_b, o_f, o_b, p, onw)
    g_w_out = _matmul_tn(ai.reshape(NTOK, D), dao.reshape(NTOK, D), D, D, 1024, "wgrad_out")
    dqn, dkn, dvn, dbias = _na_bwd(qn, kn, vb, bias, dna)
    dp_na, dqw, dkw = _na_prep_bwd(p, qw, kw, bd, dqn, dkn, dvn)
    early = [g_w_out.reshape(8, D // 8, D), g_w_ffn_in4.reshape(8, D // 2, FFT), g_w_ffn_out.reshape(8, DFF // 8, D)]
    res = _scan_bwd(dq_, dk_, dv_, gates, st_f, st_b, inv_f, inv_b, do, carry=early if scatter_early else ())
    dqf, dkf, dvf, dgf, dqb, dkb, dvb, dgb = res[:8]
    if scatter_early:
        g_w_out, g_w_ffn_in4, g_w_ffn_out = res[8:]
    dpq, dpk, dpv, gcq, gck, gcv = _dn_prep_bwd(p, cw, (dqf, dqb), (dkf, dkb), (dvf, dvb))
    dgt, dprm = _gates_bwd(p, prm, dgf, dgb)
    dpb, grad_x, dsh1, dsc1, dcsh1, dcsc1, dn1w = _inproj_bwd(
        (dpq, dpk, dpv), dz, dp_na, dgt, w_in_p, x, ctx, sh1, sc1, csh1, csc1, norm1_w, dx1)
    g_w_in_p = _matmul_tn(dpb.reshape(NB * T, PC), hm.reshape(NB * T, D), PC, D, 512, "wgrad_in")
    g_rpb = _rpb_fold(dbias)

    dmod_x = jnp.concatenate([dsh1, dsc1, dg1, dsh2, dsc2, dg2], axis=2).reshape(NB, 6 * D)
    dmod_c = jnp.concatenate([dcsh1.reshape(D), dcsc1.reshape(D), jnp.zeros((4 * D,), F32)])
    g_conv = jnp.concatenate([gcq[:5], gck[:5], gcv[:5]], axis=1)
    small = dict(
        norm1_w=dn1w, norm2_w=dn2w, a_log=dprm[0, :8], dt_bias=dprm[1, :8],
        out_norm_w=donw[0].reshape(DNH, 128).sum(0), q_norm_w=dqw[0].reshape(NAH, 64).sum(0),
        k_norm_w=dkw[0].reshape(NAH, 64).sum(0), rpb=g_rpb, conv_w=g_conv)
    return loss8[0, 0], grad_x, dmod_x, dmod_c, small, g_w_in_p, g_w_out, g_w_ffn_in4, g_w_ffn_out


def _perm_w_in(wt):
    return jnp.concatenate([wt[:2048], wt[2064:3600], wt[2048:2064], jnp.zeros((PC - IN_COLS, wt.shape[1]), wt.dtype)],
                           axis=0)


def _unperm_w_in(gt):
    return jnp.concatenate([gt[:2048], gt[COL_G:COL_G + 16], gt[2048:COL_G]], axis=0)


def _misc_row(a_log, dt_bias, onw, qnw, knw):
    return jnp.concatenate([a_log.reshape(8), dt_bias.reshape(8), jnp.zeros((112,), F32), onw.reshape(128),
                            qnw.reshape(64), knw.reshape(64), jnp.zeros((D - 384,), F32)])


def _pack_small(c_ctx, n1, n2, b_ada, a_log, dt_bias, onw, qnw, knw, rpb, conv_s):
    misc = _misc_row(a_log, dt_bias, onw, qnw, knw)
    rp = jnp.concatenate([rpb.reshape(-1), jnp.zeros((4 * D - 3720,), F32)])
    cv = jnp.concatenate([conv_s.reshape(-1), jnp.zeros((2 * D - 1920,), F32)])
    return jnp.concatenate([c_ctx.reshape(-1), n1.reshape(-1), n2.reshape(-1), b_ada.reshape(-1), misc, rp, cv]).reshape(16, D)


def _unpack_small(a):
    f = a.reshape(-1)
    misc = f[9 * D:10 * D]
    return dict(
        c_ctx=f[0:D], norm1_w=f[D:2 * D].reshape(1, D), norm2_w=f[2 * D:3 * D].reshape(1, D),
        b_ada=f[3 * D:9 * D].reshape(1, 6 * D), dn_A_log=misc[0:8].reshape(1, 2, 4), dn_dt_bias=misc[8:16].reshape(1, 2, 4),
        dn_out_norm_w=misc[128:256].reshape(1, 128), na_q_norm_w=misc[256:320].reshape(1, 64),
        na_k_norm_w=misc[320:384].reshape(1, 64), na_rpb=f[10 * D:10 * D + 3720].reshape(1, 8, 15, 31),
        dn_conv_w=f[14 * D:14 * D + 1920].reshape(1, 5, 384))


_WEIGHTS = ['c_ctx', 'norm1_w', 'norm2_w', 'w_ada', 'b_ada', 'w_in', 'dn_conv_w', 'dn_A_log', 'dn_dt_bias',
            'dn_out_norm_w', 'na_q_norm_w', 'na_k_norm_w', 'na_rpb', 'w_out', 'w_ffn_in', 'w_ffn_out']


def kernel(x, c, ctx, c_ctx, norm1_w, norm2_w, w_ada, b_ada, w_in, dn_conv_w, dn_A_log, dn_dt_bias, dn_out_norm_w, na_q_norm_w, na_k_norm_w, na_rpb, w_out, w_ffn_in, w_ffn_out, loss_target, m_c_ctx, m_norm1_w, m_norm2_w, m_w_ada, m_b_ada, m_w_in, m_dn_conv_w, m_dn_A_log, m_dn_dt_bias, m_dn_out_norm_w, m_na_q_norm_w, m_na_k_norm_w, m_na_rpb, m_w_out, m_w_ffn_in, m_w_ffn_out, v_c_ctx, v_norm1_w, v_norm2_w, v_w_ada, v_b_ada, v_w_in, v_dn_conv_w, v_dn_A_log, v_dn_dt_bias, v_dn_out_norm_w, v_na_q_norm_w, v_na_k_norm_w, v_na_rpb, v_w_out, v_w_ffn_in, v_w_ffn_out):
    ix, iy, ic = lax.axis_index("x"), lax.axis_index("y"), lax.axis_index("c")
    chip = 2 * ix + iy
    me = 4 * ix + 2 * iy + ic

    crow = jnp.concatenate([c, c_ctx.reshape(1, D), jnp.zeros((5, D), F32)], axis=0)
    cvrow = jnp.concatenate([dn_conv_w[0].reshape(-1), jnp.zeros((2 * D - 1920,), F32)]).reshape(2, D)
    g0 = _allgather8(jnp.concatenate([crow, cvrow, jnp.zeros((6, D), F32)], axis=0), "gather_cond")
    cin = g0[:, :8].reshape(64, D)
    conv_full = jnp.concatenate([g0[2 * s, 8:10].reshape(-1)[:1920].reshape(5, 384) for s in range(4)], axis=1)

    w_ada_s = w_ada[0]
    b_s = lax.dynamic_slice(b_ada, (0, chip * 1536), (1, 1536))
    mod_s = _ada_fwd(cin, w_ada_s, b_s)
    mods = _gather_rows(mod_s, "gather_mod")
    mod_mine = jnp.concatenate([mods[0], mods[2], mods[4], mods[6]], axis=1)
    mod_x, mod_c = mod_mine[0:NB], mod_mine[2]

    def half(w2d):
        r = w2d.shape[0] // 2
        return lax.dynamic_slice(w2d, (ic * r, 0), (r, w2d.shape[1]))

    w_in_t = w_in[0].T
    (gw_in,) = _gather_weights([_cast_bf16(lax.dynamic_slice(w_in_t, (0, ic * (D // 2)), (900, D // 2)), "cast_w_in")])
    late = [_cast_bf16(half(w_out[0]), "cast_w_out"), _cast_bf16(half(w_ffn_in[0]), "cast_w_ffn_in"),
            _cast_bf16(half(w_ffn_out[0]), "cast_w_ffn_out")]
    w_in_p = _perm_w_in(gw_in.reshape(4, 2, 900, D // 2).transpose(0, 2, 1, 3).reshape(IN_COLS, D))

    (loss_part, grad_x, dmod_x, dmod_c, small, g_w_in_p, r_w_out, r_w_ffn_in, r_w_ffn_out) = _local_step(
        x, ctx, loss_target, mod_x, mod_c, norm1_w, norm2_w, w_in_p, conv_full, dn_A_log[0], dn_dt_bias[0],
        dn_out_norm_w[0], na_q_norm_w[0], na_k_norm_w[0], na_rpb[0], None, None, None,
        scatter_early=True, gather_late=late)
    loss = lax.psum(loss_part, ("x", "y", "c"))

    dm = jnp.concatenate([dmod_x, dmod_c.reshape(1, 6 * D), jnp.zeros((5, 6 * D), F32)], axis=0).reshape(48, D)
    misc = _misc_row(small["a_log"], small["dt_bias"], small["out_norm_w"], small["q_norm_w"], small["k_norm_w"])
    sm = jnp.concatenate([
        small["norm1_w"].reshape(-1), small["norm2_w"].reshape(-1), misc,
        small["rpb"].reshape(-1), jnp.zeros((4 * D - 3720,), F32),
        small["conv_w"].reshape(-1), jnp.zeros((8 * D - 7680,), F32), jnp.zeros((D,), F32)]).reshape(16, D)
    gsl, gtot = _allgather8(jnp.concatenate([dm, sm], axis=0), "gather_small", with_sum=True)
    gmod = gsl[:, :48].reshape(64, 6 * D)
    st = gtot[48:64].reshape(-1)
    g_n1, g_n2, g_misc = st[0:D].reshape(1, D), st[D:2 * D].reshape(1, D), st[2 * D:3 * D]
    g_rpb = st[3 * D:3 * D + 3720].reshape(1, 8, 15, 31)
    g_conv_full = st[7 * D:7 * D + 7680].reshape(5, 1536)
    g_conv_s = lax.dynamic_slice(g_conv_full, (0, chip * 384), (5, 384)).reshape(1, 5, 384)

    gmod_s = lax.dynamic_slice(gmod, (0, chip * 1536), (64, 1536))
    g_w_ada_s, pctx = _ada_bwd(cin, gmod_s, w_ada_s)
    parts = _allgather8(pctx, "gather_cctx")
    g_b_ada, g_c_ctx = _ada_small(gmod, parts, c_ctx.reshape(1, D))

    g_w_in8 = _unperm_w_in(g_w_in_p).reshape(4, 900, 2, D // 2).transpose(0, 2, 1, 3).reshape(8, 900, D // 2)
    names = ["w_in", "w_out", "w_ffn_in", "w_ffn_out"]
    early = _sibling_exchange([_sum_slots(r, "sum_" + nm)
                               for r, nm in zip([r_w_out, r_w_ffn_in, r_w_ffn_out], names[1:])], "sibling_early")
    g_w_out_s = early[0].reshape(1, D // 4, D)
    g_w_ffn_in_s = early[1].reshape(1, D, FFT)
    g_w_ffn_out_s = early[2].reshape(1, DFF // 4, D)
    two = lambda a: a.reshape(a.shape[1], a.shape[2])
    hosted_names = ["w_ada", "w_out", "w_ffn_in"]
    hosted_g = dict(w_ada=g_w_ada_s, w_out=two(g_w_out_s), w_ffn_in=two(g_w_ffn_in_s))
    hosted_w = dict(w_ada=(w_ada, m_w_ada, v_w_ada), w_out=(w_out, m_w_out, v_w_out), w_ffn_in=(w_ffn_in, m_w_ffn_in, v_w_ffn_in))
    hosted_res, (r_w_in,) = _adamw_hosting(
        [(two(hosted_w[k][0]), hosted_g[k], two(hosted_w[k][1]), two(hosted_w[k][2])) for k in hosted_names], [g_w_in8])
    (g_w_in_half,) = _sibling_exchange([_sum_slots(r_w_in, "sum_w_in")], "sibling_w_in")
    g_w_in_t = g_w_in_half.transpose(1, 0, 2).reshape(900, D)

    grads = dict(
        c_ctx=g_c_ctx.reshape(D), norm1_w=g_n1, norm2_w=g_n2, w_ada=g_w_ada_s.reshape(1, D, 1536), b_ada=g_b_ada,
        w_in=g_w_in_t, dn_conv_w=g_conv_s, dn_A_log=g_misc[0:8].reshape(1, 2, 4), dn_dt_bias=g_misc[8:16].reshape(1, 2, 4),
        dn_out_norm_w=g_misc[128:256].reshape(1, 128), na_q_norm_w=g_misc[256:320].reshape(1, 64),
        na_k_norm_w=g_misc[320:384].reshape(1, 64), na_rpb=g_rpb, w_out=g_w_out_s, w_ffn_in=g_w_ffn_in_s,
        w_ffn_out=g_w_ffn_out_s)
    pk = lambda d_: _pack_small(d_["c_ctx"], d_["norm1_w"], d_["norm2_w"], d_["b_ada"], d_["dn_A_log"], d_["dn_dt_bias"],
                                d_["dn_out_norm_w"], d_["na_q_norm_w"], d_["na_k_norm_w"], d_["na_rpb"], d_["dn_conv_w"])
    wts = dict(c_ctx=c_ctx, norm1_w=norm1_w, norm2_w=norm2_w, b_ada=b_ada, dn_A_log=dn_A_log, dn_dt_bias=dn_dt_bias,
               dn_out_norm_w=dn_out_norm_w, na_q_norm_w=na_q_norm_w, na_k_norm_w=na_k_norm_w, na_rpb=na_rpb, dn_conv_w=dn_conv_w)
    ms = dict(c_ctx=m_c_ctx, norm1_w=m_norm1_w, norm2_w=m_norm2_w, b_ada=m_b_ada, dn_A_log=m_dn_A_log, dn_dt_bias=m_dn_dt_bias,
              dn_out_norm_w=m_dn_out_norm_w, na_q_norm_w=m_na_q_norm_w, na_k_norm_w=m_na_k_norm_w, na_rpb=m_na_rpb,
              dn_conv_w=m_dn_conv_w)
    vs = dict(c_ctx=v_c_ctx, norm1_w=v_norm1_w, norm2_w=v_norm2_w, b_ada=v_b_ada, dn_A_log=v_dn_A_log, dn_dt_bias=v_dn_dt_bias,
              dn_out_norm_w=v_dn_out_norm_w, na_q_norm_w=v_na_q_norm_w, na_k_norm_w=v_na_k_norm_w, na_rpb=v_na_rpb,
              dn_conv_w=v_dn_conv_w)
    sd, snm, snv, _ = _adamw(pk(wts), pk(grads), pk(ms), pk(vs), "adamw_small")
    delta, new_m, new_v = _unpack_small(sd), _unpack_small(snm), _unpack_small(snv)
    for k, res in zip(hosted_names, hosted_res):
        shp = hosted_w[k][0].shape
        delta[k], new_m[k], new_v[k], grads[k] = [a.reshape(shp) for a in res]
    big = dict(w_in=(w_in, m_w_in, v_w_in), w_ffn_out=(w_ffn_out, m_w_ffn_out, v_w_ffn_out))
    for nm, (w_, m_, v_) in big.items():
        shp = w_.shape
        if nm == "w_in":
            r2, back = (lambda a: a[0].T), (lambda a: a.T.reshape(shp))
            res = _adamw(w_in_t, grads[nm], r2(m_), r2(v_), "adamw_" + nm)
        else:
            r2, back = (lambda a: a.reshape(shp[1], shp[2])), (lambda a: a.reshape(shp))
            res = _adamw(r2(w_), r2(grads[nm]), r2(m_), r2(v_), "adamw_" + nm)
        delta[nm], new_m[nm], new_v[nm], grads[nm] = [back(a) for a in res]

    return (loss, grad_x, *[grads[n] for n in _WEIGHTS], *[delta[n] for n in _WEIGHTS],
            *[new_m[n] for n in _WEIGHTS], *[new_v[n] for n in _WEIGHTS])
```

```python
import functools

import numpy as np
import jax
import jax.numpy as jnp
from jax import lax
from jax.experimental import pallas as pl
from jax.experimental.pallas import tpu as pltpu

F32 = jnp.float32
BF16 = jnp.bfloat16

D = 1024
NB = 2
SEQ = 2048
CT = 256
T = CT + SEQ
NTOK = NB * SEQ
CH = 64
NCH = T // CH
NCH_CTX = CT // CH
DNH, DNW = 4, 512
NAH, NAW = 8, 512
GRID_W = 64
ROWS = SEQ // GRID_W
WIN_ROWS = 8
WIN_COLS = 16
NWIN = WIN_ROWS * GRID_W
DFF = 2816
FFT = 1408
IN_COLS = 3600
PC = 3712
COL_Z, COL_NA, COL_G = 1536, 2048, 3584
EPS = 1e-6
NEG = -1e30
RT = 256
VMEM_BIG = 56 * 1024 * 1024
MESH = pl.DeviceIdType.MESH

ADAM_LR, ADAM_B1, ADAM_B2, ADAM_EPS, ADAM_WD, ADAM_STEP = 0.001, 0.9, 0.999, 1e-08, 0.01, 10


def _cp(vmem=None, sem=None):
    kw = {}
    if vmem is not None:
        kw["vmem_limit_bytes"] = vmem
    if sem is not None:
        kw["dimension_semantics"] = sem
    return pltpu.CompilerParams(**kw)


_NN = (((1,), (0,)), ((), ()))
_NT = (((1,), (1,)), ((), ()))
_TN = (((0,), (0,)), ((), ()))


def _dg(a, b, dims):
    return lax.dot_general(a.astype(BF16), b.astype(BF16), dims, preferred_element_type=F32)


@jax.custom_vjp
def _mm(a, b):
    return _dg(a, b, _NN)


_mm.defvjp(lambda a, b: (_dg(a, b, _NN), (a, b)),
           lambda r, g: (_dg(g, r[1], _NT), _dg(r[0], g, _TN)))


def _split_dot(a, b, dims):
    hi = b.astype(BF16)
    lo = (b - hi.astype(F32)).astype(BF16)
    ab = a.astype(BF16)
    return (lax.dot_general(ab, hi, dims, preferred_element_type=F32)
            + lax.dot_general(ab, lo, dims, preferred_element_type=F32))


@jax.custom_vjp
def _mask_mm(mask, b):
    return _split_dot(mask, b, _NN)


_mask_mm.defvjp(lambda mask, b: (_split_dot(mask, b, _NN), mask),
                lambda mask, g: (jnp.zeros_like(mask), _split_dot(mask, g, _TN)))


def _hmm(a, b):
    return jnp.dot(a, b, precision=lax.Precision.HIGHEST, preferred_element_type=F32)


def _sigmoid(x):
    return 1.0 / (1.0 + jnp.exp(-x))


def _silu(x):
    return x * _sigmoid(x)


def _norm_mod(x, w, sh, sc):
    r = lax.rsqrt(jnp.mean(x * x, axis=-1, keepdims=True) + EPS)
    return (x * r * w) * (1.0 + sc) + sh


def _my_pos():
    return lax.axis_index("x"), lax.axis_index("y"), lax.axis_index("c")


def _flip(pos, k):
    x, y, c = pos
    return (1 - x if k & 4 else x, 1 - y if k & 2 else y, 1 - c if k & 1 else c)


def _slot(pos):
    return 4 * pos[0] + 2 * pos[1] + pos[2]


def _allgather8(v, name, with_sum=False):
    R, W = v.shape

    def body(v_ref, out_ref, *rest):
        if with_sum:
            tot_ref, send_sems, recv_sems, lsem = rest
        else:
            send_sems, recv_sems, lsem = rest
        me = _my_pos()
        mine = pltpu.make_async_copy(v_ref, out_ref.at[_slot(me)], lsem)
        mine.start()
        sends = []
        for k in range(1, 8):
            cp = pltpu.make_async_remote_copy(
                src_ref=v_ref, dst_ref=out_ref.at[_slot(me)], send_sem=send_sems.at[k - 1],
                recv_sem=recv_sems.at[k - 1], device_id=_flip(me, k), device_id_type=MESH)
            cp.start()
            sends.append(cp)
        for k in range(1, 8):
            peer = _flip(me, k)
            pltpu.make_async_remote_copy(
                src_ref=v_ref, dst_ref=out_ref.at[_slot(peer)], send_sem=send_sems.at[k - 1],
                recv_sem=recv_sems.at[k - 1], device_id=peer, device_id_type=MESH).wait_recv()
        for cp in sends:
            cp.wait_send()
        mine.wait()
        if with_sum:
            acc = out_ref[0]
            for s in range(1, 8):
                acc = acc + out_ref[s]
            tot_ref[...] = acc

    out_shape = [jax.ShapeDtypeStruct((8, R, W), F32)]
    out_specs = [pl.BlockSpec(memory_space=pltpu.VMEM)]
    if with_sum:
        out_shape.append(jax.ShapeDtypeStruct((R, W), F32))
        out_specs.append(pl.BlockSpec(memory_space=pltpu.VMEM))
    res = pl.pallas_call(
        body, name=name, out_shape=out_shape,
        in_specs=[pl.BlockSpec(memory_space=pltpu.VMEM)], out_specs=out_specs,
        scratch_shapes=[pltpu.SemaphoreType.DMA((7,)), pltpu.SemaphoreType.DMA((7,)), pltpu.SemaphoreType.DMA],
        compiler_params=_cp(vmem=VMEM_BIG),
    )(v)
    return res if with_sum else res[0]


def _gather_rows(v, name):
    W = v.shape[1]

    def body(v_ref, out_ref, send_sems, recv_sems, lsem):
        pos = _my_pos()
        me = _slot(pos)
        rows = lambda p: v_ref.at[pl.ds(pl.multiple_of(8 * p, 8), 8)]
        locals_ = [(rows(me), out_ref.at[me])]
        xfers = [(rows(_slot(_flip(pos, k))), out_ref.at[me], out_ref.at[_slot(_flip(pos, k))], _flip(pos, k))
                 for k in range(1, 8)]
        _xfer_start((locals_, xfers), (send_sems, recv_sems, lsem))
        _xfer_wait((locals_, xfers), (send_sems, recv_sems, lsem))

    vm = pl.BlockSpec(memory_space=pltpu.VMEM)
    return pl.pallas_call(
        body, name=name, out_shape=jax.ShapeDtypeStruct((8, 8, W), F32), in_specs=[vm], out_specs=vm,
        scratch_shapes=_xfer_sems(1, 7),
    )(v)


def _gather_weights(halves):
    n = len(halves)

    def body(*refs):
        ins, outs = refs[:n], refs[n:2 * n]
        send_sems, recv_sems, lsems = refs[2 * n:]
        me = _my_pos()
        x, y, c = me
        sib = (x, y, 1 - c)
        chips = [4, 2, 6]

        def cp(a, k, block, to, src=None):
            dst = outs[a].at[_slot(block)]
            return pltpu.make_async_remote_copy(
                src_ref=dst if src is None else src, dst_ref=dst, send_sem=send_sems.at[a, k],
                recv_sem=recv_sems.at[a, k], device_id=to, device_id_type=MESH)

        mine, first, passed = [], [], []
        for a in range(n):
            m = pltpu.make_async_copy(ins[a], outs[a].at[_slot(me)], lsems.at[a])
            m.start()
            mine.append(m)
            first.append(cp(a, 0, me, sib, src=ins[a]))
            for j, k in enumerate(chips):
                first.append(cp(a, 1 + j, me, _flip(me, k), src=ins[a]))
        for f in first:
            f.start()
        for j, k in enumerate(chips):
            for a in range(n):
                cp(a, 1 + j, _flip(me, k), me).wait_recv()
                p = cp(a, 4 + j, _flip(me, k), sib)
                p.start()
                passed.append(p)
        for a in range(n):
            cp(a, 0, sib, me).wait_recv()
            for j, k in enumerate(chips):
                cp(a, 4 + j, _flip(sib, k), me).wait_recv()
        for f in first + passed:
            f.wait_send()
        for m in mine:
            m.wait()

    vm = pl.BlockSpec(memory_space=pltpu.VMEM)
    return pl.pallas_call(
        body, name="gather_weights",
        out_shape=[jax.ShapeDtypeStruct((8,) + h.shape, h.dtype) for h in halves],
        in_specs=[vm] * n, out_specs=[vm] * n,
        scratch_shapes=[pltpu.SemaphoreType.DMA((n, 7)), pltpu.SemaphoreType.DMA((n, 7)),
                        pltpu.SemaphoreType.DMA((n,))],
        compiler_params=_cp(vmem=VMEM_BIG),
    )(*halves)


def _xfer_start(plan, sems):
    send_sems, recv_sems, lsems = sems
    locals_, xfers = plan
    for i, (src, dst) in enumerate(locals_):
        pltpu.make_async_copy(src, dst, lsems.at[i]).start()
    for i, (src, dst, _, peer) in enumerate(xfers):
        pltpu.make_async_remote_copy(src_ref=src, dst_ref=dst, send_sem=send_sems.at[i], recv_sem=recv_sems.at[i],
                                     device_id=peer, device_id_type=MESH).start()


def _xfer_wait(plan, sems):
    send_sems, recv_sems, lsems = sems
    locals_, xfers = plan
    for i, (src, _, landing, peer) in enumerate(xfers):
        pltpu.make_async_remote_copy(src_ref=src, dst_ref=landing, send_sem=send_sems.at[i], recv_sem=recv_sems.at[i],
                                     device_id=peer, device_id_type=MESH).wait_recv()
    for i, (src, dst, _, peer) in enumerate(xfers):
        pltpu.make_async_remote_copy(src_ref=src, dst_ref=dst, send_sem=send_sems.at[i], recv_sem=recv_sems.at[i],
                                     device_id=peer, device_id_type=MESH).wait_send()
    for i, (src, dst) in enumerate(locals_):
        pltpu.make_async_copy(src, dst, lsems.at[i]).wait()


def _xfer_sems(n_local, n_xfer):
    return [pltpu.SemaphoreType.DMA((n_xfer,)), pltpu.SemaphoreType.DMA((n_xfer,)), pltpu.SemaphoreType.DMA((n_local,))]


def _scatter_plan(ins, outs, pos):
    me = _slot(pos)
    locals_, xfers = [], []
    for src, dst in zip(ins, outs):
        locals_.append((src.at[me], dst.at[me]))
        for k in range(1, 8):
            peer = _flip(pos, k)
            xfers.append((src.at[_slot(peer)], dst.at[me], dst.at[_slot(peer)], peer))
    return locals_, xfers


def _sibling_exchange(halves, name):
    n = len(halves)

    def body(*refs):
        ins, outs = refs[:n], refs[n:2 * n]
        send_sems, recv_sems, lsems = refs[2 * n:]
        x, y, c = _my_pos()
        sib = (x, y, 1 - c)
        mine, sends = [], []
        for a in range(n):
            m = pltpu.make_async_copy(ins[a], outs[a].at[c], lsems.at[a])
            m.start()
            mine.append(m)
            cp = pltpu.make_async_remote_copy(
                src_ref=ins[a], dst_ref=outs[a].at[c], send_sem=send_sems.at[a], recv_sem=recv_sems.at[a],
                device_id=sib, device_id_type=MESH)
            cp.start()
            sends.append(cp)
        for a in range(n):
            pltpu.make_async_remote_copy(
                src_ref=ins[a], dst_ref=outs[a].at[1 - c], send_sem=send_sems.at[a], recv_sem=recv_sems.at[a],
                device_id=sib, device_id_type=MESH).wait_recv()
        for cp in sends:
            cp.wait_send()
        for m in mine:
            m.wait()

    vm = pl.BlockSpec(memory_space=pltpu.VMEM)
    return pl.pallas_call(
        body, name=name,
        out_shape=[jax.ShapeDtypeStruct((2,) + h.shape, h.dtype) for h in halves],
        in_specs=[vm] * n, out_specs=[vm] * n,
        scratch_shapes=[pltpu.SemaphoreType.DMA((n,)), pltpu.SemaphoreType.DMA((n,)), pltpu.SemaphoreType.DMA((n,))],
        compiler_params=_cp(vmem=VMEM_BIG),
    )(*halves)


def _row_tile(r, cap=512):
    for t in (cap, 256, 128, 64, 32, 16, 8):
        if t <= cap and r % t == 0:
            return t
    return r


def _cast_bf16(a, name):
    R, W = a.shape
    tr = _row_tile(R, 256)

    def body(a_ref, o_ref):
        o_ref[...] = a_ref[...].astype(BF16)

    return pl.pallas_call(
        body, name=name, grid=(R // tr,), out_shape=jax.ShapeDtypeStruct((R, W), BF16),
        in_specs=[pl.BlockSpec((tr, W), lambda i: (i, 0))], out_specs=pl.BlockSpec((tr, W), lambda i: (i, 0)),
    )(a)


def _sum_slots(r, name):
    _, R, W = r.shape
    tr = _row_tile(R, 128)

    def body(r_ref, o_ref):
        acc = r_ref[0].astype(F32)
        for s_ in range(1, 8):
            acc = acc + r_ref[s_].astype(F32)
        o_ref[...] = acc.astype(BF16)

    return pl.pallas_call(
        body, name=name, grid=(R // tr,), out_shape=jax.ShapeDtypeStruct((R, W), BF16),
        in_specs=[pl.BlockSpec((8, tr, W), lambda i: (0, i, 0))], out_specs=pl.BlockSpec((tr, W), lambda i: (i, 0)),
        compiler_params=_cp(vmem=VMEM_BIG),
    )(r)


def _adamw(w, g, m, v, name):
    R, W = w.shape
    tr = _row_tile(R, 256)
    tc = 256 if (tr == R and R * W * 4 > (1 << 21) and W % 256 == 0) else W
    c1 = 1.0 / (1.0 - ADAM_B1 ** ADAM_STEP)
    c2 = 1.0 / (1.0 - ADAM_B2 ** ADAM_STEP)

    def body(w_ref, g_ref, m_ref, v_ref, d_ref, nm_ref, nv_ref, g32_ref):
        gg = g_ref[...].astype(F32)
        g32_ref[...] = gg
        nm = ADAM_B1 * m_ref[...] + (1.0 - ADAM_B1) * gg
        nv = ADAM_B2 * v_ref[...] + (1.0 - ADAM_B2) * (gg * gg)
        d_ref[...] = -ADAM_LR * ((nm * c1) / (jnp.sqrt(nv * c2) + ADAM_EPS) + ADAM_WD * w_ref[...])
        nm_ref[...] = nm
        nv_ref[...] = nv

    spec = pl.BlockSpec((tr, tc), lambda i, j: (i, j))
    return pl.pallas_call(
        body, name=name, grid=(R // tr, W // tc), out_shape=[jax.ShapeDtypeStruct((R, W), F32)] * 4,
        in_specs=[spec] * 4, out_specs=[spec] * 4, compiler_params=_cp(vmem=VMEM_BIG),
    )(w, g, m, v)


def _adamw_hosting(items, carry):
    n, nc, steps = len(items), len(carry), 8
    c1 = 1.0 / (1.0 - ADAM_B1 ** ADAM_STEP)
    c2 = 1.0 / (1.0 - ADAM_B2 ** ADAM_STEP)

    def body(*refs):
        ins, c_in = refs[:4 * n], refs[4 * n:4 * n + nc]
        outs, c_out = refs[4 * n + nc:8 * n + nc], refs[8 * n + nc:8 * n + 2 * nc]
        sems = refs[8 * n + 2 * nc:]
        if nc:
            @pl.when(pl.program_id(0) == 0)
            def _():
                _xfer_start(_scatter_plan(c_in, c_out, _my_pos()), sems)

        for a in range(n):
            w_ref, g_ref, m_ref, v_ref = ins[4 * a:4 * a + 4]
            d_ref, nm_ref, nv_ref, g32_ref = outs[4 * a:4 * a + 4]
            gg = g_ref[...].astype(F32)
            g32_ref[...] = gg
            nm = ADAM_B1 * m_ref[...] + (1.0 - ADAM_B1) * gg
            nv = ADAM_B2 * v_ref[...] + (1.0 - ADAM_B2) * (gg * gg)
            d_ref[...] = -ADAM_LR * ((nm * c1) / (jnp.sqrt(nv * c2) + ADAM_EPS) + ADAM_WD * w_ref[...])
            nm_ref[...] = nm
            nv_ref[...] = nv

        if nc:
            @pl.when(pl.program_id(0) == steps - 1)
            def _():
                _xfer_wait(_scatter_plan(c_in, c_out, _my_pos()), sems)

    specs = [pl.BlockSpec((w.shape[0] // steps, w.shape[1]), lambda i: (i, 0)) for w, _, _, _ in items]
    hbm = pl.BlockSpec(memory_space=pltpu.HBM)
    res = pl.pallas_call(
        body, name="adamw_hosting", grid=(steps,),
        out_shape=[jax.ShapeDtypeStruct(w.shape, F32) for w, _, _, _ in items for _ in range(4)]
        + [jax.ShapeDtypeStruct(g.shape, g.dtype) for g in carry],
        in_specs=[sp for sp in specs for _ in range(4)] + [hbm] * nc,
        out_specs=[sp for sp in specs for _ in range(4)] + [hbm] * nc,
        scratch_shapes=_xfer_sems(nc, 7 * nc) if nc else [],
        compiler_params=_cp(vmem=VMEM_BIG),
    )(*[a for it in items for a in it], *carry)
    return [res[4 * a:4 * a + 4] for a in range(n)], list(res[4 * n:])


def _matmul_tn(a, b, tm, tn, tk, name, nn=1, b_map=None, out_shape=None, out_block=None, out_map=None,
               b_block=None):
    K, M = a.shape
    nk = K // tk
    nm = M // tm

    def body(a_ref, b_ref, o_ref, acc_ref):
        kk = pl.program_id(2)

        @pl.when(kk == 0)
        def _():
            acc_ref[...] = jnp.zeros_like(acc_ref)

        acc_ref[...] += lax.dot_general(a_ref[...], b_ref[...], _TN, preferred_element_type=F32)

        @pl.when(kk == nk - 1)
        def _():
            o_ref[...] = acc_ref[...].astype(BF16)

    return pl.pallas_call(
        body, name=name, grid=(nm, nn, nk),
        out_shape=jax.ShapeDtypeStruct(out_shape or (M, b.shape[-1]), BF16),
        in_specs=[pl.BlockSpec((tk, tm), lambda i, j, k: (k, i)),
                  pl.BlockSpec(b_block or (tk, tn), b_map or (lambda i, j, k: (k, j)))],
        out_specs=pl.BlockSpec(out_block or (tm, tn), out_map or (lambda i, j, k: (i, j))),
        scratch_shapes=[pltpu.VMEM((tm, tn), F32)],
        compiler_params=_cp(vmem=VMEM_BIG),
    )(a, b)


def _ada_fwd(cin, w_ada_s, b_s):
    tn = 512

    def body(c_ref, w_ref, b_ref, o_ref):
        o_ref[...] = _dg(_silu(c_ref[...]), w_ref[...], _NN) + b_ref[...]

    return pl.pallas_call(
        body, name="ada_fwd", grid=(1536 // tn,), out_shape=jax.ShapeDtypeStruct((64, 1536), F32),
        in_specs=[pl.BlockSpec((64, D), lambda j: (0, 0)), pl.BlockSpec((D, tn), lambda j: (0, j)),
                  pl.BlockSpec((1, tn), lambda j: (0, j))],
        out_specs=pl.BlockSpec((64, tn), lambda j: (0, j)),
        compiler_params=_cp(vmem=VMEM_BIG),
    )(cin, w_ada_s, b_s)


def _ada_bwd(cin, gmod_s, w_ada_s):
    tn = 512

    def body(c_ref, g_ref, w_ref, gw_ref, pc_ref):
        j = pl.program_id(0)
        g = g_ref[...]
        gw_ref[...] = _dg(_silu(c_ref[...]), g, _TN)
        rr = lax.broadcasted_iota(jnp.int32, (8, 64), 0)
        cc = lax.broadcasted_iota(jnp.int32, (8, 64), 1)
        sel = jnp.where((rr == 0) & ((cc & 7) == 2), 1.0, 0.0).astype(F32)
        dctx = _hmm(sel, g)
        part = _dg(dctx, w_ref[...], _NT)

        @pl.when(j == 0)
        def _():
            pc_ref[...] = jnp.zeros_like(pc_ref)

        pc_ref[...] += part

    return pl.pallas_call(
        body, name="ada_bwd", grid=(1536 // tn,),
        out_shape=[jax.ShapeDtypeStruct((D, 1536), F32), jax.ShapeDtypeStruct((8, D), F32)],
        in_specs=[pl.BlockSpec((64, D), lambda j: (0, 0)), pl.BlockSpec((64, tn), lambda j: (0, j)),
                  pl.BlockSpec((D, tn), lambda j: (0, j))],
        out_specs=[pl.BlockSpec((D, tn), lambda j: (0, j)), pl.BlockSpec((8, D), lambda j: (0, 0))],
        compiler_params=_cp(vmem=VMEM_BIG),
    )(cin, gmod_s, w_ada_s)


def _ada_small(gmod, parts, c_ctx):
    def body(g_ref, p_ref, c_ref, gb_ref, gc_ref):
        gb_ref[...] = jnp.sum(g_ref[...], axis=0, keepdims=True)
        ds = p_ref[0][0:1] + p_ref[2][0:1] + p_ref[4][0:1] + p_ref[6][0:1]
        cc = c_ref[...]
        s = _sigmoid(cc)
        gc_ref[...] = ds * (s * (1.0 + cc * (1.0 - s)))

    vm = pl.BlockSpec(memory_space=pltpu.VMEM)
    return pl.pallas_call(
        body, name="ada_small",
        out_shape=[jax.ShapeDtypeStruct((1, 6 * D), F32), jax.ShapeDtypeStruct((1, D), F32)],
        in_specs=[vm, vm, vm], out_specs=[vm, vm],
    )(gmod, parts, c_ctx)


def _tok_specs():
    xs = pl.BlockSpec((None, RT, D), lambda b, j: (b, jnp.maximum(j - 1, 0), 0))
    cs = pl.BlockSpec((None, RT, D), lambda b, j: (b, 0, 0))
    ms = pl.BlockSpec((None, 1, D), lambda b, j: (b, 0, 0))
    cms = pl.BlockSpec((None, 1, D), lambda b, j: (0, 0, 0))
    return xs, cs, ms, cms


def _inproj_fwd(x, ctx, sh1, sc1, csh1, csc1, n1w, w_in_p):
    def body(x_ref, c_ref, sh_ref, sc_ref, csh_ref, csc_ref, w1_ref, w_ref, p_ref, hm_ref):
        is_ctx = pl.program_id(1) == 0
        xt = jnp.where(is_ctx, c_ref[...], x_ref[...])
        sh = jnp.where(is_ctx, csh_ref[...], sh_ref[...])
        sc = jnp.where(is_ctx, csc_ref[...], sc_ref[...])
        hb = _norm_mod(xt, w1_ref[...], sh, sc).astype(BF16)
        hm_ref[...] = hb
        p_ref[...] = lax.dot_general(hb, w_ref[...], _NT, preferred_element_type=F32)

    xs, cs, ms, cms = _tok_specs()
    return pl.pallas_call(
        body, name="inproj_fwd", grid=(NB, T // RT),
        out_shape=[jax.ShapeDtypeStruct((NB, T, PC), F32), jax.ShapeDtypeStruct((NB, T, D), BF16)],
        in_specs=[xs, cs, ms, ms, cms, cms, pl.BlockSpec((1, D), lambda b, j: (0, 0)),
                  pl.BlockSpec((PC, D), lambda b, j: (0, 0))],
        out_specs=[pl.BlockSpec((None, RT, PC), lambda b, j: (b, j, 0)),
                   pl.BlockSpec((None, RT, D), lambda b, j: (b, j, 0))],
        compiler_params=_cp(vmem=VMEM_BIG),
    )(x, ctx, sh1, sc1, csh1, csc1, n1w, w_in_p)


def _inproj_bwd(dp_dn, dz, dp_na, dgt, w_in_p, x, ctx, sh1, sc1, csh1, csc1, n1w, dx1):
    def body(d1q_ref, d1k_ref, d1v_ref, d2_ref, d3_ref, d4_ref, w_ref, x_ref, c_ref, sh_ref, sc_ref, csh_ref, csc_ref,
             w1_ref, dx1_ref, dpb_ref, gx_ref, dsh_ref, dsc_ref, dcsh_ref, dcsc_ref, dw1_ref):
        b, j = pl.program_id(0), pl.program_id(1)
        is_ctx = j == 0
        dp = jnp.concatenate([d1q_ref[...], d1k_ref[...], d1v_ref[...], d2_ref[...], d3_ref[...], d4_ref[...]],
                             axis=1).astype(BF16)
        dpb_ref[...] = dp
        dhm = jnp.dot(dp, w_ref[...], preferred_element_type=F32)
        xt = jnp.where(is_ctx, c_ref[...], x_ref[...])
        sh = jnp.where(is_ctx, csh_ref[...], sh_ref[...])
        sc = jnp.where(is_ctx, csc_ref[...], sc_ref[...])
        _, vjp = jax.vjp(_norm_mod, xt, w1_ref[...], sh, sc)
        dxt, dw1, dsh, dsc = vjp(dhm)

        @pl.when((b == 0) & (j == 0))
        def _():
            dw1_ref[...] = jnp.zeros_like(dw1_ref)
            dcsh_ref[...] = jnp.zeros_like(dcsh_ref)
            dcsc_ref[...] = jnp.zeros_like(dcsc_ref)

        dw1_ref[...] += dw1

        @pl.when(is_ctx)
        def _():
            dcsh_ref[...] += dsh
            dcsc_ref[...] += dsc

        @pl.when(j == 1)
        def _():
            dsh_ref[...] = jnp.zeros_like(dsh_ref)
            dsc_ref[...] = jnp.zeros_like(dsc_ref)

        @pl.when(j >= 1)
        def _():
            dsh_ref[...] += dsh
            dsc_ref[...] += dsc
            gx_ref[...] = dx1_ref[...] + dxt

    xs, cs, ms, cms = _tok_specs()
    tsp = lambda w: pl.BlockSpec((None, RT, w), lambda b, j: (b, j, 0))
    one = pl.BlockSpec((1, D), lambda b, j: (0, 0))
    return pl.pallas_call(
        body, name="inproj_bwd", grid=(NB, T // RT),
        out_shape=[jax.ShapeDtypeStruct((NB, T, PC), BF16), jax.ShapeDtypeStruct((NB, SEQ, D), F32),
                   jax.ShapeDtypeStruct((NB, 1, D), F32), jax.ShapeDtypeStruct((NB, 1, D), F32),
                   jax.ShapeDtypeStruct((1, 1, D), F32), jax.ShapeDtypeStruct((1, 1, D), F32),
                   jax.ShapeDtypeStruct((1, D), F32)],
        in_specs=[tsp(512), tsp(512), tsp(512), tsp(512), tsp(1536), tsp(128), pl.BlockSpec((PC, D), lambda b, j: (0, 0)),
                  xs, cs, ms, ms, cms, cms, one, _rows2d_spec(D)],
        out_specs=[tsp(PC), xs, ms, ms, cms, cms, one],
        compiler_params=_cp(vmem=VMEM_BIG),
    )(*dp_dn, dz, dp_na, dgt, w_in_p, x, ctx, sh1, sc1, csh1, csc1, n1w, dx1)


def _shift_masks():
    t = lax.broadcasted_iota(jnp.int32, (T, 1), 0)
    lo = jnp.where(t < CT, 0, CT)
    hi = jnp.where(t < CT, CT, T)
    return {off: (t + off >= lo) & (t + off < hi) for off in range(-2, 3)}


def _shifted(u, off, masks):
    s = u if off == 0 else pltpu.roll(u, (-off) % T, 0)
    return jnp.where(masks[off], s, 0.0)


def _conv_pre(u, w, masks):
    acc = _shifted(u, -2, masks) * w[0:1, :]
    for kk in range(1, 5):
        acc = acc + _shifted(u, kk - 2, masks) * w[kk:kk + 1, :]
    return acc


def _act_q(pre):
    a = _silu(pre)
    return a * lax.rsqrt(jnp.sum(a * a, axis=-1, keepdims=True) + EPS) * (128.0 ** -0.5)


def _act_k(pre):
    a = _silu(pre)
    return a * lax.rsqrt(jnp.sum(a * a, axis=-1, keepdims=True) + EPS)


def _dn_prep_fwd(p, cw):
    def body(pq_ref, pk_ref, pv_ref, wq_ref, wk_ref, wv_ref, q_ref, k_ref, v_ref):
        masks = _shift_masks()
        q_ref[...] = _act_q(_conv_pre(pq_ref[...], wq_ref[...], masks))
        k_ref[...] = _act_k(_conv_pre(pk_ref[...], wk_ref[...], masks))
        v_ref[...] = _silu(_conv_pre(pv_ref[...], wv_ref[...], masks))

    ps = lambda o: pl.BlockSpec((None, T, 128), lambda b, h: (b, 0, o + h))
    ws = lambda o: pl.BlockSpec((8, 128), lambda b, h: (0, o + h))
    os_ = pl.BlockSpec((None, T, 128), lambda b, h: (b, 0, h))
    return pl.pallas_call(
        body, name="dn_prep_fwd", grid=(NB, DNH), out_shape=[jax.ShapeDtypeStruct((NB, T, DNW), F32)] * 3,
        in_specs=[ps(0), ps(4), ps(8), ws(0), ws(4), ws(8)], out_specs=[os_] * 3,
        compiler_params=_cp(vmem=VMEM_BIG),
    )(p, p, p, cw, cw, cw)


def _dn_prep_bwd(p, cw, dqs, dks, dvs):
    def body(pq_ref, pk_ref, pv_ref, wq_ref, wk_ref, wv_ref, dqf_ref, dqb_ref, dkf_ref, dkb_ref, dvf_ref, dvb_ref,
             oq_ref, ok_ref, ov_ref, gq_ref, gk_ref, gv_ref):
        b = pl.program_id(1)
        masks = _shift_masks()

        def one(p_ref, w_ref, df_ref, db_ref, act, o_ref, g_ref):
            u = p_ref[...]
            w = w_ref[...]
            _, vjp = jax.vjp(act, _conv_pre(u, w, masks))
            (dpre,) = vjp(df_ref[...] + db_ref[...])
            du = _shifted(dpre, 2, masks) * w[0:1, :]
            rows = [jnp.sum(dpre * _shifted(u, -2, masks), axis=0, keepdims=True)]
            for kk in range(1, 5):
                du = du + _shifted(dpre, 2 - kk, masks) * w[kk:kk + 1, :]
                rows.append(jnp.sum(dpre * _shifted(u, kk - 2, masks), axis=0, keepdims=True))
            o_ref[...] = du
            gw = jnp.concatenate(rows + [jnp.zeros((3, 128), F32)], axis=0)

            @pl.when(b == 0)
            def _():
                g_ref[...] = gw

            @pl.when(b > 0)
            def _():
                g_ref[...] += gw

        one(pq_ref, wq_ref, dqf_ref, dqb_ref, _act_q, oq_ref, gq_ref)
        one(pk_ref, wk_ref, dkf_ref, dkb_ref, _act_k, ok_ref, gk_ref)
        one(pv_ref, wv_ref, dvf_ref, dvb_ref, _silu, ov_ref, gv_ref)

    ps = lambda o: pl.BlockSpec((None, T, 128), lambda h, b: (b, 0, o + h))
    ws = lambda o: pl.BlockSpec((8, 128), lambda h, b: (0, o + h))
    ds = pl.BlockSpec((None, T, 128), lambda h, b: (b, 0, h))
    outs = pl.pallas_call(
        body, name="dn_prep_bwd", grid=(DNH, NB),
        out_shape=[jax.ShapeDtypeStruct((NB, T, DNW), F32)] * 3 + [jax.ShapeDtypeStruct((8, DNW), F32)] * 3,
        in_specs=[ps(0), ps(4), ps(8), ws(0), ws(4), ws(8)] + [ds] * 6,
        out_specs=[ds] * 3
        + [pl.BlockSpec((8, 128), lambda h, b: (0, h))] * 3,
        compiler_params=_cp(vmem=VMEM_BIG),
    )(p, p, p, cw, cw, cw, *dqs, *dks, *dvs)
    return outs


def _gate_fn(a, prm):
    lane = lax.broadcasted_iota(jnp.int32, a.shape, 1)
    z = a + prm[1:2, :]
    sp = jnp.maximum(z, 0.0) + jnp.log(1.0 + jnp.exp(-jnp.abs(z)))
    g = -jnp.exp(prm[0:1, :]) * sp
    return jnp.where(lane < 8, g, jnp.where(lane < 16, _sigmoid(a), 0.0))


def _gates_fwd(p, prm):
    def body(a_ref, prm_ref, o_ref):
        o_ref[...] = _gate_fn(a_ref[...], prm_ref[...])

    return pl.pallas_call(
        body, name="gates_fwd", grid=(NB,), out_shape=jax.ShapeDtypeStruct((NB, T, 128), F32),
        in_specs=[pl.BlockSpec((None, T, 128), lambda b: (b, 0, COL_G // 128)), pl.BlockSpec((8, 128), lambda b: (0, 0))],
        out_specs=pl.BlockSpec((None, T, 128), lambda b: (b, 0, 0)),
    )(p, prm)


def _gates_bwd(p, prm, dgf, dgb):
    def body(a_ref, prm_ref, dgf_ref, dgb_ref, o_ref, dprm_ref):
        b = pl.program_id(0)
        tot = dgf_ref[...] + dgb_ref[...]
        _, vjp = jax.vjp(_gate_fn, a_ref[...], prm_ref[...])
        da, dprm = vjp(tot)
        o_ref[...] = da

        @pl.when(b == 0)
        def _():
            dprm_ref[...] = dprm

        @pl.when(b > 0)
        def _():
            dprm_ref[...] += dprm

    return pl.pallas_call(
        body, name="gates_bwd", grid=(NB,),
        out_shape=[jax.ShapeDtypeStruct((NB, T, 128), F32), jax.ShapeDtypeStruct((8, 128), F32)],
        in_specs=[pl.BlockSpec((None, T, 128), lambda b: (b, 0, COL_G // 128)), pl.BlockSpec((8, 128), lambda b: (0, 0)),
                  pl.BlockSpec((None, T, 128), lambda b: (b, 0, 0)), pl.BlockSpec((None, T, 128), lambda b: (b, 0, 0))],
        out_specs=[pl.BlockSpec((None, T, 128), lambda b: (b, 0, 0)), pl.BlockSpec((8, 128), lambda b: (0, 0))],
        compiler_params=_cp(vmem=VMEM_BIG),
    )(p, prm, dgf, dgb)


def _lockstep(gens, joint=None):
    n = len(gens)
    res, vals = [None] * n, [None] * n
    while True:
        msgs = []
        for i, g in enumerate(gens):
            try:
                msgs.append((g.send(vals[i]), False))
            except StopIteration as e:
                res[i] = e.value
                msgs.append((None, True))
        if msgs[0][1]:
            return res
        if msgs[0][0] is not None:
            vals = list(joint(tuple(m[0] for m in msgs)))
        else:
            vals = [None] * n


def _tri_inv_gen(m, md, m1, m2, eye):
    n = -(m * md)
    t = eye + n
    pw = n
    for _ in range(3):
        pw = _dg(pw, pw, _NN)
        yield
        t = t + _dg(t, pw, _NN)
        yield
    a = _dg(t, m * m1, _NN)
    yield
    t = t - _dg(a, t, _NN)
    yield
    a = _dg(t, m * m2, _NN)
    yield
    return t - _dg(a, t, _NN)


def _tri_inv_bwd_gen(t, g):
    tt = t.T
    a = _dg(tt, g, _NN)
    yield
    return -_dg(a, tt, _NN)


@jax.custom_vjp
def _tri_inv(ms, md, m1, m2, eye):
    return tuple(_lockstep([_tri_inv_gen(m, md, m1, m2, eye) for m in ms]))


def _tri_inv_f(ms, md, m1, m2, eye):
    ts = _tri_inv(ms, md, m1, m2, eye)
    return ts, (ts, md)


def _tri_inv_b(res, gs):
    ts, md = res
    z = jnp.zeros_like(md)
    return (tuple(_lockstep([_tri_inv_bwd_gen(t, g) for t, g in zip(ts, gs)])), z, z, z, z)


_tri_inv.defvjp(_tri_inv_f, _tri_inv_b)


@jax.custom_vjp
def _tri_inv_saved(ms, ts):
    return tuple(t.astype(F32) for t in ts)


def _tri_inv_saved_f(ms, ts):
    return _tri_inv_saved(ms, ts), ts


def _tri_inv_saved_b(ts, gs):
    dms = tuple(_lockstep([_tri_inv_bwd_gen(t, g) for t, g in zip(ts, gs)]))
    return dms, tuple(jnp.zeros_like(t) for t in ts)


_tri_inv_saved.defvjp(_tri_inv_saved_f, _tri_inv_saved_b)


SR = DNH * CH


def _scan_masks():
    i, j = np.arange(SR)[:, None], np.arange(SR)[None, :]
    same = (i >> 6) == (j >> 6)
    cm = np.zeros((2, 6, SR, SR), np.float32)
    for d in range(2):
        diff = i - j if d == 0 else j - i
        cm[d, 0] = same & (diff >= 0)
        cm[d, 1] = same & (diff > 0)
        cm[d, 2] = (i >> 4) == (j >> 4)
        cm[d, 3] = ((i >> 5) == (j >> 5)) & ((i >> 4) != (j >> 4))
        cm[d, 4] = (i >> 5) != (j >> 5)
        cm[d, 5] = i == j
    t, u = np.arange(CH)[:, None], np.arange(CH)[None, :]
    c64 = np.stack([t >= u, t <= u]).astype(np.float32)
    return jnp.asarray(cm), jnp.asarray(c64)


def _hsl(h):
    return slice(CH * h, CH * h + CH), slice(128 * h, 128 * h + 128)


@jax.custom_vjp
def _heads_mm(x, s):
    return jnp.concatenate([_dg(x[_hsl(h)[0]], s[:, _hsl(h)[1]], _NN) for h in range(DNH)], axis=0)


def _heads_mm_b(res, g):
    x, s = res
    dx = jnp.concatenate([_dg(g[_hsl(h)[0]], s[:, _hsl(h)[1]], _NT) for h in range(DNH)], axis=0)
    ds = jnp.concatenate([_dg(x[_hsl(h)[0]], g[_hsl(h)[0]], _TN) for h in range(DNH)], axis=1)
    return dx, ds


_heads_mm.defvjp(lambda x, s: (_heads_mm(x, s), (x, s)), _heads_mm_b)


@jax.custom_vjp
def _heads_outer(k, v):
    return jnp.concatenate([_dg(k[_hsl(h)[0]], v[_hsl(h)[0]], _TN) for h in range(DNH)], axis=1)


def _heads_outer_b(res, g):
    k, v = res
    dk = jnp.concatenate([_dg(v[_hsl(h)[0]], g[:, _hsl(h)[1]], _NT) for h in range(DNH)], axis=0)
    dv = jnp.concatenate([_dg(k[_hsl(h)[0]], g[:, _hsl(h)[1]], _NN) for h in range(DNH)], axis=0)
    return dk, dv


_heads_outer.defvjp(lambda k, v: (_heads_outer(k, v), (k, v)), _heads_outer_b)


@jax.custom_vjp
def _mm_shared_lhs(t, a, b):
    r = _dg(t, jnp.concatenate([a, b], axis=1), _NN)
    return r[:, :a.shape[1]], r[:, a.shape[1]:]


def _mm_shared_lhs_b(res, gs):
    t, a, b = res
    g = jnp.concatenate(gs, axis=1)
    dab = _dg(t, g, _TN)
    return _dg(g, jnp.concatenate([a, b], axis=1), _NT), dab[:, :a.shape[1]], dab[:, a.shape[1]:]


_mm_shared_lhs.defvjp(lambda t, a, b: (_mm_shared_lhs(t, a, b), (t, a, b)), _mm_shared_lhs_b)


@jax.custom_vjp
def _mm_nt_shared_rhs(a, b, k):
    r = _dg(jnp.concatenate([a, b], axis=0), k, _NT)
    return r[:a.shape[0]], r[a.shape[0]:]


def _mm_nt_shared_rhs_b(res, gs):
    a, b, k = res
    g = jnp.concatenate(gs, axis=0)
    dab = _dg(g, k, _NN)
    return dab[:a.shape[0]], dab[a.shape[0]:], _dg(g, jnp.concatenate([a, b], axis=0), _TN)


_mm_nt_shared_rhs.defvjp(lambda a, b, k: (_mm_nt_shared_rhs(a, b, k), (a, b, k)), _mm_nt_shared_rhs_b)


def _chunk_gen(q, k, v, gb, s, cm, c64, d):
    incl, strict = cm[0], cm[1]
    lane = lax.broadcasted_iota(jnp.int32, (1, 128), 1)
    oh = lambda n: jnp.where(lane == n, 1.0, 0.0).astype(F32)
    col = lambda x, base: jnp.concatenate(
        [jnp.sum(x * oh(base + d * DNH + h), axis=1, keepdims=True) for h in range(DNH)], axis=0)
    beta = col(gb, 8)
    gam = col(_mask_mm(c64, gb), 0)
    yield
    tot = jnp.sum(gb, axis=0, keepdims=True)
    gtots = [jnp.sum(tot * oh(d * DNH + h), axis=1, keepdims=True) for h in range(DNH)]
    gtot = jnp.concatenate([jnp.broadcast_to(t, (CH, 1)) for t in gtots], axis=0)
    eg = jnp.concatenate([jnp.broadcast_to(jnp.exp(t), (1, 128)) for t in gtots], axis=1)
    row0 = jnp.where(lax.broadcasted_iota(jnp.int32, (128, 1), 0) == 0, 1.0, 0.0).astype(F32)
    gam_row = jnp.sum(jnp.broadcast_to(gam, (SR, 128)).T * row0, axis=0, keepdims=True)
    dec = jnp.exp(jnp.where(incl > 0.5, gam - gam_row, NEG))
    e_gam = jnp.exp(gam)
    kb = k * beta
    kk, qk = _mm_nt_shared_rhs(kb, q, k)
    yield
    tm = yield strict * (kk * dec)
    u, w = _mm_shared_lhs(tm, v * beta, kb * e_gam)
    yield
    qk = qk * dec
    ws = _heads_mm(w, s)
    yield
    v_new = u - ws
    qs = _heads_mm(q * e_gam, s)
    yield
    o = qs + _mm(qk, v_new)
    yield
    s_new = s * eg + _heads_outer(k * jnp.exp(gtot - gam), v_new)
    return o, s_new, tm


def _chunks(args, cm_ref, c64_ref, saved=None):
    cm0 = cm_ref[0]
    if saved is None:
        inv = lambda ms: _tri_inv(ms, cm0[2], cm0[3], cm0[4], cm0[5])
    else:
        inv = lambda ms: _tri_inv_saved(ms, saved)
    return _lockstep([_chunk_gen(q, k, v, gb, s, cm_ref[d], c64_ref[d], d) for q, k, v, gb, s, d in args], inv)


def _chunk_rev(s):
    return jnp.where(s < NCH_CTX, NCH_CTX - 1 - s, NCH + NCH_CTX - 1 - s)


def _scan_specs(step):
    cf = lambda s: step(s)
    cb = lambda s: _chunk_rev(step(s))
    tok = lambda c: pl.BlockSpec((NB, CH, DNW), lambda s: (0, c(s), 0))
    gat = lambda c: pl.BlockSpec((NB, CH, 128), lambda s: (0, c(s), 0))
    sta = lambda c: pl.BlockSpec((NB, None, 128, DNW), lambda s: (0, c(s), 0, 0))
    inv = lambda c: pl.BlockSpec((NB, None, SR, SR), lambda s: (0, c(s), 0, 0))
    return (tok(cf), gat(cf), sta(cf), inv(cf)), (tok(cb), gat(cb), sta(cb), inv(cb))


def _const_specs():
    return [pl.BlockSpec((2, 6, SR, SR), lambda s: (0, 0, 0, 0)), pl.BlockSpec((2, CH, CH), lambda s: (0, 0, 0))]


SCAN_CHAINS = [(b, d) for b in range(NB) for d in range(2)]


def _stack(ref, b):
    return jnp.concatenate([ref[b, :, 128 * h:128 * h + 128] for h in range(DNH)], axis=0)


def _unstack_to(ref, b, val):
    for h in range(DNH):
        ref[b, :, 128 * h:128 * h + 128] = val[CH * h:CH * h + CH]


def _gather_plan(ins, outs, pos):
    me = _slot(pos)
    locals_, xfers = [], []
    for src, dst in zip(ins, outs):
        locals_.append((src, dst.at[me]))
        for k in range(1, 8):
            peer = _flip(pos, k)
            xfers.append((src, dst.at[me], dst.at[_slot(peer)], peer))
    return locals_, xfers


def _scan_fwd(q, k, v, gates, carry=()):
    nc = len(carry)

    def body(*refs):
        cm_ref, c64_ref, qf, kf, vf, gf, qb, kb, vb, gb = refs[:10]
        c_in = refs[10:10 + nc]
        of_ref, sf_ref, tf_ref, ob_ref, sb_ref, tb_ref = refs[10 + nc:16 + nc]
        c_out = refs[16 + nc:16 + 2 * nc]
        s_ref = refs[16 + 2 * nc]
        sems = refs[17 + 2 * nc:]
        if nc:
            @pl.when(pl.program_id(0) == 0)
            def _():
                _xfer_start(_gather_plan(c_in, c_out, _my_pos()), sems)

        @pl.when(pl.program_id(0) == 0)
        def _():
            s_ref[...] = jnp.zeros_like(s_ref)

        dirs = ((qf, kf, vf, gf, of_ref, sf_ref, tf_ref), (qb, kb, vb, gb, ob_ref, sb_ref, tb_ref))
        args = [(_stack(dirs[d][0], b), _stack(dirs[d][1], b), _stack(dirs[d][2], b), dirs[d][3][b], s_ref[i], d)
                for i, (b, d) in enumerate(SCAN_CHAINS)]
        for i, (o, s_new, tm) in enumerate(_chunks(args, cm_ref, c64_ref)):
            b, d = SCAN_CHAINS[i]
            dirs[d][5][b] = args[i][4]
            dirs[d][6][b] = tm.astype(BF16)
            _unstack_to(dirs[d][4], b, o)
            s_ref[i] = s_new

        if nc:
            @pl.when(pl.program_id(0) == NCH - 1)
            def _():
                _xfer_wait(_gather_plan(c_in, c_out, _my_pos()), sems)

    (tf, gf_, sf_, if_), (tb, gb_, sb_, ib_) = _scan_specs(lambda s: s)
    o_sh = jax.ShapeDtypeStruct((NB, T, DNW), F32)
    st_sh = jax.ShapeDtypeStruct((NB, NCH, 128, DNW), F32)
    inv_sh = jax.ShapeDtypeStruct((NB, NCH, SR, SR), BF16)
    hbm = pl.BlockSpec(memory_space=pltpu.HBM)
    return pl.pallas_call(
        body, name="scan_fwd", grid=(NCH,),
        out_shape=[o_sh, st_sh, inv_sh, o_sh, st_sh, inv_sh] + [jax.ShapeDtypeStruct((8,) + h.shape, h.dtype) for h in carry],
        in_specs=_const_specs() + [tf, tf, tf, gf_, tb, tb, tb, gb_] + [hbm] * nc,
        out_specs=[tf, sf_, if_, tb, sb_, ib_] + [hbm] * nc,
        scratch_shapes=[pltpu.VMEM((len(SCAN_CHAINS), 128, DNW), F32)] + (_xfer_sems(nc, 7 * nc) if nc else []),
        compiler_params=_cp(vmem=VMEM_BIG),
    )(*_scan_masks(), q, k, v, gates, q, k, v, gates, *carry)


def _scan_bwd(q, k, v, gates, st_f, st_b, inv_f, inv_b, do, carry=()):
    nc = len(carry)

    def body(*refs):
        (cm_ref, c64_ref, qf, kf, vf, gf, sf, tf_, dof, qb, kb, vb, gb, sb, tb_, dob) = refs[:16]
        c_in = refs[16:16 + nc]
        dqf, dkf, dvf, dgf, dqb, dkb, dvb, dgb = refs[16 + nc:24 + nc]
        c_out = refs[24 + nc:24 + 2 * nc]
        ds_ref = refs[24 + 2 * nc]
        sems = refs[25 + 2 * nc:]
        first = pl.program_id(0) == 0
        last = pl.program_id(0) == NCH - 1
        if nc:
            @pl.when(first)
            def _():
                _xfer_start(_scatter_plan(c_in, c_out, _my_pos()), sems)

        @pl.when(first)
        def _():
            ds_ref[...] = jnp.zeros_like(ds_ref)

        dirs = ((qf, kf, vf, gf, sf, dof, dqf, dkf, dvf, dgf), (qb, kb, vb, gb, sb, dob, dqb, dkb, dvb, dgb))
        prim = tuple((_stack(dirs[d][0], b), _stack(dirs[d][1], b), _stack(dirs[d][2], b), dirs[d][3][b], dirs[d][4][b])
                     for b, d in SCAN_CHAINS)
        cots = tuple((_stack(dirs[d][5], b), ds_ref[i]) for i, (b, d) in enumerate(SCAN_CHAINS))
        saved = tuple((tf_, tb_)[d][b] for b, d in SCAN_CHAINS)
        _, vjp = jax.vjp(lambda ps: tuple(r[:2] for r in _chunks([p + (SCAN_CHAINS[i][1],) for i, p in enumerate(ps)],
                                                                cm_ref, c64_ref, saved)), prim)
        (grads,) = vjp(cots)
        for i, (dq, dk, dv, dg, ds) in enumerate(grads):
            b, d = SCAN_CHAINS[i]
            _unstack_to(dirs[d][6], b, dq)
            _unstack_to(dirs[d][7], b, dk)
            _unstack_to(dirs[d][8], b, dv)
            dirs[d][9][b] = dg
            ds_ref[i] = ds

        if nc:
            @pl.when(last)
            def _():
                _xfer_wait(_scatter_plan(c_in, c_out, _my_pos()), sems)

    (tf, gf_, sf_, if_), (tb, gb_, sb_, ib_) = _scan_specs(lambda s: NCH - 1 - s)
    o_sh = jax.ShapeDtypeStruct((NB, T, DNW), F32)
    g_sh = jax.ShapeDtypeStruct((NB, T, 128), F32)
    hbm = pl.BlockSpec(memory_space=pltpu.HBM)
    return pl.pallas_call(
        body, name="scan_bwd", grid=(NCH,),
        out_shape=[o_sh, o_sh, o_sh, g_sh] * 2 + [jax.ShapeDtypeStruct(g.shape, g.dtype) for g in carry],
        in_specs=_const_specs() + [tf, tf, tf, gf_, sf_, if_, tf, tb, tb, tb, gb_, sb_, ib_, tb] + [hbm] * nc,
        out_specs=[tf, tf, tf, gf_, tb, tb, tb, gb_] + [hbm] * nc,
        scratch_shapes=[pltpu.VMEM((len(SCAN_CHAINS), 128, DNW), F32)] + (_xfer_sems(nc, 7 * nc) if nc else []),
        compiler_params=_cp(vmem=VMEM_BIG),
    )(*_scan_masks(), q, k, v, gates, st_f, inv_f, do, q, k, v, gates, st_b, inv_b, do, *carry)


def _segmean(a, bd):
    hi = a.astype(BF16)
    lo = (a - hi.astype(F32)).astype(BF16)
    return jnp.dot(hi, bd, preferred_element_type=F32) + jnp.dot(lo, bd, preferred_element_type=F32)


def _na_prep_fwd(p, qw, kw, bd):
    def body(q_ref, k_ref, v_ref, qw_ref, kw_ref, bd_ref, oq_ref, ok_ref, ov_ref):
        bdm = bd_ref[...]
        q = q_ref[...]
        k = k_ref[...]
        oq_ref[...] = (q * lax.rsqrt(_segmean(q * q, bdm) + EPS) * qw_ref[...] * 0.125).astype(BF16)
        ok_ref[...] = (k * lax.rsqrt(_segmean(k * k, bdm) + EPS) * kw_ref[...]).astype(BF16)
        ov_ref[...] = v_ref[...].astype(BF16)

    ps = lambda o: pl.BlockSpec((None, RT, NAW), lambda b, j: (b, j, COL_NA // NAW + o))
    one = pl.BlockSpec((1, NAW), lambda b, j: (0, 0))
    os_ = pl.BlockSpec((None, RT, NAW), lambda b, j: (b, j, 0))
    return pl.pallas_call(
        body, name="na_prep_fwd", grid=(NB, T // RT), out_shape=[jax.ShapeDtypeStruct((NB, T, NAW), BF16)] * 3,
        in_specs=[ps(0), ps(1), ps(2), one, one, pl.BlockSpec((NAW, NAW), lambda b, j: (0, 0))], out_specs=[os_] * 3,
    )(p, p, p, qw, kw, bd)


def _na_prep_bwd(p, qw, kw, bd, dq, dk, dv):
    def body(q_ref, k_ref, qw_ref, kw_ref, bd_ref, dq_ref, dk_ref, dv_ref, o_ref, gq_ref, gk_ref):
        b, j = pl.program_id(0), pl.program_id(1)
        bdm = bd_ref[...]

        def one(x, w, dy, scale):
            r = lax.rsqrt(_segmean(x * x, bdm) + EPS)
            g = dy * w * scale
            dx = r * g - x * (r * r * r) * _segmean(g * x, bdm)
            dw = jnp.sum(dy * x * r * scale, axis=0, keepdims=True)
            return dx, dw

        dqv = jnp.where(j == 0, 0.0, dq_ref[...])
        dxq, dwq = one(q_ref[...], qw_ref[...], dqv, 0.125)
        dxk, dwk = one(k_ref[...], kw_ref[...], dk_ref[...], 1.0)
        o_ref[...] = jnp.concatenate([dxq, dxk, dv_ref[...]], axis=1)
        pad = jnp.zeros((7, NAW), F32)

        @pl.when((b == 0) & (j == 0))
        def _():
            gq_ref[...] = jnp.zeros_like(gq_ref)
            gk_ref[...] = jnp.zeros_like(gk_ref)

        gq_ref[...] += jnp.concatenate([dwq, pad], axis=0)
        gk_ref[...] += jnp.concatenate([dwk, pad], axis=0)

    ps = lambda o: pl.BlockSpec((None, RT, NAW), lambda b, j: (b, j, COL_NA // NAW + o))
    one = pl.BlockSpec((1, NAW), lambda b, j: (0, 0))
    ts = pl.BlockSpec((None, RT, NAW), lambda b, j: (b, j, 0))
    acc = pl.BlockSpec((8, NAW), lambda b, j: (0, 0))
    return pl.pallas_call(
        body, name="na_prep_bwd", grid=(NB, T // RT),
        out_shape=[jax.ShapeDtypeStruct((NB, T, 1536), F32), jax.ShapeDtypeStruct((8, NAW), F32),
                   jax.ShapeDtypeStruct((8, NAW), F32)],
        in_specs=[ps(0), ps(1), one, one, pl.BlockSpec((NAW, NAW), lambda b, j: (0, 0)),
                  pl.BlockSpec((None, RT, NAW), lambda b, j: (b, jnp.maximum(j - 1, 0), 0)), ts, ts],
        out_specs=[pl.BlockSpec((None, RT, 1536), lambda b, j: (b, j, 0)), acc, acc],
        compiler_params=_cp(vmem=VMEM_BIG),
    )(p, p, qw, kw, bd, dq, dk, dv)


def _na_row0(r):
    return jnp.clip(r - WIN_ROWS // 2, 0, ROWS - WIN_ROWS)


def _na_cls(r):
    return _na_row0(r) - r + WIN_ROWS - 1


NHS = 4
NLW = NHS * 64
NSR = NHS * GRID_W


def _na_stack(x):
    lane = lax.broadcasted_iota(jnp.int32, (1, NLW), 1)
    return jnp.concatenate([jnp.where((lane >> 6) == h, x, jnp.zeros_like(x)) for h in range(NHS)], axis=0)


def _na_unstack(y):
    lane = lax.broadcasted_iota(jnp.int32, (1, NLW), 1)
    out = jnp.zeros((GRID_W, NLW), y.dtype)
    for h in range(NHS):
        out = jnp.where((lane >> 6) == h, y[GRID_W * h:GRID_W * h + GRID_W], out)
    return out


def _na_fwd_gen(q, kw, vw, kc, vc, bias):
    q4 = _na_stack(q)
    s_w = lax.dot_general(q4, kw, _NT, preferred_element_type=F32) + bias
    yield
    s_c = lax.dot_general(q4, kc, _NT, preferred_element_type=F32)
    yield
    mx = jnp.maximum(jnp.max(s_w, axis=1, keepdims=True), jnp.max(s_c, axis=1, keepdims=True))
    p_w = jnp.exp(s_w - mx)
    p_c = jnp.exp(s_c - mx)
    inv = 1.0 / (jnp.sum(p_w, axis=1, keepdims=True) + jnp.sum(p_c, axis=1, keepdims=True))
    yield
    o_w = jnp.dot(p_w.astype(BF16), vw, preferred_element_type=F32)
    yield
    o4 = (o_w + jnp.dot(p_c.astype(BF16), vc, preferred_element_type=F32)) * inv
    return _na_unstack(o4)


NRF = 2


def _na_fwd(qn, kn, vb, bias, carry=()):
    nc = len(carry)
    grid = (NB, ROWS // NRF)
    ng = NAH // NHS

    def body(*refs):
        q_ref, k_ref, v_ref = refs[:3]
        b_refs, rest = refs[3:3 + NRF], refs[3 + NRF:]
        c_in, o_ref, c_out, sems = rest[:nc], rest[nc], rest[nc + 1:2 * nc + 1], rest[2 * nc + 1:]
        ids = [pl.program_id(a) for a in range(2)]
        if nc:
            @pl.when((ids[0] == 0) & (ids[1] == 0))
            def _():
                _xfer_start(_gather_plan(c_in, c_out, _my_pos()), sems)

        gens = []
        for i in range(NRF):
            start = pl.multiple_of(CT + _na_row0(NRF * ids[1] + i) * GRID_W, GRID_W)
            rw = slice(GRID_W * i, GRID_W * (i + 1))
            for g in range(ng):
                ln = slice(NLW * g, NLW * g + NLW)
                gens.append(_na_fwd_gen(q_ref[rw, ln], k_ref[pl.ds(start, NWIN), ln], v_ref[pl.ds(start, NWIN), ln],
                                        k_ref[0:CT, ln], v_ref[0:CT, ln], b_refs[i][g]))
        for n, o in enumerate(_lockstep(gens)):
            i, g = n // ng, n % ng
            o_ref[GRID_W * i:GRID_W * (i + 1), NLW * g:NLW * g + NLW] = o.astype(BF16)

        if nc:
            @pl.when((ids[0] == grid[0] - 1) & (ids[1] == grid[1] - 1))
            def _():
                _xfer_wait(_gather_plan(c_in, c_out, _my_pos()), sems)

    qs = pl.BlockSpec((None, NRF * GRID_W, NAW), lambda b, j: (b, CT // (NRF * GRID_W) + j, 0))
    kv = pl.BlockSpec((None, T, NAW), lambda b, j: (b, 0, 0))
    bss = [pl.BlockSpec((None, ng, NSR, NWIN), lambda b, j, i=i: (_na_cls(NRF * j + i), 0, 0, 0)) for i in range(NRF)]
    lat = pl.BlockSpec((None, NRF * GRID_W, NAW), lambda b, j: (b, j, 0))
    hbm = pl.BlockSpec(memory_space=pltpu.HBM)
    res = pl.pallas_call(
        body, name="na_fwd", grid=grid,
        out_shape=[jax.ShapeDtypeStruct((NB, SEQ, NAW), BF16)] + [jax.ShapeDtypeStruct((8,) + h.shape, h.dtype) for h in carry],
        in_specs=[qs, kv, kv] + bss + [hbm] * nc, out_specs=[lat] + [hbm] * nc,
        scratch_shapes=_xfer_sems(nc, 7 * nc) if nc else [],
        compiler_params=_cp(vmem=VMEM_BIG),
    )(qn, kn, vb, *([bias] * NRF), *carry)
    return res[0], list(res[1:])


def _na_bwd_gen(q, kw, vw, kc, vc, bias, do):
    q4 = _na_stack(q)
    s_w = lax.dot_general(q4, kw, _NT, preferred_element_type=F32) + bias
    yield
    s_c = lax.dot_general(q4, kc, _NT, preferred_element_type=F32)
    yield
    mx = jnp.maximum(jnp.max(s_w, axis=1, keepdims=True), jnp.max(s_c, axis=1, keepdims=True))
    p_w = jnp.exp(s_w - mx)
    p_c = jnp.exp(s_c - mx)
    inv = 1.0 / (jnp.sum(p_w, axis=1, keepdims=True) + jnp.sum(p_c, axis=1, keepdims=True))
    p_w = p_w * inv
    p_c = p_c * inv
    do4 = _na_stack(do.astype(BF16))
    dp_w = lax.dot_general(do4, vw, _NT, preferred_element_type=F32)
    yield
    dp_c = lax.dot_general(do4, vc, _NT, preferred_element_type=F32)
    yield
    delta = jnp.sum(p_w * dp_w, axis=1, keepdims=True) + jnp.sum(p_c * dp_c, axis=1, keepdims=True)
    ds_w = p_w * (dp_w - delta)
    ds_c = p_c * (dp_c - delta)
    ds_wb, ds_cb = ds_w.astype(BF16), ds_c.astype(BF16)
    dq4 = jnp.dot(ds_wb, kw, preferred_element_type=F32)
    yield
    dq4 = dq4 + jnp.dot(ds_cb, kc, preferred_element_type=F32)
    yield
    dkw = lax.dot_general(ds_wb, q4, _TN, preferred_element_type=F32)
    yield
    dvw = lax.dot_general(p_w.astype(BF16), do4, _TN, preferred_element_type=F32)
    yield
    dkc = lax.dot_general(ds_cb, q4, _TN, preferred_element_type=F32)
    yield
    dvc = lax.dot_general(p_c.astype(BF16), do4, _TN, preferred_element_type=F32)
    return ds_w, _na_unstack(dq4), dkw, dvw, dkc, dvc


NRS = 4


def _na_bwd(qn, kn, vb, bias, do):
    def body(q_ref, k_ref, v_ref, *rest):
        b_refs, (do_ref, dq_ref, dk_ref, dv_ref, db_ref) = rest[:NRS], rest[NRS:]
        b, j = pl.program_id(1), pl.program_id(2)

        @pl.when((b == 0) & (j == 0))
        def _():
            db_ref[...] = jnp.zeros_like(db_ref)

        @pl.when(j == 0)
        def _():
            dk_ref[...] = jnp.zeros_like(dk_ref)
            dv_ref[...] = jnp.zeros_like(dv_ref)

        kc, vc = k_ref[0:CT, :], v_ref[0:CT, :]
        rows = [NRS * j + i for i in range(NRS)]
        starts = [pl.multiple_of(CT + _na_row0(r) * GRID_W, GRID_W) for r in rows]
        gens = [_na_bwd_gen(q_ref[GRID_W * i:GRID_W * (i + 1), :], k_ref[pl.ds(starts[i], NWIN), :],
                            v_ref[pl.ds(starts[i], NWIN), :], kc, vc, b_refs[i][...],
                            do_ref[GRID_W * i:GRID_W * (i + 1), :]) for i in range(NRS)]
        for i, (ds_w, dq, dkw, dvw, dkc, dvc) in enumerate(_lockstep(gens)):
            db_ref[_na_cls(rows[i])] += ds_w
            dq_ref[GRID_W * i:GRID_W * (i + 1), :] = dq
            dk_ref[pl.ds(starts[i], NWIN), :] += dkw
            dv_ref[pl.ds(starts[i], NWIN), :] += dvw
            dk_ref[0:CT, :] += dkc
            dv_ref[0:CT, :] += dvc

    qs = pl.BlockSpec((None, NRS * GRID_W, NLW), lambda g, b, j: (b, CT // (NRS * GRID_W) + j, g))
    kv = pl.BlockSpec((None, T, NLW), lambda g, b, j: (b, 0, g))
    bss = [pl.BlockSpec((None, None, NSR, NWIN), lambda g, b, j, i=i: (_na_cls(NRS * j + i), g, 0, 0)) for i in range(NRS)]
    lat = pl.BlockSpec((None, NRS * GRID_W, NLW), lambda g, b, j: (b, j, g))
    return pl.pallas_call(
        body, name="na_bwd", grid=(NAH // NHS, NB, ROWS // NRS),
        out_shape=[jax.ShapeDtypeStruct((NB, SEQ, NAW), F32)] + [jax.ShapeDtypeStruct((NB, T, NAW), F32)] * 2
        + [jax.ShapeDtypeStruct((WIN_ROWS, NAH // NHS, NSR, NWIN), F32)],
        in_specs=[qs, kv, kv] + bss + [lat],
        out_specs=[lat, kv, kv, pl.BlockSpec((WIN_ROWS, None, NSR, NWIN), lambda g, b, j: (0, g, 0, 0))],
        compiler_params=_cp(vmem=VMEM_BIG),
    )(qn, kn, vb, *([bias] * NRS), do)


def _na_tables():
    cols = np.arange(GRID_W)
    win_start = np.clip(cols - WIN_COLS // 2, 0, GRID_W - WIN_COLS)
    kc = cols[None, :]
    valid = (kc >= win_start[:, None]) & (kc < win_start[:, None] + WIN_COLS)
    rel = np.clip(kc - cols[:, None] + WIN_COLS - 1, 0, 2 * WIN_COLS - 2)
    return valid, rel


def _bias_table(rpb):
    valid, rel = _na_tables()
    nrel = 2 * WIN_COLS - 1
    shifts = np.stack([(rel == r) & valid for r in range(nrel)]).astype(np.float32)
    neg = np.where(valid, 0.0, NEG).astype(np.float32)

    def body(rpb_ref, sh_ref, neg_ref, o_ref):
        h = pl.program_id(0)
        for dr in range(2 * WIN_ROWS - 1):
            tile = neg_ref[...]
            for r in range(nrel):
                tile = tile + rpb_ref[h, dr, r] * sh_ref[r]
            for cls in range(WIN_ROWS):
                i = dr - cls
                if 0 <= i < WIN_ROWS:
                    o_ref[cls, :, i * GRID_W:(i + 1) * GRID_W] = tile

    return pl.pallas_call(
        body, name="bias_table", grid=(NAH,),
        out_shape=jax.ShapeDtypeStruct((WIN_ROWS, NAH // NHS, NSR, NWIN), F32),
        in_specs=[pl.BlockSpec(memory_space=pltpu.SMEM), pl.BlockSpec((nrel, GRID_W, GRID_W), lambda h: (0, 0, 0)),
                  pl.BlockSpec((GRID_W, GRID_W), lambda h: (0, 0))],
        out_specs=pl.BlockSpec((WIN_ROWS, None, GRID_W, NWIN), lambda h: (0, h // NHS, h % NHS, 0)),
    )(rpb, jnp.asarray(shifts), jnp.asarray(neg))


def _rpb_fold(db):
    def body1(d_ref, o_ref):
        for dr in range(2 * WIN_ROWS - 1):
            acc = None
            for c in range(WIN_ROWS):
                i = dr - c
                if 0 <= i < WIN_ROWS:
                    blk = d_ref[c, :, i * GRID_W:(i + 1) * GRID_W]
                    acc = blk if acc is None else acc + blk
            o_ref[dr] = acc

    t15 = pl.pallas_call(
        body1, name="rpb_fold_rows", grid=(NAH,),
        out_shape=jax.ShapeDtypeStruct((NAH, 2 * WIN_ROWS - 1, GRID_W, GRID_W), F32),
        in_specs=[pl.BlockSpec((WIN_ROWS, None, GRID_W, NWIN), lambda h: (0, h, 0, 0))],
        out_specs=pl.BlockSpec((None, 2 * WIN_ROWS - 1, GRID_W, GRID_W), lambda h: (h, 0, 0, 0)),
        compiler_params=_cp(vmem=VMEM_BIG),
    )(db.reshape(WIN_ROWS, NAH, GRID_W, NWIN))
    valid, rel = _na_tables()
    onehot = np.zeros((GRID_W * GRID_W, 128), np.float32)
    flat_rel, flat_valid = rel.reshape(-1), valid.reshape(-1)
    onehot[np.arange(GRID_W * GRID_W)[flat_valid], flat_rel[flat_valid]] = 1.0

    def body2(a_ref, oh_ref, o_ref):
        o_ref[...] = _hmm(a_ref[...], oh_ref[...])

    vm = pl.BlockSpec(memory_space=pltpu.VMEM)
    out = pl.pallas_call(
        body2, name="rpb_fold_cols", out_shape=jax.ShapeDtypeStruct((NAH * 15, 128), F32),
        in_specs=[vm, vm], out_specs=vm, compiler_params=_cp(vmem=VMEM_BIG),
    )(t15.reshape(NAH * 15, GRID_W * GRID_W), jnp.asarray(onehot))
    return out[:, :31].reshape(NAH, 15, 31)


def _dn_out(o, z, w):
    outs = []
    for h in range(DNH):
        sl = slice(128 * h, 128 * h + 128)
        oh = o[:, sl]
        r = lax.rsqrt(jnp.mean(oh * oh, axis=-1, keepdims=True) + EPS)
        outs.append(oh * r * w[:, sl] * _silu(z[:, sl]))
    return jnp.concatenate(outs, axis=1)


def _outproj_fwd(of, ob, p, na, onw, x, g1, w_out):
    def body(of_ref, ob_ref, z_ref, na_ref, w_ref, x_ref, g_ref, wo_ref, x1_ref, ai_ref, ao_ref):
        dn = _dn_out(of_ref[...] + ob_ref[...], z_ref[...], w_ref[...])
        ai = jnp.concatenate([dn.astype(BF16), na_ref[...]], axis=1)
        ai_ref[...] = ai
        ao = jnp.dot(ai, wo_ref[...], preferred_element_type=F32)
        ao_ref[...] = ao
        x1_ref[...] = x_ref[...] + g_ref[...] * ao

    lat = lambda w: pl.BlockSpec((None, RT, w), lambda b, j: (b, j, 0))
    return pl.pallas_call(
        body, name="outproj_fwd", grid=(NB, SEQ // RT),
        out_shape=[jax.ShapeDtypeStruct((NB, SEQ, D), F32), jax.ShapeDtypeStruct((NB, SEQ, D), BF16),
                   jax.ShapeDtypeStruct((NB, SEQ, D), F32)],
        in_specs=[pl.BlockSpec((None, RT, DNW), lambda b, j: (b, j + 1, 0)),
                  pl.BlockSpec((None, RT, DNW), lambda b, j: (b, j + 1, 0)),
                  pl.BlockSpec((None, RT, DNW), lambda b, j: (b, j + 1, COL_Z // DNW)),
                  lat(NAW), pl.BlockSpec((1, DNW), lambda b, j: (0, 0)), lat(D),
                  pl.BlockSpec((None, 1, D), lambda b, j: (b, 0, 0)), pl.BlockSpec((D, D), lambda b, j: (0, 0))],
        out_specs=[lat(D), lat(D), lat(D)],
        compiler_params=_cp(vmem=VMEM_BIG),
    )(of, ob, p, na, onw, x, g1, w_out)


def _outproj_bwd(dx1, ao, g1, w_out, of, ob, p, onw):
    def body(dx_ref, ao_ref, g_ref, wo_ref, of_ref, ob_ref, z_ref, w_ref, dao_ref, do_ref, dz_ref, dna_ref, dg_ref, dw_ref):
        b, j = pl.program_id(0), pl.program_id(1)

        @pl.when((b == 0) & (j == 0))
        def _():
            dw_ref[...] = jnp.zeros_like(dw_ref)

        @pl.when(j == 0)
        def _():
            do_ref[...] = jnp.zeros_like(do_ref)
            dz_ref[...] = jnp.zeros_like(dz_ref)

        @pl.when(j == 1)
        def _():
            dg_ref[...] = jnp.zeros_like(dg_ref)

        @pl.when(j >= 1)
        def _():
            dx = dx_ref[...]
            dg_ref[...] += jnp.sum(dx * ao_ref[...], axis=0, keepdims=True)
            dao = (g_ref[...] * dx).astype(BF16)
            dao_ref[...] = dao
            dai = lax.dot_general(dao, wo_ref[...], _NT, preferred_element_type=F32)
            dna_ref[...] = dai[:, DNW:]
            _, vjp = jax.vjp(_dn_out, of_ref[...] + ob_ref[...], z_ref[...], w_ref[...])
            do, dz, dw = vjp(dai[:, :DNW])
            do_ref[...] = do
            dz_ref[...] = dz
            dw_ref[...] += jnp.concatenate([dw, jnp.zeros((7, DNW), F32)], axis=0)

    lat = lambda w: pl.BlockSpec((None, RT, w), lambda b, j: (b, jnp.maximum(j - 1, 0), 0))
    tok = lambda w: pl.BlockSpec((None, RT, w), lambda b, j: (b, j, 0))
    return pl.pallas_call(
        body, name="outproj_bwd", grid=(NB, T // RT),
        out_shape=[jax.ShapeDtypeStruct((NB, SEQ, D), BF16), jax.ShapeDtypeStruct((NB, T, DNW), F32),
                   jax.ShapeDtypeStruct((NB, T, DNW), F32), jax.ShapeDtypeStruct((NB, SEQ, NAW), F32),
                   jax.ShapeDtypeStruct((NB, 1, D), F32), jax.ShapeDtypeStruct((8, DNW), F32)],
        in_specs=[_rows2d_spec(D), lat(D), pl.BlockSpec((None, 1, D), lambda b, j: (b, 0, 0)),
                  pl.BlockSpec((D, D), lambda b, j: (0, 0)),
                  tok(DNW), tok(DNW),
                  pl.BlockSpec((None, RT, DNW), lambda b, j: (b, j, COL_Z // DNW)),
                  pl.BlockSpec((1, DNW), lambda b, j: (0, 0))],
        out_specs=[lat(D), tok(DNW), tok(DNW), lat(NAW), pl.BlockSpec((None, 1, D), lambda b, j: (b, 0, 0)),
                   pl.BlockSpec((8, DNW), lambda b, j: (0, 0))],
        compiler_params=_cp(vmem=VMEM_BIG),
    )(dx1, ao, g1, w_out, of, ob, p, onw)


FT_FWD, FT_BWD = 512, 256
NFF = DFF // FFT


def _ffn_specs(ft):
    row = lambda w: pl.BlockSpec((ft, w), lambda i, f: (i, 0))
    ex = pl.BlockSpec((None, 1, D), lambda i, f: (i // (SEQ // ft), 0, 0))
    one = pl.BlockSpec((1, D), lambda i, f: (0, 0))
    wg = pl.BlockSpec((None, D, FFT), lambda i, f: (f, 0, 0))
    wu = pl.BlockSpec((None, D, FFT), lambda i, f: (NFF + f, 0, 0))
    wo = pl.BlockSpec((FFT, D), lambda i, f: (f, 0))
    ff = pl.BlockSpec((ft, FFT), lambda i, f: (i, f))
    return row, ex, one, wg, wu, wo, ff


def _ffn_fwd(x1, sh2, sc2, g2, n2w, w_in4, w_out, target):
    ft, tpe = FT_FWD, SEQ // FT_FWD

    def body(x_ref, sh_ref, sc_ref, g_ref, w2_ref, wg_ref, wu_ref, wo_ref, t_ref,
             h2_ref, gt_ref, up_ref, dy_ref, dg_ref, loss_ref, acc_ref):
        i, f = pl.program_id(0), pl.program_id(1)

        @pl.when(f == 0)
        def _():
            h2_ref[...] = _norm_mod(x_ref[...], w2_ref[...], sh_ref[...], sc_ref[...]).astype(BF16)
            acc_ref[...] = jnp.zeros_like(acc_ref)

        h2 = h2_ref[...]
        gate = jnp.dot(h2, wg_ref[...], preferred_element_type=F32)
        up = jnp.dot(h2, wu_ref[...], preferred_element_type=F32)
        gt_ref[...] = gate.astype(BF16)
        up_ref[...] = up.astype(BF16)
        acc_ref[...] += jnp.dot((_silu(gate) * up).astype(BF16), wo_ref[...], preferred_element_type=F32)

        @pl.when((i == 0) & (f == 0))
        def _():
            loss_ref[...] = jnp.zeros_like(loss_ref)

        @pl.when(f == NFF - 1)
        def _():
            ffn = acc_ref[...]
            err = x_ref[...] + g_ref[...] * ffn - t_ref[...]
            dy = err * (1.0 / D)
            dy_ref[...] = dy
            part = jnp.sum(dy * ffn, axis=0, keepdims=True)
            loss_ref[...] += jnp.sum(jnp.sum(err * err, axis=1, keepdims=True), axis=0, keepdims=True) * (0.5 / D)

            @pl.when(i % tpe == 0)
            def _():
                dg_ref[...] = part

            @pl.when(i % tpe != 0)
            def _():
                dg_ref[...] += part

    row, ex, one, wg, wu, wo, ff = _ffn_specs(ft)
    return pl.pallas_call(
        body, name="ffn_fwd", grid=(NTOK // ft, NFF),
        out_shape=[jax.ShapeDtypeStruct((NTOK, D), BF16), jax.ShapeDtypeStruct((NTOK, DFF), BF16),
                   jax.ShapeDtypeStruct((NTOK, DFF), BF16), jax.ShapeDtypeStruct((NTOK, D), F32),
                   jax.ShapeDtypeStruct((NB, 1, D), F32), jax.ShapeDtypeStruct((8, 128), F32)],
        in_specs=[row(D), ex, ex, ex, one, wg, wu, wo, row(D)],
        out_specs=[row(D), ff, ff, row(D), ex, pl.BlockSpec((8, 128), lambda i, f: (0, 0))],
        scratch_shapes=[pltpu.VMEM((ft, D), F32)],
        compiler_params=_cp(vmem=VMEM_BIG),
    )(x1, sh2, sc2, g2, n2w, w_in4, w_in4, w_out, target)


def _rows2d_spec(width):
    return pl.BlockSpec((RT, width), lambda b, j: (b * (SEQ // RT) + jnp.maximum(j - 1, 0), 0))


def _ffn_bwd(dy, gate, up, x1, sh2, sc2, g2, n2w, w_in4, w_out):
    ft, tpe = FT_BWD, SEQ // FT_BWD
    ni = NTOK // ft

    def body(dy_ref, gt_ref, up_ref, x_ref, sh_ref, sc_ref, g_ref, w2_ref, wg_ref, wu_ref, wo_ref,
             dff_ref, act_ref, dgu_ref, dx_ref, dsh_ref, dsc_ref, dw_ref, acc_ref):
        f, i = pl.program_id(0), pl.program_id(1)
        rows = pl.ds(pl.multiple_of(i * ft, ft), ft)
        dff = (g_ref[...] * dy_ref[...]).astype(BF16)
        dff_ref[...] = dff
        dact = lax.dot_general(dff, wo_ref[...], _NT, preferred_element_type=F32)
        gate, up = gt_ref[...].astype(F32), up_ref[...].astype(F32)
        sg = _sigmoid(gate)
        sl = gate * sg
        act_ref[...] = (sl * up).astype(BF16)
        dgate = (dact * up * (sg * (1.0 + gate * (1.0 - sg)))).astype(BF16)
        dup = (dact * sl).astype(BF16)
        dgu_ref[0] = dgate
        dgu_ref[1] = dup
        part = (lax.dot_general(dgate, wg_ref[...], _NT, preferred_element_type=F32)
                + lax.dot_general(dup, wu_ref[...], _NT, preferred_element_type=F32))

        @pl.when(f == 0)
        def _():
            acc_ref[rows, :] = part

        @pl.when(f > 0)
        def _():
            acc_ref[rows, :] += part

        @pl.when((i == 0) & (f == 0))
        def _():
            dw_ref[...] = jnp.zeros_like(dw_ref)

        @pl.when(f == NFF - 1)
        def _():
            _, vjp = jax.vjp(_norm_mod, x_ref[...], w2_ref[...], sh_ref[...], sc_ref[...])
            dx, dw, dsh, dsc = vjp(acc_ref[rows, :])
            dx_ref[...] = dy_ref[...] + dx
            dw_ref[...] += dw

            @pl.when(i % tpe == 0)
            def _():
                dsh_ref[...] = dsh
                dsc_ref[...] = dsc

            @pl.when(i % tpe != 0)
            def _():
                dsh_ref[...] += dsh
                dsc_ref[...] += dsc

    last = lambda f, real, spare: jnp.where(f == NFF - 1, real, spare)
    row = lambda w: pl.BlockSpec((ft, w), lambda f, i: (i, 0))
    ex = pl.BlockSpec((None, 1, D), lambda f, i: (i // tpe, 0, 0))
    one = pl.BlockSpec((1, D), lambda f, i: (0, 0))
    wg = pl.BlockSpec((None, D, FFT), lambda f, i: (f, 0, 0))
    wu = pl.BlockSpec((None, D, FFT), lambda f, i: (NFF + f, 0, 0))
    wo = pl.BlockSpec((FFT, D), lambda f, i: (f, 0))
    ff = pl.BlockSpec((ft, FFT), lambda f, i: (i, f))
    row_once = lambda w: pl.BlockSpec((ft, w), lambda f, i: (last(f, i, ni), 0))
    ex_once = pl.BlockSpec((None, 1, D), lambda f, i: (last(f, i // tpe, NB), 0, 0))
    return pl.pallas_call(
        body, name="ffn_bwd", grid=(NFF, ni),
        out_shape=[jax.ShapeDtypeStruct((NTOK + ft, D), BF16), jax.ShapeDtypeStruct((NTOK, DFF), BF16),
                   jax.ShapeDtypeStruct((2, NTOK, DFF), BF16), jax.ShapeDtypeStruct((NTOK + ft, D), F32),
                   jax.ShapeDtypeStruct((NB + 1, 1, D), F32), jax.ShapeDtypeStruct((NB + 1, 1, D), F32),
                   jax.ShapeDtypeStruct((1, D), F32)],
        in_specs=[row(D), ff, ff, row(D), ex, ex, ex, one, wg, wu, wo],
        out_specs=[row_once(D), ff, pl.BlockSpec((2, ft, FFT), lambda f, i: (0, i, f)), row_once(D), ex_once, ex_once, one],
        scratch_shapes=[pltpu.VMEM((NTOK, D), F32)],
        compiler_params=_cp(vmem=VMEM_BIG),
    )(dy, gate, up, x1, sh2, sc2, g2, n2w, w_in4, w_in4, w_out)


def _local_step(x, ctx, target, mod_x, mod_c, norm1_w, norm2_w, w_in_p, conv_w, a_log, dt_bias, out_norm_w,
                q_norm_w, k_norm_w, rpb, w_out_b, w_ffn_in4, w_ffn_out_b, scatter_early=False, gather_late=None):
    sh1, sc1, g1, sh2, sc2, g2 = [mod_x[:, i * D:(i + 1) * D].reshape(NB, 1, D) for i in range(6)]
    csh1, csc1 = mod_c[0:D].reshape(1, 1, D), mod_c[D:2 * D].reshape(1, 1, D)
    cw = jnp.concatenate([conv_w, jnp.zeros((3, 1536), F32)], axis=0)
    prm = jnp.pad(jnp.stack([a_log.reshape(8), dt_bias.reshape(8)]), ((0, 6), (0, 120)))
    onw = jnp.tile(out_norm_w.reshape(1, 128), (1, DNH))
    qw = jnp.tile(q_norm_w.reshape(1, 64), (1, NAH))
    kw = jnp.tile(k_norm_w.reshape(1, 64), (1, NAH))
    seg = np.arange(NAW) // 64
    bd = jnp.asarray((seg[:, None] == seg[None, :]).astype(np.float32) / 64.0, dtype=BF16)
    bias = _bias_table(rpb)

    p, hm = _inproj_fwd(x, ctx, sh1, sc1, csh1, csc1, norm1_w, w_in_p)
    dq_, dk_, dv_ = _dn_prep_fwd(p, cw)
    gates = _gates_fwd(p, prm)
    res = _scan_fwd(dq_, dk_, dv_, gates, carry=gather_late[:2] if gather_late else ())
    o_f, st_f, inv_f, o_b, st_b, inv_b = res[:6]
    qn, kn, vb = _na_prep_fwd(p, qw, kw, bd)
    na, got = _na_fwd(qn, kn, vb, bias, carry=gather_late[2:] if gather_late else ())
    if gather_late:
        w_out_b = res[6].reshape(D, D)
        w_ffn_in4 = res[7].reshape(4, D, FFT)
        w_ffn_out_b = got[0].reshape(DFF, D)
    x1, ai, ao = _outproj_fwd(o_f, o_b, p, na, onw, x, g1, w_out_b)
    x1f = x1.reshape(NTOK, D)
    h2, gate, up, dy, dg2, loss8 = _ffn_fwd(x1f, sh2, sc2, g2, norm2_w, w_ffn_in4, w_ffn_out_b,
                                             target.reshape(NTOK, D))
    dff, act, dgu, dx1, dsh2, dsc2, dn2w = _ffn_bwd(dy, gate, up, x1f, sh2, sc2, g2, norm2_w, w_ffn_in4, w_ffn_out_b)
    g_w_ffn_out = _matmul_tn(act, dff, FFT, D, 512, "wgrad_ffn_out")
    g_w_ffn_in4 = _matmul_tn(
        h2, dgu, D, FFT, 1024, "wgrad_ffn_in", nn=2 * NFF,
        b_block=(None, 1024, FFT), b_map=lambda i, j, k: (j // NFF, k, j % NFF),
        out_shape=(4, D, FFT), out_block=(None, D, FFT), out_map=lambda i, j, k: (j, i, 0))
    dsh2, dsc2 = dsh2[:NB], dsc2[:NB]
    dao, do, dz, dna, dg1, donw = _outproj_bwd(dx1, ao, g1, w_out_b, o_f, o_b, p, onw)
    g_w_out = _matmul_tn(ai.reshape(NTOK, D), dao.reshape(NTOK, D), D, D, 1024, "wgrad_out")
    dqn, dkn, dvn, dbias = _na_bwd(qn, kn, vb, bias, dna)
    dp_na, dqw, dkw = _na_prep_bwd(p, qw, kw, bd, dqn, dkn, dvn)
    early = [g_w_out.reshape(8, D // 8, D), g_w_ffn_in4.reshape(8, D // 2, FFT), g_w_ffn_out.reshape(8, DFF // 8, D)]
    res = _scan_bwd(dq_, dk_, dv_, gates, st_f, st_b, inv_f, inv_b, do, carry=early if scatter_early else ())
    dqf, dkf, dvf, dgf, dqb, dkb, dvb, dgb = res[:8]
    if scatter_early:
        g_w_out, g_w_ffn_in4, g_w_ffn_out = res[8:]
    dpq, dpk, dpv, gcq, gck, gcv = _dn_prep_bwd(p, cw, (dqf, dqb), (dkf, dkb), (dvf, dvb))
    dgt, dprm = _gates_bwd(p, prm, dgf, dgb)
    dpb, grad_x, dsh1, dsc1, dcsh1, dcsc1, dn1w = _inproj_bwd(
        (dpq, dpk, dpv), dz, dp_na, dgt, w_in_p, x, ctx, sh1, sc1, csh1, csc1, norm1_w, dx1)
    g_w_in_p = _matmul_tn(dpb.reshape(NB * T, PC), hm.reshape(NB * T, D), PC, D, 512, "wgrad_in")
    g_rpb = _rpb_fold(dbias)

    dmod_x = jnp.concatenate([dsh1, dsc1, dg1, dsh2, dsc2, dg2], axis=2).reshape(NB, 6 * D)
    dmod_c = jnp.concatenate([dcsh1.reshape(D), dcsc1.reshape(D), jnp.zeros((4 * D,), F32)])
    g_conv = jnp.concatenate([gcq[:5], gck[:5], gcv[:5]], axis=1)
    small = dict(
        norm1_w=dn1w, norm2_w=dn2w, a_log=dprm[0, :8], dt_bias=dprm[1, :8],
        out_norm_w=donw[0].reshape(DNH, 128).sum(0), q_norm_w=dqw[0].reshape(NAH, 64).sum(0),
        k_norm_w=dkw[0].reshape(NAH, 64).sum(0), rpb=g_rpb, conv_w=g_conv)
    return loss8[0, 0], grad_x, dmod_x, dmod_c, small, g_w_in_p, g_w_out, g_w_ffn_in4, g_w_ffn_out


def _perm_w_in(wt):
    return jnp.concatenate([wt[:2048], wt[2064:3600], wt[2048:2064], jnp.zeros((PC - IN_COLS, wt.shape[1]), wt.dtype)],
                           axis=0)


def _unperm_w_in(gt):
    return jnp.concatenate([gt[:2048], gt[COL_G:COL_G + 16], gt[2048:COL_G]], axis=0)


def _misc_row(a_log, dt_bias, onw, qnw, knw):
    return jnp.concatenate([a_log.reshape(8), dt_bias.reshape(8), jnp.zeros((112,), F32), onw.reshape(128),
                            qnw.reshape(64), knw.reshape(64), jnp.zeros((D - 384,), F32)])


def _pack_small(c_ctx, n1, n2, b_ada, a_log, dt_bias, onw, qnw, knw, rpb, conv_s):
    misc = _misc_row(a_log, dt_bias, onw, qnw, knw)
    rp = jnp.concatenate([rpb.reshape(-1), jnp.zeros((4 * D - 3720,), F32)])
    cv = jnp.concatenate([conv_s.reshape(-1), jnp.zeros((2 * D - 1920,), F32)])
    return jnp.concatenate([c_ctx.reshape(-1), n1.reshape(-1), n2.reshape(-1), b_ada.reshape(-1), misc, rp, cv]).reshape(16, D)


def _unpack_small(a):
    f = a.reshape(-1)
    misc = f[9 * D:10 * D]
    return dict(
        c_ctx=f[0:D], norm1_w=f[D:2 * D].reshape(1, D), norm2_w=f[2 * D:3 * D].reshape(1, D),
        b_ada=f[3 * D:9 * D].reshape(1, 6 * D), dn_A_log=misc[0:8].reshape(1, 2, 4), dn_dt_bias=misc[8:16].reshape(1, 2, 4),
        dn_out_norm_w=misc[128:256].reshape(1, 128), na_q_norm_w=misc[256:320].reshape(1, 64),
        na_k_norm_w=misc[320:384].reshape(1, 64), na_rpb=f[10 * D:10 * D + 3720].reshape(1, 8, 15, 31),
        dn_conv_w=f[14 * D:14 * D + 1920].reshape(1, 5, 384))


_WEIGHTS = ['c_ctx', 'norm1_w', 'norm2_w', 'w_ada', 'b_ada', 'w_in', 'dn_conv_w', 'dn_A_log', 'dn_dt_bias',
            'dn_out_norm_w', 'na_q_norm_w', 'na_k_norm_w', 'na_rpb', 'w_out', 'w_ffn_in', 'w_ffn_out']


def kernel(x, c, ctx, c_ctx, norm1_w, norm2_w, w_ada, b_ada, w_in, dn_conv_w, dn_A_log, dn_dt_bias, dn_out_norm_w, na_q_norm_w, na_k_norm_w, na_rpb, w_out, w_ffn_in, w_ffn_out, loss_target, m_c_ctx, m_norm1_w, m_norm2_w, m_w_ada, m_b_ada, m_w_in, m_dn_conv_w, m_dn_A_log, m_dn_dt_bias, m_dn_out_norm_w, m_na_q_norm_w, m_na_k_norm_w, m_na_rpb, m_w_out, m_w_ffn_in, m_w_ffn_out, v_c_ctx, v_norm1_w, v_norm2_w, v_w_ada, v_b_ada, v_w_in, v_dn_conv_w, v_dn_A_log, v_dn_dt_bias, v_dn_out_norm_w, v_na_q_norm_w, v_na_k_norm_w, v_na_rpb, v_w_out, v_w_ffn_in, v_w_ffn_out):
    ix, iy, ic = lax.axis_index("x"), lax.axis_index("y"), lax.axis_index("c")
    chip = 2 * ix + iy
    me = 4 * ix + 2 * iy + ic

    crow = jnp.concatenate([c, c_ctx.reshape(1, D), jnp.zeros((5, D), F32)], axis=0)
    cvrow = jnp.concatenate([dn_conv_w[0].reshape(-1), jnp.zeros((2 * D - 1920,), F32)]).reshape(2, D)
    g0 = _allgather8(jnp.concatenate([crow, cvrow, jnp.zeros((6, D), F32)], axis=0), "gather_cond")
    cin = g0[:, :8].reshape(64, D)
    conv_full = jnp.concatenate([g0[2 * s, 8:10].reshape(-1)[:1920].reshape(5, 384) for s in range(4)], axis=1)

    w_ada_s = w_ada[0]
    b_s = lax.dynamic_slice(b_ada, (0, chip * 1536), (1, 1536))
    mod_s = _ada_fwd(cin, w_ada_s, b_s)
    mods = _gather_rows(mod_s, "gather_mod")
    mod_mine = jnp.concatenate([mods[0], mods[2], mods[4], mods[6]], axis=1)
    mod_x, mod_c = mod_mine[0:NB], mod_mine[2]

    def half(w2d):
        r = w2d.shape[0] // 2
        return lax.dynamic_slice(w2d, (ic * r, 0), (r, w2d.shape[1]))

    w_in_t = w_in[0].T
    (gw_in,) = _gather_weights([_cast_bf16(lax.dynamic_slice(w_in_t, (0, ic * (D // 2)), (900, D // 2)), "cast_w_in")])
    late = [_cast_bf16(half(w_out[0]), "cast_w_out"), _cast_bf16(half(w_ffn_in[0]), "cast_w_ffn_in"),
            _cast_bf16(half(w_ffn_out[0]), "cast_w_ffn_out")]
    w_in_p = _perm_w_in(gw_in.reshape(4, 2, 900, D // 2).transpose(0, 2, 1, 3).reshape(IN_COLS, D))

    (loss_part, grad_x, dmod_x, dmod_c, small, g_w_in_p, r_w_out, r_w_ffn_in, r_w_ffn_out) = _local_step(
        x, ctx, loss_target, mod_x, mod_c, norm1_w, norm2_w, w_in_p, conv_full, dn_A_log[0], dn_dt_bias[0],
        dn_out_norm_w[0], na_q_norm_w[0], na_k_norm_w[0], na_rpb[0], None, None, None,
        scatter_early=True, gather_late=late)
    loss = lax.psum(loss_part, ("x", "y", "c"))

    dm = jnp.concatenate([dmod_x, dmod_c.reshape(1, 6 * D), jnp.zeros((5, 6 * D), F32)], axis=0).reshape(48, D)
    misc = _misc_row(small["a_log"], small["dt_bias"], small["out_norm_w"], small["q_norm_w"], small["k_norm_w"])
    sm = jnp.concatenate([
        small["norm1_w"].reshape(-1), small["norm2_w"].reshape(-1), misc,
        small["rpb"].reshape(-1), jnp.zeros((4 * D - 3720,), F32),
        small["conv_w"].reshape(-1), jnp.zeros((8 * D - 7680,), F32), jnp.zeros((D,), F32)]).reshape(16, D)
    gsl, gtot = _allgather8(jnp.concatenate([dm, sm], axis=0), "gather_small", with_sum=True)
    gmod = gsl[:, :48].reshape(64, 6 * D)
    st = gtot[48:64].reshape(-1)
    g_n1, g_n2, g_misc = st[0:D].reshape(1, D), st[D:2 * D].reshape(1, D), st[2 * D:3 * D]
    g_rpb = st[3 * D:3 * D + 3720].reshape(1, 8, 15, 31)
    g_conv_full = st[7 * D:7 * D + 7680].reshape(5, 1536)
    g_conv_s = lax.dynamic_slice(g_conv_full, (0, chip * 384), (5, 384)).reshape(1, 5, 384)

    gmod_s = lax.dynamic_slice(gmod, (0, chip * 1536), (64, 1536))
    g_w_ada_s, pctx = _ada_bwd(cin, gmod_s, w_ada_s)
    parts = _allgather8(pctx, "gather_cctx")
    g_b_ada, g_c_ctx = _ada_small(gmod, parts, c_ctx.reshape(1, D))

    g_w_in8 = _unperm_w_in(g_w_in_p).reshape(4, 900, 2, D // 2).transpose(0, 2, 1, 3).reshape(8, 900, D // 2)
    names = ["w_in", "w_out", "w_ffn_in", "w_ffn_out"]
    early = _sibling_exchange([_sum_slots(r, "sum_" + nm)
                               for r, nm in zip([r_w_out, r_w_ffn_in, r_w_ffn_out], names[1:])], "sibling_early")
    g_w_out_s = early[0].reshape(1, D // 4, D)
    g_w_ffn_in_s = early[1].reshape(1, D, FFT)
    g_w_ffn_out_s = early[2].reshape(1, DFF // 4, D)
    two = lambda a: a.reshape(a.shape[1], a.shape[2])
    hosted_names = ["w_ada", "w_out", "w_ffn_in"]
    hosted_g = dict(w_ada=g_w_ada_s, w_out=two(g_w_out_s), w_ffn_in=two(g_w_ffn_in_s))
    hosted_w = dict(w_ada=(w_ada, m_w_ada, v_w_ada), w_out=(w_out, m_w_out, v_w_out), w_ffn_in=(w_ffn_in, m_w_ffn_in, v_w_ffn_in))
    hosted_res, (r_w_in,) = _adamw_hosting(
        [(two(hosted_w[k][0]), hosted_g[k], two(hosted_w[k][1]), two(hosted_w[k][2])) for k in hosted_names], [g_w_in8])
    (g_w_in_half,) = _sibling_exchange([_sum_slots(r_w_in, "sum_w_in")], "sibling_w_in")
    g_w_in_t = g_w_in_half.transpose(1, 0, 2).reshape(900, D)

    grads = dict(
        c_ctx=g_c_ctx.reshape(D), norm1_w=g_n1, norm2_w=g_n2, w_ada=g_w_ada_s.reshape(1, D, 1536), b_ada=g_b_ada,
        w_in=g_w_in_t, dn_conv_w=g_conv_s, dn_A_log=g_misc[0:8].reshape(1, 2, 4), dn_dt_bias=g_misc[8:16].reshape(1, 2, 4),
        dn_out_norm_w=g_misc[128:256].reshape(1, 128), na_q_norm_w=g_misc[256:320].reshape(1, 64),
        na_k_norm_w=g_misc[320:384].reshape(1, 64), na_rpb=g_rpb, w_out=g_w_out_s, w_ffn_in=g_w_ffn_in_s,
        w_ffn_out=g_w_ffn_out_s)
    pk = lambda d_: _pack_small(d_["c_ctx"], d_["norm1_w"], d_["norm2_w"], d_["b_ada"], d_["dn_A_log"], d_["dn_dt_bias"],
                                d_["dn_out_norm_w"], d_["na_q_norm_w"], d_["na_k_norm_w"], d_["na_rpb"], d_["dn_conv_w"])
    wts = dict(c_ctx=c_ctx, norm1_w=norm1_w, norm2_w=norm2_w, b_ada=b_ada, dn_A_log=dn_A_log, dn_dt_bias=dn_dt_bias,
               dn_out_norm_w=dn_out_norm_w, na_q_norm_w=na_q_norm_w, na_k_norm_w=na_k_norm_w, na_rpb=na_rpb, dn_conv_w=dn_conv_w)
    ms = dict(c_ctx=m_c_ctx, norm1_w=m_norm1_w, norm2_w=m_norm2_w, b_ada=m_b_ada, dn_A_log=m_dn_A_log, dn_dt_bias=m_dn_dt_bias,
              dn_out_norm_w=m_dn_out_norm_w, na_q_norm_w=m_na_q_norm_w, na_k_norm_w=m_na_k_norm_w, na_rpb=m_na_rpb,
              dn_conv_w=m_dn_conv_w)
    vs = dict(c_ctx=v_c_ctx, norm1_w=v_norm1_w, norm2_w=v_norm2_w, b_ada=v_b_ada, dn_A_log=v_dn_A_log, dn_dt_bias=v_dn_dt_bias,
              dn_out_norm_w=v_dn_out_norm_w, na_q_norm_w=v_na_q_norm_w, na_k_norm_w=v_na_k_norm_w, na_rpb=v_na_rpb,
              dn_conv_w=v_dn_conv_w)
    sd, snm, snv, _ = _adamw(pk(wts), pk(grads), pk(ms), pk(vs), "adamw_small")
    delta, new_m, new_v = _unpack_small(sd), _unpack_small(snm), _unpack_small(snv)
    for k, res in zip(hosted_names, hosted_res):
        shp = hosted_w[k][0].shape
        delta[k], new_m[k], new_v[k], grads[k] = [a.reshape(shp) for a in res]
    big = dict(w_in=(w_in, m_w_in, v_w_in), w_ffn_out=(w_ffn_out, m_w_ffn_out, v_w_ffn_out))
    for nm, (w_, m_, v_) in big.items():
        shp = w_.shape
        if nm == "w_in":
            r2, back = (lambda a: a[0].T), (lambda a: a.T.reshape(shp))
            res = _adamw(w_in_t, grads[nm], r2(m_), r2(v_), "adamw_" + nm)
        else:
            r2, back = (lambda a: a.reshape(shp[1], shp[2])), (lambda a: a.reshape(shp))
            res = _adamw(r2(w_), r2(grads[nm]), r2(m_), r2(v_), "adamw_" + nm)
        delta[nm], new_m[nm], new_v[nm], grads[nm] = [back(a) for a in res]

    return (loss, grad_x, *[grads[n] for n in _WEIGHTS], *[delta[n] for n in _WEIGHTS],
            *[new_m[n] for n in _WEIGHTS], *[new_v[n] for n in _WEIGHTS])
```

```python
import functools

import numpy as np
import jax
import jax.numpy as jnp
from jax import lax
from jax.experimental import pallas as pl
from jax.experimental.pallas import tpu as pltpu

F32 = jnp.float32
BF16 = jnp.bfloat16

D = 1024
NB = 2
SEQ = 2048
CT = 256
T = CT + SEQ
NTOK = NB * SEQ
CH = 64
NCH = T // CH
NCH_CTX = CT // CH
DNH, DNW = 4, 512
NAH, NAW = 8, 512
GRID_W = 64
ROWS = SEQ // GRID_W
WIN_ROWS = 8
WIN_COLS = 16
NWIN = WIN_ROWS * GRID_W
DFF = 2816
FFT = 1408
IN_COLS = 3600
PC = 3712
COL_Z, COL_NA, COL_G = 1536, 2048, 3584
EPS = 1e-6
NEG = -1e30
RT = 256
VMEM_BIG = 56 * 1024 * 1024
MESH = pl.DeviceIdType.MESH

ADAM_LR, ADAM_B1, ADAM_B2, ADAM_EPS, ADAM_WD, ADAM_STEP = 0.001, 0.9, 0.999, 1e-08, 0.01, 10


def _cp(vmem=None, sem=None):
    kw = {}
    if vmem is not None:
        kw["vmem_limit_bytes"] = vmem
    if sem is not None:
        kw["dimension_semantics"] = sem
    return pltpu.CompilerParams(**kw)


_NN = (((1,), (0,)), ((), ()))
_NT = (((1,), (1,)), ((), ()))
_TN = (((0,), (0,)), ((), ()))


def _dg(a, b, dims):
    return lax.dot_general(a.astype(BF16), b.astype(BF16), dims, preferred_element_type=F32)


@jax.custom_vjp
def _mm(a, b):
    return _dg(a, b, _NN)


_mm.defvjp(lambda a, b: (_dg(a, b, _NN), (a, b)),
           lambda r, g: (_dg(g, r[1], _NT), _dg(r[0], g, _TN)))


def _split_dot(a, b, dims):
    hi = b.astype(BF16)
    lo = (b - hi.astype(F32)).astype(BF16)
    ab = a.astype(BF16)
    return (lax.dot_general(ab, hi, dims, preferred_element_type=F32)
            + lax.dot_general(ab, lo, dims, preferred_element_type=F32))


@jax.custom_vjp
def _mask_mm(mask, b):
    return _split_dot(mask, b, _NN)


_mask_mm.defvjp(lambda mask, b: (_split_dot(mask, b, _NN), mask),
                lambda mask, g: (jnp.zeros_like(mask), _split_dot(mask, g, _TN)))


def _hmm(a, b):
    return jnp.dot(a, b, precision=lax.Precision.HIGHEST, preferred_element_type=F32)


def _sigmoid(x):
    return 1.0 / (1.0 + jnp.exp(-x))


def _silu(x):
    return x * _sigmoid(x)


def _norm_mod(x, w, sh, sc):
    r = lax.rsqrt(jnp.mean(x * x, axis=-1, keepdims=True) + EPS)
    return (x * r * w) * (1.0 + sc) + sh


def _my_pos():
    return lax.axis_index("x"), lax.axis_index("y"), lax.axis_index("c")


def _flip(pos, k):
    x, y, c = pos
    return (1 - x if k & 4 else x, 1 - y if k & 2 else y, 1 - c if k & 1 else c)


def _slot(pos):
    return 4 * pos[0] + 2 * pos[1] + pos[2]


def _allgather8(v, name, with_sum=False):
    R, W = v.shape

    def body(v_ref, out_ref, *rest):
        if with_sum:
            tot_ref, send_sems, recv_sems, lsem = rest
        else:
            send_sems, recv_sems, lsem = rest
        me = _my_pos()
        mine = pltpu.make_async_copy(v_ref, out_ref.at[_slot(me)], lsem)
        mine.start()
        sends = []
        for k in range(1, 8):
            cp = pltpu.make_async_remote_copy(
                src_ref=v_ref, dst_ref=out_ref.at[_slot(me)], send_sem=send_sems.at[k - 1],
                recv_sem=recv_sems.at[k - 1], device_id=_flip(me, k), device_id_type=MESH)
            cp.start()
            sends.append(cp)
        for k in range(1, 8):
            peer = _flip(me, k)
            pltpu.make_async_remote_copy(
                src_ref=v_ref, dst_ref=out_ref.at[_slot(peer)], send_sem=send_sems.at[k - 1],
                recv_sem=recv_sems.at[k - 1], device_id=peer, device_id_type=MESH).wait_recv()
        for cp in sends:
            cp.wait_send()
        mine.wait()
        if with_sum:
            acc = out_ref[0]
            for s in range(1, 8):
                acc = acc + out_ref[s]
            tot_ref[...] = acc

    out_shape = [jax.ShapeDtypeStruct((8, R, W), F32)]
    out_specs = [pl.BlockSpec(memory_space=pltpu.VMEM)]
    if with_sum:
        out_shape.append(jax.ShapeDtypeStruct((R, W), F32))
        out_specs.append(pl.BlockSpec(memory_space=pltpu.VMEM))
    res = pl.pallas_call(
        body, name=name, out_shape=out_shape,
        in_specs=[pl.BlockSpec(memory_space=pltpu.VMEM)], out_specs=out_specs,
        scratch_shapes=[pltpu.SemaphoreType.DMA((7,)), pltpu.SemaphoreType.DMA((7,)), pltpu.SemaphoreType.DMA],
        compiler_params=_cp(vmem=VMEM_BIG),
    )(v)
    return res if with_sum else res[0]


def _gather_rows(v, name):
    W = v.shape[1]

    def body(v_ref, out_ref, send_sems, recv_sems, lsem):
        pos = _my_pos()
        me = _slot(pos)
        rows = lambda p: v_ref.at[pl.ds(pl.multiple_of(8 * p, 8), 8)]
        locals_ = [(rows(me), out_ref.at[me])]
        xfers = [(rows(_slot(_flip(pos, k))), out_ref.at[me], out_ref.at[_slot(_flip(pos, k))], _flip(pos, k))
                 for k in range(1, 8)]
        _xfer_start((locals_, xfers), (send_sems, recv_sems, lsem))
        _xfer_wait((locals_, xfers), (send_sems, recv_sems, lsem))

    vm = pl.BlockSpec(memory_space=pltpu.VMEM)
    return pl.pallas_call(
        body, name=name, out_shape=jax.ShapeDtypeStruct((8, 8, W), F32), in_specs=[vm], out_specs=vm,
        scratch_shapes=_xfer_sems(1, 7),
    )(v)


def _gather_weights(halves):
    n = len(halves)

    def body(*refs):
        ins, outs = refs[:n], refs[n:2 * n]
        send_sems, recv_sems, lsems = refs[2 * n:]
        me = _my_pos()
        x, y, c = me
        sib = (x, y, 1 - c)
        chips = [4, 2, 6]

        def cp(a, k, block, to, src=None):
            dst = outs[a].at[_slot(block)]
            return pltpu.make_async_remote_copy(
                src_ref=dst if src is None else src, dst_ref=dst, send_sem=send_sems.at[a, k],
                recv_sem=recv_sems.at[a, k], device_id=to, device_id_type=MESH)

        mine, first, passed = [], [], []
        for a in range(n):
            m = pltpu.make_async_copy(ins[a], outs[a].at[_slot(me)], lsems.at[a])
            m.start()
            mine.append(m)
            first.append(cp(a, 0, me, sib, src=ins[a]))
            for j, k in enumerate(chips):
                first.append(cp(a, 1 + j, me, _flip(me, k), src=ins[a]))
        for f in first:
            f.start()
        for j, k in enumerate(chips):
            for a in range(n):
                cp(a, 1 + j, _flip(me, k), me).wait_recv()
                p = cp(a, 4 + j, _flip(me, k), sib)
                p.start()
                passed.append(p)
        for a in range(n):
            cp(a, 0, sib, me).wait_recv()
            for j, k in enumerate(chips):
                cp(a, 4 + j, _flip(sib, k), me).wait_recv()
        for f in first + passed:
            f.wait_send()
        for m in mine:
            m.wait()

    vm = pl.BlockSpec(memory_space=pltpu.VMEM)
    return pl.pallas_call(
        body, name="gather_weights",
        out_shape=[jax.ShapeDtypeStruct((8,) + h.shape, h.dtype) for h in halves],
        in_specs=[vm] * n, out_specs=[vm] * n,
        scratch_shapes=[pltpu.SemaphoreType.DMA((n, 7)), pltpu.SemaphoreType.DMA((n, 7)),
                        pltpu.SemaphoreType.DMA((n,))],
        compiler_params=_cp(vmem=VMEM_BIG),
    )(*halves)


def _xfer_start(plan, sems):
    send_sems, recv_sems, lsems = sems
    locals_, xfers = plan
    for i, (src, dst) in enumerate(locals_):
        pltpu.make_async_copy(src, dst, lsems.at[i]).start()
    for i, (src, dst, _, peer) in enumerate(xfers):
        pltpu.make_async_remote_copy(src_ref=src, dst_ref=dst, send_sem=send_sems.at[i], recv_sem=recv_sems.at[i],
                                     device_id=peer, device_id_type=MESH).start()


def _xfer_wait(plan, sems):
    send_sems, recv_sems, lsems = sems
    locals_, xfers = plan
    for i, (src, _, landing, peer) in enumerate(xfers):
        pltpu.make_async_remote_copy(src_ref=src, dst_ref=landing, send_sem=send_sems.at[i], recv_sem=recv_sems.at[i],
                                     device_id=peer, device_id_type=MESH).wait_recv()
    for i, (src, dst, _, peer) in enumerate(xfers):
        pltpu.make_async_remote_copy(src_ref=src, dst_ref=dst, send_sem=send_sems.at[i], recv_sem=recv_sems.at[i],
                                     device_id=peer, device_id_type=MESH).wait_send()
    for i, (src, dst) in enumerate(locals_):
        pltpu.make_async_copy(src, dst, lsems.at[i]).wait()


def _xfer_sems(n_local, n_xfer):
    return [pltpu.SemaphoreType.DMA((n_xfer,)), pltpu.SemaphoreType.DMA((n_xfer,)), pltpu.SemaphoreType.DMA((n_local,))]


def _scatter_plan(ins, outs, pos):
    me = _slot(pos)
    locals_, xfers = [], []
    for src, dst in zip(ins, outs):
        locals_.append((src.at[me], dst.at[me]))
        for k in range(1, 8):
            peer = _flip(pos, k)
            xfers.append((src.at[_slot(peer)], dst.at[me], dst.at[_slot(peer)], peer))
    return locals_, xfers


def _sibling_exchange(halves, name):
    n = len(halves)

    def body(*refs):
        ins, outs = refs[:n], refs[n:2 * n]
        send_sems, recv_sems, lsems = refs[2 * n:]
        x, y, c = _my_pos()
        sib = (x, y, 1 - c)
        mine, sends = [], []
        for a in range(n):
            m = pltpu.make_async_copy(ins[a], outs[a].at[c], lsems.at[a])
            m.start()
            mine.append(m)
            cp = pltpu.make_async_remote_copy(
                src_ref=ins[a], dst_ref=outs[a].at[c], send_sem=send_sems.at[a], recv_sem=recv_sems.at[a],
                device_id=sib, device_id_type=MESH)
            cp.start()
            sends.append(cp)
        for a in range(n):
            pltpu.make_async_remote_copy(
                src_ref=ins[a], dst_ref=outs[a].at[1 - c], send_sem=send_sems.at[a], recv_sem=recv_sems.at[a],
                device_id=sib, device_id_type=MESH).wait_recv()
        for cp in sends:
            cp.wait_send()
        for m in mine:
            m.wait()

    vm = pl.BlockSpec(memory_space=pltpu.VMEM)
    return pl.pallas_call(
        body, name=name,
        out_shape=[jax.ShapeDtypeStruct((2,) + h.shape, h.dtype) for h in halves],
        in_specs=[vm] * n, out_specs=[vm] * n,
        scratch_shapes=[pltpu.SemaphoreType.DMA((n,)), pltpu.SemaphoreType.DMA((n,)), pltpu.SemaphoreType.DMA((n,))],
        compiler_params=_cp(vmem=VMEM_BIG),
    )(*halves)


def _row_tile(r, cap=512):
    for t in (cap, 256, 128, 64, 32, 16, 8):
        if t <= cap and r % t == 0:
            return t
    return r


def _cast_bf16(a, name):
    R, W = a.shape
    tr = _row_tile(R, 256)

    def body(a_ref, o_ref):
        o_ref[...] = a_ref[...].astype(BF16)

    return pl.pallas_call(
        body, name=name, grid=(R // tr,), out_shape=jax.ShapeDtypeStruct((R, W), BF16),
        in_specs=[pl.BlockSpec((tr, W), lambda i: (i, 0))], out_specs=pl.BlockSpec((tr, W), lambda i: (i, 0)),
    )(a)


def _sum_slots(r, name):
    _, R, W = r.shape
    tr = _row_tile(R, 128)

    def body(r_ref, o_ref):
        acc = r_ref[0].astype(F32)
        for s_ in range(1, 8):
            acc = acc + r_ref[s_].astype(F32)
        o_ref[...] = acc.astype(BF16)

    return pl.pallas_call(
        body, name=name, grid=(R // tr,), out_shape=jax.ShapeDtypeStruct((R, W), BF16),
        in_specs=[pl.BlockSpec((8, tr, W), lambda i: (0, i, 0))], out_specs=pl.BlockSpec((tr, W), lambda i: (i, 0)),
        compiler_params=_cp(vmem=VMEM_BIG),
    )(r)


def _adamw(w, g, m, v, name):
    R, W = w.shape
    tr = _row_tile(R, 256)
    tc = 256 if (tr == R and R * W * 4 > (1 << 21) and W % 256 == 0) else W
    c1 = 1.0 / (1.0 - ADAM_B1 ** ADAM_STEP)
    c2 = 1.0 / (1.0 - ADAM_B2 ** ADAM_STEP)

    def body(w_ref, g_ref, m_ref, v_ref, d_ref, nm_ref, nv_ref, g32_ref):
        gg = g_ref[...].astype(F32)
        g32_ref[...] = gg
        nm = ADAM_B1 * m_ref[...] + (1.0 - ADAM_B1) * gg
        nv = ADAM_B2 * v_ref[...] + (1.0 - ADAM_B2) * (gg * gg)
        d_ref[...] = -ADAM_LR * ((nm * c1) / (jnp.sqrt(nv * c2) + ADAM_EPS) + ADAM_WD * w_ref[...])
        nm_ref[...] = nm
        nv_ref[...] = nv

    spec = pl.BlockSpec((tr, tc), lambda i, j: (i, j))
    return pl.pallas_call(
        body, name=name, grid=(R // tr, W // tc), out_shape=[jax.ShapeDtypeStruct((R, W), F32)] * 4,
        in_specs=[spec] * 4, out_specs=[spec] * 4, compiler_params=_cp(vmem=VMEM_BIG),
    )(w, g, m, v)


def _adamw_hosting(items, carry):
    n, nc, steps = len(items), len(carry), 8
    c1 = 1.0 / (1.0 - ADAM_B1 ** ADAM_STEP)
    c2 = 1.0 / (1.0 - ADAM_B2 ** ADAM_STEP)

    def body(*refs):
        ins, c_in = refs[:4 * n], refs[4 * n:4 * n + nc]
        outs, c_out = refs[4 * n + nc:8 * n + nc], refs[8 * n + nc:8 * n + 2 * nc]
        sems = refs[8 * n + 2 * nc:]
        if nc:
            @pl.when(pl.program_id(0) == 0)
            def _():
                _xfer_start(_scatter_plan(c_in, c_out, _my_pos()), sems)

        for a in range(n):
            w_ref, g_ref, m_ref, v_ref = ins[4 * a:4 * a + 4]
            d_ref, nm_ref, nv_ref, g32_ref = outs[4 * a:4 * a + 4]
            gg = g_ref[...].astype(F32)
            g32_ref[...] = gg
            nm = ADAM_B1 * m_ref[...] + (1.0 - ADAM_B1) * gg
            nv = ADAM_B2 * v_ref[...] + (1.0 - ADAM_B2) * (gg * gg)
            d_ref[...] = -ADAM_LR * ((nm * c1) / (jnp.sqrt(nv * c2) + ADAM_EPS) + ADAM_WD * w_ref[...])
            nm_ref[...] = nm
            nv_ref[...] = nv

        if nc:
            @pl.when(pl.program_id(0) == steps - 1)
            def _():
                _xfer_wait(_scatter_plan(c_in, c_out, _my_pos()), sems)

    specs = [pl.BlockSpec((w.shape[0] // steps, w.shape[1]), lambda i: (i, 0)) for w, _, _, _ in items]
    hbm = pl.BlockSpec(memory_space=pltpu.HBM)
    res = pl.pallas_call(
        body, name="adamw_hosting", grid=(steps,),
        out_shape=[jax.ShapeDtypeStruct(w.shape, F32) for w, _, _, _ in items for _ in range(4)]
        + [jax.ShapeDtypeStruct(g.shape, g.dtype) for g in carry],
        in_specs=[sp for sp in specs for _ in range(4)] + [hbm] * nc,
        out_specs=[sp for sp in specs for _ in range(4)] + [hbm] * nc,
        scratch_shapes=_xfer_sems(nc, 7 * nc) if nc else [],
        compiler_params=_cp(vmem=VMEM_BIG),
    )(*[a for it in items for a in it], *carry)
    return [res[4 * a:4 * a + 4] for a in range(n)], list(res[4 * n:])


def _matmul_tn(a, b, tm, tn, tk, name, nn=1, b_map=None, out_shape=None, out_block=None, out_map=None,
               b_block=None):
    K, M = a.shape
    nk = K // tk
    nm = M // tm

    def body(a_ref, b_ref, o_ref, acc_ref):
        kk = pl.program_id(2)

        @pl.when(kk == 0)
        def _():
            acc_ref[...] = jnp.zeros_like(acc_ref)

        acc_ref[...] += lax.dot_general(a_ref[...], b_ref[...], _TN, preferred_element_type=F32)

        @pl.when(kk == nk - 1)
        def _():
            o_ref[...] = acc_ref[...].astype(BF16)

    return pl.pallas_call(
        body, name=name, grid=(nm, nn, nk),
        out_shape=jax.ShapeDtypeStruct(out_shape or (M, b.shape[-1]), BF16),
        in_specs=[pl.BlockSpec((tk, tm), lambda i, j, k: (k, i)),
                  pl.BlockSpec(b_block or (tk, tn), b_map or (lambda i, j, k: (k, j)))],
        out_specs=pl.BlockSpec(out_block or (tm, tn), out_map or (lambda i, j, k: (i, j))),
        scratch_shapes=[pltpu.VMEM((tm, tn), F32)],
        compiler_params=_cp(vmem=VMEM_BIG),
    )(a, b)


def _ada_fwd(cin, w_ada_s, b_s):
    tn = 512

    def body(c_ref, w_ref, b_ref, o_ref):
        o_ref[...] = _dg(_silu(c_ref[...]), w_ref[...], _NN) + b_ref[...]

    return pl.pallas_call(
        body, name="ada_fwd", grid=(1536 // tn,), out_shape=jax.ShapeDtypeStruct((64, 1536), F32),
        in_specs=[pl.BlockSpec((64, D), lambda j: (0, 0)), pl.BlockSpec((D, tn), lambda j: (0, j)),
                  pl.BlockSpec((1, tn), lambda j: (0, j))],
        out_specs=pl.BlockSpec((64, tn), lambda j: (0, j)),
        compiler_params=_cp(vmem=VMEM_BIG),
    )(cin, w_ada_s, b_s)


def _ada_bwd(cin, gmod_s, w_ada_s):
    tn = 512

    def body(c_ref, g_ref, w_ref, gw_ref, pc_ref):
        j = pl.program_id(0)
        g = g_ref[...]
        gw_ref[...] = _dg(_silu(c_ref[...]), g, _TN)
        rr = lax.broadcasted_iota(jnp.int32, (8, 64), 0)
        cc = lax.broadcasted_iota(jnp.int32, (8, 64), 1)
        sel = jnp.where((rr == 0) & ((cc & 7) == 2), 1.0, 0.0).astype(F32)
        dctx = _hmm(sel, g)
        part = _dg(dctx, w_ref[...], _NT)

        @pl.when(j == 0)
        def _():
            pc_ref[...] = jnp.zeros_like(pc_ref)

        pc_ref[...] += part

    return pl.pallas_call(
        body, name="ada_bwd", grid=(1536 // tn,),
        out_shape=[jax.ShapeDtypeStruct((D, 1536), F32), jax.ShapeDtypeStruct((8, D), F32)],
        in_specs=[pl.BlockSpec((64, D), lambda j: (0, 0)), pl.BlockSpec((64, tn), lambda j: (0, j)),
                  pl.BlockSpec((D, tn), lambda j: (0, j))],
        out_specs=[pl.BlockSpec((D, tn), lambda j: (0, j)), pl.BlockSpec((8, D), lambda j: (0, 0))],
        compiler_params=_cp(vmem=VMEM_BIG),
    )(cin, gmod_s, w_ada_s)


def _ada_small(gmod, parts, c_ctx):
    def body(g_ref, p_ref, c_ref, gb_ref, gc_ref):
        gb_ref[...] = jnp.sum(g_ref[...], axis=0, keepdims=True)
        ds = p_ref[0][0:1] + p_ref[2][0:1] + p_ref[4][0:1] + p_ref[6][0:1]
        cc = c_ref[...]
        s = _sigmoid(cc)
        gc_ref[...] = ds * (s * (1.0 + cc * (1.0 - s)))

    vm = pl.BlockSpec(memory_space=pltpu.VMEM)
    return pl.pallas_call(
        body, name="ada_small",
        out_shape=[jax.ShapeDtypeStruct((1, 6 * D), F32), jax.ShapeDtypeStruct((1, D), F32)],
        in_specs=[vm, vm, vm], out_specs=[vm, vm],
    )(gmod, parts, c_ctx)


def _tok_specs():
    xs = pl.BlockSpec((None, RT, D), lambda b, j: (b, jnp.maximum(j - 1, 0), 0))
    cs = pl.BlockSpec((None, RT, D), lambda b, j: (b, 0, 0))
    ms = pl.BlockSpec((None, 1, D), lambda b, j: (b, 0, 0))
    cms = pl.BlockSpec((None, 1, D), lambda b, j: (0, 0, 0))
    return xs, cs, ms, cms


def _inproj_fwd(x, ctx, sh1, sc1, csh1, csc1, n1w, w_in_p):
    def body(x_ref, c_ref, sh_ref, sc_ref, csh_ref, csc_ref, w1_ref, w_ref, p_ref, hm_ref):
        is_ctx = pl.program_id(1) == 0
        xt = jnp.where(is_ctx, c_ref[...], x_ref[...])
        sh = jnp.where(is_ctx, csh_ref[...], sh_ref[...])
        sc = jnp.where(is_ctx, csc_ref[...], sc_ref[...])
        hb = _norm_mod(xt, w1_ref[...], sh, sc).astype(BF16)
        hm_ref[...] = hb
        p_ref[...] = lax.dot_general(hb, w_ref[...], _NT, preferred_element_type=F32)

    xs, cs, ms, cms = _tok_specs()
    return pl.pallas_call(
        body, name="inproj_fwd", grid=(NB, T // RT),
        out_shape=[jax.ShapeDtypeStruct((NB, T, PC), F32), jax.ShapeDtypeStruct((NB, T, D), BF16)],
        in_specs=[xs, cs, ms, ms, cms, cms, pl.BlockSpec((1, D), lambda b, j: (0, 0)),
                  pl.BlockSpec((PC, D), lambda b, j: (0, 0))],
        out_specs=[pl.BlockSpec((None, RT, PC), lambda b, j: (b, j, 0)),
                   pl.BlockSpec((None, RT, D), lambda b, j: (b, j, 0))],
        compiler_params=_cp(vmem=VMEM_BIG),
    )(x, ctx, sh1, sc1, csh1, csc1, n1w, w_in_p)


def _inproj_bwd(dp_dn, dz, dp_na, dgt, w_in_p, x, ctx, sh1, sc1, csh1, csc1, n1w, dx1):
    def body(d1q_ref, d1k_ref, d1v_ref, d2_ref, d3_ref, d4_ref, w_ref, x_ref, c_ref, sh_ref, sc_ref, csh_ref, csc_ref,
             w1_ref, dx1_ref, dpb_ref, gx_ref, dsh_ref, dsc_ref, dcsh_ref, dcsc_ref, dw1_ref):
        b, j = pl.program_id(0), pl.program_id(1)
        is_ctx = j == 0
        dp = jnp.concatenate([d1q_ref[...], d1k_ref[...], d1v_ref[...], d2_ref[...], d3_ref[...], d4_ref[...]],
                             axis=1).astype(BF16)
        dpb_ref[...] = dp
        dhm = jnp.dot(dp, w_ref[...], preferred_element_type=F32)
        xt = jnp.where(is_ctx, c_ref[...], x_ref[...])
        sh = jnp.where(is_ctx, csh_ref[...], sh_ref[...])
        sc = jnp.where(is_ctx, csc_ref[...], sc_ref[...])
        _, vjp = jax.vjp(_norm_mod, xt, w1_ref[...], sh, sc)
        dxt, dw1, dsh, dsc = vjp(dhm)

        @pl.when((b == 0) & (j == 0))
        def _():
            dw1_ref[...] = jnp.zeros_like(dw1_ref)
            dcsh_ref[...] = jnp.zeros_like(dcsh_ref)
            dcsc_ref[...] = jnp.zeros_like(dcsc_ref)

        dw1_ref[...] += dw1

        @pl.when(is_ctx)
        def _():
            dcsh_ref[...] += dsh
            dcsc_ref[...] += dsc

        @pl.when(j == 1)
        def _():
            dsh_ref[...] = jnp.zeros_like(dsh_ref)
            dsc_ref[...] = jnp.zeros_like(dsc_ref)

        @pl.when(j >= 1)
        def _():
            dsh_ref[...] += dsh
            dsc_ref[...] += dsc
            gx_ref[...] = dx1_ref[...] + dxt

    xs, cs, ms, cms = _tok_specs()
    tsp = lambda w: pl.BlockSpec((None, RT, w), lambda b, j: (b, j, 0))
    one = pl.BlockSpec((1, D), lambda b, j: (0, 0))
    return pl.pallas_call(
        body, name="inproj_bwd", grid=(NB, T // RT),
        out_shape=[jax.ShapeDtypeStruct((NB, T, PC), BF16), jax.ShapeDtypeStruct((NB, SEQ, D), F32),
                   jax.ShapeDtypeStruct((NB, 1, D), F32), jax.ShapeDtypeStruct((NB, 1, D), F32),
                   jax.ShapeDtypeStruct((1, 1, D), F32), jax.ShapeDtypeStruct((1, 1, D), F32),
                   jax.ShapeDtypeStruct((1, D), F32)],
        in_specs=[tsp(512), tsp(512), tsp(512), tsp(512), tsp(1536), tsp(128), pl.BlockSpec((PC, D), lambda b, j: (0, 0)),
                  xs, cs, ms, ms, cms, cms, one, _rows2d_spec(D)],
        out_specs=[tsp(PC), xs, ms, ms, cms, cms, one],
        compiler_params=_cp(vmem=VMEM_BIG),
    )(*dp_dn, dz, dp_na, dgt, w_in_p, x, ctx, sh1, sc1, csh1, csc1, n1w, dx1)


def _shift_masks():
    t = lax.broadcasted_iota(jnp.int32, (T, 1), 0)
    lo = jnp.where(t < CT, 0, CT)
    hi = jnp.where(t < CT, CT, T)
    return {off: (t + off >= lo) & (t + off < hi) for off in range(-2, 3)}


def _shifted(u, off, masks):
    s = u if off == 0 else pltpu.roll(u, (-off) % T, 0)
    return jnp.where(masks[off], s, 0.0)


def _conv_pre(u, w, masks):
    acc = _shifted(u, -2, masks) * w[0:1, :]
    for kk in range(1, 5):
        acc = acc + _shifted(u, kk - 2, masks) * w[kk:kk + 1, :]
    return acc


def _act_q(pre):
    a = _silu(pre)
    return a * lax.rsqrt(jnp.sum(a * a, axis=-1, keepdims=True) + EPS) * (128.0 ** -0.5)


def _act_k(pre):
    a = _silu(pre)
    return a * lax.rsqrt(jnp.sum(a * a, axis=-1, keepdims=True) + EPS)


def _dn_prep_fwd(p, cw):
    def body(pq_ref, pk_ref, pv_ref, wq_ref, wk_ref, wv_ref, q_ref, k_ref, v_ref):
        masks = _shift_masks()
        q_ref[...] = _act_q(_conv_pre(pq_ref[...], wq_ref[...], masks))
        k_ref[...] = _act_k(_conv_pre(pk_ref[...], wk_ref[...], masks))
        v_ref[...] = _silu(_conv_pre(pv_ref[...], wv_ref[...], masks))

    ps = lambda o: pl.BlockSpec((None, T, 128), lambda b, h: (b, 0, o + h))
    ws = lambda o: pl.BlockSpec((8, 128), lambda b, h: (0, o + h))
    os_ = pl.BlockSpec((None, T, 128), lambda b, h: (b, 0, h))
    return pl.pallas_call(
        body, name="dn_prep_fwd", grid=(NB, DNH), out_shape=[jax.ShapeDtypeStruct((NB, T, DNW), F32)] * 3,
        in_specs=[ps(0), ps(4), ps(8), ws(0), ws(4), ws(8)], out_specs=[os_] * 3,
        compiler_params=_cp(vmem=VMEM_BIG),
    )(p, p, p, cw, cw, cw)


def _dn_prep_bwd(p, cw, dqs, dks, dvs):
    def body(pq_ref, pk_ref, pv_ref, wq_ref, wk_ref, wv_ref, dqf_ref, dqb_ref, dkf_ref, dkb_ref, dvf_ref, dvb_ref,
             oq_ref, ok_ref, ov_ref, gq_ref, gk_ref, gv_ref):
        b = pl.program_id(1)
        masks = _shift_masks()

        def one(p_ref, w_ref, df_ref, db_ref, act, o_ref, g_ref):
            u = p_ref[...]
            w = w_ref[...]
            _, vjp = jax.vjp(act, _conv_pre(u, w, masks))
            (dpre,) = vjp(df_ref[...] + db_ref[...])
            du = _shifted(dpre, 2, masks) * w[0:1, :]
            rows = [jnp.sum(dpre * _shifted(u, -2, masks), axis=0, keepdims=True)]
            for kk in range(1, 5):
                du = du + _shifted(dpre, 2 - kk, masks) * w[kk:kk + 1, :]
                rows.append(jnp.sum(dpre * _shifted(u, kk - 2, masks), axis=0, keepdims=True))
            o_ref[...] = du
            gw = jnp.concatenate(rows + [jnp.zeros((3, 128), F32)], axis=0)

            @pl.when(b == 0)
            def _():
                g_ref[...] = gw

            @pl.when(b > 0)
            def _():
                g_ref[...] += gw

        one(pq_ref, wq_ref, dqf_ref, dqb_ref, _act_q, oq_ref, gq_ref)
        one(pk_ref, wk_ref, dkf_ref, dkb_ref, _act_k, ok_ref, gk_ref)
        one(pv_ref, wv_ref, dvf_ref, dvb_ref, _silu, ov_ref, gv_ref)

    ps = lambda o: pl.BlockSpec((None, T, 128), lambda h, b: (b, 0, o + h))
    ws = lambda o: pl.BlockSpec((8, 128), lambda h, b: (0, o + h))
    ds = pl.BlockSpec((None, T, 128), lambda h, b: (b, 0, h))
    outs = pl.pallas_call(
        body, name="dn_prep_bwd", grid=(DNH, NB),
        out_shape=[jax.ShapeDtypeStruct((NB, T, DNW), F32)] * 3 + [jax.ShapeDtypeStruct((8, DNW), F32)] * 3,
        in_specs=[ps(0), ps(4), ps(8), ws(0), ws(4), ws(8)] + [ds] * 6,
        out_specs=[ds] * 3
        + [pl.BlockSpec((8, 128), lambda h, b: (0, h))] * 3,
        compiler_params=_cp(vmem=VMEM_BIG),
    )(p, p, p, cw, cw, cw, *dqs, *dks, *dvs)
    return outs


def _gate_fn(a, prm):
    lane = lax.broadcasted_iota(jnp.int32, a.shape, 1)
    z = a + prm[1:2, :]
    sp = jnp.maximum(z, 0.0) + jnp.log(1.0 + jnp.exp(-jnp.abs(z)))
    g = -jnp.exp(prm[0:1, :]) * sp
    return jnp.where(lane < 8, g, jnp.where(lane < 16, _sigmoid(a), 0.0))


def _gates_fwd(p, prm):
    def body(a_ref, prm_ref, o_ref):
        o_ref[...] = _gate_fn(a_ref[...], prm_ref[...])

    return pl.pallas_call(
        body, name="gates_fwd", grid=(NB,), out_shape=jax.ShapeDtypeStruct((NB, T, 128), F32),
        in_specs=[pl.BlockSpec((None, T, 128), lambda b: (b, 0, COL_G // 128)), pl.BlockSpec((8, 128), lambda b: (0, 0))],
        out_specs=pl.BlockSpec((None, T, 128), lambda b: (b, 0, 0)),
    )(p, prm)


def _gates_bwd(p, prm, dgf, dgb):
    def body(a_ref, prm_ref, dgf_ref, dgb_ref, o_ref, dprm_ref):
        b = pl.program_id(0)
        tot = dgf_ref[...] + dgb_ref[...]
        _, vjp = jax.vjp(_gate_fn, a_ref[...], prm_ref[...])
        da, dprm = vjp(tot)
        o_ref[...] = da

        @pl.when(b == 0)
        def _():
            dprm_ref[...] = dprm

        @pl.when(b > 0)
        def _():
            dprm_ref[...] += dprm

    return pl.pallas_call(
        body, name="gates_bwd", grid=(NB,),
        out_shape=[jax.ShapeDtypeStruct((NB, T, 128), F32), jax.ShapeDtypeStruct((8, 128), F32)],
        in_specs=[pl.BlockSpec((None, T, 128), lambda b: (b, 0, COL_G // 128)), pl.BlockSpec((8, 128), lambda b: (0, 0)),
                  pl.BlockSpec((None, T, 128), lambda b: (b, 0, 0)), pl.BlockSpec((None, T, 128), lambda b: (b, 0, 0))],
        out_specs=[pl.BlockSpec((None, T, 128), lambda b: (b, 0, 0)), pl.BlockSpec((8, 128), lambda b: (0, 0))],
        compiler_params=_cp(vmem=VMEM_BIG),
    )(p, prm, dgf, dgb)


def _lockstep(gens, joint=None):
    n = len(gens)
    res, vals = [None] * n, [None] * n
    while True:
        msgs = []
        for i, g in enumerate(gens):
            try:
                msgs.append((g.send(vals[i]), False))
            except StopIteration as e:
                res[i] = e.value
                msgs.append((None, True))
        if msgs[0][1]:
            return res
        if msgs[0][0] is not None:
            vals = list(joint(tuple(m[0] for m in msgs)))
        else:
            vals = [None] * n


def _tri_inv_gen(m, md, m1, m2, eye):
    n = -(m * md)
    t = eye + n
    pw = n
    for _ in range(3):
        pw = _dg(pw, pw, _NN)
        yield
        t = t + _dg(t, pw, _NN)
        yield
    a = _dg(t, m * m1, _NN)
    yield
    t = t - _dg(a, t, _NN)
    yield
    a = _dg(t, m * m2, _NN)
    yield
    return t - _dg(a, t, _NN)


def _tri_inv_bwd_gen(t, g):
    tt = t.T
    a = _dg(tt, g, _NN)
    yield
    return -_dg(a, tt, _NN)


@jax.custom_vjp
def _tri_inv(ms, md, m1, m2, eye):
    return tuple(_lockstep([_tri_inv_gen(m, md, m1, m2, eye) for m in ms]))


def _tri_inv_f(ms, md, m1, m2, eye):
    ts = _tri_inv(ms, md, m1, m2, eye)
    return ts, (ts, md)


def _tri_inv_b(res, gs):
    ts, md = res
    z = jnp.zeros_like(md)
    return (tuple(_lockstep([_tri_inv_bwd_gen(t, g) for t, g in zip(ts, gs)])), z, z, z, z)


_tri_inv.defvjp(_tri_inv_f, _tri_inv_b)


@jax.custom_vjp
def _tri_inv_saved(ms, ts):
    return tuple(t.astype(F32) for t in ts)


def _tri_inv_saved_f(ms, ts):
    return _tri_inv_saved(ms, ts), ts


def _tri_inv_saved_b(ts, gs):
    dms = tuple(_lockstep([_tri_inv_bwd_gen(t, g) for t, g in zip(ts, gs)]))
    return dms, tuple(jnp.zeros_like(t) for t in ts)


_tri_inv_saved.defvjp(_tri_inv_saved_f, _tri_inv_saved_b)


SR = DNH * CH


def _scan_masks():
    i, j = np.arange(SR)[:, None], np.arange(SR)[None, :]
    same = (i >> 6) == (j >> 6)
    cm = np.zeros((2, 6, SR, SR), np.float32)
    for d in range(2):
        diff = i - j if d == 0 else j - i
        cm[d, 0] = same & (diff >= 0)
        cm[d, 1] = same & (diff > 0)
        cm[d, 2] = (i >> 4) == (j >> 4)
        cm[d, 3] = ((i >> 5) == (j >> 5)) & ((i >> 4) != (j >> 4))
        cm[d, 4] = (i >> 5) != (j >> 5)
        cm[d, 5] = i == j
    t, u = np.arange(CH)[:, None], np.arange(CH)[None, :]
    c64 = np.stack([t >= u, t <= u]).astype(np.float32)
    return jnp.asarray(cm), jnp.asarray(c64)


def _hsl(h):
    return slice(CH * h, CH * h + CH), slice(128 * h, 128 * h + 128)


@jax.custom_vjp
def _heads_mm(x, s):
    return jnp.concatenate([_dg(x[_hsl(h)[0]], s[:, _hsl(h)[1]], _NN) for h in range(DNH)], axis=0)


def _heads_mm_b(res, g):
    x, s = res
    dx = jnp.concatenate([_dg(g[_hsl(h)[0]], s[:, _hsl(h)[1]], _NT) for h in range(DNH)], axis=0)
    ds = jnp.concatenate([_dg(x[_hsl(h)[0]], g[_hsl(h)[0]], _TN) for h in range(DNH)], axis=1)
    return dx, ds


_heads_mm.defvjp(lambda x, s: (_heads_mm(x, s), (x, s)), _heads_mm_b)


@jax.custom_vjp
def _heads_outer(k, v):
    return jnp.concatenate([_dg(k[_hsl(h)[0]], v[_hsl(h)[0]], _TN) for h in range(DNH)], axis=1)


def _heads_outer_b(res, g):
    k, v = res
    dk = jnp.concatenate([_dg(v[_hsl(h)[0]], g[:, _hsl(h)[1]], _NT) for h in range(DNH)], axis=0)
    dv = jnp.concatenate([_dg(k[_hsl(h)[0]], g[:, _hsl(h)[1]], _NN) for h in range(DNH)], axis=0)
    return dk, dv


_heads_outer.defvjp(lambda k, v: (_heads_outer(k, v), (k, v)), _heads_outer_b)


@jax.custom_vjp
def _mm_shared_lhs(t, a, b):
    r = _dg(t, jnp.concatenate([a, b], axis=1), _NN)
    return r[:, :a.shape[1]], r[:, a.shape[1]:]


def _mm_shared_lhs_b(res, gs):
    t, a, b = res
    g = jnp.concatenate(gs, axis=1)
    dab = _dg(t, g, _TN)
    return _dg(g, jnp.concatenate([a, b], axis=1), _NT), dab[:, :a.shape[1]], dab[:, a.shape[1]:]


_mm_shared_lhs.defvjp(lambda t, a, b: (_mm_shared_lhs(t, a, b), (t, a, b)), _mm_shared_lhs_b)


@jax.custom_vjp
def _mm_nt_shared_rhs(a, b, k):
    r = _dg(jnp.concatenate([a, b], axis=0), k, _NT)
    return r[:a.shape[0]], r[a.shape[0]:]


def _mm_nt_shared_rhs_b(res, gs):
    a, b, k = res
    g = jnp.concatenate(gs, axis=0)
    dab = _dg(g, k, _NN)
    return dab[:a.shape[0]], dab[a.shape[0]:], _dg(g, jnp.concatenate([a, b], axis=0), _TN)


_mm_nt_shared_rhs.defvjp(lambda a, b, k: (_mm_nt_shared_rhs(a, b, k), (a, b, k)), _mm_nt_shared_rhs_b)


def _chunk_gen(q, k, v, gb, s, cm, c64, d):
    incl, strict = cm[0], cm[1]
    lane = lax.broadcasted_iota(jnp.int32, (1, 128), 1)
    oh = lambda n: jnp.where(lane == n, 1.0, 0.0).astype(F32)
    col = lambda x, base: jnp.concatenate(
        [jnp.sum(x * oh(base + d * DNH + h), axis=1, keepdims=True) for h in range(DNH)], axis=0)
    beta = col(gb, 8)
    gam = col(_mask_mm(c64, gb), 0)
    yield
    tot = jnp.sum(gb, axis=0, keepdims=True)
    gtots = [jnp.sum(tot * oh(d * DNH + h), axis=1, keepdims=True) for h in range(DNH)]
    gtot = jnp.concatenate([jnp.broadcast_to(t, (CH, 1)) for t in gtots], axis=0)
    eg = jnp.concatenate([jnp.broadcast_to(jnp.exp(t), (1, 128)) for t in gtots], axis=1)
    row0 = jnp.where(lax.broadcasted_iota(jnp.int32, (128, 1), 0) == 0, 1.0, 0.0).astype(F32)
    gam_row = jnp.sum(jnp.broadcast_to(gam, (SR, 128)).T * row0, axis=0, keepdims=True)
    dec = jnp.exp(jnp.where(incl > 0.5, gam - gam_row, NEG))
    e_gam = jnp.exp(gam)
    kb = k * beta
    kk, qk = _mm_nt_shared_rhs(kb, q, k)
    yield
    tm = yield strict * (kk * dec)
    u, w = _mm_shared_lhs(tm, v * beta, kb * e_gam)
    yield
    qk = qk * dec
    ws = _heads_mm(w, s)
    yield
    v_new = u - ws
    qs = _heads_mm(q * e_gam, s)
    yield
    o = qs + _mm(qk, v_new)
    yield
    s_new = s * eg + _heads_outer(k * jnp.exp(gtot - gam), v_new)
    return o, s_new, tm


def _chunks(args, cm_ref, c64_ref, saved=None):
    cm0 = cm_ref[0]
    if saved is None:
        inv = lambda ms: _tri_inv(ms, cm0[2], cm0[3], cm0[4], cm0[5])
    else:
        inv = lambda ms: _tri_inv_saved(ms, saved)
    return _lockstep([_chunk_gen(q, k, v, gb, s, cm_ref[d], c64_ref[d], d) for q, k, v, gb, s, d in args], inv)


def _chunk_rev(s):
    return jnp.where(s < NCH_CTX, NCH_CTX - 1 - s, NCH + NCH_CTX - 1 - s)


def _scan_specs(step):
    cf = lambda s: step(s)
    cb = lambda s: _chunk_rev(step(s))
    tok = lambda c: pl.BlockSpec((NB, CH, DNW), lambda s: (0, c(s), 0))
    gat = lambda c: pl.BlockSpec((NB, CH, 128), lambda s: (0, c(s), 0))
    sta = lambda c: pl.BlockSpec((NB, None, 128, DNW), lambda s: (0, c(s), 0, 0))
    inv = lambda c: pl.BlockSpec((NB, None, SR, SR), lambda s: (0, c(s), 0, 0))
    return (tok(cf), gat(cf), sta(cf), inv(cf)), (tok(cb), gat(cb), sta(cb), inv(cb))


def _const_specs():
    return [pl.BlockSpec((2, 6, SR, SR), lambda s: (0, 0, 0, 0)), pl.BlockSpec((2, CH, CH), lambda s: (0, 0, 0))]


SCAN_CHAINS = [(b, d) for b in range(NB) for d in range(2)]


def _stack(ref, b):
    return jnp.concatenate([ref[b, :, 128 * h:128 * h + 128] for h in range(DNH)], axis=0)


def _unstack_to(ref, b, val):
    for h in range(DNH):
        ref[b, :, 128 * h:128 * h + 128] = val[CH * h:CH * h + CH]


def _gather_plan(ins, outs, pos):
    me = _slot(pos)
    locals_, xfers = [], []
    for src, dst in zip(ins, outs):
        locals_.append((src, dst.at[me]))
        for k in range(1, 8):
            peer = _flip(pos, k)
            xfers.append((src, dst.at[me], dst.at[_slot(peer)], peer))
    return locals_, xfers


def _scan_fwd(q, k, v, gates, carry=()):
    nc = len(carry)

    def body(*refs):
        cm_ref, c64_ref, qf, kf, vf, gf, qb, kb, vb, gb = refs[:10]
        c_in = refs[10:10 + nc]
        of_ref, sf_ref, tf_ref, ob_ref, sb_ref, tb_ref = refs[10 + nc:16 + nc]
        c_out = refs[16 + nc:16 + 2 * nc]
        s_ref = refs[16 + 2 * nc]
        sems = refs[17 + 2 * nc:]
        if nc:
            @pl.when(pl.program_id(0) == 0)
            def _():
                _xfer_start(_gather_plan(c_in, c_out, _my_pos()), sems)

        @pl.when(pl.program_id(0) == 0)
        def _():
            s_ref[...] = jnp.zeros_like(s_ref)

        dirs = ((qf, kf, vf, gf, of_ref, sf_ref, tf_ref), (qb, kb, vb, gb, ob_ref, sb_ref, tb_ref))
        args = [(_stack(dirs[d][0], b), _stack(dirs[d][1], b), _stack(dirs[d][2], b), dirs[d][3][b], s_ref[i], d)
                for i, (b, d) in enumerate(SCAN_CHAINS)]
        for i, (o, s_new, tm) in enumerate(_chunks(args, cm_ref, c64_ref)):
            b, d = SCAN_CHAINS[i]
            dirs[d][5][b] = args[i][4]
            dirs[d][6][b] = tm.astype(BF16)
            _unstack_to(dirs[d][4], b, o)
            s_ref[i] = s_new

        if nc:
            @pl.when(pl.program_id(0) == NCH - 1)
            def _():
                _xfer_wait(_gather_plan(c_in, c_out, _my_pos()), sems)

    (tf, gf_, sf_, if_), (tb, gb_, sb_, ib_) = _scan_specs(lambda s: s)
    o_sh = jax.ShapeDtypeStruct((NB, T, DNW), F32)
    st_sh = jax.ShapeDtypeStruct((NB, NCH, 128, DNW), F32)
    inv_sh = jax.ShapeDtypeStruct((NB, NCH, SR, SR), BF16)
    hbm = pl.BlockSpec(memory_space=pltpu.HBM)
    return pl.pallas_call(
        body, name="scan_fwd", grid=(NCH,),
        out_shape=[o_sh, st_sh, inv_sh, o_sh, st_sh, inv_sh] + [jax.ShapeDtypeStruct((8,) + h.shape, h.dtype) for h in carry],
        in_specs=_const_specs() + [tf, tf, tf, gf_, tb, tb, tb, gb_] + [hbm] * nc,
        out_specs=[tf, sf_, if_, tb, sb_, ib_] + [hbm] * nc,
        scratch_shapes=[pltpu.VMEM((len(SCAN_CHAINS), 128, DNW), F32)] + (_xfer_sems(nc, 7 * nc) if nc else []),
        compiler_params=_cp(vmem=VMEM_BIG),
    )(*_scan_masks(), q, k, v, gates, q, k, v, gates, *carry)


def _scan_bwd(q, k, v, gates, st_f, st_b, inv_f, inv_b, do, carry=()):
    nc = len(carry)

    def body(*refs):
        (cm_ref, c64_ref, qf, kf, vf, gf, sf, tf_, dof, qb, kb, vb, gb, sb, tb_, dob) = refs[:16]
        c_in = refs[16:16 + nc]
        dqf, dkf, dvf, dgf, dqb, dkb, dvb, dgb = refs[16 + nc:24 + nc]
        c_out = refs[24 + nc:24 + 2 * nc]
        ds_ref = refs[24 + 2 * nc]
        sems = refs[25 + 2 * nc:]
        first = pl.program_id(0) == 0
        last = pl.program_id(0) == NCH - 1
        if nc:
            @pl.when(first)
            def _():
                _xfer_start(_scatter_plan(c_in, c_out, _my_pos()), sems)

        @pl.when(first)
        def _():
            ds_ref[...] = jnp.zeros_like(ds_ref)

        dirs = ((qf, kf, vf, gf, sf, dof, dqf, dkf, dvf, dgf), (qb, kb, vb, gb, sb, dob, dqb, dkb, dvb, dgb))
        prim = tuple((_stack(dirs[d][0], b), _stack(dirs[d][1], b), _stack(dirs[d][2], b), dirs[d][3][b], dirs[d][4][b])
                     for b, d in SCAN_CHAINS)
        cots = tuple((_stack(dirs[d][5], b), ds_ref[i]) for i, (b, d) in enumerate(SCAN_CHAINS))
        saved = tuple((tf_, tb_)[d][b] for b, d in SCAN_CHAINS)
        _, vjp = jax.vjp(lambda ps: tuple(r[:2] for r in _chunks([p + (SCAN_CHAINS[i][1],) for i, p in enumerate(ps)],
                                                                cm_ref, c64_ref, saved)), prim)
        (grads,) = vjp(cots)
        for i, (dq, dk, dv, dg, ds) in enumerate(grads):
            b, d = SCAN_CHAINS[i]
            _unstack_to(dirs[d][6], b, dq)
            _unstack_to(dirs[d][7], b, dk)
            _unstack_to(dirs[d][8], b, dv)
            dirs[d][9][b] = dg
            ds_ref[i] = ds

        if nc:
            @pl.when(last)
            def _():
                _xfer_wait(_scatter_plan(c_in, c_out, _my_pos()), sems)

    (tf, gf_, sf_, if_), (tb, gb_, sb_, ib_) = _scan_specs(lambda s: NCH - 1 - s)
    o_sh = jax.ShapeDtypeStruct((NB, T, DNW), F32)
    g_sh = jax.ShapeDtypeStruct((NB, T, 128), F32)
    hbm = pl.BlockSpec(memory_space=pltpu.HBM)
    return pl.pallas_call(
        body, name="scan_bwd", grid=(NCH,),
        out_shape=[o_sh, o_sh, o_sh, g_sh] * 2 + [jax.ShapeDtypeStruct(g.shape, g.dtype) for g in carry],
        in_specs=_const_specs() + [tf, tf, tf, gf_, sf_, if_, tf, tb, tb, tb, gb_, sb_, ib_, tb] + [hbm] * nc,
        out_specs=[tf, tf, tf, gf_, tb, tb, tb, gb_] + [hbm] * nc,
        scratch_shapes=[pltpu.VMEM((len(SCAN_CHAINS), 128, DNW), F32)] + (_xfer_sems(nc, 7 * nc) if nc else []),
        compiler_params=_cp(vmem=VMEM_BIG),
    )(*_scan_masks(), q, k, v, gates, st_f, inv_f, do, q, k, v, gates, st_b, inv_b, do, *carry)


def _segmean(a, bd):
    hi = a.astype(BF16)
    lo = (a - hi.astype(F32)).astype(BF16)
    return jnp.dot(hi, bd, preferred_element_type=F32) + jnp.dot(lo, bd, preferred_element_type=F32)


def _na_prep_fwd(p, qw, kw, bd):
    def body(q_ref, k_ref, v_ref, qw_ref, kw_ref, bd_ref, oq_ref, ok_ref, ov_ref):
        bdm = bd_ref[...]
        q = q_ref[...]
        k = k_ref[...]
        oq_ref[...] = (q * lax.rsqrt(_segmean(q * q, bdm) + EPS) * qw_ref[...] * 0.125).astype(BF16)
        ok_ref[...] = (k * lax.rsqrt(_segmean(k * k, bdm) + EPS) * kw_ref[...]).astype(BF16)
        ov_ref[...] = v_ref[...].astype(BF16)

    ps = lambda o: pl.BlockSpec((None, RT, NAW), lambda b, j: (b, j, COL_NA // NAW + o))
    one = pl.BlockSpec((1, NAW), lambda b, j: (0, 0))
    os_ = pl.BlockSpec((None, RT, NAW), lambda b, j: (b, j, 0))
    return pl.pallas_call(
        body, name="na_prep_fwd", grid=(NB, T // RT), out_shape=[jax.ShapeDtypeStruct((NB, T, NAW), BF16)] * 3,
        in_specs=[ps(0), ps(1), ps(2), one, one, pl.BlockSpec((NAW, NAW), lambda b, j: (0, 0))], out_specs=[os_] * 3,
    )(p, p, p, qw, kw, bd)


def _na_prep_bwd(p, qw, kw, bd, dq, dk, dv):
    def body(q_ref, k_ref, qw_ref, kw_ref, bd_ref, dq_ref, dk_ref, dv_ref, o_ref, gq_ref, gk_ref):
        b, j = pl.program_id(0), pl.program_id(1)
        bdm = bd_ref[...]

        def one(x, w, dy, scale):
            r = lax.rsqrt(_segmean(x * x, bdm) + EPS)
            g = dy * w * scale
            dx = r * g - x * (r * r * r) * _segmean(g * x, bdm)
            dw = jnp.sum(dy * x * r * scale, axis=0, keepdims=True)
            return dx, dw

        dqv = jnp.where(j == 0, 0.0, dq_ref[...])
        dxq, dwq = one(q_ref[...], qw_ref[...], dqv, 0.125)
        dxk, dwk = one(k_ref[...], kw_ref[...], dk_ref[...], 1.0)
        o_ref[...] = jnp.concatenate([dxq, dxk, dv_ref[...]], axis=1)
        pad = jnp.zeros((7, NAW), F32)

        @pl.when((b == 0) & (j == 0))
        def _():
            gq_ref[...] = jnp.zeros_like(gq_ref)
            gk_ref[...] = jnp.zeros_like(gk_ref)

        gq_ref[...] += jnp.concatenate([dwq, pad], axis=0)
        gk_ref[...] += jnp.concatenate([dwk, pad], axis=0)

    ps = lambda o: pl.BlockSpec((None, RT, NAW), lambda b, j: (b, j, COL_NA // NAW + o))
    one = pl.BlockSpec((1, NAW), lambda b, j: (0, 0))
    ts = pl.BlockSpec((None, RT, NAW), lambda b, j: (b, j, 0))
    acc = pl.BlockSpec((8, NAW), lambda b, j: (0, 0))
    return pl.pallas_call(
        body, name="na_prep_bwd", grid=(NB, T // RT),
        out_shape=[jax.ShapeDtypeStruct((NB, T, 1536), F32), jax.ShapeDtypeStruct((8, NAW), F32),
                   jax.ShapeDtypeStruct((8, NAW), F32)],
        in_specs=[ps(0), ps(1), one, one, pl.BlockSpec((NAW, NAW), lambda b, j: (0, 0)),
                  pl.BlockSpec((None, RT, NAW), lambda b, j: (b, jnp.maximum(j - 1, 0), 0)), ts, ts],
        out_specs=[pl.BlockSpec((None, RT, 1536), lambda b, j: (b, j, 0)), acc, acc],
        compiler_params=_cp(vmem=VMEM_BIG),
    )(p, p, qw, kw, bd, dq, dk, dv)


def _na_row0(r):
    return jnp.clip(r - WIN_ROWS // 2, 0, ROWS - WIN_ROWS)


def _na_cls(r):
    return _na_row0(r) - r + WIN_ROWS - 1


NHS = 4
NLW = NHS * 64
NSR = NHS * GRID_W


def _na_stack(x):
    lane = lax.broadcasted_iota(jnp.int32, (1, NLW), 1)
    return jnp.concatenate([jnp.where((lane >> 6) == h, x, jnp.zeros_like(x)) for h in range(NHS)], axis=0)


def _na_unstack(y):
    lane = lax.broadcasted_iota(jnp.int32, (1, NLW), 1)
    out = jnp.zeros((GRID_W, NLW), y.dtype)
    for h in range(NHS):
        out = jnp.where((lane >> 6) == h, y[GRID_W * h:GRID_W * h + GRID_W], out)
    return out


def _na_fwd_gen(q, kw, vw, kc, vc, bias):
    q4 = _na_stack(q)
    s_w = lax.dot_general(q4, kw, _NT, preferred_element_type=F32) + bias
    yield
    s_c = lax.dot_general(q4, kc, _NT, preferred_element_type=F32)
    yield
    mx = jnp.maximum(jnp.max(s_w, axis=1, keepdims=True), jnp.max(s_c, axis=1, keepdims=True))
    p_w = jnp.exp(s_w - mx)
    p_c = jnp.exp(s_c - mx)
    inv = 1.0 / (jnp.sum(p_w, axis=1, keepdims=True) + jnp.sum(p_c, axis=1, keepdims=True))
    yield
    o_w = jnp.dot(p_w.astype(BF16), vw, preferred_element_type=F32)
    yield
    o4 = (o_w + jnp.dot(p_c.astype(BF16), vc, preferred_element_type=F32)) * inv
    return _na_unstack(o4)


NRF = 2


def _na_fwd(qn, kn, vb, bias, carry=()):
    nc = len(carry)
    grid = (NB, ROWS // NRF)
    ng = NAH // NHS

    def body(*refs):
        q_ref, k_ref, v_ref = refs[:3]
        b_refs, rest = refs[3:3 + NRF], refs[3 + NRF:]
        c_in, o_ref, c_out, sems = rest[:nc], rest[nc], rest[nc + 1:2 * nc + 1], rest[2 * nc + 1:]
        ids = [pl.program_id(a) for a in range(2)]
        if nc:
            @pl.when((ids[0] == 0) & (ids[1] == 0))
            def _():
                _xfer_start(_gather_plan(c_in, c_out, _my_pos()), sems)

        gens = []
        for i in range(NRF):
            start = pl.multiple_of(CT + _na_row0(NRF * ids[1] + i) * GRID_W, GRID_W)
            rw = slice(GRID_W * i, GRID_W * (i + 1))
            for g in range(ng):
                ln = slice(NLW * g, NLW * g + NLW)
                gens.append(_na_fwd_gen(q_ref[rw, ln], k_ref[pl.ds(start, NWIN), ln], v_ref[pl.ds(start, NWIN), ln],
                                        k_ref[0:CT, ln], v_ref[0:CT, ln], b_refs[i][g]))
        for n, o in enumerate(_lockstep(gens)):
            i, g = n // ng, n % ng
            o_ref[GRID_W * i:GRID_W * (i + 1), NLW * g:NLW * g + NLW] = o.astype(BF16)

        if nc:
            @pl.when((ids[0] == grid[0] - 1) & (ids[1] == grid[1] - 1))
            def _():
                _xfer_wait(_gather_plan(c_in, c_out, _my_pos()), sems)

    qs = pl.BlockSpec((None, NRF * GRID_W, NAW), lambda b, j: (b, CT // (NRF * GRID_W) + j, 0))
    kv = pl.BlockSpec((None, T, NAW), lambda b, j: (b, 0, 0))
    bss = [pl.BlockSpec((None, ng, NSR, NWIN), lambda b, j, i=i: (_na_cls(NRF * j + i), 0, 0, 0)) for i in range(NRF)]
    lat = pl.BlockSpec((None, NRF * GRID_W, NAW), lambda b, j: (b, j, 0))
    hbm = pl.BlockSpec(memory_space=pltpu.HBM)
    res = pl.pallas_call(
        body, name="na_fwd", grid=grid,
        out_shape=[jax.ShapeDtypeStruct((NB, SEQ, NAW), BF16)] + [jax.ShapeDtypeStruct((8,) + h.shape, h.dtype) for h in carry],
        in_specs=[qs, kv, kv] + bss + [hbm] * nc, out_specs=[lat] + [hbm] * nc,
        scratch_shapes=_xfer_sems(nc, 7 * nc) if nc else [],
        compiler_params=_cp(vmem=VMEM_BIG),
    )(qn, kn, vb, *([bias] * NRF), *carry)
    return res[0], list(res[1:])


def _na_bwd_gen(q, kw, vw, kc, vc, bias, do):
    q4 = _na_stack(q)
    s_w = lax.dot_general(q4, kw, _NT, preferred_element_type=F32) + bias
    yield
    s_c = lax.dot_general(q4, kc, _NT, preferred_element_type=F32)
    yield
    mx = jnp.maximum(jnp.max(s_w, axis=1, keepdims=True), jnp.max(s_c, axis=1, keepdims=True))
    p_w = jnp.exp(s_w - mx)
    p_c = jnp.exp(s_c - mx)
    inv = 1.0 / (jnp.sum(p_w, axis=1, keepdims=True) + jnp.sum(p_c, axis=1, keepdims=True))
    p_w = p_w * inv
    p_c = p_c * inv
    do4 = _na_stack(do.astype(BF16))
    dp_w = lax.dot_general(do4, vw, _NT, preferred_element_type=F32)
    yield
    dp_c = lax.dot_general(do4, vc, _NT, preferred_element_type=F32)
    yield
    delta = jnp.sum(p_w * dp_w, axis=1, keepdims=True) + jnp.sum(p_c * dp_c, axis=1, keepdims=True)
    ds_w = p_w * (dp_w - delta)
    ds_c = p_c * (dp_c - delta)
    ds_wb, ds_cb = ds_w.astype(BF16), ds_c.astype(BF16)
    dq4 = jnp.dot(ds_wb, kw, preferred_element_type=F32)
    yield
    dq4 = dq4 + jnp.dot(ds_cb, kc, preferred_element_type=F32)
    yield
    dkw = lax.dot_general(ds_wb, q4, _TN, preferred_element_type=F32)
    yield
    dvw = lax.dot_general(p_w.astype(BF16), do4, _TN, preferred_element_type=F32)
    yield
    dkc = lax.dot_general(ds_cb, q4, _TN, preferred_element_type=F32)
    yield
    dvc = lax.dot_general(p_c.astype(BF16), do4, _TN, preferred_element_type=F32)
    return ds_w, _na_unstack(dq4), dkw, dvw, dkc, dvc


NRS = 4


def _na_bwd(qn, kn, vb, bias, do, carry=()):
    nc = len(carry)
    grid = (NAH // NHS, NB, ROWS // NRS)

    def body(q_ref, k_ref, v_ref, *rest):
        b_refs, do_ref, c_in = rest[:NRS], rest[NRS], rest[NRS + 1:NRS + 1 + nc]
        dq_ref, dk_ref, dv_ref, db_ref = rest[NRS + 1 + nc:NRS + 5 + nc]
        c_out, sems = rest[NRS + 5 + nc:NRS + 5 + 2 * nc], rest[NRS + 5 + 2 * nc:]
        ids = [pl.program_id(a) for a in range(3)]
        b, j = ids[1], ids[2]
        if nc:
            @pl.when((ids[0] == 0) & (b == 0) & (j == 0))
            def _():
                _xfer_start(_scatter_plan(c_in, c_out, _my_pos()), sems)

        @pl.when((b == 0) & (j == 0))
        def _():
            db_ref[...] = jnp.zeros_like(db_ref)

        @pl.when(j == 0)
        def _():
            dk_ref[...] = jnp.zeros_like(dk_ref)
            dv_ref[...] = jnp.zeros_like(dv_ref)

        kc, vc = k_ref[0:CT, :], v_ref[0:CT, :]
        rows = [NRS * j + i for i in range(NRS)]
        starts = [pl.multiple_of(CT + _na_row0(r) * GRID_W, GRID_W) for r in rows]
        gens = [_na_bwd_gen(q_ref[GRID_W * i:GRID_W * (i + 1), :], k_ref[pl.ds(starts[i], NWIN), :],
                            v_ref[pl.ds(starts[i], NWIN), :], kc, vc, b_refs[i][...],
                            do_ref[GRID_W * i:GRID_W * (i + 1), :]) for i in range(NRS)]
        for i, (ds_w, dq, dkw, dvw, dkc, dvc) in enumerate(_lockstep(gens)):
            db_ref[_na_cls(rows[i])] += ds_w
            dq_ref[GRID_W * i:GRID_W * (i + 1), :] = dq
            dk_ref[pl.ds(starts[i], NWIN), :] += dkw
            dv_ref[pl.ds(starts[i], NWIN), :] += dvw
            dk_ref[0:CT, :] += dkc
            dv_ref[0:CT, :] += dvc

        if nc:
            @pl.when((ids[0] == grid[0] - 1) & (b == grid[1] - 1) & (j == grid[2] - 1))
            def _():
                _xfer_wait(_scatter_plan(c_in, c_out, _my_pos()), sems)

    qs = pl.BlockSpec((None, NRS * GRID_W, NLW), lambda g, b, j: (b, CT // (NRS * GRID_W) + j, g))
    kv = pl.BlockSpec((None, T, NLW), lambda g, b, j: (b, 0, g))
    bss = [pl.BlockSpec((None, None, NSR, NWIN), lambda g, b, j, i=i: (_na_cls(NRS * j + i), g, 0, 0)) for i in range(NRS)]
    lat = pl.BlockSpec((None, NRS * GRID_W, NLW), lambda g, b, j: (b, j, g))
    hbm = pl.BlockSpec(memory_space=pltpu.HBM)
    res = pl.pallas_call(
        body, name="na_bwd", grid=grid,
        out_shape=[jax.ShapeDtypeStruct((NB, SEQ, NAW), F32)] + [jax.ShapeDtypeStruct((NB, T, NAW), F32)] * 2
        + [jax.ShapeDtypeStruct((WIN_ROWS, NAH // NHS, NSR, NWIN), F32)]
        + [jax.ShapeDtypeStruct(g.shape, g.dtype) for g in carry],
        in_specs=[qs, kv, kv] + bss + [lat] + [hbm] * nc,
        out_specs=[lat, kv, kv, pl.BlockSpec((WIN_ROWS, None, NSR, NWIN), lambda g, b, j: (0, g, 0, 0))] + [hbm] * nc,
        scratch_shapes=_xfer_sems(nc, 7 * nc) if nc else [],
        compiler_params=_cp(vmem=VMEM_BIG),
    )(qn, kn, vb, *([bias] * NRS), do, *carry)
    return res[:4], list(res[4:])


def _na_tables():
    cols = np.arange(GRID_W)
    win_start = np.clip(cols - WIN_COLS // 2, 0, GRID_W - WIN_COLS)
    kc = cols[None, :]
    valid = (kc >= win_start[:, None]) & (kc < win_start[:, None] + WIN_COLS)
    rel = np.clip(kc - cols[:, None] + WIN_COLS - 1, 0, 2 * WIN_COLS - 2)
    return valid, rel


def _bias_table(rpb):
    valid, rel = _na_tables()
    nrel = 2 * WIN_COLS - 1
    shifts = np.stack([(rel == r) & valid for r in range(nrel)]).astype(np.float32)
    neg = np.where(valid, 0.0, NEG).astype(np.float32)

    def body(rpb_ref, sh_ref, neg_ref, o_ref):
        h = pl.program_id(0)
        for dr in range(2 * WIN_ROWS - 1):
            tile = neg_ref[...]
            for r in range(nrel):
                tile = tile + rpb_ref[h, dr, r] * sh_ref[r]
            for cls in range(WIN_ROWS):
                i = dr - cls
                if 0 <= i < WIN_ROWS:
                    o_ref[cls, :, i * GRID_W:(i + 1) * GRID_W] = tile

    return pl.pallas_call(
        body, name="bias_table", grid=(NAH,),
        out_shape=jax.ShapeDtypeStruct((WIN_ROWS, NAH // NHS, NSR, NWIN), F32),
        in_specs=[pl.BlockSpec(memory_space=pltpu.SMEM), pl.BlockSpec((nrel, GRID_W, GRID_W), lambda h: (0, 0, 0)),
                  pl.BlockSpec((GRID_W, GRID_W), lambda h: (0, 0))],
        out_specs=pl.BlockSpec((WIN_ROWS, None, GRID_W, NWIN), lambda h: (0, h // NHS, h % NHS, 0)),
    )(rpb, jnp.asarray(shifts), jnp.asarray(neg))


def _rpb_fold(db):
    def body1(d_ref, o_ref):
        for dr in range(2 * WIN_ROWS - 1):
            acc = None
            for c in range(WIN_ROWS):
                i = dr - c
                if 0 <= i < WIN_ROWS:
                    blk = d_ref[c, :, i * GRID_W:(i + 1) * GRID_W]
                    acc = blk if acc is None else acc + blk
            o_ref[dr] = acc

    t15 = pl.pallas_call(
        body1, name="rpb_fold_rows", grid=(NAH,),
        out_shape=jax.ShapeDtypeStruct((NAH, 2 * WIN_ROWS - 1, GRID_W, GRID_W), F32),
        in_specs=[pl.BlockSpec((WIN_ROWS, None, GRID_W, NWIN), lambda h: (0, h, 0, 0))],
        out_specs=pl.BlockSpec((None, 2 * WIN_ROWS - 1, GRID_W, GRID_W), lambda h: (h, 0, 0, 0)),
        compiler_params=_cp(vmem=VMEM_BIG),
    )(db.reshape(WIN_ROWS, NAH, GRID_W, NWIN))
    valid, rel = _na_tables()
    onehot = np.zeros((GRID_W * GRID_W, 128), np.float32)
    flat_rel, flat_valid = rel.reshape(-1), valid.reshape(-1)
    onehot[np.arange(GRID_W * GRID_W)[flat_valid], flat_rel[flat_valid]] = 1.0

    def body2(a_ref, oh_ref, o_ref):
        o_ref[...] = _hmm(a_ref[...], oh_ref[...])

    vm = pl.BlockSpec(memory_space=pltpu.VMEM)
    out = pl.pallas_call(
        body2, name="rpb_fold_cols", out_shape=jax.ShapeDtypeStruct((NAH * 15, 128), F32),
        in_specs=[vm, vm], out_specs=vm, compiler_params=_cp(vmem=VMEM_BIG),
    )(t15.reshape(NAH * 15, GRID_W * GRID_W), jnp.asarray(onehot))
    return out[:, :31].reshape(NAH, 15, 31)


def _dn_out(o, z, w):
    outs = []
    for h in range(DNH):
        sl = slice(128 * h, 128 * h + 128)
        oh = o[:, sl]
        r = lax.rsqrt(jnp.mean(oh * oh, axis=-1, keepdims=True) + EPS)
        outs.append(oh * r * w[:, sl] * _silu(z[:, sl]))
    return jnp.concatenate(outs, axis=1)


def _outproj_fwd(of, ob, p, na, onw, x, g1, w_out):
    def body(of_ref, ob_ref, z_ref, na_ref, w_ref, x_ref, g_ref, wo_ref, x1_ref, ai_ref, ao_ref):
        dn = _dn_out(of_ref[...] + ob_ref[...], z_ref[...], w_ref[...])
        ai = jnp.concatenate([dn.astype(BF16), na_ref[...]], axis=1)
        ai_ref[...] = ai
        ao = jnp.dot(ai, wo_ref[...], preferred_element_type=F32)
        ao_ref[...] = ao
        x1_ref[...] = x_ref[...] + g_ref[...] * ao

    lat = lambda w: pl.BlockSpec((None, RT, w), lambda b, j: (b, j, 0))
    return pl.pallas_call(
        body, name="outproj_fwd", grid=(NB, SEQ // RT),
        out_shape=[jax.ShapeDtypeStruct((NB, SEQ, D), F32), jax.ShapeDtypeStruct((NB, SEQ, D), BF16),
                   jax.ShapeDtypeStruct((NB, SEQ, D), F32)],
        in_specs=[pl.BlockSpec((None, RT, DNW), lambda b, j: (b, j + 1, 0)),
                  pl.BlockSpec((None, RT, DNW), lambda b, j: (b, j + 1, 0)),
                  pl.BlockSpec((None, RT, DNW), lambda b, j: (b, j + 1, COL_Z // DNW)),
                  lat(NAW), pl.BlockSpec((1, DNW), lambda b, j: (0, 0)), lat(D),
                  pl.BlockSpec((None, 1, D), lambda b, j: (b, 0, 0)), pl.BlockSpec((D, D), lambda b, j: (0, 0))],
        out_specs=[lat(D), lat(D), lat(D)],
        compiler_params=_cp(vmem=VMEM_BIG),
    )(of, ob, p, na, onw, x, g1, w_out)


def _outproj_bwd(dx1, ao, g1, w_out, of, ob, p, onw):
    def body(dx_ref, ao_ref, g_ref, wo_ref, of_ref, ob_ref, z_ref, w_ref, dao_ref, do_ref, dz_ref, dna_ref, dg_ref, dw_ref):
        b, j = pl.program_id(0), pl.program_id(1)

        @pl.when((b == 0) & (j == 0))
        def _():
            dw_ref[...] = jnp.zeros_like(dw_ref)

        @pl.when(j == 0)
        def _():
            do_ref[...] = jnp.zeros_like(do_ref)
            dz_ref[...] = jnp.zeros_like(dz_ref)

        @pl.when(j == 1)
        def _():
            dg_ref[...] = jnp.zeros_like(dg_ref)

        @pl.when(j >= 1)
        def _():
            dx = dx_ref[...]
            dg_ref[...] += jnp.sum(dx * ao_ref[...], axis=0, keepdims=True)
            dao = (g_ref[...] * dx).astype(BF16)
            dao_ref[...] = dao
            dai = lax.dot_general(dao, wo_ref[...], _NT, preferred_element_type=F32)
            dna_ref[...] = dai[:, DNW:]
            _, vjp = jax.vjp(_dn_out, of_ref[...] + ob_ref[...], z_ref[...], w_ref[...])
            do, dz, dw = vjp(dai[:, :DNW])
            do_ref[...] = do
            dz_ref[...] = dz
            dw_ref[...] += jnp.concatenate([dw, jnp.zeros((7, DNW), F32)], axis=0)

    lat = lambda w: pl.BlockSpec((None, RT, w), lambda b, j: (b, jnp.maximum(j - 1, 0), 0))
    tok = lambda w: pl.BlockSpec((None, RT, w), lambda b, j: (b, j, 0))
    return pl.pallas_call(
        body, name="outproj_bwd", grid=(NB, T // RT),
        out_shape=[jax.ShapeDtypeStruct((NB, SEQ, D), BF16), jax.ShapeDtypeStruct((NB, T, DNW), F32),
                   jax.ShapeDtypeStruct((NB, T, DNW), F32), jax.ShapeDtypeStruct((NB, SEQ, NAW), F32),
                   jax.ShapeDtypeStruct((NB, 1, D), F32), jax.ShapeDtypeStruct((8, DNW), F32)],
        in_specs=[_rows2d_spec(D), lat(D), pl.BlockSpec((None, 1, D), lambda b, j: (b, 0, 0)),
                  pl.BlockSpec((D, D), lambda b, j: (0, 0)),
                  tok(DNW), tok(DNW),
                  pl.BlockSpec((None, RT, DNW), lambda b, j: (b, j, COL_Z // DNW)),
                  pl.BlockSpec((1, DNW), lambda b, j: (0, 0))],
        out_specs=[lat(D), tok(DNW), tok(DNW), lat(NAW), pl.BlockSpec((None, 1, D), lambda b, j: (b, 0, 0)),
                   pl.BlockSpec((8, DNW), lambda b, j: (0, 0))],
        compiler_params=_cp(vmem=VMEM_BIG),
    )(dx1, ao, g1, w_out, of, ob, p, onw)


FT_FWD, FT_BWD = 512, 256
NFF = DFF // FFT


def _ffn_specs(ft):
    row = lambda w: pl.BlockSpec((ft, w), lambda i, f: (i, 0))
    ex = pl.BlockSpec((None, 1, D), lambda i, f: (i // (SEQ // ft), 0, 0))
    one = pl.BlockSpec((1, D), lambda i, f: (0, 0))
    wg = pl.BlockSpec((None, D, FFT), lambda i, f: (f, 0, 0))
    wu = pl.BlockSpec((None, D, FFT), lambda i, f: (NFF + f, 0, 0))
    wo = pl.BlockSpec((FFT, D), lambda i, f: (f, 0))
    ff = pl.BlockSpec((ft, FFT), lambda i, f: (i, f))
    return row, ex, one, wg, wu, wo, ff


def _ffn_fwd(x1, sh2, sc2, g2, n2w, w_in4, w_out, target):
    ft, tpe = FT_FWD, SEQ // FT_FWD

    def body(x_ref, sh_ref, sc_ref, g_ref, w2_ref, wg_ref, wu_ref, wo_ref, t_ref,
             h2_ref, gt_ref, up_ref, dy_ref, dg_ref, loss_ref, acc_ref):
        i, f = pl.program_id(0), pl.program_id(1)

        @pl.when(f == 0)
        def _():
            h2_ref[...] = _norm_mod(x_ref[...], w2_ref[...], sh_ref[...], sc_ref[...]).astype(BF16)
            acc_ref[...] = jnp.zeros_like(acc_ref)

        h2 = h2_ref[...]
        gate = jnp.dot(h2, wg_ref[...], preferred_element_type=F32)
        up = jnp.dot(h2, wu_ref[...], preferred_element_type=F32)
        gt_ref[...] = gate.astype(BF16)
        up_ref[...] = up.astype(BF16)
        acc_ref[...] += jnp.dot((_silu(gate) * up).astype(BF16), wo_ref[...], preferred_element_type=F32)

        @pl.when((i == 0) & (f == 0))
        def _():
            loss_ref[...] = jnp.zeros_like(loss_ref)

        @pl.when(f == NFF - 1)
        def _():
            ffn = acc_ref[...]
            err = x_ref[...] + g_ref[...] * ffn - t_ref[...]
            dy = err * (1.0 / D)
            dy_ref[...] = dy
            part = jnp.sum(dy * ffn, axis=0, keepdims=True)
            loss_ref[...] += jnp.sum(jnp.sum(err * err, axis=1, keepdims=True), axis=0, keepdims=True) * (0.5 / D)

            @pl.when(i % tpe == 0)
            def _():
                dg_ref[...] = part

            @pl.when(i % tpe != 0)
            def _():
                dg_ref[...] += part

    row, ex, one, wg, wu, wo, ff = _ffn_specs(ft)
    return pl.pallas_call(
        body, name="ffn_fwd", grid=(NTOK // ft, NFF),
        out_shape=[jax.ShapeDtypeStruct((NTOK, D), BF16), jax.ShapeDtypeStruct((NTOK, DFF), BF16),
                   jax.ShapeDtypeStruct((NTOK, DFF), BF16), jax.ShapeDtypeStruct((NTOK, D), F32),
                   jax.ShapeDtypeStruct((NB, 1, D), F32), jax.ShapeDtypeStruct((8, 128), F32)],
        in_specs=[row(D), ex, ex, ex, one, wg, wu, wo, row(D)],
        out_specs=[row(D), ff, ff, row(D), ex, pl.BlockSpec((8, 128), lambda i, f: (0, 0))],
        scratch_shapes=[pltpu.VMEM((ft, D), F32)],
        compiler_params=_cp(vmem=VMEM_BIG),
    )(x1, sh2, sc2, g2, n2w, w_in4, w_in4, w_out, target)


def _rows2d_spec(width):
    return pl.BlockSpec((RT, width), lambda b, j: (b * (SEQ // RT) + jnp.maximum(j - 1, 0), 0))


def _ffn_bwd(dy, gate, up, x1, sh2, sc2, g2, n2w, w_in4, w_out):
    ft, tpe = FT_BWD, SEQ // FT_BWD
    ni = NTOK // ft

    def body(dy_ref, gt_ref, up_ref, x_ref, sh_ref, sc_ref, g_ref, w2_ref, wg_ref, wu_ref, wo_ref,
             dff_ref, act_ref, dgu_ref, dx_ref, dsh_ref, dsc_ref, dw_ref, acc_ref):
        f, i = pl.program_id(0), pl.program_id(1)
        rows = pl.ds(pl.multiple_of(i * ft, ft), ft)
        dff = (g_ref[...] * dy_ref[...]).astype(BF16)
        dff_ref[...] = dff
        dact = lax.dot_general(dff, wo_ref[...], _NT, preferred_element_type=F32)
        gate, up = gt_ref[...].astype(F32), up_ref[...].astype(F32)
        sg = _sigmoid(gate)
        sl = gate * sg
        act_ref[...] = (sl * up).astype(BF16)
        dgate = (dact * up * (sg * (1.0 + gate * (1.0 - sg)))).astype(BF16)
        dup = (dact * sl).astype(BF16)
        dgu_ref[0] = dgate
        dgu_ref[1] = dup
        part = (lax.dot_general(dgate, wg_ref[...], _NT, preferred_element_type=F32)
                + lax.dot_general(dup, wu_ref[...], _NT, preferred_element_type=F32))

        @pl.when(f == 0)
        def _():
            acc_ref[rows, :] = part

        @pl.when(f > 0)
        def _():
            acc_ref[rows, :] += part

        @pl.when((i == 0) & (f == 0))
        def _():
            dw_ref[...] = jnp.zeros_like(dw_ref)

        @pl.when(f == NFF - 1)
        def _():
            _, vjp = jax.vjp(_norm_mod, x_ref[...], w2_ref[...], sh_ref[...], sc_ref[...])
            dx, dw, dsh, dsc = vjp(acc_ref[rows, :])
            dx_ref[...] = dy_ref[...] + dx
            dw_ref[...] += dw

            @pl.when(i % tpe == 0)
            def _():
                dsh_ref[...] = dsh
                dsc_ref[...] = dsc

            @pl.when(i % tpe != 0)
            def _():
                dsh_ref[...] += dsh
                dsc_ref[...] += dsc

    last = lambda f, real, spare: jnp.where(f == NFF - 1, real, spare)
    row = lambda w: pl.BlockSpec((ft, w), lambda f, i: (i, 0))
    ex = pl.BlockSpec((None, 1, D), lambda f, i: (i // tpe, 0, 0))
    one = pl.BlockSpec((1, D), lambda f, i: (0, 0))
    wg = pl.BlockSpec((None, D, FFT), lambda f, i: (f, 0, 0))
    wu = pl.BlockSpec((None, D, FFT), lambda f, i: (NFF + f, 0, 0))
    wo = pl.BlockSpec((FFT, D), lambda f, i: (f, 0))
    ff = pl.BlockSpec((ft, FFT), lambda f, i: (i, f))
    row_once = lambda w: pl.BlockSpec((ft, w), lambda f, i: (last(f, i, ni), 0))
    ex_once = pl.BlockSpec((None, 1, D), lambda f, i: (last(f, i // tpe, NB), 0, 0))
    return pl.pallas_call(
        body, name="ffn_bwd", grid=(NFF, ni),
        out_shape=[jax.ShapeDtypeStruct((NTOK + ft, D), BF16), jax.ShapeDtypeStruct((NTOK, DFF), BF16),
                   jax.ShapeDtypeStruct((2, NTOK, DFF), BF16), jax.ShapeDtypeStruct((NTOK + ft, D), F32),
                   jax.ShapeDtypeStruct((NB + 1, 1, D), F32), jax.ShapeDtypeStruct((NB + 1, 1, D), F32),
                   jax.ShapeDtypeStruct((1, D), F32)],
        in_specs=[row(D), ff, ff, row(D), ex, ex, ex, one, wg, wu, wo],
        out_specs=[row_once(D), ff, pl.BlockSpec((2, ft, FFT), lambda f, i: (0, i, f)), row_once(D), ex_once, ex_once, one],
        scratch_shapes=[pltpu.VMEM((NTOK, D), F32)],
        compiler_params=_cp(vmem=VMEM_BIG),
    )(dy, gate, up, x1, sh2, sc2, g2, n2w, w_in4, w_in4, w_out)


def _local_step(x, ctx, target, mod_x, mod_c, norm1_w, norm2_w, w_in_p, conv_w, a_log, dt_bias, out_norm_w,
                q_norm_w, k_norm_w, rpb, w_out_b, w_ffn_in4, w_ffn_out_b, scatter_early=False, gather_late=None):
    sh1, sc1, g1, sh2, sc2, g2 = [mod_x[:, i * D:(i + 1) * D].reshape(NB, 1, D) for i in range(6)]
    csh1, csc1 = mod_c[0:D].reshape(1, 1, D), mod_c[D:2 * D].reshape(1, 1, D)
    cw = jnp.concatenate([conv_w, jnp.zeros((3, 1536), F32)], axis=0)
    prm = jnp.pad(jnp.stack([a_log.reshape(8), dt_bias.reshape(8)]), ((0, 6), (0, 120)))
    onw = jnp.tile(out_norm_w.reshape(1, 128), (1, DNH))
    qw = jnp.tile(q_norm_w.reshape(1, 64), (1, NAH))
    kw = jnp.tile(k_norm_w.reshape(1, 64), (1, NAH))
    seg = np.arange(NAW) // 64
    bd = jnp.asarray((seg[:, None] == seg[None, :]).astype(np.float32) / 64.0, dtype=BF16)
    bias = _bias_table(rpb)

    p, hm = _inproj_fwd(x, ctx, sh1, sc1, csh1, csc1, norm1_w, w_in_p)
    dq_, dk_, dv_ = _dn_prep_fwd(p, cw)
    gates = _gates_fwd(p, prm)
    res = _scan_fwd(dq_, dk_, dv_, gates, carry=gather_late[:2] if gather_late else ())
    o_f, st_f, inv_f, o_b, st_b, inv_b = res[:6]
    qn, kn, vb = _na_prep_fwd(p, qw, kw, bd)
    na, got = _na_fwd(qn, kn, vb, bias, carry=gather_late[2:] if gather_late else ())
    if gather_late:
        w_out_b = res[6].reshape(D, D)
        w_ffn_in4 = res[7].reshape(4, D, FFT)
        w_ffn_out_b = got[0].reshape(DFF, D)
    x1, ai, ao = _outproj_fwd(o_f, o_b, p, na, onw, x, g1, w_out_b)
    x1f = x1.reshape(NTOK, D)
    h2, gate, up, dy, dg2, loss8 = _ffn_fwd(x1f, sh2, sc2, g2, norm2_w, w_ffn_in4, w_ffn_out_b,
                                             target.reshape(NTOK, D))
    dff, act, dgu, dx1, dsh2, dsc2, dn2w = _ffn_bwd(dy, gate, up, x1f, sh2, sc2, g2, norm2_w, w_ffn_in4, w_ffn_out_b)
    g_w_ffn_out = _matmul_tn(act, dff, FFT, D, 512, "wgrad_ffn_out")
    g_w_ffn_in4 = _matmul_tn(
        h2, dgu, D, FFT, 1024, "wgrad_ffn_in", nn=2 * NFF,
        b_block=(None, 1024, FFT), b_map=lambda i, j, k: (j // NFF, k, j % NFF),
        out_shape=(4, D, FFT), out_block=(None, D, FFT), out_map=lambda i, j, k: (j, i, 0))
    dsh2, dsc2 = dsh2[:NB], dsc2[:NB]
    dao, do, dz, dna, dg1, donw = _outproj_bwd(dx1, ao, g1, w_out_b, o_f, o_b, p, onw)
    g_w_out = _matmul_tn(ai.reshape(NTOK, D), dao.reshape(NTOK, D), D, D, 1024, "wgrad_out")
    early = [g_w_out.reshape(8, D // 8, D), g_w_ffn_in4.reshape(8, D // 2, FFT), g_w_ffn_out.reshape(8, DFF // 8, D)]
    (dqn, dkn, dvn, dbias), got_na = _na_bwd(qn, kn, vb, bias, dna, carry=[early[0], early[2]] if scatter_early else ())
    dp_na, dqw, dkw = _na_prep_bwd(p, qw, kw, bd, dqn, dkn, dvn)
    res = _scan_bwd(dq_, dk_, dv_, gates, st_f, st_b, inv_f, inv_b, do, carry=[early[1]] if scatter_early else ())
    dqf, dkf, dvf, dgf, dqb, dkb, dvb, dgb = res[:8]
    if scatter_early:
        (g_w_out, g_w_ffn_out), (g_w_ffn_in4,) = got_na, res[8:]
    dpq, dpk, dpv, gcq, gck, gcv = _dn_prep_bwd(p, cw, (dqf, dqb), (dkf, dkb), (dvf, dvb))
    dgt, dprm = _gates_bwd(p, prm, dgf, dgb)
    dpb, grad_x, dsh1, dsc1, dcsh1, dcsc1, dn1w = _inproj_bwd(
        (dpq, dpk, dpv), dz, dp_na, dgt, w_in_p, x, ctx, sh1, sc1, csh1, csc1, norm1_w, dx1)
    g_w_in_p = _matmul_tn(dpb.reshape(NB * T, PC), hm.reshape(NB * T, D), PC, D, 512, "wgrad_in")
    g_rpb = _rpb_fold(dbias)

    dmod_x = jnp.concatenate([dsh1, dsc1, dg1, dsh2, dsc2, dg2], axis=2).reshape(NB, 6 * D)
    dmod_c = jnp.concatenate([dcsh1.reshape(D), dcsc1.reshape(D), jnp.zeros((4 * D,), F32)])
    g_conv = jnp.concatenate([gcq[:5], gck[:5], gcv[:5]], axis=1)
    small = dict(
        norm1_w=dn1w, norm2_w=dn2w, a_log=dprm[0, :8], dt_bias=dprm[1, :8],
        out_norm_w=donw[0].reshape(DNH, 128).sum(0), q_norm_w=dqw[0].reshape(NAH, 64).sum(0),
        k_norm_w=dkw[0].reshape(NAH, 64).sum(0), rpb=g_rpb, conv_w=g_conv)
    return loss8[0, 0], grad_x, dmod_x, dmod_c, small, g_w_in_p, g_w_out, g_w_ffn_in4, g_w_ffn_out


def _perm_w_in(wt):
    return jnp.concatenate([wt[:2048], wt[2064:3600], wt[2048:2064], jnp.zeros((PC - IN_COLS, wt.shape[1]), wt.dtype)],
                           axis=0)


def _unperm_w_in(gt):
    return jnp.concatenate([gt[:2048], gt[COL_G:COL_G + 16], gt[2048:COL_G]], axis=0)


def _misc_row(a_log, dt_bias, onw, qnw, knw):
    return jnp.concatenate([a_log.reshape(8), dt_bias.reshape(8), jnp.zeros((112,), F32), onw.reshape(128),
                            qnw.reshape(64), knw.reshape(64), jnp.zeros((D - 384,), F32)])


def _pack_small(c_ctx, n1, n2, b_ada, a_log, dt_bias, onw, qnw, knw, rpb, conv_s):
    misc = _misc_row(a_log, dt_bias, onw, qnw, knw)
    rp = jnp.concatenate([rpb.reshape(-1), jnp.zeros((4 * D - 3720,), F32)])
    cv = jnp.concatenate([conv_s.reshape(-1), jnp.zeros((2 * D - 1920,), F32)])
    return jnp.concatenate([c_ctx.reshape(-1), n1.reshape(-1), n2.reshape(-1), b_ada.reshape(-1), misc, rp, cv]).reshape(16, D)


def _unpack_small(a):
    f = a.reshape(-1)
    misc = f[9 * D:10 * D]
    return dict(
        c_ctx=f[0:D], norm1_w=f[D:2 * D].reshape(1, D), norm2_w=f[2 * D:3 * D].reshape(1, D),
        b_ada=f[3 * D:9 * D].reshape(1, 6 * D), dn_A_log=misc[0:8].reshape(1, 2, 4), dn_dt_bias=misc[8:16].reshape(1, 2, 4),
        dn_out_norm_w=misc[128:256].reshape(1, 128), na_q_norm_w=misc[256:320].reshape(1, 64),
        na_k_norm_w=misc[320:384].reshape(1, 64), na_rpb=f[10 * D:10 * D + 3720].reshape(1, 8, 15, 31),
        dn_conv_w=f[14 * D:14 * D + 1920].reshape(1, 5, 384))


_WEIGHTS = ['c_ctx', 'norm1_w', 'norm2_w', 'w_ada', 'b_ada', 'w_in', 'dn_conv_w', 'dn_A_log', 'dn_dt_bias',
            'dn_out_norm_w', 'na_q_norm_w', 'na_k_norm_w', 'na_rpb', 'w_out', 'w_ffn_in', 'w_ffn_out']


def kernel(x, c, ctx, c_ctx, norm1_w, norm2_w, w_ada, b_ada, w_in, dn_conv_w, dn_A_log, dn_dt_bias, dn_out_norm_w, na_q_norm_w, na_k_norm_w, na_rpb, w_out, w_ffn_in, w_ffn_out, loss_target, m_c_ctx, m_norm1_w, m_norm2_w, m_w_ada, m_b_ada, m_w_in, m_dn_conv_w, m_dn_A_log, m_dn_dt_bias, m_dn_out_norm_w, m_na_q_norm_w, m_na_k_norm_w, m_na_rpb, m_w_out, m_w_ffn_in, m_w_ffn_out, v_c_ctx, v_norm1_w, v_norm2_w, v_w_ada, v_b_ada, v_w_in, v_dn_conv_w, v_dn_A_log, v_dn_dt_bias, v_dn_out_norm_w, v_na_q_norm_w, v_na_k_norm_w, v_na_rpb, v_w_out, v_w_ffn_in, v_w_ffn_out):
    ix, iy, ic = lax.axis_index("x"), lax.axis_index("y"), lax.axis_index("c")
    chip = 2 * ix + iy
    me = 4 * ix + 2 * iy + ic

    crow = jnp.concatenate([c, c_ctx.reshape(1, D), jnp.zeros((5, D), F32)], axis=0)
    cvrow = jnp.concatenate([dn_conv_w[0].reshape(-1), jnp.zeros((2 * D - 1920,), F32)]).reshape(2, D)
    g0 = _allgather8(jnp.concatenate([crow, cvrow, jnp.zeros((6, D), F32)], axis=0), "gather_cond")
    cin = g0[:, :8].reshape(64, D)
    conv_full = jnp.concatenate([g0[2 * s, 8:10].reshape(-1)[:1920].reshape(5, 384) for s in range(4)], axis=1)

    w_ada_s = w_ada[0]
    b_s = lax.dynamic_slice(b_ada, (0, chip * 1536), (1, 1536))
    mod_s = _ada_fwd(cin, w_ada_s, b_s)
    mods = _gather_rows(mod_s, "gather_mod")
    mod_mine = jnp.concatenate([mods[0], mods[2], mods[4], mods[6]], axis=1)
    mod_x, mod_c = mod_mine[0:NB], mod_mine[2]

    def half(w2d):
        r = w2d.shape[0] // 2
        return lax.dynamic_slice(w2d, (ic * r, 0), (r, w2d.shape[1]))

    w_in_t = w_in[0].T
    (gw_in,) = _gather_weights([_cast_bf16(lax.dynamic_slice(w_in_t, (0, ic * (D // 2)), (900, D // 2)), "cast_w_in")])
    late = [_cast_bf16(half(w_out[0]), "cast_w_out"), _cast_bf16(half(w_ffn_in[0]), "cast_w_ffn_in"),
            _cast_bf16(half(w_ffn_out[0]), "cast_w_ffn_out")]
    w_in_p = _perm_w_in(gw_in.reshape(4, 2, 900, D // 2).transpose(0, 2, 1, 3).reshape(IN_COLS, D))

    (loss_part, grad_x, dmod_x, dmod_c, small, g_w_in_p, r_w_out, r_w_ffn_in, r_w_ffn_out) = _local_step(
        x, ctx, loss_target, mod_x, mod_c, norm1_w, norm2_w, w_in_p, conv_full, dn_A_log[0], dn_dt_bias[0],
        dn_out_norm_w[0], na_q_norm_w[0], na_k_norm_w[0], na_rpb[0], None, None, None,
        scatter_early=True, gather_late=late)
    loss = lax.psum(loss_part, ("x", "y", "c"))

    dm = jnp.concatenate([dmod_x, dmod_c.reshape(1, 6 * D), jnp.zeros((5, 6 * D), F32)], axis=0).reshape(48, D)
    misc = _misc_row(small["a_log"], small["dt_bias"], small["out_norm_w"], small["q_norm_w"], small["k_norm_w"])
    sm = jnp.concatenate([
        small["norm1_w"].reshape(-1), small["norm2_w"].reshape(-1), misc,
        small["rpb"].reshape(-1), jnp.zeros((4 * D - 3720,), F32),
        small["conv_w"].reshape(-1), jnp.zeros((8 * D - 7680,), F32), jnp.zeros((D,), F32)]).reshape(16, D)
    gsl, gtot = _allgather8(jnp.concatenate([dm, sm], axis=0), "gather_small", with_sum=True)
    gmod = gsl[:, :48].reshape(64, 6 * D)
    st = gtot[48:64].reshape(-1)
    g_n1, g_n2, g_misc = st[0:D].reshape(1, D), st[D:2 * D].reshape(1, D), st[2 * D:3 * D]
    g_rpb = st[3 * D:3 * D + 3720].reshape(1, 8, 15, 31)
    g_conv_full = st[7 * D:7 * D + 7680].reshape(5, 1536)
    g_conv_s = lax.dynamic_slice(g_conv_full, (0, chip * 384), (5, 384)).reshape(1, 5, 384)

    gmod_s = lax.dynamic_slice(gmod, (0, chip * 1536), (64, 1536))
    g_w_ada_s, pctx = _ada_bwd(cin, gmod_s, w_ada_s)
    parts = _allgather8(pctx, "gather_cctx")
    g_b_ada, g_c_ctx = _ada_small(gmod, parts, c_ctx.reshape(1, D))

    g_w_in8 = _unperm_w_in(g_w_in_p).reshape(4, 900, 2, D // 2).transpose(0, 2, 1, 3).reshape(8, 900, D // 2)
    names = ["w_in", "w_out", "w_ffn_in", "w_ffn_out"]
    early = _sibling_exchange([_sum_slots(r, "sum_" + nm)
                               for r, nm in zip([r_w_out, r_w_ffn_in, r_w_ffn_out], names[1:])], "sibling_early")
    g_w_out_s = early[0].reshape(1, D // 4, D)
    g_w_ffn_in_s = early[1].reshape(1, D, FFT)
    g_w_ffn_out_s = early[2].reshape(1, DFF // 4, D)
    two = lambda a: a.reshape(a.shape[1], a.shape[2])
    hosted_names = ["w_ada", "w_out", "w_ffn_in"]
    hosted_g = dict(w_ada=g_w_ada_s, w_out=two(g_w_out_s), w_ffn_in=two(g_w_ffn_in_s))
    hosted_w = dict(w_ada=(w_ada, m_w_ada, v_w_ada), w_out=(w_out, m_w_out, v_w_out), w_ffn_in=(w_ffn_in, m_w_ffn_in, v_w_ffn_in))
    hosted_res, (r_w_in,) = _adamw_hosting(
        [(two(hosted_w[k][0]), hosted_g[k], two(hosted_w[k][1]), two(hosted_w[k][2])) for k in hosted_names], [g_w_in8])
    (g_w_in_half,) = _sibling_exchange([_sum_slots(r_w_in, "sum_w_in")], "sibling_w_in")
    g_w_in_t = g_w_in_half.transpose(1, 0, 2).reshape(900, D)

    grads = dict(
        c_ctx=g_c_ctx.reshape(D), norm1_w=g_n1, norm2_w=g_n2, w_ada=g_w_ada_s.reshape(1, D, 1536), b_ada=g_b_ada,
        w_in=g_w_in_t, dn_conv_w=g_conv_s, dn_A_log=g_misc[0:8].reshape(1, 2, 4), dn_dt_bias=g_misc[8:16].reshape(1, 2, 4),
        dn_out_norm_w=g_misc[128:256].reshape(1, 128), na_q_norm_w=g_misc[256:320].reshape(1, 64),
        na_k_norm_w=g_misc[320:384].reshape(1, 64), na_rpb=g_rpb, w_out=g_w_out_s, w_ffn_in=g_w_ffn_in_s,
        w_ffn_out=g_w_ffn_out_s)
    pk = lambda d_: _pack_small(d_["c_ctx"], d_["norm1_w"], d_["norm2_w"], d_["b_ada"], d_["dn_A_log"], d_["dn_dt_bias"],
                                d_["dn_out_norm_w"], d_["na_q_norm_w"], d_["na_k_norm_w"], d_["na_rpb"], d_["dn_conv_w"])
    wts = dict(c_ctx=c_ctx, norm1_w=norm1_w, norm2_w=norm2_w, b_ada=b_ada, dn_A_log=dn_A_log, dn_dt_bias=dn_dt_bias,
               dn_out_norm_w=dn_out_norm_w, na_q_norm_w=na_q_norm_w, na_k_norm_w=na_k_norm_w, na_rpb=na_rpb, dn_conv_w=dn_conv_w)
    ms = dict(c_ctx=m_c_ctx, norm1_w=m_norm1_w, norm2_w=m_norm2_w, b_ada=m_b_ada, dn_A_log=m_dn_A_log, dn_dt_bias=m_dn_dt_bias,
              dn_out_norm_w=m_dn_out_norm_w, na_q_norm_w=m_na_q_norm_w, na_k_norm_w=m_na_k_norm_w, na_rpb=m_na_rpb,
              dn_conv_w=m_dn_conv_w)
    vs = dict(c_ctx=v_c_ctx, norm1_w=v_norm1_w, norm2_w=v_norm2_w, b_ada=v_b_ada, dn_A_log=v_dn_A_log, dn_dt_bias=v_dn_dt_bias,
              dn_out_norm_w=v_dn_out_norm_w, na_q_norm_w=v_na_q_norm_w, na_k_norm_w=v_na_k_norm_w, na_rpb=v_na_rpb,
              dn_conv_w=v_dn_conv_w)
    sd, snm, snv, _ = _adamw(pk(wts), pk(grads), pk(ms), pk(vs), "adamw_small")
    delta, new_m, new_v = _unpack_small(sd), _unpack_small(snm), _unpack_small(snv)
    for k, res in zip(hosted_names, hosted_res):
        shp = hosted_w[k][0].shape
        delta[k], new_m[k], new_v[k], grads[k] = [a.reshape(shp) for a in res]
    big = dict(w_in=(w_in, m_w_in, v_w_in), w_ffn_out=(w_ffn_out, m_w_ffn_out, v_w_ffn_out))
    for nm, (w_, m_, v_) in big.items():
        shp = w_.shape
        if nm == "w_in":
            r2, back = (lambda a: a[0].T), (lambda a: a.T.reshape(shp))
            res = _adamw(w_in_t, grads[nm], r2(m_), r2(v_), "adamw_" + nm)
        else:
            r2, back = (lambda a: a.reshape(shp[1], shp[2])), (lambda a: a.reshape(shp))
            res = _adamw(r2(w_), r2(grads[nm]), r2(m_), r2(v_), "adamw_" + nm)
        delta[nm], new_m[nm], new_v[nm], grads[nm] = [back(a) for a in res]

    return (loss, grad_x, *[grads[n] for n in _WEIGHTS], *[delta[n] for n in _WEIGHTS],
            *[new_m[n] for n in _WEIGHTS], *[new_v[n] for n in _WEIGHTS])
```

```python
import functools

import numpy as np
import jax
import jax.numpy as jnp
from jax import lax
from jax.experimental import pallas as pl
from jax.experimental.pallas import tpu as pltpu

F32 = jnp.float32
BF16 = jnp.bfloat16

D = 1024
NB = 2
SEQ = 2048
CT = 256
T = CT + SEQ
NTOK = NB * SEQ
CH = 64
NCH = T // CH
NCH_CTX = CT // CH
DNH, DNW = 4, 512
NAH, NAW = 8, 512
GRID_W = 64
ROWS = SEQ // GRID_W
WIN_ROWS = 8
WIN_COLS = 16
NWIN = WIN_ROWS * GRID_W
DFF = 2816
FFT = 1408
IN_COLS = 3600
PC = 3712
COL_Z, COL_NA, COL_G = 1536, 2048, 3584
EPS = 1e-6
NEG = -1e30
RT = 256
VMEM_BIG = 56 * 1024 * 1024
MESH = pl.DeviceIdType.MESH

ADAM_LR, ADAM_B1, ADAM_B2, ADAM_EPS, ADAM_WD, ADAM_STEP = 0.001, 0.9, 0.999, 1e-08, 0.01, 10


def _cp(vmem=None, sem=None):
    kw = {}
    if vmem is not None:
        kw["vmem_limit_bytes"] = vmem
    if sem is not None:
        kw["dimension_semantics"] = sem
    return pltpu.CompilerParams(**kw)


_NN = (((1,), (0,)), ((), ()))
_NT = (((1,), (1,)), ((), ()))
_TN = (((0,), (0,)), ((), ()))


def _dg(a, b, dims):
    return lax.dot_general(a.astype(BF16), b.astype(BF16), dims, preferred_element_type=F32)


@jax.custom_vjp
def _mm(a, b):
    return _dg(a, b, _NN)


_mm.defvjp(lambda a, b: (_dg(a, b, _NN), (a, b)),
           lambda r, g: (_dg(g, r[1], _NT), _dg(r[0], g, _TN)))


def _split_dot(a, b, dims):
    hi = b.astype(BF16)
    lo = (b - hi.astype(F32)).astype(BF16)
    ab = a.astype(BF16)
    return (lax.dot_general(ab, hi, dims, preferred_element_type=F32)
            + lax.dot_general(ab, lo, dims, preferred_element_type=F32))


@jax.custom_vjp
def _mask_mm(mask, b):
    return _split_dot(mask, b, _NN)


_mask_mm.defvjp(lambda mask, b: (_split_dot(mask, b, _NN), mask),
                lambda mask, g: (jnp.zeros_like(mask), _split_dot(mask, g, _TN)))


def _hmm(a, b):
    return jnp.dot(a, b, precision=lax.Precision.HIGHEST, preferred_element_type=F32)


def _sigmoid(x):
    return 1.0 / (1.0 + jnp.exp(-x))


def _silu(x):
    return x * _sigmoid(x)


def _norm_mod(x, w, sh, sc):
    r = lax.rsqrt(jnp.mean(x * x, axis=-1, keepdims=True) + EPS)
    return (x * r * w) * (1.0 + sc) + sh


def _my_pos():
    return lax.axis_index("x"), lax.axis_index("y"), lax.axis_index("c")


def _flip(pos, k):
    x, y, c = pos
    return (1 - x if k & 4 else x, 1 - y if k & 2 else y, 1 - c if k & 1 else c)


def _slot(pos):
    return 4 * pos[0] + 2 * pos[1] + pos[2]


def _allgather8(v, name, with_sum=False):
    R, W = v.shape

    def body(v_ref, out_ref, *rest):
        if with_sum:
            tot_ref, send_sems, recv_sems, lsem = rest
        else:
            send_sems, recv_sems, lsem = rest
        me = _my_pos()
        mine = pltpu.make_async_copy(v_ref, out_ref.at[_slot(me)], lsem)
        mine.start()
        sends = []
        for k in range(1, 8):
            cp = pltpu.make_async_remote_copy(
                src_ref=v_ref, dst_ref=out_ref.at[_slot(me)], send_sem=send_sems.at[k - 1],
                recv_sem=recv_sems.at[k - 1], device_id=_flip(me, k), device_id_type=MESH)
            cp.start()
            sends.append(cp)
        for k in range(1, 8):
            peer = _flip(me, k)
            pltpu.make_async_remote_copy(
                src_ref=v_ref, dst_ref=out_ref.at[_slot(peer)], send_sem=send_sems.at[k - 1],
                recv_sem=recv_sems.at[k - 1], device_id=peer, device_id_type=MESH).wait_recv()
        for cp in sends:
            cp.wait_send()
        mine.wait()
        if with_sum:
            acc = out_ref[0]
            for s in range(1, 8):
                acc = acc + out_ref[s]
            tot_ref[...] = acc

    out_shape = [jax.ShapeDtypeStruct((8, R, W), F32)]
    out_specs = [pl.BlockSpec(memory_space=pltpu.VMEM)]
    if with_sum:
        out_shape.append(jax.ShapeDtypeStruct((R, W), F32))
        out_specs.append(pl.BlockSpec(memory_space=pltpu.VMEM))
    res = pl.pallas_call(
        body, name=name, out_shape=out_shape,
        in_specs=[pl.BlockSpec(memory_space=pltpu.VMEM)], out_specs=out_specs,
        scratch_shapes=[pltpu.SemaphoreType.DMA((7,)), pltpu.SemaphoreType.DMA((7,)), pltpu.SemaphoreType.DMA],
        compiler_params=_cp(vmem=VMEM_BIG),
    )(v)
    return res if with_sum else res[0]


def _gather_rows(v, name):
    W = v.shape[1]

    def body(v_ref, out_ref, send_sems, recv_sems, lsem):
        pos = _my_pos()
        me = _slot(pos)
        rows = lambda p: v_ref.at[pl.ds(pl.multiple_of(8 * p, 8), 8)]
        locals_ = [(rows(me), out_ref.at[me])]
        xfers = [(rows(_slot(_flip(pos, k))), out_ref.at[me], out_ref.at[_slot(_flip(pos, k))], _flip(pos, k))
                 for k in range(1, 8)]
        _xfer_start((locals_, xfers), (send_sems, recv_sems, lsem))
        _xfer_wait((locals_, xfers), (send_sems, recv_sems, lsem))

    vm = pl.BlockSpec(memory_space=pltpu.VMEM)
    return pl.pallas_call(
        body, name=name, out_shape=jax.ShapeDtypeStruct((8, 8, W), F32), in_specs=[vm], out_specs=vm,
        scratch_shapes=_xfer_sems(1, 7),
    )(v)


def _gather_weights(halves, small):
    n = len(halves)

    def body(*refs):
        ins, s_in, outs, s_out = refs[:n], refs[n], refs[n + 1:2 * n + 1], refs[2 * n + 1]
        send_sems, recv_sems, lsems = refs[2 * n + 2:2 * n + 5]
        s_sems = refs[2 * n + 5:]
        s_plan = _gather_plan([s_in], [s_out], _my_pos())
        _xfer_start(s_plan, s_sems)
        me = _my_pos()
        x, y, c = me
        sib = (x, y, 1 - c)
        chips = [4, 2, 6]

        def cp(a, k, block, to, src=None):
            dst = outs[a].at[_slot(block)]
            return pltpu.make_async_remote_copy(
                src_ref=dst if src is None else src, dst_ref=dst, send_sem=send_sems.at[a, k],
                recv_sem=recv_sems.at[a, k], device_id=to, device_id_type=MESH)

        mine, first, passed = [], [], []
        for a in range(n):
            m = pltpu.make_async_copy(ins[a], outs[a].at[_slot(me)], lsems.at[a])
            m.start()
            mine.append(m)
            first.append(cp(a, 0, me, sib, src=ins[a]))
            for j, k in enumerate(chips):
                first.append(cp(a, 1 + j, me, _flip(me, k), src=ins[a]))
        for f in first:
            f.start()
        for j, k in enumerate(chips):
            for a in range(n):
                cp(a, 1 + j, _flip(me, k), me).wait_recv()
                p = cp(a, 4 + j, _flip(me, k), sib)
                p.start()
                passed.append(p)
        for a in range(n):
            cp(a, 0, sib, me).wait_recv()
            for j, k in enumerate(chips):
                cp(a, 4 + j, _flip(sib, k), me).wait_recv()
        for f in first + passed:
            f.wait_send()
        for m in mine:
            m.wait()
        _xfer_wait(s_plan, s_sems)

    vm = pl.BlockSpec(memory_space=pltpu.VMEM)
    return pl.pallas_call(
        body, name="gather_weights",
        out_shape=[jax.ShapeDtypeStruct((8,) + h.shape, h.dtype) for h in list(halves) + [small]],
        in_specs=[vm] * (n + 1), out_specs=[vm] * (n + 1),
        scratch_shapes=[pltpu.SemaphoreType.DMA((n, 7)), pltpu.SemaphoreType.DMA((n, 7)),
                        pltpu.SemaphoreType.DMA((n,))] + _xfer_sems(1, 7),
        compiler_params=_cp(vmem=VMEM_BIG),
    )(*halves, small)


def _xfer_start(plan, sems):
    send_sems, recv_sems, lsems = sems
    locals_, xfers = plan
    for i, (src, dst) in enumerate(locals_):
        pltpu.make_async_copy(src, dst, lsems.at[i]).start()
    for i, (src, dst, _, peer) in enumerate(xfers):
        pltpu.make_async_remote_copy(src_ref=src, dst_ref=dst, send_sem=send_sems.at[i], recv_sem=recv_sems.at[i],
                                     device_id=peer, device_id_type=MESH).start()


def _xfer_wait(plan, sems):
    send_sems, recv_sems, lsems = sems
    locals_, xfers = plan
    for i, (src, _, landing, peer) in enumerate(xfers):
        pltpu.make_async_remote_copy(src_ref=src, dst_ref=landing, send_sem=send_sems.at[i], recv_sem=recv_sems.at[i],
                                     device_id=peer, device_id_type=MESH).wait_recv()
    for i, (src, dst, _, peer) in enumerate(xfers):
        pltpu.make_async_remote_copy(src_ref=src, dst_ref=dst, send_sem=send_sems.at[i], recv_sem=recv_sems.at[i],
                                     device_id=peer, device_id_type=MESH).wait_send()
    for i, (src, dst) in enumerate(locals_):
        pltpu.make_async_copy(src, dst, lsems.at[i]).wait()


def _xfer_sems(n_local, n_xfer):
    return [pltpu.SemaphoreType.DMA((n_xfer,)), pltpu.SemaphoreType.DMA((n_xfer,)), pltpu.SemaphoreType.DMA((n_local,))]


def _scatter_plan(ins, outs, pos):
    me = _slot(pos)
    locals_, xfers = [], []
    for src, dst in zip(ins, outs):
        locals_.append((src.at[me], dst.at[me]))
        for k in range(1, 8):
            peer = _flip(pos, k)
            xfers.append((src.at[_slot(peer)], dst.at[me], dst.at[_slot(peer)], peer))
    return locals_, xfers


def _sibling_exchange(halves, name):
    n = len(halves)

    def body(*refs):
        ins, outs = refs[:n], refs[n:2 * n]
        send_sems, recv_sems, lsems = refs[2 * n:]
        x, y, c = _my_pos()
        sib = (x, y, 1 - c)
        mine, sends = [], []
        for a in range(n):
            m = pltpu.make_async_copy(ins[a], outs[a].at[c], lsems.at[a])
            m.start()
            mine.append(m)
            cp = pltpu.make_async_remote_copy(
                src_ref=ins[a], dst_ref=outs[a].at[c], send_sem=send_sems.at[a], recv_sem=recv_sems.at[a],
                device_id=sib, device_id_type=MESH)
            cp.start()
            sends.append(cp)
        for a in range(n):
            pltpu.make_async_remote_copy(
                src_ref=ins[a], dst_ref=outs[a].at[1 - c], send_sem=send_sems.at[a], recv_sem=recv_sems.at[a],
                device_id=sib, device_id_type=MESH).wait_recv()
        for cp in sends:
            cp.wait_send()
        for m in mine:
            m.wait()

    vm = pl.BlockSpec(memory_space=pltpu.VMEM)
    return pl.pallas_call(
        body, name=name,
        out_shape=[jax.ShapeDtypeStruct((2,) + h.shape, h.dtype) for h in halves],
        in_specs=[vm] * n, out_specs=[vm] * n,
        scratch_shapes=[pltpu.SemaphoreType.DMA((n,)), pltpu.SemaphoreType.DMA((n,)), pltpu.SemaphoreType.DMA((n,))],
        compiler_params=_cp(vmem=VMEM_BIG),
    )(*halves)


def _row_tile(r, cap=512):
    for t in (cap, 256, 128, 64, 32, 16, 8):
        if t <= cap and r % t == 0:
            return t
    return r


def _cast_bf16(a, name):
    R, W = a.shape
    tr = _row_tile(R, 256)

    def body(a_ref, o_ref):
        o_ref[...] = a_ref[...].astype(BF16)

    return pl.pallas_call(
        body, name=name, grid=(R // tr,), out_shape=jax.ShapeDtypeStruct((R, W), BF16),
        in_specs=[pl.BlockSpec((tr, W), lambda i: (i, 0))], out_specs=pl.BlockSpec((tr, W), lambda i: (i, 0)),
    )(a)


def _sum_slots(r, name):
    _, R, W = r.shape
    tr = _row_tile(R, 128)

    def body(r_ref, o_ref):
        acc = r_ref[0].astype(F32)
        for s_ in range(1, 8):
            acc = acc + r_ref[s_].astype(F32)
        o_ref[...] = acc.astype(BF16)

    return pl.pallas_call(
        body, name=name, grid=(R // tr,), out_shape=jax.ShapeDtypeStruct((R, W), BF16),
        in_specs=[pl.BlockSpec((8, tr, W), lambda i: (0, i, 0))], out_specs=pl.BlockSpec((tr, W), lambda i: (i, 0)),
        compiler_params=_cp(vmem=VMEM_BIG),
    )(r)


def _adamw(w, g, m, v, name):
    R, W = w.shape
    tr = _row_tile(R, 256)
    tc = 256 if (tr == R and R * W * 4 > (1 << 21) and W % 256 == 0) else W
    c1 = 1.0 / (1.0 - ADAM_B1 ** ADAM_STEP)
    c2 = 1.0 / (1.0 - ADAM_B2 ** ADAM_STEP)

    def body(w_ref, g_ref, m_ref, v_ref, d_ref, nm_ref, nv_ref, g32_ref):
        gg = g_ref[...].astype(F32)
        g32_ref[...] = gg
        nm = ADAM_B1 * m_ref[...] + (1.0 - ADAM_B1) * gg
        nv = ADAM_B2 * v_ref[...] + (1.0 - ADAM_B2) * (gg * gg)
        d_ref[...] = -ADAM_LR * ((nm * c1) / (jnp.sqrt(nv * c2) + ADAM_EPS) + ADAM_WD * w_ref[...])
        nm_ref[...] = nm
        nv_ref[...] = nv

    spec = pl.BlockSpec((tr, tc), lambda i, j: (i, j))
    return pl.pallas_call(
        body, name=name, grid=(R // tr, W // tc), out_shape=[jax.ShapeDtypeStruct((R, W), F32)] * 4,
        in_specs=[spec] * 4, out_specs=[spec] * 4, compiler_params=_cp(vmem=VMEM_BIG),
    )(w, g, m, v)


def _adamw_hosting(items, carry):
    n, nc, steps = len(items), len(carry), 8
    c1 = 1.0 / (1.0 - ADAM_B1 ** ADAM_STEP)
    c2 = 1.0 / (1.0 - ADAM_B2 ** ADAM_STEP)

    def body(*refs):
        ins, c_in = refs[:4 * n], refs[4 * n:4 * n + nc]
        outs, c_out = refs[4 * n + nc:8 * n + nc], refs[8 * n + nc:8 * n + 2 * nc]
        sems = refs[8 * n + 2 * nc:]
        if nc:
            @pl.when(pl.program_id(0) == 0)
            def _():
                _xfer_start(_scatter_plan(c_in, c_out, _my_pos()), sems)

        for a in range(n):
            w_ref, g_ref, m_ref, v_ref = ins[4 * a:4 * a + 4]
            d_ref, nm_ref, nv_ref, g32_ref = outs[4 * a:4 * a + 4]
            gg = g_ref[...].astype(F32)
            g32_ref[...] = gg
            nm = ADAM_B1 * m_ref[...] + (1.0 - ADAM_B1) * gg
            nv = ADAM_B2 * v_ref[...] + (1.0 - ADAM_B2) * (gg * gg)
            d_ref[...] = -ADAM_LR * ((nm * c1) / (jnp.sqrt(nv * c2) + ADAM_EPS) + ADAM_WD * w_ref[...])
            nm_ref[...] = nm
            nv_ref[...] = nv

        if nc:
            @pl.when(pl.program_id(0) == steps - 1)
            def _():
                _xfer_wait(_scatter_plan(c_in, c_out, _my_pos()), sems)

    specs = [pl.BlockSpec((w.shape[0] // steps, w.shape[1]), lambda i: (i, 0)) for w, _, _, _ in items]
    hbm = pl.BlockSpec(memory_space=pltpu.HBM)
    res = pl.pallas_call(
        body, name="adamw_hosting", grid=(steps,),
        out_shape=[jax.ShapeDtypeStruct(w.shape, F32) for w, _, _, _ in items for _ in range(4)]
        + [jax.ShapeDtypeStruct(g.shape, g.dtype) for g in carry],
        in_specs=[sp for sp in specs for _ in range(4)] + [hbm] * nc,
        out_specs=[sp for sp in specs for _ in range(4)] + [hbm] * nc,
        scratch_shapes=_xfer_sems(nc, 7 * nc) if nc else [],
        compiler_params=_cp(vmem=VMEM_BIG),
    )(*[a for it in items for a in it], *carry)
    return [res[4 * a:4 * a + 4] for a in range(n)], list(res[4 * n:])


def _matmul_tn(a, b, tm, tn, tk, name, nn=1, b_map=None, out_shape=None, out_block=None, out_map=None,
               b_block=None):
    K, M = a.shape
    nk = K // tk
    nm = M // tm

    def body(a_ref, b_ref, o_ref, acc_ref):
        kk = pl.program_id(2)

        @pl.when(kk == 0)
        def _():
            acc_ref[...] = jnp.zeros_like(acc_ref)

        acc_ref[...] += lax.dot_general(a_ref[...], b_ref[...], _TN, preferred_element_type=F32)

        @pl.when(kk == nk - 1)
        def _():
            o_ref[...] = acc_ref[...].astype(BF16)

    return pl.pallas_call(
        body, name=name, grid=(nm, nn, nk),
        out_shape=jax.ShapeDtypeStruct(out_shape or (M, b.shape[-1]), BF16),
        in_specs=[pl.BlockSpec((tk, tm), lambda i, j, k: (k, i)),
                  pl.BlockSpec(b_block or (tk, tn), b_map or (lambda i, j, k: (k, j)))],
        out_specs=pl.BlockSpec(out_block or (tm, tn), out_map or (lambda i, j, k: (i, j))),
        scratch_shapes=[pltpu.VMEM((tm, tn), F32)],
        compiler_params=_cp(vmem=VMEM_BIG),
    )(a, b)


def _ada_fwd(cin, w_ada_s, b_s):
    tn = 512

    def body(c_ref, w_ref, b_ref, o_ref):
        o_ref[...] = _dg(_silu(c_ref[...]), w_ref[...], _NN) + b_ref[...]

    return pl.pallas_call(
        body, name="ada_fwd", grid=(1536 // tn,), out_shape=jax.ShapeDtypeStruct((64, 1536), F32),
        in_specs=[pl.BlockSpec((64, D), lambda j: (0, 0)), pl.BlockSpec((D, tn), lambda j: (0, j)),
                  pl.BlockSpec((1, tn), lambda j: (0, j))],
        out_specs=pl.BlockSpec((64, tn), lambda j: (0, j)),
        compiler_params=_cp(vmem=VMEM_BIG),
    )(cin, w_ada_s, b_s)


def _ada_bwd(cin, gmod_s, w_ada_s):
    tn = 512

    def body(c_ref, g_ref, w_ref, gw_ref, pc_ref):
        j = pl.program_id(0)
        g = g_ref[...]
        gw_ref[...] = _dg(_silu(c_ref[...]), g, _TN)
        rr = lax.broadcasted_iota(jnp.int32, (8, 64), 0)
        cc = lax.broadcasted_iota(jnp.int32, (8, 64), 1)
        sel = jnp.where((rr == 0) & ((cc & 7) == 2), 1.0, 0.0).astype(F32)
        dctx = _hmm(sel, g)
        part = _dg(dctx, w_ref[...], _NT)

        @pl.when(j == 0)
        def _():
            pc_ref[...] = jnp.zeros_like(pc_ref)

        pc_ref[...] += part

    return pl.pallas_call(
        body, name="ada_bwd", grid=(1536 // tn,),
        out_shape=[jax.ShapeDtypeStruct((D, 1536), F32), jax.ShapeDtypeStruct((8, D), F32)],
        in_specs=[pl.BlockSpec((64, D), lambda j: (0, 0)), pl.BlockSpec((64, tn), lambda j: (0, j)),
                  pl.BlockSpec((D, tn), lambda j: (0, j))],
        out_specs=[pl.BlockSpec((D, tn), lambda j: (0, j)), pl.BlockSpec((8, D), lambda j: (0, 0))],
        compiler_params=_cp(vmem=VMEM_BIG),
    )(cin, gmod_s, w_ada_s)


def _ada_small(gmod, parts, c_ctx):
    def body(g_ref, p_ref, c_ref, gb_ref, gc_ref):
        gb_ref[...] = jnp.sum(g_ref[...], axis=0, keepdims=True)
        ds = p_ref[0][0:1] + p_ref[2][0:1] + p_ref[4][0:1] + p_ref[6][0:1]
        cc = c_ref[...]
        s = _sigmoid(cc)
        gc_ref[...] = ds * (s * (1.0 + cc * (1.0 - s)))

    vm = pl.BlockSpec(memory_space=pltpu.VMEM)
    return pl.pallas_call(
        body, name="ada_small",
        out_shape=[jax.ShapeDtypeStruct((1, 6 * D), F32), jax.ShapeDtypeStruct((1, D), F32)],
        in_specs=[vm, vm, vm], out_specs=[vm, vm],
    )(gmod, parts, c_ctx)


def _tok_specs():
    xs = pl.BlockSpec((None, RT, D), lambda b, j: (b, jnp.maximum(j - 1, 0), 0))
    cs = pl.BlockSpec((None, RT, D), lambda b, j: (b, 0, 0))
    ms = pl.BlockSpec((None, 1, D), lambda b, j: (b, 0, 0))
    cms = pl.BlockSpec((None, 1, D), lambda b, j: (0, 0, 0))
    return xs, cs, ms, cms


def _inproj_fwd(x, ctx, sh1, sc1, csh1, csc1, n1w, w_in_p):
    def body(x_ref, c_ref, sh_ref, sc_ref, csh_ref, csc_ref, w1_ref, w_ref, p_ref, hm_ref):
        is_ctx = pl.program_id(1) == 0
        xt = jnp.where(is_ctx, c_ref[...], x_ref[...])
        sh = jnp.where(is_ctx, csh_ref[...], sh_ref[...])
        sc = jnp.where(is_ctx, csc_ref[...], sc_ref[...])
        hb = _norm_mod(xt, w1_ref[...], sh, sc).astype(BF16)
        hm_ref[...] = hb
        p_ref[...] = lax.dot_general(hb, w_ref[...], _NT, preferred_element_type=F32)

    xs, cs, ms, cms = _tok_specs()
    return pl.pallas_call(
        body, name="inproj_fwd", grid=(NB, T // RT),
        out_shape=[jax.ShapeDtypeStruct((NB, T, PC), F32), jax.ShapeDtypeStruct((NB, T, D), BF16)],
        in_specs=[xs, cs, ms, ms, cms, cms, pl.BlockSpec((1, D), lambda b, j: (0, 0)),
                  pl.BlockSpec((PC, D), lambda b, j: (0, 0))],
        out_specs=[pl.BlockSpec((None, RT, PC), lambda b, j: (b, j, 0)),
                   pl.BlockSpec((None, RT, D), lambda b, j: (b, j, 0))],
        compiler_params=_cp(vmem=VMEM_BIG),
    )(x, ctx, sh1, sc1, csh1, csc1, n1w, w_in_p)


def _inproj_bwd(dp_dn, dz, dp_na, dgt, w_in_p, x, ctx, sh1, sc1, csh1, csc1, n1w, dx1):
    def body(d1q_ref, d1k_ref, d1v_ref, d2_ref, d3_ref, d4_ref, w_ref, x_ref, c_ref, sh_ref, sc_ref, csh_ref, csc_ref,
             w1_ref, dx1_ref, dpb_ref, gx_ref, dsh_ref, dsc_ref, dcsh_ref, dcsc_ref, dw1_ref):
        b, j = pl.program_id(0), pl.program_id(1)
        is_ctx = j == 0
        dp = jnp.concatenate([d1q_ref[...], d1k_ref[...], d1v_ref[...], d2_ref[...], d3_ref[...], d4_ref[...]],
                             axis=1).astype(BF16)
        dpb_ref[...] = dp
        dhm = jnp.dot(dp, w_ref[...], preferred_element_type=F32)
        xt = jnp.where(is_ctx, c_ref[...], x_ref[...])
        sh = jnp.where(is_ctx, csh_ref[...], sh_ref[...])
        sc = jnp.where(is_ctx, csc_ref[...], sc_ref[...])
        _, vjp = jax.vjp(_norm_mod, xt, w1_ref[...], sh, sc)
        dxt, dw1, dsh, dsc = vjp(dhm)

        @pl.when((b == 0) & (j == 0))
        def _():
            dw1_ref[...] = jnp.zeros_like(dw1_ref)
            dcsh_ref[...] = jnp.zeros_like(dcsh_ref)
            dcsc_ref[...] = jnp.zeros_like(dcsc_ref)

        dw1_ref[...] += dw1

        @pl.when(is_ctx)
        def _():
            dcsh_ref[...] += dsh
            dcsc_ref[...] += dsc

        @pl.when(j == 1)
        def _():
            dsh_ref[...] = jnp.zeros_like(dsh_ref)
            dsc_ref[...] = jnp.zeros_like(dsc_ref)

        @pl.when(j >= 1)
        def _():
            dsh_ref[...] += dsh
            dsc_ref[...] += dsc
            gx_ref[...] = dx1_ref[...] + dxt

    xs, cs, ms, cms = _tok_specs()
    tsp = lambda w: pl.BlockSpec((None, RT, w), lambda b, j: (b, j, 0))
    one = pl.BlockSpec((1, D), lambda b, j: (0, 0))
    return pl.pallas_call(
        body, name="inproj_bwd", grid=(NB, T // RT),
        out_shape=[jax.ShapeDtypeStruct((NB, T, PC), BF16), jax.ShapeDtypeStruct((NB, SEQ, D), F32),
                   jax.ShapeDtypeStruct((NB, 1, D), F32), jax.ShapeDtypeStruct((NB, 1, D), F32),
                   jax.ShapeDtypeStruct((1, 1, D), F32), jax.ShapeDtypeStruct((1, 1, D), F32),
                   jax.ShapeDtypeStruct((1, D), F32)],
        in_specs=[tsp(512), tsp(512), tsp(512), tsp(512), tsp(1536), tsp(128), pl.BlockSpec((PC, D), lambda b, j: (0, 0)),
                  xs, cs, ms, ms, cms, cms, one, _rows2d_spec(D)],
        out_specs=[tsp(PC), xs, ms, ms, cms, cms, one],
        compiler_params=_cp(vmem=VMEM_BIG),
    )(*dp_dn, dz, dp_na, dgt, w_in_p, x, ctx, sh1, sc1, csh1, csc1, n1w, dx1)


def _shift_masks():
    t = lax.broadcasted_iota(jnp.int32, (T, 1), 0)
    lo = jnp.where(t < CT, 0, CT)
    hi = jnp.where(t < CT, CT, T)
    return {off: (t + off >= lo) & (t + off < hi) for off in range(-2, 3)}


def _shifted(u, off, masks):
    s = u if off == 0 else pltpu.roll(u, (-off) % T, 0)
    return jnp.where(masks[off], s, 0.0)


def _conv_pre(u, w, masks):
    acc = _shifted(u, -2, masks) * w[0:1, :]
    for kk in range(1, 5):
        acc = acc + _shifted(u, kk - 2, masks) * w[kk:kk + 1, :]
    return acc


def _act_q(pre):
    a = _silu(pre)
    return a * lax.rsqrt(jnp.sum(a * a, axis=-1, keepdims=True) + EPS) * (128.0 ** -0.5)


def _act_k(pre):
    a = _silu(pre)
    return a * lax.rsqrt(jnp.sum(a * a, axis=-1, keepdims=True) + EPS)


def _dn_prep_fwd(p, cw):
    def body(pq_ref, pk_ref, pv_ref, wq_ref, wk_ref, wv_ref, q_ref, k_ref, v_ref):
        masks = _shift_masks()
        q_ref[...] = _act_q(_conv_pre(pq_ref[...], wq_ref[...], masks))
        k_ref[...] = _act_k(_conv_pre(pk_ref[...], wk_ref[...], masks))
        v_ref[...] = _silu(_conv_pre(pv_ref[...], wv_ref[...], masks))

    ps = lambda o: pl.BlockSpec((None, T, 128), lambda b, h: (b, 0, o + h))
    ws = lambda o: pl.BlockSpec((8, 128), lambda b, h: (0, o + h))
    os_ = pl.BlockSpec((None, T, 128), lambda b, h: (b, 0, h))
    return pl.pallas_call(
        body, name="dn_prep_fwd", grid=(NB, DNH), out_shape=[jax.ShapeDtypeStruct((NB, T, DNW), F32)] * 3,
        in_specs=[ps(0), ps(4), ps(8), ws(0), ws(4), ws(8)], out_specs=[os_] * 3,
        compiler_params=_cp(vmem=VMEM_BIG),
    )(p, p, p, cw, cw, cw)


def _dn_prep_bwd(p, cw, dqs, dks, dvs):
    def body(pq_ref, pk_ref, pv_ref, wq_ref, wk_ref, wv_ref, dqf_ref, dqb_ref, dkf_ref, dkb_ref, dvf_ref, dvb_ref,
             oq_ref, ok_ref, ov_ref, gq_ref, gk_ref, gv_ref):
        b = pl.program_id(1)
        masks = _shift_masks()

        def one(p_ref, w_ref, df_ref, db_ref, act, o_ref, g_ref):
            u = p_ref[...]
            w = w_ref[...]
            _, vjp = jax.vjp(act, _conv_pre(u, w, masks))
            (dpre,) = vjp(df_ref[...] + db_ref[...])
            du = _shifted(dpre, 2, masks) * w[0:1, :]
            rows = [jnp.sum(dpre * _shifted(u, -2, masks), axis=0, keepdims=True)]
            for kk in range(1, 5):
                du = du + _shifted(dpre, 2 - kk, masks) * w[kk:kk + 1, :]
                rows.append(jnp.sum(dpre * _shifted(u, kk - 2, masks), axis=0, keepdims=True))
            o_ref[...] = du
            gw = jnp.concatenate(rows + [jnp.zeros((3, 128), F32)], axis=0)

            @pl.when(b == 0)
            def _():
                g_ref[...] = gw

            @pl.when(b > 0)
            def _():
                g_ref[...] += gw

        one(pq_ref, wq_ref, dqf_ref, dqb_ref, _act_q, oq_ref, gq_ref)
        one(pk_ref, wk_ref, dkf_ref, dkb_ref, _act_k, ok_ref, gk_ref)
        one(pv_ref, wv_ref, dvf_ref, dvb_ref, _silu, ov_ref, gv_ref)

    ps = lambda o: pl.BlockSpec((None, T, 128), lambda h, b: (b, 0, o + h))
    ws = lambda o: pl.BlockSpec((8, 128), lambda h, b: (0, o + h))
    ds = pl.BlockSpec((None, T, 128), lambda h, b: (b, 0, h))
    outs = pl.pallas_call(
        body, name="dn_prep_bwd", grid=(DNH, NB),
        out_shape=[jax.ShapeDtypeStruct((NB, T, DNW), F32)] * 3 + [jax.ShapeDtypeStruct((8, DNW), F32)] * 3,
        in_specs=[ps(0), ps(4), ps(8), ws(0), ws(4), ws(8)] + [ds] * 6,
        out_specs=[ds] * 3
        + [pl.BlockSpec((8, 128), lambda h, b: (0, h))] * 3,
        compiler_params=_cp(vmem=VMEM_BIG),
    )(p, p, p, cw, cw, cw, *dqs, *dks, *dvs)
    return outs


def _gate_fn(a, prm):
    lane = lax.broadcasted_iota(jnp.int32, a.shape, 1)
    z = a + prm[1:2, :]
    sp = jnp.maximum(z, 0.0) + jnp.log(1.0 + jnp.exp(-jnp.abs(z)))
    g = -jnp.exp(prm[0:1, :]) * sp
    return jnp.where(lane < 8, g, jnp.where(lane < 16, _sigmoid(a), 0.0))


def _gates_fwd(p, prm):
    def body(a_ref, prm_ref, o_ref):
        o_ref[...] = _gate_fn(a_ref[...], prm_ref[...])

    return pl.pallas_call(
        body, name="gates_fwd", grid=(NB,), out_shape=jax.ShapeDtypeStruct((NB, T, 128), F32),
        in_specs=[pl.BlockSpec((None, T, 128), lambda b: (b, 0, COL_G // 128)), pl.BlockSpec((8, 128), lambda b: (0, 0))],
        out_specs=pl.BlockSpec((None, T, 128), lambda b: (b, 0, 0)),
    )(p, prm)


def _gates_bwd(p, prm, dgf, dgb):
    def body(a_ref, prm_ref, dgf_ref, dgb_ref, o_ref, dprm_ref):
        b = pl.program_id(0)
        tot = dgf_ref[...] + dgb_ref[...]
        _, vjp = jax.vjp(_gate_fn, a_ref[...], prm_ref[...])
        da, dprm = vjp(tot)
        o_ref[...] = da

        @pl.when(b == 0)
        def _():
            dprm_ref[...] = dprm

        @pl.when(b > 0)
        def _():
            dprm_ref[...] += dprm

    return pl.pallas_call(
        body, name="gates_bwd", grid=(NB,),
        out_shape=[jax.ShapeDtypeStruct((NB, T, 128), F32), jax.ShapeDtypeStruct((8, 128), F32)],
        in_specs=[pl.BlockSpec((None, T, 128), lambda b: (b, 0, COL_G // 128)), pl.BlockSpec((8, 128), lambda b: (0, 0)),
                  pl.BlockSpec((None, T, 128), lambda b: (b, 0, 0)), pl.BlockSpec((None, T, 128), lambda b: (b, 0, 0))],
        out_specs=[pl.BlockSpec((None, T, 128), lambda b: (b, 0, 0)), pl.BlockSpec((8, 128), lambda b: (0, 0))],
        compiler_params=_cp(vmem=VMEM_BIG),
    )(p, prm, dgf, dgb)


def _lockstep(gens, joint=None):
    n = len(gens)
    res, vals = [None] * n, [None] * n
    while True:
        msgs = []
        for i, g in enumerate(gens):
            try:
                msgs.append((g.send(vals[i]), False))
            except StopIteration as e:
                res[i] = e.value
                msgs.append((None, True))
        if msgs[0][1]:
            return res
        if msgs[0][0] is not None:
            vals = list(joint(tuple(m[0] for m in msgs)))
        else:
            vals = [None] * n


def _tri_inv_gen(m, md, m1, m2, eye):
    n = -(m * md)
    t = eye + n
    pw = n
    for _ in range(3):
        pw = _dg(pw, pw, _NN)
        yield
        t = t + _dg(t, pw, _NN)
        yield
    a = _dg(t, m * m1, _NN)
    yield
    t = t - _dg(a, t, _NN)
    yield
    a = _dg(t, m * m2, _NN)
    yield
    return t - _dg(a, t, _NN)


def _tri_inv_bwd_gen(t, g):
    tt = t.T
    a = _dg(tt, g, _NN)
    yield
    return -_dg(a, tt, _NN)


@jax.custom_vjp
def _tri_inv(ms, md, m1, m2, eye):
    return tuple(_lockstep([_tri_inv_gen(m, md, m1, m2, eye) for m in ms]))


def _tri_inv_f(ms, md, m1, m2, eye):
    ts = _tri_inv(ms, md, m1, m2, eye)
    return ts, (ts, md)


def _tri_inv_b(res, gs):
    ts, md = res
    z = jnp.zeros_like(md)
    return (tuple(_lockstep([_tri_inv_bwd_gen(t, g) for t, g in zip(ts, gs)])), z, z, z, z)


_tri_inv.defvjp(_tri_inv_f, _tri_inv_b)


@jax.custom_vjp
def _tri_inv_saved(ms, ts):
    return tuple(t.astype(F32) for t in ts)


def _tri_inv_saved_f(ms, ts):
    return _tri_inv_saved(ms, ts), ts


def _tri_inv_saved_b(ts, gs):
    dms = tuple(_lockstep([_tri_inv_bwd_gen(t, g) for t, g in zip(ts, gs)]))
    return dms, tuple(jnp.zeros_like(t) for t in ts)


_tri_inv_saved.defvjp(_tri_inv_saved_f, _tri_inv_saved_b)


SR = DNH * CH


def _scan_masks():
    i, j = np.arange(SR)[:, None], np.arange(SR)[None, :]
    same = (i >> 6) == (j >> 6)
    cm = np.zeros((2, 6, SR, SR), np.float32)
    for d in range(2):
        diff = i - j if d == 0 else j - i
        cm[d, 0] = same & (diff >= 0)
        cm[d, 1] = same & (diff > 0)
        cm[d, 2] = (i >> 4) == (j >> 4)
        cm[d, 3] = ((i >> 5) == (j >> 5)) & ((i >> 4) != (j >> 4))
        cm[d, 4] = (i >> 5) != (j >> 5)
        cm[d, 5] = i == j
    t, u = np.arange(CH)[:, None], np.arange(CH)[None, :]
    c64 = np.stack([t >= u, t <= u]).astype(np.float32)
    return jnp.asarray(cm), jnp.asarray(c64)


def _hsl(h):
    return slice(CH * h, CH * h + CH), slice(128 * h, 128 * h + 128)


@jax.custom_vjp
def _heads_mm(x, s):
    return jnp.concatenate([_dg(x[_hsl(h)[0]], s[:, _hsl(h)[1]], _NN) for h in range(DNH)], axis=0)


def _heads_mm_b(res, g):
    x, s = res
    dx = jnp.concatenate([_dg(g[_hsl(h)[0]], s[:, _hsl(h)[1]], _NT) for h in range(DNH)], axis=0)
    ds = jnp.concatenate([_dg(x[_hsl(h)[0]], g[_hsl(h)[0]], _TN) for h in range(DNH)], axis=1)
    return dx, ds


_heads_mm.defvjp(lambda x, s: (_heads_mm(x, s), (x, s)), _heads_mm_b)


@jax.custom_vjp
def _heads_outer(k, v):
    return jnp.concatenate([_dg(k[_hsl(h)[0]], v[_hsl(h)[0]], _TN) for h in range(DNH)], axis=1)


def _heads_outer_b(res, g):
    k, v = res
    dk = jnp.concatenate([_dg(v[_hsl(h)[0]], g[:, _hsl(h)[1]], _NT) for h in range(DNH)], axis=0)
    dv = jnp.concatenate([_dg(k[_hsl(h)[0]], g[:, _hsl(h)[1]], _NN) for h in range(DNH)], axis=0)
    return dk, dv


_heads_outer.defvjp(lambda k, v: (_heads_outer(k, v), (k, v)), _heads_outer_b)


@jax.custom_vjp
def _mm_shared_lhs(t, a, b):
    r = _dg(t, jnp.concatenate([a, b], axis=1), _NN)
    return r[:, :a.shape[1]], r[:, a.shape[1]:]


def _mm_shared_lhs_b(res, gs):
    t, a, b = res
    g = jnp.concatenate(gs, axis=1)
    dab = _dg(t, g, _TN)
    return _dg(g, jnp.concatenate([a, b], axis=1), _NT), dab[:, :a.shape[1]], dab[:, a.shape[1]:]


_mm_shared_lhs.defvjp(lambda t, a, b: (_mm_shared_lhs(t, a, b), (t, a, b)), _mm_shared_lhs_b)


@jax.custom_vjp
def _mm_nt_shared_rhs(a, b, k):
    r = _dg(jnp.concatenate([a, b], axis=0), k, _NT)
    return r[:a.shape[0]], r[a.shape[0]:]


def _mm_nt_shared_rhs_b(res, gs):
    a, b, k = res
    g = jnp.concatenate(gs, axis=0)
    dab = _dg(g, k, _NN)
    return dab[:a.shape[0]], dab[a.shape[0]:], _dg(g, jnp.concatenate([a, b], axis=0), _TN)


_mm_nt_shared_rhs.defvjp(lambda a, b, k: (_mm_nt_shared_rhs(a, b, k), (a, b, k)), _mm_nt_shared_rhs_b)


def _chunk_gen(q, k, v, gb, s, cm, c64, d):
    incl, strict = cm[0], cm[1]
    lane = lax.broadcasted_iota(jnp.int32, (1, 128), 1)
    oh = lambda n: jnp.where(lane == n, 1.0, 0.0).astype(F32)
    col = lambda x, base: jnp.concatenate(
        [jnp.sum(x * oh(base + d * DNH + h), axis=1, keepdims=True) for h in range(DNH)], axis=0)
    beta = col(gb, 8)
    gam = col(_mask_mm(c64, gb), 0)
    yield
    tot = jnp.sum(gb, axis=0, keepdims=True)
    gtots = [jnp.sum(tot * oh(d * DNH + h), axis=1, keepdims=True) for h in range(DNH)]
    gtot = jnp.concatenate([jnp.broadcast_to(t, (CH, 1)) for t in gtots], axis=0)
    eg = jnp.concatenate([jnp.broadcast_to(jnp.exp(t), (1, 128)) for t in gtots], axis=1)
    row0 = jnp.where(lax.broadcasted_iota(jnp.int32, (128, 1), 0) == 0, 1.0, 0.0).astype(F32)
    gam_row = jnp.sum(jnp.broadcast_to(gam, (SR, 128)).T * row0, axis=0, keepdims=True)
    dec = jnp.exp(jnp.where(incl > 0.5, gam - gam_row, NEG))
    e_gam = jnp.exp(gam)
    kb = k * beta
    kk, qk = _mm_nt_shared_rhs(kb, q, k)
    yield
    tm = yield strict * (kk * dec)
    u, w = _mm_shared_lhs(tm, v * beta, kb * e_gam)
    yield
    qk = qk * dec
    ws = _heads_mm(w, s)
    yield
    v_new = u - ws
    qs = _heads_mm(q * e_gam, s)
    yield
    o = qs + _mm(qk, v_new)
    yield
    s_new = s * eg + _heads_outer(k * jnp.exp(gtot - gam), v_new)
    return o, s_new, tm


def _chunks(args, cm_ref, c64_ref, saved=None):
    cm0 = cm_ref[0]
    if saved is None:
        inv = lambda ms: _tri_inv(ms, cm0[2], cm0[3], cm0[4], cm0[5])
    else:
        inv = lambda ms: _tri_inv_saved(ms, saved)
    return _lockstep([_chunk_gen(q, k, v, gb, s, cm_ref[d], c64_ref[d], d) for q, k, v, gb, s, d in args], inv)


def _chunk_rev(s):
    return jnp.where(s < NCH_CTX, NCH_CTX - 1 - s, NCH + NCH_CTX - 1 - s)


def _scan_specs(step):
    cf = lambda s: step(s)
    cb = lambda s: _chunk_rev(step(s))
    tok = lambda c: pl.BlockSpec((NB, CH, DNW), lambda s: (0, c(s), 0))
    gat = lambda c: pl.BlockSpec((NB, CH, 128), lambda s: (0, c(s), 0))
    sta = lambda c: pl.BlockSpec((NB, None, 128, DNW), lambda s: (0, c(s), 0, 0))
    inv = lambda c: pl.BlockSpec((NB, None, SR, SR), lambda s: (0, c(s), 0, 0))
    return (tok(cf), gat(cf), sta(cf), inv(cf)), (tok(cb), gat(cb), sta(cb), inv(cb))


def _const_specs():
    return [pl.BlockSpec((2, 6, SR, SR), lambda s: (0, 0, 0, 0)), pl.BlockSpec((2, CH, CH), lambda s: (0, 0, 0))]


SCAN_CHAINS = [(b, d) for b in range(NB) for d in range(2)]


def _stack(ref, b):
    return jnp.concatenate([ref[b, :, 128 * h:128 * h + 128] for h in range(DNH)], axis=0)


def _unstack_to(ref, b, val):
    for h in range(DNH):
        ref[b, :, 128 * h:128 * h + 128] = val[CH * h:CH * h + CH]


def _gather_plan(ins, outs, pos):
    me = _slot(pos)
    locals_, xfers = [], []
    for src, dst in zip(ins, outs):
        locals_.append((src, dst.at[me]))
        for k in range(1, 8):
            peer = _flip(pos, k)
            xfers.append((src, dst.at[me], dst.at[_slot(peer)], peer))
    return locals_, xfers


def _scan_fwd(q, k, v, gates, carry=()):
    nc = len(carry)

    def body(*refs):
        cm_ref, c64_ref, qf, kf, vf, gf, qb, kb, vb, gb = refs[:10]
        c_in = refs[10:10 + nc]
        of_ref, sf_ref, tf_ref, ob_ref, sb_ref, tb_ref = refs[10 + nc:16 + nc]
        c_out = refs[16 + nc:16 + 2 * nc]
        s_ref = refs[16 + 2 * nc]
        sems = refs[17 + 2 * nc:]
        if nc:
            @pl.when(pl.program_id(0) == 0)
            def _():
                _xfer_start(_gather_plan(c_in, c_out, _my_pos()), sems)

        @pl.when(pl.program_id(0) == 0)
        def _():
            s_ref[...] = jnp.zeros_like(s_ref)

        dirs = ((qf, kf, vf, gf, of_ref, sf_ref, tf_ref), (qb, kb, vb, gb, ob_ref, sb_ref, tb_ref))
        args = [(_stack(dirs[d][0], b), _stack(dirs[d][1], b), _stack(dirs[d][2], b), dirs[d][3][b], s_ref[i], d)
                for i, (b, d) in enumerate(SCAN_CHAINS)]
        for i, (o, s_new, tm) in enumerate(_chunks(args, cm_ref, c64_ref)):
            b, d = SCAN_CHAINS[i]
            dirs[d][5][b] = args[i][4]
            dirs[d][6][b] = tm.astype(BF16)
            _unstack_to(dirs[d][4], b, o)
            s_ref[i] = s_new

        if nc:
            @pl.when(pl.program_id(0) == NCH - 1)
            def _():
                _xfer_wait(_gather_plan(c_in, c_out, _my_pos()), sems)

    (tf, gf_, sf_, if_), (tb, gb_, sb_, ib_) = _scan_specs(lambda s: s)
    o_sh = jax.ShapeDtypeStruct((NB, T, DNW), F32)
    st_sh = jax.ShapeDtypeStruct((NB, NCH, 128, DNW), F32)
    inv_sh = jax.ShapeDtypeStruct((NB, NCH, SR, SR), BF16)
    hbm = pl.BlockSpec(memory_space=pltpu.HBM)
    return pl.pallas_call(
        body, name="scan_fwd", grid=(NCH,),
        out_shape=[o_sh, st_sh, inv_sh, o_sh, st_sh, inv_sh] + [jax.ShapeDtypeStruct((8,) + h.shape, h.dtype) for h in carry],
        in_specs=_const_specs() + [tf, tf, tf, gf_, tb, tb, tb, gb_] + [hbm] * nc,
        out_specs=[tf, sf_, if_, tb, sb_, ib_] + [hbm] * nc,
        scratch_shapes=[pltpu.VMEM((len(SCAN_CHAINS), 128, DNW), F32)] + (_xfer_sems(nc, 7 * nc) if nc else []),
        compiler_params=_cp(vmem=VMEM_BIG),
    )(*_scan_masks(), q, k, v, gates, q, k, v, gates, *carry)


def _scan_bwd(q, k, v, gates, st_f, st_b, inv_f, inv_b, do, carry=()):
    nc = len(carry)

    def body(*refs):
        (cm_ref, c64_ref, qf, kf, vf, gf, sf, tf_, dof, qb, kb, vb, gb, sb, tb_, dob) = refs[:16]
        c_in = refs[16:16 + nc]
        dqf, dkf, dvf, dgf, dqb, dkb, dvb, dgb = refs[16 + nc:24 + nc]
        c_out = refs[24 + nc:24 + 2 * nc]
        ds_ref = refs[24 + 2 * nc]
        sems = refs[25 + 2 * nc:]
        first = pl.program_id(0) == 0
        last = pl.program_id(0) == NCH - 1
        if nc:
            @pl.when(first)
            def _():
                _xfer_start(_scatter_plan(c_in, c_out, _my_pos()), sems)

        @pl.when(first)
        def _():
            ds_ref[...] = jnp.zeros_like(ds_ref)

        dirs = ((qf, kf, vf, gf, sf, dof, dqf, dkf, dvf, dgf), (qb, kb, vb, gb, sb, dob, dqb, dkb, dvb, dgb))
        prim = tuple((_stack(dirs[d][0], b), _stack(dirs[d][1], b), _stack(dirs[d][2], b), dirs[d][3][b], dirs[d][4][b])
                     for b, d in SCAN_CHAINS)
        cots = tuple((_stack(dirs[d][5], b), ds_ref[i]) for i, (b, d) in enumerate(SCAN_CHAINS))
        saved = tuple((tf_, tb_)[d][b] for b, d in SCAN_CHAINS)
        _, vjp = jax.vjp(lambda ps: tuple(r[:2] for r in _chunks([p + (SCAN_CHAINS[i][1],) for i, p in enumerate(ps)],
                                                                cm_ref, c64_ref, saved)), prim)
        (grads,) = vjp(cots)
        for i, (dq, dk, dv, dg, ds) in enumerate(grads):
            b, d = SCAN_CHAINS[i]
            _unstack_to(dirs[d][6], b, dq)
            _unstack_to(dirs[d][7], b, dk)
            _unstack_to(dirs[d][8], b, dv)
            dirs[d][9][b] = dg
            ds_ref[i] = ds

        if nc:
            @pl.when(last)
            def _():
                _xfer_wait(_scatter_plan(c_in, c_out, _my_pos()), sems)

    (tf, gf_, sf_, if_), (tb, gb_, sb_, ib_) = _scan_specs(lambda s: NCH - 1 - s)
    o_sh = jax.ShapeDtypeStruct((NB, T, DNW), F32)
    g_sh = jax.ShapeDtypeStruct((NB, T, 128), F32)
    hbm = pl.BlockSpec(memory_space=pltpu.HBM)
    return pl.pallas_call(
        body, name="scan_bwd", grid=(NCH,),
        out_shape=[o_sh, o_sh, o_sh, g_sh] * 2 + [jax.ShapeDtypeStruct(g.shape, g.dtype) for g in carry],
        in_specs=_const_specs() + [tf, tf, tf, gf_, sf_, if_, tf, tb, tb, tb, gb_, sb_, ib_, tb] + [hbm] * nc,
        out_specs=[tf, tf, tf, gf_, tb, tb, tb, gb_] + [hbm] * nc,
        scratch_shapes=[pltpu.VMEM((len(SCAN_CHAINS), 128, DNW), F32)] + (_xfer_sems(nc, 7 * nc) if nc else []),
        compiler_params=_cp(vmem=VMEM_BIG),
    )(*_scan_masks(), q, k, v, gates, st_f, inv_f, do, q, k, v, gates, st_b, inv_b, do, *carry)


def _segmean(a, bd):
    hi = a.astype(BF16)
    lo = (a - hi.astype(F32)).astype(BF16)
    return jnp.dot(hi, bd, preferred_element_type=F32) + jnp.dot(lo, bd, preferred_element_type=F32)


def _na_prep_fwd(p, qw, kw, bd):
    def body(q_ref, k_ref, v_ref, qw_ref, kw_ref, bd_ref, oq_ref, ok_ref, ov_ref):
        bdm = bd_ref[...]
        q = q_ref[...]
        k = k_ref[...]
        oq_ref[...] = (q * lax.rsqrt(_segmean(q * q, bdm) + EPS) * qw_ref[...] * 0.125).astype(BF16)
        ok_ref[...] = (k * lax.rsqrt(_segmean(k * k, bdm) + EPS) * kw_ref[...]).astype(BF16)
        ov_ref[...] = v_ref[...].astype(BF16)

    ps = lambda o: pl.BlockSpec((None, RT, NAW), lambda b, j: (b, j, COL_NA // NAW + o))
    one = pl.BlockSpec((1, NAW), lambda b, j: (0, 0))
    os_ = pl.BlockSpec((None, RT, NAW), lambda b, j: (b, j, 0))
    return pl.pallas_call(
        body, name="na_prep_fwd", grid=(NB, T // RT), out_shape=[jax.ShapeDtypeStruct((NB, T, NAW), BF16)] * 3,
        in_specs=[ps(0), ps(1), ps(2), one, one, pl.BlockSpec((NAW, NAW), lambda b, j: (0, 0))], out_specs=[os_] * 3,
    )(p, p, p, qw, kw, bd)


def _na_prep_bwd(p, qw, kw, bd, dq, dk, dv):
    def body(q_ref, k_ref, qw_ref, kw_ref, bd_ref, dq_ref, dk_ref, dv_ref, o_ref, gq_ref, gk_ref):
        b, j = pl.program_id(0), pl.program_id(1)
        bdm = bd_ref[...]

        def one(x, w, dy, scale):
            r = lax.rsqrt(_segmean(x * x, bdm) + EPS)
            g = dy * w * scale
            dx = r * g - x * (r * r * r) * _segmean(g * x, bdm)
            dw = jnp.sum(dy * x * r * scale, axis=0, keepdims=True)
            return dx, dw

        dqv = jnp.where(j == 0, 0.0, dq_ref[...])
        dxq, dwq = one(q_ref[...], qw_ref[...], dqv, 0.125)
        dxk, dwk = one(k_ref[...], kw_ref[...], dk_ref[...], 1.0)
        o_ref[...] = jnp.concatenate([dxq, dxk, dv_ref[...]], axis=1)
        pad = jnp.zeros((7, NAW), F32)

        @pl.when((b == 0) & (j == 0))
        def _():
            gq_ref[...] = jnp.zeros_like(gq_ref)
            gk_ref[...] = jnp.zeros_like(gk_ref)

        gq_ref[...] += jnp.concatenate([dwq, pad], axis=0)
        gk_ref[...] += jnp.concatenate([dwk, pad], axis=0)

    ps = lambda o: pl.BlockSpec((None, RT, NAW), lambda b, j: (b, j, COL_NA // NAW + o))
    one = pl.BlockSpec((1, NAW), lambda b, j: (0, 0))
    ts = pl.BlockSpec((None, RT, NAW), lambda b, j: (b, j, 0))
    acc = pl.BlockSpec((8, NAW), lambda b, j: (0, 0))
    return pl.pallas_call(
        body, name="na_prep_bwd", grid=(NB, T // RT),
        out_shape=[jax.ShapeDtypeStruct((NB, T, 1536), F32), jax.ShapeDtypeStruct((8, NAW), F32),
                   jax.ShapeDtypeStruct((8, NAW), F32)],
        in_specs=[ps(0), ps(1), one, one, pl.BlockSpec((NAW, NAW), lambda b, j: (0, 0)),
                  pl.BlockSpec((None, RT, NAW), lambda b, j: (b, jnp.maximum(j - 1, 0), 0)), ts, ts],
        out_specs=[pl.BlockSpec((None, RT, 1536), lambda b, j: (b, j, 0)), acc, acc],
        compiler_params=_cp(vmem=VMEM_BIG),
    )(p, p, qw, kw, bd, dq, dk, dv)


def _na_row0(r):
    return jnp.clip(r - WIN_ROWS // 2, 0, ROWS - WIN_ROWS)


def _na_cls(r):
    return _na_row0(r) - r + WIN_ROWS - 1


NHS = 4
NLW = NHS * 64
NSR = NHS * GRID_W


def _na_stack(x):
    lane = lax.broadcasted_iota(jnp.int32, (1, NLW), 1)
    return jnp.concatenate([jnp.where((lane >> 6) == h, x, jnp.zeros_like(x)) for h in range(NHS)], axis=0)


def _na_unstack(y):
    lane = lax.broadcasted_iota(jnp.int32, (1, NLW), 1)
    out = jnp.zeros((GRID_W, NLW), y.dtype)
    for h in range(NHS):
        out = jnp.where((lane >> 6) == h, y[GRID_W * h:GRID_W * h + GRID_W], out)
    return out


def _na_fwd_gen(q, kw, vw, kc, vc, bias):
    q4 = _na_stack(q)
    s_w = lax.dot_general(q4, kw, _NT, preferred_element_type=F32) + bias
    yield
    s_c = lax.dot_general(q4, kc, _NT, preferred_element_type=F32)
    yield
    mx = jnp.maximum(jnp.max(s_w, axis=1, keepdims=True), jnp.max(s_c, axis=1, keepdims=True))
    p_w = jnp.exp(s_w - mx)
    p_c = jnp.exp(s_c - mx)
    inv = 1.0 / (jnp.sum(p_w, axis=1, keepdims=True) + jnp.sum(p_c, axis=1, keepdims=True))
    yield
    o_w = jnp.dot(p_w.astype(BF16), vw, preferred_element_type=F32)
    yield
    o4 = (o_w + jnp.dot(p_c.astype(BF16), vc, preferred_element_type=F32)) * inv
    return _na_unstack(o4)


NRF = 2


def _na_fwd(qn, kn, vb, bias, carry=()):
    nc = len(carry)
    grid = (NB, ROWS // NRF)
    ng = NAH // NHS

    def body(*refs):
        q_ref, k_ref, v_ref = refs[:3]
        b_refs, rest = refs[3:3 + NRF], refs[3 + NRF:]
        c_in, o_ref, c_out, sems = rest[:nc], rest[nc], rest[nc + 1:2 * nc + 1], rest[2 * nc + 1:]
        ids = [pl.program_id(a) for a in range(2)]
        if nc:
            @pl.when((ids[0] == 0) & (ids[1] == 0))
            def _():
                _xfer_start(_gather_plan(c_in, c_out, _my_pos()), sems)

        gens = []
        for i in range(NRF):
            start = pl.multiple_of(CT + _na_row0(NRF * ids[1] + i) * GRID_W, GRID_W)
            rw = slice(GRID_W * i, GRID_W * (i + 1))
            for g in range(ng):
                ln = slice(NLW * g, NLW * g + NLW)
                gens.append(_na_fwd_gen(q_ref[rw, ln], k_ref[pl.ds(start, NWIN), ln], v_ref[pl.ds(start, NWIN), ln],
                                        k_ref[0:CT, ln], v_ref[0:CT, ln], b_refs[i][g]))
        for n, o in enumerate(_lockstep(gens)):
            i, g = n // ng, n % ng
            o_ref[GRID_W * i:GRID_W * (i + 1), NLW * g:NLW * g + NLW] = o.astype(BF16)

        if nc:
            @pl.when((ids[0] == grid[0] - 1) & (ids[1] == grid[1] - 1))
            def _():
                _xfer_wait(_gather_plan(c_in, c_out, _my_pos()), sems)

    qs = pl.BlockSpec((None, NRF * GRID_W, NAW), lambda b, j: (b, CT // (NRF * GRID_W) + j, 0))
    kv = pl.BlockSpec((None, T, NAW), lambda b, j: (b, 0, 0))
    bss = [pl.BlockSpec((None, ng, NSR, NWIN), lambda b, j, i=i: (_na_cls(NRF * j + i), 0, 0, 0)) for i in range(NRF)]
    lat = pl.BlockSpec((None, NRF * GRID_W, NAW), lambda b, j: (b, j, 0))
    hbm = pl.BlockSpec(memory_space=pltpu.HBM)
    res = pl.pallas_call(
        body, name="na_fwd", grid=grid,
        out_shape=[jax.ShapeDtypeStruct((NB, SEQ, NAW), BF16)] + [jax.ShapeDtypeStruct((8,) + h.shape, h.dtype) for h in carry],
        in_specs=[qs, kv, kv] + bss + [hbm] * nc, out_specs=[lat] + [hbm] * nc,
        scratch_shapes=_xfer_sems(nc, 7 * nc) if nc else [],
        compiler_params=_cp(vmem=VMEM_BIG),
    )(qn, kn, vb, *([bias] * NRF), *carry)
    return res[0], list(res[1:])


def _na_bwd_gen(q, kw, vw, kc, vc, bias, do):
    q4 = _na_stack(q)
    s_w = lax.dot_general(q4, kw, _NT, preferred_element_type=F32) + bias
    yield
    s_c = lax.dot_general(q4, kc, _NT, preferred_element_type=F32)
    yield
    mx = jnp.maximum(jnp.max(s_w, axis=1, keepdims=True), jnp.max(s_c, axis=1, keepdims=True))
    p_w = jnp.exp(s_w - mx)
    p_c = jnp.exp(s_c - mx)
    inv = 1.0 / (jnp.sum(p_w, axis=1, keepdims=True) + jnp.sum(p_c, axis=1, keepdims=True))
    p_w = p_w * inv
    p_c = p_c * inv
    do4 = _na_stack(do.astype(BF16))
    dp_w = lax.dot_general(do4, vw, _NT, preferred_element_type=F32)
    yield
    dp_c = lax.dot_general(do4, vc, _NT, preferred_element_type=F32)
    yield
    delta = jnp.sum(p_w * dp_w, axis=1, keepdims=True) + jnp.sum(p_c * dp_c, axis=1, keepdims=True)
    ds_w = p_w * (dp_w - delta)
    ds_c = p_c * (dp_c - delta)
    ds_wb, ds_cb = ds_w.astype(BF16), ds_c.astype(BF16)
    dq4 = jnp.dot(ds_wb, kw, preferred_element_type=F32)
    yield
    dq4 = dq4 + jnp.dot(ds_cb, kc, preferred_element_type=F32)
    yield
    dkw = lax.dot_general(ds_wb, q4, _TN, preferred_element_type=F32)
    yield
    dvw = lax.dot_general(p_w.astype(BF16), do4, _TN, preferred_element_type=F32)
    yield
    dkc = lax.dot_general(ds_cb, q4, _TN, preferred_element_type=F32)
    yield
    dvc = lax.dot_general(p_c.astype(BF16), do4, _TN, preferred_element_type=F32)
    return ds_w, _na_unstack(dq4), dkw, dvw, dkc, dvc


NRS = 4


def _na_bwd(qn, kn, vb, bias, do):
    def body(q_ref, k_ref, v_ref, *rest):
        b_refs, (do_ref, dq_ref, dk_ref, dv_ref, db_ref) = rest[:NRS], rest[NRS:]
        b, j = pl.program_id(1), pl.program_id(2)

        @pl.when((b == 0) & (j == 0))
        def _():
            db_ref[...] = jnp.zeros_like(db_ref)

        @pl.when(j == 0)
        def _():
            dk_ref[...] = jnp.zeros_like(dk_ref)
            dv_ref[...] = jnp.zeros_like(dv_ref)

        kc, vc = k_ref[0:CT, :], v_ref[0:CT, :]
        rows = [NRS * j + i for i in range(NRS)]
        starts = [pl.multiple_of(CT + _na_row0(r) * GRID_W, GRID_W) for r in rows]
        gens = [_na_bwd_gen(q_ref[GRID_W * i:GRID_W * (i + 1), :], k_ref[pl.ds(starts[i], NWIN), :],
                            v_ref[pl.ds(starts[i], NWIN), :], kc, vc, b_refs[i][...],
                            do_ref[GRID_W * i:GRID_W * (i + 1), :]) for i in range(NRS)]
        for i, (ds_w, dq, dkw, dvw, dkc, dvc) in enumerate(_lockstep(gens)):
            db_ref[_na_cls(rows[i])] += ds_w
            dq_ref[GRID_W * i:GRID_W * (i + 1), :] = dq
            dk_ref[pl.ds(starts[i], NWIN), :] += dkw
            dv_ref[pl.ds(starts[i], NWIN), :] += dvw
            dk_ref[0:CT, :] += dkc
            dv_ref[0:CT, :] += dvc

    qs = pl.BlockSpec((None, NRS * GRID_W, NLW), lambda g, b, j: (b, CT // (NRS * GRID_W) + j, g))
    kv = pl.BlockSpec((None, T, NLW), lambda g, b, j: (b, 0, g))
    bss = [pl.BlockSpec((None, None, NSR, NWIN), lambda g, b, j, i=i: (_na_cls(NRS * j + i), g, 0, 0)) for i in range(NRS)]
    lat = pl.BlockSpec((None, NRS * GRID_W, NLW), lambda g, b, j: (b, j, g))
    return pl.pallas_call(
        body, name="na_bwd", grid=(NAH // NHS, NB, ROWS // NRS),
        out_shape=[jax.ShapeDtypeStruct((NB, SEQ, NAW), F32)] + [jax.ShapeDtypeStruct((NB, T, NAW), F32)] * 2
        + [jax.ShapeDtypeStruct((WIN_ROWS, NAH // NHS, NSR, NWIN), F32)],
        in_specs=[qs, kv, kv] + bss + [lat],
        out_specs=[lat, kv, kv, pl.BlockSpec((WIN_ROWS, None, NSR, NWIN), lambda g, b, j: (0, g, 0, 0))],
        compiler_params=_cp(vmem=VMEM_BIG),
    )(qn, kn, vb, *([bias] * NRS), do)


def _na_tables():
    cols = np.arange(GRID_W)
    win_start = np.clip(cols - WIN_COLS // 2, 0, GRID_W - WIN_COLS)
    kc = cols[None, :]
    valid = (kc >= win_start[:, None]) & (kc < win_start[:, None] + WIN_COLS)
    rel = np.clip(kc - cols[:, None] + WIN_COLS - 1, 0, 2 * WIN_COLS - 2)
    return valid, rel


def _bias_table(rpb):
    valid, rel = _na_tables()
    nrel = 2 * WIN_COLS - 1
    shifts = np.stack([(rel == r) & valid for r in range(nrel)]).astype(np.float32)
    neg = np.where(valid, 0.0, NEG).astype(np.float32)

    def body(rpb_ref, sh_ref, neg_ref, o_ref):
        h = pl.program_id(0)
        for dr in range(2 * WIN_ROWS - 1):
            tile = neg_ref[...]
            for r in range(nrel):
                tile = tile + rpb_ref[h, dr, r] * sh_ref[r]
            for cls in range(WIN_ROWS):
                i = dr - cls
                if 0 <= i < WIN_ROWS:
                    o_ref[cls, :, i * GRID_W:(i + 1) * GRID_W] = tile

    return pl.pallas_call(
        body, name="bias_table", grid=(NAH,),
        out_shape=jax.ShapeDtypeStruct((WIN_ROWS, NAH // NHS, NSR, NWIN), F32),
        in_specs=[pl.BlockSpec(memory_space=pltpu.SMEM), pl.BlockSpec((nrel, GRID_W, GRID_W), lambda h: (0, 0, 0)),
                  pl.BlockSpec((GRID_W, GRID_W), lambda h: (0, 0))],
        out_specs=pl.BlockSpec((WIN_ROWS, None, GRID_W, NWIN), lambda h: (0, h // NHS, h % NHS, 0)),
    )(rpb, jnp.asarray(shifts), jnp.asarray(neg))


def _rpb_fold(db):
    def body1(d_ref, o_ref):
        for dr in range(2 * WIN_ROWS - 1):
            acc = None
            for c in range(WIN_ROWS):
                i = dr - c
                if 0 <= i < WIN_ROWS:
                    blk = d_ref[c, :, i * GRID_W:(i + 1) * GRID_W]
                    acc = blk if acc is None else acc + blk
            o_ref[dr] = acc

    t15 = pl.pallas_call(
        body1, name="rpb_fold_rows", grid=(NAH,),
        out_shape=jax.ShapeDtypeStruct((NAH, 2 * WIN_ROWS - 1, GRID_W, GRID_W), F32),
        in_specs=[pl.BlockSpec((WIN_ROWS, None, GRID_W, NWIN), lambda h: (0, h, 0, 0))],
        out_specs=pl.BlockSpec((None, 2 * WIN_ROWS - 1, GRID_W, GRID_W), lambda h: (h, 0, 0, 0)),
        compiler_params=_cp(vmem=VMEM_BIG),
    )(db.reshape(WIN_ROWS, NAH, GRID_W, NWIN))
    valid, rel = _na_tables()
    onehot = np.zeros((GRID_W * GRID_W, 128), np.float32)
    flat_rel, flat_valid = rel.reshape(-1), valid.reshape(-1)
    onehot[np.arange(GRID_W * GRID_W)[flat_valid], flat_rel[flat_valid]] = 1.0

    def body2(a_ref, oh_ref, o_ref):
        o_ref[...] = _hmm(a_ref[...], oh_ref[...])

    vm = pl.BlockSpec(memory_space=pltpu.VMEM)
    out = pl.pallas_call(
        body2, name="rpb_fold_cols", out_shape=jax.ShapeDtypeStruct((NAH * 15, 128), F32),
        in_specs=[vm, vm], out_specs=vm, compiler_params=_cp(vmem=VMEM_BIG),
    )(t15.reshape(NAH * 15, GRID_W * GRID_W), jnp.asarray(onehot))
    return out[:, :31].reshape(NAH, 15, 31)


def _dn_out(o, z, w):
    outs = []
    for h in range(DNH):
        sl = slice(128 * h, 128 * h + 128)
        oh = o[:, sl]
        r = lax.rsqrt(jnp.mean(oh * oh, axis=-1, keepdims=True) + EPS)
        outs.append(oh * r * w[:, sl] * _silu(z[:, sl]))
    return jnp.concatenate(outs, axis=1)


def _outproj_fwd(of, ob, p, na, onw, x, g1, w_out):
    def body(of_ref, ob_ref, z_ref, na_ref, w_ref, x_ref, g_ref, wo_ref, x1_ref, ai_ref, ao_ref):
        dn = _dn_out(of_ref[...] + ob_ref[...], z_ref[...], w_ref[...])
        ai = jnp.concatenate([dn.astype(BF16), na_ref[...]], axis=1)
        ai_ref[...] = ai
        ao = jnp.dot(ai, wo_ref[...], preferred_element_type=F32)
        ao_ref[...] = ao
        x1_ref[...] = x_ref[...] + g_ref[...] * ao

    lat = lambda w: pl.BlockSpec((None, RT, w), lambda b, j: (b, j, 0))
    return pl.pallas_call(
        body, name="outproj_fwd", grid=(NB, SEQ // RT),
        out_shape=[jax.ShapeDtypeStruct((NB, SEQ, D), F32), jax.ShapeDtypeStruct((NB, SEQ, D), BF16),
                   jax.ShapeDtypeStruct((NB, SEQ, D), F32)],
        in_specs=[pl.BlockSpec((None, RT, DNW), lambda b, j: (b, j + 1, 0)),
                  pl.BlockSpec((None, RT, DNW), lambda b, j: (b, j + 1, 0)),
                  pl.BlockSpec((None, RT, DNW), lambda b, j: (b, j + 1, COL_Z // DNW)),
                  lat(NAW), pl.BlockSpec((1, DNW), lambda b, j: (0, 0)), lat(D),
                  pl.BlockSpec((None, 1, D), lambda b, j: (b, 0, 0)), pl.BlockSpec((D, D), lambda b, j: (0, 0))],
        out_specs=[lat(D), lat(D), lat(D)],
        compiler_params=_cp(vmem=VMEM_BIG),
    )(of, ob, p, na, onw, x, g1, w_out)


def _outproj_bwd(dx1, ao, g1, w_out, of, ob, p, onw):
    def body(dx_ref, ao_ref, g_ref, wo_ref, of_ref, ob_ref, z_ref, w_ref, dao_ref, do_ref, dz_ref, dna_ref, dg_ref, dw_ref):
        b, j = pl.program_id(0), pl.program_id(1)

        @pl.when((b == 0) & (j == 0))
        def _():
            dw_ref[...] = jnp.zeros_like(dw_ref)

        @pl.when(j == 0)
        def _():
            do_ref[...] = jnp.zeros_like(do_ref)
            dz_ref[...] = jnp.zeros_like(dz_ref)

        @pl.when(j == 1)
        def _():
            dg_ref[...] = jnp.zeros_like(dg_ref)

        @pl.when(j >= 1)
        def _():
            dx = dx_ref[...]
            dg_ref[...] += jnp.sum(dx * ao_ref[...], axis=0, keepdims=True)
            dao = (g_ref[...] * dx).astype(BF16)
            dao_ref[...] = dao
            dai = lax.dot_general(dao, wo_ref[...], _NT, preferred_element_type=F32)
            dna_ref[...] = dai[:, DNW:]
            _, vjp = jax.vjp(_dn_out, of_ref[...] + ob_ref[...], z_ref[...], w_ref[...])
            do, dz, dw = vjp(dai[:, :DNW])
            do_ref[...] = do
            dz_ref[...] = dz
            dw_ref[...] += jnp.concatenate([dw, jnp.zeros((7, DNW), F32)], axis=0)

    lat = lambda w: pl.BlockSpec((None, RT, w), lambda b, j: (b, jnp.maximum(j - 1, 0), 0))
    tok = lambda w: pl.BlockSpec((None, RT, w), lambda b, j: (b, j, 0))
    return pl.pallas_call(
        body, name="outproj_bwd", grid=(NB, T // RT),
        out_shape=[jax.ShapeDtypeStruct((NB, SEQ, D), BF16), jax.ShapeDtypeStruct((NB, T, DNW), F32),
                   jax.ShapeDtypeStruct((NB, T, DNW), F32), jax.ShapeDtypeStruct((NB, SEQ, NAW), F32),
                   jax.ShapeDtypeStruct((NB, 1, D), F32), jax.ShapeDtypeStruct((8, DNW), F32)],
        in_specs=[_rows2d_spec(D), lat(D), pl.BlockSpec((None, 1, D), lambda b, j: (b, 0, 0)),
                  pl.BlockSpec((D, D), lambda b, j: (0, 0)),
                  tok(DNW), tok(DNW),
                  pl.BlockSpec((None, RT, DNW), lambda b, j: (b, j, COL_Z // DNW)),
                  pl.BlockSpec((1, DNW), lambda b, j: (0, 0))],
        out_specs=[lat(D), tok(DNW), tok(DNW), lat(NAW), pl.BlockSpec((None, 1, D), lambda b, j: (b, 0, 0)),
                   pl.BlockSpec((8, DNW), lambda b, j: (0, 0))],
        compiler_params=_cp(vmem=VMEM_BIG),
    )(dx1, ao, g1, w_out, of, ob, p, onw)


FT_FWD, FT_BWD = 512, 256
NFF = DFF // FFT


def _ffn_specs(ft):
    row = lambda w: pl.BlockSpec((ft, w), lambda i, f: (i, 0))
    ex = pl.BlockSpec((None, 1, D), lambda i, f: (i // (SEQ // ft), 0, 0))
    one = pl.BlockSpec((1, D), lambda i, f: (0, 0))
    wg = pl.BlockSpec((None, D, FFT), lambda i, f: (f, 0, 0))
    wu = pl.BlockSpec((None, D, FFT), lambda i, f: (NFF + f, 0, 0))
    wo = pl.BlockSpec((FFT, D), lambda i, f: (f, 0))
    ff = pl.BlockSpec((ft, FFT), lambda i, f: (i, f))
    return row, ex, one, wg, wu, wo, ff


def _ffn_fwd(x1, sh2, sc2, g2, n2w, w_in4, w_out, target):
    ft, tpe = FT_FWD, SEQ // FT_FWD

    def body(x_ref, sh_ref, sc_ref, g_ref, w2_ref, wg_ref, wu_ref, wo_ref, t_ref,
             h2_ref, gt_ref, up_ref, dy_ref, dg_ref, loss_ref, acc_ref):
        i, f = pl.program_id(0), pl.program_id(1)

        @pl.when(f == 0)
        def _():
            h2_ref[...] = _norm_mod(x_ref[...], w2_ref[...], sh_ref[...], sc_ref[...]).astype(BF16)
            acc_ref[...] = jnp.zeros_like(acc_ref)

        h2 = h2_ref[...]
        gate = jnp.dot(h2, wg_ref[...], preferred_element_type=F32)
        up = jnp.dot(h2, wu_ref[...], preferred_element_type=F32)
        gt_ref[...] = gate.astype(BF16)
        up_ref[...] = up.astype(BF16)
        acc_ref[...] += jnp.dot((_silu(gate) * up).astype(BF16), wo_ref[...], preferred_element_type=F32)

        @pl.when((i == 0) & (f == 0))
        def _():
            loss_ref[...] = jnp.zeros_like(loss_ref)

        @pl.when(f == NFF - 1)
        def _():
            ffn = acc_ref[...]
            err = x_ref[...] + g_ref[...] * ffn - t_ref[...]
            dy = err * (1.0 / D)
            dy_ref[...] = dy
            part = jnp.sum(dy * ffn, axis=0, keepdims=True)
            loss_ref[...] += jnp.sum(jnp.sum(err * err, axis=1, keepdims=True), axis=0, keepdims=True) * (0.5 / D)

            @pl.when(i % tpe == 0)
            def _():
                dg_ref[...] = part

            @pl.when(i % tpe != 0)
            def _():
                dg_ref[...] += part

    row, ex, one, wg, wu, wo, ff = _ffn_specs(ft)
    return pl.pallas_call(
        body, name="ffn_fwd", grid=(NTOK // ft, NFF),
        out_shape=[jax.ShapeDtypeStruct((NTOK, D), BF16), jax.ShapeDtypeStruct((NTOK, DFF), BF16),
                   jax.ShapeDtypeStruct((NTOK, DFF), BF16), jax.ShapeDtypeStruct((NTOK, D), F32),
                   jax.ShapeDtypeStruct((NB, 1, D), F32), jax.ShapeDtypeStruct((8, 128), F32)],
        in_specs=[row(D), ex, ex, ex, one, wg, wu, wo, row(D)],
        out_specs=[row(D), ff, ff, row(D), ex, pl.BlockSpec((8, 128), lambda i, f: (0, 0))],
        scratch_shapes=[pltpu.VMEM((ft, D), F32)],
        compiler_params=_cp(vmem=VMEM_BIG),
    )(x1, sh2, sc2, g2, n2w, w_in4, w_in4, w_out, target)


def _rows2d_spec(width):
    return pl.BlockSpec((RT, width), lambda b, j: (b * (SEQ // RT) + jnp.maximum(j - 1, 0), 0))


def _ffn_bwd(dy, gate, up, x1, sh2, sc2, g2, n2w, w_in4, w_out):
    ft, tpe = FT_BWD, SEQ // FT_BWD
    ni = NTOK // ft

    def body(dy_ref, gt_ref, up_ref, x_ref, sh_ref, sc_ref, g_ref, w2_ref, wg_ref, wu_ref, wo_ref,
             dff_ref, act_ref, dgu_ref, dx_ref, dsh_ref, dsc_ref, dw_ref, acc_ref):
        f, i = pl.program_id(0), pl.program_id(1)
        rows = pl.ds(pl.multiple_of(i * ft, ft), ft)
        dff = (g_ref[...] * dy_ref[...]).astype(BF16)
        dff_ref[...] = dff
        dact = lax.dot_general(dff, wo_ref[...], _NT, preferred_element_type=F32)
        gate, up = gt_ref[...].astype(F32), up_ref[...].astype(F32)
        sg = _sigmoid(gate)
        sl = gate * sg
        act_ref[...] = (sl * up).astype(BF16)
        dgate = (dact * up * (sg * (1.0 + gate * (1.0 - sg)))).astype(BF16)
        dup = (dact * sl).astype(BF16)
        dgu_ref[0] = dgate
        dgu_ref[1] = dup
        part = (lax.dot_general(dgate, wg_ref[...], _NT, preferred_element_type=F32)
                + lax.dot_general(dup, wu_ref[...], _NT, preferred_element_type=F32))

        @pl.when(f == 0)
        def _():
            acc_ref[rows, :] = part

        @pl.when(f > 0)
        def _():
            acc_ref[rows, :] += part

        @pl.when((i == 0) & (f == 0))
        def _():
            dw_ref[...] = jnp.zeros_like(dw_ref)

        @pl.when(f == NFF - 1)
        def _():
            _, vjp = jax.vjp(_norm_mod, x_ref[...], w2_ref[...], sh_ref[...], sc_ref[...])
            dx, dw, dsh, dsc = vjp(acc_ref[rows, :])
            dx_ref[...] = dy_ref[...] + dx
            dw_ref[...] += dw

            @pl.when(i % tpe == 0)
            def _():
                dsh_ref[...] = dsh
                dsc_ref[...] = dsc

            @pl.when(i % tpe != 0)
            def _():
                dsh_ref[...] += dsh
                dsc_ref[...] += dsc

    last = lambda f, real, spare: jnp.where(f == NFF - 1, real, spare)
    row = lambda w: pl.BlockSpec((ft, w), lambda f, i: (i, 0))
    ex = pl.BlockSpec((None, 1, D), lambda f, i: (i // tpe, 0, 0))
    one = pl.BlockSpec((1, D), lambda f, i: (0, 0))
    wg = pl.BlockSpec((None, D, FFT), lambda f, i: (f, 0, 0))
    wu = pl.BlockSpec((None, D, FFT), lambda f, i: (NFF + f, 0, 0))
    wo = pl.BlockSpec((FFT, D), lambda f, i: (f, 0))
    ff = pl.BlockSpec((ft, FFT), lambda f, i: (i, f))
    row_once = lambda w: pl.BlockSpec((ft, w), lambda f, i: (last(f, i, ni), 0))
    ex_once = pl.BlockSpec((None, 1, D), lambda f, i: (last(f, i // tpe, NB), 0, 0))
    return pl.pallas_call(
        body, name="ffn_bwd", grid=(NFF, ni),
        out_shape=[jax.ShapeDtypeStruct((NTOK + ft, D), BF16), jax.ShapeDtypeStruct((NTOK, DFF), BF16),
                   jax.ShapeDtypeStruct((2, NTOK, DFF), BF16), jax.ShapeDtypeStruct((NTOK + ft, D), F32),
                   jax.ShapeDtypeStruct((NB + 1, 1, D), F32), jax.ShapeDtypeStruct((NB + 1, 1, D), F32),
                   jax.ShapeDtypeStruct((1, D), F32)],
        in_specs=[row(D), ff, ff, row(D), ex, ex, ex, one, wg, wu, wo],
        out_specs=[row_once(D), ff, pl.BlockSpec((2, ft, FFT), lambda f, i: (0, i, f)), row_once(D), ex_once, ex_once, one],
        scratch_shapes=[pltpu.VMEM((NTOK, D), F32)],
        compiler_params=_cp(vmem=VMEM_BIG),
    )(dy, gate, up, x1, sh2, sc2, g2, n2w, w_in4, w_in4, w_out)


def _local_step(x, ctx, target, mod_x, mod_c, norm1_w, norm2_w, w_in_p, conv_w, a_log, dt_bias, out_norm_w,
                q_norm_w, k_norm_w, rpb, w_out_b, w_ffn_in4, w_ffn_out_b, scatter_early=False, gather_late=None):
    sh1, sc1, g1, sh2, sc2, g2 = [mod_x[:, i * D:(i + 1) * D].reshape(NB, 1, D) for i in range(6)]
    csh1, csc1 = mod_c[0:D].reshape(1, 1, D), mod_c[D:2 * D].reshape(1, 1, D)
    cw = jnp.concatenate([conv_w, jnp.zeros((3, 1536), F32)], axis=0)
    prm = jnp.pad(jnp.stack([a_log.reshape(8), dt_bias.reshape(8)]), ((0, 6), (0, 120)))
    onw = jnp.tile(out_norm_w.reshape(1, 128), (1, DNH))
    qw = jnp.tile(q_norm_w.reshape(1, 64), (1, NAH))
    kw = jnp.tile(k_norm_w.reshape(1, 64), (1, NAH))
    seg = np.arange(NAW) // 64
    bd = jnp.asarray((seg[:, None] == seg[None, :]).astype(np.float32) / 64.0, dtype=BF16)
    bias = _bias_table(rpb)

    p, hm = _inproj_fwd(x, ctx, sh1, sc1, csh1, csc1, norm1_w, w_in_p)
    dq_, dk_, dv_ = _dn_prep_fwd(p, cw)
    gates = _gates_fwd(p, prm)
    res = _scan_fwd(dq_, dk_, dv_, gates, carry=gather_late[:2] if gather_late else ())
    o_f, st_f, inv_f, o_b, st_b, inv_b = res[:6]
    qn, kn, vb = _na_prep_fwd(p, qw, kw, bd)
    na, got = _na_fwd(qn, kn, vb, bias, carry=gather_late[2:] if gather_late else ())
    if gather_late:
        w_out_b = res[6].reshape(D, D)
        w_ffn_in4 = res[7].reshape(4, D, FFT)
        w_ffn_out_b = got[0].reshape(DFF, D)
    x1, ai, ao = _outproj_fwd(o_f, o_b, p, na, onw, x, g1, w_out_b)
    x1f = x1.reshape(NTOK, D)
    h2, gate, up, dy, dg2, loss8 = _ffn_fwd(x1f, sh2, sc2, g2, norm2_w, w_ffn_in4, w_ffn_out_b,
                                             target.reshape(NTOK, D))
    dff, act, dgu, dx1, dsh2, dsc2, dn2w = _ffn_bwd(dy, gate, up, x1f, sh2, sc2, g2, norm2_w, w_ffn_in4, w_ffn_out_b)
    g_w_ffn_out = _matmul_tn(act, dff, FFT, D, 512, "wgrad_ffn_out")
    g_w_ffn_in4 = _matmul_tn(
        h2, dgu, D, FFT, 1024, "wgrad_ffn_in", nn=2 * NFF,
        b_block=(None, 1024, FFT), b_map=lambda i, j, k: (j // NFF, k, j % NFF),
        out_shape=(4, D, FFT), out_block=(None, D, FFT), out_map=lambda i, j, k: (j, i, 0))
    dsh2, dsc2 = dsh2[:NB], dsc2[:NB]
    dao, do, dz, dna, dg1, donw = _outproj_bwd(dx1, ao, g1, w_out_b, o_f, o_b, p, onw)
    g_w_out = _matmul_tn(ai.reshape(NTOK, D), dao.reshape(NTOK, D), D, D, 1024, "wgrad_out")
    dqn, dkn, dvn, dbias = _na_bwd(qn, kn, vb, bias, dna)
    dp_na, dqw, dkw = _na_prep_bwd(p, qw, kw, bd, dqn, dkn, dvn)
    early = [g_w_out.reshape(8, D // 8, D), g_w_ffn_in4.reshape(8, D // 2, FFT), g_w_ffn_out.reshape(8, DFF // 8, D)]
    res = _scan_bwd(dq_, dk_, dv_, gates, st_f, st_b, inv_f, inv_b, do, carry=early if scatter_early else ())
    dqf, dkf, dvf, dgf, dqb, dkb, dvb, dgb = res[:8]
    if scatter_early:
        g_w_out, g_w_ffn_in4, g_w_ffn_out = res[8:]
    dpq, dpk, dpv, gcq, gck, gcv = _dn_prep_bwd(p, cw, (dqf, dqb), (dkf, dkb), (dvf, dvb))
    dgt, dprm = _gates_bwd(p, prm, dgf, dgb)
    dpb, grad_x, dsh1, dsc1, dcsh1, dcsc1, dn1w = _inproj_bwd(
        (dpq, dpk, dpv), dz, dp_na, dgt, w_in_p, x, ctx, sh1, sc1, csh1, csc1, norm1_w, dx1)
    g_w_in_p = _matmul_tn(dpb.reshape(NB * T, PC), hm.reshape(NB * T, D), PC, D, 512, "wgrad_in")
    g_rpb = _rpb_fold(dbias)

    dmod_x = jnp.concatenate([dsh1, dsc1, dg1, dsh2, dsc2, dg2], axis=2).reshape(NB, 6 * D)
    dmod_c = jnp.concatenate([dcsh1.reshape(D), dcsc1.reshape(D), jnp.zeros((4 * D,), F32)])
    g_conv = jnp.concatenate([gcq[:5], gck[:5], gcv[:5]], axis=1)
    small = dict(
        norm1_w=dn1w, norm2_w=dn2w, a_log=dprm[0, :8], dt_bias=dprm[1, :8],
        out_norm_w=donw[0].reshape(DNH, 128).sum(0), q_norm_w=dqw[0].reshape(NAH, 64).sum(0),
        k_norm_w=dkw[0].reshape(NAH, 64).sum(0), rpb=g_rpb, conv_w=g_conv)
    return loss8[0, 0], grad_x, dmod_x, dmod_c, small, g_w_in_p, g_w_out, g_w_ffn_in4, g_w_ffn_out


def _perm_w_in(wt):
    return jnp.concatenate([wt[:2048], wt[2064:3600], wt[2048:2064], jnp.zeros((PC - IN_COLS, wt.shape[1]), wt.dtype)],
                           axis=0)


def _unperm_w_in(gt):
    return jnp.concatenate([gt[:2048], gt[COL_G:COL_G + 16], gt[2048:COL_G]], axis=0)


def _misc_row(a_log, dt_bias, onw, qnw, knw):
    return jnp.concatenate([a_log.reshape(8), dt_bias.reshape(8), jnp.zeros((112,), F32), onw.reshape(128),
                            qnw.reshape(64), knw.reshape(64), jnp.zeros((D - 384,), F32)])


def _pack_small(c_ctx, n1, n2, b_ada, a_log, dt_bias, onw, qnw, knw, rpb, conv_s):
    misc = _misc_row(a_log, dt_bias, onw, qnw, knw)
    rp = jnp.concatenate([rpb.reshape(-1), jnp.zeros((4 * D - 3720,), F32)])
    cv = jnp.concatenate([conv_s.reshape(-1), jnp.zeros((2 * D - 1920,), F32)])
    return jnp.concatenate([c_ctx.reshape(-1), n1.reshape(-1), n2.reshape(-1), b_ada.reshape(-1), misc, rp, cv]).reshape(16, D)


def _unpack_small(a):
    f = a.reshape(-1)
    misc = f[9 * D:10 * D]
    return dict(
        c_ctx=f[0:D], norm1_w=f[D:2 * D].reshape(1, D), norm2_w=f[2 * D:3 * D].reshape(1, D),
        b_ada=f[3 * D:9 * D].reshape(1, 6 * D), dn_A_log=misc[0:8].reshape(1, 2, 4), dn_dt_bias=misc[8:16].reshape(1, 2, 4),
        dn_out_norm_w=misc[128:256].reshape(1, 128), na_q_norm_w=misc[256:320].reshape(1, 64),
        na_k_norm_w=misc[320:384].reshape(1, 64), na_rpb=f[10 * D:10 * D + 3720].reshape(1, 8, 15, 31),
        dn_conv_w=f[14 * D:14 * D + 1920].reshape(1, 5, 384))


_WEIGHTS = ['c_ctx', 'norm1_w', 'norm2_w', 'w_ada', 'b_ada', 'w_in', 'dn_conv_w', 'dn_A_log', 'dn_dt_bias',
            'dn_out_norm_w', 'na_q_norm_w', 'na_k_norm_w', 'na_rpb', 'w_out', 'w_ffn_in', 'w_ffn_out']


def kernel(x, c, ctx, c_ctx, norm1_w, norm2_w, w_ada, b_ada, w_in, dn_conv_w, dn_A_log, dn_dt_bias, dn_out_norm_w, na_q_norm_w, na_k_norm_w, na_rpb, w_out, w_ffn_in, w_ffn_out, loss_target, m_c_ctx, m_norm1_w, m_norm2_w, m_w_ada, m_b_ada, m_w_in, m_dn_conv_w, m_dn_A_log, m_dn_dt_bias, m_dn_out_norm_w, m_na_q_norm_w, m_na_k_norm_w, m_na_rpb, m_w_out, m_w_ffn_in, m_w_ffn_out, v_c_ctx, v_norm1_w, v_norm2_w, v_w_ada, v_b_ada, v_w_in, v_dn_conv_w, v_dn_A_log, v_dn_dt_bias, v_dn_out_norm_w, v_na_q_norm_w, v_na_k_norm_w, v_na_rpb, v_w_out, v_w_ffn_in, v_w_ffn_out):
    ix, iy, ic = lax.axis_index("x"), lax.axis_index("y"), lax.axis_index("c")
    chip = 2 * ix + iy
    me = 4 * ix + 2 * iy + ic

    crow = jnp.concatenate([c, c_ctx.reshape(1, D), jnp.zeros((5, D), F32)], axis=0)
    cvrow = jnp.concatenate([dn_conv_w[0].reshape(-1), jnp.zeros((2 * D - 1920,), F32)]).reshape(2, D)
    w_in_t = w_in[0].T
    gw_in, g0 = _gather_weights([_cast_bf16(lax.dynamic_slice(w_in_t, (0, ic * (D // 2)), (900, D // 2)), "cast_w_in")],
                                jnp.concatenate([crow, cvrow, jnp.zeros((6, D), F32)], axis=0))
    cin = g0[:, :8].reshape(64, D)
    conv_full = jnp.concatenate([g0[2 * s, 8:10].reshape(-1)[:1920].reshape(5, 384) for s in range(4)], axis=1)

    w_ada_s = w_ada[0]
    b_s = lax.dynamic_slice(b_ada, (0, chip * 1536), (1, 1536))
    mod_s = _ada_fwd(cin, w_ada_s, b_s)
    mods = _gather_rows(mod_s, "gather_mod")
    mod_mine = jnp.concatenate([mods[0], mods[2], mods[4], mods[6]], axis=1)
    mod_x, mod_c = mod_mine[0:NB], mod_mine[2]

    def half(w2d):
        r = w2d.shape[0] // 2
        return lax.dynamic_slice(w2d, (ic * r, 0), (r, w2d.shape[1]))

    late = [_cast_bf16(half(w_out[0]), "cast_w_out"), _cast_bf16(half(w_ffn_in[0]), "cast_w_ffn_in"),
            _cast_bf16(half(w_ffn_out[0]), "cast_w_ffn_out")]
    w_in_p = _perm_w_in(gw_in.reshape(4, 2, 900, D // 2).transpose(0, 2, 1, 3).reshape(IN_COLS, D))

    (loss_part, grad_x, dmod_x, dmod_c, small, g_w_in_p, r_w_out, r_w_ffn_in, r_w_ffn_out) = _local_step(
        x, ctx, loss_target, mod_x, mod_c, norm1_w, norm2_w, w_in_p, conv_full, dn_A_log[0], dn_dt_bias[0],
        dn_out_norm_w[0], na_q_norm_w[0], na_k_norm_w[0], na_rpb[0], None, None, None,
        scatter_early=True, gather_late=late)
    loss = lax.psum(loss_part, ("x", "y", "c"))

    dm = jnp.concatenate([dmod_x, dmod_c.reshape(1, 6 * D), jnp.zeros((5, 6 * D), F32)], axis=0).reshape(48, D)
    misc = _misc_row(small["a_log"], small["dt_bias"], small["out_norm_w"], small["q_norm_w"], small["k_norm_w"])
    sm = jnp.concatenate([
        small["norm1_w"].reshape(-1), small["norm2_w"].reshape(-1), misc,
        small["rpb"].reshape(-1), jnp.zeros((4 * D - 3720,), F32),
        small["conv_w"].reshape(-1), jnp.zeros((8 * D - 7680,), F32), jnp.zeros((D,), F32)]).reshape(16, D)
    gsl, gtot = _allgather8(jnp.concatenate([dm, sm], axis=0), "gather_small", with_sum=True)
    gmod = gsl[:, :48].reshape(64, 6 * D)
    st = gtot[48:64].reshape(-1)
    g_n1, g_n2, g_misc = st[0:D].reshape(1, D), st[D:2 * D].reshape(1, D), st[2 * D:3 * D]
    g_rpb = st[3 * D:3 * D + 3720].reshape(1, 8, 15, 31)
    g_conv_full = st[7 * D:7 * D + 7680].reshape(5, 1536)
    g_conv_s = lax.dynamic_slice(g_conv_full, (0, chip * 384), (5, 384)).reshape(1, 5, 384)

    gmod_s = lax.dynamic_slice(gmod, (0, chip * 1536), (64, 1536))
    g_w_ada_s, pctx = _ada_bwd(cin, gmod_s, w_ada_s)
    parts = _allgather8(pctx, "gather_cctx")
    g_b_ada, g_c_ctx = _ada_small(gmod, parts, c_ctx.reshape(1, D))

    g_w_in8 = _unperm_w_in(g_w_in_p).reshape(4, 900, 2, D // 2).transpose(0, 2, 1, 3).reshape(8, 900, D // 2)
    names = ["w_in", "w_out", "w_ffn_in", "w_ffn_out"]
    early = _sibling_exchange([_sum_slots(r, "sum_" + nm)
                               for r, nm in zip([r_w_out, r_w_ffn_in, r_w_ffn_out], names[1:])], "sibling_early")
    g_w_out_s = early[0].reshape(1, D // 4, D)
    g_w_ffn_in_s = early[1].reshape(1, D, FFT)
    g_w_ffn_out_s = early[2].reshape(1, DFF // 4, D)
    two = lambda a: a.reshape(a.shape[1], a.shape[2])
    hosted_names = ["w_ada", "w_out", "w_ffn_in"]
    hosted_g = dict(w_ada=g_w_ada_s, w_out=two(g_w_out_s), w_ffn_in=two(g_w_ffn_in_s))
    hosted_w = dict(w_ada=(w_ada, m_w_ada, v_w_ada), w_out=(w_out, m_w_out, v_w_out), w_ffn_in=(w_ffn_in, m_w_ffn_in, v_w_ffn_in))
    hosted_res, (r_w_in,) = _adamw_hosting(
        [(two(hosted_w[k][0]), hosted_g[k], two(hosted_w[k][1]), two(hosted_w[k][2])) for k in hosted_names], [g_w_in8])
    (g_w_in_half,) = _sibling_exchange([_sum_slots(r_w_in, "sum_w_in")], "sibling_w_in")
    g_w_in_t = g_w_in_half.transpose(1, 0, 2).reshape(900, D)

    grads = dict(
        c_ctx=g_c_ctx.reshape(D), norm1_w=g_n1, norm2_w=g_n2, w_ada=g_w_ada_s.reshape(1, D, 1536), b_ada=g_b_ada,
        w_in=g_w_in_t, dn_conv_w=g_conv_s, dn_A_log=g_misc[0:8].reshape(1, 2, 4), dn_dt_bias=g_misc[8:16].reshape(1, 2, 4),
        dn_out_norm_w=g_misc[128:256].reshape(1, 128), na_q_norm_w=g_misc[256:320].reshape(1, 64),
        na_k_norm_w=g_misc[320:384].reshape(1, 64), na_rpb=g_rpb, w_out=g_w_out_s, w_ffn_in=g_w_ffn_in_s,
        w_ffn_out=g_w_ffn_out_s)
    pk = lambda d_: _pack_small(d_["c_ctx"], d_["norm1_w"], d_["norm2_w"], d_["b_ada"], d_["dn_A_log"], d_["dn_dt_bias"],
                                d_["dn_out_norm_w"], d_["na_q_norm_w"], d_["na_k_norm_w"], d_["na_rpb"], d_["dn_conv_w"])
    wts = dict(c_ctx=c_ctx, norm1_w=norm1_w, norm2_w=norm2_w, b_ada=b_ada, dn_A_log=dn_A_log, dn_dt_bias=dn_dt_bias,
               dn_out_norm_w=dn_out_norm_w, na_q_norm_w=na_q_norm_w, na_k_norm_w=na_k_norm_w, na_rpb=na_rpb, dn_conv_w=dn_conv_w)
    ms = dict(c_ctx=m_c_ctx, norm1_w=m_norm1_w, norm2_w=m_norm2_w, b_ada=m_b_ada, dn_A_log=m_dn_A_log, dn_dt_bias=m_dn_dt_bias,
              dn_out_norm_w=m_dn_out_norm_w, na_q_norm_w=m_na_q_norm_w, na_k_norm_w=m_na_k_norm_w, na_rpb=m_na_rpb,
              dn_conv_w=m_dn_conv_w)
    vs = dict(c_ctx=v_c_ctx, norm1_w=v_norm1_w, norm2_w=v_norm2_w, b_ada=v_b_ada, dn_A_log=v_dn_A_log, dn_dt_bias=v_dn_dt_bias,
              dn_out_norm_w=v_dn_out_norm_w, na_q_norm_w=v_na_q_norm_w, na_k_norm_w=v_na_k_norm_w, na_rpb=v_na_rpb,
              dn_conv_w=v_dn_conv_w)
    sd, snm, snv, _ = _adamw(pk(wts), pk(grads), pk(ms), pk(vs), "adamw_small")
    delta, new_m, new_v = _unpack_small(sd), _unpack_small(snm), _unpack_small(snv)
    for k, res in zip(hosted_names, hosted_res):
        shp = hosted_w[k][0].shape
        delta[k], new_m[k], new_v[k], grads[k] = [a.reshape(shp) for a in res]
    big = dict(w_in=(w_in, m_w_in, v_w_in), w_ffn_out=(w_ffn_out, m_w_ffn_out, v_w_ffn_out))
    for nm, (w_, m_, v_) in big.items():
        shp = w_.shape
        if nm == "w_in":
            r2, back = (lambda a: a[0].T), (lambda a: a.T.reshape(shp))
            res = _adamw(w_in_t, grads[nm], r2(m_), r2(v_), "adamw_" + nm)
        else:
            r2, back = (lambda a: a.reshape(shp[1], shp[2])), (lambda a: a.reshape(shp))
            res = _adamw(r2(w_), r2(grads[nm]), r2(m_), r2(v_), "adamw_" + nm)
        delta[nm], new_m[nm], new_v[nm], grads[nm] = [back(a) for a in res]

    return (loss, grad_x, *[grads[n] for n in _WEIGHTS], *[delta[n] for n in _WEIGHTS],
            *[new_m[n] for n in _WEIGHTS], *[new_v[n] for n in _WEIGHTS])
```

```python
import functools

import numpy as np
import jax
import jax.numpy as jnp
from jax import lax
from jax.experimental import pallas as pl
from jax.experimental.pallas import tpu as pltpu

F32 = jnp.float32
BF16 = jnp.bfloat16

D = 1024
NB = 2
SEQ = 2048
CT = 256
T = CT + SEQ
NTOK = NB * SEQ
CH = 64
NCH = T // CH
NCH_CTX = CT // CH
DNH, DNW = 4, 512
NAH, NAW = 8, 512
GRID_W = 64
ROWS = SEQ // GRID_W
WIN_ROWS = 8
WIN_COLS = 16
NWIN = WIN_ROWS * GRID_W
DFF = 2816
FFT = 1408
IN_COLS = 3600
PC = 3712
COL_Z, COL_NA, COL_G = 1536, 2048, 3584
EPS = 1e-6
NEG = -1e30
RT = 256
VMEM_BIG = 56 * 1024 * 1024
MESH = pl.DeviceIdType.MESH

ADAM_LR, ADAM_B1, ADAM_B2, ADAM_EPS, ADAM_WD, ADAM_STEP = 0.001, 0.9, 0.999, 1e-08, 0.01, 10


def _cp(vmem=None, sem=None):
    kw = {}
    if vmem is not None:
        kw["vmem_limit_bytes"] = vmem
    if sem is not None:
        kw["dimension_semantics"] = sem
    return pltpu.CompilerParams(**kw)


_NN = (((1,), (0,)), ((), ()))
_NT = (((1,), (1,)), ((), ()))
_TN = (((0,), (0,)), ((), ()))


def _dg(a, b, dims):
    return lax.dot_general(a.astype(BF16), b.astype(BF16), dims, preferred_element_type=F32)


@jax.custom_vjp
def _mm(a, b):
    return _dg(a, b, _NN)


_mm.defvjp(lambda a, b: (_dg(a, b, _NN), (a, b)),
           lambda r, g: (_dg(g, r[1], _NT), _dg(r[0], g, _TN)))


def _split_dot(a, b, dims):
    hi = b.astype(BF16)
    lo = (b - hi.astype(F32)).astype(BF16)
    ab = a.astype(BF16)
    return (lax.dot_general(ab, hi, dims, preferred_element_type=F32)
            + lax.dot_general(ab, lo, dims, preferred_element_type=F32))


@jax.custom_vjp
def _mask_mm(mask, b):
    return _split_dot(mask, b, _NN)


_mask_mm.defvjp(lambda mask, b: (_split_dot(mask, b, _NN), mask),
                lambda mask, g: (jnp.zeros_like(mask), _split_dot(mask, g, _TN)))


def _hmm(a, b):
    return jnp.dot(a, b, precision=lax.Precision.HIGHEST, preferred_element_type=F32)


def _sigmoid(x):
    return 1.0 / (1.0 + jnp.exp(-x))


def _silu(x):
    return x * _sigmoid(x)


def _norm_mod(x, w, sh, sc):
    r = lax.rsqrt(jnp.mean(x * x, axis=-1, keepdims=True) + EPS)
    return (x * r * w) * (1.0 + sc) + sh


def _my_pos():
    return lax.axis_index("x"), lax.axis_index("y"), lax.axis_index("c")


def _flip(pos, k):
    x, y, c = pos
    return (1 - x if k & 4 else x, 1 - y if k & 2 else y, 1 - c if k & 1 else c)


def _slot(pos):
    return 4 * pos[0] + 2 * pos[1] + pos[2]


def _allgather8(v, name, with_sum=False):
    R, W = v.shape

    def body(v_ref, out_ref, *rest):
        if with_sum:
            tot_ref, send_sems, recv_sems, lsem = rest
        else:
            send_sems, recv_sems, lsem = rest
        me = _my_pos()
        mine = pltpu.make_async_copy(v_ref, out_ref.at[_slot(me)], lsem)
        mine.start()
        sends = []
        for k in range(1, 8):
            cp = pltpu.make_async_remote_copy(
                src_ref=v_ref, dst_ref=out_ref.at[_slot(me)], send_sem=send_sems.at[k - 1],
                recv_sem=recv_sems.at[k - 1], device_id=_flip(me, k), device_id_type=MESH)
            cp.start()
            sends.append(cp)
        for k in range(1, 8):
            peer = _flip(me, k)
            pltpu.make_async_remote_copy(
                src_ref=v_ref, dst_ref=out_ref.at[_slot(peer)], send_sem=send_sems.at[k - 1],
                recv_sem=recv_sems.at[k - 1], device_id=peer, device_id_type=MESH).wait_recv()
        for cp in sends:
            cp.wait_send()
        mine.wait()
        if with_sum:
            acc = out_ref[0]
            for s in range(1, 8):
                acc = acc + out_ref[s]
            tot_ref[...] = acc

    out_shape = [jax.ShapeDtypeStruct((8, R, W), F32)]
    out_specs = [pl.BlockSpec(memory_space=pltpu.VMEM)]
    if with_sum:
        out_shape.append(jax.ShapeDtypeStruct((R, W), F32))
        out_specs.append(pl.BlockSpec(memory_space=pltpu.VMEM))
    res = pl.pallas_call(
        body, name=name, out_shape=out_shape,
        in_specs=[pl.BlockSpec(memory_space=pltpu.VMEM)], out_specs=out_specs,
        scratch_shapes=[pltpu.SemaphoreType.DMA((7,)), pltpu.SemaphoreType.DMA((7,)), pltpu.SemaphoreType.DMA],
        compiler_params=_cp(vmem=VMEM_BIG),
    )(v)
    return res if with_sum else res[0]


def _gather_rows(v, name):
    W = v.shape[1]

    def body(v_ref, out_ref, send_sems, recv_sems, lsem):
        pos = _my_pos()
        me = _slot(pos)
        rows = lambda p: v_ref.at[pl.ds(pl.multiple_of(8 * p, 8), 8)]
        locals_ = [(rows(me), out_ref.at[me])]
        xfers = [(rows(_slot(_flip(pos, k))), out_ref.at[me], out_ref.at[_slot(_flip(pos, k))], _flip(pos, k))
                 for k in range(1, 8)]
        _xfer_start((locals_, xfers), (send_sems, recv_sems, lsem))
        _xfer_wait((locals_, xfers), (send_sems, recv_sems, lsem))

    vm = pl.BlockSpec(memory_space=pltpu.VMEM)
    return pl.pallas_call(
        body, name=name, out_shape=jax.ShapeDtypeStruct((8, 8, W), F32), in_specs=[vm], out_specs=vm,
        scratch_shapes=_xfer_sems(1, 7),
    )(v)


def _gather_weights(halves, small):
    n = len(halves)

    def body(*refs):
        ins, s_in, outs, s_out = refs[:n], refs[n], refs[n + 1:2 * n + 1], refs[2 * n + 1]
        send_sems, recv_sems, lsems = refs[2 * n + 2:2 * n + 5]
        s_sems = refs[2 * n + 5:]
        s_plan = _gather_plan([s_in], [s_out], _my_pos())
        _xfer_start(s_plan, s_sems)
        me = _my_pos()
        x, y, c = me
        sib = (x, y, 1 - c)
        chips = [4, 2, 6]

        def cp(a, k, block, to, src=None):
            dst = outs[a].at[_slot(block)]
            return pltpu.make_async_remote_copy(
                src_ref=dst if src is None else src, dst_ref=dst, send_sem=send_sems.at[a, k],
                recv_sem=recv_sems.at[a, k], device_id=to, device_id_type=MESH)

        mine, first, passed = [], [], []
        for a in range(n):
            m = pltpu.make_async_copy(ins[a], outs[a].at[_slot(me)], lsems.at[a])
            m.start()
            mine.append(m)
            first.append(cp(a, 0, me, sib, src=ins[a]))
            for j, k in enumerate(chips):
                first.append(cp(a, 1 + j, me, _flip(me, k), src=ins[a]))
        for f in first:
            f.start()
        for j, k in enumerate(chips):
            for a in range(n):
                cp(a, 1 + j, _flip(me, k), me).wait_recv()
                p = cp(a, 4 + j, _flip(me, k), sib)
                p.start()
                passed.append(p)
        for a in range(n):
            cp(a, 0, sib, me).wait_recv()
            for j, k in enumerate(chips):
                cp(a, 4 + j, _flip(sib, k), me).wait_recv()
        for f in first + passed:
            f.wait_send()
        for m in mine:
            m.wait()
        _xfer_wait(s_plan, s_sems)

    vm = pl.BlockSpec(memory_space=pltpu.VMEM)
    return pl.pallas_call(
        body, name="gather_weights",
        out_shape=[jax.ShapeDtypeStruct((8,) + h.shape, h.dtype) for h in list(halves) + [small]],
        in_specs=[vm] * (n + 1), out_specs=[vm] * (n + 1),
        scratch_shapes=[pltpu.SemaphoreType.DMA((n, 7)), pltpu.SemaphoreType.DMA((n, 7)),
                        pltpu.SemaphoreType.DMA((n,))] + _xfer_sems(1, 7),
        compiler_params=_cp(vmem=VMEM_BIG),
    )(*halves, small)


def _xfer_start(plan, sems):
    send_sems, recv_sems, lsems = sems
    locals_, xfers = plan
    for i, (src, dst) in enumerate(locals_):
        pltpu.make_async_copy(src, dst, lsems.at[i]).start()
    for i, (src, dst, _, peer) in enumerate(xfers):
        pltpu.make_async_remote_copy(src_ref=src, dst_ref=dst, send_sem=send_sems.at[i], recv_sem=recv_sems.at[i],
                                     device_id=peer, device_id_type=MESH).start()


def _xfer_wait(plan, sems):
    send_sems, recv_sems, lsems = sems
    locals_, xfers = plan
    for i, (src, _, landing, peer) in enumerate(xfers):
        pltpu.make_async_remote_copy(src_ref=src, dst_ref=landing, send_sem=send_sems.at[i], recv_sem=recv_sems.at[i],
                                     device_id=peer, device_id_type=MESH).wait_recv()
    for i, (src, dst, _, peer) in enumerate(xfers):
        pltpu.make_async_remote_copy(src_ref=src, dst_ref=dst, send_sem=send_sems.at[i], recv_sem=recv_sems.at[i],
                                     device_id=peer, device_id_type=MESH).wait_send()
    for i, (src, dst) in enumerate(locals_):
        pltpu.make_async_copy(src, dst, lsems.at[i]).wait()


def _xfer_sems(n_local, n_xfer):
    return [pltpu.SemaphoreType.DMA((n_xfer,)), pltpu.SemaphoreType.DMA((n_xfer,)), pltpu.SemaphoreType.DMA((n_local,))]


def _scatter_plan(ins, outs, pos):
    me = _slot(pos)
    locals_, xfers = [], []
    for src, dst in zip(ins, outs):
        locals_.append((src.at[me], dst.at[me]))
        for k in range(1, 8):
            peer = _flip(pos, k)
            xfers.append((src.at[_slot(peer)], dst.at[me], dst.at[_slot(peer)], peer))
    return locals_, xfers


def _sibling_exchange(halves, name):
    n = len(halves)

    def body(*refs):
        ins, outs = refs[:n], refs[n:2 * n]
        send_sems, recv_sems, lsems = refs[2 * n:]
        x, y, c = _my_pos()
        sib = (x, y, 1 - c)
        mine, sends = [], []
        for a in range(n):
            m = pltpu.make_async_copy(ins[a], outs[a].at[c], lsems.at[a])
            m.start()
            mine.append(m)
            cp = pltpu.make_async_remote_copy(
                src_ref=ins[a], dst_ref=outs[a].at[c], send_sem=send_sems.at[a], recv_sem=recv_sems.at[a],
                device_id=sib, device_id_type=MESH)
            cp.start()
            sends.append(cp)
        for a in range(n):
            pltpu.make_async_remote_copy(
                src_ref=ins[a], dst_ref=outs[a].at[1 - c], send_sem=send_sems.at[a], recv_sem=recv_sems.at[a],
                device_id=sib, device_id_type=MESH).wait_recv()
        for cp in sends:
            cp.wait_send()
        for m in mine:
            m.wait()

    vm = pl.BlockSpec(memory_space=pltpu.VMEM)
    return pl.pallas_call(
        body, name=name,
        out_shape=[jax.ShapeDtypeStruct((2,) + h.shape, h.dtype) for h in halves],
        in_specs=[vm] * n, out_specs=[vm] * n,
        scratch_shapes=[pltpu.SemaphoreType.DMA((n,)), pltpu.SemaphoreType.DMA((n,)), pltpu.SemaphoreType.DMA((n,))],
        compiler_params=_cp(vmem=VMEM_BIG),
    )(*halves)


def _row_tile(r, cap=512):
    for t in (cap, 256, 128, 64, 32, 16, 8):
        if t <= cap and r % t == 0:
            return t
    return r


def _cast_bf16(a, name):
    R, W = a.shape
    tr = _row_tile(R, 256)

    def body(a_ref, o_ref):
        o_ref[...] = a_ref[...].astype(BF16)

    return pl.pallas_call(
        body, name=name, grid=(R // tr,), out_shape=jax.ShapeDtypeStruct((R, W), BF16),
        in_specs=[pl.BlockSpec((tr, W), lambda i: (i, 0))], out_specs=pl.BlockSpec((tr, W), lambda i: (i, 0)),
    )(a)


def _sum_slots(r, name):
    _, R, W = r.shape
    tr = _row_tile(R, 128)

    def body(r_ref, o_ref):
        acc = r_ref[0].astype(F32)
        for s_ in range(1, 8):
            acc = acc + r_ref[s_].astype(F32)
        o_ref[...] = acc.astype(BF16)

    return pl.pallas_call(
        body, name=name, grid=(R // tr,), out_shape=jax.ShapeDtypeStruct((R, W), BF16),
        in_specs=[pl.BlockSpec((8, tr, W), lambda i: (0, i, 0))], out_specs=pl.BlockSpec((tr, W), lambda i: (i, 0)),
        compiler_params=_cp(vmem=VMEM_BIG),
    )(r)


def _adamw(w, g, m, v, name):
    R, W = w.shape
    tr = _row_tile(R, 256)
    tc = 256 if (tr == R and R * W * 4 > (1 << 21) and W % 256 == 0) else W
    c1 = 1.0 / (1.0 - ADAM_B1 ** ADAM_STEP)
    c2 = 1.0 / (1.0 - ADAM_B2 ** ADAM_STEP)

    def body(w_ref, g_ref, m_ref, v_ref, d_ref, nm_ref, nv_ref, g32_ref):
        gg = g_ref[...].astype(F32)
        g32_ref[...] = gg
        nm = ADAM_B1 * m_ref[...] + (1.0 - ADAM_B1) * gg
        nv = ADAM_B2 * v_ref[...] + (1.0 - ADAM_B2) * (gg * gg)
        d_ref[...] = -ADAM_LR * ((nm * c1) / (jnp.sqrt(nv * c2) + ADAM_EPS) + ADAM_WD * w_ref[...])
        nm_ref[...] = nm
        nv_ref[...] = nv

    spec = pl.BlockSpec((tr, tc), lambda i, j: (i, j))
    return pl.pallas_call(
        body, name=name, grid=(R // tr, W // tc), out_shape=[jax.ShapeDtypeStruct((R, W), F32)] * 4,
        in_specs=[spec] * 4, out_specs=[spec] * 4, compiler_params=_cp(vmem=VMEM_BIG),
    )(w, g, m, v)


def _adamw_hosting(items, carry, ride):
    n, ns, nc, steps = len(items), len(carry), len(carry) + len(ride), 8
    c1 = 1.0 / (1.0 - ADAM_B1 ** ADAM_STEP)
    c2 = 1.0 / (1.0 - ADAM_B2 ** ADAM_STEP)

    def body(*refs):
        ins, c_in = refs[:4 * n], refs[4 * n:4 * n + nc]
        outs, c_out = refs[4 * n + nc:8 * n + nc], refs[8 * n + nc:8 * n + 2 * nc]
        sems = refs[8 * n + 2 * nc:]

        def plan():
            pos = _my_pos()
            l1, x1 = _scatter_plan(c_in[:ns], c_out[:ns], pos)
            l2, x2 = _gather_plan(c_in[ns:], c_out[ns:], pos)
            return l1 + l2, x1 + x2

        if nc:
            @pl.when(pl.program_id(0) == 0)
            def _():
                _xfer_start(plan(), sems)

        for a in range(n):
            w_ref, g_ref, m_ref, v_ref = ins[4 * a:4 * a + 4]
            d_ref, nm_ref, nv_ref, g32_ref = outs[4 * a:4 * a + 4]
            gg = g_ref[...].astype(F32)
            g32_ref[...] = gg
            nm = ADAM_B1 * m_ref[...] + (1.0 - ADAM_B1) * gg
            nv = ADAM_B2 * v_ref[...] + (1.0 - ADAM_B2) * (gg * gg)
            d_ref[...] = -ADAM_LR * ((nm * c1) / (jnp.sqrt(nv * c2) + ADAM_EPS) + ADAM_WD * w_ref[...])
            nm_ref[...] = nm
            nv_ref[...] = nv

        if nc:
            @pl.when(pl.program_id(0) == steps - 1)
            def _():
                _xfer_wait(plan(), sems)

    specs = [pl.BlockSpec((w.shape[0] // steps, w.shape[1]), lambda i: (i, 0)) for w, _, _, _ in items]
    hbm = pl.BlockSpec(memory_space=pltpu.HBM)
    res = pl.pallas_call(
        body, name="adamw_hosting", grid=(steps,),
        out_shape=[jax.ShapeDtypeStruct(w.shape, F32) for w, _, _, _ in items for _ in range(4)]
        + [jax.ShapeDtypeStruct(g.shape, g.dtype) for g in carry]
        + [jax.ShapeDtypeStruct((8,) + r.shape, r.dtype) for r in ride],
        in_specs=[sp for sp in specs for _ in range(4)] + [hbm] * nc,
        out_specs=[sp for sp in specs for _ in range(4)] + [hbm] * nc,
        scratch_shapes=_xfer_sems(nc, 7 * nc) if nc else [],
        compiler_params=_cp(vmem=VMEM_BIG),
    )(*[a for it in items for a in it], *carry, *ride)
    return [res[4 * a:4 * a + 4] for a in range(n)], list(res[4 * n:])


def _matmul_tn(a, b, tm, tn, tk, name, nn=1, b_map=None, out_shape=None, out_block=None, out_map=None,
               b_block=None):
    K, M = a.shape
    nk = K // tk
    nm = M // tm

    def body(a_ref, b_ref, o_ref, acc_ref):
        kk = pl.program_id(2)

        @pl.when(kk == 0)
        def _():
            acc_ref[...] = jnp.zeros_like(acc_ref)

        acc_ref[...] += lax.dot_general(a_ref[...], b_ref[...], _TN, preferred_element_type=F32)

        @pl.when(kk == nk - 1)
        def _():
            o_ref[...] = acc_ref[...].astype(BF16)

    return pl.pallas_call(
        body, name=name, grid=(nm, nn, nk),
        out_shape=jax.ShapeDtypeStruct(out_shape or (M, b.shape[-1]), BF16),
        in_specs=[pl.BlockSpec((tk, tm), lambda i, j, k: (k, i)),
                  pl.BlockSpec(b_block or (tk, tn), b_map or (lambda i, j, k: (k, j)))],
        out_specs=pl.BlockSpec(out_block or (tm, tn), out_map or (lambda i, j, k: (i, j))),
        scratch_shapes=[pltpu.VMEM((tm, tn), F32)],
        compiler_params=_cp(vmem=VMEM_BIG),
    )(a, b)


def _ada_fwd(cin, w_ada_s, b_s):
    tn = 512

    def body(c_ref, w_ref, b_ref, o_ref):
        o_ref[...] = _dg(_silu(c_ref[...]), w_ref[...], _NN) + b_ref[...]

    return pl.pallas_call(
        body, name="ada_fwd", grid=(1536 // tn,), out_shape=jax.ShapeDtypeStruct((64, 1536), F32),
        in_specs=[pl.BlockSpec((64, D), lambda j: (0, 0)), pl.BlockSpec((D, tn), lambda j: (0, j)),
                  pl.BlockSpec((1, tn), lambda j: (0, j))],
        out_specs=pl.BlockSpec((64, tn), lambda j: (0, j)),
        compiler_params=_cp(vmem=VMEM_BIG),
    )(cin, w_ada_s, b_s)


def _ada_bwd(cin, gmod_s, w_ada_s):
    tn = 512

    def body(c_ref, g_ref, w_ref, gw_ref, pc_ref):
        j = pl.program_id(0)
        g = g_ref[...]
        gw_ref[...] = _dg(_silu(c_ref[...]), g, _TN)
        rr = lax.broadcasted_iota(jnp.int32, (8, 64), 0)
        cc = lax.broadcasted_iota(jnp.int32, (8, 64), 1)
        sel = jnp.where((rr == 0) & ((cc & 7) == 2), 1.0, 0.0).astype(F32)
        dctx = _hmm(sel, g)
        part = _dg(dctx, w_ref[...], _NT)

        @pl.when(j == 0)
        def _():
            pc_ref[...] = jnp.zeros_like(pc_ref)

        pc_ref[...] += part

    return pl.pallas_call(
        body, name="ada_bwd", grid=(1536 // tn,),
        out_shape=[jax.ShapeDtypeStruct((D, 1536), F32), jax.ShapeDtypeStruct((8, D), F32)],
        in_specs=[pl.BlockSpec((64, D), lambda j: (0, 0)), pl.BlockSpec((64, tn), lambda j: (0, j)),
                  pl.BlockSpec((D, tn), lambda j: (0, j))],
        out_specs=[pl.BlockSpec((D, tn), lambda j: (0, j)), pl.BlockSpec((8, D), lambda j: (0, 0))],
        compiler_params=_cp(vmem=VMEM_BIG),
    )(cin, gmod_s, w_ada_s)


def _ada_small(gmod, parts, c_ctx):
    def body(g_ref, p_ref, c_ref, gb_ref, gc_ref):
        gb_ref[...] = jnp.sum(g_ref[...], axis=0, keepdims=True)
        ds = p_ref[0][0:1] + p_ref[2][0:1] + p_ref[4][0:1] + p_ref[6][0:1]
        cc = c_ref[...]
        s = _sigmoid(cc)
        gc_ref[...] = ds * (s * (1.0 + cc * (1.0 - s)))

    vm = pl.BlockSpec(memory_space=pltpu.VMEM)
    return pl.pallas_call(
        body, name="ada_small",
        out_shape=[jax.ShapeDtypeStruct((1, 6 * D), F32), jax.ShapeDtypeStruct((1, D), F32)],
        in_specs=[vm, vm, vm], out_specs=[vm, vm],
    )(gmod, parts, c_ctx)


def _tok_specs():
    xs = pl.BlockSpec((None, RT, D), lambda b, j: (b, jnp.maximum(j - 1, 0), 0))
    cs = pl.BlockSpec((None, RT, D), lambda b, j: (b, 0, 0))
    ms = pl.BlockSpec((None, 1, D), lambda b, j: (b, 0, 0))
    cms = pl.BlockSpec((None, 1, D), lambda b, j: (0, 0, 0))
    return xs, cs, ms, cms


def _inproj_fwd(x, ctx, sh1, sc1, csh1, csc1, n1w, w_in_p):
    def body(x_ref, c_ref, sh_ref, sc_ref, csh_ref, csc_ref, w1_ref, w_ref, p_ref, hm_ref):
        is_ctx = pl.program_id(1) == 0
        xt = jnp.where(is_ctx, c_ref[...], x_ref[...])
        sh = jnp.where(is_ctx, csh_ref[...], sh_ref[...])
        sc = jnp.where(is_ctx, csc_ref[...], sc_ref[...])
        hb = _norm_mod(xt, w1_ref[...], sh, sc).astype(BF16)
        hm_ref[...] = hb
        p_ref[...] = lax.dot_general(hb, w_ref[...], _NT, preferred_element_type=F32)

    xs, cs, ms, cms = _tok_specs()
    return pl.pallas_call(
        body, name="inproj_fwd", grid=(NB, T // RT),
        out_shape=[jax.ShapeDtypeStruct((NB, T, PC), F32), jax.ShapeDtypeStruct((NB, T, D), BF16)],
        in_specs=[xs, cs, ms, ms, cms, cms, pl.BlockSpec((1, D), lambda b, j: (0, 0)),
                  pl.BlockSpec((PC, D), lambda b, j: (0, 0))],
        out_specs=[pl.BlockSpec((None, RT, PC), lambda b, j: (b, j, 0)),
                   pl.BlockSpec((None, RT, D), lambda b, j: (b, j, 0))],
        compiler_params=_cp(vmem=VMEM_BIG),
    )(x, ctx, sh1, sc1, csh1, csc1, n1w, w_in_p)


def _inproj_bwd(dp_dn, dz, dp_na, dgt, w_in_p, x, ctx, sh1, sc1, csh1, csc1, n1w, dx1):
    def body(d1q_ref, d1k_ref, d1v_ref, d2_ref, d3_ref, d4_ref, w_ref, x_ref, c_ref, sh_ref, sc_ref, csh_ref, csc_ref,
             w1_ref, dx1_ref, dpb_ref, gx_ref, dsh_ref, dsc_ref, dcsh_ref, dcsc_ref, dw1_ref):
        b, j = pl.program_id(0), pl.program_id(1)
        is_ctx = j == 0
        dp = jnp.concatenate([d1q_ref[...], d1k_ref[...], d1v_ref[...], d2_ref[...], d3_ref[...], d4_ref[...]],
                             axis=1).astype(BF16)
        dpb_ref[...] = dp
        dhm = jnp.dot(dp, w_ref[...], preferred_element_type=F32)
        xt = jnp.where(is_ctx, c_ref[...], x_ref[...])
        sh = jnp.where(is_ctx, csh_ref[...], sh_ref[...])
        sc = jnp.where(is_ctx, csc_ref[...], sc_ref[...])
        _, vjp = jax.vjp(_norm_mod, xt, w1_ref[...], sh, sc)
        dxt, dw1, dsh, dsc = vjp(dhm)

        @pl.when((b == 0) & (j == 0))
        def _():
            dw1_ref[...] = jnp.zeros_like(dw1_ref)
            dcsh_ref[...] = jnp.zeros_like(dcsh_ref)
            dcsc_ref[...] = jnp.zeros_like(dcsc_ref)

        dw1_ref[...] += dw1

        @pl.when(is_ctx)
        def _():
            dcsh_ref[...] += dsh
            dcsc_ref[...] += dsc

        @pl.when(j == 1)
        def _():
            dsh_ref[...] = jnp.zeros_like(dsh_ref)
            dsc_ref[...] = jnp.zeros_like(dsc_ref)

        @pl.when(j >= 1)
        def _():
            dsh_ref[...] += dsh
            dsc_ref[...] += dsc
            gx_ref[...] = dx1_ref[...] + dxt

    xs, cs, ms, cms = _tok_specs()
    tsp = lambda w: pl.BlockSpec((None, RT, w), lambda b, j: (b, j, 0))
    one = pl.BlockSpec((1, D), lambda b, j: (0, 0))
    return pl.pallas_call(
        body, name="inproj_bwd", grid=(NB, T // RT),
        out_shape=[jax.ShapeDtypeStruct((NB, T, PC), BF16), jax.ShapeDtypeStruct((NB, SEQ, D), F32),
                   jax.ShapeDtypeStruct((NB, 1, D), F32), jax.ShapeDtypeStruct((NB, 1, D), F32),
                   jax.ShapeDtypeStruct((1, 1, D), F32), jax.ShapeDtypeStruct((1, 1, D), F32),
                   jax.ShapeDtypeStruct((1, D), F32)],
        in_specs=[tsp(512), tsp(512), tsp(512), tsp(512), tsp(1536), tsp(128), pl.BlockSpec((PC, D), lambda b, j: (0, 0)),
                  xs, cs, ms, ms, cms, cms, one, _rows2d_spec(D)],
        out_specs=[tsp(PC), xs, ms, ms, cms, cms, one],
        compiler_params=_cp(vmem=VMEM_BIG),
    )(*dp_dn, dz, dp_na, dgt, w_in_p, x, ctx, sh1, sc1, csh1, csc1, n1w, dx1)


def _shift_masks():
    t = lax.broadcasted_iota(jnp.int32, (T, 1), 0)
    lo = jnp.where(t < CT, 0, CT)
    hi = jnp.where(t < CT, CT, T)
    return {off: (t + off >= lo) & (t + off < hi) for off in range(-2, 3)}


def _shifted(u, off, masks):
    s = u if off == 0 else pltpu.roll(u, (-off) % T, 0)
    return jnp.where(masks[off], s, 0.0)


def _conv_pre(u, w, masks):
    acc = _shifted(u, -2, masks) * w[0:1, :]
    for kk in range(1, 5):
        acc = acc + _shifted(u, kk - 2, masks) * w[kk:kk + 1, :]
    return acc


def _act_q(pre):
    a = _silu(pre)
    return a * lax.rsqrt(jnp.sum(a * a, axis=-1, keepdims=True) + EPS) * (128.0 ** -0.5)


def _act_k(pre):
    a = _silu(pre)
    return a * lax.rsqrt(jnp.sum(a * a, axis=-1, keepdims=True) + EPS)


def _dn_prep_fwd(p, cw):
    def body(pq_ref, pk_ref, pv_ref, wq_ref, wk_ref, wv_ref, q_ref, k_ref, v_ref):
        masks = _shift_masks()
        q_ref[...] = _act_q(_conv_pre(pq_ref[...], wq_ref[...], masks))
        k_ref[...] = _act_k(_conv_pre(pk_ref[...], wk_ref[...], masks))
        v_ref[...] = _silu(_conv_pre(pv_ref[...], wv_ref[...], masks))

    ps = lambda o: pl.BlockSpec((None, T, 128), lambda b, h: (b, 0, o + h))
    ws = lambda o: pl.BlockSpec((8, 128), lambda b, h: (0, o + h))
    os_ = pl.BlockSpec((None, T, 128), lambda b, h: (b, 0, h))
    return pl.pallas_call(
        body, name="dn_prep_fwd", grid=(NB, DNH), out_shape=[jax.ShapeDtypeStruct((NB, T, DNW), F32)] * 3,
        in_specs=[ps(0), ps(4), ps(8), ws(0), ws(4), ws(8)], out_specs=[os_] * 3,
        compiler_params=_cp(vmem=VMEM_BIG),
    )(p, p, p, cw, cw, cw)


def _dn_prep_bwd(p, cw, dqs, dks, dvs):
    def body(pq_ref, pk_ref, pv_ref, wq_ref, wk_ref, wv_ref, dqf_ref, dqb_ref, dkf_ref, dkb_ref, dvf_ref, dvb_ref,
             oq_ref, ok_ref, ov_ref, gq_ref, gk_ref, gv_ref):
        b = pl.program_id(1)
        masks = _shift_masks()

        def one(p_ref, w_ref, df_ref, db_ref, act, o_ref, g_ref):
            u = p_ref[...]
            w = w_ref[...]
            _, vjp = jax.vjp(act, _conv_pre(u, w, masks))
            (dpre,) = vjp(df_ref[...] + db_ref[...])
            du = _shifted(dpre, 2, masks) * w[0:1, :]
            rows = [jnp.sum(dpre * _shifted(u, -2, masks), axis=0, keepdims=True)]
            for kk in range(1, 5):
                du = du + _shifted(dpre, 2 - kk, masks) * w[kk:kk + 1, :]
                rows.append(jnp.sum(dpre * _shifted(u, kk - 2, masks), axis=0, keepdims=True))
            o_ref[...] = du
            gw = jnp.concatenate(rows + [jnp.zeros((3, 128), F32)], axis=0)

            @pl.when(b == 0)
            def _():
                g_ref[...] = gw

            @pl.when(b > 0)
            def _():
                g_ref[...] += gw

        one(pq_ref, wq_ref, dqf_ref, dqb_ref, _act_q, oq_ref, gq_ref)
        one(pk_ref, wk_ref, dkf_ref, dkb_ref, _act_k, ok_ref, gk_ref)
        one(pv_ref, wv_ref, dvf_ref, dvb_ref, _silu, ov_ref, gv_ref)

    ps = lambda o: pl.BlockSpec((None, T, 128), lambda h, b: (b, 0, o + h))
    ws = lambda o: pl.BlockSpec((8, 128), lambda h, b: (0, o + h))
    ds = pl.BlockSpec((None, T, 128), lambda h, b: (b, 0, h))
    outs = pl.pallas_call(
        body, name="dn_prep_bwd", grid=(DNH, NB),
        out_shape=[jax.ShapeDtypeStruct((NB, T, DNW), F32)] * 3 + [jax.ShapeDtypeStruct((8, DNW), F32)] * 3,
        in_specs=[ps(0), ps(4), ps(8), ws(0), ws(4), ws(8)] + [ds] * 6,
        out_specs=[ds] * 3
        + [pl.BlockSpec((8, 128), lambda h, b: (0, h))] * 3,
        compiler_params=_cp(vmem=VMEM_BIG),
    )(p, p, p, cw, cw, cw, *dqs, *dks, *dvs)
    return outs


def _gate_fn(a, prm):
    lane = lax.broadcasted_iota(jnp.int32, a.shape, 1)
    z = a + prm[1:2, :]
    sp = jnp.maximum(z, 0.0) + jnp.log(1.0 + jnp.exp(-jnp.abs(z)))
    g = -jnp.exp(prm[0:1, :]) * sp
    return jnp.where(lane < 8, g, jnp.where(lane < 16, _sigmoid(a), 0.0))


def _gates_fwd(p, prm):
    def body(a_ref, prm_ref, o_ref):
        o_ref[...] = _gate_fn(a_ref[...], prm_ref[...])

    return pl.pallas_call(
        body, name="gates_fwd", grid=(NB,), out_shape=jax.ShapeDtypeStruct((NB, T, 128), F32),
        in_specs=[pl.BlockSpec((None, T, 128), lambda b: (b, 0, COL_G // 128)), pl.BlockSpec((8, 128), lambda b: (0, 0))],
        out_specs=pl.BlockSpec((None, T, 128), lambda b: (b, 0, 0)),
    )(p, prm)


def _gates_bwd(p, prm, dgf, dgb):
    def body(a_ref, prm_ref, dgf_ref, dgb_ref, o_ref, dprm_ref):
        b = pl.program_id(0)
        tot = dgf_ref[...] + dgb_ref[...]
        _, vjp = jax.vjp(_gate_fn, a_ref[...], prm_ref[...])
        da, dprm = vjp(tot)
        o_ref[...] = da

        @pl.when(b == 0)
        def _():
            dprm_ref[...] = dprm

        @pl.when(b > 0)
        def _():
            dprm_ref[...] += dprm

    return pl.pallas_call(
        body, name="gates_bwd", grid=(NB,),
        out_shape=[jax.ShapeDtypeStruct((NB, T, 128), F32), jax.ShapeDtypeStruct((8, 128), F32)],
        in_specs=[pl.BlockSpec((None, T, 128), lambda b: (b, 0, COL_G // 128)), pl.BlockSpec((8, 128), lambda b: (0, 0)),
                  pl.BlockSpec((None, T, 128), lambda b: (b, 0, 0)), pl.BlockSpec((None, T, 128), lambda b: (b, 0, 0))],
        out_specs=[pl.BlockSpec((None, T, 128), lambda b: (b, 0, 0)), pl.BlockSpec((8, 128), lambda b: (0, 0))],
        compiler_params=_cp(vmem=VMEM_BIG),
    )(p, prm, dgf, dgb)


def _lockstep(gens, joint=None):
    n = len(gens)
    res, vals = [None] * n, [None] * n
    while True:
        msgs = []
        for i, g in enumerate(gens):
            try:
                msgs.append((g.send(vals[i]), False))
            except StopIteration as e:
                res[i] = e.value
                msgs.append((None, True))
        if msgs[0][1]:
            return res
        if msgs[0][0] is not None:
            vals = list(joint(tuple(m[0] for m in msgs)))
        else:
            vals = [None] * n


def _tri_inv_gen(m, md, m1, m2, eye):
    n = -(m * md)
    t = eye + n
    pw = n
    for _ in range(3):
        pw = _dg(pw, pw, _NN)
        yield
        t = t + _dg(t, pw, _NN)
        yield
    a = _dg(t, m * m1, _NN)
    yield
    t = t - _dg(a, t, _NN)
    yield
    a = _dg(t, m * m2, _NN)
    yield
    return t - _dg(a, t, _NN)


def _tri_inv_bwd_gen(t, g):
    tt = t.T
    a = _dg(tt, g, _NN)
    yield
    return -_dg(a, tt, _NN)


@jax.custom_vjp
def _tri_inv(ms, md, m1, m2, eye):
    return tuple(_lockstep([_tri_inv_gen(m, md, m1, m2, eye) for m in ms]))


def _tri_inv_f(ms, md, m1, m2, eye):
    ts = _tri_inv(ms, md, m1, m2, eye)
    return ts, (ts, md)


def _tri_inv_b(res, gs):
    ts, md = res
    z = jnp.zeros_like(md)
    return (tuple(_lockstep([_tri_inv_bwd_gen(t, g) for t, g in zip(ts, gs)])), z, z, z, z)


_tri_inv.defvjp(_tri_inv_f, _tri_inv_b)


@jax.custom_vjp
def _tri_inv_saved(ms, ts):
    return tuple(t.astype(F32) for t in ts)


def _tri_inv_saved_f(ms, ts):
    return _tri_inv_saved(ms, ts), ts


def _tri_inv_saved_b(ts, gs):
    dms = tuple(_lockstep([_tri_inv_bwd_gen(t, g) for t, g in zip(ts, gs)]))
    return dms, tuple(jnp.zeros_like(t) for t in ts)


_tri_inv_saved.defvjp(_tri_inv_saved_f, _tri_inv_saved_b)


SR = DNH * CH


def _scan_masks():
    i, j = np.arange(SR)[:, None], np.arange(SR)[None, :]
    same = (i >> 6) == (j >> 6)
    cm = np.zeros((2, 6, SR, SR), np.float32)
    for d in range(2):
        diff = i - j if d == 0 else j - i
        cm[d, 0] = same & (diff >= 0)
        cm[d, 1] = same & (diff > 0)
        cm[d, 2] = (i >> 4) == (j >> 4)
        cm[d, 3] = ((i >> 5) == (j >> 5)) & ((i >> 4) != (j >> 4))
        cm[d, 4] = (i >> 5) != (j >> 5)
        cm[d, 5] = i == j
    t, u = np.arange(CH)[:, None], np.arange(CH)[None, :]
    c64 = np.stack([t >= u, t <= u]).astype(np.float32)
    return jnp.asarray(cm), jnp.asarray(c64)


def _hsl(h):
    return slice(CH * h, CH * h + CH), slice(128 * h, 128 * h + 128)


@jax.custom_vjp
def _heads_mm(x, s):
    return jnp.concatenate([_dg(x[_hsl(h)[0]], s[:, _hsl(h)[1]], _NN) for h in range(DNH)], axis=0)


def _heads_mm_b(res, g):
    x, s = res
    dx = jnp.concatenate([_dg(g[_hsl(h)[0]], s[:, _hsl(h)[1]], _NT) for h in range(DNH)], axis=0)
    ds = jnp.concatenate([_dg(x[_hsl(h)[0]], g[_hsl(h)[0]], _TN) for h in range(DNH)], axis=1)
    return dx, ds


_heads_mm.defvjp(lambda x, s: (_heads_mm(x, s), (x, s)), _heads_mm_b)


@jax.custom_vjp
def _heads_outer(k, v):
    return jnp.concatenate([_dg(k[_hsl(h)[0]], v[_hsl(h)[0]], _TN) for h in range(DNH)], axis=1)


def _heads_outer_b(res, g):
    k, v = res
    dk = jnp.concatenate([_dg(v[_hsl(h)[0]], g[:, _hsl(h)[1]], _NT) for h in range(DNH)], axis=0)
    dv = jnp.concatenate([_dg(k[_hsl(h)[0]], g[:, _hsl(h)[1]], _NN) for h in range(DNH)], axis=0)
    return dk, dv


_heads_outer.defvjp(lambda k, v: (_heads_outer(k, v), (k, v)), _heads_outer_b)


@jax.custom_vjp
def _mm_shared_lhs(t, a, b):
    r = _dg(t, jnp.concatenate([a, b], axis=1), _NN)
    return r[:, :a.shape[1]], r[:, a.shape[1]:]


def _mm_shared_lhs_b(res, gs):
    t, a, b = res
    g = jnp.concatenate(gs, axis=1)
    dab = _dg(t, g, _TN)
    return _dg(g, jnp.concatenate([a, b], axis=1), _NT), dab[:, :a.shape[1]], dab[:, a.shape[1]:]


_mm_shared_lhs.defvjp(lambda t, a, b: (_mm_shared_lhs(t, a, b), (t, a, b)), _mm_shared_lhs_b)


@jax.custom_vjp
def _mm_nt_shared_rhs(a, b, k):
    r = _dg(jnp.concatenate([a, b], axis=0), k, _NT)
    return r[:a.shape[0]], r[a.shape[0]:]


def _mm_nt_shared_rhs_b(res, gs):
    a, b, k = res
    g = jnp.concatenate(gs, axis=0)
    dab = _dg(g, k, _NN)
    return dab[:a.shape[0]], dab[a.shape[0]:], _dg(g, jnp.concatenate([a, b], axis=0), _TN)


_mm_nt_shared_rhs.defvjp(lambda a, b, k: (_mm_nt_shared_rhs(a, b, k), (a, b, k)), _mm_nt_shared_rhs_b)


def _chunk_gen(q, k, v, gb, s, cm, c64, d):
    incl, strict = cm[0], cm[1]
    lane = lax.broadcasted_iota(jnp.int32, (1, 128), 1)
    oh = lambda n: jnp.where(lane == n, 1.0, 0.0).astype(F32)
    col = lambda x, base: jnp.concatenate(
        [jnp.sum(x * oh(base + d * DNH + h), axis=1, keepdims=True) for h in range(DNH)], axis=0)
    beta = col(gb, 8)
    gam = col(_mask_mm(c64, gb), 0)
    yield
    tot = jnp.sum(gb, axis=0, keepdims=True)
    gtots = [jnp.sum(tot * oh(d * DNH + h), axis=1, keepdims=True) for h in range(DNH)]
    gtot = jnp.concatenate([jnp.broadcast_to(t, (CH, 1)) for t in gtots], axis=0)
    eg = jnp.concatenate([jnp.broadcast_to(jnp.exp(t), (1, 128)) for t in gtots], axis=1)
    row0 = jnp.where(lax.broadcasted_iota(jnp.int32, (128, 1), 0) == 0, 1.0, 0.0).astype(F32)
    gam_row = jnp.sum(jnp.broadcast_to(gam, (SR, 128)).T * row0, axis=0, keepdims=True)
    dec = jnp.exp(jnp.where(incl > 0.5, gam - gam_row, NEG))
    e_gam = jnp.exp(gam)
    kb = k * beta
    kk, qk = _mm_nt_shared_rhs(kb, q, k)
    yield
    tm = yield strict * (kk * dec)
    u, w = _mm_shared_lhs(tm, v * beta, kb * e_gam)
    yield
    qk = qk * dec
    ws = _heads_mm(w, s)
    yield
    v_new = u - ws
    qs = _heads_mm(q * e_gam, s)
    yield
    o = qs + _mm(qk, v_new)
    yield
    s_new = s * eg + _heads_outer(k * jnp.exp(gtot - gam), v_new)
    return o, s_new, tm


def _chunks(args, cm_ref, c64_ref, saved=None):
    cm0 = cm_ref[0]
    if saved is None:
        inv = lambda ms: _tri_inv(ms, cm0[2], cm0[3], cm0[4], cm0[5])
    else:
        inv = lambda ms: _tri_inv_saved(ms, saved)
    return _lockstep([_chunk_gen(q, k, v, gb, s, cm_ref[d], c64_ref[d], d) for q, k, v, gb, s, d in args], inv)


def _chunk_rev(s):
    return jnp.where(s < NCH_CTX, NCH_CTX - 1 - s, NCH + NCH_CTX - 1 - s)


def _scan_specs(step):
    cf = lambda s: step(s)
    cb = lambda s: _chunk_rev(step(s))
    tok = lambda c: pl.BlockSpec((NB, CH, DNW), lambda s: (0, c(s), 0))
    gat = lambda c: pl.BlockSpec((NB, CH, 128), lambda s: (0, c(s), 0))
    sta = lambda c: pl.BlockSpec((NB, None, 128, DNW), lambda s: (0, c(s), 0, 0))
    inv = lambda c: pl.BlockSpec((NB, None, SR, SR), lambda s: (0, c(s), 0, 0))
    return (tok(cf), gat(cf), sta(cf), inv(cf)), (tok(cb), gat(cb), sta(cb), inv(cb))


def _const_specs():
    return [pl.BlockSpec((2, 6, SR, SR), lambda s: (0, 0, 0, 0)), pl.BlockSpec((2, CH, CH), lambda s: (0, 0, 0))]


SCAN_CHAINS = [(b, d) for b in range(NB) for d in range(2)]


def _stack(ref, b):
    return jnp.concatenate([ref[b, :, 128 * h:128 * h + 128] for h in range(DNH)], axis=0)


def _unstack_to(ref, b, val):
    for h in range(DNH):
        ref[b, :, 128 * h:128 * h + 128] = val[CH * h:CH * h + CH]


def _gather_plan(ins, outs, pos):
    me = _slot(pos)
    locals_, xfers = [], []
    for src, dst in zip(ins, outs):
        locals_.append((src, dst.at[me]))
        for k in range(1, 8):
            peer = _flip(pos, k)
            xfers.append((src, dst.at[me], dst.at[_slot(peer)], peer))
    return locals_, xfers


def _scan_fwd(q, k, v, gates, carry=()):
    nc = len(carry)

    def body(*refs):
        cm_ref, c64_ref, qf, kf, vf, gf, qb, kb, vb, gb = refs[:10]
        c_in = refs[10:10 + nc]
        of_ref, sf_ref, tf_ref, ob_ref, sb_ref, tb_ref = refs[10 + nc:16 + nc]
        c_out = refs[16 + nc:16 + 2 * nc]
        s_ref = refs[16 + 2 * nc]
        sems = refs[17 + 2 * nc:]
        if nc:
            @pl.when(pl.program_id(0) == 0)
            def _():
                _xfer_start(_gather_plan(c_in, c_out, _my_pos()), sems)

        @pl.when(pl.program_id(0) == 0)
        def _():
            s_ref[...] = jnp.zeros_like(s_ref)

        dirs = ((qf, kf, vf, gf, of_ref, sf_ref, tf_ref), (qb, kb, vb, gb, ob_ref, sb_ref, tb_ref))
        args = [(_stack(dirs[d][0], b), _stack(dirs[d][1], b), _stack(dirs[d][2], b), dirs[d][3][b], s_ref[i], d)
                for i, (b, d) in enumerate(SCAN_CHAINS)]
        for i, (o, s_new, tm) in enumerate(_chunks(args, cm_ref, c64_ref)):
            b, d = SCAN_CHAINS[i]
            dirs[d][5][b] = args[i][4]
            dirs[d][6][b] = tm.astype(BF16)
            _unstack_to(dirs[d][4], b, o)
            s_ref[i] = s_new

        if nc:
            @pl.when(pl.program_id(0) == NCH - 1)
            def _():
                _xfer_wait(_gather_plan(c_in, c_out, _my_pos()), sems)

    (tf, gf_, sf_, if_), (tb, gb_, sb_, ib_) = _scan_specs(lambda s: s)
    o_sh = jax.ShapeDtypeStruct((NB, T, DNW), F32)
    st_sh = jax.ShapeDtypeStruct((NB, NCH, 128, DNW), F32)
    inv_sh = jax.ShapeDtypeStruct((NB, NCH, SR, SR), BF16)
    hbm = pl.BlockSpec(memory_space=pltpu.HBM)
    return pl.pallas_call(
        body, name="scan_fwd", grid=(NCH,),
        out_shape=[o_sh, st_sh, inv_sh, o_sh, st_sh, inv_sh] + [jax.ShapeDtypeStruct((8,) + h.shape, h.dtype) for h in carry],
        in_specs=_const_specs() + [tf, tf, tf, gf_, tb, tb, tb, gb_] + [hbm] * nc,
        out_specs=[tf, sf_, if_, tb, sb_, ib_] + [hbm] * nc,
        scratch_shapes=[pltpu.VMEM((len(SCAN_CHAINS), 128, DNW), F32)] + (_xfer_sems(nc, 7 * nc) if nc else []),
        compiler_params=_cp(vmem=VMEM_BIG),
    )(*_scan_masks(), q, k, v, gates, q, k, v, gates, *carry)


def _scan_bwd(q, k, v, gates, st_f, st_b, inv_f, inv_b, do, carry=()):
    nc = len(carry)

    def body(*refs):
        (cm_ref, c64_ref, qf, kf, vf, gf, sf, tf_, dof, qb, kb, vb, gb, sb, tb_, dob) = refs[:16]
        c_in = refs[16:16 + nc]
        dqf, dkf, dvf, dgf, dqb, dkb, dvb, dgb = refs[16 + nc:24 + nc]
        c_out = refs[24 + nc:24 + 2 * nc]
        ds_ref = refs[24 + 2 * nc]
        sems = refs[25 + 2 * nc:]
        first = pl.program_id(0) == 0
        last = pl.program_id(0) == NCH - 1
        if nc:
            @pl.when(first)
            def _():
                _xfer_start(_scatter_plan(c_in, c_out, _my_pos()), sems)

        @pl.when(first)
        def _():
            ds_ref[...] = jnp.zeros_like(ds_ref)

        dirs = ((qf, kf, vf, gf, sf, dof, dqf, dkf, dvf, dgf), (qb, kb, vb, gb, sb, dob, dqb, dkb, dvb, dgb))
        prim = tuple((_stack(dirs[d][0], b), _stack(dirs[d][1], b), _stack(dirs[d][2], b), dirs[d][3][b], dirs[d][4][b])
                     for b, d in SCAN_CHAINS)
        cots = tuple((_stack(dirs[d][5], b), ds_ref[i]) for i, (b, d) in enumerate(SCAN_CHAINS))
        saved = tuple((tf_, tb_)[d][b] for b, d in SCAN_CHAINS)
        _, vjp = jax.vjp(lambda ps: tuple(r[:2] for r in _chunks([p + (SCAN_CHAINS[i][1],) for i, p in enumerate(ps)],
                                                                cm_ref, c64_ref, saved)), prim)
        (grads,) = vjp(cots)
        for i, (dq, dk, dv, dg, ds) in enumerate(grads):
            b, d = SCAN_CHAINS[i]
            _unstack_to(dirs[d][6], b, dq)
            _unstack_to(dirs[d][7], b, dk)
            _unstack_to(dirs[d][8], b, dv)
            dirs[d][9][b] = dg
            ds_ref[i] = ds

        if nc:
            @pl.when(last)
            def _():
                _xfer_wait(_scatter_plan(c_in, c_out, _my_pos()), sems)

    (tf, gf_, sf_, if_), (tb, gb_, sb_, ib_) = _scan_specs(lambda s: NCH - 1 - s)
    o_sh = jax.ShapeDtypeStruct((NB, T, DNW), F32)
    g_sh = jax.ShapeDtypeStruct((NB, T, 128), F32)
    hbm = pl.BlockSpec(memory_space=pltpu.HBM)
    return pl.pallas_call(
        body, name="scan_bwd", grid=(NCH,),
        out_shape=[o_sh, o_sh, o_sh, g_sh] * 2 + [jax.ShapeDtypeStruct(g.shape, g.dtype) for g in carry],
        in_specs=_const_specs() + [tf, tf, tf, gf_, sf_, if_, tf, tb, tb, tb, gb_, sb_, ib_, tb] + [hbm] * nc,
        out_specs=[tf, tf, tf, gf_, tb, tb, tb, gb_] + [hbm] * nc,
        scratch_shapes=[pltpu.VMEM((len(SCAN_CHAINS), 128, DNW), F32)] + (_xfer_sems(nc, 7 * nc) if nc else []),
        compiler_params=_cp(vmem=VMEM_BIG),
    )(*_scan_masks(), q, k, v, gates, st_f, inv_f, do, q, k, v, gates, st_b, inv_b, do, *carry)


def _segmean(a, bd):
    hi = a.astype(BF16)
    lo = (a - hi.astype(F32)).astype(BF16)
    return jnp.dot(hi, bd, preferred_element_type=F32) + jnp.dot(lo, bd, preferred_element_type=F32)


def _na_prep_fwd(p, qw, kw, bd):
    def body(q_ref, k_ref, v_ref, qw_ref, kw_ref, bd_ref, oq_ref, ok_ref, ov_ref):
        bdm = bd_ref[...]
        q = q_ref[...]
        k = k_ref[...]
        oq_ref[...] = (q * lax.rsqrt(_segmean(q * q, bdm) + EPS) * qw_ref[...] * 0.125).astype(BF16)
        ok_ref[...] = (k * lax.rsqrt(_segmean(k * k, bdm) + EPS) * kw_ref[...]).astype(BF16)
        ov_ref[...] = v_ref[...].astype(BF16)

    ps = lambda o: pl.BlockSpec((None, RT, NAW), lambda b, j: (b, j, COL_NA // NAW + o))
    one = pl.BlockSpec((1, NAW), lambda b, j: (0, 0))
    os_ = pl.BlockSpec((None, RT, NAW), lambda b, j: (b, j, 0))
    return pl.pallas_call(
        body, name="na_prep_fwd", grid=(NB, T // RT), out_shape=[jax.ShapeDtypeStruct((NB, T, NAW), BF16)] * 3,
        in_specs=[ps(0), ps(1), ps(2), one, one, pl.BlockSpec((NAW, NAW), lambda b, j: (0, 0))], out_specs=[os_] * 3,
    )(p, p, p, qw, kw, bd)


def _na_prep_bwd(p, qw, kw, bd, dq, dk, dv):
    def body(q_ref, k_ref, qw_ref, kw_ref, bd_ref, dq_ref, dk_ref, dv_ref, o_ref, gq_ref, gk_ref):
        b, j = pl.program_id(0), pl.program_id(1)
        bdm = bd_ref[...]

        def one(x, w, dy, scale):
            r = lax.rsqrt(_segmean(x * x, bdm) + EPS)
            g = dy * w * scale
            dx = r * g - x * (r * r * r) * _segmean(g * x, bdm)
            dw = jnp.sum(dy * x * r * scale, axis=0, keepdims=True)
            return dx, dw

        dqv = jnp.where(j == 0, 0.0, dq_ref[...])
        dxq, dwq = one(q_ref[...], qw_ref[...], dqv, 0.125)
        dxk, dwk = one(k_ref[...], kw_ref[...], dk_ref[...], 1.0)
        o_ref[...] = jnp.concatenate([dxq, dxk, dv_ref[...]], axis=1)
        pad = jnp.zeros((7, NAW), F32)

        @pl.when((b == 0) & (j == 0))
        def _():
            gq_ref[...] = jnp.zeros_like(gq_ref)
            gk_ref[...] = jnp.zeros_like(gk_ref)

        gq_ref[...] += jnp.concatenate([dwq, pad], axis=0)
        gk_ref[...] += jnp.concatenate([dwk, pad], axis=0)

    ps = lambda o: pl.BlockSpec((None, RT, NAW), lambda b, j: (b, j, COL_NA // NAW + o))
    one = pl.BlockSpec((1, NAW), lambda b, j: (0, 0))
    ts = pl.BlockSpec((None, RT, NAW), lambda b, j: (b, j, 0))
    acc = pl.BlockSpec((8, NAW), lambda b, j: (0, 0))
    return pl.pallas_call(
        body, name="na_prep_bwd", grid=(NB, T // RT),
        out_shape=[jax.ShapeDtypeStruct((NB, T, 1536), F32), jax.ShapeDtypeStruct((8, NAW), F32),
                   jax.ShapeDtypeStruct((8, NAW), F32)],
        in_specs=[ps(0), ps(1), one, one, pl.BlockSpec((NAW, NAW), lambda b, j: (0, 0)),
                  pl.BlockSpec((None, RT, NAW), lambda b, j: (b, jnp.maximum(j - 1, 0), 0)), ts, ts],
        out_specs=[pl.BlockSpec((None, RT, 1536), lambda b, j: (b, j, 0)), acc, acc],
        compiler_params=_cp(vmem=VMEM_BIG),
    )(p, p, qw, kw, bd, dq, dk, dv)


def _na_row0(r):
    return jnp.clip(r - WIN_ROWS // 2, 0, ROWS - WIN_ROWS)


def _na_cls(r):
    return _na_row0(r) - r + WIN_ROWS - 1


NHS = 4
NLW = NHS * 64
NSR = NHS * GRID_W


def _na_stack(x):
    lane = lax.broadcasted_iota(jnp.int32, (1, NLW), 1)
    return jnp.concatenate([jnp.where((lane >> 6) == h, x, jnp.zeros_like(x)) for h in range(NHS)], axis=0)


def _na_unstack(y):
    lane = lax.broadcasted_iota(jnp.int32, (1, NLW), 1)
    out = jnp.zeros((GRID_W, NLW), y.dtype)
    for h in range(NHS):
        out = jnp.where((lane >> 6) == h, y[GRID_W * h:GRID_W * h + GRID_W], out)
    return out


def _na_fwd_gen(q, kw, vw, kc, vc, bias):
    q4 = _na_stack(q)
    s_w = lax.dot_general(q4, kw, _NT, preferred_element_type=F32) + bias
    yield
    s_c = lax.dot_general(q4, kc, _NT, preferred_element_type=F32)
    yield
    mx = jnp.maximum(jnp.max(s_w, axis=1, keepdims=True), jnp.max(s_c, axis=1, keepdims=True))
    p_w = jnp.exp(s_w - mx)
    p_c = jnp.exp(s_c - mx)
    inv = 1.0 / (jnp.sum(p_w, axis=1, keepdims=True) + jnp.sum(p_c, axis=1, keepdims=True))
    yield
    o_w = jnp.dot(p_w.astype(BF16), vw, preferred_element_type=F32)
    yield
    o4 = (o_w + jnp.dot(p_c.astype(BF16), vc, preferred_element_type=F32)) * inv
    return _na_unstack(o4)


NRF = 2


def _na_fwd(qn, kn, vb, bias, carry=()):
    nc = len(carry)
    grid = (NB, ROWS // NRF)
    ng = NAH // NHS

    def body(*refs):
        q_ref, k_ref, v_ref = refs[:3]
        b_refs, rest = refs[3:3 + NRF], refs[3 + NRF:]
        c_in, o_ref, c_out, sems = rest[:nc], rest[nc], rest[nc + 1:2 * nc + 1], rest[2 * nc + 1:]
        ids = [pl.program_id(a) for a in range(2)]
        if nc:
            @pl.when((ids[0] == 0) & (ids[1] == 0))
            def _():
                _xfer_start(_gather_plan(c_in, c_out, _my_pos()), sems)

        gens = []
        for i in range(NRF):
            start = pl.multiple_of(CT + _na_row0(NRF * ids[1] + i) * GRID_W, GRID_W)
            rw = slice(GRID_W * i, GRID_W * (i + 1))
            for g in range(ng):
                ln = slice(NLW * g, NLW * g + NLW)
                gens.append(_na_fwd_gen(q_ref[rw, ln], k_ref[pl.ds(start, NWIN), ln], v_ref[pl.ds(start, NWIN), ln],
                                        k_ref[0:CT, ln], v_ref[0:CT, ln], b_refs[i][g]))
        for n, o in enumerate(_lockstep(gens)):
            i, g = n // ng, n % ng
            o_ref[GRID_W * i:GRID_W * (i + 1), NLW * g:NLW * g + NLW] = o.astype(BF16)

        if nc:
            @pl.when((ids[0] == grid[0] - 1) & (ids[1] == grid[1] - 1))
            def _():
                _xfer_wait(_gather_plan(c_in, c_out, _my_pos()), sems)

    qs = pl.BlockSpec((None, NRF * GRID_W, NAW), lambda b, j: (b, CT // (NRF * GRID_W) + j, 0))
    kv = pl.BlockSpec((None, T, NAW), lambda b, j: (b, 0, 0))
    bss = [pl.BlockSpec((None, ng, NSR, NWIN), lambda b, j, i=i: (_na_cls(NRF * j + i), 0, 0, 0)) for i in range(NRF)]
    lat = pl.BlockSpec((None, NRF * GRID_W, NAW), lambda b, j: (b, j, 0))
    hbm = pl.BlockSpec(memory_space=pltpu.HBM)
    res = pl.pallas_call(
        body, name="na_fwd", grid=grid,
        out_shape=[jax.ShapeDtypeStruct((NB, SEQ, NAW), BF16)] + [jax.ShapeDtypeStruct((8,) + h.shape, h.dtype) for h in carry],
        in_specs=[qs, kv, kv] + bss + [hbm] * nc, out_specs=[lat] + [hbm] * nc,
        scratch_shapes=_xfer_sems(nc, 7 * nc) if nc else [],
        compiler_params=_cp(vmem=VMEM_BIG),
    )(qn, kn, vb, *([bias] * NRF), *carry)
    return res[0], list(res[1:])


def _na_bwd_gen(q, kw, vw, kc, vc, bias, do):
    q4 = _na_stack(q)
    s_w = lax.dot_general(q4, kw, _NT, preferred_element_type=F32) + bias
    yield
    s_c = lax.dot_general(q4, kc, _NT, preferred_element_type=F32)
    yield
    mx = jnp.maximum(jnp.max(s_w, axis=1, keepdims=True), jnp.max(s_c, axis=1, keepdims=True))
    p_w = jnp.exp(s_w - mx)
    p_c = jnp.exp(s_c - mx)
    inv = 1.0 / (jnp.sum(p_w, axis=1, keepdims=True) + jnp.sum(p_c, axis=1, keepdims=True))
    p_w = p_w * inv
    p_c = p_c * inv
    do4 = _na_stack(do.astype(BF16))
    dp_w = lax.dot_general(do4, vw, _NT, preferred_element_type=F32)
    yield
    dp_c = lax.dot_general(do4, vc, _NT, preferred_element_type=F32)
    yield
    delta = jnp.sum(p_w * dp_w, axis=1, keepdims=True) + jnp.sum(p_c * dp_c, axis=1, keepdims=True)
    ds_w = p_w * (dp_w - delta)
    ds_c = p_c * (dp_c - delta)
    ds_wb, ds_cb = ds_w.astype(BF16), ds_c.astype(BF16)
    dq4 = jnp.dot(ds_wb, kw, preferred_element_type=F32)
    yield
    dq4 = dq4 + jnp.dot(ds_cb, kc, preferred_element_type=F32)
    yield
    dkw = lax.dot_general(ds_wb, q4, _TN, preferred_element_type=F32)
    yield
    dvw = lax.dot_general(p_w.astype(BF16), do4, _TN, preferred_element_type=F32)
    yield
    dkc = lax.dot_general(ds_cb, q4, _TN, preferred_element_type=F32)
    yield
    dvc = lax.dot_general(p_c.astype(BF16), do4, _TN, preferred_element_type=F32)
    return ds_w, _na_unstack(dq4), dkw, dvw, dkc, dvc


NRS = 4


def _na_bwd(qn, kn, vb, bias, do):
    def body(q_ref, k_ref, v_ref, *rest):
        b_refs, (do_ref, dq_ref, dk_ref, dv_ref, db_ref) = rest[:NRS], rest[NRS:]
        b, j = pl.program_id(1), pl.program_id(2)

        @pl.when((b == 0) & (j == 0))
        def _():
            db_ref[...] = jnp.zeros_like(db_ref)

        @pl.when(j == 0)
        def _():
            dk_ref[...] = jnp.zeros_like(dk_ref)
            dv_ref[...] = jnp.zeros_like(dv_ref)

        kc, vc = k_ref[0:CT, :], v_ref[0:CT, :]
        rows = [NRS * j + i for i in range(NRS)]
        starts = [pl.multiple_of(CT + _na_row0(r) * GRID_W, GRID_W) for r in rows]
        gens = [_na_bwd_gen(q_ref[GRID_W * i:GRID_W * (i + 1), :], k_ref[pl.ds(starts[i], NWIN), :],
                            v_ref[pl.ds(starts[i], NWIN), :], kc, vc, b_refs[i][...],
                            do_ref[GRID_W * i:GRID_W * (i + 1), :]) for i in range(NRS)]
        for i, (ds_w, dq, dkw, dvw, dkc, dvc) in enumerate(_lockstep(gens)):
            db_ref[_na_cls(rows[i])] += ds_w
            dq_ref[GRID_W * i:GRID_W * (i + 1), :] = dq
            dk_ref[pl.ds(starts[i], NWIN), :] += dkw
            dv_ref[pl.ds(starts[i], NWIN), :] += dvw
            dk_ref[0:CT, :] += dkc
            dv_ref[0:CT, :] += dvc

    qs = pl.BlockSpec((None, NRS * GRID_W, NLW), lambda g, b, j: (b, CT // (NRS * GRID_W) + j, g))
    kv = pl.BlockSpec((None, T, NLW), lambda g, b, j: (b, 0, g))
    bss = [pl.BlockSpec((None, None, NSR, NWIN), lambda g, b, j, i=i: (_na_cls(NRS * j + i), g, 0, 0)) for i in range(NRS)]
    lat = pl.BlockSpec((None, NRS * GRID_W, NLW), lambda g, b, j: (b, j, g))
    return pl.pallas_call(
        body, name="na_bwd", grid=(NAH // NHS, NB, ROWS // NRS),
        out_shape=[jax.ShapeDtypeStruct((NB, SEQ, NAW), F32)] + [jax.ShapeDtypeStruct((NB, T, NAW), F32)] * 2
        + [jax.ShapeDtypeStruct((WIN_ROWS, NAH // NHS, NSR, NWIN), F32)],
        in_specs=[qs, kv, kv] + bss + [lat],
        out_specs=[lat, kv, kv, pl.BlockSpec((WIN_ROWS, None, NSR, NWIN), lambda g, b, j: (0, g, 0, 0))],
        compiler_params=_cp(vmem=VMEM_BIG),
    )(qn, kn, vb, *([bias] * NRS), do)


def _na_tables():
    cols = np.arange(GRID_W)
    win_start = np.clip(cols - WIN_COLS // 2, 0, GRID_W - WIN_COLS)
    kc = cols[None, :]
    valid = (kc >= win_start[:, None]) & (kc < win_start[:, None] + WIN_COLS)
    rel = np.clip(kc - cols[:, None] + WIN_COLS - 1, 0, 2 * WIN_COLS - 2)
    return valid, rel


def _bias_table(rpb):
    valid, rel = _na_tables()
    nrel = 2 * WIN_COLS - 1
    shifts = np.stack([(rel == r) & valid for r in range(nrel)]).astype(np.float32)
    neg = np.where(valid, 0.0, NEG).astype(np.float32)

    def body(rpb_ref, sh_ref, neg_ref, o_ref):
        h = pl.program_id(0)
        for dr in range(2 * WIN_ROWS - 1):
            tile = neg_ref[...]
            for r in range(nrel):
                tile = tile + rpb_ref[h, dr, r] * sh_ref[r]
            for cls in range(WIN_ROWS):
                i = dr - cls
                if 0 <= i < WIN_ROWS:
                    o_ref[cls, :, i * GRID_W:(i + 1) * GRID_W] = tile

    return pl.pallas_call(
        body, name="bias_table", grid=(NAH,),
        out_shape=jax.ShapeDtypeStruct((WIN_ROWS, NAH // NHS, NSR, NWIN), F32),
        in_specs=[pl.BlockSpec(memory_space=pltpu.SMEM), pl.BlockSpec((nrel, GRID_W, GRID_W), lambda h: (0, 0, 0)),
                  pl.BlockSpec((GRID_W, GRID_W), lambda h: (0, 0))],
        out_specs=pl.BlockSpec((WIN_ROWS, None, GRID_W, NWIN), lambda h: (0, h // NHS, h % NHS, 0)),
    )(rpb, jnp.asarray(shifts), jnp.asarray(neg))


def _rpb_fold(db):
    def body1(d_ref, o_ref):
        for dr in range(2 * WIN_ROWS - 1):
            acc = None
            for c in range(WIN_ROWS):
                i = dr - c
                if 0 <= i < WIN_ROWS:
                    blk = d_ref[c, :, i * GRID_W:(i + 1) * GRID_W]
                    acc = blk if acc is None else acc + blk
            o_ref[dr] = acc

    t15 = pl.pallas_call(
        body1, name="rpb_fold_rows", grid=(NAH,),
        out_shape=jax.ShapeDtypeStruct((NAH, 2 * WIN_ROWS - 1, GRID_W, GRID_W), F32),
        in_specs=[pl.BlockSpec((WIN_ROWS, None, GRID_W, NWIN), lambda h: (0, h, 0, 0))],
        out_specs=pl.BlockSpec((None, 2 * WIN_ROWS - 1, GRID_W, GRID_W), lambda h: (h, 0, 0, 0)),
        compiler_params=_cp(vmem=VMEM_BIG),
    )(db.reshape(WIN_ROWS, NAH, GRID_W, NWIN))
    valid, rel = _na_tables()
    onehot = np.zeros((GRID_W * GRID_W, 128), np.float32)
    flat_rel, flat_valid = rel.reshape(-1), valid.reshape(-1)
    onehot[np.arange(GRID_W * GRID_W)[flat_valid], flat_rel[flat_valid]] = 1.0

    def body2(a_ref, oh_ref, o_ref):
        o_ref[...] = _hmm(a_ref[...], oh_ref[...])

    vm = pl.BlockSpec(memory_space=pltpu.VMEM)
    out = pl.pallas_call(
        body2, name="rpb_fold_cols", out_shape=jax.ShapeDtypeStruct((NAH * 15, 128), F32),
        in_specs=[vm, vm], out_specs=vm, compiler_params=_cp(vmem=VMEM_BIG),
    )(t15.reshape(NAH * 15, GRID_W * GRID_W), jnp.asarray(onehot))
    return out[:, :31].reshape(NAH, 15, 31)


def _dn_out(o, z, w):
    outs = []
    for h in range(DNH):
        sl = slice(128 * h, 128 * h + 128)
        oh = o[:, sl]
        r = lax.rsqrt(jnp.mean(oh * oh, axis=-1, keepdims=True) + EPS)
        outs.append(oh * r * w[:, sl] * _silu(z[:, sl]))
    return jnp.concatenate(outs, axis=1)


def _outproj_fwd(of, ob, p, na, onw, x, g1, w_out):
    def body(of_ref, ob_ref, z_ref, na_ref, w_ref, x_ref, g_ref, wo_ref, x1_ref, ai_ref, ao_ref):
        dn = _dn_out(of_ref[...] + ob_ref[...], z_ref[...], w_ref[...])
        ai = jnp.concatenate([dn.astype(BF16), na_ref[...]], axis=1)
        ai_ref[...] = ai
        ao = jnp.dot(ai, wo_ref[...], preferred_element_type=F32)
        ao_ref[...] = ao
        x1_ref[...] = x_ref[...] + g_ref[...] * ao

    lat = lambda w: pl.BlockSpec((None, RT, w), lambda b, j: (b, j, 0))
    return pl.pallas_call(
        body, name="outproj_fwd", grid=(NB, SEQ // RT),
        out_shape=[jax.ShapeDtypeStruct((NB, SEQ, D), F32), jax.ShapeDtypeStruct((NB, SEQ, D), BF16),
                   jax.ShapeDtypeStruct((NB, SEQ, D), F32)],
        in_specs=[pl.BlockSpec((None, RT, DNW), lambda b, j: (b, j + 1, 0)),
                  pl.BlockSpec((None, RT, DNW), lambda b, j: (b, j + 1, 0)),
                  pl.BlockSpec((None, RT, DNW), lambda b, j: (b, j + 1, COL_Z // DNW)),
                  lat(NAW), pl.BlockSpec((1, DNW), lambda b, j: (0, 0)), lat(D),
                  pl.BlockSpec((None, 1, D), lambda b, j: (b, 0, 0)), pl.BlockSpec((D, D), lambda b, j: (0, 0))],
        out_specs=[lat(D), lat(D), lat(D)],
        compiler_params=_cp(vmem=VMEM_BIG),
    )(of, ob, p, na, onw, x, g1, w_out)


def _outproj_bwd(dx1, ao, g1, w_out, of, ob, p, onw):
    def body(dx_ref, ao_ref, g_ref, wo_ref, of_ref, ob_ref, z_ref, w_ref, dao_ref, do_ref, dz_ref, dna_ref, dg_ref, dw_ref):
        b, j = pl.program_id(0), pl.program_id(1)

        @pl.when((b == 0) & (j == 0))
        def _():
            dw_ref[...] = jnp.zeros_like(dw_ref)

        @pl.when(j == 0)
        def _():
            do_ref[...] = jnp.zeros_like(do_ref)
            dz_ref[...] = jnp.zeros_like(dz_ref)

        @pl.when(j == 1)
        def _():
            dg_ref[...] = jnp.zeros_like(dg_ref)

        @pl.when(j >= 1)
        def _():
            dx = dx_ref[...]
            dg_ref[...] += jnp.sum(dx * ao_ref[...], axis=0, keepdims=True)
            dao = (g_ref[...] * dx).astype(BF16)
            dao_ref[...] = dao
            dai = lax.dot_general(dao, wo_ref[...], _NT, preferred_element_type=F32)
            dna_ref[...] = dai[:, DNW:]
            _, vjp = jax.vjp(_dn_out, of_ref[...] + ob_ref[...], z_ref[...], w_ref[...])
            do, dz, dw = vjp(dai[:, :DNW])
            do_ref[...] = do
            dz_ref[...] = dz
            dw_ref[...] += jnp.concatenate([dw, jnp.zeros((7, DNW), F32)], axis=0)

    lat = lambda w: pl.BlockSpec((None, RT, w), lambda b, j: (b, jnp.maximum(j - 1, 0), 0))
    tok = lambda w: pl.BlockSpec((None, RT, w), lambda b, j: (b, j, 0))
    return pl.pallas_call(
        body, name="outproj_bwd", grid=(NB, T // RT),
        out_shape=[jax.ShapeDtypeStruct((NB, SEQ, D), BF16), jax.ShapeDtypeStruct((NB, T, DNW), F32),
                   jax.ShapeDtypeStruct((NB, T, DNW), F32), jax.ShapeDtypeStruct((NB, SEQ, NAW), F32),
                   jax.ShapeDtypeStruct((NB, 1, D), F32), jax.ShapeDtypeStruct((8, DNW), F32)],
        in_specs=[_rows2d_spec(D), lat(D), pl.BlockSpec((None, 1, D), lambda b, j: (b, 0, 0)),
                  pl.BlockSpec((D, D), lambda b, j: (0, 0)),
                  tok(DNW), tok(DNW),
                  pl.BlockSpec((None, RT, DNW), lambda b, j: (b, j, COL_Z // DNW)),
                  pl.BlockSpec((1, DNW), lambda b, j: (0, 0))],
        out_specs=[lat(D), tok(DNW), tok(DNW), lat(NAW), pl.BlockSpec((None, 1, D), lambda b, j: (b, 0, 0)),
                   pl.BlockSpec((8, DNW), lambda b, j: (0, 0))],
        compiler_params=_cp(vmem=VMEM_BIG),
    )(dx1, ao, g1, w_out, of, ob, p, onw)


FT_FWD, FT_BWD = 512, 256
NFF = DFF // FFT


def _ffn_specs(ft):
    row = lambda w: pl.BlockSpec((ft, w), lambda i, f: (i, 0))
    ex = pl.BlockSpec((None, 1, D), lambda i, f: (i // (SEQ // ft), 0, 0))
    one = pl.BlockSpec((1, D), lambda i, f: (0, 0))
    wg = pl.BlockSpec((None, D, FFT), lambda i, f: (f, 0, 0))
    wu = pl.BlockSpec((None, D, FFT), lambda i, f: (NFF + f, 0, 0))
    wo = pl.BlockSpec((FFT, D), lambda i, f: (f, 0))
    ff = pl.BlockSpec((ft, FFT), lambda i, f: (i, f))
    return row, ex, one, wg, wu, wo, ff


def _ffn_fwd(x1, sh2, sc2, g2, n2w, w_in4, w_out, target):
    ft, tpe = FT_FWD, SEQ // FT_FWD

    def body(x_ref, sh_ref, sc_ref, g_ref, w2_ref, wg_ref, wu_ref, wo_ref, t_ref,
             h2_ref, gt_ref, up_ref, dy_ref, dg_ref, loss_ref, acc_ref):
        i, f = pl.program_id(0), pl.program_id(1)

        @pl.when(f == 0)
        def _():
            h2_ref[...] = _norm_mod(x_ref[...], w2_ref[...], sh_ref[...], sc_ref[...]).astype(BF16)
            acc_ref[...] = jnp.zeros_like(acc_ref)

        h2 = h2_ref[...]
        gate = jnp.dot(h2, wg_ref[...], preferred_element_type=F32)
        up = jnp.dot(h2, wu_ref[...], preferred_element_type=F32)
        gt_ref[...] = gate.astype(BF16)
        up_ref[...] = up.astype(BF16)
        acc_ref[...] += jnp.dot((_silu(gate) * up).astype(BF16), wo_ref[...], preferred_element_type=F32)

        @pl.when((i == 0) & (f == 0))
        def _():
            loss_ref[...] = jnp.zeros_like(loss_ref)

        @pl.when(f == NFF - 1)
        def _():
            ffn = acc_ref[...]
            err = x_ref[...] + g_ref[...] * ffn - t_ref[...]
            dy = err * (1.0 / D)
            dy_ref[...] = dy
            part = jnp.sum(dy * ffn, axis=0, keepdims=True)
            loss_ref[...] += jnp.sum(jnp.sum(err * err, axis=1, keepdims=True), axis=0, keepdims=True) * (0.5 / D)

            @pl.when(i % tpe == 0)
            def _():
                dg_ref[...] = part

            @pl.when(i % tpe != 0)
            def _():
                dg_ref[...] += part

    row, ex, one, wg, wu, wo, ff = _ffn_specs(ft)
    return pl.pallas_call(
        body, name="ffn_fwd", grid=(NTOK // ft, NFF),
        out_shape=[jax.ShapeDtypeStruct((NTOK, D), BF16), jax.ShapeDtypeStruct((NTOK, DFF), BF16),
                   jax.ShapeDtypeStruct((NTOK, DFF), BF16), jax.ShapeDtypeStruct((NTOK, D), F32),
                   jax.ShapeDtypeStruct((NB, 1, D), F32), jax.ShapeDtypeStruct((8, 128), F32)],
        in_specs=[row(D), ex, ex, ex, one, wg, wu, wo, row(D)],
        out_specs=[row(D), ff, ff, row(D), ex, pl.BlockSpec((8, 128), lambda i, f: (0, 0))],
        scratch_shapes=[pltpu.VMEM((ft, D), F32)],
        compiler_params=_cp(vmem=VMEM_BIG),
    )(x1, sh2, sc2, g2, n2w, w_in4, w_in4, w_out, target)


def _rows2d_spec(width):
    return pl.BlockSpec((RT, width), lambda b, j: (b * (SEQ // RT) + jnp.maximum(j - 1, 0), 0))


def _ffn_bwd(dy, gate, up, x1, sh2, sc2, g2, n2w, w_in4, w_out):
    ft, tpe = FT_BWD, SEQ // FT_BWD
    ni = NTOK // ft

    def body(dy_ref, gt_ref, up_ref, x_ref, sh_ref, sc_ref, g_ref, w2_ref, wg_ref, wu_ref, wo_ref,
             dff_ref, act_ref, dgu_ref, dx_ref, dsh_ref, dsc_ref, dw_ref, acc_ref):
        f, i = pl.program_id(0), pl.program_id(1)
        rows = pl.ds(pl.multiple_of(i * ft, ft), ft)
        dff = (g_ref[...] * dy_ref[...]).astype(BF16)
        dff_ref[...] = dff
        dact = lax.dot_general(dff, wo_ref[...], _NT, preferred_element_type=F32)
        gate, up = gt_ref[...].astype(F32), up_ref[...].astype(F32)
        sg = _sigmoid(gate)
        sl = gate * sg
        act_ref[...] = (sl * up).astype(BF16)
        dgate = (dact * up * (sg * (1.0 + gate * (1.0 - sg)))).astype(BF16)
        dup = (dact * sl).astype(BF16)
        dgu_ref[0] = dgate
        dgu_ref[1] = dup
        part = (lax.dot_general(dgate, wg_ref[...], _NT, preferred_element_type=F32)
                + lax.dot_general(dup, wu_ref[...], _NT, preferred_element_type=F32))

        @pl.when(f == 0)
        def _():
            acc_ref[rows, :] = part

        @pl.when(f > 0)
        def _():
            acc_ref[rows, :] += part

        @pl.when((i == 0) & (f == 0))
        def _():
            dw_ref[...] = jnp.zeros_like(dw_ref)

        @pl.when(f == NFF - 1)
        def _():
            _, vjp = jax.vjp(_norm_mod, x_ref[...], w2_ref[...], sh_ref[...], sc_ref[...])
            dx, dw, dsh, dsc = vjp(acc_ref[rows, :])
            dx_ref[...] = dy_ref[...] + dx
            dw_ref[...] += dw

            @pl.when(i % tpe == 0)
            def _():
                dsh_ref[...] = dsh
                dsc_ref[...] = dsc

            @pl.when(i % tpe != 0)
            def _():
                dsh_ref[...] += dsh
                dsc_ref[...] += dsc

    last = lambda f, real, spare: jnp.where(f == NFF - 1, real, spare)
    row = lambda w: pl.BlockSpec((ft, w), lambda f, i: (i, 0))
    ex = pl.BlockSpec((None, 1, D), lambda f, i: (i // tpe, 0, 0))
    one = pl.BlockSpec((1, D), lambda f, i: (0, 0))
    wg = pl.BlockSpec((None, D, FFT), lambda f, i: (f, 0, 0))
    wu = pl.BlockSpec((None, D, FFT), lambda f, i: (NFF + f, 0, 0))
    wo = pl.BlockSpec((FFT, D), lambda f, i: (f, 0))
    ff = pl.BlockSpec((ft, FFT), lambda f, i: (i, f))
    row_once = lambda w: pl.BlockSpec((ft, w), lambda f, i: (last(f, i, ni), 0))
    ex_once = pl.BlockSpec((None, 1, D), lambda f, i: (last(f, i // tpe, NB), 0, 0))
    return pl.pallas_call(
        body, name="ffn_bwd", grid=(NFF, ni),
        out_shape=[jax.ShapeDtypeStruct((NTOK + ft, D), BF16), jax.ShapeDtypeStruct((NTOK, DFF), BF16),
                   jax.ShapeDtypeStruct((2, NTOK, DFF), BF16), jax.ShapeDtypeStruct((NTOK + ft, D), F32),
                   jax.ShapeDtypeStruct((NB + 1, 1, D), F32), jax.ShapeDtypeStruct((NB + 1, 1, D), F32),
                   jax.ShapeDtypeStruct((1, D), F32)],
        in_specs=[row(D), ff, ff, row(D), ex, ex, ex, one, wg, wu, wo],
        out_specs=[row_once(D), ff, pl.BlockSpec((2, ft, FFT), lambda f, i: (0, i, f)), row_once(D), ex_once, ex_once, one],
        scratch_shapes=[pltpu.VMEM((NTOK, D), F32)],
        compiler_params=_cp(vmem=VMEM_BIG),
    )(dy, gate, up, x1, sh2, sc2, g2, n2w, w_in4, w_in4, w_out)


def _local_step(x, ctx, target, mod_x, mod_c, norm1_w, norm2_w, w_in_p, conv_w, a_log, dt_bias, out_norm_w,
                q_norm_w, k_norm_w, rpb, w_out_b, w_ffn_in4, w_ffn_out_b, scatter_early=False, gather_late=None):
    sh1, sc1, g1, sh2, sc2, g2 = [mod_x[:, i * D:(i + 1) * D].reshape(NB, 1, D) for i in range(6)]
    csh1, csc1 = mod_c[0:D].reshape(1, 1, D), mod_c[D:2 * D].reshape(1, 1, D)
    cw = jnp.concatenate([conv_w, jnp.zeros((3, 1536), F32)], axis=0)
    prm = jnp.pad(jnp.stack([a_log.reshape(8), dt_bias.reshape(8)]), ((0, 6), (0, 120)))
    onw = jnp.tile(out_norm_w.reshape(1, 128), (1, DNH))
    qw = jnp.tile(q_norm_w.reshape(1, 64), (1, NAH))
    kw = jnp.tile(k_norm_w.reshape(1, 64), (1, NAH))
    seg = np.arange(NAW) // 64
    bd = jnp.asarray((seg[:, None] == seg[None, :]).astype(np.float32) / 64.0, dtype=BF16)
    bias = _bias_table(rpb)

    p, hm = _inproj_fwd(x, ctx, sh1, sc1, csh1, csc1, norm1_w, w_in_p)
    dq_, dk_, dv_ = _dn_prep_fwd(p, cw)
    gates = _gates_fwd(p, prm)
    res = _scan_fwd(dq_, dk_, dv_, gates, carry=gather_late[:2] if gather_late else ())
    o_f, st_f, inv_f, o_b, st_b, inv_b = res[:6]
    qn, kn, vb = _na_prep_fwd(p, qw, kw, bd)
    na, got = _na_fwd(qn, kn, vb, bias, carry=gather_late[2:] if gather_late else ())
    if gather_late:
        w_out_b = res[6].reshape(D, D)
        w_ffn_in4 = res[7].reshape(4, D, FFT)
        w_ffn_out_b = got[0].reshape(DFF, D)
    x1, ai, ao = _outproj_fwd(o_f, o_b, p, na, onw, x, g1, w_out_b)
    x1f = x1.reshape(NTOK, D)
    h2, gate, up, dy, dg2, loss8 = _ffn_fwd(x1f, sh2, sc2, g2, norm2_w, w_ffn_in4, w_ffn_out_b,
                                             target.reshape(NTOK, D))
    dff, act, dgu, dx1, dsh2, dsc2, dn2w = _ffn_bwd(dy, gate, up, x1f, sh2, sc2, g2, norm2_w, w_ffn_in4, w_ffn_out_b)
    g_w_ffn_out = _matmul_tn(act, dff, FFT, D, 512, "wgrad_ffn_out")
    g_w_ffn_in4 = _matmul_tn(
        h2, dgu, D, FFT, 1024, "wgrad_ffn_in", nn=2 * NFF,
        b_block=(None, 1024, FFT), b_map=lambda i, j, k: (j // NFF, k, j % NFF),
        out_shape=(4, D, FFT), out_block=(None, D, FFT), out_map=lambda i, j, k: (j, i, 0))
    dsh2, dsc2 = dsh2[:NB], dsc2[:NB]
    dao, do, dz, dna, dg1, donw = _outproj_bwd(dx1, ao, g1, w_out_b, o_f, o_b, p, onw)
    g_w_out = _matmul_tn(ai.reshape(NTOK, D), dao.reshape(NTOK, D), D, D, 1024, "wgrad_out")
    dqn, dkn, dvn, dbias = _na_bwd(qn, kn, vb, bias, dna)
    dp_na, dqw, dkw = _na_prep_bwd(p, qw, kw, bd, dqn, dkn, dvn)
    early = [g_w_out.reshape(8, D // 8, D), g_w_ffn_in4.reshape(8, D // 2, FFT), g_w_ffn_out.reshape(8, DFF // 8, D)]
    res = _scan_bwd(dq_, dk_, dv_, gates, st_f, st_b, inv_f, inv_b, do, carry=early if scatter_early else ())
    dqf, dkf, dvf, dgf, dqb, dkb, dvb, dgb = res[:8]
    if scatter_early:
        g_w_out, g_w_ffn_in4, g_w_ffn_out = res[8:]
    dpq, dpk, dpv, gcq, gck, gcv = _dn_prep_bwd(p, cw, (dqf, dqb), (dkf, dkb), (dvf, dvb))
    dgt, dprm = _gates_bwd(p, prm, dgf, dgb)
    dpb, grad_x, dsh1, dsc1, dcsh1, dcsc1, dn1w = _inproj_bwd(
        (dpq, dpk, dpv), dz, dp_na, dgt, w_in_p, x, ctx, sh1, sc1, csh1, csc1, norm1_w, dx1)
    g_w_in_p = _matmul_tn(dpb.reshape(NB * T, PC), hm.reshape(NB * T, D), PC, D, 512, "wgrad_in")
    g_rpb = _rpb_fold(dbias)

    dmod_x = jnp.concatenate([dsh1, dsc1, dg1, dsh2, dsc2, dg2], axis=2).reshape(NB, 6 * D)
    dmod_c = jnp.concatenate([dcsh1.reshape(D), dcsc1.reshape(D), jnp.zeros((4 * D,), F32)])
    g_conv = jnp.concatenate([gcq[:5], gck[:5], gcv[:5]], axis=1)
    small = dict(
        norm1_w=dn1w, norm2_w=dn2w, a_log=dprm[0, :8], dt_bias=dprm[1, :8],
        out_norm_w=donw[0].reshape(DNH, 128).sum(0), q_norm_w=dqw[0].reshape(NAH, 64).sum(0),
        k_norm_w=dkw[0].reshape(NAH, 64).sum(0), rpb=g_rpb, conv_w=g_conv)
    return loss8[0, 0], grad_x, dmod_x, dmod_c, small, g_w_in_p, g_w_out, g_w_ffn_in4, g_w_ffn_out


def _perm_w_in(wt):
    return jnp.concatenate([wt[:2048], wt[2064:3600], wt[2048:2064], jnp.zeros((PC - IN_COLS, wt.shape[1]), wt.dtype)],
                           axis=0)


def _unperm_w_in(gt):
    return jnp.concatenate([gt[:2048], gt[COL_G:COL_G + 16], gt[2048:COL_G]], axis=0)


def _misc_row(a_log, dt_bias, onw, qnw, knw):
    return jnp.concatenate([a_log.reshape(8), dt_bias.reshape(8), jnp.zeros((112,), F32), onw.reshape(128),
                            qnw.reshape(64), knw.reshape(64), jnp.zeros((D - 384,), F32)])


def _pack_small(c_ctx, n1, n2, b_ada, a_log, dt_bias, onw, qnw, knw, rpb, conv_s):
    misc = _misc_row(a_log, dt_bias, onw, qnw, knw)
    rp = jnp.concatenate([rpb.reshape(-1), jnp.zeros((4 * D - 3720,), F32)])
    cv = jnp.concatenate([conv_s.reshape(-1), jnp.zeros((2 * D - 1920,), F32)])
    return jnp.concatenate([c_ctx.reshape(-1), n1.reshape(-1), n2.reshape(-1), b_ada.reshape(-1), misc, rp, cv]).reshape(16, D)


def _unpack_small(a):
    f = a.reshape(-1)
    misc = f[9 * D:10 * D]
    return dict(
        c_ctx=f[0:D], norm1_w=f[D:2 * D].reshape(1, D), norm2_w=f[2 * D:3 * D].reshape(1, D),
        b_ada=f[3 * D:9 * D].reshape(1, 6 * D), dn_A_log=misc[0:8].reshape(1, 2, 4), dn_dt_bias=misc[8:16].reshape(1, 2, 4),
        dn_out_norm_w=misc[128:256].reshape(1, 128), na_q_norm_w=misc[256:320].reshape(1, 64),
        na_k_norm_w=misc[320:384].reshape(1, 64), na_rpb=f[10 * D:10 * D + 3720].reshape(1, 8, 15, 31),
        dn_conv_w=f[14 * D:14 * D + 1920].reshape(1, 5, 384))


_WEIGHTS = ['c_ctx', 'norm1_w', 'norm2_w', 'w_ada', 'b_ada', 'w_in', 'dn_conv_w', 'dn_A_log', 'dn_dt_bias',
            'dn_out_norm_w', 'na_q_norm_w', 'na_k_norm_w', 'na_rpb', 'w_out', 'w_ffn_in', 'w_ffn_out']


def kernel(x, c, ctx, c_ctx, norm1_w, norm2_w, w_ada, b_ada, w_in, dn_conv_w, dn_A_log, dn_dt_bias, dn_out_norm_w, na_q_norm_w, na_k_norm_w, na_rpb, w_out, w_ffn_in, w_ffn_out, loss_target, m_c_ctx, m_norm1_w, m_norm2_w, m_w_ada, m_b_ada, m_w_in, m_dn_conv_w, m_dn_A_log, m_dn_dt_bias, m_dn_out_norm_w, m_na_q_norm_w, m_na_k_norm_w, m_na_rpb, m_w_out, m_w_ffn_in, m_w_ffn_out, v_c_ctx, v_norm1_w, v_norm2_w, v_w_ada, v_b_ada, v_w_in, v_dn_conv_w, v_dn_A_log, v_dn_dt_bias, v_dn_out_norm_w, v_na_q_norm_w, v_na_k_norm_w, v_na_rpb, v_w_out, v_w_ffn_in, v_w_ffn_out):
    ix, iy, ic = lax.axis_index("x"), lax.axis_index("y"), lax.axis_index("c")
    chip = 2 * ix + iy
    me = 4 * ix + 2 * iy + ic

    crow = jnp.concatenate([c, c_ctx.reshape(1, D), jnp.zeros((5, D), F32)], axis=0)
    cvrow = jnp.concatenate([dn_conv_w[0].reshape(-1), jnp.zeros((2 * D - 1920,), F32)]).reshape(2, D)
    w_in_t = w_in[0].T
    gw_in, g0 = _gather_weights([_cast_bf16(lax.dynamic_slice(w_in_t, (0, ic * (D // 2)), (900, D // 2)), "cast_w_in")],
                                jnp.concatenate([crow, cvrow, jnp.zeros((6, D), F32)], axis=0))
    cin = g0[:, :8].reshape(64, D)
    conv_full = jnp.concatenate([g0[2 * s, 8:10].reshape(-1)[:1920].reshape(5, 384) for s in range(4)], axis=1)

    w_ada_s = w_ada[0]
    b_s = lax.dynamic_slice(b_ada, (0, chip * 1536), (1, 1536))
    mod_s = _ada_fwd(cin, w_ada_s, b_s)
    mods = _gather_rows(mod_s, "gather_mod")
    mod_mine = jnp.concatenate([mods[0], mods[2], mods[4], mods[6]], axis=1)
    mod_x, mod_c = mod_mine[0:NB], mod_mine[2]

    def half(w2d):
        r = w2d.shape[0] // 2
        return lax.dynamic_slice(w2d, (ic * r, 0), (r, w2d.shape[1]))

    late = [_cast_bf16(half(w_out[0]), "cast_w_out"), _cast_bf16(half(w_ffn_in[0]), "cast_w_ffn_in"),
            _cast_bf16(half(w_ffn_out[0]), "cast_w_ffn_out")]
    w_in_p = _perm_w_in(gw_in.reshape(4, 2, 900, D // 2).transpose(0, 2, 1, 3).reshape(IN_COLS, D))

    (loss_part, grad_x, dmod_x, dmod_c, small, g_w_in_p, r_w_out, r_w_ffn_in, r_w_ffn_out) = _local_step(
        x, ctx, loss_target, mod_x, mod_c, norm1_w, norm2_w, w_in_p, conv_full, dn_A_log[0], dn_dt_bias[0],
        dn_out_norm_w[0], na_q_norm_w[0], na_k_norm_w[0], na_rpb[0], None, None, None,
        scatter_early=True, gather_late=late)
    loss = lax.psum(loss_part, ("x", "y", "c"))

    dm = jnp.concatenate([dmod_x, dmod_c.reshape(1, 6 * D), jnp.zeros((5, 6 * D), F32)], axis=0).reshape(48, D)
    misc = _misc_row(small["a_log"], small["dt_bias"], small["out_norm_w"], small["q_norm_w"], small["k_norm_w"])
    sm = jnp.concatenate([
        small["norm1_w"].reshape(-1), small["norm2_w"].reshape(-1), misc,
        small["rpb"].reshape(-1), jnp.zeros((4 * D - 3720,), F32),
        small["conv_w"].reshape(-1), jnp.zeros((8 * D - 7680,), F32), jnp.zeros((D,), F32)]).reshape(16, D)
    gsl, gtot = _allgather8(jnp.concatenate([dm, sm], axis=0), "gather_small", with_sum=True)
    gmod = gsl[:, :48].reshape(64, 6 * D)
    st = gtot[48:64].reshape(-1)
    g_n1, g_n2, g_misc = st[0:D].reshape(1, D), st[D:2 * D].reshape(1, D), st[2 * D:3 * D]
    g_rpb = st[3 * D:3 * D + 3720].reshape(1, 8, 15, 31)
    g_conv_full = st[7 * D:7 * D + 7680].reshape(5, 1536)
    g_conv_s = lax.dynamic_slice(g_conv_full, (0, chip * 384), (5, 384)).reshape(1, 5, 384)

    gmod_s = lax.dynamic_slice(gmod, (0, chip * 1536), (64, 1536))
    g_w_ada_s, pctx = _ada_bwd(cin, gmod_s, w_ada_s)

    g_w_in8 = _unperm_w_in(g_w_in_p).reshape(4, 900, 2, D // 2).transpose(0, 2, 1, 3).reshape(8, 900, D // 2)
    names = ["w_in", "w_out", "w_ffn_in", "w_ffn_out"]
    early = _sibling_exchange([_sum_slots(r, "sum_" + nm)
                               for r, nm in zip([r_w_out, r_w_ffn_in, r_w_ffn_out], names[1:])], "sibling_early")
    g_w_out_s = early[0].reshape(1, D // 4, D)
    g_w_ffn_in_s = early[1].reshape(1, D, FFT)
    g_w_ffn_out_s = early[2].reshape(1, DFF // 4, D)
    two = lambda a: a.reshape(a.shape[1], a.shape[2])
    hosted_names = ["w_ada", "w_out", "w_ffn_in"]
    hosted_g = dict(w_ada=g_w_ada_s, w_out=two(g_w_out_s), w_ffn_in=two(g_w_ffn_in_s))
    hosted_w = dict(w_ada=(w_ada, m_w_ada, v_w_ada), w_out=(w_out, m_w_out, v_w_out), w_ffn_in=(w_ffn_in, m_w_ffn_in, v_w_ffn_in))
    hosted_res, (r_w_in, parts) = _adamw_hosting(
        [(two(hosted_w[k][0]), hosted_g[k], two(hosted_w[k][1]), two(hosted_w[k][2])) for k in hosted_names], [g_w_in8], [pctx])
    g_b_ada, g_c_ctx = _ada_small(gmod, parts, c_ctx.reshape(1, D))
    (g_w_in_half,) = _sibling_exchange([_sum_slots(r_w_in, "sum_w_in")], "sibling_w_in")
    g_w_in_t = g_w_in_half.transpose(1, 0, 2).reshape(900, D)

    grads = dict(
        c_ctx=g_c_ctx.reshape(D), norm1_w=g_n1, norm2_w=g_n2, w_ada=g_w_ada_s.reshape(1, D, 1536), b_ada=g_b_ada,
        w_in=g_w_in_t, dn_conv_w=g_conv_s, dn_A_log=g_misc[0:8].reshape(1, 2, 4), dn_dt_bias=g_misc[8:16].reshape(1, 2, 4),
        dn_out_norm_w=g_misc[128:256].reshape(1, 128), na_q_norm_w=g_misc[256:320].reshape(1, 64),
        na_k_norm_w=g_misc[320:384].reshape(1, 64), na_rpb=g_rpb, w_out=g_w_out_s, w_ffn_in=g_w_ffn_in_s,
        w_ffn_out=g_w_ffn_out_s)
    pk = lambda d_: _pack_small(d_["c_ctx"], d_["norm1_w"], d_["norm2_w"], d_["b_ada"], d_["dn_A_log"], d_["dn_dt_bias"],
                                d_["dn_out_norm_w"], d_["na_q_norm_w"], d_["na_k_norm_w"], d_["na_rpb"], d_["dn_conv_w"])
    wts = dict(c_ctx=c_ctx, norm1_w=norm1_w, norm2_w=norm2_w, b_ada=b_ada, dn_A_log=dn_A_log, dn_dt_bias=dn_dt_bias,
               dn_out_norm_w=dn_out_norm_w, na_q_norm_w=na_q_norm_w, na_k_norm_w=na_k_norm_w, na_rpb=na_rpb, dn_conv_w=dn_conv_w)
    ms = dict(c_ctx=m_c_ctx, norm1_w=m_norm1_w, norm2_w=m_norm2_w, b_ada=m_b_ada, dn_A_log=m_dn_A_log, dn_dt_bias=m_dn_dt_bias,
              dn_out_norm_w=m_dn_out_norm_w, na_q_norm_w=m_na_q_norm_w, na_k_norm_w=m_na_k_norm_w, na_rpb=m_na_rpb,
              dn_conv_w=m_dn_conv_w)
    vs = dict(c_ctx=v_c_ctx, norm1_w=v_norm1_w, norm2_w=v_norm2_w, b_ada=v_b_ada, dn_A_log=v_dn_A_log, dn_dt_bias=v_dn_dt_bias,
              dn_out_norm_w=v_dn_out_norm_w, na_q_norm_w=v_na_q_norm_w, na_k_norm_w=v_na_k_norm_w, na_rpb=v_na_rpb,
              dn_conv_w=v_dn_conv_w)
    sd, snm, snv, _ = _adamw(pk(wts), pk(grads), pk(ms), pk(vs), "adamw_small")
    delta, new_m, new_v = _unpack_small(sd), _unpack_small(snm), _unpack_small(snv)
    for k, res in zip(hosted_names, hosted_res):
        shp = hosted_w[k][0].shape
        delta[k], new_m[k], new_v[k], grads[k] = [a.reshape(shp) for a in res]
    big = dict(w_in=(w_in, m_w_in, v_w_in), w_ffn_out=(w_ffn_out, m_w_ffn_out, v_w_ffn_out))
    for nm, (w_, m_, v_) in big.items():
        shp = w_.shape
        if nm == "w_in":
            r2, back = (lambda a: a[0].T), (lambda a: a.T.reshape(shp))
            res = _adamw(w_in_t, grads[nm], r2(m_), r2(v_), "adamw_" + nm)
        else:
            r2, back = (lambda a: a.reshape(shp[1], shp[2])), (lambda a: a.reshape(shp))
            res = _adamw(r2(w_), r2(grads[nm]), r2(m_), r2(v_), "adamw_" + nm)
        delta[nm], new_m[nm], new_v[nm], grads[nm] = [back(a) for a in res]

    return (loss, grad_x, *[grads[n] for n in _WEIGHTS], *[delta[n] for n in _WEIGHTS],
            *[new_m[n] for n in _WEIGHTS], *[new_v[n] for n in _WEIGHTS])
```

```python
import functools

import numpy as np
import jax
import jax.numpy as jnp
from jax import lax
from jax.experimental import pallas as pl
from jax.experimental.pallas import tpu as pltpu

F32 = jnp.float32
BF16 = jnp.bfloat16

D = 1024
NB = 2
SEQ = 2048
CT = 256
T = CT + SEQ
NTOK = NB * SEQ
CH = 64
NCH = T // CH
NCH_CTX = CT // CH
DNH, DNW = 4, 512
NAH, NAW = 8, 512
GRID_W = 64
ROWS = SEQ // GRID_W
WIN_ROWS = 8
WIN_COLS = 16
NWIN = WIN_ROWS * GRID_W
DFF = 2816
FFT = 1408
IN_COLS = 3600
PC = 3712
COL_Z, COL_NA, COL_G = 1536, 2048, 3584
EPS = 1e-6
NEG = -1e30
RT = 256
VMEM_BIG = 56 * 1024 * 1024
MESH = pl.DeviceIdType.MESH

ADAM_LR, ADAM_B1, ADAM_B2, ADAM_EPS, ADAM_WD, ADAM_STEP = 0.001, 0.9, 0.999, 1e-08, 0.01, 10


def _cp(vmem=None, sem=None):
    kw = {}
    if vmem is not None:
        kw["vmem_limit_bytes"] = vmem
    if sem is not None:
        kw["dimension_semantics"] = sem
    return pltpu.CompilerParams(**kw)


_NN = (((1,), (0,)), ((), ()))
_NT = (((1,), (1,)), ((), ()))
_TN = (((0,), (0,)), ((), ()))


def _dg(a, b, dims):
    return lax.dot_general(a.astype(BF16), b.astype(BF16), dims, preferred_element_type=F32)


@jax.custom_vjp
def _mm(a, b):
    return _dg(a, b, _NN)


_mm.defvjp(lambda a, b: (_dg(a, b, _NN), (a, b)),
           lambda r, g: (_dg(g, r[1], _NT), _dg(r[0], g, _TN)))


def _split_dot(a, b, dims):
    hi = b.astype(BF16)
    lo = (b - hi.astype(F32)).astype(BF16)
    ab = a.astype(BF16)
    return (lax.dot_general(ab, hi, dims, preferred_element_type=F32)
            + lax.dot_general(ab, lo, dims, preferred_element_type=F32))


@jax.custom_vjp
def _mask_mm(mask, b):
    return _split_dot(mask, b, _NN)


_mask_mm.defvjp(lambda mask, b: (_split_dot(mask, b, _NN), mask),
                lambda mask, g: (jnp.zeros_like(mask), _split_dot(mask, g, _TN)))


def _hmm(a, b):
    return jnp.dot(a, b, precision=lax.Precision.HIGHEST, preferred_element_type=F32)


def _sigmoid(x):
    return 1.0 / (1.0 + jnp.exp(-x))


def _silu(x):
    return x * _sigmoid(x)


def _norm_mod(x, w, sh, sc):
    r = lax.rsqrt(jnp.mean(x * x, axis=-1, keepdims=True) + EPS)
    return (x * r * w) * (1.0 + sc) + sh


def _my_pos():
    return lax.axis_index("x"), lax.axis_index("y"), lax.axis_index("c")


def _flip(pos, k):
    x, y, c = pos
    return (1 - x if k & 4 else x, 1 - y if k & 2 else y, 1 - c if k & 1 else c)


def _slot(pos):
    return 4 * pos[0] + 2 * pos[1] + pos[2]


def _allgather8(v, name, with_sum=False):
    R, W = v.shape

    def body(v_ref, out_ref, *rest):
        if with_sum:
            tot_ref, send_sems, recv_sems, lsem = rest
        else:
            send_sems, recv_sems, lsem = rest
        me = _my_pos()
        mine = pltpu.make_async_copy(v_ref, out_ref.at[_slot(me)], lsem)
        mine.start()
        sends = []
        for k in range(1, 8):
            cp = pltpu.make_async_remote_copy(
                src_ref=v_ref, dst_ref=out_ref.at[_slot(me)], send_sem=send_sems.at[k - 1],
                recv_sem=recv_sems.at[k - 1], device_id=_flip(me, k), device_id_type=MESH)
            cp.start()
            sends.append(cp)
        for k in range(1, 8):
            peer = _flip(me, k)
            pltpu.make_async_remote_copy(
                src_ref=v_ref, dst_ref=out_ref.at[_slot(peer)], send_sem=send_sems.at[k - 1],
                recv_sem=recv_sems.at[k - 1], device_id=peer, device_id_type=MESH).wait_recv()
        for cp in sends:
            cp.wait_send()
        mine.wait()
        if with_sum:
            acc = out_ref[0]
            for s in range(1, 8):
                acc = acc + out_ref[s]
            tot_ref[...] = acc

    out_shape = [jax.ShapeDtypeStruct((8, R, W), F32)]
    out_specs = [pl.BlockSpec(memory_space=pltpu.VMEM)]
    if with_sum:
        out_shape.append(jax.ShapeDtypeStruct((R, W), F32))
        out_specs.append(pl.BlockSpec(memory_space=pltpu.VMEM))
    res = pl.pallas_call(
        body, name=name, out_shape=out_shape,
        in_specs=[pl.BlockSpec(memory_space=pltpu.VMEM)], out_specs=out_specs,
        scratch_shapes=[pltpu.SemaphoreType.DMA((7,)), pltpu.SemaphoreType.DMA((7,)), pltpu.SemaphoreType.DMA],
        compiler_params=_cp(vmem=VMEM_BIG),
    )(v)
    return res if with_sum else res[0]


def _gather_rows(v, name):
    W = v.shape[1]

    def body(v_ref, out_ref, send_sems, recv_sems, lsem):
        pos = _my_pos()
        me = _slot(pos)
        rows = lambda p: v_ref.at[pl.ds(pl.multiple_of(8 * p, 8), 8)]
        locals_ = [(rows(me), out_ref.at[me])]
        xfers = [(rows(_slot(_flip(pos, k))), out_ref.at[me], out_ref.at[_slot(_flip(pos, k))], _flip(pos, k))
                 for k in range(1, 8)]
        _xfer_start((locals_, xfers), (send_sems, recv_sems, lsem))
        _xfer_wait((locals_, xfers), (send_sems, recv_sems, lsem))

    vm = pl.BlockSpec(memory_space=pltpu.VMEM)
    return pl.pallas_call(
        body, name=name, out_shape=jax.ShapeDtypeStruct((8, 8, W), F32), in_specs=[vm], out_specs=vm,
        scratch_shapes=_xfer_sems(1, 7),
    )(v)


def _gather_weights(halves, small):
    n = len(halves)

    def body(*refs):
        ins, s_in, outs, s_out = refs[:n], refs[n], refs[n + 1:2 * n + 1], refs[2 * n + 1]
        send_sems, recv_sems, lsems = refs[2 * n + 2:2 * n + 5]
        s_sems = refs[2 * n + 5:]
        s_plan = _gather_plan([s_in], [s_out], _my_pos())
        _xfer_start(s_plan, s_sems)
        me = _my_pos()
        x, y, c = me
        sib = (x, y, 1 - c)
        chips = [4, 2, 6]

        def cp(a, k, block, to, src=None):
            dst = outs[a].at[_slot(block)]
            return pltpu.make_async_remote_copy(
                src_ref=dst if src is None else src, dst_ref=dst, send_sem=send_sems.at[a, k],
                recv_sem=recv_sems.at[a, k], device_id=to, device_id_type=MESH)

        mine, first, passed = [], [], []
        for a in range(n):
            m = pltpu.make_async_copy(ins[a], outs[a].at[_slot(me)], lsems.at[a])
            m.start()
            mine.append(m)
            first.append(cp(a, 0, me, sib, src=ins[a]))
            for j, k in enumerate(chips):
                first.append(cp(a, 1 + j, me, _flip(me, k), src=ins[a]))
        for f in first:
            f.start()
        for j, k in enumerate(chips):
            for a in range(n):
                cp(a, 1 + j, _flip(me, k), me).wait_recv()
                p = cp(a, 4 + j, _flip(me, k), sib)
                p.start()
                passed.append(p)
        for a in range(n):
            cp(a, 0, sib, me).wait_recv()
            for j, k in enumerate(chips):
                cp(a, 4 + j, _flip(sib, k), me).wait_recv()
        for f in first + passed:
            f.wait_send()
        for m in mine:
            m.wait()
        _xfer_wait(s_plan, s_sems)

    vm = pl.BlockSpec(memory_space=pltpu.VMEM)
    return pl.pallas_call(
        body, name="gather_weights",
        out_shape=[jax.ShapeDtypeStruct((8,) + h.shape, h.dtype) for h in list(halves) + [small]],
        in_specs=[vm] * (n + 1), out_specs=[vm] * (n + 1),
        scratch_shapes=[pltpu.SemaphoreType.DMA((n, 7)), pltpu.SemaphoreType.DMA((n, 7)),
                        pltpu.SemaphoreType.DMA((n,))] + _xfer_sems(1, 7),
        compiler_params=_cp(vmem=VMEM_BIG),
    )(*halves, small)


def _xfer_start(plan, sems):
    send_sems, recv_sems, lsems = sems
    locals_, xfers = plan
    for i, (src, dst) in enumerate(locals_):
        pltpu.make_async_copy(src, dst, lsems.at[i]).start()
    for i, (src, dst, _, peer) in enumerate(xfers):
        pltpu.make_async_remote_copy(src_ref=src, dst_ref=dst, send_sem=send_sems.at[i], recv_sem=recv_sems.at[i],
                                     device_id=peer, device_id_type=MESH).start()


def _xfer_wait(plan, sems):
    send_sems, recv_sems, lsems = sems
    locals_, xfers = plan
    for i, (src, _, landing, peer) in enumerate(xfers):
        pltpu.make_async_remote_copy(src_ref=src, dst_ref=landing, send_sem=send_sems.at[i], recv_sem=recv_sems.at[i],
                                     device_id=peer, device_id_type=MESH).wait_recv()
    for i, (src, dst, _, peer) in enumerate(xfers):
        pltpu.make_async_remote_copy(src_ref=src, dst_ref=dst, send_sem=send_sems.at[i], recv_sem=recv_sems.at[i],
                                     device_id=peer, device_id_type=MESH).wait_send()
    for i, (src, dst) in enumerate(locals_):
        pltpu.make_async_copy(src, dst, lsems.at[i]).wait()


def _xfer_sems(n_local, n_xfer):
    return [pltpu.SemaphoreType.DMA((n_xfer,)), pltpu.SemaphoreType.DMA((n_xfer,)), pltpu.SemaphoreType.DMA((n_local,))]


def _scatter_plan(ins, outs, pos):
    me = _slot(pos)
    locals_, xfers = [], []
    for src, dst in zip(ins, outs):
        locals_.append((src.at[me], dst.at[me]))
        for k in range(1, 8):
            peer = _flip(pos, k)
            xfers.append((src.at[_slot(peer)], dst.at[me], dst.at[_slot(peer)], peer))
    return locals_, xfers


def _sibling_exchange(halves, name):
    n = len(halves)

    def body(*refs):
        ins, outs = refs[:n], refs[n:2 * n]
        send_sems, recv_sems, lsems = refs[2 * n:]
        x, y, c = _my_pos()
        sib = (x, y, 1 - c)
        mine, sends = [], []
        for a in range(n):
            m = pltpu.make_async_copy(ins[a], outs[a].at[c], lsems.at[a])
            m.start()
            mine.append(m)
            cp = pltpu.make_async_remote_copy(
                src_ref=ins[a], dst_ref=outs[a].at[c], send_sem=send_sems.at[a], recv_sem=recv_sems.at[a],
                device_id=sib, device_id_type=MESH)
            cp.start()
            sends.append(cp)
        for a in range(n):
            pltpu.make_async_remote_copy(
                src_ref=ins[a], dst_ref=outs[a].at[1 - c], send_sem=send_sems.at[a], recv_sem=recv_sems.at[a],
                device_id=sib, device_id_type=MESH).wait_recv()
        for cp in sends:
            cp.wait_send()
        for m in mine:
            m.wait()

    vm = pl.BlockSpec(memory_space=pltpu.VMEM)
    return pl.pallas_call(
        body, name=name,
        out_shape=[jax.ShapeDtypeStruct((2,) + h.shape, h.dtype) for h in halves],
        in_specs=[vm] * n, out_specs=[vm] * n,
        scratch_shapes=[pltpu.SemaphoreType.DMA((n,)), pltpu.SemaphoreType.DMA((n,)), pltpu.SemaphoreType.DMA((n,))],
        compiler_params=_cp(vmem=VMEM_BIG),
    )(*halves)


def _row_tile(r, cap=512):
    for t in (cap, 256, 128, 64, 32, 16, 8):
        if t <= cap and r % t == 0:
            return t
    return r


def _cast_bf16(a, name):
    R, W = a.shape
    tr = _row_tile(R, 256)

    def body(a_ref, o_ref):
        o_ref[...] = a_ref[...].astype(BF16)

    return pl.pallas_call(
        body, name=name, grid=(R // tr,), out_shape=jax.ShapeDtypeStruct((R, W), BF16),
        in_specs=[pl.BlockSpec((tr, W), lambda i: (i, 0))], out_specs=pl.BlockSpec((tr, W), lambda i: (i, 0)),
    )(a)


def _sum_slots(r, name):
    _, R, W = r.shape
    tr = _row_tile(R, 128)

    def body(r_ref, o_ref):
        acc = r_ref[0].astype(F32)
        for s_ in range(1, 8):
            acc = acc + r_ref[s_].astype(F32)
        o_ref[...] = acc.astype(BF16)

    return pl.pallas_call(
        body, name=name, grid=(R // tr,), out_shape=jax.ShapeDtypeStruct((R, W), BF16),
        in_specs=[pl.BlockSpec((8, tr, W), lambda i: (0, i, 0))], out_specs=pl.BlockSpec((tr, W), lambda i: (i, 0)),
        compiler_params=_cp(vmem=VMEM_BIG),
    )(r)


def _adamw(w, g, m, v, name):
    R, W = w.shape
    tr = _row_tile(R, 256)
    tc = 256 if (tr == R and R * W * 4 > (1 << 21) and W % 256 == 0) else W
    c1 = 1.0 / (1.0 - ADAM_B1 ** ADAM_STEP)
    c2 = 1.0 / (1.0 - ADAM_B2 ** ADAM_STEP)

    def body(w_ref, g_ref, m_ref, v_ref, d_ref, nm_ref, nv_ref, g32_ref):
        gg = g_ref[...].astype(F32)
        g32_ref[...] = gg
        nm = ADAM_B1 * m_ref[...] + (1.0 - ADAM_B1) * gg
        nv = ADAM_B2 * v_ref[...] + (1.0 - ADAM_B2) * (gg * gg)
        d_ref[...] = -ADAM_LR * ((nm * c1) / (jnp.sqrt(nv * c2) + ADAM_EPS) + ADAM_WD * w_ref[...])
        nm_ref[...] = nm
        nv_ref[...] = nv

    spec = pl.BlockSpec((tr, tc), lambda i, j: (i, j))
    return pl.pallas_call(
        body, name=name, grid=(R // tr, W // tc), out_shape=[jax.ShapeDtypeStruct((R, W), F32)] * 4,
        in_specs=[spec] * 4, out_specs=[spec] * 4, compiler_params=_cp(vmem=VMEM_BIG),
    )(w, g, m, v)


def _adamw_hosting(items, carry):
    n, nc, steps = len(items), len(carry), 8
    c1 = 1.0 / (1.0 - ADAM_B1 ** ADAM_STEP)
    c2 = 1.0 / (1.0 - ADAM_B2 ** ADAM_STEP)

    def body(*refs):
        ins, c_in = refs[:4 * n], refs[4 * n:4 * n + nc]
        outs, c_out = refs[4 * n + nc:8 * n + nc], refs[8 * n + nc:8 * n + 2 * nc]
        sems = refs[8 * n + 2 * nc:]
        if nc:
            @pl.when(pl.program_id(0) == 0)
            def _():
                _xfer_start(_scatter_plan(c_in, c_out, _my_pos()), sems)

        for a in range(n):
            w_ref, g_ref, m_ref, v_ref = ins[4 * a:4 * a + 4]
            d_ref, nm_ref, nv_ref, g32_ref = outs[4 * a:4 * a + 4]
            gg = g_ref[...].astype(F32)
            g32_ref[...] = gg
            nm = ADAM_B1 * m_ref[...] + (1.0 - ADAM_B1) * gg
            nv = ADAM_B2 * v_ref[...] + (1.0 - ADAM_B2) * (gg * gg)
            d_ref[...] = -ADAM_LR * ((nm * c1) / (jnp.sqrt(nv * c2) + ADAM_EPS) + ADAM_WD * w_ref[...])
            nm_ref[...] = nm
            nv_ref[...] = nv

        if nc:
            @pl.when(pl.program_id(0) == steps - 1)
            def _():
                _xfer_wait(_scatter_plan(c_in, c_out, _my_pos()), sems)

    specs = [pl.BlockSpec((w.shape[0] // steps, w.shape[1]), lambda i: (i, 0)) for w, _, _, _ in items]
    hbm = pl.BlockSpec(memory_space=pltpu.HBM)
    res = pl.pallas_call(
        body, name="adamw_hosting", grid=(steps,),
        out_shape=[jax.ShapeDtypeStruct(w.shape, F32) for w, _, _, _ in items for _ in range(4)]
        + [jax.ShapeDtypeStruct(g.shape, g.dtype) for g in carry],
        in_specs=[sp for sp in specs for _ in range(4)] + [hbm] * nc,
        out_specs=[sp for sp in specs for _ in range(4)] + [hbm] * nc,
        scratch_shapes=_xfer_sems(nc, 7 * nc) if nc else [],
        compiler_params=_cp(vmem=VMEM_BIG),
    )(*[a for it in items for a in it], *carry)
    return [res[4 * a:4 * a + 4] for a in range(n)], list(res[4 * n:])


def _matmul_tn(a, b, tm, tn, tk, name, nn=1, b_map=None, out_shape=None, out_block=None, out_map=None,
               b_block=None):
    K, M = a.shape
    nk = K // tk
    nm = M // tm

    def body(a_ref, b_ref, o_ref, acc_ref):
        kk = pl.program_id(2)

        @pl.when(kk == 0)
        def _():
            acc_ref[...] = jnp.zeros_like(acc_ref)

        acc_ref[...] += lax.dot_general(a_ref[...], b_ref[...], _TN, preferred_element_type=F32)

        @pl.when(kk == nk - 1)
        def _():
            o_ref[...] = acc_ref[...].astype(BF16)

    return pl.pallas_call(
        body, name=name, grid=(nm, nn, nk),
        out_shape=jax.ShapeDtypeStruct(out_shape or (M, b.shape[-1]), BF16),
        in_specs=[pl.BlockSpec((tk, tm), lambda i, j, k: (k, i)),
                  pl.BlockSpec(b_block or (tk, tn), b_map or (lambda i, j, k: (k, j)))],
        out_specs=pl.BlockSpec(out_block or (tm, tn), out_map or (lambda i, j, k: (i, j))),
        scratch_shapes=[pltpu.VMEM((tm, tn), F32)],
        compiler_params=_cp(vmem=VMEM_BIG),
    )(a, b)


def _ada_fwd(cin, w_ada_s, b_s):
    tn = 512

    def body(c_ref, w_ref, b_ref, o_ref):
        o_ref[...] = _dg(_silu(c_ref[...]), w_ref[...], _NN) + b_ref[...]

    return pl.pallas_call(
        body, name="ada_fwd", grid=(1536 // tn,), out_shape=jax.ShapeDtypeStruct((64, 1536), F32),
        in_specs=[pl.BlockSpec((64, D), lambda j: (0, 0)), pl.BlockSpec((D, tn), lambda j: (0, j)),
                  pl.BlockSpec((1, tn), lambda j: (0, j))],
        out_specs=pl.BlockSpec((64, tn), lambda j: (0, j)),
        compiler_params=_cp(vmem=VMEM_BIG),
    )(cin, w_ada_s, b_s)


def _ada_bwd(cin, gmod_s, w_ada_s):
    tn = 512

    def body(c_ref, g_ref, w_ref, gw_ref, pc_ref):
        j = pl.program_id(0)
        g = g_ref[...]
        gw_ref[...] = _dg(_silu(c_ref[...]), g, _TN)
        rr = lax.broadcasted_iota(jnp.int32, (8, 64), 0)
        cc = lax.broadcasted_iota(jnp.int32, (8, 64), 1)
        sel = jnp.where((rr == 0) & ((cc & 7) == 2), 1.0, 0.0).astype(F32)
        dctx = _hmm(sel, g)
        part = _dg(dctx, w_ref[...], _NT)

        @pl.when(j == 0)
        def _():
            pc_ref[...] = jnp.zeros_like(pc_ref)

        pc_ref[...] += part

    return pl.pallas_call(
        body, name="ada_bwd", grid=(1536 // tn,),
        out_shape=[jax.ShapeDtypeStruct((D, 1536), F32), jax.ShapeDtypeStruct((8, D), F32)],
        in_specs=[pl.BlockSpec((64, D), lambda j: (0, 0)), pl.BlockSpec((64, tn), lambda j: (0, j)),
                  pl.BlockSpec((D, tn), lambda j: (0, j))],
        out_specs=[pl.BlockSpec((D, tn), lambda j: (0, j)), pl.BlockSpec((8, D), lambda j: (0, 0))],
        compiler_params=_cp(vmem=VMEM_BIG),
    )(cin, gmod_s, w_ada_s)


def _ada_small(gmod, parts, c_ctx):
    def body(g_ref, p_ref, c_ref, gb_ref, gc_ref):
        gb_ref[...] = jnp.sum(g_ref[...], axis=0, keepdims=True)
        ds = p_ref[0][0:1] + p_ref[2][0:1] + p_ref[4][0:1] + p_ref[6][0:1]
        cc = c_ref[...]
        s = _sigmoid(cc)
        gc_ref[...] = ds * (s * (1.0 + cc * (1.0 - s)))

    vm = pl.BlockSpec(memory_space=pltpu.VMEM)
    return pl.pallas_call(
        body, name="ada_small",
        out_shape=[jax.ShapeDtypeStruct((1, 6 * D), F32), jax.ShapeDtypeStruct((1, D), F32)],
        in_specs=[vm, vm, vm], out_specs=[vm, vm],
    )(gmod, parts, c_ctx)


def _tok_specs():
    xs = pl.BlockSpec((None, RT, D), lambda b, j: (b, jnp.maximum(j - 1, 0), 0))
    cs = pl.BlockSpec((None, RT, D), lambda b, j: (b, 0, 0))
    ms = pl.BlockSpec((None, 1, D), lambda b, j: (b, 0, 0))
    cms = pl.BlockSpec((None, 1, D), lambda b, j: (0, 0, 0))
    return xs, cs, ms, cms


def _inproj_fwd(x, ctx, sh1, sc1, csh1, csc1, n1w, w_in_p):
    def body(x_ref, c_ref, sh_ref, sc_ref, csh_ref, csc_ref, w1_ref, w_ref, p_ref, hm_ref):
        is_ctx = pl.program_id(1) == 0
        xt = jnp.where(is_ctx, c_ref[...], x_ref[...])
        sh = jnp.where(is_ctx, csh_ref[...], sh_ref[...])
        sc = jnp.where(is_ctx, csc_ref[...], sc_ref[...])
        hb = _norm_mod(xt, w1_ref[...], sh, sc).astype(BF16)
        hm_ref[...] = hb
        p_ref[...] = lax.dot_general(hb, w_ref[...], _NT, preferred_element_type=F32)

    xs, cs, ms, cms = _tok_specs()
    return pl.pallas_call(
        body, name="inproj_fwd", grid=(NB, T // RT),
        out_shape=[jax.ShapeDtypeStruct((NB, T, PC), F32), jax.ShapeDtypeStruct((NB, T, D), BF16)],
        in_specs=[xs, cs, ms, ms, cms, cms, pl.BlockSpec((1, D), lambda b, j: (0, 0)),
                  pl.BlockSpec((PC, D), lambda b, j: (0, 0))],
        out_specs=[pl.BlockSpec((None, RT, PC), lambda b, j: (b, j, 0)),
                   pl.BlockSpec((None, RT, D), lambda b, j: (b, j, 0))],
        compiler_params=_cp(vmem=VMEM_BIG),
    )(x, ctx, sh1, sc1, csh1, csc1, n1w, w_in_p)


def _inproj_bwd(dp_dn, dz, dp_na, dgt, w_in_p, x, ctx, sh1, sc1, csh1, csc1, n1w, dx1):
    def body(d1q_ref, d1k_ref, d1v_ref, d2_ref, d3_ref, d4_ref, w_ref, x_ref, c_ref, sh_ref, sc_ref, csh_ref, csc_ref,
             w1_ref, dx1_ref, dpb_ref, gx_ref, dsh_ref, dsc_ref, dcsh_ref, dcsc_ref, dw1_ref):
        b, j = pl.program_id(0), pl.program_id(1)
        is_ctx = j == 0
        dp = jnp.concatenate([d1q_ref[...], d1k_ref[...], d1v_ref[...], d2_ref[...], d3_ref[...], d4_ref[...]],
                             axis=1).astype(BF16)
        dpb_ref[...] = dp
        dhm = jnp.dot(dp, w_ref[...], preferred_element_type=F32)
        xt = jnp.where(is_ctx, c_ref[...], x_ref[...])
        sh = jnp.where(is_ctx, csh_ref[...], sh_ref[...])
        sc = jnp.where(is_ctx, csc_ref[...], sc_ref[...])
        _, vjp = jax.vjp(_norm_mod, xt, w1_ref[...], sh, sc)
        dxt, dw1, dsh, dsc = vjp(dhm)

        @pl.when((b == 0) & (j == 0))
        def _():
            dw1_ref[...] = jnp.zeros_like(dw1_ref)
            dcsh_ref[...] = jnp.zeros_like(dcsh_ref)
            dcsc_ref[...] = jnp.zeros_like(dcsc_ref)

        dw1_ref[...] += dw1

        @pl.when(is_ctx)
        def _():
            dcsh_ref[...] += dsh
            dcsc_ref[...] += dsc

        @pl.when(j == 1)
        def _():
            dsh_ref[...] = jnp.zeros_like(dsh_ref)
            dsc_ref[...] = jnp.zeros_like(dsc_ref)

        @pl.when(j >= 1)
        def _():
            dsh_ref[...] += dsh
            dsc_ref[...] += dsc
            gx_ref[...] = dx1_ref[...] + dxt

    xs, cs, ms, cms = _tok_specs()
    tsp = lambda w: pl.BlockSpec((None, RT, w), lambda b, j: (b, j, 0))
    one = pl.BlockSpec((1, D), lambda b, j: (0, 0))
    return pl.pallas_call(
        body, name="inproj_bwd", grid=(NB, T // RT),
        out_shape=[jax.ShapeDtypeStruct((NB, T, PC), BF16), jax.ShapeDtypeStruct((NB, SEQ, D), F32),
                   jax.ShapeDtypeStruct((NB, 1, D), F32), jax.ShapeDtypeStruct((NB, 1, D), F32),
                   jax.ShapeDtypeStruct((1, 1, D), F32), jax.ShapeDtypeStruct((1, 1, D), F32),
                   jax.ShapeDtypeStruct((1, D), F32)],
        in_specs=[tsp(512), tsp(512), tsp(512), tsp(512), tsp(1536), tsp(128), pl.BlockSpec((PC, D), lambda b, j: (0, 0)),
                  xs, cs, ms, ms, cms, cms, one, _rows2d_spec(D)],
        out_specs=[tsp(PC), xs, ms, ms, cms, cms, one],
        compiler_params=_cp(vmem=VMEM_BIG),
    )(*dp_dn, dz, dp_na, dgt, w_in_p, x, ctx, sh1, sc1, csh1, csc1, n1w, dx1)


def _shift_masks():
    t = lax.broadcasted_iota(jnp.int32, (T, 1), 0)
    lo = jnp.where(t < CT, 0, CT)
    hi = jnp.where(t < CT, CT, T)
    return {off: (t + off >= lo) & (t + off < hi) for off in range(-2, 3)}


def _shifted(u, off, masks):
    s = u if off == 0 else pltpu.roll(u, (-off) % T, 0)
    return jnp.where(masks[off], s, 0.0)


def _conv_pre(u, w, masks):
    acc = _shifted(u, -2, masks) * w[0:1, :]
    for kk in range(1, 5):
        acc = acc + _shifted(u, kk - 2, masks) * w[kk:kk + 1, :]
    return acc


def _act_q(pre):
    a = _silu(pre)
    return a * lax.rsqrt(jnp.sum(a * a, axis=-1, keepdims=True) + EPS) * (128.0 ** -0.5)


def _act_k(pre):
    a = _silu(pre)
    return a * lax.rsqrt(jnp.sum(a * a, axis=-1, keepdims=True) + EPS)


def _dn_prep_fwd(p, cw, carry=()):
    nc = len(carry)

    def body(pq_ref, pk_ref, pv_ref, wq_ref, wk_ref, wv_ref, *rest):
        c_in, (q_ref, k_ref, v_ref) = rest[:nc], rest[nc:nc + 3]
        c_out, sems = rest[nc + 3:2 * nc + 3], rest[2 * nc + 3:]
        b, h = pl.program_id(0), pl.program_id(1)
        if nc:
            @pl.when((b == 0) & (h == 0))
            def _():
                _xfer_start(_gather_plan(c_in, c_out, _my_pos()), sems)

        masks = _shift_masks()
        q_ref[...] = _act_q(_conv_pre(pq_ref[...], wq_ref[...], masks))
        k_ref[...] = _act_k(_conv_pre(pk_ref[...], wk_ref[...], masks))
        v_ref[...] = _silu(_conv_pre(pv_ref[...], wv_ref[...], masks))

        if nc:
            @pl.when((b == NB - 1) & (h == DNH - 1))
            def _():
                _xfer_wait(_gather_plan(c_in, c_out, _my_pos()), sems)

    ps = lambda o: pl.BlockSpec((None, T, 128), lambda b, h: (b, 0, o + h))
    ws = lambda o: pl.BlockSpec((8, 128), lambda b, h: (0, o + h))
    os_ = pl.BlockSpec((None, T, 128), lambda b, h: (b, 0, h))
    hbm = pl.BlockSpec(memory_space=pltpu.HBM)
    res = pl.pallas_call(
        body, name="dn_prep_fwd", grid=(NB, DNH),
        out_shape=[jax.ShapeDtypeStruct((NB, T, DNW), F32)] * 3
        + [jax.ShapeDtypeStruct((8,) + h_.shape, h_.dtype) for h_ in carry],
        in_specs=[ps(0), ps(4), ps(8), ws(0), ws(4), ws(8)] + [hbm] * nc, out_specs=[os_] * 3 + [hbm] * nc,
        scratch_shapes=_xfer_sems(nc, 7 * nc) if nc else [],
        compiler_params=_cp(vmem=VMEM_BIG),
    )(p, p, p, cw, cw, cw, *carry)
    return res[:3], list(res[3:])


def _dn_prep_bwd(p, cw, dqs, dks, dvs):
    def body(pq_ref, pk_ref, pv_ref, wq_ref, wk_ref, wv_ref, dqf_ref, dqb_ref, dkf_ref, dkb_ref, dvf_ref, dvb_ref,
             oq_ref, ok_ref, ov_ref, gq_ref, gk_ref, gv_ref):
        b = pl.program_id(1)
        masks = _shift_masks()

        def one(p_ref, w_ref, df_ref, db_ref, act, o_ref, g_ref):
            u = p_ref[...]
            w = w_ref[...]
            _, vjp = jax.vjp(act, _conv_pre(u, w, masks))
            (dpre,) = vjp(df_ref[...] + db_ref[...])
            du = _shifted(dpre, 2, masks) * w[0:1, :]
            rows = [jnp.sum(dpre * _shifted(u, -2, masks), axis=0, keepdims=True)]
            for kk in range(1, 5):
                du = du + _shifted(dpre, 2 - kk, masks) * w[kk:kk + 1, :]
                rows.append(jnp.sum(dpre * _shifted(u, kk - 2, masks), axis=0, keepdims=True))
            o_ref[...] = du
            gw = jnp.concatenate(rows + [jnp.zeros((3, 128), F32)], axis=0)

            @pl.when(b == 0)
            def _():
                g_ref[...] = gw

            @pl.when(b > 0)
            def _():
                g_ref[...] += gw

        one(pq_ref, wq_ref, dqf_ref, dqb_ref, _act_q, oq_ref, gq_ref)
        one(pk_ref, wk_ref, dkf_ref, dkb_ref, _act_k, ok_ref, gk_ref)
        one(pv_ref, wv_ref, dvf_ref, dvb_ref, _silu, ov_ref, gv_ref)

    ps = lambda o: pl.BlockSpec((None, T, 128), lambda h, b: (b, 0, o + h))
    ws = lambda o: pl.BlockSpec((8, 128), lambda h, b: (0, o + h))
    ds = pl.BlockSpec((None, T, 128), lambda h, b: (b, 0, h))
    outs = pl.pallas_call(
        body, name="dn_prep_bwd", grid=(DNH, NB),
        out_shape=[jax.ShapeDtypeStruct((NB, T, DNW), F32)] * 3 + [jax.ShapeDtypeStruct((8, DNW), F32)] * 3,
        in_specs=[ps(0), ps(4), ps(8), ws(0), ws(4), ws(8)] + [ds] * 6,
        out_specs=[ds] * 3
        + [pl.BlockSpec((8, 128), lambda h, b: (0, h))] * 3,
        compiler_params=_cp(vmem=VMEM_BIG),
    )(p, p, p, cw, cw, cw, *dqs, *dks, *dvs)
    return outs


def _gate_fn(a, prm):
    lane = lax.broadcasted_iota(jnp.int32, a.shape, 1)
    z = a + prm[1:2, :]
    sp = jnp.maximum(z, 0.0) + jnp.log(1.0 + jnp.exp(-jnp.abs(z)))
    g = -jnp.exp(prm[0:1, :]) * sp
    return jnp.where(lane < 8, g, jnp.where(lane < 16, _sigmoid(a), 0.0))


def _gates_fwd(p, prm):
    def body(a_ref, prm_ref, o_ref):
        o_ref[...] = _gate_fn(a_ref[...], prm_ref[...])

    return pl.pallas_call(
        body, name="gates_fwd", grid=(NB,), out_shape=jax.ShapeDtypeStruct((NB, T, 128), F32),
        in_specs=[pl.BlockSpec((None, T, 128), lambda b: (b, 0, COL_G // 128)), pl.BlockSpec((8, 128), lambda b: (0, 0))],
        out_specs=pl.BlockSpec((None, T, 128), lambda b: (b, 0, 0)),
    )(p, prm)


def _gates_bwd(p, prm, dgf, dgb):
    def body(a_ref, prm_ref, dgf_ref, dgb_ref, o_ref, dprm_ref):
        b = pl.program_id(0)
        tot = dgf_ref[...] + dgb_ref[...]
        _, vjp = jax.vjp(_gate_fn, a_ref[...], prm_ref[...])
        da, dprm = vjp(tot)
        o_ref[...] = da

        @pl.when(b == 0)
        def _():
            dprm_ref[...] = dprm

        @pl.when(b > 0)
        def _():
            dprm_ref[...] += dprm

    return pl.pallas_call(
        body, name="gates_bwd", grid=(NB,),
        out_shape=[jax.ShapeDtypeStruct((NB, T, 128), F32), jax.ShapeDtypeStruct((8, 128), F32)],
        in_specs=[pl.BlockSpec((None, T, 128), lambda b: (b, 0, COL_G // 128)), pl.BlockSpec((8, 128), lambda b: (0, 0)),
                  pl.BlockSpec((None, T, 128), lambda b: (b, 0, 0)), pl.BlockSpec((None, T, 128), lambda b: (b, 0, 0))],
        out_specs=[pl.BlockSpec((None, T, 128), lambda b: (b, 0, 0)), pl.BlockSpec((8, 128), lambda b: (0, 0))],
        compiler_params=_cp(vmem=VMEM_BIG),
    )(p, prm, dgf, dgb)


def _lockstep(gens, joint=None):
    n = len(gens)
    res, vals = [None] * n, [None] * n
    while True:
        msgs = []
        for i, g in enumerate(gens):
            try:
                msgs.append((g.send(vals[i]), False))
            except StopIteration as e:
                res[i] = e.value
                msgs.append((None, True))
        if msgs[0][1]:
            return res
        if msgs[0][0] is not None:
            vals = list(joint(tuple(m[0] for m in msgs)))
        else:
            vals = [None] * n


def _tri_inv_gen(m, md, m1, m2, eye):
    n = -(m * md)
    t = eye + n
    pw = n
    for _ in range(3):
        pw = _dg(pw, pw, _NN)
        yield
        t = t + _dg(t, pw, _NN)
        yield
    a = _dg(t, m * m1, _NN)
    yield
    t = t - _dg(a, t, _NN)
    yield
    a = _dg(t, m * m2, _NN)
    yield
    return t - _dg(a, t, _NN)


def _tri_inv_bwd_gen(t, g):
    tt = t.T
    a = _dg(tt, g, _NN)
    yield
    return -_dg(a, tt, _NN)


@jax.custom_vjp
def _tri_inv(ms, md, m1, m2, eye):
    return tuple(_lockstep([_tri_inv_gen(m, md, m1, m2, eye) for m in ms]))


def _tri_inv_f(ms, md, m1, m2, eye):
    ts = _tri_inv(ms, md, m1, m2, eye)
    return ts, (ts, md)


def _tri_inv_b(res, gs):
    ts, md = res
    z = jnp.zeros_like(md)
    return (tuple(_lockstep([_tri_inv_bwd_gen(t, g) for t, g in zip(ts, gs)])), z, z, z, z)


_tri_inv.defvjp(_tri_inv_f, _tri_inv_b)


@jax.custom_vjp
def _tri_inv_saved(ms, ts):
    return tuple(t.astype(F32) for t in ts)


def _tri_inv_saved_f(ms, ts):
    return _tri_inv_saved(ms, ts), ts


def _tri_inv_saved_b(ts, gs):
    dms = tuple(_lockstep([_tri_inv_bwd_gen(t, g) for t, g in zip(ts, gs)]))
    return dms, tuple(jnp.zeros_like(t) for t in ts)


_tri_inv_saved.defvjp(_tri_inv_saved_f, _tri_inv_saved_b)


SR = DNH * CH


def _scan_masks():
    i, j = np.arange(SR)[:, None], np.arange(SR)[None, :]
    same = (i >> 6) == (j >> 6)
    cm = np.zeros((2, 6, SR, SR), np.float32)
    for d in range(2):
        diff = i - j if d == 0 else j - i
        cm[d, 0] = same & (diff >= 0)
        cm[d, 1] = same & (diff > 0)
        cm[d, 2] = (i >> 4) == (j >> 4)
        cm[d, 3] = ((i >> 5) == (j >> 5)) & ((i >> 4) != (j >> 4))
        cm[d, 4] = (i >> 5) != (j >> 5)
        cm[d, 5] = i == j
    t, u = np.arange(CH)[:, None], np.arange(CH)[None, :]
    c64 = np.stack([t >= u, t <= u]).astype(np.float32)
    return jnp.asarray(cm), jnp.asarray(c64)


def _hsl(h):
    return slice(CH * h, CH * h + CH), slice(128 * h, 128 * h + 128)


@jax.custom_vjp
def _heads_mm(x, s):
    return jnp.concatenate([_dg(x[_hsl(h)[0]], s[:, _hsl(h)[1]], _NN) for h in range(DNH)], axis=0)


def _heads_mm_b(res, g):
    x, s = res
    dx = jnp.concatenate([_dg(g[_hsl(h)[0]], s[:, _hsl(h)[1]], _NT) for h in range(DNH)], axis=0)
    ds = jnp.concatenate([_dg(x[_hsl(h)[0]], g[_hsl(h)[0]], _TN) for h in range(DNH)], axis=1)
    return dx, ds


_heads_mm.defvjp(lambda x, s: (_heads_mm(x, s), (x, s)), _heads_mm_b)


@jax.custom_vjp
def _heads_outer(k, v):
    return jnp.concatenate([_dg(k[_hsl(h)[0]], v[_hsl(h)[0]], _TN) for h in range(DNH)], axis=1)


def _heads_outer_b(res, g):
    k, v = res
    dk = jnp.concatenate([_dg(v[_hsl(h)[0]], g[:, _hsl(h)[1]], _NT) for h in range(DNH)], axis=0)
    dv = jnp.concatenate([_dg(k[_hsl(h)[0]], g[:, _hsl(h)[1]], _NN) for h in range(DNH)], axis=0)
    return dk, dv


_heads_outer.defvjp(lambda k, v: (_heads_outer(k, v), (k, v)), _heads_outer_b)


@jax.custom_vjp
def _mm_shared_lhs(t, a, b):
    r = _dg(t, jnp.concatenate([a, b], axis=1), _NN)
    return r[:, :a.shape[1]], r[:, a.shape[1]:]


def _mm_shared_lhs_b(res, gs):
    t, a, b = res
    g = jnp.concatenate(gs, axis=1)
    dab = _dg(t, g, _TN)
    return _dg(g, jnp.concatenate([a, b], axis=1), _NT), dab[:, :a.shape[1]], dab[:, a.shape[1]:]


_mm_shared_lhs.defvjp(lambda t, a, b: (_mm_shared_lhs(t, a, b), (t, a, b)), _mm_shared_lhs_b)


@jax.custom_vjp
def _mm_nt_shared_rhs(a, b, k):
    r = _dg(jnp.concatenate([a, b], axis=0), k, _NT)
    return r[:a.shape[0]], r[a.shape[0]:]


def _mm_nt_shared_rhs_b(res, gs):
    a, b, k = res
    g = jnp.concatenate(gs, axis=0)
    dab = _dg(g, k, _NN)
    return dab[:a.shape[0]], dab[a.shape[0]:], _dg(g, jnp.concatenate([a, b], axis=0), _TN)


_mm_nt_shared_rhs.defvjp(lambda a, b, k: (_mm_nt_shared_rhs(a, b, k), (a, b, k)), _mm_nt_shared_rhs_b)


def _chunk_gen(q, k, v, gb, s, cm, c64, d):
    incl, strict = cm[0], cm[1]
    lane = lax.broadcasted_iota(jnp.int32, (1, 128), 1)
    oh = lambda n: jnp.where(lane == n, 1.0, 0.0).astype(F32)
    col = lambda x, base: jnp.concatenate(
        [jnp.sum(x * oh(base + d * DNH + h), axis=1, keepdims=True) for h in range(DNH)], axis=0)
    beta = col(gb, 8)
    gam = col(_mask_mm(c64, gb), 0)
    yield
    tot = jnp.sum(gb, axis=0, keepdims=True)
    gtots = [jnp.sum(tot * oh(d * DNH + h), axis=1, keepdims=True) for h in range(DNH)]
    gtot = jnp.concatenate([jnp.broadcast_to(t, (CH, 1)) for t in gtots], axis=0)
    eg = jnp.concatenate([jnp.broadcast_to(jnp.exp(t), (1, 128)) for t in gtots], axis=1)
    row0 = jnp.where(lax.broadcasted_iota(jnp.int32, (128, 1), 0) == 0, 1.0, 0.0).astype(F32)
    gam_row = jnp.sum(jnp.broadcast_to(gam, (SR, 128)).T * row0, axis=0, keepdims=True)
    dec = jnp.exp(jnp.where(incl > 0.5, gam - gam_row, NEG))
    e_gam = jnp.exp(gam)
    kb = k * beta
    kk, qk = _mm_nt_shared_rhs(kb, q, k)
    yield
    tm = yield strict * (kk * dec)
    u, w = _mm_shared_lhs(tm, v * beta, kb * e_gam)
    yield
    qk = qk * dec
    ws = _heads_mm(w, s)
    yield
    v_new = u - ws
    qs = _heads_mm(q * e_gam, s)
    yield
    o = qs + _mm(qk, v_new)
    yield
    s_new = s * eg + _heads_outer(k * jnp.exp(gtot - gam), v_new)
    return o, s_new, tm


def _chunks(args, cm_ref, c64_ref, saved=None):
    cm0 = cm_ref[0]
    if saved is None:
        inv = lambda ms: _tri_inv(ms, cm0[2], cm0[3], cm0[4], cm0[5])
    else:
        inv = lambda ms: _tri_inv_saved(ms, saved)
    return _lockstep([_chunk_gen(q, k, v, gb, s, cm_ref[d], c64_ref[d], d) for q, k, v, gb, s, d in args], inv)


def _chunk_rev(s):
    return jnp.where(s < NCH_CTX, NCH_CTX - 1 - s, NCH + NCH_CTX - 1 - s)


def _scan_specs(step):
    cf = lambda s: step(s)
    cb = lambda s: _chunk_rev(step(s))
    tok = lambda c: pl.BlockSpec((NB, CH, DNW), lambda s: (0, c(s), 0))
    gat = lambda c: pl.BlockSpec((NB, CH, 128), lambda s: (0, c(s), 0))
    sta = lambda c: pl.BlockSpec((NB, None, 128, DNW), lambda s: (0, c(s), 0, 0))
    inv = lambda c: pl.BlockSpec((NB, None, SR, SR), lambda s: (0, c(s), 0, 0))
    return (tok(cf), gat(cf), sta(cf), inv(cf)), (tok(cb), gat(cb), sta(cb), inv(cb))


def _const_specs():
    return [pl.BlockSpec((2, 6, SR, SR), lambda s: (0, 0, 0, 0)), pl.BlockSpec((2, CH, CH), lambda s: (0, 0, 0))]


SCAN_CHAINS = [(b, d) for b in range(NB) for d in range(2)]


def _stack(ref, b):
    return jnp.concatenate([ref[b, :, 128 * h:128 * h + 128] for h in range(DNH)], axis=0)


def _unstack_to(ref, b, val):
    for h in range(DNH):
        ref[b, :, 128 * h:128 * h + 128] = val[CH * h:CH * h + CH]


def _gather_plan(ins, outs, pos):
    me = _slot(pos)
    locals_, xfers = [], []
    for src, dst in zip(ins, outs):
        locals_.append((src, dst.at[me]))
        for k in range(1, 8):
            peer = _flip(pos, k)
            xfers.append((src, dst.at[me], dst.at[_slot(peer)], peer))
    return locals_, xfers


def _scan_fwd(q, k, v, gates, carry=()):
    nc = len(carry)

    def body(*refs):
        cm_ref, c64_ref, qf, kf, vf, gf, qb, kb, vb, gb = refs[:10]
        c_in = refs[10:10 + nc]
        of_ref, sf_ref, tf_ref, ob_ref, sb_ref, tb_ref = refs[10 + nc:16 + nc]
        c_out = refs[16 + nc:16 + 2 * nc]
        s_ref = refs[16 + 2 * nc]
        sems = refs[17 + 2 * nc:]
        if nc:
            @pl.when(pl.program_id(0) == 0)
            def _():
                _xfer_start(_gather_plan(c_in, c_out, _my_pos()), sems)

        @pl.when(pl.program_id(0) == 0)
        def _():
            s_ref[...] = jnp.zeros_like(s_ref)

        dirs = ((qf, kf, vf, gf, of_ref, sf_ref, tf_ref), (qb, kb, vb, gb, ob_ref, sb_ref, tb_ref))
        args = [(_stack(dirs[d][0], b), _stack(dirs[d][1], b), _stack(dirs[d][2], b), dirs[d][3][b], s_ref[i], d)
                for i, (b, d) in enumerate(SCAN_CHAINS)]
        for i, (o, s_new, tm) in enumerate(_chunks(args, cm_ref, c64_ref)):
            b, d = SCAN_CHAINS[i]
            dirs[d][5][b] = args[i][4]
            dirs[d][6][b] = tm.astype(BF16)
            _unstack_to(dirs[d][4], b, o)
            s_ref[i] = s_new

        if nc:
            @pl.when(pl.program_id(0) == NCH - 1)
            def _():
                _xfer_wait(_gather_plan(c_in, c_out, _my_pos()), sems)

    (tf, gf_, sf_, if_), (tb, gb_, sb_, ib_) = _scan_specs(lambda s: s)
    o_sh = jax.ShapeDtypeStruct((NB, T, DNW), F32)
    st_sh = jax.ShapeDtypeStruct((NB, NCH, 128, DNW), F32)
    inv_sh = jax.ShapeDtypeStruct((NB, NCH, SR, SR), BF16)
    hbm = pl.BlockSpec(memory_space=pltpu.HBM)
    return pl.pallas_call(
        body, name="scan_fwd", grid=(NCH,),
        out_shape=[o_sh, st_sh, inv_sh, o_sh, st_sh, inv_sh] + [jax.ShapeDtypeStruct((8,) + h.shape, h.dtype) for h in carry],
        in_specs=_const_specs() + [tf, tf, tf, gf_, tb, tb, tb, gb_] + [hbm] * nc,
        out_specs=[tf, sf_, if_, tb, sb_, ib_] + [hbm] * nc,
        scratch_shapes=[pltpu.VMEM((len(SCAN_CHAINS), 128, DNW), F32)] + (_xfer_sems(nc, 7 * nc) if nc else []),
        compiler_params=_cp(vmem=VMEM_BIG),
    )(*_scan_masks(), q, k, v, gates, q, k, v, gates, *carry)


def _scan_bwd(q, k, v, gates, st_f, st_b, inv_f, inv_b, do, carry=()):
    nc = len(carry)

    def body(*refs):
        (cm_ref, c64_ref, qf, kf, vf, gf, sf, tf_, dof, qb, kb, vb, gb, sb, tb_, dob) = refs[:16]
        c_in = refs[16:16 + nc]
        dqf, dkf, dvf, dgf, dqb, dkb, dvb, dgb = refs[16 + nc:24 + nc]
        c_out = refs[24 + nc:24 + 2 * nc]
        ds_ref = refs[24 + 2 * nc]
        sems = refs[25 + 2 * nc:]
        first = pl.program_id(0) == 0
        last = pl.program_id(0) == NCH - 1
        if nc:
            @pl.when(first)
            def _():
                _xfer_start(_scatter_plan(c_in, c_out, _my_pos()), sems)

        @pl.when(first)
        def _():
            ds_ref[...] = jnp.zeros_like(ds_ref)

        dirs = ((qf, kf, vf, gf, sf, dof, dqf, dkf, dvf, dgf), (qb, kb, vb, gb, sb, dob, dqb, dkb, dvb, dgb))
        prim = tuple((_stack(dirs[d][0], b), _stack(dirs[d][1], b), _stack(dirs[d][2], b), dirs[d][3][b], dirs[d][4][b])
                     for b, d in SCAN_CHAINS)
        cots = tuple((_stack(dirs[d][5], b), ds_ref[i]) for i, (b, d) in enumerate(SCAN_CHAINS))
        saved = tuple((tf_, tb_)[d][b] for b, d in SCAN_CHAINS)
        _, vjp = jax.vjp(lambda ps: tuple(r[:2] for r in _chunks([p + (SCAN_CHAINS[i][1],) for i, p in enumerate(ps)],
                                                                cm_ref, c64_ref, saved)), prim)
        (grads,) = vjp(cots)
        for i, (dq, dk, dv, dg, ds) in enumerate(grads):
            b, d = SCAN_CHAINS[i]
            _unstack_to(dirs[d][6], b, dq)
            _unstack_to(dirs[d][7], b, dk)
            _unstack_to(dirs[d][8], b, dv)
            dirs[d][9][b] = dg
            ds_ref[i] = ds

        if nc:
            @pl.when(last)
            def _():
                _xfer_wait(_scatter_plan(c_in, c_out, _my_pos()), sems)

    (tf, gf_, sf_, if_), (tb, gb_, sb_, ib_) = _scan_specs(lambda s: NCH - 1 - s)
    o_sh = jax.ShapeDtypeStruct((NB, T, DNW), F32)
    g_sh = jax.ShapeDtypeStruct((NB, T, 128), F32)
    hbm = pl.BlockSpec(memory_space=pltpu.HBM)
    return pl.pallas_call(
        body, name="scan_bwd", grid=(NCH,),
        out_shape=[o_sh, o_sh, o_sh, g_sh] * 2 + [jax.ShapeDtypeStruct(g.shape, g.dtype) for g in carry],
        in_specs=_const_specs() + [tf, tf, tf, gf_, sf_, if_, tf, tb, tb, tb, gb_, sb_, ib_, tb] + [hbm] * nc,
        out_specs=[tf, tf, tf, gf_, tb, tb, tb, gb_] + [hbm] * nc,
        scratch_shapes=[pltpu.VMEM((len(SCAN_CHAINS), 128, DNW), F32)] + (_xfer_sems(nc, 7 * nc) if nc else []),
        compiler_params=_cp(vmem=VMEM_BIG),
    )(*_scan_masks(), q, k, v, gates, st_f, inv_f, do, q, k, v, gates, st_b, inv_b, do, *carry)


def _segmean(a, bd):
    hi = a.astype(BF16)
    lo = (a - hi.astype(F32)).astype(BF16)
    return jnp.dot(hi, bd, preferred_element_type=F32) + jnp.dot(lo, bd, preferred_element_type=F32)


def _na_prep_fwd(p, qw, kw, bd):
    def body(q_ref, k_ref, v_ref, qw_ref, kw_ref, bd_ref, oq_ref, ok_ref, ov_ref):
        bdm = bd_ref[...]
        q = q_ref[...]
        k = k_ref[...]
        oq_ref[...] = (q * lax.rsqrt(_segmean(q * q, bdm) + EPS) * qw_ref[...] * 0.125).astype(BF16)
        ok_ref[...] = (k * lax.rsqrt(_segmean(k * k, bdm) + EPS) * kw_ref[...]).astype(BF16)
        ov_ref[...] = v_ref[...].astype(BF16)

    ps = lambda o: pl.BlockSpec((None, RT, NAW), lambda b, j: (b, j, COL_NA // NAW + o))
    one = pl.BlockSpec((1, NAW), lambda b, j: (0, 0))
    os_ = pl.BlockSpec((None, RT, NAW), lambda b, j: (b, j, 0))
    return pl.pallas_call(
        body, name="na_prep_fwd", grid=(NB, T // RT), out_shape=[jax.ShapeDtypeStruct((NB, T, NAW), BF16)] * 3,
        in_specs=[ps(0), ps(1), ps(2), one, one, pl.BlockSpec((NAW, NAW), lambda b, j: (0, 0))], out_specs=[os_] * 3,
    )(p, p, p, qw, kw, bd)


def _na_prep_bwd(p, qw, kw, bd, dq, dk, dv):
    def body(q_ref, k_ref, qw_ref, kw_ref, bd_ref, dq_ref, dk_ref, dv_ref, o_ref, gq_ref, gk_ref):
        b, j = pl.program_id(0), pl.program_id(1)
        bdm = bd_ref[...]

        def one(x, w, dy, scale):
            r = lax.rsqrt(_segmean(x * x, bdm) + EPS)
            g = dy * w * scale
            dx = r * g - x * (r * r * r) * _segmean(g * x, bdm)
            dw = jnp.sum(dy * x * r * scale, axis=0, keepdims=True)
            return dx, dw

        dqv = jnp.where(j == 0, 0.0, dq_ref[...])
        dxq, dwq = one(q_ref[...], qw_ref[...], dqv, 0.125)
        dxk, dwk = one(k_ref[...], kw_ref[...], dk_ref[...], 1.0)
        o_ref[...] = jnp.concatenate([dxq, dxk, dv_ref[...]], axis=1)
        pad = jnp.zeros((7, NAW), F32)

        @pl.when((b == 0) & (j == 0))
        def _():
            gq_ref[...] = jnp.zeros_like(gq_ref)
            gk_ref[...] = jnp.zeros_like(gk_ref)

        gq_ref[...] += jnp.concatenate([dwq, pad], axis=0)
        gk_ref[...] += jnp.concatenate([dwk, pad], axis=0)

    ps = lambda o: pl.BlockSpec((None, RT, NAW), lambda b, j: (b, j, COL_NA // NAW + o))
    one = pl.BlockSpec((1, NAW), lambda b, j: (0, 0))
    ts = pl.BlockSpec((None, RT, NAW), lambda b, j: (b, j, 0))
    acc = pl.BlockSpec((8, NAW), lambda b, j: (0, 0))
    return pl.pallas_call(
        body, name="na_prep_bwd", grid=(NB, T // RT),
        out_shape=[jax.ShapeDtypeStruct((NB, T, 1536), F32), jax.ShapeDtypeStruct((8, NAW), F32),
                   jax.ShapeDtypeStruct((8, NAW), F32)],
        in_specs=[ps(0), ps(1), one, one, pl.BlockSpec((NAW, NAW), lambda b, j: (0, 0)),
                  pl.BlockSpec((None, RT, NAW), lambda b, j: (b, jnp.maximum(j - 1, 0), 0)), ts, ts],
        out_specs=[pl.BlockSpec((None, RT, 1536), lambda b, j: (b, j, 0)), acc, acc],
        compiler_params=_cp(vmem=VMEM_BIG),
    )(p, p, qw, kw, bd, dq, dk, dv)


def _na_row0(r):
    return jnp.clip(r - WIN_ROWS // 2, 0, ROWS - WIN_ROWS)


def _na_cls(r):
    return _na_row0(r) - r + WIN_ROWS - 1


NHS = 4
NLW = NHS * 64
NSR = NHS * GRID_W


def _na_stack(x):
    lane = lax.broadcasted_iota(jnp.int32, (1, NLW), 1)
    return jnp.concatenate([jnp.where((lane >> 6) == h, x, jnp.zeros_like(x)) for h in range(NHS)], axis=0)


def _na_unstack(y):
    lane = lax.broadcasted_iota(jnp.int32, (1, NLW), 1)
    out = jnp.zeros((GRID_W, NLW), y.dtype)
    for h in range(NHS):
        out = jnp.where((lane >> 6) == h, y[GRID_W * h:GRID_W * h + GRID_W], out)
    return out


def _na_fwd_gen(q, kw, vw, kc, vc, bias):
    q4 = _na_stack(q)
    s_w = lax.dot_general(q4, kw, _NT, preferred_element_type=F32) + bias
    yield
    s_c = lax.dot_general(q4, kc, _NT, preferred_element_type=F32)
    yield
    mx = jnp.maximum(jnp.max(s_w, axis=1, keepdims=True), jnp.max(s_c, axis=1, keepdims=True))
    p_w = jnp.exp(s_w - mx)
    p_c = jnp.exp(s_c - mx)
    inv = 1.0 / (jnp.sum(p_w, axis=1, keepdims=True) + jnp.sum(p_c, axis=1, keepdims=True))
    yield
    o_w = jnp.dot(p_w.astype(BF16), vw, preferred_element_type=F32)
    yield
    o4 = (o_w + jnp.dot(p_c.astype(BF16), vc, preferred_element_type=F32)) * inv
    return _na_unstack(o4)


NRF = 2


def _na_fwd(qn, kn, vb, bias, carry=()):
    nc = len(carry)
    grid = (NB, ROWS // NRF)
    ng = NAH // NHS

    def body(*refs):
        q_ref, k_ref, v_ref = refs[:3]
        b_refs, rest = refs[3:3 + NRF], refs[3 + NRF:]
        c_in, o_ref, c_out, sems = rest[:nc], rest[nc], rest[nc + 1:2 * nc + 1], rest[2 * nc + 1:]
        ids = [pl.program_id(a) for a in range(2)]
        if nc:
            @pl.when((ids[0] == 0) & (ids[1] == 0))
            def _():
                _xfer_start(_gather_plan(c_in, c_out, _my_pos()), sems)

        gens = []
        for i in range(NRF):
            start = pl.multiple_of(CT + _na_row0(NRF * ids[1] + i) * GRID_W, GRID_W)
            rw = slice(GRID_W * i, GRID_W * (i + 1))
            for g in range(ng):
                ln = slice(NLW * g, NLW * g + NLW)
                gens.append(_na_fwd_gen(q_ref[rw, ln], k_ref[pl.ds(start, NWIN), ln], v_ref[pl.ds(start, NWIN), ln],
                                        k_ref[0:CT, ln], v_ref[0:CT, ln], b_refs[i][g]))
        for n, o in enumerate(_lockstep(gens)):
            i, g = n // ng, n % ng
            o_ref[GRID_W * i:GRID_W * (i + 1), NLW * g:NLW * g + NLW] = o.astype(BF16)

        if nc:
            @pl.when((ids[0] == grid[0] - 1) & (ids[1] == grid[1] - 1))
            def _():
                _xfer_wait(_gather_plan(c_in, c_out, _my_pos()), sems)

    qs = pl.BlockSpec((None, NRF * GRID_W, NAW), lambda b, j: (b, CT // (NRF * GRID_W) + j, 0))
    kv = pl.BlockSpec((None, T, NAW), lambda b, j: (b, 0, 0))
    bss = [pl.BlockSpec((None, ng, NSR, NWIN), lambda b, j, i=i: (_na_cls(NRF * j + i), 0, 0, 0)) for i in range(NRF)]
    lat = pl.BlockSpec((None, NRF * GRID_W, NAW), lambda b, j: (b, j, 0))
    hbm = pl.BlockSpec(memory_space=pltpu.HBM)
    res = pl.pallas_call(
        body, name="na_fwd", grid=grid,
        out_shape=[jax.ShapeDtypeStruct((NB, SEQ, NAW), BF16)] + [jax.ShapeDtypeStruct((8,) + h.shape, h.dtype) for h in carry],
        in_specs=[qs, kv, kv] + bss + [hbm] * nc, out_specs=[lat] + [hbm] * nc,
        scratch_shapes=_xfer_sems(nc, 7 * nc) if nc else [],
        compiler_params=_cp(vmem=VMEM_BIG),
    )(qn, kn, vb, *([bias] * NRF), *carry)
    return res[0], list(res[1:])


def _na_bwd_gen(q, kw, vw, kc, vc, bias, do):
    q4 = _na_stack(q)
    s_w = lax.dot_general(q4, kw, _NT, preferred_element_type=F32) + bias
    yield
    s_c = lax.dot_general(q4, kc, _NT, preferred_element_type=F32)
    yield
    mx = jnp.maximum(jnp.max(s_w, axis=1, keepdims=True), jnp.max(s_c, axis=1, keepdims=True))
    p_w = jnp.exp(s_w - mx)
    p_c = jnp.exp(s_c - mx)
    inv = 1.0 / (jnp.sum(p_w, axis=1, keepdims=True) + jnp.sum(p_c, axis=1, keepdims=True))
    p_w = p_w * inv
    p_c = p_c * inv
    do4 = _na_stack(do.astype(BF16))
    dp_w = lax.dot_general(do4, vw, _NT, preferred_element_type=F32)
    yield
    dp_c = lax.dot_general(do4, vc, _NT, preferred_element_type=F32)
    yield
    delta = jnp.sum(p_w * dp_w, axis=1, keepdims=True) + jnp.sum(p_c * dp_c, axis=1, keepdims=True)
    ds_w = p_w * (dp_w - delta)
    ds_c = p_c * (dp_c - delta)
    ds_wb, ds_cb = ds_w.astype(BF16), ds_c.astype(BF16)
    dq4 = jnp.dot(ds_wb, kw, preferred_element_type=F32)
    yield
    dq4 = dq4 + jnp.dot(ds_cb, kc, preferred_element_type=F32)
    yield
    dkw = lax.dot_general(ds_wb, q4, _TN, preferred_element_type=F32)
    yield
    dvw = lax.dot_general(p_w.astype(BF16), do4, _TN, preferred_element_type=F32)
    yield
    dkc = lax.dot_general(ds_cb, q4, _TN, preferred_element_type=F32)
    yield
    dvc = lax.dot_general(p_c.astype(BF16), do4, _TN, preferred_element_type=F32)
    return ds_w, _na_unstack(dq4), dkw, dvw, dkc, dvc


NRS = 4


def _na_bwd(qn, kn, vb, bias, do):
    def body(q_ref, k_ref, v_ref, *rest):
        b_refs, (do_ref, dq_ref, dk_ref, dv_ref, db_ref) = rest[:NRS], rest[NRS:]
        b, j = pl.program_id(1), pl.program_id(2)

        @pl.when((b == 0) & (j == 0))
        def _():
            db_ref[...] = jnp.zeros_like(db_ref)

        @pl.when(j == 0)
        def _():
            dk_ref[...] = jnp.zeros_like(dk_ref)
            dv_ref[...] = jnp.zeros_like(dv_ref)

        kc, vc = k_ref[0:CT, :], v_ref[0:CT, :]
        rows = [NRS * j + i for i in range(NRS)]
        starts = [pl.multiple_of(CT + _na_row0(r) * GRID_W, GRID_W) for r in rows]
        gens = [_na_bwd_gen(q_ref[GRID_W * i:GRID_W * (i + 1), :], k_ref[pl.ds(starts[i], NWIN), :],
                            v_ref[pl.ds(starts[i], NWIN), :], kc, vc, b_refs[i][...],
                            do_ref[GRID_W * i:GRID_W * (i + 1), :]) for i in range(NRS)]
        for i, (ds_w, dq, dkw, dvw, dkc, dvc) in enumerate(_lockstep(gens)):
            db_ref[_na_cls(rows[i])] += ds_w
            dq_ref[GRID_W * i:GRID_W * (i + 1), :] = dq
            dk_ref[pl.ds(starts[i], NWIN), :] += dkw
            dv_ref[pl.ds(starts[i], NWIN), :] += dvw
            dk_ref[0:CT, :] += dkc
            dv_ref[0:CT, :] += dvc

    qs = pl.BlockSpec((None, NRS * GRID_W, NLW), lambda g, b, j: (b, CT // (NRS * GRID_W) + j, g))
    kv = pl.BlockSpec((None, T, NLW), lambda g, b, j: (b, 0, g))
    bss = [pl.BlockSpec((None, None, NSR, NWIN), lambda g, b, j, i=i: (_na_cls(NRS * j + i), g, 0, 0)) for i in range(NRS)]
    lat = pl.BlockSpec((None, NRS * GRID_W, NLW), lambda g, b, j: (b, j, g))
    return pl.pallas_call(
        body, name="na_bwd", grid=(NAH // NHS, NB, ROWS // NRS),
        out_shape=[jax.ShapeDtypeStruct((NB, SEQ, NAW), F32)] + [jax.ShapeDtypeStruct((NB, T, NAW), F32)] * 2
        + [jax.ShapeDtypeStruct((WIN_ROWS, NAH // NHS, NSR, NWIN), F32)],
        in_specs=[qs, kv, kv] + bss + [lat],
        out_specs=[lat, kv, kv, pl.BlockSpec((WIN_ROWS, None, NSR, NWIN), lambda g, b, j: (0, g, 0, 0))],
        compiler_params=_cp(vmem=VMEM_BIG),
    )(qn, kn, vb, *([bias] * NRS), do)


def _na_tables():
    cols = np.arange(GRID_W)
    win_start = np.clip(cols - WIN_COLS // 2, 0, GRID_W - WIN_COLS)
    kc = cols[None, :]
    valid = (kc >= win_start[:, None]) & (kc < win_start[:, None] + WIN_COLS)
    rel = np.clip(kc - cols[:, None] + WIN_COLS - 1, 0, 2 * WIN_COLS - 2)
    return valid, rel


def _bias_table(rpb):
    valid, rel = _na_tables()
    nrel = 2 * WIN_COLS - 1
    shifts = np.stack([(rel == r) & valid for r in range(nrel)]).astype(np.float32)
    neg = np.where(valid, 0.0, NEG).astype(np.float32)

    def body(rpb_ref, sh_ref, neg_ref, o_ref):
        h = pl.program_id(0)
        for dr in range(2 * WIN_ROWS - 1):
            tile = neg_ref[...]
            for r in range(nrel):
                tile = tile + rpb_ref[h, dr, r] * sh_ref[r]
            for cls in range(WIN_ROWS):
                i = dr - cls
                if 0 <= i < WIN_ROWS:
                    o_ref[cls, :, i * GRID_W:(i + 1) * GRID_W] = tile

    return pl.pallas_call(
        body, name="bias_table", grid=(NAH,),
        out_shape=jax.ShapeDtypeStruct((WIN_ROWS, NAH // NHS, NSR, NWIN), F32),
        in_specs=[pl.BlockSpec(memory_space=pltpu.SMEM), pl.BlockSpec((nrel, GRID_W, GRID_W), lambda h: (0, 0, 0)),
                  pl.BlockSpec((GRID_W, GRID_W), lambda h: (0, 0))],
        out_specs=pl.BlockSpec((WIN_ROWS, None, GRID_W, NWIN), lambda h: (0, h // NHS, h % NHS, 0)),
    )(rpb, jnp.asarray(shifts), jnp.asarray(neg))


def _rpb_fold(db):
    def body1(d_ref, o_ref):
        for dr in range(2 * WIN_ROWS - 1):
            acc = None
            for c in range(WIN_ROWS):
                i = dr - c
                if 0 <= i < WIN_ROWS:
                    blk = d_ref[c, :, i * GRID_W:(i + 1) * GRID_W]
                    acc = blk if acc is None else acc + blk
            o_ref[dr] = acc

    t15 = pl.pallas_call(
        body1, name="rpb_fold_rows", grid=(NAH,),
        out_shape=jax.ShapeDtypeStruct((NAH, 2 * WIN_ROWS - 1, GRID_W, GRID_W), F32),
        in_specs=[pl.BlockSpec((WIN_ROWS, None, GRID_W, NWIN), lambda h: (0, h, 0, 0))],
        out_specs=pl.BlockSpec((None, 2 * WIN_ROWS - 1, GRID_W, GRID_W), lambda h: (h, 0, 0, 0)),
        compiler_params=_cp(vmem=VMEM_BIG),
    )(db.reshape(WIN_ROWS, NAH, GRID_W, NWIN))
    valid, rel = _na_tables()
    onehot = np.zeros((GRID_W * GRID_W, 128), np.float32)
    flat_rel, flat_valid = rel.reshape(-1), valid.reshape(-1)
    onehot[np.arange(GRID_W * GRID_W)[flat_valid], flat_rel[flat_valid]] = 1.0

    def body2(a_ref, oh_ref, o_ref):
        o_ref[...] = _hmm(a_ref[...], oh_ref[...])

    vm = pl.BlockSpec(memory_space=pltpu.VMEM)
    out = pl.pallas_call(
        body2, name="rpb_fold_cols", out_shape=jax.ShapeDtypeStruct((NAH * 15, 128), F32),
        in_specs=[vm, vm], out_specs=vm, compiler_params=_cp(vmem=VMEM_BIG),
    )(t15.reshape(NAH * 15, GRID_W * GRID_W), jnp.asarray(onehot))
    return out[:, :31].reshape(NAH, 15, 31)


def _dn_out(o, z, w):
    outs = []
    for h in range(DNH):
        sl = slice(128 * h, 128 * h + 128)
        oh = o[:, sl]
        r = lax.rsqrt(jnp.mean(oh * oh, axis=-1, keepdims=True) + EPS)
        outs.append(oh * r * w[:, sl] * _silu(z[:, sl]))
    return jnp.concatenate(outs, axis=1)


def _outproj_fwd(of, ob, p, na, onw, x, g1, w_out):
    def body(of_ref, ob_ref, z_ref, na_ref, w_ref, x_ref, g_ref, wo_ref, x1_ref, ai_ref, ao_ref):
        dn = _dn_out(of_ref[...] + ob_ref[...], z_ref[...], w_ref[...])
        ai = jnp.concatenate([dn.astype(BF16), na_ref[...]], axis=1)
        ai_ref[...] = ai
        ao = jnp.dot(ai, wo_ref[...], preferred_element_type=F32)
        ao_ref[...] = ao
        x1_ref[...] = x_ref[...] + g_ref[...] * ao

    lat = lambda w: pl.BlockSpec((None, RT, w), lambda b, j: (b, j, 0))
    return pl.pallas_call(
        body, name="outproj_fwd", grid=(NB, SEQ // RT),
        out_shape=[jax.ShapeDtypeStruct((NB, SEQ, D), F32), jax.ShapeDtypeStruct((NB, SEQ, D), BF16),
                   jax.ShapeDtypeStruct((NB, SEQ, D), F32)],
        in_specs=[pl.BlockSpec((None, RT, DNW), lambda b, j: (b, j + 1, 0)),
                  pl.BlockSpec((None, RT, DNW), lambda b, j: (b, j + 1, 0)),
                  pl.BlockSpec((None, RT, DNW), lambda b, j: (b, j + 1, COL_Z // DNW)),
                  lat(NAW), pl.BlockSpec((1, DNW), lambda b, j: (0, 0)), lat(D),
                  pl.BlockSpec((None, 1, D), lambda b, j: (b, 0, 0)), pl.BlockSpec((D, D), lambda b, j: (0, 0))],
        out_specs=[lat(D), lat(D), lat(D)],
        compiler_params=_cp(vmem=VMEM_BIG),
    )(of, ob, p, na, onw, x, g1, w_out)


def _outproj_bwd(dx1, ao, g1, w_out, of, ob, p, onw):
    def body(dx_ref, ao_ref, g_ref, wo_ref, of_ref, ob_ref, z_ref, w_ref, dao_ref, do_ref, dz_ref, dna_ref, dg_ref, dw_ref):
        b, j = pl.program_id(0), pl.program_id(1)

        @pl.when((b == 0) & (j == 0))
        def _():
            dw_ref[...] = jnp.zeros_like(dw_ref)

        @pl.when(j == 0)
        def _():
            do_ref[...] = jnp.zeros_like(do_ref)
            dz_ref[...] = jnp.zeros_like(dz_ref)

        @pl.when(j == 1)
        def _():
            dg_ref[...] = jnp.zeros_like(dg_ref)

        @pl.when(j >= 1)
        def _():
            dx = dx_ref[...]
            dg_ref[...] += jnp.sum(dx * ao_ref[...], axis=0, keepdims=True)
            dao = (g_ref[...] * dx).astype(BF16)
            dao_ref[...] = dao
            dai = lax.dot_general(dao, wo_ref[...], _NT, preferred_element_type=F32)
            dna_ref[...] = dai[:, DNW:]
            _, vjp = jax.vjp(_dn_out, of_ref[...] + ob_ref[...], z_ref[...], w_ref[...])
            do, dz, dw = vjp(dai[:, :DNW])
            do_ref[...] = do
            dz_ref[...] = dz
            dw_ref[...] += jnp.concatenate([dw, jnp.zeros((7, DNW), F32)], axis=0)

    lat = lambda w: pl.BlockSpec((None, RT, w), lambda b, j: (b, jnp.maximum(j - 1, 0), 0))
    tok = lambda w: pl.BlockSpec((None, RT, w), lambda b, j: (b, j, 0))
    return pl.pallas_call(
        body, name="outproj_bwd", grid=(NB, T // RT),
        out_shape=[jax.ShapeDtypeStruct((NB, SEQ, D), BF16), jax.ShapeDtypeStruct((NB, T, DNW), F32),
                   jax.ShapeDtypeStruct((NB, T, DNW), F32), jax.ShapeDtypeStruct((NB, SEQ, NAW), F32),
                   jax.ShapeDtypeStruct((NB, 1, D), F32), jax.ShapeDtypeStruct((8, DNW), F32)],
        in_specs=[_rows2d_spec(D), lat(D), pl.BlockSpec((None, 1, D), lambda b, j: (b, 0, 0)),
                  pl.BlockSpec((D, D), lambda b, j: (0, 0)),
                  tok(DNW), tok(DNW),
                  pl.BlockSpec((None, RT, DNW), lambda b, j: (b, j, COL_Z // DNW)),
                  pl.BlockSpec((1, DNW), lambda b, j: (0, 0))],
        out_specs=[lat(D), tok(DNW), tok(DNW), lat(NAW), pl.BlockSpec((None, 1, D), lambda b, j: (b, 0, 0)),
                   pl.BlockSpec((8, DNW), lambda b, j: (0, 0))],
        compiler_params=_cp(vmem=VMEM_BIG),
    )(dx1, ao, g1, w_out, of, ob, p, onw)


FT_FWD, FT_BWD = 512, 256
NFF = DFF // FFT


def _ffn_specs(ft):
    row = lambda w: pl.BlockSpec((ft, w), lambda i, f: (i, 0))
    ex = pl.BlockSpec((None, 1, D), lambda i, f: (i // (SEQ // ft), 0, 0))
    one = pl.BlockSpec((1, D), lambda i, f: (0, 0))
    wg = pl.BlockSpec((None, D, FFT), lambda i, f: (f, 0, 0))
    wu = pl.BlockSpec((None, D, FFT), lambda i, f: (NFF + f, 0, 0))
    wo = pl.BlockSpec((FFT, D), lambda i, f: (f, 0))
    ff = pl.BlockSpec((ft, FFT), lambda i, f: (i, f))
    return row, ex, one, wg, wu, wo, ff


def _ffn_fwd(x1, sh2, sc2, g2, n2w, w_in4, w_out, target):
    ft, tpe = FT_FWD, SEQ // FT_FWD

    def body(x_ref, sh_ref, sc_ref, g_ref, w2_ref, wg_ref, wu_ref, wo_ref, t_ref,
             h2_ref, gt_ref, up_ref, dy_ref, dg_ref, loss_ref, acc_ref):
        i, f = pl.program_id(0), pl.program_id(1)

        @pl.when(f == 0)
        def _():
            h2_ref[...] = _norm_mod(x_ref[...], w2_ref[...], sh_ref[...], sc_ref[...]).astype(BF16)
            acc_ref[...] = jnp.zeros_like(acc_ref)

        h2 = h2_ref[...]
        gate = jnp.dot(h2, wg_ref[...], preferred_element_type=F32)
        up = jnp.dot(h2, wu_ref[...], preferred_element_type=F32)
        gt_ref[...] = gate.astype(BF16)
        up_ref[...] = up.astype(BF16)
        acc_ref[...] += jnp.dot((_silu(gate) * up).astype(BF16), wo_ref[...], preferred_element_type=F32)

        @pl.when((i == 0) & (f == 0))
        def _():
            loss_ref[...] = jnp.zeros_like(loss_ref)

        @pl.when(f == NFF - 1)
        def _():
            ffn = acc_ref[...]
            err = x_ref[...] + g_ref[...] * ffn - t_ref[...]
            dy = err * (1.0 / D)
            dy_ref[...] = dy
            part = jnp.sum(dy * ffn, axis=0, keepdims=True)
            loss_ref[...] += jnp.sum(jnp.sum(err * err, axis=1, keepdims=True), axis=0, keepdims=True) * (0.5 / D)

            @pl.when(i % tpe == 0)
            def _():
                dg_ref[...] = part

            @pl.when(i % tpe != 0)
            def _():
                dg_ref[...] += part

    row, ex, one, wg, wu, wo, ff = _ffn_specs(ft)
    return pl.pallas_call(
        body, name="ffn_fwd", grid=(NTOK // ft, NFF),
        out_shape=[jax.ShapeDtypeStruct((NTOK, D), BF16), jax.ShapeDtypeStruct((NTOK, DFF), BF16),
                   jax.ShapeDtypeStruct((NTOK, DFF), BF16), jax.ShapeDtypeStruct((NTOK, D), F32),
                   jax.ShapeDtypeStruct((NB, 1, D), F32), jax.ShapeDtypeStruct((8, 128), F32)],
        in_specs=[row(D), ex, ex, ex, one, wg, wu, wo, row(D)],
        out_specs=[row(D), ff, ff, row(D), ex, pl.BlockSpec((8, 128), lambda i, f: (0, 0))],
        scratch_shapes=[pltpu.VMEM((ft, D), F32)],
        compiler_params=_cp(vmem=VMEM_BIG),
    )(x1, sh2, sc2, g2, n2w, w_in4, w_in4, w_out, target)


def _rows2d_spec(width):
    return pl.BlockSpec((RT, width), lambda b, j: (b * (SEQ // RT) + jnp.maximum(j - 1, 0), 0))


def _ffn_bwd(dy, gate, up, x1, sh2, sc2, g2, n2w, w_in4, w_out):
    ft, tpe = FT_BWD, SEQ // FT_BWD
    ni = NTOK // ft

    def body(dy_ref, gt_ref, up_ref, x_ref, sh_ref, sc_ref, g_ref, w2_ref, wg_ref, wu_ref, wo_ref,
             dff_ref, act_ref, dgu_ref, dx_ref, dsh_ref, dsc_ref, dw_ref, acc_ref):
        f, i = pl.program_id(0), pl.program_id(1)
        rows = pl.ds(pl.multiple_of(i * ft, ft), ft)
        dff = (g_ref[...] * dy_ref[...]).astype(BF16)
        dff_ref[...] = dff
        dact = lax.dot_general(dff, wo_ref[...], _NT, preferred_element_type=F32)
        gate, up = gt_ref[...].astype(F32), up_ref[...].astype(F32)
        sg = _sigmoid(gate)
        sl = gate * sg
        act_ref[...] = (sl * up).astype(BF16)
        dgate = (dact * up * (sg * (1.0 + gate * (1.0 - sg)))).astype(BF16)
        dup = (dact * sl).astype(BF16)
        dgu_ref[0] = dgate
        dgu_ref[1] = dup
        part = (lax.dot_general(dgate, wg_ref[...], _NT, preferred_element_type=F32)
                + lax.dot_general(dup, wu_ref[...], _NT, preferred_element_type=F32))

        @pl.when(f == 0)
        def _():
            acc_ref[rows, :] = part

        @pl.when(f > 0)
        def _():
            acc_ref[rows, :] += part

        @pl.when((i == 0) & (f == 0))
        def _():
            dw_ref[...] = jnp.zeros_like(dw_ref)

        @pl.when(f == NFF - 1)
        def _():
            _, vjp = jax.vjp(_norm_mod, x_ref[...], w2_ref[...], sh_ref[...], sc_ref[...])
            dx, dw, dsh, dsc = vjp(acc_ref[rows, :])
            dx_ref[...] = dy_ref[...] + dx
            dw_ref[...] += dw

            @pl.when(i % tpe == 0)
            def _():
                dsh_ref[...] = dsh
                dsc_ref[...] = dsc

            @pl.when(i % tpe != 0)
            def _():
                dsh_ref[...] += dsh
                dsc_ref[...] += dsc

    last = lambda f, real, spare: jnp.where(f == NFF - 1, real, spare)
    row = lambda w: pl.BlockSpec((ft, w), lambda f, i: (i, 0))
    ex = pl.BlockSpec((None, 1, D), lambda f, i: (i // tpe, 0, 0))
    one = pl.BlockSpec((1, D), lambda f, i: (0, 0))
    wg = pl.BlockSpec((None, D, FFT), lambda f, i: (f, 0, 0))
    wu = pl.BlockSpec((None, D, FFT), lambda f, i: (NFF + f, 0, 0))
    wo = pl.BlockSpec((FFT, D), lambda f, i: (f, 0))
    ff = pl.BlockSpec((ft, FFT), lambda f, i: (i, f))
    row_once = lambda w: pl.BlockSpec((ft, w), lambda f, i: (last(f, i, ni), 0))
    ex_once = pl.BlockSpec((None, 1, D), lambda f, i: (last(f, i // tpe, NB), 0, 0))
    return pl.pallas_call(
        body, name="ffn_bwd", grid=(NFF, ni),
        out_shape=[jax.ShapeDtypeStruct((NTOK + ft, D), BF16), jax.ShapeDtypeStruct((NTOK, DFF), BF16),
                   jax.ShapeDtypeStruct((2, NTOK, DFF), BF16), jax.ShapeDtypeStruct((NTOK + ft, D), F32),
                   jax.ShapeDtypeStruct((NB + 1, 1, D), F32), jax.ShapeDtypeStruct((NB + 1, 1, D), F32),
                   jax.ShapeDtypeStruct((1, D), F32)],
        in_specs=[row(D), ff, ff, row(D), ex, ex, ex, one, wg, wu, wo],
        out_specs=[row_once(D), ff, pl.BlockSpec((2, ft, FFT), lambda f, i: (0, i, f)), row_once(D), ex_once, ex_once, one],
        scratch_shapes=[pltpu.VMEM((NTOK, D), F32)],
        compiler_params=_cp(vmem=VMEM_BIG),
    )(dy, gate, up, x1, sh2, sc2, g2, n2w, w_in4, w_in4, w_out)


def _local_step(x, ctx, target, mod_x, mod_c, norm1_w, norm2_w, w_in_p, conv_w, a_log, dt_bias, out_norm_w,
                q_norm_w, k_norm_w, rpb, w_out_b, w_ffn_in4, w_ffn_out_b, scatter_early=False, gather_late=None):
    sh1, sc1, g1, sh2, sc2, g2 = [mod_x[:, i * D:(i + 1) * D].reshape(NB, 1, D) for i in range(6)]
    csh1, csc1 = mod_c[0:D].reshape(1, 1, D), mod_c[D:2 * D].reshape(1, 1, D)
    cw = jnp.concatenate([conv_w, jnp.zeros((3, 1536), F32)], axis=0)
    prm = jnp.pad(jnp.stack([a_log.reshape(8), dt_bias.reshape(8)]), ((0, 6), (0, 120)))
    onw = jnp.tile(out_norm_w.reshape(1, 128), (1, DNH))
    qw = jnp.tile(q_norm_w.reshape(1, 64), (1, NAH))
    kw = jnp.tile(k_norm_w.reshape(1, 64), (1, NAH))
    seg = np.arange(NAW) // 64
    bd = jnp.asarray((seg[:, None] == seg[None, :]).astype(np.float32) / 64.0, dtype=BF16)
    bias = _bias_table(rpb)

    p, hm = _inproj_fwd(x, ctx, sh1, sc1, csh1, csc1, norm1_w, w_in_p)
    (dq_, dk_, dv_), got_prep = _dn_prep_fwd(p, cw, carry=gather_late[:1] if gather_late else ())
    gates = _gates_fwd(p, prm)
    res = _scan_fwd(dq_, dk_, dv_, gates, carry=gather_late[1:2] if gather_late else ())
    o_f, st_f, inv_f, o_b, st_b, inv_b = res[:6]
    qn, kn, vb = _na_prep_fwd(p, qw, kw, bd)
    na, got = _na_fwd(qn, kn, vb, bias, carry=gather_late[2:] if gather_late else ())
    if gather_late:
        w_out_b = got_prep[0].reshape(D, D)
        w_ffn_in4 = res[6].reshape(4, D, FFT)
        w_ffn_out_b = got[0].reshape(DFF, D)
    x1, ai, ao = _outproj_fwd(o_f, o_b, p, na, onw, x, g1, w_out_b)
    x1f = x1.reshape(NTOK, D)
    h2, gate, up, dy, dg2, loss8 = _ffn_fwd(x1f, sh2, sc2, g2, norm2_w, w_ffn_in4, w_ffn_out_b,
                                             target.reshape(NTOK, D))
    dff, act, dgu, dx1, dsh2, dsc2, dn2w = _ffn_bwd(dy, gate, up, x1f, sh2, sc2, g2, norm2_w, w_ffn_in4, w_ffn_out_b)
    g_w_ffn_out = _matmul_tn(act, dff, FFT, D, 512, "wgrad_ffn_out")
    g_w_ffn_in4 = _matmul_tn(
        h2, dgu, D, FFT, 1024, "wgrad_ffn_in", nn=2 * NFF,
        b_block=(None, 1024, FFT), b_map=lambda i, j, k: (j // NFF, k, j % NFF),
        out_shape=(4, D, FFT), out_block=(None, D, FFT), out_map=lambda i, j, k: (j, i, 0))
    dsh2, dsc2 = dsh2[:NB], dsc2[:NB]
    dao, do, dz, dna, dg1, donw = _outproj_bwd(dx1, ao, g1, w_out_b, o_f, o_b, p, onw)
    g_w_out = _matmul_tn(ai.reshape(NTOK, D), dao.reshape(NTOK, D), D, D, 1024, "wgrad_out")
    dqn, dkn, dvn, dbias = _na_bwd(qn, kn, vb, bias, dna)
    dp_na, dqw, dkw = _na_prep_bwd(p, qw, kw, bd, dqn, dkn, dvn)
    early = [g_w_out.reshape(8, D // 8, D), g_w_ffn_in4.reshape(8, D // 2, FFT), g_w_ffn_out.reshape(8, DFF // 8, D)]
    res = _scan_bwd(dq_, dk_, dv_, gates, st_f, st_b, inv_f, inv_b, do, carry=early if scatter_early else ())
    dqf, dkf, dvf, dgf, dqb, dkb, dvb, dgb = res[:8]
    if scatter_early:
        g_w_out, g_w_ffn_in4, g_w_ffn_out = res[8:]
    dpq, dpk, dpv, gcq, gck, gcv = _dn_prep_bwd(p, cw, (dqf, dqb), (dkf, dkb), (dvf, dvb))
    dgt, dprm = _gates_bwd(p, prm, dgf, dgb)
    dpb, grad_x, dsh1, dsc1, dcsh1, dcsc1, dn1w = _inproj_bwd(
        (dpq, dpk, dpv), dz, dp_na, dgt, w_in_p, x, ctx, sh1, sc1, csh1, csc1, norm1_w, dx1)
    g_w_in_p = _matmul_tn(dpb.reshape(NB * T, PC), hm.reshape(NB * T, D), PC, D, 512, "wgrad_in")
    g_rpb = _rpb_fold(dbias)

    dmod_x = jnp.concatenate([dsh1, dsc1, dg1, dsh2, dsc2, dg2], axis=2).reshape(NB, 6 * D)
    dmod_c = jnp.concatenate([dcsh1.reshape(D), dcsc1.reshape(D), jnp.zeros((4 * D,), F32)])
    g_conv = jnp.concatenate([gcq[:5], gck[:5], gcv[:5]], axis=1)
    small = dict(
        norm1_w=dn1w, norm2_w=dn2w, a_log=dprm[0, :8], dt_bias=dprm[1, :8],
        out_norm_w=donw[0].reshape(DNH, 128).sum(0), q_norm_w=dqw[0].reshape(NAH, 64).sum(0),
        k_norm_w=dkw[0].reshape(NAH, 64).sum(0), rpb=g_rpb, conv_w=g_conv)
    return loss8[0, 0], grad_x, dmod_x, dmod_c, small, g_w_in_p, g_w_out, g_w_ffn_in4, g_w_ffn_out


def _perm_w_in(wt):
    return jnp.concatenate([wt[:2048], wt[2064:3600], wt[2048:2064], jnp.zeros((PC - IN_COLS, wt.shape[1]), wt.dtype)],
                           axis=0)


def _unperm_w_in(gt):
    return jnp.concatenate([gt[:2048], gt[COL_G:COL_G + 16], gt[2048:COL_G]], axis=0)


def _misc_row(a_log, dt_bias, onw, qnw, knw):
    return jnp.concatenate([a_log.reshape(8), dt_bias.reshape(8), jnp.zeros((112,), F32), onw.reshape(128),
                            qnw.reshape(64), knw.reshape(64), jnp.zeros((D - 384,), F32)])


def _pack_small(c_ctx, n1, n2, b_ada, a_log, dt_bias, onw, qnw, knw, rpb, conv_s):
    misc = _misc_row(a_log, dt_bias, onw, qnw, knw)
    rp = jnp.concatenate([rpb.reshape(-1), jnp.zeros((4 * D - 3720,), F32)])
    cv = jnp.concatenate([conv_s.reshape(-1), jnp.zeros((2 * D - 1920,), F32)])
    return jnp.concatenate([c_ctx.reshape(-1), n1.reshape(-1), n2.reshape(-1), b_ada.reshape(-1), misc, rp, cv]).reshape(16, D)


def _unpack_small(a):
    f = a.reshape(-1)
    misc = f[9 * D:10 * D]
    return dict(
        c_ctx=f[0:D], norm1_w=f[D:2 * D].reshape(1, D), norm2_w=f[2 * D:3 * D].reshape(1, D),
        b_ada=f[3 * D:9 * D].reshape(1, 6 * D), dn_A_log=misc[0:8].reshape(1, 2, 4), dn_dt_bias=misc[8:16].reshape(1, 2, 4),
        dn_out_norm_w=misc[128:256].reshape(1, 128), na_q_norm_w=misc[256:320].reshape(1, 64),
        na_k_norm_w=misc[320:384].reshape(1, 64), na_rpb=f[10 * D:10 * D + 3720].reshape(1, 8, 15, 31),
        dn_conv_w=f[14 * D:14 * D + 1920].reshape(1, 5, 384))


_WEIGHTS = ['c_ctx', 'norm1_w', 'norm2_w', 'w_ada', 'b_ada', 'w_in', 'dn_conv_w', 'dn_A_log', 'dn_dt_bias',
            'dn_out_norm_w', 'na_q_norm_w', 'na_k_norm_w', 'na_rpb', 'w_out', 'w_ffn_in', 'w_ffn_out']


def kernel(x, c, ctx, c_ctx, norm1_w, norm2_w, w_ada, b_ada, w_in, dn_conv_w, dn_A_log, dn_dt_bias, dn_out_norm_w, na_q_norm_w, na_k_norm_w, na_rpb, w_out, w_ffn_in, w_ffn_out, loss_target, m_c_ctx, m_norm1_w, m_norm2_w, m_w_ada, m_b_ada, m_w_in, m_dn_conv_w, m_dn_A_log, m_dn_dt_bias, m_dn_out_norm_w, m_na_q_norm_w, m_na_k_norm_w, m_na_rpb, m_w_out, m_w_ffn_in, m_w_ffn_out, v_c_ctx, v_norm1_w, v_norm2_w, v_w_ada, v_b_ada, v_w_in, v_dn_conv_w, v_dn_A_log, v_dn_dt_bias, v_dn_out_norm_w, v_na_q_norm_w, v_na_k_norm_w, v_na_rpb, v_w_out, v_w_ffn_in, v_w_ffn_out):
    ix, iy, ic = lax.axis_index("x"), lax.axis_index("y"), lax.axis_index("c")
    chip = 2 * ix + iy
    me = 4 * ix + 2 * iy + ic

    crow = jnp.concatenate([c, c_ctx.reshape(1, D), jnp.zeros((5, D), F32)], axis=0)
    cvrow = jnp.concatenate([dn_conv_w[0].reshape(-1), jnp.zeros((2 * D - 1920,), F32)]).reshape(2, D)
    w_in_t = w_in[0].T
    gw_in, g0 = _gather_weights([_cast_bf16(lax.dynamic_slice(w_in_t, (0, ic * (D // 2)), (900, D // 2)), "cast_w_in")],
                                jnp.concatenate([crow, cvrow, jnp.zeros((6, D), F32)], axis=0))
    cin = g0[:, :8].reshape(64, D)
    conv_full = jnp.concatenate([g0[2 * s, 8:10].reshape(-1)[:1920].reshape(5, 384) for s in range(4)], axis=1)

    w_ada_s = w_ada[0]
    b_s = lax.dynamic_slice(b_ada, (0, chip * 1536), (1, 1536))
    mod_s = _ada_fwd(cin, w_ada_s, b_s)
    mods = _gather_rows(mod_s, "gather_mod")
    mod_mine = jnp.concatenate([mods[0], mods[2], mods[4], mods[6]], axis=1)
    mod_x, mod_c = mod_mine[0:NB], mod_mine[2]

    def half(w2d):
        r = w2d.shape[0] // 2
        return lax.dynamic_slice(w2d, (ic * r, 0), (r, w2d.shape[1]))

    late = [_cast_bf16(half(w_out[0]), "cast_w_out"), _cast_bf16(half(w_ffn_in[0]), "cast_w_ffn_in"),
            _cast_bf16(half(w_ffn_out[0]), "cast_w_ffn_out")]
    w_in_p = _perm_w_in(gw_in.reshape(4, 2, 900, D // 2).transpose(0, 2, 1, 3).reshape(IN_COLS, D))

    (loss_part, grad_x, dmod_x, dmod_c, small, g_w_in_p, r_w_out, r_w_ffn_in, r_w_ffn_out) = _local_step(
        x, ctx, loss_target, mod_x, mod_c, norm1_w, norm2_w, w_in_p, conv_full, dn_A_log[0], dn_dt_bias[0],
        dn_out_norm_w[0], na_q_norm_w[0], na_k_norm_w[0], na_rpb[0], None, None, None,
        scatter_early=True, gather_late=late)
    loss = lax.psum(loss_part, ("x", "y", "c"))

    dm = jnp.concatenate([dmod_x, dmod_c.reshape(1, 6 * D), jnp.zeros((5, 6 * D), F32)], axis=0).reshape(48, D)
    misc = _misc_row(small["a_log"], small["dt_bias"], small["out_norm_w"], small["q_norm_w"], small["k_norm_w"])
    sm = jnp.concatenate([
        small["norm1_w"].reshape(-1), small["norm2_w"].reshape(-1), misc,
        small["rpb"].reshape(-1), jnp.zeros((4 * D - 3720,), F32),
        small["conv_w"].reshape(-1), jnp.zeros((8 * D - 7680,), F32), jnp.zeros((D,), F32)]).reshape(16, D)
    gsl, gtot = _allgather8(jnp.concatenate([dm, sm], axis=0), "gather_small", with_sum=True)
    gmod = gsl[:, :48].reshape(64, 6 * D)
    st = gtot[48:64].reshape(-1)
    g_n1, g_n2, g_misc = st[0:D].reshape(1, D), st[D:2 * D].reshape(1, D), st[2 * D:3 * D]
    g_rpb = st[3 * D:3 * D + 3720].reshape(1, 8, 15, 31)
    g_conv_full = st[7 * D:7 * D + 7680].reshape(5, 1536)
    g_conv_s = lax.dynamic_slice(g_conv_full, (0, chip * 384), (5, 384)).reshape(1, 5, 384)

    gmod_s = lax.dynamic_slice(gmod, (0, chip * 1536), (64, 1536))
    g_w_ada_s, pctx = _ada_bwd(cin, gmod_s, w_ada_s)
    parts = _allgather8(pctx, "gather_cctx")
    g_b_ada, g_c_ctx = _ada_small(gmod, parts, c_ctx.reshape(1, D))

    g_w_in8 = _unperm_w_in(g_w_in_p).reshape(4, 900, 2, D // 2).transpose(0, 2, 1, 3).reshape(8, 900, D // 2)
    names = ["w_in", "w_out", "w_ffn_in", "w_ffn_out"]
    early = _sibling_exchange([_sum_slots(r, "sum_" + nm)
                               for r, nm in zip([r_w_out, r_w_ffn_in, r_w_ffn_out], names[1:])], "sibling_early")
    g_w_out_s = early[0].reshape(1, D // 4, D)
    g_w_ffn_in_s = early[1].reshape(1, D, FFT)
    g_w_ffn_out_s = early[2].reshape(1, DFF // 4, D)
    two = lambda a: a.reshape(a.shape[1], a.shape[2])
    hosted_names = ["w_ada", "w_out", "w_ffn_in"]
    hosted_g = dict(w_ada=g_w_ada_s, w_out=two(g_w_out_s), w_ffn_in=two(g_w_ffn_in_s))
    hosted_w = dict(w_ada=(w_ada, m_w_ada, v_w_ada), w_out=(w_out, m_w_out, v_w_out), w_ffn_in=(w_ffn_in, m_w_ffn_in, v_w_ffn_in))
    hosted_res, (r_w_in,) = _adamw_hosting(
        [(two(hosted_w[k][0]), hosted_g[k], two(hosted_w[k][1]), two(hosted_w[k][2])) for k in hosted_names], [g_w_in8])
    (g_w_in_half,) = _sibling_exchange([_sum_slots(r_w_in, "sum_w_in")], "sibling_w_in")
    g_w_in_t = g_w_in_half.transpose(1, 0, 2).reshape(900, D)

    grads = dict(
        c_ctx=g_c_ctx.reshape(D), norm1_w=g_n1, norm2_w=g_n2, w_ada=g_w_ada_s.reshape(1, D, 1536), b_ada=g_b_ada,
        w_in=g_w_in_t, dn_conv_w=g_conv_s, dn_A_log=g_misc[0:8].reshape(1, 2, 4), dn_dt_bias=g_misc[8:16].reshape(1, 2, 4),
        dn_out_norm_w=g_misc[128:256].reshape(1, 128), na_q_norm_w=g_misc[256:320].reshape(1, 64),
        na_k_norm_w=g_misc[320:384].reshape(1, 64), na_rpb=g_rpb, w_out=g_w_out_s, w_ffn_in=g_w_ffn_in_s,
        w_ffn_out=g_w_ffn_out_s)
    pk = lambda d_: _pack_small(d_["c_ctx"], d_["norm1_w"], d_["norm2_w"], d_["b_ada"], d_["dn_A_log"], d_["dn_dt_bias"],
                                d_["dn_out_norm_w"], d_["na_q_norm_w"], d_["na_k_norm_w"], d_["na_rpb"], d_["dn_conv_w"])
    wts = dict(c_ctx=c_ctx, norm1_w=norm1_w, norm2_w=norm2_w, b_ada=b_ada, dn_A_log=dn_A_log, dn_dt_bias=dn_dt_bias,
               dn_out_norm_w=dn_out_norm_w, na_q_norm_w=na_q_norm_w, na_k_norm_w=na_k_norm_w, na_rpb=na_rpb, dn_conv_w=dn_conv_w)
    ms = dict(c_ctx=m_c_ctx, norm1_w=m_norm1_w, norm2_w=m_norm2_w, b_ada=m_b_ada, dn_A_log=m_dn_A_log, dn_dt_bias=m_dn_dt_bias,
              dn_out_norm_w=m_dn_out_norm_w, na_q_norm_w=m_na_q_norm_w, na_k_norm_w=m_na_k_norm_w, na_rpb=m_na_rpb,
              dn_conv_w=m_dn_conv_w)
    vs = dict(c_ctx=v_c_ctx, norm1_w=v_norm1_w, norm2_w=v_norm2_w, b_ada=v_b_ada, dn_A_log=v_dn_A_log, dn_dt_bias=v_dn_dt_bias,
              dn_out_norm_w=v_dn_out_norm_w, na_q_norm_w=v_na_q_norm_w, na_k_norm_w=v_na_k_norm_w, na_rpb=v_na_rpb,
              dn_conv_w=v_dn_conv_w)
    sd, snm, snv, _ = _adamw(pk(wts), pk(grads), pk(ms), pk(vs), "adamw_small")
    delta, new_m, new_v = _unpack_small(sd), _unpack_small(snm), _unpack_small(snv)
    for k, res in zip(hosted_names, hosted_res):
        shp = hosted_w[k][0].shape
        delta[k], new_m[k], new_v[k], grads[k] = [a.reshape(shp) for a in res]
    big = dict(w_in=(w_in, m_w_in, v_w_in), w_ffn_out=(w_ffn_out, m_w_ffn_out, v_w_ffn_out))
    for nm, (w_, m_, v_) in big.items():
        shp = w_.shape
        if nm == "w_in":
            r2, back = (lambda a: a[0].T), (lambda a: a.T.reshape(shp))
            res = _adamw(w_in_t, grads[nm], r2(m_), r2(v_), "adamw_" + nm)
        else:
            r2, back = (lambda a: a.reshape(shp[1], shp[2])), (lambda a: a.reshape(shp))
            res = _adamw(r2(w_), r2(grads[nm]), r2(m_), r2(v_), "adamw_" + nm)
        delta[nm], new_m[nm], new_v[nm], grads[nm] = [back(a) for a in res]

    return (loss, grad_x, *[grads[n] for n in _WEIGHTS], *[delta[n] for n in _WEIGHTS],
            *[new_m[n] for n in _WEIGHTS], *[new_v[n] for n in _WEIGHTS])
```

```python
import functools

import numpy as np
import jax
import jax.numpy as jnp
from jax import lax
from jax.experimental import pallas as pl
from jax.experimental.pallas import tpu as pltpu

F32 = jnp.float32
BF16 = jnp.bfloat16

D = 1024
NB = 2
SEQ = 2048
CT = 256
T = CT + SEQ
NTOK = NB * SEQ
CH = 64
NCH = T // CH
NCH_CTX = CT // CH
DNH, DNW = 4, 512
NAH, NAW = 8, 512
GRID_W = 64
ROWS = SEQ // GRID_W
WIN_ROWS = 8
WIN_COLS = 16
NWIN = WIN_ROWS * GRID_W
DFF = 2816
FFT = 1408
IN_COLS = 3600
PC = 3712
COL_Z, COL_NA, COL_G = 1536, 2048, 3584
EPS = 1e-6
NEG = -1e30
RT = 256
VMEM_BIG = 56 * 1024 * 1024
MESH = pl.DeviceIdType.MESH

ADAM_LR, ADAM_B1, ADAM_B2, ADAM_EPS, ADAM_WD, ADAM_STEP = 0.001, 0.9, 0.999, 1e-08, 0.01, 10


def _cp(vmem=None, sem=None):
    kw = {}
    if vmem is not None:
        kw["vmem_limit_bytes"] = vmem
    if sem is not None:
        kw["dimension_semantics"] = sem
    return pltpu.CompilerParams(**kw)


_NN = (((1,), (0,)), ((), ()))
_NT = (((1,), (1,)), ((), ()))
_TN = (((0,), (0,)), ((), ()))


def _dg(a, b, dims):
    return lax.dot_general(a.astype(BF16), b.astype(BF16), dims, preferred_element_type=F32)


@jax.custom_vjp
def _mm(a, b):
    return _dg(a, b, _NN)


_mm.defvjp(lambda a, b: (_dg(a, b, _NN), (a, b)),
           lambda r, g: (_dg(g, r[1], _NT), _dg(r[0], g, _TN)))


def _split_dot(a, b, dims):
    hi = b.astype(BF16)
    lo = (b - hi.astype(F32)).astype(BF16)
    ab = a.astype(BF16)
    return (lax.dot_general(ab, hi, dims, preferred_element_type=F32)
            + lax.dot_general(ab, lo, dims, preferred_element_type=F32))


@jax.custom_vjp
def _mask_mm(mask, b):
    return _split_dot(mask, b, _NN)


_mask_mm.defvjp(lambda mask, b: (_split_dot(mask, b, _NN), mask),
                lambda mask, g: (jnp.zeros_like(mask), _split_dot(mask, g, _TN)))


def _hmm(a, b):
    return jnp.dot(a, b, precision=lax.Precision.HIGHEST, preferred_element_type=F32)


def _sigmoid(x):
    return 1.0 / (1.0 + jnp.exp(-x))


def _silu(x):
    return x * _sigmoid(x)


def _norm_mod(x, w, sh, sc):
    r = lax.rsqrt(jnp.mean(x * x, axis=-1, keepdims=True) + EPS)
    return (x * r * w) * (1.0 + sc) + sh


def _my_pos():
    return lax.axis_index("x"), lax.axis_index("y"), lax.axis_index("c")


def _flip(pos, k):
    x, y, c = pos
    return (1 - x if k & 4 else x, 1 - y if k & 2 else y, 1 - c if k & 1 else c)


def _slot(pos):
    return 4 * pos[0] + 2 * pos[1] + pos[2]


def _allgather8(v, name, with_sum=False):
    R, W = v.shape

    def body(v_ref, out_ref, *rest):
        if with_sum:
            tot_ref, send_sems, recv_sems, lsem = rest
        else:
            send_sems, recv_sems, lsem = rest
        me = _my_pos()
        mine = pltpu.make_async_copy(v_ref, out_ref.at[_slot(me)], lsem)
        mine.start()
        sends = []
        for k in range(1, 8):
            cp = pltpu.make_async_remote_copy(
                src_ref=v_ref, dst_ref=out_ref.at[_slot(me)], send_sem=send_sems.at[k - 1],
                recv_sem=recv_sems.at[k - 1], device_id=_flip(me, k), device_id_type=MESH)
            cp.start()
            sends.append(cp)
        for k in range(1, 8):
            peer = _flip(me, k)
            pltpu.make_async_remote_copy(
                src_ref=v_ref, dst_ref=out_ref.at[_slot(peer)], send_sem=send_sems.at[k - 1],
                recv_sem=recv_sems.at[k - 1], device_id=peer, device_id_type=MESH).wait_recv()
        for cp in sends:
            cp.wait_send()
        mine.wait()
        if with_sum:
            acc = out_ref[0]
            for s in range(1, 8):
                acc = acc + out_ref[s]
            tot_ref[...] = acc

    out_shape = [jax.ShapeDtypeStruct((8, R, W), F32)]
    out_specs = [pl.BlockSpec(memory_space=pltpu.VMEM)]
    if with_sum:
        out_shape.append(jax.ShapeDtypeStruct((R, W), F32))
        out_specs.append(pl.BlockSpec(memory_space=pltpu.VMEM))
    res = pl.pallas_call(
        body, name=name, out_shape=out_shape,
        in_specs=[pl.BlockSpec(memory_space=pltpu.VMEM)], out_specs=out_specs,
        scratch_shapes=[pltpu.SemaphoreType.DMA((7,)), pltpu.SemaphoreType.DMA((7,)), pltpu.SemaphoreType.DMA],
        compiler_params=_cp(vmem=VMEM_BIG),
    )(v)
    return res if with_sum else res[0]


def _gather_rows(v, name):
    W = v.shape[1]

    def body(v_ref, out_ref, send_sems, recv_sems, lsem):
        pos = _my_pos()
        me = _slot(pos)
        rows = lambda p: v_ref.at[pl.ds(pl.multiple_of(8 * p, 8), 8)]
        locals_ = [(rows(me), out_ref.at[me])]
        xfers = [(rows(_slot(_flip(pos, k))), out_ref.at[me], out_ref.at[_slot(_flip(pos, k))], _flip(pos, k))
                 for k in range(1, 8)]
        _xfer_start((locals_, xfers), (send_sems, recv_sems, lsem))
        _xfer_wait((locals_, xfers), (send_sems, recv_sems, lsem))

    vm = pl.BlockSpec(memory_space=pltpu.VMEM)
    return pl.pallas_call(
        body, name=name, out_shape=jax.ShapeDtypeStruct((8, 8, W), F32), in_specs=[vm], out_specs=vm,
        scratch_shapes=_xfer_sems(1, 7),
    )(v)


def _gather_weights(halves, small):
    n = len(halves)

    def body(*refs):
        ins, s_in, outs, s_out = refs[:n], refs[n], refs[n + 1:2 * n + 1], refs[2 * n + 1]
        send_sems, recv_sems, lsems = refs[2 * n + 2:2 * n + 5]
        s_sems = refs[2 * n + 5:]
        s_plan = _gather_plan([s_in], [s_out], _my_pos())
        _xfer_start(s_plan, s_sems)
        me = _my_pos()
        x, y, c = me
        sib = (x, y, 1 - c)
        chips = [4, 2, 6]

        def cp(a, k, block, to, src=None):
            dst = outs[a].at[_slot(block)]
            return pltpu.make_async_remote_copy(
                src_ref=dst if src is None else src, dst_ref=dst, send_sem=send_sems.at[a, k],
                recv_sem=recv_sems.at[a, k], device_id=to, device_id_type=MESH)

        mine, first, passed = [], [], []
        for a in range(n):
            m = pltpu.make_async_copy(ins[a], outs[a].at[_slot(me)], lsems.at[a])
            m.start()
            mine.append(m)
            first.append(cp(a, 0, me, sib, src=ins[a]))
            for j, k in enumerate(chips):
                first.append(cp(a, 1 + j, me, _flip(me, k), src=ins[a]))
        for f in first:
            f.start()
        for j, k in enumerate(chips):
            for a in range(n):
                cp(a, 1 + j, _flip(me, k), me).wait_recv()
                p = cp(a, 4 + j, _flip(me, k), sib)
                p.start()
                passed.append(p)
        for a in range(n):
            cp(a, 0, sib, me).wait_recv()
            for j, k in enumerate(chips):
                cp(a, 4 + j, _flip(sib, k), me).wait_recv()
        for f in first + passed:
            f.wait_send()
        for m in mine:
            m.wait()
        _xfer_wait(s_plan, s_sems)

    vm = pl.BlockSpec(memory_space=pltpu.VMEM)
    return pl.pallas_call(
        body, name="gather_weights",
        out_shape=[jax.ShapeDtypeStruct((8,) + h.shape, h.dtype) for h in list(halves) + [small]],
        in_specs=[vm] * (n + 1), out_specs=[vm] * (n + 1),
        scratch_shapes=[pltpu.SemaphoreType.DMA((n, 7)), pltpu.SemaphoreType.DMA((n, 7)),
                        pltpu.SemaphoreType.DMA((n,))] + _xfer_sems(1, 7),
        compiler_params=_cp(vmem=VMEM_BIG),
    )(*halves, small)


def _xfer_start(plan, sems):
    send_sems, recv_sems, lsems = sems
    locals_, xfers = plan
    for i, (src, dst) in enumerate(locals_):
        pltpu.make_async_copy(src, dst, lsems.at[i]).start()
    for i, (src, dst, _, peer) in enumerate(xfers):
        pltpu.make_async_remote_copy(src_ref=src, dst_ref=dst, send_sem=send_sems.at[i], recv_sem=recv_sems.at[i],
                                     device_id=peer, device_id_type=MESH).start()


def _xfer_wait(plan, sems):
    send_sems, recv_sems, lsems = sems
    locals_, xfers = plan
    for i, (src, _, landing, peer) in enumerate(xfers):
        pltpu.make_async_remote_copy(src_ref=src, dst_ref=landing, send_sem=send_sems.at[i], recv_sem=recv_sems.at[i],
                                     device_id=peer, device_id_type=MESH).wait_recv()
    for i, (src, dst, _, peer) in enumerate(xfers):
        pltpu.make_async_remote_copy(src_ref=src, dst_ref=dst, send_sem=send_sems.at[i], recv_sem=recv_sems.at[i],
                                     device_id=peer, device_id_type=MESH).wait_send()
    for i, (src, dst) in enumerate(locals_):
        pltpu.make_async_copy(src, dst, lsems.at[i]).wait()


def _xfer_sems(n_local, n_xfer):
    return [pltpu.SemaphoreType.DMA((n_xfer,)), pltpu.SemaphoreType.DMA((n_xfer,)), pltpu.SemaphoreType.DMA((n_local,))]


def _scatter_plan(ins, outs, pos):
    me = _slot(pos)
    locals_, xfers = [], []
    for src, dst in zip(ins, outs):
        locals_.append((src.at[me], dst.at[me]))
        for k in range(1, 8):
            peer = _flip(pos, k)
            xfers.append((src.at[_slot(peer)], dst.at[me], dst.at[_slot(peer)], peer))
    return locals_, xfers


def _sibling_exchange(halves, name):
    n = len(halves)

    def body(*refs):
        ins, outs = refs[:n], refs[n:2 * n]
        send_sems, recv_sems, lsems = refs[2 * n:]
        x, y, c = _my_pos()
        sib = (x, y, 1 - c)
        mine, sends = [], []
        for a in range(n):
            m = pltpu.make_async_copy(ins[a], outs[a].at[c], lsems.at[a])
            m.start()
            mine.append(m)
            cp = pltpu.make_async_remote_copy(
                src_ref=ins[a], dst_ref=outs[a].at[c], send_sem=send_sems.at[a], recv_sem=recv_sems.at[a],
                device_id=sib, device_id_type=MESH)
            cp.start()
            sends.append(cp)
        for a in range(n):
            pltpu.make_async_remote_copy(
                src_ref=ins[a], dst_ref=outs[a].at[1 - c], send_sem=send_sems.at[a], recv_sem=recv_sems.at[a],
                device_id=sib, device_id_type=MESH).wait_recv()
        for cp in sends:
            cp.wait_send()
        for m in mine:
            m.wait()

    vm = pl.BlockSpec(memory_space=pltpu.VMEM)
    return pl.pallas_call(
        body, name=name,
        out_shape=[jax.ShapeDtypeStruct((2,) + h.shape, h.dtype) for h in halves],
        in_specs=[vm] * n, out_specs=[vm] * n,
        scratch_shapes=[pltpu.SemaphoreType.DMA((n,)), pltpu.SemaphoreType.DMA((n,)), pltpu.SemaphoreType.DMA((n,))],
        compiler_params=_cp(vmem=VMEM_BIG),
    )(*halves)


def _row_tile(r, cap=512):
    for t in (cap, 256, 128, 64, 32, 16, 8):
        if t <= cap and r % t == 0:
            return t
    return r


def _cast_bf16(a, name):
    R, W = a.shape
    tr = _row_tile(R, 256)

    def body(a_ref, o_ref):
        o_ref[...] = a_ref[...].astype(BF16)

    return pl.pallas_call(
        body, name=name, grid=(R // tr,), out_shape=jax.ShapeDtypeStruct((R, W), BF16),
        in_specs=[pl.BlockSpec((tr, W), lambda i: (i, 0))], out_specs=pl.BlockSpec((tr, W), lambda i: (i, 0)),
    )(a)


def _sum_slots(r, name):
    _, R, W = r.shape
    tr = _row_tile(R, 128)

    def body(r_ref, o_ref):
        acc = r_ref[0].astype(F32)
        for s_ in range(1, 8):
            acc = acc + r_ref[s_].astype(F32)
        o_ref[...] = acc.astype(BF16)

    return pl.pallas_call(
        body, name=name, grid=(R // tr,), out_shape=jax.ShapeDtypeStruct((R, W), BF16),
        in_specs=[pl.BlockSpec((8, tr, W), lambda i: (0, i, 0))], out_specs=pl.BlockSpec((tr, W), lambda i: (i, 0)),
        compiler_params=_cp(vmem=VMEM_BIG),
    )(r)


def _adamw(w, g, m, v, name):
    R, W = w.shape
    tr = _row_tile(R, 256)
    tc = 256 if (tr == R and R * W * 4 > (1 << 21) and W % 256 == 0) else W
    c1 = 1.0 / (1.0 - ADAM_B1 ** ADAM_STEP)
    c2 = 1.0 / (1.0 - ADAM_B2 ** ADAM_STEP)

    def body(w_ref, g_ref, m_ref, v_ref, d_ref, nm_ref, nv_ref, g32_ref):
        gg = g_ref[...].astype(F32)
        g32_ref[...] = gg
        nm = ADAM_B1 * m_ref[...] + (1.0 - ADAM_B1) * gg
        nv = ADAM_B2 * v_ref[...] + (1.0 - ADAM_B2) * (gg * gg)
        d_ref[...] = -ADAM_LR * ((nm * c1) / (jnp.sqrt(nv * c2) + ADAM_EPS) + ADAM_WD * w_ref[...])
        nm_ref[...] = nm
        nv_ref[...] = nv

    spec = pl.BlockSpec((tr, tc), lambda i, j: (i, j))
    return pl.pallas_call(
        body, name=name, grid=(R // tr, W // tc), out_shape=[jax.ShapeDtypeStruct((R, W), F32)] * 4,
        in_specs=[spec] * 4, out_specs=[spec] * 4, compiler_params=_cp(vmem=VMEM_BIG),
    )(w, g, m, v)


def _adamw_hosting(items, carry):
    n, nc, steps = len(items), len(carry), 8
    c1 = 1.0 / (1.0 - ADAM_B1 ** ADAM_STEP)
    c2 = 1.0 / (1.0 - ADAM_B2 ** ADAM_STEP)

    def body(*refs):
        ins, c_in = refs[:4 * n], refs[4 * n:4 * n + nc]
        outs, c_out = refs[4 * n + nc:8 * n + nc], refs[8 * n + nc:8 * n + 2 * nc]
        sems = refs[8 * n + 2 * nc:]
        if nc:
            @pl.when(pl.program_id(0) == 0)
            def _():
                _xfer_start(_scatter_plan(c_in, c_out, _my_pos()), sems)

        for a in range(n):
            w_ref, g_ref, m_ref, v_ref = ins[4 * a:4 * a + 4]
            d_ref, nm_ref, nv_ref, g32_ref = outs[4 * a:4 * a + 4]
            gg = g_ref[...].astype(F32)
            g32_ref[...] = gg
            nm = ADAM_B1 * m_ref[...] + (1.0 - ADAM_B1) * gg
            nv = ADAM_B2 * v_ref[...] + (1.0 - ADAM_B2) * (gg * gg)
            d_ref[...] = -ADAM_LR * ((nm * c1) / (jnp.sqrt(nv * c2) + ADAM_EPS) + ADAM_WD * w_ref[...])
            nm_ref[...] = nm
            nv_ref[...] = nv

        if nc:
            @pl.when(pl.program_id(0) == steps - 1)
            def _():
                _xfer_wait(_scatter_plan(c_in, c_out, _my_pos()), sems)

    specs = [pl.BlockSpec((w.shape[0] // steps, w.shape[1]), lambda i: (i, 0)) for w, _, _, _ in items]
    hbm = pl.BlockSpec(memory_space=pltpu.HBM)
    res = pl.pallas_call(
        body, name="adamw_hosting", grid=(steps,),
        out_shape=[jax.ShapeDtypeStruct(w.shape, F32) for w, _, _, _ in items for _ in range(4)]
        + [jax.ShapeDtypeStruct(g.shape, g.dtype) for g in carry],
        in_specs=[sp for sp in specs for _ in range(4)] + [hbm] * nc,
        out_specs=[sp for sp in specs for _ in range(4)] + [hbm] * nc,
        scratch_shapes=_xfer_sems(nc, 7 * nc) if nc else [],
        compiler_params=_cp(vmem=VMEM_BIG),
    )(*[a for it in items for a in it], *carry)
    return [res[4 * a:4 * a + 4] for a in range(n)], list(res[4 * n:])


def _matmul_tn(a, b, tm, tn, tk, name, nn=1, b_map=None, out_shape=None, out_block=None, out_map=None,
               b_block=None):
    K, M = a.shape
    nk = K // tk
    nm = M // tm

    def body(a_ref, b_ref, o_ref, acc_ref):
        kk = pl.program_id(2)

        @pl.when(kk == 0)
        def _():
            acc_ref[...] = jnp.zeros_like(acc_ref)

        acc_ref[...] += lax.dot_general(a_ref[...], b_ref[...], _TN, preferred_element_type=F32)

        @pl.when(kk == nk - 1)
        def _():
            o_ref[...] = acc_ref[...].astype(BF16)

    return pl.pallas_call(
        body, name=name, grid=(nm, nn, nk),
        out_shape=jax.ShapeDtypeStruct(out_shape or (M, b.shape[-1]), BF16),
        in_specs=[pl.BlockSpec((tk, tm), lambda i, j, k: (k, i)),
                  pl.BlockSpec(b_block or (tk, tn), b_map or (lambda i, j, k: (k, j)))],
        out_specs=pl.BlockSpec(out_block or (tm, tn), out_map or (lambda i, j, k: (i, j))),
        scratch_shapes=[pltpu.VMEM((tm, tn), F32)],
        compiler_params=_cp(vmem=VMEM_BIG),
    )(a, b)


def _ada_fwd(cin, w_ada_s, b_s):
    tn = 512

    def body(c_ref, w_ref, b_ref, o_ref):
        o_ref[...] = _dg(_silu(c_ref[...]), w_ref[...], _NN) + b_ref[...]

    return pl.pallas_call(
        body, name="ada_fwd", grid=(1536 // tn,), out_shape=jax.ShapeDtypeStruct((64, 1536), F32),
        in_specs=[pl.BlockSpec((64, D), lambda j: (0, 0)), pl.BlockSpec((D, tn), lambda j: (0, j)),
                  pl.BlockSpec((1, tn), lambda j: (0, j))],
        out_specs=pl.BlockSpec((64, tn), lambda j: (0, j)),
        compiler_params=_cp(vmem=VMEM_BIG),
    )(cin, w_ada_s, b_s)


def _ada_bwd(cin, gmod_s, w_ada_s):
    tn = 512

    def body(c_ref, g_ref, w_ref, gw_ref, pc_ref):
        j = pl.program_id(0)
        g = g_ref[...]
        gw_ref[...] = _dg(_silu(c_ref[...]), g, _TN)
        rr = lax.broadcasted_iota(jnp.int32, (8, 64), 0)
        cc = lax.broadcasted_iota(jnp.int32, (8, 64), 1)
        sel = jnp.where((rr == 0) & ((cc & 7) == 2), 1.0, 0.0).astype(F32)
        dctx = _hmm(sel, g)
        part = _dg(dctx, w_ref[...], _NT)

        @pl.when(j == 0)
        def _():
            pc_ref[...] = jnp.zeros_like(pc_ref)

        pc_ref[...] += part

    return pl.pallas_call(
        body, name="ada_bwd", grid=(1536 // tn,),
        out_shape=[jax.ShapeDtypeStruct((D, 1536), F32), jax.ShapeDtypeStruct((8, D), F32)],
        in_specs=[pl.BlockSpec((64, D), lambda j: (0, 0)), pl.BlockSpec((64, tn), lambda j: (0, j)),
                  pl.BlockSpec((D, tn), lambda j: (0, j))],
        out_specs=[pl.BlockSpec((D, tn), lambda j: (0, j)), pl.BlockSpec((8, D), lambda j: (0, 0))],
        compiler_params=_cp(vmem=VMEM_BIG),
    )(cin, gmod_s, w_ada_s)


def _ada_small(gmod, parts, c_ctx):
    def body(g_ref, p_ref, c_ref, gb_ref, gc_ref):
        gb_ref[...] = jnp.sum(g_ref[...], axis=0, keepdims=True)
        ds = p_ref[0][0:1] + p_ref[2][0:1] + p_ref[4][0:1] + p_ref[6][0:1]
        cc = c_ref[...]
        s = _sigmoid(cc)
        gc_ref[...] = ds * (s * (1.0 + cc * (1.0 - s)))

    vm = pl.BlockSpec(memory_space=pltpu.VMEM)
    return pl.pallas_call(
        body, name="ada_small",
        out_shape=[jax.ShapeDtypeStruct((1, 6 * D), F32), jax.ShapeDtypeStruct((1, D), F32)],
        in_specs=[vm, vm, vm], out_specs=[vm, vm],
    )(gmod, parts, c_ctx)


def _tok_specs():
    xs = pl.BlockSpec((None, RT, D), lambda b, j: (b, jnp.maximum(j - 1, 0), 0))
    cs = pl.BlockSpec((None, RT, D), lambda b, j: (b, 0, 0))
    ms = pl.BlockSpec((None, 1, D), lambda b, j: (b, 0, 0))
    cms = pl.BlockSpec((None, 1, D), lambda b, j: (0, 0, 0))
    return xs, cs, ms, cms


def _inproj_fwd(x, ctx, sh1, sc1, csh1, csc1, n1w, w_in_p):
    def body(x_ref, c_ref, sh_ref, sc_ref, csh_ref, csc_ref, w1_ref, w_ref, p_ref, hm_ref):
        is_ctx = pl.program_id(1) == 0
        xt = jnp.where(is_ctx, c_ref[...], x_ref[...])
        sh = jnp.where(is_ctx, csh_ref[...], sh_ref[...])
        sc = jnp.where(is_ctx, csc_ref[...], sc_ref[...])
        hb = _norm_mod(xt, w1_ref[...], sh, sc).astype(BF16)
        hm_ref[...] = hb
        p_ref[...] = lax.dot_general(hb, w_ref[...], _NT, preferred_element_type=F32)

    xs, cs, ms, cms = _tok_specs()
    return pl.pallas_call(
        body, name="inproj_fwd", grid=(NB, T // RT),
        out_shape=[jax.ShapeDtypeStruct((NB, T, PC), F32), jax.ShapeDtypeStruct((NB, T, D), BF16)],
        in_specs=[xs, cs, ms, ms, cms, cms, pl.BlockSpec((1, D), lambda b, j: (0, 0)),
                  pl.BlockSpec((PC, D), lambda b, j: (0, 0))],
        out_specs=[pl.BlockSpec((None, RT, PC), lambda b, j: (b, j, 0)),
                   pl.BlockSpec((None, RT, D), lambda b, j: (b, j, 0))],
        compiler_params=_cp(vmem=VMEM_BIG),
    )(x, ctx, sh1, sc1, csh1, csc1, n1w, w_in_p)


def _inproj_bwd(dp_dn, dz, dp_na, dgt, w_in_p, x, ctx, sh1, sc1, csh1, csc1, n1w, dx1):
    def body(d1q_ref, d1k_ref, d1v_ref, d2_ref, d3_ref, d4_ref, w_ref, x_ref, c_ref, sh_ref, sc_ref, csh_ref, csc_ref,
             w1_ref, dx1_ref, dpb_ref, gx_ref, dsh_ref, dsc_ref, dcsh_ref, dcsc_ref, dw1_ref):
        b, j = pl.program_id(0), pl.program_id(1)
        is_ctx = j == 0
        dp = jnp.concatenate([d1q_ref[...], d1k_ref[...], d1v_ref[...], d2_ref[...], d3_ref[...], d4_ref[...]],
                             axis=1).astype(BF16)
        dpb_ref[...] = dp
        dhm = jnp.dot(dp, w_ref[...], preferred_element_type=F32)
        xt = jnp.where(is_ctx, c_ref[...], x_ref[...])
        sh = jnp.where(is_ctx, csh_ref[...], sh_ref[...])
        sc = jnp.where(is_ctx, csc_ref[...], sc_ref[...])
        _, vjp = jax.vjp(_norm_mod, xt, w1_ref[...], sh, sc)
        dxt, dw1, dsh, dsc = vjp(dhm)

        @pl.when((b == 0) & (j == 0))
        def _():
            dw1_ref[...] = jnp.zeros_like(dw1_ref)
            dcsh_ref[...] = jnp.zeros_like(dcsh_ref)
            dcsc_ref[...] = jnp.zeros_like(dcsc_ref)

        dw1_ref[...] += dw1

        @pl.when(is_ctx)
        def _():
            dcsh_ref[...] += dsh
            dcsc_ref[...] += dsc

        @pl.when(j == 1)
        def _():
            dsh_ref[...] = jnp.zeros_like(dsh_ref)
            dsc_ref[...] = jnp.zeros_like(dsc_ref)

        @pl.when(j >= 1)
        def _():
            dsh_ref[...] += dsh
            dsc_ref[...] += dsc
            gx_ref[...] = dx1_ref[...] + dxt

    xs, cs, ms, cms = _tok_specs()
    tsp = lambda w: pl.BlockSpec((None, RT, w), lambda b, j: (b, j, 0))
    one = pl.BlockSpec((1, D), lambda b, j: (0, 0))
    return pl.pallas_call(
        body, name="inproj_bwd", grid=(NB, T // RT),
        out_shape=[jax.ShapeDtypeStruct((NB, T, PC), BF16), jax.ShapeDtypeStruct((NB, SEQ, D), F32),
                   jax.ShapeDtypeStruct((NB, 1, D), F32), jax.ShapeDtypeStruct((NB, 1, D), F32),
                   jax.ShapeDtypeStruct((1, 1, D), F32), jax.ShapeDtypeStruct((1, 1, D), F32),
                   jax.ShapeDtypeStruct((1, D), F32)],
        in_specs=[tsp(512), tsp(512), tsp(512), tsp(512), tsp(1536), tsp(128), pl.BlockSpec((PC, D), lambda b, j: (0, 0)),
                  xs, cs, ms, ms, cms, cms, one, _rows2d_spec(D)],
        out_specs=[tsp(PC), xs, ms, ms, cms, cms, one],
        compiler_params=_cp(vmem=VMEM_BIG),
    )(*dp_dn, dz, dp_na, dgt, w_in_p, x, ctx, sh1, sc1, csh1, csc1, n1w, dx1)


def _shift_masks():
    t = lax.broadcasted_iota(jnp.int32, (T, 1), 0)
    lo = jnp.where(t < CT, 0, CT)
    hi = jnp.where(t < CT, CT, T)
    return {off: (t + off >= lo) & (t + off < hi) for off in range(-2, 3)}


def _shifted(u, off, masks):
    s = u if off == 0 else pltpu.roll(u, (-off) % T, 0)
    return jnp.where(masks[off], s, 0.0)


def _conv_pre(u, w, masks):
    acc = _shifted(u, -2, masks) * w[0:1, :]
    for kk in range(1, 5):
        acc = acc + _shifted(u, kk - 2, masks) * w[kk:kk + 1, :]
    return acc


def _act_q(pre):
    a = _silu(pre)
    return a * lax.rsqrt(jnp.sum(a * a, axis=-1, keepdims=True) + EPS) * (128.0 ** -0.5)


def _act_k(pre):
    a = _silu(pre)
    return a * lax.rsqrt(jnp.sum(a * a, axis=-1, keepdims=True) + EPS)


def _dn_prep_fwd(p, cw):
    def body(pq_ref, pk_ref, pv_ref, wq_ref, wk_ref, wv_ref, q_ref, k_ref, v_ref):
        masks = _shift_masks()
        q_ref[...] = _act_q(_conv_pre(pq_ref[...], wq_ref[...], masks))
        k_ref[...] = _act_k(_conv_pre(pk_ref[...], wk_ref[...], masks))
        v_ref[...] = _silu(_conv_pre(pv_ref[...], wv_ref[...], masks))

    ps = lambda o: pl.BlockSpec((None, T, 128), lambda b, h: (b, 0, o + h))
    ws = lambda o: pl.BlockSpec((8, 128), lambda b, h: (0, o + h))
    os_ = pl.BlockSpec((None, T, 128), lambda b, h: (b, 0, h))
    return pl.pallas_call(
        body, name="dn_prep_fwd", grid=(NB, DNH), out_shape=[jax.ShapeDtypeStruct((NB, T, DNW), F32)] * 3,
        in_specs=[ps(0), ps(4), ps(8), ws(0), ws(4), ws(8)], out_specs=[os_] * 3,
        compiler_params=_cp(vmem=VMEM_BIG),
    )(p, p, p, cw, cw, cw)


def _dn_prep_bwd(p, cw, dqs, dks, dvs):
    def body(pq_ref, pk_ref, pv_ref, wq_ref, wk_ref, wv_ref, dqf_ref, dqb_ref, dkf_ref, dkb_ref, dvf_ref, dvb_ref,
             oq_ref, ok_ref, ov_ref, gq_ref, gk_ref, gv_ref):
        b = pl.program_id(1)
        masks = _shift_masks()

        def one(p_ref, w_ref, df_ref, db_ref, act, o_ref, g_ref):
            u = p_ref[...]
            w = w_ref[...]
            _, vjp = jax.vjp(act, _conv_pre(u, w, masks))
            (dpre,) = vjp(df_ref[...] + db_ref[...])
            du = _shifted(dpre, 2, masks) * w[0:1, :]
            rows = [jnp.sum(dpre * _shifted(u, -2, masks), axis=0, keepdims=True)]
            for kk in range(1, 5):
                du = du + _shifted(dpre, 2 - kk, masks) * w[kk:kk + 1, :]
                rows.append(jnp.sum(dpre * _shifted(u, kk - 2, masks), axis=0, keepdims=True))
            o_ref[...] = du
            gw = jnp.concatenate(rows + [jnp.zeros((3, 128), F32)], axis=0)

            @pl.when(b == 0)
            def _():
                g_ref[...] = gw

            @pl.when(b > 0)
            def _():
                g_ref[...] += gw

        one(pq_ref, wq_ref, dqf_ref, dqb_ref, _act_q, oq_ref, gq_ref)
        one(pk_ref, wk_ref, dkf_ref, dkb_ref, _act_k, ok_ref, gk_ref)
        one(pv_ref, wv_ref, dvf_ref, dvb_ref, _silu, ov_ref, gv_ref)

    ps = lambda o: pl.BlockSpec((None, T, 128), lambda h, b: (b, 0, o + h))
    ws = lambda o: pl.BlockSpec((8, 128), lambda h, b: (0, o + h))
    ds = pl.BlockSpec((None, T, 128), lambda h, b: (b, 0, h))
    outs = pl.pallas_call(
        body, name="dn_prep_bwd", grid=(DNH, NB),
        out_shape=[jax.ShapeDtypeStruct((NB, T, DNW), F32)] * 3 + [jax.ShapeDtypeStruct((8, DNW), F32)] * 3,
        in_specs=[ps(0), ps(4), ps(8), ws(0), ws(4), ws(8)] + [ds] * 6,
        out_specs=[ds] * 3
        + [pl.BlockSpec((8, 128), lambda h, b: (0, h))] * 3,
        compiler_params=_cp(vmem=VMEM_BIG),
    )(p, p, p, cw, cw, cw, *dqs, *dks, *dvs)
    return outs


def _gate_fn(a, prm):
    lane = lax.broadcasted_iota(jnp.int32, a.shape, 1)
    z = a + prm[1:2, :]
    sp = jnp.maximum(z, 0.0) + jnp.log(1.0 + jnp.exp(-jnp.abs(z)))
    g = -jnp.exp(prm[0:1, :]) * sp
    return jnp.where(lane < 8, g, jnp.where(lane < 16, _sigmoid(a), 0.0))


def _gates_fwd(p, prm):
    def body(a_ref, prm_ref, o_ref):
        o_ref[...] = _gate_fn(a_ref[...], prm_ref[...])

    return pl.pallas_call(
        body, name="gates_fwd", grid=(NB,), out_shape=jax.ShapeDtypeStruct((NB, T, 128), F32),
        in_specs=[pl.BlockSpec((None, T, 128), lambda b: (b, 0, COL_G // 128)), pl.BlockSpec((8, 128), lambda b: (0, 0))],
        out_specs=pl.BlockSpec((None, T, 128), lambda b: (b, 0, 0)),
    )(p, prm)


def _gates_bwd(p, prm, dgf, dgb):
    def body(a_ref, prm_ref, dgf_ref, dgb_ref, o_ref, dprm_ref):
        b = pl.program_id(0)
        tot = dgf_ref[...] + dgb_ref[...]
        _, vjp = jax.vjp(_gate_fn, a_ref[...], prm_ref[...])
        da, dprm = vjp(tot)
        o_ref[...] = da

        @pl.when(b == 0)
        def _():
            dprm_ref[...] = dprm

        @pl.when(b > 0)
        def _():
            dprm_ref[...] += dprm

    return pl.pallas_call(
        body, name="gates_bwd", grid=(NB,),
        out_shape=[jax.ShapeDtypeStruct((NB, T, 128), F32), jax.ShapeDtypeStruct((8, 128), F32)],
        in_specs=[pl.BlockSpec((None, T, 128), lambda b: (b, 0, COL_G // 128)), pl.BlockSpec((8, 128), lambda b: (0, 0)),
                  pl.BlockSpec((None, T, 128), lambda b: (b, 0, 0)), pl.BlockSpec((None, T, 128), lambda b: (b, 0, 0))],
        out_specs=[pl.BlockSpec((None, T, 128), lambda b: (b, 0, 0)), pl.BlockSpec((8, 128), lambda b: (0, 0))],
        compiler_params=_cp(vmem=VMEM_BIG),
    )(p, prm, dgf, dgb)


def _lockstep(gens, joint=None):
    n = len(gens)
    res, vals = [None] * n, [None] * n
    while True:
        msgs = []
        for i, g in enumerate(gens):
            try:
                msgs.append((g.send(vals[i]), False))
            except StopIteration as e:
                res[i] = e.value
                msgs.append((None, True))
        if msgs[0][1]:
            return res
        if msgs[0][0] is not None:
            vals = list(joint(tuple(m[0] for m in msgs)))
        else:
            vals = [None] * n


def _tri_inv_gen(m, md, m1, m2, eye):
    n = -(m * md)
    t = eye + n
    pw = n
    for _ in range(3):
        pw = _dg(pw, pw, _NN)
        yield
        t = t + _dg(t, pw, _NN)
        yield
    a = _dg(t, m * m1, _NN)
    yield
    t = t - _dg(a, t, _NN)
    yield
    a = _dg(t, m * m2, _NN)
    yield
    return t - _dg(a, t, _NN)


def _tri_inv_bwd_gen(t, g):
    tt = t.T
    a = _dg(tt, g, _NN)
    yield
    return -_dg(a, tt, _NN)


@jax.custom_vjp
def _tri_inv(ms, md, m1, m2, eye):
    return tuple(_lockstep([_tri_inv_gen(m, md, m1, m2, eye) for m in ms]))


def _tri_inv_f(ms, md, m1, m2, eye):
    ts = _tri_inv(ms, md, m1, m2, eye)
    return ts, (ts, md)


def _tri_inv_b(res, gs):
    ts, md = res
    z = jnp.zeros_like(md)
    return (tuple(_lockstep([_tri_inv_bwd_gen(t, g) for t, g in zip(ts, gs)])), z, z, z, z)


_tri_inv.defvjp(_tri_inv_f, _tri_inv_b)


@jax.custom_vjp
def _tri_inv_saved(ms, ts):
    return tuple(t.astype(F32) for t in ts)


def _tri_inv_saved_f(ms, ts):
    return _tri_inv_saved(ms, ts), ts


def _tri_inv_saved_b(ts, gs):
    dms = tuple(_lockstep([_tri_inv_bwd_gen(t, g) for t, g in zip(ts, gs)]))
    return dms, tuple(jnp.zeros_like(t) for t in ts)


_tri_inv_saved.defvjp(_tri_inv_saved_f, _tri_inv_saved_b)


SR = DNH * CH


def _scan_masks():
    i, j = np.arange(SR)[:, None], np.arange(SR)[None, :]
    same = (i >> 6) == (j >> 6)
    cm = np.zeros((2, 6, SR, SR), np.float32)
    for d in range(2):
        diff = i - j if d == 0 else j - i
        cm[d, 0] = same & (diff >= 0)
        cm[d, 1] = same & (diff > 0)
        cm[d, 2] = (i >> 4) == (j >> 4)
        cm[d, 3] = ((i >> 5) == (j >> 5)) & ((i >> 4) != (j >> 4))
        cm[d, 4] = (i >> 5) != (j >> 5)
        cm[d, 5] = i == j
    t, u = np.arange(CH)[:, None], np.arange(CH)[None, :]
    c64 = np.stack([t >= u, t <= u]).astype(np.float32)
    return jnp.asarray(cm), jnp.asarray(c64)


def _hsl(h):
    return slice(CH * h, CH * h + CH), slice(128 * h, 128 * h + 128)


@jax.custom_vjp
def _heads_mm(x, s):
    return jnp.concatenate([_dg(x[_hsl(h)[0]], s[:, _hsl(h)[1]], _NN) for h in range(DNH)], axis=0)


def _heads_mm_b(res, g):
    x, s = res
    dx = jnp.concatenate([_dg(g[_hsl(h)[0]], s[:, _hsl(h)[1]], _NT) for h in range(DNH)], axis=0)
    ds = jnp.concatenate([_dg(x[_hsl(h)[0]], g[_hsl(h)[0]], _TN) for h in range(DNH)], axis=1)
    return dx, ds


_heads_mm.defvjp(lambda x, s: (_heads_mm(x, s), (x, s)), _heads_mm_b)


@jax.custom_vjp
def _heads_outer(k, v):
    return jnp.concatenate([_dg(k[_hsl(h)[0]], v[_hsl(h)[0]], _TN) for h in range(DNH)], axis=1)


def _heads_outer_b(res, g):
    k, v = res
    dk = jnp.concatenate([_dg(v[_hsl(h)[0]], g[:, _hsl(h)[1]], _NT) for h in range(DNH)], axis=0)
    dv = jnp.concatenate([_dg(k[_hsl(h)[0]], g[:, _hsl(h)[1]], _NN) for h in range(DNH)], axis=0)
    return dk, dv


_heads_outer.defvjp(lambda k, v: (_heads_outer(k, v), (k, v)), _heads_outer_b)


@jax.custom_vjp
def _mm_shared_lhs(t, a, b):
    r = _dg(t, jnp.concatenate([a, b], axis=1), _NN)
    return r[:, :a.shape[1]], r[:, a.shape[1]:]


def _mm_shared_lhs_b(res, gs):
    t, a, b = res
    g = jnp.concatenate(gs, axis=1)
    dab = _dg(t, g, _TN)
    return _dg(g, jnp.concatenate([a, b], axis=1), _NT), dab[:, :a.shape[1]], dab[:, a.shape[1]:]


_mm_shared_lhs.defvjp(lambda t, a, b: (_mm_shared_lhs(t, a, b), (t, a, b)), _mm_shared_lhs_b)


@jax.custom_vjp
def _mm_nt_shared_rhs(a, b, k):
    r = _dg(jnp.concatenate([a, b], axis=0), k, _NT)
    return r[:a.shape[0]], r[a.shape[0]:]


def _mm_nt_shared_rhs_b(res, gs):
    a, b, k = res
    g = jnp.concatenate(gs, axis=0)
    dab = _dg(g, k, _NN)
    return dab[:a.shape[0]], dab[a.shape[0]:], _dg(g, jnp.concatenate([a, b], axis=0), _TN)


_mm_nt_shared_rhs.defvjp(lambda a, b, k: (_mm_nt_shared_rhs(a, b, k), (a, b, k)), _mm_nt_shared_rhs_b)


def _chunk_gen(q, k, v, gb, s, cm, c64, d):
    incl, strict = cm[0], cm[1]
    lane = lax.broadcasted_iota(jnp.int32, (1, 128), 1)
    oh = lambda n: jnp.where(lane == n, 1.0, 0.0).astype(F32)
    col = lambda x, base: jnp.concatenate(
        [jnp.sum(x * oh(base + d * DNH + h), axis=1, keepdims=True) for h in range(DNH)], axis=0)
    beta = col(gb, 8)
    gam = col(_mask_mm(c64, gb), 0)
    yield
    tot = jnp.sum(gb, axis=0, keepdims=True)
    gtots = [jnp.sum(tot * oh(d * DNH + h), axis=1, keepdims=True) for h in range(DNH)]
    gtot = jnp.concatenate([jnp.broadcast_to(t, (CH, 1)) for t in gtots], axis=0)
    eg = jnp.concatenate([jnp.broadcast_to(jnp.exp(t), (1, 128)) for t in gtots], axis=1)
    row0 = jnp.where(lax.broadcasted_iota(jnp.int32, (128, 1), 0) == 0, 1.0, 0.0).astype(F32)
    gam_row = jnp.sum(jnp.broadcast_to(gam, (SR, 128)).T * row0, axis=0, keepdims=True)
    dec = jnp.exp(jnp.where(incl > 0.5, gam - gam_row, NEG))
    e_gam = jnp.exp(gam)
    kb = k * beta
    kk, qk = _mm_nt_shared_rhs(kb, q, k)
    yield
    tm = yield strict * (kk * dec)
    u, w = _mm_shared_lhs(tm, v * beta, kb * e_gam)
    yield
    qk = qk * dec
    ws = _heads_mm(w, s)
    yield
    v_new = u - ws
    qs = _heads_mm(q * e_gam, s)
    yield
    o = qs + _mm(qk, v_new)
    yield
    s_new = s * eg + _heads_outer(k * jnp.exp(gtot - gam), v_new)
    return o, s_new, tm


def _chunks(args, cm_ref, c64_ref, saved=None):
    cm0 = cm_ref[0]
    if saved is None:
        inv = lambda ms: _tri_inv(ms, cm0[2], cm0[3], cm0[4], cm0[5])
    else:
        inv = lambda ms: _tri_inv_saved(ms, saved)
    return _lockstep([_chunk_gen(q, k, v, gb, s, cm_ref[d], c64_ref[d], d) for q, k, v, gb, s, d in args], inv)


def _chunk_rev(s):
    return jnp.where(s < NCH_CTX, NCH_CTX - 1 - s, NCH + NCH_CTX - 1 - s)


def _scan_specs(step):
    cf = lambda s: step(s)
    cb = lambda s: _chunk_rev(step(s))
    tok = lambda c: pl.BlockSpec((NB, CH, DNW), lambda s: (0, c(s), 0))
    gat = lambda c: pl.BlockSpec((NB, CH, 128), lambda s: (0, c(s), 0))
    sta = lambda c: pl.BlockSpec((NB, None, 128, DNW), lambda s: (0, c(s), 0, 0))
    inv = lambda c: pl.BlockSpec((NB, None, SR, SR), lambda s: (0, c(s), 0, 0))
    return (tok(cf), gat(cf), sta(cf), inv(cf)), (tok(cb), gat(cb), sta(cb), inv(cb))


def _const_specs():
    return [pl.BlockSpec((2, 6, SR, SR), lambda s: (0, 0, 0, 0)), pl.BlockSpec((2, CH, CH), lambda s: (0, 0, 0))]


SCAN_CHAINS = [(b, d) for b in range(NB) for d in range(2)]


def _stack(ref, b):
    return jnp.concatenate([ref[b, :, 128 * h:128 * h + 128] for h in range(DNH)], axis=0)


def _unstack_to(ref, b, val):
    for h in range(DNH):
        ref[b, :, 128 * h:128 * h + 128] = val[CH * h:CH * h + CH]


def _gather_plan(ins, outs, pos):
    me = _slot(pos)
    locals_, xfers = [], []
    for src, dst in zip(ins, outs):
        locals_.append((src, dst.at[me]))
        for k in range(1, 8):
            peer = _flip(pos, k)
            xfers.append((src, dst.at[me], dst.at[_slot(peer)], peer))
    return locals_, xfers


def _scan_fwd(q, k, v, gates, carry=()):
    nc = len(carry)

    def body(*refs):
        cm_ref, c64_ref, qf, kf, vf, gf, qb, kb, vb, gb = refs[:10]
        c_in = refs[10:10 + nc]
        of_ref, sf_ref, tf_ref, ob_ref, sb_ref, tb_ref = refs[10 + nc:16 + nc]
        c_out = refs[16 + nc:16 + 2 * nc]
        s_ref = refs[16 + 2 * nc]
        sems = refs[17 + 2 * nc:]
        if nc:
            @pl.when(pl.program_id(0) == 0)
            def _():
                _xfer_start(_gather_plan(c_in, c_out, _my_pos()), sems)

        @pl.when(pl.program_id(0) == 0)
        def _():
            s_ref[...] = jnp.zeros_like(s_ref)

        dirs = ((qf, kf, vf, gf, of_ref, sf_ref, tf_ref), (qb, kb, vb, gb, ob_ref, sb_ref, tb_ref))
        args = [(_stack(dirs[d][0], b), _stack(dirs[d][1], b), _stack(dirs[d][2], b), dirs[d][3][b], s_ref[i], d)
                for i, (b, d) in enumerate(SCAN_CHAINS)]
        for i, (o, s_new, tm) in enumerate(_chunks(args, cm_ref, c64_ref)):
            b, d = SCAN_CHAINS[i]
            dirs[d][5][b] = args[i][4]
            dirs[d][6][b] = tm.astype(BF16)
            _unstack_to(dirs[d][4], b, o)
            s_ref[i] = s_new

        if nc:
            @pl.when(pl.program_id(0) == NCH - 1)
            def _():
                _xfer_wait(_gather_plan(c_in, c_out, _my_pos()), sems)

    (tf, gf_, sf_, if_), (tb, gb_, sb_, ib_) = _scan_specs(lambda s: s)
    o_sh = jax.ShapeDtypeStruct((NB, T, DNW), F32)
    st_sh = jax.ShapeDtypeStruct((NB, NCH, 128, DNW), F32)
    inv_sh = jax.ShapeDtypeStruct((NB, NCH, SR, SR), BF16)
    hbm = pl.BlockSpec(memory_space=pltpu.HBM)
    return pl.pallas_call(
        body, name="scan_fwd", grid=(NCH,),
        out_shape=[o_sh, st_sh, inv_sh, o_sh, st_sh, inv_sh] + [jax.ShapeDtypeStruct((8,) + h.shape, h.dtype) for h in carry],
        in_specs=_const_specs() + [tf, tf, tf, gf_, tb, tb, tb, gb_] + [hbm] * nc,
        out_specs=[tf, sf_, if_, tb, sb_, ib_] + [hbm] * nc,
        scratch_shapes=[pltpu.VMEM((len(SCAN_CHAINS), 128, DNW), F32)] + (_xfer_sems(nc, 7 * nc) if nc else []),
        compiler_params=_cp(vmem=VMEM_BIG),
    )(*_scan_masks(), q, k, v, gates, q, k, v, gates, *carry)


def _scan_bwd(q, k, v, gates, st_f, st_b, inv_f, inv_b, do, carry=()):
    nc = len(carry)

    def body(*refs):
        (cm_ref, c64_ref, qf, kf, vf, gf, sf, tf_, dof, qb, kb, vb, gb, sb, tb_, dob) = refs[:16]
        c_in = refs[16:16 + nc]
        dqf, dkf, dvf, dgf, dqb, dkb, dvb, dgb = refs[16 + nc:24 + nc]
        c_out = refs[24 + nc:24 + 2 * nc]
        ds_ref = refs[24 + 2 * nc]
        sems = refs[25 + 2 * nc:]
        first = pl.program_id(0) == 0
        last = pl.program_id(0) == NCH - 1
        if nc:
            @pl.when(first)
            def _():
                _xfer_start(_scatter_plan(c_in, c_out, _my_pos()), sems)

        @pl.when(first)
        def _():
            ds_ref[...] = jnp.zeros_like(ds_ref)

        dirs = ((qf, kf, vf, gf, sf, dof, dqf, dkf, dvf, dgf), (qb, kb, vb, gb, sb, dob, dqb, dkb, dvb, dgb))
        prim = tuple((_stack(dirs[d][0], b), _stack(dirs[d][1], b), _stack(dirs[d][2], b), dirs[d][3][b], dirs[d][4][b])
                     for b, d in SCAN_CHAINS)
        cots = tuple((_stack(dirs[d][5], b), ds_ref[i]) for i, (b, d) in enumerate(SCAN_CHAINS))
        saved = tuple((tf_, tb_)[d][b] for b, d in SCAN_CHAINS)
        _, vjp = jax.vjp(lambda ps: tuple(r[:2] for r in _chunks([p + (SCAN_CHAINS[i][1],) for i, p in enumerate(ps)],
                                                                cm_ref, c64_ref, saved)), prim)
        (grads,) = vjp(cots)
        for i, (dq, dk, dv, dg, ds) in enumerate(grads):
            b, d = SCAN_CHAINS[i]
            _unstack_to(dirs[d][6], b, dq)
            _unstack_to(dirs[d][7], b, dk)
            _unstack_to(dirs[d][8], b, dv)
            dirs[d][9][b] = dg
            ds_ref[i] = ds

        if nc:
            @pl.when(last)
            def _():
                _xfer_wait(_scatter_plan(c_in, c_out, _my_pos()), sems)

    (tf, gf_, sf_, if_), (tb, gb_, sb_, ib_) = _scan_specs(lambda s: NCH - 1 - s)
    o_sh = jax.ShapeDtypeStruct((NB, T, DNW), F32)
    g_sh = jax.ShapeDtypeStruct((NB, T, 128), F32)
    hbm = pl.BlockSpec(memory_space=pltpu.HBM)
    return pl.pallas_call(
        body, name="scan_bwd", grid=(NCH,),
        out_shape=[o_sh, o_sh, o_sh, g_sh] * 2 + [jax.ShapeDtypeStruct(g.shape, g.dtype) for g in carry],
        in_specs=_const_specs() + [tf, tf, tf, gf_, sf_, if_, tf, tb, tb, tb, gb_, sb_, ib_, tb] + [hbm] * nc,
        out_specs=[tf, tf, tf, gf_, tb, tb, tb, gb_] + [hbm] * nc,
        scratch_shapes=[pltpu.VMEM((len(SCAN_CHAINS), 128, DNW), F32)] + (_xfer_sems(nc, 7 * nc) if nc else []),
        compiler_params=_cp(vmem=VMEM_BIG),
    )(*_scan_masks(), q, k, v, gates, st_f, inv_f, do, q, k, v, gates, st_b, inv_b, do, *carry)


def _segmean(a, bd):
    hi = a.astype(BF16)
    lo = (a - hi.astype(F32)).astype(BF16)
    return jnp.dot(hi, bd, preferred_element_type=F32) + jnp.dot(lo, bd, preferred_element_type=F32)


def _na_prep_fwd(p, qw, kw, bd):
    def body(q_ref, k_ref, v_ref, qw_ref, kw_ref, bd_ref, oq_ref, ok_ref, ov_ref):
        bdm = bd_ref[...]
        q = q_ref[...]
        k = k_ref[...]
        oq_ref[...] = (q * lax.rsqrt(_segmean(q * q, bdm) + EPS) * qw_ref[...] * 0.125).astype(BF16)
        ok_ref[...] = (k * lax.rsqrt(_segmean(k * k, bdm) + EPS) * kw_ref[...]).astype(BF16)
        ov_ref[...] = v_ref[...].astype(BF16)

    ps = lambda o: pl.BlockSpec((None, RT, NAW), lambda b, j: (b, j, COL_NA // NAW + o))
    one = pl.BlockSpec((1, NAW), lambda b, j: (0, 0))
    os_ = pl.BlockSpec((None, RT, NAW), lambda b, j: (b, j, 0))
    return pl.pallas_call(
        body, name="na_prep_fwd", grid=(NB, T // RT), out_shape=[jax.ShapeDtypeStruct((NB, T, NAW), BF16)] * 3,
        in_specs=[ps(0), ps(1), ps(2), one, one, pl.BlockSpec((NAW, NAW), lambda b, j: (0, 0))], out_specs=[os_] * 3,
    )(p, p, p, qw, kw, bd)


def _na_prep_bwd(p, qw, kw, bd, dq, dk, dv):
    def body(q_ref, k_ref, qw_ref, kw_ref, bd_ref, dq_ref, dk_ref, dv_ref, o_ref, gq_ref, gk_ref):
        b, j = pl.program_id(0), pl.program_id(1)
        bdm = bd_ref[...]

        def one(x, w, dy, scale):
            r = lax.rsqrt(_segmean(x * x, bdm) + EPS)
            g = dy * w * scale
            dx = r * g - x * (r * r * r) * _segmean(g * x, bdm)
            dw = jnp.sum(dy * x * r * scale, axis=0, keepdims=True)
            return dx, dw

        dqv = jnp.where(j == 0, 0.0, dq_ref[...])
        dxq, dwq = one(q_ref[...], qw_ref[...], dqv, 0.125)
        dxk, dwk = one(k_ref[...], kw_ref[...], dk_ref[...], 1.0)
        o_ref[...] = jnp.concatenate([dxq, dxk, dv_ref[...]], axis=1)
        pad = jnp.zeros((7, NAW), F32)

        @pl.when((b == 0) & (j == 0))
        def _():
            gq_ref[...] = jnp.zeros_like(gq_ref)
            gk_ref[...] = jnp.zeros_like(gk_ref)

        gq_ref[...] += jnp.concatenate([dwq, pad], axis=0)
        gk_ref[...] += jnp.concatenate([dwk, pad], axis=0)

    ps = lambda o: pl.BlockSpec((None, RT, NAW), lambda b, j: (b, j, COL_NA // NAW + o))
    one = pl.BlockSpec((1, NAW), lambda b, j: (0, 0))
    ts = pl.BlockSpec((None, RT, NAW), lambda b, j: (b, j, 0))
    acc = pl.BlockSpec((8, NAW), lambda b, j: (0, 0))
    return pl.pallas_call(
        body, name="na_prep_bwd", grid=(NB, T // RT),
        out_shape=[jax.ShapeDtypeStruct((NB, T, 1536), F32), jax.ShapeDtypeStruct((8, NAW), F32),
                   jax.ShapeDtypeStruct((8, NAW), F32)],
        in_specs=[ps(0), ps(1), one, one, pl.BlockSpec((NAW, NAW), lambda b, j: (0, 0)),
                  pl.BlockSpec((None, RT, NAW), lambda b, j: (b, jnp.maximum(j - 1, 0), 0)), ts, ts],
        out_specs=[pl.BlockSpec((None, RT, 1536), lambda b, j: (b, j, 0)), acc, acc],
        compiler_params=_cp(vmem=VMEM_BIG),
    )(p, p, qw, kw, bd, dq, dk, dv)


def _na_row0(r):
    return jnp.clip(r - WIN_ROWS // 2, 0, ROWS - WIN_ROWS)


def _na_cls(r):
    return _na_row0(r) - r + WIN_ROWS - 1


NHS = 4
NLW = NHS * 64
NSR = NHS * GRID_W


def _na_stack(x):
    lane = lax.broadcasted_iota(jnp.int32, (1, NLW), 1)
    return jnp.concatenate([jnp.where((lane >> 6) == h, x, jnp.zeros_like(x)) for h in range(NHS)], axis=0)


def _na_unstack(y):
    lane = lax.broadcasted_iota(jnp.int32, (1, NLW), 1)
    out = jnp.zeros((GRID_W, NLW), y.dtype)
    for h in range(NHS):
        out = jnp.where((lane >> 6) == h, y[GRID_W * h:GRID_W * h + GRID_W], out)
    return out


def _na_fwd_gen(q, kw, vw, kc, vc, bias):
    q4 = _na_stack(q)
    s_w = lax.dot_general(q4, kw, _NT, preferred_element_type=F32) + bias
    yield
    s_c = lax.dot_general(q4, kc, _NT, preferred_element_type=F32)
    yield
    mx = jnp.maximum(jnp.max(s_w, axis=1, keepdims=True), jnp.max(s_c, axis=1, keepdims=True))
    p_w = jnp.exp(s_w - mx)
    p_c = jnp.exp(s_c - mx)
    inv = 1.0 / (jnp.sum(p_w, axis=1, keepdims=True) + jnp.sum(p_c, axis=1, keepdims=True))
    yield
    o_w = jnp.dot(p_w.astype(BF16), vw, preferred_element_type=F32)
    yield
    o4 = (o_w + jnp.dot(p_c.astype(BF16), vc, preferred_element_type=F32)) * inv
    return _na_unstack(o4)


NRF = 2


def _na_fwd(qn, kn, vb, bias, carry=()):
    nc = len(carry)
    grid = (NB, ROWS // NRF)
    ng = NAH // NHS

    def body(*refs):
        q_ref, k_ref, v_ref = refs[:3]
        b_refs, rest = refs[3:3 + NRF], refs[3 + NRF:]
        c_in, o_ref, c_out, sems = rest[:nc], rest[nc], rest[nc + 1:2 * nc + 1], rest[2 * nc + 1:]
        ids = [pl.program_id(a) for a in range(2)]
        if nc:
            @pl.when((ids[0] == 0) & (ids[1] == 0))
            def _():
                _xfer_start(_gather_plan(c_in, c_out, _my_pos()), sems)

        gens = []
        for i in range(NRF):
            start = pl.multiple_of(CT + _na_row0(NRF * ids[1] + i) * GRID_W, GRID_W)
            rw = slice(GRID_W * i, GRID_W * (i + 1))
            for g in range(ng):
                ln = slice(NLW * g, NLW * g + NLW)
                gens.append(_na_fwd_gen(q_ref[rw, ln], k_ref[pl.ds(start, NWIN), ln], v_ref[pl.ds(start, NWIN), ln],
                                        k_ref[0:CT, ln], v_ref[0:CT, ln], b_refs[i][g]))
        for n, o in enumerate(_lockstep(gens)):
            i, g = n // ng, n % ng
            o_ref[GRID_W * i:GRID_W * (i + 1), NLW * g:NLW * g + NLW] = o.astype(BF16)

        if nc:
            @pl.when((ids[0] == grid[0] - 1) & (ids[1] == grid[1] - 1))
            def _():
                _xfer_wait(_gather_plan(c_in, c_out, _my_pos()), sems)

    qs = pl.BlockSpec((None, NRF * GRID_W, NAW), lambda b, j: (b, CT // (NRF * GRID_W) + j, 0))
    kv = pl.BlockSpec((None, T, NAW), lambda b, j: (b, 0, 0))
    bss = [pl.BlockSpec((None, ng, NSR, NWIN), lambda b, j, i=i: (_na_cls(NRF * j + i), 0, 0, 0)) for i in range(NRF)]
    lat = pl.BlockSpec((None, NRF * GRID_W, NAW), lambda b, j: (b, j, 0))
    hbm = pl.BlockSpec(memory_space=pltpu.HBM)
    res = pl.pallas_call(
        body, name="na_fwd", grid=grid,
        out_shape=[jax.ShapeDtypeStruct((NB, SEQ, NAW), BF16)] + [jax.ShapeDtypeStruct((8,) + h.shape, h.dtype) for h in carry],
        in_specs=[qs, kv, kv] + bss + [hbm] * nc, out_specs=[lat] + [hbm] * nc,
        scratch_shapes=_xfer_sems(nc, 7 * nc) if nc else [],
        compiler_params=_cp(vmem=VMEM_BIG),
    )(qn, kn, vb, *([bias] * NRF), *carry)
    return res[0], list(res[1:])


def _na_bwd_gen(q, kw, vw, kc, vc, bias, do):
    q4 = _na_stack(q)
    s_w = lax.dot_general(q4, kw, _NT, preferred_element_type=F32) + bias
    yield
    s_c = lax.dot_general(q4, kc, _NT, preferred_element_type=F32)
    yield
    mx = jnp.maximum(jnp.max(s_w, axis=1, keepdims=True), jnp.max(s_c, axis=1, keepdims=True))
    p_w = jnp.exp(s_w - mx)
    p_c = jnp.exp(s_c - mx)
    inv = 1.0 / (jnp.sum(p_w, axis=1, keepdims=True) + jnp.sum(p_c, axis=1, keepdims=True))
    p_w = p_w * inv
    p_c = p_c * inv
    do4 = _na_stack(do.astype(BF16))
    dp_w = lax.dot_general(do4, vw, _NT, preferred_element_type=F32)
    yield
    dp_c = lax.dot_general(do4, vc, _NT, preferred_element_type=F32)
    yield
    delta = jnp.sum(p_w * dp_w, axis=1, keepdims=True) + jnp.sum(p_c * dp_c, axis=1, keepdims=True)
    ds_w = p_w * (dp_w - delta)
    ds_c = p_c * (dp_c - delta)
    ds_wb, ds_cb = ds_w.astype(BF16), ds_c.astype(BF16)
    dq4 = jnp.dot(ds_wb, kw, preferred_element_type=F32)
    yield
    dq4 = dq4 + jnp.dot(ds_cb, kc, preferred_element_type=F32)
    yield
    dkw = lax.dot_general(ds_wb, q4, _TN, preferred_element_type=F32)
    yield
    dvw = lax.dot_general(p_w.astype(BF16), do4, _TN, preferred_element_type=F32)
    yield
    dkc = lax.dot_general(ds_cb, q4, _TN, preferred_element_type=F32)
    yield
    dvc = lax.dot_general(p_c.astype(BF16), do4, _TN, preferred_element_type=F32)
    return ds_w, _na_unstack(dq4), dkw, dvw, dkc, dvc


NRS = 4


def _na_bwd(qn, kn, vb, bias, do):
    def body(q_ref, k_ref, v_ref, *rest):
        b_refs, (do_ref, dq_ref, dk_ref, dv_ref, db_ref) = rest[:NRS], rest[NRS:]
        b, j = pl.program_id(1), pl.program_id(2)

        @pl.when((b == 0) & (j == 0))
        def _():
            db_ref[...] = jnp.zeros_like(db_ref)

        @pl.when(j == 0)
        def _():
            dk_ref[...] = jnp.zeros_like(dk_ref)
            dv_ref[...] = jnp.zeros_like(dv_ref)

        kc, vc = k_ref[0:CT, :], v_ref[0:CT, :]
        rows = [NRS * j + i for i in range(NRS)]
        starts = [pl.multiple_of(CT + _na_row0(r) * GRID_W, GRID_W) for r in rows]
        gens = [_na_bwd_gen(q_ref[GRID_W * i:GRID_W * (i + 1), :], k_ref[pl.ds(starts[i], NWIN), :],
                            v_ref[pl.ds(starts[i], NWIN), :], kc, vc, b_refs[i][...],
                            do_ref[GRID_W * i:GRID_W * (i + 1), :]) for i in range(NRS)]
        for i, (ds_w, dq, dkw, dvw, dkc, dvc) in enumerate(_lockstep(gens)):
            db_ref[_na_cls(rows[i])] += ds_w
            dq_ref[GRID_W * i:GRID_W * (i + 1), :] = dq
            dk_ref[pl.ds(starts[i], NWIN), :] += dkw
            dv_ref[pl.ds(starts[i], NWIN), :] += dvw
            dk_ref[0:CT, :] += dkc
            dv_ref[0:CT, :] += dvc

    qs = pl.BlockSpec((None, NRS * GRID_W, NLW), lambda g, b, j: (b, CT // (NRS * GRID_W) + j, g))
    kv = pl.BlockSpec((None, T, NLW), lambda g, b, j: (b, 0, g))
    bss = [pl.BlockSpec((None, None, NSR, NWIN), lambda g, b, j, i=i: (_na_cls(NRS * j + i), g, 0, 0)) for i in range(NRS)]
    lat = pl.BlockSpec((None, NRS * GRID_W, NLW), lambda g, b, j: (b, j, g))
    return pl.pallas_call(
        body, name="na_bwd", grid=(NAH // NHS, NB, ROWS // NRS),
        out_shape=[jax.ShapeDtypeStruct((NB, SEQ, NAW), F32)] + [jax.ShapeDtypeStruct((NB, T, NAW), F32)] * 2
        + [jax.ShapeDtypeStruct((WIN_ROWS, NAH // NHS, NSR, NWIN), F32)],
        in_specs=[qs, kv, kv] + bss + [lat],
        out_specs=[lat, kv, kv, pl.BlockSpec((WIN_ROWS, None, NSR, NWIN), lambda g, b, j: (0, g, 0, 0))],
        compiler_params=_cp(vmem=VMEM_BIG),
    )(qn, kn, vb, *([bias] * NRS), do)


def _na_tables():
    cols = np.arange(GRID_W)
    win_start = np.clip(cols - WIN_COLS // 2, 0, GRID_W - WIN_COLS)
    kc = cols[None, :]
    valid = (kc >= win_start[:, None]) & (kc < win_start[:, None] + WIN_COLS)
    rel = np.clip(kc - cols[:, None] + WIN_COLS - 1, 0, 2 * WIN_COLS - 2)
    return valid, rel


def _bias_table(rpb):
    valid, rel = _na_tables()
    nrel = 2 * WIN_COLS - 1
    shifts = np.stack([(rel == r) & valid for r in range(nrel)]).astype(np.float32)
    neg = np.where(valid, 0.0, NEG).astype(np.float32)

    def body(rpb_ref, sh_ref, neg_ref, o_ref):
        h = pl.program_id(0)
        for dr in range(2 * WIN_ROWS - 1):
            tile = neg_ref[...]
            for r in range(nrel):
                tile = tile + rpb_ref[h, dr, r] * sh_ref[r]
            for cls in range(WIN_ROWS):
                i = dr - cls
                if 0 <= i < WIN_ROWS:
                    o_ref[cls, :, i * GRID_W:(i + 1) * GRID_W] = tile

    return pl.pallas_call(
        body, name="bias_table", grid=(NAH,),
        out_shape=jax.ShapeDtypeStruct((WIN_ROWS, NAH // NHS, NSR, NWIN), F32),
        in_specs=[pl.BlockSpec(memory_space=pltpu.SMEM), pl.BlockSpec((nrel, GRID_W, GRID_W), lambda h: (0, 0, 0)),
                  pl.BlockSpec((GRID_W, GRID_W), lambda h: (0, 0))],
        out_specs=pl.BlockSpec((WIN_ROWS, None, GRID_W, NWIN), lambda h: (0, h // NHS, h % NHS, 0)),
    )(rpb, jnp.asarray(shifts), jnp.asarray(neg))


def _rpb_fold(db):
    def body1(d_ref, o_ref):
        for dr in range(2 * WIN_ROWS - 1):
            acc = None
            for c in range(WIN_ROWS):
                i = dr - c
                if 0 <= i < WIN_ROWS:
                    blk = d_ref[c, :, i * GRID_W:(i + 1) * GRID_W]
                    acc = blk if acc is None else acc + blk
            o_ref[dr] = acc

    t15 = pl.pallas_call(
        body1, name="rpb_fold_rows", grid=(NAH,),
        out_shape=jax.ShapeDtypeStruct((NAH, 2 * WIN_ROWS - 1, GRID_W, GRID_W), F32),
        in_specs=[pl.BlockSpec((WIN_ROWS, None, GRID_W, NWIN), lambda h: (0, h, 0, 0))],
        out_specs=pl.BlockSpec((None, 2 * WIN_ROWS - 1, GRID_W, GRID_W), lambda h: (h, 0, 0, 0)),
        compiler_params=_cp(vmem=VMEM_BIG),
    )(db.reshape(WIN_ROWS, NAH, GRID_W, NWIN))
    valid, rel = _na_tables()
    onehot = np.zeros((GRID_W * GRID_W, 128), np.float32)
    flat_rel, flat_valid = rel.reshape(-1), valid.reshape(-1)
    onehot[np.arange(GRID_W * GRID_W)[flat_valid], flat_rel[flat_valid]] = 1.0

    def body2(a_ref, oh_ref, o_ref):
        o_ref[...] = _hmm(a_ref[...], oh_ref[...])

    vm = pl.BlockSpec(memory_space=pltpu.VMEM)
    out = pl.pallas_call(
        body2, name="rpb_fold_cols", out_shape=jax.ShapeDtypeStruct((NAH * 15, 128), F32),
        in_specs=[vm, vm], out_specs=vm, compiler_params=_cp(vmem=VMEM_BIG),
    )(t15.reshape(NAH * 15, GRID_W * GRID_W), jnp.asarray(onehot))
    return out[:, :31].reshape(NAH, 15, 31)


def _dn_out(o, z, w):
    outs = []
    for h in range(DNH):
        sl = slice(128 * h, 128 * h + 128)
        oh = o[:, sl]
        r = lax.rsqrt(jnp.mean(oh * oh, axis=-1, keepdims=True) + EPS)
        outs.append(oh * r * w[:, sl] * _silu(z[:, sl]))
    return jnp.concatenate(outs, axis=1)


def _outproj_fwd(of, ob, p, na, onw, x, g1, w_out):
    def body(of_ref, ob_ref, z_ref, na_ref, w_ref, x_ref, g_ref, wo_ref, x1_ref, ai_ref, ao_ref):
        dn = _dn_out(of_ref[...] + ob_ref[...], z_ref[...], w_ref[...])
        ai = jnp.concatenate([dn.astype(BF16), na_ref[...]], axis=1)
        ai_ref[...] = ai
        ao = jnp.dot(ai, wo_ref[...], preferred_element_type=F32)
        ao_ref[...] = ao
        x1_ref[...] = x_ref[...] + g_ref[...] * ao

    lat = lambda w: pl.BlockSpec((None, RT, w), lambda b, j: (b, j, 0))
    return pl.pallas_call(
        body, name="outproj_fwd", grid=(NB, SEQ // RT),
        out_shape=[jax.ShapeDtypeStruct((NB, SEQ, D), F32), jax.ShapeDtypeStruct((NB, SEQ, D), BF16),
                   jax.ShapeDtypeStruct((NB, SEQ, D), F32)],
        in_specs=[pl.BlockSpec((None, RT, DNW), lambda b, j: (b, j + 1, 0)),
                  pl.BlockSpec((None, RT, DNW), lambda b, j: (b, j + 1, 0)),
                  pl.BlockSpec((None, RT, DNW), lambda b, j: (b, j + 1, COL_Z // DNW)),
                  lat(NAW), pl.BlockSpec((1, DNW), lambda b, j: (0, 0)), lat(D),
                  pl.BlockSpec((None, 1, D), lambda b, j: (b, 0, 0)), pl.BlockSpec((D, D), lambda b, j: (0, 0))],
        out_specs=[lat(D), lat(D), lat(D)],
        compiler_params=_cp(vmem=VMEM_BIG),
    )(of, ob, p, na, onw, x, g1, w_out)


def _outproj_bwd(dx1, ao, g1, w_out, of, ob, p, onw):
    def body(dx_ref, ao_ref, g_ref, wo_ref, of_ref, ob_ref, z_ref, w_ref, dao_ref, do_ref, dz_ref, dna_ref, dg_ref, dw_ref):
        b, j = pl.program_id(0), pl.program_id(1)

        @pl.when((b == 0) & (j == 0))
        def _():
            dw_ref[...] = jnp.zeros_like(dw_ref)

        @pl.when(j == 0)
        def _():
            do_ref[...] = jnp.zeros_like(do_ref)
            dz_ref[...] = jnp.zeros_like(dz_ref)

        @pl.when(j == 1)
        def _():
            dg_ref[...] = jnp.zeros_like(dg_ref)

        @pl.when(j >= 1)
        def _():
            dx = dx_ref[...]
            dg_ref[...] += jnp.sum(dx * ao_ref[...], axis=0, keepdims=True)
            dao = (g_ref[...] * dx).astype(BF16)
            dao_ref[...] = dao
            dai = lax.dot_general(dao, wo_ref[...], _NT, preferred_element_type=F32)
            dna_ref[...] = dai[:, DNW:]
            _, vjp = jax.vjp(_dn_out, of_ref[...] + ob_ref[...], z_ref[...], w_ref[...])
            do, dz, dw = vjp(dai[:, :DNW])
            do_ref[...] = do
            dz_ref[...] = dz
            dw_ref[...] += jnp.concatenate([dw, jnp.zeros((7, DNW), F32)], axis=0)

    lat = lambda w: pl.BlockSpec((None, RT, w), lambda b, j: (b, jnp.maximum(j - 1, 0), 0))
    tok = lambda w: pl.BlockSpec((None, RT, w), lambda b, j: (b, j, 0))
    return pl.pallas_call(
        body, name="outproj_bwd", grid=(NB, T // RT),
        out_shape=[jax.ShapeDtypeStruct((NB, SEQ, D), BF16), jax.ShapeDtypeStruct((NB, T, DNW), F32),
                   jax.ShapeDtypeStruct((NB, T, DNW), F32), jax.ShapeDtypeStruct((NB, SEQ, NAW), F32),
                   jax.ShapeDtypeStruct((NB, 1, D), F32), jax.ShapeDtypeStruct((8, DNW), F32)],
        in_specs=[_rows2d_spec(D), lat(D), pl.BlockSpec((None, 1, D), lambda b, j: (b, 0, 0)),
                  pl.BlockSpec((D, D), lambda b, j: (0, 0)),
                  tok(DNW), tok(DNW),
                  pl.BlockSpec((None, RT, DNW), lambda b, j: (b, j, COL_Z // DNW)),
                  pl.BlockSpec((1, DNW), lambda b, j: (0, 0))],
        out_specs=[lat(D), tok(DNW), tok(DNW), lat(NAW), pl.BlockSpec((None, 1, D), lambda b, j: (b, 0, 0)),
                   pl.BlockSpec((8, DNW), lambda b, j: (0, 0))],
        compiler_params=_cp(vmem=VMEM_BIG),
    )(dx1, ao, g1, w_out, of, ob, p, onw)


FT_FWD, FT_BWD = 512, 256
NFF = DFF // FFT


def _ffn_specs(ft):
    row = lambda w: pl.BlockSpec((ft, w), lambda i, f: (i, 0))
    ex = pl.BlockSpec((None, 1, D), lambda i, f: (i // (SEQ // ft), 0, 0))
    one = pl.BlockSpec((1, D), lambda i, f: (0, 0))
    wg = pl.BlockSpec((None, D, FFT), lambda i, f: (f, 0, 0))
    wu = pl.BlockSpec((None, D, FFT), lambda i, f: (NFF + f, 0, 0))
    wo = pl.BlockSpec((FFT, D), lambda i, f: (f, 0))
    ff = pl.BlockSpec((ft, FFT), lambda i, f: (i, f))
    return row, ex, one, wg, wu, wo, ff


def _ffn_fwd(x1, sh2, sc2, g2, n2w, w_in4, w_out, target):
    ft, tpe = FT_FWD, SEQ // FT_FWD

    def body(x_ref, sh_ref, sc_ref, g_ref, w2_ref, wg_ref, wu_ref, wo_ref, t_ref,
             h2_ref, gt_ref, up_ref, dy_ref, dg_ref, loss_ref, acc_ref):
        i, f = pl.program_id(0), pl.program_id(1)

        @pl.when(f == 0)
        def _():
            h2_ref[...] = _norm_mod(x_ref[...], w2_ref[...], sh_ref[...], sc_ref[...]).astype(BF16)
            acc_ref[...] = jnp.zeros_like(acc_ref)

        h2 = h2_ref[...]
        gate = jnp.dot(h2, wg_ref[...], preferred_element_type=F32)
        up = jnp.dot(h2, wu_ref[...], preferred_element_type=F32)
        gt_ref[...] = gate.astype(BF16)
        up_ref[...] = up.astype(BF16)
        acc_ref[...] += jnp.dot((_silu(gate) * up).astype(BF16), wo_ref[...], preferred_element_type=F32)

        @pl.when((i == 0) & (f == 0))
        def _():
            loss_ref[...] = jnp.zeros_like(loss_ref)

        @pl.when(f == NFF - 1)
        def _():
            ffn = acc_ref[...]
            err = x_ref[...] + g_ref[...] * ffn - t_ref[...]
            dy = err * (1.0 / D)
            dy_ref[...] = dy
            part = jnp.sum(dy * ffn, axis=0, keepdims=True)
            loss_ref[...] += jnp.sum(jnp.sum(err * err, axis=1, keepdims=True), axis=0, keepdims=True) * (0.5 / D)

            @pl.when(i % tpe == 0)
            def _():
                dg_ref[...] = part

            @pl.when(i % tpe != 0)
            def _():
                dg_ref[...] += part

    row, ex, one, wg, wu, wo, ff = _ffn_specs(ft)
    return pl.pallas_call(
        body, name="ffn_fwd", grid=(NTOK // ft, NFF),
        out_shape=[jax.ShapeDtypeStruct((NTOK, D), BF16), jax.ShapeDtypeStruct((NTOK, DFF), BF16),
                   jax.ShapeDtypeStruct((NTOK, DFF), BF16), jax.ShapeDtypeStruct((NTOK, D), F32),
                   jax.ShapeDtypeStruct((NB, 1, D), F32), jax.ShapeDtypeStruct((8, 128), F32)],
        in_specs=[row(D), ex, ex, ex, one, wg, wu, wo, row(D)],
        out_specs=[row(D), ff, ff, row(D), ex, pl.BlockSpec((8, 128), lambda i, f: (0, 0))],
        scratch_shapes=[pltpu.VMEM((ft, D), F32)],
        compiler_params=_cp(vmem=VMEM_BIG),
    )(x1, sh2, sc2, g2, n2w, w_in4, w_in4, w_out, target)


def _rows2d_spec(width):
    return pl.BlockSpec((RT, width), lambda b, j: (b * (SEQ // RT) + jnp.maximum(j - 1, 0), 0))


def _ffn_bwd(dy, gate, up, x1, sh2, sc2, g2, n2w, w_in4, w_out):
    ft, tpe = FT_BWD, SEQ // FT_BWD
    ni = NTOK // ft

    def body(dy_ref, gt_ref, up_ref, x_ref, sh_ref, sc_ref, g_ref, w2_ref, wg_ref, wu_ref, wo_ref,
             dff_ref, act_ref, dgu_ref, dx_ref, dsh_ref, dsc_ref, dw_ref, acc_ref):
        f, i = pl.program_id(0), pl.program_id(1)
        rows = pl.ds(pl.multiple_of(i * ft, ft), ft)
        dff = (g_ref[...] * dy_ref[...]).astype(BF16)
        dff_ref[...] = dff
        dact = lax.dot_general(dff, wo_ref[...], _NT, preferred_element_type=F32)
        gate, up = gt_ref[...].astype(F32), up_ref[...].astype(F32)
        sg = _sigmoid(gate)
        sl = gate * sg
        act_ref[...] = (sl * up).astype(BF16)
        dgate = (dact * up * (sg * (1.0 + gate * (1.0 - sg)))).astype(BF16)
        dup = (dact * sl).astype(BF16)
        dgu_ref[0] = dgate
        dgu_ref[1] = dup
        part = (lax.dot_general(dgate, wg_ref[...], _NT, preferred_element_type=F32)
                + lax.dot_general(dup, wu_ref[...], _NT, preferred_element_type=F32))

        @pl.when(f == 0)
        def _():
            acc_ref[rows, :] = part

        @pl.when(f > 0)
        def _():
            acc_ref[rows, :] += part

        @pl.when((i == 0) & (f == 0))
        def _():
            dw_ref[...] = jnp.zeros_like(dw_ref)

        @pl.when(f == NFF - 1)
        def _():
            _, vjp = jax.vjp(_norm_mod, x_ref[...], w2_ref[...], sh_ref[...], sc_ref[...])
            dx, dw, dsh, dsc = vjp(acc_ref[rows, :])
            dx_ref[...] = dy_ref[...] + dx
            dw_ref[...] += dw

            @pl.when(i % tpe == 0)
            def _():
                dsh_ref[...] = dsh
                dsc_ref[...] = dsc

            @pl.when(i % tpe != 0)
            def _():
                dsh_ref[...] += dsh
                dsc_ref[...] += dsc

    last = lambda f, real, spare: jnp.where(f == NFF - 1, real, spare)
    row = lambda w: pl.BlockSpec((ft, w), lambda f, i: (i, 0))
    ex = pl.BlockSpec((None, 1, D), lambda f, i: (i // tpe, 0, 0))
    one = pl.BlockSpec((1, D), lambda f, i: (0, 0))
    wg = pl.BlockSpec((None, D, FFT), lambda f, i: (f, 0, 0))
    wu = pl.BlockSpec((None, D, FFT), lambda f, i: (NFF + f, 0, 0))
    wo = pl.BlockSpec((FFT, D), lambda f, i: (f, 0))
    ff = pl.BlockSpec((ft, FFT), lambda f, i: (i, f))
    row_once = lambda w: pl.BlockSpec((ft, w), lambda f, i: (last(f, i, ni), 0))
    ex_once = pl.BlockSpec((None, 1, D), lambda f, i: (last(f, i // tpe, NB), 0, 0))
    return pl.pallas_call(
        body, name="ffn_bwd", grid=(NFF, ni),
        out_shape=[jax.ShapeDtypeStruct((NTOK + ft, D), BF16), jax.ShapeDtypeStruct((NTOK, DFF), BF16),
                   jax.ShapeDtypeStruct((2, NTOK, DFF), BF16), jax.ShapeDtypeStruct((NTOK + ft, D), F32),
                   jax.ShapeDtypeStruct((NB + 1, 1, D), F32), jax.ShapeDtypeStruct((NB + 1, 1, D), F32),
                   jax.ShapeDtypeStruct((1, D), F32)],
        in_specs=[row(D), ff, ff, row(D), ex, ex, ex, one, wg, wu, wo],
        out_specs=[row_once(D), ff, pl.BlockSpec((2, ft, FFT), lambda f, i: (0, i, f)), row_once(D), ex_once, ex_once, one],
        scratch_shapes=[pltpu.VMEM((NTOK, D), F32)],
        compiler_params=_cp(vmem=VMEM_BIG),
    )(dy, gate, up, x1, sh2, sc2, g2, n2w, w_in4, w_in4, w_out)


def _local_step(x, ctx, target, mod_x, mod_c, norm1_w, norm2_w, w_in_p, conv_w, a_log, dt_bias, out_norm_w,
                q_norm_w, k_norm_w, rpb, w_out_b, w_ffn_in4, w_ffn_out_b, scatter_early=False, gather_late=None):
    sh1, sc1, g1, sh2, sc2, g2 = [mod_x[:, i * D:(i + 1) * D].reshape(NB, 1, D) for i in range(6)]
    csh1, csc1 = mod_c[0:D].reshape(1, 1, D), mod_c[D:2 * D].reshape(1, 1, D)
    cw = jnp.concatenate([conv_w, jnp.zeros((3, 1536), F32)], axis=0)
    prm = jnp.pad(jnp.stack([a_log.reshape(8), dt_bias.reshape(8)]), ((0, 6), (0, 120)))
    onw = jnp.tile(out_norm_w.reshape(1, 128), (1, DNH))
    qw = jnp.tile(q_norm_w.reshape(1, 64), (1, NAH))
    kw = jnp.tile(k_norm_w.reshape(1, 64), (1, NAH))
    seg = np.arange(NAW) // 64
    bd = jnp.asarray((seg[:, None] == seg[None, :]).astype(np.float32) / 64.0, dtype=BF16)
    bias = _bias_table(rpb)

    p, hm = _inproj_fwd(x, ctx, sh1, sc1, csh1, csc1, norm1_w, w_in_p)
    dq_, dk_, dv_ = _dn_prep_fwd(p, cw)
    gates = _gates_fwd(p, prm)
    res = _scan_fwd(dq_, dk_, dv_, gates, carry=gather_late[:2] if gather_late else ())
    o_f, st_f, inv_f, o_b, st_b, inv_b = res[:6]
    qn, kn, vb = _na_prep_fwd(p, qw, kw, bd)
    na, got = _na_fwd(qn, kn, vb, bias, carry=gather_late[2:] if gather_late else ())
    if gather_late:
        w_out_b = res[6].reshape(D, D)
        w_ffn_in4 = res[7].reshape(4, D, FFT)
        w_ffn_out_b = got[0].reshape(DFF, D)
    x1, ai, ao = _outproj_fwd(o_f, o_b, p, na, onw, x, g1, w_out_b)
    x1f = x1.reshape(NTOK, D)
    h2, gate, up, dy, dg2, loss8 = _ffn_fwd(x1f, sh2, sc2, g2, norm2_w, w_ffn_in4, w_ffn_out_b,
                                             target.reshape(NTOK, D))
    dff, act, dgu, dx1, dsh2, dsc2, dn2w = _ffn_bwd(dy, gate, up, x1f, sh2, sc2, g2, norm2_w, w_ffn_in4, w_ffn_out_b)
    g_w_ffn_out = _matmul_tn(act, dff, FFT, D, 512, "wgrad_ffn_out")
    g_w_ffn_in4 = _matmul_tn(
        h2, dgu, D, FFT, 1024, "wgrad_ffn_in", nn=2 * NFF,
        b_block=(None, 1024, FFT), b_map=lambda i, j, k: (j // NFF, k, j % NFF),
        out_shape=(4, D, FFT), out_block=(None, D, FFT), out_map=lambda i, j, k: (j, i, 0))
    dsh2, dsc2 = dsh2[:NB], dsc2[:NB]
    dao, do, dz, dna, dg1, donw = _outproj_bwd(dx1, ao, g1, w_out_b, o_f, o_b, p, onw)
    g_w_out = _matmul_tn(ai.reshape(NTOK, D), dao.reshape(NTOK, D), D, D, 1024, "wgrad_out")
    dqn, dkn, dvn, dbias = _na_bwd(qn, kn, vb, bias, dna)
    dp_na, dqw, dkw = _na_prep_bwd(p, qw, kw, bd, dqn, dkn, dvn)
    early = [g_w_out.reshape(8, D // 8, D), g_w_ffn_in4.reshape(8, D // 2, FFT), g_w_ffn_out.reshape(8, DFF // 8, D)]
    res = _scan_bwd(dq_, dk_, dv_, gates, st_f, st_b, inv_f, inv_b, do, carry=early if scatter_early else ())
    dqf, dkf, dvf, dgf, dqb, dkb, dvb, dgb = res[:8]
    if scatter_early:
        g_w_out, g_w_ffn_in4, g_w_ffn_out = res[8:]
    dpq, dpk, dpv, gcq, gck, gcv = _dn_prep_bwd(p, cw, (dqf, dqb), (dkf, dkb), (dvf, dvb))
    dgt, dprm = _gates_bwd(p, prm, dgf, dgb)
    dpb, grad_x, dsh1, dsc1, dcsh1, dcsc1, dn1w = _inproj_bwd(
        (dpq, dpk, dpv), dz, dp_na, dgt, w_in_p, x, ctx, sh1, sc1, csh1, csc1, norm1_w, dx1)
    g_w_in_p = _matmul_tn(dpb.reshape(NB * T, PC), hm.reshape(NB * T, D), PC, D, 512, "wgrad_in")
    g_rpb = _rpb_fold(dbias)

    dmod_x = jnp.concatenate([dsh1, dsc1, dg1, dsh2, dsc2, dg2], axis=2).reshape(NB, 6 * D)
    dmod_c = jnp.concatenate([dcsh1.reshape(D), dcsc1.reshape(D), jnp.zeros((4 * D,), F32)])
    g_conv = jnp.concatenate([gcq[:5], gck[:5], gcv[:5]], axis=1)
    small = dict(
        norm1_w=dn1w, norm2_w=dn2w, a_log=dprm[0, :8], dt_bias=dprm[1, :8],
        out_norm_w=donw[0].reshape(DNH, 128).sum(0), q_norm_w=dqw[0].reshape(NAH, 64).sum(0),
        k_norm_w=dkw[0].reshape(NAH, 64).sum(0), rpb=g_rpb, conv_w=g_conv)
    return loss8[0, 0], grad_x, dmod_x, dmod_c, small, g_w_in_p, g_w_out, g_w_ffn_in4, g_w_ffn_out


def _perm_w_in(wt):
    return jnp.concatenate([wt[:2048], wt[2064:3600], wt[2048:2064], jnp.zeros((PC - IN_COLS, wt.shape[1]), wt.dtype)],
                           axis=0)


def _unperm_w_in(gt):
    return jnp.concatenate([gt[:2048], gt[COL_G:COL_G + 16], gt[2048:COL_G]], axis=0)


def _misc_row(a_log, dt_bias, onw, qnw, knw):
    return jnp.concatenate([a_log.reshape(8), dt_bias.reshape(8), jnp.zeros((112,), F32), onw.reshape(128),
                            qnw.reshape(64), knw.reshape(64), jnp.zeros((D - 384,), F32)])


def _pack_small(c_ctx, n1, n2, b_ada, a_log, dt_bias, onw, qnw, knw, rpb, conv_s):
    misc = _misc_row(a_log, dt_bias, onw, qnw, knw)
    rp = jnp.concatenate([rpb.reshape(-1), jnp.zeros((4 * D - 3720,), F32)])
    cv = jnp.concatenate([conv_s.reshape(-1), jnp.zeros((2 * D - 1920,), F32)])
    return jnp.concatenate([c_ctx.reshape(-1), n1.reshape(-1), n2.reshape(-1), b_ada.reshape(-1), misc, rp, cv]).reshape(16, D)


def _unpack_small(a):
    f = a.reshape(-1)
    misc = f[9 * D:10 * D]
    return dict(
        c_ctx=f[0:D], norm1_w=f[D:2 * D].reshape(1, D), norm2_w=f[2 * D:3 * D].reshape(1, D),
        b_ada=f[3 * D:9 * D].reshape(1, 6 * D), dn_A_log=misc[0:8].reshape(1, 2, 4), dn_dt_bias=misc[8:16].reshape(1, 2, 4),
        dn_out_norm_w=misc[128:256].reshape(1, 128), na_q_norm_w=misc[256:320].reshape(1, 64),
        na_k_norm_w=misc[320:384].reshape(1, 64), na_rpb=f[10 * D:10 * D + 3720].reshape(1, 8, 15, 31),
        dn_conv_w=f[14 * D:14 * D + 1920].reshape(1, 5, 384))


_WEIGHTS = ['c_ctx', 'norm1_w', 'norm2_w', 'w_ada', 'b_ada', 'w_in', 'dn_conv_w', 'dn_A_log', 'dn_dt_bias',
            'dn_out_norm_w', 'na_q_norm_w', 'na_k_norm_w', 'na_rpb', 'w_out', 'w_ffn_in', 'w_ffn_out']


def kernel(x, c, ctx, c_ctx, norm1_w, norm2_w, w_ada, b_ada, w_in, dn_conv_w, dn_A_log, dn_dt_bias, dn_out_norm_w, na_q_norm_w, na_k_norm_w, na_rpb, w_out, w_ffn_in, w_ffn_out, loss_target, m_c_ctx, m_norm1_w, m_norm2_w, m_w_ada, m_b_ada, m_w_in, m_dn_conv_w, m_dn_A_log, m_dn_dt_bias, m_dn_out_norm_w, m_na_q_norm_w, m_na_k_norm_w, m_na_rpb, m_w_out, m_w_ffn_in, m_w_ffn_out, v_c_ctx, v_norm1_w, v_norm2_w, v_w_ada, v_b_ada, v_w_in, v_dn_conv_w, v_dn_A_log, v_dn_dt_bias, v_dn_out_norm_w, v_na_q_norm_w, v_na_k_norm_w, v_na_rpb, v_w_out, v_w_ffn_in, v_w_ffn_out):
    ix, iy, ic = lax.axis_index("x"), lax.axis_index("y"), lax.axis_index("c")
    chip = 2 * ix + iy
    me = 4 * ix + 2 * iy + ic

    crow = jnp.concatenate([c, c_ctx.reshape(1, D), jnp.zeros((5, D), F32)], axis=0)
    cvrow = jnp.concatenate([dn_conv_w[0].reshape(-1), jnp.zeros((2 * D - 1920,), F32)]).reshape(2, D)
    w_in_t = w_in[0].T
    gw_in, g0 = _gather_weights([_cast_bf16(lax.dynamic_slice(w_in_t, (0, ic * (D // 2)), (900, D // 2)), "cast_w_in")],
                                jnp.concatenate([crow, cvrow, jnp.zeros((6, D), F32)], axis=0))
    cin = g0[:, :8].reshape(64, D)
    conv_full = jnp.concatenate([g0[2 * s, 8:10].reshape(-1)[:1920].reshape(5, 384) for s in range(4)], axis=1)

    w_ada_s = w_ada[0]
    b_s = lax.dynamic_slice(b_ada, (0, chip * 1536), (1, 1536))
    mod_s = _ada_fwd(cin, w_ada_s, b_s)
    mods = _gather_rows(mod_s, "gather_mod")
    mod_mine = jnp.concatenate([mods[0], mods[2], mods[4], mods[6]], axis=1)
    mod_x, mod_c = mod_mine[0:NB], mod_mine[2]

    def half(w2d):
        r = w2d.shape[0] // 2
        return lax.dynamic_slice(w2d, (ic * r, 0), (r, w2d.shape[1]))

    late = [_cast_bf16(half(w_out[0]), "cast_w_out"), _cast_bf16(half(w_ffn_in[0]), "cast_w_ffn_in"),
            _cast_bf16(half(w_ffn_out[0]), "cast_w_ffn_out")]
    w_in_p = _perm_w_in(gw_in.reshape(4, 2, 900, D // 2).transpose(0, 2, 1, 3).reshape(IN_COLS, D))

    (loss_part, grad_x, dmod_x, dmod_c, small, g_w_in_p, r_w_out, r_w_ffn_in, r_w_ffn_out) = _local_step(
        x, ctx, loss_target, mod_x, mod_c, norm1_w, norm2_w, w_in_p, conv_full, dn_A_log[0], dn_dt_bias[0],
        dn_out_norm_w[0], na_q_norm_w[0], na_k_norm_w[0], na_rpb[0], None, None, None,
        scatter_early=True, gather_late=late)

    dm = jnp.concatenate([dmod_x, dmod_c.reshape(1, 6 * D), jnp.zeros((5, 6 * D), F32)], axis=0).reshape(48, D)
    misc = _misc_row(small["a_log"], small["dt_bias"], small["out_norm_w"], small["q_norm_w"], small["k_norm_w"])
    sm = jnp.concatenate([
        small["norm1_w"].reshape(-1), small["norm2_w"].reshape(-1), misc,
        small["rpb"].reshape(-1), jnp.zeros((4 * D - 3720,), F32),
        small["conv_w"].reshape(-1), jnp.zeros((8 * D - 7680,), F32),
        loss_part.reshape(1), jnp.zeros((D - 1,), F32)]).reshape(16, D)
    gsl, gtot = _allgather8(jnp.concatenate([dm, sm], axis=0), "gather_small", with_sum=True)
    gmod = gsl[:, :48].reshape(64, 6 * D)
    st = gtot[48:64].reshape(-1)
    loss = st[15 * D]
    g_n1, g_n2, g_misc = st[0:D].reshape(1, D), st[D:2 * D].reshape(1, D), st[2 * D:3 * D]
    g_rpb = st[3 * D:3 * D + 3720].reshape(1, 8, 15, 31)
    g_conv_full = st[7 * D:7 * D + 7680].reshape(5, 1536)
    g_conv_s = lax.dynamic_slice(g_conv_full, (0, chip * 384), (5, 384)).reshape(1, 5, 384)

    gmod_s = lax.dynamic_slice(gmod, (0, chip * 1536), (64, 1536))
    g_w_ada_s, pctx = _ada_bwd(cin, gmod_s, w_ada_s)
    parts = _allgather8(pctx, "gather_cctx")
    g_b_ada, g_c_ctx = _ada_small(gmod, parts, c_ctx.reshape(1, D))

    g_w_in8 = _unperm_w_in(g_w_in_p).reshape(4, 900, 2, D // 2).transpose(0, 2, 1, 3).reshape(8, 900, D // 2)
    names = ["w_in", "w_out", "w_ffn_in", "w_ffn_out"]
    early = _sibling_exchange([_sum_slots(r, "sum_" + nm)
                               for r, nm in zip([r_w_out, r_w_ffn_in, r_w_ffn_out], names[1:])], "sibling_early")
    g_w_out_s = early[0].reshape(1, D // 4, D)
    g_w_ffn_in_s = early[1].reshape(1, D, FFT)
    g_w_ffn_out_s = early[2].reshape(1, DFF // 4, D)
    two = lambda a: a.reshape(a.shape[1], a.shape[2])
    hosted_names = ["w_ada", "w_out", "w_ffn_in"]
    hosted_g = dict(w_ada=g_w_ada_s, w_out=two(g_w_out_s), w_ffn_in=two(g_w_ffn_in_s))
    hosted_w = dict(w_ada=(w_ada, m_w_ada, v_w_ada), w_out=(w_out, m_w_out, v_w_out), w_ffn_in=(w_ffn_in, m_w_ffn_in, v_w_ffn_in))
    hosted_res, (r_w_in,) = _adamw_hosting(
        [(two(hosted_w[k][0]), hosted_g[k], two(hosted_w[k][1]), two(hosted_w[k][2])) for k in hosted_names], [g_w_in8])
    (g_w_in_half,) = _sibling_exchange([_sum_slots(r_w_in, "sum_w_in")], "sibling_w_in")
    g_w_in_t = g_w_in_half.transpose(1, 0, 2).reshape(900, D)

    grads = dict(
        c_ctx=g_c_ctx.reshape(D), norm1_w=g_n1, norm2_w=g_n2, w_ada=g_w_ada_s.reshape(1, D, 1536), b_ada=g_b_ada,
        w_in=g_w_in_t, dn_conv_w=g_conv_s, dn_A_log=g_misc[0:8].reshape(1, 2, 4), dn_dt_bias=g_misc[8:16].reshape(1, 2, 4),
        dn_out_norm_w=g_misc[128:256].reshape(1, 128), na_q_norm_w=g_misc[256:320].reshape(1, 64),
        na_k_norm_w=g_misc[320:384].reshape(1, 64), na_rpb=g_rpb, w_out=g_w_out_s, w_ffn_in=g_w_ffn_in_s,
        w_ffn_out=g_w_ffn_out_s)
    pk = lambda d_: _pack_small(d_["c_ctx"], d_["norm1_w"], d_["norm2_w"], d_["b_ada"], d_["dn_A_log"], d_["dn_dt_bias"],
                                d_["dn_out_norm_w"], d_["na_q_norm_w"], d_["na_k_norm_w"], d_["na_rpb"], d_["dn_conv_w"])
    wts = dict(c_ctx=c_ctx, norm1_w=norm1_w, norm2_w=norm2_w, b_ada=b_ada, dn_A_log=dn_A_log, dn_dt_bias=dn_dt_bias,
               dn_out_norm_w=dn_out_norm_w, na_q_norm_w=na_q_norm_w, na_k_norm_w=na_k_norm_w, na_rpb=na_rpb, dn_conv_w=dn_conv_w)
    ms = dict(c_ctx=m_c_ctx, norm1_w=m_norm1_w, norm2_w=m_norm2_w, b_ada=m_b_ada, dn_A_log=m_dn_A_log, dn_dt_bias=m_dn_dt_bias,
              dn_out_norm_w=m_dn_out_norm_w, na_q_norm_w=m_na_q_norm_w, na_k_norm_w=m_na_k_norm_w, na_rpb=m_na_rpb,
              dn_conv_w=m_dn_conv_w)
    vs = dict(c_ctx=v_c_ctx, norm1_w=v_norm1_w, norm2_w=v_norm2_w, b_ada=v_b_ada, dn_A_log=v_dn_A_log, dn_dt_bias=v_dn_dt_bias,
              dn_out_norm_w=v_dn_out_norm_w, na_q_norm_w=v_na_q_norm_w, na_k_norm_w=v_na_k_norm_w, na_rpb=v_na_rpb,
              dn_conv_w=v_dn_conv_w)
    sd, snm, snv, _ = _adamw(pk(wts), pk(grads), pk(ms), pk(vs), "adamw_small")
    delta, new_m, new_v = _unpack_small(sd), _unpack_small(snm), _unpack_small(snv)
    for k, res in zip(hosted_names, hosted_res):
        shp = hosted_w[k][0].shape
        delta[k], new_m[k], new_v[k], grads[k] = [a.reshape(shp) for a in res]
    big = dict(w_in=(w_in, m_w_in, v_w_in), w_ffn_out=(w_ffn_out, m_w_ffn_out, v_w_ffn_out))
    for nm, (w_, m_, v_) in big.items():
        shp = w_.shape
        if nm == "w_in":
            r2, back = (lambda a: a[0].T), (lambda a: a.T.reshape(shp))
            res = _adamw(w_in_t, grads[nm], r2(m_), r2(v_), "adamw_" + nm)
        else:
            r2, back = (lambda a: a.reshape(shp[1], shp[2])), (lambda a: a.reshape(shp))
            res = _adamw(r2(w_), r2(grads[nm]), r2(m_), r2(v_), "adamw_" + nm)
        delta[nm], new_m[nm], new_v[nm], grads[nm] = [back(a) for a in res]

    return (loss, grad_x, *[grads[n] for n in _WEIGHTS], *[delta[n] for n in _WEIGHTS],
            *[new_m[n] for n in _WEIGHTS], *[new_v[n] for n in _WEIGHTS])
```
